```python
import jax, jax.numpy as jnp
from jax import lax
import numpy as np

D_MODEL = 1024
BATCH = 32
SEQ = 2048
DEPTH = 1

N_META = 16
D_MIX = D_MODEL
D_CONV = D_MIX // 2
D_POOL = D_MIX - D_CONV
CONV_HEADS = 8
CONV_WIDTH = 3
POOL_WINDOWS = (2, 4, 8, 16)
N_POOL_GROUPS = len(POOL_WINDOWS)
POOL_GROUP = D_POOL // N_POOL_GROUPS
D_IN_PROJ = 3 * D_CONV + D_POOL
D_FF = ((int(np.ceil(8 * D_MODEL / 3)) + 255) // 256) * 256
RMS_EPS = 1e-6

kernel_name = "hymba_conv_pool_hybrid_block"


def rms_norm(x, g):
    xf = x.astype(jnp.float32)
    y = xf * lax.rsqrt(jnp.mean(xf * xf, axis=-1, keepdims=True) + RMS_EPS)
    return (y * g.astype(jnp.float32)).astype(x.dtype)


def causal_short_conv(u, w):
    k_width = w.shape[0]
    seq_len = u.shape[1]
    up = jnp.pad(u, ((0, 0), (k_width - 1, 0), (0, 0)))
    y = w[0] * up[:, 0:seq_len]
    for k in range(1, k_width):
        y = y + w[k] * up[:, k:k + seq_len]
    return y


def multiscale_pool(u, pool_w, pool_scale):
    bsz, seq_len, _ = u.shape
    ug = u.reshape(bsz, seq_len, N_POOL_GROUPS, POOL_GROUP)
    pos = jnp.arange(seq_len)
    outs = []
    for g, win in enumerate(POOL_WINDOWS):
        xg = ug[:, :, g].astype(jnp.float32)
        cs = jnp.cumsum(xg, axis=1)
        cs_prev = jnp.pad(cs, ((0, 0), (win, 0), (0, 0)))[:, :seq_len]
        cnt = jnp.minimum(pos + 1, win).astype(jnp.float32)[None, :, None]
        outs.append((cs - cs_prev) / cnt - xg)
    pooled = jnp.stack(outs, axis=2).astype(u.dtype)
    mixed = jnp.einsum('blgc,gcd->blgd', pooled, pool_w)
    return mixed.reshape(bsz, seq_len, D_POOL) * pool_scale


def _fwd_setup_inputs(seed: int = 0) -> dict:
    key = jax.random.key(seed)
    ks = jax.random.split(key, 16)
    f32 = jnp.float32

    def nrm(k, shape, scale):
        return jax.random.normal(k, shape, f32) * scale

    def gain(k):
        return 1.0 + 0.05 * jax.random.normal(k, (DEPTH, D_MODEL), f32)

    return {
        "x": jax.random.normal(ks[0], (BATCH, SEQ, D_MODEL), f32),
        "meta_tokens": nrm(ks[1], (N_META, D_MODEL), 1.0),
        "norm_mix_pre": gain(ks[2]),
        "w_in": nrm(ks[3], (DEPTH, D_MODEL, D_IN_PROJ), D_MODEL ** -0.5),
        "conv_w": nrm(ks[4], (DEPTH, CONV_WIDTH, D_CONV), CONV_WIDTH ** -0.5),
        "pool_w": nrm(ks[5], (DEPTH, N_POOL_GROUPS, POOL_GROUP, POOL_GROUP), POOL_GROUP ** -0.5),
        "pool_scale": 1.0 + 0.1 * jax.random.normal(ks[6], (DEPTH, D_POOL), f32),
        "w_out": nrm(ks[7], (DEPTH, D_MIX, D_MODEL), D_MIX ** -0.5),
        "norm_mix_post": gain(ks[8]),
        "norm_ffn_pre": gain(ks[9]),
        "w_gate": nrm(ks[10], (DEPTH, D_MODEL, D_FF), D_MODEL ** -0.5),
        "w_up": nrm(ks[11], (DEPTH, D_MODEL, D_FF), D_MODEL ** -0.5),
        "w_down": nrm(ks[12], (DEPTH, D_FF, D_MODEL), D_FF ** -0.5),
        "norm_ffn_post": gain(ks[13]),
    }


def _fwd_reference(x, meta_tokens, norm_mix_pre, w_in, conv_w, pool_w, pool_scale, w_out,
              norm_mix_post, norm_ffn_pre, w_gate, w_up, w_down, norm_ffn_post):
    bsz = x.shape[0]
    meta = jnp.broadcast_to(meta_tokens[None].astype(x.dtype), (bsz, N_META, D_MODEL))
    h = jnp.concatenate([meta, x], axis=1)

    for i in range(DEPTH):
        a = rms_norm(h, norm_mix_pre[i])
        z = a @ w_in[i]
        b_gate = z[..., 0:D_CONV]
        c_gate = z[..., D_CONV:2 * D_CONV]
        v = z[..., 2 * D_CONV:3 * D_CONV]
        p = z[..., 3 * D_CONV:]
        y_conv = b_gate * causal_short_conv(c_gate * v, conv_w[i])
        y_pool = multiscale_pool(p, pool_w[i], pool_scale[i])
        m = jnp.concatenate([y_conv, y_pool], axis=-1) @ w_out[i]
        h = h + rms_norm(m, norm_mix_post[i])

        f = rms_norm(h, norm_ffn_pre[i])
        g = jax.nn.silu(f @ w_gate[i]) * (f @ w_up[i])
        h = h + rms_norm(g @ w_down[i], norm_ffn_post[i])

    return h[:, N_META:]


import jax as _jax
import jax.numpy as _jnp

TWIN_FORMAT = 'train_step'
FWD_PARAMS = ['x', 'meta_tokens', 'norm_mix_pre', 'w_in', 'conv_w', 'pool_w', 'pool_scale', 'w_out', 'norm_mix_post', 'norm_ffn_pre', 'w_gate', 'w_up', 'w_down', 'norm_ffn_post']
TWIN_WEIGHTS = ['meta_tokens', 'norm_mix_pre', 'w_in', 'conv_w', 'pool_w', 'pool_scale', 'w_out', 'norm_mix_post', 'norm_ffn_pre', 'w_gate', 'w_up', 'w_down', 'norm_ffn_post']
TWIN_DIFF_INPUT = 'x'
TWIN_INPUTS = ['x', 'meta_tokens', 'norm_mix_pre', 'w_in', 'conv_w', 'pool_w', 'pool_scale', 'w_out', 'norm_mix_post', 'norm_ffn_pre', 'w_gate', 'w_up', 'w_down', 'norm_ffn_post', 'loss_target', 'm_meta_tokens', 'm_norm_mix_pre', 'm_w_in', 'm_conv_w', 'm_pool_w', 'm_pool_scale', 'm_w_out', 'm_norm_mix_post', 'm_norm_ffn_pre', 'm_w_gate', 'm_w_up', 'm_w_down', 'm_norm_ffn_post', 'v_meta_tokens', 'v_norm_mix_pre', 'v_w_in', 'v_conv_w', 'v_pool_w', 'v_pool_scale', 'v_w_out', 'v_norm_mix_post', 'v_norm_ffn_pre', 'v_w_gate', 'v_w_up', 'v_w_down', 'v_norm_ffn_post']
TWIN_OUTPUTS = ['loss', 'grad_x', 'grad_meta_tokens', 'grad_norm_mix_pre', 'grad_w_in', 'grad_conv_w', 'grad_pool_w', 'grad_pool_scale', 'grad_w_out', 'grad_norm_mix_post', 'grad_norm_ffn_pre', 'grad_w_gate', 'grad_w_up', 'grad_w_down', 'grad_norm_ffn_post', 'delta_meta_tokens', 'delta_norm_mix_pre', 'delta_w_in', 'delta_conv_w', 'delta_pool_w', 'delta_pool_scale', 'delta_w_out', 'delta_norm_mix_post', 'delta_norm_ffn_pre', 'delta_w_gate', 'delta_w_up', 'delta_w_down', 'delta_norm_ffn_post', 'new_m_meta_tokens', 'new_m_norm_mix_pre', 'new_m_w_in', 'new_m_conv_w', 'new_m_pool_w', 'new_m_pool_scale', 'new_m_w_out', 'new_m_norm_mix_post', 'new_m_norm_ffn_pre', 'new_m_w_gate', 'new_m_w_up', 'new_m_w_down', 'new_m_norm_ffn_post', 'new_v_meta_tokens', 'new_v_norm_mix_pre', 'new_v_w_in', 'new_v_conv_w', 'new_v_pool_w', 'new_v_pool_scale', 'new_v_w_out', 'new_v_norm_mix_post', 'new_v_norm_ffn_pre', 'new_v_w_gate', 'new_v_w_up', 'new_v_w_down', 'new_v_norm_ffn_post']
TWIN_LEAF_KINDS = {'loss': 'loss', 'grad_x': 'grad_x', 'grad_meta_tokens': 'grad_w', 'grad_norm_mix_pre': 'grad_w', 'grad_w_in': 'grad_w', 'grad_conv_w': 'grad_w', 'grad_pool_w': 'grad_w', 'grad_pool_scale': 'grad_w', 'grad_w_out': 'grad_w', 'grad_norm_mix_post': 'grad_w', 'grad_norm_ffn_pre': 'grad_w', 'grad_w_gate': 'grad_w', 'grad_w_up': 'grad_w', 'grad_w_down': 'grad_w', 'grad_norm_ffn_post': 'grad_w', 'delta_meta_tokens': 'delta_w', 'delta_norm_mix_pre': 'delta_w', 'delta_w_in': 'delta_w', 'delta_conv_w': 'delta_w', 'delta_pool_w': 'delta_w', 'delta_pool_scale': 'delta_w', 'delta_w_out': 'delta_w', 'delta_norm_mix_post': 'delta_w', 'delta_norm_ffn_pre': 'delta_w', 'delta_w_gate': 'delta_w', 'delta_w_up': 'delta_w', 'delta_w_down': 'delta_w', 'delta_norm_ffn_post': 'delta_w', 'new_m_meta_tokens': 'new_m', 'new_m_norm_mix_pre': 'new_m', 'new_m_w_in': 'new_m', 'new_m_conv_w': 'new_m', 'new_m_pool_w': 'new_m', 'new_m_pool_scale': 'new_m', 'new_m_w_out': 'new_m', 'new_m_norm_mix_post': 'new_m', 'new_m_norm_ffn_pre': 'new_m', 'new_m_w_gate': 'new_m', 'new_m_w_up': 'new_m', 'new_m_w_down': 'new_m', 'new_m_norm_ffn_post': 'new_m', 'new_v_meta_tokens': 'new_v', 'new_v_norm_mix_pre': 'new_v', 'new_v_w_in': 'new_v', 'new_v_conv_w': 'new_v', 'new_v_pool_w': 'new_v', 'new_v_pool_scale': 'new_v', 'new_v_w_out': 'new_v', 'new_v_norm_mix_post': 'new_v', 'new_v_norm_ffn_pre': 'new_v', 'new_v_w_gate': 'new_v', 'new_v_w_up': 'new_v', 'new_v_w_down': 'new_v', 'new_v_norm_ffn_post': 'new_v'}


def _forward(args):
    return _fwd_reference(*[args[k] for k in FWD_PARAMS])


def _output_shape():
    out = _jax.eval_shape(lambda: _forward(_fwd_setup_inputs(0)))
    return out.shape, out.dtype

N_MICROBATCH = 1
ADAM_LR = 0.001
ADAM_B1 = 0.9
ADAM_B2 = 0.999
ADAM_EPS = 1e-08
ADAM_WD = 0.01
ADAM_STEP = 10
PER_EXAMPLE_BATCH_AXIS = {'x': 0, 'loss_target': 0}
SHARED_INPUTS = []
_WEIGHT_DTYPES = {'meta_tokens': _jnp.float32, 'norm_mix_pre': _jnp.float32, 'w_in': _jnp.float32, 'conv_w': _jnp.float32, 'pool_w': _jnp.float32, 'pool_scale': _jnp.float32, 'w_out': _jnp.float32, 'norm_mix_post': _jnp.float32, 'norm_ffn_pre': _jnp.float32, 'w_gate': _jnp.float32, 'w_up': _jnp.float32, 'w_down': _jnp.float32, 'norm_ffn_post': _jnp.float32}
MOMENT_SCALE = {'meta_tokens': 1.077984e-02, 'norm_mix_pre': 1.155722e+00, 'w_in': 7.509167e-01, 'conv_w': 6.942598e-01, 'pool_w': 1.119133e+00, 'pool_scale': 1.259731e+00, 'w_out': 9.633250e-01, 'norm_mix_post': 6.369818e+01, 'norm_ffn_pre': 8.109324e-01, 'w_gate': 2.742134e-01, 'w_up': 4.417794e-01, 'w_down': 7.351291e-01, 'norm_ffn_post': 6.384628e+01}


def _to_microbatches(a, axis):
    t = _jnp.moveaxis(a, axis, 0)
    t = t.reshape((N_MICROBATCH, t.shape[0] // N_MICROBATCH) + t.shape[1:])
    return _jnp.moveaxis(t, 1, axis + 1)


def setup_inputs(seed: int = 0) -> dict:
    inp = _fwd_setup_inputs(seed)
    key = _jax.random.fold_in(_jax.random.key(seed), 7919)
    shape, _ = _output_shape()
    out = dict(inp)
    out["loss_target"] = _jax.random.normal(_jax.random.fold_in(key, 0), shape, _jnp.float32)
    for i, name in enumerate(TWIN_WEIGHTS):
        w = inp[name].astype(_jnp.float32)
        if MOMENT_SCALE is None:
            s = _jnp.sqrt(_jnp.mean(_jnp.square(w)) + 1e-30)
        else:
            s = MOMENT_SCALE[name]
        km, kv = _jax.random.split(_jax.random.fold_in(key, i + 1))
        out[name] = w
        out["m_" + name] = s * _jax.random.normal(km, w.shape, _jnp.float32)
        out["v_" + name] = (s * s) * _jax.random.uniform(kv, w.shape, _jnp.float32, 0.5, 1.5)
    if N_MICROBATCH > 1:
        for name, axis in PER_EXAMPLE_BATCH_AXIS.items():
            out[name] = _to_microbatches(out[name], axis)
    return {'x': out['x'], 'meta_tokens': out['meta_tokens'], 'norm_mix_pre': out['norm_mix_pre'], 'w_in': out['w_in'], 'conv_w': out['conv_w'], 'pool_w': out['pool_w'], 'pool_scale': out['pool_scale'], 'w_out': out['w_out'], 'norm_mix_post': out['norm_mix_post'], 'norm_ffn_pre': out['norm_ffn_pre'], 'w_gate': out['w_gate'], 'w_up': out['w_up'], 'w_down': out['w_down'], 'norm_ffn_post': out['norm_ffn_post'], 'loss_target': out['loss_target'], 'm_meta_tokens': out['m_meta_tokens'], 'm_norm_mix_pre': out['m_norm_mix_pre'], 'm_w_in': out['m_w_in'], 'm_conv_w': out['m_conv_w'], 'm_pool_w': out['m_pool_w'], 'm_pool_scale': out['m_pool_scale'], 'm_w_out': out['m_w_out'], 'm_norm_mix_post': out['m_norm_mix_post'], 'm_norm_ffn_pre': out['m_norm_ffn_pre'], 'm_w_gate': out['m_w_gate'], 'm_w_up': out['m_w_up'], 'm_w_down': out['m_w_down'], 'm_norm_ffn_post': out['m_norm_ffn_post'], 'v_meta_tokens': out['v_meta_tokens'], 'v_norm_mix_pre': out['v_norm_mix_pre'], 'v_w_in': out['v_w_in'], 'v_conv_w': out['v_conv_w'], 'v_pool_w': out['v_pool_w'], 'v_pool_scale': out['v_pool_scale'], 'v_w_out': out['v_w_out'], 'v_norm_mix_post': out['v_norm_mix_post'], 'v_norm_ffn_pre': out['v_norm_ffn_pre'], 'v_w_gate': out['v_w_gate'], 'v_w_up': out['v_w_up'], 'v_w_down': out['v_w_down'], 'v_norm_ffn_post': out['v_norm_ffn_post']}


def _loss(weights, diff, rest, loss_target):
    with _jax.named_scope("forward"):
        args = {**rest, TWIN_DIFF_INPUT: diff, **{k: w.astype(_WEIGHT_DTYPES[k]) for k, w in weights.items()}}
        y = _forward(args)
    with _jax.named_scope("loss_head"):
        err = _jnp.square(y.astype(_jnp.float32) - loss_target)
        return 0.5 * _jnp.sum(_jnp.mean(err, axis=-1)) if err.ndim else 0.5 * err


def _adamw(w, g, m, v):
    m = ADAM_B1 * m + (1.0 - ADAM_B1) * g
    v = ADAM_B2 * v + (1.0 - ADAM_B2) * _jnp.square(g)
    m_hat = m / (1.0 - ADAM_B1 ** ADAM_STEP)
    v_hat = v / (1.0 - ADAM_B2 ** ADAM_STEP)
    delta = -ADAM_LR * (m_hat / (_jnp.sqrt(v_hat) + ADAM_EPS) + ADAM_WD * w)
    return delta, m, v


def reference(x, meta_tokens, norm_mix_pre, w_in, conv_w, pool_w, pool_scale, w_out, norm_mix_post, norm_ffn_pre, w_gate, w_up, w_down, norm_ffn_post, loss_target, m_meta_tokens, m_norm_mix_pre, m_w_in, m_conv_w, m_pool_w, m_pool_scale, m_w_out, m_norm_mix_post, m_norm_ffn_pre, m_w_gate, m_w_up, m_w_down, m_norm_ffn_post, v_meta_tokens, v_norm_mix_pre, v_w_in, v_conv_w, v_pool_w, v_pool_scale, v_w_out, v_norm_mix_post, v_norm_ffn_pre, v_w_gate, v_w_up, v_w_down, v_norm_ffn_post):
    given = dict(x=x, meta_tokens=meta_tokens, norm_mix_pre=norm_mix_pre, w_in=w_in, conv_w=conv_w, pool_w=pool_w, pool_scale=pool_scale, w_out=w_out, norm_mix_post=norm_mix_post, norm_ffn_pre=norm_ffn_pre, w_gate=w_gate, w_up=w_up, w_down=w_down, norm_ffn_post=norm_ffn_post, loss_target=loss_target, m_meta_tokens=m_meta_tokens, m_norm_mix_pre=m_norm_mix_pre, m_w_in=m_w_in, m_conv_w=m_conv_w, m_pool_w=m_pool_w, m_pool_scale=m_pool_scale, m_w_out=m_w_out, m_norm_mix_post=m_norm_mix_post, m_norm_ffn_pre=m_norm_ffn_pre, m_w_gate=m_w_gate, m_w_up=m_w_up, m_w_down=m_w_down, m_norm_ffn_post=m_norm_ffn_post, v_meta_tokens=v_meta_tokens, v_norm_mix_pre=v_norm_mix_pre, v_w_in=v_w_in, v_conv_w=v_conv_w, v_pool_w=v_pool_w, v_pool_scale=v_pool_scale, v_w_out=v_w_out, v_norm_mix_post=v_norm_mix_post, v_norm_ffn_pre=v_norm_ffn_pre, v_w_gate=v_w_gate, v_w_up=v_w_up, v_w_down=v_w_down, v_norm_ffn_post=v_norm_ffn_post)
    weights = {n: given[n] for n in TWIN_WEIGHTS}
    shared = {n: given[n] for n in SHARED_INPUTS}
    per_example = {n: given[n] for n in ['x']}
    grad_fn = _jax.value_and_grad(_loss, argnums=(0, 1))

    def one_microbatch(ex, loss_target):
        ex = dict(ex)
        diff = ex.pop(TWIN_DIFF_INPUT)
        return grad_fn(weights, diff, {**shared, **ex}, loss_target)

    if N_MICROBATCH == 1:
        loss, (grad_w, grad_x) = one_microbatch(per_example, given["loss_target"])
    else:
        def body(carry, xs):
            loss_sum, grad_sum = carry
            l_k, (gw_k, gx_k) = one_microbatch(xs[0], xs[1])
            with _jax.named_scope("update"):
                return (loss_sum + l_k, _jax.tree.map(_jnp.add, grad_sum, gw_k)), gx_k

        init = (_jnp.zeros((), _jnp.float32), _jax.tree.map(_jnp.zeros_like, weights))
        (loss, grad_w), grad_x = _jax.lax.scan(body, init, (per_example, given["loss_target"]))
    with _jax.named_scope("update"):
        delta_w, new_m, new_v = {}, {}, {}
        for n in TWIN_WEIGHTS:
            delta_w[n], new_m[n], new_v[n] = _adamw(weights[n], grad_w[n], given["m_" + n], given["v_" + n])
    return (loss, grad_x, *[grad_w[n] for n in TWIN_WEIGHTS], *[delta_w[n] for n in TWIN_WEIGHTS],
            *[new_m[n] for n in TWIN_WEIGHTS], *[new_v[n] for n in TWIN_WEIGHTS])
```

```python
import functools

import jax
import jax.numpy as jnp
from jax import lax
from jax.experimental import pallas as pl
from jax.experimental.pallas import tpu as pltpu

F32 = jnp.float32
BF16 = jnp.bfloat16
MESH = pl.DeviceIdType.MESH

D_MODEL = 1024
D_CONV = 512
D_POOL = 512
POOL_GROUP = 128
N_POOL_GROUPS = 4
D_IN_PROJ = 2048
D_FF = 2816
N_CHIPS = 4
FF_SHARD = D_FF // N_CHIPS
IN_SHARD = D_IN_PROJ // N_CHIPS
OUT_SHARD = D_MODEL // N_CHIPS
N_META = 16
HALO = 16
RMS_EPS = 1e-6

ADAM_LR = 0.001
ADAM_B1 = 0.9
ADAM_B2 = 0.999
ADAM_EPS = 1e-08
ADAM_WD = 0.01
ADAM_STEP = 10

TM_MIX_FWD = 512
TM_MIX_BWD = 256
TM_FFN = 256
TK_DW = 512
VMEM_LIMIT = 56 * 1024 * 1024


def _cparams(n_grid):
    return pltpu.CompilerParams(dimension_semantics=("arbitrary",) * n_grid, vmem_limit_bytes=VMEM_LIMIT)


def _dot(a, b):
    return jnp.dot(a, b, preferred_element_type=F32)


def _dot_nt(a, b):
    return lax.dot_general(a, b, (((1,), (1,)), ((), ())), preferred_element_type=F32)


def _dot_tn(a, b):
    return lax.dot_general(a, b, (((0,), (0,)), ((), ())), preferred_element_type=F32)


def _rows8(v):
    r, c = v.shape
    return v.reshape(r // 8, 8, c).sum(axis=0)


def _rstd(v):
    return lax.rsqrt(jnp.mean(v * v, axis=-1, keepdims=True) + RMS_EPS)


def _rms_bwd(dy, xhat, rstd, gain):
    dyg = dy * gain
    return rstd * (dyg - xhat * jnp.mean(dyg * xhat, axis=-1, keepdims=True))


def _sigmoid(v):
    return 1.0 / (1.0 + jnp.exp(-v))


def _gcols(g):
    return slice(g * POOL_GROUP, (g + 1) * POOL_GROUP)


def _pool_fwd(pb, g, n):
    win = 2 << g
    cur = pb[HALO:HALO + n, _gcols(g)]
    acc = cur
    for k in range(1, win):
        acc = acc + pb[HALO - k:HALO - k + n, _gcols(g)]
    return acc * (1.0 / win) - cur


def _pool_bwd(qb, g, n):
    win = 2 << g
    cur = qb[0:n, _gcols(g)]
    acc = cur
    for k in range(1, win):
        acc = acc + qb[k:k + n, _gcols(g)]
    return acc * (1.0 / win) - cur


def _full(shape):
    nd = len(shape)
    return pl.BlockSpec(shape, lambda *_: (0,) * nd)


ANY = pl.BlockSpec(memory_space=pl.ANY)


def _mesh_pos():
    x, y, c = lax.axis_index("x"), lax.axis_index("y"), lax.axis_index("c")
    chips = [(1 - x, y), (x, 1 - y), (1 - x, 1 - y)]
    return x, y, c, chips


def _half(ref, h):
    hr = ref.shape[0] // 2
    return ref.at[pl.ds(h * hr, hr), :]


def _all_gather_shards(shards):
    n = len(shards)

    def body(*refs):
        ins, outs = refs[:n], refs[n:2 * n]
        send_sems, recv_sems, loc_sems = refs[2 * n:]
        x, y, c, chips = _mesh_pos()
        me = 2 * x + y
        sib = (x, y, 1 - c)
        local = [pltpu.make_async_copy(ins[a], outs[a].at[me], loc_sems.at[a]) for a in range(n)]
        for cp in local:
            cp.start()

        def copy(a, k, src, dst, to):
            return pltpu.make_async_remote_copy(src_ref=src, dst_ref=dst, send_sem=send_sems.at[6 * a + k],
                                                recv_sem=recv_sems.at[6 * a + k], device_id=to, device_id_type=MESH)

        started = []
        for a in range(n):
            for k, chip in enumerate(chips):
                cp = copy(a, k, _half(ins[a], c), _half(outs[a].at[me], c), (*chip, c))
                cp.start()
                started.append(cp)
        for a in range(n):
            for k, chip in enumerate(chips):
                slot = _half(outs[a].at[2 * chip[0] + chip[1]], c)
                copy(a, k, slot, slot, sib).wait_recv()
                cp = copy(a, 3 + k, slot, slot, sib)
                cp.start()
                started.append(cp)
        for a in range(n):
            for k, chip in enumerate(chips):
                slot = _half(outs[a].at[2 * chip[0] + chip[1]], 1 - c)
                copy(a, 3 + k, slot, slot, sib).wait_recv()
        for cp in started:
            cp.wait_send()
        for cp in local:
            cp.wait()

    return pl.pallas_call(
        body, name="all_gather_weights",
        out_shape=[jax.ShapeDtypeStruct((N_CHIPS,) + s.shape, s.dtype) for s in shards],
        in_specs=[ANY] * n, out_specs=[ANY] * n,
        scratch_shapes=[pltpu.SemaphoreType.DMA((6 * n,)), pltpu.SemaphoreType.DMA((6 * n,)),
                        pltpu.SemaphoreType.DMA((n,))],
    )(*shards)


def _exchange_halves(grads):
    n = len(grads)

    def body(*refs):
        ins, keeps, recvs = refs[:n], refs[n:2 * n], refs[2 * n:3 * n]
        send_sems, recv_sems, loc_sems = refs[3 * n:]
        x, y, c, _ = _mesh_pos()
        sib = (x, y, 1 - c)

        def rows(ref, h):
            hr = ref.shape[1] // 2
            return ref.at[:, pl.ds(h * hr, hr), :]

        local = [pltpu.make_async_copy(rows(ins[a], c), keeps[a], loc_sems.at[a]) for a in range(n)]
        remote = [pltpu.make_async_remote_copy(src_ref=rows(ins[a], 1 - c), dst_ref=recvs[a],
                                               send_sem=send_sems.at[a], recv_sem=recv_sems.at[a],
                                               device_id=sib, device_id_type=MESH) for a in range(n)]
        for cp in remote + local:
            cp.start()
        for cp in remote:
            cp.wait()
        for cp in local:
            cp.wait()

    half = [jax.ShapeDtypeStruct((g.shape[0], g.shape[1] // 2, g.shape[2]), g.dtype) for g in grads]
    outs = pl.pallas_call(
        body, name="grad_exchange_halves", out_shape=half + half,
        in_specs=[ANY] * n, out_specs=[ANY] * (2 * n),
        scratch_shapes=[pltpu.SemaphoreType.DMA((n,)), pltpu.SemaphoreType.DMA((n,)), pltpu.SemaphoreType.DMA((n,))],
    )(*grads)
    return outs[:n], outs[n:]


def _scatter_to_chips(sums_f32, sums_bf16):
    n = len(sums_f32)

    def body(*refs):
        s32, sbf = refs[:n], refs[n:2 * n]
        owns, rbufs = refs[2 * n:3 * n], refs[3 * n:4 * n]
        send_sems, recv_sems, loc_sems = refs[4 * n:]
        x, y, c, chips = _mesh_pos()
        me = 2 * x + y
        local = [pltpu.make_async_copy(s32[a].at[me], owns[a], loc_sems.at[a]) for a in range(n)]
        remote = []
        for a in range(n):
            for k, chip in enumerate(chips):
                remote.append(pltpu.make_async_remote_copy(
                    src_ref=sbf[a].at[2 * chip[0] + chip[1]], dst_ref=rbufs[a].at[k],
                    send_sem=send_sems.at[3 * a + k], recv_sem=recv_sems.at[3 * a + k],
                    device_id=(*chip, c), device_id_type=MESH))
        for cp in remote + local:
            cp.start()
        for cp in remote:
            cp.wait()
        for cp in local:
            cp.wait()

    own_shapes = [jax.ShapeDtypeStruct(s.shape[1:], F32) for s in sums_f32]
    rbuf_shapes = [jax.ShapeDtypeStruct((3,) + s.shape[1:], BF16) for s in sums_f32]
    outs = pl.pallas_call(
        body, name="grad_scatter_to_chips", out_shape=own_shapes + rbuf_shapes,
        in_specs=[ANY] * (2 * n), out_specs=[ANY] * (2 * n),
        scratch_shapes=[pltpu.SemaphoreType.DMA((3 * n,)), pltpu.SemaphoreType.DMA((3 * n,)),
                        pltpu.SemaphoreType.DMA((n,))],
    )(*sums_f32, *sums_bf16)
    return outs[:n], outs[n:]


def _gather_halves(reduced):
    n = len(reduced)

    def body(*refs):
        ins, outs = refs[:n], refs[n:2 * n]
        send_sems, recv_sems, loc_sems = refs[2 * n:]
        x, y, c, _ = _mesh_pos()
        sib = (x, y, 1 - c)
        local = [pltpu.make_async_copy(ins[a], _half(outs[a], c), loc_sems.at[a]) for a in range(n)]
        remote = [pltpu.make_async_remote_copy(src_ref=ins[a], dst_ref=_half(outs[a], c),
                                               send_sem=send_sems.at[a], recv_sem=recv_sems.at[a],
                                               device_id=sib, device_id_type=MESH) for a in range(n)]
        for cp in remote + local:
            cp.start()
        for a in range(n):
            pltpu.make_async_remote_copy(src_ref=ins[a], dst_ref=_half(outs[a], 1 - c), send_sem=send_sems.at[a],
                                         recv_sem=recv_sems.at[a], device_id=sib, device_id_type=MESH).wait_recv()
        for cp in remote:
            cp.wait_send()
        for cp in local:
            cp.wait()

    return pl.pallas_call(
        body, name="grad_gather_halves",
        out_shape=[jax.ShapeDtypeStruct((2 * r.shape[0], r.shape[1]), F32) for r in reduced],
        in_specs=[ANY] * n, out_specs=[ANY] * n,
        scratch_shapes=[pltpu.SemaphoreType.DMA((n,)), pltpu.SemaphoreType.DMA((n,)), pltpu.SemaphoreType.DMA((n,))],
    )(*reduced)


SMALL_A_ROWS = 24
SMALL_B_ROWS = 8
SMALL_C_ROWS = N_POOL_GROUPS * POOL_GROUP


def _all_reduce_small(dg1, dg1m, dg2, dg3, dg4, lossp, dmeta, dscale, dconv, dpoolw):
    def body(dg1_ref, dg1m_ref, dg2_ref, dg3_ref, dg4_ref, loss_ref, dmeta_ref, dsc_ref, dcw_ref, dpw_ref,
             a_out, b_out, c_out, a_buf, b_buf, c_buf, a_rcv, b_rcv, c_rcv, send_sems, recv_sems):
        x, y, c, _ = _mesh_pos()
        peers = [(x, y, 1 - c), (1 - x, y, c), (x, 1 - y, c)]

        def rowsum(v):
            return jnp.sum(v, axis=0, keepdims=True)

        a_buf[0, 0:1, :] = rowsum(dg1_ref[...] + dg1m_ref[...])
        a_buf[0, 1:2, :] = rowsum(dg2_ref[...])
        a_buf[0, 2:3, :] = rowsum(dg3_ref[...])
        a_buf[0, 3:4, :] = rowsum(dg4_ref[...])
        loss = jnp.sum(rowsum(loss_ref[...]), axis=1, keepdims=True) * (0.5 / D_MODEL)
        a_buf[0, 4:5, :] = jnp.broadcast_to(loss, (1, D_MODEL))
        a_buf[0, 5:8, :] = jnp.zeros((3, D_MODEL), F32)
        a_buf[0, 8:24, :] = dmeta_ref[...]
        b_buf[0, 0:1, :] = rowsum(dsc_ref[...])
        for k in range(3):
            b_buf[0, 1 + k:2 + k, :] = rowsum(dcw_ref[8 * k:8 * k + 8, :])
        b_buf[0, 4:8, :] = jnp.zeros((4, D_CONV), F32)
        c_buf[0] = dpw_ref[...]

        for st, peer in enumerate(peers):
            cps = []
            for i, (buf, rcv) in enumerate(((a_buf, a_rcv), (b_buf, b_rcv), (c_buf, c_rcv))):
                cps.append(pltpu.make_async_remote_copy(
                    src_ref=buf.at[st], dst_ref=rcv.at[st], send_sem=send_sems.at[3 * st + i],
                    recv_sem=recv_sems.at[3 * st + i], device_id=peer, device_id_type=MESH))
            for cp in cps:
                cp.start()
            for cp in cps:
                cp.wait()
            if st < 2:
                a_buf[st + 1] = a_buf[st] + a_rcv[st]
                b_buf[st + 1] = b_buf[st] + b_rcv[st]
                c_buf[st + 1] = c_buf[st] + c_rcv[st]
            else:
                a_out[...] = a_buf[st] + a_rcv[st]
                b_out[...] = b_buf[st] + b_rcv[st]
                c_out[...] = c_buf[st] + c_rcv[st]

    vm = pl.BlockSpec(memory_space=pltpu.VMEM)
    shapes = [(SMALL_A_ROWS, D_MODEL), (SMALL_B_ROWS, D_CONV), (SMALL_C_ROWS, POOL_GROUP)]
    return pl.pallas_call(
        body, name="all_reduce_small",
        out_shape=[jax.ShapeDtypeStruct(s, F32) for s in shapes],
        in_specs=[vm] * 10, out_specs=[vm] * 3,
        scratch_shapes=[pltpu.VMEM((3,) + s, F32) for s in shapes] + [pltpu.VMEM((3,) + s, F32) for s in shapes]
        + [pltpu.SemaphoreType.DMA((9,)), pltpu.SemaphoreType.DMA((9,))],
    )(dg1, dg1m, dg2, dg3, dg4, lossp, dmeta, dscale, dconv, dpoolw)


def _row_block(rows):
    for cand in (512, 448, 384, 352, 320, 256, 128, 64, 32, 16):
        if rows % cand == 0:
            return cand
    return rows


def _add_pairs(keep, recv):
    shape = keep.shape
    rows, cols = shape[0] * shape[1], shape[2]
    br = _row_block(rows)

    def body(a_ref, b_ref, o32_ref, obf_ref):
        s = a_ref[...] + b_ref[...]
        o32_ref[...] = s
        obf_ref[...] = s.astype(BF16)

    spec = pl.BlockSpec((br, cols), lambda i: (i, 0))
    o32, obf = pl.pallas_call(
        body, name="grad_add_pairs", grid=(rows // br,),
        out_shape=[jax.ShapeDtypeStruct((rows, cols), F32), jax.ShapeDtypeStruct((rows, cols), BF16)],
        in_specs=[spec, spec], out_specs=[spec, spec], compiler_params=_cparams(1),
    )(keep.reshape(rows, cols), recv.reshape(rows, cols))
    return o32.reshape(shape), obf.reshape(shape)


def _add_chips(own, rbuf):
    rows, cols = own.shape
    br = _row_block(rows)

    def body(o_ref, r_ref, out_ref):
        out_ref[...] = ((o_ref[...] + r_ref[0].astype(F32)) + r_ref[1].astype(F32)) + r_ref[2].astype(F32)

    return pl.pallas_call(
        body, name="grad_add_chips", grid=(rows // br,),
        out_shape=jax.ShapeDtypeStruct((rows, cols), F32),
        in_specs=[pl.BlockSpec((br, cols), lambda i: (i, 0)), pl.BlockSpec((3, br, cols), lambda i: (0, i, 0))],
        out_specs=pl.BlockSpec((br, cols), lambda i: (i, 0)), compiler_params=_cparams(1),
    )(own, rbuf)


def _adamw_math(w, g, m, v):
    m2 = ADAM_B1 * m + (1.0 - ADAM_B1) * g
    v2 = ADAM_B2 * v + (1.0 - ADAM_B2) * (g * g)
    m_hat = m2 / (1.0 - ADAM_B1 ** ADAM_STEP)
    v_hat = v2 / (1.0 - ADAM_B2 ** ADAM_STEP)
    delta = -ADAM_LR * (m_hat / (jnp.sqrt(v_hat) + ADAM_EPS) + ADAM_WD * w)
    return delta, m2, v2


def _adamw_big(w, g, m, v):
    rows, cols = w.shape
    br = _row_block(rows)

    def body(w_ref, g_ref, m_ref, v_ref, d_ref, m2_ref, v2_ref):
        d, m2, v2 = _adamw_math(w_ref[...], g_ref[...], m_ref[...], v_ref[...])
        d_ref[...] = d
        m2_ref[...] = m2
        v2_ref[...] = v2

    spec = pl.BlockSpec((br, cols), lambda i: (i, 0))
    return pl.pallas_call(
        body, name="adamw_big", grid=(rows // br,),
        out_shape=[jax.ShapeDtypeStruct((rows, cols), F32)] * 3,
        in_specs=[spec] * 4, out_specs=[spec] * 3, compiler_params=_cparams(1),
    )(w, g, m, v)


def _adamw_small(groups):
    n = len(groups)

    def body(*refs):
        ins, outs = refs[:4 * n], refs[4 * n:]
        for i in range(n):
            w, g, m, v = (r[...] for r in ins[4 * i:4 * i + 4])
            d, m2, v2 = _adamw_math(w, g, m, v)
            outs[3 * i][...] = d
            outs[3 * i + 1][...] = m2
            outs[3 * i + 2][...] = v2

    vm = pl.BlockSpec(memory_space=pltpu.VMEM)
    flat = [a for grp in groups for a in grp]
    out_shape = [jax.ShapeDtypeStruct(grp[0].shape, F32) for grp in groups for _ in range(3)]
    outs = pl.pallas_call(body, name="adamw_small", out_shape=out_shape,
                          in_specs=[vm] * (4 * n), out_specs=[vm] * (3 * n))(*flat)
    return [tuple(outs[3 * i:3 * i + 3]) for i in range(n)]


def _load_weights(pairs, sem):
    for src, dst in pairs:
        cp = pltpu.make_async_copy(src, dst, sem)
        cp.start()
        cp.wait()


def _meta_fwd(meta_full, g1, win_all):
    def body(meta_ref, g1_ref, win_ref, z_ref):
        xm = meta_ref[...]
        a = (xm * _rstd(xm) * g1_ref[...]).astype(BF16)
        for j in range(N_CHIPS):
            z_ref[:, j * IN_SHARD:(j + 1) * IN_SHARD] = _dot(a, win_ref[j])

    vm = pl.BlockSpec(memory_space=pltpu.VMEM)
    return pl.pallas_call(body, name="meta_fwd", out_shape=jax.ShapeDtypeStruct((N_META, D_IN_PROJ), F32),
                          in_specs=[vm] * 3, out_specs=vm)(meta_full, g1, win_all)


def _mixer_fwd(x3, zmeta, g1, g2, convw, poolw, pscale, win_all, wout):
    n_seq, seq, _ = x3.shape
    tm = min(TM_MIX_FWD, seq)
    n_t = seq // tm

    def body(x_ref, zm_ref, g1_ref, g2_ref, cw_ref, pw_ref, ps_ref, win_hbm, wout_hbm,
             z_ref, m_ref, h1_ref, win_v, wout_v, cvb, pb, sem):
        s, t = pl.program_id(0), pl.program_id(1)

        @pl.when((s == 0) & (t == 0))
        def _():
            _load_weights([(win_hbm, win_v), (wout_hbm, wout_v)], sem)

        xt = x_ref[0]
        a = (xt * _rstd(xt) * g1_ref[...]).astype(BF16)
        zb = _dot(a, win_v[0])
        zc = _dot(a, win_v[1])
        zv = _dot(a, win_v[2])
        zp = _dot(a, win_v[3])
        z_ref[0, :, 0:IN_SHARD] = zb
        z_ref[0, :, IN_SHARD:2 * IN_SHARD] = zc
        z_ref[0, :, 2 * IN_SHARD:3 * IN_SHARD] = zv
        z_ref[0, :, 3 * IN_SHARD:4 * IN_SHARD] = zp

        @pl.when(t == 0)
        def _():
            cvb[0:HALO, :] = zm_ref[:, IN_SHARD:2 * IN_SHARD] * zm_ref[:, 2 * IN_SHARD:3 * IN_SHARD]
            pb[0:HALO, :] = zm_ref[:, 3 * IN_SHARD:4 * IN_SHARD]

        @pl.when(t > 0)
        def _():
            cvb[0:HALO, :] = cvb[tm:tm + HALO, :]
            pb[0:HALO, :] = pb[tm:tm + HALO, :]

        cv = zc * zv
        cvb[HALO:HALO + tm, :] = cv
        pb[HALO:HALO + tm, :] = zp
        cw = cw_ref[...]
        conv = cw[0:1] * cvb[HALO - 2:HALO - 2 + tm, :] + cw[1:2] * cvb[HALO - 1:HALO - 1 + tm, :] + cw[2:3] * cv
        parts = [(zb * conv).astype(BF16)]
        for g in range(N_POOL_GROUPS):
            pooled = _pool_fwd(pb, g, tm).astype(BF16)
            parts.append((_dot(pooled, pw_ref[g]) * ps_ref[:, _gcols(g)]).astype(BF16))
        m = _dot(jnp.concatenate(parts, axis=1), wout_v[...])
        m_ref[0] = m
        h1_ref[0] = xt + m * _rstd(m) * g2_ref[...]

    row = lambda c: pl.BlockSpec((1, tm, c), lambda s, t: (s, t, 0))
    return pl.pallas_call(
        body, name="mixer_fwd", grid=(n_seq, n_t),
        out_shape=[jax.ShapeDtypeStruct((n_seq, seq, D_IN_PROJ), F32), jax.ShapeDtypeStruct((n_seq, seq, D_MODEL), F32),
                   jax.ShapeDtypeStruct((n_seq, seq, D_MODEL), F32)],
        in_specs=[row(D_MODEL), _full((N_META, D_IN_PROJ)), _full((1, D_MODEL)), _full((1, D_MODEL)),
                  _full((3, D_CONV)), _full((N_POOL_GROUPS, POOL_GROUP, POOL_GROUP)), _full((1, D_POOL)), ANY, ANY],
        out_specs=[row(D_IN_PROJ), row(D_MODEL), row(D_MODEL)],
        scratch_shapes=[pltpu.VMEM((N_CHIPS, D_MODEL, IN_SHARD), BF16), pltpu.VMEM((D_MODEL, D_MODEL), BF16),
                        pltpu.VMEM((HALO + tm, D_CONV), F32), pltpu.VMEM((HALO + tm, D_POOL), F32),
                        pltpu.SemaphoreType.DMA],
        compiler_params=_cparams(2),
    )(x3, zmeta, g1, g2, convw, poolw, pscale, win_all, wout)


def _ffn_fwd_bwd(h1, target, g3, g4, wg_all, wu_all, wd_all):
    n_rows = h1.shape[0]
    tm = min(TM_FFN, n_rows)

    def body(h1_ref, t_ref, g3_ref, g4_ref, wg_hbm, wu_hbm, wd_hbm,
             dh1_ref, f_ref, dd_ref, ds_ref, du_ref, gg_ref, loss_ref, dg3_ref, dg4_ref,
             wg_v, wu_v, wd_v, s_sc, u_sc, sem):
        @pl.when(pl.program_id(0) == 0)
        def _():
            _load_weights([(wg_hbm, wg_v), (wu_hbm, wu_v), (wd_hbm, wd_v)], sem)
            loss_ref[...] = jnp.zeros_like(loss_ref)
            dg3_ref[...] = jnp.zeros_like(dg3_ref)
            dg4_ref[...] = jnp.zeros_like(dg4_ref)

        h1v = h1_ref[...]
        r3 = _rstd(h1v)
        hh = h1v * r3
        g3v, g4v = g3_ref[...], g4_ref[...]
        f = (hh * g3v).astype(BF16)
        f_ref[...] = f
        d = jnp.zeros((tm, D_MODEL), F32)
        for j in range(N_CHIPS):
            s = _dot(f, wg_v[j])
            u = _dot(f, wu_v[j])
            s_sc[j] = s
            u_sc[j] = u
            gj = (s * _sigmoid(s) * u).astype(BF16)
            gg_ref[j] = gj
            d = d + _dot(gj, wd_v[j])
        r4 = _rstd(d)
        dh = d * r4
        err = (h1v + dh * g4v) - t_ref[...]
        loss_ref[...] += _rows8(err * err)
        dy = err * (1.0 / D_MODEL)
        dg4_ref[...] += _rows8(dy * dh)
        ddb = _rms_bwd(dy, dh, r4, g4v).astype(BF16)
        dd_ref[...] = ddb
        df = jnp.zeros((tm, D_MODEL), F32)
        for j in range(N_CHIPS):
            dgg = _dot_nt(ddb, wd_v[j])
            s = s_sc[j]
            u = u_sc[j]
            sig = _sigmoid(s)
            dsj = (dgg * u * (sig * (1.0 + s * (1.0 - sig)))).astype(BF16)
            duj = (dgg * (s * sig)).astype(BF16)
            ds_ref[j] = dsj
            du_ref[j] = duj
            df = df + _dot_nt(dsj, wg_v[j]) + _dot_nt(duj, wu_v[j])
        dg3_ref[...] += _rows8(df * hh)
        dh1_ref[...] = dy + _rms_bwd(df, hh, r3, g3v)

    row = pl.BlockSpec((tm, D_MODEL), lambda i: (i, 0))
    ffrow = pl.BlockSpec((N_CHIPS, tm, FF_SHARD), lambda i: (0, i, 0))
    acc = _full((8, D_MODEL))
    act_bf = jax.ShapeDtypeStruct((n_rows, D_MODEL), BF16)
    ff_bf = jax.ShapeDtypeStruct((N_CHIPS, n_rows, FF_SHARD), BF16)
    acc_shape = jax.ShapeDtypeStruct((8, D_MODEL), F32)
    return pl.pallas_call(
        body, name="ffn_fwd_bwd", grid=(n_rows // tm,),
        out_shape=[jax.ShapeDtypeStruct((n_rows, D_MODEL), F32), act_bf, act_bf, ff_bf, ff_bf, ff_bf,
                   acc_shape, acc_shape, acc_shape],
        in_specs=[row, row, _full((1, D_MODEL)), _full((1, D_MODEL)), ANY, ANY, ANY],
        out_specs=[row, row, row, ffrow, ffrow, ffrow, acc, acc, acc],
        scratch_shapes=[pltpu.VMEM((N_CHIPS, D_MODEL, FF_SHARD), BF16), pltpu.VMEM((N_CHIPS, D_MODEL, FF_SHARD), BF16),
                        pltpu.VMEM((N_CHIPS, FF_SHARD, D_MODEL), BF16),
                        pltpu.VMEM((N_CHIPS, tm, FF_SHARD), F32), pltpu.VMEM((N_CHIPS, tm, FF_SHARD), F32),
                        pltpu.SemaphoreType.DMA],
        compiler_params=_cparams(1),
    )(h1, target, g3, g4, wg_all, wu_all, wd_all)


def _ffn_weight_grads(f, ds, du, gg, dd):
    n_rows = f.shape[0]
    tk = min(TK_DW, n_rows)

    def body(f_ref, ds_ref, du_ref, gg_ref, dd_ref, dwg_ref, dwu_ref, dwd_ref):
        @pl.when(pl.program_id(1) == 0)
        def _():
            dwg_ref[...] = jnp.zeros_like(dwg_ref)
            dwu_ref[...] = jnp.zeros_like(dwu_ref)
            dwd_ref[...] = jnp.zeros_like(dwd_ref)

        fv = f_ref[...]
        dwg_ref[0] += _dot_tn(fv, ds_ref[0])
        dwu_ref[0] += _dot_tn(fv, du_ref[0])
        dwd_ref[0] += _dot_tn(gg_ref[0], dd_ref[...])

    row = pl.BlockSpec((tk, D_MODEL), lambda j, k: (k, 0))
    ffrow = pl.BlockSpec((1, tk, FF_SHARD), lambda j, k: (j, k, 0))
    return pl.pallas_call(
        body, name="ffn_weight_grads", grid=(N_CHIPS, n_rows // tk),
        out_shape=[jax.ShapeDtypeStruct((N_CHIPS, D_MODEL, FF_SHARD), F32)] * 2
        + [jax.ShapeDtypeStruct((N_CHIPS, FF_SHARD, D_MODEL), F32)],
        in_specs=[row, ffrow, ffrow, ffrow, row],
        out_specs=[pl.BlockSpec((1, D_MODEL, FF_SHARD), lambda j, k: (j, 0, 0))] * 2
        + [pl.BlockSpec((1, FF_SHARD, D_MODEL), lambda j, k: (j, 0, 0))],
        compiler_params=_cparams(2),
    )(f, ds, du, gg, dd)


def _mixer_bwd(dh1, m3, z3, x3, zmeta, g1, g2, convw, poolw, pscale, win_all, wout):
    n_seq, seq, _ = x3.shape
    tm = min(TM_MIX_BWD, seq)
    n_t = seq // tm
    hb = tm // HALO

    def body(dh1_ref, m_ref, z_ref, zh_ref, x_ref, zm_ref, g1_ref, g2_ref, cw_ref, pw_ref, ps_ref, win_hbm, wout_hbm,
             dx_ref, dz_ref, a_ref, yc_ref, dm_ref, dg1_ref, dg2_ref, dsc_ref, dcw_ref, dpw_ref, dzm_ref,
             win_v, wout_v, cvb, pb, dcb, dqb, mcb, mqb, sem):
        s, i = pl.program_id(0), pl.program_id(1)
        tr = n_t - 1 - i

        @pl.when((s == 0) & (i == 0))
        def _():
            _load_weights([(win_hbm, win_v), (wout_hbm, wout_v)], sem)
            for ref in (dg1_ref, dg2_ref, dsc_ref, dcw_ref, dpw_ref, dzm_ref):
                ref[...] = jnp.zeros_like(ref)

        @pl.when(i == 0)
        def _():
            dcb[tm:tm + HALO, :] = jnp.zeros((HALO, D_CONV), F32)
            dqb[tm:tm + HALO, :] = jnp.zeros((HALO, D_POOL), F32)

        @pl.when(i > 0)
        def _():
            dcb[tm:tm + HALO, :] = dcb[0:HALO, :]
            dqb[tm:tm + HALO, :] = dqb[0:HALO, :]

        g1v, g2v = g1_ref[...], g2_ref[...]
        dh1v = dh1_ref[0]
        mv = m_ref[0]
        r2 = _rstd(mv)
        mh = mv * r2
        dg2_ref[...] += _rows8(dh1v * mh)
        dmb = _rms_bwd(dh1v, mh, r2, g2v).astype(BF16)
        dm_ref[...] = dmb
        dyc = _dot_nt(dmb, wout_v[...])
        dyconv = dyc[:, 0:D_CONV]

        zb = z_ref[0, :, 0:IN_SHARD]
        zc = z_ref[0, :, IN_SHARD:2 * IN_SHARD]
        zv = z_ref[0, :, 2 * IN_SHARD:3 * IN_SHARD]
        zp = z_ref[0, :, 3 * IN_SHARD:4 * IN_SHARD]
        halo = jnp.where(tr == 0, zm_ref[...], zh_ref[0])
        cvb[0:HALO, :] = halo[:, IN_SHARD:2 * IN_SHARD] * halo[:, 2 * IN_SHARD:3 * IN_SHARD]
        pb[0:HALO, :] = halo[:, 3 * IN_SHARD:4 * IN_SHARD]
        cv0 = zc * zv
        cvb[HALO:HALO + tm, :] = cv0
        pb[HALO:HALO + tm, :] = zp
        cw = cw_ref[...]
        cv2 = cvb[HALO - 2:HALO - 2 + tm, :]
        cv1 = cvb[HALO - 1:HALO - 1 + tm, :]
        conv = cw[0:1] * cv2 + cw[1:2] * cv1 + cw[2:3] * cv0
        parts = [(zb * conv).astype(BF16)]
        for g in range(N_POOL_GROUPS):
            pooled = _pool_fwd(pb, g, tm).astype(BF16)
            mixed = _dot(pooled, pw_ref[g])
            sc = ps_ref[:, _gcols(g)]
            parts.append((mixed * sc).astype(BF16))
            dyp = dyc[:, D_CONV + g * POOL_GROUP:D_CONV + (g + 1) * POOL_GROUP]
            dsc_ref[:, _gcols(g)] += _rows8(dyp * mixed)
            dmix = (dyp * sc).astype(BF16)
            dpw_ref[g] += _dot_tn(pooled, dmix)
            dqb[0:tm, _gcols(g)] = _dot_nt(dmix, pw_ref[g])
        yc_ref[...] = jnp.concatenate(parts, axis=1)

        dconv = dyconv * zb
        dcb[0:tm, :] = dconv
        dcv = cw[2:3] * dconv + cw[1:2] * dcb[1:1 + tm, :] + cw[0:1] * dcb[2:2 + tm, :]
        dcw_ref[0:8, :] += _rows8(dconv * cv2)
        dcw_ref[8:16, :] += _rows8(dconv * cv1)
        dcw_ref[16:24, :] += _rows8(dconv * cv0)
        dzs = [(dyconv * conv).astype(BF16), (dcv * zv).astype(BF16), (dcv * zc).astype(BF16),
               jnp.concatenate([_pool_bwd(dqb, g, tm) for g in range(N_POOL_GROUPS)], axis=1).astype(BF16)]
        da = jnp.zeros((tm, D_MODEL), F32)
        for j in range(N_CHIPS):
            dz_ref[j] = dzs[j]
            da = da + _dot_nt(dzs[j], win_v[j])
        xt = x_ref[0]
        r1 = _rstd(xt)
        xh = xt * r1
        a_ref[...] = (xh * g1v).astype(BF16)
        dg1_ref[...] += _rows8(da * xh)
        dx_ref[0] = dh1v + _rms_bwd(da, xh, r1, g1v)

        @pl.when(tr == 0)
        def _():
            mcb[0:HALO, :] = jnp.zeros((HALO, D_CONV), F32)
            mqb[0:HALO, :] = jnp.zeros((HALO, D_POOL), F32)
            mcb[HALO:2 * HALO, :] = dcb[0:HALO, :]
            mqb[HALO:2 * HALO, :] = dqb[0:HALO, :]
            dcv_m = cw[1:2] * mcb[1:1 + HALO, :] + cw[0:1] * mcb[2:2 + HALO, :]
            dzm_ref[:, IN_SHARD:2 * IN_SHARD] += dcv_m * zm_ref[:, 2 * IN_SHARD:3 * IN_SHARD]
            dzm_ref[:, 2 * IN_SHARD:3 * IN_SHARD] += dcv_m * zm_ref[:, IN_SHARD:2 * IN_SHARD]
            dzm_ref[:, 3 * IN_SHARD:4 * IN_SHARD] += jnp.concatenate(
                [_pool_bwd(mqb, g, HALO) for g in range(N_POOL_GROUPS)], axis=1)

    row3 = lambda c: pl.BlockSpec((1, tm, c), lambda s, i: (s, n_t - 1 - i, 0))
    row2 = lambda c: pl.BlockSpec((tm, c), lambda s, i: (s * n_t + n_t - 1 - i, 0))
    halo_spec = pl.BlockSpec((1, HALO, D_IN_PROJ), lambda s, i: (s, jnp.maximum((n_t - 1 - i) * hb - 1, 0), 0))
    n_rows = n_seq * seq
    act_bf = jax.ShapeDtypeStruct((n_rows, D_MODEL), BF16)
    return pl.pallas_call(
        body, name="mixer_bwd", grid=(n_seq, n_t),
        out_shape=[jax.ShapeDtypeStruct((n_seq, seq, D_MODEL), F32),
                   jax.ShapeDtypeStruct((N_CHIPS, n_rows, IN_SHARD), BF16), act_bf, act_bf, act_bf,
                   jax.ShapeDtypeStruct((8, D_MODEL), F32), jax.ShapeDtypeStruct((8, D_MODEL), F32),
                   jax.ShapeDtypeStruct((8, D_POOL), F32), jax.ShapeDtypeStruct((24, D_CONV), F32),
                   jax.ShapeDtypeStruct((N_POOL_GROUPS, POOL_GROUP, POOL_GROUP), F32),
                   jax.ShapeDtypeStruct((N_META, D_IN_PROJ), F32)],
        in_specs=[row3(D_MODEL), row3(D_MODEL), row3(D_IN_PROJ), halo_spec, row3(D_MODEL),
                  _full((N_META, D_IN_PROJ)), _full((1, D_MODEL)), _full((1, D_MODEL)), _full((3, D_CONV)),
                  _full((N_POOL_GROUPS, POOL_GROUP, POOL_GROUP)), _full((1, D_POOL)), ANY, ANY],
        out_specs=[row3(D_MODEL), pl.BlockSpec((N_CHIPS, tm, IN_SHARD), lambda s, i: (0, s * n_t + n_t - 1 - i, 0)),
                   row2(D_MODEL), row2(D_MODEL), row2(D_MODEL),
                   _full((8, D_MODEL)), _full((8, D_MODEL)), _full((8, D_POOL)), _full((24, D_CONV)),
                   _full((N_POOL_GROUPS, POOL_GROUP, POOL_GROUP)), _full((N_META, D_IN_PROJ))],
        scratch_shapes=[pltpu.VMEM((N_CHIPS, D_MODEL, IN_SHARD), BF16), pltpu.VMEM((D_MODEL, D_MODEL), BF16),
                        pltpu.VMEM((HALO + tm, D_CONV), F32), pltpu.VMEM((HALO + tm, D_POOL), F32),
                        pltpu.VMEM((tm + HALO, D_CONV), F32), pltpu.VMEM((tm + HALO, D_POOL), F32),
                        pltpu.VMEM((2 * HALO, D_CONV), F32), pltpu.VMEM((2 * HALO, D_POOL), F32),
                        pltpu.SemaphoreType.DMA],
        compiler_params=_cparams(2),
    )(dh1, m3, z3, z3, x3, zmeta, g1, g2, convw, poolw, pscale, win_all, wout)


def _meta_bwd(dzm, meta_full, g1, win_all):
    def body(dzm_ref, meta_ref, g1_ref, win_ref, dmeta_ref, dg1_ref, a_ref, dzb_ref):
        xm = meta_ref[...]
        r = _rstd(xm)
        xh = xm * r
        g1v = g1_ref[...]
        a_ref[...] = (xh * g1v).astype(BF16)
        da = jnp.zeros((N_META, D_MODEL), F32)
        for j in range(N_CHIPS):
            dzj = dzm_ref[:, j * IN_SHARD:(j + 1) * IN_SHARD].astype(BF16)
            dzb_ref[j] = dzj
            da = da + _dot_nt(dzj, win_ref[j])
        dg1_ref[...] = _rows8(da * xh)
        dmeta_ref[...] = _rms_bwd(da, xh, r, g1v)

    vm = pl.BlockSpec(memory_space=pltpu.VMEM)
    return pl.pallas_call(
        body, name="meta_bwd",
        out_shape=[jax.ShapeDtypeStruct((N_META, D_MODEL), F32), jax.ShapeDtypeStruct((8, D_MODEL), F32),
                   jax.ShapeDtypeStruct((N_META, D_MODEL), BF16), jax.ShapeDtypeStruct((N_CHIPS, N_META, IN_SHARD), BF16)],
        in_specs=[vm] * 4, out_specs=[vm] * 4,
    )(dzm, meta_full, g1, win_all)


def _mixer_weight_grads(a, dz, ycat, dm, a_meta, dz_meta):
    n_rows = a.shape[0]
    tk = min(TK_DW, n_rows)

    def body(a_ref, dz_ref, yc_ref, dm_ref, am_ref, dzm_ref, dwin_ref, dwout_ref):
        @pl.when(pl.program_id(1) == 0)
        def _():
            dwin_ref[0] = _dot_tn(am_ref[...], dzm_ref[0])
            dwout_ref[...] = jnp.zeros_like(dwout_ref)

        dwin_ref[0] += _dot_tn(a_ref[...], dz_ref[0])
        dwout_ref[0] += _dot_tn(yc_ref[...], dm_ref[...])

    row = pl.BlockSpec((tk, D_MODEL), lambda j, k: (k, 0))
    return pl.pallas_call(
        body, name="mixer_weight_grads", grid=(N_CHIPS, n_rows // tk),
        out_shape=[jax.ShapeDtypeStruct((N_CHIPS, D_MODEL, IN_SHARD), F32),
                   jax.ShapeDtypeStruct((N_CHIPS, OUT_SHARD, D_MODEL), F32)],
        in_specs=[row, pl.BlockSpec((1, tk, IN_SHARD), lambda j, k: (j, k, 0)),
                  pl.BlockSpec((tk, OUT_SHARD), lambda j, k: (k, j)), row,
                  _full((N_META, D_MODEL)), pl.BlockSpec((1, N_META, IN_SHARD), lambda j, k: (j, 0, 0))],
        out_specs=[pl.BlockSpec((1, D_MODEL, IN_SHARD), lambda j, k: (j, 0, 0)),
                   pl.BlockSpec((1, OUT_SHARD, D_MODEL), lambda j, k: (j, 0, 0))],
        compiler_params=_cparams(2),
    )(a, dz, ycat, dm, a_meta, dz_meta)


def kernel(x, meta_tokens, norm_mix_pre, w_in, conv_w, pool_w, pool_scale, w_out, norm_mix_post, norm_ffn_pre, w_gate, w_up, w_down, norm_ffn_post, loss_target, m_meta_tokens, m_norm_mix_pre, m_w_in, m_conv_w, m_pool_w, m_pool_scale, m_w_out, m_norm_mix_post, m_norm_ffn_pre, m_w_gate, m_w_up, m_w_down, m_norm_ffn_post, v_meta_tokens, v_norm_mix_pre, v_w_in, v_conv_w, v_pool_w, v_pool_scale, v_w_out, v_norm_mix_post, v_norm_ffn_pre, v_w_gate, v_w_up, v_w_down, v_norm_ffn_post):
    n_seq, seq, _ = x.shape
    n_rows = n_seq * seq
    chip = 2 * lax.axis_index("x") + lax.axis_index("y")
    meta_cols = D_MODEL // N_CHIPS
    conv_cols = D_CONV // N_CHIPS

    small = jnp.zeros((2 * HALO, meta_cols), F32)
    small = small.at[0:N_META, :].set(meta_tokens).at[N_META:N_META + 3, 0:conv_cols].set(conv_w[0])
    win_all, wout_all, wg_all, wu_all, wd_all, small_all = _all_gather_shards(
        [w_in[0].astype(BF16), w_out[0].astype(BF16), w_gate[0].astype(BF16), w_up[0].astype(BF16),
         w_down[0].astype(BF16), small])
    meta_full = small_all[:, 0:N_META, :].transpose(1, 0, 2).reshape(N_META, D_MODEL)
    conv_full = small_all[:, N_META:N_META + 3, 0:conv_cols].transpose(1, 0, 2).reshape(3, D_CONV)
    wout_full = wout_all.reshape(D_MODEL, D_MODEL)
    poolw_bf = pool_w[0].astype(BF16)
    pscale = pool_scale
    g1, g2, g3, g4 = norm_mix_pre, norm_mix_post, norm_ffn_pre, norm_ffn_post

    zmeta = _meta_fwd(meta_full, g1, win_all)
    z3, m3, h1 = _mixer_fwd(x, zmeta, g1, g2, conv_full, poolw_bf, pscale, win_all, wout_full)
    dh1, f_bf, dd_bf, ds_bf, du_bf, gg_bf, lossp, dg3p, dg4p = _ffn_fwd_bwd(
        h1.reshape(n_rows, D_MODEL), loss_target.reshape(n_rows, D_MODEL), g3, g4, wg_all, wu_all, wd_all)
    dwg, dwu, dwd = _ffn_weight_grads(f_bf, ds_bf, du_bf, gg_bf, dd_bf)
    (grad_x, dz_bf, a_bf, yc_bf, dm_bf, dg1p, dg2p, dscp, dcwp, dpw, dzm) = _mixer_bwd(
        dh1.reshape(n_seq, seq, D_MODEL), m3, z3, x, zmeta, g1, g2, conv_full, poolw_bf, pscale, win_all, wout_full)
    dmeta, dg1m, a_meta, dz_meta = _meta_bwd(dzm, meta_full, g1, win_all)
    dwin, dwout = _mixer_weight_grads(a_bf, dz_bf, yc_bf, dm_bf, a_meta, dz_meta)

    grads = [dwin, dwout, dwg, dwu, dwd]
    keeps, recvs = _exchange_halves(grads)
    sums = [_add_pairs(k, r) for k, r in zip(keeps, recvs)]
    owns, rbufs = _scatter_to_chips([s[0] for s in sums], [s[1] for s in sums])
    reduced = [_add_chips(o, r) for o, r in zip(owns, rbufs)]
    g_win, g_wout, g_wg, g_wu, g_wd = _gather_halves(reduced)

    a_red, b_red, c_red = _all_reduce_small(dg1p, dg1m, dg2p, dg3p, dg4p, lossp, dmeta, dscp, dcwp,
                                            dpw.reshape(SMALL_C_ROWS, POOL_GROUP))
    loss = a_red[4, 0]
    g_g1, g_g2, g_g3, g_g4 = a_red[0:1], a_red[1:2], a_red[2:3], a_red[3:4]
    g_meta = lax.dynamic_slice(a_red, (8, chip * meta_cols), (N_META, meta_cols))
    g_pscale = b_red[0:1]
    g_conv = lax.dynamic_slice(b_red, (1, chip * conv_cols), (3, conv_cols))
    g_poolw = c_red

    big = [(w_in, g_win, m_w_in, v_w_in), (w_out, g_wout, m_w_out, v_w_out), (w_gate, g_wg, m_w_gate, v_w_gate),
           (w_up, g_wu, m_w_up, v_w_up), (w_down, g_wd, m_w_down, v_w_down)]
    big_out = [_adamw_big(w[0], g, m[0], v[0]) for (w, g, m, v) in big]
    small_groups = [
        (meta_tokens, g_meta, m_meta_tokens, v_meta_tokens),
        (g1, g_g1, m_norm_mix_pre, v_norm_mix_pre),
        (conv_w[0], g_conv, m_conv_w[0], v_conv_w[0]),
        (pool_w.reshape(SMALL_C_ROWS, POOL_GROUP), g_poolw, m_pool_w.reshape(SMALL_C_ROWS, POOL_GROUP),
         v_pool_w.reshape(SMALL_C_ROWS, POOL_GROUP)),
        (pool_scale, g_pscale, m_pool_scale, v_pool_scale),
        (g2, g_g2, m_norm_mix_post, v_norm_mix_post),
        (g3, g_g3, m_norm_ffn_pre, v_norm_ffn_pre),
        (g4, g_g4, m_norm_ffn_post, v_norm_ffn_post),
    ]
    small_out = _adamw_small(small_groups)

    grads_out = [g_meta, g_g1, g_win[None], g_conv[None], g_poolw.reshape(pool_w.shape), g_pscale, g_wout[None],
                 g_g2, g_g3, g_wg[None], g_wu[None], g_wd[None], g_g4]
    s_meta, s_g1, s_conv, s_poolw, s_pscale, s_g2, s_g3, s_g4 = small_out
    b_win, b_wout, b_wg, b_wu, b_wd = big_out

    def leaf(k):
        return [s_meta[k], s_g1[k], b_win[k][None], s_conv[k][None], s_poolw[k].reshape(pool_w.shape), s_pscale[k],
                b_wout[k][None], s_g2[k], s_g3[k], b_wg[k][None], b_wu[k][None], b_wd[k][None], s_g4[k]]

    return (loss, grad_x, *grads_out, *leaf(0), *leaf(1), *leaf(2))
```

```python
import jax
import jax.numpy as jnp
from jax import lax
from jax.experimental import pallas as pl
from jax.experimental.pallas import tpu as pltpu

F32 = jnp.float32
BF16 = jnp.bfloat16
MESH = pl.DeviceIdType.MESH

D_MODEL = 1024
D_CONV = 512
D_POOL = 512
POOL_GROUP = 128
N_POOL_GROUPS = 4
D_IN_PROJ = 2048
D_FF = 2816
N_CHIPS = 4
FF_SHARD = D_FF // N_CHIPS
IN_SHARD = D_IN_PROJ // N_CHIPS
OUT_SHARD = D_MODEL // N_CHIPS
N_META = 16
HALO = 16
RMS_EPS = 1e-6

ADAM_LR = 0.001
ADAM_B1 = 0.9
ADAM_B2 = 0.999
ADAM_EPS = 1e-08
ADAM_WD = 0.01
ADAM_STEP = 10

TM_MIX_FWD = 512
TM_MIX_BWD = 256
TM_FFN = 256
TK_DW = 512
VMEM_LIMIT = 56 * 1024 * 1024


def _cparams(n_grid):
    return pltpu.CompilerParams(dimension_semantics=("arbitrary",) * n_grid, vmem_limit_bytes=VMEM_LIMIT)


def _dot(a, b):
    return jnp.dot(a, b, preferred_element_type=F32)


def _dot_nt(a, b):
    return lax.dot_general(a, b, (((1,), (1,)), ((), ())), preferred_element_type=F32)


def _dot_tn(a, b):
    return lax.dot_general(a, b, (((0,), (0,)), ((), ())), preferred_element_type=F32)


def _rows8(v):
    r, c = v.shape
    return v.reshape(r // 8, 8, c).sum(axis=0)


def _rstd(v):
    return lax.rsqrt(jnp.mean(v * v, axis=-1, keepdims=True) + RMS_EPS)


def _rms_bwd(dy, xhat, rstd, gain):
    dyg = dy * gain
    return rstd * (dyg - xhat * jnp.mean(dyg * xhat, axis=-1, keepdims=True))


def _sigmoid(v):
    return 1.0 / (1.0 + jnp.exp(-v))


def _gcols(g):
    return slice(g * POOL_GROUP, (g + 1) * POOL_GROUP)


def _pool_fwd(pb, g, n):
    win = 2 << g
    cur = pb[HALO:HALO + n, _gcols(g)]
    acc = cur
    for k in range(1, win):
        acc = acc + pb[HALO - k:HALO - k + n, _gcols(g)]
    return acc * (1.0 / win) - cur


def _pool_bwd(qb, g, n):
    win = 2 << g
    cur = qb[0:n, _gcols(g)]
    acc = cur
    for k in range(1, win):
        acc = acc + qb[k:k + n, _gcols(g)]
    return acc * (1.0 / win) - cur


def _full(shape):
    nd = len(shape)
    return pl.BlockSpec(shape, lambda *_: (0,) * nd)


ANY = pl.BlockSpec(memory_space=pl.ANY)


def _mesh_pos():
    x, y, c = lax.axis_index("x"), lax.axis_index("y"), lax.axis_index("c")
    chips = [(1 - x, y), (x, 1 - y), (1 - x, 1 - y)]
    return x, y, c, chips


def _half(ref, h):
    hr = ref.shape[0] // 2
    return ref.at[pl.ds(h * hr, hr), :]


def _all_gather_shards(shards):
    n = len(shards)

    def body(*refs):
        ins, outs = refs[:n], refs[n:2 * n]
        send_sems, recv_sems = refs[2 * n:]
        x, y, c, chips = _mesh_pos()
        me = 2 * x + y
        sib = (x, y, 1 - c)

        def copy(a, k, src, dst, to):
            return pltpu.make_async_remote_copy(src_ref=src, dst_ref=dst, send_sem=send_sems.at[6 * a + k],
                                                recv_sem=recv_sems.at[6 * a + k], device_id=to, device_id_type=MESH)

        started = []
        for a in range(n):
            for k, chip in enumerate(chips):
                cp = copy(a, k, _half(ins[a], c), _half(outs[a].at[me], c), (*chip, c))
                cp.start()
                started.append(cp)
        for a in range(n):
            for k, chip in enumerate(chips):
                slot = _half(outs[a].at[2 * chip[0] + chip[1]], c)
                copy(a, k, slot, slot, sib).wait_recv()
                cp = copy(a, 3 + k, slot, slot, sib)
                cp.start()
                started.append(cp)
        for a in range(n):
            for k, chip in enumerate(chips):
                slot = _half(outs[a].at[2 * chip[0] + chip[1]], 1 - c)
                copy(a, 3 + k, slot, slot, sib).wait_recv()
        for cp in started:
            cp.wait_send()

    outs = pl.pallas_call(
        body, name="all_gather_weights",
        out_shape=[jax.ShapeDtypeStruct((N_CHIPS,) + s.shape, s.dtype) for s in shards],
        in_specs=[ANY] * n, out_specs=[ANY] * n,
        scratch_shapes=[pltpu.SemaphoreType.DMA((6 * n,)), pltpu.SemaphoreType.DMA((6 * n,))],
    )(*shards)
    chip = 2 * lax.axis_index("x") + lax.axis_index("y")
    return [lax.dynamic_update_slice(o, s[None], (chip, 0, 0)) for o, s in zip(outs, shards)]


def _exchange_halves(grads):
    n = len(grads)

    def body(*refs):
        ins, recvs = refs[:n], refs[n:2 * n]
        send_sems, recv_sems = refs[2 * n:]
        x, y, c, _ = _mesh_pos()
        sib = (x, y, 1 - c)

        def rows(ref, h):
            hr = ref.shape[1] // 2
            return ref.at[:, pl.ds(h * hr, hr), :]

        remote = [pltpu.make_async_remote_copy(src_ref=rows(ins[a], 1 - c), dst_ref=recvs[a],
                                               send_sem=send_sems.at[a], recv_sem=recv_sems.at[a],
                                               device_id=sib, device_id_type=MESH) for a in range(n)]
        for cp in remote:
            cp.start()
        for cp in remote:
            cp.wait()

    return pl.pallas_call(
        body, name="grad_exchange_halves",
        out_shape=[jax.ShapeDtypeStruct((g.shape[0], g.shape[1] // 2, g.shape[2]), g.dtype) for g in grads],
        in_specs=[ANY] * n, out_specs=[ANY] * n,
        scratch_shapes=[pltpu.SemaphoreType.DMA((n,)), pltpu.SemaphoreType.DMA((n,))],
    )(*grads)


def _scatter_to_chips(sums_bf16):
    n = len(sums_bf16)

    def body(*refs):
        sbf, rbufs = refs[:n], refs[n:2 * n]
        send_sems, recv_sems = refs[2 * n:]
        x, y, c, chips = _mesh_pos()
        remote = []
        for a in range(n):
            for k, chip in enumerate(chips):
                remote.append(pltpu.make_async_remote_copy(
                    src_ref=sbf[a].at[2 * chip[0] + chip[1]], dst_ref=rbufs[a].at[k],
                    send_sem=send_sems.at[3 * a + k], recv_sem=recv_sems.at[3 * a + k],
                    device_id=(*chip, c), device_id_type=MESH))
        for cp in remote:
            cp.start()
        for cp in remote:
            cp.wait()

    return pl.pallas_call(
        body, name="grad_scatter_to_chips",
        out_shape=[jax.ShapeDtypeStruct((3,) + s.shape[1:], BF16) for s in sums_bf16],
        in_specs=[ANY] * n, out_specs=[ANY] * n,
        scratch_shapes=[pltpu.SemaphoreType.DMA((3 * n,)), pltpu.SemaphoreType.DMA((3 * n,))],
    )(*sums_bf16)


def _gather_halves(halves):
    n = len(halves)

    def body(*refs):
        ins, outs = refs[:n], refs[n:2 * n]
        send_sems, recv_sems = refs[2 * n:]
        x, y, c, _ = _mesh_pos()
        sib = (x, y, 1 - c)
        remote = [pltpu.make_async_remote_copy(src_ref=ins[a].at[c], dst_ref=outs[a].at[c],
                                               send_sem=send_sems.at[a], recv_sem=recv_sems.at[a],
                                               device_id=sib, device_id_type=MESH) for a in range(n)]
        for cp in remote:
            cp.start()
        for a in range(n):
            pltpu.make_async_remote_copy(src_ref=ins[a].at[1 - c], dst_ref=outs[a].at[1 - c], send_sem=send_sems.at[a],
                                         recv_sem=recv_sems.at[a], device_id=sib, device_id_type=MESH).wait_recv()
        for cp in remote:
            cp.wait_send()

    return pl.pallas_call(
        body, name="grad_gather_halves",
        out_shape=[jax.ShapeDtypeStruct(h.shape, F32) for h in halves],
        in_specs=[ANY] * n, out_specs=[ANY] * n, input_output_aliases={a: a for a in range(n)},
        scratch_shapes=[pltpu.SemaphoreType.DMA((n,)), pltpu.SemaphoreType.DMA((n,))],
    )(*halves)


SMALL_A_ROWS = 24
SMALL_B_ROWS = 8
SMALL_C_ROWS = N_POOL_GROUPS * POOL_GROUP


def _all_reduce_small(dg1, dg1m, dg2, dg3, dg4, lossp, dmeta, dscale, dconv, dpoolw):
    def body(dg1_ref, dg1m_ref, dg2_ref, dg3_ref, dg4_ref, loss_ref, dmeta_ref, dsc_ref, dcw_ref, dpw_ref,
             a_out, b_out, c_out, a_buf, b_buf, c_buf, a_rcv, b_rcv, c_rcv, send_sems, recv_sems):
        x, y, c, _ = _mesh_pos()
        peers = [(x, y, 1 - c), (1 - x, y, c), (x, 1 - y, c)]

        def rowsum(v):
            return jnp.sum(v, axis=0, keepdims=True)

        a_buf[0, 0:1, :] = rowsum(dg1_ref[...] + dg1m_ref[...])
        a_buf[0, 1:2, :] = rowsum(dg2_ref[...])
        a_buf[0, 2:3, :] = rowsum(dg3_ref[...])
        a_buf[0, 3:4, :] = rowsum(dg4_ref[...])
        loss = jnp.sum(rowsum(loss_ref[...]), axis=1, keepdims=True) * (0.5 / D_MODEL)
        a_buf[0, 4:5, :] = jnp.broadcast_to(loss, (1, D_MODEL))
        a_buf[0, 5:8, :] = jnp.zeros((3, D_MODEL), F32)
        a_buf[0, 8:24, :] = dmeta_ref[...]
        b_buf[0, 0:1, :] = rowsum(dsc_ref[...])
        for k in range(3):
            b_buf[0, 1 + k:2 + k, :] = rowsum(dcw_ref[8 * k:8 * k + 8, :])
        b_buf[0, 4:8, :] = jnp.zeros((4, D_CONV), F32)
        c_buf[0] = dpw_ref[...]

        for st, peer in enumerate(peers):
            cps = []
            for i, (buf, rcv) in enumerate(((a_buf, a_rcv), (b_buf, b_rcv), (c_buf, c_rcv))):
                cps.append(pltpu.make_async_remote_copy(
                    src_ref=buf.at[st], dst_ref=rcv.at[st], send_sem=send_sems.at[3 * st + i],
                    recv_sem=recv_sems.at[3 * st + i], device_id=peer, device_id_type=MESH))
            for cp in cps:
                cp.start()
            for cp in cps:
                cp.wait()
            if st < 2:
                a_buf[st + 1] = a_buf[st] + a_rcv[st]
                b_buf[st + 1] = b_buf[st] + b_rcv[st]
                c_buf[st + 1] = c_buf[st] + c_rcv[st]
            else:
                a_out[...] = a_buf[st] + a_rcv[st]
                b_out[...] = b_buf[st] + b_rcv[st]
                c_out[...] = c_buf[st] + c_rcv[st]

    vm = pl.BlockSpec(memory_space=pltpu.VMEM)
    shapes = [(SMALL_A_ROWS, D_MODEL), (SMALL_B_ROWS, D_CONV), (SMALL_C_ROWS, POOL_GROUP)]
    return pl.pallas_call(
        body, name="all_reduce_small",
        out_shape=[jax.ShapeDtypeStruct(s, F32) for s in shapes],
        in_specs=[vm] * 10, out_specs=[vm] * 3,
        scratch_shapes=[pltpu.VMEM((3,) + s, F32) for s in shapes] + [pltpu.VMEM((3,) + s, F32) for s in shapes]
        + [pltpu.SemaphoreType.DMA((9,)), pltpu.SemaphoreType.DMA((9,))],
    )(dg1, dg1m, dg2, dg3, dg4, lossp, dmeta, dscale, dconv, dpoolw)


def _row_block(rows):
    for cand in (512, 448, 384, 352, 320, 256, 128, 64, 32, 16):
        if rows % cand == 0:
            return cand
    return rows


def _add_pairs(grad, recv, place):
    n_sh, rows2, cols = grad.shape
    hr = rows2 // 2
    br = _row_block(hr)

    def body(place_ref, a_ref, b_ref, o_ref):
        o_ref[...] = (a_ref[0] + b_ref[...]).astype(BF16)

    return pl.pallas_call(
        body, name="grad_add_pairs",
        grid_spec=pltpu.PrefetchScalarGridSpec(
            num_scalar_prefetch=1, grid=(n_sh, hr // br),
            in_specs=[pl.BlockSpec((1, 1, br, cols), lambda j, i, p: (j, p[1], i, 0)),
                      pl.BlockSpec((1, br, cols), lambda j, i, p: (j, i, 0))],
            out_specs=pl.BlockSpec((1, br, cols), lambda j, i, p: (j, i, 0))),
        out_shape=jax.ShapeDtypeStruct((n_sh, hr, cols), BF16), compiler_params=_cparams(2),
    )(place, grad.reshape(n_sh, 2, hr, cols), recv)


def _add_chips(grad, recv, rbuf, place):
    n_sh, rows2, cols = grad.shape
    hr = rows2 // 2
    br = _row_block(hr)

    def body(place_ref, a_ref, b_ref, r_ref, o_ref):
        own = a_ref[0, 0] + b_ref[0]
        o_ref[0] = ((own + r_ref[0].astype(F32)) + r_ref[1].astype(F32)) + r_ref[2].astype(F32)

    return pl.pallas_call(
        body, name="grad_add_chips",
        grid_spec=pltpu.PrefetchScalarGridSpec(
            num_scalar_prefetch=1, grid=(hr // br,),
            in_specs=[pl.BlockSpec((1, 1, br, cols), lambda i, p: (p[0], p[1], i, 0)),
                      pl.BlockSpec((1, br, cols), lambda i, p: (p[0], i, 0)),
                      pl.BlockSpec((3, br, cols), lambda i, p: (0, i, 0))],
            out_specs=pl.BlockSpec((1, br, cols), lambda i, p: (p[1], i, 0))),
        out_shape=jax.ShapeDtypeStruct((2, hr, cols), F32), compiler_params=_cparams(1),
    )(place, grad.reshape(n_sh, 2, hr, cols), recv, rbuf)


def _adamw_math(w, g, m, v):
    m2 = ADAM_B1 * m + (1.0 - ADAM_B1) * g
    v2 = ADAM_B2 * v + (1.0 - ADAM_B2) * (g * g)
    m_hat = m2 / (1.0 - ADAM_B1 ** ADAM_STEP)
    v_hat = v2 / (1.0 - ADAM_B2 ** ADAM_STEP)
    delta = -ADAM_LR * (m_hat / (jnp.sqrt(v_hat) + ADAM_EPS) + ADAM_WD * w)
    return delta, m2, v2


def _adamw_big(w, g, m, v):
    rows, cols = w.shape
    br = _row_block(rows)

    def body(w_ref, g_ref, m_ref, v_ref, d_ref, m2_ref, v2_ref):
        d, m2, v2 = _adamw_math(w_ref[...], g_ref[...], m_ref[...], v_ref[...])
        d_ref[...] = d
        m2_ref[...] = m2
        v2_ref[...] = v2

    spec = pl.BlockSpec((br, cols), lambda i: (i, 0))
    return pl.pallas_call(
        body, name="adamw_big", grid=(rows // br,),
        out_shape=[jax.ShapeDtypeStruct((rows, cols), F32)] * 3,
        in_specs=[spec] * 4, out_specs=[spec] * 3, compiler_params=_cparams(1),
    )(w, g, m, v)


def _adamw_small(groups):
    n = len(groups)

    def body(*refs):
        ins, outs = refs[:4 * n], refs[4 * n:]
        for i in range(n):
            w, g, m, v = (r[...] for r in ins[4 * i:4 * i + 4])
            d, m2, v2 = _adamw_math(w, g, m, v)
            outs[3 * i][...] = d
            outs[3 * i + 1][...] = m2
            outs[3 * i + 2][...] = v2

    vm = pl.BlockSpec(memory_space=pltpu.VMEM)
    flat = [a for grp in groups for a in grp]
    out_shape = [jax.ShapeDtypeStruct(grp[0].shape, F32) for grp in groups for _ in range(3)]
    outs = pl.pallas_call(body, name="adamw_small", out_shape=out_shape,
                          in_specs=[vm] * (4 * n), out_specs=[vm] * (3 * n))(*flat)
    return [tuple(outs[3 * i:3 * i + 3]) for i in range(n)]


def _load_weights(pairs, sem):
    for src, dst in pairs:
        cp = pltpu.make_async_copy(src, dst, sem)
        cp.start()
        cp.wait()


def _meta_fwd(meta_full, g1, win_all):
    def body(meta_ref, g1_ref, win_ref, z_ref):
        xm = meta_ref[...]
        a = (xm * _rstd(xm) * g1_ref[...]).astype(BF16)
        for j in range(N_CHIPS):
            z_ref[:, j * IN_SHARD:(j + 1) * IN_SHARD] = _dot(a, win_ref[j])

    vm = pl.BlockSpec(memory_space=pltpu.VMEM)
    return pl.pallas_call(body, name="meta_fwd", out_shape=jax.ShapeDtypeStruct((N_META, D_IN_PROJ), F32),
                          in_specs=[vm] * 3, out_specs=vm)(meta_full, g1, win_all)


def _mixer_fwd(x3, zmeta, g1, g2, convw, poolw, pscale, win_all, wout):
    n_seq, seq, _ = x3.shape
    tm = min(TM_MIX_FWD, seq)
    n_t = seq // tm

    def body(x_ref, zm_ref, g1_ref, g2_ref, cw_ref, pw_ref, ps_ref, win_hbm, wout_hbm,
             z_ref, m_ref, h1_ref, win_v, wout_v, cvb, pb, sem):
        s, t = pl.program_id(0), pl.program_id(1)

        @pl.when((s == 0) & (t == 0))
        def _():
            _load_weights([(win_hbm, win_v), (wout_hbm, wout_v)], sem)

        xt = x_ref[0]
        a = (xt * _rstd(xt) * g1_ref[...]).astype(BF16)
        zb = _dot(a, win_v[0])
        zc = _dot(a, win_v[1])
        zv = _dot(a, win_v[2])
        zp = _dot(a, win_v[3])
        z_ref[0, :, 0:IN_SHARD] = zb
        z_ref[0, :, IN_SHARD:2 * IN_SHARD] = zc
        z_ref[0, :, 2 * IN_SHARD:3 * IN_SHARD] = zv
        z_ref[0, :, 3 * IN_SHARD:4 * IN_SHARD] = zp

        @pl.when(t == 0)
        def _():
            cvb[0:HALO, :] = zm_ref[:, IN_SHARD:2 * IN_SHARD] * zm_ref[:, 2 * IN_SHARD:3 * IN_SHARD]
            pb[0:HALO, :] = zm_ref[:, 3 * IN_SHARD:4 * IN_SHARD]

        @pl.when(t > 0)
        def _():
            cvb[0:HALO, :] = cvb[tm:tm + HALO, :]
            pb[0:HALO, :] = pb[tm:tm + HALO, :]

        cv = zc * zv
        cvb[HALO:HALO + tm, :] = cv
        pb[HALO:HALO + tm, :] = zp
        cw = cw_ref[...]
        conv = cw[0:1] * cvb[HALO - 2:HALO - 2 + tm, :] + cw[1:2] * cvb[HALO - 1:HALO - 1 + tm, :] + cw[2:3] * cv
        parts = [(zb * conv).astype(BF16)]
        for g in range(N_POOL_GROUPS):
            pooled = _pool_fwd(pb, g, tm).astype(BF16)
            parts.append((_dot(pooled, pw_ref[g]) * ps_ref[:, _gcols(g)]).astype(BF16))
        m = _dot(jnp.concatenate(parts, axis=1), wout_v[...])
        m_ref[0] = m
        h1_ref[0] = xt + m * _rstd(m) * g2_ref[...]

    row = lambda c: pl.BlockSpec((1, tm, c), lambda s, t: (s, t, 0))
    return pl.pallas_call(
        body, name="mixer_fwd", grid=(n_seq, n_t),
        out_shape=[jax.ShapeDtypeStruct((n_seq, seq, D_IN_PROJ), F32), jax.ShapeDtypeStruct((n_seq, seq, D_MODEL), F32),
                   jax.ShapeDtypeStruct((n_seq, seq, D_MODEL), F32)],
        in_specs=[row(D_MODEL), _full((N_META, D_IN_PROJ)), _full((1, D_MODEL)), _full((1, D_MODEL)),
                  _full((3, D_CONV)), _full((N_POOL_GROUPS, POOL_GROUP, POOL_GROUP)), _full((1, D_POOL)), ANY, ANY],
        out_specs=[row(D_IN_PROJ), row(D_MODEL), row(D_MODEL)],
        scratch_shapes=[pltpu.VMEM((N_CHIPS, D_MODEL, IN_SHARD), BF16), pltpu.VMEM((D_MODEL, D_MODEL), BF16),
                        pltpu.VMEM((HALO + tm, D_CONV), F32), pltpu.VMEM((HALO + tm, D_POOL), F32),
                        pltpu.SemaphoreType.DMA],
        compiler_params=_cparams(2),
    )(x3, zmeta, g1, g2, convw, poolw, pscale, win_all, wout)


def _ffn_fwd_bwd(h1, target, g3, g4, wg_all, wu_all, wd_all):
    n_rows = h1.shape[0]
    tm = min(TM_FFN, n_rows)

    def body(h1_ref, t_ref, g3_ref, g4_ref, wg_hbm, wu_hbm, wd_hbm,
             dh1_ref, f_ref, dd_ref, ds_ref, du_ref, gg_ref, loss_ref, dg3_ref, dg4_ref,
             wg_v, wu_v, wd_v, s_sc, u_sc, sem):
        @pl.when(pl.program_id(0) == 0)
        def _():
            _load_weights([(wg_hbm, wg_v), (wu_hbm, wu_v), (wd_hbm, wd_v)], sem)
            loss_ref[...] = jnp.zeros_like(loss_ref)
            dg3_ref[...] = jnp.zeros_like(dg3_ref)
            dg4_ref[...] = jnp.zeros_like(dg4_ref)

        h1v = h1_ref[...]
        r3 = _rstd(h1v)
        hh = h1v * r3
        g3v, g4v = g3_ref[...], g4_ref[...]
        f = (hh * g3v).astype(BF16)
        f_ref[...] = f
        d = jnp.zeros((tm, D_MODEL), F32)
        for j in range(N_CHIPS):
            s = _dot(f, wg_v[j])
            u = _dot(f, wu_v[j])
            s_sc[j] = s
            u_sc[j] = u
            gj = (s * _sigmoid(s) * u).astype(BF16)
            gg_ref[j] = gj
            d = d + _dot(gj, wd_v[j])
        r4 = _rstd(d)
        dh = d * r4
        err = (h1v + dh * g4v) - t_ref[...]
        loss_ref[...] += _rows8(err * err)
        dy = err * (1.0 / D_MODEL)
        dg4_ref[...] += _rows8(dy * dh)
        ddb = _rms_bwd(dy, dh, r4, g4v).astype(BF16)
        dd_ref[...] = ddb
        df = jnp.zeros((tm, D_MODEL), F32)
        for j in range(N_CHIPS):
            dgg = _dot_nt(ddb, wd_v[j])
            s = s_sc[j]
            u = u_sc[j]
            sig = _sigmoid(s)
            dsj = (dgg * u * (sig * (1.0 + s * (1.0 - sig)))).astype(BF16)
            duj = (dgg * (s * sig)).astype(BF16)
            ds_ref[j] = dsj
            du_ref[j] = duj
            df = df + _dot_nt(dsj, wg_v[j]) + _dot_nt(duj, wu_v[j])
        dg3_ref[...] += _rows8(df * hh)
        dh1_ref[...] = dy + _rms_bwd(df, hh, r3, g3v)

    row = pl.BlockSpec((tm, D_MODEL), lambda i: (i, 0))
    ffrow = pl.BlockSpec((N_CHIPS, tm, FF_SHARD), lambda i: (0, i, 0))
    acc = _full((8, D_MODEL))
    act_bf = jax.ShapeDtypeStruct((n_rows, D_MODEL), BF16)
    ff_bf = jax.ShapeDtypeStruct((N_CHIPS, n_rows, FF_SHARD), BF16)
    acc_shape = jax.ShapeDtypeStruct((8, D_MODEL), F32)
    return pl.pallas_call(
        body, name="ffn_fwd_bwd", grid=(n_rows // tm,),
        out_shape=[jax.ShapeDtypeStruct((n_rows, D_MODEL), F32), act_bf, act_bf, ff_bf, ff_bf, ff_bf,
                   acc_shape, acc_shape, acc_shape],
        in_specs=[row, row, _full((1, D_MODEL)), _full((1, D_MODEL)), ANY, ANY, ANY],
        out_specs=[row, row, row, ffrow, ffrow, ffrow, acc, acc, acc],
        scratch_shapes=[pltpu.VMEM((N_CHIPS, D_MODEL, FF_SHARD), BF16), pltpu.VMEM((N_CHIPS, D_MODEL, FF_SHARD), BF16),
                        pltpu.VMEM((N_CHIPS, FF_SHARD, D_MODEL), BF16),
                        pltpu.VMEM((N_CHIPS, tm, FF_SHARD), F32), pltpu.VMEM((N_CHIPS, tm, FF_SHARD), F32),
                        pltpu.SemaphoreType.DMA],
        compiler_params=_cparams(1),
    )(h1, target, g3, g4, wg_all, wu_all, wd_all)


def _ffn_weight_grads(f, ds, du, gg, dd):
    n_rows = f.shape[0]
    tk = min(TK_DW, n_rows)

    def body(f_ref, ds_ref, du_ref, gg_ref, dd_ref, dwg_ref, dwu_ref, dwd_ref):
        @pl.when(pl.program_id(1) == 0)
        def _():
            dwg_ref[...] = jnp.zeros_like(dwg_ref)
            dwu_ref[...] = jnp.zeros_like(dwu_ref)
            dwd_ref[...] = jnp.zeros_like(dwd_ref)

        fv = f_ref[...]
        dwg_ref[0] += _dot_tn(fv, ds_ref[0])
        dwu_ref[0] += _dot_tn(fv, du_ref[0])
        dwd_ref[0] += _dot_tn(gg_ref[0], dd_ref[...])

    row = pl.BlockSpec((tk, D_MODEL), lambda j, k: (k, 0))
    ffrow = pl.BlockSpec((1, tk, FF_SHARD), lambda j, k: (j, k, 0))
    return pl.pallas_call(
        body, name="ffn_weight_grads", grid=(N_CHIPS, n_rows // tk),
        out_shape=[jax.ShapeDtypeStruct((N_CHIPS, D_MODEL, FF_SHARD), F32)] * 2
        + [jax.ShapeDtypeStruct((N_CHIPS, FF_SHARD, D_MODEL), F32)],
        in_specs=[row, ffrow, ffrow, ffrow, row],
        out_specs=[pl.BlockSpec((1, D_MODEL, FF_SHARD), lambda j, k: (j, 0, 0))] * 2
        + [pl.BlockSpec((1, FF_SHARD, D_MODEL), lambda j, k: (j, 0, 0))],
        compiler_params=_cparams(2),
    )(f, ds, du, gg, dd)


def _mixer_bwd(dh1, m3, z3, x3, zmeta, g1, g2, convw, poolw, pscale, win_all, wout):
    n_seq, seq, _ = x3.shape
    tm = min(TM_MIX_BWD, seq)
    n_t = seq // tm
    hb = tm // HALO

    def body(dh1_ref, m_ref, z_ref, zh_ref, x_ref, zm_ref, g1_ref, g2_ref, cw_ref, pw_ref, ps_ref, win_hbm, wout_hbm,
             dx_ref, dz_ref, a_ref, yc_ref, dm_ref, dg1_ref, dg2_ref, dsc_ref, dcw_ref, dpw_ref, dzm_ref,
             win_v, wout_v, cvb, pb, dcb, dqb, mcb, mqb, sem):
        s, i = pl.program_id(0), pl.program_id(1)
        tr = n_t - 1 - i

        @pl.when((s == 0) & (i == 0))
        def _():
            _load_weights([(win_hbm, win_v), (wout_hbm, wout_v)], sem)
            for ref in (dg1_ref, dg2_ref, dsc_ref, dcw_ref, dpw_ref, dzm_ref):
                ref[...] = jnp.zeros_like(ref)

        @pl.when(i == 0)
        def _():
            dcb[tm:tm + HALO, :] = jnp.zeros((HALO, D_CONV), F32)
            dqb[tm:tm + HALO, :] = jnp.zeros((HALO, D_POOL), F32)

        @pl.when(i > 0)
        def _():
            dcb[tm:tm + HALO, :] = dcb[0:HALO, :]
            dqb[tm:tm + HALO, :] = dqb[0:HALO, :]

        g1v, g2v = g1_ref[...], g2_ref[...]
        dh1v = dh1_ref[0]
        mv = m_ref[0]
        r2 = _rstd(mv)
        mh = mv * r2
        dg2_ref[...] += _rows8(dh1v * mh)
        dmb = _rms_bwd(dh1v, mh, r2, g2v).astype(BF16)
        dm_ref[...] = dmb
        dyc = _dot_nt(dmb, wout_v[...])
        dyconv = dyc[:, 0:D_CONV]

        zb = z_ref[0, :, 0:IN_SHARD]
        zc = z_ref[0, :, IN_SHARD:2 * IN_SHARD]
        zv = z_ref[0, :, 2 * IN_SHARD:3 * IN_SHARD]
        zp = z_ref[0, :, 3 * IN_SHARD:4 * IN_SHARD]
        halo = jnp.where(tr == 0, zm_ref[...], zh_ref[0])
        cvb[0:HALO, :] = halo[:, IN_SHARD:2 * IN_SHARD] * halo[:, 2 * IN_SHARD:3 * IN_SHARD]
        pb[0:HALO, :] = halo[:, 3 * IN_SHARD:4 * IN_SHARD]
        cv0 = zc * zv
        cvb[HALO:HALO + tm, :] = cv0
        pb[HALO:HALO + tm, :] = zp
        cw = cw_ref[...]
        cv2 = cvb[HALO - 2:HALO - 2 + tm, :]
        cv1 = cvb[HALO - 1:HALO - 1 + tm, :]
        conv = cw[0:1] * cv2 + cw[1:2] * cv1 + cw[2:3] * cv0
        parts = [(zb * conv).astype(BF16)]
        for g in range(N_POOL_GROUPS):
            pooled = _pool_fwd(pb, g, tm).astype(BF16)
            mixed = _dot(pooled, pw_ref[g])
            sc = ps_ref[:, _gcols(g)]
            parts.append((mixed * sc).astype(BF16))
            dyp = dyc[:, D_CONV + g * POOL_GROUP:D_CONV + (g + 1) * POOL_GROUP]
            dsc_ref[:, _gcols(g)] += _rows8(dyp * mixed)
            dmix = (dyp * sc).astype(BF16)
            dpw_ref[g] += _dot_tn(pooled, dmix)
            dqb[0:tm, _gcols(g)] = _dot_nt(dmix, pw_ref[g])
        yc_ref[...] = jnp.concatenate(parts, axis=1)

        dconv = dyconv * zb
        dcb[0:tm, :] = dconv
        dcv = cw[2:3] * dconv + cw[1:2] * dcb[1:1 + tm, :] + cw[0:1] * dcb[2:2 + tm, :]
        dcw_ref[0:8, :] += _rows8(dconv * cv2)
        dcw_ref[8:16, :] += _rows8(dconv * cv1)
        dcw_ref[16:24, :] += _rows8(dconv * cv0)
        dzs = [(dyconv * conv).astype(BF16), (dcv * zv).astype(BF16), (dcv * zc).astype(BF16),
               jnp.concatenate([_pool_bwd(dqb, g, tm) for g in range(N_POOL_GROUPS)], axis=1).astype(BF16)]
        da = jnp.zeros((tm, D_MODEL), F32)
        for j in range(N_CHIPS):
            dz_ref[j] = dzs[j]
            da = da + _dot_nt(dzs[j], win_v[j])
        xt = x_ref[0]
        r1 = _rstd(xt)
        xh = xt * r1
        a_ref[...] = (xh * g1v).astype(BF16)
        dg1_ref[...] += _rows8(da * xh)
        dx_ref[0] = dh1v + _rms_bwd(da, xh, r1, g1v)

        @pl.when(tr == 0)
        def _():
            mcb[0:HALO, :] = jnp.zeros((HALO, D_CONV), F32)
            mqb[0:HALO, :] = jnp.zeros((HALO, D_POOL), F32)
            mcb[HALO:2 * HALO, :] = dcb[0:HALO, :]
            mqb[HALO:2 * HALO, :] = dqb[0:HALO, :]
            dcv_m = cw[1:2] * mcb[1:1 + HALO, :] + cw[0:1] * mcb[2:2 + HALO, :]
            dzm_ref[:, IN_SHARD:2 * IN_SHARD] += dcv_m * zm_ref[:, 2 * IN_SHARD:3 * IN_SHARD]
            dzm_ref[:, 2 * IN_SHARD:3 * IN_SHARD] += dcv_m * zm_ref[:, IN_SHARD:2 * IN_SHARD]
            dzm_ref[:, 3 * IN_SHARD:4 * IN_SHARD] += jnp.concatenate(
                [_pool_bwd(mqb, g, HALO) for g in range(N_POOL_GROUPS)], axis=1)

    row3 = lambda c: pl.BlockSpec((1, tm, c), lambda s, i: (s, n_t - 1 - i, 0))
    row2 = lambda c: pl.BlockSpec((tm, c), lambda s, i: (s * n_t + n_t - 1 - i, 0))
    halo_spec = pl.BlockSpec((1, HALO, D_IN_PROJ), lambda s, i: (s, jnp.maximum((n_t - 1 - i) * hb - 1, 0), 0))
    n_rows = n_seq * seq
    act_bf = jax.ShapeDtypeStruct((n_rows, D_MODEL), BF16)
    return pl.pallas_call(
        body, name="mixer_bwd", grid=(n_seq, n_t),
        out_shape=[jax.ShapeDtypeStruct((n_seq, seq, D_MODEL), F32),
                   jax.ShapeDtypeStruct((N_CHIPS, n_rows, IN_SHARD), BF16), act_bf, act_bf, act_bf,
                   jax.ShapeDtypeStruct((8, D_MODEL), F32), jax.ShapeDtypeStruct((8, D_MODEL), F32),
                   jax.ShapeDtypeStruct((8, D_POOL), F32), jax.ShapeDtypeStruct((24, D_CONV), F32),
                   jax.ShapeDtypeStruct((N_POOL_GROUPS, POOL_GROUP, POOL_GROUP), F32),
                   jax.ShapeDtypeStruct((N_META, D_IN_PROJ), F32)],
        in_specs=[row3(D_MODEL), row3(D_MODEL), row3(D_IN_PROJ), halo_spec, row3(D_MODEL),
                  _full((N_META, D_IN_PROJ)), _full((1, D_MODEL)), _full((1, D_MODEL)), _full((3, D_CONV)),
                  _full((N_POOL_GROUPS, POOL_GROUP, POOL_GROUP)), _full((1, D_POOL)), ANY, ANY],
        out_specs=[row3(D_MODEL), pl.BlockSpec((N_CHIPS, tm, IN_SHARD), lambda s, i: (0, s * n_t + n_t - 1 - i, 0)),
                   row2(D_MODEL), row2(D_MODEL), row2(D_MODEL),
                   _full((8, D_MODEL)), _full((8, D_MODEL)), _full((8, D_POOL)), _full((24, D_CONV)),
                   _full((N_POOL_GROUPS, POOL_GROUP, POOL_GROUP)), _full((N_META, D_IN_PROJ))],
        scratch_shapes=[pltpu.VMEM((N_CHIPS, D_MODEL, IN_SHARD), BF16), pltpu.VMEM((D_MODEL, D_MODEL), BF16),
                        pltpu.VMEM((HALO + tm, D_CONV), F32), pltpu.VMEM((HALO + tm, D_POOL), F32),
                        pltpu.VMEM((tm + HALO, D_CONV), F32), pltpu.VMEM((tm + HALO, D_POOL), F32),
                        pltpu.VMEM((2 * HALO, D_CONV), F32), pltpu.VMEM((2 * HALO, D_POOL), F32),
                        pltpu.SemaphoreType.DMA],
        compiler_params=_cparams(2),
    )(dh1, m3, z3, z3, x3, zmeta, g1, g2, convw, poolw, pscale, win_all, wout)


def _meta_bwd(dzm, meta_full, g1, win_all):
    def body(dzm_ref, meta_ref, g1_ref, win_ref, dmeta_ref, dg1_ref, a_ref, dzb_ref):
        xm = meta_ref[...]
        r = _rstd(xm)
        xh = xm * r
        g1v = g1_ref[...]
        a_ref[...] = (xh * g1v).astype(BF16)
        da = jnp.zeros((N_META, D_MODEL), F32)
        for j in range(N_CHIPS):
            dzj = dzm_ref[:, j * IN_SHARD:(j + 1) * IN_SHARD].astype(BF16)
            dzb_ref[j] = dzj
            da = da + _dot_nt(dzj, win_ref[j])
        dg1_ref[...] = _rows8(da * xh)
        dmeta_ref[...] = _rms_bwd(da, xh, r, g1v)

    vm = pl.BlockSpec(memory_space=pltpu.VMEM)
    return pl.pallas_call(
        body, name="meta_bwd",
        out_shape=[jax.ShapeDtypeStruct((N_META, D_MODEL), F32), jax.ShapeDtypeStruct((8, D_MODEL), F32),
                   jax.ShapeDtypeStruct((N_META, D_MODEL), BF16), jax.ShapeDtypeStruct((N_CHIPS, N_META, IN_SHARD), BF16)],
        in_specs=[vm] * 4, out_specs=[vm] * 4,
    )(dzm, meta_full, g1, win_all)


def _mixer_weight_grads(a, dz, ycat, dm, a_meta, dz_meta):
    n_rows = a.shape[0]
    tk = min(TK_DW, n_rows)

    def body(a_ref, dz_ref, yc_ref, dm_ref, am_ref, dzm_ref, dwin_ref, dwout_ref):
        @pl.when(pl.program_id(1) == 0)
        def _():
            dwin_ref[0] = _dot_tn(am_ref[...], dzm_ref[0])
            dwout_ref[...] = jnp.zeros_like(dwout_ref)

        dwin_ref[0] += _dot_tn(a_ref[...], dz_ref[0])
        dwout_ref[0] += _dot_tn(yc_ref[...], dm_ref[...])

    row = pl.BlockSpec((tk, D_MODEL), lambda j, k: (k, 0))
    return pl.pallas_call(
        body, name="mixer_weight_grads", grid=(N_CHIPS, n_rows // tk),
        out_shape=[jax.ShapeDtypeStruct((N_CHIPS, D_MODEL, IN_SHARD), F32),
                   jax.ShapeDtypeStruct((N_CHIPS, OUT_SHARD, D_MODEL), F32)],
        in_specs=[row, pl.BlockSpec((1, tk, IN_SHARD), lambda j, k: (j, k, 0)),
                  pl.BlockSpec((tk, OUT_SHARD), lambda j, k: (k, j)), row,
                  _full((N_META, D_MODEL)), pl.BlockSpec((1, N_META, IN_SHARD), lambda j, k: (j, 0, 0))],
        out_specs=[pl.BlockSpec((1, D_MODEL, IN_SHARD), lambda j, k: (j, 0, 0)),
                   pl.BlockSpec((1, OUT_SHARD, D_MODEL), lambda j, k: (j, 0, 0))],
        compiler_params=_cparams(2),
    )(a, dz, ycat, dm, a_meta, dz_meta)


def kernel(x, meta_tokens, norm_mix_pre, w_in, conv_w, pool_w, pool_scale, w_out, norm_mix_post, norm_ffn_pre, w_gate, w_up, w_down, norm_ffn_post, loss_target, m_meta_tokens, m_norm_mix_pre, m_w_in, m_conv_w, m_pool_w, m_pool_scale, m_w_out, m_norm_mix_post, m_norm_ffn_pre, m_w_gate, m_w_up, m_w_down, m_norm_ffn_post, v_meta_tokens, v_norm_mix_pre, v_w_in, v_conv_w, v_pool_w, v_pool_scale, v_w_out, v_norm_mix_post, v_norm_ffn_pre, v_w_gate, v_w_up, v_w_down, v_norm_ffn_post):
    n_seq, seq, _ = x.shape
    n_rows = n_seq * seq
    chip = 2 * lax.axis_index("x") + lax.axis_index("y")
    meta_cols = D_MODEL // N_CHIPS
    conv_cols = D_CONV // N_CHIPS

    small = jnp.zeros((2 * HALO, meta_cols), F32)
    small = small.at[0:N_META, :].set(meta_tokens).at[N_META:N_META + 3, 0:conv_cols].set(conv_w[0])
    win_all, wout_all, wg_all, wu_all, wd_all, small_all = _all_gather_shards(
        [w_in[0].astype(BF16), w_out[0].astype(BF16), w_gate[0].astype(BF16), w_up[0].astype(BF16),
         w_down[0].astype(BF16), small])
    meta_full = small_all[:, 0:N_META, :].transpose(1, 0, 2).reshape(N_META, D_MODEL)
    conv_full = small_all[:, N_META:N_META + 3, 0:conv_cols].transpose(1, 0, 2).reshape(3, D_CONV)
    wout_full = wout_all.reshape(D_MODEL, D_MODEL)
    poolw_bf = pool_w[0].astype(BF16)
    pscale = pool_scale
    g1, g2, g3, g4 = norm_mix_pre, norm_mix_post, norm_ffn_pre, norm_ffn_post

    zmeta = _meta_fwd(meta_full, g1, win_all)
    z3, m3, h1 = _mixer_fwd(x, zmeta, g1, g2, conv_full, poolw_bf, pscale, win_all, wout_full)
    dh1, f_bf, dd_bf, ds_bf, du_bf, gg_bf, lossp, dg3p, dg4p = _ffn_fwd_bwd(
        h1.reshape(n_rows, D_MODEL), loss_target.reshape(n_rows, D_MODEL), g3, g4, wg_all, wu_all, wd_all)
    dwg, dwu, dwd = _ffn_weight_grads(f_bf, ds_bf, du_bf, gg_bf, dd_bf)
    (grad_x, dz_bf, a_bf, yc_bf, dm_bf, dg1p, dg2p, dscp, dcwp, dpw, dzm) = _mixer_bwd(
        dh1.reshape(n_seq, seq, D_MODEL), m3, z3, x, zmeta, g1, g2, conv_full, poolw_bf, pscale, win_all, wout_full)
    dmeta, dg1m, a_meta, dz_meta = _meta_bwd(dzm, meta_full, g1, win_all)
    dwin, dwout = _mixer_weight_grads(a_bf, dz_bf, yc_bf, dm_bf, a_meta, dz_meta)

    grads = [dwin, dwout, dwg, dwu, dwd]
    place = jnp.stack([chip, lax.axis_index("c")]).astype(jnp.int32)
    recvs = _exchange_halves(grads)
    rbufs = _scatter_to_chips([_add_pairs(g, r, place) for g, r in zip(grads, recvs)])
    reduced = _gather_halves([_add_chips(g, r, rb, place) for g, r, rb in zip(grads, recvs, rbufs)])
    g_win, g_wout, g_wg, g_wu, g_wd = [r.reshape(2 * r.shape[1], r.shape[2]) for r in reduced]

    a_red, b_red, c_red = _all_reduce_small(dg1p, dg1m, dg2p, dg3p, dg4p, lossp, dmeta, dscp, dcwp,
                                            dpw.reshape(SMALL_C_ROWS, POOL_GROUP))
    loss = a_red[4, 0]
    g_g1, g_g2, g_g3, g_g4 = a_red[0:1], a_red[1:2], a_red[2:3], a_red[3:4]
    g_meta = lax.dynamic_slice(a_red, (8, chip * meta_cols), (N_META, meta_cols))
    g_pscale = b_red[0:1]
    g_conv = lax.dynamic_slice(b_red, (1, chip * conv_cols), (3, conv_cols))
    g_poolw = c_red

    big = [(w_in, g_win, m_w_in, v_w_in), (w_out, g_wout, m_w_out, v_w_out), (w_gate, g_wg, m_w_gate, v_w_gate),
           (w_up, g_wu, m_w_up, v_w_up), (w_down, g_wd, m_w_down, v_w_down)]
    big_out = [_adamw_big(w[0], g, m[0], v[0]) for (w, g, m, v) in big]
    small_groups = [
        (meta_tokens, g_meta, m_meta_tokens, v_meta_tokens),
        (g1, g_g1, m_norm_mix_pre, v_norm_mix_pre),
        (conv_w[0], g_conv, m_conv_w[0], v_conv_w[0]),
        (pool_w.reshape(SMALL_C_ROWS, POOL_GROUP), g_poolw, m_pool_w.reshape(SMALL_C_ROWS, POOL_GROUP),
         v_pool_w.reshape(SMALL_C_ROWS, POOL_GROUP)),
        (pool_scale, g_pscale, m_pool_scale, v_pool_scale),
        (g2, g_g2, m_norm_mix_post, v_norm_mix_post),
        (g3, g_g3, m_norm_ffn_pre, v_norm_ffn_pre),
        (g4, g_g4, m_norm_ffn_post, v_norm_ffn_post),
    ]
    small_out = _adamw_small(small_groups)

    grads_out = [g_meta, g_g1, g_win[None], g_conv[None], g_poolw.reshape(pool_w.shape), g_pscale, g_wout[None],
                 g_g2, g_g3, g_wg[None], g_wu[None], g_wd[None], g_g4]
    s_meta, s_g1, s_conv, s_poolw, s_pscale, s_g2, s_g3, s_g4 = small_out
    b_win, b_wout, b_wg, b_wu, b_wd = big_out

    def leaf(k):
        return [s_meta[k], s_g1[k], b_win[k][None], s_conv[k][None], s_poolw[k].reshape(pool_w.shape), s_pscale[k],
                b_wout[k][None], s_g2[k], s_g3[k], b_wg[k][None], b_wu[k][None], b_wd[k][None], s_g4[k]]

    return (loss, grad_x, *grads_out, *leaf(0), *leaf(1), *leaf(2))
```

```python
import jax
import jax.numpy as jnp
from jax import lax
from jax.experimental import pallas as pl
from jax.experimental.pallas import tpu as pltpu

F32 = jnp.float32
BF16 = jnp.bfloat16
MESH = pl.DeviceIdType.MESH

D_MODEL = 1024
D_CONV = 512
D_POOL = 512
POOL_GROUP = 128
N_POOL_GROUPS = 4
D_IN_PROJ = 2048
D_FF = 2816
N_CHIPS = 4
FF_SHARD = D_FF // N_CHIPS
IN_SHARD = D_IN_PROJ // N_CHIPS
OUT_SHARD = D_MODEL // N_CHIPS
N_META = 16
HALO = 16
RMS_EPS = 1e-6

ADAM_LR = 0.001
ADAM_B1 = 0.9
ADAM_B2 = 0.999
ADAM_EPS = 1e-08
ADAM_WD = 0.01
ADAM_STEP = 10

TM_MIX_FWD = 512
TM_MIX_BWD = 256
TM_FFN = 256
TK_DW = 512
VMEM_LIMIT = 56 * 1024 * 1024


def _cparams(n_grid):
    return pltpu.CompilerParams(dimension_semantics=("arbitrary",) * n_grid, vmem_limit_bytes=VMEM_LIMIT)


def _dot(a, b):
    return jnp.dot(a, b, preferred_element_type=F32)


def _dot_nt(a, b):
    return lax.dot_general(a, b, (((1,), (1,)), ((), ())), preferred_element_type=F32)


def _dot_tn(a, b):
    return lax.dot_general(a, b, (((0,), (0,)), ((), ())), preferred_element_type=F32)


def _rows8(v):
    r, c = v.shape
    return v.reshape(r // 8, 8, c).sum(axis=0)


def _rstd(v):
    return lax.rsqrt(jnp.mean(v * v, axis=-1, keepdims=True) + RMS_EPS)


def _rms_bwd(dy, xhat, rstd, gain):
    dyg = dy * gain
    return rstd * (dyg - xhat * jnp.mean(dyg * xhat, axis=-1, keepdims=True))


def _sigmoid(v):
    return 1.0 / (1.0 + jnp.exp(-v))


def _gcols(g):
    return slice(g * POOL_GROUP, (g + 1) * POOL_GROUP)


def _pool_fwd(pb, g, n):
    win = 2 << g
    cur = pb[HALO:HALO + n, _gcols(g)]
    acc = cur
    for k in range(1, win):
        acc = acc + pb[HALO - k:HALO - k + n, _gcols(g)]
    return acc * (1.0 / win) - cur


def _pool_bwd(qb, g, n):
    win = 2 << g
    cur = qb[0:n, _gcols(g)]
    acc = cur
    for k in range(1, win):
        acc = acc + qb[k:k + n, _gcols(g)]
    return acc * (1.0 / win) - cur


def _full(shape):
    nd = len(shape)
    return pl.BlockSpec(shape, lambda *_: (0,) * nd)


ANY = pl.BlockSpec(memory_space=pl.ANY)


def _mesh_pos():
    x, y, c = lax.axis_index("x"), lax.axis_index("y"), lax.axis_index("c")
    chips = [(1 - x, y), (x, 1 - y), (1 - x, 1 - y)]
    return x, y, c, chips


def _half(ref, h):
    hr = ref.shape[0] // 2
    return ref.at[pl.ds(h * hr, hr), :]


class _AllGather:
    def __init__(self, ins, outs, send_sems, recv_sems):
        self.ins, self.outs, self.send_sems, self.recv_sems = ins, outs, send_sems, recv_sems
        self.n = len(ins)

    @staticmethod
    def scratch(n):
        return [pltpu.SemaphoreType.DMA((6 * n,)), pltpu.SemaphoreType.DMA((6 * n,))]

    @staticmethod
    def out_shape(shards):
        return [jax.ShapeDtypeStruct((N_CHIPS,) + s.shape, s.dtype) for s in shards]

    def _copy(self, a, k, src, dst, to):
        return pltpu.make_async_remote_copy(src_ref=src, dst_ref=dst, send_sem=self.send_sems.at[6 * a + k],
                                            recv_sem=self.recv_sems.at[6 * a + k], device_id=to, device_id_type=MESH)

    def _ici(self, a, k):
        x, y, c, chips = _mesh_pos()
        return self._copy(a, k, _half(self.ins[a], c), _half(self.outs[a].at[2 * x + y], c), (*chips[k], c))

    def _d2d(self, a, k, h):
        x, y, c, chips = _mesh_pos()
        slot = _half(self.outs[a].at[2 * chips[k][0] + chips[k][1]], h)
        return self._copy(a, 3 + k, slot, slot, (x, y, 1 - c))

    def start(self):
        for a in range(self.n):
            for k in range(3):
                self._ici(a, k).start()

    def forward(self, a):
        c = lax.axis_index("c")
        for k in range(3):
            self._ici(a, k).wait_recv()
            self._d2d(a, k, c).start()

    def finish(self):
        c = lax.axis_index("c")
        for a in range(self.n):
            for k in range(3):
                self._d2d(a, k, 1 - c).wait_recv()
        for a in range(self.n):
            for k in range(3):
                self._ici(a, k).wait_send()
                self._d2d(a, k, c).wait_send()


def _fill_own_slot(gathered, shards):
    chip = 2 * lax.axis_index("x") + lax.axis_index("y")
    return [lax.dynamic_update_slice(o, s[None], (chip, 0, 0)) for o, s in zip(gathered, shards)]


def _all_gather_shards(shards):
    n = len(shards)

    def body(*refs):
        ag = _AllGather(refs[:n], refs[n:2 * n], *refs[2 * n:])
        ag.start()
        for a in range(n):
            ag.forward(a)
        ag.finish()

    outs = pl.pallas_call(
        body, name="all_gather_weights", out_shape=_AllGather.out_shape(shards),
        in_specs=[ANY] * n, out_specs=[ANY] * n, scratch_shapes=_AllGather.scratch(n),
    )(*shards)
    return _fill_own_slot(outs, shards)


class _ExchangeHalves:
    def __init__(self, ins, recvs, send_sems, recv_sems):
        self.ins, self.recvs, self.send_sems, self.recv_sems = ins, recvs, send_sems, recv_sems

    @staticmethod
    def scratch(n):
        return [pltpu.SemaphoreType.DMA((n,)), pltpu.SemaphoreType.DMA((n,))]

    @staticmethod
    def out_shape(grads):
        return [jax.ShapeDtypeStruct((g.shape[0], g.shape[1] // 2, g.shape[2]), g.dtype) for g in grads]

    def _copies(self):
        x, y, c, _ = _mesh_pos()
        out = []
        for a, (src, dst) in enumerate(zip(self.ins, self.recvs)):
            hr = src.shape[1] // 2
            out.append(pltpu.make_async_remote_copy(
                src_ref=src.at[:, pl.ds((1 - c) * hr, hr), :], dst_ref=dst, send_sem=self.send_sems.at[a],
                recv_sem=self.recv_sems.at[a], device_id=(x, y, 1 - c), device_id_type=MESH))
        return out

    def start(self):
        for cp in self._copies():
            cp.start()

    def finish(self):
        for cp in self._copies():
            cp.wait()


def _exchange_halves(grads):
    n = len(grads)

    def body(*refs):
        ex = _ExchangeHalves(refs[:n], refs[n:2 * n], *refs[2 * n:])
        ex.start()
        ex.finish()

    return pl.pallas_call(
        body, name="grad_exchange_halves", out_shape=_ExchangeHalves.out_shape(grads),
        in_specs=[ANY] * n, out_specs=[ANY] * n, scratch_shapes=_ExchangeHalves.scratch(n),
    )(*grads)


class _ScatterToChips:
    def __init__(self, ins, rbufs, send_sems, recv_sems):
        self.ins, self.rbufs, self.send_sems, self.recv_sems = ins, rbufs, send_sems, recv_sems

    @staticmethod
    def scratch(n):
        return [pltpu.SemaphoreType.DMA((3 * n,)), pltpu.SemaphoreType.DMA((3 * n,))]

    @staticmethod
    def out_shape(sums):
        return [jax.ShapeDtypeStruct((3,) + s.shape[1:], BF16) for s in sums]

    def _copies(self):
        x, y, c, chips = _mesh_pos()
        out = []
        for a, (src, dst) in enumerate(zip(self.ins, self.rbufs)):
            for k, chip in enumerate(chips):
                out.append(pltpu.make_async_remote_copy(
                    src_ref=src.at[2 * chip[0] + chip[1]], dst_ref=dst.at[k], send_sem=self.send_sems.at[3 * a + k],
                    recv_sem=self.recv_sems.at[3 * a + k], device_id=(*chip, c), device_id_type=MESH))
        return out

    def start(self):
        for cp in self._copies():
            cp.start()

    def finish(self):
        for cp in self._copies():
            cp.wait()


def _scatter_to_chips(sums_bf16):
    n = len(sums_bf16)

    def body(*refs):
        sc = _ScatterToChips(refs[:n], refs[n:2 * n], *refs[2 * n:])
        sc.start()
        sc.finish()

    return pl.pallas_call(
        body, name="grad_scatter_to_chips", out_shape=_ScatterToChips.out_shape(sums_bf16),
        in_specs=[ANY] * n, out_specs=[ANY] * n, scratch_shapes=_ScatterToChips.scratch(n),
    )(*sums_bf16)


def _gather_halves(halves):
    n = len(halves)

    def body(*refs):
        ins, outs = refs[:n], refs[n:2 * n]
        send_sems, recv_sems = refs[2 * n:]
        x, y, c, _ = _mesh_pos()
        sib = (x, y, 1 - c)
        remote = [pltpu.make_async_remote_copy(src_ref=ins[a].at[c], dst_ref=outs[a].at[c],
                                               send_sem=send_sems.at[a], recv_sem=recv_sems.at[a],
                                               device_id=sib, device_id_type=MESH) for a in range(n)]
        for cp in remote:
            cp.start()
        for a in range(n):
            pltpu.make_async_remote_copy(src_ref=ins[a].at[1 - c], dst_ref=outs[a].at[1 - c], send_sem=send_sems.at[a],
                                         recv_sem=recv_sems.at[a], device_id=sib, device_id_type=MESH).wait_recv()
        for cp in remote:
            cp.wait_send()

    return pl.pallas_call(
        body, name="grad_gather_halves",
        out_shape=[jax.ShapeDtypeStruct(h.shape, F32) for h in halves],
        in_specs=[ANY] * n, out_specs=[ANY] * n, input_output_aliases={a: a for a in range(n)},
        scratch_shapes=[pltpu.SemaphoreType.DMA((n,)), pltpu.SemaphoreType.DMA((n,))],
    )(*halves)


SMALL_A_ROWS = 24
SMALL_B_ROWS = 8
SMALL_C_ROWS = N_POOL_GROUPS * POOL_GROUP


def _all_reduce_small(dg1, dg1m, dg2, dg3, dg4, lossp, dmeta, dscale, dconv, dpoolw):
    def body(dg1_ref, dg1m_ref, dg2_ref, dg3_ref, dg4_ref, loss_ref, dmeta_ref, dsc_ref, dcw_ref, dpw_ref,
             a_out, b_out, c_out, a_buf, b_buf, c_buf, a_rcv, b_rcv, c_rcv, send_sems, recv_sems):
        x, y, c, _ = _mesh_pos()
        peers = [(x, y, 1 - c), (1 - x, y, c), (x, 1 - y, c)]

        def rowsum(v):
            return jnp.sum(v, axis=0, keepdims=True)

        a_buf[0, 0:1, :] = rowsum(dg1_ref[...] + dg1m_ref[...])
        a_buf[0, 1:2, :] = rowsum(dg2_ref[...])
        a_buf[0, 2:3, :] = rowsum(dg3_ref[...])
        a_buf[0, 3:4, :] = rowsum(dg4_ref[...])
        loss = jnp.sum(rowsum(loss_ref[...]), axis=1, keepdims=True) * (0.5 / D_MODEL)
        a_buf[0, 4:5, :] = jnp.broadcast_to(loss, (1, D_MODEL))
        a_buf[0, 5:8, :] = jnp.zeros((3, D_MODEL), F32)
        a_buf[0, 8:24, :] = dmeta_ref[...]
        b_buf[0, 0:1, :] = rowsum(dsc_ref[...])
        for k in range(3):
            b_buf[0, 1 + k:2 + k, :] = rowsum(dcw_ref[8 * k:8 * k + 8, :])
        b_buf[0, 4:8, :] = jnp.zeros((4, D_CONV), F32)
        c_buf[0] = dpw_ref[...]

        for st, peer in enumerate(peers):
            cps = []
            for i, (buf, rcv) in enumerate(((a_buf, a_rcv), (b_buf, b_rcv), (c_buf, c_rcv))):
                cps.append(pltpu.make_async_remote_copy(
                    src_ref=buf.at[st], dst_ref=rcv.at[st], send_sem=send_sems.at[3 * st + i],
                    recv_sem=recv_sems.at[3 * st + i], device_id=peer, device_id_type=MESH))
            for cp in cps:
                cp.start()
            for cp in cps:
                cp.wait()
            if st < 2:
                a_buf[st + 1] = a_buf[st] + a_rcv[st]
                b_buf[st + 1] = b_buf[st] + b_rcv[st]
                c_buf[st + 1] = c_buf[st] + c_rcv[st]
            else:
                a_out[...] = a_buf[st] + a_rcv[st]
                b_out[...] = b_buf[st] + b_rcv[st]
                c_out[...] = c_buf[st] + c_rcv[st]

    vm = pl.BlockSpec(memory_space=pltpu.VMEM)
    shapes = [(SMALL_A_ROWS, D_MODEL), (SMALL_B_ROWS, D_CONV), (SMALL_C_ROWS, POOL_GROUP)]
    return pl.pallas_call(
        body, name="all_reduce_small",
        out_shape=[jax.ShapeDtypeStruct(s, F32) for s in shapes],
        in_specs=[vm] * 10, out_specs=[vm] * 3,
        scratch_shapes=[pltpu.VMEM((3,) + s, F32) for s in shapes] + [pltpu.VMEM((3,) + s, F32) for s in shapes]
        + [pltpu.SemaphoreType.DMA((9,)), pltpu.SemaphoreType.DMA((9,))],
    )(dg1, dg1m, dg2, dg3, dg4, lossp, dmeta, dscale, dconv, dpoolw)


def _row_block(rows):
    for cand in (512, 448, 384, 352, 320, 256, 128, 64, 32, 16):
        if rows % cand == 0:
            return cand
    return rows


def _add_pairs(grad, recv, place):
    n_sh, rows2, cols = grad.shape
    hr = rows2 // 2
    br = _row_block(hr)

    def body(place_ref, a_ref, b_ref, o_ref):
        o_ref[...] = (a_ref[0] + b_ref[...]).astype(BF16)

    return pl.pallas_call(
        body, name="grad_add_pairs",
        grid_spec=pltpu.PrefetchScalarGridSpec(
            num_scalar_prefetch=1, grid=(n_sh, hr // br),
            in_specs=[pl.BlockSpec((1, 1, br, cols), lambda j, i, p: (j, p[1], i, 0)),
                      pl.BlockSpec((1, br, cols), lambda j, i, p: (j, i, 0))],
            out_specs=pl.BlockSpec((1, br, cols), lambda j, i, p: (j, i, 0))),
        out_shape=jax.ShapeDtypeStruct((n_sh, hr, cols), BF16), compiler_params=_cparams(2),
    )(place, grad.reshape(n_sh, 2, hr, cols), recv)


def _add_chips(grad, recv, rbuf, place):
    n_sh, rows2, cols = grad.shape
    hr = rows2 // 2
    br = _row_block(hr)

    def body(place_ref, a_ref, b_ref, r_ref, o_ref):
        own = a_ref[0, 0] + b_ref[0]
        o_ref[0] = ((own + r_ref[0].astype(F32)) + r_ref[1].astype(F32)) + r_ref[2].astype(F32)

    return pl.pallas_call(
        body, name="grad_add_chips",
        grid_spec=pltpu.PrefetchScalarGridSpec(
            num_scalar_prefetch=1, grid=(hr // br,),
            in_specs=[pl.BlockSpec((1, 1, br, cols), lambda i, p: (p[0], p[1], i, 0)),
                      pl.BlockSpec((1, br, cols), lambda i, p: (p[0], i, 0)),
                      pl.BlockSpec((3, br, cols), lambda i, p: (0, i, 0))],
            out_specs=pl.BlockSpec((1, br, cols), lambda i, p: (p[1], i, 0))),
        out_shape=jax.ShapeDtypeStruct((2, hr, cols), F32), compiler_params=_cparams(1),
    )(place, grad.reshape(n_sh, 2, hr, cols), recv, rbuf)


def _adamw_math(w, g, m, v):
    m2 = ADAM_B1 * m + (1.0 - ADAM_B1) * g
    v2 = ADAM_B2 * v + (1.0 - ADAM_B2) * (g * g)
    m_hat = m2 / (1.0 - ADAM_B1 ** ADAM_STEP)
    v_hat = v2 / (1.0 - ADAM_B2 ** ADAM_STEP)
    delta = -ADAM_LR * (m_hat / (jnp.sqrt(v_hat) + ADAM_EPS) + ADAM_WD * w)
    return delta, m2, v2


def _adamw_big(w, g, m, v):
    rows, cols = w.shape
    br = _row_block(rows)

    def body(w_ref, g_ref, m_ref, v_ref, d_ref, m2_ref, v2_ref):
        d, m2, v2 = _adamw_math(w_ref[...], g_ref[...], m_ref[...], v_ref[...])
        d_ref[...] = d
        m2_ref[...] = m2
        v2_ref[...] = v2

    spec = pl.BlockSpec((br, cols), lambda i: (i, 0))
    return pl.pallas_call(
        body, name="adamw_big", grid=(rows // br,),
        out_shape=[jax.ShapeDtypeStruct((rows, cols), F32)] * 3,
        in_specs=[spec] * 4, out_specs=[spec] * 3, compiler_params=_cparams(1),
    )(w, g, m, v)


def _adamw_small(groups):
    n = len(groups)

    def body(*refs):
        ins, outs = refs[:4 * n], refs[4 * n:]
        for i in range(n):
            w, g, m, v = (r[...] for r in ins[4 * i:4 * i + 4])
            d, m2, v2 = _adamw_math(w, g, m, v)
            outs[3 * i][...] = d
            outs[3 * i + 1][...] = m2
            outs[3 * i + 2][...] = v2

    vm = pl.BlockSpec(memory_space=pltpu.VMEM)
    flat = [a for grp in groups for a in grp]
    out_shape = [jax.ShapeDtypeStruct(grp[0].shape, F32) for grp in groups for _ in range(3)]
    outs = pl.pallas_call(body, name="adamw_small", out_shape=out_shape,
                          in_specs=[vm] * (4 * n), out_specs=[vm] * (3 * n))(*flat)
    return [tuple(outs[3 * i:3 * i + 3]) for i in range(n)]


def _load_weights(pairs, sem):
    for src, dst in pairs:
        cp = pltpu.make_async_copy(src, dst, sem)
        cp.start()
        cp.wait()


def _meta_fwd(meta_full, g1, win_all):
    def body(meta_ref, g1_ref, win_ref, z_ref):
        xm = meta_ref[...]
        a = (xm * _rstd(xm) * g1_ref[...]).astype(BF16)
        for j in range(N_CHIPS):
            z_ref[:, j * IN_SHARD:(j + 1) * IN_SHARD] = _dot(a, win_ref[j])

    vm = pl.BlockSpec(memory_space=pltpu.VMEM)
    return pl.pallas_call(body, name="meta_fwd", out_shape=jax.ShapeDtypeStruct((N_META, D_IN_PROJ), F32),
                          in_specs=[vm] * 3, out_specs=vm)(meta_full, g1, win_all)


def _mixer_fwd(x3, zmeta, g1, g2, convw, poolw, pscale, win_all, wout, ffn_shards):
    n_seq, seq, _ = x3.shape
    tm = min(TM_MIX_FWD, seq)
    n_t = seq // tm
    n_steps = n_seq * n_t
    n_ag = len(ffn_shards)

    def body(x_ref, zm_ref, g1_ref, g2_ref, cw_ref, pw_ref, ps_ref, win_hbm, wout_hbm, *rest):
        ag = _AllGather(rest[:n_ag], rest[n_ag + 3:2 * n_ag + 3], *rest[-2:])
        z_ref, m_ref, h1_ref = rest[n_ag:n_ag + 3]
        win_v, wout_v, cvb, pb, sem = rest[2 * n_ag + 3:-2]
        s, t = pl.program_id(0), pl.program_id(1)
        step = s * n_t + t

        @pl.when(step == 0)
        def _():
            ag.start()
            _load_weights([(win_hbm, win_v), (wout_hbm, wout_v)], sem)

        for a in range(n_ag):
            @pl.when(step == min(((a + 1) * n_steps) // n_ag, n_steps - 1))
            def _():
                ag.forward(a)

        xt = x_ref[0]
        a = (xt * _rstd(xt) * g1_ref[...]).astype(BF16)
        zb = _dot(a, win_v[0])
        zc = _dot(a, win_v[1])
        zv = _dot(a, win_v[2])
        zp = _dot(a, win_v[3])
        z_ref[0, :, 0:IN_SHARD] = zb
        z_ref[0, :, IN_SHARD:2 * IN_SHARD] = zc
        z_ref[0, :, 2 * IN_SHARD:3 * IN_SHARD] = zv
        z_ref[0, :, 3 * IN_SHARD:4 * IN_SHARD] = zp

        @pl.when(t == 0)
        def _():
            cvb[0:HALO, :] = zm_ref[:, IN_SHARD:2 * IN_SHARD] * zm_ref[:, 2 * IN_SHARD:3 * IN_SHARD]
            pb[0:HALO, :] = zm_ref[:, 3 * IN_SHARD:4 * IN_SHARD]

        @pl.when(t > 0)
        def _():
            cvb[0:HALO, :] = cvb[tm:tm + HALO, :]
            pb[0:HALO, :] = pb[tm:tm + HALO, :]

        cv = zc * zv
        cvb[HALO:HALO + tm, :] = cv
        pb[HALO:HALO + tm, :] = zp
        cw = cw_ref[...]
        conv = cw[0:1] * cvb[HALO - 2:HALO - 2 + tm, :] + cw[1:2] * cvb[HALO - 1:HALO - 1 + tm, :] + cw[2:3] * cv
        parts = [(zb * conv).astype(BF16)]
        for g in range(N_POOL_GROUPS):
            pooled = _pool_fwd(pb, g, tm).astype(BF16)
            parts.append((_dot(pooled, pw_ref[g]) * ps_ref[:, _gcols(g)]).astype(BF16))
        m = _dot(jnp.concatenate(parts, axis=1), wout_v[...])
        m_ref[0] = m
        h1_ref[0] = xt + m * _rstd(m) * g2_ref[...]

        @pl.when(step == n_steps - 1)
        def _():
            ag.finish()

    row = lambda c: pl.BlockSpec((1, tm, c), lambda s, t: (s, t, 0))
    outs = pl.pallas_call(
        body, name="mixer_fwd", grid=(n_seq, n_t),
        out_shape=[jax.ShapeDtypeStruct((n_seq, seq, D_IN_PROJ), F32), jax.ShapeDtypeStruct((n_seq, seq, D_MODEL), F32),
                   jax.ShapeDtypeStruct((n_seq, seq, D_MODEL), F32)] + _AllGather.out_shape(ffn_shards),
        in_specs=[row(D_MODEL), _full((N_META, D_IN_PROJ)), _full((1, D_MODEL)), _full((1, D_MODEL)),
                  _full((3, D_CONV)), _full((N_POOL_GROUPS, POOL_GROUP, POOL_GROUP)), _full((1, D_POOL)), ANY, ANY]
        + [ANY] * n_ag,
        out_specs=[row(D_IN_PROJ), row(D_MODEL), row(D_MODEL)] + [ANY] * n_ag,
        scratch_shapes=[pltpu.VMEM((N_CHIPS, D_MODEL, IN_SHARD), BF16), pltpu.VMEM((D_MODEL, D_MODEL), BF16),
                        pltpu.VMEM((HALO + tm, D_CONV), F32), pltpu.VMEM((HALO + tm, D_POOL), F32),
                        pltpu.SemaphoreType.DMA] + _AllGather.scratch(n_ag),
        compiler_params=_cparams(2),
    )(x3, zmeta, g1, g2, convw, poolw, pscale, win_all, wout, *ffn_shards)
    return outs[:3], _fill_own_slot(outs[3:], ffn_shards)


def _ffn_fwd_bwd(h1, target, g3, g4, wg_all, wu_all, wd_all):
    n_rows = h1.shape[0]
    tm = min(TM_FFN, n_rows)

    def body(h1_ref, t_ref, g3_ref, g4_ref, wg_hbm, wu_hbm, wd_hbm,
             dh1_ref, f_ref, dd_ref, ds_ref, du_ref, gg_ref, loss_ref, dg3_ref, dg4_ref,
             wg_v, wu_v, wd_v, s_sc, u_sc, sem):
        @pl.when(pl.program_id(0) == 0)
        def _():
            _load_weights([(wg_hbm, wg_v), (wu_hbm, wu_v), (wd_hbm, wd_v)], sem)
            loss_ref[...] = jnp.zeros_like(loss_ref)
            dg3_ref[...] = jnp.zeros_like(dg3_ref)
            dg4_ref[...] = jnp.zeros_like(dg4_ref)

        h1v = h1_ref[...]
        r3 = _rstd(h1v)
        hh = h1v * r3
        g3v, g4v = g3_ref[...], g4_ref[...]
        f = (hh * g3v).astype(BF16)
        f_ref[...] = f
        d = jnp.zeros((tm, D_MODEL), F32)
        for j in range(N_CHIPS):
            s = _dot(f, wg_v[j])
            u = _dot(f, wu_v[j])
            s_sc[j] = s
            u_sc[j] = u
            gj = (s * _sigmoid(s) * u).astype(BF16)
            gg_ref[j] = gj
            d = d + _dot(gj, wd_v[j])
        r4 = _rstd(d)
        dh = d * r4
        err = (h1v + dh * g4v) - t_ref[...]
        loss_ref[...] += _rows8(err * err)
        dy = err * (1.0 / D_MODEL)
        dg4_ref[...] += _rows8(dy * dh)
        ddb = _rms_bwd(dy, dh, r4, g4v).astype(BF16)
        dd_ref[...] = ddb
        df = jnp.zeros((tm, D_MODEL), F32)
        for j in range(N_CHIPS):
            dgg = _dot_nt(ddb, wd_v[j])
            s = s_sc[j]
            u = u_sc[j]
            sig = _sigmoid(s)
            dsj = (dgg * u * (sig * (1.0 + s * (1.0 - sig)))).astype(BF16)
            duj = (dgg * (s * sig)).astype(BF16)
            ds_ref[j] = dsj
            du_ref[j] = duj
            df = df + _dot_nt(dsj, wg_v[j]) + _dot_nt(duj, wu_v[j])
        dg3_ref[...] += _rows8(df * hh)
        dh1_ref[...] = dy + _rms_bwd(df, hh, r3, g3v)

    row = pl.BlockSpec((tm, D_MODEL), lambda i: (i, 0))
    ffrow = pl.BlockSpec((N_CHIPS, tm, FF_SHARD), lambda i: (0, i, 0))
    acc = _full((8, D_MODEL))
    act_bf = jax.ShapeDtypeStruct((n_rows, D_MODEL), BF16)
    ff_bf = jax.ShapeDtypeStruct((N_CHIPS, n_rows, FF_SHARD), BF16)
    acc_shape = jax.ShapeDtypeStruct((8, D_MODEL), F32)
    return pl.pallas_call(
        body, name="ffn_fwd_bwd", grid=(n_rows // tm,),
        out_shape=[jax.ShapeDtypeStruct((n_rows, D_MODEL), F32), act_bf, act_bf, ff_bf, ff_bf, ff_bf,
                   acc_shape, acc_shape, acc_shape],
        in_specs=[row, row, _full((1, D_MODEL)), _full((1, D_MODEL)), ANY, ANY, ANY],
        out_specs=[row, row, row, ffrow, ffrow, ffrow, acc, acc, acc],
        scratch_shapes=[pltpu.VMEM((N_CHIPS, D_MODEL, FF_SHARD), BF16), pltpu.VMEM((N_CHIPS, D_MODEL, FF_SHARD), BF16),
                        pltpu.VMEM((N_CHIPS, FF_SHARD, D_MODEL), BF16),
                        pltpu.VMEM((N_CHIPS, tm, FF_SHARD), F32), pltpu.VMEM((N_CHIPS, tm, FF_SHARD), F32),
                        pltpu.SemaphoreType.DMA],
        compiler_params=_cparams(1),
    )(h1, target, g3, g4, wg_all, wu_all, wd_all)


def _ffn_weight_grads(f, ds, du, gg, dd):
    n_rows = f.shape[0]
    tk = min(TK_DW, n_rows)

    def body(f_ref, ds_ref, du_ref, gg_ref, dd_ref, dwg_ref, dwu_ref, dwd_ref):
        @pl.when(pl.program_id(1) == 0)
        def _():
            dwg_ref[...] = jnp.zeros_like(dwg_ref)
            dwu_ref[...] = jnp.zeros_like(dwu_ref)
            dwd_ref[...] = jnp.zeros_like(dwd_ref)

        fv = f_ref[...]
        dwg_ref[0] += _dot_tn(fv, ds_ref[0])
        dwu_ref[0] += _dot_tn(fv, du_ref[0])
        dwd_ref[0] += _dot_tn(gg_ref[0], dd_ref[...])

    row = pl.BlockSpec((tk, D_MODEL), lambda j, k: (k, 0))
    ffrow = pl.BlockSpec((1, tk, FF_SHARD), lambda j, k: (j, k, 0))
    return pl.pallas_call(
        body, name="ffn_weight_grads", grid=(N_CHIPS, n_rows // tk),
        out_shape=[jax.ShapeDtypeStruct((N_CHIPS, D_MODEL, FF_SHARD), F32)] * 2
        + [jax.ShapeDtypeStruct((N_CHIPS, FF_SHARD, D_MODEL), F32)],
        in_specs=[row, ffrow, ffrow, ffrow, row],
        out_specs=[pl.BlockSpec((1, D_MODEL, FF_SHARD), lambda j, k: (j, 0, 0))] * 2
        + [pl.BlockSpec((1, FF_SHARD, D_MODEL), lambda j, k: (j, 0, 0))],
        compiler_params=_cparams(2),
    )(f, ds, du, gg, dd)


def _mixer_bwd(dh1, m3, z3, x3, zmeta, g1, g2, convw, poolw, pscale, win_all, wout, ffn_grads):
    n_seq, seq, _ = x3.shape
    tm = min(TM_MIX_BWD, seq)
    n_t = seq // tm
    hb = tm // HALO
    n_ex = len(ffn_grads)

    def body(dh1_ref, m_ref, z_ref, zh_ref, x_ref, zm_ref, g1_ref, g2_ref, cw_ref, pw_ref, ps_ref, win_hbm, wout_hbm,
             *rest):
        ex = _ExchangeHalves(rest[:n_ex], rest[n_ex + 11:2 * n_ex + 11], *rest[-2:])
        (dx_ref, dz_ref, a_ref, yc_ref, dm_ref, dg1_ref, dg2_ref, dsc_ref, dcw_ref, dpw_ref,
         dzm_ref) = rest[n_ex:n_ex + 11]
        win_v, wout_v, cvb, pb, dcb, dqb, mcb, mqb, sem = rest[2 * n_ex + 11:-2]
        s, i = pl.program_id(0), pl.program_id(1)
        tr = n_t - 1 - i

        @pl.when((s == 0) & (i == 0))
        def _():
            ex.start()
            _load_weights([(win_hbm, win_v), (wout_hbm, wout_v)], sem)
            for ref in (dg1_ref, dg2_ref, dsc_ref, dcw_ref, dpw_ref, dzm_ref):
                ref[...] = jnp.zeros_like(ref)

        @pl.when(i == 0)
        def _():
            dcb[tm:tm + HALO, :] = jnp.zeros((HALO, D_CONV), F32)
            dqb[tm:tm + HALO, :] = jnp.zeros((HALO, D_POOL), F32)

        @pl.when(i > 0)
        def _():
            dcb[tm:tm + HALO, :] = dcb[0:HALO, :]
            dqb[tm:tm + HALO, :] = dqb[0:HALO, :]

        g1v, g2v = g1_ref[...], g2_ref[...]
        dh1v = dh1_ref[0]
        mv = m_ref[0]
        r2 = _rstd(mv)
        mh = mv * r2
        dg2_ref[...] += _rows8(dh1v * mh)
        dmb = _rms_bwd(dh1v, mh, r2, g2v).astype(BF16)
        dm_ref[...] = dmb
        dyc = _dot_nt(dmb, wout_v[...])
        dyconv = dyc[:, 0:D_CONV]

        zb = z_ref[0, :, 0:IN_SHARD]
        zc = z_ref[0, :, IN_SHARD:2 * IN_SHARD]
        zv = z_ref[0, :, 2 * IN_SHARD:3 * IN_SHARD]
        zp = z_ref[0, :, 3 * IN_SHARD:4 * IN_SHARD]
        halo = jnp.where(tr == 0, zm_ref[...], zh_ref[0])
        cvb[0:HALO, :] = halo[:, IN_SHARD:2 * IN_SHARD] * halo[:, 2 * IN_SHARD:3 * IN_SHARD]
        pb[0:HALO, :] = halo[:, 3 * IN_SHARD:4 * IN_SHARD]
        cv0 = zc * zv
        cvb[HALO:HALO + tm, :] = cv0
        pb[HALO:HALO + tm, :] = zp
        cw = cw_ref[...]
        cv2 = cvb[HALO - 2:HALO - 2 + tm, :]
        cv1 = cvb[HALO - 1:HALO - 1 + tm, :]
        conv = cw[0:1] * cv2 + cw[1:2] * cv1 + cw[2:3] * cv0
        parts = [(zb * conv).astype(BF16)]
        for g in range(N_POOL_GROUPS):
            pooled = _pool_fwd(pb, g, tm).astype(BF16)
            mixed = _dot(pooled, pw_ref[g])
            sc = ps_ref[:, _gcols(g)]
            parts.append((mixed * sc).astype(BF16))
            dyp = dyc[:, D_CONV + g * POOL_GROUP:D_CONV + (g + 1) * POOL_GROUP]
            dsc_ref[:, _gcols(g)] += _rows8(dyp * mixed)
            dmix = (dyp * sc).astype(BF16)
            dpw_ref[g] += _dot_tn(pooled, dmix)
            dqb[0:tm, _gcols(g)] = _dot_nt(dmix, pw_ref[g])
        yc_ref[...] = jnp.concatenate(parts, axis=1)

        dconv = dyconv * zb
        dcb[0:tm, :] = dconv
        dcv = cw[2:3] * dconv + cw[1:2] * dcb[1:1 + tm, :] + cw[0:1] * dcb[2:2 + tm, :]
        dcw_ref[0:8, :] += _rows8(dconv * cv2)
        dcw_ref[8:16, :] += _rows8(dconv * cv1)
        dcw_ref[16:24, :] += _rows8(dconv * cv0)
        dzs = [(dyconv * conv).astype(BF16), (dcv * zv).astype(BF16), (dcv * zc).astype(BF16),
               jnp.concatenate([_pool_bwd(dqb, g, tm) for g in range(N_POOL_GROUPS)], axis=1).astype(BF16)]
        da = jnp.zeros((tm, D_MODEL), F32)
        for j in range(N_CHIPS):
            dz_ref[j] = dzs[j]
            da = da + _dot_nt(dzs[j], win_v[j])
        xt = x_ref[0]
        r1 = _rstd(xt)
        xh = xt * r1
        a_ref[...] = (xh * g1v).astype(BF16)
        dg1_ref[...] += _rows8(da * xh)
        dx_ref[0] = dh1v + _rms_bwd(da, xh, r1, g1v)

        @pl.when(tr == 0)
        def _():
            mcb[0:HALO, :] = jnp.zeros((HALO, D_CONV), F32)
            mqb[0:HALO, :] = jnp.zeros((HALO, D_POOL), F32)
            mcb[HALO:2 * HALO, :] = dcb[0:HALO, :]
            mqb[HALO:2 * HALO, :] = dqb[0:HALO, :]
            dcv_m = cw[1:2] * mcb[1:1 + HALO, :] + cw[0:1] * mcb[2:2 + HALO, :]
            dzm_ref[:, IN_SHARD:2 * IN_SHARD] += dcv_m * zm_ref[:, 2 * IN_SHARD:3 * IN_SHARD]
            dzm_ref[:, 2 * IN_SHARD:3 * IN_SHARD] += dcv_m * zm_ref[:, IN_SHARD:2 * IN_SHARD]
            dzm_ref[:, 3 * IN_SHARD:4 * IN_SHARD] += jnp.concatenate(
                [_pool_bwd(mqb, g, HALO) for g in range(N_POOL_GROUPS)], axis=1)

        @pl.when((s == n_seq - 1) & (i == n_t - 1))
        def _():
            ex.finish()

    row3 = lambda c: pl.BlockSpec((1, tm, c), lambda s, i: (s, n_t - 1 - i, 0))
    row2 = lambda c: pl.BlockSpec((tm, c), lambda s, i: (s * n_t + n_t - 1 - i, 0))
    halo_spec = pl.BlockSpec((1, HALO, D_IN_PROJ), lambda s, i: (s, jnp.maximum((n_t - 1 - i) * hb - 1, 0), 0))
    n_rows = n_seq * seq
    act_bf = jax.ShapeDtypeStruct((n_rows, D_MODEL), BF16)
    outs = pl.pallas_call(
        body, name="mixer_bwd", grid=(n_seq, n_t),
        out_shape=[jax.ShapeDtypeStruct((n_seq, seq, D_MODEL), F32),
                   jax.ShapeDtypeStruct((N_CHIPS, n_rows, IN_SHARD), BF16), act_bf, act_bf, act_bf,
                   jax.ShapeDtypeStruct((8, D_MODEL), F32), jax.ShapeDtypeStruct((8, D_MODEL), F32),
                   jax.ShapeDtypeStruct((8, D_POOL), F32), jax.ShapeDtypeStruct((24, D_CONV), F32),
                   jax.ShapeDtypeStruct((N_POOL_GROUPS, POOL_GROUP, POOL_GROUP), F32),
                   jax.ShapeDtypeStruct((N_META, D_IN_PROJ), F32)] + _ExchangeHalves.out_shape(ffn_grads),
        in_specs=[row3(D_MODEL), row3(D_MODEL), row3(D_IN_PROJ), halo_spec, row3(D_MODEL),
                  _full((N_META, D_IN_PROJ)), _full((1, D_MODEL)), _full((1, D_MODEL)), _full((3, D_CONV)),
                  _full((N_POOL_GROUPS, POOL_GROUP, POOL_GROUP)), _full((1, D_POOL)), ANY, ANY] + [ANY] * n_ex,
        out_specs=[row3(D_MODEL), pl.BlockSpec((N_CHIPS, tm, IN_SHARD), lambda s, i: (0, s * n_t + n_t - 1 - i, 0)),
                   row2(D_MODEL), row2(D_MODEL), row2(D_MODEL),
                   _full((8, D_MODEL)), _full((8, D_MODEL)), _full((8, D_POOL)), _full((24, D_CONV)),
                   _full((N_POOL_GROUPS, POOL_GROUP, POOL_GROUP)), _full((N_META, D_IN_PROJ))] + [ANY] * n_ex,
        scratch_shapes=[pltpu.VMEM((N_CHIPS, D_MODEL, IN_SHARD), BF16), pltpu.VMEM((D_MODEL, D_MODEL), BF16),
                        pltpu.VMEM((HALO + tm, D_CONV), F32), pltpu.VMEM((HALO + tm, D_POOL), F32),
                        pltpu.VMEM((tm + HALO, D_CONV), F32), pltpu.VMEM((tm + HALO, D_POOL), F32),
                        pltpu.VMEM((2 * HALO, D_CONV), F32), pltpu.VMEM((2 * HALO, D_POOL), F32),
                        pltpu.SemaphoreType.DMA] + _ExchangeHalves.scratch(n_ex),
        compiler_params=_cparams(2),
    )(dh1, m3, z3, z3, x3, zmeta, g1, g2, convw, poolw, pscale, win_all, wout, *ffn_grads)
    return outs[:11], outs[11:]


def _meta_bwd(dzm, meta_full, g1, win_all):
    def body(dzm_ref, meta_ref, g1_ref, win_ref, dmeta_ref, dg1_ref, a_ref, dzb_ref):
        xm = meta_ref[...]
        r = _rstd(xm)
        xh = xm * r
        g1v = g1_ref[...]
        a_ref[...] = (xh * g1v).astype(BF16)
        da = jnp.zeros((N_META, D_MODEL), F32)
        for j in range(N_CHIPS):
            dzj = dzm_ref[:, j * IN_SHARD:(j + 1) * IN_SHARD].astype(BF16)
            dzb_ref[j] = dzj
            da = da + _dot_nt(dzj, win_ref[j])
        dg1_ref[...] = _rows8(da * xh)
        dmeta_ref[...] = _rms_bwd(da, xh, r, g1v)

    vm = pl.BlockSpec(memory_space=pltpu.VMEM)
    return pl.pallas_call(
        body, name="meta_bwd",
        out_shape=[jax.ShapeDtypeStruct((N_META, D_MODEL), F32), jax.ShapeDtypeStruct((8, D_MODEL), F32),
                   jax.ShapeDtypeStruct((N_META, D_MODEL), BF16), jax.ShapeDtypeStruct((N_CHIPS, N_META, IN_SHARD), BF16)],
        in_specs=[vm] * 4, out_specs=[vm] * 4,
    )(dzm, meta_full, g1, win_all)


def _mixer_weight_grads(a, dz, ycat, dm, a_meta, dz_meta, ffn_sums):
    n_rows = a.shape[0]
    tk = min(TK_DW, n_rows)
    n_k = n_rows // tk
    n_sc = len(ffn_sums)

    def body(a_ref, dz_ref, yc_ref, dm_ref, am_ref, dzm_ref, *rest):
        sc = _ScatterToChips(rest[:n_sc], rest[n_sc + 2:2 * n_sc + 2], *rest[-2:])
        dwin_ref, dwout_ref = rest[n_sc:n_sc + 2]
        j, k = pl.program_id(0), pl.program_id(1)

        @pl.when((j == 0) & (k == 0))
        def _():
            sc.start()

        @pl.when(k == 0)
        def _():
            dwin_ref[0] = _dot_tn(am_ref[...], dzm_ref[0])
            dwout_ref[...] = jnp.zeros_like(dwout_ref)

        dwin_ref[0] += _dot_tn(a_ref[...], dz_ref[0])
        dwout_ref[0] += _dot_tn(yc_ref[...], dm_ref[...])

        @pl.when((j == N_CHIPS - 1) & (k == n_k - 1))
        def _():
            sc.finish()

    row = pl.BlockSpec((tk, D_MODEL), lambda j, k: (k, 0))
    outs = pl.pallas_call(
        body, name="mixer_weight_grads", grid=(N_CHIPS, n_k),
        out_shape=[jax.ShapeDtypeStruct((N_CHIPS, D_MODEL, IN_SHARD), F32),
                   jax.ShapeDtypeStruct((N_CHIPS, OUT_SHARD, D_MODEL), F32)] + _ScatterToChips.out_shape(ffn_sums),
        in_specs=[row, pl.BlockSpec((1, tk, IN_SHARD), lambda j, k: (j, k, 0)),
                  pl.BlockSpec((tk, OUT_SHARD), lambda j, k: (k, j)), row,
                  _full((N_META, D_MODEL)), pl.BlockSpec((1, N_META, IN_SHARD), lambda j, k: (j, 0, 0))]
        + [ANY] * n_sc,
        out_specs=[pl.BlockSpec((1, D_MODEL, IN_SHARD), lambda j, k: (j, 0, 0)),
                   pl.BlockSpec((1, OUT_SHARD, D_MODEL), lambda j, k: (j, 0, 0))] + [ANY] * n_sc,
        scratch_shapes=_ScatterToChips.scratch(n_sc),
        compiler_params=_cparams(2),
    )(a, dz, ycat, dm, a_meta, dz_meta, *ffn_sums)
    return outs[:2], outs[2:]


def kernel(x, meta_tokens, norm_mix_pre, w_in, conv_w, pool_w, pool_scale, w_out, norm_mix_post, norm_ffn_pre, w_gate, w_up, w_down, norm_ffn_post, loss_target, m_meta_tokens, m_norm_mix_pre, m_w_in, m_conv_w, m_pool_w, m_pool_scale, m_w_out, m_norm_mix_post, m_norm_ffn_pre, m_w_gate, m_w_up, m_w_down, m_norm_ffn_post, v_meta_tokens, v_norm_mix_pre, v_w_in, v_conv_w, v_pool_w, v_pool_scale, v_w_out, v_norm_mix_post, v_norm_ffn_pre, v_w_gate, v_w_up, v_w_down, v_norm_ffn_post):
    n_seq, seq, _ = x.shape
    n_rows = n_seq * seq
    chip = 2 * lax.axis_index("x") + lax.axis_index("y")
    meta_cols = D_MODEL // N_CHIPS
    conv_cols = D_CONV // N_CHIPS

    small = jnp.zeros((2 * HALO, meta_cols), F32)
    small = small.at[0:N_META, :].set(meta_tokens).at[N_META:N_META + 3, 0:conv_cols].set(conv_w[0])
    win_all, wout_all, small_all = _all_gather_shards([w_in[0].astype(BF16), w_out[0].astype(BF16), small])
    meta_full = small_all[:, 0:N_META, :].transpose(1, 0, 2).reshape(N_META, D_MODEL)
    conv_full = small_all[:, N_META:N_META + 3, 0:conv_cols].transpose(1, 0, 2).reshape(3, D_CONV)
    wout_full = wout_all.reshape(D_MODEL, D_MODEL)
    poolw_bf = pool_w[0].astype(BF16)
    pscale = pool_scale
    g1, g2, g3, g4 = norm_mix_pre, norm_mix_post, norm_ffn_pre, norm_ffn_post
    place = jnp.stack([chip, lax.axis_index("c")]).astype(jnp.int32)

    zmeta = _meta_fwd(meta_full, g1, win_all)
    (z3, m3, h1), (wg_all, wu_all, wd_all) = _mixer_fwd(
        x, zmeta, g1, g2, conv_full, poolw_bf, pscale, win_all, wout_full,
        [w_gate[0].astype(BF16), w_up[0].astype(BF16), w_down[0].astype(BF16)])
    dh1, f_bf, dd_bf, ds_bf, du_bf, gg_bf, lossp, dg3p, dg4p = _ffn_fwd_bwd(
        h1.reshape(n_rows, D_MODEL), loss_target.reshape(n_rows, D_MODEL), g3, g4, wg_all, wu_all, wd_all)
    ffn_grads = _ffn_weight_grads(f_bf, ds_bf, du_bf, gg_bf, dd_bf)
    (grad_x, dz_bf, a_bf, yc_bf, dm_bf, dg1p, dg2p, dscp, dcwp, dpw, dzm), ffn_recvs = _mixer_bwd(
        dh1.reshape(n_seq, seq, D_MODEL), m3, z3, x, zmeta, g1, g2, conv_full, poolw_bf, pscale, win_all, wout_full,
        ffn_grads)
    dmeta, dg1m, a_meta, dz_meta = _meta_bwd(dzm, meta_full, g1, win_all)
    mix_grads, ffn_rbufs = _mixer_weight_grads(
        a_bf, dz_bf, yc_bf, dm_bf, a_meta, dz_meta, [_add_pairs(g, r, place) for g, r in zip(ffn_grads, ffn_recvs)])

    mix_recvs = _exchange_halves(mix_grads)
    mix_rbufs = _scatter_to_chips([_add_pairs(g, r, place) for g, r in zip(mix_grads, mix_recvs)])
    grads = list(mix_grads) + list(ffn_grads)
    recvs = list(mix_recvs) + list(ffn_recvs)
    rbufs = list(mix_rbufs) + list(ffn_rbufs)
    reduced = _gather_halves([_add_chips(g, r, rb, place) for g, r, rb in zip(grads, recvs, rbufs)])
    g_win, g_wout, g_wg, g_wu, g_wd = [r.reshape(2 * r.shape[1], r.shape[2]) for r in reduced]

    a_red, b_red, c_red = _all_reduce_small(dg1p, dg1m, dg2p, dg3p, dg4p, lossp, dmeta, dscp, dcwp,
                                            dpw.reshape(SMALL_C_ROWS, POOL_GROUP))
    loss = a_red[4, 0]
    g_g1, g_g2, g_g3, g_g4 = a_red[0:1], a_red[1:2], a_red[2:3], a_red[3:4]
    g_meta = lax.dynamic_slice(a_red, (8, chip * meta_cols), (N_META, meta_cols))
    g_pscale = b_red[0:1]
    g_conv = lax.dynamic_slice(b_red, (1, chip * conv_cols), (3, conv_cols))
    g_poolw = c_red

    big = [(w_in, g_win, m_w_in, v_w_in), (w_out, g_wout, m_w_out, v_w_out), (w_gate, g_wg, m_w_gate, v_w_gate),
           (w_up, g_wu, m_w_up, v_w_up), (w_down, g_wd, m_w_down, v_w_down)]
    big_out = [_adamw_big(w[0], g, m[0], v[0]) for (w, g, m, v) in big]
    small_groups = [
        (meta_tokens, g_meta, m_meta_tokens, v_meta_tokens),
        (g1, g_g1, m_norm_mix_pre, v_norm_mix_pre),
        (conv_w[0], g_conv, m_conv_w[0], v_conv_w[0]),
        (pool_w.reshape(SMALL_C_ROWS, POOL_GROUP), g_poolw, m_pool_w.reshape(SMALL_C_ROWS, POOL_GROUP),
         v_pool_w.reshape(SMALL_C_ROWS, POOL_GROUP)),
        (pool_scale, g_pscale, m_pool_scale, v_pool_scale),
        (g2, g_g2, m_norm_mix_post, v_norm_mix_post),
        (g3, g_g3, m_norm_ffn_pre, v_norm_ffn_pre),
        (g4, g_g4, m_norm_ffn_post, v_norm_ffn_post),
    ]
    small_out = _adamw_small(small_groups)

    grads_out = [g_meta, g_g1, g_win[None], g_conv[None], g_poolw.reshape(pool_w.shape), g_pscale, g_wout[None],
                 g_g2, g_g3, g_wg[None], g_wu[None], g_wd[None], g_g4]
    s_meta, s_g1, s_conv, s_poolw, s_pscale, s_g2, s_g3, s_g4 = small_out
    b_win, b_wout, b_wg, b_wu, b_wd = big_out

    def leaf(k):
        return [s_meta[k], s_g1[k], b_win[k][None], s_conv[k][None], s_poolw[k].reshape(pool_w.shape), s_pscale[k],
                b_wout[k][None], s_g2[k], s_g3[k], b_wg[k][None], b_wu[k][None], b_wd[k][None], s_g4[k]]

    return (loss, grad_x, *grads_out, *leaf(0), *leaf(1), *leaf(2))
```

```python
import jax
import jax.numpy as jnp
from jax import lax
from jax.experimental import pallas as pl
from jax.experimental.pallas import tpu as pltpu

F32 = jnp.float32
BF16 = jnp.bfloat16
MESH = pl.DeviceIdType.MESH

D_MODEL = 1024
D_CONV = 512
D_POOL = 512
POOL_GROUP = 128
N_POOL_GROUPS = 4
D_IN_PROJ = 2048
D_FF = 2816
N_CHIPS = 4
FF_SHARD = D_FF // N_CHIPS
IN_SHARD = D_IN_PROJ // N_CHIPS
OUT_SHARD = D_MODEL // N_CHIPS
N_META = 16
HALO = 16
RMS_EPS = 1e-6

ADAM_LR = 0.001
ADAM_B1 = 0.9
ADAM_B2 = 0.999
ADAM_EPS = 1e-08
ADAM_WD = 0.01
ADAM_STEP = 10

TM_MIX_FWD = 512
TM_MIX_BWD = 256
TM_FFN = 256
TK_DW = 512
FF_CHUNK = 512
VMEM_LIMIT = 56 * 1024 * 1024


def _cparams(n_grid):
    return pltpu.CompilerParams(dimension_semantics=("arbitrary",) * n_grid, vmem_limit_bytes=VMEM_LIMIT)


def _dot(a, b):
    return jnp.dot(a, b, preferred_element_type=F32)


def _dot_nt(a, b):
    return lax.dot_general(a, b, (((1,), (1,)), ((), ())), preferred_element_type=F32)


def _dot_tn(a, b):
    return lax.dot_general(a, b, (((0,), (0,)), ((), ())), preferred_element_type=F32)


def _rows8(v):
    r, c = v.shape
    return v.reshape(r // 8, 8, c).sum(axis=0)


def _rstd(v):
    return lax.rsqrt(jnp.mean(v * v, axis=-1, keepdims=True) + RMS_EPS)


def _rms_bwd(dy, xhat, rstd, gain):
    dyg = dy * gain
    return rstd * (dyg - xhat * jnp.mean(dyg * xhat, axis=-1, keepdims=True))


def _sigmoid(v):
    return 1.0 / (1.0 + jnp.exp(-v))


def _gcols(g):
    return slice(g * POOL_GROUP, (g + 1) * POOL_GROUP)


def _pool_fwd(pb, g, n):
    win = 2 << g
    cur = pb[HALO:HALO + n, _gcols(g)]
    acc = cur
    for k in range(1, win):
        acc = acc + pb[HALO - k:HALO - k + n, _gcols(g)]
    return acc * (1.0 / win) - cur


def _pool_bwd(qb, g, n):
    win = 2 << g
    cur = qb[0:n, _gcols(g)]
    acc = cur
    for k in range(1, win):
        acc = acc + qb[k:k + n, _gcols(g)]
    return acc * (1.0 / win) - cur


def _full(shape):
    nd = len(shape)
    return pl.BlockSpec(shape, lambda *_: (0,) * nd)


ANY = pl.BlockSpec(memory_space=pl.ANY)


def _mesh_pos():
    x, y, c = lax.axis_index("x"), lax.axis_index("y"), lax.axis_index("c")
    chips = [(1 - x, y), (x, 1 - y), (1 - x, 1 - y)]
    return x, y, c, chips


def _half(ref, h):
    hr = ref.shape[0] // 2
    return ref.at[pl.ds(h * hr, hr), :]


class _AllGather:
    def __init__(self, ins, outs, send_sems, recv_sems):
        self.ins, self.outs, self.send_sems, self.recv_sems = ins, outs, send_sems, recv_sems
        self.n = len(ins)

    @staticmethod
    def scratch(n):
        return [pltpu.SemaphoreType.DMA((6 * n,)), pltpu.SemaphoreType.DMA((6 * n,))]

    @staticmethod
    def out_shape(shards):
        return [jax.ShapeDtypeStruct((N_CHIPS,) + s.shape, s.dtype) for s in shards]

    def _copy(self, a, k, src, dst, to):
        return pltpu.make_async_remote_copy(src_ref=src, dst_ref=dst, send_sem=self.send_sems.at[6 * a + k],
                                            recv_sem=self.recv_sems.at[6 * a + k], device_id=to, device_id_type=MESH)

    def _ici(self, a, k):
        x, y, c, chips = _mesh_pos()
        return self._copy(a, k, _half(self.ins[a], c), _half(self.outs[a].at[2 * x + y], c), (*chips[k], c))

    def _d2d(self, a, k, h):
        x, y, c, chips = _mesh_pos()
        slot = _half(self.outs[a].at[2 * chips[k][0] + chips[k][1]], h)
        return self._copy(a, 3 + k, slot, slot, (x, y, 1 - c))

    def start(self):
        for a in range(self.n):
            for k in range(3):
                self._ici(a, k).start()

    def forward(self, a):
        c = lax.axis_index("c")
        for k in range(3):
            self._ici(a, k).wait_recv()
            self._d2d(a, k, c).start()

    def finish(self):
        c = lax.axis_index("c")
        for a in range(self.n):
            for k in range(3):
                self._d2d(a, k, 1 - c).wait_recv()
        for a in range(self.n):
            for k in range(3):
                self._ici(a, k).wait_send()
                self._d2d(a, k, c).wait_send()


def _fill_own_slot(gathered, shards):
    chip = 2 * lax.axis_index("x") + lax.axis_index("y")
    return [lax.dynamic_update_slice(o, s[None], (chip, 0, 0)) for o, s in zip(gathered, shards)]


def _all_gather_shards(shards):
    n = len(shards)

    def body(*refs):
        ag = _AllGather(refs[:n], refs[n:2 * n], *refs[2 * n:])
        ag.start()
        for a in range(n):
            ag.forward(a)
        ag.finish()

    outs = pl.pallas_call(
        body, name="all_gather_weights", out_shape=_AllGather.out_shape(shards),
        in_specs=[ANY] * n, out_specs=[ANY] * n, scratch_shapes=_AllGather.scratch(n),
    )(*shards)
    return _fill_own_slot(outs, shards)


class _ExchangeHalves:
    def __init__(self, ins, recvs, send_sems, recv_sems):
        self.ins, self.recvs, self.send_sems, self.recv_sems = ins, recvs, send_sems, recv_sems

    @staticmethod
    def scratch(n):
        return [pltpu.SemaphoreType.DMA((n,)), pltpu.SemaphoreType.DMA((n,))]

    @staticmethod
    def out_shape(grads):
        return [jax.ShapeDtypeStruct((g.shape[0], g.shape[1] // 2, g.shape[2]), g.dtype) for g in grads]

    def _copies(self):
        x, y, c, _ = _mesh_pos()
        out = []
        for a, (src, dst) in enumerate(zip(self.ins, self.recvs)):
            hr = src.shape[1] // 2
            out.append(pltpu.make_async_remote_copy(
                src_ref=src.at[:, pl.ds((1 - c) * hr, hr), :], dst_ref=dst, send_sem=self.send_sems.at[a],
                recv_sem=self.recv_sems.at[a], device_id=(x, y, 1 - c), device_id_type=MESH))
        return out

    def start(self):
        for cp in self._copies():
            cp.start()

    def finish(self):
        for cp in self._copies():
            cp.wait()


def _exchange_halves(grads):
    n = len(grads)

    def body(*refs):
        ex = _ExchangeHalves(refs[:n], refs[n:2 * n], *refs[2 * n:])
        ex.start()
        ex.finish()

    return pl.pallas_call(
        body, name="grad_exchange_halves", out_shape=_ExchangeHalves.out_shape(grads),
        in_specs=[ANY] * n, out_specs=[ANY] * n, scratch_shapes=_ExchangeHalves.scratch(n),
    )(*grads)


class _ScatterToChips:
    def __init__(self, ins, rbufs, send_sems, recv_sems):
        self.ins, self.rbufs, self.send_sems, self.recv_sems = ins, rbufs, send_sems, recv_sems

    @staticmethod
    def scratch(n):
        return [pltpu.SemaphoreType.DMA((3 * n,)), pltpu.SemaphoreType.DMA((3 * n,))]

    @staticmethod
    def out_shape(sums):
        return [jax.ShapeDtypeStruct((3,) + s.shape[1:], BF16) for s in sums]

    def _copies(self):
        x, y, c, chips = _mesh_pos()
        out = []
        for a, (src, dst) in enumerate(zip(self.ins, self.rbufs)):
            for k, chip in enumerate(chips):
                out.append(pltpu.make_async_remote_copy(
                    src_ref=src.at[2 * chip[0] + chip[1]], dst_ref=dst.at[k], send_sem=self.send_sems.at[3 * a + k],
                    recv_sem=self.recv_sems.at[3 * a + k], device_id=(*chip, c), device_id_type=MESH))
        return out

    def start(self):
        for cp in self._copies():
            cp.start()

    def finish(self):
        for cp in self._copies():
            cp.wait()


def _scatter_to_chips(sums_bf16):
    n = len(sums_bf16)

    def body(*refs):
        sc = _ScatterToChips(refs[:n], refs[n:2 * n], *refs[2 * n:])
        sc.start()
        sc.finish()

    return pl.pallas_call(
        body, name="grad_scatter_to_chips", out_shape=_ScatterToChips.out_shape(sums_bf16),
        in_specs=[ANY] * n, out_specs=[ANY] * n, scratch_shapes=_ScatterToChips.scratch(n),
    )(*sums_bf16)


def _gather_halves(halves):
    n = len(halves)

    def body(*refs):
        ins, outs = refs[:n], refs[n:2 * n]
        send_sems, recv_sems = refs[2 * n:]
        x, y, c, _ = _mesh_pos()
        sib = (x, y, 1 - c)
        remote = [pltpu.make_async_remote_copy(src_ref=ins[a].at[c], dst_ref=outs[a].at[c],
                                               send_sem=send_sems.at[a], recv_sem=recv_sems.at[a],
                                               device_id=sib, device_id_type=MESH) for a in range(n)]
        for cp in remote:
            cp.start()
        for a in range(n):
            pltpu.make_async_remote_copy(src_ref=ins[a].at[1 - c], dst_ref=outs[a].at[1 - c], send_sem=send_sems.at[a],
                                         recv_sem=recv_sems.at[a], device_id=sib, device_id_type=MESH).wait_recv()
        for cp in remote:
            cp.wait_send()

    return pl.pallas_call(
        body, name="grad_gather_halves",
        out_shape=[jax.ShapeDtypeStruct(h.shape, F32) for h in halves],
        in_specs=[ANY] * n, out_specs=[ANY] * n, input_output_aliases={a: a for a in range(n)},
        scratch_shapes=[pltpu.SemaphoreType.DMA((n,)), pltpu.SemaphoreType.DMA((n,))],
    )(*halves)


SMALL_A_ROWS = 24
SMALL_B_ROWS = 8
SMALL_C_ROWS = N_POOL_GROUPS * POOL_GROUP


def _all_reduce_small(dg1, dg1m, dg2, dg3, dg4, lossp, dmeta, dscale, dconv, dpoolw):
    def body(dg1_ref, dg1m_ref, dg2_ref, dg3_ref, dg4_ref, loss_ref, dmeta_ref, dsc_ref, dcw_ref, dpw_ref,
             a_out, b_out, c_out, a_buf, b_buf, c_buf, a_rcv, b_rcv, c_rcv, send_sems, recv_sems):
        x, y, c, _ = _mesh_pos()
        peers = [(x, y, 1 - c), (1 - x, y, c), (x, 1 - y, c)]

        def rowsum(v):
            return jnp.sum(v, axis=0, keepdims=True)

        a_buf[0, 0:1, :] = rowsum(dg1_ref[...] + dg1m_ref[...])
        a_buf[0, 1:2, :] = rowsum(dg2_ref[...])
        a_buf[0, 2:3, :] = rowsum(dg3_ref[...])
        a_buf[0, 3:4, :] = rowsum(dg4_ref[...])
        loss = jnp.sum(rowsum(loss_ref[...]), axis=1, keepdims=True) * (0.5 / D_MODEL)
        a_buf[0, 4:5, :] = jnp.broadcast_to(loss, (1, D_MODEL))
        a_buf[0, 5:8, :] = jnp.zeros((3, D_MODEL), F32)
        a_buf[0, 8:24, :] = dmeta_ref[...]
        b_buf[0, 0:1, :] = rowsum(dsc_ref[...])
        for k in range(3):
            b_buf[0, 1 + k:2 + k, :] = rowsum(dcw_ref[8 * k:8 * k + 8, :])
        b_buf[0, 4:8, :] = jnp.zeros((4, D_CONV), F32)
        c_buf[0] = dpw_ref[...]

        for st, peer in enumerate(peers):
            cps = []
            for i, (buf, rcv) in enumerate(((a_buf, a_rcv), (b_buf, b_rcv), (c_buf, c_rcv))):
                cps.append(pltpu.make_async_remote_copy(
                    src_ref=buf.at[st], dst_ref=rcv.at[st], send_sem=send_sems.at[3 * st + i],
                    recv_sem=recv_sems.at[3 * st + i], device_id=peer, device_id_type=MESH))
            for cp in cps:
                cp.start()
            for cp in cps:
                cp.wait()
            if st < 2:
                a_buf[st + 1] = a_buf[st] + a_rcv[st]
                b_buf[st + 1] = b_buf[st] + b_rcv[st]
                c_buf[st + 1] = c_buf[st] + c_rcv[st]
            else:
                a_out[...] = a_buf[st] + a_rcv[st]
                b_out[...] = b_buf[st] + b_rcv[st]
                c_out[...] = c_buf[st] + c_rcv[st]

    vm = pl.BlockSpec(memory_space=pltpu.VMEM)
    shapes = [(SMALL_A_ROWS, D_MODEL), (SMALL_B_ROWS, D_CONV), (SMALL_C_ROWS, POOL_GROUP)]
    return pl.pallas_call(
        body, name="all_reduce_small",
        out_shape=[jax.ShapeDtypeStruct(s, F32) for s in shapes],
        in_specs=[vm] * 10, out_specs=[vm] * 3,
        scratch_shapes=[pltpu.VMEM((3,) + s, F32) for s in shapes] + [pltpu.VMEM((3,) + s, F32) for s in shapes]
        + [pltpu.SemaphoreType.DMA((9,)), pltpu.SemaphoreType.DMA((9,))],
    )(dg1, dg1m, dg2, dg3, dg4, lossp, dmeta, dscale, dconv, dpoolw)


def _row_block(rows):
    for cand in (512, 448, 384, 352, 320, 256, 128, 64, 32, 16):
        if rows % cand == 0:
            return cand
    return rows


def _add_pairs(grad, recv, place):
    n_sh, rows2, cols = grad.shape
    hr = rows2 // 2
    br = _row_block(hr)

    def body(place_ref, a_ref, b_ref, o_ref):
        o_ref[...] = (a_ref[0] + b_ref[...]).astype(BF16)

    return pl.pallas_call(
        body, name="grad_add_pairs",
        grid_spec=pltpu.PrefetchScalarGridSpec(
            num_scalar_prefetch=1, grid=(n_sh, hr // br),
            in_specs=[pl.BlockSpec((1, 1, br, cols), lambda j, i, p: (j, p[1], i, 0)),
                      pl.BlockSpec((1, br, cols), lambda j, i, p: (j, i, 0))],
            out_specs=pl.BlockSpec((1, br, cols), lambda j, i, p: (j, i, 0))),
        out_shape=jax.ShapeDtypeStruct((n_sh, hr, cols), BF16), compiler_params=_cparams(2),
    )(place, grad.reshape(n_sh, 2, hr, cols), recv)


def _add_chips(grad, recv, rbuf, place):
    n_sh, rows2, cols = grad.shape
    hr = rows2 // 2
    br = _row_block(hr)

    def body(place_ref, a_ref, b_ref, r_ref, o_ref):
        own = a_ref[0, 0] + b_ref[0]
        o_ref[0] = ((own + r_ref[0].astype(F32)) + r_ref[1].astype(F32)) + r_ref[2].astype(F32)

    return pl.pallas_call(
        body, name="grad_add_chips",
        grid_spec=pltpu.PrefetchScalarGridSpec(
            num_scalar_prefetch=1, grid=(hr // br,),
            in_specs=[pl.BlockSpec((1, 1, br, cols), lambda i, p: (p[0], p[1], i, 0)),
                      pl.BlockSpec((1, br, cols), lambda i, p: (p[0], i, 0)),
                      pl.BlockSpec((3, br, cols), lambda i, p: (0, i, 0))],
            out_specs=pl.BlockSpec((1, br, cols), lambda i, p: (p[1], i, 0))),
        out_shape=jax.ShapeDtypeStruct((2, hr, cols), F32), compiler_params=_cparams(1),
    )(place, grad.reshape(n_sh, 2, hr, cols), recv, rbuf)


def _adamw_math(w, g, m, v):
    m2 = ADAM_B1 * m + (1.0 - ADAM_B1) * g
    v2 = ADAM_B2 * v + (1.0 - ADAM_B2) * (g * g)
    m_hat = m2 / (1.0 - ADAM_B1 ** ADAM_STEP)
    v_hat = v2 / (1.0 - ADAM_B2 ** ADAM_STEP)
    delta = -ADAM_LR * (m_hat / (jnp.sqrt(v_hat) + ADAM_EPS) + ADAM_WD * w)
    return delta, m2, v2


def _adamw_big(w, g, m, v):
    rows, cols = w.shape
    br = _row_block(rows)

    def body(w_ref, g_ref, m_ref, v_ref, d_ref, m2_ref, v2_ref):
        d, m2, v2 = _adamw_math(w_ref[...], g_ref[...], m_ref[...], v_ref[...])
        d_ref[...] = d
        m2_ref[...] = m2
        v2_ref[...] = v2

    spec = pl.BlockSpec((br, cols), lambda i: (i, 0))
    return pl.pallas_call(
        body, name="adamw_big", grid=(rows // br,),
        out_shape=[jax.ShapeDtypeStruct((rows, cols), F32)] * 3,
        in_specs=[spec] * 4, out_specs=[spec] * 3, compiler_params=_cparams(1),
    )(w, g, m, v)


def _adamw_small(groups):
    n = len(groups)

    def body(*refs):
        ins, outs = refs[:4 * n], refs[4 * n:]
        for i in range(n):
            w, g, m, v = (r[...] for r in ins[4 * i:4 * i + 4])
            d, m2, v2 = _adamw_math(w, g, m, v)
            outs[3 * i][...] = d
            outs[3 * i + 1][...] = m2
            outs[3 * i + 2][...] = v2

    vm = pl.BlockSpec(memory_space=pltpu.VMEM)
    flat = [a for grp in groups for a in grp]
    out_shape = [jax.ShapeDtypeStruct(grp[0].shape, F32) for grp in groups for _ in range(3)]
    outs = pl.pallas_call(body, name="adamw_small", out_shape=out_shape,
                          in_specs=[vm] * (4 * n), out_specs=[vm] * (3 * n))(*flat)
    return [tuple(outs[3 * i:3 * i + 3]) for i in range(n)]


def _load_weights(pairs, sem):
    for src, dst in pairs:
        cp = pltpu.make_async_copy(src, dst, sem)
        cp.start()
        cp.wait()


def _meta_fwd(meta_full, g1, win_all):
    def body(meta_ref, g1_ref, win_ref, z_ref):
        xm = meta_ref[...]
        a = (xm * _rstd(xm) * g1_ref[...]).astype(BF16)
        for j in range(N_CHIPS):
            z_ref[:, j * IN_SHARD:(j + 1) * IN_SHARD] = _dot(a, win_ref[j])

    vm = pl.BlockSpec(memory_space=pltpu.VMEM)
    return pl.pallas_call(body, name="meta_fwd", out_shape=jax.ShapeDtypeStruct((N_META, D_IN_PROJ), F32),
                          in_specs=[vm] * 3, out_specs=vm)(meta_full, g1, win_all)


def _mixer_fwd(x3, zmeta, g1, g2, convw, poolw, pscale, win_all, wout, ffn_shards):
    n_seq, seq, _ = x3.shape
    tm = min(TM_MIX_FWD, seq)
    n_t = seq // tm
    n_steps = n_seq * n_t
    n_ag = len(ffn_shards)

    def body(x_ref, zm_ref, g1_ref, g2_ref, cw_ref, pw_ref, ps_ref, win_hbm, wout_hbm, *rest):
        ag = _AllGather(rest[:n_ag], rest[n_ag + 3:2 * n_ag + 3], *rest[-2:])
        z_ref, m_ref, h1_ref = rest[n_ag:n_ag + 3]
        win_v, wout_v, cvb, pb, sem = rest[2 * n_ag + 3:-2]
        s, t = pl.program_id(0), pl.program_id(1)
        step = s * n_t + t

        @pl.when(step == 0)
        def _():
            ag.start()
            _load_weights([(win_hbm, win_v), (wout_hbm, wout_v)], sem)

        for a in range(n_ag):
            @pl.when(step == min(((a + 1) * n_steps) // n_ag, n_steps - 1))
            def _():
                ag.forward(a)

        xt = x_ref[0]
        a = (xt * _rstd(xt) * g1_ref[...]).astype(BF16)
        zb = _dot(a, win_v[0])
        zc = _dot(a, win_v[1])
        zv = _dot(a, win_v[2])
        zp = _dot(a, win_v[3])
        z_ref[0, :, 0:IN_SHARD] = zb
        z_ref[0, :, IN_SHARD:2 * IN_SHARD] = zc
        z_ref[0, :, 2 * IN_SHARD:3 * IN_SHARD] = zv
        z_ref[0, :, 3 * IN_SHARD:4 * IN_SHARD] = zp

        @pl.when(t == 0)
        def _():
            cvb[0:HALO, :] = zm_ref[:, IN_SHARD:2 * IN_SHARD] * zm_ref[:, 2 * IN_SHARD:3 * IN_SHARD]
            pb[0:HALO, :] = zm_ref[:, 3 * IN_SHARD:4 * IN_SHARD]

        @pl.when(t > 0)
        def _():
            cvb[0:HALO, :] = cvb[tm:tm + HALO, :]
            pb[0:HALO, :] = pb[tm:tm + HALO, :]

        cv = zc * zv
        cvb[HALO:HALO + tm, :] = cv
        pb[HALO:HALO + tm, :] = zp
        cw = cw_ref[...]
        conv = cw[0:1] * cvb[HALO - 2:HALO - 2 + tm, :] + cw[1:2] * cvb[HALO - 1:HALO - 1 + tm, :] + cw[2:3] * cv
        parts = [(zb * conv).astype(BF16)]
        for g in range(N_POOL_GROUPS):
            pooled = _pool_fwd(pb, g, tm).astype(BF16)
            parts.append((_dot(pooled, pw_ref[g]) * ps_ref[:, _gcols(g)]).astype(BF16))
        m = _dot(jnp.concatenate(parts, axis=1), wout_v[...])
        m_ref[0] = m
        h1_ref[0] = xt + m * _rstd(m) * g2_ref[...]

        @pl.when(step == n_steps - 1)
        def _():
            ag.finish()

    row = lambda c: pl.BlockSpec((1, tm, c), lambda s, t: (s, t, 0))
    outs = pl.pallas_call(
        body, name="mixer_fwd", grid=(n_seq, n_t),
        out_shape=[jax.ShapeDtypeStruct((n_seq, seq, D_IN_PROJ), F32), jax.ShapeDtypeStruct((n_seq, seq, D_MODEL), F32),
                   jax.ShapeDtypeStruct((n_seq, seq, D_MODEL), F32)] + _AllGather.out_shape(ffn_shards),
        in_specs=[row(D_MODEL), _full((N_META, D_IN_PROJ)), _full((1, D_MODEL)), _full((1, D_MODEL)),
                  _full((3, D_CONV)), _full((N_POOL_GROUPS, POOL_GROUP, POOL_GROUP)), _full((1, D_POOL)), ANY, ANY]
        + [ANY] * n_ag,
        out_specs=[row(D_IN_PROJ), row(D_MODEL), row(D_MODEL)] + [ANY] * n_ag,
        scratch_shapes=[pltpu.VMEM((N_CHIPS, D_MODEL, IN_SHARD), BF16), pltpu.VMEM((D_MODEL, D_MODEL), BF16),
                        pltpu.VMEM((HALO + tm, D_CONV), F32), pltpu.VMEM((HALO + tm, D_POOL), F32),
                        pltpu.SemaphoreType.DMA] + _AllGather.scratch(n_ag),
        compiler_params=_cparams(2),
    )(x3, zmeta, g1, g2, convw, poolw, pscale, win_all, wout, *ffn_shards)
    return outs[:3], _fill_own_slot(outs[3:], ffn_shards)


def _ffn_chunks():
    out, r0 = [], 0
    while r0 < D_FF:
        out.append((r0, min(FF_CHUNK, D_FF - r0)))
        r0 += FF_CHUNK
    return out


def _ffn_fwd_bwd(h1, target, g3, g4, wg_t, wu_t, wd):
    n_rows = h1.shape[0]
    tm = min(TM_FFN, n_rows)
    chunks = _ffn_chunks()

    def body(h1_ref, t_ref, g3_ref, g4_ref, wg_hbm, wu_hbm, wd_hbm,
             dh1_ref, f_ref, dd_ref, ds_ref, du_ref, gg_ref, loss_ref, dg3_ref, dg4_ref,
             wg_v, wu_v, wd_v, s_sc, u_sc, sem):
        @pl.when(pl.program_id(0) == 0)
        def _():
            _load_weights([(wg_hbm, wg_v), (wu_hbm, wu_v), (wd_hbm, wd_v)], sem)
            loss_ref[...] = jnp.zeros_like(loss_ref)
            dg3_ref[...] = jnp.zeros_like(dg3_ref)
            dg4_ref[...] = jnp.zeros_like(dg4_ref)

        h1v = h1_ref[...]
        r3 = _rstd(h1v)
        hh = h1v * r3
        g3v, g4v = g3_ref[...], g4_ref[...]
        f = (hh * g3v).astype(BF16)
        f_ref[...] = f
        d = jnp.zeros((tm, D_MODEL), F32)
        for r0, sz in chunks:
            s = _dot_nt(f, wg_v[r0:r0 + sz, :])
            u = _dot_nt(f, wu_v[r0:r0 + sz, :])
            s_sc[:, r0:r0 + sz] = s
            u_sc[:, r0:r0 + sz] = u
            gc = (s * _sigmoid(s) * u).astype(BF16)
            gg_ref[:, r0:r0 + sz] = gc
            d = d + _dot(gc, wd_v[r0:r0 + sz, :])
        r4 = _rstd(d)
        dh = d * r4
        err = (h1v + dh * g4v) - t_ref[...]
        loss_ref[...] += _rows8(err * err)
        dy = err * (1.0 / D_MODEL)
        dg4_ref[...] += _rows8(dy * dh)
        ddb = _rms_bwd(dy, dh, r4, g4v).astype(BF16)
        dd_ref[...] = ddb
        df = jnp.zeros((tm, D_MODEL), F32)
        for r0, sz in chunks:
            dgg = _dot_nt(ddb, wd_v[r0:r0 + sz, :])
            s = s_sc[:, r0:r0 + sz]
            u = u_sc[:, r0:r0 + sz]
            sig = _sigmoid(s)
            dsc = (dgg * u * (sig * (1.0 + s * (1.0 - sig)))).astype(BF16)
            duc = (dgg * (s * sig)).astype(BF16)
            ds_ref[:, r0:r0 + sz] = dsc
            du_ref[:, r0:r0 + sz] = duc
            df = df + _dot(dsc, wg_v[r0:r0 + sz, :]) + _dot(duc, wu_v[r0:r0 + sz, :])
        dg3_ref[...] += _rows8(df * hh)
        dh1_ref[...] = dy + _rms_bwd(df, hh, r3, g3v)

    row = pl.BlockSpec((tm, D_MODEL), lambda i: (i, 0))
    ffrow = pl.BlockSpec((tm, D_FF), lambda i: (i, 0))
    acc = _full((8, D_MODEL))
    act_bf = jax.ShapeDtypeStruct((n_rows, D_MODEL), BF16)
    ff_bf = jax.ShapeDtypeStruct((n_rows, D_FF), BF16)
    acc_shape = jax.ShapeDtypeStruct((8, D_MODEL), F32)
    w_vmem = pltpu.VMEM((D_FF, D_MODEL), BF16)
    return pl.pallas_call(
        body, name="ffn_fwd_bwd", grid=(n_rows // tm,),
        out_shape=[jax.ShapeDtypeStruct((n_rows, D_MODEL), F32), act_bf, act_bf, ff_bf, ff_bf, ff_bf,
                   acc_shape, acc_shape, acc_shape],
        in_specs=[row, row, _full((1, D_MODEL)), _full((1, D_MODEL)), ANY, ANY, ANY],
        out_specs=[row, row, row, ffrow, ffrow, ffrow, acc, acc, acc],
        scratch_shapes=[w_vmem, w_vmem, w_vmem, pltpu.VMEM((tm, D_FF), F32), pltpu.VMEM((tm, D_FF), F32),
                        pltpu.SemaphoreType.DMA],
        compiler_params=_cparams(1),
    )(h1, target, g3, g4, wg_t, wu_t, wd)


def _ffn_weight_grads(f, ds, du, gg, dd):
    n_rows = f.shape[0]
    tk = min(TK_DW, n_rows)
    half = D_FF // 2

    def body(f_ref, ds_ref, du_ref, gg_ref, dd_ref, dwg_ref, dwu_ref, dwd_ref):
        @pl.when(pl.program_id(1) == 0)
        def _():
            dwg_ref[...] = jnp.zeros_like(dwg_ref)
            dwu_ref[...] = jnp.zeros_like(dwu_ref)
            dwd_ref[...] = jnp.zeros_like(dwd_ref)

        fv = f_ref[...]
        dwg_ref[...] += _dot_tn(ds_ref[...], fv)
        dwu_ref[...] += _dot_tn(du_ref[...], fv)
        dwd_ref[...] += _dot_tn(gg_ref[...], dd_ref[...])

    row = pl.BlockSpec((tk, D_MODEL), lambda c, k: (k, 0))
    ffrow = pl.BlockSpec((tk, half), lambda c, k: (k, c))
    out = pl.BlockSpec((half, D_MODEL), lambda c, k: (c, 0))
    return pl.pallas_call(
        body, name="ffn_weight_grads", grid=(2, n_rows // tk),
        out_shape=[jax.ShapeDtypeStruct((D_FF, D_MODEL), F32)] * 3,
        in_specs=[row, ffrow, ffrow, ffrow, row], out_specs=[out, out, out],
        compiler_params=_cparams(2),
    )(f, ds, du, gg, dd)


def _mixer_bwd(dh1, m3, z3, x3, zmeta, g1, g2, convw, poolw, pscale, win_all, wout, ffn_grads):
    n_seq, seq, _ = x3.shape
    tm = min(TM_MIX_BWD, seq)
    n_t = seq // tm
    hb = tm // HALO
    n_ex = len(ffn_grads)

    def body(dh1_ref, m_ref, z_ref, zh_ref, x_ref, zm_ref, g1_ref, g2_ref, cw_ref, pw_ref, ps_ref, win_hbm, wout_hbm,
             *rest):
        ex = _ExchangeHalves(rest[:n_ex], rest[n_ex + 11:2 * n_ex + 11], *rest[-2:])
        (dx_ref, dz_ref, a_ref, yc_ref, dm_ref, dg1_ref, dg2_ref, dsc_ref, dcw_ref, dpw_ref,
         dzm_ref) = rest[n_ex:n_ex + 11]
        win_v, wout_v, cvb, pb, dcb, dqb, mcb, mqb, sem = rest[2 * n_ex + 11:-2]
        s, i = pl.program_id(0), pl.program_id(1)
        tr = n_t - 1 - i

        @pl.when((s == 0) & (i == 0))
        def _():
            ex.start()
            _load_weights([(win_hbm, win_v), (wout_hbm, wout_v)], sem)
            for ref in (dg1_ref, dg2_ref, dsc_ref, dcw_ref, dpw_ref, dzm_ref):
                ref[...] = jnp.zeros_like(ref)

        @pl.when(i == 0)
        def _():
            dcb[tm:tm + HALO, :] = jnp.zeros((HALO, D_CONV), F32)
            dqb[tm:tm + HALO, :] = jnp.zeros((HALO, D_POOL), F32)

        @pl.when(i > 0)
        def _():
            dcb[tm:tm + HALO, :] = dcb[0:HALO, :]
            dqb[tm:tm + HALO, :] = dqb[0:HALO, :]

        g1v, g2v = g1_ref[...], g2_ref[...]
        dh1v = dh1_ref[0]
        mv = m_ref[0]
        r2 = _rstd(mv)
        mh = mv * r2
        dg2_ref[...] += _rows8(dh1v * mh)
        dmb = _rms_bwd(dh1v, mh, r2, g2v).astype(BF16)
        dm_ref[...] = dmb
        dyc = _dot_nt(dmb, wout_v[...])
        dyconv = dyc[:, 0:D_CONV]

        zb = z_ref[0, :, 0:IN_SHARD]
        zc = z_ref[0, :, IN_SHARD:2 * IN_SHARD]
        zv = z_ref[0, :, 2 * IN_SHARD:3 * IN_SHARD]
        zp = z_ref[0, :, 3 * IN_SHARD:4 * IN_SHARD]
        halo = jnp.where(tr == 0, zm_ref[...], zh_ref[0])
        cvb[0:HALO, :] = halo[:, IN_SHARD:2 * IN_SHARD] * halo[:, 2 * IN_SHARD:3 * IN_SHARD]
        pb[0:HALO, :] = halo[:, 3 * IN_SHARD:4 * IN_SHARD]
        cv0 = zc * zv
        cvb[HALO:HALO + tm, :] = cv0
        pb[HALO:HALO + tm, :] = zp
        cw = cw_ref[...]
        cv2 = cvb[HALO - 2:HALO - 2 + tm, :]
        cv1 = cvb[HALO - 1:HALO - 1 + tm, :]
        conv = cw[0:1] * cv2 + cw[1:2] * cv1 + cw[2:3] * cv0
        parts = [(zb * conv).astype(BF16)]
        for g in range(N_POOL_GROUPS):
            pooled = _pool_fwd(pb, g, tm).astype(BF16)
            mixed = _dot(pooled, pw_ref[g])
            sc = ps_ref[:, _gcols(g)]
            parts.append((mixed * sc).astype(BF16))
            dyp = dyc[:, D_CONV + g * POOL_GROUP:D_CONV + (g + 1) * POOL_GROUP]
            dsc_ref[:, _gcols(g)] += _rows8(dyp * mixed)
            dmix = (dyp * sc).astype(BF16)
            dpw_ref[g] += _dot_tn(pooled, dmix)
            dqb[0:tm, _gcols(g)] = _dot_nt(dmix, pw_ref[g])
        yc_ref[...] = jnp.concatenate(parts, axis=1)

        dconv = dyconv * zb
        dcb[0:tm, :] = dconv
        dcv = cw[2:3] * dconv + cw[1:2] * dcb[1:1 + tm, :] + cw[0:1] * dcb[2:2 + tm, :]
        dcw_ref[0:8, :] += _rows8(dconv * cv2)
        dcw_ref[8:16, :] += _rows8(dconv * cv1)
        dcw_ref[16:24, :] += _rows8(dconv * cv0)
        dzs = [(dyconv * conv).astype(BF16), (dcv * zv).astype(BF16), (dcv * zc).astype(BF16),
               jnp.concatenate([_pool_bwd(dqb, g, tm) for g in range(N_POOL_GROUPS)], axis=1).astype(BF16)]
        da = jnp.zeros((tm, D_MODEL), F32)
        for j in range(N_CHIPS):
            dz_ref[j] = dzs[j]
            da = da + _dot_nt(dzs[j], win_v[j])
        xt = x_ref[0]
        r1 = _rstd(xt)
        xh = xt * r1
        a_ref[...] = (xh * g1v).astype(BF16)
        dg1_ref[...] += _rows8(da * xh)
        dx_ref[0] = dh1v + _rms_bwd(da, xh, r1, g1v)

        @pl.when(tr == 0)
        def _():
            mcb[0:HALO, :] = jnp.zeros((HALO, D_CONV), F32)
            mqb[0:HALO, :] = jnp.zeros((HALO, D_POOL), F32)
            mcb[HALO:2 * HALO, :] = dcb[0:HALO, :]
            mqb[HALO:2 * HALO, :] = dqb[0:HALO, :]
            dcv_m = cw[1:2] * mcb[1:1 + HALO, :] + cw[0:1] * mcb[2:2 + HALO, :]
            dzm_ref[:, IN_SHARD:2 * IN_SHARD] += dcv_m * zm_ref[:, 2 * IN_SHARD:3 * IN_SHARD]
            dzm_ref[:, 2 * IN_SHARD:3 * IN_SHARD] += dcv_m * zm_ref[:, IN_SHARD:2 * IN_SHARD]
            dzm_ref[:, 3 * IN_SHARD:4 * IN_SHARD] += jnp.concatenate(
                [_pool_bwd(mqb, g, HALO) for g in range(N_POOL_GROUPS)], axis=1)

        @pl.when((s == n_seq - 1) & (i == n_t - 1))
        def _():
            ex.finish()

    row3 = lambda c: pl.BlockSpec((1, tm, c), lambda s, i: (s, n_t - 1 - i, 0))
    row2 = lambda c: pl.BlockSpec((tm, c), lambda s, i: (s * n_t + n_t - 1 - i, 0))
    halo_spec = pl.BlockSpec((1, HALO, D_IN_PROJ), lambda s, i: (s, jnp.maximum((n_t - 1 - i) * hb - 1, 0), 0))
    n_rows = n_seq * seq
    act_bf = jax.ShapeDtypeStruct((n_rows, D_MODEL), BF16)
    outs = pl.pallas_call(
        body, name="mixer_bwd", grid=(n_seq, n_t),
        out_shape=[jax.ShapeDtypeStruct((n_seq, seq, D_MODEL), F32),
                   jax.ShapeDtypeStruct((N_CHIPS, n_rows, IN_SHARD), BF16), act_bf, act_bf, act_bf,
                   jax.ShapeDtypeStruct((8, D_MODEL), F32), jax.ShapeDtypeStruct((8, D_MODEL), F32),
                   jax.ShapeDtypeStruct((8, D_POOL), F32), jax.ShapeDtypeStruct((24, D_CONV), F32),
                   jax.ShapeDtypeStruct((N_POOL_GROUPS, POOL_GROUP, POOL_GROUP), F32),
                   jax.ShapeDtypeStruct((N_META, D_IN_PROJ), F32)] + _ExchangeHalves.out_shape(ffn_grads),
        in_specs=[row3(D_MODEL), row3(D_MODEL), row3(D_IN_PROJ), halo_spec, row3(D_MODEL),
                  _full((N_META, D_IN_PROJ)), _full((1, D_MODEL)), _full((1, D_MODEL)), _full((3, D_CONV)),
                  _full((N_POOL_GROUPS, POOL_GROUP, POOL_GROUP)), _full((1, D_POOL)), ANY, ANY] + [ANY] * n_ex,
        out_specs=[row3(D_MODEL), pl.BlockSpec((N_CHIPS, tm, IN_SHARD), lambda s, i: (0, s * n_t + n_t - 1 - i, 0)),
                   row2(D_MODEL), row2(D_MODEL), row2(D_MODEL),
                   _full((8, D_MODEL)), _full((8, D_MODEL)), _full((8, D_POOL)), _full((24, D_CONV)),
                   _full((N_POOL_GROUPS, POOL_GROUP, POOL_GROUP)), _full((N_META, D_IN_PROJ))] + [ANY] * n_ex,
        scratch_shapes=[pltpu.VMEM((N_CHIPS, D_MODEL, IN_SHARD), BF16), pltpu.VMEM((D_MODEL, D_MODEL), BF16),
                        pltpu.VMEM((HALO + tm, D_CONV), F32), pltpu.VMEM((HALO + tm, D_POOL), F32),
                        pltpu.VMEM((tm + HALO, D_CONV), F32), pltpu.VMEM((tm + HALO, D_POOL), F32),
                        pltpu.VMEM((2 * HALO, D_CONV), F32), pltpu.VMEM((2 * HALO, D_POOL), F32),
                        pltpu.SemaphoreType.DMA] + _ExchangeHalves.scratch(n_ex),
        compiler_params=_cparams(2),
    )(dh1, m3, z3, z3, x3, zmeta, g1, g2, convw, poolw, pscale, win_all, wout, *ffn_grads)
    return outs[:11], outs[11:]


def _meta_bwd(dzm, meta_full, g1, win_all):
    def body(dzm_ref, meta_ref, g1_ref, win_ref, dmeta_ref, dg1_ref, a_ref, dzb_ref):
        xm = meta_ref[...]
        r = _rstd(xm)
        xh = xm * r
        g1v = g1_ref[...]
        a_ref[...] = (xh * g1v).astype(BF16)
        da = jnp.zeros((N_META, D_MODEL), F32)
        for j in range(N_CHIPS):
            dzj = dzm_ref[:, j * IN_SHARD:(j + 1) * IN_SHARD].astype(BF16)
            dzb_ref[j] = dzj
            da = da + _dot_nt(dzj, win_ref[j])
        dg1_ref[...] = _rows8(da * xh)
        dmeta_ref[...] = _rms_bwd(da, xh, r, g1v)

    vm = pl.BlockSpec(memory_space=pltpu.VMEM)
    return pl.pallas_call(
        body, name="meta_bwd",
        out_shape=[jax.ShapeDtypeStruct((N_META, D_MODEL), F32), jax.ShapeDtypeStruct((8, D_MODEL), F32),
                   jax.ShapeDtypeStruct((N_META, D_MODEL), BF16), jax.ShapeDtypeStruct((N_CHIPS, N_META, IN_SHARD), BF16)],
        in_specs=[vm] * 4, out_specs=[vm] * 4,
    )(dzm, meta_full, g1, win_all)


def _mixer_weight_grads(a, dz, ycat, dm, a_meta, dz_meta, ffn_sums):
    n_rows = a.shape[0]
    tk = min(TK_DW, n_rows)
    n_k = n_rows // tk
    n_sc = len(ffn_sums)

    def body(a_ref, dz_ref, yc_ref, dm_ref, am_ref, dzm_ref, *rest):
        sc = _ScatterToChips(rest[:n_sc], rest[n_sc + 2:2 * n_sc + 2], *rest[-2:])
        dwin_ref, dwout_ref = rest[n_sc:n_sc + 2]
        j, k = pl.program_id(0), pl.program_id(1)

        @pl.when((j == 0) & (k == 0))
        def _():
            sc.start()

        @pl.when(k == 0)
        def _():
            dwin_ref[0] = _dot_tn(am_ref[...], dzm_ref[0])
            dwout_ref[...] = jnp.zeros_like(dwout_ref)

        dwin_ref[0] += _dot_tn(a_ref[...], dz_ref[0])
        dwout_ref[0] += _dot_tn(yc_ref[...], dm_ref[...])

        @pl.when((j == N_CHIPS - 1) & (k == n_k - 1))
        def _():
            sc.finish()

    row = pl.BlockSpec((tk, D_MODEL), lambda j, k: (k, 0))
    outs = pl.pallas_call(
        body, name="mixer_weight_grads", grid=(N_CHIPS, n_k),
        out_shape=[jax.ShapeDtypeStruct((N_CHIPS, D_MODEL, IN_SHARD), F32),
                   jax.ShapeDtypeStruct((N_CHIPS, OUT_SHARD, D_MODEL), F32)] + _ScatterToChips.out_shape(ffn_sums),
        in_specs=[row, pl.BlockSpec((1, tk, IN_SHARD), lambda j, k: (j, k, 0)),
                  pl.BlockSpec((tk, OUT_SHARD), lambda j, k: (k, j)), row,
                  _full((N_META, D_MODEL)), pl.BlockSpec((1, N_META, IN_SHARD), lambda j, k: (j, 0, 0))]
        + [ANY] * n_sc,
        out_specs=[pl.BlockSpec((1, D_MODEL, IN_SHARD), lambda j, k: (j, 0, 0)),
                   pl.BlockSpec((1, OUT_SHARD, D_MODEL), lambda j, k: (j, 0, 0))] + [ANY] * n_sc,
        scratch_shapes=_ScatterToChips.scratch(n_sc),
        compiler_params=_cparams(2),
    )(a, dz, ycat, dm, a_meta, dz_meta, *ffn_sums)
    return outs[:2], outs[2:]


def kernel(x, meta_tokens, norm_mix_pre, w_in, conv_w, pool_w, pool_scale, w_out, norm_mix_post, norm_ffn_pre, w_gate, w_up, w_down, norm_ffn_post, loss_target, m_meta_tokens, m_norm_mix_pre, m_w_in, m_conv_w, m_pool_w, m_pool_scale, m_w_out, m_norm_mix_post, m_norm_ffn_pre, m_w_gate, m_w_up, m_w_down, m_norm_ffn_post, v_meta_tokens, v_norm_mix_pre, v_w_in, v_conv_w, v_pool_w, v_pool_scale, v_w_out, v_norm_mix_post, v_norm_ffn_pre, v_w_gate, v_w_up, v_w_down, v_norm_ffn_post):
    n_seq, seq, _ = x.shape
    n_rows = n_seq * seq
    chip = 2 * lax.axis_index("x") + lax.axis_index("y")
    meta_cols = D_MODEL // N_CHIPS
    conv_cols = D_CONV // N_CHIPS

    small = jnp.zeros((2 * HALO, meta_cols), F32)
    small = small.at[0:N_META, :].set(meta_tokens).at[N_META:N_META + 3, 0:conv_cols].set(conv_w[0])
    win_all, wout_all, small_all = _all_gather_shards([w_in[0].astype(BF16), w_out[0].astype(BF16), small])
    meta_full = small_all[:, 0:N_META, :].transpose(1, 0, 2).reshape(N_META, D_MODEL)
    conv_full = small_all[:, N_META:N_META + 3, 0:conv_cols].transpose(1, 0, 2).reshape(3, D_CONV)
    wout_full = wout_all.reshape(D_MODEL, D_MODEL)
    poolw_bf = pool_w[0].astype(BF16)
    pscale = pool_scale
    g1, g2, g3, g4 = norm_mix_pre, norm_mix_post, norm_ffn_pre, norm_ffn_post
    place = jnp.stack([chip, lax.axis_index("c")]).astype(jnp.int32)

    zmeta = _meta_fwd(meta_full, g1, win_all)
    (z3, m3, h1), ffn_w = _mixer_fwd(
        x, zmeta, g1, g2, conv_full, poolw_bf, pscale, win_all, wout_full,
        [w_gate[0].T.astype(BF16), w_up[0].T.astype(BF16), w_down[0].astype(BF16)])
    wg_t, wu_t, wd_full = [w.reshape(D_FF, D_MODEL) for w in ffn_w]
    dh1, f_bf, dd_bf, ds_bf, du_bf, gg_bf, lossp, dg3p, dg4p = _ffn_fwd_bwd(
        h1.reshape(n_rows, D_MODEL), loss_target.reshape(n_rows, D_MODEL), g3, g4, wg_t, wu_t, wd_full)
    ffn_grads = [g.reshape(N_CHIPS, FF_SHARD, D_MODEL) for g in _ffn_weight_grads(f_bf, ds_bf, du_bf, gg_bf, dd_bf)]
    (grad_x, dz_bf, a_bf, yc_bf, dm_bf, dg1p, dg2p, dscp, dcwp, dpw, dzm), ffn_recvs = _mixer_bwd(
        dh1.reshape(n_seq, seq, D_MODEL), m3, z3, x, zmeta, g1, g2, conv_full, poolw_bf, pscale, win_all, wout_full,
        ffn_grads)
    dmeta, dg1m, a_meta, dz_meta = _meta_bwd(dzm, meta_full, g1, win_all)
    mix_grads, ffn_rbufs = _mixer_weight_grads(
        a_bf, dz_bf, yc_bf, dm_bf, a_meta, dz_meta, [_add_pairs(g, r, place) for g, r in zip(ffn_grads, ffn_recvs)])

    mix_recvs = _exchange_halves(mix_grads)
    mix_rbufs = _scatter_to_chips([_add_pairs(g, r, place) for g, r in zip(mix_grads, mix_recvs)])
    grads = list(mix_grads) + list(ffn_grads)
    recvs = list(mix_recvs) + list(ffn_recvs)
    rbufs = list(mix_rbufs) + list(ffn_rbufs)
    reduced = _gather_halves([_add_chips(g, r, rb, place) for g, r, rb in zip(grads, recvs, rbufs)])
    g_win, g_wout, g_wg_t, g_wu_t, g_wd = [r.reshape(2 * r.shape[1], r.shape[2]) for r in reduced]

    a_red, b_red, c_red = _all_reduce_small(dg1p, dg1m, dg2p, dg3p, dg4p, lossp, dmeta, dscp, dcwp,
                                            dpw.reshape(SMALL_C_ROWS, POOL_GROUP))
    loss = a_red[4, 0]
    g_g1, g_g2, g_g3, g_g4 = a_red[0:1], a_red[1:2], a_red[2:3], a_red[3:4]
    g_meta = lax.dynamic_slice(a_red, (8, chip * meta_cols), (N_META, meta_cols))
    g_pscale = b_red[0:1]
    g_conv = lax.dynamic_slice(b_red, (1, chip * conv_cols), (3, conv_cols))
    g_poolw = c_red

    big = [(w_in[0], g_win, m_w_in[0], v_w_in[0]), (w_out[0], g_wout, m_w_out[0], v_w_out[0]),
           (w_gate[0].T, g_wg_t, m_w_gate[0].T, v_w_gate[0].T), (w_up[0].T, g_wu_t, m_w_up[0].T, v_w_up[0].T),
           (w_down[0], g_wd, m_w_down[0], v_w_down[0])]
    big_out = [_adamw_big(w, g, m, v) for (w, g, m, v) in big]
    big_out[2] = [o.T for o in big_out[2]]
    big_out[3] = [o.T for o in big_out[3]]
    g_wg, g_wu = g_wg_t.T, g_wu_t.T
    small_groups = [
        (meta_tokens, g_meta, m_meta_tokens, v_meta_tokens),
        (g1, g_g1, m_norm_mix_pre, v_norm_mix_pre),
        (conv_w[0], g_conv, m_conv_w[0], v_conv_w[0]),
        (pool_w.reshape(SMALL_C_ROWS, POOL_GROUP), g_poolw, m_pool_w.reshape(SMALL_C_ROWS, POOL_GROUP),
         v_pool_w.reshape(SMALL_C_ROWS, POOL_GROUP)),
        (pool_scale, g_pscale, m_pool_scale, v_pool_scale),
        (g2, g_g2, m_norm_mix_post, v_norm_mix_post),
        (g3, g_g3, m_norm_ffn_pre, v_norm_ffn_pre),
        (g4, g_g4, m_norm_ffn_post, v_norm_ffn_post),
    ]
    small_out = _adamw_small(small_groups)

    grads_out = [g_meta, g_g1, g_win[None], g_conv[None], g_poolw.reshape(pool_w.shape), g_pscale, g_wout[None],
                 g_g2, g_g3, g_wg[None], g_wu[None], g_wd[None], g_g4]
    s_meta, s_g1, s_conv, s_poolw, s_pscale, s_g2, s_g3, s_g4 = small_out
    b_win, b_wout, b_wg, b_wu, b_wd = big_out

    def leaf(k):
        return [s_meta[k], s_g1[k], b_win[k][None], s_conv[k][None], s_poolw[k].reshape(pool_w.shape), s_pscale[k],
                b_wout[k][None], s_g2[k], s_g3[k], b_wg[k][None], b_wu[k][None], b_wd[k][None], s_g4[k]]

    return (loss, grad_x, *grads_out, *leaf(0), *leaf(1), *leaf(2))
```

```python
import jax
import jax.numpy as jnp
from jax import lax
from jax.experimental import pallas as pl
from jax.experimental.pallas import tpu as pltpu

F32 = jnp.float32
BF16 = jnp.bfloat16
MESH = pl.DeviceIdType.MESH

D_MODEL = 1024
D_CONV = 512
D_POOL = 512
POOL_GROUP = 128
N_POOL_GROUPS = 4
D_IN_PROJ = 2048
D_FF = 2816
N_CHIPS = 4
FF_SHARD = D_FF // N_CHIPS
IN_SHARD = D_IN_PROJ // N_CHIPS
OUT_SHARD = D_MODEL // N_CHIPS
N_META = 16
HALO = 16
RMS_EPS = 1e-6

ADAM_LR = 0.001
ADAM_B1 = 0.9
ADAM_B2 = 0.999
ADAM_EPS = 1e-08
ADAM_WD = 0.01
ADAM_STEP = 10

TM_MIX_FWD = 512
TM_MIX_BWD = 256
TM_FFN = 256
TK_DW = 512
FF_CHUNK = 512
VMEM_LIMIT = 56 * 1024 * 1024


def _cparams(n_grid):
    return pltpu.CompilerParams(dimension_semantics=("arbitrary",) * n_grid, vmem_limit_bytes=VMEM_LIMIT)


def _dot(a, b):
    return jnp.dot(a, b, preferred_element_type=F32)


def _dot_nt(a, b):
    return lax.dot_general(a, b, (((1,), (1,)), ((), ())), preferred_element_type=F32)


def _dot_tn(a, b):
    return lax.dot_general(a, b, (((0,), (0,)), ((), ())), preferred_element_type=F32)


def _rows8(v):
    r, c = v.shape
    return v.reshape(r // 8, 8, c).sum(axis=0)


def _rstd(v):
    return lax.rsqrt(jnp.mean(v * v, axis=-1, keepdims=True) + RMS_EPS)


def _rms_bwd(dy, xhat, rstd, gain):
    dyg = dy * gain
    return rstd * (dyg - xhat * jnp.mean(dyg * xhat, axis=-1, keepdims=True))


def _sigmoid(v):
    return 1.0 / (1.0 + jnp.exp(-v))


def _gcols(g):
    return slice(g * POOL_GROUP, (g + 1) * POOL_GROUP)


def _window_sum(e, g, ahead):
    n = e.shape[0]
    w = e
    for level in range(g + 1):
        shift = 1 << level
        w = w + pltpu.roll(w, (n - shift) if ahead else shift, 0)
    return w


def _pool_fwd(pb, g, n):
    e = pb[0:HALO + n, _gcols(g)]
    return _window_sum(e, g, False)[HALO:, :] * (1.0 / (2 << g)) - e[HALO:, :]


def _pool_bwd(qb, g, n):
    e = qb[0:n + HALO, _gcols(g)]
    return _window_sum(e, g, True)[0:n, :] * (1.0 / (2 << g)) - e[0:n, :]


def _full(shape):
    nd = len(shape)
    return pl.BlockSpec(shape, lambda *_: (0,) * nd)


ANY = pl.BlockSpec(memory_space=pl.ANY)


def _mesh_pos():
    x, y, c = lax.axis_index("x"), lax.axis_index("y"), lax.axis_index("c")
    chips = [(1 - x, y), (x, 1 - y), (1 - x, 1 - y)]
    return x, y, c, chips


def _half(ref, h):
    hr = ref.shape[0] // 2
    return ref.at[pl.ds(h * hr, hr), :]


class _AllGather:
    def __init__(self, ins, outs, send_sems, recv_sems):
        self.ins, self.outs, self.send_sems, self.recv_sems = ins, outs, send_sems, recv_sems
        self.n = len(ins)

    @staticmethod
    def scratch(n):
        return [pltpu.SemaphoreType.DMA((6 * n,)), pltpu.SemaphoreType.DMA((6 * n,))]

    @staticmethod
    def out_shape(shards):
        return [jax.ShapeDtypeStruct((N_CHIPS,) + s.shape, s.dtype) for s in shards]

    def _copy(self, a, k, src, dst, to):
        return pltpu.make_async_remote_copy(src_ref=src, dst_ref=dst, send_sem=self.send_sems.at[6 * a + k],
                                            recv_sem=self.recv_sems.at[6 * a + k], device_id=to, device_id_type=MESH)

    def _ici(self, a, k):
        x, y, c, chips = _mesh_pos()
        return self._copy(a, k, _half(self.ins[a], c), _half(self.outs[a].at[2 * x + y], c), (*chips[k], c))

    def _d2d(self, a, k, h):
        x, y, c, chips = _mesh_pos()
        slot = _half(self.outs[a].at[2 * chips[k][0] + chips[k][1]], h)
        return self._copy(a, 3 + k, slot, slot, (x, y, 1 - c))

    def start(self):
        for a in range(self.n):
            for k in range(3):
                self._ici(a, k).start()

    def forward(self, a):
        c = lax.axis_index("c")
        for k in range(3):
            self._ici(a, k).wait_recv()
            self._d2d(a, k, c).start()

    def finish(self):
        c = lax.axis_index("c")
        for a in range(self.n):
            for k in range(3):
                self._d2d(a, k, 1 - c).wait_recv()
        for a in range(self.n):
            for k in range(3):
                self._ici(a, k).wait_send()
                self._d2d(a, k, c).wait_send()


def _fill_own_slot(gathered, shards):
    chip = 2 * lax.axis_index("x") + lax.axis_index("y")
    return [lax.dynamic_update_slice(o, s[None], (chip, 0, 0)) for o, s in zip(gathered, shards)]


def _all_gather_shards(shards):
    n = len(shards)

    def body(*refs):
        ag = _AllGather(refs[:n], refs[n:2 * n], *refs[2 * n:])
        ag.start()
        for a in range(n):
            ag.forward(a)
        ag.finish()

    outs = pl.pallas_call(
        body, name="all_gather_weights", out_shape=_AllGather.out_shape(shards),
        in_specs=[ANY] * n, out_specs=[ANY] * n, scratch_shapes=_AllGather.scratch(n),
    )(*shards)
    return _fill_own_slot(outs, shards)


class _ExchangeHalves:
    def __init__(self, ins, recvs, send_sems, recv_sems):
        self.ins, self.recvs, self.send_sems, self.recv_sems = ins, recvs, send_sems, recv_sems

    @staticmethod
    def scratch(n):
        return [pltpu.SemaphoreType.DMA((n,)), pltpu.SemaphoreType.DMA((n,))]

    @staticmethod
    def out_shape(grads):
        return [jax.ShapeDtypeStruct((g.shape[0], g.shape[1] // 2, g.shape[2]), g.dtype) for g in grads]

    def _copies(self):
        x, y, c, _ = _mesh_pos()
        out = []
        for a, (src, dst) in enumerate(zip(self.ins, self.recvs)):
            hr = src.shape[1] // 2
            out.append(pltpu.make_async_remote_copy(
                src_ref=src.at[:, pl.ds((1 - c) * hr, hr), :], dst_ref=dst, send_sem=self.send_sems.at[a],
                recv_sem=self.recv_sems.at[a], device_id=(x, y, 1 - c), device_id_type=MESH))
        return out

    def start(self):
        for cp in self._copies():
            cp.start()

    def finish(self):
        for cp in self._copies():
            cp.wait()


def _exchange_halves(grads):
    n = len(grads)

    def body(*refs):
        ex = _ExchangeHalves(refs[:n], refs[n:2 * n], *refs[2 * n:])
        ex.start()
        ex.finish()

    return pl.pallas_call(
        body, name="grad_exchange_halves", out_shape=_ExchangeHalves.out_shape(grads),
        in_specs=[ANY] * n, out_specs=[ANY] * n, scratch_shapes=_ExchangeHalves.scratch(n),
    )(*grads)


class _ScatterToChips:
    def __init__(self, ins, rbufs, send_sems, recv_sems):
        self.ins, self.rbufs, self.send_sems, self.recv_sems = ins, rbufs, send_sems, recv_sems

    @staticmethod
    def scratch(n):
        return [pltpu.SemaphoreType.DMA((3 * n,)), pltpu.SemaphoreType.DMA((3 * n,))]

    @staticmethod
    def out_shape(sums):
        return [jax.ShapeDtypeStruct((3,) + s.shape[1:], BF16) for s in sums]

    def _copies(self):
        x, y, c, chips = _mesh_pos()
        out = []
        for a, (src, dst) in enumerate(zip(self.ins, self.rbufs)):
            for k, chip in enumerate(chips):
                out.append(pltpu.make_async_remote_copy(
                    src_ref=src.at[2 * chip[0] + chip[1]], dst_ref=dst.at[k], send_sem=self.send_sems.at[3 * a + k],
                    recv_sem=self.recv_sems.at[3 * a + k], device_id=(*chip, c), device_id_type=MESH))
        return out

    def start(self):
        for cp in self._copies():
            cp.start()

    def finish(self):
        for cp in self._copies():
            cp.wait()


def _scatter_to_chips(sums_bf16):
    n = len(sums_bf16)

    def body(*refs):
        sc = _ScatterToChips(refs[:n], refs[n:2 * n], *refs[2 * n:])
        sc.start()
        sc.finish()

    return pl.pallas_call(
        body, name="grad_scatter_to_chips", out_shape=_ScatterToChips.out_shape(sums_bf16),
        in_specs=[ANY] * n, out_specs=[ANY] * n, scratch_shapes=_ScatterToChips.scratch(n),
    )(*sums_bf16)


def _gather_halves(halves):
    n = len(halves)

    def body(*refs):
        ins, outs = refs[:n], refs[n:2 * n]
        send_sems, recv_sems = refs[2 * n:]
        x, y, c, _ = _mesh_pos()
        sib = (x, y, 1 - c)
        remote = [pltpu.make_async_remote_copy(src_ref=ins[a].at[c], dst_ref=outs[a].at[c],
                                               send_sem=send_sems.at[a], recv_sem=recv_sems.at[a],
                                               device_id=sib, device_id_type=MESH) for a in range(n)]
        for cp in remote:
            cp.start()
        for a in range(n):
            pltpu.make_async_remote_copy(src_ref=ins[a].at[1 - c], dst_ref=outs[a].at[1 - c], send_sem=send_sems.at[a],
                                         recv_sem=recv_sems.at[a], device_id=sib, device_id_type=MESH).wait_recv()
        for cp in remote:
            cp.wait_send()

    return pl.pallas_call(
        body, name="grad_gather_halves",
        out_shape=[jax.ShapeDtypeStruct(h.shape, F32) for h in halves],
        in_specs=[ANY] * n, out_specs=[ANY] * n, input_output_aliases={a: a for a in range(n)},
        scratch_shapes=[pltpu.SemaphoreType.DMA((n,)), pltpu.SemaphoreType.DMA((n,))],
    )(*halves)


SMALL_A_ROWS = 24
SMALL_B_ROWS = 8
SMALL_C_ROWS = N_POOL_GROUPS * POOL_GROUP


def _all_reduce_small(dg1, dg1m, dg2, dg3, dg4, lossp, dmeta, dscale, dconv, dpoolw):
    def body(dg1_ref, dg1m_ref, dg2_ref, dg3_ref, dg4_ref, loss_ref, dmeta_ref, dsc_ref, dcw_ref, dpw_ref,
             a_out, b_out, c_out, a_buf, b_buf, c_buf, a_rcv, b_rcv, c_rcv, send_sems, recv_sems):
        x, y, c, _ = _mesh_pos()
        peers = [(x, y, 1 - c), (1 - x, y, c), (x, 1 - y, c)]

        def rowsum(v):
            return jnp.sum(v, axis=0, keepdims=True)

        a_buf[0, 0:1, :] = rowsum(dg1_ref[...] + dg1m_ref[...])
        a_buf[0, 1:2, :] = rowsum(dg2_ref[...])
        a_buf[0, 2:3, :] = rowsum(dg3_ref[...])
        a_buf[0, 3:4, :] = rowsum(dg4_ref[...])
        loss = jnp.sum(rowsum(loss_ref[...]), axis=1, keepdims=True) * (0.5 / D_MODEL)
        a_buf[0, 4:5, :] = jnp.broadcast_to(loss, (1, D_MODEL))
        a_buf[0, 5:8, :] = jnp.zeros((3, D_MODEL), F32)
        a_buf[0, 8:24, :] = dmeta_ref[...]
        b_buf[0, 0:1, :] = rowsum(dsc_ref[...])
        for k in range(3):
            b_buf[0, 1 + k:2 + k, :] = rowsum(dcw_ref[8 * k:8 * k + 8, :])
        b_buf[0, 4:8, :] = jnp.zeros((4, D_CONV), F32)
        c_buf[0] = dpw_ref[...]

        for st, peer in enumerate(peers):
            cps = []
            for i, (buf, rcv) in enumerate(((a_buf, a_rcv), (b_buf, b_rcv), (c_buf, c_rcv))):
                cps.append(pltpu.make_async_remote_copy(
                    src_ref=buf.at[st], dst_ref=rcv.at[st], send_sem=send_sems.at[3 * st + i],
                    recv_sem=recv_sems.at[3 * st + i], device_id=peer, device_id_type=MESH))
            for cp in cps:
                cp.start()
            for cp in cps:
                cp.wait()
            if st < 2:
                a_buf[st + 1] = a_buf[st] + a_rcv[st]
                b_buf[st + 1] = b_buf[st] + b_rcv[st]
                c_buf[st + 1] = c_buf[st] + c_rcv[st]
            else:
                a_out[...] = a_buf[st] + a_rcv[st]
                b_out[...] = b_buf[st] + b_rcv[st]
                c_out[...] = c_buf[st] + c_rcv[st]

    vm = pl.BlockSpec(memory_space=pltpu.VMEM)
    shapes = [(SMALL_A_ROWS, D_MODEL), (SMALL_B_ROWS, D_CONV), (SMALL_C_ROWS, POOL_GROUP)]
    return pl.pallas_call(
        body, name="all_reduce_small",
        out_shape=[jax.ShapeDtypeStruct(s, F32) for s in shapes],
        in_specs=[vm] * 10, out_specs=[vm] * 3,
        scratch_shapes=[pltpu.VMEM((3,) + s, F32) for s in shapes] + [pltpu.VMEM((3,) + s, F32) for s in shapes]
        + [pltpu.SemaphoreType.DMA((9,)), pltpu.SemaphoreType.DMA((9,))],
    )(dg1, dg1m, dg2, dg3, dg4, lossp, dmeta, dscale, dconv, dpoolw)


def _row_block(rows):
    for cand in (512, 448, 384, 352, 320, 256, 128, 64, 32, 16):
        if rows % cand == 0:
            return cand
    return rows


def _add_pairs(grad, recv, place):
    n_sh, rows2, cols = grad.shape
    hr = rows2 // 2
    br = _row_block(hr)

    def body(place_ref, a_ref, b_ref, o_ref):
        o_ref[...] = (a_ref[0] + b_ref[...]).astype(BF16)

    return pl.pallas_call(
        body, name="grad_add_pairs",
        grid_spec=pltpu.PrefetchScalarGridSpec(
            num_scalar_prefetch=1, grid=(n_sh, hr // br),
            in_specs=[pl.BlockSpec((1, 1, br, cols), lambda j, i, p: (j, p[1], i, 0)),
                      pl.BlockSpec((1, br, cols), lambda j, i, p: (j, i, 0))],
            out_specs=pl.BlockSpec((1, br, cols), lambda j, i, p: (j, i, 0))),
        out_shape=jax.ShapeDtypeStruct((n_sh, hr, cols), BF16), compiler_params=_cparams(2),
    )(place, grad.reshape(n_sh, 2, hr, cols), recv)


def _add_chips(grad, recv, rbuf, place):
    n_sh, rows2, cols = grad.shape
    hr = rows2 // 2
    br = _row_block(hr)

    def body(place_ref, a_ref, b_ref, r_ref, o_ref):
        own = a_ref[0, 0] + b_ref[0]
        o_ref[0] = ((own + r_ref[0].astype(F32)) + r_ref[1].astype(F32)) + r_ref[2].astype(F32)

    return pl.pallas_call(
        body, name="grad_add_chips",
        grid_spec=pltpu.PrefetchScalarGridSpec(
            num_scalar_prefetch=1, grid=(hr // br,),
            in_specs=[pl.BlockSpec((1, 1, br, cols), lambda i, p: (p[0], p[1], i, 0)),
                      pl.BlockSpec((1, br, cols), lambda i, p: (p[0], i, 0)),
                      pl.BlockSpec((3, br, cols), lambda i, p: (0, i, 0))],
            out_specs=pl.BlockSpec((1, br, cols), lambda i, p: (p[1], i, 0))),
        out_shape=jax.ShapeDtypeStruct((2, hr, cols), F32), compiler_params=_cparams(1),
    )(place, grad.reshape(n_sh, 2, hr, cols), recv, rbuf)


def _adamw_math(w, g, m, v):
    m2 = ADAM_B1 * m + (1.0 - ADAM_B1) * g
    v2 = ADAM_B2 * v + (1.0 - ADAM_B2) * (g * g)
    m_hat = m2 / (1.0 - ADAM_B1 ** ADAM_STEP)
    v_hat = v2 / (1.0 - ADAM_B2 ** ADAM_STEP)
    delta = -ADAM_LR * (m_hat / (jnp.sqrt(v_hat) + ADAM_EPS) + ADAM_WD * w)
    return delta, m2, v2


def _adamw_big(w, g, m, v):
    rows, cols = w.shape
    br = _row_block(rows)

    def body(w_ref, g_ref, m_ref, v_ref, d_ref, m2_ref, v2_ref):
        d, m2, v2 = _adamw_math(w_ref[...], g_ref[...], m_ref[...], v_ref[...])
        d_ref[...] = d
        m2_ref[...] = m2
        v2_ref[...] = v2

    spec = pl.BlockSpec((br, cols), lambda i: (i, 0))
    return pl.pallas_call(
        body, name="adamw_big", grid=(rows // br,),
        out_shape=[jax.ShapeDtypeStruct((rows, cols), F32)] * 3,
        in_specs=[spec] * 4, out_specs=[spec] * 3, compiler_params=_cparams(1),
    )(w, g, m, v)


def _adamw_small(groups):
    n = len(groups)

    def body(*refs):
        ins, outs = refs[:4 * n], refs[4 * n:]
        for i in range(n):
            w, g, m, v = (r[...] for r in ins[4 * i:4 * i + 4])
            d, m2, v2 = _adamw_math(w, g, m, v)
            outs[3 * i][...] = d
            outs[3 * i + 1][...] = m2
            outs[3 * i + 2][...] = v2

    vm = pl.BlockSpec(memory_space=pltpu.VMEM)
    flat = [a for grp in groups for a in grp]
    out_shape = [jax.ShapeDtypeStruct(grp[0].shape, F32) for grp in groups for _ in range(3)]
    outs = pl.pallas_call(body, name="adamw_small", out_shape=out_shape,
                          in_specs=[vm] * (4 * n), out_specs=[vm] * (3 * n))(*flat)
    return [tuple(outs[3 * i:3 * i + 3]) for i in range(n)]


def _load_weights(pairs, sem):
    for src, dst in pairs:
        cp = pltpu.make_async_copy(src, dst, sem)
        cp.start()
        cp.wait()


def _meta_fwd(meta_full, g1, win_all):
    def body(meta_ref, g1_ref, win_ref, z_ref):
        xm = meta_ref[...]
        a = (xm * _rstd(xm) * g1_ref[...]).astype(BF16)
        for j in range(N_CHIPS):
            z_ref[:, j * IN_SHARD:(j + 1) * IN_SHARD] = _dot(a, win_ref[j])

    vm = pl.BlockSpec(memory_space=pltpu.VMEM)
    return pl.pallas_call(body, name="meta_fwd", out_shape=jax.ShapeDtypeStruct((N_META, D_IN_PROJ), F32),
                          in_specs=[vm] * 3, out_specs=vm)(meta_full, g1, win_all)


def _mixer_fwd(x3, zmeta, g1, g2, convw, poolw, pscale, win_all, wout, ffn_shards):
    n_seq, seq, _ = x3.shape
    tm = min(TM_MIX_FWD, seq)
    n_t = seq // tm
    n_steps = n_seq * n_t
    n_ag = len(ffn_shards)

    def body(x_ref, zm_ref, g1_ref, g2_ref, cw_ref, pw_ref, ps_ref, win_hbm, wout_hbm, *rest):
        ag = _AllGather(rest[:n_ag], rest[n_ag + 3:2 * n_ag + 3], *rest[-2:])
        z_ref, m_ref, h1_ref = rest[n_ag:n_ag + 3]
        win_v, wout_v, cvb, pb, sem = rest[2 * n_ag + 3:-2]
        s, t = pl.program_id(0), pl.program_id(1)
        step = s * n_t + t

        @pl.when(step == 0)
        def _():
            ag.start()
            _load_weights([(win_hbm, win_v), (wout_hbm, wout_v)], sem)

        for a in range(n_ag):
            @pl.when(step == min(((a + 1) * n_steps) // n_ag, n_steps - 1))
            def _():
                ag.forward(a)

        xt = x_ref[0]
        a = (xt * _rstd(xt) * g1_ref[...]).astype(BF16)
        zb = _dot(a, win_v[0])
        zc = _dot(a, win_v[1])
        zv = _dot(a, win_v[2])
        zp = _dot(a, win_v[3])
        z_ref[0, :, 0:IN_SHARD] = zb
        z_ref[0, :, IN_SHARD:2 * IN_SHARD] = zc
        z_ref[0, :, 2 * IN_SHARD:3 * IN_SHARD] = zv
        z_ref[0, :, 3 * IN_SHARD:4 * IN_SHARD] = zp

        @pl.when(t == 0)
        def _():
            cvb[0:HALO, :] = zm_ref[:, IN_SHARD:2 * IN_SHARD] * zm_ref[:, 2 * IN_SHARD:3 * IN_SHARD]
            pb[0:HALO, :] = zm_ref[:, 3 * IN_SHARD:4 * IN_SHARD]

        @pl.when(t > 0)
        def _():
            cvb[0:HALO, :] = cvb[tm:tm + HALO, :]
            pb[0:HALO, :] = pb[tm:tm + HALO, :]

        cv = zc * zv
        cvb[HALO:HALO + tm, :] = cv
        pb[HALO:HALO + tm, :] = zp
        cw = cw_ref[...]
        conv = cw[0:1] * cvb[HALO - 2:HALO - 2 + tm, :] + cw[1:2] * cvb[HALO - 1:HALO - 1 + tm, :] + cw[2:3] * cv
        parts = [(zb * conv).astype(BF16)]
        for g in range(N_POOL_GROUPS):
            pooled = _pool_fwd(pb, g, tm).astype(BF16)
            parts.append((_dot(pooled, pw_ref[g]) * ps_ref[:, _gcols(g)]).astype(BF16))
        m = _dot(jnp.concatenate(parts, axis=1), wout_v[...])
        m_ref[0] = m
        h1_ref[0] = xt + m * _rstd(m) * g2_ref[...]

        @pl.when(step == n_steps - 1)
        def _():
            ag.finish()

    row = lambda c: pl.BlockSpec((1, tm, c), lambda s, t: (s, t, 0))
    outs = pl.pallas_call(
        body, name="mixer_fwd", grid=(n_seq, n_t),
        out_shape=[jax.ShapeDtypeStruct((n_seq, seq, D_IN_PROJ), F32), jax.ShapeDtypeStruct((n_seq, seq, D_MODEL), F32),
                   jax.ShapeDtypeStruct((n_seq, seq, D_MODEL), F32)] + _AllGather.out_shape(ffn_shards),
        in_specs=[row(D_MODEL), _full((N_META, D_IN_PROJ)), _full((1, D_MODEL)), _full((1, D_MODEL)),
                  _full((3, D_CONV)), _full((N_POOL_GROUPS, POOL_GROUP, POOL_GROUP)), _full((1, D_POOL)), ANY, ANY]
        + [ANY] * n_ag,
        out_specs=[row(D_IN_PROJ), row(D_MODEL), row(D_MODEL)] + [ANY] * n_ag,
        scratch_shapes=[pltpu.VMEM((N_CHIPS, D_MODEL, IN_SHARD), BF16), pltpu.VMEM((D_MODEL, D_MODEL), BF16),
                        pltpu.VMEM((HALO + tm, D_CONV), F32), pltpu.VMEM((HALO + tm, D_POOL), F32),
                        pltpu.SemaphoreType.DMA] + _AllGather.scratch(n_ag),
        compiler_params=_cparams(2),
    )(x3, zmeta, g1, g2, convw, poolw, pscale, win_all, wout, *ffn_shards)
    return outs[:3], _fill_own_slot(outs[3:], ffn_shards)


def _ffn_chunks():
    out, r0 = [], 0
    while r0 < D_FF:
        out.append((r0, min(FF_CHUNK, D_FF - r0)))
        r0 += FF_CHUNK
    return out


def _ffn_fwd_bwd(h1, target, g3, g4, wg_t, wu_t, wd):
    n_rows = h1.shape[0]
    tm = min(TM_FFN, n_rows)
    chunks = _ffn_chunks()

    def body(h1_ref, t_ref, g3_ref, g4_ref, wg_hbm, wu_hbm, wd_hbm,
             dh1_ref, f_ref, dd_ref, ds_ref, du_ref, gg_ref, loss_ref, dg3_ref, dg4_ref,
             wg_v, wu_v, wd_v, s_sc, u_sc, sem):
        @pl.when(pl.program_id(0) == 0)
        def _():
            _load_weights([(wg_hbm, wg_v), (wu_hbm, wu_v), (wd_hbm, wd_v)], sem)
            loss_ref[...] = jnp.zeros_like(loss_ref)
            dg3_ref[...] = jnp.zeros_like(dg3_ref)
            dg4_ref[...] = jnp.zeros_like(dg4_ref)

        h1v = h1_ref[...]
        r3 = _rstd(h1v)
        hh = h1v * r3
        g3v, g4v = g3_ref[...], g4_ref[...]
        f = (hh * g3v).astype(BF16)
        f_ref[...] = f
        d = jnp.zeros((tm, D_MODEL), F32)
        for r0, sz in chunks:
            s = _dot_nt(f, wg_v[r0:r0 + sz, :])
            u = _dot_nt(f, wu_v[r0:r0 + sz, :])
            s_sc[:, r0:r0 + sz] = s
            u_sc[:, r0:r0 + sz] = u
            gc = (s * _sigmoid(s) * u).astype(BF16)
            gg_ref[:, r0:r0 + sz] = gc
            d = d + _dot(gc, wd_v[r0:r0 + sz, :])
        r4 = _rstd(d)
        dh = d * r4
        err = (h1v + dh * g4v) - t_ref[...]
        loss_ref[...] += _rows8(err * err)
        dy = err * (1.0 / D_MODEL)
        dg4_ref[...] += _rows8(dy * dh)
        ddb = _rms_bwd(dy, dh, r4, g4v).astype(BF16)
        dd_ref[...] = ddb
        df = jnp.zeros((tm, D_MODEL), F32)
        for r0, sz in chunks:
            dgg = _dot_nt(ddb, wd_v[r0:r0 + sz, :])
            s = s_sc[:, r0:r0 + sz]
            u = u_sc[:, r0:r0 + sz]
            sig = _sigmoid(s)
            dsc = (dgg * u * (sig * (1.0 + s * (1.0 - sig)))).astype(BF16)
            duc = (dgg * (s * sig)).astype(BF16)
            ds_ref[:, r0:r0 + sz] = dsc
            du_ref[:, r0:r0 + sz] = duc
            df = df + _dot(dsc, wg_v[r0:r0 + sz, :]) + _dot(duc, wu_v[r0:r0 + sz, :])
        dg3_ref[...] += _rows8(df * hh)
        dh1_ref[...] = dy + _rms_bwd(df, hh, r3, g3v)

    row = pl.BlockSpec((tm, D_MODEL), lambda i: (i, 0))
    ffrow = pl.BlockSpec((tm, D_FF), lambda i: (i, 0))
    acc = _full((8, D_MODEL))
    act_bf = jax.ShapeDtypeStruct((n_rows, D_MODEL), BF16)
    ff_bf = jax.ShapeDtypeStruct((n_rows, D_FF), BF16)
    acc_shape = jax.ShapeDtypeStruct((8, D_MODEL), F32)
    w_vmem = pltpu.VMEM((D_FF, D_MODEL), BF16)
    return pl.pallas_call(
        body, name="ffn_fwd_bwd", grid=(n_rows // tm,),
        out_shape=[jax.ShapeDtypeStruct((n_rows, D_MODEL), F32), act_bf, act_bf, ff_bf, ff_bf, ff_bf,
                   acc_shape, acc_shape, acc_shape],
        in_specs=[row, row, _full((1, D_MODEL)), _full((1, D_MODEL)), ANY, ANY, ANY],
        out_specs=[row, row, row, ffrow, ffrow, ffrow, acc, acc, acc],
        scratch_shapes=[w_vmem, w_vmem, w_vmem, pltpu.VMEM((tm, D_FF), F32), pltpu.VMEM((tm, D_FF), F32),
                        pltpu.SemaphoreType.DMA],
        compiler_params=_cparams(1),
    )(h1, target, g3, g4, wg_t, wu_t, wd)


def _ffn_weight_grads(name, acts, other, exchanged):
    n_rows = other.shape[0]
    tk = min(TK_DW, n_rows)
    n_k = n_rows // tk
    half = D_FF // 2
    n_a, n_ex = len(acts), len(exchanged)

    def body(other_ref, *rest):
        act_refs = rest[:n_a]
        out_refs = rest[n_a + n_ex:2 * n_a + n_ex]
        c, k = pl.program_id(0), pl.program_id(1)
        if n_ex:
            ex = _ExchangeHalves(rest[n_a:n_a + n_ex], rest[2 * n_a + n_ex:2 * n_a + 2 * n_ex], *rest[-2:])

            @pl.when((c == 0) & (k == 0))
            def _():
                ex.start()

        @pl.when(k == 0)
        def _():
            for o in out_refs:
                o[...] = jnp.zeros_like(o)

        ov = other_ref[...]
        for a, o in zip(act_refs, out_refs):
            o[...] += _dot_tn(a[...], ov)

        if n_ex:
            @pl.when((c == 1) & (k == n_k - 1))
            def _():
                ex.finish()

    row = pl.BlockSpec((tk, D_MODEL), lambda c, k: (k, 0))
    ffrow = pl.BlockSpec((tk, half), lambda c, k: (k, c))
    out = pl.BlockSpec((half, D_MODEL), lambda c, k: (c, 0))
    outs = pl.pallas_call(
        body, name=name, grid=(2, n_k),
        out_shape=[jax.ShapeDtypeStruct((D_FF, D_MODEL), F32)] * n_a + _ExchangeHalves.out_shape(exchanged),
        in_specs=[row] + [ffrow] * n_a + [ANY] * n_ex, out_specs=[out] * n_a + [ANY] * n_ex,
        scratch_shapes=_ExchangeHalves.scratch(n_ex) if n_ex else [],
        compiler_params=_cparams(2),
    )(other, *acts, *exchanged)
    return outs[:n_a], outs[n_a:]


def _mixer_bwd(dh1, m3, z3, x3, zmeta, g1, g2, convw, poolw, pscale, win_all, wout, exchanged, scattered):
    n_seq, seq, _ = x3.shape
    tm = min(TM_MIX_BWD, seq)
    n_t = seq // tm
    hb = tm // HALO
    n_ex, n_sc = len(exchanged), len(scattered)
    n_cm = n_ex + n_sc

    def body(dh1_ref, m_ref, z_ref, zh_ref, x_ref, zm_ref, g1_ref, g2_ref, cw_ref, pw_ref, ps_ref, win_hbm, wout_hbm,
             *rest):
        outs0 = n_cm + 11
        ex = _ExchangeHalves(rest[:n_ex], rest[outs0:outs0 + n_ex], *rest[-4:-2])
        sc = _ScatterToChips(rest[n_ex:n_cm], rest[outs0 + n_ex:outs0 + n_cm], *rest[-2:])
        (dx_ref, dz_ref, a_ref, yc_ref, dm_ref, dg1_ref, dg2_ref, dsc_ref, dcw_ref, dpw_ref,
         dzm_ref) = rest[n_cm:outs0]
        win_v, wout_v, cvb, pb, dcb, dqb, mcb, mqb, sem = rest[outs0 + n_cm:-4]
        s, i = pl.program_id(0), pl.program_id(1)
        tr = n_t - 1 - i

        @pl.when((s == 0) & (i == 0))
        def _():
            sc.start()
            ex.start()
            _load_weights([(win_hbm, win_v), (wout_hbm, wout_v)], sem)
            for ref in (dg1_ref, dg2_ref, dsc_ref, dcw_ref, dpw_ref, dzm_ref):
                ref[...] = jnp.zeros_like(ref)

        @pl.when(i == 0)
        def _():
            dcb[tm:tm + HALO, :] = jnp.zeros((HALO, D_CONV), F32)
            dqb[tm:tm + HALO, :] = jnp.zeros((HALO, D_POOL), F32)

        @pl.when(i > 0)
        def _():
            dcb[tm:tm + HALO, :] = dcb[0:HALO, :]
            dqb[tm:tm + HALO, :] = dqb[0:HALO, :]

        g1v, g2v = g1_ref[...], g2_ref[...]
        dh1v = dh1_ref[0]
        mv = m_ref[0]
        r2 = _rstd(mv)
        mh = mv * r2
        dg2_ref[...] += _rows8(dh1v * mh)
        dmb = _rms_bwd(dh1v, mh, r2, g2v).astype(BF16)
        dm_ref[...] = dmb
        dyc = _dot_nt(dmb, wout_v[...])
        dyconv = dyc[:, 0:D_CONV]

        zb = z_ref[0, :, 0:IN_SHARD]
        zc = z_ref[0, :, IN_SHARD:2 * IN_SHARD]
        zv = z_ref[0, :, 2 * IN_SHARD:3 * IN_SHARD]
        zp = z_ref[0, :, 3 * IN_SHARD:4 * IN_SHARD]
        halo = jnp.where(tr == 0, zm_ref[...], zh_ref[0])
        cvb[0:HALO, :] = halo[:, IN_SHARD:2 * IN_SHARD] * halo[:, 2 * IN_SHARD:3 * IN_SHARD]
        pb[0:HALO, :] = halo[:, 3 * IN_SHARD:4 * IN_SHARD]
        cv0 = zc * zv
        cvb[HALO:HALO + tm, :] = cv0
        pb[HALO:HALO + tm, :] = zp
        cw = cw_ref[...]
        cv2 = cvb[HALO - 2:HALO - 2 + tm, :]
        cv1 = cvb[HALO - 1:HALO - 1 + tm, :]
        conv = cw[0:1] * cv2 + cw[1:2] * cv1 + cw[2:3] * cv0
        parts = [(zb * conv).astype(BF16)]
        for g in range(N_POOL_GROUPS):
            pooled = _pool_fwd(pb, g, tm).astype(BF16)
            mixed = _dot(pooled, pw_ref[g])
            scale = ps_ref[:, _gcols(g)]
            parts.append((mixed * scale).astype(BF16))
            dyp = dyc[:, D_CONV + g * POOL_GROUP:D_CONV + (g + 1) * POOL_GROUP]
            dsc_ref[:, _gcols(g)] += _rows8(dyp * mixed)
            dmix = (dyp * scale).astype(BF16)
            dpw_ref[g] += _dot_tn(pooled, dmix)
            dqb[0:tm, _gcols(g)] = _dot_nt(dmix, pw_ref[g])
        yc_ref[...] = jnp.concatenate(parts, axis=1)

        dconv = dyconv * zb
        dcb[0:tm, :] = dconv
        dcv = cw[2:3] * dconv + cw[1:2] * dcb[1:1 + tm, :] + cw[0:1] * dcb[2:2 + tm, :]
        dcw_ref[0:8, :] += _rows8(dconv * cv2)
        dcw_ref[8:16, :] += _rows8(dconv * cv1)
        dcw_ref[16:24, :] += _rows8(dconv * cv0)
        dzs = [(dyconv * conv).astype(BF16), (dcv * zv).astype(BF16), (dcv * zc).astype(BF16),
               jnp.concatenate([_pool_bwd(dqb, g, tm) for g in range(N_POOL_GROUPS)], axis=1).astype(BF16)]
        da = jnp.zeros((tm, D_MODEL), F32)
        for j in range(N_CHIPS):
            dz_ref[j] = dzs[j]
            da = da + _dot_nt(dzs[j], win_v[j])
        xt = x_ref[0]
        r1 = _rstd(xt)
        xh = xt * r1
        a_ref[...] = (xh * g1v).astype(BF16)
        dg1_ref[...] += _rows8(da * xh)
        dx_ref[0] = dh1v + _rms_bwd(da, xh, r1, g1v)

        @pl.when(tr == 0)
        def _():
            mcb[0:HALO, :] = jnp.zeros((HALO, D_CONV), F32)
            mqb[0:HALO, :] = jnp.zeros((HALO, D_POOL), F32)
            mcb[HALO:2 * HALO, :] = dcb[0:HALO, :]
            mqb[HALO:2 * HALO, :] = dqb[0:HALO, :]
            dcv_m = cw[1:2] * mcb[1:1 + HALO, :] + cw[0:1] * mcb[2:2 + HALO, :]
            dzm_ref[:, IN_SHARD:2 * IN_SHARD] += dcv_m * zm_ref[:, 2 * IN_SHARD:3 * IN_SHARD]
            dzm_ref[:, 2 * IN_SHARD:3 * IN_SHARD] += dcv_m * zm_ref[:, IN_SHARD:2 * IN_SHARD]
            dzm_ref[:, 3 * IN_SHARD:4 * IN_SHARD] += jnp.concatenate(
                [_pool_bwd(mqb, g, HALO) for g in range(N_POOL_GROUPS)], axis=1)

        @pl.when((s == n_seq - 1) & (i == n_t - 1))
        def _():
            ex.finish()
            sc.finish()

    row3 = lambda c: pl.BlockSpec((1, tm, c), lambda s, i: (s, n_t - 1 - i, 0))
    row2 = lambda c: pl.BlockSpec((tm, c), lambda s, i: (s * n_t + n_t - 1 - i, 0))
    halo_spec = pl.BlockSpec((1, HALO, D_IN_PROJ), lambda s, i: (s, jnp.maximum((n_t - 1 - i) * hb - 1, 0), 0))
    n_rows = n_seq * seq
    act_bf = jax.ShapeDtypeStruct((n_rows, D_MODEL), BF16)
    outs = pl.pallas_call(
        body, name="mixer_bwd", grid=(n_seq, n_t),
        out_shape=[jax.ShapeDtypeStruct((n_seq, seq, D_MODEL), F32),
                   jax.ShapeDtypeStruct((N_CHIPS, n_rows, IN_SHARD), BF16), act_bf, act_bf, act_bf,
                   jax.ShapeDtypeStruct((8, D_MODEL), F32), jax.ShapeDtypeStruct((8, D_MODEL), F32),
                   jax.ShapeDtypeStruct((8, D_POOL), F32), jax.ShapeDtypeStruct((24, D_CONV), F32),
                   jax.ShapeDtypeStruct((N_POOL_GROUPS, POOL_GROUP, POOL_GROUP), F32),
                   jax.ShapeDtypeStruct((N_META, D_IN_PROJ), F32)]
        + _ExchangeHalves.out_shape(exchanged) + _ScatterToChips.out_shape(scattered),
        in_specs=[row3(D_MODEL), row3(D_MODEL), row3(D_IN_PROJ), halo_spec, row3(D_MODEL),
                  _full((N_META, D_IN_PROJ)), _full((1, D_MODEL)), _full((1, D_MODEL)), _full((3, D_CONV)),
                  _full((N_POOL_GROUPS, POOL_GROUP, POOL_GROUP)), _full((1, D_POOL)), ANY, ANY] + [ANY] * n_cm,
        out_specs=[row3(D_MODEL), pl.BlockSpec((N_CHIPS, tm, IN_SHARD), lambda s, i: (0, s * n_t + n_t - 1 - i, 0)),
                   row2(D_MODEL), row2(D_MODEL), row2(D_MODEL),
                   _full((8, D_MODEL)), _full((8, D_MODEL)), _full((8, D_POOL)), _full((24, D_CONV)),
                   _full((N_POOL_GROUPS, POOL_GROUP, POOL_GROUP)), _full((N_META, D_IN_PROJ))] + [ANY] * n_cm,
        scratch_shapes=[pltpu.VMEM((N_CHIPS, D_MODEL, IN_SHARD), BF16), pltpu.VMEM((D_MODEL, D_MODEL), BF16),
                        pltpu.VMEM((HALO + tm, D_CONV), F32), pltpu.VMEM((HALO + tm, D_POOL), F32),
                        pltpu.VMEM((tm + HALO, D_CONV), F32), pltpu.VMEM((tm + HALO, D_POOL), F32),
                        pltpu.VMEM((2 * HALO, D_CONV), F32), pltpu.VMEM((2 * HALO, D_POOL), F32),
                        pltpu.SemaphoreType.DMA] + _ExchangeHalves.scratch(n_ex) + _ScatterToChips.scratch(n_sc),
        compiler_params=_cparams(2),
    )(dh1, m3, z3, z3, x3, zmeta, g1, g2, convw, poolw, pscale, win_all, wout, *exchanged, *scattered)
    return outs[:11], outs[11:11 + n_ex], outs[11 + n_ex:]


def _meta_bwd(dzm, meta_full, g1, win_all):
    def body(dzm_ref, meta_ref, g1_ref, win_ref, dmeta_ref, dg1_ref, a_ref, dzb_ref):
        xm = meta_ref[...]
        r = _rstd(xm)
        xh = xm * r
        g1v = g1_ref[...]
        a_ref[...] = (xh * g1v).astype(BF16)
        da = jnp.zeros((N_META, D_MODEL), F32)
        for j in range(N_CHIPS):
            dzj = dzm_ref[:, j * IN_SHARD:(j + 1) * IN_SHARD].astype(BF16)
            dzb_ref[j] = dzj
            da = da + _dot_nt(dzj, win_ref[j])
        dg1_ref[...] = _rows8(da * xh)
        dmeta_ref[...] = _rms_bwd(da, xh, r, g1v)

    vm = pl.BlockSpec(memory_space=pltpu.VMEM)
    return pl.pallas_call(
        body, name="meta_bwd",
        out_shape=[jax.ShapeDtypeStruct((N_META, D_MODEL), F32), jax.ShapeDtypeStruct((8, D_MODEL), F32),
                   jax.ShapeDtypeStruct((N_META, D_MODEL), BF16), jax.ShapeDtypeStruct((N_CHIPS, N_META, IN_SHARD), BF16)],
        in_specs=[vm] * 4, out_specs=[vm] * 4,
    )(dzm, meta_full, g1, win_all)


def _mixer_weight_grads(a, dz, ycat, dm, a_meta, dz_meta, ffn_sums):
    n_rows = a.shape[0]
    tk = min(TK_DW, n_rows)
    n_k = n_rows // tk
    n_sc = len(ffn_sums)

    def body(a_ref, dz_ref, yc_ref, dm_ref, am_ref, dzm_ref, *rest):
        sc = _ScatterToChips(rest[:n_sc], rest[n_sc + 2:2 * n_sc + 2], *rest[-2:])
        dwin_ref, dwout_ref = rest[n_sc:n_sc + 2]
        k = pl.program_id(0)

        @pl.when(k == 0)
        def _():
            sc.start()
            am_t = am_ref[...].T
            for j in range(N_CHIPS):
                dwin_ref[j] = _dot(am_t, dzm_ref[j])
            dwout_ref[...] = jnp.zeros_like(dwout_ref)

        a_t = a_ref[...].T
        for j in range(N_CHIPS):
            dwin_ref[j] += _dot(a_t, dz_ref[j])
        dwout_ref[...] += _dot_tn(yc_ref[...], dm_ref[...])

        @pl.when(k == n_k - 1)
        def _():
            sc.finish()

    row = pl.BlockSpec((tk, D_MODEL), lambda k: (k, 0))
    outs = pl.pallas_call(
        body, name="mixer_weight_grads", grid=(n_k,),
        out_shape=[jax.ShapeDtypeStruct((N_CHIPS, D_MODEL, IN_SHARD), F32),
                   jax.ShapeDtypeStruct((D_MODEL, D_MODEL), F32)] + _ScatterToChips.out_shape(ffn_sums),
        in_specs=[row, pl.BlockSpec((N_CHIPS, tk, IN_SHARD), lambda k: (0, k, 0)), row, row,
                  _full((N_META, D_MODEL)), _full((N_CHIPS, N_META, IN_SHARD))] + [ANY] * n_sc,
        out_specs=[_full((N_CHIPS, D_MODEL, IN_SHARD)), _full((D_MODEL, D_MODEL))] + [ANY] * n_sc,
        scratch_shapes=_ScatterToChips.scratch(n_sc),
        compiler_params=_cparams(1),
    )(a, dz, ycat, dm, a_meta, dz_meta, *ffn_sums)
    return [outs[0], outs[1].reshape(N_CHIPS, OUT_SHARD, D_MODEL)], outs[2:]


def kernel(x, meta_tokens, norm_mix_pre, w_in, conv_w, pool_w, pool_scale, w_out, norm_mix_post, norm_ffn_pre, w_gate, w_up, w_down, norm_ffn_post, loss_target, m_meta_tokens, m_norm_mix_pre, m_w_in, m_conv_w, m_pool_w, m_pool_scale, m_w_out, m_norm_mix_post, m_norm_ffn_pre, m_w_gate, m_w_up, m_w_down, m_norm_ffn_post, v_meta_tokens, v_norm_mix_pre, v_w_in, v_conv_w, v_pool_w, v_pool_scale, v_w_out, v_norm_mix_post, v_norm_ffn_pre, v_w_gate, v_w_up, v_w_down, v_norm_ffn_post):
    n_seq, seq, _ = x.shape
    n_rows = n_seq * seq
    chip = 2 * lax.axis_index("x") + lax.axis_index("y")
    meta_cols = D_MODEL // N_CHIPS
    conv_cols = D_CONV // N_CHIPS

    small = jnp.zeros((2 * HALO, meta_cols), F32)
    small = small.at[0:N_META, :].set(meta_tokens).at[N_META:N_META + 3, 0:conv_cols].set(conv_w[0])
    win_all, wout_all, small_all = _all_gather_shards([w_in[0].astype(BF16), w_out[0].astype(BF16), small])
    meta_full = small_all[:, 0:N_META, :].transpose(1, 0, 2).reshape(N_META, D_MODEL)
    conv_full = small_all[:, N_META:N_META + 3, 0:conv_cols].transpose(1, 0, 2).reshape(3, D_CONV)
    wout_full = wout_all.reshape(D_MODEL, D_MODEL)
    poolw_bf = pool_w[0].astype(BF16)
    pscale = pool_scale
    g1, g2, g3, g4 = norm_mix_pre, norm_mix_post, norm_ffn_pre, norm_ffn_post
    place = jnp.stack([chip, lax.axis_index("c")]).astype(jnp.int32)

    zmeta = _meta_fwd(meta_full, g1, win_all)
    (z3, m3, h1), ffn_w = _mixer_fwd(
        x, zmeta, g1, g2, conv_full, poolw_bf, pscale, win_all, wout_full,
        [w_gate[0].T.astype(BF16), w_up[0].T.astype(BF16), w_down[0].astype(BF16)])
    wg_t, wu_t, wd_full = [w.reshape(D_FF, D_MODEL) for w in ffn_w]
    dh1, f_bf, dd_bf, ds_bf, du_bf, gg_bf, lossp, dg3p, dg4p = _ffn_fwd_bwd(
        h1.reshape(n_rows, D_MODEL), loss_target.reshape(n_rows, D_MODEL), g3, g4, wg_t, wu_t, wd_full)
    as_shards = lambda g: g.reshape(N_CHIPS, FF_SHARD, D_MODEL)
    (dwd,), _ = _ffn_weight_grads("ffn_weight_grads_down", [gg_bf], dd_bf, [])
    dwd = as_shards(dwd)
    (dwg_t, dwu_t), (dwd_recv,) = _ffn_weight_grads("ffn_weight_grads_gate_up", [ds_bf, du_bf], f_bf, [dwd])
    dwg_t, dwu_t = as_shards(dwg_t), as_shards(dwu_t)
    ((grad_x, dz_bf, a_bf, yc_bf, dm_bf, dg1p, dg2p, dscp, dcwp, dpw, dzm), (dwg_recv, dwu_recv),
     (dwd_rbuf,)) = _mixer_bwd(
        dh1.reshape(n_seq, seq, D_MODEL), m3, z3, x, zmeta, g1, g2, conv_full, poolw_bf, pscale, win_all, wout_full,
        [dwg_t, dwu_t], [_add_pairs(dwd, dwd_recv, place)])
    dmeta, dg1m, a_meta, dz_meta = _meta_bwd(dzm, meta_full, g1, win_all)
    mix_grads, (dwg_rbuf, dwu_rbuf) = _mixer_weight_grads(
        a_bf, dz_bf, yc_bf, dm_bf, a_meta, dz_meta,
        [_add_pairs(dwg_t, dwg_recv, place), _add_pairs(dwu_t, dwu_recv, place)])

    mix_recvs = _exchange_halves(mix_grads)
    mix_rbufs = _scatter_to_chips([_add_pairs(g, r, place) for g, r in zip(mix_grads, mix_recvs)])
    grads = list(mix_grads) + [dwg_t, dwu_t, dwd]
    recvs = list(mix_recvs) + [dwg_recv, dwu_recv, dwd_recv]
    rbufs = list(mix_rbufs) + [dwg_rbuf, dwu_rbuf, dwd_rbuf]
    reduced = _gather_halves([_add_chips(g, r, rb, place) for g, r, rb in zip(grads, recvs, rbufs)])
    g_win, g_wout, g_wg_t, g_wu_t, g_wd = [r.reshape(2 * r.shape[1], r.shape[2]) for r in reduced]

    a_red, b_red, c_red = _all_reduce_small(dg1p, dg1m, dg2p, dg3p, dg4p, lossp, dmeta, dscp, dcwp,
                                            dpw.reshape(SMALL_C_ROWS, POOL_GROUP))
    loss = a_red[4, 0]
    g_g1, g_g2, g_g3, g_g4 = a_red[0:1], a_red[1:2], a_red[2:3], a_red[3:4]
    g_meta = lax.dynamic_slice(a_red, (8, chip * meta_cols), (N_META, meta_cols))
    g_pscale = b_red[0:1]
    g_conv = lax.dynamic_slice(b_red, (1, chip * conv_cols), (3, conv_cols))
    g_poolw = c_red

    big = [(w_in[0], g_win, m_w_in[0], v_w_in[0]), (w_out[0], g_wout, m_w_out[0], v_w_out[0]),
           (w_gate[0].T, g_wg_t, m_w_gate[0].T, v_w_gate[0].T), (w_up[0].T, g_wu_t, m_w_up[0].T, v_w_up[0].T),
           (w_down[0], g_wd, m_w_down[0], v_w_down[0])]
    big_out = [_adamw_big(w, g, m, v) for (w, g, m, v) in big]
    big_out[2] = [o.T for o in big_out[2]]
    big_out[3] = [o.T for o in big_out[3]]
    g_wg, g_wu = g_wg_t.T, g_wu_t.T
    small_groups = [
        (meta_tokens, g_meta, m_meta_tokens, v_meta_tokens),
        (g1, g_g1, m_norm_mix_pre, v_norm_mix_pre),
        (conv_w[0], g_conv, m_conv_w[0], v_conv_w[0]),
        (pool_w.reshape(SMALL_C_ROWS, POOL_GROUP), g_poolw, m_pool_w.reshape(SMALL_C_ROWS, POOL_GROUP),
         v_pool_w.reshape(SMALL_C_ROWS, POOL_GROUP)),
        (pool_scale, g_pscale, m_pool_scale, v_pool_scale),
        (g2, g_g2, m_norm_mix_post, v_norm_mix_post),
        (g3, g_g3, m_norm_ffn_pre, v_norm_ffn_pre),
        (g4, g_g4, m_norm_ffn_post, v_norm_ffn_post),
    ]
    small_out = _adamw_small(small_groups)

    grads_out = [g_meta, g_g1, g_win[None], g_conv[None], g_poolw.reshape(pool_w.shape), g_pscale, g_wout[None],
                 g_g2, g_g3, g_wg[None], g_wu[None], g_wd[None], g_g4]
    s_meta, s_g1, s_conv, s_poolw, s_pscale, s_g2, s_g3, s_g4 = small_out
    b_win, b_wout, b_wg, b_wu, b_wd = big_out

    def leaf(k):
        return [s_meta[k], s_g1[k], b_win[k][None], s_conv[k][None], s_poolw[k].reshape(pool_w.shape), s_pscale[k],
                b_wout[k][None], s_g2[k], s_g3[k], b_wg[k][None], b_wu[k][None], b_wd[k][None], s_g4[k]]

    return (loss, grad_x, *grads_out, *leaf(0), *leaf(1), *leaf(2))
```

```python
import jax
import jax.numpy as jnp
from jax import lax
from jax.experimental import pallas as pl
from jax.experimental.pallas import tpu as pltpu

F32 = jnp.float32
BF16 = jnp.bfloat16
MESH = pl.DeviceIdType.MESH

D_MODEL = 1024
D_CONV = 512
D_POOL = 512
POOL_GROUP = 128
N_POOL_GROUPS = 4
D_IN_PROJ = 2048
D_FF = 2816
N_CHIPS = 4
FF_SHARD = D_FF // N_CHIPS
IN_SHARD = D_IN_PROJ // N_CHIPS
OUT_SHARD = D_MODEL // N_CHIPS
N_META = 16
HALO = 16
RMS_EPS = 1e-6

ADAM_LR = 0.001
ADAM_B1 = 0.9
ADAM_B2 = 0.999
ADAM_EPS = 1e-08
ADAM_WD = 0.01
ADAM_STEP = 10

TM_MIX_FWD = 512
TM_MIX_BWD = 512
TM_FFN = 256
TK_DW = 1024
FF_CHUNK = 512
VMEM_LIMIT = 56 * 1024 * 1024


def _cparams(n_grid):
    return pltpu.CompilerParams(dimension_semantics=("arbitrary",) * n_grid, vmem_limit_bytes=VMEM_LIMIT)


def _dot(a, b):
    return jnp.dot(a, b, preferred_element_type=F32)


def _dot_nt(a, b):
    return lax.dot_general(a, b, (((1,), (1,)), ((), ())), preferred_element_type=F32)


def _dot_tn(a, b):
    return lax.dot_general(a, b, (((0,), (0,)), ((), ())), preferred_element_type=F32)


def _rows8(v):
    r, c = v.shape
    return v.reshape(r // 8, 8, c).sum(axis=0)


def _rstd(v):
    return lax.rsqrt(jnp.mean(v * v, axis=-1, keepdims=True) + RMS_EPS)


def _rms_bwd(dy, xhat, rstd, gain):
    dyg = dy * gain
    return rstd * (dyg - xhat * jnp.mean(dyg * xhat, axis=-1, keepdims=True))


def _sigmoid(v):
    return 1.0 / (1.0 + jnp.exp(-v))


def _gcols(g):
    return slice(g * POOL_GROUP, (g + 1) * POOL_GROUP)


def _window_sum(e, g, ahead):
    n = e.shape[0]
    w = e
    for level in range(g + 1):
        shift = 1 << level
        w = w + pltpu.roll(w, (n - shift) if ahead else shift, 0)
    return w


def _pool_fwd(pb, g, n):
    e = pb[0:HALO + n, _gcols(g)]
    return _window_sum(e, g, False)[HALO:, :] * (1.0 / (2 << g)) - e[HALO:, :]


def _pool_bwd(qb, g, n):
    e = qb[0:n + HALO, _gcols(g)]
    return _window_sum(e, g, True)[0:n, :] * (1.0 / (2 << g)) - e[0:n, :]


def _full(shape):
    nd = len(shape)
    return pl.BlockSpec(shape, lambda *_: (0,) * nd)


ANY = pl.BlockSpec(memory_space=pl.ANY)


def _mesh_pos():
    x, y, c = lax.axis_index("x"), lax.axis_index("y"), lax.axis_index("c")
    chips = [(1 - x, y), (x, 1 - y), (1 - x, 1 - y)]
    return x, y, c, chips


def _half(ref, h):
    hr = ref.shape[0] // 2
    return ref.at[pl.ds(h * hr, hr), :]


class _AllGather:
    def __init__(self, ins, outs, send_sems, recv_sems):
        self.ins, self.outs, self.send_sems, self.recv_sems = ins, outs, send_sems, recv_sems
        self.n = len(ins)

    @staticmethod
    def scratch(n):
        return [pltpu.SemaphoreType.DMA((6 * n,)), pltpu.SemaphoreType.DMA((6 * n,))]

    @staticmethod
    def out_shape(shards):
        return [jax.ShapeDtypeStruct((N_CHIPS,) + s.shape, s.dtype) for s in shards]

    def _copy(self, a, k, src, dst, to):
        return pltpu.make_async_remote_copy(src_ref=src, dst_ref=dst, send_sem=self.send_sems.at[6 * a + k],
                                            recv_sem=self.recv_sems.at[6 * a + k], device_id=to, device_id_type=MESH)

    def _ici(self, a, k):
        x, y, c, chips = _mesh_pos()
        return self._copy(a, k, _half(self.ins[a], c), _half(self.outs[a].at[2 * x + y], c), (*chips[k], c))

    def _d2d(self, a, k, h):
        x, y, c, chips = _mesh_pos()
        slot = _half(self.outs[a].at[2 * chips[k][0] + chips[k][1]], h)
        return self._copy(a, 3 + k, slot, slot, (x, y, 1 - c))

    def start(self):
        for a in range(self.n):
            for k in range(3):
                self._ici(a, k).start()

    def forward(self, a):
        c = lax.axis_index("c")
        for k in range(3):
            self._ici(a, k).wait_recv()
            self._d2d(a, k, c).start()

    def finish(self):
        c = lax.axis_index("c")
        for a in range(self.n):
            for k in range(3):
                self._d2d(a, k, 1 - c).wait_recv()
        for a in range(self.n):
            for k in range(3):
                self._ici(a, k).wait_send()
                self._d2d(a, k, c).wait_send()


def _fill_own_slot(gathered, shards):
    chip = 2 * lax.axis_index("x") + lax.axis_index("y")
    return [lax.dynamic_update_slice(o, s[None], (chip, 0, 0)) for o, s in zip(gathered, shards)]


def _all_gather_shards(shards):
    n = len(shards)

    def body(*refs):
        ag = _AllGather(refs[:n], refs[n:2 * n], *refs[2 * n:])
        ag.start()
        for a in range(n):
            ag.forward(a)
        ag.finish()

    outs = pl.pallas_call(
        body, name="all_gather_weights", out_shape=_AllGather.out_shape(shards),
        in_specs=[ANY] * n, out_specs=[ANY] * n, scratch_shapes=_AllGather.scratch(n),
    )(*shards)
    return _fill_own_slot(outs, shards)


class _ExchangeHalves:
    def __init__(self, ins, recvs, send_sems, recv_sems):
        self.ins, self.recvs, self.send_sems, self.recv_sems = ins, recvs, send_sems, recv_sems

    @staticmethod
    def scratch(n):
        return [pltpu.SemaphoreType.DMA((n,)), pltpu.SemaphoreType.DMA((n,))]

    @staticmethod
    def out_shape(grads):
        return [jax.ShapeDtypeStruct((g.shape[0], g.shape[1] // 2, g.shape[2]), g.dtype) for g in grads]

    def _copies(self):
        x, y, c, _ = _mesh_pos()
        out = []
        for a, (src, dst) in enumerate(zip(self.ins, self.recvs)):
            hr = src.shape[1] // 2
            out.append(pltpu.make_async_remote_copy(
                src_ref=src.at[:, pl.ds((1 - c) * hr, hr), :], dst_ref=dst, send_sem=self.send_sems.at[a],
                recv_sem=self.recv_sems.at[a], device_id=(x, y, 1 - c), device_id_type=MESH))
        return out

    def start(self):
        for cp in self._copies():
            cp.start()

    def finish(self):
        for cp in self._copies():
            cp.wait()


def _exchange_halves(grads):
    n = len(grads)

    def body(*refs):
        ex = _ExchangeHalves(refs[:n], refs[n:2 * n], *refs[2 * n:])
        ex.start()
        ex.finish()

    return pl.pallas_call(
        body, name="grad_exchange_halves", out_shape=_ExchangeHalves.out_shape(grads),
        in_specs=[ANY] * n, out_specs=[ANY] * n, scratch_shapes=_ExchangeHalves.scratch(n),
    )(*grads)


class _ScatterToChips:
    def __init__(self, ins, rbufs, send_sems, recv_sems):
        self.ins, self.rbufs, self.send_sems, self.recv_sems = ins, rbufs, send_sems, recv_sems

    @staticmethod
    def scratch(n):
        return [pltpu.SemaphoreType.DMA((3 * n,)), pltpu.SemaphoreType.DMA((3 * n,))]

    @staticmethod
    def out_shape(sums):
        return [jax.ShapeDtypeStruct((3,) + s.shape[1:], BF16) for s in sums]

    def _copies(self):
        x, y, c, chips = _mesh_pos()
        out = []
        for a, (src, dst) in enumerate(zip(self.ins, self.rbufs)):
            for k, chip in enumerate(chips):
                out.append(pltpu.make_async_remote_copy(
                    src_ref=src.at[2 * chip[0] + chip[1]], dst_ref=dst.at[k], send_sem=self.send_sems.at[3 * a + k],
                    recv_sem=self.recv_sems.at[3 * a + k], device_id=(*chip, c), device_id_type=MESH))
        return out

    def start(self):
        for cp in self._copies():
            cp.start()

    def finish(self):
        for cp in self._copies():
            cp.wait()


def _scatter_to_chips(sums_bf16):
    n = len(sums_bf16)

    def body(*refs):
        sc = _ScatterToChips(refs[:n], refs[n:2 * n], *refs[2 * n:])
        sc.start()
        sc.finish()

    return pl.pallas_call(
        body, name="grad_scatter_to_chips", out_shape=_ScatterToChips.out_shape(sums_bf16),
        in_specs=[ANY] * n, out_specs=[ANY] * n, scratch_shapes=_ScatterToChips.scratch(n),
    )(*sums_bf16)


def _gather_halves(halves):
    n = len(halves)

    def body(*refs):
        ins, outs = refs[:n], refs[n:2 * n]
        send_sems, recv_sems = refs[2 * n:]
        x, y, c, _ = _mesh_pos()
        sib = (x, y, 1 - c)
        remote = [pltpu.make_async_remote_copy(src_ref=ins[a].at[c], dst_ref=outs[a].at[c],
                                               send_sem=send_sems.at[a], recv_sem=recv_sems.at[a],
                                               device_id=sib, device_id_type=MESH) for a in range(n)]
        for cp in remote:
            cp.start()
        for a in range(n):
            pltpu.make_async_remote_copy(src_ref=ins[a].at[1 - c], dst_ref=outs[a].at[1 - c], send_sem=send_sems.at[a],
                                         recv_sem=recv_sems.at[a], device_id=sib, device_id_type=MESH).wait_recv()
        for cp in remote:
            cp.wait_send()

    return pl.pallas_call(
        body, name="grad_gather_halves",
        out_shape=[jax.ShapeDtypeStruct(h.shape, F32) for h in halves],
        in_specs=[ANY] * n, out_specs=[ANY] * n, input_output_aliases={a: a for a in range(n)},
        scratch_shapes=[pltpu.SemaphoreType.DMA((n,)), pltpu.SemaphoreType.DMA((n,))],
    )(*halves)


SMALL_A_ROWS = 24
SMALL_B_ROWS = 8
SMALL_C_ROWS = N_POOL_GROUPS * POOL_GROUP


class _AllReduceSmall:
    N_IN = 10
    SHAPES = [(SMALL_A_ROWS, D_MODEL), (SMALL_B_ROWS, D_CONV), (SMALL_C_ROWS, POOL_GROUP)]

    def __init__(self, ins, outs, scratch):
        self.ins, self.outs = ins, outs
        self.bufs, self.rcvs, self.send_sems, self.recv_sems = scratch[:3], scratch[3:6], scratch[6], scratch[7]

    @classmethod
    def scratch(cls):
        return ([pltpu.VMEM((3,) + s, F32) for s in cls.SHAPES] + [pltpu.VMEM((3,) + s, F32) for s in cls.SHAPES]
                + [pltpu.SemaphoreType.DMA((9,)), pltpu.SemaphoreType.DMA((9,))])

    @classmethod
    def out_shape(cls):
        return [jax.ShapeDtypeStruct(s, F32) for s in cls.SHAPES]

    def _copies(self, st):
        x, y, c, _ = _mesh_pos()
        peer = [(x, y, 1 - c), (1 - x, y, c), (x, 1 - y, c)][st]
        return [pltpu.make_async_remote_copy(
            src_ref=buf.at[st], dst_ref=rcv.at[st], send_sem=self.send_sems.at[3 * st + i],
            recv_sem=self.recv_sems.at[3 * st + i], device_id=peer, device_id_type=MESH)
            for i, (buf, rcv) in enumerate(zip(self.bufs, self.rcvs))]

    def pack_and_send(self):
        dg1_ref, dg1m_ref, dg2_ref, dg3_ref, dg4_ref, loss_ref, dmeta_ref, dsc_ref, dcw_ref, dpw_ref = self.ins
        a_buf, b_buf, c_buf = self.bufs

        def rowsum(v):
            return jnp.sum(v, axis=0, keepdims=True)

        a_buf[0, 0:1, :] = rowsum(dg1_ref[...] + dg1m_ref[...])
        a_buf[0, 1:2, :] = rowsum(dg2_ref[...])
        a_buf[0, 2:3, :] = rowsum(dg3_ref[...])
        a_buf[0, 3:4, :] = rowsum(dg4_ref[...])
        loss = jnp.sum(rowsum(loss_ref[...]), axis=1, keepdims=True) * (0.5 / D_MODEL)
        a_buf[0, 4:5, :] = jnp.broadcast_to(loss, (1, D_MODEL))
        a_buf[0, 5:8, :] = jnp.zeros((3, D_MODEL), F32)
        a_buf[0, 8:24, :] = dmeta_ref[...]
        b_buf[0, 0:1, :] = rowsum(dsc_ref[...])
        for k in range(3):
            b_buf[0, 1 + k:2 + k, :] = rowsum(dcw_ref[8 * k:8 * k + 8, :])
        b_buf[0, 4:8, :] = jnp.zeros((4, D_CONV), F32)
        c_buf[0] = dpw_ref[...]
        for cp in self._copies(0):
            cp.start()

    def combine(self, st):
        for cp in self._copies(st):
            cp.wait()
        if st < 2:
            for buf, rcv in zip(self.bufs, self.rcvs):
                buf[st + 1] = buf[st] + rcv[st]
            for cp in self._copies(st + 1):
                cp.start()
        else:
            for out, buf, rcv in zip(self.outs, self.bufs, self.rcvs):
                out[...] = buf[st] + rcv[st]


def _row_block(rows):
    for cand in (512, 448, 384, 352, 320, 256, 128, 64, 32, 16):
        if rows % cand == 0:
            return cand
    return rows


def _add_pairs(grad, recv, place):
    n_sh, rows2, cols = grad.shape
    hr = rows2 // 2
    br = _row_block(hr)

    def body(place_ref, a_ref, b_ref, o_ref):
        o_ref[...] = (a_ref[0] + b_ref[...]).astype(BF16)

    return pl.pallas_call(
        body, name="grad_add_pairs",
        grid_spec=pltpu.PrefetchScalarGridSpec(
            num_scalar_prefetch=1, grid=(n_sh, hr // br),
            in_specs=[pl.BlockSpec((1, 1, br, cols), lambda j, i, p: (j, p[1], i, 0)),
                      pl.BlockSpec((1, br, cols), lambda j, i, p: (j, i, 0))],
            out_specs=pl.BlockSpec((1, br, cols), lambda j, i, p: (j, i, 0))),
        out_shape=jax.ShapeDtypeStruct((n_sh, hr, cols), BF16), compiler_params=_cparams(2),
    )(place, grad.reshape(n_sh, 2, hr, cols), recv)


def _add_chips(grad, recv, rbuf, place):
    n_sh, rows2, cols = grad.shape
    hr = rows2 // 2
    br = _row_block(hr)

    def body(place_ref, a_ref, b_ref, r_ref, o_ref):
        own = a_ref[0, 0] + b_ref[0]
        o_ref[0] = ((own + r_ref[0].astype(F32)) + r_ref[1].astype(F32)) + r_ref[2].astype(F32)

    return pl.pallas_call(
        body, name="grad_add_chips",
        grid_spec=pltpu.PrefetchScalarGridSpec(
            num_scalar_prefetch=1, grid=(hr // br,),
            in_specs=[pl.BlockSpec((1, 1, br, cols), lambda i, p: (p[0], p[1], i, 0)),
                      pl.BlockSpec((1, br, cols), lambda i, p: (p[0], i, 0)),
                      pl.BlockSpec((3, br, cols), lambda i, p: (0, i, 0))],
            out_specs=pl.BlockSpec((1, br, cols), lambda i, p: (p[1], i, 0))),
        out_shape=jax.ShapeDtypeStruct((2, hr, cols), F32), compiler_params=_cparams(1),
    )(place, grad.reshape(n_sh, 2, hr, cols), recv, rbuf)


def _adamw_math(w, g, m, v):
    m2 = ADAM_B1 * m + (1.0 - ADAM_B1) * g
    v2 = ADAM_B2 * v + (1.0 - ADAM_B2) * (g * g)
    m_hat = m2 / (1.0 - ADAM_B1 ** ADAM_STEP)
    v_hat = v2 / (1.0 - ADAM_B2 ** ADAM_STEP)
    delta = -ADAM_LR * (m_hat / (jnp.sqrt(v_hat) + ADAM_EPS) + ADAM_WD * w)
    return delta, m2, v2


def _adamw_big(w, g, m, v):
    rows, cols = w.shape
    br = _row_block(rows)

    def body(w_ref, g_ref, m_ref, v_ref, d_ref, m2_ref, v2_ref):
        d, m2, v2 = _adamw_math(w_ref[...], g_ref[...], m_ref[...], v_ref[...])
        d_ref[...] = d
        m2_ref[...] = m2
        v2_ref[...] = v2

    spec = pl.BlockSpec((br, cols), lambda i: (i, 0))
    return pl.pallas_call(
        body, name="adamw_big", grid=(rows // br,),
        out_shape=[jax.ShapeDtypeStruct((rows, cols), F32)] * 3,
        in_specs=[spec] * 4, out_specs=[spec] * 3, compiler_params=_cparams(1),
    )(w, g, m, v)


def _adamw_small(groups):
    n = len(groups)

    def body(*refs):
        ins, outs = refs[:4 * n], refs[4 * n:]
        for i in range(n):
            w, g, m, v = (r[...] for r in ins[4 * i:4 * i + 4])
            d, m2, v2 = _adamw_math(w, g, m, v)
            outs[3 * i][...] = d
            outs[3 * i + 1][...] = m2
            outs[3 * i + 2][...] = v2

    vm = pl.BlockSpec(memory_space=pltpu.VMEM)
    flat = [a for grp in groups for a in grp]
    out_shape = [jax.ShapeDtypeStruct(grp[0].shape, F32) for grp in groups for _ in range(3)]
    outs = pl.pallas_call(body, name="adamw_small", out_shape=out_shape,
                          in_specs=[vm] * (4 * n), out_specs=[vm] * (3 * n))(*flat)
    return [tuple(outs[3 * i:3 * i + 3]) for i in range(n)]


def _load_weights(pairs, sem):
    for src, dst in pairs:
        cp = pltpu.make_async_copy(src, dst, sem)
        cp.start()
        cp.wait()


def _meta_fwd(meta_full, g1, win_all):
    def body(meta_ref, g1_ref, win_ref, z_ref):
        xm = meta_ref[...]
        a = (xm * _rstd(xm) * g1_ref[...]).astype(BF16)
        for j in range(N_CHIPS):
            z_ref[:, j * IN_SHARD:(j + 1) * IN_SHARD] = _dot(a, win_ref[j])

    vm = pl.BlockSpec(memory_space=pltpu.VMEM)
    return pl.pallas_call(body, name="meta_fwd", out_shape=jax.ShapeDtypeStruct((N_META, D_IN_PROJ), F32),
                          in_specs=[vm] * 3, out_specs=vm)(meta_full, g1, win_all)


def _mixer_fwd(x3, zmeta, g1, g2, convw, poolw, pscale, win_all, wout, ffn_shards):
    n_seq, seq, _ = x3.shape
    tm = min(TM_MIX_FWD, seq)
    n_t = seq // tm
    n_steps = n_seq * n_t
    n_ag = len(ffn_shards)

    def body(x_ref, zm_ref, g1_ref, g2_ref, cw_ref, pw_ref, ps_ref, win_hbm, wout_hbm, *rest):
        ag = _AllGather(rest[:n_ag], rest[n_ag + 3:2 * n_ag + 3], *rest[-2:])
        z_ref, m_ref, h1_ref = rest[n_ag:n_ag + 3]
        win_v, wout_v, cvb, pb, sem = rest[2 * n_ag + 3:-2]
        s, t = pl.program_id(0), pl.program_id(1)
        step = s * n_t + t

        @pl.when(step == 0)
        def _():
            ag.start()
            _load_weights([(win_hbm, win_v), (wout_hbm, wout_v)], sem)

        for a in range(n_ag):
            @pl.when(step == min(((a + 1) * n_steps) // n_ag, n_steps - 1))
            def _():
                ag.forward(a)

        xt = x_ref[0]
        a = (xt * _rstd(xt) * g1_ref[...]).astype(BF16)
        zb = _dot(a, win_v[0])
        zc = _dot(a, win_v[1])
        zv = _dot(a, win_v[2])
        zp = _dot(a, win_v[3])
        z_ref[0, :, 0:IN_SHARD] = zb
        z_ref[0, :, IN_SHARD:2 * IN_SHARD] = zc
        z_ref[0, :, 2 * IN_SHARD:3 * IN_SHARD] = zv
        z_ref[0, :, 3 * IN_SHARD:4 * IN_SHARD] = zp

        @pl.when(t == 0)
        def _():
            cvb[0:HALO, :] = zm_ref[:, IN_SHARD:2 * IN_SHARD] * zm_ref[:, 2 * IN_SHARD:3 * IN_SHARD]
            pb[0:HALO, :] = zm_ref[:, 3 * IN_SHARD:4 * IN_SHARD]

        @pl.when(t > 0)
        def _():
            cvb[0:HALO, :] = cvb[tm:tm + HALO, :]
            pb[0:HALO, :] = pb[tm:tm + HALO, :]

        cv = zc * zv
        cvb[HALO:HALO + tm, :] = cv
        pb[HALO:HALO + tm, :] = zp
        cw = cw_ref[...]
        conv = cw[0:1] * cvb[HALO - 2:HALO - 2 + tm, :] + cw[1:2] * cvb[HALO - 1:HALO - 1 + tm, :] + cw[2:3] * cv
        parts = [(zb * conv).astype(BF16)]
        for g in range(N_POOL_GROUPS):
            pooled = _pool_fwd(pb, g, tm).astype(BF16)
            parts.append((_dot(pooled, pw_ref[g]) * ps_ref[:, _gcols(g)]).astype(BF16))
        m = _dot(jnp.concatenate(parts, axis=1), wout_v[...])
        m_ref[0] = m
        h1_ref[0] = xt + m * _rstd(m) * g2_ref[...]

        @pl.when(step == n_steps - 1)
        def _():
            ag.finish()

    row = lambda c: pl.BlockSpec((1, tm, c), lambda s, t: (s, t, 0))
    outs = pl.pallas_call(
        body, name="mixer_fwd", grid=(n_seq, n_t),
        out_shape=[jax.ShapeDtypeStruct((n_seq, seq, D_IN_PROJ), F32), jax.ShapeDtypeStruct((n_seq, seq, D_MODEL), F32),
                   jax.ShapeDtypeStruct((n_seq, seq, D_MODEL), F32)] + _AllGather.out_shape(ffn_shards),
        in_specs=[row(D_MODEL), _full((N_META, D_IN_PROJ)), _full((1, D_MODEL)), _full((1, D_MODEL)),
                  _full((3, D_CONV)), _full((N_POOL_GROUPS, POOL_GROUP, POOL_GROUP)), _full((1, D_POOL)), ANY, ANY]
        + [ANY] * n_ag,
        out_specs=[row(D_IN_PROJ), row(D_MODEL), row(D_MODEL)] + [ANY] * n_ag,
        scratch_shapes=[pltpu.VMEM((N_CHIPS, D_MODEL, IN_SHARD), BF16), pltpu.VMEM((D_MODEL, D_MODEL), BF16),
                        pltpu.VMEM((HALO + tm, D_CONV), F32), pltpu.VMEM((HALO + tm, D_POOL), F32),
                        pltpu.SemaphoreType.DMA] + _AllGather.scratch(n_ag),
        compiler_params=_cparams(2),
    )(x3, zmeta, g1, g2, convw, poolw, pscale, win_all, wout, *ffn_shards)
    return outs[:3], _fill_own_slot(outs[3:], ffn_shards)


def _ffn_chunks():
    out, r0 = [], 0
    while r0 < D_FF:
        out.append((r0, min(FF_CHUNK, D_FF - r0)))
        r0 += FF_CHUNK
    return out


def _ffn_fwd_bwd(h1, target, g3, g4, wg_t, wu_t, wd):
    n_rows = h1.shape[0]
    tm = min(TM_FFN, n_rows)
    chunks = _ffn_chunks()

    def body(h1_ref, t_ref, g3_ref, g4_ref, wg_hbm, wu_hbm, wd_hbm,
             dh1_ref, f_ref, dd_ref, ds_ref, du_ref, gg_ref, loss_ref, dg3_ref, dg4_ref,
             wg_v, wu_v, wd_v, s_sc, u_sc, sem):
        @pl.when(pl.program_id(0) == 0)
        def _():
            _load_weights([(wg_hbm, wg_v), (wu_hbm, wu_v), (wd_hbm, wd_v)], sem)
            loss_ref[...] = jnp.zeros_like(loss_ref)
            dg3_ref[...] = jnp.zeros_like(dg3_ref)
            dg4_ref[...] = jnp.zeros_like(dg4_ref)

        h1v = h1_ref[...]
        r3 = _rstd(h1v)
        hh = h1v * r3
        g3v, g4v = g3_ref[...], g4_ref[...]
        f = (hh * g3v).astype(BF16)
        f_ref[...] = f
        d = jnp.zeros((tm, D_MODEL), F32)
        for r0, sz in chunks:
            s = _dot_nt(f, wg_v[r0:r0 + sz, :])
            u = _dot_nt(f, wu_v[r0:r0 + sz, :])
            s_sc[:, r0:r0 + sz] = s
            u_sc[:, r0:r0 + sz] = u
            gc = (s * _sigmoid(s) * u).astype(BF16)
            gg_ref[:, r0:r0 + sz] = gc
            d = d + _dot(gc, wd_v[r0:r0 + sz, :])
        r4 = _rstd(d)
        dh = d * r4
        err = (h1v + dh * g4v) - t_ref[...]
        loss_ref[...] += _rows8(err * err)
        dy = err * (1.0 / D_MODEL)
        dg4_ref[...] += _rows8(dy * dh)
        ddb = _rms_bwd(dy, dh, r4, g4v).astype(BF16)
        dd_ref[...] = ddb
        df = jnp.zeros((tm, D_MODEL), F32)
        for r0, sz in chunks:
            dgg = _dot_nt(ddb, wd_v[r0:r0 + sz, :])
            s = s_sc[:, r0:r0 + sz]
            u = u_sc[:, r0:r0 + sz]
            sig = _sigmoid(s)
            dsc = (dgg * u * (sig * (1.0 + s * (1.0 - sig)))).astype(BF16)
            duc = (dgg * (s * sig)).astype(BF16)
            ds_ref[:, r0:r0 + sz] = dsc
            du_ref[:, r0:r0 + sz] = duc
            df = df + _dot(dsc, wg_v[r0:r0 + sz, :]) + _dot(duc, wu_v[r0:r0 + sz, :])
        dg3_ref[...] += _rows8(df * hh)
        dh1_ref[...] = dy + _rms_bwd(df, hh, r3, g3v)

    row = pl.BlockSpec((tm, D_MODEL), lambda i: (i, 0))
    ffrow = pl.BlockSpec((tm, D_FF), lambda i: (i, 0))
    acc = _full((8, D_MODEL))
    act_bf = jax.ShapeDtypeStruct((n_rows, D_MODEL), BF16)
    ff_bf = jax.ShapeDtypeStruct((n_rows, D_FF), BF16)
    acc_shape = jax.ShapeDtypeStruct((8, D_MODEL), F32)
    w_vmem = pltpu.VMEM((D_FF, D_MODEL), BF16)
    return pl.pallas_call(
        body, name="ffn_fwd_bwd", grid=(n_rows // tm,),
        out_shape=[jax.ShapeDtypeStruct((n_rows, D_MODEL), F32), act_bf, act_bf, ff_bf, ff_bf, ff_bf,
                   acc_shape, acc_shape, acc_shape],
        in_specs=[row, row, _full((1, D_MODEL)), _full((1, D_MODEL)), ANY, ANY, ANY],
        out_specs=[row, row, row, ffrow, ffrow, ffrow, acc, acc, acc],
        scratch_shapes=[w_vmem, w_vmem, w_vmem, pltpu.VMEM((tm, D_FF), F32), pltpu.VMEM((tm, D_FF), F32),
                        pltpu.SemaphoreType.DMA],
        compiler_params=_cparams(1),
    )(h1, target, g3, g4, wg_t, wu_t, wd)


def _ffn_weight_grads(name, acts, other, exchanged):
    n_rows = other.shape[0]
    tk = min(TK_DW, n_rows)
    n_k = n_rows // tk
    half = D_FF // 2
    n_a, n_ex = len(acts), len(exchanged)

    def body(other_ref, *rest):
        act_refs = rest[:n_a]
        out_refs = rest[n_a + n_ex:2 * n_a + n_ex]
        c, k = pl.program_id(0), pl.program_id(1)
        if n_ex:
            ex = _ExchangeHalves(rest[n_a:n_a + n_ex], rest[2 * n_a + n_ex:2 * n_a + 2 * n_ex], *rest[-2:])

            @pl.when((c == 0) & (k == 0))
            def _():
                ex.start()

        @pl.when(k == 0)
        def _():
            for o in out_refs:
                o[...] = jnp.zeros_like(o)

        ov = other_ref[...]
        for a, o in zip(act_refs, out_refs):
            o[...] += _dot_tn(a[...], ov)

        if n_ex:
            @pl.when((c == 1) & (k == n_k - 1))
            def _():
                ex.finish()

    row = pl.BlockSpec((tk, D_MODEL), lambda c, k: (k, 0))
    ffrow = pl.BlockSpec((tk, half), lambda c, k: (k, c))
    out = pl.BlockSpec((half, D_MODEL), lambda c, k: (c, 0))
    outs = pl.pallas_call(
        body, name=name, grid=(2, n_k),
        out_shape=[jax.ShapeDtypeStruct((D_FF, D_MODEL), F32)] * n_a + _ExchangeHalves.out_shape(exchanged),
        in_specs=[row] + [ffrow] * n_a + [ANY] * n_ex, out_specs=[out] * n_a + [ANY] * n_ex,
        scratch_shapes=_ExchangeHalves.scratch(n_ex) if n_ex else [],
        compiler_params=_cparams(2),
    )(other, *acts, *exchanged)
    return outs[:n_a], outs[n_a:]


def _mixer_bwd(dh1, m3, z3, x3, zmeta, g1, g2, convw, poolw, pscale, win_all, wout, exchanged, scattered):
    n_seq, seq, _ = x3.shape
    tm = min(TM_MIX_BWD, seq)
    n_t = seq // tm
    hb = tm // HALO
    n_ex, n_sc = len(exchanged), len(scattered)
    n_cm = n_ex + n_sc

    def body(dh1_ref, m_ref, z_ref, zh_ref, x_ref, zm_ref, g1_ref, g2_ref, cw_ref, pw_ref, ps_ref, win_hbm, wout_hbm,
             *rest):
        outs0 = n_cm + 11
        ex = _ExchangeHalves(rest[:n_ex], rest[outs0:outs0 + n_ex], *rest[-4:-2])
        sc = _ScatterToChips(rest[n_ex:n_cm], rest[outs0 + n_ex:outs0 + n_cm], *rest[-2:])
        (dx_ref, dz_ref, a_ref, yc_ref, dm_ref, dg1_ref, dg2_ref, dsc_ref, dcw_ref, dpw_ref,
         dzm_ref) = rest[n_cm:outs0]
        win_v, wout_v, cvb, pb, dcb, dqb, mcb, mqb, sem = rest[outs0 + n_cm:-4]
        s, i = pl.program_id(0), pl.program_id(1)
        tr = n_t - 1 - i

        @pl.when((s == 0) & (i == 0))
        def _():
            sc.start()
            ex.start()
            _load_weights([(win_hbm, win_v), (wout_hbm, wout_v)], sem)
            for ref in (dg1_ref, dg2_ref, dsc_ref, dcw_ref, dpw_ref, dzm_ref):
                ref[...] = jnp.zeros_like(ref)

        @pl.when(i == 0)
        def _():
            dcb[tm:tm + HALO, :] = jnp.zeros((HALO, D_CONV), F32)
            dqb[tm:tm + HALO, :] = jnp.zeros((HALO, D_POOL), F32)

        @pl.when(i > 0)
        def _():
            dcb[tm:tm + HALO, :] = dcb[0:HALO, :]
            dqb[tm:tm + HALO, :] = dqb[0:HALO, :]

        g1v, g2v = g1_ref[...], g2_ref[...]
        dh1v = dh1_ref[0]
        mv = m_ref[0]
        r2 = _rstd(mv)
        mh = mv * r2
        dg2_ref[...] += _rows8(dh1v * mh)
        dmb = _rms_bwd(dh1v, mh, r2, g2v).astype(BF16)
        dm_ref[...] = dmb
        dyc = _dot_nt(dmb, wout_v[...])
        dyconv = dyc[:, 0:D_CONV]

        zb = z_ref[0, :, 0:IN_SHARD]
        zc = z_ref[0, :, IN_SHARD:2 * IN_SHARD]
        zv = z_ref[0, :, 2 * IN_SHARD:3 * IN_SHARD]
        zp = z_ref[0, :, 3 * IN_SHARD:4 * IN_SHARD]
        halo = jnp.where(tr == 0, zm_ref[...], zh_ref[0])
        cvb[0:HALO, :] = halo[:, IN_SHARD:2 * IN_SHARD] * halo[:, 2 * IN_SHARD:3 * IN_SHARD]
        pb[0:HALO, :] = halo[:, 3 * IN_SHARD:4 * IN_SHARD]
        cv0 = zc * zv
        cvb[HALO:HALO + tm, :] = cv0
        pb[HALO:HALO + tm, :] = zp
        cw = cw_ref[...]
        cv2 = cvb[HALO - 2:HALO - 2 + tm, :]
        cv1 = cvb[HALO - 1:HALO - 1 + tm, :]
        conv = cw[0:1] * cv2 + cw[1:2] * cv1 + cw[2:3] * cv0
        parts = [(zb * conv).astype(BF16)]
        for g in range(N_POOL_GROUPS):
            pooled = _pool_fwd(pb, g, tm).astype(BF16)
            mixed = _dot(pooled, pw_ref[g])
            scale = ps_ref[:, _gcols(g)]
            parts.append((mixed * scale).astype(BF16))
            dyp = dyc[:, D_CONV + g * POOL_GROUP:D_CONV + (g + 1) * POOL_GROUP]
            dsc_ref[:, _gcols(g)] += _rows8(dyp * mixed)
            dmix = (dyp * scale).astype(BF16)
            dpw_ref[g] += _dot_tn(pooled, dmix)
            dqb[0:tm, _gcols(g)] = _dot_nt(dmix, pw_ref[g])
        yc_ref[...] = jnp.concatenate(parts, axis=1)

        dconv = dyconv * zb
        dcb[0:tm, :] = dconv
        dcv = cw[2:3] * dconv + cw[1:2] * dcb[1:1 + tm, :] + cw[0:1] * dcb[2:2 + tm, :]
        dcw_ref[0:8, :] += _rows8(dconv * cv2)
        dcw_ref[8:16, :] += _rows8(dconv * cv1)
        dcw_ref[16:24, :] += _rows8(dconv * cv0)
        dzs = [(dyconv * conv).astype(BF16), (dcv * zv).astype(BF16), (dcv * zc).astype(BF16),
               jnp.concatenate([_pool_bwd(dqb, g, tm) for g in range(N_POOL_GROUPS)], axis=1).astype(BF16)]
        da = jnp.zeros((tm, D_MODEL), F32)
        for j in range(N_CHIPS):
            dz_ref[j] = dzs[j]
            da = da + _dot_nt(dzs[j], win_v[j])
        xt = x_ref[0]
        r1 = _rstd(xt)
        xh = xt * r1
        a_ref[...] = (xh * g1v).astype(BF16)
        dg1_ref[...] += _rows8(da * xh)
        dx_ref[0] = dh1v + _rms_bwd(da, xh, r1, g1v)

        @pl.when(tr == 0)
        def _():
            mcb[0:HALO, :] = jnp.zeros((HALO, D_CONV), F32)
            mqb[0:HALO, :] = jnp.zeros((HALO, D_POOL), F32)
            mcb[HALO:2 * HALO, :] = dcb[0:HALO, :]
            mqb[HALO:2 * HALO, :] = dqb[0:HALO, :]
            dcv_m = cw[1:2] * mcb[1:1 + HALO, :] + cw[0:1] * mcb[2:2 + HALO, :]
            dzm_ref[:, IN_SHARD:2 * IN_SHARD] += dcv_m * zm_ref[:, 2 * IN_SHARD:3 * IN_SHARD]
            dzm_ref[:, 2 * IN_SHARD:3 * IN_SHARD] += dcv_m * zm_ref[:, IN_SHARD:2 * IN_SHARD]
            dzm_ref[:, 3 * IN_SHARD:4 * IN_SHARD] += jnp.concatenate(
                [_pool_bwd(mqb, g, HALO) for g in range(N_POOL_GROUPS)], axis=1)

        @pl.when((s == n_seq - 1) & (i == n_t - 1))
        def _():
            ex.finish()
            sc.finish()

    row3 = lambda c: pl.BlockSpec((1, tm, c), lambda s, i: (s, n_t - 1 - i, 0))
    row2 = lambda c: pl.BlockSpec((tm, c), lambda s, i: (s * n_t + n_t - 1 - i, 0))
    halo_spec = pl.BlockSpec((1, HALO, D_IN_PROJ), lambda s, i: (s, jnp.maximum((n_t - 1 - i) * hb - 1, 0), 0))
    n_rows = n_seq * seq
    act_bf = jax.ShapeDtypeStruct((n_rows, D_MODEL), BF16)
    outs = pl.pallas_call(
        body, name="mixer_bwd", grid=(n_seq, n_t),
        out_shape=[jax.ShapeDtypeStruct((n_seq, seq, D_MODEL), F32),
                   jax.ShapeDtypeStruct((N_CHIPS, n_rows, IN_SHARD), BF16), act_bf, act_bf, act_bf,
                   jax.ShapeDtypeStruct((8, D_MODEL), F32), jax.ShapeDtypeStruct((8, D_MODEL), F32),
                   jax.ShapeDtypeStruct((8, D_POOL), F32), jax.ShapeDtypeStruct((24, D_CONV), F32),
                   jax.ShapeDtypeStruct((N_POOL_GROUPS, POOL_GROUP, POOL_GROUP), F32),
                   jax.ShapeDtypeStruct((N_META, D_IN_PROJ), F32)]
        + _ExchangeHalves.out_shape(exchanged) + _ScatterToChips.out_shape(scattered),
        in_specs=[row3(D_MODEL), row3(D_MODEL), row3(D_IN_PROJ), halo_spec, row3(D_MODEL),
                  _full((N_META, D_IN_PROJ)), _full((1, D_MODEL)), _full((1, D_MODEL)), _full((3, D_CONV)),
                  _full((N_POOL_GROUPS, POOL_GROUP, POOL_GROUP)), _full((1, D_POOL)), ANY, ANY] + [ANY] * n_cm,
        out_specs=[row3(D_MODEL), pl.BlockSpec((N_CHIPS, tm, IN_SHARD), lambda s, i: (0, s * n_t + n_t - 1 - i, 0)),
                   row2(D_MODEL), row2(D_MODEL), row2(D_MODEL),
                   _full((8, D_MODEL)), _full((8, D_MODEL)), _full((8, D_POOL)), _full((24, D_CONV)),
                   _full((N_POOL_GROUPS, POOL_GROUP, POOL_GROUP)), _full((N_META, D_IN_PROJ))] + [ANY] * n_cm,
        scratch_shapes=[pltpu.VMEM((N_CHIPS, D_MODEL, IN_SHARD), BF16), pltpu.VMEM((D_MODEL, D_MODEL), BF16),
                        pltpu.VMEM((HALO + tm, D_CONV), F32), pltpu.VMEM((HALO + tm, D_POOL), F32),
                        pltpu.VMEM((tm + HALO, D_CONV), F32), pltpu.VMEM((tm + HALO, D_POOL), F32),
                        pltpu.VMEM((2 * HALO, D_CONV), F32), pltpu.VMEM((2 * HALO, D_POOL), F32),
                        pltpu.SemaphoreType.DMA] + _ExchangeHalves.scratch(n_ex) + _ScatterToChips.scratch(n_sc),
        compiler_params=_cparams(2),
    )(dh1, m3, z3, z3, x3, zmeta, g1, g2, convw, poolw, pscale, win_all, wout, *exchanged, *scattered)
    return outs[:11], outs[11:11 + n_ex], outs[11 + n_ex:]


def _meta_bwd(dzm, meta_full, g1, win_all):
    def body(dzm_ref, meta_ref, g1_ref, win_ref, dmeta_ref, dg1_ref, a_ref, dzb_ref):
        xm = meta_ref[...]
        r = _rstd(xm)
        xh = xm * r
        g1v = g1_ref[...]
        a_ref[...] = (xh * g1v).astype(BF16)
        da = jnp.zeros((N_META, D_MODEL), F32)
        for j in range(N_CHIPS):
            dzj = dzm_ref[:, j * IN_SHARD:(j + 1) * IN_SHARD].astype(BF16)
            dzb_ref[j] = dzj
            da = da + _dot_nt(dzj, win_ref[j])
        dg1_ref[...] = _rows8(da * xh)
        dmeta_ref[...] = _rms_bwd(da, xh, r, g1v)

    vm = pl.BlockSpec(memory_space=pltpu.VMEM)
    return pl.pallas_call(
        body, name="meta_bwd",
        out_shape=[jax.ShapeDtypeStruct((N_META, D_MODEL), F32), jax.ShapeDtypeStruct((8, D_MODEL), F32),
                   jax.ShapeDtypeStruct((N_META, D_MODEL), BF16), jax.ShapeDtypeStruct((N_CHIPS, N_META, IN_SHARD), BF16)],
        in_specs=[vm] * 4, out_specs=[vm] * 4,
    )(dzm, meta_full, g1, win_all)


def _mixer_weight_grads(a, dz, ycat, dm, a_meta, dz_meta, ffn_sums, small):
    n_rows = a.shape[0]
    tk = min(TK_DW, n_rows)
    n_k = n_rows // tk
    n_sc, n_sm = len(ffn_sums), _AllReduceSmall.N_IN

    def body(a_ref, dz_ref, yc_ref, dm_ref, am_ref, dzm_ref, *rest):
        ins, outs, scratch = rest[:n_sc + n_sm], rest[n_sc + n_sm:2 * n_sc + n_sm + 5], rest[2 * n_sc + n_sm + 5:]
        dwin_ref, dwout_ref = outs[:2]
        scatter = _ScatterToChips(ins[:n_sc], outs[2:2 + n_sc], *scratch[:2])
        reduce_small = _AllReduceSmall(ins[n_sc:], outs[2 + n_sc:], scratch[2:])
        k = pl.program_id(0)

        @pl.when(k == 0)
        def _():
            scatter.start()
            reduce_small.pack_and_send()
            am_t = am_ref[...].T
            for j in range(N_CHIPS):
                dwin_ref[j] = _dot(am_t, dzm_ref[j])
            dwout_ref[...] = jnp.zeros_like(dwout_ref)

        for st in range(2):
            @pl.when(k == ((st + 1) * n_k) // 3)
            def _():
                reduce_small.combine(st)

        a_t = a_ref[...].T
        for j in range(N_CHIPS):
            dwin_ref[j] += _dot(a_t, dz_ref[j])
        dwout_ref[...] += _dot_tn(yc_ref[...], dm_ref[...])

        @pl.when(k == n_k - 1)
        def _():
            reduce_small.combine(2)
            scatter.finish()

    row = pl.BlockSpec((tk, D_MODEL), lambda k: (k, 0))
    outs = pl.pallas_call(
        body, name="mixer_weight_grads", grid=(n_k,),
        out_shape=[jax.ShapeDtypeStruct((N_CHIPS, D_MODEL, IN_SHARD), F32),
                   jax.ShapeDtypeStruct((D_MODEL, D_MODEL), F32)] + _ScatterToChips.out_shape(ffn_sums)
        + _AllReduceSmall.out_shape(),
        in_specs=[row, pl.BlockSpec((N_CHIPS, tk, IN_SHARD), lambda k: (0, k, 0)), row, row,
                  _full((N_META, D_MODEL)), _full((N_CHIPS, N_META, IN_SHARD))] + [ANY] * n_sc
        + [_full(s.shape) for s in small],
        out_specs=[_full((N_CHIPS, D_MODEL, IN_SHARD)), _full((D_MODEL, D_MODEL))] + [ANY] * n_sc
        + [_full(s) for s in _AllReduceSmall.SHAPES],
        scratch_shapes=_ScatterToChips.scratch(n_sc) + _AllReduceSmall.scratch(),
        compiler_params=_cparams(1),
    )(a, dz, ycat, dm, a_meta, dz_meta, *ffn_sums, *small)
    return ([outs[0], outs[1].reshape(N_CHIPS, OUT_SHARD, D_MODEL)], outs[2:2 + n_sc], outs[2 + n_sc:])


def kernel(x, meta_tokens, norm_mix_pre, w_in, conv_w, pool_w, pool_scale, w_out, norm_mix_post, norm_ffn_pre, w_gate, w_up, w_down, norm_ffn_post, loss_target, m_meta_tokens, m_norm_mix_pre, m_w_in, m_conv_w, m_pool_w, m_pool_scale, m_w_out, m_norm_mix_post, m_norm_ffn_pre, m_w_gate, m_w_up, m_w_down, m_norm_ffn_post, v_meta_tokens, v_norm_mix_pre, v_w_in, v_conv_w, v_pool_w, v_pool_scale, v_w_out, v_norm_mix_post, v_norm_ffn_pre, v_w_gate, v_w_up, v_w_down, v_norm_ffn_post):
    n_seq, seq, _ = x.shape
    n_rows = n_seq * seq
    chip = 2 * lax.axis_index("x") + lax.axis_index("y")
    meta_cols = D_MODEL // N_CHIPS
    conv_cols = D_CONV // N_CHIPS

    small = jnp.zeros((2 * HALO, meta_cols), F32)
    small = small.at[0:N_META, :].set(meta_tokens).at[N_META:N_META + 3, 0:conv_cols].set(conv_w[0])
    win_all, wout_all, small_all = _all_gather_shards([w_in[0].astype(BF16), w_out[0].astype(BF16), small])
    meta_full = small_all[:, 0:N_META, :].transpose(1, 0, 2).reshape(N_META, D_MODEL)
    conv_full = small_all[:, N_META:N_META + 3, 0:conv_cols].transpose(1, 0, 2).reshape(3, D_CONV)
    wout_full = wout_all.reshape(D_MODEL, D_MODEL)
    poolw_bf = pool_w[0].astype(BF16)
    pscale = pool_scale
    g1, g2, g3, g4 = norm_mix_pre, norm_mix_post, norm_ffn_pre, norm_ffn_post
    place = jnp.stack([chip, lax.axis_index("c")]).astype(jnp.int32)

    zmeta = _meta_fwd(meta_full, g1, win_all)
    (z3, m3, h1), ffn_w = _mixer_fwd(
        x, zmeta, g1, g2, conv_full, poolw_bf, pscale, win_all, wout_full,
        [w_gate[0].T.astype(BF16), w_up[0].T.astype(BF16), w_down[0].astype(BF16)])
    wg_t, wu_t, wd_full = [w.reshape(D_FF, D_MODEL) for w in ffn_w]
    dh1, f_bf, dd_bf, ds_bf, du_bf, gg_bf, lossp, dg3p, dg4p = _ffn_fwd_bwd(
        h1.reshape(n_rows, D_MODEL), loss_target.reshape(n_rows, D_MODEL), g3, g4, wg_t, wu_t, wd_full)
    as_shards = lambda g: g.reshape(N_CHIPS, FF_SHARD, D_MODEL)
    (dwd,), _ = _ffn_weight_grads("ffn_weight_grads_down", [gg_bf], dd_bf, [])
    dwd = as_shards(dwd)
    (dwg_t, dwu_t), (dwd_recv,) = _ffn_weight_grads("ffn_weight_grads_gate_up", [ds_bf, du_bf], f_bf, [dwd])
    dwg_t, dwu_t = as_shards(dwg_t), as_shards(dwu_t)
    ((grad_x, dz_bf, a_bf, yc_bf, dm_bf, dg1p, dg2p, dscp, dcwp, dpw, dzm), (dwg_recv, dwu_recv),
     (dwd_rbuf,)) = _mixer_bwd(
        dh1.reshape(n_seq, seq, D_MODEL), m3, z3, x, zmeta, g1, g2, conv_full, poolw_bf, pscale, win_all, wout_full,
        [dwg_t, dwu_t], [_add_pairs(dwd, dwd_recv, place)])
    dmeta, dg1m, a_meta, dz_meta = _meta_bwd(dzm, meta_full, g1, win_all)
    mix_grads, (dwg_rbuf, dwu_rbuf), (a_red, b_red, c_red) = _mixer_weight_grads(
        a_bf, dz_bf, yc_bf, dm_bf, a_meta, dz_meta,
        [_add_pairs(dwg_t, dwg_recv, place), _add_pairs(dwu_t, dwu_recv, place)],
        [dg1p, dg1m, dg2p, dg3p, dg4p, lossp, dmeta, dscp, dcwp, dpw.reshape(SMALL_C_ROWS, POOL_GROUP)])

    mix_recvs = _exchange_halves(mix_grads)
    mix_rbufs = _scatter_to_chips([_add_pairs(g, r, place) for g, r in zip(mix_grads, mix_recvs)])
    grads = list(mix_grads) + [dwg_t, dwu_t, dwd]
    recvs = list(mix_recvs) + [dwg_recv, dwu_recv, dwd_recv]
    rbufs = list(mix_rbufs) + [dwg_rbuf, dwu_rbuf, dwd_rbuf]
    reduced = _gather_halves([_add_chips(g, r, rb, place) for g, r, rb in zip(grads, recvs, rbufs)])
    g_win, g_wout, g_wg_t, g_wu_t, g_wd = [r.reshape(2 * r.shape[1], r.shape[2]) for r in reduced]

    loss = a_red[4, 0]
    g_g1, g_g2, g_g3, g_g4 = a_red[0:1], a_red[1:2], a_red[2:3], a_red[3:4]
    g_meta = lax.dynamic_slice(a_red, (8, chip * meta_cols), (N_META, meta_cols))
    g_pscale = b_red[0:1]
    g_conv = lax.dynamic_slice(b_red, (1, chip * conv_cols), (3, conv_cols))
    g_poolw = c_red

    big = [(w_in[0], g_win, m_w_in[0], v_w_in[0]), (w_out[0], g_wout, m_w_out[0], v_w_out[0]),
           (w_gate[0].T, g_wg_t, m_w_gate[0].T, v_w_gate[0].T), (w_up[0].T, g_wu_t, m_w_up[0].T, v_w_up[0].T),
           (w_down[0], g_wd, m_w_down[0], v_w_down[0])]
    big_out = [_adamw_big(w, g, m, v) for (w, g, m, v) in big]
    big_out[2] = [o.T for o in big_out[2]]
    big_out[3] = [o.T for o in big_out[3]]
    g_wg, g_wu = g_wg_t.T, g_wu_t.T
    small_groups = [
        (meta_tokens, g_meta, m_meta_tokens, v_meta_tokens),
        (g1, g_g1, m_norm_mix_pre, v_norm_mix_pre),
        (conv_w[0], g_conv, m_conv_w[0], v_conv_w[0]),
        (pool_w.reshape(SMALL_C_ROWS, POOL_GROUP), g_poolw, m_pool_w.reshape(SMALL_C_ROWS, POOL_GROUP),
         v_pool_w.reshape(SMALL_C_ROWS, POOL_GROUP)),
        (pool_scale, g_pscale, m_pool_scale, v_pool_scale),
        (g2, g_g2, m_norm_mix_post, v_norm_mix_post),
        (g3, g_g3, m_norm_ffn_pre, v_norm_ffn_pre),
        (g4, g_g4, m_norm_ffn_post, v_norm_ffn_post),
    ]
    small_out = _adamw_small(small_groups)

    grads_out = [g_meta, g_g1, g_win[None], g_conv[None], g_poolw.reshape(pool_w.shape), g_pscale, g_wout[None],
                 g_g2, g_g3, g_wg[None], g_wu[None], g_wd[None], g_g4]
    s_meta, s_g1, s_conv, s_poolw, s_pscale, s_g2, s_g3, s_g4 = small_out
    b_win, b_wout, b_wg, b_wu, b_wd = big_out

    def leaf(k):
        return [s_meta[k], s_g1[k], b_win[k][None], s_conv[k][None], s_poolw[k].reshape(pool_w.shape), s_pscale[k],
                b_wout[k][None], s_g2[k], s_g3[k], b_wg[k][None], b_wu[k][None], b_wd[k][None], s_g4[k]]

    return (loss, grad_x, *grads_out, *leaf(0), *leaf(1), *leaf(2))
```

```python
import jax
import jax.numpy as jnp
from jax import lax
from jax.experimental import pallas as pl
from jax.experimental.pallas import tpu as pltpu

F32 = jnp.float32
BF16 = jnp.bfloat16
MESH = pl.DeviceIdType.MESH

D_MODEL = 1024
D_CONV = 512
D_POOL = 512
POOL_GROUP = 128
N_POOL_GROUPS = 4
D_IN_PROJ = 2048
D_FF = 2816
N_CHIPS = 4
FF_SHARD = D_FF // N_CHIPS
IN_SHARD = D_IN_PROJ // N_CHIPS
OUT_SHARD = D_MODEL // N_CHIPS
N_META = 16
HALO = 16
RMS_EPS = 1e-6

ADAM_LR = 0.001
ADAM_B1 = 0.9
ADAM_B2 = 0.999
ADAM_EPS = 1e-08
ADAM_WD = 0.01
ADAM_STEP = 10

TM_MIX_FWD = 512
TM_MIX_BWD = 512
TM_FFN = 256
TK_DW = 1024
FF_CHUNK = 1024
VMEM_LIMIT = 56 * 1024 * 1024


def _cparams(n_grid):
    return pltpu.CompilerParams(dimension_semantics=("arbitrary",) * n_grid, vmem_limit_bytes=VMEM_LIMIT)


def _dot(a, b):
    return jnp.dot(a, b, preferred_element_type=F32)


def _dot_nt(a, b):
    return lax.dot_general(a, b, (((1,), (1,)), ((), ())), preferred_element_type=F32)


def _dot_tn(a, b):
    return lax.dot_general(a, b, (((0,), (0,)), ((), ())), preferred_element_type=F32)


def _rows8(v):
    r, c = v.shape
    return v.reshape(r // 8, 8, c).sum(axis=0)


def _rstd(v):
    return lax.rsqrt(jnp.mean(v * v, axis=-1, keepdims=True) + RMS_EPS)


def _rms_bwd(dy, xhat, rstd, gain):
    dyg = dy * gain
    return rstd * (dyg - xhat * jnp.mean(dyg * xhat, axis=-1, keepdims=True))


def _sigmoid(v):
    return 1.0 / (1.0 + jnp.exp(-v))


def _gcols(g):
    return slice(g * POOL_GROUP, (g + 1) * POOL_GROUP)


def _window_sum(e, g, ahead):
    n = e.shape[0]
    w = e
    for level in range(g + 1):
        shift = 1 << level
        w = w + pltpu.roll(w, (n - shift) if ahead else shift, 0)
    return w


def _pool_fwd(pb, g, n):
    e = pb[0:HALO + n, _gcols(g)]
    return _window_sum(e, g, False)[HALO:, :] * (1.0 / (2 << g)) - e[HALO:, :]


def _pool_bwd(qb, g, n):
    e = qb[0:n + HALO, _gcols(g)]
    return _window_sum(e, g, True)[0:n, :] * (1.0 / (2 << g)) - e[0:n, :]


def _full(shape):
    nd = len(shape)
    return pl.BlockSpec(shape, lambda *_: (0,) * nd)


ANY = pl.BlockSpec(memory_space=pl.ANY)


def _mesh_pos():
    x, y, c = lax.axis_index("x"), lax.axis_index("y"), lax.axis_index("c")
    chips = [(1 - x, y), (x, 1 - y), (1 - x, 1 - y)]
    return x, y, c, chips


def _half(ref, h):
    hr = ref.shape[0] // 2
    return ref.at[pl.ds(h * hr, hr), :]


class _AllGather:
    def __init__(self, ins, outs, send_sems, recv_sems):
        self.ins, self.outs, self.send_sems, self.recv_sems = ins, outs, send_sems, recv_sems
        self.n = len(ins)

    @staticmethod
    def scratch(n):
        return [pltpu.SemaphoreType.DMA((6 * n,)), pltpu.SemaphoreType.DMA((6 * n,))]

    @staticmethod
    def out_shape(shards):
        return [jax.ShapeDtypeStruct((N_CHIPS,) + s.shape, s.dtype) for s in shards]

    def _copy(self, a, k, src, dst, to):
        return pltpu.make_async_remote_copy(src_ref=src, dst_ref=dst, send_sem=self.send_sems.at[6 * a + k],
                                            recv_sem=self.recv_sems.at[6 * a + k], device_id=to, device_id_type=MESH)

    def _ici(self, a, k):
        x, y, c, chips = _mesh_pos()
        return self._copy(a, k, _half(self.ins[a], c), _half(self.outs[a].at[2 * x + y], c), (*chips[k], c))

    def _d2d(self, a, k, h):
        x, y, c, chips = _mesh_pos()
        slot = _half(self.outs[a].at[2 * chips[k][0] + chips[k][1]], h)
        return self._copy(a, 3 + k, slot, slot, (x, y, 1 - c))

    def start(self):
        for a in range(self.n):
            for k in range(3):
                self._ici(a, k).start()

    def forward(self, a):
        c = lax.axis_index("c")
        for k in range(3):
            self._ici(a, k).wait_recv()
            self._d2d(a, k, c).start()

    def finish(self):
        c = lax.axis_index("c")
        for a in range(self.n):
            for k in range(3):
                self._d2d(a, k, 1 - c).wait_recv()
        for a in range(self.n):
            for k in range(3):
                self._ici(a, k).wait_send()
                self._d2d(a, k, c).wait_send()


def _fill_own_slot(gathered, shards):
    chip = 2 * lax.axis_index("x") + lax.axis_index("y")
    return [lax.dynamic_update_slice(o, s[None], (chip, 0, 0)) for o, s in zip(gathered, shards)]


def _all_gather_shards(shards):
    n = len(shards)

    def body(*refs):
        ag = _AllGather(refs[:n], refs[n:2 * n], *refs[2 * n:])
        ag.start()
        for a in range(n):
            ag.forward(a)
        ag.finish()

    outs = pl.pallas_call(
        body, name="all_gather_weights", out_shape=_AllGather.out_shape(shards),
        in_specs=[ANY] * n, out_specs=[ANY] * n, scratch_shapes=_AllGather.scratch(n),
    )(*shards)
    return _fill_own_slot(outs, shards)


class _ExchangeHalves:
    def __init__(self, ins, recvs, send_sems, recv_sems):
        self.ins, self.recvs, self.send_sems, self.recv_sems = ins, recvs, send_sems, recv_sems

    @staticmethod
    def scratch(n):
        return [pltpu.SemaphoreType.DMA((n,)), pltpu.SemaphoreType.DMA((n,))]

    @staticmethod
    def out_shape(grads):
        return [jax.ShapeDtypeStruct((g.shape[0], g.shape[1] // 2, g.shape[2]), g.dtype) for g in grads]

    def _copies(self):
        x, y, c, _ = _mesh_pos()
        out = []
        for a, (src, dst) in enumerate(zip(self.ins, self.recvs)):
            hr = src.shape[1] // 2
            out.append(pltpu.make_async_remote_copy(
                src_ref=src.at[:, pl.ds((1 - c) * hr, hr), :], dst_ref=dst, send_sem=self.send_sems.at[a],
                recv_sem=self.recv_sems.at[a], device_id=(x, y, 1 - c), device_id_type=MESH))
        return out

    def start(self):
        for cp in self._copies():
            cp.start()

    def finish(self):
        for cp in self._copies():
            cp.wait()


def _exchange_halves(grads):
    n = len(grads)

    def body(*refs):
        ex = _ExchangeHalves(refs[:n], refs[n:2 * n], *refs[2 * n:])
        ex.start()
        ex.finish()

    return pl.pallas_call(
        body, name="grad_exchange_halves", out_shape=_ExchangeHalves.out_shape(grads),
        in_specs=[ANY] * n, out_specs=[ANY] * n, scratch_shapes=_ExchangeHalves.scratch(n),
    )(*grads)


class _ScatterToChips:
    def __init__(self, ins, rbufs, send_sems, recv_sems):
        self.ins, self.rbufs, self.send_sems, self.recv_sems = ins, rbufs, send_sems, recv_sems

    @staticmethod
    def scratch(n):
        return [pltpu.SemaphoreType.DMA((3 * n,)), pltpu.SemaphoreType.DMA((3 * n,))]

    @staticmethod
    def out_shape(sums):
        return [jax.ShapeDtypeStruct((3,) + s.shape[1:], BF16) for s in sums]

    def _copies(self):
        x, y, c, chips = _mesh_pos()
        out = []
        for a, (src, dst) in enumerate(zip(self.ins, self.rbufs)):
            for k, chip in enumerate(chips):
                out.append(pltpu.make_async_remote_copy(
                    src_ref=src.at[2 * chip[0] + chip[1]], dst_ref=dst.at[k], send_sem=self.send_sems.at[3 * a + k],
                    recv_sem=self.recv_sems.at[3 * a + k], device_id=(*chip, c), device_id_type=MESH))
        return out

    def start(self):
        for cp in self._copies():
            cp.start()

    def finish(self):
        for cp in self._copies():
            cp.wait()


def _scatter_to_chips(sums_bf16):
    n = len(sums_bf16)

    def body(*refs):
        sc = _ScatterToChips(refs[:n], refs[n:2 * n], *refs[2 * n:])
        sc.start()
        sc.finish()

    return pl.pallas_call(
        body, name="grad_scatter_to_chips", out_shape=_ScatterToChips.out_shape(sums_bf16),
        in_specs=[ANY] * n, out_specs=[ANY] * n, scratch_shapes=_ScatterToChips.scratch(n),
    )(*sums_bf16)


def _gather_halves(halves):
    n = len(halves)

    def body(*refs):
        ins, outs = refs[:n], refs[n:2 * n]
        send_sems, recv_sems = refs[2 * n:]
        x, y, c, _ = _mesh_pos()
        sib = (x, y, 1 - c)
        remote = [pltpu.make_async_remote_copy(src_ref=ins[a].at[c], dst_ref=outs[a].at[c],
                                               send_sem=send_sems.at[a], recv_sem=recv_sems.at[a],
                                               device_id=sib, device_id_type=MESH) for a in range(n)]
        for cp in remote:
            cp.start()
        for a in range(n):
            pltpu.make_async_remote_copy(src_ref=ins[a].at[1 - c], dst_ref=outs[a].at[1 - c], send_sem=send_sems.at[a],
                                         recv_sem=recv_sems.at[a], device_id=sib, device_id_type=MESH).wait_recv()
        for cp in remote:
            cp.wait_send()

    return pl.pallas_call(
        body, name="grad_gather_halves",
        out_shape=[jax.ShapeDtypeStruct(h.shape, F32) for h in halves],
        in_specs=[ANY] * n, out_specs=[ANY] * n, input_output_aliases={a: a for a in range(n)},
        scratch_shapes=[pltpu.SemaphoreType.DMA((n,)), pltpu.SemaphoreType.DMA((n,))],
    )(*halves)


SMALL_A_ROWS = 24
SMALL_B_ROWS = 8
SMALL_C_ROWS = N_POOL_GROUPS * POOL_GROUP


class _AllReduceSmall:
    N_IN = 10
    SHAPES = [(SMALL_A_ROWS, D_MODEL), (SMALL_B_ROWS, D_CONV), (SMALL_C_ROWS, POOL_GROUP)]

    def __init__(self, ins, outs, scratch):
        self.ins, self.outs = ins, outs
        self.bufs, self.rcvs, self.send_sems, self.recv_sems = scratch[:3], scratch[3:6], scratch[6], scratch[7]

    @classmethod
    def scratch(cls):
        return ([pltpu.VMEM((3,) + s, F32) for s in cls.SHAPES] + [pltpu.VMEM((3,) + s, F32) for s in cls.SHAPES]
                + [pltpu.SemaphoreType.DMA((9,)), pltpu.SemaphoreType.DMA((9,))])

    @classmethod
    def out_shape(cls):
        return [jax.ShapeDtypeStruct(s, F32) for s in cls.SHAPES]

    def _copies(self, st):
        x, y, c, _ = _mesh_pos()
        peer = [(x, y, 1 - c), (1 - x, y, c), (x, 1 - y, c)][st]
        return [pltpu.make_async_remote_copy(
            src_ref=buf.at[st], dst_ref=rcv.at[st], send_sem=self.send_sems.at[3 * st + i],
            recv_sem=self.recv_sems.at[3 * st + i], device_id=peer, device_id_type=MESH)
            for i, (buf, rcv) in enumerate(zip(self.bufs, self.rcvs))]

    def pack_and_send(self):
        dg1_ref, dg1m_ref, dg2_ref, dg3_ref, dg4_ref, loss_ref, dmeta_ref, dsc_ref, dcw_ref, dpw_ref = self.ins
        a_buf, b_buf, c_buf = self.bufs

        def rowsum(v):
            return jnp.sum(v, axis=0, keepdims=True)

        a_buf[0, 0:1, :] = rowsum(dg1_ref[...] + dg1m_ref[...])
        a_buf[0, 1:2, :] = rowsum(dg2_ref[...])
        a_buf[0, 2:3, :] = rowsum(dg3_ref[...])
        a_buf[0, 3:4, :] = rowsum(dg4_ref[...])
        loss = jnp.sum(rowsum(loss_ref[...]), axis=1, keepdims=True) * (0.5 / D_MODEL)
        a_buf[0, 4:5, :] = jnp.broadcast_to(loss, (1, D_MODEL))
        a_buf[0, 5:8, :] = jnp.zeros((3, D_MODEL), F32)
        a_buf[0, 8:24, :] = dmeta_ref[...]
        b_buf[0, 0:1, :] = rowsum(dsc_ref[...])
        for k in range(3):
            b_buf[0, 1 + k:2 + k, :] = rowsum(dcw_ref[8 * k:8 * k + 8, :])
        b_buf[0, 4:8, :] = jnp.zeros((4, D_CONV), F32)
        c_buf[0] = dpw_ref[...]
        for cp in self._copies(0):
            cp.start()

    def combine(self, st):
        for cp in self._copies(st):
            cp.wait()
        if st < 2:
            for buf, rcv in zip(self.bufs, self.rcvs):
                buf[st + 1] = buf[st] + rcv[st]
            for cp in self._copies(st + 1):
                cp.start()
        else:
            for out, buf, rcv in zip(self.outs, self.bufs, self.rcvs):
                out[...] = buf[st] + rcv[st]


def _row_block(rows):
    for cand in (512, 448, 384, 352, 320, 256, 128, 64, 32, 16):
        if rows % cand == 0:
            return cand
    return rows


def _add_pairs(grad, recv, place):
    n_sh, rows2, cols = grad.shape
    hr = rows2 // 2
    br = _row_block(hr)

    def body(place_ref, a_ref, b_ref, o_ref):
        o_ref[...] = (a_ref[0] + b_ref[...]).astype(BF16)

    return pl.pallas_call(
        body, name="grad_add_pairs",
        grid_spec=pltpu.PrefetchScalarGridSpec(
            num_scalar_prefetch=1, grid=(n_sh, hr // br),
            in_specs=[pl.BlockSpec((1, 1, br, cols), lambda j, i, p: (j, p[1], i, 0)),
                      pl.BlockSpec((1, br, cols), lambda j, i, p: (j, i, 0))],
            out_specs=pl.BlockSpec((1, br, cols), lambda j, i, p: (j, i, 0))),
        out_shape=jax.ShapeDtypeStruct((n_sh, hr, cols), BF16), compiler_params=_cparams(2),
    )(place, grad.reshape(n_sh, 2, hr, cols), recv)


def _add_chips(grad, recv, rbuf, place):
    n_sh, rows2, cols = grad.shape
    hr = rows2 // 2
    br = _row_block(hr)

    def body(place_ref, a_ref, b_ref, r_ref, o_ref):
        own = a_ref[0, 0] + b_ref[0]
        o_ref[0] = ((own + r_ref[0].astype(F32)) + r_ref[1].astype(F32)) + r_ref[2].astype(F32)

    return pl.pallas_call(
        body, name="grad_add_chips",
        grid_spec=pltpu.PrefetchScalarGridSpec(
            num_scalar_prefetch=1, grid=(hr // br,),
            in_specs=[pl.BlockSpec((1, 1, br, cols), lambda i, p: (p[0], p[1], i, 0)),
                      pl.BlockSpec((1, br, cols), lambda i, p: (p[0], i, 0)),
                      pl.BlockSpec((3, br, cols), lambda i, p: (0, i, 0))],
            out_specs=pl.BlockSpec((1, br, cols), lambda i, p: (p[1], i, 0))),
        out_shape=jax.ShapeDtypeStruct((2, hr, cols), F32), compiler_params=_cparams(1),
    )(place, grad.reshape(n_sh, 2, hr, cols), recv, rbuf)


def _adamw_math(w, g, m, v):
    m2 = ADAM_B1 * m + (1.0 - ADAM_B1) * g
    v2 = ADAM_B2 * v + (1.0 - ADAM_B2) * (g * g)
    m_hat = m2 / (1.0 - ADAM_B1 ** ADAM_STEP)
    v_hat = v2 / (1.0 - ADAM_B2 ** ADAM_STEP)
    delta = -ADAM_LR * (m_hat / (jnp.sqrt(v_hat) + ADAM_EPS) + ADAM_WD * w)
    return delta, m2, v2


def _adamw_big(w, g, m, v):
    rows, cols = w.shape
    br = _row_block(rows)

    def body(w_ref, g_ref, m_ref, v_ref, d_ref, m2_ref, v2_ref):
        d, m2, v2 = _adamw_math(w_ref[...], g_ref[...], m_ref[...], v_ref[...])
        d_ref[...] = d
        m2_ref[...] = m2
        v2_ref[...] = v2

    spec = pl.BlockSpec((br, cols), lambda i: (i, 0))
    return pl.pallas_call(
        body, name="adamw_big", grid=(rows // br,),
        out_shape=[jax.ShapeDtypeStruct((rows, cols), F32)] * 3,
        in_specs=[spec] * 4, out_specs=[spec] * 3, compiler_params=_cparams(1),
    )(w, g, m, v)


def _adamw_small(groups):
    n = len(groups)

    def body(*refs):
        ins, outs = refs[:4 * n], refs[4 * n:]
        for i in range(n):
            w, g, m, v = (r[...] for r in ins[4 * i:4 * i + 4])
            d, m2, v2 = _adamw_math(w, g, m, v)
            outs[3 * i][...] = d
            outs[3 * i + 1][...] = m2
            outs[3 * i + 2][...] = v2

    vm = pl.BlockSpec(memory_space=pltpu.VMEM)
    flat = [a for grp in groups for a in grp]
    out_shape = [jax.ShapeDtypeStruct(grp[0].shape, F32) for grp in groups for _ in range(3)]
    outs = pl.pallas_call(body, name="adamw_small", out_shape=out_shape,
                          in_specs=[vm] * (4 * n), out_specs=[vm] * (3 * n))(*flat)
    return [tuple(outs[3 * i:3 * i + 3]) for i in range(n)]


def _load_weights(pairs, sem):
    for src, dst in pairs:
        cp = pltpu.make_async_copy(src, dst, sem)
        cp.start()
        cp.wait()


def _meta_fwd(meta_full, g1, win_all):
    def body(meta_ref, g1_ref, win_ref, z_ref):
        xm = meta_ref[...]
        a = (xm * _rstd(xm) * g1_ref[...]).astype(BF16)
        for j in range(N_CHIPS):
            z_ref[:, j * IN_SHARD:(j + 1) * IN_SHARD] = _dot(a, win_ref[j])

    vm = pl.BlockSpec(memory_space=pltpu.VMEM)
    return pl.pallas_call(body, name="meta_fwd", out_shape=jax.ShapeDtypeStruct((N_META, D_IN_PROJ), F32),
                          in_specs=[vm] * 3, out_specs=vm)(meta_full, g1, win_all)


def _mixer_fwd(x3, zmeta, g1, g2, convw, poolw, pscale, win_all, wout, ffn_shards):
    n_seq, seq, _ = x3.shape
    tm = min(TM_MIX_FWD, seq)
    n_t = seq // tm
    n_steps = n_seq * n_t
    n_ag = len(ffn_shards)

    def body(x_ref, zm_ref, g1_ref, g2_ref, cw_ref, pw_ref, ps_ref, win_hbm, wout_hbm, *rest):
        ag = _AllGather(rest[:n_ag], rest[n_ag + 3:2 * n_ag + 3], *rest[-2:])
        z_ref, m_ref, h1_ref = rest[n_ag:n_ag + 3]
        win_v, wout_v, cvb, pb, sem = rest[2 * n_ag + 3:-2]
        s, t = pl.program_id(0), pl.program_id(1)
        step = s * n_t + t

        @pl.when(step == 0)
        def _():
            ag.start()
            _load_weights([(win_hbm, win_v), (wout_hbm, wout_v)], sem)

        for a in range(n_ag):
            @pl.when(step == min(((a + 1) * n_steps) // n_ag, n_steps - 1))
            def _():
                ag.forward(a)

        xt = x_ref[0]
        a = (xt * _rstd(xt) * g1_ref[...]).astype(BF16)
        zb = _dot(a, win_v[0])
        zc = _dot(a, win_v[1])
        zv = _dot(a, win_v[2])
        zp = _dot(a, win_v[3])
        z_ref[0, :, 0:IN_SHARD] = zb
        z_ref[0, :, IN_SHARD:2 * IN_SHARD] = zc
        z_ref[0, :, 2 * IN_SHARD:3 * IN_SHARD] = zv
        z_ref[0, :, 3 * IN_SHARD:4 * IN_SHARD] = zp

        @pl.when(t == 0)
        def _():
            cvb[0:HALO, :] = zm_ref[:, IN_SHARD:2 * IN_SHARD] * zm_ref[:, 2 * IN_SHARD:3 * IN_SHARD]
            pb[0:HALO, :] = zm_ref[:, 3 * IN_SHARD:4 * IN_SHARD]

        @pl.when(t > 0)
        def _():
            cvb[0:HALO, :] = cvb[tm:tm + HALO, :]
            pb[0:HALO, :] = pb[tm:tm + HALO, :]

        cv = zc * zv
        cvb[HALO:HALO + tm, :] = cv
        pb[HALO:HALO + tm, :] = zp
        cw = cw_ref[...]
        conv = cw[0:1] * cvb[HALO - 2:HALO - 2 + tm, :] + cw[1:2] * cvb[HALO - 1:HALO - 1 + tm, :] + cw[2:3] * cv
        parts = [(zb * conv).astype(BF16)]
        for g in range(N_POOL_GROUPS):
            pooled = _pool_fwd(pb, g, tm).astype(BF16)
            parts.append((_dot(pooled, pw_ref[g]) * ps_ref[:, _gcols(g)]).astype(BF16))
        m = _dot(jnp.concatenate(parts, axis=1), wout_v[...])
        m_ref[0] = m
        h1_ref[0] = xt + m * _rstd(m) * g2_ref[...]

        @pl.when(step == n_steps - 1)
        def _():
            ag.finish()

    row = lambda c: pl.BlockSpec((1, tm, c), lambda s, t: (s, t, 0))
    outs = pl.pallas_call(
        body, name="mixer_fwd", grid=(n_seq, n_t),
        out_shape=[jax.ShapeDtypeStruct((n_seq, seq, D_IN_PROJ), F32), jax.ShapeDtypeStruct((n_seq, seq, D_MODEL), F32),
                   jax.ShapeDtypeStruct((n_seq, seq, D_MODEL), F32)] + _AllGather.out_shape(ffn_shards),
        in_specs=[row(D_MODEL), _full((N_META, D_IN_PROJ)), _full((1, D_MODEL)), _full((1, D_MODEL)),
                  _full((3, D_CONV)), _full((N_POOL_GROUPS, POOL_GROUP, POOL_GROUP)), _full((1, D_POOL)), ANY, ANY]
        + [ANY] * n_ag,
        out_specs=[row(D_IN_PROJ), row(D_MODEL), row(D_MODEL)] + [ANY] * n_ag,
        scratch_shapes=[pltpu.VMEM((N_CHIPS, D_MODEL, IN_SHARD), BF16), pltpu.VMEM((D_MODEL, D_MODEL), BF16),
                        pltpu.VMEM((HALO + tm, D_CONV), F32), pltpu.VMEM((HALO + tm, D_POOL), F32),
                        pltpu.SemaphoreType.DMA] + _AllGather.scratch(n_ag),
        compiler_params=_cparams(2),
    )(x3, zmeta, g1, g2, convw, poolw, pscale, win_all, wout, *ffn_shards)
    return outs[:3], _fill_own_slot(outs[3:], ffn_shards)


def _ffn_chunks():
    out, r0 = [], 0
    while r0 < D_FF:
        out.append((r0, min(FF_CHUNK, D_FF - r0)))
        r0 += FF_CHUNK
    return out


def _ffn_fwd_bwd(h1, target, g3, g4, wg_t, wu_t, wd):
    n_rows = h1.shape[0]
    tm = min(TM_FFN, n_rows)
    chunks = _ffn_chunks()

    def body(h1_ref, t_ref, g3_ref, g4_ref, wg_hbm, wu_hbm, wd_hbm,
             dh1_ref, f_ref, dd_ref, ds_ref, du_ref, gg_ref, loss_ref, dg3_ref, dg4_ref,
             wg_v, wu_v, wd_v, s_sc, u_sc, sem):
        @pl.when(pl.program_id(0) == 0)
        def _():
            _load_weights([(wg_hbm, wg_v), (wu_hbm, wu_v), (wd_hbm, wd_v)], sem)
            loss_ref[...] = jnp.zeros_like(loss_ref)
            dg3_ref[...] = jnp.zeros_like(dg3_ref)
            dg4_ref[...] = jnp.zeros_like(dg4_ref)

        h1v = h1_ref[...]
        r3 = _rstd(h1v)
        hh = h1v * r3
        g3v, g4v = g3_ref[...], g4_ref[...]
        f = (hh * g3v).astype(BF16)
        f_ref[...] = f
        d = jnp.zeros((tm, D_MODEL), F32)
        for r0, sz in chunks:
            s = _dot_nt(f, wg_v[r0:r0 + sz, :])
            u = _dot_nt(f, wu_v[r0:r0 + sz, :])
            s_sc[:, r0:r0 + sz] = s
            u_sc[:, r0:r0 + sz] = u
            gc = (s * _sigmoid(s) * u).astype(BF16)
            gg_ref[:, r0:r0 + sz] = gc
            d = d + _dot(gc, wd_v[r0:r0 + sz, :])
        r4 = _rstd(d)
        dh = d * r4
        err = (h1v + dh * g4v) - t_ref[...]
        loss_ref[...] += _rows8(err * err)
        dy = err * (1.0 / D_MODEL)
        dg4_ref[...] += _rows8(dy * dh)
        ddb = _rms_bwd(dy, dh, r4, g4v).astype(BF16)
        dd_ref[...] = ddb
        df = jnp.zeros((tm, D_MODEL), F32)
        for r0, sz in chunks:
            dgg = _dot_nt(ddb, wd_v[r0:r0 + sz, :])
            s = s_sc[:, r0:r0 + sz]
            u = u_sc[:, r0:r0 + sz]
            sig = _sigmoid(s)
            dsc = (dgg * u * (sig * (1.0 + s * (1.0 - sig)))).astype(BF16)
            duc = (dgg * (s * sig)).astype(BF16)
            ds_ref[:, r0:r0 + sz] = dsc
            du_ref[:, r0:r0 + sz] = duc
            df = df + _dot(dsc, wg_v[r0:r0 + sz, :]) + _dot(duc, wu_v[r0:r0 + sz, :])
        dg3_ref[...] += _rows8(df * hh)
        dh1_ref[...] = dy + _rms_bwd(df, hh, r3, g3v)

    row = pl.BlockSpec((tm, D_MODEL), lambda i: (i, 0))
    ffrow = pl.BlockSpec((tm, D_FF), lambda i: (i, 0))
    acc = _full((8, D_MODEL))
    act_bf = jax.ShapeDtypeStruct((n_rows, D_MODEL), BF16)
    ff_bf = jax.ShapeDtypeStruct((n_rows, D_FF), BF16)
    acc_shape = jax.ShapeDtypeStruct((8, D_MODEL), F32)
    w_vmem = pltpu.VMEM((D_FF, D_MODEL), BF16)
    return pl.pallas_call(
        body, name="ffn_fwd_bwd", grid=(n_rows // tm,),
        out_shape=[jax.ShapeDtypeStruct((n_rows, D_MODEL), F32), act_bf, act_bf, ff_bf, ff_bf, ff_bf,
                   acc_shape, acc_shape, acc_shape],
        in_specs=[row, row, _full((1, D_MODEL)), _full((1, D_MODEL)), ANY, ANY, ANY],
        out_specs=[row, row, row, ffrow, ffrow, ffrow, acc, acc, acc],
        scratch_shapes=[w_vmem, w_vmem, w_vmem, pltpu.VMEM((tm, D_FF), F32), pltpu.VMEM((tm, D_FF), F32),
                        pltpu.SemaphoreType.DMA],
        compiler_params=_cparams(1),
    )(h1, target, g3, g4, wg_t, wu_t, wd)


def _ffn_weight_grads(name, acts, other, exchanged):
    n_rows = other.shape[0]
    tk = min(TK_DW, n_rows)
    n_k = n_rows // tk
    half = D_FF // 2
    n_a, n_ex = len(acts), len(exchanged)

    def body(other_ref, *rest):
        act_refs = rest[:n_a]
        out_refs = rest[n_a + n_ex:2 * n_a + n_ex]
        c, k = pl.program_id(0), pl.program_id(1)
        if n_ex:
            ex = _ExchangeHalves(rest[n_a:n_a + n_ex], rest[2 * n_a + n_ex:2 * n_a + 2 * n_ex], *rest[-2:])

            @pl.when((c == 0) & (k == 0))
            def _():
                ex.start()

        @pl.when(k == 0)
        def _():
            for o in out_refs:
                o[...] = jnp.zeros_like(o)

        ov = other_ref[...]
        for a, o in zip(act_refs, out_refs):
            o[...] += _dot_tn(a[...], ov)

        if n_ex:
            @pl.when((c == 1) & (k == n_k - 1))
            def _():
                ex.finish()

    row = pl.BlockSpec((tk, D_MODEL), lambda c, k: (k, 0))
    ffrow = pl.BlockSpec((tk, half), lambda c, k: (k, c))
    out = pl.BlockSpec((half, D_MODEL), lambda c, k: (c, 0))
    outs = pl.pallas_call(
        body, name=name, grid=(2, n_k),
        out_shape=[jax.ShapeDtypeStruct((D_FF, D_MODEL), F32)] * n_a + _ExchangeHalves.out_shape(exchanged),
        in_specs=[row] + [ffrow] * n_a + [ANY] * n_ex, out_specs=[out] * n_a + [ANY] * n_ex,
        scratch_shapes=_ExchangeHalves.scratch(n_ex) if n_ex else [],
        compiler_params=_cparams(2),
    )(other, *acts, *exchanged)
    return outs[:n_a], outs[n_a:]


def _mixer_bwd(dh1, m3, z3, x3, zmeta, g1, g2, convw, poolw, pscale, win_all, wout, exchanged, scattered):
    n_seq, seq, _ = x3.shape
    tm = min(TM_MIX_BWD, seq)
    n_t = seq // tm
    hb = tm // HALO
    n_ex, n_sc = len(exchanged), len(scattered)
    n_cm = n_ex + n_sc

    def body(dh1_ref, m_ref, z_ref, zh_ref, x_ref, zm_ref, g1_ref, g2_ref, cw_ref, pw_ref, ps_ref, win_hbm, wout_hbm,
             *rest):
        outs0 = n_cm + 11
        ex = _ExchangeHalves(rest[:n_ex], rest[outs0:outs0 + n_ex], *rest[-4:-2])
        sc = _ScatterToChips(rest[n_ex:n_cm], rest[outs0 + n_ex:outs0 + n_cm], *rest[-2:])
        (dx_ref, dz_ref, a_ref, yc_ref, dm_ref, dg1_ref, dg2_ref, dsc_ref, dcw_ref, dpw_ref,
         dzm_ref) = rest[n_cm:outs0]
        win_v, wout_v, cvb, pb, dcb, dqb, mcb, mqb, sem = rest[outs0 + n_cm:-4]
        s, i = pl.program_id(0), pl.program_id(1)
        tr = n_t - 1 - i

        @pl.when((s == 0) & (i == 0))
        def _():
            sc.start()
            ex.start()
            _load_weights([(win_hbm, win_v), (wout_hbm, wout_v)], sem)
            for ref in (dg1_ref, dg2_ref, dsc_ref, dcw_ref, dpw_ref, dzm_ref):
                ref[...] = jnp.zeros_like(ref)

        @pl.when(i == 0)
        def _():
            dcb[tm:tm + HALO, :] = jnp.zeros((HALO, D_CONV), F32)
            dqb[tm:tm + HALO, :] = jnp.zeros((HALO, D_POOL), F32)

        @pl.when(i > 0)
        def _():
            dcb[tm:tm + HALO, :] = dcb[0:HALO, :]
            dqb[tm:tm + HALO, :] = dqb[0:HALO, :]

        g1v, g2v = g1_ref[...], g2_ref[...]
        dh1v = dh1_ref[0]
        mv = m_ref[0]
        r2 = _rstd(mv)
        mh = mv * r2
        dg2_ref[...] += _rows8(dh1v * mh)
        dmb = _rms_bwd(dh1v, mh, r2, g2v).astype(BF16)
        dm_ref[...] = dmb
        dyc = _dot_nt(dmb, wout_v[...])
        dyconv = dyc[:, 0:D_CONV]

        zb = z_ref[0, :, 0:IN_SHARD]
        zc = z_ref[0, :, IN_SHARD:2 * IN_SHARD]
        zv = z_ref[0, :, 2 * IN_SHARD:3 * IN_SHARD]
        zp = z_ref[0, :, 3 * IN_SHARD:4 * IN_SHARD]
        halo = jnp.where(tr == 0, zm_ref[...], zh_ref[0])
        cvb[0:HALO, :] = halo[:, IN_SHARD:2 * IN_SHARD] * halo[:, 2 * IN_SHARD:3 * IN_SHARD]
        pb[0:HALO, :] = halo[:, 3 * IN_SHARD:4 * IN_SHARD]
        cv0 = zc * zv
        cvb[HALO:HALO + tm, :] = cv0
        pb[HALO:HALO + tm, :] = zp
        cw = cw_ref[...]
        cv2 = cvb[HALO - 2:HALO - 2 + tm, :]
        cv1 = cvb[HALO - 1:HALO - 1 + tm, :]
        conv = cw[0:1] * cv2 + cw[1:2] * cv1 + cw[2:3] * cv0
        parts = [(zb * conv).astype(BF16)]
        for g in range(N_POOL_GROUPS):
            pooled = _pool_fwd(pb, g, tm).astype(BF16)
            mixed = _dot(pooled, pw_ref[g])
            scale = ps_ref[:, _gcols(g)]
            parts.append((mixed * scale).astype(BF16))
            dyp = dyc[:, D_CONV + g * POOL_GROUP:D_CONV + (g + 1) * POOL_GROUP]
            dsc_ref[:, _gcols(g)] += _rows8(dyp * mixed)
            dmix = (dyp * scale).astype(BF16)
            dpw_ref[g] += _dot_tn(pooled, dmix)
            dqb[0:tm, _gcols(g)] = _dot_nt(dmix, pw_ref[g])
        yc_ref[...] = jnp.concatenate(parts, axis=1)

        dconv = dyconv * zb
        dcb[0:tm, :] = dconv
        dcv = cw[2:3] * dconv + cw[1:2] * dcb[1:1 + tm, :] + cw[0:1] * dcb[2:2 + tm, :]
        dcw_ref[0:8, :] += _rows8(dconv * cv2)
        dcw_ref[8:16, :] += _rows8(dconv * cv1)
        dcw_ref[16:24, :] += _rows8(dconv * cv0)
        dzs = [(dyconv * conv).astype(BF16), (dcv * zv).astype(BF16), (dcv * zc).astype(BF16),
               jnp.concatenate([_pool_bwd(dqb, g, tm) for g in range(N_POOL_GROUPS)], axis=1).astype(BF16)]
        da = jnp.zeros((tm, D_MODEL), F32)
        for j in range(N_CHIPS):
            dz_ref[j] = dzs[j]
            da = da + _dot_nt(dzs[j], win_v[j])
        xt = x_ref[0]
        r1 = _rstd(xt)
        xh = xt * r1
        a_ref[...] = (xh * g1v).astype(BF16)
        dg1_ref[...] += _rows8(da * xh)
        dx_ref[0] = dh1v + _rms_bwd(da, xh, r1, g1v)

        @pl.when(tr == 0)
        def _():
            mcb[0:HALO, :] = jnp.zeros((HALO, D_CONV), F32)
            mqb[0:HALO, :] = jnp.zeros((HALO, D_POOL), F32)
            mcb[HALO:2 * HALO, :] = dcb[0:HALO, :]
            mqb[HALO:2 * HALO, :] = dqb[0:HALO, :]
            dcv_m = cw[1:2] * mcb[1:1 + HALO, :] + cw[0:1] * mcb[2:2 + HALO, :]
            dzm_ref[:, IN_SHARD:2 * IN_SHARD] += dcv_m * zm_ref[:, 2 * IN_SHARD:3 * IN_SHARD]
            dzm_ref[:, 2 * IN_SHARD:3 * IN_SHARD] += dcv_m * zm_ref[:, IN_SHARD:2 * IN_SHARD]
            dzm_ref[:, 3 * IN_SHARD:4 * IN_SHARD] += jnp.concatenate(
                [_pool_bwd(mqb, g, HALO) for g in range(N_POOL_GROUPS)], axis=1)

        @pl.when((s == n_seq - 1) & (i == n_t - 1))
        def _():
            ex.finish()
            sc.finish()

    row3 = lambda c: pl.BlockSpec((1, tm, c), lambda s, i: (s, n_t - 1 - i, 0))
    row2 = lambda c: pl.BlockSpec((tm, c), lambda s, i: (s * n_t + n_t - 1 - i, 0))
    halo_spec = pl.BlockSpec((1, HALO, D_IN_PROJ), lambda s, i: (s, jnp.maximum((n_t - 1 - i) * hb - 1, 0), 0))
    n_rows = n_seq * seq
    act_bf = jax.ShapeDtypeStruct((n_rows, D_MODEL), BF16)
    outs = pl.pallas_call(
        body, name="mixer_bwd", grid=(n_seq, n_t),
        out_shape=[jax.ShapeDtypeStruct((n_seq, seq, D_MODEL), F32),
                   jax.ShapeDtypeStruct((N_CHIPS, n_rows, IN_SHARD), BF16), act_bf, act_bf, act_bf,
                   jax.ShapeDtypeStruct((8, D_MODEL), F32), jax.ShapeDtypeStruct((8, D_MODEL), F32),
                   jax.ShapeDtypeStruct((8, D_POOL), F32), jax.ShapeDtypeStruct((24, D_CONV), F32),
                   jax.ShapeDtypeStruct((N_POOL_GROUPS, POOL_GROUP, POOL_GROUP), F32),
                   jax.ShapeDtypeStruct((N_META, D_IN_PROJ), F32)]
        + _ExchangeHalves.out_shape(exchanged) + _ScatterToChips.out_shape(scattered),
        in_specs=[row3(D_MODEL), row3(D_MODEL), row3(D_IN_PROJ), halo_spec, row3(D_MODEL),
                  _full((N_META, D_IN_PROJ)), _full((1, D_MODEL)), _full((1, D_MODEL)), _full((3, D_CONV)),
                  _full((N_POOL_GROUPS, POOL_GROUP, POOL_GROUP)), _full((1, D_POOL)), ANY, ANY] + [ANY] * n_cm,
        out_specs=[row3(D_MODEL), pl.BlockSpec((N_CHIPS, tm, IN_SHARD), lambda s, i: (0, s * n_t + n_t - 1 - i, 0)),
                   row2(D_MODEL), row2(D_MODEL), row2(D_MODEL),
                   _full((8, D_MODEL)), _full((8, D_MODEL)), _full((8, D_POOL)), _full((24, D_CONV)),
                   _full((N_POOL_GROUPS, POOL_GROUP, POOL_GROUP)), _full((N_META, D_IN_PROJ))] + [ANY] * n_cm,
        scratch_shapes=[pltpu.VMEM((N_CHIPS, D_MODEL, IN_SHARD), BF16), pltpu.VMEM((D_MODEL, D_MODEL), BF16),
                        pltpu.VMEM((HALO + tm, D_CONV), F32), pltpu.VMEM((HALO + tm, D_POOL), F32),
                        pltpu.VMEM((tm + HALO, D_CONV), F32), pltpu.VMEM((tm + HALO, D_POOL), F32),
                        pltpu.VMEM((2 * HALO, D_CONV), F32), pltpu.VMEM((2 * HALO, D_POOL), F32),
                        pltpu.SemaphoreType.DMA] + _ExchangeHalves.scratch(n_ex) + _ScatterToChips.scratch(n_sc),
        compiler_params=_cparams(2),
    )(dh1, m3, z3, z3, x3, zmeta, g1, g2, convw, poolw, pscale, win_all, wout, *exchanged, *scattered)
    return outs[:11], outs[11:11 + n_ex], outs[11 + n_ex:]


def _meta_bwd(dzm, meta_full, g1, win_all):
    def body(dzm_ref, meta_ref, g1_ref, win_ref, dmeta_ref, dg1_ref, a_ref, dzb_ref):
        xm = meta_ref[...]
        r = _rstd(xm)
        xh = xm * r
        g1v = g1_ref[...]
        a_ref[...] = (xh * g1v).astype(BF16)
        da = jnp.zeros((N_META, D_MODEL), F32)
        for j in range(N_CHIPS):
            dzj = dzm_ref[:, j * IN_SHARD:(j + 1) * IN_SHARD].astype(BF16)
            dzb_ref[j] = dzj
            da = da + _dot_nt(dzj, win_ref[j])
        dg1_ref[...] = _rows8(da * xh)
        dmeta_ref[...] = _rms_bwd(da, xh, r, g1v)

    vm = pl.BlockSpec(memory_space=pltpu.VMEM)
    return pl.pallas_call(
        body, name="meta_bwd",
        out_shape=[jax.ShapeDtypeStruct((N_META, D_MODEL), F32), jax.ShapeDtypeStruct((8, D_MODEL), F32),
                   jax.ShapeDtypeStruct((N_META, D_MODEL), BF16), jax.ShapeDtypeStruct((N_CHIPS, N_META, IN_SHARD), BF16)],
        in_specs=[vm] * 4, out_specs=[vm] * 4,
    )(dzm, meta_full, g1, win_all)


def _mixer_weight_grads(a, dz, ycat, dm, a_meta, dz_meta, ffn_sums, small):
    n_rows = a.shape[0]
    tk = min(TK_DW, n_rows)
    n_k = n_rows // tk
    n_sc, n_sm = len(ffn_sums), _AllReduceSmall.N_IN

    def body(a_ref, dz_ref, yc_ref, dm_ref, am_ref, dzm_ref, *rest):
        ins, outs, scratch = rest[:n_sc + n_sm], rest[n_sc + n_sm:2 * n_sc + n_sm + 5], rest[2 * n_sc + n_sm + 5:]
        dwin_ref, dwout_ref = outs[:2]
        scatter = _ScatterToChips(ins[:n_sc], outs[2:2 + n_sc], *scratch[:2])
        reduce_small = _AllReduceSmall(ins[n_sc:], outs[2 + n_sc:], scratch[2:])
        k = pl.program_id(0)

        @pl.when(k == 0)
        def _():
            scatter.start()
            reduce_small.pack_and_send()
            am_t = am_ref[...].T
            for j in range(N_CHIPS):
                dwin_ref[j] = _dot(am_t, dzm_ref[j])
            dwout_ref[...] = jnp.zeros_like(dwout_ref)

        for st in range(2):
            @pl.when(k == ((st + 1) * n_k) // 3)
            def _():
                reduce_small.combine(st)

        a_t = a_ref[...].T
        for j in range(N_CHIPS):
            dwin_ref[j] += _dot(a_t, dz_ref[j])
        dwout_ref[...] += _dot_tn(yc_ref[...], dm_ref[...])

        @pl.when(k == n_k - 1)
        def _():
            reduce_small.combine(2)
            scatter.finish()

    row = pl.BlockSpec((tk, D_MODEL), lambda k: (k, 0))
    outs = pl.pallas_call(
        body, name="mixer_weight_grads", grid=(n_k,),
        out_shape=[jax.ShapeDtypeStruct((N_CHIPS, D_MODEL, IN_SHARD), F32),
                   jax.ShapeDtypeStruct((D_MODEL, D_MODEL), F32)] + _ScatterToChips.out_shape(ffn_sums)
        + _AllReduceSmall.out_shape(),
        in_specs=[row, pl.BlockSpec((N_CHIPS, tk, IN_SHARD), lambda k: (0, k, 0)), row, row,
                  _full((N_META, D_MODEL)), _full((N_CHIPS, N_META, IN_SHARD))] + [ANY] * n_sc
        + [_full(s.shape) for s in small],
        out_specs=[_full((N_CHIPS, D_MODEL, IN_SHARD)), _full((D_MODEL, D_MODEL))] + [ANY] * n_sc
        + [_full(s) for s in _AllReduceSmall.SHAPES],
        scratch_shapes=_ScatterToChips.scratch(n_sc) + _AllReduceSmall.scratch(),
        compiler_params=_cparams(1),
    )(a, dz, ycat, dm, a_meta, dz_meta, *ffn_sums, *small)
    return ([outs[0], outs[1].reshape(N_CHIPS, OUT_SHARD, D_MODEL)], outs[2:2 + n_sc], outs[2 + n_sc:])


def kernel(x, meta_tokens, norm_mix_pre, w_in, conv_w, pool_w, pool_scale, w_out, norm_mix_post, norm_ffn_pre, w_gate, w_up, w_down, norm_ffn_post, loss_target, m_meta_tokens, m_norm_mix_pre, m_w_in, m_conv_w, m_pool_w, m_pool_scale, m_w_out, m_norm_mix_post, m_norm_ffn_pre, m_w_gate, m_w_up, m_w_down, m_norm_ffn_post, v_meta_tokens, v_norm_mix_pre, v_w_in, v_conv_w, v_pool_w, v_pool_scale, v_w_out, v_norm_mix_post, v_norm_ffn_pre, v_w_gate, v_w_up, v_w_down, v_norm_ffn_post):
    n_seq, seq, _ = x.shape
    n_rows = n_seq * seq
    chip = 2 * lax.axis_index("x") + lax.axis_index("y")
    meta_cols = D_MODEL // N_CHIPS
    conv_cols = D_CONV // N_CHIPS

    small = jnp.zeros((2 * HALO, meta_cols), F32)
    small = small.at[0:N_META, :].set(meta_tokens).at[N_META:N_META + 3, 0:conv_cols].set(conv_w[0])
    win_all, wout_all, small_all = _all_gather_shards([w_in[0].astype(BF16), w_out[0].astype(BF16), small])
    meta_full = small_all[:, 0:N_META, :].transpose(1, 0, 2).reshape(N_META, D_MODEL)
    conv_full = small_all[:, N_META:N_META + 3, 0:conv_cols].transpose(1, 0, 2).reshape(3, D_CONV)
    wout_full = wout_all.reshape(D_MODEL, D_MODEL)
    poolw_bf = pool_w[0].astype(BF16)
    pscale = pool_scale
    g1, g2, g3, g4 = norm_mix_pre, norm_mix_post, norm_ffn_pre, norm_ffn_post
    place = jnp.stack([chip, lax.axis_index("c")]).astype(jnp.int32)

    zmeta = _meta_fwd(meta_full, g1, win_all)
    (z3, m3, h1), ffn_w = _mixer_fwd(
        x, zmeta, g1, g2, conv_full, poolw_bf, pscale, win_all, wout_full,
        [w_gate[0].T.astype(BF16), w_up[0].T.astype(BF16), w_down[0].astype(BF16)])
    wg_t, wu_t, wd_full = [w.reshape(D_FF, D_MODEL) for w in ffn_w]
    dh1, f_bf, dd_bf, ds_bf, du_bf, gg_bf, lossp, dg3p, dg4p = _ffn_fwd_bwd(
        h1.reshape(n_rows, D_MODEL), loss_target.reshape(n_rows, D_MODEL), g3, g4, wg_t, wu_t, wd_full)
    as_shards = lambda g: g.reshape(N_CHIPS, FF_SHARD, D_MODEL)
    (dwd,), _ = _ffn_weight_grads("ffn_weight_grads_down", [gg_bf], dd_bf, [])
    dwd = as_shards(dwd)
    (dwg_t,), (dwd_recv,) = _ffn_weight_grads("ffn_weight_grads_gate", [ds_bf], f_bf, [dwd])
    dwg_t = as_shards(dwg_t)
    (dwu_t,), (dwg_recv,) = _ffn_weight_grads("ffn_weight_grads_up", [du_bf], f_bf, [dwg_t])
    dwu_t = as_shards(dwu_t)
    ((grad_x, dz_bf, a_bf, yc_bf, dm_bf, dg1p, dg2p, dscp, dcwp, dpw, dzm), (dwu_recv,),
     (dwd_rbuf, dwg_rbuf)) = _mixer_bwd(
        dh1.reshape(n_seq, seq, D_MODEL), m3, z3, x, zmeta, g1, g2, conv_full, poolw_bf, pscale, win_all, wout_full,
        [dwu_t], [_add_pairs(dwd, dwd_recv, place), _add_pairs(dwg_t, dwg_recv, place)])
    dmeta, dg1m, a_meta, dz_meta = _meta_bwd(dzm, meta_full, g1, win_all)
    mix_grads, (dwu_rbuf,), (a_red, b_red, c_red) = _mixer_weight_grads(
        a_bf, dz_bf, yc_bf, dm_bf, a_meta, dz_meta, [_add_pairs(dwu_t, dwu_recv, place)],
        [dg1p, dg1m, dg2p, dg3p, dg4p, lossp, dmeta, dscp, dcwp, dpw.reshape(SMALL_C_ROWS, POOL_GROUP)])

    mix_recvs = _exchange_halves(mix_grads)
    mix_rbufs = _scatter_to_chips([_add_pairs(g, r, place) for g, r in zip(mix_grads, mix_recvs)])
    grads = list(mix_grads) + [dwg_t, dwu_t, dwd]
    recvs = list(mix_recvs) + [dwg_recv, dwu_recv, dwd_recv]
    rbufs = list(mix_rbufs) + [dwg_rbuf, dwu_rbuf, dwd_rbuf]
    reduced = _gather_halves([_add_chips(g, r, rb, place) for g, r, rb in zip(grads, recvs, rbufs)])
    g_win, g_wout, g_wg_t, g_wu_t, g_wd = [r.reshape(2 * r.shape[1], r.shape[2]) for r in reduced]

    loss = a_red[4, 0]
    g_g1, g_g2, g_g3, g_g4 = a_red[0:1], a_red[1:2], a_red[2:3], a_red[3:4]
    g_meta = lax.dynamic_slice(a_red, (8, chip * meta_cols), (N_META, meta_cols))
    g_pscale = b_red[0:1]
    g_conv = lax.dynamic_slice(b_red, (1, chip * conv_cols), (3, conv_cols))
    g_poolw = c_red

    big = [(w_in[0], g_win, m_w_in[0], v_w_in[0]), (w_out[0], g_wout, m_w_out[0], v_w_out[0]),
           (w_gate[0].T, g_wg_t, m_w_gate[0].T, v_w_gate[0].T), (w_up[0].T, g_wu_t, m_w_up[0].T, v_w_up[0].T),
           (w_down[0], g_wd, m_w_down[0], v_w_down[0])]
    big_out = [_adamw_big(w, g, m, v) for (w, g, m, v) in big]
    big_out[2] = [o.T for o in big_out[2]]
    big_out[3] = [o.T for o in big_out[3]]
    g_wg, g_wu = g_wg_t.T, g_wu_t.T
    small_groups = [
        (meta_tokens, g_meta, m_meta_tokens, v_meta_tokens),
        (g1, g_g1, m_norm_mix_pre, v_norm_mix_pre),
        (conv_w[0], g_conv, m_conv_w[0], v_conv_w[0]),
        (pool_w.reshape(SMALL_C_ROWS, POOL_GROUP), g_poolw, m_pool_w.reshape(SMALL_C_ROWS, POOL_GROUP),
         v_pool_w.reshape(SMALL_C_ROWS, POOL_GROUP)),
        (pool_scale, g_pscale, m_pool_scale, v_pool_scale),
        (g2, g_g2, m_norm_mix_post, v_norm_mix_post),
        (g3, g_g3, m_norm_ffn_pre, v_norm_ffn_pre),
        (g4, g_g4, m_norm_ffn_post, v_norm_ffn_post),
    ]
    small_out = _adamw_small(small_groups)

    grads_out = [g_meta, g_g1, g_win[None], g_conv[None], g_poolw.reshape(pool_w.shape), g_pscale, g_wout[None],
                 g_g2, g_g3, g_wg[None], g_wu[None], g_wd[None], g_g4]
    s_meta, s_g1, s_conv, s_poolw, s_pscale, s_g2, s_g3, s_g4 = small_out
    b_win, b_wout, b_wg, b_wu, b_wd = big_out

    def leaf(k):
        return [s_meta[k], s_g1[k], b_win[k][None], s_conv[k][None], s_poolw[k].reshape(pool_w.shape), s_pscale[k],
                b_wout[k][None], s_g2[k], s_g3[k], b_wg[k][None], b_wu[k][None], b_wd[k][None], s_g4[k]]

    return (loss, grad_x, *grads_out, *leaf(0), *leaf(1), *leaf(2))
```

```python
import jax
import jax.numpy as jnp
from jax import lax
from jax.experimental import pallas as pl
from jax.experimental.pallas import tpu as pltpu

F32 = jnp.float32
BF16 = jnp.bfloat16
MESH = pl.DeviceIdType.MESH

D_MODEL = 1024
D_CONV = 512
D_POOL = 512
POOL_GROUP = 128
N_POOL_GROUPS = 4
D_IN_PROJ = 2048
D_FF = 2816
N_CHIPS = 4
FF_SHARD = D_FF // N_CHIPS
IN_SHARD = D_IN_PROJ // N_CHIPS
OUT_SHARD = D_MODEL // N_CHIPS
D_Z = 3 * IN_SHARD
N_META = 16
HALO = 16
RMS_EPS = 1e-6

ADAM_LR = 0.001
ADAM_B1 = 0.9
ADAM_B2 = 0.999
ADAM_EPS = 1e-08
ADAM_WD = 0.01
ADAM_STEP = 10

TM_MIX_FWD = 512
TM_MIX_BWD = 512
TM_FFN = 256
TK_DW = 1024
FF_CHUNK = 1024
VMEM_LIMIT = 56 * 1024 * 1024


def _cparams(n_grid):
    return pltpu.CompilerParams(dimension_semantics=("arbitrary",) * n_grid, vmem_limit_bytes=VMEM_LIMIT)


def _dot(a, b):
    return jnp.dot(a, b, preferred_element_type=F32)


def _dot_nt(a, b):
    return lax.dot_general(a, b, (((1,), (1,)), ((), ())), preferred_element_type=F32)


def _dot_tn(a, b):
    return lax.dot_general(a, b, (((0,), (0,)), ((), ())), preferred_element_type=F32)


def _rows8(v):
    r, c = v.shape
    return v.reshape(r // 8, 8, c).sum(axis=0)


def _rstd(v):
    return lax.rsqrt(jnp.mean(v * v, axis=-1, keepdims=True) + RMS_EPS)


def _rms_bwd(dy, xhat, rstd, gain):
    dyg = dy * gain
    return rstd * (dyg - xhat * jnp.mean(dyg * xhat, axis=-1, keepdims=True))


def _sigmoid(v):
    return 1.0 / (1.0 + jnp.exp(-v))


def _gcols(g):
    return slice(g * POOL_GROUP, (g + 1) * POOL_GROUP)


def _window_sum(e, g, ahead):
    n = e.shape[0]
    w = e
    for level in range(g + 1):
        shift = 1 << level
        w = w + pltpu.roll(w, (n - shift) if ahead else shift, 0)
    return w


def _pool_fwd(pb, g, n):
    e = pb[0:HALO + n, _gcols(g)]
    return _window_sum(e, g, False)[HALO:, :] * (1.0 / (2 << g)) - e[HALO:, :]


def _pool_bwd(qb, g, n):
    e = qb[0:n + HALO, _gcols(g)]
    return _window_sum(e, g, True)[0:n, :] * (1.0 / (2 << g)) - e[0:n, :]


def _full(shape):
    nd = len(shape)
    return pl.BlockSpec(shape, lambda *_: (0,) * nd)


ANY = pl.BlockSpec(memory_space=pl.ANY)


def _mesh_pos():
    x, y, c = lax.axis_index("x"), lax.axis_index("y"), lax.axis_index("c")
    chips = [(1 - x, y), (x, 1 - y), (1 - x, 1 - y)]
    return x, y, c, chips


def _half(ref, h):
    hr = ref.shape[0] // 2
    return ref.at[pl.ds(h * hr, hr), :]


class _AllGather:
    def __init__(self, ins, outs, send_sems, recv_sems):
        self.ins, self.outs, self.send_sems, self.recv_sems = ins, outs, send_sems, recv_sems
        self.n = len(ins)

    @staticmethod
    def scratch(n):
        return [pltpu.SemaphoreType.DMA((6 * n,)), pltpu.SemaphoreType.DMA((6 * n,))]

    @staticmethod
    def out_shape(shards):
        return [jax.ShapeDtypeStruct((N_CHIPS,) + s.shape, s.dtype) for s in shards]

    def _copy(self, a, k, src, dst, to):
        return pltpu.make_async_remote_copy(src_ref=src, dst_ref=dst, send_sem=self.send_sems.at[6 * a + k],
                                            recv_sem=self.recv_sems.at[6 * a + k], device_id=to, device_id_type=MESH)

    def _ici(self, a, k):
        x, y, c, chips = _mesh_pos()
        return self._copy(a, k, _half(self.ins[a], c), _half(self.outs[a].at[2 * x + y], c), (*chips[k], c))

    def _d2d(self, a, k, h):
        x, y, c, chips = _mesh_pos()
        slot = _half(self.outs[a].at[2 * chips[k][0] + chips[k][1]], h)
        return self._copy(a, 3 + k, slot, slot, (x, y, 1 - c))

    def start(self):
        for a in range(self.n):
            for k in range(3):
                self._ici(a, k).start()

    def forward(self, a):
        c = lax.axis_index("c")
        for k in range(3):
            self._ici(a, k).wait_recv()
            self._d2d(a, k, c).start()

    def finish(self):
        c = lax.axis_index("c")
        for a in range(self.n):
            for k in range(3):
                self._d2d(a, k, 1 - c).wait_recv()
        for a in range(self.n):
            for k in range(3):
                self._ici(a, k).wait_send()
                self._d2d(a, k, c).wait_send()


def _fill_own_slot(gathered, shards):
    chip = 2 * lax.axis_index("x") + lax.axis_index("y")
    return [lax.dynamic_update_slice(o, s[None], (chip, 0, 0)) for o, s in zip(gathered, shards)]


def _all_gather_shards(shards):
    n = len(shards)

    def body(*refs):
        ag = _AllGather(refs[:n], refs[n:2 * n], *refs[2 * n:])
        ag.start()
        for a in range(n):
            ag.forward(a)
        ag.finish()

    outs = pl.pallas_call(
        body, name="all_gather_weights", out_shape=_AllGather.out_shape(shards),
        in_specs=[ANY] * n, out_specs=[ANY] * n, scratch_shapes=_AllGather.scratch(n),
    )(*shards)
    return _fill_own_slot(outs, shards)


class _ExchangeHalves:
    def __init__(self, ins, recvs, send_sems, recv_sems):
        self.ins, self.recvs, self.send_sems, self.recv_sems = ins, recvs, send_sems, recv_sems

    @staticmethod
    def scratch(n):
        return [pltpu.SemaphoreType.DMA((n,)), pltpu.SemaphoreType.DMA((n,))]

    @staticmethod
    def out_shape(grads):
        return [jax.ShapeDtypeStruct((g.shape[0], g.shape[1] // 2, g.shape[2]), g.dtype) for g in grads]

    def _copies(self):
        x, y, c, _ = _mesh_pos()
        out = []
        for a, (src, dst) in enumerate(zip(self.ins, self.recvs)):
            hr = src.shape[1] // 2
            out.append(pltpu.make_async_remote_copy(
                src_ref=src.at[:, pl.ds((1 - c) * hr, hr), :], dst_ref=dst, send_sem=self.send_sems.at[a],
                recv_sem=self.recv_sems.at[a], device_id=(x, y, 1 - c), device_id_type=MESH))
        return out

    def start(self):
        for cp in self._copies():
            cp.start()

    def finish(self):
        for cp in self._copies():
            cp.wait()


def _exchange_halves(grads):
    n = len(grads)

    def body(*refs):
        ex = _ExchangeHalves(refs[:n], refs[n:2 * n], *refs[2 * n:])
        ex.start()
        ex.finish()

    return pl.pallas_call(
        body, name="grad_exchange_halves", out_shape=_ExchangeHalves.out_shape(grads),
        in_specs=[ANY] * n, out_specs=[ANY] * n, scratch_shapes=_ExchangeHalves.scratch(n),
    )(*grads)


class _ScatterToChips:
    def __init__(self, ins, rbufs, send_sems, recv_sems):
        self.ins, self.rbufs, self.send_sems, self.recv_sems = ins, rbufs, send_sems, recv_sems

    @staticmethod
    def scratch(n):
        return [pltpu.SemaphoreType.DMA((3 * n,)), pltpu.SemaphoreType.DMA((3 * n,))]

    @staticmethod
    def out_shape(sums):
        return [jax.ShapeDtypeStruct((3,) + s.shape[1:], BF16) for s in sums]

    def _copies(self):
        x, y, c, chips = _mesh_pos()
        out = []
        for a, (src, dst) in enumerate(zip(self.ins, self.rbufs)):
            for k, chip in enumerate(chips):
                out.append(pltpu.make_async_remote_copy(
                    src_ref=src.at[2 * chip[0] + chip[1]], dst_ref=dst.at[k], send_sem=self.send_sems.at[3 * a + k],
                    recv_sem=self.recv_sems.at[3 * a + k], device_id=(*chip, c), device_id_type=MESH))
        return out

    def start(self):
        for cp in self._copies():
            cp.start()

    def finish(self):
        for cp in self._copies():
            cp.wait()


def _scatter_to_chips(sums_bf16):
    n = len(sums_bf16)

    def body(*refs):
        sc = _ScatterToChips(refs[:n], refs[n:2 * n], *refs[2 * n:])
        sc.start()
        sc.finish()

    return pl.pallas_call(
        body, name="grad_scatter_to_chips", out_shape=_ScatterToChips.out_shape(sums_bf16),
        in_specs=[ANY] * n, out_specs=[ANY] * n, scratch_shapes=_ScatterToChips.scratch(n),
    )(*sums_bf16)


def _gather_halves(halves):
    n = len(halves)

    def body(*refs):
        ins, outs = refs[:n], refs[n:2 * n]
        send_sems, recv_sems = refs[2 * n:]
        x, y, c, _ = _mesh_pos()
        sib = (x, y, 1 - c)
        remote = [pltpu.make_async_remote_copy(src_ref=ins[a].at[c], dst_ref=outs[a].at[c],
                                               send_sem=send_sems.at[a], recv_sem=recv_sems.at[a],
                                               device_id=sib, device_id_type=MESH) for a in range(n)]
        for cp in remote:
            cp.start()
        for a in range(n):
            pltpu.make_async_remote_copy(src_ref=ins[a].at[1 - c], dst_ref=outs[a].at[1 - c], send_sem=send_sems.at[a],
                                         recv_sem=recv_sems.at[a], device_id=sib, device_id_type=MESH).wait_recv()
        for cp in remote:
            cp.wait_send()

    return pl.pallas_call(
        body, name="grad_gather_halves",
        out_shape=[jax.ShapeDtypeStruct(h.shape, F32) for h in halves],
        in_specs=[ANY] * n, out_specs=[ANY] * n, input_output_aliases={a: a for a in range(n)},
        scratch_shapes=[pltpu.SemaphoreType.DMA((n,)), pltpu.SemaphoreType.DMA((n,))],
    )(*halves)


SMALL_A_ROWS = 24
SMALL_B_ROWS = 8
SMALL_C_ROWS = N_POOL_GROUPS * POOL_GROUP


class _AllReduceSmall:
    N_IN = 10
    SHAPES = [(SMALL_A_ROWS, D_MODEL), (SMALL_B_ROWS, D_CONV), (SMALL_C_ROWS, POOL_GROUP)]

    def __init__(self, ins, outs, scratch):
        self.ins, self.outs = ins, outs
        self.bufs, self.rcvs, self.send_sems, self.recv_sems = scratch[:3], scratch[3:6], scratch[6], scratch[7]

    @classmethod
    def scratch(cls):
        return ([pltpu.VMEM((3,) + s, F32) for s in cls.SHAPES] + [pltpu.VMEM((3,) + s, F32) for s in cls.SHAPES]
                + [pltpu.SemaphoreType.DMA((9,)), pltpu.SemaphoreType.DMA((9,))])

    @classmethod
    def out_shape(cls):
        return [jax.ShapeDtypeStruct(s, F32) for s in cls.SHAPES]

    def _copies(self, st):
        x, y, c, _ = _mesh_pos()
        peer = [(x, y, 1 - c), (1 - x, y, c), (x, 1 - y, c)][st]
        return [pltpu.make_async_remote_copy(
            src_ref=buf.at[st], dst_ref=rcv.at[st], send_sem=self.send_sems.at[3 * st + i],
            recv_sem=self.recv_sems.at[3 * st + i], device_id=peer, device_id_type=MESH)
            for i, (buf, rcv) in enumerate(zip(self.bufs, self.rcvs))]

    def pack_and_send(self):
        dg1_ref, dg1m_ref, dg2_ref, dg3_ref, dg4_ref, loss_ref, dmeta_ref, dsc_ref, dcw_ref, dpw_ref = self.ins
        a_buf, b_buf, c_buf = self.bufs

        def rowsum(v):
            return jnp.sum(v, axis=0, keepdims=True)

        a_buf[0, 0:1, :] = rowsum(dg1_ref[...] + dg1m_ref[...])
        a_buf[0, 1:2, :] = rowsum(dg2_ref[...])
        a_buf[0, 2:3, :] = rowsum(dg3_ref[...])
        a_buf[0, 3:4, :] = rowsum(dg4_ref[...])
        loss = jnp.sum(rowsum(loss_ref[...]), axis=1, keepdims=True) * (0.5 / D_MODEL)
        a_buf[0, 4:5, :] = jnp.broadcast_to(loss, (1, D_MODEL))
        a_buf[0, 5:8, :] = jnp.zeros((3, D_MODEL), F32)
        a_buf[0, 8:24, :] = dmeta_ref[...]
        b_buf[0, 0:1, :] = rowsum(dsc_ref[...])
        for k in range(3):
            b_buf[0, 1 + k:2 + k, :] = rowsum(dcw_ref[8 * k:8 * k + 8, :])
        b_buf[0, 4:8, :] = jnp.zeros((4, D_CONV), F32)
        c_buf[0] = dpw_ref[...]
        for cp in self._copies(0):
            cp.start()

    def combine(self, st):
        for cp in self._copies(st):
            cp.wait()
        if st < 2:
            for buf, rcv in zip(self.bufs, self.rcvs):
                buf[st + 1] = buf[st] + rcv[st]
            for cp in self._copies(st + 1):
                cp.start()
        else:
            for out, buf, rcv in zip(self.outs, self.bufs, self.rcvs):
                out[...] = buf[st] + rcv[st]


def _row_block(rows):
    for cand in (512, 448, 384, 352, 320, 256, 128, 64, 32, 16):
        if rows % cand == 0:
            return cand
    return rows


def _add_pairs(grad, recv, place):
    n_sh, rows2, cols = grad.shape
    hr = rows2 // 2
    br = _row_block(hr)

    def body(place_ref, a_ref, b_ref, o_ref):
        o_ref[...] = (a_ref[0] + b_ref[...]).astype(BF16)

    return pl.pallas_call(
        body, name="grad_add_pairs",
        grid_spec=pltpu.PrefetchScalarGridSpec(
            num_scalar_prefetch=1, grid=(n_sh, hr // br),
            in_specs=[pl.BlockSpec((1, 1, br, cols), lambda j, i, p: (j, p[1], i, 0)),
                      pl.BlockSpec((1, br, cols), lambda j, i, p: (j, i, 0))],
            out_specs=pl.BlockSpec((1, br, cols), lambda j, i, p: (j, i, 0))),
        out_shape=jax.ShapeDtypeStruct((n_sh, hr, cols), BF16), compiler_params=_cparams(2),
    )(place, grad.reshape(n_sh, 2, hr, cols), recv)


def _add_chips(grad, recv, rbuf, place):
    n_sh, rows2, cols = grad.shape
    hr = rows2 // 2
    br = _row_block(hr)

    def body(place_ref, a_ref, b_ref, r_ref, o_ref):
        own = a_ref[0, 0] + b_ref[0]
        o_ref[0] = ((own + r_ref[0].astype(F32)) + r_ref[1].astype(F32)) + r_ref[2].astype(F32)

    return pl.pallas_call(
        body, name="grad_add_chips",
        grid_spec=pltpu.PrefetchScalarGridSpec(
            num_scalar_prefetch=1, grid=(hr // br,),
            in_specs=[pl.BlockSpec((1, 1, br, cols), lambda i, p: (p[0], p[1], i, 0)),
                      pl.BlockSpec((1, br, cols), lambda i, p: (p[0], i, 0)),
                      pl.BlockSpec((3, br, cols), lambda i, p: (0, i, 0))],
            out_specs=pl.BlockSpec((1, br, cols), lambda i, p: (p[1], i, 0))),
        out_shape=jax.ShapeDtypeStruct((2, hr, cols), F32), compiler_params=_cparams(1),
    )(place, grad.reshape(n_sh, 2, hr, cols), recv, rbuf)


def _adamw_math(w, g, m, v):
    m2 = ADAM_B1 * m + (1.0 - ADAM_B1) * g
    v2 = ADAM_B2 * v + (1.0 - ADAM_B2) * (g * g)
    m_hat = m2 / (1.0 - ADAM_B1 ** ADAM_STEP)
    v_hat = v2 / (1.0 - ADAM_B2 ** ADAM_STEP)
    delta = -ADAM_LR * (m_hat / (jnp.sqrt(v_hat) + ADAM_EPS) + ADAM_WD * w)
    return delta, m2, v2


def _adamw_big(w, g, m, v):
    rows, cols = w.shape
    br = _row_block(rows)

    def body(w_ref, g_ref, m_ref, v_ref, d_ref, m2_ref, v2_ref):
        d, m2, v2 = _adamw_math(w_ref[...], g_ref[...], m_ref[...], v_ref[...])
        d_ref[...] = d
        m2_ref[...] = m2
        v2_ref[...] = v2

    spec = pl.BlockSpec((br, cols), lambda i: (i, 0))
    return pl.pallas_call(
        body, name="adamw_big", grid=(rows // br,),
        out_shape=[jax.ShapeDtypeStruct((rows, cols), F32)] * 3,
        in_specs=[spec] * 4, out_specs=[spec] * 3, compiler_params=_cparams(1),
    )(w, g, m, v)


def _adamw_small(groups):
    n = len(groups)

    def body(*refs):
        ins, outs = refs[:4 * n], refs[4 * n:]
        for i in range(n):
            w, g, m, v = (r[...] for r in ins[4 * i:4 * i + 4])
            d, m2, v2 = _adamw_math(w, g, m, v)
            outs[3 * i][...] = d
            outs[3 * i + 1][...] = m2
            outs[3 * i + 2][...] = v2

    vm = pl.BlockSpec(memory_space=pltpu.VMEM)
    flat = [a for grp in groups for a in grp]
    out_shape = [jax.ShapeDtypeStruct(grp[0].shape, F32) for grp in groups for _ in range(3)]
    outs = pl.pallas_call(body, name="adamw_small", out_shape=out_shape,
                          in_specs=[vm] * (4 * n), out_specs=[vm] * (3 * n))(*flat)
    return [tuple(outs[3 * i:3 * i + 3]) for i in range(n)]


def _load_weights(pairs, sem):
    for src, dst in pairs:
        cp = pltpu.make_async_copy(src, dst, sem)
        cp.start()
        cp.wait()


def _meta_fwd(meta_full, g1, win_all):
    def body(meta_ref, g1_ref, win_ref, z_ref):
        xm = meta_ref[...]
        a = (xm * _rstd(xm) * g1_ref[...]).astype(BF16)
        for j in range(N_CHIPS):
            z_ref[:, j * IN_SHARD:(j + 1) * IN_SHARD] = _dot(a, win_ref[j])

    vm = pl.BlockSpec(memory_space=pltpu.VMEM)
    return pl.pallas_call(body, name="meta_fwd", out_shape=jax.ShapeDtypeStruct((N_META, D_IN_PROJ), F32),
                          in_specs=[vm] * 3, out_specs=vm)(meta_full, g1, win_all)


def _mixer_fwd(x3, zmeta, g1, g2, convw, poolw, pscale, win_all, wout, ffn_shards):
    n_seq, seq, _ = x3.shape
    tm = min(TM_MIX_FWD, seq)
    n_t = seq // tm
    n_steps = n_seq * n_t
    n_ag = len(ffn_shards)

    def body(x_ref, zm_ref, g1_ref, g2_ref, cw_ref, pw_ref, ps_ref, win_hbm, wout_hbm, *rest):
        ag = _AllGather(rest[:n_ag], rest[n_ag + 7:2 * n_ag + 7], *rest[-2:])
        z_ref, m_ref, h1_ref, a_ref, conv_ref, pooled_ref, yc_ref = rest[n_ag:n_ag + 7]
        win_v, wout_v, cvb, pb, sem = rest[2 * n_ag + 7:-2]
        s, t = pl.program_id(0), pl.program_id(1)
        step = s * n_t + t

        @pl.when(step == 0)
        def _():
            ag.start()
            _load_weights([(win_hbm, win_v), (wout_hbm, wout_v)], sem)

        for a in range(n_ag):
            @pl.when(step == min(((a + 1) * n_steps) // n_ag, n_steps - 1))
            def _():
                ag.forward(a)

        xt = x_ref[0]
        a = (xt * _rstd(xt) * g1_ref[...]).astype(BF16)
        a_ref[...] = a
        zb = _dot(a, win_v[0])
        zc = _dot(a, win_v[1])
        zv = _dot(a, win_v[2])
        zp = _dot(a, win_v[3])
        z_ref[0, :, 0:IN_SHARD] = zb
        z_ref[0, :, IN_SHARD:2 * IN_SHARD] = zc
        z_ref[0, :, 2 * IN_SHARD:3 * IN_SHARD] = zv

        @pl.when(t == 0)
        def _():
            cvb[0:HALO, :] = zm_ref[:, IN_SHARD:2 * IN_SHARD] * zm_ref[:, 2 * IN_SHARD:3 * IN_SHARD]
            pb[0:HALO, :] = zm_ref[:, 3 * IN_SHARD:4 * IN_SHARD]

        @pl.when(t > 0)
        def _():
            cvb[0:HALO, :] = cvb[tm:tm + HALO, :]
            pb[0:HALO, :] = pb[tm:tm + HALO, :]

        cv = zc * zv
        cvb[HALO:HALO + tm, :] = cv
        pb[HALO:HALO + tm, :] = zp
        cw = cw_ref[...]
        conv = cw[0:1] * cvb[HALO - 2:HALO - 2 + tm, :] + cw[1:2] * cvb[HALO - 1:HALO - 1 + tm, :] + cw[2:3] * cv
        conv_ref[...] = conv
        parts = [(zb * conv).astype(BF16)]
        for g in range(N_POOL_GROUPS):
            pooled = _pool_fwd(pb, g, tm).astype(BF16)
            pooled_ref[:, _gcols(g)] = pooled
            parts.append((_dot(pooled, pw_ref[g]) * ps_ref[:, _gcols(g)]).astype(BF16))
        ycat = jnp.concatenate(parts, axis=1)
        yc_ref[...] = ycat
        m = _dot(ycat, wout_v[...])
        m_ref[0] = m
        h1_ref[0] = xt + m * _rstd(m) * g2_ref[...]

        @pl.when(step == n_steps - 1)
        def _():
            ag.finish()

    n_rows = n_seq * seq
    row = lambda c: pl.BlockSpec((1, tm, c), lambda s, t: (s, t, 0))
    row2 = lambda c: pl.BlockSpec((tm, c), lambda s, t: (s * n_t + t, 0))
    outs = pl.pallas_call(
        body, name="mixer_fwd", grid=(n_seq, n_t),
        out_shape=[jax.ShapeDtypeStruct((n_seq, seq, D_Z), F32), jax.ShapeDtypeStruct((n_seq, seq, D_MODEL), F32),
                   jax.ShapeDtypeStruct((n_seq, seq, D_MODEL), F32), jax.ShapeDtypeStruct((n_rows, D_MODEL), BF16),
                   jax.ShapeDtypeStruct((n_rows, D_CONV), F32), jax.ShapeDtypeStruct((n_rows, D_POOL), BF16),
                   jax.ShapeDtypeStruct((n_rows, D_MODEL), BF16)] + _AllGather.out_shape(ffn_shards),
        in_specs=[row(D_MODEL), _full((N_META, D_IN_PROJ)), _full((1, D_MODEL)), _full((1, D_MODEL)),
                  _full((3, D_CONV)), _full((N_POOL_GROUPS, POOL_GROUP, POOL_GROUP)), _full((1, D_POOL)), ANY, ANY]
        + [ANY] * n_ag,
        out_specs=[row(D_Z), row(D_MODEL), row(D_MODEL), row2(D_MODEL), row2(D_CONV), row2(D_POOL), row2(D_MODEL)]
        + [ANY] * n_ag,
        scratch_shapes=[pltpu.VMEM((N_CHIPS, D_MODEL, IN_SHARD), BF16), pltpu.VMEM((D_MODEL, D_MODEL), BF16),
                        pltpu.VMEM((HALO + tm, D_CONV), F32), pltpu.VMEM((HALO + tm, D_POOL), F32),
                        pltpu.SemaphoreType.DMA] + _AllGather.scratch(n_ag),
        compiler_params=_cparams(2),
    )(x3, zmeta, g1, g2, convw, poolw, pscale, win_all, wout, *ffn_shards)
    return outs[:7], _fill_own_slot(outs[7:], ffn_shards)


def _ffn_chunks():
    out, r0 = [], 0
    while r0 < D_FF:
        out.append((r0, min(FF_CHUNK, D_FF - r0)))
        r0 += FF_CHUNK
    return out


def _ffn_fwd_bwd(h1, target, g3, g4, wg_t, wu_t, wd):
    n_rows = h1.shape[0]
    tm = min(TM_FFN, n_rows)
    chunks = _ffn_chunks()

    def body(h1_ref, t_ref, g3_ref, g4_ref, wg_hbm, wu_hbm, wd_hbm,
             dh1_ref, f_ref, dd_ref, ds_ref, du_ref, gg_ref, loss_ref, dg3_ref, dg4_ref,
             wg_v, wu_v, wd_v, s_sc, u_sc, sem):
        @pl.when(pl.program_id(0) == 0)
        def _():
            _load_weights([(wg_hbm, wg_v), (wu_hbm, wu_v), (wd_hbm, wd_v)], sem)
            loss_ref[...] = jnp.zeros_like(loss_ref)
            dg3_ref[...] = jnp.zeros_like(dg3_ref)
            dg4_ref[...] = jnp.zeros_like(dg4_ref)

        h1v = h1_ref[...]
        r3 = _rstd(h1v)
        hh = h1v * r3
        g3v, g4v = g3_ref[...], g4_ref[...]
        f = (hh * g3v).astype(BF16)
        f_ref[...] = f
        d = jnp.zeros((tm, D_MODEL), F32)
        for r0, sz in chunks:
            s = _dot_nt(f, wg_v[r0:r0 + sz, :])
            u = _dot_nt(f, wu_v[r0:r0 + sz, :])
            s_sc[:, r0:r0 + sz] = s
            u_sc[:, r0:r0 + sz] = u
            gc = (s * _sigmoid(s) * u).astype(BF16)
            gg_ref[:, r0:r0 + sz] = gc
            d = d + _dot(gc, wd_v[r0:r0 + sz, :])
        r4 = _rstd(d)
        dh = d * r4
        err = (h1v + dh * g4v) - t_ref[...]
        loss_ref[...] += _rows8(err * err)
        dy = err * (1.0 / D_MODEL)
        dg4_ref[...] += _rows8(dy * dh)
        ddb = _rms_bwd(dy, dh, r4, g4v).astype(BF16)
        dd_ref[...] = ddb
        df = jnp.zeros((tm, D_MODEL), F32)
        for r0, sz in chunks:
            dgg = _dot_nt(ddb, wd_v[r0:r0 + sz, :])
            s = s_sc[:, r0:r0 + sz]
            u = u_sc[:, r0:r0 + sz]
            sig = _sigmoid(s)
            dsc = (dgg * u * (sig * (1.0 + s * (1.0 - sig)))).astype(BF16)
            duc = (dgg * (s * sig)).astype(BF16)
            ds_ref[:, r0:r0 + sz] = dsc
            du_ref[:, r0:r0 + sz] = duc
            df = df + _dot(dsc, wg_v[r0:r0 + sz, :]) + _dot(duc, wu_v[r0:r0 + sz, :])
        dg3_ref[...] += _rows8(df * hh)
        dh1_ref[...] = dy + _rms_bwd(df, hh, r3, g3v)

    row = pl.BlockSpec((tm, D_MODEL), lambda i: (i, 0))
    ffrow = pl.BlockSpec((tm, D_FF), lambda i: (i, 0))
    acc = _full((8, D_MODEL))
    act_bf = jax.ShapeDtypeStruct((n_rows, D_MODEL), BF16)
    ff_bf = jax.ShapeDtypeStruct((n_rows, D_FF), BF16)
    acc_shape = jax.ShapeDtypeStruct((8, D_MODEL), F32)
    w_vmem = pltpu.VMEM((D_FF, D_MODEL), BF16)
    return pl.pallas_call(
        body, name="ffn_fwd_bwd", grid=(n_rows // tm,),
        out_shape=[jax.ShapeDtypeStruct((n_rows, D_MODEL), F32), act_bf, act_bf, ff_bf, ff_bf, ff_bf,
                   acc_shape, acc_shape, acc_shape],
        in_specs=[row, row, _full((1, D_MODEL)), _full((1, D_MODEL)), ANY, ANY, ANY],
        out_specs=[row, row, row, ffrow, ffrow, ffrow, acc, acc, acc],
        scratch_shapes=[w_vmem, w_vmem, w_vmem, pltpu.VMEM((tm, D_FF), F32), pltpu.VMEM((tm, D_FF), F32),
                        pltpu.SemaphoreType.DMA],
        compiler_params=_cparams(1),
    )(h1, target, g3, g4, wg_t, wu_t, wd)


def _ffn_weight_grads(name, acts, other, exchanged):
    n_rows = other.shape[0]
    tk = min(TK_DW, n_rows)
    n_k = n_rows // tk
    half = D_FF // 2
    n_a, n_ex = len(acts), len(exchanged)

    def body(other_ref, *rest):
        act_refs = rest[:n_a]
        out_refs = rest[n_a + n_ex:2 * n_a + n_ex]
        c, k = pl.program_id(0), pl.program_id(1)
        if n_ex:
            ex = _ExchangeHalves(rest[n_a:n_a + n_ex], rest[2 * n_a + n_ex:2 * n_a + 2 * n_ex], *rest[-2:])

            @pl.when((c == 0) & (k == 0))
            def _():
                ex.start()

        @pl.when(k == 0)
        def _():
            for o in out_refs:
                o[...] = jnp.zeros_like(o)

        ov = other_ref[...]
        for a, o in zip(act_refs, out_refs):
            o[...] += _dot_tn(a[...], ov)

        if n_ex:
            @pl.when((c == 1) & (k == n_k - 1))
            def _():
                ex.finish()

    row = pl.BlockSpec((tk, D_MODEL), lambda c, k: (k, 0))
    ffrow = pl.BlockSpec((tk, half), lambda c, k: (k, c))
    out = pl.BlockSpec((half, D_MODEL), lambda c, k: (c, 0))
    outs = pl.pallas_call(
        body, name=name, grid=(2, n_k),
        out_shape=[jax.ShapeDtypeStruct((D_FF, D_MODEL), F32)] * n_a + _ExchangeHalves.out_shape(exchanged),
        in_specs=[row] + [ffrow] * n_a + [ANY] * n_ex, out_specs=[out] * n_a + [ANY] * n_ex,
        scratch_shapes=_ExchangeHalves.scratch(n_ex) if n_ex else [],
        compiler_params=_cparams(2),
    )(other, *acts, *exchanged)
    return outs[:n_a], outs[n_a:]


def _mixer_bwd(dh1, m3, z3, conv2, pooled2, x3, zmeta, g1, g2, convw, poolw, pscale, win_all, wout, exchanged,
               scattered):
    n_seq, seq, _ = x3.shape
    tm = min(TM_MIX_BWD, seq)
    n_t = seq // tm
    n_ex, n_sc = len(exchanged), len(scattered)
    n_cm = n_ex + n_sc

    def body(dh1_ref, m_ref, z_ref, conv_ref, pooled_ref, x_ref, zm_ref, g1_ref, g2_ref, cw_ref, pw_ref, ps_ref,
             win_hbm, wout_hbm, *rest):
        outs0 = n_cm + 9
        ex = _ExchangeHalves(rest[:n_ex], rest[outs0:outs0 + n_ex], *rest[-4:-2])
        sc = _ScatterToChips(rest[n_ex:n_cm], rest[outs0 + n_ex:outs0 + n_cm], *rest[-2:])
        dx_ref, dz_ref, dm_ref, dg1_ref, dg2_ref, dsc_ref, dcw_ref, dpw_ref, dzm_ref = rest[n_cm:outs0]
        win_v, wout_v, dcb, dqb, mcb, mqb, sem = rest[outs0 + n_cm:-4]
        s, i = pl.program_id(0), pl.program_id(1)
        tr = n_t - 1 - i

        @pl.when((s == 0) & (i == 0))
        def _():
            sc.start()
            ex.start()
            _load_weights([(win_hbm, win_v), (wout_hbm, wout_v)], sem)
            for ref in (dg1_ref, dg2_ref, dsc_ref, dcw_ref, dpw_ref, dzm_ref):
                ref[...] = jnp.zeros_like(ref)

        @pl.when(i == 0)
        def _():
            dcb[tm:tm + HALO, :] = jnp.zeros((HALO, D_CONV), F32)
            dqb[tm:tm + HALO, :] = jnp.zeros((HALO, D_POOL), F32)

        @pl.when(i > 0)
        def _():
            dcb[tm:tm + HALO, :] = dcb[0:HALO, :]
            dqb[tm:tm + HALO, :] = dqb[0:HALO, :]

        g1v, g2v = g1_ref[...], g2_ref[...]
        dh1v = dh1_ref[0]
        mv = m_ref[0]
        r2 = _rstd(mv)
        mh = mv * r2
        dg2_ref[...] += _rows8(dh1v * mh)
        dmb = _rms_bwd(dh1v, mh, r2, g2v).astype(BF16)
        dm_ref[...] = dmb
        dyc = _dot_nt(dmb, wout_v[...])
        dyconv = dyc[:, 0:D_CONV]

        for g in range(N_POOL_GROUPS):
            pooled = pooled_ref[:, _gcols(g)]
            mixed = _dot(pooled, pw_ref[g])
            scale = ps_ref[:, _gcols(g)]
            dyp = dyc[:, D_CONV + g * POOL_GROUP:D_CONV + (g + 1) * POOL_GROUP]
            dsc_ref[:, _gcols(g)] += _rows8(dyp * mixed)
            dmix = (dyp * scale).astype(BF16)
            dpw_ref[g] += _dot_tn(pooled, dmix)
            dqb[0:tm, _gcols(g)] = _dot_nt(dmix, pw_ref[g])

        zb = z_ref[0, :, 0:IN_SHARD]
        zc = z_ref[0, :, IN_SHARD:2 * IN_SHARD]
        zv = z_ref[0, :, 2 * IN_SHARD:3 * IN_SHARD]
        cw = cw_ref[...]
        dconv = dyconv * zb
        dcb[0:tm, :] = dconv
        d1 = dcb[1:1 + tm, :]
        d2 = dcb[2:2 + tm, :]
        dcv = cw[2:3] * dconv + cw[1:2] * d1 + cw[0:1] * d2
        cv = zc * zv
        dcw_ref[0:8, :] += _rows8(cv * d2)
        dcw_ref[8:16, :] += _rows8(cv * d1)
        dcw_ref[16:24, :] += _rows8(cv * dconv)
        dzs = [(dyconv * conv_ref[...]).astype(BF16), (dcv * zv).astype(BF16), (dcv * zc).astype(BF16),
               jnp.concatenate([_pool_bwd(dqb, g, tm) for g in range(N_POOL_GROUPS)], axis=1).astype(BF16)]
        da = jnp.zeros((tm, D_MODEL), F32)
        for j in range(N_CHIPS):
            dz_ref[j] = dzs[j]
            da = da + _dot_nt(dzs[j], win_v[j])
        xt = x_ref[0]
        r1 = _rstd(xt)
        xh = xt * r1
        dg1_ref[...] += _rows8(da * xh)
        dx_ref[0] = dh1v + _rms_bwd(da, xh, r1, g1v)

        @pl.when(tr == 0)
        def _():
            mcb[0:HALO, :] = jnp.zeros((HALO, D_CONV), F32)
            mqb[0:HALO, :] = jnp.zeros((HALO, D_POOL), F32)
            mcb[HALO:2 * HALO, :] = dcb[0:HALO, :]
            mqb[HALO:2 * HALO, :] = dqb[0:HALO, :]
            m1 = mcb[1:1 + HALO, :]
            m2 = mcb[2:2 + HALO, :]
            zc_m = zm_ref[:, IN_SHARD:2 * IN_SHARD]
            zv_m = zm_ref[:, 2 * IN_SHARD:3 * IN_SHARD]
            cv_m = zc_m * zv_m
            dcw_ref[0:8, :] += _rows8(cv_m * m2)
            dcw_ref[8:16, :] += _rows8(cv_m * m1)
            dcv_m = cw[1:2] * m1 + cw[0:1] * m2
            dzm_ref[:, IN_SHARD:2 * IN_SHARD] += dcv_m * zv_m
            dzm_ref[:, 2 * IN_SHARD:3 * IN_SHARD] += dcv_m * zc_m
            dzm_ref[:, 3 * IN_SHARD:4 * IN_SHARD] += jnp.concatenate(
                [_pool_bwd(mqb, g, HALO) for g in range(N_POOL_GROUPS)], axis=1)

        @pl.when((s == n_seq - 1) & (i == n_t - 1))
        def _():
            ex.finish()
            sc.finish()

    row3 = lambda c: pl.BlockSpec((1, tm, c), lambda s, i: (s, n_t - 1 - i, 0))
    row2 = lambda c: pl.BlockSpec((tm, c), lambda s, i: (s * n_t + n_t - 1 - i, 0))
    n_rows = n_seq * seq
    outs = pl.pallas_call(
        body, name="mixer_bwd", grid=(n_seq, n_t),
        out_shape=[jax.ShapeDtypeStruct((n_seq, seq, D_MODEL), F32),
                   jax.ShapeDtypeStruct((N_CHIPS, n_rows, IN_SHARD), BF16), jax.ShapeDtypeStruct((n_rows, D_MODEL), BF16),
                   jax.ShapeDtypeStruct((8, D_MODEL), F32), jax.ShapeDtypeStruct((8, D_MODEL), F32),
                   jax.ShapeDtypeStruct((8, D_POOL), F32), jax.ShapeDtypeStruct((24, D_CONV), F32),
                   jax.ShapeDtypeStruct((N_POOL_GROUPS, POOL_GROUP, POOL_GROUP), F32),
                   jax.ShapeDtypeStruct((N_META, D_IN_PROJ), F32)]
        + _ExchangeHalves.out_shape(exchanged) + _ScatterToChips.out_shape(scattered),
        in_specs=[row3(D_MODEL), row3(D_MODEL), row3(D_Z), row2(D_CONV), row2(D_POOL), row3(D_MODEL),
                  _full((N_META, D_IN_PROJ)), _full((1, D_MODEL)), _full((1, D_MODEL)), _full((3, D_CONV)),
                  _full((N_POOL_GROUPS, POOL_GROUP, POOL_GROUP)), _full((1, D_POOL)), ANY, ANY] + [ANY] * n_cm,
        out_specs=[row3(D_MODEL), pl.BlockSpec((N_CHIPS, tm, IN_SHARD), lambda s, i: (0, s * n_t + n_t - 1 - i, 0)),
                   row2(D_MODEL),
                   _full((8, D_MODEL)), _full((8, D_MODEL)), _full((8, D_POOL)), _full((24, D_CONV)),
                   _full((N_POOL_GROUPS, POOL_GROUP, POOL_GROUP)), _full((N_META, D_IN_PROJ))] + [ANY] * n_cm,
        scratch_shapes=[pltpu.VMEM((N_CHIPS, D_MODEL, IN_SHARD), BF16), pltpu.VMEM((D_MODEL, D_MODEL), BF16),
                        pltpu.VMEM((tm + HALO, D_CONV), F32), pltpu.VMEM((tm + HALO, D_POOL), F32),
                        pltpu.VMEM((2 * HALO, D_CONV), F32), pltpu.VMEM((2 * HALO, D_POOL), F32),
                        pltpu.SemaphoreType.DMA] + _ExchangeHalves.scratch(n_ex) + _ScatterToChips.scratch(n_sc),
        compiler_params=_cparams(2),
    )(dh1, m3, z3, conv2, pooled2, x3, zmeta, g1, g2, convw, poolw, pscale, win_all, wout, *exchanged, *scattered)
    return outs[:9], outs[9:9 + n_ex], outs[9 + n_ex:]


def _meta_bwd(dzm, meta_full, g1, win_all):
    def body(dzm_ref, meta_ref, g1_ref, win_ref, dmeta_ref, dg1_ref, a_ref, dzb_ref):
        xm = meta_ref[...]
        r = _rstd(xm)
        xh = xm * r
        g1v = g1_ref[...]
        a_ref[...] = (xh * g1v).astype(BF16)
        da = jnp.zeros((N_META, D_MODEL), F32)
        for j in range(N_CHIPS):
            dzj = dzm_ref[:, j * IN_SHARD:(j + 1) * IN_SHARD].astype(BF16)
            dzb_ref[j] = dzj
            da = da + _dot_nt(dzj, win_ref[j])
        dg1_ref[...] = _rows8(da * xh)
        dmeta_ref[...] = _rms_bwd(da, xh, r, g1v)

    vm = pl.BlockSpec(memory_space=pltpu.VMEM)
    return pl.pallas_call(
        body, name="meta_bwd",
        out_shape=[jax.ShapeDtypeStruct((N_META, D_MODEL), F32), jax.ShapeDtypeStruct((8, D_MODEL), F32),
                   jax.ShapeDtypeStruct((N_META, D_MODEL), BF16), jax.ShapeDtypeStruct((N_CHIPS, N_META, IN_SHARD), BF16)],
        in_specs=[vm] * 4, out_specs=[vm] * 4,
    )(dzm, meta_full, g1, win_all)


def _mixer_weight_grads(a, dz, ycat, dm, a_meta, dz_meta, ffn_sums, small):
    n_rows = a.shape[0]
    tk = min(TK_DW, n_rows)
    n_k = n_rows // tk
    n_sc, n_sm = len(ffn_sums), _AllReduceSmall.N_IN

    def body(a_ref, dz_ref, yc_ref, dm_ref, am_ref, dzm_ref, *rest):
        ins, outs, scratch = rest[:n_sc + n_sm], rest[n_sc + n_sm:2 * n_sc + n_sm + 5], rest[2 * n_sc + n_sm + 5:]
        dwin_ref, dwout_ref = outs[:2]
        scatter = _ScatterToChips(ins[:n_sc], outs[2:2 + n_sc], *scratch[:2])
        reduce_small = _AllReduceSmall(ins[n_sc:], outs[2 + n_sc:], scratch[2:])
        k = pl.program_id(0)

        @pl.when(k == 0)
        def _():
            scatter.start()
            reduce_small.pack_and_send()
            am_t = am_ref[...].T
            for j in range(N_CHIPS):
                dwin_ref[j] = _dot(am_t, dzm_ref[j])
            dwout_ref[...] = jnp.zeros_like(dwout_ref)

        for st in range(2):
            @pl.when(k == ((st + 1) * n_k) // 3)
            def _():
                reduce_small.combine(st)

        a_t = a_ref[...].T
        for j in range(N_CHIPS):
            dwin_ref[j] += _dot(a_t, dz_ref[j])
        dwout_ref[...] += _dot_tn(yc_ref[...], dm_ref[...])

        @pl.when(k == n_k - 1)
        def _():
            reduce_small.combine(2)
            scatter.finish()

    row = pl.BlockSpec((tk, D_MODEL), lambda k: (k, 0))
    outs = pl.pallas_call(
        body, name="mixer_weight_grads", grid=(n_k,),
        out_shape=[jax.ShapeDtypeStruct((N_CHIPS, D_MODEL, IN_SHARD), F32),
                   jax.ShapeDtypeStruct((D_MODEL, D_MODEL), F32)] + _ScatterToChips.out_shape(ffn_sums)
        + _AllReduceSmall.out_shape(),
        in_specs=[row, pl.BlockSpec((N_CHIPS, tk, IN_SHARD), lambda k: (0, k, 0)), row, row,
                  _full((N_META, D_MODEL)), _full((N_CHIPS, N_META, IN_SHARD))] + [ANY] * n_sc
        + [_full(s.shape) for s in small],
        out_specs=[_full((N_CHIPS, D_MODEL, IN_SHARD)), _full((D_MODEL, D_MODEL))] + [ANY] * n_sc
        + [_full(s) for s in _AllReduceSmall.SHAPES],
        scratch_shapes=_ScatterToChips.scratch(n_sc) + _AllReduceSmall.scratch(),
        compiler_params=_cparams(1),
    )(a, dz, ycat, dm, a_meta, dz_meta, *ffn_sums, *small)
    return ([outs[0], outs[1].reshape(N_CHIPS, OUT_SHARD, D_MODEL)], outs[2:2 + n_sc], outs[2 + n_sc:])


def kernel(x, meta_tokens, norm_mix_pre, w_in, conv_w, pool_w, pool_scale, w_out, norm_mix_post, norm_ffn_pre, w_gate, w_up, w_down, norm_ffn_post, loss_target, m_meta_tokens, m_norm_mix_pre, m_w_in, m_conv_w, m_pool_w, m_pool_scale, m_w_out, m_norm_mix_post, m_norm_ffn_pre, m_w_gate, m_w_up, m_w_down, m_norm_ffn_post, v_meta_tokens, v_norm_mix_pre, v_w_in, v_conv_w, v_pool_w, v_pool_scale, v_w_out, v_norm_mix_post, v_norm_ffn_pre, v_w_gate, v_w_up, v_w_down, v_norm_ffn_post):
    n_seq, seq, _ = x.shape
    n_rows = n_seq * seq
    chip = 2 * lax.axis_index("x") + lax.axis_index("y")
    meta_cols = D_MODEL // N_CHIPS
    conv_cols = D_CONV // N_CHIPS

    small = jnp.zeros((2 * HALO, meta_cols), F32)
    small = small.at[0:N_META, :].set(meta_tokens).at[N_META:N_META + 3, 0:conv_cols].set(conv_w[0])
    win_all, wout_all, small_all = _all_gather_shards([w_in[0].astype(BF16), w_out[0].astype(BF16), small])
    meta_full = small_all[:, 0:N_META, :].transpose(1, 0, 2).reshape(N_META, D_MODEL)
    conv_full = small_all[:, N_META:N_META + 3, 0:conv_cols].transpose(1, 0, 2).reshape(3, D_CONV)
    wout_full = wout_all.reshape(D_MODEL, D_MODEL)
    poolw_bf = pool_w[0].astype(BF16)
    pscale = pool_scale
    g1, g2, g3, g4 = norm_mix_pre, norm_mix_post, norm_ffn_pre, norm_ffn_post
    place = jnp.stack([chip, lax.axis_index("c")]).astype(jnp.int32)

    zmeta = _meta_fwd(meta_full, g1, win_all)
    (z3, m3, h1, a_bf, conv2, pooled2, yc_bf), ffn_w = _mixer_fwd(
        x, zmeta, g1, g2, conv_full, poolw_bf, pscale, win_all, wout_full,
        [w_gate[0].T.astype(BF16), w_up[0].T.astype(BF16), w_down[0].astype(BF16)])
    wg_t, wu_t, wd_full = [w.reshape(D_FF, D_MODEL) for w in ffn_w]
    dh1, f_bf, dd_bf, ds_bf, du_bf, gg_bf, lossp, dg3p, dg4p = _ffn_fwd_bwd(
        h1.reshape(n_rows, D_MODEL), loss_target.reshape(n_rows, D_MODEL), g3, g4, wg_t, wu_t, wd_full)
    as_shards = lambda g: g.reshape(N_CHIPS, FF_SHARD, D_MODEL)
    (dwd,), _ = _ffn_weight_grads("ffn_weight_grads_down", [gg_bf], dd_bf, [])
    dwd = as_shards(dwd)
    (dwg_t,), (dwd_recv,) = _ffn_weight_grads("ffn_weight_grads_gate", [ds_bf], f_bf, [dwd])
    dwg_t = as_shards(dwg_t)
    (dwu_t,), (dwg_recv,) = _ffn_weight_grads("ffn_weight_grads_up", [du_bf], f_bf, [dwg_t])
    dwu_t = as_shards(dwu_t)
    ((grad_x, dz_bf, dm_bf, dg1p, dg2p, dscp, dcwp, dpw, dzm), (dwu_recv,), (dwd_rbuf, dwg_rbuf)) = _mixer_bwd(
        dh1.reshape(n_seq, seq, D_MODEL), m3, z3, conv2, pooled2, x, zmeta, g1, g2, conv_full, poolw_bf, pscale,
        win_all, wout_full, [dwu_t], [_add_pairs(dwd, dwd_recv, place), _add_pairs(dwg_t, dwg_recv, place)])
    dmeta, dg1m, a_meta, dz_meta = _meta_bwd(dzm, meta_full, g1, win_all)
    mix_grads, (dwu_rbuf,), (a_red, b_red, c_red) = _mixer_weight_grads(
        a_bf, dz_bf, yc_bf, dm_bf, a_meta, dz_meta, [_add_pairs(dwu_t, dwu_recv, place)],
        [dg1p, dg1m, dg2p, dg3p, dg4p, lossp, dmeta, dscp, dcwp, dpw.reshape(SMALL_C_ROWS, POOL_GROUP)])

    mix_recvs = _exchange_halves(mix_grads)
    mix_rbufs = _scatter_to_chips([_add_pairs(g, r, place) for g, r in zip(mix_grads, mix_recvs)])
    grads = list(mix_grads) + [dwg_t, dwu_t, dwd]
    recvs = list(mix_recvs) + [dwg_recv, dwu_recv, dwd_recv]
    rbufs = list(mix_rbufs) + [dwg_rbuf, dwu_rbuf, dwd_rbuf]
    reduced = _gather_halves([_add_chips(g, r, rb, place) for g, r, rb in zip(grads, recvs, rbufs)])
    g_win, g_wout, g_wg_t, g_wu_t, g_wd = [r.reshape(2 * r.shape[1], r.shape[2]) for r in reduced]

    loss = a_red[4, 0]
    g_g1, g_g2, g_g3, g_g4 = a_red[0:1], a_red[1:2], a_red[2:3], a_red[3:4]
    g_meta = lax.dynamic_slice(a_red, (8, chip * meta_cols), (N_META, meta_cols))
    g_pscale = b_red[0:1]
    g_conv = lax.dynamic_slice(b_red, (1, chip * conv_cols), (3, conv_cols))
    g_poolw = c_red

    big = [(w_in[0], g_win, m_w_in[0], v_w_in[0]), (w_out[0], g_wout, m_w_out[0], v_w_out[0]),
           (w_gate[0].T, g_wg_t, m_w_gate[0].T, v_w_gate[0].T), (w_up[0].T, g_wu_t, m_w_up[0].T, v_w_up[0].T),
           (w_down[0], g_wd, m_w_down[0], v_w_down[0])]
    big_out = [_adamw_big(w, g, m, v) for (w, g, m, v) in big]
    big_out[2] = [o.T for o in big_out[2]]
    big_out[3] = [o.T for o in big_out[3]]
    g_wg, g_wu = g_wg_t.T, g_wu_t.T
    small_groups = [
        (meta_tokens, g_meta, m_meta_tokens, v_meta_tokens),
        (g1, g_g1, m_norm_mix_pre, v_norm_mix_pre),
        (conv_w[0], g_conv, m_conv_w[0], v_conv_w[0]),
        (pool_w.reshape(SMALL_C_ROWS, POOL_GROUP), g_poolw, m_pool_w.reshape(SMALL_C_ROWS, POOL_GROUP),
         v_pool_w.reshape(SMALL_C_ROWS, POOL_GROUP)),
        (pool_scale, g_pscale, m_pool_scale, v_pool_scale),
        (g2, g_g2, m_norm_mix_post, v_norm_mix_post),
        (g3, g_g3, m_norm_ffn_pre, v_norm_ffn_pre),
        (g4, g_g4, m_norm_ffn_post, v_norm_ffn_post),
    ]
    small_out = _adamw_small(small_groups)

    grads_out = [g_meta, g_g1, g_win[None], g_conv[None], g_poolw.reshape(pool_w.shape), g_pscale, g_wout[None],
                 g_g2, g_g3, g_wg[None], g_wu[None], g_wd[None], g_g4]
    s_meta, s_g1, s_conv, s_poolw, s_pscale, s_g2, s_g3, s_g4 = small_out
    b_win, b_wout, b_wg, b_wu, b_wd = big_out

    def leaf(k):
        return [s_meta[k], s_g1[k], b_win[k][None], s_conv[k][None], s_poolw[k].reshape(pool_w.shape), s_pscale[k],
                b_wout[k][None], s_g2[k], s_g3[k], b_wg[k][None], b_wu[k][None], b_wd[k][None], s_g4[k]]

    return (loss, grad_x, *grads_out, *leaf(0), *leaf(1), *leaf(2))
```

```python
import jax
import jax.numpy as jnp
from jax import lax
from jax.experimental import pallas as pl
from jax.experimental.pallas import tpu as pltpu

F32 = jnp.float32
BF16 = jnp.bfloat16
MESH = pl.DeviceIdType.MESH

D_MODEL = 1024
D_CONV = 512
D_POOL = 512
POOL_GROUP = 128
N_POOL_GROUPS = 4
D_IN_PROJ = 2048
D_FF = 2816
N_CHIPS = 4
FF_SHARD = D_FF // N_CHIPS
IN_SHARD = D_IN_PROJ // N_CHIPS
OUT_SHARD = D_MODEL // N_CHIPS
D_Z = 3 * IN_SHARD
N_META = 16
HALO = 16
RMS_EPS = 1e-6

ADAM_LR = 0.001
ADAM_B1 = 0.9
ADAM_B2 = 0.999
ADAM_EPS = 1e-08
ADAM_WD = 0.01
ADAM_STEP = 10

TM_MIX_FWD = 512
TM_MIX_BWD = 512
SUB_MIX_BWD = 256
TM_FFN = 512
SUB_FFN = 256
TK_DW = 1024
FF_CHUNK = 1024
VMEM_LIMIT = 56 * 1024 * 1024
VMEM_LIMIT_FFN = 63 * 1024 * 1024


def _cparams(n_grid):
    return pltpu.CompilerParams(dimension_semantics=("arbitrary",) * n_grid, vmem_limit_bytes=VMEM_LIMIT)


def _dot(a, b):
    return jnp.dot(a, b, preferred_element_type=F32)


def _dot_nt(a, b):
    return lax.dot_general(a, b, (((1,), (1,)), ((), ())), preferred_element_type=F32)


def _dot_tn(a, b):
    return lax.dot_general(a, b, (((0,), (0,)), ((), ())), preferred_element_type=F32)


def _rows8(v):
    r, c = v.shape
    return v.reshape(r // 8, 8, c).sum(axis=0)


def _rstd(v):
    return lax.rsqrt(jnp.mean(v * v, axis=-1, keepdims=True) + RMS_EPS)


def _rms_bwd(dy, xhat, rstd, gain):
    dyg = dy * gain
    return rstd * (dyg - xhat * jnp.mean(dyg * xhat, axis=-1, keepdims=True))


def _sigmoid(v):
    return 1.0 / (1.0 + jnp.exp(-v))


def _gcols(g):
    return slice(g * POOL_GROUP, (g + 1) * POOL_GROUP)


def _window_sum(e, g, ahead):
    n = e.shape[0]
    w = e
    for level in range(g + 1):
        shift = 1 << level
        w = w + pltpu.roll(w, (n - shift) if ahead else shift, 0)
    return w


def _pool_fwd(pb, g, n):
    e = pb[0:HALO + n, _gcols(g)]
    return _window_sum(e, g, False)[HALO:, :] * (1.0 / (2 << g)) - e[HALO:, :]


def _pool_bwd(qb, g, r0, n):
    e = qb[r0:r0 + n + HALO, _gcols(g)]
    return _window_sum(e, g, True)[0:n, :] * (1.0 / (2 << g)) - e[0:n, :]


def _full(shape):
    nd = len(shape)
    return pl.BlockSpec(shape, lambda *_: (0,) * nd)


ANY = pl.BlockSpec(memory_space=pl.ANY)


def _mesh_pos():
    x, y, c = lax.axis_index("x"), lax.axis_index("y"), lax.axis_index("c")
    chips = [(1 - x, y), (x, 1 - y), (1 - x, 1 - y)]
    return x, y, c, chips


def _half(ref, h):
    hr = ref.shape[0] // 2
    return ref.at[pl.ds(h * hr, hr), :]


class _AllGather:
    def __init__(self, ins, outs, send_sems, recv_sems):
        self.ins, self.outs, self.send_sems, self.recv_sems = ins, outs, send_sems, recv_sems
        self.n = len(ins)

    @staticmethod
    def scratch(n):
        return [pltpu.SemaphoreType.DMA((6 * n,)), pltpu.SemaphoreType.DMA((6 * n,))]

    @staticmethod
    def out_shape(shards):
        return [jax.ShapeDtypeStruct((N_CHIPS,) + s.shape, s.dtype) for s in shards]

    def _copy(self, a, k, src, dst, to):
        return pltpu.make_async_remote_copy(src_ref=src, dst_ref=dst, send_sem=self.send_sems.at[6 * a + k],
                                            recv_sem=self.recv_sems.at[6 * a + k], device_id=to, device_id_type=MESH)

    def _ici(self, a, k):
        x, y, c, chips = _mesh_pos()
        return self._copy(a, k, _half(self.ins[a], c), _half(self.outs[a].at[2 * x + y], c), (*chips[k], c))

    def _d2d(self, a, k, h):
        x, y, c, chips = _mesh_pos()
        slot = _half(self.outs[a].at[2 * chips[k][0] + chips[k][1]], h)
        return self._copy(a, 3 + k, slot, slot, (x, y, 1 - c))

    def start(self):
        for a in range(self.n):
            for k in range(3):
                self._ici(a, k).start()

    def forward(self, a):
        c = lax.axis_index("c")
        for k in range(3):
            self._ici(a, k).wait_recv()
            self._d2d(a, k, c).start()

    def finish(self):
        c = lax.axis_index("c")
        for a in range(self.n):
            for k in range(3):
                self._d2d(a, k, 1 - c).wait_recv()
        for a in range(self.n):
            for k in range(3):
                self._ici(a, k).wait_send()
                self._d2d(a, k, c).wait_send()


def _fill_own_slot(gathered, shards):
    chip = 2 * lax.axis_index("x") + lax.axis_index("y")
    return [lax.dynamic_update_slice(o, s[None], (chip, 0, 0)) for o, s in zip(gathered, shards)]


def _all_gather_shards(shards):
    n = len(shards)

    def body(*refs):
        ag = _AllGather(refs[:n], refs[n:2 * n], *refs[2 * n:])
        ag.start()
        for a in range(n):
            ag.forward(a)
        ag.finish()

    outs = pl.pallas_call(
        body, name="all_gather_weights", out_shape=_AllGather.out_shape(shards),
        in_specs=[ANY] * n, out_specs=[ANY] * n, scratch_shapes=_AllGather.scratch(n),
    )(*shards)
    return _fill_own_slot(outs, shards)


class _ExchangeHalves:
    def __init__(self, ins, recvs, send_sems, recv_sems):
        self.ins, self.recvs, self.send_sems, self.recv_sems = ins, recvs, send_sems, recv_sems

    @staticmethod
    def scratch(n):
        return [pltpu.SemaphoreType.DMA((n,)), pltpu.SemaphoreType.DMA((n,))]

    @staticmethod
    def out_shape(grads):
        return [jax.ShapeDtypeStruct((g.shape[0], g.shape[1] // 2, g.shape[2]), g.dtype) for g in grads]

    def _copies(self):
        x, y, c, _ = _mesh_pos()
        out = []
        for a, (src, dst) in enumerate(zip(self.ins, self.recvs)):
            hr = src.shape[1] // 2
            out.append(pltpu.make_async_remote_copy(
                src_ref=src.at[:, pl.ds((1 - c) * hr, hr), :], dst_ref=dst, send_sem=self.send_sems.at[a],
                recv_sem=self.recv_sems.at[a], device_id=(x, y, 1 - c), device_id_type=MESH))
        return out

    def start(self):
        for cp in self._copies():
            cp.start()

    def finish(self):
        for cp in self._copies():
            cp.wait()


def _exchange_halves(grads):
    n = len(grads)

    def body(*refs):
        ex = _ExchangeHalves(refs[:n], refs[n:2 * n], *refs[2 * n:])
        ex.start()
        ex.finish()

    return pl.pallas_call(
        body, name="grad_exchange_halves", out_shape=_ExchangeHalves.out_shape(grads),
        in_specs=[ANY] * n, out_specs=[ANY] * n, scratch_shapes=_ExchangeHalves.scratch(n),
    )(*grads)


class _ScatterToChips:
    def __init__(self, ins, rbufs, send_sems, recv_sems):
        self.ins, self.rbufs, self.send_sems, self.recv_sems = ins, rbufs, send_sems, recv_sems

    @staticmethod
    def scratch(n):
        return [pltpu.SemaphoreType.DMA((3 * n,)), pltpu.SemaphoreType.DMA((3 * n,))]

    @staticmethod
    def out_shape(sums):
        return [jax.ShapeDtypeStruct((3,) + s.shape[1:], BF16) for s in sums]

    def _copies(self):
        x, y, c, chips = _mesh_pos()
        out = []
        for a, (src, dst) in enumerate(zip(self.ins, self.rbufs)):
            for k, chip in enumerate(chips):
                out.append(pltpu.make_async_remote_copy(
                    src_ref=src.at[2 * chip[0] + chip[1]], dst_ref=dst.at[k], send_sem=self.send_sems.at[3 * a + k],
                    recv_sem=self.recv_sems.at[3 * a + k], device_id=(*chip, c), device_id_type=MESH))
        return out

    def start(self):
        for cp in self._copies():
            cp.start()

    def finish(self):
        for cp in self._copies():
            cp.wait()


def _scatter_to_chips(sums_bf16):
    n = len(sums_bf16)

    def body(*refs):
        sc = _ScatterToChips(refs[:n], refs[n:2 * n], *refs[2 * n:])
        sc.start()
        sc.finish()

    return pl.pallas_call(
        body, name="grad_scatter_to_chips", out_shape=_ScatterToChips.out_shape(sums_bf16),
        in_specs=[ANY] * n, out_specs=[ANY] * n, scratch_shapes=_ScatterToChips.scratch(n),
    )(*sums_bf16)


def _gather_halves(halves):
    n = len(halves)

    def body(*refs):
        ins, outs = refs[:n], refs[n:2 * n]
        send_sems, recv_sems = refs[2 * n:]
        x, y, c, _ = _mesh_pos()
        sib = (x, y, 1 - c)
        remote = [pltpu.make_async_remote_copy(src_ref=ins[a].at[c], dst_ref=outs[a].at[c],
                                               send_sem=send_sems.at[a], recv_sem=recv_sems.at[a],
                                               device_id=sib, device_id_type=MESH) for a in range(n)]
        for cp in remote:
            cp.start()
        for a in range(n):
            pltpu.make_async_remote_copy(src_ref=ins[a].at[1 - c], dst_ref=outs[a].at[1 - c], send_sem=send_sems.at[a],
                                         recv_sem=recv_sems.at[a], device_id=sib, device_id_type=MESH).wait_recv()
        for cp in remote:
            cp.wait_send()

    return pl.pallas_call(
        body, name="grad_gather_halves",
        out_shape=[jax.ShapeDtypeStruct(h.shape, F32) for h in halves],
        in_specs=[ANY] * n, out_specs=[ANY] * n, input_output_aliases={a: a for a in range(n)},
        scratch_shapes=[pltpu.SemaphoreType.DMA((n,)), pltpu.SemaphoreType.DMA((n,))],
    )(*halves)


SMALL_A_ROWS = 24
SMALL_B_ROWS = 8
SMALL_C_ROWS = N_POOL_GROUPS * POOL_GROUP


class _AllReduceSmall:
    N_IN = 10
    SHAPES = [(SMALL_A_ROWS, D_MODEL), (SMALL_B_ROWS, D_CONV), (SMALL_C_ROWS, POOL_GROUP)]

    def __init__(self, ins, outs, scratch):
        self.ins, self.outs = ins, outs
        self.bufs, self.rcvs, self.send_sems, self.recv_sems = scratch[:3], scratch[3:6], scratch[6], scratch[7]

    @classmethod
    def scratch(cls):
        return ([pltpu.VMEM((3,) + s, F32) for s in cls.SHAPES] + [pltpu.VMEM((3,) + s, F32) for s in cls.SHAPES]
                + [pltpu.SemaphoreType.DMA((9,)), pltpu.SemaphoreType.DMA((9,))])

    @classmethod
    def out_shape(cls):
        return [jax.ShapeDtypeStruct(s, F32) for s in cls.SHAPES]

    def _copies(self, st):
        x, y, c, _ = _mesh_pos()
        peer = [(x, y, 1 - c), (1 - x, y, c), (x, 1 - y, c)][st]
        return [pltpu.make_async_remote_copy(
            src_ref=buf.at[st], dst_ref=rcv.at[st], send_sem=self.send_sems.at[3 * st + i],
            recv_sem=self.recv_sems.at[3 * st + i], device_id=peer, device_id_type=MESH)
            for i, (buf, rcv) in enumerate(zip(self.bufs, self.rcvs))]

    def pack_and_send(self):
        dg1_ref, dg1m_ref, dg2_ref, dg3_ref, dg4_ref, loss_ref, dmeta_ref, dsc_ref, dcw_ref, dpw_ref = self.ins
        a_buf, b_buf, c_buf = self.bufs

        def rowsum(v):
            return jnp.sum(v, axis=0, keepdims=True)

        a_buf[0, 0:1, :] = rowsum(dg1_ref[...] + dg1m_ref[...])
        a_buf[0, 1:2, :] = rowsum(dg2_ref[...])
        a_buf[0, 2:3, :] = rowsum(dg3_ref[...])
        a_buf[0, 3:4, :] = rowsum(dg4_ref[...])
        loss = jnp.sum(rowsum(loss_ref[...]), axis=1, keepdims=True) * (0.5 / D_MODEL)
        a_buf[0, 4:5, :] = jnp.broadcast_to(loss, (1, D_MODEL))
        a_buf[0, 5:8, :] = jnp.zeros((3, D_MODEL), F32)
        a_buf[0, 8:24, :] = dmeta_ref[...]
        b_buf[0, 0:1, :] = rowsum(dsc_ref[...])
        for k in range(3):
            b_buf[0, 1 + k:2 + k, :] = rowsum(dcw_ref[8 * k:8 * k + 8, :])
        b_buf[0, 4:8, :] = jnp.zeros((4, D_CONV), F32)
        c_buf[0] = dpw_ref[...]
        for cp in self._copies(0):
            cp.start()

    def combine(self, st):
        for cp in self._copies(st):
            cp.wait()
        if st < 2:
            for buf, rcv in zip(self.bufs, self.rcvs):
                buf[st + 1] = buf[st] + rcv[st]
            for cp in self._copies(st + 1):
                cp.start()
        else:
            for out, buf, rcv in zip(self.outs, self.bufs, self.rcvs):
                out[...] = buf[st] + rcv[st]


def _row_block(rows):
    for cand in (512, 448, 384, 352, 320, 256, 128, 64, 32, 16):
        if rows % cand == 0:
            return cand
    return rows


def _add_pairs(grad, recv, place):
    n_sh, rows2, cols = grad.shape
    hr = rows2 // 2
    br = _row_block(hr)

    def body(place_ref, a_ref, b_ref, o_ref):
        o_ref[...] = (a_ref[0] + b_ref[...]).astype(BF16)

    return pl.pallas_call(
        body, name="grad_add_pairs",
        grid_spec=pltpu.PrefetchScalarGridSpec(
            num_scalar_prefetch=1, grid=(n_sh, hr // br),
            in_specs=[pl.BlockSpec((1, 1, br, cols), lambda j, i, p: (j, p[1], i, 0)),
                      pl.BlockSpec((1, br, cols), lambda j, i, p: (j, i, 0))],
            out_specs=pl.BlockSpec((1, br, cols), lambda j, i, p: (j, i, 0))),
        out_shape=jax.ShapeDtypeStruct((n_sh, hr, cols), BF16), compiler_params=_cparams(2),
    )(place, grad.reshape(n_sh, 2, hr, cols), recv)


def _add_chips(grad, recv, rbuf, place):
    n_sh, rows2, cols = grad.shape
    hr = rows2 // 2
    br = _row_block(hr)

    def body(place_ref, a_ref, b_ref, r_ref, o_ref):
        own = a_ref[0, 0] + b_ref[0]
        o_ref[0] = ((own + r_ref[0].astype(F32)) + r_ref[1].astype(F32)) + r_ref[2].astype(F32)

    return pl.pallas_call(
        body, name="grad_add_chips",
        grid_spec=pltpu.PrefetchScalarGridSpec(
            num_scalar_prefetch=1, grid=(hr // br,),
            in_specs=[pl.BlockSpec((1, 1, br, cols), lambda i, p: (p[0], p[1], i, 0)),
                      pl.BlockSpec((1, br, cols), lambda i, p: (p[0], i, 0)),
                      pl.BlockSpec((3, br, cols), lambda i, p: (0, i, 0))],
            out_specs=pl.BlockSpec((1, br, cols), lambda i, p: (p[1], i, 0))),
        out_shape=jax.ShapeDtypeStruct((2, hr, cols), F32), compiler_params=_cparams(1),
    )(place, grad.reshape(n_sh, 2, hr, cols), recv, rbuf)


def _adamw_math(w, g, m, v):
    m2 = ADAM_B1 * m + (1.0 - ADAM_B1) * g
    v2 = ADAM_B2 * v + (1.0 - ADAM_B2) * (g * g)
    m_hat = m2 / (1.0 - ADAM_B1 ** ADAM_STEP)
    v_hat = v2 / (1.0 - ADAM_B2 ** ADAM_STEP)
    delta = -ADAM_LR * (m_hat / (jnp.sqrt(v_hat) + ADAM_EPS) + ADAM_WD * w)
    return delta, m2, v2


def _adamw_big(w, g, m, v):
    rows, cols = w.shape
    br = _row_block(rows)

    def body(w_ref, g_ref, m_ref, v_ref, d_ref, m2_ref, v2_ref):
        d, m2, v2 = _adamw_math(w_ref[...], g_ref[...], m_ref[...], v_ref[...])
        d_ref[...] = d
        m2_ref[...] = m2
        v2_ref[...] = v2

    spec = pl.BlockSpec((br, cols), lambda i: (i, 0))
    return pl.pallas_call(
        body, name="adamw_big", grid=(rows // br,),
        out_shape=[jax.ShapeDtypeStruct((rows, cols), F32)] * 3,
        in_specs=[spec] * 4, out_specs=[spec] * 3, compiler_params=_cparams(1),
    )(w, g, m, v)


def _adamw_small(groups):
    n = len(groups)

    def body(*refs):
        ins, outs = refs[:4 * n], refs[4 * n:]
        for i in range(n):
            w, g, m, v = (r[...] for r in ins[4 * i:4 * i + 4])
            d, m2, v2 = _adamw_math(w, g, m, v)
            outs[3 * i][...] = d
            outs[3 * i + 1][...] = m2
            outs[3 * i + 2][...] = v2

    vm = pl.BlockSpec(memory_space=pltpu.VMEM)
    flat = [a for grp in groups for a in grp]
    out_shape = [jax.ShapeDtypeStruct(grp[0].shape, F32) for grp in groups for _ in range(3)]
    outs = pl.pallas_call(body, name="adamw_small", out_shape=out_shape,
                          in_specs=[vm] * (4 * n), out_specs=[vm] * (3 * n))(*flat)
    return [tuple(outs[3 * i:3 * i + 3]) for i in range(n)]


def _load_weights(pairs, sem):
    for src, dst in pairs:
        cp = pltpu.make_async_copy(src, dst, sem)
        cp.start()
        cp.wait()


def _meta_fwd(meta_full, g1, win_all):
    def body(meta_ref, g1_ref, win_ref, z_ref):
        xm = meta_ref[...]
        a = (xm * _rstd(xm) * g1_ref[...]).astype(BF16)
        for j in range(N_CHIPS):
            z_ref[:, j * IN_SHARD:(j + 1) * IN_SHARD] = _dot(a, win_ref[j])

    vm = pl.BlockSpec(memory_space=pltpu.VMEM)
    return pl.pallas_call(body, name="meta_fwd", out_shape=jax.ShapeDtypeStruct((N_META, D_IN_PROJ), F32),
                          in_specs=[vm] * 3, out_specs=vm)(meta_full, g1, win_all)


def _mixer_fwd(x3, zmeta, g1, g2, convw, poolw, pscale, win_all, wout, ffn_shards):
    n_seq, seq, _ = x3.shape
    tm = min(TM_MIX_FWD, seq)
    n_t = seq // tm
    n_steps = n_seq * n_t
    n_ag = len(ffn_shards)

    def body(x_ref, zm_ref, g1_ref, g2_ref, cw_ref, pw_ref, ps_ref, win_hbm, wout_hbm, *rest):
        ag = _AllGather(rest[:n_ag], rest[n_ag + 7:2 * n_ag + 7], *rest[-2:])
        z_ref, m_ref, h1_ref, a_ref, conv_ref, pooled_ref, yc_ref = rest[n_ag:n_ag + 7]
        win_v, wout_v, cvb, pb, sem = rest[2 * n_ag + 7:-2]
        s, t = pl.program_id(0), pl.program_id(1)
        step = s * n_t + t

        @pl.when(step == 0)
        def _():
            ag.start()
            _load_weights([(win_hbm, win_v), (wout_hbm, wout_v)], sem)

        for a in range(n_ag):
            @pl.when(step == min(((a + 1) * n_steps) // n_ag, n_steps - 1))
            def _():
                ag.forward(a)

        xt = x_ref[0]
        a = (xt * _rstd(xt) * g1_ref[...]).astype(BF16)
        a_ref[...] = a
        zb = _dot(a, win_v[0])
        zc = _dot(a, win_v[1])
        zv = _dot(a, win_v[2])
        zp = _dot(a, win_v[3])
        z_ref[0, :, 0:IN_SHARD] = zb
        z_ref[0, :, IN_SHARD:2 * IN_SHARD] = zc
        z_ref[0, :, 2 * IN_SHARD:3 * IN_SHARD] = zv

        @pl.when(t == 0)
        def _():
            cvb[0:HALO, :] = zm_ref[:, IN_SHARD:2 * IN_SHARD] * zm_ref[:, 2 * IN_SHARD:3 * IN_SHARD]
            pb[0:HALO, :] = zm_ref[:, 3 * IN_SHARD:4 * IN_SHARD]

        @pl.when(t > 0)
        def _():
            cvb[0:HALO, :] = cvb[tm:tm + HALO, :]
            pb[0:HALO, :] = pb[tm:tm + HALO, :]

        cv = zc * zv
        cvb[HALO:HALO + tm, :] = cv
        pb[HALO:HALO + tm, :] = zp
        cw = cw_ref[...]
        conv = cw[0:1] * cvb[HALO - 2:HALO - 2 + tm, :] + cw[1:2] * cvb[HALO - 1:HALO - 1 + tm, :] + cw[2:3] * cv
        conv_ref[...] = conv
        parts = [(zb * conv).astype(BF16)]
        for g in range(N_POOL_GROUPS):
            pooled = _pool_fwd(pb, g, tm).astype(BF16)
            pooled_ref[:, _gcols(g)] = pooled
            parts.append((_dot(pooled, pw_ref[g]) * ps_ref[:, _gcols(g)]).astype(BF16))
        ycat = jnp.concatenate(parts, axis=1)
        yc_ref[...] = ycat
        m = _dot(ycat, wout_v[...])
        m_ref[0] = m
        h1_ref[0] = xt + m * _rstd(m) * g2_ref[...]

        @pl.when(step == n_steps - 1)
        def _():
            ag.finish()

    n_rows = n_seq * seq
    row = lambda c: pl.BlockSpec((1, tm, c), lambda s, t: (s, t, 0))
    row2 = lambda c: pl.BlockSpec((tm, c), lambda s, t: (s * n_t + t, 0))
    outs = pl.pallas_call(
        body, name="mixer_fwd", grid=(n_seq, n_t),
        out_shape=[jax.ShapeDtypeStruct((n_seq, seq, D_Z), F32), jax.ShapeDtypeStruct((n_seq, seq, D_MODEL), F32),
                   jax.ShapeDtypeStruct((n_seq, seq, D_MODEL), F32), jax.ShapeDtypeStruct((n_rows, D_MODEL), BF16),
                   jax.ShapeDtypeStruct((n_rows, D_CONV), F32), jax.ShapeDtypeStruct((n_rows, D_POOL), BF16),
                   jax.ShapeDtypeStruct((n_rows, D_MODEL), BF16)] + _AllGather.out_shape(ffn_shards),
        in_specs=[row(D_MODEL), _full((N_META, D_IN_PROJ)), _full((1, D_MODEL)), _full((1, D_MODEL)),
                  _full((3, D_CONV)), _full((N_POOL_GROUPS, POOL_GROUP, POOL_GROUP)), _full((1, D_POOL)), ANY, ANY]
        + [ANY] * n_ag,
        out_specs=[row(D_Z), row(D_MODEL), row(D_MODEL), row2(D_MODEL), row2(D_CONV), row2(D_POOL), row2(D_MODEL)]
        + [ANY] * n_ag,
        scratch_shapes=[pltpu.VMEM((N_CHIPS, D_MODEL, IN_SHARD), BF16), pltpu.VMEM((D_MODEL, D_MODEL), BF16),
                        pltpu.VMEM((HALO + tm, D_CONV), F32), pltpu.VMEM((HALO + tm, D_POOL), F32),
                        pltpu.SemaphoreType.DMA] + _AllGather.scratch(n_ag),
        compiler_params=_cparams(2),
    )(x3, zmeta, g1, g2, convw, poolw, pscale, win_all, wout, *ffn_shards)
    return outs[:7], _fill_own_slot(outs[7:], ffn_shards)


def _ffn_chunks():
    out, r0 = [], 0
    while r0 < D_FF:
        out.append((r0, min(FF_CHUNK, D_FF - r0)))
        r0 += FF_CHUNK
    return out


def _ffn_fwd_bwd(h1, target, g3, g4, wg_t, wu_t, wd):
    n_rows = h1.shape[0]
    tm = min(TM_FFN, n_rows)
    sub = min(SUB_FFN, tm)
    n_steps = n_rows // tm
    chunks = _ffn_chunks()
    subs = [slice(r, r + sub) for r in range(0, tm, sub)]

    def body(h1_ref, t_ref, g3_ref, g4_ref, wg_hbm, wu_hbm, wd_hbm,
             dh1_ref, f_ref, dd_ref, ds_hbm, du_hbm, gg_hbm, loss_ref, dg3_ref, dg4_ref,
             wg_v, wu_v, wd_v, s_sc, u_sc, ds_st, du_st, gg_st, sem, out_sems):
        i = pl.program_id(0)
        row0 = pl.multiple_of(i * tm, tm)

        def out_copy(k, stage, hbm):
            return pltpu.make_async_copy(stage, hbm.at[pl.ds(row0, tm), :], out_sems.at[k])

        @pl.when(i == 0)
        def _():
            _load_weights([(wg_hbm, wg_v), (wu_hbm, wu_v), (wd_hbm, wd_v)], sem)
            loss_ref[...] = jnp.zeros_like(loss_ref)
            dg3_ref[...] = jnp.zeros_like(dg3_ref)
            dg4_ref[...] = jnp.zeros_like(dg4_ref)

        @pl.when(i > 0)
        def _():
            out_copy(2, gg_st, gg_hbm).wait()

        g3v, g4v = g3_ref[...], g4_ref[...]

        def forward(rows):
            h1v = h1_ref[rows, :]
            r3 = _rstd(h1v)
            hh = h1v * r3
            f = (hh * g3v).astype(BF16)
            f_ref[rows, :] = f
            d = jnp.zeros((sub, D_MODEL), F32)
            for r0, sz in chunks:
                s = _dot_nt(f, wg_v[r0:r0 + sz, :])
                u = _dot_nt(f, wu_v[r0:r0 + sz, :])
                s_sc[rows, r0:r0 + sz] = s
                u_sc[rows, r0:r0 + sz] = u
                gc = (s * _sigmoid(s) * u).astype(BF16)
                gg_st[rows, r0:r0 + sz] = gc
                d = d + _dot(gc, wd_v[r0:r0 + sz, :])
            return h1v, r3, hh, d

        def loss_phase(rows, h1v, r3, hh, d):
            r4 = _rstd(d)
            dh = d * r4
            err = (h1v + dh * g4v) - t_ref[rows, :]
            loss_ref[...] += _rows8(err * err)
            dy = err * (1.0 / D_MODEL)
            dg4_ref[...] += _rows8(dy * dh)
            ddb = _rms_bwd(dy, dh, r4, g4v).astype(BF16)
            dd_ref[rows, :] = ddb
            return r3, hh, dy, ddb

        def backward(rows, r3, hh, dy, ddb):
            df = jnp.zeros((sub, D_MODEL), F32)
            for r0, sz in chunks:
                dgg = _dot_nt(ddb, wd_v[r0:r0 + sz, :])
                s = s_sc[rows, r0:r0 + sz]
                u = u_sc[rows, r0:r0 + sz]
                sig = _sigmoid(s)
                dsc = (dgg * u * (sig * (1.0 + s * (1.0 - sig)))).astype(BF16)
                duc = (dgg * (s * sig)).astype(BF16)
                ds_st[rows, r0:r0 + sz] = dsc
                du_st[rows, r0:r0 + sz] = duc
                df = df + _dot(dsc, wg_v[r0:r0 + sz, :]) + _dot(duc, wu_v[r0:r0 + sz, :])
            dg3_ref[...] += _rows8(df * hh)
            dh1_ref[rows, :] = dy + _rms_bwd(df, hh, r3, g3v)

        fwd = [forward(subs[0])]
        mid = []
        for j in range(1, len(subs)):
            mid.append(loss_phase(subs[j - 1], *fwd[j - 1]))
            fwd.append(forward(subs[j]))
        out_copy(2, gg_st, gg_hbm).start()

        @pl.when(i > 0)
        def _():
            out_copy(0, ds_st, ds_hbm).wait()
            out_copy(1, du_st, du_hbm).wait()

        for j in range(len(subs)):
            if j == len(subs) - 1:
                mid.append(loss_phase(subs[j], *fwd[j]))
            backward(subs[j], *mid[j])
        out_copy(0, ds_st, ds_hbm).start()
        out_copy(1, du_st, du_hbm).start()

        @pl.when(i == n_steps - 1)
        def _():
            out_copy(0, ds_st, ds_hbm).wait()
            out_copy(1, du_st, du_hbm).wait()
            out_copy(2, gg_st, gg_hbm).wait()

    row = pl.BlockSpec((tm, D_MODEL), lambda i: (i, 0))
    acc = _full((8, D_MODEL))
    act_bf = jax.ShapeDtypeStruct((n_rows, D_MODEL), BF16)
    ff_bf = jax.ShapeDtypeStruct((n_rows, D_FF), BF16)
    acc_shape = jax.ShapeDtypeStruct((8, D_MODEL), F32)
    w_vmem = pltpu.VMEM((D_FF, D_MODEL), BF16)
    stage = pltpu.VMEM((tm, D_FF), BF16)
    return pl.pallas_call(
        body, name="ffn_fwd_bwd", grid=(n_steps,),
        out_shape=[jax.ShapeDtypeStruct((n_rows, D_MODEL), F32), act_bf, act_bf, ff_bf, ff_bf, ff_bf,
                   acc_shape, acc_shape, acc_shape],
        in_specs=[row, row, _full((1, D_MODEL)), _full((1, D_MODEL)), ANY, ANY, ANY],
        out_specs=[row, row, row, ANY, ANY, ANY, acc, acc, acc],
        scratch_shapes=[w_vmem, w_vmem, w_vmem, pltpu.VMEM((tm, D_FF), F32), pltpu.VMEM((tm, D_FF), F32),
                        stage, stage, stage, pltpu.SemaphoreType.DMA, pltpu.SemaphoreType.DMA((3,))],
        compiler_params=pltpu.CompilerParams(dimension_semantics=("arbitrary",), vmem_limit_bytes=VMEM_LIMIT_FFN),
    )(h1, target, g3, g4, wg_t, wu_t, wd)


def _ffn_weight_grads(name, acts, other, exchanged):
    n_rows = other.shape[0]
    tk = min(TK_DW, n_rows)
    n_k = n_rows // tk
    half = D_FF // 2
    n_a, n_ex = len(acts), len(exchanged)

    def body(other_ref, *rest):
        act_refs = rest[:n_a]
        out_refs = rest[n_a + n_ex:2 * n_a + n_ex]
        c, k = pl.program_id(0), pl.program_id(1)
        if n_ex:
            ex = _ExchangeHalves(rest[n_a:n_a + n_ex], rest[2 * n_a + n_ex:2 * n_a + 2 * n_ex], *rest[-2:])

            @pl.when((c == 0) & (k == 0))
            def _():
                ex.start()

        @pl.when(k == 0)
        def _():
            for o in out_refs:
                o[...] = jnp.zeros_like(o)

        ov = other_ref[...]
        for a, o in zip(act_refs, out_refs):
            o[...] += _dot_tn(a[...], ov)

        if n_ex:
            @pl.when((c == 1) & (k == n_k - 1))
            def _():
                ex.finish()

    row = pl.BlockSpec((tk, D_MODEL), lambda c, k: (k, 0))
    ffrow = pl.BlockSpec((tk, half), lambda c, k: (k, c))
    out = pl.BlockSpec((half, D_MODEL), lambda c, k: (c, 0))
    outs = pl.pallas_call(
        body, name=name, grid=(2, n_k),
        out_shape=[jax.ShapeDtypeStruct((D_FF, D_MODEL), F32)] * n_a + _ExchangeHalves.out_shape(exchanged),
        in_specs=[row] + [ffrow] * n_a + [ANY] * n_ex, out_specs=[out] * n_a + [ANY] * n_ex,
        scratch_shapes=_ExchangeHalves.scratch(n_ex) if n_ex else [],
        compiler_params=_cparams(2),
    )(other, *acts, *exchanged)
    return outs[:n_a], outs[n_a:]


def _mixer_bwd(dh1, m3, z3, conv2, pooled2, x3, zmeta, g1, g2, convw, poolw, pscale, win_all, wout, exchanged,
               scattered):
    n_seq, seq, _ = x3.shape
    tm = min(TM_MIX_BWD, seq)
    sub = min(SUB_MIX_BWD, tm)
    n_t = seq // tm
    n_ex, n_sc = len(exchanged), len(scattered)
    n_cm = n_ex + n_sc

    def body(dh1_ref, m_ref, z_ref, conv_ref, pooled_ref, x_ref, zm_ref, g1_ref, g2_ref, cw_ref, pw_ref, ps_ref,
             win_hbm, wout_hbm, *rest):
        outs0 = n_cm + 9
        ex = _ExchangeHalves(rest[:n_ex], rest[outs0:outs0 + n_ex], *rest[-4:-2])
        sc = _ScatterToChips(rest[n_ex:n_cm], rest[outs0 + n_ex:outs0 + n_cm], *rest[-2:])
        dx_ref, dz_ref, dm_ref, dg1_ref, dg2_ref, dsc_ref, dcw_ref, dpw_ref, dzm_ref = rest[n_cm:outs0]
        win_v, wout_v, dcb, dqb, mcb, mqb, sem = rest[outs0 + n_cm:-4]
        s, i = pl.program_id(0), pl.program_id(1)
        tr = n_t - 1 - i

        @pl.when((s == 0) & (i == 0))
        def _():
            sc.start()
            ex.start()
            _load_weights([(win_hbm, win_v), (wout_hbm, wout_v)], sem)
            for ref in (dg1_ref, dg2_ref, dsc_ref, dcw_ref, dpw_ref, dzm_ref):
                ref[...] = jnp.zeros_like(ref)

        @pl.when(i == 0)
        def _():
            dcb[tm:tm + HALO, :] = jnp.zeros((HALO, D_CONV), F32)
            dqb[tm:tm + HALO, :] = jnp.zeros((HALO, D_POOL), F32)

        @pl.when(i > 0)
        def _():
            dcb[tm:tm + HALO, :] = dcb[0:HALO, :]
            dqb[tm:tm + HALO, :] = dqb[0:HALO, :]

        g1v, g2v = g1_ref[...], g2_ref[...]
        cw = cw_ref[...]

        for r0 in range(tm - sub, -1, -sub):
            rows = slice(r0, r0 + sub)
            dh1v = dh1_ref[0, rows, :]
            mv = m_ref[0, rows, :]
            r2 = _rstd(mv)
            mh = mv * r2
            dg2_ref[...] += _rows8(dh1v * mh)
            dmb = _rms_bwd(dh1v, mh, r2, g2v).astype(BF16)
            dm_ref[rows, :] = dmb
            dyc = _dot_nt(dmb, wout_v[...])
            dyconv = dyc[:, 0:D_CONV]

            for g in range(N_POOL_GROUPS):
                pooled = pooled_ref[rows, _gcols(g)]
                mixed = _dot(pooled, pw_ref[g])
                scale = ps_ref[:, _gcols(g)]
                dyp = dyc[:, D_CONV + g * POOL_GROUP:D_CONV + (g + 1) * POOL_GROUP]
                dsc_ref[:, _gcols(g)] += _rows8(dyp * mixed)
                dmix = (dyp * scale).astype(BF16)
                dpw_ref[g] += _dot_tn(pooled, dmix)
                dqb[rows, _gcols(g)] = _dot_nt(dmix, pw_ref[g])

            zb = z_ref[0, rows, 0:IN_SHARD]
            zc = z_ref[0, rows, IN_SHARD:2 * IN_SHARD]
            zv = z_ref[0, rows, 2 * IN_SHARD:3 * IN_SHARD]
            dconv = dyconv * zb
            dcb[rows, :] = dconv
            d1 = dcb[r0 + 1:r0 + 1 + sub, :]
            d2 = dcb[r0 + 2:r0 + 2 + sub, :]
            dcv = cw[2:3] * dconv + cw[1:2] * d1 + cw[0:1] * d2
            cv = zc * zv
            dcw_ref[0:8, :] += _rows8(cv * d2)
            dcw_ref[8:16, :] += _rows8(cv * d1)
            dcw_ref[16:24, :] += _rows8(cv * dconv)
            dzs = [(dyconv * conv_ref[rows, :]).astype(BF16), (dcv * zv).astype(BF16), (dcv * zc).astype(BF16),
                   jnp.concatenate([_pool_bwd(dqb, g, r0, sub) for g in range(N_POOL_GROUPS)], axis=1).astype(BF16)]
            da = jnp.zeros((sub, D_MODEL), F32)
            for j in range(N_CHIPS):
                dz_ref[j, rows, :] = dzs[j]
                da = da + _dot_nt(dzs[j], win_v[j])
            xt = x_ref[0, rows, :]
            r1 = _rstd(xt)
            xh = xt * r1
            dg1_ref[...] += _rows8(da * xh)
            dx_ref[0, rows, :] = dh1v + _rms_bwd(da, xh, r1, g1v)

        @pl.when(tr == 0)
        def _():
            mcb[0:HALO, :] = jnp.zeros((HALO, D_CONV), F32)
            mqb[0:HALO, :] = jnp.zeros((HALO, D_POOL), F32)
            mcb[HALO:2 * HALO, :] = dcb[0:HALO, :]
            mqb[HALO:2 * HALO, :] = dqb[0:HALO, :]
            m1 = mcb[1:1 + HALO, :]
            m2 = mcb[2:2 + HALO, :]
            zc_m = zm_ref[:, IN_SHARD:2 * IN_SHARD]
            zv_m = zm_ref[:, 2 * IN_SHARD:3 * IN_SHARD]
            cv_m = zc_m * zv_m
            dcw_ref[0:8, :] += _rows8(cv_m * m2)
            dcw_ref[8:16, :] += _rows8(cv_m * m1)
            dcv_m = cw[1:2] * m1 + cw[0:1] * m2
            dzm_ref[:, IN_SHARD:2 * IN_SHARD] += dcv_m * zv_m
            dzm_ref[:, 2 * IN_SHARD:3 * IN_SHARD] += dcv_m * zc_m
            dzm_ref[:, 3 * IN_SHARD:4 * IN_SHARD] += jnp.concatenate(
                [_pool_bwd(mqb, g, 0, HALO) for g in range(N_POOL_GROUPS)], axis=1)

        @pl.when((s == n_seq - 1) & (i == n_t - 1))
        def _():
            ex.finish()
            sc.finish()

    row3 = lambda c: pl.BlockSpec((1, tm, c), lambda s, i: (s, n_t - 1 - i, 0))
    row2 = lambda c: pl.BlockSpec((tm, c), lambda s, i: (s * n_t + n_t - 1 - i, 0))
    n_rows = n_seq * seq
    outs = pl.pallas_call(
        body, name="mixer_bwd", grid=(n_seq, n_t),
        out_shape=[jax.ShapeDtypeStruct((n_seq, seq, D_MODEL), F32),
                   jax.ShapeDtypeStruct((N_CHIPS, n_rows, IN_SHARD), BF16), jax.ShapeDtypeStruct((n_rows, D_MODEL), BF16),
                   jax.ShapeDtypeStruct((8, D_MODEL), F32), jax.ShapeDtypeStruct((8, D_MODEL), F32),
                   jax.ShapeDtypeStruct((8, D_POOL), F32), jax.ShapeDtypeStruct((24, D_CONV), F32),
                   jax.ShapeDtypeStruct((N_POOL_GROUPS, POOL_GROUP, POOL_GROUP), F32),
                   jax.ShapeDtypeStruct((N_META, D_IN_PROJ), F32)]
        + _ExchangeHalves.out_shape(exchanged) + _ScatterToChips.out_shape(scattered),
        in_specs=[row3(D_MODEL), row3(D_MODEL), row3(D_Z), row2(D_CONV), row2(D_POOL), row3(D_MODEL),
                  _full((N_META, D_IN_PROJ)), _full((1, D_MODEL)), _full((1, D_MODEL)), _full((3, D_CONV)),
                  _full((N_POOL_GROUPS, POOL_GROUP, POOL_GROUP)), _full((1, D_POOL)), ANY, ANY] + [ANY] * n_cm,
        out_specs=[row3(D_MODEL), pl.BlockSpec((N_CHIPS, tm, IN_SHARD), lambda s, i: (0, s * n_t + n_t - 1 - i, 0)),
                   row2(D_MODEL),
                   _full((8, D_MODEL)), _full((8, D_MODEL)), _full((8, D_POOL)), _full((24, D_CONV)),
                   _full((N_POOL_GROUPS, POOL_GROUP, POOL_GROUP)), _full((N_META, D_IN_PROJ))] + [ANY] * n_cm,
        scratch_shapes=[pltpu.VMEM((N_CHIPS, D_MODEL, IN_SHARD), BF16), pltpu.VMEM((D_MODEL, D_MODEL), BF16),
                        pltpu.VMEM((tm + HALO, D_CONV), F32), pltpu.VMEM((tm + HALO, D_POOL), F32),
                        pltpu.VMEM((2 * HALO, D_CONV), F32), pltpu.VMEM((2 * HALO, D_POOL), F32),
                        pltpu.SemaphoreType.DMA] + _ExchangeHalves.scratch(n_ex) + _ScatterToChips.scratch(n_sc),
        compiler_params=_cparams(2),
    )(dh1, m3, z3, conv2, pooled2, x3, zmeta, g1, g2, convw, poolw, pscale, win_all, wout, *exchanged, *scattered)
    return outs[:9], outs[9:9 + n_ex], outs[9 + n_ex:]


def _meta_bwd(dzm, meta_full, g1, win_all):
    def body(dzm_ref, meta_ref, g1_ref, win_ref, dmeta_ref, dg1_ref, a_ref, dzb_ref):
        xm = meta_ref[...]
        r = _rstd(xm)
        xh = xm * r
        g1v = g1_ref[...]
        a_ref[...] = (xh * g1v).astype(BF16)
        da = jnp.zeros((N_META, D_MODEL), F32)
        for j in range(N_CHIPS):
            dzj = dzm_ref[:, j * IN_SHARD:(j + 1) * IN_SHARD].astype(BF16)
            dzb_ref[j] = dzj
            da = da + _dot_nt(dzj, win_ref[j])
        dg1_ref[...] = _rows8(da * xh)
        dmeta_ref[...] = _rms_bwd(da, xh, r, g1v)

    vm = pl.BlockSpec(memory_space=pltpu.VMEM)
    return pl.pallas_call(
        body, name="meta_bwd",
        out_shape=[jax.ShapeDtypeStruct((N_META, D_MODEL), F32), jax.ShapeDtypeStruct((8, D_MODEL), F32),
                   jax.ShapeDtypeStruct((N_META, D_MODEL), BF16), jax.ShapeDtypeStruct((N_CHIPS, N_META, IN_SHARD), BF16)],
        in_specs=[vm] * 4, out_specs=[vm] * 4,
    )(dzm, meta_full, g1, win_all)


def _mixer_weight_grads(a, dz, ycat, dm, a_meta, dz_meta, ffn_sums, small):
    n_rows = a.shape[0]
    tk = min(TK_DW, n_rows)
    n_k = n_rows // tk
    n_sc, n_sm = len(ffn_sums), _AllReduceSmall.N_IN

    def body(a_ref, dz_ref, yc_ref, dm_ref, am_ref, dzm_ref, *rest):
        ins, outs, scratch = rest[:n_sc + n_sm], rest[n_sc + n_sm:2 * n_sc + n_sm + 5], rest[2 * n_sc + n_sm + 5:]
        dwin_ref, dwout_ref = outs[:2]
        scatter = _ScatterToChips(ins[:n_sc], outs[2:2 + n_sc], *scratch[:2])
        reduce_small = _AllReduceSmall(ins[n_sc:], outs[2 + n_sc:], scratch[2:])
        k = pl.program_id(0)

        @pl.when(k == 0)
        def _():
            scatter.start()
            reduce_small.pack_and_send()
            am_t = am_ref[...].T
            for j in range(N_CHIPS):
                dwin_ref[j] = _dot(am_t, dzm_ref[j])
            dwout_ref[...] = jnp.zeros_like(dwout_ref)

        for st in range(2):
            @pl.when(k == ((st + 1) * n_k) // 3)
            def _():
                reduce_small.combine(st)

        a_t = a_ref[...].T
        for j in range(N_CHIPS):
            dwin_ref[j] += _dot(a_t, dz_ref[j])
        dwout_ref[...] += _dot_tn(yc_ref[...], dm_ref[...])

        @pl.when(k == n_k - 1)
        def _():
            reduce_small.combine(2)
            scatter.finish()

    row = pl.BlockSpec((tk, D_MODEL), lambda k: (k, 0))
    outs = pl.pallas_call(
        body, name="mixer_weight_grads", grid=(n_k,),
        out_shape=[jax.ShapeDtypeStruct((N_CHIPS, D_MODEL, IN_SHARD), F32),
                   jax.ShapeDtypeStruct((D_MODEL, D_MODEL), F32)] + _ScatterToChips.out_shape(ffn_sums)
        + _AllReduceSmall.out_shape(),
        in_specs=[row, pl.BlockSpec((N_CHIPS, tk, IN_SHARD), lambda k: (0, k, 0)), row, row,
                  _full((N_META, D_MODEL)), _full((N_CHIPS, N_META, IN_SHARD))] + [ANY] * n_sc
        + [_full(s.shape) for s in small],
        out_specs=[_full((N_CHIPS, D_MODEL, IN_SHARD)), _full((D_MODEL, D_MODEL))] + [ANY] * n_sc
        + [_full(s) for s in _AllReduceSmall.SHAPES],
        scratch_shapes=_ScatterToChips.scratch(n_sc) + _AllReduceSmall.scratch(),
        compiler_params=_cparams(1),
    )(a, dz, ycat, dm, a_meta, dz_meta, *ffn_sums, *small)
    return ([outs[0], outs[1].reshape(N_CHIPS, OUT_SHARD, D_MODEL)], outs[2:2 + n_sc], outs[2 + n_sc:])


def kernel(x, meta_tokens, norm_mix_pre, w_in, conv_w, pool_w, pool_scale, w_out, norm_mix_post, norm_ffn_pre, w_gate, w_up, w_down, norm_ffn_post, loss_target, m_meta_tokens, m_norm_mix_pre, m_w_in, m_conv_w, m_pool_w, m_pool_scale, m_w_out, m_norm_mix_post, m_norm_ffn_pre, m_w_gate, m_w_up, m_w_down, m_norm_ffn_post, v_meta_tokens, v_norm_mix_pre, v_w_in, v_conv_w, v_pool_w, v_pool_scale, v_w_out, v_norm_mix_post, v_norm_ffn_pre, v_w_gate, v_w_up, v_w_down, v_norm_ffn_post):
    n_seq, seq, _ = x.shape
    n_rows = n_seq * seq
    chip = 2 * lax.axis_index("x") + lax.axis_index("y")
    meta_cols = D_MODEL // N_CHIPS
    conv_cols = D_CONV // N_CHIPS

    small = jnp.zeros((2 * HALO, meta_cols), F32)
    small = small.at[0:N_META, :].set(meta_tokens).at[N_META:N_META + 3, 0:conv_cols].set(conv_w[0])
    win_all, wout_all, small_all = _all_gather_shards([w_in[0].astype(BF16), w_out[0].astype(BF16), small])
    meta_full = small_all[:, 0:N_META, :].transpose(1, 0, 2).reshape(N_META, D_MODEL)
    conv_full = small_all[:, N_META:N_META + 3, 0:conv_cols].transpose(1, 0, 2).reshape(3, D_CONV)
    wout_full = wout_all.reshape(D_MODEL, D_MODEL)
    poolw_bf = pool_w[0].astype(BF16)
    pscale = pool_scale
    g1, g2, g3, g4 = norm_mix_pre, norm_mix_post, norm_ffn_pre, norm_ffn_post
    place = jnp.stack([chip, lax.axis_index("c")]).astype(jnp.int32)

    zmeta = _meta_fwd(meta_full, g1, win_all)
    (z3, m3, h1, a_bf, conv2, pooled2, yc_bf), ffn_w = _mixer_fwd(
        x, zmeta, g1, g2, conv_full, poolw_bf, pscale, win_all, wout_full,
        [w_gate[0].T.astype(BF16), w_up[0].T.astype(BF16), w_down[0].astype(BF16)])
    wg_t, wu_t, wd_full = [w.reshape(D_FF, D_MODEL) for w in ffn_w]
    dh1, f_bf, dd_bf, ds_bf, du_bf, gg_bf, lossp, dg3p, dg4p = _ffn_fwd_bwd(
        h1.reshape(n_rows, D_MODEL), loss_target.reshape(n_rows, D_MODEL), g3, g4, wg_t, wu_t, wd_full)
    as_shards = lambda g: g.reshape(N_CHIPS, FF_SHARD, D_MODEL)
    (dwd,), _ = _ffn_weight_grads("ffn_weight_grads_down", [gg_bf], dd_bf, [])
    dwd = as_shards(dwd)
    (dwg_t,), (dwd_recv,) = _ffn_weight_grads("ffn_weight_grads_gate", [ds_bf], f_bf, [dwd])
    dwg_t = as_shards(dwg_t)
    (dwu_t,), (dwg_recv,) = _ffn_weight_grads("ffn_weight_grads_up", [du_bf], f_bf, [dwg_t])
    dwu_t = as_shards(dwu_t)
    ((grad_x, dz_bf, dm_bf, dg1p, dg2p, dscp, dcwp, dpw, dzm), (dwu_recv,), (dwd_rbuf, dwg_rbuf)) = _mixer_bwd(
        dh1.reshape(n_seq, seq, D_MODEL), m3, z3, conv2, pooled2, x, zmeta, g1, g2, conv_full, poolw_bf, pscale,
        win_all, wout_full, [dwu_t], [_add_pairs(dwd, dwd_recv, place), _add_pairs(dwg_t, dwg_recv, place)])
    dmeta, dg1m, a_meta, dz_meta = _meta_bwd(dzm, meta_full, g1, win_all)
    mix_grads, (dwu_rbuf,), (a_red, b_red, c_red) = _mixer_weight_grads(
        a_bf, dz_bf, yc_bf, dm_bf, a_meta, dz_meta, [_add_pairs(dwu_t, dwu_recv, place)],
        [dg1p, dg1m, dg2p, dg3p, dg4p, lossp, dmeta, dscp, dcwp, dpw.reshape(SMALL_C_ROWS, POOL_GROUP)])

    mix_recvs = _exchange_halves(mix_grads)
    mix_rbufs = _scatter_to_chips([_add_pairs(g, r, place) for g, r in zip(mix_grads, mix_recvs)])
    grads = list(mix_grads) + [dwg_t, dwu_t, dwd]
    recvs = list(mix_recvs) + [dwg_recv, dwu_recv, dwd_recv]
    rbufs = list(mix_rbufs) + [dwg_rbuf, dwu_rbuf, dwd_rbuf]
    reduced = _gather_halves([_add_chips(g, r, rb, place) for g, r, rb in zip(grads, recvs, rbufs)])
    g_win, g_wout, g_wg_t, g_wu_t, g_wd = [r.reshape(2 * r.shape[1], r.shape[2]) for r in reduced]

    loss = a_red[4, 0]
    g_g1, g_g2, g_g3, g_g4 = a_red[0:1], a_red[1:2], a_red[2:3], a_red[3:4]
    g_meta = lax.dynamic_slice(a_red, (8, chip * meta_cols), (N_META, meta_cols))
    g_pscale = b_red[0:1]
    g_conv = lax.dynamic_slice(b_red, (1, chip * conv_cols), (3, conv_cols))
    g_poolw = c_red

    big = [(w_in[0], g_win, m_w_in[0], v_w_in[0]), (w_out[0], g_wout, m_w_out[0], v_w_out[0]),
           (w_gate[0].T, g_wg_t, m_w_gate[0].T, v_w_gate[0].T), (w_up[0].T, g_wu_t, m_w_up[0].T, v_w_up[0].T),
           (w_down[0], g_wd, m_w_down[0], v_w_down[0])]
    big_out = [_adamw_big(w, g, m, v) for (w, g, m, v) in big]
    big_out[2] = [o.T for o in big_out[2]]
    big_out[3] = [o.T for o in big_out[3]]
    g_wg, g_wu = g_wg_t.T, g_wu_t.T
    small_groups = [
        (meta_tokens, g_meta, m_meta_tokens, v_meta_tokens),
        (g1, g_g1, m_norm_mix_pre, v_norm_mix_pre),
        (conv_w[0], g_conv, m_conv_w[0], v_conv_w[0]),
        (pool_w.reshape(SMALL_C_ROWS, POOL_GROUP), g_poolw, m_pool_w.reshape(SMALL_C_ROWS, POOL_GROUP),
         v_pool_w.reshape(SMALL_C_ROWS, POOL_GROUP)),
        (pool_scale, g_pscale, m_pool_scale, v_pool_scale),
        (g2, g_g2, m_norm_mix_post, v_norm_mix_post),
        (g3, g_g3, m_norm_ffn_pre, v_norm_ffn_pre),
        (g4, g_g4, m_norm_ffn_post, v_norm_ffn_post),
    ]
    small_out = _adamw_small(small_groups)

    grads_out = [g_meta, g_g1, g_win[None], g_conv[None], g_poolw.reshape(pool_w.shape), g_pscale, g_wout[None],
                 g_g2, g_g3, g_wg[None], g_wu[None], g_wd[None], g_g4]
    s_meta, s_g1, s_conv, s_poolw, s_pscale, s_g2, s_g3, s_g4 = small_out
    b_win, b_wout, b_wg, b_wu, b_wd = big_out

    def leaf(k):
        return [s_meta[k], s_g1[k], b_win[k][None], s_conv[k][None], s_poolw[k].reshape(pool_w.shape), s_pscale[k],
                b_wout[k][None], s_g2[k], s_g3[k], b_wg[k][None], b_wu[k][None], b_wd[k][None], s_g4[k]]

    return (loss, grad_x, *grads_out, *leaf(0), *leaf(1), *leaf(2))
```

```python
import jax
import jax.numpy as jnp
from jax import lax
from jax.experimental import pallas as pl
from jax.experimental.pallas import tpu as pltpu

F32 = jnp.float32
BF16 = jnp.bfloat16
MESH = pl.DeviceIdType.MESH

D_MODEL = 1024
D_CONV = 512
D_POOL = 512
POOL_GROUP = 128
N_POOL_GROUPS = 4
D_IN_PROJ = 2048
D_FF = 2816
N_CHIPS = 4
FF_SHARD = D_FF // N_CHIPS
IN_SHARD = D_IN_PROJ // N_CHIPS
OUT_SHARD = D_MODEL // N_CHIPS
D_Z = 3 * IN_SHARD
N_META = 16
HALO = 16
RMS_EPS = 1e-6

ADAM_LR = 0.001
ADAM_B1 = 0.9
ADAM_B2 = 0.999
ADAM_EPS = 1e-08
ADAM_WD = 0.01
ADAM_STEP = 10

TM_MIX_FWD = 512
TM_MIX_BWD = 512
SUB_MIX_BWD = 512
TM_FFN = 256
TK_DW = 1024
FF_CHUNK = 1024
VMEM_LIMIT = 56 * 1024 * 1024


def _cparams(n_grid):
    return pltpu.CompilerParams(dimension_semantics=("arbitrary",) * n_grid, vmem_limit_bytes=VMEM_LIMIT)


def _dot(a, b):
    return jnp.dot(a, b, preferred_element_type=F32)


def _dot_nt(a, b):
    return lax.dot_general(a, b, (((1,), (1,)), ((), ())), preferred_element_type=F32)


def _dot_tn(a, b):
    return lax.dot_general(a, b, (((0,), (0,)), ((), ())), preferred_element_type=F32)


def _rows8(v):
    r, c = v.shape
    return v.reshape(r // 8, 8, c).sum(axis=0)


def _rstd(v):
    return lax.rsqrt(jnp.mean(v * v, axis=-1, keepdims=True) + RMS_EPS)


def _rms_bwd(dy, xhat, rstd, gain):
    dyg = dy * gain
    return rstd * (dyg - xhat * jnp.mean(dyg * xhat, axis=-1, keepdims=True))


def _sigmoid(v):
    return 1.0 / (1.0 + jnp.exp(-v))


def _gcols(g):
    return slice(g * POOL_GROUP, (g + 1) * POOL_GROUP)


def _window_sum(e, g, ahead):
    n = e.shape[0]
    w = e
    for level in range(g + 1):
        shift = 1 << level
        w = w + pltpu.roll(w, (n - shift) if ahead else shift, 0)
    return w


def _pool_fwd(pb, g, n):
    e = pb[0:HALO + n, _gcols(g)]
    return _window_sum(e, g, False)[HALO:, :] * (1.0 / (2 << g)) - e[HALO:, :]


def _pool_bwd(qb, g, r0, n):
    e = qb[r0:r0 + n + HALO, _gcols(g)]
    return _window_sum(e, g, True)[0:n, :] * (1.0 / (2 << g)) - e[0:n, :]


def _full(shape):
    nd = len(shape)
    return pl.BlockSpec(shape, lambda *_: (0,) * nd)


ANY = pl.BlockSpec(memory_space=pl.ANY)


def _mesh_pos():
    x, y, c = lax.axis_index("x"), lax.axis_index("y"), lax.axis_index("c")
    chips = [(1 - x, y), (x, 1 - y), (1 - x, 1 - y)]
    return x, y, c, chips


def _half(ref, h):
    hr = ref.shape[0] // 2
    return ref.at[pl.ds(h * hr, hr), :]


class _AllGather:
    def __init__(self, ins, outs, send_sems, recv_sems):
        self.ins, self.outs, self.send_sems, self.recv_sems = ins, outs, send_sems, recv_sems
        self.n = len(ins)

    @staticmethod
    def scratch(n):
        return [pltpu.SemaphoreType.DMA((6 * n,)), pltpu.SemaphoreType.DMA((6 * n,))]

    @staticmethod
    def out_shape(shards):
        return [jax.ShapeDtypeStruct((N_CHIPS,) + s.shape, s.dtype) for s in shards]

    def _copy(self, a, k, src, dst, to):
        return pltpu.make_async_remote_copy(src_ref=src, dst_ref=dst, send_sem=self.send_sems.at[6 * a + k],
                                            recv_sem=self.recv_sems.at[6 * a + k], device_id=to, device_id_type=MESH)

    def _ici(self, a, k):
        x, y, c, chips = _mesh_pos()
        return self._copy(a, k, _half(self.ins[a], c), _half(self.outs[a].at[2 * x + y], c), (*chips[k], c))

    def _d2d(self, a, k, h):
        x, y, c, chips = _mesh_pos()
        slot = _half(self.outs[a].at[2 * chips[k][0] + chips[k][1]], h)
        return self._copy(a, 3 + k, slot, slot, (x, y, 1 - c))

    def start(self):
        for a in range(self.n):
            for k in range(3):
                self._ici(a, k).start()

    def forward(self, a):
        c = lax.axis_index("c")
        for k in range(3):
            self._ici(a, k).wait_recv()
            self._d2d(a, k, c).start()

    def finish(self):
        c = lax.axis_index("c")
        for a in range(self.n):
            for k in range(3):
                self._d2d(a, k, 1 - c).wait_recv()
        for a in range(self.n):
            for k in range(3):
                self._ici(a, k).wait_send()
                self._d2d(a, k, c).wait_send()


def _fill_own_slot(gathered, shards):
    chip = 2 * lax.axis_index("x") + lax.axis_index("y")
    return [lax.dynamic_update_slice(o, s[None], (chip, 0, 0)) for o, s in zip(gathered, shards)]


def _all_gather_shards(shards):
    n = len(shards)

    def body(*refs):
        ag = _AllGather(refs[:n], refs[n:2 * n], *refs[2 * n:])
        ag.start()
        for a in range(n):
            ag.forward(a)
        ag.finish()

    outs = pl.pallas_call(
        body, name="all_gather_weights", out_shape=_AllGather.out_shape(shards),
        in_specs=[ANY] * n, out_specs=[ANY] * n, scratch_shapes=_AllGather.scratch(n),
    )(*shards)
    return _fill_own_slot(outs, shards)


class _ExchangeHalves:
    def __init__(self, ins, recvs, send_sems, recv_sems):
        self.ins, self.recvs, self.send_sems, self.recv_sems = ins, recvs, send_sems, recv_sems

    @staticmethod
    def scratch(n):
        return [pltpu.SemaphoreType.DMA((n,)), pltpu.SemaphoreType.DMA((n,))]

    @staticmethod
    def out_shape(grads):
        return [jax.ShapeDtypeStruct((g.shape[0], g.shape[1] // 2, g.shape[2]), g.dtype) for g in grads]

    def _copies(self):
        x, y, c, _ = _mesh_pos()
        out = []
        for a, (src, dst) in enumerate(zip(self.ins, self.recvs)):
            hr = src.shape[1] // 2
            out.append(pltpu.make_async_remote_copy(
                src_ref=src.at[:, pl.ds((1 - c) * hr, hr), :], dst_ref=dst, send_sem=self.send_sems.at[a],
                recv_sem=self.recv_sems.at[a], device_id=(x, y, 1 - c), device_id_type=MESH))
        return out

    def start(self):
        for cp in self._copies():
            cp.start()

    def finish(self):
        for cp in self._copies():
            cp.wait()


def _exchange_halves(grads):
    n = len(grads)

    def body(*refs):
        ex = _ExchangeHalves(refs[:n], refs[n:2 * n], *refs[2 * n:])
        ex.start()
        ex.finish()

    return pl.pallas_call(
        body, name="grad_exchange_halves", out_shape=_ExchangeHalves.out_shape(grads),
        in_specs=[ANY] * n, out_specs=[ANY] * n, scratch_shapes=_ExchangeHalves.scratch(n),
    )(*grads)


class _ScatterToChips:
    def __init__(self, ins, rbufs, send_sems, recv_sems):
        self.ins, self.rbufs, self.send_sems, self.recv_sems = ins, rbufs, send_sems, recv_sems

    @staticmethod
    def scratch(n):
        return [pltpu.SemaphoreType.DMA((3 * n,)), pltpu.SemaphoreType.DMA((3 * n,))]

    @staticmethod
    def out_shape(sums):
        return [jax.ShapeDtypeStruct((3,) + s.shape[1:], BF16) for s in sums]

    def _copies(self):
        x, y, c, chips = _mesh_pos()
        out = []
        for a, (src, dst) in enumerate(zip(self.ins, self.rbufs)):
            for k, chip in enumerate(chips):
                out.append(pltpu.make_async_remote_copy(
                    src_ref=src.at[2 * chip[0] + chip[1]], dst_ref=dst.at[k], send_sem=self.send_sems.at[3 * a + k],
                    recv_sem=self.recv_sems.at[3 * a + k], device_id=(*chip, c), device_id_type=MESH))
        return out

    def start(self):
        for cp in self._copies():
            cp.start()

    def finish(self):
        for cp in self._copies():
            cp.wait()


def _gather_halves(halves):
    n = len(halves)

    def body(*refs):
        ins, outs = refs[:n], refs[n:2 * n]
        send_sems, recv_sems = refs[2 * n:]
        x, y, c, _ = _mesh_pos()
        sib = (x, y, 1 - c)
        remote = [pltpu.make_async_remote_copy(src_ref=ins[a].at[c], dst_ref=outs[a].at[c],
                                               send_sem=send_sems.at[a], recv_sem=recv_sems.at[a],
                                               device_id=sib, device_id_type=MESH) for a in range(n)]
        for cp in remote:
            cp.start()
        for a in range(n):
            pltpu.make_async_remote_copy(src_ref=ins[a].at[1 - c], dst_ref=outs[a].at[1 - c], send_sem=send_sems.at[a],
                                         recv_sem=recv_sems.at[a], device_id=sib, device_id_type=MESH).wait_recv()
        for cp in remote:
            cp.wait_send()

    return pl.pallas_call(
        body, name="grad_gather_halves",
        out_shape=[jax.ShapeDtypeStruct(h.shape, F32) for h in halves],
        in_specs=[ANY] * n, out_specs=[ANY] * n, input_output_aliases={a: a for a in range(n)},
        scratch_shapes=[pltpu.SemaphoreType.DMA((n,)), pltpu.SemaphoreType.DMA((n,))],
    )(*halves)


SMALL_A_ROWS = 24
SMALL_B_ROWS = 8
SMALL_C_ROWS = N_POOL_GROUPS * POOL_GROUP


class _AllReduceSmall:
    N_IN = 10
    SHAPES = [(SMALL_A_ROWS, D_MODEL), (SMALL_B_ROWS, D_CONV), (SMALL_C_ROWS, POOL_GROUP)]

    def __init__(self, ins, outs, scratch):
        self.ins, self.outs = ins, outs
        self.bufs, self.rcvs, self.send_sems, self.recv_sems = scratch[:3], scratch[3:6], scratch[6], scratch[7]

    @classmethod
    def scratch(cls):
        return ([pltpu.VMEM((3,) + s, F32) for s in cls.SHAPES] + [pltpu.VMEM((3,) + s, F32) for s in cls.SHAPES]
                + [pltpu.SemaphoreType.DMA((9,)), pltpu.SemaphoreType.DMA((9,))])

    @classmethod
    def out_shape(cls):
        return [jax.ShapeDtypeStruct(s, F32) for s in cls.SHAPES]

    def _copies(self, st):
        x, y, c, _ = _mesh_pos()
        peer = [(x, y, 1 - c), (1 - x, y, c), (x, 1 - y, c)][st]
        return [pltpu.make_async_remote_copy(
            src_ref=buf.at[st], dst_ref=rcv.at[st], send_sem=self.send_sems.at[3 * st + i],
            recv_sem=self.recv_sems.at[3 * st + i], device_id=peer, device_id_type=MESH)
            for i, (buf, rcv) in enumerate(zip(self.bufs, self.rcvs))]

    def pack_and_send(self):
        dg1_ref, dg1m_ref, dg2_ref, dg3_ref, dg4_ref, loss_ref, dmeta_ref, dsc_ref, dcw_ref, dpw_ref = self.ins
        a_buf, b_buf, c_buf = self.bufs

        def rowsum(v):
            return jnp.sum(v, axis=0, keepdims=True)

        a_buf[0, 0:1, :] = rowsum(dg1_ref[...] + dg1m_ref[...])
        a_buf[0, 1:2, :] = rowsum(dg2_ref[...])
        a_buf[0, 2:3, :] = rowsum(dg3_ref[...])
        a_buf[0, 3:4, :] = rowsum(dg4_ref[...])
        loss = jnp.sum(rowsum(loss_ref[...]), axis=1, keepdims=True) * (0.5 / D_MODEL)
        a_buf[0, 4:5, :] = jnp.broadcast_to(loss, (1, D_MODEL))
        a_buf[0, 5:8, :] = jnp.zeros((3, D_MODEL), F32)
        a_buf[0, 8:24, :] = dmeta_ref[...]
        b_buf[0, 0:1, :] = rowsum(dsc_ref[...])
        for k in range(3):
            b_buf[0, 1 + k:2 + k, :] = rowsum(dcw_ref[8 * k:8 * k + 8, :])
        b_buf[0, 4:8, :] = jnp.zeros((4, D_CONV), F32)
        c_buf[0] = dpw_ref[...]
        for cp in self._copies(0):
            cp.start()

    def combine(self, st):
        for cp in self._copies(st):
            cp.wait()
        if st < 2:
            for buf, rcv in zip(self.bufs, self.rcvs):
                buf[st + 1] = buf[st] + rcv[st]
            for cp in self._copies(st + 1):
                cp.start()
        else:
            for out, buf, rcv in zip(self.outs, self.bufs, self.rcvs):
                out[...] = buf[st] + rcv[st]


def _row_block(rows):
    for cand in (512, 448, 384, 352, 320, 256, 128, 64, 32, 16):
        if rows % cand == 0:
            return cand
    return rows


def _add_pairs(grad, recv, place):
    n_sh, rows2, cols = grad.shape
    hr = rows2 // 2
    br = _row_block(hr)

    def body(place_ref, a_ref, b_ref, o_ref):
        o_ref[...] = (a_ref[0] + b_ref[...]).astype(BF16)

    return pl.pallas_call(
        body, name="grad_add_pairs",
        grid_spec=pltpu.PrefetchScalarGridSpec(
            num_scalar_prefetch=1, grid=(n_sh, hr // br),
            in_specs=[pl.BlockSpec((1, 1, br, cols), lambda j, i, p: (j, p[1], i, 0)),
                      pl.BlockSpec((1, br, cols), lambda j, i, p: (j, i, 0))],
            out_specs=pl.BlockSpec((1, br, cols), lambda j, i, p: (j, i, 0))),
        out_shape=jax.ShapeDtypeStruct((n_sh, hr, cols), BF16), compiler_params=_cparams(2),
    )(place, grad.reshape(n_sh, 2, hr, cols), recv)


def _add_chips(grads, recvs, rbufs, place, scattered=(), name="grad_add_chips"):
    n, n_sc = len(grads), len(scattered)
    n_sh, rows2, cols = grads[0].shape
    hr = rows2 // 2
    br = _row_block(hr)
    n_steps = hr // br

    def body(place_ref, *refs):
        a_refs, b_refs, r_refs = refs[:n], refs[n:2 * n], refs[2 * n:3 * n]
        o_refs = refs[3 * n + n_sc:4 * n + n_sc]
        if n_sc:
            scatter = _ScatterToChips(refs[3 * n:3 * n + n_sc], refs[4 * n + n_sc:4 * n + 2 * n_sc], *refs[-2:])

            @pl.when(pl.program_id(0) == 0)
            def _():
                scatter.start()

        for a_ref, b_ref, r_ref, o_ref in zip(a_refs, b_refs, r_refs, o_refs):
            own = a_ref[0, 0] + b_ref[0]
            o_ref[0] = ((own + r_ref[0].astype(F32)) + r_ref[1].astype(F32)) + r_ref[2].astype(F32)

        if n_sc:
            @pl.when(pl.program_id(0) == n_steps - 1)
            def _():
                scatter.finish()

    outs = pl.pallas_call(
        body, name=name,
        grid_spec=pltpu.PrefetchScalarGridSpec(
            num_scalar_prefetch=1, grid=(n_steps,),
            in_specs=[pl.BlockSpec((1, 1, br, cols), lambda i, p: (p[0], p[1], i, 0))] * n
            + [pl.BlockSpec((1, br, cols), lambda i, p: (p[0], i, 0))] * n
            + [pl.BlockSpec((3, br, cols), lambda i, p: (0, i, 0))] * n + [ANY] * n_sc,
            out_specs=[pl.BlockSpec((1, br, cols), lambda i, p: (p[1], i, 0))] * n + [ANY] * n_sc,
            scratch_shapes=_ScatterToChips.scratch(n_sc) if n_sc else []),
        out_shape=[jax.ShapeDtypeStruct((2, hr, cols), F32)] * n + _ScatterToChips.out_shape(list(scattered)),
        compiler_params=_cparams(1),
    )(place, *[g.reshape(n_sh, 2, hr, cols) for g in grads], *recvs, *rbufs, *scattered)
    return outs[:n], outs[n:]


def _adamw_math(w, g, m, v):
    m2 = ADAM_B1 * m + (1.0 - ADAM_B1) * g
    v2 = ADAM_B2 * v + (1.0 - ADAM_B2) * (g * g)
    m_hat = m2 / (1.0 - ADAM_B1 ** ADAM_STEP)
    v_hat = v2 / (1.0 - ADAM_B2 ** ADAM_STEP)
    delta = -ADAM_LR * (m_hat / (jnp.sqrt(v_hat) + ADAM_EPS) + ADAM_WD * w)
    return delta, m2, v2


def _adamw_big(w, g, m, v):
    rows, cols = w.shape
    br = _row_block(rows)

    def body(w_ref, g_ref, m_ref, v_ref, d_ref, m2_ref, v2_ref):
        d, m2, v2 = _adamw_math(w_ref[...], g_ref[...], m_ref[...], v_ref[...])
        d_ref[...] = d
        m2_ref[...] = m2
        v2_ref[...] = v2

    spec = pl.BlockSpec((br, cols), lambda i: (i, 0))
    return pl.pallas_call(
        body, name="adamw_big", grid=(rows // br,),
        out_shape=[jax.ShapeDtypeStruct((rows, cols), F32)] * 3,
        in_specs=[spec] * 4, out_specs=[spec] * 3, compiler_params=_cparams(1),
    )(w, g, m, v)


def _adamw_small(groups):
    n = len(groups)

    def body(*refs):
        ins, outs = refs[:4 * n], refs[4 * n:]
        for i in range(n):
            w, g, m, v = (r[...] for r in ins[4 * i:4 * i + 4])
            d, m2, v2 = _adamw_math(w, g, m, v)
            outs[3 * i][...] = d
            outs[3 * i + 1][...] = m2
            outs[3 * i + 2][...] = v2

    vm = pl.BlockSpec(memory_space=pltpu.VMEM)
    flat = [a for grp in groups for a in grp]
    out_shape = [jax.ShapeDtypeStruct(grp[0].shape, F32) for grp in groups for _ in range(3)]
    outs = pl.pallas_call(body, name="adamw_small", out_shape=out_shape,
                          in_specs=[vm] * (4 * n), out_specs=[vm] * (3 * n))(*flat)
    return [tuple(outs[3 * i:3 * i + 3]) for i in range(n)]


def _load_weights(pairs, sem):
    for src, dst in pairs:
        cp = pltpu.make_async_copy(src, dst, sem)
        cp.start()
        cp.wait()


def _meta_fwd(meta_full, g1, win_all):
    def body(meta_ref, g1_ref, win_ref, z_ref):
        xm = meta_ref[...]
        a = (xm * _rstd(xm) * g1_ref[...]).astype(BF16)
        for j in range(N_CHIPS):
            z_ref[:, j * IN_SHARD:(j + 1) * IN_SHARD] = _dot(a, win_ref[j])

    vm = pl.BlockSpec(memory_space=pltpu.VMEM)
    return pl.pallas_call(body, name="meta_fwd", out_shape=jax.ShapeDtypeStruct((N_META, D_IN_PROJ), F32),
                          in_specs=[vm] * 3, out_specs=vm)(meta_full, g1, win_all)


def _mixer_fwd(x3, zmeta, g1, g2, convw, poolw, pscale, win_all, wout, ffn_shards):
    n_seq, seq, _ = x3.shape
    tm = min(TM_MIX_FWD, seq)
    n_t = seq // tm
    n_steps = n_seq * n_t
    n_ag = len(ffn_shards)

    def body(x_ref, zm_ref, g1_ref, g2_ref, cw_ref, pw_ref, ps_ref, win_hbm, wout_hbm, *rest):
        ag = _AllGather(rest[:n_ag], rest[n_ag + 7:2 * n_ag + 7], *rest[-2:])
        z_ref, m_ref, h1_ref, a_ref, conv_ref, pooled_ref, yc_ref = rest[n_ag:n_ag + 7]
        win_v, wout_v, cvb, pb, sem = rest[2 * n_ag + 7:-2]
        s, t = pl.program_id(0), pl.program_id(1)
        step = s * n_t + t

        @pl.when(step == 0)
        def _():
            ag.start()
            _load_weights([(win_hbm, win_v), (wout_hbm, wout_v)], sem)

        for a in range(n_ag):
            @pl.when(step == min(((a + 1) * n_steps) // n_ag, n_steps - 1))
            def _():
                ag.forward(a)

        xt = x_ref[0]
        a = (xt * _rstd(xt) * g1_ref[...]).astype(BF16)
        a_ref[...] = a
        zb = _dot(a, win_v[0])
        zc = _dot(a, win_v[1])
        zv = _dot(a, win_v[2])
        zp = _dot(a, win_v[3])
        z_ref[0, :, 0:IN_SHARD] = zb
        z_ref[0, :, IN_SHARD:2 * IN_SHARD] = zc
        z_ref[0, :, 2 * IN_SHARD:3 * IN_SHARD] = zv

        @pl.when(t == 0)
        def _():
            cvb[0:HALO, :] = zm_ref[:, IN_SHARD:2 * IN_SHARD] * zm_ref[:, 2 * IN_SHARD:3 * IN_SHARD]
            pb[0:HALO, :] = zm_ref[:, 3 * IN_SHARD:4 * IN_SHARD]

        @pl.when(t > 0)
        def _():
            cvb[0:HALO, :] = cvb[tm:tm + HALO, :]
            pb[0:HALO, :] = pb[tm:tm + HALO, :]

        cv = zc * zv
        cvb[HALO:HALO + tm, :] = cv
        pb[HALO:HALO + tm, :] = zp
        cw = cw_ref[...]
        conv = cw[0:1] * cvb[HALO - 2:HALO - 2 + tm, :] + cw[1:2] * cvb[HALO - 1:HALO - 1 + tm, :] + cw[2:3] * cv
        conv_ref[...] = conv
        parts = [(zb * conv).astype(BF16)]
        for g in range(N_POOL_GROUPS):
            pooled = _pool_fwd(pb, g, tm).astype(BF16)
            pooled_ref[:, _gcols(g)] = pooled
            parts.append((_dot(pooled, pw_ref[g]) * ps_ref[:, _gcols(g)]).astype(BF16))
        ycat = jnp.concatenate(parts, axis=1)
        yc_ref[...] = ycat
        m = _dot(ycat, wout_v[...])
        m_ref[0] = m
        h1_ref[0] = xt + m * _rstd(m) * g2_ref[...]

        @pl.when(step == n_steps - 1)
        def _():
            ag.finish()

    n_rows = n_seq * seq
    row = lambda c: pl.BlockSpec((1, tm, c), lambda s, t: (s, t, 0))
    row2 = lambda c: pl.BlockSpec((tm, c), lambda s, t: (s * n_t + t, 0))
    outs = pl.pallas_call(
        body, name="mixer_fwd", grid=(n_seq, n_t),
        out_shape=[jax.ShapeDtypeStruct((n_seq, seq, D_Z), F32), jax.ShapeDtypeStruct((n_seq, seq, D_MODEL), F32),
                   jax.ShapeDtypeStruct((n_seq, seq, D_MODEL), F32), jax.ShapeDtypeStruct((n_rows, D_MODEL), BF16),
                   jax.ShapeDtypeStruct((n_rows, D_CONV), F32), jax.ShapeDtypeStruct((n_rows, D_POOL), BF16),
                   jax.ShapeDtypeStruct((n_rows, D_MODEL), BF16)] + _AllGather.out_shape(ffn_shards),
        in_specs=[row(D_MODEL), _full((N_META, D_IN_PROJ)), _full((1, D_MODEL)), _full((1, D_MODEL)),
                  _full((3, D_CONV)), _full((N_POOL_GROUPS, POOL_GROUP, POOL_GROUP)), _full((1, D_POOL)), ANY, ANY]
        + [ANY] * n_ag,
        out_specs=[row(D_Z), row(D_MODEL), row(D_MODEL), row2(D_MODEL), row2(D_CONV), row2(D_POOL), row2(D_MODEL)]
        + [ANY] * n_ag,
        scratch_shapes=[pltpu.VMEM((N_CHIPS, D_MODEL, IN_SHARD), BF16), pltpu.VMEM((D_MODEL, D_MODEL), BF16),
                        pltpu.VMEM((HALO + tm, D_CONV), F32), pltpu.VMEM((HALO + tm, D_POOL), F32),
                        pltpu.SemaphoreType.DMA] + _AllGather.scratch(n_ag),
        compiler_params=_cparams(2),
    )(x3, zmeta, g1, g2, convw, poolw, pscale, win_all, wout, *ffn_shards)
    return outs[:7], _fill_own_slot(outs[7:], ffn_shards)


def _ffn_chunks():
    out, r0 = [], 0
    while r0 < D_FF:
        out.append((r0, min(FF_CHUNK, D_FF - r0)))
        r0 += FF_CHUNK
    return out


def _ffn_fwd_bwd(h1, target, g3, g4, wg_t, wu_t, wd):
    n_rows = h1.shape[0]
    tm = min(TM_FFN, n_rows)
    chunks = _ffn_chunks()

    def body(h1_ref, t_ref, g3_ref, g4_ref, wg_hbm, wu_hbm, wd_hbm,
             dh1_ref, f_ref, dd_ref, ds_ref, du_ref, gg_ref, loss_ref, dg3_ref, dg4_ref,
             wg_v, wu_v, wd_v, s_sc, u_sc, sem):
        @pl.when(pl.program_id(0) == 0)
        def _():
            _load_weights([(wg_hbm, wg_v), (wu_hbm, wu_v), (wd_hbm, wd_v)], sem)
            loss_ref[...] = jnp.zeros_like(loss_ref)
            dg3_ref[...] = jnp.zeros_like(dg3_ref)
            dg4_ref[...] = jnp.zeros_like(dg4_ref)

        h1v = h1_ref[...]
        r3 = _rstd(h1v)
        hh = h1v * r3
        g3v, g4v = g3_ref[...], g4_ref[...]
        f = (hh * g3v).astype(BF16)
        f_ref[...] = f
        d = jnp.zeros((tm, D_MODEL), F32)
        for r0, sz in chunks:
            s = _dot_nt(f, wg_v[r0:r0 + sz, :])
            u = _dot_nt(f, wu_v[r0:r0 + sz, :])
            s_sc[:, r0:r0 + sz] = s
            u_sc[:, r0:r0 + sz] = u
            gc = (s * _sigmoid(s) * u).astype(BF16)
            gg_ref[:, r0:r0 + sz] = gc
            d = d + _dot(gc, wd_v[r0:r0 + sz, :])
        r4 = _rstd(d)
        dh = d * r4
        err = (h1v + dh * g4v) - t_ref[...]
        loss_ref[...] += _rows8(err * err)
        dy = err * (1.0 / D_MODEL)
        dg4_ref[...] += _rows8(dy * dh)
        ddb = _rms_bwd(dy, dh, r4, g4v).astype(BF16)
        dd_ref[...] = ddb
        df = jnp.zeros((tm, D_MODEL), F32)
        for r0, sz in chunks:
            dgg = _dot_nt(ddb, wd_v[r0:r0 + sz, :])
            s = s_sc[:, r0:r0 + sz]
            u = u_sc[:, r0:r0 + sz]
            sig = _sigmoid(s)
            dsc = (dgg * u * (sig * (1.0 + s * (1.0 - sig)))).astype(BF16)
            duc = (dgg * (s * sig)).astype(BF16)
            ds_ref[:, r0:r0 + sz] = dsc
            du_ref[:, r0:r0 + sz] = duc
            df = df + _dot(dsc, wg_v[r0:r0 + sz, :]) + _dot(duc, wu_v[r0:r0 + sz, :])
        dg3_ref[...] += _rows8(df * hh)
        dh1_ref[...] = dy + _rms_bwd(df, hh, r3, g3v)

    row = pl.BlockSpec((tm, D_MODEL), lambda i: (i, 0))
    ffrow = pl.BlockSpec((tm, D_FF), lambda i: (i, 0))
    acc = _full((8, D_MODEL))
    act_bf = jax.ShapeDtypeStruct((n_rows, D_MODEL), BF16)
    ff_bf = jax.ShapeDtypeStruct((n_rows, D_FF), BF16)
    acc_shape = jax.ShapeDtypeStruct((8, D_MODEL), F32)
    w_vmem = pltpu.VMEM((D_FF, D_MODEL), BF16)
    return pl.pallas_call(
        body, name="ffn_fwd_bwd", grid=(n_rows // tm,),
        out_shape=[jax.ShapeDtypeStruct((n_rows, D_MODEL), F32), act_bf, act_bf, ff_bf, ff_bf, ff_bf,
                   acc_shape, acc_shape, acc_shape],
        in_specs=[row, row, _full((1, D_MODEL)), _full((1, D_MODEL)), ANY, ANY, ANY],
        out_specs=[row, row, row, ffrow, ffrow, ffrow, acc, acc, acc],
        scratch_shapes=[w_vmem, w_vmem, w_vmem, pltpu.VMEM((tm, D_FF), F32), pltpu.VMEM((tm, D_FF), F32),
                        pltpu.SemaphoreType.DMA],
        compiler_params=_cparams(1),
    )(h1, target, g3, g4, wg_t, wu_t, wd)


def _ffn_weight_grads(name, acts, other, exchanged):
    n_rows = other.shape[0]
    n_a, n_ex = len(acts), len(exchanged)
    n_c = n_a
    tk = min(TK_DW // (2 // n_c), n_rows)
    n_k = n_rows // tk
    half = D_FF // n_c

    def body(other_ref, *rest):
        act_refs = rest[:n_a]
        out_refs = rest[n_a + n_ex:2 * n_a + n_ex]
        c, k = pl.program_id(0), pl.program_id(1)
        if n_ex:
            ex = _ExchangeHalves(rest[n_a:n_a + n_ex], rest[2 * n_a + n_ex:2 * n_a + 2 * n_ex], *rest[-2:])

            @pl.when((c == 0) & (k == 0))
            def _():
                ex.start()

        @pl.when(k == 0)
        def _():
            for o in out_refs:
                o[...] = jnp.zeros_like(o)

        ov = other_ref[...]
        for a, o in zip(act_refs, out_refs):
            o[...] += _dot_tn(a[...], ov)

        if n_ex:
            @pl.when((c == n_c - 1) & (k == n_k - 1))
            def _():
                ex.finish()

    row = pl.BlockSpec((tk, D_MODEL), lambda c, k: (k, 0))
    ffrow = pl.BlockSpec((tk, half), lambda c, k: (k, c))
    out = pl.BlockSpec((half, D_MODEL), lambda c, k: (c, 0))
    outs = pl.pallas_call(
        body, name=name, grid=(n_c, n_k),
        out_shape=[jax.ShapeDtypeStruct((D_FF, D_MODEL), F32)] * n_a + _ExchangeHalves.out_shape(exchanged),
        in_specs=[row] + [ffrow] * n_a + [ANY] * n_ex, out_specs=[out] * n_a + [ANY] * n_ex,
        scratch_shapes=_ExchangeHalves.scratch(n_ex) if n_ex else [],
        compiler_params=_cparams(2),
    )(other, *acts, *exchanged)
    return outs[:n_a], outs[n_a:]


def _mixer_bwd(dh1, m3, z3, conv2, pooled2, x3, zmeta, g1, g2, convw, poolw, pscale, win_all, wout, exchanged,
               scattered):
    n_seq, seq, _ = x3.shape
    tm = min(TM_MIX_BWD, seq)
    sub = min(SUB_MIX_BWD, tm)
    n_t = seq // tm
    n_ex, n_sc = len(exchanged), len(scattered)
    n_cm = n_ex + n_sc

    def body(dh1_ref, m_ref, z_ref, conv_ref, pooled_ref, x_ref, zm_ref, g1_ref, g2_ref, cw_ref, pw_ref, ps_ref,
             win_hbm, wout_hbm, *rest):
        outs0 = n_cm + 9
        ex = _ExchangeHalves(rest[:n_ex], rest[outs0:outs0 + n_ex], *rest[-4:-2])
        sc = _ScatterToChips(rest[n_ex:n_cm], rest[outs0 + n_ex:outs0 + n_cm], *rest[-2:])
        dx_ref, dz_ref, dm_ref, dg1_ref, dg2_ref, dsc_ref, dcw_ref, dpw_ref, dzm_ref = rest[n_cm:outs0]
        win_v, wout_v, dcb, dqb, mcb, mqb, sem = rest[outs0 + n_cm:-4]
        s, i = pl.program_id(0), pl.program_id(1)
        tr = n_t - 1 - i

        @pl.when((s == 0) & (i == 0))
        def _():
            sc.start()
            ex.start()
            _load_weights([(win_hbm, win_v), (wout_hbm, wout_v)], sem)
            for ref in (dg1_ref, dg2_ref, dsc_ref, dcw_ref, dpw_ref, dzm_ref):
                ref[...] = jnp.zeros_like(ref)

        @pl.when(i == 0)
        def _():
            dcb[tm:tm + HALO, :] = jnp.zeros((HALO, D_CONV), F32)
            dqb[tm:tm + HALO, :] = jnp.zeros((HALO, D_POOL), F32)

        @pl.when(i > 0)
        def _():
            dcb[tm:tm + HALO, :] = dcb[0:HALO, :]
            dqb[tm:tm + HALO, :] = dqb[0:HALO, :]

        g1v, g2v = g1_ref[...], g2_ref[...]
        cw = cw_ref[...]

        for r0 in range(tm - sub, -1, -sub):
            rows = slice(r0, r0 + sub)
            dh1v = dh1_ref[0, rows, :]
            mv = m_ref[0, rows, :]
            r2 = _rstd(mv)
            mh = mv * r2
            dg2_ref[...] += _rows8(dh1v * mh)
            dmb = _rms_bwd(dh1v, mh, r2, g2v).astype(BF16)
            dm_ref[rows, :] = dmb
            dyc = _dot_nt(dmb, wout_v[...])
            dyconv = dyc[:, 0:D_CONV]

            for g in range(N_POOL_GROUPS):
                pooled = pooled_ref[rows, _gcols(g)]
                mixed = _dot(pooled, pw_ref[g])
                scale = ps_ref[:, _gcols(g)]
                dyp = dyc[:, D_CONV + g * POOL_GROUP:D_CONV + (g + 1) * POOL_GROUP]
                dsc_ref[:, _gcols(g)] += _rows8(dyp * mixed)
                dmix = (dyp * scale).astype(BF16)
                dpw_ref[g] += _dot_tn(pooled, dmix)
                dqb[rows, _gcols(g)] = _dot_nt(dmix, pw_ref[g])

            zb = z_ref[0, rows, 0:IN_SHARD]
            zc = z_ref[0, rows, IN_SHARD:2 * IN_SHARD]
            zv = z_ref[0, rows, 2 * IN_SHARD:3 * IN_SHARD]
            dconv = dyconv * zb
            dcb[rows, :] = dconv
            d1 = dcb[r0 + 1:r0 + 1 + sub, :]
            d2 = dcb[r0 + 2:r0 + 2 + sub, :]
            dcv = cw[2:3] * dconv + cw[1:2] * d1 + cw[0:1] * d2
            cv = zc * zv
            dcw_ref[0:8, :] += _rows8(cv * d2)
            dcw_ref[8:16, :] += _rows8(cv * d1)
            dcw_ref[16:24, :] += _rows8(cv * dconv)
            dzs = [(dyconv * conv_ref[rows, :]).astype(BF16), (dcv * zv).astype(BF16), (dcv * zc).astype(BF16),
                   jnp.concatenate([_pool_bwd(dqb, g, r0, sub) for g in range(N_POOL_GROUPS)], axis=1).astype(BF16)]
            da = jnp.zeros((sub, D_MODEL), F32)
            for j in range(N_CHIPS):
                dz_ref[j, rows, :] = dzs[j]
                da = da + _dot_nt(dzs[j], win_v[j])
            xt = x_ref[0, rows, :]
            r1 = _rstd(xt)
            xh = xt * r1
            dg1_ref[...] += _rows8(da * xh)
            dx_ref[0, rows, :] = dh1v + _rms_bwd(da, xh, r1, g1v)

        @pl.when(tr == 0)
        def _():
            mcb[0:HALO, :] = jnp.zeros((HALO, D_CONV), F32)
            mqb[0:HALO, :] = jnp.zeros((HALO, D_POOL), F32)
            mcb[HALO:2 * HALO, :] = dcb[0:HALO, :]
            mqb[HALO:2 * HALO, :] = dqb[0:HALO, :]
            m1 = mcb[1:1 + HALO, :]
            m2 = mcb[2:2 + HALO, :]
            zc_m = zm_ref[:, IN_SHARD:2 * IN_SHARD]
            zv_m = zm_ref[:, 2 * IN_SHARD:3 * IN_SHARD]
            cv_m = zc_m * zv_m
            dcw_ref[0:8, :] += _rows8(cv_m * m2)
            dcw_ref[8:16, :] += _rows8(cv_m * m1)
            dcv_m = cw[1:2] * m1 + cw[0:1] * m2
            dzm_ref[:, IN_SHARD:2 * IN_SHARD] += dcv_m * zv_m
            dzm_ref[:, 2 * IN_SHARD:3 * IN_SHARD] += dcv_m * zc_m
            dzm_ref[:, 3 * IN_SHARD:4 * IN_SHARD] += jnp.concatenate(
                [_pool_bwd(mqb, g, 0, HALO) for g in range(N_POOL_GROUPS)], axis=1)

        @pl.when((s == n_seq - 1) & (i == n_t - 1))
        def _():
            ex.finish()
            sc.finish()

    row3 = lambda c: pl.BlockSpec((1, tm, c), lambda s, i: (s, n_t - 1 - i, 0))
    row2 = lambda c: pl.BlockSpec((tm, c), lambda s, i: (s * n_t + n_t - 1 - i, 0))
    n_rows = n_seq * seq
    outs = pl.pallas_call(
        body, name="mixer_bwd", grid=(n_seq, n_t),
        out_shape=[jax.ShapeDtypeStruct((n_seq, seq, D_MODEL), F32),
                   jax.ShapeDtypeStruct((N_CHIPS, n_rows, IN_SHARD), BF16), jax.ShapeDtypeStruct((n_rows, D_MODEL), BF16),
                   jax.ShapeDtypeStruct((8, D_MODEL), F32), jax.ShapeDtypeStruct((8, D_MODEL), F32),
                   jax.ShapeDtypeStruct((8, D_POOL), F32), jax.ShapeDtypeStruct((24, D_CONV), F32),
                   jax.ShapeDtypeStruct((N_POOL_GROUPS, POOL_GROUP, POOL_GROUP), F32),
                   jax.ShapeDtypeStruct((N_META, D_IN_PROJ), F32)]
        + _ExchangeHalves.out_shape(exchanged) + _ScatterToChips.out_shape(scattered),
        in_specs=[row3(D_MODEL), row3(D_MODEL), row3(D_Z), row2(D_CONV), row2(D_POOL), row3(D_MODEL),
                  _full((N_META, D_IN_PROJ)), _full((1, D_MODEL)), _full((1, D_MODEL)), _full((3, D_CONV)),
                  _full((N_POOL_GROUPS, POOL_GROUP, POOL_GROUP)), _full((1, D_POOL)), ANY, ANY] + [ANY] * n_cm,
        out_specs=[row3(D_MODEL), pl.BlockSpec((N_CHIPS, tm, IN_SHARD), lambda s, i: (0, s * n_t + n_t - 1 - i, 0)),
                   row2(D_MODEL),
                   _full((8, D_MODEL)), _full((8, D_MODEL)), _full((8, D_POOL)), _full((24, D_CONV)),
                   _full((N_POOL_GROUPS, POOL_GROUP, POOL_GROUP)), _full((N_META, D_IN_PROJ))] + [ANY] * n_cm,
        scratch_shapes=[pltpu.VMEM((N_CHIPS, D_MODEL, IN_SHARD), BF16), pltpu.VMEM((D_MODEL, D_MODEL), BF16),
                        pltpu.VMEM((tm + HALO, D_CONV), F32), pltpu.VMEM((tm + HALO, D_POOL), F32),
                        pltpu.VMEM((2 * HALO, D_CONV), F32), pltpu.VMEM((2 * HALO, D_POOL), F32),
                        pltpu.SemaphoreType.DMA] + _ExchangeHalves.scratch(n_ex) + _ScatterToChips.scratch(n_sc),
        compiler_params=_cparams(2),
    )(dh1, m3, z3, conv2, pooled2, x3, zmeta, g1, g2, convw, poolw, pscale, win_all, wout, *exchanged, *scattered)
    return outs[:9], outs[9:9 + n_ex], outs[9 + n_ex:]


def _meta_bwd(dzm, meta_full, g1, win_all):
    def body(dzm_ref, meta_ref, g1_ref, win_ref, dmeta_ref, dg1_ref, a_ref, dzb_ref):
        xm = meta_ref[...]
        r = _rstd(xm)
        xh = xm * r
        g1v = g1_ref[...]
        a_ref[...] = (xh * g1v).astype(BF16)
        da = jnp.zeros((N_META, D_MODEL), F32)
        for j in range(N_CHIPS):
            dzj = dzm_ref[:, j * IN_SHARD:(j + 1) * IN_SHARD].astype(BF16)
            dzb_ref[j] = dzj
            da = da + _dot_nt(dzj, win_ref[j])
        dg1_ref[...] = _rows8(da * xh)
        dmeta_ref[...] = _rms_bwd(da, xh, r, g1v)

    vm = pl.BlockSpec(memory_space=pltpu.VMEM)
    return pl.pallas_call(
        body, name="meta_bwd",
        out_shape=[jax.ShapeDtypeStruct((N_META, D_MODEL), F32), jax.ShapeDtypeStruct((8, D_MODEL), F32),
                   jax.ShapeDtypeStruct((N_META, D_MODEL), BF16), jax.ShapeDtypeStruct((N_CHIPS, N_META, IN_SHARD), BF16)],
        in_specs=[vm] * 4, out_specs=[vm] * 4,
    )(dzm, meta_full, g1, win_all)


def _mixer_weight_grads(a, dz, ycat, dm, a_meta, dz_meta, ffn_sums, small):
    n_rows = a.shape[0]
    tk = min(TK_DW, n_rows)
    n_k = n_rows // tk
    n_sc, n_sm = len(ffn_sums), _AllReduceSmall.N_IN

    def body(a_ref, dz_ref, yc_ref, dm_ref, am_ref, dzm_ref, *rest):
        ins, outs, scratch = rest[:n_sc + n_sm], rest[n_sc + n_sm:2 * n_sc + n_sm + 5], rest[2 * n_sc + n_sm + 5:]
        dwin_ref, dwout_ref = outs[:2]
        scatter = _ScatterToChips(ins[:n_sc], outs[2:2 + n_sc], *scratch[:2])
        reduce_small = _AllReduceSmall(ins[n_sc:], outs[2 + n_sc:], scratch[2:])
        k = pl.program_id(0)

        @pl.when(k == 0)
        def _():
            scatter.start()
            reduce_small.pack_and_send()
            am_t = am_ref[...].T
            for j in range(N_CHIPS):
                dwin_ref[j] = _dot(am_t, dzm_ref[j])
            dwout_ref[...] = jnp.zeros_like(dwout_ref)

        for st in range(2):
            @pl.when(k == ((st + 1) * n_k) // 3)
            def _():
                reduce_small.combine(st)

        a_t = a_ref[...].T
        for j in range(N_CHIPS):
            dwin_ref[j] += _dot(a_t, dz_ref[j])
        dwout_ref[...] += _dot_tn(yc_ref[...], dm_ref[...])

        @pl.when(k == n_k - 1)
        def _():
            reduce_small.combine(2)
            scatter.finish()

    row = pl.BlockSpec((tk, D_MODEL), lambda k: (k, 0))
    outs = pl.pallas_call(
        body, name="mixer_weight_grads", grid=(n_k,),
        out_shape=[jax.ShapeDtypeStruct((N_CHIPS, D_MODEL, IN_SHARD), F32),
                   jax.ShapeDtypeStruct((D_MODEL, D_MODEL), F32)] + _ScatterToChips.out_shape(ffn_sums)
        + _AllReduceSmall.out_shape(),
        in_specs=[row, pl.BlockSpec((N_CHIPS, tk, IN_SHARD), lambda k: (0, k, 0)), row, row,
                  _full((N_META, D_MODEL)), _full((N_CHIPS, N_META, IN_SHARD))] + [ANY] * n_sc
        + [_full(s.shape) for s in small],
        out_specs=[_full((N_CHIPS, D_MODEL, IN_SHARD)), _full((D_MODEL, D_MODEL))] + [ANY] * n_sc
        + [_full(s) for s in _AllReduceSmall.SHAPES],
        scratch_shapes=_ScatterToChips.scratch(n_sc) + _AllReduceSmall.scratch(),
        compiler_params=_cparams(1),
    )(a, dz, ycat, dm, a_meta, dz_meta, *ffn_sums, *small)
    return ([outs[0], outs[1].reshape(N_CHIPS, OUT_SHARD, D_MODEL)], outs[2:2 + n_sc], outs[2 + n_sc:])


def kernel(x, meta_tokens, norm_mix_pre, w_in, conv_w, pool_w, pool_scale, w_out, norm_mix_post, norm_ffn_pre, w_gate, w_up, w_down, norm_ffn_post, loss_target, m_meta_tokens, m_norm_mix_pre, m_w_in, m_conv_w, m_pool_w, m_pool_scale, m_w_out, m_norm_mix_post, m_norm_ffn_pre, m_w_gate, m_w_up, m_w_down, m_norm_ffn_post, v_meta_tokens, v_norm_mix_pre, v_w_in, v_conv_w, v_pool_w, v_pool_scale, v_w_out, v_norm_mix_post, v_norm_ffn_pre, v_w_gate, v_w_up, v_w_down, v_norm_ffn_post):
    n_seq, seq, _ = x.shape
    n_rows = n_seq * seq
    chip = 2 * lax.axis_index("x") + lax.axis_index("y")
    meta_cols = D_MODEL // N_CHIPS
    conv_cols = D_CONV // N_CHIPS

    small = jnp.zeros((2 * HALO, meta_cols), F32)
    small = small.at[0:N_META, :].set(meta_tokens).at[N_META:N_META + 3, 0:conv_cols].set(conv_w[0])
    win_all, wout_all, small_all = _all_gather_shards([w_in[0].astype(BF16), w_out[0].astype(BF16), small])
    meta_full = small_all[:, 0:N_META, :].transpose(1, 0, 2).reshape(N_META, D_MODEL)
    conv_full = small_all[:, N_META:N_META + 3, 0:conv_cols].transpose(1, 0, 2).reshape(3, D_CONV)
    wout_full = wout_all.reshape(D_MODEL, D_MODEL)
    poolw_bf = pool_w[0].astype(BF16)
    pscale = pool_scale
    g1, g2, g3, g4 = norm_mix_pre, norm_mix_post, norm_ffn_pre, norm_ffn_post
    place = jnp.stack([chip, lax.axis_index("c")]).astype(jnp.int32)

    zmeta = _meta_fwd(meta_full, g1, win_all)
    (z3, m3, h1, a_bf, conv2, pooled2, yc_bf), ffn_w = _mixer_fwd(
        x, zmeta, g1, g2, conv_full, poolw_bf, pscale, win_all, wout_full,
        [w_gate[0].T.astype(BF16), w_up[0].T.astype(BF16), w_down[0].astype(BF16)])
    wg_t, wu_t, wd_full = [w.reshape(D_FF, D_MODEL) for w in ffn_w]
    dh1, f_bf, dd_bf, ds_bf, du_bf, gg_bf, lossp, dg3p, dg4p = _ffn_fwd_bwd(
        h1.reshape(n_rows, D_MODEL), loss_target.reshape(n_rows, D_MODEL), g3, g4, wg_t, wu_t, wd_full)
    as_shards = lambda g: g.reshape(N_CHIPS, FF_SHARD, D_MODEL)
    (dwd,), _ = _ffn_weight_grads("ffn_weight_grads_down", [gg_bf], dd_bf, [])
    dwd = as_shards(dwd)
    (dwg_t,), (dwd_recv,) = _ffn_weight_grads("ffn_weight_grads_gate", [ds_bf], f_bf, [dwd])
    dwg_t = as_shards(dwg_t)
    (dwu_t,), (dwg_recv,) = _ffn_weight_grads("ffn_weight_grads_up", [du_bf], f_bf, [dwg_t])
    dwu_t = as_shards(dwu_t)
    ((grad_x, dz_bf, dm_bf, dg1p, dg2p, dscp, dcwp, dpw, dzm), (dwu_recv,), (dwd_rbuf, dwg_rbuf)) = _mixer_bwd(
        dh1.reshape(n_seq, seq, D_MODEL), m3, z3, conv2, pooled2, x, zmeta, g1, g2, conv_full, poolw_bf, pscale,
        win_all, wout_full, [dwu_t], [_add_pairs(dwd, dwd_recv, place), _add_pairs(dwg_t, dwg_recv, place)])
    dmeta, dg1m, a_meta, dz_meta = _meta_bwd(dzm, meta_full, g1, win_all)
    mix_grads, (dwu_rbuf,), (a_red, b_red, c_red) = _mixer_weight_grads(
        a_bf, dz_bf, yc_bf, dm_bf, a_meta, dz_meta, [_add_pairs(dwu_t, dwu_recv, place)],
        [dg1p, dg1m, dg2p, dg3p, dg4p, lossp, dmeta, dscp, dcwp, dpw.reshape(SMALL_C_ROWS, POOL_GROUP)])

    mix_recvs = _exchange_halves(mix_grads)
    ffn_red, mix_rbufs = _add_chips([dwg_t, dwu_t, dwd], [dwg_recv, dwu_recv, dwd_recv],
                                    [dwg_rbuf, dwu_rbuf, dwd_rbuf], place,
                                    [_add_pairs(g, r, place) for g, r in zip(mix_grads, mix_recvs)],
                                    name="grad_add_chips_ffn")
    mix_red = [_add_chips([g], [r], [rb], place)[0][0] for g, r, rb in zip(mix_grads, mix_recvs, mix_rbufs)]
    reduced = _gather_halves(mix_red + list(ffn_red))
    g_win, g_wout, g_wg_t, g_wu_t, g_wd = [r.reshape(2 * r.shape[1], r.shape[2]) for r in reduced]

    loss = a_red[4, 0]
    g_g1, g_g2, g_g3, g_g4 = a_red[0:1], a_red[1:2], a_red[2:3], a_red[3:4]
    g_meta = lax.dynamic_slice(a_red, (8, chip * meta_cols), (N_META, meta_cols))
    g_pscale = b_red[0:1]
    g_conv = lax.dynamic_slice(b_red, (1, chip * conv_cols), (3, conv_cols))
    g_poolw = c_red

    big = [(w_in[0], g_win, m_w_in[0], v_w_in[0]), (w_out[0], g_wout, m_w_out[0], v_w_out[0]),
           (w_gate[0].T, g_wg_t, m_w_gate[0].T, v_w_gate[0].T), (w_up[0].T, g_wu_t, m_w_up[0].T, v_w_up[0].T),
           (w_down[0], g_wd, m_w_down[0], v_w_down[0])]
    big_out = [_adamw_big(w, g, m, v) for (w, g, m, v) in big]
    big_out[2] = [o.T for o in big_out[2]]
    big_out[3] = [o.T for o in big_out[3]]
    g_wg, g_wu = g_wg_t.T, g_wu_t.T
    small_groups = [
        (meta_tokens, g_meta, m_meta_tokens, v_meta_tokens),
        (g1, g_g1, m_norm_mix_pre, v_norm_mix_pre),
        (conv_w[0], g_conv, m_conv_w[0], v_conv_w[0]),
        (pool_w.reshape(SMALL_C_ROWS, POOL_GROUP), g_poolw, m_pool_w.reshape(SMALL_C_ROWS, POOL_GROUP),
         v_pool_w.reshape(SMALL_C_ROWS, POOL_GROUP)),
        (pool_scale, g_pscale, m_pool_scale, v_pool_scale),
        (g2, g_g2, m_norm_mix_post, v_norm_mix_post),
        (g3, g_g3, m_norm_ffn_pre, v_norm_ffn_pre),
        (g4, g_g4, m_norm_ffn_post, v_norm_ffn_post),
    ]
    small_out = _adamw_small(small_groups)

    grads_out = [g_meta, g_g1, g_win[None], g_conv[None], g_poolw.reshape(pool_w.shape), g_pscale, g_wout[None],
                 g_g2, g_g3, g_wg[None], g_wu[None], g_wd[None], g_g4]
    s_meta, s_g1, s_conv, s_poolw, s_pscale, s_g2, s_g3, s_g4 = small_out
    b_win, b_wout, b_wg, b_wu, b_wd = big_out

    def leaf(k):
        return [s_meta[k], s_g1[k], b_win[k][None], s_conv[k][None], s_poolw[k].reshape(pool_w.shape), s_pscale[k],
                b_wout[k][None], s_g2[k], s_g3[k], b_wg[k][None], b_wu[k][None], b_wd[k][None], s_g4[k]]

    return (loss, grad_x, *grads_out, *leaf(0), *leaf(1), *leaf(2))
```

```python
import jax
import jax.numpy as jnp
from jax import lax
from jax.experimental import pallas as pl
from jax.experimental.pallas import tpu as pltpu

F32 = jnp.float32
BF16 = jnp.bfloat16
MESH = pl.DeviceIdType.MESH

D_MODEL = 1024
D_CONV = 512
D_POOL = 512
POOL_GROUP = 128
N_POOL_GROUPS = 4
D_IN_PROJ = 2048
D_FF = 2816
N_CHIPS = 4
FF_SHARD = D_FF // N_CHIPS
IN_SHARD = D_IN_PROJ // N_CHIPS
OUT_SHARD = D_MODEL // N_CHIPS
D_Z = 3 * IN_SHARD
N_META = 16
HALO = 16
RMS_EPS = 1e-6

ADAM_LR = 0.001
ADAM_B1 = 0.9
ADAM_B2 = 0.999
ADAM_EPS = 1e-08
ADAM_WD = 0.01
ADAM_STEP = 10

TM_MIX_FWD = 512
TM_MIX_BWD = 512
SUB_MIX_BWD = 512
TM_FFN = 256
TK_DW = 1024
FF_CHUNK = 1024
VMEM_LIMIT = 56 * 1024 * 1024


def _cparams(n_grid):
    return pltpu.CompilerParams(dimension_semantics=("arbitrary",) * n_grid, vmem_limit_bytes=VMEM_LIMIT)


def _dot(a, b):
    return jnp.dot(a, b, preferred_element_type=F32)


def _dot_nt(a, b):
    return lax.dot_general(a, b, (((1,), (1,)), ((), ())), preferred_element_type=F32)


def _dot_tn(a, b):
    return lax.dot_general(a, b, (((0,), (0,)), ((), ())), preferred_element_type=F32)


def _rows8(v):
    r, c = v.shape
    return v.reshape(r // 8, 8, c).sum(axis=0)


def _rstd(v):
    return lax.rsqrt(jnp.mean(v * v, axis=-1, keepdims=True) + RMS_EPS)


def _rms_bwd(dy, xhat, rstd, gain):
    dyg = dy * gain
    return rstd * (dyg - xhat * jnp.mean(dyg * xhat, axis=-1, keepdims=True))


def _sigmoid(v):
    return 1.0 / (1.0 + jnp.exp(-v))


def _gcols(g):
    return slice(g * POOL_GROUP, (g + 1) * POOL_GROUP)


def _window_sum(e, g, ahead):
    n = e.shape[0]
    w = e
    for level in range(g + 1):
        shift = 1 << level
        w = w + pltpu.roll(w, (n - shift) if ahead else shift, 0)
    return w


def _pool_fwd(pb, g, n):
    e = pb[0:HALO + n, _gcols(g)]
    return _window_sum(e, g, False)[HALO:, :] * (1.0 / (2 << g)) - e[HALO:, :]


def _pool_bwd(qb, g, r0, n):
    e = qb[r0:r0 + n + HALO, _gcols(g)]
    return _window_sum(e, g, True)[0:n, :] * (1.0 / (2 << g)) - e[0:n, :]


def _full(shape):
    nd = len(shape)
    return pl.BlockSpec(shape, lambda *_: (0,) * nd)


ANY = pl.BlockSpec(memory_space=pl.ANY)


def _mesh_pos():
    x, y, c = lax.axis_index("x"), lax.axis_index("y"), lax.axis_index("c")
    chips = [(1 - x, y), (x, 1 - y), (1 - x, 1 - y)]
    return x, y, c, chips


def _half(ref, h):
    hr = ref.shape[0] // 2
    return ref.at[pl.ds(h * hr, hr), :]


class _AllGather:
    PER_ARRAY = 9

    def __init__(self, ins, outs, send_sems, recv_sems):
        self.ins, self.outs, self.send_sems, self.recv_sems = ins, outs, send_sems, recv_sems
        self.n = len(ins)

    @classmethod
    def scratch(cls, n):
        return [pltpu.SemaphoreType.DMA((cls.PER_ARRAY * n,)), pltpu.SemaphoreType.DMA((cls.PER_ARRAY * n,))]

    @staticmethod
    def out_shape(shards):
        return [jax.ShapeDtypeStruct((N_CHIPS,) + s.shape, s.dtype) for s in shards]

    def _copy(self, a, k, src, dst, to):
        i = self.PER_ARRAY * a + k
        return pltpu.make_async_remote_copy(src_ref=src, dst_ref=dst, send_sem=self.send_sems.at[i],
                                            recv_sem=self.recv_sems.at[i], device_id=to, device_id_type=MESH)

    def _piece(self, a, chip, piece, h=None):
        h = lax.axis_index("c") if h is None else h
        rows = self.ins[a].shape[0] // 4
        return self.outs[a].at[chip].at[pl.ds((2 * h + piece) * rows, rows), :]

    def _own(self, a, k):
        x, y, c, chips = _mesh_pos()
        piece = (1, 0, 0, 1)[k]
        rows = self.ins[a].shape[0] // 4
        src = self.ins[a].at[pl.ds((2 * c + piece) * rows, rows), :]
        return self._copy(a, k, src, self._piece(a, 2 * x + y, piece), (*chips[k // 2], c))

    def _relay(self, a, k):
        x, y, c, chips = _mesh_pos()
        source, to, piece = (chips[1], chips[0], 0) if k == 4 else (chips[0], chips[1], 1)
        rows = self._piece(a, 2 * source[0] + source[1], piece)
        return self._copy(a, k, rows, rows, (*to, c))

    def _sibling(self, a, k, h):
        x, y, c, chips = _mesh_pos()
        chip = chips[k - 6]
        slot = _half(self.outs[a].at[2 * chip[0] + chip[1]], h)
        return self._copy(a, k, slot, slot, (x, y, 1 - c))

    def start(self):
        for a in range(self.n):
            for k in range(4):
                self._own(a, k).start()

    def relay(self, a):
        self._own(a, 2).wait_recv()
        self._relay(a, 4).start()
        self._own(a, 0).wait_recv()
        self._relay(a, 5).start()

    def forward(self, a):
        c = lax.axis_index("c")
        self._own(a, 1).wait_recv()
        self._sibling(a, 6, c).start()
        self._own(a, 3).wait_recv()
        self._sibling(a, 7, c).start()
        self._relay(a, 4).wait_recv()
        self._relay(a, 5).wait_recv()
        self._sibling(a, 8, c).start()

    def finish(self):
        c = lax.axis_index("c")
        for a in range(self.n):
            for k in range(6, 9):
                self._sibling(a, k, 1 - c).wait_recv()
        for a in range(self.n):
            for k in range(4):
                self._own(a, k).wait_send()
            for k in range(4, 6):
                self._relay(a, k).wait_send()
            for k in range(6, 9):
                self._sibling(a, k, c).wait_send()


def _fill_own_slot(gathered, shards):
    chip = 2 * lax.axis_index("x") + lax.axis_index("y")
    return [lax.dynamic_update_slice(o, s[None], (chip, 0, 0)) for o, s in zip(gathered, shards)]


def _all_gather_shards(shards):
    n = len(shards)

    def body(*refs):
        ag = _AllGather(refs[:n], refs[n:2 * n], *refs[2 * n:])
        ag.start()
        for a in range(n):
            ag.relay(a)
        for a in range(n):
            ag.forward(a)
        ag.finish()

    outs = pl.pallas_call(
        body, name="all_gather_weights", out_shape=_AllGather.out_shape(shards),
        in_specs=[ANY] * n, out_specs=[ANY] * n, scratch_shapes=_AllGather.scratch(n),
    )(*shards)
    return _fill_own_slot(outs, shards)


class _ExchangeHalves:
    def __init__(self, ins, recvs, send_sems, recv_sems):
        self.ins, self.recvs, self.send_sems, self.recv_sems = ins, recvs, send_sems, recv_sems

    @staticmethod
    def scratch(n):
        return [pltpu.SemaphoreType.DMA((n,)), pltpu.SemaphoreType.DMA((n,))]

    @staticmethod
    def out_shape(grads):
        return [jax.ShapeDtypeStruct((g.shape[0], g.shape[1] // 2, g.shape[2]), g.dtype) for g in grads]

    def _copies(self):
        x, y, c, _ = _mesh_pos()
        out = []
        for a, (src, dst) in enumerate(zip(self.ins, self.recvs)):
            hr = src.shape[1] // 2
            out.append(pltpu.make_async_remote_copy(
                src_ref=src.at[:, pl.ds((1 - c) * hr, hr), :], dst_ref=dst, send_sem=self.send_sems.at[a],
                recv_sem=self.recv_sems.at[a], device_id=(x, y, 1 - c), device_id_type=MESH))
        return out

    def start(self):
        for cp in self._copies():
            cp.start()

    def finish(self):
        for cp in self._copies():
            cp.wait()


def _exchange_halves(grads):
    n = len(grads)

    def body(*refs):
        ex = _ExchangeHalves(refs[:n], refs[n:2 * n], *refs[2 * n:])
        ex.start()
        ex.finish()

    return pl.pallas_call(
        body, name="grad_exchange_halves", out_shape=_ExchangeHalves.out_shape(grads),
        in_specs=[ANY] * n, out_specs=[ANY] * n, scratch_shapes=_ExchangeHalves.scratch(n),
    )(*grads)


class _ScatterToChips:
    def __init__(self, ins, rbufs, send_sems, recv_sems):
        self.ins, self.rbufs, self.send_sems, self.recv_sems = ins, rbufs, send_sems, recv_sems

    @staticmethod
    def scratch(n):
        return [pltpu.SemaphoreType.DMA((3 * n,)), pltpu.SemaphoreType.DMA((3 * n,))]

    @staticmethod
    def out_shape(sums):
        return [jax.ShapeDtypeStruct((3,) + s.shape[1:], BF16) for s in sums]

    def _copies(self):
        x, y, c, chips = _mesh_pos()
        out = []
        for a, (src, dst) in enumerate(zip(self.ins, self.rbufs)):
            for k, chip in enumerate(chips):
                out.append(pltpu.make_async_remote_copy(
                    src_ref=src.at[2 * chip[0] + chip[1]], dst_ref=dst.at[k], send_sem=self.send_sems.at[3 * a + k],
                    recv_sem=self.recv_sems.at[3 * a + k], device_id=(*chip, c), device_id_type=MESH))
        return out

    def start(self):
        for cp in self._copies():
            cp.start()

    def finish(self):
        for cp in self._copies():
            cp.wait()


def _gather_halves(halves):
    n = len(halves)

    def body(*refs):
        ins, outs = refs[:n], refs[n:2 * n]
        send_sems, recv_sems = refs[2 * n:]
        x, y, c, _ = _mesh_pos()
        sib = (x, y, 1 - c)
        remote = [pltpu.make_async_remote_copy(src_ref=ins[a].at[c], dst_ref=outs[a].at[c],
                                               send_sem=send_sems.at[a], recv_sem=recv_sems.at[a],
                                               device_id=sib, device_id_type=MESH) for a in range(n)]
        for cp in remote:
            cp.start()
        for a in range(n):
            pltpu.make_async_remote_copy(src_ref=ins[a].at[1 - c], dst_ref=outs[a].at[1 - c], send_sem=send_sems.at[a],
                                         recv_sem=recv_sems.at[a], device_id=sib, device_id_type=MESH).wait_recv()
        for cp in remote:
            cp.wait_send()

    return pl.pallas_call(
        body, name="grad_gather_halves",
        out_shape=[jax.ShapeDtypeStruct(h.shape, F32) for h in halves],
        in_specs=[ANY] * n, out_specs=[ANY] * n, input_output_aliases={a: a for a in range(n)},
        scratch_shapes=[pltpu.SemaphoreType.DMA((n,)), pltpu.SemaphoreType.DMA((n,))],
    )(*halves)


SMALL_A_ROWS = 24
SMALL_B_ROWS = 8
SMALL_C_ROWS = N_POOL_GROUPS * POOL_GROUP


class _AllReduceSmall:
    N_IN = 10
    SHAPES = [(SMALL_A_ROWS, D_MODEL), (SMALL_B_ROWS, D_CONV), (SMALL_C_ROWS, POOL_GROUP)]

    def __init__(self, ins, outs, scratch):
        self.ins, self.outs = ins, outs
        self.bufs, self.rcvs, self.send_sems, self.recv_sems = scratch[:3], scratch[3:6], scratch[6], scratch[7]

    @classmethod
    def scratch(cls):
        return ([pltpu.VMEM((3,) + s, F32) for s in cls.SHAPES] + [pltpu.VMEM((3,) + s, F32) for s in cls.SHAPES]
                + [pltpu.SemaphoreType.DMA((9,)), pltpu.SemaphoreType.DMA((9,))])

    @classmethod
    def out_shape(cls):
        return [jax.ShapeDtypeStruct(s, F32) for s in cls.SHAPES]

    def _copies(self, st):
        x, y, c, _ = _mesh_pos()
        peer = [(x, y, 1 - c), (1 - x, y, c), (x, 1 - y, c)][st]
        return [pltpu.make_async_remote_copy(
            src_ref=buf.at[st], dst_ref=rcv.at[st], send_sem=self.send_sems.at[3 * st + i],
            recv_sem=self.recv_sems.at[3 * st + i], device_id=peer, device_id_type=MESH)
            for i, (buf, rcv) in enumerate(zip(self.bufs, self.rcvs))]

    def pack_and_send(self):
        dg1_ref, dg1m_ref, dg2_ref, dg3_ref, dg4_ref, loss_ref, dmeta_ref, dsc_ref, dcw_ref, dpw_ref = self.ins
        a_buf, b_buf, c_buf = self.bufs

        def rowsum(v):
            return jnp.sum(v, axis=0, keepdims=True)

        a_buf[0, 0:1, :] = rowsum(dg1_ref[...] + dg1m_ref[...])
        a_buf[0, 1:2, :] = rowsum(dg2_ref[...])
        a_buf[0, 2:3, :] = rowsum(dg3_ref[...])
        a_buf[0, 3:4, :] = rowsum(dg4_ref[...])
        loss = jnp.sum(rowsum(loss_ref[...]), axis=1, keepdims=True) * (0.5 / D_MODEL)
        a_buf[0, 4:5, :] = jnp.broadcast_to(loss, (1, D_MODEL))
        a_buf[0, 5:8, :] = jnp.zeros((3, D_MODEL), F32)
        a_buf[0, 8:24, :] = dmeta_ref[...]
        b_buf[0, 0:1, :] = rowsum(dsc_ref[...])
        for k in range(3):
            b_buf[0, 1 + k:2 + k, :] = rowsum(dcw_ref[8 * k:8 * k + 8, :])
        b_buf[0, 4:8, :] = jnp.zeros((4, D_CONV), F32)
        c_buf[0] = dpw_ref[...]
        for cp in self._copies(0):
            cp.start()

    def combine(self, st):
        for cp in self._copies(st):
            cp.wait()
        if st < 2:
            for buf, rcv in zip(self.bufs, self.rcvs):
                buf[st + 1] = buf[st] + rcv[st]
            for cp in self._copies(st + 1):
                cp.start()
        else:
            for out, buf, rcv in zip(self.outs, self.bufs, self.rcvs):
                out[...] = buf[st] + rcv[st]


def _row_block(rows):
    for cand in (512, 448, 384, 352, 320, 256, 128, 64, 32, 16):
        if rows % cand == 0:
            return cand
    return rows


def _add_pairs(grad, recv, place):
    n_sh, rows2, cols = grad.shape
    hr = rows2 // 2
    br = _row_block(hr)

    def body(place_ref, a_ref, b_ref, o_ref):
        o_ref[...] = (a_ref[0] + b_ref[...]).astype(BF16)

    return pl.pallas_call(
        body, name="grad_add_pairs",
        grid_spec=pltpu.PrefetchScalarGridSpec(
            num_scalar_prefetch=1, grid=(n_sh, hr // br),
            in_specs=[pl.BlockSpec((1, 1, br, cols), lambda j, i, p: (j, p[1], i, 0)),
                      pl.BlockSpec((1, br, cols), lambda j, i, p: (j, i, 0))],
            out_specs=pl.BlockSpec((1, br, cols), lambda j, i, p: (j, i, 0))),
        out_shape=jax.ShapeDtypeStruct((n_sh, hr, cols), BF16), compiler_params=_cparams(2),
    )(place, grad.reshape(n_sh, 2, hr, cols), recv)


def _add_chips(grads, recvs, rbufs, place, scattered=(), name="grad_add_chips"):
    n, n_sc = len(grads), len(scattered)
    n_sh, rows2, cols = grads[0].shape
    hr = rows2 // 2
    br = _row_block(hr)
    n_steps = hr // br

    def body(place_ref, *refs):
        a_refs, b_refs, r_refs = refs[:n], refs[n:2 * n], refs[2 * n:3 * n]
        o_refs = refs[3 * n + n_sc:4 * n + n_sc]
        if n_sc:
            scatter = _ScatterToChips(refs[3 * n:3 * n + n_sc], refs[4 * n + n_sc:4 * n + 2 * n_sc], *refs[-2:])

            @pl.when(pl.program_id(0) == 0)
            def _():
                scatter.start()

        for a_ref, b_ref, r_ref, o_ref in zip(a_refs, b_refs, r_refs, o_refs):
            own = a_ref[0, 0] + b_ref[0]
            o_ref[0] = ((own + r_ref[0].astype(F32)) + r_ref[1].astype(F32)) + r_ref[2].astype(F32)

        if n_sc:
            @pl.when(pl.program_id(0) == n_steps - 1)
            def _():
                scatter.finish()

    outs = pl.pallas_call(
        body, name=name,
        grid_spec=pltpu.PrefetchScalarGridSpec(
            num_scalar_prefetch=1, grid=(n_steps,),
            in_specs=[pl.BlockSpec((1, 1, br, cols), lambda i, p: (p[0], p[1], i, 0))] * n
            + [pl.BlockSpec((1, br, cols), lambda i, p: (p[0], i, 0))] * n
            + [pl.BlockSpec((3, br, cols), lambda i, p: (0, i, 0))] * n + [ANY] * n_sc,
            out_specs=[pl.BlockSpec((1, br, cols), lambda i, p: (p[1], i, 0))] * n + [ANY] * n_sc,
            scratch_shapes=_ScatterToChips.scratch(n_sc) if n_sc else []),
        out_shape=[jax.ShapeDtypeStruct((2, hr, cols), F32)] * n + _ScatterToChips.out_shape(list(scattered)),
        compiler_params=_cparams(1),
    )(place, *[g.reshape(n_sh, 2, hr, cols) for g in grads], *recvs, *rbufs, *scattered)
    return outs[:n], outs[n:]


def _adamw_math(w, g, m, v):
    m2 = ADAM_B1 * m + (1.0 - ADAM_B1) * g
    v2 = ADAM_B2 * v + (1.0 - ADAM_B2) * (g * g)
    m_hat = m2 / (1.0 - ADAM_B1 ** ADAM_STEP)
    v_hat = v2 / (1.0 - ADAM_B2 ** ADAM_STEP)
    delta = -ADAM_LR * (m_hat / (jnp.sqrt(v_hat) + ADAM_EPS) + ADAM_WD * w)
    return delta, m2, v2


def _adamw_big(w, g, m, v):
    rows, cols = w.shape
    br = _row_block(rows)

    def body(w_ref, g_ref, m_ref, v_ref, d_ref, m2_ref, v2_ref):
        d, m2, v2 = _adamw_math(w_ref[...], g_ref[...], m_ref[...], v_ref[...])
        d_ref[...] = d
        m2_ref[...] = m2
        v2_ref[...] = v2

    spec = pl.BlockSpec((br, cols), lambda i: (i, 0))
    return pl.pallas_call(
        body, name="adamw_big", grid=(rows // br,),
        out_shape=[jax.ShapeDtypeStruct((rows, cols), F32)] * 3,
        in_specs=[spec] * 4, out_specs=[spec] * 3, compiler_params=_cparams(1),
    )(w, g, m, v)


def _adamw_small(groups):
    n = len(groups)

    def body(*refs):
        ins, outs = refs[:4 * n], refs[4 * n:]
        for i in range(n):
            w, g, m, v = (r[...] for r in ins[4 * i:4 * i + 4])
            d, m2, v2 = _adamw_math(w, g, m, v)
            outs[3 * i][...] = d
            outs[3 * i + 1][...] = m2
            outs[3 * i + 2][...] = v2

    vm = pl.BlockSpec(memory_space=pltpu.VMEM)
    flat = [a for grp in groups for a in grp]
    out_shape = [jax.ShapeDtypeStruct(grp[0].shape, F32) for grp in groups for _ in range(3)]
    outs = pl.pallas_call(body, name="adamw_small", out_shape=out_shape,
                          in_specs=[vm] * (4 * n), out_specs=[vm] * (3 * n))(*flat)
    return [tuple(outs[3 * i:3 * i + 3]) for i in range(n)]


def _load_weights(pairs, sem):
    for src, dst in pairs:
        cp = pltpu.make_async_copy(src, dst, sem)
        cp.start()
        cp.wait()


def _meta_fwd(meta_full, g1, win_all):
    def body(meta_ref, g1_ref, win_ref, z_ref):
        xm = meta_ref[...]
        a = (xm * _rstd(xm) * g1_ref[...]).astype(BF16)
        for j in range(N_CHIPS):
            z_ref[:, j * IN_SHARD:(j + 1) * IN_SHARD] = _dot(a, win_ref[j])

    vm = pl.BlockSpec(memory_space=pltpu.VMEM)
    return pl.pallas_call(body, name="meta_fwd", out_shape=jax.ShapeDtypeStruct((N_META, D_IN_PROJ), F32),
                          in_specs=[vm] * 3, out_specs=vm)(meta_full, g1, win_all)


def _mixer_fwd(x3, zmeta, g1, g2, convw, poolw, pscale, win_all, wout, ffn_shards):
    n_seq, seq, _ = x3.shape
    tm = min(TM_MIX_FWD, seq)
    n_t = seq // tm
    n_steps = n_seq * n_t
    n_ag = len(ffn_shards)

    def body(x_ref, zm_ref, g1_ref, g2_ref, cw_ref, pw_ref, ps_ref, win_hbm, wout_hbm, *rest):
        ag = _AllGather(rest[:n_ag], rest[n_ag + 7:2 * n_ag + 7], *rest[-2:])
        z_ref, m_ref, h1_ref, a_ref, conv_ref, pooled_ref, yc_ref = rest[n_ag:n_ag + 7]
        win_v, wout_v, cvb, pb, sem = rest[2 * n_ag + 7:-2]
        s, t = pl.program_id(0), pl.program_id(1)
        step = s * n_t + t

        @pl.when(step == 0)
        def _():
            ag.start()
            _load_weights([(win_hbm, win_v), (wout_hbm, wout_v)], sem)

        for a in range(n_ag):
            @pl.when(step == ((a + 1) * n_steps) // (2 * n_ag + 2))
            def _():
                ag.relay(a)

        for a in range(n_ag):
            @pl.when(step == min(n_steps // 2 + ((a + 1) * n_steps) // (2 * n_ag + 2), n_steps - 1))
            def _():
                ag.forward(a)

        xt = x_ref[0]
        a = (xt * _rstd(xt) * g1_ref[...]).astype(BF16)
        a_ref[...] = a
        zb = _dot(a, win_v[0])
        zc = _dot(a, win_v[1])
        zv = _dot(a, win_v[2])
        zp = _dot(a, win_v[3])
        z_ref[0, :, 0:IN_SHARD] = zb
        z_ref[0, :, IN_SHARD:2 * IN_SHARD] = zc
        z_ref[0, :, 2 * IN_SHARD:3 * IN_SHARD] = zv

        @pl.when(t == 0)
        def _():
            cvb[0:HALO, :] = zm_ref[:, IN_SHARD:2 * IN_SHARD] * zm_ref[:, 2 * IN_SHARD:3 * IN_SHARD]
            pb[0:HALO, :] = zm_ref[:, 3 * IN_SHARD:4 * IN_SHARD]

        @pl.when(t > 0)
        def _():
            cvb[0:HALO, :] = cvb[tm:tm + HALO, :]
            pb[0:HALO, :] = pb[tm:tm + HALO, :]

        cv = zc * zv
        cvb[HALO:HALO + tm, :] = cv
        pb[HALO:HALO + tm, :] = zp
        cw = cw_ref[...]
        conv = cw[0:1] * cvb[HALO - 2:HALO - 2 + tm, :] + cw[1:2] * cvb[HALO - 1:HALO - 1 + tm, :] + cw[2:3] * cv
        conv_ref[...] = conv
        parts = [(zb * conv).astype(BF16)]
        for g in range(N_POOL_GROUPS):
            pooled = _pool_fwd(pb, g, tm).astype(BF16)
            pooled_ref[:, _gcols(g)] = pooled
            parts.append((_dot(pooled, pw_ref[g]) * ps_ref[:, _gcols(g)]).astype(BF16))
        ycat = jnp.concatenate(parts, axis=1)
        yc_ref[...] = ycat
        m = _dot(ycat, wout_v[...])
        m_ref[0] = m
        h1_ref[0] = xt + m * _rstd(m) * g2_ref[...]

        @pl.when(step == n_steps - 1)
        def _():
            ag.finish()

    n_rows = n_seq * seq
    row = lambda c: pl.BlockSpec((1, tm, c), lambda s, t: (s, t, 0))
    row2 = lambda c: pl.BlockSpec((tm, c), lambda s, t: (s * n_t + t, 0))
    outs = pl.pallas_call(
        body, name="mixer_fwd", grid=(n_seq, n_t),
        out_shape=[jax.ShapeDtypeStruct((n_seq, seq, D_Z), F32), jax.ShapeDtypeStruct((n_seq, seq, D_MODEL), F32),
                   jax.ShapeDtypeStruct((n_seq, seq, D_MODEL), F32), jax.ShapeDtypeStruct((n_rows, D_MODEL), BF16),
                   jax.ShapeDtypeStruct((n_rows, D_CONV), F32), jax.ShapeDtypeStruct((n_rows, D_POOL), BF16),
                   jax.ShapeDtypeStruct((n_rows, D_MODEL), BF16)] + _AllGather.out_shape(ffn_shards),
        in_specs=[row(D_MODEL), _full((N_META, D_IN_PROJ)), _full((1, D_MODEL)), _full((1, D_MODEL)),
                  _full((3, D_CONV)), _full((N_POOL_GROUPS, POOL_GROUP, POOL_GROUP)), _full((1, D_POOL)), ANY, ANY]
        + [ANY] * n_ag,
        out_specs=[row(D_Z), row(D_MODEL), row(D_MODEL), row2(D_MODEL), row2(D_CONV), row2(D_POOL), row2(D_MODEL)]
        + [ANY] * n_ag,
        scratch_shapes=[pltpu.VMEM((N_CHIPS, D_MODEL, IN_SHARD), BF16), pltpu.VMEM((D_MODEL, D_MODEL), BF16),
                        pltpu.VMEM((HALO + tm, D_CONV), F32), pltpu.VMEM((HALO + tm, D_POOL), F32),
                        pltpu.SemaphoreType.DMA] + _AllGather.scratch(n_ag),
        compiler_params=_cparams(2),
    )(x3, zmeta, g1, g2, convw, poolw, pscale, win_all, wout, *ffn_shards)
    return outs[:7], _fill_own_slot(outs[7:], ffn_shards)


def _ffn_chunks():
    out, r0 = [], 0
    while r0 < D_FF:
        out.append((r0, min(FF_CHUNK, D_FF - r0)))
        r0 += FF_CHUNK
    return out


def _ffn_fwd_bwd(h1, target, g3, g4, wg_t, wu_t, wd):
    n_rows = h1.shape[0]
    tm = min(TM_FFN, n_rows)
    chunks = _ffn_chunks()

    def body(h1_ref, t_ref, g3_ref, g4_ref, wg_hbm, wu_hbm, wd_hbm,
             dh1_ref, f_ref, dd_ref, ds_ref, du_ref, gg_ref, loss_ref, dg3_ref, dg4_ref,
             wg_v, wu_v, wd_v, s_sc, u_sc, sem):
        @pl.when(pl.program_id(0) == 0)
        def _():
            _load_weights([(wg_hbm, wg_v), (wu_hbm, wu_v), (wd_hbm, wd_v)], sem)
            loss_ref[...] = jnp.zeros_like(loss_ref)
            dg3_ref[...] = jnp.zeros_like(dg3_ref)
            dg4_ref[...] = jnp.zeros_like(dg4_ref)

        h1v = h1_ref[...]
        r3 = _rstd(h1v)
        hh = h1v * r3
        g3v, g4v = g3_ref[...], g4_ref[...]
        f = (hh * g3v).astype(BF16)
        f_ref[...] = f
        d = jnp.zeros((tm, D_MODEL), F32)
        for r0, sz in chunks:
            s = _dot_nt(f, wg_v[r0:r0 + sz, :])
            u = _dot_nt(f, wu_v[r0:r0 + sz, :])
            s_sc[:, r0:r0 + sz] = s
            u_sc[:, r0:r0 + sz] = u
            gc = (s * _sigmoid(s) * u).astype(BF16)
            gg_ref[:, r0:r0 + sz] = gc
            d = d + _dot(gc, wd_v[r0:r0 + sz, :])
        r4 = _rstd(d)
        dh = d * r4
        err = (h1v + dh * g4v) - t_ref[...]
        loss_ref[...] += _rows8(err * err)
        dy = err * (1.0 / D_MODEL)
        dg4_ref[...] += _rows8(dy * dh)
        ddb = _rms_bwd(dy, dh, r4, g4v).astype(BF16)
        dd_ref[...] = ddb
        df = jnp.zeros((tm, D_MODEL), F32)
        for r0, sz in chunks:
            dgg = _dot_nt(ddb, wd_v[r0:r0 + sz, :])
            s = s_sc[:, r0:r0 + sz]
            u = u_sc[:, r0:r0 + sz]
            sig = _sigmoid(s)
            dsc = (dgg * u * (sig * (1.0 + s * (1.0 - sig)))).astype(BF16)
            duc = (dgg * (s * sig)).astype(BF16)
            ds_ref[:, r0:r0 + sz] = dsc
            du_ref[:, r0:r0 + sz] = duc
            df = df + _dot(dsc, wg_v[r0:r0 + sz, :]) + _dot(duc, wu_v[r0:r0 + sz, :])
        dg3_ref[...] += _rows8(df * hh)
        dh1_ref[...] = dy + _rms_bwd(df, hh, r3, g3v)

    row = pl.BlockSpec((tm, D_MODEL), lambda i: (i, 0))
    ffrow = pl.BlockSpec((tm, D_FF), lambda i: (i, 0))
    acc = _full((8, D_MODEL))
    act_bf = jax.ShapeDtypeStruct((n_rows, D_MODEL), BF16)
    ff_bf = jax.ShapeDtypeStruct((n_rows, D_FF), BF16)
    acc_shape = jax.ShapeDtypeStruct((8, D_MODEL), F32)
    w_vmem = pltpu.VMEM((D_FF, D_MODEL), BF16)
    return pl.pallas_call(
        body, name="ffn_fwd_bwd", grid=(n_rows // tm,),
        out_shape=[jax.ShapeDtypeStruct((n_rows, D_MODEL), F32), act_bf, act_bf, ff_bf, ff_bf, ff_bf,
                   acc_shape, acc_shape, acc_shape],
        in_specs=[row, row, _full((1, D_MODEL)), _full((1, D_MODEL)), ANY, ANY, ANY],
        out_specs=[row, row, row, ffrow, ffrow, ffrow, acc, acc, acc],
        scratch_shapes=[w_vmem, w_vmem, w_vmem, pltpu.VMEM((tm, D_FF), F32), pltpu.VMEM((tm, D_FF), F32),
                        pltpu.SemaphoreType.DMA],
        compiler_params=_cparams(1),
    )(h1, target, g3, g4, wg_t, wu_t, wd)


def _ffn_weight_grads(name, acts, other, exchanged):
    n_rows = other.shape[0]
    n_a, n_ex = len(acts), len(exchanged)
    n_c = n_a
    tk = min(TK_DW // (2 // n_c), n_rows)
    n_k = n_rows // tk
    half = D_FF // n_c

    def body(other_ref, *rest):
        act_refs = rest[:n_a]
        out_refs = rest[n_a + n_ex:2 * n_a + n_ex]
        c, k = pl.program_id(0), pl.program_id(1)
        if n_ex:
            ex = _ExchangeHalves(rest[n_a:n_a + n_ex], rest[2 * n_a + n_ex:2 * n_a + 2 * n_ex], *rest[-2:])

            @pl.when((c == 0) & (k == 0))
            def _():
                ex.start()

        @pl.when(k == 0)
        def _():
            for o in out_refs:
                o[...] = jnp.zeros_like(o)

        ov = other_ref[...]
        for a, o in zip(act_refs, out_refs):
            o[...] += _dot_tn(a[...], ov)

        if n_ex:
            @pl.when((c == n_c - 1) & (k == n_k - 1))
            def _():
                ex.finish()

    row = pl.BlockSpec((tk, D_MODEL), lambda c, k: (k, 0))
    ffrow = pl.BlockSpec((tk, half), lambda c, k: (k, c))
    out = pl.BlockSpec((half, D_MODEL), lambda c, k: (c, 0))
    outs = pl.pallas_call(
        body, name=name, grid=(n_c, n_k),
        out_shape=[jax.ShapeDtypeStruct((D_FF, D_MODEL), F32)] * n_a + _ExchangeHalves.out_shape(exchanged),
        in_specs=[row] + [ffrow] * n_a + [ANY] * n_ex, out_specs=[out] * n_a + [ANY] * n_ex,
        scratch_shapes=_ExchangeHalves.scratch(n_ex) if n_ex else [],
        compiler_params=_cparams(2),
    )(other, *acts, *exchanged)
    return outs[:n_a], outs[n_a:]


def _mixer_bwd(dh1, m3, z3, conv2, pooled2, x3, zmeta, g1, g2, convw, poolw, pscale, win_all, wout, exchanged,
               scattered):
    n_seq, seq, _ = x3.shape
    tm = min(TM_MIX_BWD, seq)
    sub = min(SUB_MIX_BWD, tm)
    n_t = seq // tm
    n_ex, n_sc = len(exchanged), len(scattered)
    n_cm = n_ex + n_sc

    def body(dh1_ref, m_ref, z_ref, conv_ref, pooled_ref, x_ref, zm_ref, g1_ref, g2_ref, cw_ref, pw_ref, ps_ref,
             win_hbm, wout_hbm, *rest):
        outs0 = n_cm + 9
        ex = _ExchangeHalves(rest[:n_ex], rest[outs0:outs0 + n_ex], *rest[-4:-2])
        sc = _ScatterToChips(rest[n_ex:n_cm], rest[outs0 + n_ex:outs0 + n_cm], *rest[-2:])
        dx_ref, dz_ref, dm_ref, dg1_ref, dg2_ref, dsc_ref, dcw_ref, dpw_ref, dzm_ref = rest[n_cm:outs0]
        win_v, wout_v, dcb, dqb, mcb, mqb, sem = rest[outs0 + n_cm:-4]
        s, i = pl.program_id(0), pl.program_id(1)
        tr = n_t - 1 - i

        @pl.when((s == 0) & (i == 0))
        def _():
            sc.start()
            ex.start()
            _load_weights([(win_hbm, win_v), (wout_hbm, wout_v)], sem)
            for ref in (dg1_ref, dg2_ref, dsc_ref, dcw_ref, dpw_ref, dzm_ref):
                ref[...] = jnp.zeros_like(ref)

        @pl.when(i == 0)
        def _():
            dcb[tm:tm + HALO, :] = jnp.zeros((HALO, D_CONV), F32)
            dqb[tm:tm + HALO, :] = jnp.zeros((HALO, D_POOL), F32)

        @pl.when(i > 0)
        def _():
            dcb[tm:tm + HALO, :] = dcb[0:HALO, :]
            dqb[tm:tm + HALO, :] = dqb[0:HALO, :]

        g1v, g2v = g1_ref[...], g2_ref[...]
        cw = cw_ref[...]

        for r0 in range(tm - sub, -1, -sub):
            rows = slice(r0, r0 + sub)
            dh1v = dh1_ref[0, rows, :]
            mv = m_ref[0, rows, :]
            r2 = _rstd(mv)
            mh = mv * r2
            dg2_ref[...] += _rows8(dh1v * mh)
            dmb = _rms_bwd(dh1v, mh, r2, g2v).astype(BF16)
            dm_ref[rows, :] = dmb
            dyc = _dot_nt(dmb, wout_v[...])
            dyconv = dyc[:, 0:D_CONV]

            for g in range(N_POOL_GROUPS):
                pooled = pooled_ref[rows, _gcols(g)]
                mixed = _dot(pooled, pw_ref[g])
                scale = ps_ref[:, _gcols(g)]
                dyp = dyc[:, D_CONV + g * POOL_GROUP:D_CONV + (g + 1) * POOL_GROUP]
                dsc_ref[:, _gcols(g)] += _rows8(dyp * mixed)
                dmix = (dyp * scale).astype(BF16)
                dpw_ref[g] += _dot_tn(pooled, dmix)
                dqb[rows, _gcols(g)] = _dot_nt(dmix, pw_ref[g])

            zb = z_ref[0, rows, 0:IN_SHARD]
            zc = z_ref[0, rows, IN_SHARD:2 * IN_SHARD]
            zv = z_ref[0, rows, 2 * IN_SHARD:3 * IN_SHARD]
            dconv = dyconv * zb
            dcb[rows, :] = dconv
            d1 = dcb[r0 + 1:r0 + 1 + sub, :]
            d2 = dcb[r0 + 2:r0 + 2 + sub, :]
            dcv = cw[2:3] * dconv + cw[1:2] * d1 + cw[0:1] * d2
            cv = zc * zv
            dcw_ref[0:8, :] += _rows8(cv * d2)
            dcw_ref[8:16, :] += _rows8(cv * d1)
            dcw_ref[16:24, :] += _rows8(cv * dconv)
            dzs = [(dyconv * conv_ref[rows, :]).astype(BF16), (dcv * zv).astype(BF16), (dcv * zc).astype(BF16),
                   jnp.concatenate([_pool_bwd(dqb, g, r0, sub) for g in range(N_POOL_GROUPS)], axis=1).astype(BF16)]
            da = jnp.zeros((sub, D_MODEL), F32)
            for j in range(N_CHIPS):
                dz_ref[j, rows, :] = dzs[j]
                da = da + _dot_nt(dzs[j], win_v[j])
            xt = x_ref[0, rows, :]
            r1 = _rstd(xt)
            xh = xt * r1
            dg1_ref[...] += _rows8(da * xh)
            dx_ref[0, rows, :] = dh1v + _rms_bwd(da, xh, r1, g1v)

        @pl.when(tr == 0)
        def _():
            mcb[0:HALO, :] = jnp.zeros((HALO, D_CONV), F32)
            mqb[0:HALO, :] = jnp.zeros((HALO, D_POOL), F32)
            mcb[HALO:2 * HALO, :] = dcb[0:HALO, :]
            mqb[HALO:2 * HALO, :] = dqb[0:HALO, :]
            m1 = mcb[1:1 + HALO, :]
            m2 = mcb[2:2 + HALO, :]
            zc_m = zm_ref[:, IN_SHARD:2 * IN_SHARD]
            zv_m = zm_ref[:, 2 * IN_SHARD:3 * IN_SHARD]
            cv_m = zc_m * zv_m
            dcw_ref[0:8, :] += _rows8(cv_m * m2)
            dcw_ref[8:16, :] += _rows8(cv_m * m1)
            dcv_m = cw[1:2] * m1 + cw[0:1] * m2
            dzm_ref[:, IN_SHARD:2 * IN_SHARD] += dcv_m * zv_m
            dzm_ref[:, 2 * IN_SHARD:3 * IN_SHARD] += dcv_m * zc_m
            dzm_ref[:, 3 * IN_SHARD:4 * IN_SHARD] += jnp.concatenate(
                [_pool_bwd(mqb, g, 0, HALO) for g in range(N_POOL_GROUPS)], axis=1)

        @pl.when((s == n_seq - 1) & (i == n_t - 1))
        def _():
            ex.finish()
            sc.finish()

    row3 = lambda c: pl.BlockSpec((1, tm, c), lambda s, i: (s, n_t - 1 - i, 0))
    row2 = lambda c: pl.BlockSpec((tm, c), lambda s, i: (s * n_t + n_t - 1 - i, 0))
    n_rows = n_seq * seq
    outs = pl.pallas_call(
        body, name="mixer_bwd", grid=(n_seq, n_t),
        out_shape=[jax.ShapeDtypeStruct((n_seq, seq, D_MODEL), F32),
                   jax.ShapeDtypeStruct((N_CHIPS, n_rows, IN_SHARD), BF16), jax.ShapeDtypeStruct((n_rows, D_MODEL), BF16),
                   jax.ShapeDtypeStruct((8, D_MODEL), F32), jax.ShapeDtypeStruct((8, D_MODEL), F32),
                   jax.ShapeDtypeStruct((8, D_POOL), F32), jax.ShapeDtypeStruct((24, D_CONV), F32),
                   jax.ShapeDtypeStruct((N_POOL_GROUPS, POOL_GROUP, POOL_GROUP), F32),
                   jax.ShapeDtypeStruct((N_META, D_IN_PROJ), F32)]
        + _ExchangeHalves.out_shape(exchanged) + _ScatterToChips.out_shape(scattered),
        in_specs=[row3(D_MODEL), row3(D_MODEL), row3(D_Z), row2(D_CONV), row2(D_POOL), row3(D_MODEL),
                  _full((N_META, D_IN_PROJ)), _full((1, D_MODEL)), _full((1, D_MODEL)), _full((3, D_CONV)),
                  _full((N_POOL_GROUPS, POOL_GROUP, POOL_GROUP)), _full((1, D_POOL)), ANY, ANY] + [ANY] * n_cm,
        out_specs=[row3(D_MODEL), pl.BlockSpec((N_CHIPS, tm, IN_SHARD), lambda s, i: (0, s * n_t + n_t - 1 - i, 0)),
                   row2(D_MODEL),
                   _full((8, D_MODEL)), _full((8, D_MODEL)), _full((8, D_POOL)), _full((24, D_CONV)),
                   _full((N_POOL_GROUPS, POOL_GROUP, POOL_GROUP)), _full((N_META, D_IN_PROJ))] + [ANY] * n_cm,
        scratch_shapes=[pltpu.VMEM((N_CHIPS, D_MODEL, IN_SHARD), BF16), pltpu.VMEM((D_MODEL, D_MODEL), BF16),
                        pltpu.VMEM((tm + HALO, D_CONV), F32), pltpu.VMEM((tm + HALO, D_POOL), F32),
                        pltpu.VMEM((2 * HALO, D_CONV), F32), pltpu.VMEM((2 * HALO, D_POOL), F32),
                        pltpu.SemaphoreType.DMA] + _ExchangeHalves.scratch(n_ex) + _ScatterToChips.scratch(n_sc),
        compiler_params=_cparams(2),
    )(dh1, m3, z3, conv2, pooled2, x3, zmeta, g1, g2, convw, poolw, pscale, win_all, wout, *exchanged, *scattered)
    return outs[:9], outs[9:9 + n_ex], outs[9 + n_ex:]


def _meta_bwd(dzm, meta_full, g1, win_all):
    def body(dzm_ref, meta_ref, g1_ref, win_ref, dmeta_ref, dg1_ref, a_ref, dzb_ref):
        xm = meta_ref[...]
        r = _rstd(xm)
        xh = xm * r
        g1v = g1_ref[...]
        a_ref[...] = (xh * g1v).astype(BF16)
        da = jnp.zeros((N_META, D_MODEL), F32)
        for j in range(N_CHIPS):
            dzj = dzm_ref[:, j * IN_SHARD:(j + 1) * IN_SHARD].astype(BF16)
            dzb_ref[j] = dzj
            da = da + _dot_nt(dzj, win_ref[j])
        dg1_ref[...] = _rows8(da * xh)
        dmeta_ref[...] = _rms_bwd(da, xh, r, g1v)

    vm = pl.BlockSpec(memory_space=pltpu.VMEM)
    return pl.pallas_call(
        body, name="meta_bwd",
        out_shape=[jax.ShapeDtypeStruct((N_META, D_MODEL), F32), jax.ShapeDtypeStruct((8, D_MODEL), F32),
                   jax.ShapeDtypeStruct((N_META, D_MODEL), BF16), jax.ShapeDtypeStruct((N_CHIPS, N_META, IN_SHARD), BF16)],
        in_specs=[vm] * 4, out_specs=[vm] * 4,
    )(dzm, meta_full, g1, win_all)


def _mixer_weight_grads(a, dz, ycat, dm, a_meta, dz_meta, ffn_sums, small):
    n_rows = a.shape[0]
    tk = min(TK_DW, n_rows)
    n_k = n_rows // tk
    n_sc, n_sm = len(ffn_sums), _AllReduceSmall.N_IN

    def body(a_ref, dz_ref, yc_ref, dm_ref, am_ref, dzm_ref, *rest):
        ins, outs, scratch = rest[:n_sc + n_sm], rest[n_sc + n_sm:2 * n_sc + n_sm + 5], rest[2 * n_sc + n_sm + 5:]
        dwin_ref, dwout_ref = outs[:2]
        scatter = _ScatterToChips(ins[:n_sc], outs[2:2 + n_sc], *scratch[:2])
        reduce_small = _AllReduceSmall(ins[n_sc:], outs[2 + n_sc:], scratch[2:])
        k = pl.program_id(0)

        @pl.when(k == 0)
        def _():
            scatter.start()
            reduce_small.pack_and_send()
            am_t = am_ref[...].T
            for j in range(N_CHIPS):
                dwin_ref[j] = _dot(am_t, dzm_ref[j])
            dwout_ref[...] = jnp.zeros_like(dwout_ref)

        for st in range(2):
            @pl.when(k == ((st + 1) * n_k) // 3)
            def _():
                reduce_small.combine(st)

        a_t = a_ref[...].T
        for j in range(N_CHIPS):
            dwin_ref[j] += _dot(a_t, dz_ref[j])
        dwout_ref[...] += _dot_tn(yc_ref[...], dm_ref[...])

        @pl.when(k == n_k - 1)
        def _():
            reduce_small.combine(2)
            scatter.finish()

    row = pl.BlockSpec((tk, D_MODEL), lambda k: (k, 0))
    outs = pl.pallas_call(
        body, name="mixer_weight_grads", grid=(n_k,),
        out_shape=[jax.ShapeDtypeStruct((N_CHIPS, D_MODEL, IN_SHARD), F32),
                   jax.ShapeDtypeStruct((D_MODEL, D_MODEL), F32)] + _ScatterToChips.out_shape(ffn_sums)
        + _AllReduceSmall.out_shape(),
        in_specs=[row, pl.BlockSpec((N_CHIPS, tk, IN_SHARD), lambda k: (0, k, 0)), row, row,
                  _full((N_META, D_MODEL)), _full((N_CHIPS, N_META, IN_SHARD))] + [ANY] * n_sc
        + [_full(s.shape) for s in small],
        out_specs=[_full((N_CHIPS, D_MODEL, IN_SHARD)), _full((D_MODEL, D_MODEL))] + [ANY] * n_sc
        + [_full(s) for s in _AllReduceSmall.SHAPES],
        scratch_shapes=_ScatterToChips.scratch(n_sc) + _AllReduceSmall.scratch(),
        compiler_params=_cparams(1),
    )(a, dz, ycat, dm, a_meta, dz_meta, *ffn_sums, *small)
    return ([outs[0], outs[1].reshape(N_CHIPS, OUT_SHARD, D_MODEL)], outs[2:2 + n_sc], outs[2 + n_sc:])


def kernel(x, meta_tokens, norm_mix_pre, w_in, conv_w, pool_w, pool_scale, w_out, norm_mix_post, norm_ffn_pre, w_gate, w_up, w_down, norm_ffn_post, loss_target, m_meta_tokens, m_norm_mix_pre, m_w_in, m_conv_w, m_pool_w, m_pool_scale, m_w_out, m_norm_mix_post, m_norm_ffn_pre, m_w_gate, m_w_up, m_w_down, m_norm_ffn_post, v_meta_tokens, v_norm_mix_pre, v_w_in, v_conv_w, v_pool_w, v_pool_scale, v_w_out, v_norm_mix_post, v_norm_ffn_pre, v_w_gate, v_w_up, v_w_down, v_norm_ffn_post):
    n_seq, seq, _ = x.shape
    n_rows = n_seq * seq
    chip = 2 * lax.axis_index("x") + lax.axis_index("y")
    meta_cols = D_MODEL // N_CHIPS
    conv_cols = D_CONV // N_CHIPS

    small = jnp.zeros((2 * HALO, meta_cols), F32)
    small = small.at[0:N_META, :].set(meta_tokens).at[N_META:N_META + 3, 0:conv_cols].set(conv_w[0])
    win_all, wout_all, small_all = _all_gather_shards([w_in[0].astype(BF16), w_out[0].astype(BF16), small])
    meta_full = small_all[:, 0:N_META, :].transpose(1, 0, 2).reshape(N_META, D_MODEL)
    conv_full = small_all[:, N_META:N_META + 3, 0:conv_cols].transpose(1, 0, 2).reshape(3, D_CONV)
    wout_full = wout_all.reshape(D_MODEL, D_MODEL)
    poolw_bf = pool_w[0].astype(BF16)
    pscale = pool_scale
    g1, g2, g3, g4 = norm_mix_pre, norm_mix_post, norm_ffn_pre, norm_ffn_post
    place = jnp.stack([chip, lax.axis_index("c")]).astype(jnp.int32)

    zmeta = _meta_fwd(meta_full, g1, win_all)
    (z3, m3, h1, a_bf, conv2, pooled2, yc_bf), ffn_w = _mixer_fwd(
        x, zmeta, g1, g2, conv_full, poolw_bf, pscale, win_all, wout_full,
        [w_gate[0].T.astype(BF16), w_up[0].T.astype(BF16), w_down[0].astype(BF16)])
    wg_t, wu_t, wd_full = [w.reshape(D_FF, D_MODEL) for w in ffn_w]
    dh1, f_bf, dd_bf, ds_bf, du_bf, gg_bf, lossp, dg3p, dg4p = _ffn_fwd_bwd(
        h1.reshape(n_rows, D_MODEL), loss_target.reshape(n_rows, D_MODEL), g3, g4, wg_t, wu_t, wd_full)
    as_shards = lambda g: g.reshape(N_CHIPS, FF_SHARD, D_MODEL)
    (dwd,), _ = _ffn_weight_grads("ffn_weight_grads_down", [gg_bf], dd_bf, [])
    dwd = as_shards(dwd)
    (dwg_t,), (dwd_recv,) = _ffn_weight_grads("ffn_weight_grads_gate", [ds_bf], f_bf, [dwd])
    dwg_t = as_shards(dwg_t)
    (dwu_t,), (dwg_recv,) = _ffn_weight_grads("ffn_weight_grads_up", [du_bf], f_bf, [dwg_t])
    dwu_t = as_shards(dwu_t)
    ((grad_x, dz_bf, dm_bf, dg1p, dg2p, dscp, dcwp, dpw, dzm), (dwu_recv,), (dwd_rbuf, dwg_rbuf)) = _mixer_bwd(
        dh1.reshape(n_seq, seq, D_MODEL), m3, z3, conv2, pooled2, x, zmeta, g1, g2, conv_full, poolw_bf, pscale,
        win_all, wout_full, [dwu_t], [_add_pairs(dwd, dwd_recv, place), _add_pairs(dwg_t, dwg_recv, place)])
    dmeta, dg1m, a_meta, dz_meta = _meta_bwd(dzm, meta_full, g1, win_all)
    mix_grads, (dwu_rbuf,), (a_red, b_red, c_red) = _mixer_weight_grads(
        a_bf, dz_bf, yc_bf, dm_bf, a_meta, dz_meta, [_add_pairs(dwu_t, dwu_recv, place)],
        [dg1p, dg1m, dg2p, dg3p, dg4p, lossp, dmeta, dscp, dcwp, dpw.reshape(SMALL_C_ROWS, POOL_GROUP)])

    mix_recvs = _exchange_halves(mix_grads)
    ffn_red, mix_rbufs = _add_chips([dwg_t, dwu_t, dwd], [dwg_recv, dwu_recv, dwd_recv],
                                    [dwg_rbuf, dwu_rbuf, dwd_rbuf], place,
                                    [_add_pairs(g, r, place) for g, r in zip(mix_grads, mix_recvs)],
                                    name="grad_add_chips_ffn")
    mix_red = [_add_chips([g], [r], [rb], place)[0][0] for g, r, rb in zip(mix_grads, mix_recvs, mix_rbufs)]
    reduced = _gather_halves(mix_red + list(ffn_red))
    g_win, g_wout, g_wg_t, g_wu_t, g_wd = [r.reshape(2 * r.shape[1], r.shape[2]) for r in reduced]

    loss = a_red[4, 0]
    g_g1, g_g2, g_g3, g_g4 = a_red[0:1], a_red[1:2], a_red[2:3], a_red[3:4]
    g_meta = lax.dynamic_slice(a_red, (8, chip * meta_cols), (N_META, meta_cols))
    g_pscale = b_red[0:1]
    g_conv = lax.dynamic_slice(b_red, (1, chip * conv_cols), (3, conv_cols))
    g_poolw = c_red

    big = [(w_in[0], g_win, m_w_in[0], v_w_in[0]), (w_out[0], g_wout, m_w_out[0], v_w_out[0]),
           (w_gate[0].T, g_wg_t, m_w_gate[0].T, v_w_gate[0].T), (w_up[0].T, g_wu_t, m_w_up[0].T, v_w_up[0].T),
           (w_down[0], g_wd, m_w_down[0], v_w_down[0])]
    big_out = [_adamw_big(w, g, m, v) for (w, g, m, v) in big]
    big_out[2] = [o.T for o in big_out[2]]
    big_out[3] = [o.T for o in big_out[3]]
    g_wg, g_wu = g_wg_t.T, g_wu_t.T
    small_groups = [
        (meta_tokens, g_meta, m_meta_tokens, v_meta_tokens),
        (g1, g_g1, m_norm_mix_pre, v_norm_mix_pre),
        (conv_w[0], g_conv, m_conv_w[0], v_conv_w[0]),
        (pool_w.reshape(SMALL_C_ROWS, POOL_GROUP), g_poolw, m_pool_w.reshape(SMALL_C_ROWS, POOL_GROUP),
         v_pool_w.reshape(SMALL_C_ROWS, POOL_GROUP)),
        (pool_scale, g_pscale, m_pool_scale, v_pool_scale),
        (g2, g_g2, m_norm_mix_post, v_norm_mix_post),
        (g3, g_g3, m_norm_ffn_pre, v_norm_ffn_pre),
        (g4, g_g4, m_norm_ffn_post, v_norm_ffn_post),
    ]
    small_out = _adamw_small(small_groups)

    grads_out = [g_meta, g_g1, g_win[None], g_conv[None], g_poolw.reshape(pool_w.shape), g_pscale, g_wout[None],
                 g_g2, g_g3, g_wg[None], g_wu[None], g_wd[None], g_g4]
    s_meta, s_g1, s_conv, s_poolw, s_pscale, s_g2, s_g3, s_g4 = small_out
    b_win, b_wout, b_wg, b_wu, b_wd = big_out

    def leaf(k):
        return [s_meta[k], s_g1[k], b_win[k][None], s_conv[k][None], s_poolw[k].reshape(pool_w.shape), s_pscale[k],
                b_wout[k][None], s_g2[k], s_g3[k], b_wg[k][None], b_wu[k][None], b_wd[k][None], s_g4[k]]

    return (loss, grad_x, *grads_out, *leaf(0), *leaf(1), *leaf(2))
```

```python
import jax
import jax.numpy as jnp
from jax import lax
from jax.experimental import pallas as pl
from jax.experimental.pallas import tpu as pltpu

F32 = jnp.float32
BF16 = jnp.bfloat16
MESH = pl.DeviceIdType.MESH

D_MODEL = 1024
D_CONV = 512
D_POOL = 512
POOL_GROUP = 128
N_POOL_GROUPS = 4
D_IN_PROJ = 2048
D_FF = 2816
N_CHIPS = 4
FF_SHARD = D_FF // N_CHIPS
IN_SHARD = D_IN_PROJ // N_CHIPS
OUT_SHARD = D_MODEL // N_CHIPS
D_Z = 3 * IN_SHARD
N_META = 16
HALO = 16
RMS_EPS = 1e-6

ADAM_LR = 0.001
ADAM_B1 = 0.9
ADAM_B2 = 0.999
ADAM_EPS = 1e-08
ADAM_WD = 0.01
ADAM_STEP = 10

TM_MIX_FWD = 512
TM_MIX_BWD = 512
SUB_MIX_BWD = 512
TM_FFN = 256
TK_DW = 1024
FF_CHUNK = 1024
VMEM_LIMIT = 56 * 1024 * 1024


def _cparams(n_grid):
    return pltpu.CompilerParams(dimension_semantics=("arbitrary",) * n_grid, vmem_limit_bytes=VMEM_LIMIT)


def _dot(a, b):
    return jnp.dot(a, b, preferred_element_type=F32)


def _dot_nt(a, b):
    return lax.dot_general(a, b, (((1,), (1,)), ((), ())), preferred_element_type=F32)


def _dot_tn(a, b):
    return lax.dot_general(a, b, (((0,), (0,)), ((), ())), preferred_element_type=F32)


def _rows8(v):
    r, c = v.shape
    return v.reshape(r // 8, 8, c).sum(axis=0)


def _rstd(v):
    return lax.rsqrt(jnp.mean(v * v, axis=-1, keepdims=True) + RMS_EPS)


def _rms_bwd(dy, xhat, rstd, gain):
    dyg = dy * gain
    return rstd * (dyg - xhat * jnp.mean(dyg * xhat, axis=-1, keepdims=True))


def _sigmoid(v):
    return 1.0 / (1.0 + jnp.exp(-v))


def _gcols(g):
    return slice(g * POOL_GROUP, (g + 1) * POOL_GROUP)


def _window_sum(e, g, ahead):
    n = e.shape[0]
    w = e
    for level in range(g + 1):
        shift = 1 << level
        w = w + pltpu.roll(w, (n - shift) if ahead else shift, 0)
    return w


def _pool_fwd(pb, g, n):
    e = pb[0:HALO + n, _gcols(g)]
    return _window_sum(e, g, False)[HALO:, :] * (1.0 / (2 << g)) - e[HALO:, :]


def _pool_bwd(qb, g, r0, n):
    e = qb[r0:r0 + n + HALO, _gcols(g)]
    return _window_sum(e, g, True)[0:n, :] * (1.0 / (2 << g)) - e[0:n, :]


def _full(shape):
    nd = len(shape)
    return pl.BlockSpec(shape, lambda *_: (0,) * nd)


ANY = pl.BlockSpec(memory_space=pl.ANY)


def _mesh_pos():
    x, y, c = lax.axis_index("x"), lax.axis_index("y"), lax.axis_index("c")
    chips = [(1 - x, y), (x, 1 - y), (1 - x, 1 - y)]
    return x, y, c, chips


def _half(ref, h):
    hr = ref.shape[0] // 2
    return ref.at[pl.ds(h * hr, hr), :]


class _AllGather:
    PER_ARRAY = 9

    def __init__(self, ins, outs, send_sems, recv_sems):
        self.ins, self.outs, self.send_sems, self.recv_sems = ins, outs, send_sems, recv_sems
        self.n = len(ins)

    @classmethod
    def scratch(cls, n):
        return [pltpu.SemaphoreType.DMA((cls.PER_ARRAY * n,)), pltpu.SemaphoreType.DMA((cls.PER_ARRAY * n,))]

    @staticmethod
    def out_shape(shards):
        return [jax.ShapeDtypeStruct((N_CHIPS,) + s.shape, s.dtype) for s in shards]

    def _copy(self, a, k, src, dst, to):
        i = self.PER_ARRAY * a + k
        return pltpu.make_async_remote_copy(src_ref=src, dst_ref=dst, send_sem=self.send_sems.at[i],
                                            recv_sem=self.recv_sems.at[i], device_id=to, device_id_type=MESH)

    def _piece(self, a, chip, piece, h=None):
        h = lax.axis_index("c") if h is None else h
        rows = self.ins[a].shape[0] // 4
        return self.outs[a].at[chip].at[pl.ds((2 * h + piece) * rows, rows), :]

    def _own(self, a, k):
        x, y, c, chips = _mesh_pos()
        piece = (1, 0, 0, 1)[k]
        rows = self.ins[a].shape[0] // 4
        src = self.ins[a].at[pl.ds((2 * c + piece) * rows, rows), :]
        return self._copy(a, k, src, self._piece(a, 2 * x + y, piece), (*chips[k // 2], c))

    def _relay(self, a, k):
        x, y, c, chips = _mesh_pos()
        source, to, piece = (chips[1], chips[0], 0) if k == 4 else (chips[0], chips[1], 1)
        rows = self._piece(a, 2 * source[0] + source[1], piece)
        return self._copy(a, k, rows, rows, (*to, c))

    def _sibling(self, a, k, h):
        x, y, c, chips = _mesh_pos()
        chip = chips[k - 6]
        slot = _half(self.outs[a].at[2 * chip[0] + chip[1]], h)
        return self._copy(a, k, slot, slot, (x, y, 1 - c))

    def start(self):
        for a in range(self.n):
            for k in range(4):
                self._own(a, k).start()

    def relay(self, a):
        self._own(a, 2).wait_recv()
        self._relay(a, 4).start()
        self._own(a, 0).wait_recv()
        self._relay(a, 5).start()

    def forward(self, a):
        c = lax.axis_index("c")
        self._own(a, 1).wait_recv()
        self._sibling(a, 6, c).start()
        self._own(a, 3).wait_recv()
        self._sibling(a, 7, c).start()
        self._relay(a, 4).wait_recv()
        self._relay(a, 5).wait_recv()
        self._sibling(a, 8, c).start()

    def finish(self):
        c = lax.axis_index("c")
        for a in range(self.n):
            for k in range(6, 9):
                self._sibling(a, k, 1 - c).wait_recv()
        for a in range(self.n):
            for k in range(4):
                self._own(a, k).wait_send()
            for k in range(4, 6):
                self._relay(a, k).wait_send()
            for k in range(6, 9):
                self._sibling(a, k, c).wait_send()


def _fill_own_slot(gathered, shards):
    chip = 2 * lax.axis_index("x") + lax.axis_index("y")
    return [lax.dynamic_update_slice(o, s[None], (chip, 0, 0)) for o, s in zip(gathered, shards)]


def _all_gather_shards(shards):
    n = len(shards)

    def body(*refs):
        ag = _AllGather(refs[:n], refs[n:2 * n], *refs[2 * n:])
        ag.start()
        for a in range(n):
            ag.relay(a)
        for a in range(n):
            ag.forward(a)
        ag.finish()

    outs = pl.pallas_call(
        body, name="all_gather_weights", out_shape=_AllGather.out_shape(shards),
        in_specs=[ANY] * n, out_specs=[ANY] * n, scratch_shapes=_AllGather.scratch(n),
    )(*shards)
    return _fill_own_slot(outs, shards)


class _ExchangeHalves:
    def __init__(self, ins, recvs, send_sems, recv_sems):
        self.ins, self.recvs, self.send_sems, self.recv_sems = ins, recvs, send_sems, recv_sems

    @staticmethod
    def scratch(n):
        return [pltpu.SemaphoreType.DMA((n,)), pltpu.SemaphoreType.DMA((n,))]

    @staticmethod
    def out_shape(grads):
        return [jax.ShapeDtypeStruct((g.shape[0], g.shape[1] // 2, g.shape[2]), g.dtype) for g in grads]

    def _copies(self):
        x, y, c, _ = _mesh_pos()
        out = []
        for a, (src, dst) in enumerate(zip(self.ins, self.recvs)):
            hr = src.shape[1] // 2
            out.append(pltpu.make_async_remote_copy(
                src_ref=src.at[:, pl.ds((1 - c) * hr, hr), :], dst_ref=dst, send_sem=self.send_sems.at[a],
                recv_sem=self.recv_sems.at[a], device_id=(x, y, 1 - c), device_id_type=MESH))
        return out

    def start(self):
        for cp in self._copies():
            cp.start()

    def finish(self):
        for cp in self._copies():
            cp.wait()


def _exchange_halves(grads):
    n = len(grads)

    def body(*refs):
        ex = _ExchangeHalves(refs[:n], refs[n:2 * n], *refs[2 * n:])
        ex.start()
        ex.finish()

    return pl.pallas_call(
        body, name="grad_exchange_halves", out_shape=_ExchangeHalves.out_shape(grads),
        in_specs=[ANY] * n, out_specs=[ANY] * n, scratch_shapes=_ExchangeHalves.scratch(n),
    )(*grads)


class _ScatterToChips:
    def __init__(self, ins, rbufs, send_sems, recv_sems):
        self.ins, self.rbufs, self.send_sems, self.recv_sems = ins, rbufs, send_sems, recv_sems

    @staticmethod
    def scratch(n):
        return [pltpu.SemaphoreType.DMA((3 * n,)), pltpu.SemaphoreType.DMA((3 * n,))]

    @staticmethod
    def out_shape(sums):
        return [jax.ShapeDtypeStruct((3,) + s.shape[1:], BF16) for s in sums]

    def _copies(self):
        x, y, c, chips = _mesh_pos()
        out = []
        for a, (src, dst) in enumerate(zip(self.ins, self.rbufs)):
            for k, chip in enumerate(chips):
                out.append(pltpu.make_async_remote_copy(
                    src_ref=src.at[2 * chip[0] + chip[1]], dst_ref=dst.at[k], send_sem=self.send_sems.at[3 * a + k],
                    recv_sem=self.recv_sems.at[3 * a + k], device_id=(*chip, c), device_id_type=MESH))
        return out

    def start(self):
        for cp in self._copies():
            cp.start()

    def finish(self):
        for cp in self._copies():
            cp.wait()


def _gather_halves(halves):
    n = len(halves)

    def body(*refs):
        ins, outs = refs[:n], refs[n:2 * n]
        send_sems, recv_sems = refs[2 * n:]
        x, y, c, _ = _mesh_pos()
        sib = (x, y, 1 - c)
        remote = [pltpu.make_async_remote_copy(src_ref=ins[a].at[c], dst_ref=outs[a].at[c],
                                               send_sem=send_sems.at[a], recv_sem=recv_sems.at[a],
                                               device_id=sib, device_id_type=MESH) for a in range(n)]
        for cp in remote:
            cp.start()
        for a in range(n):
            pltpu.make_async_remote_copy(src_ref=ins[a].at[1 - c], dst_ref=outs[a].at[1 - c], send_sem=send_sems.at[a],
                                         recv_sem=recv_sems.at[a], device_id=sib, device_id_type=MESH).wait_recv()
        for cp in remote:
            cp.wait_send()

    return pl.pallas_call(
        body, name="grad_gather_halves",
        out_shape=[jax.ShapeDtypeStruct(h.shape, F32) for h in halves],
        in_specs=[ANY] * n, out_specs=[ANY] * n, input_output_aliases={a: a for a in range(n)},
        scratch_shapes=[pltpu.SemaphoreType.DMA((n,)), pltpu.SemaphoreType.DMA((n,))],
    )(*halves)


SMALL_A_ROWS = 24
SMALL_B_ROWS = 8
SMALL_C_ROWS = N_POOL_GROUPS * POOL_GROUP


class _AllReduceSmall:
    N_IN = 10
    SHAPES = [(SMALL_A_ROWS, D_MODEL), (SMALL_B_ROWS, D_CONV), (SMALL_C_ROWS, POOL_GROUP)]

    def __init__(self, ins, outs, scratch):
        self.ins, self.outs = ins, outs
        self.bufs, self.rcvs, self.send_sems, self.recv_sems = scratch[:3], scratch[3:6], scratch[6], scratch[7]

    @classmethod
    def scratch(cls):
        return ([pltpu.VMEM((3,) + s, F32) for s in cls.SHAPES] + [pltpu.VMEM((3,) + s, F32) for s in cls.SHAPES]
                + [pltpu.SemaphoreType.DMA((9,)), pltpu.SemaphoreType.DMA((9,))])

    @classmethod
    def out_shape(cls):
        return [jax.ShapeDtypeStruct(s, F32) for s in cls.SHAPES]

    def _copies(self, st):
        x, y, c, _ = _mesh_pos()
        peer = [(x, y, 1 - c), (1 - x, y, c), (x, 1 - y, c)][st]
        return [pltpu.make_async_remote_copy(
            src_ref=buf.at[st], dst_ref=rcv.at[st], send_sem=self.send_sems.at[3 * st + i],
            recv_sem=self.recv_sems.at[3 * st + i], device_id=peer, device_id_type=MESH)
            for i, (buf, rcv) in enumerate(zip(self.bufs, self.rcvs))]

    def pack_and_send(self):
        dg1_ref, dg1m_ref, dg2_ref, dg3_ref, dg4_ref, loss_ref, dmeta_ref, dsc_ref, dcw_ref, dpw_ref = self.ins
        a_buf, b_buf, c_buf = self.bufs

        def rowsum(v):
            return jnp.sum(v, axis=0, keepdims=True)

        a_buf[0, 0:1, :] = rowsum(dg1_ref[...] + dg1m_ref[...])
        a_buf[0, 1:2, :] = rowsum(dg2_ref[...])
        a_buf[0, 2:3, :] = rowsum(dg3_ref[...])
        a_buf[0, 3:4, :] = rowsum(dg4_ref[...])
        loss = jnp.sum(rowsum(loss_ref[...]), axis=1, keepdims=True) * (0.5 / D_MODEL)
        a_buf[0, 4:5, :] = jnp.broadcast_to(loss, (1, D_MODEL))
        a_buf[0, 5:8, :] = jnp.zeros((3, D_MODEL), F32)
        a_buf[0, 8:24, :] = dmeta_ref[...]
        b_buf[0, 0:1, :] = rowsum(dsc_ref[...])
        for k in range(3):
            b_buf[0, 1 + k:2 + k, :] = rowsum(dcw_ref[8 * k:8 * k + 8, :])
        b_buf[0, 4:8, :] = jnp.zeros((4, D_CONV), F32)
        c_buf[0] = dpw_ref[...]
        for cp in self._copies(0):
            cp.start()

    def combine(self, st):
        for cp in self._copies(st):
            cp.wait()
        if st < 2:
            for buf, rcv in zip(self.bufs, self.rcvs):
                buf[st + 1] = buf[st] + rcv[st]
            for cp in self._copies(st + 1):
                cp.start()
        else:
            for out, buf, rcv in zip(self.outs, self.bufs, self.rcvs):
                out[...] = buf[st] + rcv[st]


def _row_block(rows):
    for cand in (512, 448, 384, 352, 320, 256, 128, 64, 32, 16):
        if rows % cand == 0:
            return cand
    return rows


def _add_pairs(grad, recv, place):
    n_sh, rows2, cols = grad.shape
    hr = rows2 // 2
    br = _row_block(hr)

    def body(place_ref, a_ref, b_ref, o_ref):
        o_ref[...] = (a_ref[0] + b_ref[...]).astype(BF16)

    return pl.pallas_call(
        body, name="grad_add_pairs",
        grid_spec=pltpu.PrefetchScalarGridSpec(
            num_scalar_prefetch=1, grid=(n_sh, hr // br),
            in_specs=[pl.BlockSpec((1, 1, br, cols), lambda j, i, p: (j, p[1], i, 0)),
                      pl.BlockSpec((1, br, cols), lambda j, i, p: (j, i, 0))],
            out_specs=pl.BlockSpec((1, br, cols), lambda j, i, p: (j, i, 0))),
        out_shape=jax.ShapeDtypeStruct((n_sh, hr, cols), BF16), compiler_params=_cparams(2),
    )(place, grad.reshape(n_sh, 2, hr, cols), recv)


def _add_chips(grads, recvs, rbufs, place, scattered=(), name="grad_add_chips"):
    n, n_sc = len(grads), len(scattered)
    n_sh, rows2, cols = grads[0].shape
    hr = rows2 // 2
    br = _row_block(hr)
    n_steps = hr // br

    def body(place_ref, *refs):
        a_refs, b_refs, r_refs = refs[:n], refs[n:2 * n], refs[2 * n:3 * n]
        o_refs = refs[3 * n + n_sc:4 * n + n_sc]
        if n_sc:
            scatter = _ScatterToChips(refs[3 * n:3 * n + n_sc], refs[4 * n + n_sc:4 * n + 2 * n_sc], *refs[-2:])

            @pl.when(pl.program_id(0) == 0)
            def _():
                scatter.start()

        for a_ref, b_ref, r_ref, o_ref in zip(a_refs, b_refs, r_refs, o_refs):
            own = a_ref[0, 0] + b_ref[0]
            o_ref[0] = ((own + r_ref[0].astype(F32)) + r_ref[1].astype(F32)) + r_ref[2].astype(F32)

        if n_sc:
            @pl.when(pl.program_id(0) == n_steps - 1)
            def _():
                scatter.finish()

    outs = pl.pallas_call(
        body, name=name,
        grid_spec=pltpu.PrefetchScalarGridSpec(
            num_scalar_prefetch=1, grid=(n_steps,),
            in_specs=[pl.BlockSpec((1, 1, br, cols), lambda i, p: (p[0], p[1], i, 0))] * n
            + [pl.BlockSpec((1, br, cols), lambda i, p: (p[0], i, 0))] * n
            + [pl.BlockSpec((3, br, cols), lambda i, p: (0, i, 0))] * n + [ANY] * n_sc,
            out_specs=[pl.BlockSpec((1, br, cols), lambda i, p: (p[1], i, 0))] * n + [ANY] * n_sc,
            scratch_shapes=_ScatterToChips.scratch(n_sc) if n_sc else []),
        out_shape=[jax.ShapeDtypeStruct((2, hr, cols), F32)] * n + _ScatterToChips.out_shape(list(scattered)),
        compiler_params=_cparams(1),
    )(place, *[g.reshape(n_sh, 2, hr, cols) for g in grads], *recvs, *rbufs, *scattered)
    return outs[:n], outs[n:]


def _adamw_math(w, g, m, v):
    m2 = ADAM_B1 * m + (1.0 - ADAM_B1) * g
    v2 = ADAM_B2 * v + (1.0 - ADAM_B2) * (g * g)
    m_hat = m2 / (1.0 - ADAM_B1 ** ADAM_STEP)
    v_hat = v2 / (1.0 - ADAM_B2 ** ADAM_STEP)
    delta = -ADAM_LR * (m_hat / (jnp.sqrt(v_hat) + ADAM_EPS) + ADAM_WD * w)
    return delta, m2, v2


def _adamw_big(w, g, m, v):
    rows, cols = w.shape
    br = _row_block(rows)

    def body(w_ref, g_ref, m_ref, v_ref, d_ref, m2_ref, v2_ref):
        d, m2, v2 = _adamw_math(w_ref[...], g_ref[...], m_ref[...], v_ref[...])
        d_ref[...] = d
        m2_ref[...] = m2
        v2_ref[...] = v2

    spec = pl.BlockSpec((br, cols), lambda i: (i, 0))
    return pl.pallas_call(
        body, name="adamw_big", grid=(rows // br,),
        out_shape=[jax.ShapeDtypeStruct((rows, cols), F32)] * 3,
        in_specs=[spec] * 4, out_specs=[spec] * 3, compiler_params=_cparams(1),
    )(w, g, m, v)


def _adamw_small(groups):
    n = len(groups)

    def body(*refs):
        ins, outs = refs[:4 * n], refs[4 * n:]
        for i in range(n):
            w, g, m, v = (r[...] for r in ins[4 * i:4 * i + 4])
            d, m2, v2 = _adamw_math(w, g, m, v)
            outs[3 * i][...] = d
            outs[3 * i + 1][...] = m2
            outs[3 * i + 2][...] = v2

    vm = pl.BlockSpec(memory_space=pltpu.VMEM)
    flat = [a for grp in groups for a in grp]
    out_shape = [jax.ShapeDtypeStruct(grp[0].shape, F32) for grp in groups for _ in range(3)]
    outs = pl.pallas_call(body, name="adamw_small", out_shape=out_shape,
                          in_specs=[vm] * (4 * n), out_specs=[vm] * (3 * n))(*flat)
    return [tuple(outs[3 * i:3 * i + 3]) for i in range(n)]


def _load_weights(pairs, sem):
    for src, dst in pairs:
        cp = pltpu.make_async_copy(src, dst, sem)
        cp.start()
        cp.wait()


def _meta_fwd(meta_full, g1, win_all):
    def body(meta_ref, g1_ref, win_ref, z_ref):
        xm = meta_ref[...]
        a = (xm * _rstd(xm) * g1_ref[...]).astype(BF16)
        for j in range(N_CHIPS):
            z_ref[:, j * IN_SHARD:(j + 1) * IN_SHARD] = _dot(a, win_ref[j])

    vm = pl.BlockSpec(memory_space=pltpu.VMEM)
    return pl.pallas_call(body, name="meta_fwd", out_shape=jax.ShapeDtypeStruct((N_META, D_IN_PROJ), F32),
                          in_specs=[vm] * 3, out_specs=vm)(meta_full, g1, win_all)


def _mixer_fwd(x3, zmeta, g1, g2, convw, poolw, pscale, win_all, wout, ffn_shards):
    n_seq, seq, _ = x3.shape
    tm = min(TM_MIX_FWD, seq)
    n_t = seq // tm
    n_steps = n_seq * n_t
    n_ag = len(ffn_shards)

    def body(x_ref, zm_ref, g1_ref, g2_ref, cw_ref, pw_ref, ps_ref, win_hbm, wout_hbm, *rest):
        ag = _AllGather(rest[:n_ag], rest[n_ag + 7:2 * n_ag + 7], *rest[-2:])
        z_ref, m_ref, h1_ref, a_ref, conv_ref, pooled_ref, yc_ref = rest[n_ag:n_ag + 7]
        win_v, wout_v, cvb, pb, sem = rest[2 * n_ag + 7:-2]
        s, t = pl.program_id(0), pl.program_id(1)
        step = s * n_t + t

        @pl.when(step == 0)
        def _():
            ag.start()
            _load_weights([(win_hbm, win_v), (wout_hbm, wout_v)], sem)

        for a in range(n_ag):
            @pl.when(step == ((a + 1) * n_steps) // (2 * n_ag + 2))
            def _():
                ag.relay(a)

        for a in range(n_ag):
            @pl.when(step == min(n_steps // 2 + ((a + 1) * n_steps) // (2 * n_ag + 2), n_steps - 1))
            def _():
                ag.forward(a)

        @pl.when(t == 0)
        def _():
            cvb[0:HALO, :] = zm_ref[:, IN_SHARD:2 * IN_SHARD] * zm_ref[:, 2 * IN_SHARD:3 * IN_SHARD]
            pb[0:HALO, :] = zm_ref[:, 3 * IN_SHARD:4 * IN_SHARD]

        @pl.when(t > 0)
        def _():
            cvb[0:HALO, :] = cvb[tm:tm + HALO, :]
            pb[0:HALO, :] = pb[tm:tm + HALO, :]

        xt = x_ref[0]
        a = (xt * _rstd(xt) * g1_ref[...]).astype(BF16)
        a_ref[...] = a
        zb = _dot(a, win_v[0])
        zc = _dot(a, win_v[1])
        zv = _dot(a, win_v[2])
        zp = _dot(a, win_v[3])
        z_ref[0, :, 0:IN_SHARD] = zb
        z_ref[0, :, IN_SHARD:2 * IN_SHARD] = zc
        z_ref[0, :, 2 * IN_SHARD:3 * IN_SHARD] = zv
        cv = zc * zv
        cvb[HALO:HALO + tm, :] = cv
        pb[HALO:HALO + tm, :] = zp
        cw = cw_ref[...]
        conv = cw[0:1] * cvb[HALO - 2:HALO - 2 + tm, :] + cw[1:2] * cvb[HALO - 1:HALO - 1 + tm, :] + cw[2:3] * cv
        conv_ref[...] = conv
        parts = [(zb * conv).astype(BF16)]
        for g in range(N_POOL_GROUPS):
            pooled = _pool_fwd(pb, g, tm).astype(BF16)
            pooled_ref[:, _gcols(g)] = pooled
            parts.append((_dot(pooled, pw_ref[g]) * ps_ref[:, _gcols(g)]).astype(BF16))
        ycat = jnp.concatenate(parts, axis=1)
        yc_ref[...] = ycat
        m = _dot(ycat, wout_v[...])
        m_ref[0] = m
        h1_ref[0] = xt + m * _rstd(m) * g2_ref[...]

        @pl.when(step == n_steps - 1)
        def _():
            ag.finish()

    n_rows = n_seq * seq
    row = lambda c: pl.BlockSpec((1, tm, c), lambda s, t: (s, t, 0))
    row2 = lambda c: pl.BlockSpec((tm, c), lambda s, t: (s * n_t + t, 0))
    outs = pl.pallas_call(
        body, name="mixer_fwd", grid=(n_seq, n_t),
        out_shape=[jax.ShapeDtypeStruct((n_seq, seq, D_Z), F32), jax.ShapeDtypeStruct((n_seq, seq, D_MODEL), F32),
                   jax.ShapeDtypeStruct((n_seq, seq, D_MODEL), F32), jax.ShapeDtypeStruct((n_rows, D_MODEL), BF16),
                   jax.ShapeDtypeStruct((n_rows, D_CONV), F32), jax.ShapeDtypeStruct((n_rows, D_POOL), BF16),
                   jax.ShapeDtypeStruct((n_rows, D_MODEL), BF16)] + _AllGather.out_shape(ffn_shards),
        in_specs=[row(D_MODEL), _full((N_META, D_IN_PROJ)), _full((1, D_MODEL)), _full((1, D_MODEL)),
                  _full((3, D_CONV)), _full((N_POOL_GROUPS, POOL_GROUP, POOL_GROUP)), _full((1, D_POOL)), ANY, ANY]
        + [ANY] * n_ag,
        out_specs=[row(D_Z), row(D_MODEL), row(D_MODEL), row2(D_MODEL), row2(D_CONV), row2(D_POOL), row2(D_MODEL)]
        + [ANY] * n_ag,
        scratch_shapes=[pltpu.VMEM((N_CHIPS, D_MODEL, IN_SHARD), BF16), pltpu.VMEM((D_MODEL, D_MODEL), BF16),
                        pltpu.VMEM((HALO + tm, D_CONV), F32), pltpu.VMEM((HALO + tm, D_POOL), F32),
                        pltpu.SemaphoreType.DMA] + _AllGather.scratch(n_ag),
        compiler_params=_cparams(2),
    )(x3, zmeta, g1, g2, convw, poolw, pscale, win_all, wout, *ffn_shards)
    return outs[:7], _fill_own_slot(outs[7:], ffn_shards)


def _ffn_chunks():
    out, r0 = [], 0
    while r0 < D_FF:
        out.append((r0, min(FF_CHUNK, D_FF - r0)))
        r0 += FF_CHUNK
    return out


def _ffn_fwd_bwd(h1, target, g3, g4, wg_t, wu_t, wd):
    n_rows = h1.shape[0]
    tm = min(TM_FFN, n_rows)
    chunks = _ffn_chunks()

    def body(h1_ref, t_ref, g3_ref, g4_ref, wg_hbm, wu_hbm, wd_hbm,
             dh1_ref, f_ref, dd_ref, ds_ref, du_ref, gg_ref, loss_ref, dg3_ref, dg4_ref,
             wg_v, wu_v, wd_v, s_sc, u_sc, sem):
        @pl.when(pl.program_id(0) == 0)
        def _():
            _load_weights([(wg_hbm, wg_v), (wu_hbm, wu_v), (wd_hbm, wd_v)], sem)
            loss_ref[...] = jnp.zeros_like(loss_ref)
            dg3_ref[...] = jnp.zeros_like(dg3_ref)
            dg4_ref[...] = jnp.zeros_like(dg4_ref)

        h1v = h1_ref[...]
        r3 = _rstd(h1v)
        hh = h1v * r3
        g3v, g4v = g3_ref[...], g4_ref[...]
        f = (hh * g3v).astype(BF16)
        f_ref[...] = f
        d = jnp.zeros((tm, D_MODEL), F32)
        for r0, sz in chunks:
            s = _dot_nt(f, wg_v[r0:r0 + sz, :])
            u = _dot_nt(f, wu_v[r0:r0 + sz, :])
            s_sc[:, r0:r0 + sz] = s
            u_sc[:, r0:r0 + sz] = u
            gc = (s * _sigmoid(s) * u).astype(BF16)
            gg_ref[:, r0:r0 + sz] = gc
            d = d + _dot(gc, wd_v[r0:r0 + sz, :])
        r4 = _rstd(d)
        dh = d * r4
        err = (h1v + dh * g4v) - t_ref[...]
        loss_ref[...] += _rows8(err * err)
        dy = err * (1.0 / D_MODEL)
        dg4_ref[...] += _rows8(dy * dh)
        ddb = _rms_bwd(dy, dh, r4, g4v).astype(BF16)
        dd_ref[...] = ddb
        df = jnp.zeros((tm, D_MODEL), F32)
        for r0, sz in chunks:
            dgg = _dot_nt(ddb, wd_v[r0:r0 + sz, :])
            s = s_sc[:, r0:r0 + sz]
            u = u_sc[:, r0:r0 + sz]
            sig = _sigmoid(s)
            dsc = (dgg * u * (sig * (1.0 + s * (1.0 - sig)))).astype(BF16)
            duc = (dgg * (s * sig)).astype(BF16)
            ds_ref[:, r0:r0 + sz] = dsc
            du_ref[:, r0:r0 + sz] = duc
            df = df + _dot(dsc, wg_v[r0:r0 + sz, :]) + _dot(duc, wu_v[r0:r0 + sz, :])
        dg3_ref[...] += _rows8(df * hh)
        dh1_ref[...] = dy + _rms_bwd(df, hh, r3, g3v)

    row = pl.BlockSpec((tm, D_MODEL), lambda i: (i, 0))
    ffrow = pl.BlockSpec((tm, D_FF), lambda i: (i, 0))
    acc = _full((8, D_MODEL))
    act_bf = jax.ShapeDtypeStruct((n_rows, D_MODEL), BF16)
    ff_bf = jax.ShapeDtypeStruct((n_rows, D_FF), BF16)
    acc_shape = jax.ShapeDtypeStruct((8, D_MODEL), F32)
    w_vmem = pltpu.VMEM((D_FF, D_MODEL), BF16)
    return pl.pallas_call(
        body, name="ffn_fwd_bwd", grid=(n_rows // tm,),
        out_shape=[jax.ShapeDtypeStruct((n_rows, D_MODEL), F32), act_bf, act_bf, ff_bf, ff_bf, ff_bf,
                   acc_shape, acc_shape, acc_shape],
        in_specs=[row, row, _full((1, D_MODEL)), _full((1, D_MODEL)), ANY, ANY, ANY],
        out_specs=[row, row, row, ffrow, ffrow, ffrow, acc, acc, acc],
        scratch_shapes=[w_vmem, w_vmem, w_vmem, pltpu.VMEM((tm, D_FF), F32), pltpu.VMEM((tm, D_FF), F32),
                        pltpu.SemaphoreType.DMA],
        compiler_params=_cparams(1),
    )(h1, target, g3, g4, wg_t, wu_t, wd)


def _ffn_weight_grads(name, acts, other, exchanged):
    n_rows = other.shape[0]
    n_a, n_ex = len(acts), len(exchanged)
    n_c = n_a
    tk = min(TK_DW, n_rows)
    n_k = n_rows // tk
    half = D_FF // n_c

    def body(other_ref, *rest):
        act_refs = rest[:n_a]
        out_refs = rest[n_a + n_ex:2 * n_a + n_ex]
        c, k = pl.program_id(0), pl.program_id(1)
        if n_ex:
            ex = _ExchangeHalves(rest[n_a:n_a + n_ex], rest[2 * n_a + n_ex:2 * n_a + 2 * n_ex], *rest[-2:])

            @pl.when((c == 0) & (k == 0))
            def _():
                ex.start()

        @pl.when(k == 0)
        def _():
            for o in out_refs:
                o[...] = jnp.zeros_like(o)

        ov = other_ref[...]
        for a, o in zip(act_refs, out_refs):
            o[...] += _dot_tn(a[...], ov)

        if n_ex:
            @pl.when((c == n_c - 1) & (k == n_k - 1))
            def _():
                ex.finish()

    row = pl.BlockSpec((tk, D_MODEL), lambda c, k: (k, 0))
    ffrow = pl.BlockSpec((tk, half), lambda c, k: (k, c))
    out = pl.BlockSpec((half, D_MODEL), lambda c, k: (c, 0))
    outs = pl.pallas_call(
        body, name=name, grid=(n_c, n_k),
        out_shape=[jax.ShapeDtypeStruct((D_FF, D_MODEL), F32)] * n_a + _ExchangeHalves.out_shape(exchanged),
        in_specs=[row] + [ffrow] * n_a + [ANY] * n_ex, out_specs=[out] * n_a + [ANY] * n_ex,
        scratch_shapes=_ExchangeHalves.scratch(n_ex) if n_ex else [],
        compiler_params=_cparams(2),
    )(other, *acts, *exchanged)
    return outs[:n_a], outs[n_a:]


def _mixer_bwd(dh1, m3, z3, conv2, pooled2, x3, zmeta, g1, g2, convw, poolw, pscale, win_all, wout, exchanged,
               scattered):
    n_seq, seq, _ = x3.shape
    tm = min(TM_MIX_BWD, seq)
    sub = min(SUB_MIX_BWD, tm)
    n_t = seq // tm
    n_ex, n_sc = len(exchanged), len(scattered)
    n_cm = n_ex + n_sc

    def body(dh1_ref, m_ref, z_ref, conv_ref, pooled_ref, x_ref, zm_ref, g1_ref, g2_ref, cw_ref, pw_ref, ps_ref,
             win_hbm, wout_hbm, *rest):
        outs0 = n_cm + 9
        ex = _ExchangeHalves(rest[:n_ex], rest[outs0:outs0 + n_ex], *rest[-4:-2])
        sc = _ScatterToChips(rest[n_ex:n_cm], rest[outs0 + n_ex:outs0 + n_cm], *rest[-2:])
        dx_ref, dz_ref, dm_ref, dg1_ref, dg2_ref, dsc_ref, dcw_ref, dpw_ref, dzm_ref = rest[n_cm:outs0]
        win_v, wout_v, dcb, dqb, mcb, mqb, sem = rest[outs0 + n_cm:-4]
        s, i = pl.program_id(0), pl.program_id(1)
        tr = n_t - 1 - i

        @pl.when((s == 0) & (i == 0))
        def _():
            sc.start()
            ex.start()
            _load_weights([(win_hbm, win_v), (wout_hbm, wout_v)], sem)
            for ref in (dg1_ref, dg2_ref, dsc_ref, dcw_ref, dpw_ref, dzm_ref):
                ref[...] = jnp.zeros_like(ref)

        @pl.when(i == 0)
        def _():
            dcb[tm:tm + HALO, :] = jnp.zeros((HALO, D_CONV), F32)
            dqb[tm:tm + HALO, :] = jnp.zeros((HALO, D_POOL), F32)

        @pl.when(i > 0)
        def _():
            dcb[tm:tm + HALO, :] = dcb[0:HALO, :]
            dqb[tm:tm + HALO, :] = dqb[0:HALO, :]

        g1v, g2v = g1_ref[...], g2_ref[...]
        cw = cw_ref[...]

        for r0 in range(tm - sub, -1, -sub):
            rows = slice(r0, r0 + sub)
            dh1v = dh1_ref[0, rows, :]
            mv = m_ref[0, rows, :]
            r2 = _rstd(mv)
            mh = mv * r2
            dg2_ref[...] += _rows8(dh1v * mh)
            dmb = _rms_bwd(dh1v, mh, r2, g2v).astype(BF16)
            dm_ref[rows, :] = dmb
            dyc = _dot_nt(dmb, wout_v[...])
            dyconv = dyc[:, 0:D_CONV]

            for g in range(N_POOL_GROUPS):
                pooled = pooled_ref[rows, _gcols(g)]
                mixed = _dot(pooled, pw_ref[g])
                scale = ps_ref[:, _gcols(g)]
                dyp = dyc[:, D_CONV + g * POOL_GROUP:D_CONV + (g + 1) * POOL_GROUP]
                dsc_ref[:, _gcols(g)] += _rows8(dyp * mixed)
                dmix = (dyp * scale).astype(BF16)
                dpw_ref[g] += _dot_tn(pooled, dmix)
                dqb[rows, _gcols(g)] = _dot_nt(dmix, pw_ref[g])

            zb = z_ref[0, rows, 0:IN_SHARD]
            zc = z_ref[0, rows, IN_SHARD:2 * IN_SHARD]
            zv = z_ref[0, rows, 2 * IN_SHARD:3 * IN_SHARD]
            dconv = dyconv * zb
            dcb[rows, :] = dconv
            d1 = dcb[r0 + 1:r0 + 1 + sub, :]
            d2 = dcb[r0 + 2:r0 + 2 + sub, :]
            dcv = cw[2:3] * dconv + cw[1:2] * d1 + cw[0:1] * d2
            cv = zc * zv
            dcw_ref[0:8, :] += _rows8(cv * d2)
            dcw_ref[8:16, :] += _rows8(cv * d1)
            dcw_ref[16:24, :] += _rows8(cv * dconv)
            dzs = [(dyconv * conv_ref[rows, :]).astype(BF16), (dcv * zv).astype(BF16), (dcv * zc).astype(BF16),
                   jnp.concatenate([_pool_bwd(dqb, g, r0, sub) for g in range(N_POOL_GROUPS)], axis=1).astype(BF16)]
            da = jnp.zeros((sub, D_MODEL), F32)
            for j in range(N_CHIPS):
                dz_ref[j, rows, :] = dzs[j]
                da = da + _dot_nt(dzs[j], win_v[j])
            xt = x_ref[0, rows, :]
            r1 = _rstd(xt)
            xh = xt * r1
            dg1_ref[...] += _rows8(da * xh)
            dx_ref[0, rows, :] = dh1v + _rms_bwd(da, xh, r1, g1v)

        @pl.when(tr == 0)
        def _():
            mcb[0:HALO, :] = jnp.zeros((HALO, D_CONV), F32)
            mqb[0:HALO, :] = jnp.zeros((HALO, D_POOL), F32)
            mcb[HALO:2 * HALO, :] = dcb[0:HALO, :]
            mqb[HALO:2 * HALO, :] = dqb[0:HALO, :]
            m1 = mcb[1:1 + HALO, :]
            m2 = mcb[2:2 + HALO, :]
            zc_m = zm_ref[:, IN_SHARD:2 * IN_SHARD]
            zv_m = zm_ref[:, 2 * IN_SHARD:3 * IN_SHARD]
            cv_m = zc_m * zv_m
            dcw_ref[0:8, :] += _rows8(cv_m * m2)
            dcw_ref[8:16, :] += _rows8(cv_m * m1)
            dcv_m = cw[1:2] * m1 + cw[0:1] * m2
            dzm_ref[:, IN_SHARD:2 * IN_SHARD] += dcv_m * zv_m
            dzm_ref[:, 2 * IN_SHARD:3 * IN_SHARD] += dcv_m * zc_m
            dzm_ref[:, 3 * IN_SHARD:4 * IN_SHARD] += jnp.concatenate(
                [_pool_bwd(mqb, g, 0, HALO) for g in range(N_POOL_GROUPS)], axis=1)

        @pl.when((s == n_seq - 1) & (i == n_t - 1))
        def _():
            ex.finish()
            sc.finish()

    row3 = lambda c: pl.BlockSpec((1, tm, c), lambda s, i: (s, n_t - 1 - i, 0))
    row2 = lambda c: pl.BlockSpec((tm, c), lambda s, i: (s * n_t + n_t - 1 - i, 0))
    n_rows = n_seq * seq
    outs = pl.pallas_call(
        body, name="mixer_bwd", grid=(n_seq, n_t),
        out_shape=[jax.ShapeDtypeStruct((n_seq, seq, D_MODEL), F32),
                   jax.ShapeDtypeStruct((N_CHIPS, n_rows, IN_SHARD), BF16), jax.ShapeDtypeStruct((n_rows, D_MODEL), BF16),
                   jax.ShapeDtypeStruct((8, D_MODEL), F32), jax.ShapeDtypeStruct((8, D_MODEL), F32),
                   jax.ShapeDtypeStruct((8, D_POOL), F32), jax.ShapeDtypeStruct((24, D_CONV), F32),
                   jax.ShapeDtypeStruct((N_POOL_GROUPS, POOL_GROUP, POOL_GROUP), F32),
                   jax.ShapeDtypeStruct((N_META, D_IN_PROJ), F32)]
        + _ExchangeHalves.out_shape(exchanged) + _ScatterToChips.out_shape(scattered),
        in_specs=[row3(D_MODEL), row3(D_MODEL), row3(D_Z), row2(D_CONV), row2(D_POOL), row3(D_MODEL),
                  _full((N_META, D_IN_PROJ)), _full((1, D_MODEL)), _full((1, D_MODEL)), _full((3, D_CONV)),
                  _full((N_POOL_GROUPS, POOL_GROUP, POOL_GROUP)), _full((1, D_POOL)), ANY, ANY] + [ANY] * n_cm,
        out_specs=[row3(D_MODEL), pl.BlockSpec((N_CHIPS, tm, IN_SHARD), lambda s, i: (0, s * n_t + n_t - 1 - i, 0)),
                   row2(D_MODEL),
                   _full((8, D_MODEL)), _full((8, D_MODEL)), _full((8, D_POOL)), _full((24, D_CONV)),
                   _full((N_POOL_GROUPS, POOL_GROUP, POOL_GROUP)), _full((N_META, D_IN_PROJ))] + [ANY] * n_cm,
        scratch_shapes=[pltpu.VMEM((N_CHIPS, D_MODEL, IN_SHARD), BF16), pltpu.VMEM((D_MODEL, D_MODEL), BF16),
                        pltpu.VMEM((tm + HALO, D_CONV), F32), pltpu.VMEM((tm + HALO, D_POOL), F32),
                        pltpu.VMEM((2 * HALO, D_CONV), F32), pltpu.VMEM((2 * HALO, D_POOL), F32),
                        pltpu.SemaphoreType.DMA] + _ExchangeHalves.scratch(n_ex) + _ScatterToChips.scratch(n_sc),
        compiler_params=_cparams(2),
    )(dh1, m3, z3, conv2, pooled2, x3, zmeta, g1, g2, convw, poolw, pscale, win_all, wout, *exchanged, *scattered)
    return outs[:9], outs[9:9 + n_ex], outs[9 + n_ex:]


def _meta_bwd(dzm, meta_full, g1, win_all):
    def body(dzm_ref, meta_ref, g1_ref, win_ref, dmeta_ref, dg1_ref, a_ref, dzb_ref):
        xm = meta_ref[...]
        r = _rstd(xm)
        xh = xm * r
        g1v = g1_ref[...]
        a_ref[...] = (xh * g1v).astype(BF16)
        da = jnp.zeros((N_META, D_MODEL), F32)
        for j in range(N_CHIPS):
            dzj = dzm_ref[:, j * IN_SHARD:(j + 1) * IN_SHARD].astype(BF16)
            dzb_ref[j] = dzj
            da = da + _dot_nt(dzj, win_ref[j])
        dg1_ref[...] = _rows8(da * xh)
        dmeta_ref[...] = _rms_bwd(da, xh, r, g1v)

    vm = pl.BlockSpec(memory_space=pltpu.VMEM)
    return pl.pallas_call(
        body, name="meta_bwd",
        out_shape=[jax.ShapeDtypeStruct((N_META, D_MODEL), F32), jax.ShapeDtypeStruct((8, D_MODEL), F32),
                   jax.ShapeDtypeStruct((N_META, D_MODEL), BF16), jax.ShapeDtypeStruct((N_CHIPS, N_META, IN_SHARD), BF16)],
        in_specs=[vm] * 4, out_specs=[vm] * 4,
    )(dzm, meta_full, g1, win_all)


def _mixer_weight_grads(a, dz, ycat, dm, a_meta, dz_meta, ffn_sums, small):
    n_rows = a.shape[0]
    tk = min(TK_DW, n_rows)
    n_k = n_rows // tk
    n_sc, n_sm = len(ffn_sums), _AllReduceSmall.N_IN

    def body(a_ref, dz_ref, yc_ref, dm_ref, am_ref, dzm_ref, *rest):
        ins, outs, scratch = rest[:n_sc + n_sm], rest[n_sc + n_sm:2 * n_sc + n_sm + 5], rest[2 * n_sc + n_sm + 5:]
        dwin_ref, dwout_ref = outs[:2]
        scatter = _ScatterToChips(ins[:n_sc], outs[2:2 + n_sc], *scratch[:2])
        reduce_small = _AllReduceSmall(ins[n_sc:], outs[2 + n_sc:], scratch[2:])
        k = pl.program_id(0)

        @pl.when(k == 0)
        def _():
            scatter.start()
            reduce_small.pack_and_send()
            am_t = am_ref[...].T
            for j in range(N_CHIPS):
                dwin_ref[j] = _dot(am_t, dzm_ref[j])
            dwout_ref[...] = jnp.zeros_like(dwout_ref)

        for st in range(2):
            @pl.when(k == ((st + 1) * n_k) // 3)
            def _():
                reduce_small.combine(st)

        a_t = a_ref[...].T
        for j in range(N_CHIPS):
            dwin_ref[j] += _dot(a_t, dz_ref[j])
        dwout_ref[...] += _dot_tn(yc_ref[...], dm_ref[...])

        @pl.when(k == n_k - 1)
        def _():
            reduce_small.combine(2)
            scatter.finish()

    row = pl.BlockSpec((tk, D_MODEL), lambda k: (k, 0))
    outs = pl.pallas_call(
        body, name="mixer_weight_grads", grid=(n_k,),
        out_shape=[jax.ShapeDtypeStruct((N_CHIPS, D_MODEL, IN_SHARD), F32),
                   jax.ShapeDtypeStruct((D_MODEL, D_MODEL), F32)] + _ScatterToChips.out_shape(ffn_sums)
        + _AllReduceSmall.out_shape(),
        in_specs=[row, pl.BlockSpec((N_CHIPS, tk, IN_SHARD), lambda k: (0, k, 0)), row, row,
                  _full((N_META, D_MODEL)), _full((N_CHIPS, N_META, IN_SHARD))] + [ANY] * n_sc
        + [_full(s.shape) for s in small],
        out_specs=[_full((N_CHIPS, D_MODEL, IN_SHARD)), _full((D_MODEL, D_MODEL))] + [ANY] * n_sc
        + [_full(s) for s in _AllReduceSmall.SHAPES],
        scratch_shapes=_ScatterToChips.scratch(n_sc) + _AllReduceSmall.scratch(),
        compiler_params=_cparams(1),
    )(a, dz, ycat, dm, a_meta, dz_meta, *ffn_sums, *small)
    return ([outs[0], outs[1].reshape(N_CHIPS, OUT_SHARD, D_MODEL)], outs[2:2 + n_sc], outs[2 + n_sc:])


def kernel(x, meta_tokens, norm_mix_pre, w_in, conv_w, pool_w, pool_scale, w_out, norm_mix_post, norm_ffn_pre, w_gate, w_up, w_down, norm_ffn_post, loss_target, m_meta_tokens, m_norm_mix_pre, m_w_in, m_conv_w, m_pool_w, m_pool_scale, m_w_out, m_norm_mix_post, m_norm_ffn_pre, m_w_gate, m_w_up, m_w_down, m_norm_ffn_post, v_meta_tokens, v_norm_mix_pre, v_w_in, v_conv_w, v_pool_w, v_pool_scale, v_w_out, v_norm_mix_post, v_norm_ffn_pre, v_w_gate, v_w_up, v_w_down, v_norm_ffn_post):
    n_seq, seq, _ = x.shape
    n_rows = n_seq * seq
    chip = 2 * lax.axis_index("x") + lax.axis_index("y")
    meta_cols = D_MODEL // N_CHIPS
    conv_cols = D_CONV // N_CHIPS

    small = jnp.zeros((2 * HALO, meta_cols), F32)
    small = small.at[0:N_META, :].set(meta_tokens).at[N_META:N_META + 3, 0:conv_cols].set(conv_w[0])
    win_all, wout_all, small_all = _all_gather_shards([w_in[0].astype(BF16), w_out[0].astype(BF16), small])
    meta_full = small_all[:, 0:N_META, :].transpose(1, 0, 2).reshape(N_META, D_MODEL)
    conv_full = small_all[:, N_META:N_META + 3, 0:conv_cols].transpose(1, 0, 2).reshape(3, D_CONV)
    wout_full = wout_all.reshape(D_MODEL, D_MODEL)
    poolw_bf = pool_w[0].astype(BF16)
    pscale = pool_scale
    g1, g2, g3, g4 = norm_mix_pre, norm_mix_post, norm_ffn_pre, norm_ffn_post
    place = jnp.stack([chip, lax.axis_index("c")]).astype(jnp.int32)

    zmeta = _meta_fwd(meta_full, g1, win_all)
    (z3, m3, h1, a_bf, conv2, pooled2, yc_bf), ffn_w = _mixer_fwd(
        x, zmeta, g1, g2, conv_full, poolw_bf, pscale, win_all, wout_full,
        [w_gate[0].T.astype(BF16), w_up[0].T.astype(BF16), w_down[0].astype(BF16)])
    wg_t, wu_t, wd_full = [w.reshape(D_FF, D_MODEL) for w in ffn_w]
    dh1, f_bf, dd_bf, ds_bf, du_bf, gg_bf, lossp, dg3p, dg4p = _ffn_fwd_bwd(
        h1.reshape(n_rows, D_MODEL), loss_target.reshape(n_rows, D_MODEL), g3, g4, wg_t, wu_t, wd_full)
    as_shards = lambda g: g.reshape(N_CHIPS, FF_SHARD, D_MODEL)
    (dwd,), _ = _ffn_weight_grads("ffn_weight_grads_down", [gg_bf], dd_bf, [])
    dwd = as_shards(dwd)
    (dwg_t,), (dwd_recv,) = _ffn_weight_grads("ffn_weight_grads_gate", [ds_bf], f_bf, [dwd])
    dwg_t = as_shards(dwg_t)
    (dwu_t,), (dwg_recv,) = _ffn_weight_grads("ffn_weight_grads_up", [du_bf], f_bf, [dwg_t])
    dwu_t = as_shards(dwu_t)
    ((grad_x, dz_bf, dm_bf, dg1p, dg2p, dscp, dcwp, dpw, dzm), (dwu_recv,), (dwd_rbuf, dwg_rbuf)) = _mixer_bwd(
        dh1.reshape(n_seq, seq, D_MODEL), m3, z3, conv2, pooled2, x, zmeta, g1, g2, conv_full, poolw_bf, pscale,
        win_all, wout_full, [dwu_t], [_add_pairs(dwd, dwd_recv, place), _add_pairs(dwg_t, dwg_recv, place)])
    dmeta, dg1m, a_meta, dz_meta = _meta_bwd(dzm, meta_full, g1, win_all)
    mix_grads, (dwu_rbuf,), (a_red, b_red, c_red) = _mixer_weight_grads(
        a_bf, dz_bf, yc_bf, dm_bf, a_meta, dz_meta, [_add_pairs(dwu_t, dwu_recv, place)],
        [dg1p, dg1m, dg2p, dg3p, dg4p, lossp, dmeta, dscp, dcwp, dpw.reshape(SMALL_C_ROWS, POOL_GROUP)])

    mix_recvs = _exchange_halves(mix_grads)
    ffn_red, mix_rbufs = _add_chips([dwg_t, dwu_t, dwd], [dwg_recv, dwu_recv, dwd_recv],
                                    [dwg_rbuf, dwu_rbuf, dwd_rbuf], place,
                                    [_add_pairs(g, r, place) for g, r in zip(mix_grads, mix_recvs)],
                                    name="grad_add_chips_ffn")
    mix_red = [_add_chips([g], [r], [rb], place)[0][0] for g, r, rb in zip(mix_grads, mix_recvs, mix_rbufs)]
    reduced = _gather_halves(mix_red + list(ffn_red))
    g_win, g_wout, g_wg_t, g_wu_t, g_wd = [r.reshape(2 * r.shape[1], r.shape[2]) for r in reduced]

    loss = a_red[4, 0]
    g_g1, g_g2, g_g3, g_g4 = a_red[0:1], a_red[1:2], a_red[2:3], a_red[3:4]
    g_meta = lax.dynamic_slice(a_red, (8, chip * meta_cols), (N_META, meta_cols))
    g_pscale = b_red[0:1]
    g_conv = lax.dynamic_slice(b_red, (1, chip * conv_cols), (3, conv_cols))
    g_poolw = c_red

    big = [(w_in[0], g_win, m_w_in[0], v_w_in[0]), (w_out[0], g_wout, m_w_out[0], v_w_out[0]),
           (w_gate[0].T, g_wg_t, m_w_gate[0].T, v_w_gate[0].T), (w_up[0].T, g_wu_t, m_w_up[0].T, v_w_up[0].T),
           (w_down[0], g_wd, m_w_down[0], v_w_down[0])]
    big_out = [_adamw_big(w, g, m, v) for (w, g, m, v) in big]
    big_out[2] = [o.T for o in big_out[2]]
    big_out[3] = [o.T for o in big_out[3]]
    g_wg, g_wu = g_wg_t.T, g_wu_t.T
    small_groups = [
        (meta_tokens, g_meta, m_meta_tokens, v_meta_tokens),
        (g1, g_g1, m_norm_mix_pre, v_norm_mix_pre),
        (conv_w[0], g_conv, m_conv_w[0], v_conv_w[0]),
        (pool_w.reshape(SMALL_C_ROWS, POOL_GROUP), g_poolw, m_pool_w.reshape(SMALL_C_ROWS, POOL_GROUP),
         v_pool_w.reshape(SMALL_C_ROWS, POOL_GROUP)),
        (pool_scale, g_pscale, m_pool_scale, v_pool_scale),
        (g2, g_g2, m_norm_mix_post, v_norm_mix_post),
        (g3, g_g3, m_norm_ffn_pre, v_norm_ffn_pre),
        (g4, g_g4, m_norm_ffn_post, v_norm_ffn_post),
    ]
    small_out = _adamw_small(small_groups)

    grads_out = [g_meta, g_g1, g_win[None], g_conv[None], g_poolw.reshape(pool_w.shape), g_pscale, g_wout[None],
                 g_g2, g_g3, g_wg[None], g_wu[None], g_wd[None], g_g4]
    s_meta, s_g1, s_conv, s_poolw, s_pscale, s_g2, s_g3, s_g4 = small_out
    b_win, b_wout, b_wg, b_wu, b_wd = big_out

    def leaf(k):
        return [s_meta[k], s_g1[k], b_win[k][None], s_conv[k][None], s_poolw[k].reshape(pool_w.shape), s_pscale[k],
                b_wout[k][None], s_g2[k], s_g3[k], b_wg[k][None], b_wu[k][None], b_wd[k][None], s_g4[k]]

    return (loss, grad_x, *grads_out, *leaf(0), *leaf(1), *leaf(2))
```

```python
import jax
import jax.numpy as jnp
from jax import lax
from jax.experimental import pallas as pl
from jax.experimental.pallas import tpu as pltpu

F32 = jnp.float32
BF16 = jnp.bfloat16
MESH = pl.DeviceIdType.MESH

D_MODEL = 1024
D_CONV = 512
D_POOL = 512
POOL_GROUP = 128
N_POOL_GROUPS = 4
D_IN_PROJ = 2048
D_FF = 2816
N_CHIPS = 4
FF_SHARD = D_FF // N_CHIPS
IN_SHARD = D_IN_PROJ // N_CHIPS
OUT_SHARD = D_MODEL // N_CHIPS
D_Z = 3 * IN_SHARD
N_META = 16
HALO = 16
RMS_EPS = 1e-6

ADAM_LR = 0.001
ADAM_B1 = 0.9
ADAM_B2 = 0.999
ADAM_EPS = 1e-08
ADAM_WD = 0.01
ADAM_STEP = 10

TM_MIX_FWD = 512
TM_MIX_BWD = 512
SUB_MIX_BWD = 512
TM_FFN = 256
TK_DW = 1024
FF_CHUNK = 1024
VMEM_LIMIT = 56 * 1024 * 1024


def _cparams(n_grid):
    return pltpu.CompilerParams(dimension_semantics=("arbitrary",) * n_grid, vmem_limit_bytes=VMEM_LIMIT)


def _dot(a, b):
    return jnp.dot(a, b, preferred_element_type=F32)


def _dot_nt(a, b):
    return lax.dot_general(a, b, (((1,), (1,)), ((), ())), preferred_element_type=F32)


def _dot_tn(a, b):
    return lax.dot_general(a, b, (((0,), (0,)), ((), ())), preferred_element_type=F32)


def _rows8(v):
    r, c = v.shape
    return v.reshape(r // 8, 8, c).sum(axis=0)


def _rstd(v):
    return lax.rsqrt(jnp.mean(v * v, axis=-1, keepdims=True) + RMS_EPS)


def _rms_bwd(dy, xhat, rstd, gain):
    dyg = dy * gain
    return rstd * (dyg - xhat * jnp.mean(dyg * xhat, axis=-1, keepdims=True))


def _sigmoid(v):
    return 1.0 / (1.0 + jnp.exp(-v))


def _gcols(g):
    return slice(g * POOL_GROUP, (g + 1) * POOL_GROUP)


def _window_sum(e, g, ahead):
    n = e.shape[0]
    w = e
    for level in range(g + 1):
        shift = 1 << level
        w = w + pltpu.roll(w, (n - shift) if ahead else shift, 0)
    return w


def _pool_fwd(pb, g, n):
    e = pb[0:HALO + n, _gcols(g)]
    return _window_sum(e, g, False)[HALO:, :] * (1.0 / (2 << g)) - e[HALO:, :]


def _pool_bwd(qb, g, r0, n):
    e = qb[r0:r0 + n + HALO, _gcols(g)]
    return _window_sum(e, g, True)[0:n, :] * (1.0 / (2 << g)) - e[0:n, :]


def _full(shape):
    nd = len(shape)
    return pl.BlockSpec(shape, lambda *_: (0,) * nd)


ANY = pl.BlockSpec(memory_space=pl.ANY)


def _mesh_pos():
    x, y, c = lax.axis_index("x"), lax.axis_index("y"), lax.axis_index("c")
    chips = [(1 - x, y), (x, 1 - y), (1 - x, 1 - y)]
    return x, y, c, chips


def _half(ref, h):
    hr = ref.shape[0] // 2
    return ref.at[pl.ds(h * hr, hr), :]


class _AllGather:
    PER_ARRAY = 9

    def __init__(self, ins, outs, send_sems, recv_sems):
        self.ins, self.outs, self.send_sems, self.recv_sems = ins, outs, send_sems, recv_sems
        self.n = len(ins)

    @classmethod
    def scratch(cls, n):
        return [pltpu.SemaphoreType.DMA((cls.PER_ARRAY * n,)), pltpu.SemaphoreType.DMA((cls.PER_ARRAY * n,))]

    @staticmethod
    def out_shape(shards):
        return [jax.ShapeDtypeStruct((N_CHIPS,) + s.shape, s.dtype) for s in shards]

    def _copy(self, a, k, src, dst, to):
        i = self.PER_ARRAY * a + k
        return pltpu.make_async_remote_copy(src_ref=src, dst_ref=dst, send_sem=self.send_sems.at[i],
                                            recv_sem=self.recv_sems.at[i], device_id=to, device_id_type=MESH)

    def _piece(self, a, chip, piece, h=None):
        h = lax.axis_index("c") if h is None else h
        rows = self.ins[a].shape[0] // 4
        return self.outs[a].at[chip].at[pl.ds((2 * h + piece) * rows, rows), :]

    def _own(self, a, k):
        x, y, c, chips = _mesh_pos()
        piece = (1, 0, 0, 1)[k]
        rows = self.ins[a].shape[0] // 4
        src = self.ins[a].at[pl.ds((2 * c + piece) * rows, rows), :]
        return self._copy(a, k, src, self._piece(a, 2 * x + y, piece), (*chips[k // 2], c))

    def _relay(self, a, k):
        x, y, c, chips = _mesh_pos()
        source, to, piece = (chips[1], chips[0], 0) if k == 4 else (chips[0], chips[1], 1)
        rows = self._piece(a, 2 * source[0] + source[1], piece)
        return self._copy(a, k, rows, rows, (*to, c))

    def _sibling(self, a, k, h):
        x, y, c, chips = _mesh_pos()
        chip = chips[k - 6]
        slot = _half(self.outs[a].at[2 * chip[0] + chip[1]], h)
        return self._copy(a, k, slot, slot, (x, y, 1 - c))

    def start(self):
        for a in range(self.n):
            for k in range(4):
                self._own(a, k).start()

    def relay(self, a):
        self._own(a, 2).wait_recv()
        self._relay(a, 4).start()
        self._own(a, 0).wait_recv()
        self._relay(a, 5).start()

    def forward(self, a):
        c = lax.axis_index("c")
        self._own(a, 1).wait_recv()
        self._sibling(a, 6, c).start()
        self._own(a, 3).wait_recv()
        self._sibling(a, 7, c).start()
        self._relay(a, 4).wait_recv()
        self._relay(a, 5).wait_recv()
        self._sibling(a, 8, c).start()

    def finish(self):
        c = lax.axis_index("c")
        for a in range(self.n):
            for k in range(6, 9):
                self._sibling(a, k, 1 - c).wait_recv()
        for a in range(self.n):
            for k in range(4):
                self._own(a, k).wait_send()
            for k in range(4, 6):
                self._relay(a, k).wait_send()
            for k in range(6, 9):
                self._sibling(a, k, c).wait_send()


def _fill_own_slot(gathered, shards):
    chip = 2 * lax.axis_index("x") + lax.axis_index("y")
    return [lax.dynamic_update_slice(o, s[None], (chip, 0, 0)) for o, s in zip(gathered, shards)]


def _all_gather_shards(shards):
    n = len(shards)

    def body(*refs):
        ag = _AllGather(refs[:n], refs[n:2 * n], *refs[2 * n:])
        ag.start()
        for a in range(n):
            ag.relay(a)
        for a in range(n):
            ag.forward(a)
        ag.finish()

    outs = pl.pallas_call(
        body, name="all_gather_weights", out_shape=_AllGather.out_shape(shards),
        in_specs=[ANY] * n, out_specs=[ANY] * n, scratch_shapes=_AllGather.scratch(n),
    )(*shards)
    return _fill_own_slot(outs, shards)


class _ExchangeHalves:
    def __init__(self, ins, recvs, send_sems, recv_sems):
        self.ins, self.recvs, self.send_sems, self.recv_sems = ins, recvs, send_sems, recv_sems

    @staticmethod
    def scratch(n):
        return [pltpu.SemaphoreType.DMA((n,)), pltpu.SemaphoreType.DMA((n,))]

    @staticmethod
    def out_shape(grads):
        return [jax.ShapeDtypeStruct((g.shape[0], g.shape[1] // 2, g.shape[2]), g.dtype) for g in grads]

    def _copies(self):
        x, y, c, _ = _mesh_pos()
        out = []
        for a, (src, dst) in enumerate(zip(self.ins, self.recvs)):
            hr = src.shape[1] // 2
            out.append(pltpu.make_async_remote_copy(
                src_ref=src.at[:, pl.ds((1 - c) * hr, hr), :], dst_ref=dst, send_sem=self.send_sems.at[a],
                recv_sem=self.recv_sems.at[a], device_id=(x, y, 1 - c), device_id_type=MESH))
        return out

    def start(self):
        for cp in self._copies():
            cp.start()

    def finish(self):
        for cp in self._copies():
            cp.wait()


def _exchange_halves(grads):
    n = len(grads)

    def body(*refs):
        ex = _ExchangeHalves(refs[:n], refs[n:2 * n], *refs[2 * n:])
        ex.start()
        ex.finish()

    return pl.pallas_call(
        body, name="grad_exchange_halves", out_shape=_ExchangeHalves.out_shape(grads),
        in_specs=[ANY] * n, out_specs=[ANY] * n, scratch_shapes=_ExchangeHalves.scratch(n),
    )(*grads)


class _ScatterToChips:
    def __init__(self, ins, rbufs, send_sems, recv_sems):
        self.ins, self.rbufs, self.send_sems, self.recv_sems = ins, rbufs, send_sems, recv_sems

    @staticmethod
    def scratch(n):
        return [pltpu.SemaphoreType.DMA((3 * n,)), pltpu.SemaphoreType.DMA((3 * n,))]

    @staticmethod
    def out_shape(sums):
        return [jax.ShapeDtypeStruct((3,) + s.shape[1:], BF16) for s in sums]

    def _copies(self):
        x, y, c, chips = _mesh_pos()
        out = []
        for a, (src, dst) in enumerate(zip(self.ins, self.rbufs)):
            for k, chip in enumerate(chips):
                out.append(pltpu.make_async_remote_copy(
                    src_ref=src.at[2 * chip[0] + chip[1]], dst_ref=dst.at[k], send_sem=self.send_sems.at[3 * a + k],
                    recv_sem=self.recv_sems.at[3 * a + k], device_id=(*chip, c), device_id_type=MESH))
        return out

    def start(self):
        for cp in self._copies():
            cp.start()

    def finish(self):
        for cp in self._copies():
            cp.wait()


def _gather_halves(halves):
    n = len(halves)

    def body(*refs):
        ins, outs = refs[:n], refs[n:2 * n]
        send_sems, recv_sems = refs[2 * n:]
        x, y, c, _ = _mesh_pos()
        sib = (x, y, 1 - c)
        remote = [pltpu.make_async_remote_copy(src_ref=ins[a].at[c], dst_ref=outs[a].at[c],
                                               send_sem=send_sems.at[a], recv_sem=recv_sems.at[a],
                                               device_id=sib, device_id_type=MESH) for a in range(n)]
        for cp in remote:
            cp.start()
        for a in range(n):
            pltpu.make_async_remote_copy(src_ref=ins[a].at[1 - c], dst_ref=outs[a].at[1 - c], send_sem=send_sems.at[a],
                                         recv_sem=recv_sems.at[a], device_id=sib, device_id_type=MESH).wait_recv()
        for cp in remote:
            cp.wait_send()

    return pl.pallas_call(
        body, name="grad_gather_halves",
        out_shape=[jax.ShapeDtypeStruct(h.shape, F32) for h in halves],
        in_specs=[ANY] * n, out_specs=[ANY] * n, input_output_aliases={a: a for a in range(n)},
        scratch_shapes=[pltpu.SemaphoreType.DMA((n,)), pltpu.SemaphoreType.DMA((n,))],
    )(*halves)


SMALL_A_ROWS = 24
SMALL_B_ROWS = 8
SMALL_C_ROWS = N_POOL_GROUPS * POOL_GROUP


class _AllReduceSmall:
    N_IN = 10
    SHAPES = [(SMALL_A_ROWS, D_MODEL), (SMALL_B_ROWS, D_CONV), (SMALL_C_ROWS, POOL_GROUP)]

    def __init__(self, ins, outs, scratch):
        self.ins, self.outs = ins, outs
        self.bufs, self.rcvs, self.send_sems, self.recv_sems = scratch[:3], scratch[3:6], scratch[6], scratch[7]

    @classmethod
    def scratch(cls):
        return ([pltpu.VMEM((3,) + s, F32) for s in cls.SHAPES] + [pltpu.VMEM((3,) + s, F32) for s in cls.SHAPES]
                + [pltpu.SemaphoreType.DMA((9,)), pltpu.SemaphoreType.DMA((9,))])

    @classmethod
    def out_shape(cls):
        return [jax.ShapeDtypeStruct(s, F32) for s in cls.SHAPES]

    def _copies(self, st):
        x, y, c, _ = _mesh_pos()
        peer = [(x, y, 1 - c), (1 - x, y, c), (x, 1 - y, c)][st]
        return [pltpu.make_async_remote_copy(
            src_ref=buf.at[st], dst_ref=rcv.at[st], send_sem=self.send_sems.at[3 * st + i],
            recv_sem=self.recv_sems.at[3 * st + i], device_id=peer, device_id_type=MESH)
            for i, (buf, rcv) in enumerate(zip(self.bufs, self.rcvs))]

    def pack_and_send(self):
        dg1_ref, dg1m_ref, dg2_ref, dg3_ref, dg4_ref, loss_ref, dmeta_ref, dsc_ref, dcw_ref, dpw_ref = self.ins
        a_buf, b_buf, c_buf = self.bufs

        def rowsum(v):
            return jnp.sum(v, axis=0, keepdims=True)

        a_buf[0, 0:1, :] = rowsum(dg1_ref[...] + dg1m_ref[...])
        a_buf[0, 1:2, :] = rowsum(dg2_ref[...])
        a_buf[0, 2:3, :] = rowsum(dg3_ref[...])
        a_buf[0, 3:4, :] = rowsum(dg4_ref[...])
        loss = jnp.sum(rowsum(loss_ref[...]), axis=1, keepdims=True) * (0.5 / D_MODEL)
        a_buf[0, 4:5, :] = jnp.broadcast_to(loss, (1, D_MODEL))
        a_buf[0, 5:8, :] = jnp.zeros((3, D_MODEL), F32)
        a_buf[0, 8:24, :] = dmeta_ref[...]
        b_buf[0, 0:1, :] = rowsum(dsc_ref[...])
        for k in range(3):
            b_buf[0, 1 + k:2 + k, :] = rowsum(dcw_ref[8 * k:8 * k + 8, :])
        b_buf[0, 4:8, :] = jnp.zeros((4, D_CONV), F32)
        c_buf[0] = dpw_ref[...]
        for cp in self._copies(0):
            cp.start()

    def combine(self, st):
        for cp in self._copies(st):
            cp.wait()
        if st < 2:
            for buf, rcv in zip(self.bufs, self.rcvs):
                buf[st + 1] = buf[st] + rcv[st]
            for cp in self._copies(st + 1):
                cp.start()
        else:
            for out, buf, rcv in zip(self.outs, self.bufs, self.rcvs):
                out[...] = buf[st] + rcv[st]


def _row_block(rows):
    for cand in (512, 448, 384, 352, 320, 256, 128, 64, 32, 16):
        if rows % cand == 0:
            return cand
    return rows


def _add_pairs(grad, recv, place):
    n_sh, rows2, cols = grad.shape
    hr = rows2 // 2
    br = _row_block(hr)

    def body(place_ref, a_ref, b_ref, o_ref):
        o_ref[...] = (a_ref[0] + b_ref[...]).astype(BF16)

    return pl.pallas_call(
        body, name="grad_add_pairs",
        grid_spec=pltpu.PrefetchScalarGridSpec(
            num_scalar_prefetch=1, grid=(n_sh, hr // br),
            in_specs=[pl.BlockSpec((1, 1, br, cols), lambda j, i, p: (j, p[1], i, 0)),
                      pl.BlockSpec((1, br, cols), lambda j, i, p: (j, i, 0))],
            out_specs=pl.BlockSpec((1, br, cols), lambda j, i, p: (j, i, 0))),
        out_shape=jax.ShapeDtypeStruct((n_sh, hr, cols), BF16), compiler_params=_cparams(2),
    )(place, grad.reshape(n_sh, 2, hr, cols), recv)


def _add_chips(grads, recvs, rbufs, place, scattered=(), name="grad_add_chips"):
    n, n_sc = len(grads), len(scattered)
    n_sh, rows2, cols = grads[0].shape
    hr = rows2 // 2
    br = _row_block(hr)
    n_steps = hr // br

    def body(place_ref, *refs):
        a_refs, b_refs, r_refs = refs[:n], refs[n:2 * n], refs[2 * n:3 * n]
        o_refs = refs[3 * n + n_sc:4 * n + n_sc]
        if n_sc:
            scatter = _ScatterToChips(refs[3 * n:3 * n + n_sc], refs[4 * n + n_sc:4 * n + 2 * n_sc], *refs[-2:])

            @pl.when(pl.program_id(0) == 0)
            def _():
                scatter.start()

        for a_ref, b_ref, r_ref, o_ref in zip(a_refs, b_refs, r_refs, o_refs):
            own = a_ref[0, 0] + b_ref[0]
            o_ref[0] = ((own + r_ref[0].astype(F32)) + r_ref[1].astype(F32)) + r_ref[2].astype(F32)

        if n_sc:
            @pl.when(pl.program_id(0) == n_steps - 1)
            def _():
                scatter.finish()

    outs = pl.pallas_call(
        body, name=name,
        grid_spec=pltpu.PrefetchScalarGridSpec(
            num_scalar_prefetch=1, grid=(n_steps,),
            in_specs=[pl.BlockSpec((1, 1, br, cols), lambda i, p: (p[0], p[1], i, 0))] * n
            + [pl.BlockSpec((1, br, cols), lambda i, p: (p[0], i, 0))] * n
            + [pl.BlockSpec((3, br, cols), lambda i, p: (0, i, 0))] * n + [ANY] * n_sc,
            out_specs=[pl.BlockSpec((1, br, cols), lambda i, p: (p[1], i, 0))] * n + [ANY] * n_sc,
            scratch_shapes=_ScatterToChips.scratch(n_sc) if n_sc else []),
        out_shape=[jax.ShapeDtypeStruct((2, hr, cols), F32)] * n + _ScatterToChips.out_shape(list(scattered)),
        compiler_params=_cparams(1),
    )(place, *[g.reshape(n_sh, 2, hr, cols) for g in grads], *recvs, *rbufs, *scattered)
    return outs[:n], outs[n:]


def _adamw_math(w, g, m, v):
    m2 = ADAM_B1 * m + (1.0 - ADAM_B1) * g
    v2 = ADAM_B2 * v + (1.0 - ADAM_B2) * (g * g)
    m_hat = m2 / (1.0 - ADAM_B1 ** ADAM_STEP)
    v_hat = v2 / (1.0 - ADAM_B2 ** ADAM_STEP)
    delta = -ADAM_LR * (m_hat / (jnp.sqrt(v_hat) + ADAM_EPS) + ADAM_WD * w)
    return delta, m2, v2


def _adamw_big(w, g, m, v):
    rows, cols = w.shape
    br = _row_block(rows)

    def body(w_ref, g_ref, m_ref, v_ref, d_ref, m2_ref, v2_ref):
        d, m2, v2 = _adamw_math(w_ref[...], g_ref[...], m_ref[...], v_ref[...])
        d_ref[...] = d
        m2_ref[...] = m2
        v2_ref[...] = v2

    spec = pl.BlockSpec((br, cols), lambda i: (i, 0))
    return pl.pallas_call(
        body, name="adamw_big", grid=(rows // br,),
        out_shape=[jax.ShapeDtypeStruct((rows, cols), F32)] * 3,
        in_specs=[spec] * 4, out_specs=[spec] * 3, compiler_params=_cparams(1),
    )(w, g, m, v)


def _adamw_small(groups):
    n = len(groups)

    def body(*refs):
        ins, outs = refs[:4 * n], refs[4 * n:]
        for i in range(n):
            w, g, m, v = (r[...] for r in ins[4 * i:4 * i + 4])
            d, m2, v2 = _adamw_math(w, g, m, v)
            outs[3 * i][...] = d
            outs[3 * i + 1][...] = m2
            outs[3 * i + 2][...] = v2

    vm = pl.BlockSpec(memory_space=pltpu.VMEM)
    flat = [a for grp in groups for a in grp]
    out_shape = [jax.ShapeDtypeStruct(grp[0].shape, F32) for grp in groups for _ in range(3)]
    outs = pl.pallas_call(body, name="adamw_small", out_shape=out_shape,
                          in_specs=[vm] * (4 * n), out_specs=[vm] * (3 * n))(*flat)
    return [tuple(outs[3 * i:3 * i + 3]) for i in range(n)]


def _load_weights(pairs, sem):
    for src, dst in pairs:
        cp = pltpu.make_async_copy(src, dst, sem)
        cp.start()
        cp.wait()


def _meta_fwd(meta_full, g1, win_all):
    def body(meta_ref, g1_ref, win_ref, z_ref):
        xm = meta_ref[...]
        a = (xm * _rstd(xm) * g1_ref[...]).astype(BF16)
        for j in range(N_CHIPS):
            z_ref[:, j * IN_SHARD:(j + 1) * IN_SHARD] = _dot(a, win_ref[j])

    vm = pl.BlockSpec(memory_space=pltpu.VMEM)
    return pl.pallas_call(body, name="meta_fwd", out_shape=jax.ShapeDtypeStruct((N_META, D_IN_PROJ), F32),
                          in_specs=[vm] * 3, out_specs=vm)(meta_full, g1, win_all)


def _mixer_fwd(x3, zmeta, g1, g2, convw, poolw, pscale, win_all, wout, ffn_shards):
    n_seq, seq, _ = x3.shape
    tm = min(TM_MIX_FWD, seq)
    n_t = seq // tm
    n_steps = n_seq * n_t
    n_ag = len(ffn_shards)

    def body(x_ref, zm_ref, g1_ref, g2_ref, cw_ref, pw_ref, ps_ref, win_hbm, wout_hbm, *rest):
        ag = _AllGather(rest[:n_ag], rest[n_ag + 6:2 * n_ag + 6], *rest[-2:])
        z_ref, h1_ref, a_ref, conv_ref, pooled_ref, yc_ref = rest[n_ag:n_ag + 6]
        win_v, wout_v, cvb, pb, sem = rest[2 * n_ag + 6:-2]
        s, t = pl.program_id(0), pl.program_id(1)
        step = s * n_t + t

        @pl.when(step == 0)
        def _():
            ag.start()
            _load_weights([(win_hbm, win_v), (wout_hbm, wout_v)], sem)

        for a in range(n_ag):
            @pl.when(step == ((a + 1) * n_steps) // (2 * n_ag + 2))
            def _():
                ag.relay(a)

        for a in range(n_ag):
            @pl.when(step == min(n_steps // 2 + ((a + 1) * n_steps) // (2 * n_ag + 2), n_steps - 1))
            def _():
                ag.forward(a)

        @pl.when(t == 0)
        def _():
            cvb[0:HALO, :] = zm_ref[:, IN_SHARD:2 * IN_SHARD] * zm_ref[:, 2 * IN_SHARD:3 * IN_SHARD]
            pb[0:HALO, :] = zm_ref[:, 3 * IN_SHARD:4 * IN_SHARD]

        @pl.when(t > 0)
        def _():
            cvb[0:HALO, :] = cvb[tm:tm + HALO, :]
            pb[0:HALO, :] = pb[tm:tm + HALO, :]

        xt = x_ref[0]
        a = (xt * _rstd(xt) * g1_ref[...]).astype(BF16)
        a_ref[...] = a
        zb = _dot(a, win_v[0])
        zc = _dot(a, win_v[1])
        zv = _dot(a, win_v[2])
        zp = _dot(a, win_v[3])
        z_ref[0, :, 0:IN_SHARD] = zb
        z_ref[0, :, IN_SHARD:2 * IN_SHARD] = zc
        z_ref[0, :, 2 * IN_SHARD:3 * IN_SHARD] = zv
        cv = zc * zv
        cvb[HALO:HALO + tm, :] = cv
        pb[HALO:HALO + tm, :] = zp
        cw = cw_ref[...]
        conv = cw[0:1] * cvb[HALO - 2:HALO - 2 + tm, :] + cw[1:2] * cvb[HALO - 1:HALO - 1 + tm, :] + cw[2:3] * cv
        conv_ref[...] = conv
        parts = [(zb * conv).astype(BF16)]
        for g in range(N_POOL_GROUPS):
            pooled = _pool_fwd(pb, g, tm).astype(BF16)
            pooled_ref[:, _gcols(g)] = pooled
            parts.append((_dot(pooled, pw_ref[g]) * ps_ref[:, _gcols(g)]).astype(BF16))
        ycat = jnp.concatenate(parts, axis=1)
        yc_ref[...] = ycat
        m = _dot(ycat, wout_v[...])
        h1_ref[0] = xt + m * _rstd(m) * g2_ref[...]

        @pl.when(step == n_steps - 1)
        def _():
            ag.finish()

    n_rows = n_seq * seq
    row = lambda c: pl.BlockSpec((1, tm, c), lambda s, t: (s, t, 0))
    row2 = lambda c: pl.BlockSpec((tm, c), lambda s, t: (s * n_t + t, 0))
    outs = pl.pallas_call(
        body, name="mixer_fwd", grid=(n_seq, n_t),
        out_shape=[jax.ShapeDtypeStruct((n_seq, seq, D_Z), F32),
                   jax.ShapeDtypeStruct((n_seq, seq, D_MODEL), F32), jax.ShapeDtypeStruct((n_rows, D_MODEL), BF16),
                   jax.ShapeDtypeStruct((n_rows, D_CONV), F32), jax.ShapeDtypeStruct((n_rows, D_POOL), BF16),
                   jax.ShapeDtypeStruct((n_rows, D_MODEL), BF16)] + _AllGather.out_shape(ffn_shards),
        in_specs=[row(D_MODEL), _full((N_META, D_IN_PROJ)), _full((1, D_MODEL)), _full((1, D_MODEL)),
                  _full((3, D_CONV)), _full((N_POOL_GROUPS, POOL_GROUP, POOL_GROUP)), _full((1, D_POOL)), ANY, ANY]
        + [ANY] * n_ag,
        out_specs=[row(D_Z), row(D_MODEL), row2(D_MODEL), row2(D_CONV), row2(D_POOL), row2(D_MODEL)] + [ANY] * n_ag,
        scratch_shapes=[pltpu.VMEM((N_CHIPS, D_MODEL, IN_SHARD), BF16), pltpu.VMEM((D_MODEL, D_MODEL), BF16),
                        pltpu.VMEM((HALO + tm, D_CONV), F32), pltpu.VMEM((HALO + tm, D_POOL), F32),
                        pltpu.SemaphoreType.DMA] + _AllGather.scratch(n_ag),
        compiler_params=_cparams(2),
    )(x3, zmeta, g1, g2, convw, poolw, pscale, win_all, wout, *ffn_shards)
    return outs[:6], _fill_own_slot(outs[6:], ffn_shards)


def _ffn_chunks():
    out, r0 = [], 0
    while r0 < D_FF:
        out.append((r0, min(FF_CHUNK, D_FF - r0)))
        r0 += FF_CHUNK
    return out


def _ffn_fwd_bwd(h1, target, g3, g4, wg_t, wu_t, wd):
    n_rows = h1.shape[0]
    tm = min(TM_FFN, n_rows)
    chunks = _ffn_chunks()

    def body(h1_ref, t_ref, g3_ref, g4_ref, wg_hbm, wu_hbm, wd_hbm,
             dh1_ref, f_ref, dd_ref, ds_ref, du_ref, gg_ref, loss_ref, dg3_ref, dg4_ref,
             wg_v, wu_v, wd_v, s_sc, u_sc, sem):
        @pl.when(pl.program_id(0) == 0)
        def _():
            _load_weights([(wg_hbm, wg_v), (wu_hbm, wu_v), (wd_hbm, wd_v)], sem)
            loss_ref[...] = jnp.zeros_like(loss_ref)
            dg3_ref[...] = jnp.zeros_like(dg3_ref)
            dg4_ref[...] = jnp.zeros_like(dg4_ref)

        h1v = h1_ref[...]
        r3 = _rstd(h1v)
        hh = h1v * r3
        g3v, g4v = g3_ref[...], g4_ref[...]
        f = (hh * g3v).astype(BF16)
        f_ref[...] = f
        d = jnp.zeros((tm, D_MODEL), F32)
        for r0, sz in chunks:
            s = _dot_nt(f, wg_v[r0:r0 + sz, :])
            u = _dot_nt(f, wu_v[r0:r0 + sz, :])
            s_sc[:, r0:r0 + sz] = s
            u_sc[:, r0:r0 + sz] = u
            gc = (s * _sigmoid(s) * u).astype(BF16)
            gg_ref[:, r0:r0 + sz] = gc
            d = d + _dot(gc, wd_v[r0:r0 + sz, :])
        r4 = _rstd(d)
        dh = d * r4
        err = (h1v + dh * g4v) - t_ref[...]
        loss_ref[...] += _rows8(err * err)
        dy = err * (1.0 / D_MODEL)
        dg4_ref[...] += _rows8(dy * dh)
        ddb = _rms_bwd(dy, dh, r4, g4v).astype(BF16)
        dd_ref[...] = ddb
        df = jnp.zeros((tm, D_MODEL), F32)
        for r0, sz in chunks:
            dgg = _dot_nt(ddb, wd_v[r0:r0 + sz, :])
            s = s_sc[:, r0:r0 + sz]
            u = u_sc[:, r0:r0 + sz]
            sig = _sigmoid(s)
            dsc = (dgg * u * (sig * (1.0 + s * (1.0 - sig)))).astype(BF16)
            duc = (dgg * (s * sig)).astype(BF16)
            ds_ref[:, r0:r0 + sz] = dsc
            du_ref[:, r0:r0 + sz] = duc
            df = df + _dot(dsc, wg_v[r0:r0 + sz, :]) + _dot(duc, wu_v[r0:r0 + sz, :])
        dg3_ref[...] += _rows8(df * hh)
        dh1_ref[...] = dy + _rms_bwd(df, hh, r3, g3v)

    row = pl.BlockSpec((tm, D_MODEL), lambda i: (i, 0))
    ffrow = pl.BlockSpec((tm, D_FF), lambda i: (i, 0))
    acc = _full((8, D_MODEL))
    act_bf = jax.ShapeDtypeStruct((n_rows, D_MODEL), BF16)
    ff_bf = jax.ShapeDtypeStruct((n_rows, D_FF), BF16)
    acc_shape = jax.ShapeDtypeStruct((8, D_MODEL), F32)
    w_vmem = pltpu.VMEM((D_FF, D_MODEL), BF16)
    return pl.pallas_call(
        body, name="ffn_fwd_bwd", grid=(n_rows // tm,),
        out_shape=[jax.ShapeDtypeStruct((n_rows, D_MODEL), F32), act_bf, act_bf, ff_bf, ff_bf, ff_bf,
                   acc_shape, acc_shape, acc_shape],
        in_specs=[row, row, _full((1, D_MODEL)), _full((1, D_MODEL)), ANY, ANY, ANY],
        out_specs=[row, row, row, ffrow, ffrow, ffrow, acc, acc, acc],
        scratch_shapes=[w_vmem, w_vmem, w_vmem, pltpu.VMEM((tm, D_FF), F32), pltpu.VMEM((tm, D_FF), F32),
                        pltpu.SemaphoreType.DMA],
        compiler_params=_cparams(1),
    )(h1, target, g3, g4, wg_t, wu_t, wd)


def _ffn_weight_grads(name, acts, other, exchanged):
    n_rows = other.shape[0]
    n_a, n_ex = len(acts), len(exchanged)
    n_c = n_a
    tk = min(TK_DW, n_rows)
    n_k = n_rows // tk
    half = D_FF // n_c

    def body(other_ref, *rest):
        act_refs = rest[:n_a]
        out_refs = rest[n_a + n_ex:2 * n_a + n_ex]
        c, k = pl.program_id(0), pl.program_id(1)
        if n_ex:
            ex = _ExchangeHalves(rest[n_a:n_a + n_ex], rest[2 * n_a + n_ex:2 * n_a + 2 * n_ex], *rest[-2:])

            @pl.when((c == 0) & (k == 0))
            def _():
                ex.start()

        @pl.when(k == 0)
        def _():
            for o in out_refs:
                o[...] = jnp.zeros_like(o)

        ov = other_ref[...]
        for a, o in zip(act_refs, out_refs):
            o[...] += _dot_tn(a[...], ov)

        if n_ex:
            @pl.when((c == n_c - 1) & (k == n_k - 1))
            def _():
                ex.finish()

    row = pl.BlockSpec((tk, D_MODEL), lambda c, k: (k, 0))
    ffrow = pl.BlockSpec((tk, half), lambda c, k: (k, c))
    out = pl.BlockSpec((half, D_MODEL), lambda c, k: (c, 0))
    outs = pl.pallas_call(
        body, name=name, grid=(n_c, n_k),
        out_shape=[jax.ShapeDtypeStruct((D_FF, D_MODEL), F32)] * n_a + _ExchangeHalves.out_shape(exchanged),
        in_specs=[row] + [ffrow] * n_a + [ANY] * n_ex, out_specs=[out] * n_a + [ANY] * n_ex,
        scratch_shapes=_ExchangeHalves.scratch(n_ex) if n_ex else [],
        compiler_params=_cparams(2),
    )(other, *acts, *exchanged)
    return outs[:n_a], outs[n_a:]


def _mixer_bwd(dh1, ycat2, z3, conv2, pooled2, x3, zmeta, g1, g2, convw, poolw, pscale, win_all, wout, exchanged,
               scattered):
    n_seq, seq, _ = x3.shape
    tm = min(TM_MIX_BWD, seq)
    sub = min(SUB_MIX_BWD, tm)
    n_t = seq // tm
    n_ex, n_sc = len(exchanged), len(scattered)
    n_cm = n_ex + n_sc

    def body(dh1_ref, yc_ref, z_ref, conv_ref, pooled_ref, x_ref, zm_ref, g1_ref, g2_ref, cw_ref, pw_ref, ps_ref,
             win_hbm, wout_hbm, *rest):
        outs0 = n_cm + 9
        ex = _ExchangeHalves(rest[:n_ex], rest[outs0:outs0 + n_ex], *rest[-4:-2])
        sc = _ScatterToChips(rest[n_ex:n_cm], rest[outs0 + n_ex:outs0 + n_cm], *rest[-2:])
        dx_ref, dz_ref, dm_ref, dg1_ref, dg2_ref, dsc_ref, dcw_ref, dpw_ref, dzm_ref = rest[n_cm:outs0]
        win_v, wout_v, dcb, dqb, mcb, mqb, sem = rest[outs0 + n_cm:-4]
        s, i = pl.program_id(0), pl.program_id(1)
        tr = n_t - 1 - i

        @pl.when((s == 0) & (i == 0))
        def _():
            sc.start()
            ex.start()
            _load_weights([(win_hbm, win_v), (wout_hbm, wout_v)], sem)
            for ref in (dg1_ref, dg2_ref, dsc_ref, dcw_ref, dpw_ref, dzm_ref):
                ref[...] = jnp.zeros_like(ref)

        @pl.when(i == 0)
        def _():
            dcb[tm:tm + HALO, :] = jnp.zeros((HALO, D_CONV), F32)
            dqb[tm:tm + HALO, :] = jnp.zeros((HALO, D_POOL), F32)

        @pl.when(i > 0)
        def _():
            dcb[tm:tm + HALO, :] = dcb[0:HALO, :]
            dqb[tm:tm + HALO, :] = dqb[0:HALO, :]

        g1v, g2v = g1_ref[...], g2_ref[...]
        cw = cw_ref[...]

        for r0 in range(tm - sub, -1, -sub):
            rows = slice(r0, r0 + sub)
            dh1v = dh1_ref[0, rows, :]
            mv = _dot(yc_ref[rows, :], wout_v[...])
            r2 = _rstd(mv)
            mh = mv * r2
            dg2_ref[...] += _rows8(dh1v * mh)
            dmb = _rms_bwd(dh1v, mh, r2, g2v).astype(BF16)
            dm_ref[rows, :] = dmb
            dyc = _dot_nt(dmb, wout_v[...])
            dyconv = dyc[:, 0:D_CONV]

            for g in range(N_POOL_GROUPS):
                pooled = pooled_ref[rows, _gcols(g)]
                mixed = _dot(pooled, pw_ref[g])
                scale = ps_ref[:, _gcols(g)]
                dyp = dyc[:, D_CONV + g * POOL_GROUP:D_CONV + (g + 1) * POOL_GROUP]
                dsc_ref[:, _gcols(g)] += _rows8(dyp * mixed)
                dmix = (dyp * scale).astype(BF16)
                dpw_ref[g] += _dot_tn(pooled, dmix)
                dqb[rows, _gcols(g)] = _dot_nt(dmix, pw_ref[g])

            zb = z_ref[0, rows, 0:IN_SHARD]
            zc = z_ref[0, rows, IN_SHARD:2 * IN_SHARD]
            zv = z_ref[0, rows, 2 * IN_SHARD:3 * IN_SHARD]
            dconv = dyconv * zb
            dcb[rows, :] = dconv
            d1 = dcb[r0 + 1:r0 + 1 + sub, :]
            d2 = dcb[r0 + 2:r0 + 2 + sub, :]
            dcv = cw[2:3] * dconv + cw[1:2] * d1 + cw[0:1] * d2
            cv = zc * zv
            dcw_ref[0:8, :] += _rows8(cv * d2)
            dcw_ref[8:16, :] += _rows8(cv * d1)
            dcw_ref[16:24, :] += _rows8(cv * dconv)
            dzs = [(dyconv * conv_ref[rows, :]).astype(BF16), (dcv * zv).astype(BF16), (dcv * zc).astype(BF16),
                   jnp.concatenate([_pool_bwd(dqb, g, r0, sub) for g in range(N_POOL_GROUPS)], axis=1).astype(BF16)]
            da = jnp.zeros((sub, D_MODEL), F32)
            for j in range(N_CHIPS):
                dz_ref[j, rows, :] = dzs[j]
                da = da + _dot_nt(dzs[j], win_v[j])
            xt = x_ref[0, rows, :]
            r1 = _rstd(xt)
            xh = xt * r1
            dg1_ref[...] += _rows8(da * xh)
            dx_ref[0, rows, :] = dh1v + _rms_bwd(da, xh, r1, g1v)

        @pl.when(tr == 0)
        def _():
            mcb[0:HALO, :] = jnp.zeros((HALO, D_CONV), F32)
            mqb[0:HALO, :] = jnp.zeros((HALO, D_POOL), F32)
            mcb[HALO:2 * HALO, :] = dcb[0:HALO, :]
            mqb[HALO:2 * HALO, :] = dqb[0:HALO, :]
            m1 = mcb[1:1 + HALO, :]
            m2 = mcb[2:2 + HALO, :]
            zc_m = zm_ref[:, IN_SHARD:2 * IN_SHARD]
            zv_m = zm_ref[:, 2 * IN_SHARD:3 * IN_SHARD]
            cv_m = zc_m * zv_m
            dcw_ref[0:8, :] += _rows8(cv_m * m2)
            dcw_ref[8:16, :] += _rows8(cv_m * m1)
            dcv_m = cw[1:2] * m1 + cw[0:1] * m2
            dzm_ref[:, IN_SHARD:2 * IN_SHARD] += dcv_m * zv_m
            dzm_ref[:, 2 * IN_SHARD:3 * IN_SHARD] += dcv_m * zc_m
            dzm_ref[:, 3 * IN_SHARD:4 * IN_SHARD] += jnp.concatenate(
                [_pool_bwd(mqb, g, 0, HALO) for g in range(N_POOL_GROUPS)], axis=1)

        @pl.when((s == n_seq - 1) & (i == n_t - 1))
        def _():
            ex.finish()
            sc.finish()

    row3 = lambda c: pl.BlockSpec((1, tm, c), lambda s, i: (s, n_t - 1 - i, 0))
    row2 = lambda c: pl.BlockSpec((tm, c), lambda s, i: (s * n_t + n_t - 1 - i, 0))
    n_rows = n_seq * seq
    outs = pl.pallas_call(
        body, name="mixer_bwd", grid=(n_seq, n_t),
        out_shape=[jax.ShapeDtypeStruct((n_seq, seq, D_MODEL), F32),
                   jax.ShapeDtypeStruct((N_CHIPS, n_rows, IN_SHARD), BF16), jax.ShapeDtypeStruct((n_rows, D_MODEL), BF16),
                   jax.ShapeDtypeStruct((8, D_MODEL), F32), jax.ShapeDtypeStruct((8, D_MODEL), F32),
                   jax.ShapeDtypeStruct((8, D_POOL), F32), jax.ShapeDtypeStruct((24, D_CONV), F32),
                   jax.ShapeDtypeStruct((N_POOL_GROUPS, POOL_GROUP, POOL_GROUP), F32),
                   jax.ShapeDtypeStruct((N_META, D_IN_PROJ), F32)]
        + _ExchangeHalves.out_shape(exchanged) + _ScatterToChips.out_shape(scattered),
        in_specs=[row3(D_MODEL), row2(D_MODEL), row3(D_Z), row2(D_CONV), row2(D_POOL), row3(D_MODEL),
                  _full((N_META, D_IN_PROJ)), _full((1, D_MODEL)), _full((1, D_MODEL)), _full((3, D_CONV)),
                  _full((N_POOL_GROUPS, POOL_GROUP, POOL_GROUP)), _full((1, D_POOL)), ANY, ANY] + [ANY] * n_cm,
        out_specs=[row3(D_MODEL), pl.BlockSpec((N_CHIPS, tm, IN_SHARD), lambda s, i: (0, s * n_t + n_t - 1 - i, 0)),
                   row2(D_MODEL),
                   _full((8, D_MODEL)), _full((8, D_MODEL)), _full((8, D_POOL)), _full((24, D_CONV)),
                   _full((N_POOL_GROUPS, POOL_GROUP, POOL_GROUP)), _full((N_META, D_IN_PROJ))] + [ANY] * n_cm,
        scratch_shapes=[pltpu.VMEM((N_CHIPS, D_MODEL, IN_SHARD), BF16), pltpu.VMEM((D_MODEL, D_MODEL), BF16),
                        pltpu.VMEM((tm + HALO, D_CONV), F32), pltpu.VMEM((tm + HALO, D_POOL), F32),
                        pltpu.VMEM((2 * HALO, D_CONV), F32), pltpu.VMEM((2 * HALO, D_POOL), F32),
                        pltpu.SemaphoreType.DMA] + _ExchangeHalves.scratch(n_ex) + _ScatterToChips.scratch(n_sc),
        compiler_params=_cparams(2),
    )(dh1, ycat2, z3, conv2, pooled2, x3, zmeta, g1, g2, convw, poolw, pscale, win_all, wout, *exchanged, *scattered)
    return outs[:9], outs[9:9 + n_ex], outs[9 + n_ex:]


def _meta_bwd(dzm, meta_full, g1, win_all):
    def body(dzm_ref, meta_ref, g1_ref, win_ref, dmeta_ref, dg1_ref, a_ref, dzb_ref):
        xm = meta_ref[...]
        r = _rstd(xm)
        xh = xm * r
        g1v = g1_ref[...]
        a_ref[...] = (xh * g1v).astype(BF16)
        da = jnp.zeros((N_META, D_MODEL), F32)
        for j in range(N_CHIPS):
            dzj = dzm_ref[:, j * IN_SHARD:(j + 1) * IN_SHARD].astype(BF16)
            dzb_ref[j] = dzj
            da = da + _dot_nt(dzj, win_ref[j])
        dg1_ref[...] = _rows8(da * xh)
        dmeta_ref[...] = _rms_bwd(da, xh, r, g1v)

    vm = pl.BlockSpec(memory_space=pltpu.VMEM)
    return pl.pallas_call(
        body, name="meta_bwd",
        out_shape=[jax.ShapeDtypeStruct((N_META, D_MODEL), F32), jax.ShapeDtypeStruct((8, D_MODEL), F32),
                   jax.ShapeDtypeStruct((N_META, D_MODEL), BF16), jax.ShapeDtypeStruct((N_CHIPS, N_META, IN_SHARD), BF16)],
        in_specs=[vm] * 4, out_specs=[vm] * 4,
    )(dzm, meta_full, g1, win_all)


def _mixer_weight_grads(a, dz, ycat, dm, a_meta, dz_meta, ffn_sums, small):
    n_rows = a.shape[0]
    tk = min(TK_DW, n_rows)
    n_k = n_rows // tk
    n_sc, n_sm = len(ffn_sums), _AllReduceSmall.N_IN

    def body(a_ref, dz_ref, yc_ref, dm_ref, am_ref, dzm_ref, *rest):
        ins, outs, scratch = rest[:n_sc + n_sm], rest[n_sc + n_sm:2 * n_sc + n_sm + 5], rest[2 * n_sc + n_sm + 5:]
        dwin_ref, dwout_ref = outs[:2]
        scatter = _ScatterToChips(ins[:n_sc], outs[2:2 + n_sc], *scratch[:2])
        reduce_small = _AllReduceSmall(ins[n_sc:], outs[2 + n_sc:], scratch[2:])
        k = pl.program_id(0)

        @pl.when(k == 0)
        def _():
            scatter.start()
            reduce_small.pack_and_send()
            am_t = am_ref[...].T
            for j in range(N_CHIPS):
                dwin_ref[j] = _dot(am_t, dzm_ref[j])
            dwout_ref[...] = jnp.zeros_like(dwout_ref)

        for st in range(2):
            @pl.when(k == ((st + 1) * n_k) // 3)
            def _():
                reduce_small.combine(st)

        a_t = a_ref[...].T
        for j in range(N_CHIPS):
            dwin_ref[j] += _dot(a_t, dz_ref[j])
        dwout_ref[...] += _dot_tn(yc_ref[...], dm_ref[...])

        @pl.when(k == n_k - 1)
        def _():
            reduce_small.combine(2)
            scatter.finish()

    row = pl.BlockSpec((tk, D_MODEL), lambda k: (k, 0))
    outs = pl.pallas_call(
        body, name="mixer_weight_grads", grid=(n_k,),
        out_shape=[jax.ShapeDtypeStruct((N_CHIPS, D_MODEL, IN_SHARD), F32),
                   jax.ShapeDtypeStruct((D_MODEL, D_MODEL), F32)] + _ScatterToChips.out_shape(ffn_sums)
        + _AllReduceSmall.out_shape(),
        in_specs=[row, pl.BlockSpec((N_CHIPS, tk, IN_SHARD), lambda k: (0, k, 0)), row, row,
                  _full((N_META, D_MODEL)), _full((N_CHIPS, N_META, IN_SHARD))] + [ANY] * n_sc
        + [_full(s.shape) for s in small],
        out_specs=[_full((N_CHIPS, D_MODEL, IN_SHARD)), _full((D_MODEL, D_MODEL))] + [ANY] * n_sc
        + [_full(s) for s in _AllReduceSmall.SHAPES],
        scratch_shapes=_ScatterToChips.scratch(n_sc) + _AllReduceSmall.scratch(),
        compiler_params=_cparams(1),
    )(a, dz, ycat, dm, a_meta, dz_meta, *ffn_sums, *small)
    return ([outs[0], outs[1].reshape(N_CHIPS, OUT_SHARD, D_MODEL)], outs[2:2 + n_sc], outs[2 + n_sc:])


def kernel(x, meta_tokens, norm_mix_pre, w_in, conv_w, pool_w, pool_scale, w_out, norm_mix_post, norm_ffn_pre, w_gate, w_up, w_down, norm_ffn_post, loss_target, m_meta_tokens, m_norm_mix_pre, m_w_in, m_conv_w, m_pool_w, m_pool_scale, m_w_out, m_norm_mix_post, m_norm_ffn_pre, m_w_gate, m_w_up, m_w_down, m_norm_ffn_post, v_meta_tokens, v_norm_mix_pre, v_w_in, v_conv_w, v_pool_w, v_pool_scale, v_w_out, v_norm_mix_post, v_norm_ffn_pre, v_w_gate, v_w_up, v_w_down, v_norm_ffn_post):
    n_seq, seq, _ = x.shape
    n_rows = n_seq * seq
    chip = 2 * lax.axis_index("x") + lax.axis_index("y")
    meta_cols = D_MODEL // N_CHIPS
    conv_cols = D_CONV // N_CHIPS

    small = jnp.zeros((2 * HALO, meta_cols), F32)
    small = small.at[0:N_META, :].set(meta_tokens).at[N_META:N_META + 3, 0:conv_cols].set(conv_w[0])
    win_all, wout_all, small_all = _all_gather_shards([w_in[0].astype(BF16), w_out[0].astype(BF16), small])
    meta_full = small_all[:, 0:N_META, :].transpose(1, 0, 2).reshape(N_META, D_MODEL)
    conv_full = small_all[:, N_META:N_META + 3, 0:conv_cols].transpose(1, 0, 2).reshape(3, D_CONV)
    wout_full = wout_all.reshape(D_MODEL, D_MODEL)
    poolw_bf = pool_w[0].astype(BF16)
    pscale = pool_scale
    g1, g2, g3, g4 = norm_mix_pre, norm_mix_post, norm_ffn_pre, norm_ffn_post
    place = jnp.stack([chip, lax.axis_index("c")]).astype(jnp.int32)

    zmeta = _meta_fwd(meta_full, g1, win_all)
    (z3, h1, a_bf, conv2, pooled2, yc_bf), ffn_w = _mixer_fwd(
        x, zmeta, g1, g2, conv_full, poolw_bf, pscale, win_all, wout_full,
        [w_gate[0].T.astype(BF16), w_up[0].T.astype(BF16), w_down[0].astype(BF16)])
    wg_t, wu_t, wd_full = [w.reshape(D_FF, D_MODEL) for w in ffn_w]
    dh1, f_bf, dd_bf, ds_bf, du_bf, gg_bf, lossp, dg3p, dg4p = _ffn_fwd_bwd(
        h1.reshape(n_rows, D_MODEL), loss_target.reshape(n_rows, D_MODEL), g3, g4, wg_t, wu_t, wd_full)
    as_shards = lambda g: g.reshape(N_CHIPS, FF_SHARD, D_MODEL)
    (dwd,), _ = _ffn_weight_grads("ffn_weight_grads_down", [gg_bf], dd_bf, [])
    dwd = as_shards(dwd)
    (dwg_t,), (dwd_recv,) = _ffn_weight_grads("ffn_weight_grads_gate", [ds_bf], f_bf, [dwd])
    dwg_t = as_shards(dwg_t)
    (dwu_t,), (dwg_recv,) = _ffn_weight_grads("ffn_weight_grads_up", [du_bf], f_bf, [dwg_t])
    dwu_t = as_shards(dwu_t)
    ((grad_x, dz_bf, dm_bf, dg1p, dg2p, dscp, dcwp, dpw, dzm), (dwu_recv,), (dwd_rbuf, dwg_rbuf)) = _mixer_bwd(
        dh1.reshape(n_seq, seq, D_MODEL), yc_bf, z3, conv2, pooled2, x, zmeta, g1, g2, conv_full, poolw_bf, pscale,
        win_all, wout_full, [dwu_t], [_add_pairs(dwd, dwd_recv, place), _add_pairs(dwg_t, dwg_recv, place)])
    dmeta, dg1m, a_meta, dz_meta = _meta_bwd(dzm, meta_full, g1, win_all)
    mix_grads, (dwu_rbuf,), (a_red, b_red, c_red) = _mixer_weight_grads(
        a_bf, dz_bf, yc_bf, dm_bf, a_meta, dz_meta, [_add_pairs(dwu_t, dwu_recv, place)],
        [dg1p, dg1m, dg2p, dg3p, dg4p, lossp, dmeta, dscp, dcwp, dpw.reshape(SMALL_C_ROWS, POOL_GROUP)])

    mix_recvs = _exchange_halves(mix_grads)
    ffn_red, mix_rbufs = _add_chips([dwg_t, dwu_t, dwd], [dwg_recv, dwu_recv, dwd_recv],
                                    [dwg_rbuf, dwu_rbuf, dwd_rbuf], place,
                                    [_add_pairs(g, r, place) for g, r in zip(mix_grads, mix_recvs)],
                                    name="grad_add_chips_ffn")
    mix_red = [_add_chips([g], [r], [rb], place)[0][0] for g, r, rb in zip(mix_grads, mix_recvs, mix_rbufs)]
    reduced = _gather_halves(mix_red + list(ffn_red))
    g_win, g_wout, g_wg_t, g_wu_t, g_wd = [r.reshape(2 * r.shape[1], r.shape[2]) for r in reduced]

    loss = a_red[4, 0]
    g_g1, g_g2, g_g3, g_g4 = a_red[0:1], a_red[1:2], a_red[2:3], a_red[3:4]
    g_meta = lax.dynamic_slice(a_red, (8, chip * meta_cols), (N_META, meta_cols))
    g_pscale = b_red[0:1]
    g_conv = lax.dynamic_slice(b_red, (1, chip * conv_cols), (3, conv_cols))
    g_poolw = c_red

    big = [(w_in[0], g_win, m_w_in[0], v_w_in[0]), (w_out[0], g_wout, m_w_out[0], v_w_out[0]),
           (w_gate[0].T, g_wg_t, m_w_gate[0].T, v_w_gate[0].T), (w_up[0].T, g_wu_t, m_w_up[0].T, v_w_up[0].T),
           (w_down[0], g_wd, m_w_down[0], v_w_down[0])]
    big_out = [_adamw_big(w, g, m, v) for (w, g, m, v) in big]
    big_out[2] = [o.T for o in big_out[2]]
    big_out[3] = [o.T for o in big_out[3]]
    g_wg, g_wu = g_wg_t.T, g_wu_t.T
    small_groups = [
        (meta_tokens, g_meta, m_meta_tokens, v_meta_tokens),
        (g1, g_g1, m_norm_mix_pre, v_norm_mix_pre),
        (conv_w[0], g_conv, m_conv_w[0], v_conv_w[0]),
        (pool_w.reshape(SMALL_C_ROWS, POOL_GROUP), g_poolw, m_pool_w.reshape(SMALL_C_ROWS, POOL_GROUP),
         v_pool_w.reshape(SMALL_C_ROWS, POOL_GROUP)),
        (pool_scale, g_pscale, m_pool_scale, v_pool_scale),
        (g2, g_g2, m_norm_mix_post, v_norm_mix_post),
        (g3, g_g3, m_norm_ffn_pre, v_norm_ffn_pre),
        (g4, g_g4, m_norm_ffn_post, v_norm_ffn_post),
    ]
    small_out = _adamw_small(small_groups)

    grads_out = [g_meta, g_g1, g_win[None], g_conv[None], g_poolw.reshape(pool_w.shape), g_pscale, g_wout[None],
                 g_g2, g_g3, g_wg[None], g_wu[None], g_wd[None], g_g4]
    s_meta, s_g1, s_conv, s_poolw, s_pscale, s_g2, s_g3, s_g4 = small_out
    b_win, b_wout, b_wg, b_wu, b_wd = big_out

    def leaf(k):
        return [s_meta[k], s_g1[k], b_win[k][None], s_conv[k][None], s_poolw[k].reshape(pool_w.shape), s_pscale[k],
                b_wout[k][None], s_g2[k], s_g3[k], b_wg[k][None], b_wu[k][None], b_wd[k][None], s_g4[k]]

    return (loss, grad_x, *grads_out, *leaf(0), *leaf(1), *leaf(2))
```

```python
import jax
import jax.numpy as jnp
from jax import lax
from jax.experimental import pallas as pl
from jax.experimental.pallas import tpu as pltpu

F32 = jnp.float32
BF16 = jnp.bfloat16
MESH = pl.DeviceIdType.MESH

D_MODEL = 1024
D_CONV = 512
D_POOL = 512
POOL_GROUP = 128
N_POOL_GROUPS = 4
D_IN_PROJ = 2048
D_FF = 2816
N_CHIPS = 4
FF_SHARD = D_FF // N_CHIPS
IN_SHARD = D_IN_PROJ // N_CHIPS
OUT_SHARD = D_MODEL // N_CHIPS
D_Z = 3 * IN_SHARD
N_META = 16
HALO = 16
RMS_EPS = 1e-6

ADAM_LR = 0.001
ADAM_B1 = 0.9
ADAM_B2 = 0.999
ADAM_EPS = 1e-08
ADAM_WD = 0.01
ADAM_STEP = 10

TM_MIX_FWD = 512
TM_MIX_BWD = 512
SUB_MIX_BWD = 512
TM_FFN = 512
SUB_FFN = 256
TK_DW = 1024
FF_CHUNK = 1024
VMEM_LIMIT = 56 * 1024 * 1024
VMEM_LIMIT_FFN = 63 * 1024 * 1024


def _cparams(n_grid):
    return pltpu.CompilerParams(dimension_semantics=("arbitrary",) * n_grid, vmem_limit_bytes=VMEM_LIMIT)


def _dot(a, b):
    return jnp.dot(a, b, preferred_element_type=F32)


def _dot_nt(a, b):
    return lax.dot_general(a, b, (((1,), (1,)), ((), ())), preferred_element_type=F32)


def _dot_tn(a, b):
    return lax.dot_general(a, b, (((0,), (0,)), ((), ())), preferred_element_type=F32)


def _rows8(v):
    r, c = v.shape
    return v.reshape(r // 8, 8, c).sum(axis=0)


def _rstd(v):
    return lax.rsqrt(jnp.mean(v * v, axis=-1, keepdims=True) + RMS_EPS)


def _rms_bwd(dy, xhat, rstd, gain):
    dyg = dy * gain
    return rstd * (dyg - xhat * jnp.mean(dyg * xhat, axis=-1, keepdims=True))


def _sigmoid(v):
    return 1.0 / (1.0 + jnp.exp(-v))


def _gcols(g):
    return slice(g * POOL_GROUP, (g + 1) * POOL_GROUP)


def _window_sum(e, g, ahead):
    n = e.shape[0]
    w = e
    for level in range(g + 1):
        shift = 1 << level
        w = w + pltpu.roll(w, (n - shift) if ahead else shift, 0)
    return w


def _pool_fwd(pb, g, n):
    e = pb[0:HALO + n, _gcols(g)]
    return _window_sum(e, g, False)[HALO:, :] * (1.0 / (2 << g)) - e[HALO:, :]


def _pool_bwd(qb, g, r0, n):
    e = qb[r0:r0 + n + HALO, _gcols(g)]
    return _window_sum(e, g, True)[0:n, :] * (1.0 / (2 << g)) - e[0:n, :]


def _full(shape):
    nd = len(shape)
    return pl.BlockSpec(shape, lambda *_: (0,) * nd)


ANY = pl.BlockSpec(memory_space=pl.ANY)


def _mesh_pos():
    x, y, c = lax.axis_index("x"), lax.axis_index("y"), lax.axis_index("c")
    chips = [(1 - x, y), (x, 1 - y), (1 - x, 1 - y)]
    return x, y, c, chips


def _half(ref, h):
    hr = ref.shape[0] // 2
    return ref.at[pl.ds(h * hr, hr), :]


class _AllGather:
    PER_ARRAY = 9

    def __init__(self, ins, outs, send_sems, recv_sems):
        self.ins, self.outs, self.send_sems, self.recv_sems = ins, outs, send_sems, recv_sems
        self.n = len(ins)

    @classmethod
    def scratch(cls, n):
        return [pltpu.SemaphoreType.DMA((cls.PER_ARRAY * n,)), pltpu.SemaphoreType.DMA((cls.PER_ARRAY * n,))]

    @staticmethod
    def out_shape(shards):
        return [jax.ShapeDtypeStruct((N_CHIPS,) + s.shape, s.dtype) for s in shards]

    def _copy(self, a, k, src, dst, to):
        i = self.PER_ARRAY * a + k
        return pltpu.make_async_remote_copy(src_ref=src, dst_ref=dst, send_sem=self.send_sems.at[i],
                                            recv_sem=self.recv_sems.at[i], device_id=to, device_id_type=MESH)

    def _piece(self, a, chip, piece, h=None):
        h = lax.axis_index("c") if h is None else h
        rows = self.ins[a].shape[0] // 4
        return self.outs[a].at[chip].at[pl.ds((2 * h + piece) * rows, rows), :]

    def _own(self, a, k):
        x, y, c, chips = _mesh_pos()
        piece = (1, 0, 0, 1)[k]
        rows = self.ins[a].shape[0] // 4
        src = self.ins[a].at[pl.ds((2 * c + piece) * rows, rows), :]
        return self._copy(a, k, src, self._piece(a, 2 * x + y, piece), (*chips[k // 2], c))

    def _relay(self, a, k):
        x, y, c, chips = _mesh_pos()
        source, to, piece = (chips[1], chips[0], 0) if k == 4 else (chips[0], chips[1], 1)
        rows = self._piece(a, 2 * source[0] + source[1], piece)
        return self._copy(a, k, rows, rows, (*to, c))

    def _sibling(self, a, k, h):
        x, y, c, chips = _mesh_pos()
        chip = chips[k - 6]
        slot = _half(self.outs[a].at[2 * chip[0] + chip[1]], h)
        return self._copy(a, k, slot, slot, (x, y, 1 - c))

    def start(self):
        for a in range(self.n):
            for k in range(4):
                self._own(a, k).start()

    def relay(self, a):
        self._own(a, 2).wait_recv()
        self._relay(a, 4).start()
        self._own(a, 0).wait_recv()
        self._relay(a, 5).start()

    def forward(self, a):
        c = lax.axis_index("c")
        self._own(a, 1).wait_recv()
        self._sibling(a, 6, c).start()
        self._own(a, 3).wait_recv()
        self._sibling(a, 7, c).start()
        self._relay(a, 4).wait_recv()
        self._relay(a, 5).wait_recv()
        self._sibling(a, 8, c).start()

    def finish(self):
        c = lax.axis_index("c")
        for a in range(self.n):
            for k in range(6, 9):
                self._sibling(a, k, 1 - c).wait_recv()
        for a in range(self.n):
            for k in range(4):
                self._own(a, k).wait_send()
            for k in range(4, 6):
                self._relay(a, k).wait_send()
            for k in range(6, 9):
                self._sibling(a, k, c).wait_send()


def _fill_own_slot(gathered, shards):
    chip = 2 * lax.axis_index("x") + lax.axis_index("y")
    return [lax.dynamic_update_slice(o, s[None], (chip, 0, 0)) for o, s in zip(gathered, shards)]


def _all_gather_shards(shards):
    n = len(shards)

    def body(*refs):
        ag = _AllGather(refs[:n], refs[n:2 * n], *refs[2 * n:])
        ag.start()
        for a in range(n):
            ag.relay(a)
        for a in range(n):
            ag.forward(a)
        ag.finish()

    outs = pl.pallas_call(
        body, name="all_gather_weights", out_shape=_AllGather.out_shape(shards),
        in_specs=[ANY] * n, out_specs=[ANY] * n, scratch_shapes=_AllGather.scratch(n),
    )(*shards)
    return _fill_own_slot(outs, shards)


class _ExchangeHalves:
    def __init__(self, ins, recvs, send_sems, recv_sems):
        self.ins, self.recvs, self.send_sems, self.recv_sems = ins, recvs, send_sems, recv_sems

    @staticmethod
    def scratch(n):
        return [pltpu.SemaphoreType.DMA((n,)), pltpu.SemaphoreType.DMA((n,))]

    @staticmethod
    def out_shape(grads):
        return [jax.ShapeDtypeStruct((g.shape[0], g.shape[1] // 2, g.shape[2]), g.dtype) for g in grads]

    def _copies(self):
        x, y, c, _ = _mesh_pos()
        out = []
        for a, (src, dst) in enumerate(zip(self.ins, self.recvs)):
            hr = src.shape[1] // 2
            out.append(pltpu.make_async_remote_copy(
                src_ref=src.at[:, pl.ds((1 - c) * hr, hr), :], dst_ref=dst, send_sem=self.send_sems.at[a],
                recv_sem=self.recv_sems.at[a], device_id=(x, y, 1 - c), device_id_type=MESH))
        return out

    def start(self):
        for cp in self._copies():
            cp.start()

    def finish(self):
        for cp in self._copies():
            cp.wait()


def _exchange_halves(grads):
    n = len(grads)

    def body(*refs):
        ex = _ExchangeHalves(refs[:n], refs[n:2 * n], *refs[2 * n:])
        ex.start()
        ex.finish()

    return pl.pallas_call(
        body, name="grad_exchange_halves", out_shape=_ExchangeHalves.out_shape(grads),
        in_specs=[ANY] * n, out_specs=[ANY] * n, scratch_shapes=_ExchangeHalves.scratch(n),
    )(*grads)


class _ScatterToChips:
    def __init__(self, ins, rbufs, send_sems, recv_sems):
        self.ins, self.rbufs, self.send_sems, self.recv_sems = ins, rbufs, send_sems, recv_sems

    @staticmethod
    def scratch(n):
        return [pltpu.SemaphoreType.DMA((3 * n,)), pltpu.SemaphoreType.DMA((3 * n,))]

    @staticmethod
    def out_shape(sums):
        return [jax.ShapeDtypeStruct((3,) + s.shape[1:], BF16) for s in sums]

    def _copies(self):
        x, y, c, chips = _mesh_pos()
        out = []
        for a, (src, dst) in enumerate(zip(self.ins, self.rbufs)):
            for k, chip in enumerate(chips):
                out.append(pltpu.make_async_remote_copy(
                    src_ref=src.at[2 * chip[0] + chip[1]], dst_ref=dst.at[k], send_sem=self.send_sems.at[3 * a + k],
                    recv_sem=self.recv_sems.at[3 * a + k], device_id=(*chip, c), device_id_type=MESH))
        return out

    def start(self):
        for cp in self._copies():
            cp.start()

    def finish(self):
        for cp in self._copies():
            cp.wait()


def _gather_halves(halves):
    n = len(halves)

    def body(*refs):
        ins, outs = refs[:n], refs[n:2 * n]
        send_sems, recv_sems = refs[2 * n:]
        x, y, c, _ = _mesh_pos()
        sib = (x, y, 1 - c)
        remote = [pltpu.make_async_remote_copy(src_ref=ins[a].at[c], dst_ref=outs[a].at[c],
                                               send_sem=send_sems.at[a], recv_sem=recv_sems.at[a],
                                               device_id=sib, device_id_type=MESH) for a in range(n)]
        for cp in remote:
            cp.start()
        for a in range(n):
            pltpu.make_async_remote_copy(src_ref=ins[a].at[1 - c], dst_ref=outs[a].at[1 - c], send_sem=send_sems.at[a],
                                         recv_sem=recv_sems.at[a], device_id=sib, device_id_type=MESH).wait_recv()
        for cp in remote:
            cp.wait_send()

    return pl.pallas_call(
        body, name="grad_gather_halves",
        out_shape=[jax.ShapeDtypeStruct(h.shape, F32) for h in halves],
        in_specs=[ANY] * n, out_specs=[ANY] * n, input_output_aliases={a: a for a in range(n)},
        scratch_shapes=[pltpu.SemaphoreType.DMA((n,)), pltpu.SemaphoreType.DMA((n,))],
    )(*halves)


SMALL_A_ROWS = 24
SMALL_B_ROWS = 8
SMALL_C_ROWS = N_POOL_GROUPS * POOL_GROUP


class _AllReduceSmall:
    N_IN = 10
    SHAPES = [(SMALL_A_ROWS, D_MODEL), (SMALL_B_ROWS, D_CONV), (SMALL_C_ROWS, POOL_GROUP)]

    def __init__(self, ins, outs, scratch):
        self.ins, self.outs = ins, outs
        self.bufs, self.rcvs, self.send_sems, self.recv_sems = scratch[:3], scratch[3:6], scratch[6], scratch[7]

    @classmethod
    def scratch(cls):
        return ([pltpu.VMEM((3,) + s, F32) for s in cls.SHAPES] + [pltpu.VMEM((3,) + s, F32) for s in cls.SHAPES]
                + [pltpu.SemaphoreType.DMA((9,)), pltpu.SemaphoreType.DMA((9,))])

    @classmethod
    def out_shape(cls):
        return [jax.ShapeDtypeStruct(s, F32) for s in cls.SHAPES]

    def _copies(self, st):
        x, y, c, _ = _mesh_pos()
        peer = [(x, y, 1 - c), (1 - x, y, c), (x, 1 - y, c)][st]
        return [pltpu.make_async_remote_copy(
            src_ref=buf.at[st], dst_ref=rcv.at[st], send_sem=self.send_sems.at[3 * st + i],
            recv_sem=self.recv_sems.at[3 * st + i], device_id=peer, device_id_type=MESH)
            for i, (buf, rcv) in enumerate(zip(self.bufs, self.rcvs))]

    def pack_and_send(self):
        dg1_ref, dg1m_ref, dg2_ref, dg3_ref, dg4_ref, loss_ref, dmeta_ref, dsc_ref, dcw_ref, dpw_ref = self.ins
        a_buf, b_buf, c_buf = self.bufs

        def rowsum(v):
            return jnp.sum(v, axis=0, keepdims=True)

        a_buf[0, 0:1, :] = rowsum(dg1_ref[...] + dg1m_ref[...])
        a_buf[0, 1:2, :] = rowsum(dg2_ref[...])
        a_buf[0, 2:3, :] = rowsum(dg3_ref[...])
        a_buf[0, 3:4, :] = rowsum(dg4_ref[...])
        loss = jnp.sum(rowsum(loss_ref[...]), axis=1, keepdims=True) * (0.5 / D_MODEL)
        a_buf[0, 4:5, :] = jnp.broadcast_to(loss, (1, D_MODEL))
        a_buf[0, 5:8, :] = jnp.zeros((3, D_MODEL), F32)
        a_buf[0, 8:24, :] = dmeta_ref[...]
        b_buf[0, 0:1, :] = rowsum(dsc_ref[...])
        for k in range(3):
            b_buf[0, 1 + k:2 + k, :] = rowsum(dcw_ref[8 * k:8 * k + 8, :])
        b_buf[0, 4:8, :] = jnp.zeros((4, D_CONV), F32)
        c_buf[0] = dpw_ref[...]
        for cp in self._copies(0):
            cp.start()

    def combine(self, st):
        for cp in self._copies(st):
            cp.wait()
        if st < 2:
            for buf, rcv in zip(self.bufs, self.rcvs):
                buf[st + 1] = buf[st] + rcv[st]
            for cp in self._copies(st + 1):
                cp.start()
        else:
            for out, buf, rcv in zip(self.outs, self.bufs, self.rcvs):
                out[...] = buf[st] + rcv[st]


def _row_block(rows):
    for cand in (512, 448, 384, 352, 320, 256, 128, 64, 32, 16):
        if rows % cand == 0:
            return cand
    return rows


def _add_pairs(grad, recv, place):
    n_sh, rows2, cols = grad.shape
    hr = rows2 // 2
    br = _row_block(hr)

    def body(place_ref, a_ref, b_ref, o_ref):
        o_ref[...] = (a_ref[0] + b_ref[...]).astype(BF16)

    return pl.pallas_call(
        body, name="grad_add_pairs",
        grid_spec=pltpu.PrefetchScalarGridSpec(
            num_scalar_prefetch=1, grid=(n_sh, hr // br),
            in_specs=[pl.BlockSpec((1, 1, br, cols), lambda j, i, p: (j, p[1], i, 0)),
                      pl.BlockSpec((1, br, cols), lambda j, i, p: (j, i, 0))],
            out_specs=pl.BlockSpec((1, br, cols), lambda j, i, p: (j, i, 0))),
        out_shape=jax.ShapeDtypeStruct((n_sh, hr, cols), BF16), compiler_params=_cparams(2),
    )(place, grad.reshape(n_sh, 2, hr, cols), recv)


def _add_chips(grads, recvs, rbufs, place, scattered=(), name="grad_add_chips"):
    n, n_sc = len(grads), len(scattered)
    n_sh, rows2, cols = grads[0].shape
    hr = rows2 // 2
    br = _row_block(hr)
    n_steps = hr // br

    def body(place_ref, *refs):
        a_refs, b_refs, r_refs = refs[:n], refs[n:2 * n], refs[2 * n:3 * n]
        o_refs = refs[3 * n + n_sc:4 * n + n_sc]
        if n_sc:
            scatter = _ScatterToChips(refs[3 * n:3 * n + n_sc], refs[4 * n + n_sc:4 * n + 2 * n_sc], *refs[-2:])

            @pl.when(pl.program_id(0) == 0)
            def _():
                scatter.start()

        for a_ref, b_ref, r_ref, o_ref in zip(a_refs, b_refs, r_refs, o_refs):
            own = a_ref[0, 0] + b_ref[0]
            o_ref[0] = ((own + r_ref[0].astype(F32)) + r_ref[1].astype(F32)) + r_ref[2].astype(F32)

        if n_sc:
            @pl.when(pl.program_id(0) == n_steps - 1)
            def _():
                scatter.finish()

    outs = pl.pallas_call(
        body, name=name,
        grid_spec=pltpu.PrefetchScalarGridSpec(
            num_scalar_prefetch=1, grid=(n_steps,),
            in_specs=[pl.BlockSpec((1, 1, br, cols), lambda i, p: (p[0], p[1], i, 0))] * n
            + [pl.BlockSpec((1, br, cols), lambda i, p: (p[0], i, 0))] * n
            + [pl.BlockSpec((3, br, cols), lambda i, p: (0, i, 0))] * n + [ANY] * n_sc,
            out_specs=[pl.BlockSpec((1, br, cols), lambda i, p: (p[1], i, 0))] * n + [ANY] * n_sc,
            scratch_shapes=_ScatterToChips.scratch(n_sc) if n_sc else []),
        out_shape=[jax.ShapeDtypeStruct((2, hr, cols), F32)] * n + _ScatterToChips.out_shape(list(scattered)),
        compiler_params=_cparams(1),
    )(place, *[g.reshape(n_sh, 2, hr, cols) for g in grads], *recvs, *rbufs, *scattered)
    return outs[:n], outs[n:]


def _adamw_math(w, g, m, v):
    m2 = ADAM_B1 * m + (1.0 - ADAM_B1) * g
    v2 = ADAM_B2 * v + (1.0 - ADAM_B2) * (g * g)
    m_hat = m2 / (1.0 - ADAM_B1 ** ADAM_STEP)
    v_hat = v2 / (1.0 - ADAM_B2 ** ADAM_STEP)
    delta = -ADAM_LR * (m_hat / (jnp.sqrt(v_hat) + ADAM_EPS) + ADAM_WD * w)
    return delta, m2, v2


def _adamw_big(w, g, m, v):
    rows, cols = w.shape
    br = _row_block(rows)

    def body(w_ref, g_ref, m_ref, v_ref, d_ref, m2_ref, v2_ref):
        d, m2, v2 = _adamw_math(w_ref[...], g_ref[...], m_ref[...], v_ref[...])
        d_ref[...] = d
        m2_ref[...] = m2
        v2_ref[...] = v2

    spec = pl.BlockSpec((br, cols), lambda i: (i, 0))
    return pl.pallas_call(
        body, name="adamw_big", grid=(rows // br,),
        out_shape=[jax.ShapeDtypeStruct((rows, cols), F32)] * 3,
        in_specs=[spec] * 4, out_specs=[spec] * 3, compiler_params=_cparams(1),
    )(w, g, m, v)


def _adamw_small(groups):
    n = len(groups)

    def body(*refs):
        ins, outs = refs[:4 * n], refs[4 * n:]
        for i in range(n):
            w, g, m, v = (r[...] for r in ins[4 * i:4 * i + 4])
            d, m2, v2 = _adamw_math(w, g, m, v)
            outs[3 * i][...] = d
            outs[3 * i + 1][...] = m2
            outs[3 * i + 2][...] = v2

    vm = pl.BlockSpec(memory_space=pltpu.VMEM)
    flat = [a for grp in groups for a in grp]
    out_shape = [jax.ShapeDtypeStruct(grp[0].shape, F32) for grp in groups for _ in range(3)]
    outs = pl.pallas_call(body, name="adamw_small", out_shape=out_shape,
                          in_specs=[vm] * (4 * n), out_specs=[vm] * (3 * n))(*flat)
    return [tuple(outs[3 * i:3 * i + 3]) for i in range(n)]


def _load_weights(pairs, sem):
    for src, dst in pairs:
        cp = pltpu.make_async_copy(src, dst, sem)
        cp.start()
        cp.wait()


def _meta_fwd(meta_full, g1, win_all):
    def body(meta_ref, g1_ref, win_ref, z_ref):
        xm = meta_ref[...]
        a = (xm * _rstd(xm) * g1_ref[...]).astype(BF16)
        for j in range(N_CHIPS):
            z_ref[:, j * IN_SHARD:(j + 1) * IN_SHARD] = _dot(a, win_ref[j])

    vm = pl.BlockSpec(memory_space=pltpu.VMEM)
    return pl.pallas_call(body, name="meta_fwd", out_shape=jax.ShapeDtypeStruct((N_META, D_IN_PROJ), F32),
                          in_specs=[vm] * 3, out_specs=vm)(meta_full, g1, win_all)


def _mixer_fwd(x3, zmeta, g1, g2, convw, poolw, pscale, win_all, wout, ffn_shards):
    n_seq, seq, _ = x3.shape
    tm = min(TM_MIX_FWD, seq)
    n_t = seq // tm
    n_steps = n_seq * n_t
    n_ag = len(ffn_shards)

    def body(x_ref, zm_ref, g1_ref, g2_ref, cw_ref, pw_ref, ps_ref, win_hbm, wout_hbm, *rest):
        ag = _AllGather(rest[:n_ag], rest[n_ag + 7:2 * n_ag + 7], *rest[-2:])
        z_ref, m_ref, h1_ref, a_ref, conv_ref, pooled_ref, yc_ref = rest[n_ag:n_ag + 7]
        win_v, wout_v, cvb, pb, sem = rest[2 * n_ag + 7:-2]
        s, t = pl.program_id(0), pl.program_id(1)
        step = s * n_t + t

        @pl.when(step == 0)
        def _():
            ag.start()
            _load_weights([(win_hbm, win_v), (wout_hbm, wout_v)], sem)

        for a in range(n_ag):
            @pl.when(step == ((a + 1) * n_steps) // (2 * n_ag + 2))
            def _():
                ag.relay(a)

        for a in range(n_ag):
            @pl.when(step == min(n_steps // 2 + ((a + 1) * n_steps) // (2 * n_ag + 2), n_steps - 1))
            def _():
                ag.forward(a)

        @pl.when(t == 0)
        def _():
            cvb[0:HALO, :] = zm_ref[:, IN_SHARD:2 * IN_SHARD] * zm_ref[:, 2 * IN_SHARD:3 * IN_SHARD]
            pb[0:HALO, :] = zm_ref[:, 3 * IN_SHARD:4 * IN_SHARD]

        @pl.when(t > 0)
        def _():
            cvb[0:HALO, :] = cvb[tm:tm + HALO, :]
            pb[0:HALO, :] = pb[tm:tm + HALO, :]

        xt = x_ref[0]
        a = (xt * _rstd(xt) * g1_ref[...]).astype(BF16)
        a_ref[...] = a
        zb = _dot(a, win_v[0])
        zc = _dot(a, win_v[1])
        zv = _dot(a, win_v[2])
        zp = _dot(a, win_v[3])
        z_ref[0, :, 0:IN_SHARD] = zb
        z_ref[0, :, IN_SHARD:2 * IN_SHARD] = zc
        z_ref[0, :, 2 * IN_SHARD:3 * IN_SHARD] = zv
        cv = zc * zv
        cvb[HALO:HALO + tm, :] = cv
        pb[HALO:HALO + tm, :] = zp
        cw = cw_ref[...]
        conv = cw[0:1] * cvb[HALO - 2:HALO - 2 + tm, :] + cw[1:2] * cvb[HALO - 1:HALO - 1 + tm, :] + cw[2:3] * cv
        conv_ref[...] = conv
        parts = [(zb * conv).astype(BF16)]
        for g in range(N_POOL_GROUPS):
            pooled = _pool_fwd(pb, g, tm).astype(BF16)
            pooled_ref[:, _gcols(g)] = pooled
            parts.append((_dot(pooled, pw_ref[g]) * ps_ref[:, _gcols(g)]).astype(BF16))
        ycat = jnp.concatenate(parts, axis=1)
        yc_ref[...] = ycat
        m = _dot(ycat, wout_v[...])
        m_ref[0] = m
        h1_ref[0] = xt + m * _rstd(m) * g2_ref[...]

        @pl.when(step == n_steps - 1)
        def _():
            ag.finish()

    n_rows = n_seq * seq
    row = lambda c: pl.BlockSpec((1, tm, c), lambda s, t: (s, t, 0))
    row2 = lambda c: pl.BlockSpec((tm, c), lambda s, t: (s * n_t + t, 0))
    outs = pl.pallas_call(
        body, name="mixer_fwd", grid=(n_seq, n_t),
        out_shape=[jax.ShapeDtypeStruct((n_seq, seq, D_Z), F32), jax.ShapeDtypeStruct((n_seq, seq, D_MODEL), F32),
                   jax.ShapeDtypeStruct((n_seq, seq, D_MODEL), F32), jax.ShapeDtypeStruct((n_rows, D_MODEL), BF16),
                   jax.ShapeDtypeStruct((n_rows, D_CONV), F32), jax.ShapeDtypeStruct((n_rows, D_POOL), BF16),
                   jax.ShapeDtypeStruct((n_rows, D_MODEL), BF16)] + _AllGather.out_shape(ffn_shards),
        in_specs=[row(D_MODEL), _full((N_META, D_IN_PROJ)), _full((1, D_MODEL)), _full((1, D_MODEL)),
                  _full((3, D_CONV)), _full((N_POOL_GROUPS, POOL_GROUP, POOL_GROUP)), _full((1, D_POOL)), ANY, ANY]
        + [ANY] * n_ag,
        out_specs=[row(D_Z), row(D_MODEL), row(D_MODEL), row2(D_MODEL), row2(D_CONV), row2(D_POOL), row2(D_MODEL)]
        + [ANY] * n_ag,
        scratch_shapes=[pltpu.VMEM((N_CHIPS, D_MODEL, IN_SHARD), BF16), pltpu.VMEM((D_MODEL, D_MODEL), BF16),
                        pltpu.VMEM((HALO + tm, D_CONV), F32), pltpu.VMEM((HALO + tm, D_POOL), F32),
                        pltpu.SemaphoreType.DMA] + _AllGather.scratch(n_ag),
        compiler_params=_cparams(2),
    )(x3, zmeta, g1, g2, convw, poolw, pscale, win_all, wout, *ffn_shards)
    return outs[:7], _fill_own_slot(outs[7:], ffn_shards)


def _ffn_chunks():
    out, r0 = [], 0
    while r0 < D_FF:
        out.append((r0, min(FF_CHUNK, D_FF - r0)))
        r0 += FF_CHUNK
    return out


def _ffn_fwd_bwd(h1, target, g3, g4, wg_t, wu_t, wd):
    n_rows = h1.shape[0]
    tm = min(TM_FFN, n_rows)
    sub = min(SUB_FFN, tm)
    chunks = _ffn_chunks()
    subs = [slice(r, r + sub) for r in range(0, tm, sub)]

    def body(h1_ref, t_ref, g3_ref, g4_ref, wg_hbm, wu_hbm, wd_hbm,
             dh1_ref, f_ref, dd_ref, ds_ref, du_ref, gg_ref, loss_ref, dg3_ref, dg4_ref,
             wg_v, wu_v, wd_v, sem):
        @pl.when(pl.program_id(0) == 0)
        def _():
            _load_weights([(wg_hbm, wg_v), (wu_hbm, wu_v), (wd_hbm, wd_v)], sem)
            loss_ref[...] = jnp.zeros_like(loss_ref)
            dg3_ref[...] = jnp.zeros_like(dg3_ref)
            dg4_ref[...] = jnp.zeros_like(dg4_ref)

        g3v, g4v = g3_ref[...], g4_ref[...]

        def forward(rows):
            h1v = h1_ref[rows, :]
            r3 = _rstd(h1v)
            hh = h1v * r3
            f = (hh * g3v).astype(BF16)
            f_ref[rows, :] = f
            d = jnp.zeros((sub, D_MODEL), F32)
            for r0, sz in chunks:
                s = _dot_nt(f, wg_v[r0:r0 + sz, :])
                u = _dot_nt(f, wu_v[r0:r0 + sz, :])
                sig = _sigmoid(s)
                silu = s * sig
                gc = (silu * u).astype(BF16)
                gg_ref[rows, r0:r0 + sz] = gc
                ds_ref[rows, r0:r0 + sz] = (u * (sig * (1.0 + s * (1.0 - sig)))).astype(BF16)
                du_ref[rows, r0:r0 + sz] = silu.astype(BF16)
                d = d + _dot(gc, wd_v[r0:r0 + sz, :])
            return h1v, r3, hh, d

        def loss_phase(rows, h1v, r3, hh, d):
            r4 = _rstd(d)
            dh = d * r4
            err = (h1v + dh * g4v) - t_ref[rows, :]
            loss_ref[...] += _rows8(err * err)
            dy = err * (1.0 / D_MODEL)
            dg4_ref[...] += _rows8(dy * dh)
            ddb = _rms_bwd(dy, dh, r4, g4v).astype(BF16)
            dd_ref[rows, :] = ddb
            return r3, hh, dy, ddb

        def backward(rows, r3, hh, dy, ddb):
            df = jnp.zeros((sub, D_MODEL), F32)
            for r0, sz in chunks:
                dgg = _dot_nt(ddb, wd_v[r0:r0 + sz, :])
                dsc = (dgg * ds_ref[rows, r0:r0 + sz].astype(F32)).astype(BF16)
                duc = (dgg * du_ref[rows, r0:r0 + sz].astype(F32)).astype(BF16)
                ds_ref[rows, r0:r0 + sz] = dsc
                du_ref[rows, r0:r0 + sz] = duc
                df = df + _dot(dsc, wg_v[r0:r0 + sz, :]) + _dot(duc, wu_v[r0:r0 + sz, :])
            dg3_ref[...] += _rows8(df * hh)
            dh1_ref[rows, :] = dy + _rms_bwd(df, hh, r3, g3v)

        fwd = [forward(subs[0])]
        mid = []
        for j in range(1, len(subs)):
            mid.append(loss_phase(subs[j - 1], *fwd[j - 1]))
            fwd.append(forward(subs[j]))
        for j in range(len(subs)):
            if j == len(subs) - 1:
                mid.append(loss_phase(subs[j], *fwd[j]))
            backward(subs[j], *mid[j])

    row = pl.BlockSpec((tm, D_MODEL), lambda i: (i, 0))
    ffrow = pl.BlockSpec((tm, D_FF), lambda i: (i, 0))
    acc = _full((8, D_MODEL))
    act_bf = jax.ShapeDtypeStruct((n_rows, D_MODEL), BF16)
    ff_bf = jax.ShapeDtypeStruct((n_rows, D_FF), BF16)
    acc_shape = jax.ShapeDtypeStruct((8, D_MODEL), F32)
    w_vmem = pltpu.VMEM((D_FF, D_MODEL), BF16)
    return pl.pallas_call(
        body, name="ffn_fwd_bwd", grid=(n_rows // tm,),
        out_shape=[jax.ShapeDtypeStruct((n_rows, D_MODEL), F32), act_bf, act_bf, ff_bf, ff_bf, ff_bf,
                   acc_shape, acc_shape, acc_shape],
        in_specs=[row, row, _full((1, D_MODEL)), _full((1, D_MODEL)), ANY, ANY, ANY],
        out_specs=[row, row, row, ffrow, ffrow, ffrow, acc, acc, acc],
        scratch_shapes=[w_vmem, w_vmem, w_vmem, pltpu.SemaphoreType.DMA],
        compiler_params=pltpu.CompilerParams(dimension_semantics=("arbitrary",), vmem_limit_bytes=VMEM_LIMIT_FFN),
    )(h1, target, g3, g4, wg_t, wu_t, wd)


def _ffn_weight_grads(name, acts, other, exchanged):
    n_rows = other.shape[0]
    n_a, n_ex = len(acts), len(exchanged)
    n_c = n_a
    tk = min(TK_DW, n_rows)
    n_k = n_rows // tk
    half = D_FF // n_c

    def body(other_ref, *rest):
        act_refs = rest[:n_a]
        out_refs = rest[n_a + n_ex:2 * n_a + n_ex]
        c, k = pl.program_id(0), pl.program_id(1)
        if n_ex:
            ex = _ExchangeHalves(rest[n_a:n_a + n_ex], rest[2 * n_a + n_ex:2 * n_a + 2 * n_ex], *rest[-2:])

            @pl.when((c == 0) & (k == 0))
            def _():
                ex.start()

        @pl.when(k == 0)
        def _():
            for o in out_refs:
                o[...] = jnp.zeros_like(o)

        ov = other_ref[...]
        for a, o in zip(act_refs, out_refs):
            o[...] += _dot_tn(a[...], ov)

        if n_ex:
            @pl.when((c == n_c - 1) & (k == n_k - 1))
            def _():
                ex.finish()

    row = pl.BlockSpec((tk, D_MODEL), lambda c, k: (k, 0))
    ffrow = pl.BlockSpec((tk, half), lambda c, k: (k, c))
    out = pl.BlockSpec((half, D_MODEL), lambda c, k: (c, 0))
    outs = pl.pallas_call(
        body, name=name, grid=(n_c, n_k),
        out_shape=[jax.ShapeDtypeStruct((D_FF, D_MODEL), F32)] * n_a + _ExchangeHalves.out_shape(exchanged),
        in_specs=[row] + [ffrow] * n_a + [ANY] * n_ex, out_specs=[out] * n_a + [ANY] * n_ex,
        scratch_shapes=_ExchangeHalves.scratch(n_ex) if n_ex else [],
        compiler_params=_cparams(2),
    )(other, *acts, *exchanged)
    return outs[:n_a], outs[n_a:]


def _mixer_bwd(dh1, m3, z3, conv2, pooled2, x3, zmeta, g1, g2, convw, poolw, pscale, win_all, wout, exchanged,
               scattered):
    n_seq, seq, _ = x3.shape
    tm = min(TM_MIX_BWD, seq)
    sub = min(SUB_MIX_BWD, tm)
    n_t = seq // tm
    n_ex, n_sc = len(exchanged), len(scattered)
    n_cm = n_ex + n_sc

    def body(dh1_ref, m_ref, z_ref, conv_ref, pooled_ref, x_ref, zm_ref, g1_ref, g2_ref, cw_ref, pw_ref, ps_ref,
             win_hbm, wout_hbm, *rest):
        outs0 = n_cm + 9
        ex = _ExchangeHalves(rest[:n_ex], rest[outs0:outs0 + n_ex], *rest[-4:-2])
        sc = _ScatterToChips(rest[n_ex:n_cm], rest[outs0 + n_ex:outs0 + n_cm], *rest[-2:])
        dx_ref, dz_ref, dm_ref, dg1_ref, dg2_ref, dsc_ref, dcw_ref, dpw_ref, dzm_ref = rest[n_cm:outs0]
        win_v, wout_v, dcb, dqb, mcb, mqb, sem = rest[outs0 + n_cm:-4]
        s, i = pl.program_id(0), pl.program_id(1)
        tr = n_t - 1 - i

        @pl.when((s == 0) & (i == 0))
        def _():
            sc.start()
            ex.start()
            _load_weights([(win_hbm, win_v), (wout_hbm, wout_v)], sem)
            for ref in (dg1_ref, dg2_ref, dsc_ref, dcw_ref, dpw_ref, dzm_ref):
                ref[...] = jnp.zeros_like(ref)

        @pl.when(i == 0)
        def _():
            dcb[tm:tm + HALO, :] = jnp.zeros((HALO, D_CONV), F32)
            dqb[tm:tm + HALO, :] = jnp.zeros((HALO, D_POOL), F32)

        @pl.when(i > 0)
        def _():
            dcb[tm:tm + HALO, :] = dcb[0:HALO, :]
            dqb[tm:tm + HALO, :] = dqb[0:HALO, :]

        g1v, g2v = g1_ref[...], g2_ref[...]
        cw = cw_ref[...]

        for r0 in range(tm - sub, -1, -sub):
            rows = slice(r0, r0 + sub)
            dh1v = dh1_ref[0, rows, :]
            mv = m_ref[0, rows, :]
            r2 = _rstd(mv)
            mh = mv * r2
            dg2_ref[...] += _rows8(dh1v * mh)
            dmb = _rms_bwd(dh1v, mh, r2, g2v).astype(BF16)
            dm_ref[rows, :] = dmb
            dyc = _dot_nt(dmb, wout_v[...])
            dyconv = dyc[:, 0:D_CONV]

            for g in range(N_POOL_GROUPS):
                pooled = pooled_ref[rows, _gcols(g)]
                mixed = _dot(pooled, pw_ref[g])
                scale = ps_ref[:, _gcols(g)]
                dyp = dyc[:, D_CONV + g * POOL_GROUP:D_CONV + (g + 1) * POOL_GROUP]
                dsc_ref[:, _gcols(g)] += _rows8(dyp * mixed)
                dmix = (dyp * scale).astype(BF16)
                dpw_ref[g] += _dot_tn(pooled, dmix)
                dqb[rows, _gcols(g)] = _dot_nt(dmix, pw_ref[g])

            zb = z_ref[0, rows, 0:IN_SHARD]
            zc = z_ref[0, rows, IN_SHARD:2 * IN_SHARD]
            zv = z_ref[0, rows, 2 * IN_SHARD:3 * IN_SHARD]
            dconv = dyconv * zb
            dcb[rows, :] = dconv
            d1 = dcb[r0 + 1:r0 + 1 + sub, :]
            d2 = dcb[r0 + 2:r0 + 2 + sub, :]
            dcv = cw[2:3] * dconv + cw[1:2] * d1 + cw[0:1] * d2
            cv = zc * zv
            dcw_ref[0:8, :] += _rows8(cv * d2)
            dcw_ref[8:16, :] += _rows8(cv * d1)
            dcw_ref[16:24, :] += _rows8(cv * dconv)
            dzs = [(dyconv * conv_ref[rows, :]).astype(BF16), (dcv * zv).astype(BF16), (dcv * zc).astype(BF16),
                   jnp.concatenate([_pool_bwd(dqb, g, r0, sub) for g in range(N_POOL_GROUPS)], axis=1).astype(BF16)]
            da = jnp.zeros((sub, D_MODEL), F32)
            for j in range(N_CHIPS):
                dz_ref[j, rows, :] = dzs[j]
                da = da + _dot_nt(dzs[j], win_v[j])
            xt = x_ref[0, rows, :]
            r1 = _rstd(xt)
            xh = xt * r1
            dg1_ref[...] += _rows8(da * xh)
            dx_ref[0, rows, :] = dh1v + _rms_bwd(da, xh, r1, g1v)

        @pl.when(tr == 0)
        def _():
            mcb[0:HALO, :] = jnp.zeros((HALO, D_CONV), F32)
            mqb[0:HALO, :] = jnp.zeros((HALO, D_POOL), F32)
            mcb[HALO:2 * HALO, :] = dcb[0:HALO, :]
            mqb[HALO:2 * HALO, :] = dqb[0:HALO, :]
            m1 = mcb[1:1 + HALO, :]
            m2 = mcb[2:2 + HALO, :]
            zc_m = zm_ref[:, IN_SHARD:2 * IN_SHARD]
            zv_m = zm_ref[:, 2 * IN_SHARD:3 * IN_SHARD]
            cv_m = zc_m * zv_m
            dcw_ref[0:8, :] += _rows8(cv_m * m2)
            dcw_ref[8:16, :] += _rows8(cv_m * m1)
            dcv_m = cw[1:2] * m1 + cw[0:1] * m2
            dzm_ref[:, IN_SHARD:2 * IN_SHARD] += dcv_m * zv_m
            dzm_ref[:, 2 * IN_SHARD:3 * IN_SHARD] += dcv_m * zc_m
            dzm_ref[:, 3 * IN_SHARD:4 * IN_SHARD] += jnp.concatenate(
                [_pool_bwd(mqb, g, 0, HALO) for g in range(N_POOL_GROUPS)], axis=1)

        @pl.when((s == n_seq - 1) & (i == n_t - 1))
        def _():
            ex.finish()
            sc.finish()

    row3 = lambda c: pl.BlockSpec((1, tm, c), lambda s, i: (s, n_t - 1 - i, 0))
    row2 = lambda c: pl.BlockSpec((tm, c), lambda s, i: (s * n_t + n_t - 1 - i, 0))
    n_rows = n_seq * seq
    outs = pl.pallas_call(
        body, name="mixer_bwd", grid=(n_seq, n_t),
        out_shape=[jax.ShapeDtypeStruct((n_seq, seq, D_MODEL), F32),
                   jax.ShapeDtypeStruct((N_CHIPS, n_rows, IN_SHARD), BF16), jax.ShapeDtypeStruct((n_rows, D_MODEL), BF16),
                   jax.ShapeDtypeStruct((8, D_MODEL), F32), jax.ShapeDtypeStruct((8, D_MODEL), F32),
                   jax.ShapeDtypeStruct((8, D_POOL), F32), jax.ShapeDtypeStruct((24, D_CONV), F32),
                   jax.ShapeDtypeStruct((N_POOL_GROUPS, POOL_GROUP, POOL_GROUP), F32),
                   jax.ShapeDtypeStruct((N_META, D_IN_PROJ), F32)]
        + _ExchangeHalves.out_shape(exchanged) + _ScatterToChips.out_shape(scattered),
        in_specs=[row3(D_MODEL), row3(D_MODEL), row3(D_Z), row2(D_CONV), row2(D_POOL), row3(D_MODEL),
                  _full((N_META, D_IN_PROJ)), _full((1, D_MODEL)), _full((1, D_MODEL)), _full((3, D_CONV)),
                  _full((N_POOL_GROUPS, POOL_GROUP, POOL_GROUP)), _full((1, D_POOL)), ANY, ANY] + [ANY] * n_cm,
        out_specs=[row3(D_MODEL), pl.BlockSpec((N_CHIPS, tm, IN_SHARD), lambda s, i: (0, s * n_t + n_t - 1 - i, 0)),
                   row2(D_MODEL),
                   _full((8, D_MODEL)), _full((8, D_MODEL)), _full((8, D_POOL)), _full((24, D_CONV)),
                   _full((N_POOL_GROUPS, POOL_GROUP, POOL_GROUP)), _full((N_META, D_IN_PROJ))] + [ANY] * n_cm,
        scratch_shapes=[pltpu.VMEM((N_CHIPS, D_MODEL, IN_SHARD), BF16), pltpu.VMEM((D_MODEL, D_MODEL), BF16),
                        pltpu.VMEM((tm + HALO, D_CONV), F32), pltpu.VMEM((tm + HALO, D_POOL), F32),
                        pltpu.VMEM((2 * HALO, D_CONV), F32), pltpu.VMEM((2 * HALO, D_POOL), F32),
                        pltpu.SemaphoreType.DMA] + _ExchangeHalves.scratch(n_ex) + _ScatterToChips.scratch(n_sc),
        compiler_params=_cparams(2),
    )(dh1, m3, z3, conv2, pooled2, x3, zmeta, g1, g2, convw, poolw, pscale, win_all, wout, *exchanged, *scattered)
    return outs[:9], outs[9:9 + n_ex], outs[9 + n_ex:]


def _meta_bwd(dzm, meta_full, g1, win_all):
    def body(dzm_ref, meta_ref, g1_ref, win_ref, dmeta_ref, dg1_ref, a_ref, dzb_ref):
        xm = meta_ref[...]
        r = _rstd(xm)
        xh = xm * r
        g1v = g1_ref[...]
        a_ref[...] = (xh * g1v).astype(BF16)
        da = jnp.zeros((N_META, D_MODEL), F32)
        for j in range(N_CHIPS):
            dzj = dzm_ref[:, j * IN_SHARD:(j + 1) * IN_SHARD].astype(BF16)
            dzb_ref[j] = dzj
            da = da + _dot_nt(dzj, win_ref[j])
        dg1_ref[...] = _rows8(da * xh)
        dmeta_ref[...] = _rms_bwd(da, xh, r, g1v)

    vm = pl.BlockSpec(memory_space=pltpu.VMEM)
    return pl.pallas_call(
        body, name="meta_bwd",
        out_shape=[jax.ShapeDtypeStruct((N_META, D_MODEL), F32), jax.ShapeDtypeStruct((8, D_MODEL), F32),
                   jax.ShapeDtypeStruct((N_META, D_MODEL), BF16), jax.ShapeDtypeStruct((N_CHIPS, N_META, IN_SHARD), BF16)],
        in_specs=[vm] * 4, out_specs=[vm] * 4,
    )(dzm, meta_full, g1, win_all)


def _mixer_weight_grads(a, dz, ycat, dm, a_meta, dz_meta, ffn_sums, small):
    n_rows = a.shape[0]
    tk = min(TK_DW, n_rows)
    n_k = n_rows // tk
    n_sc, n_sm = len(ffn_sums), _AllReduceSmall.N_IN

    def body(a_ref, dz_ref, yc_ref, dm_ref, am_ref, dzm_ref, *rest):
        ins, outs, scratch = rest[:n_sc + n_sm], rest[n_sc + n_sm:2 * n_sc + n_sm + 5], rest[2 * n_sc + n_sm + 5:]
        dwin_ref, dwout_ref = outs[:2]
        scatter = _ScatterToChips(ins[:n_sc], outs[2:2 + n_sc], *scratch[:2])
        reduce_small = _AllReduceSmall(ins[n_sc:], outs[2 + n_sc:], scratch[2:])
        k = pl.program_id(0)

        @pl.when(k == 0)
        def _():
            scatter.start()
            reduce_small.pack_and_send()
            am_t = am_ref[...].T
            for j in range(N_CHIPS):
                dwin_ref[j] = _dot(am_t, dzm_ref[j])
            dwout_ref[...] = jnp.zeros_like(dwout_ref)

        for st in range(2):
            @pl.when(k == ((st + 1) * n_k) // 3)
            def _():
                reduce_small.combine(st)

        a_t = a_ref[...].T
        for j in range(N_CHIPS):
            dwin_ref[j] += _dot(a_t, dz_ref[j])
        dwout_ref[...] += _dot_tn(yc_ref[...], dm_ref[...])

        @pl.when(k == n_k - 1)
        def _():
            reduce_small.combine(2)
            scatter.finish()

    row = pl.BlockSpec((tk, D_MODEL), lambda k: (k, 0))
    outs = pl.pallas_call(
        body, name="mixer_weight_grads", grid=(n_k,),
        out_shape=[jax.ShapeDtypeStruct((N_CHIPS, D_MODEL, IN_SHARD), F32),
                   jax.ShapeDtypeStruct((D_MODEL, D_MODEL), F32)] + _ScatterToChips.out_shape(ffn_sums)
        + _AllReduceSmall.out_shape(),
        in_specs=[row, pl.BlockSpec((N_CHIPS, tk, IN_SHARD), lambda k: (0, k, 0)), row, row,
                  _full((N_META, D_MODEL)), _full((N_CHIPS, N_META, IN_SHARD))] + [ANY] * n_sc
        + [_full(s.shape) for s in small],
        out_specs=[_full((N_CHIPS, D_MODEL, IN_SHARD)), _full((D_MODEL, D_MODEL))] + [ANY] * n_sc
        + [_full(s) for s in _AllReduceSmall.SHAPES],
        scratch_shapes=_ScatterToChips.scratch(n_sc) + _AllReduceSmall.scratch(),
        compiler_params=_cparams(1),
    )(a, dz, ycat, dm, a_meta, dz_meta, *ffn_sums, *small)
    return ([outs[0], outs[1].reshape(N_CHIPS, OUT_SHARD, D_MODEL)], outs[2:2 + n_sc], outs[2 + n_sc:])


def kernel(x, meta_tokens, norm_mix_pre, w_in, conv_w, pool_w, pool_scale, w_out, norm_mix_post, norm_ffn_pre, w_gate, w_up, w_down, norm_ffn_post, loss_target, m_meta_tokens, m_norm_mix_pre, m_w_in, m_conv_w, m_pool_w, m_pool_scale, m_w_out, m_norm_mix_post, m_norm_ffn_pre, m_w_gate, m_w_up, m_w_down, m_norm_ffn_post, v_meta_tokens, v_norm_mix_pre, v_w_in, v_conv_w, v_pool_w, v_pool_scale, v_w_out, v_norm_mix_post, v_norm_ffn_pre, v_w_gate, v_w_up, v_w_down, v_norm_ffn_post):
    n_seq, seq, _ = x.shape
    n_rows = n_seq * seq
    chip = 2 * lax.axis_index("x") + lax.axis_index("y")
    meta_cols = D_MODEL // N_CHIPS
    conv_cols = D_CONV // N_CHIPS

    small = jnp.zeros((2 * HALO, meta_cols), F32)
    small = small.at[0:N_META, :].set(meta_tokens).at[N_META:N_META + 3, 0:conv_cols].set(conv_w[0])
    win_all, wout_all, small_all = _all_gather_shards([w_in[0].astype(BF16), w_out[0].astype(BF16), small])
    meta_full = small_all[:, 0:N_META, :].transpose(1, 0, 2).reshape(N_META, D_MODEL)
    conv_full = small_all[:, N_META:N_META + 3, 0:conv_cols].transpose(1, 0, 2).reshape(3, D_CONV)
    wout_full = wout_all.reshape(D_MODEL, D_MODEL)
    poolw_bf = pool_w[0].astype(BF16)
    pscale = pool_scale
    g1, g2, g3, g4 = norm_mix_pre, norm_mix_post, norm_ffn_pre, norm_ffn_post
    place = jnp.stack([chip, lax.axis_index("c")]).astype(jnp.int32)

    zmeta = _meta_fwd(meta_full, g1, win_all)
    (z3, m3, h1, a_bf, conv2, pooled2, yc_bf), ffn_w = _mixer_fwd(
        x, zmeta, g1, g2, conv_full, poolw_bf, pscale, win_all, wout_full,
        [w_gate[0].T.astype(BF16), w_up[0].T.astype(BF16), w_down[0].astype(BF16)])
    wg_t, wu_t, wd_full = [w.reshape(D_FF, D_MODEL) for w in ffn_w]
    dh1, f_bf, dd_bf, ds_bf, du_bf, gg_bf, lossp, dg3p, dg4p = _ffn_fwd_bwd(
        h1.reshape(n_rows, D_MODEL), loss_target.reshape(n_rows, D_MODEL), g3, g4, wg_t, wu_t, wd_full)
    as_shards = lambda g: g.reshape(N_CHIPS, FF_SHARD, D_MODEL)
    (dwd,), _ = _ffn_weight_grads("ffn_weight_grads_down", [gg_bf], dd_bf, [])
    dwd = as_shards(dwd)
    (dwg_t,), (dwd_recv,) = _ffn_weight_grads("ffn_weight_grads_gate", [ds_bf], f_bf, [dwd])
    dwg_t = as_shards(dwg_t)
    (dwu_t,), (dwg_recv,) = _ffn_weight_grads("ffn_weight_grads_up", [du_bf], f_bf, [dwg_t])
    dwu_t = as_shards(dwu_t)
    ((grad_x, dz_bf, dm_bf, dg1p, dg2p, dscp, dcwp, dpw, dzm), (dwu_recv,), (dwd_rbuf, dwg_rbuf)) = _mixer_bwd(
        dh1.reshape(n_seq, seq, D_MODEL), m3, z3, conv2, pooled2, x, zmeta, g1, g2, conv_full, poolw_bf, pscale,
        win_all, wout_full, [dwu_t], [_add_pairs(dwd, dwd_recv, place), _add_pairs(dwg_t, dwg_recv, place)])
    dmeta, dg1m, a_meta, dz_meta = _meta_bwd(dzm, meta_full, g1, win_all)
    mix_grads, (dwu_rbuf,), (a_red, b_red, c_red) = _mixer_weight_grads(
        a_bf, dz_bf, yc_bf, dm_bf, a_meta, dz_meta, [_add_pairs(dwu_t, dwu_recv, place)],
        [dg1p, dg1m, dg2p, dg3p, dg4p, lossp, dmeta, dscp, dcwp, dpw.reshape(SMALL_C_ROWS, POOL_GROUP)])

    mix_recvs = _exchange_halves(mix_grads)
    ffn_red, mix_rbufs = _add_chips([dwg_t, dwu_t, dwd], [dwg_recv, dwu_recv, dwd_recv],
                                    [dwg_rbuf, dwu_rbuf, dwd_rbuf], place,
                                    [_add_pairs(g, r, place) for g, r in zip(mix_grads, mix_recvs)],
                                    name="grad_add_chips_ffn")
    mix_red = [_add_chips([g], [r], [rb], place)[0][0] for g, r, rb in zip(mix_grads, mix_recvs, mix_rbufs)]
    reduced = _gather_halves(mix_red + list(ffn_red))
    g_win, g_wout, g_wg_t, g_wu_t, g_wd = [r.reshape(2 * r.shape[1], r.shape[2]) for r in reduced]

    loss = a_red[4, 0]
    g_g1, g_g2, g_g3, g_g4 = a_red[0:1], a_red[1:2], a_red[2:3], a_red[3:4]
    g_meta = lax.dynamic_slice(a_red, (8, chip * meta_cols), (N_META, meta_cols))
    g_pscale = b_red[0:1]
    g_conv = lax.dynamic_slice(b_red, (1, chip * conv_cols), (3, conv_cols))
    g_poolw = c_red

    big = [(w_in[0], g_win, m_w_in[0], v_w_in[0]), (w_out[0], g_wout, m_w_out[0], v_w_out[0]),
           (w_gate[0].T, g_wg_t, m_w_gate[0].T, v_w_gate[0].T), (w_up[0].T, g_wu_t, m_w_up[0].T, v_w_up[0].T),
           (w_down[0], g_wd, m_w_down[0], v_w_down[0])]
    big_out = [_adamw_big(w, g, m, v) for (w, g, m, v) in big]
    big_out[2] = [o.T for o in big_out[2]]
    big_out[3] = [o.T for o in big_out[3]]
    g_wg, g_wu = g_wg_t.T, g_wu_t.T
    small_groups = [
        (meta_tokens, g_meta, m_meta_tokens, v_meta_tokens),
        (g1, g_g1, m_norm_mix_pre, v_norm_mix_pre),
        (conv_w[0], g_conv, m_conv_w[0], v_conv_w[0]),
        (pool_w.reshape(SMALL_C_ROWS, POOL_GROUP), g_poolw, m_pool_w.reshape(SMALL_C_ROWS, POOL_GROUP),
         v_pool_w.reshape(SMALL_C_ROWS, POOL_GROUP)),
        (pool_scale, g_pscale, m_pool_scale, v_pool_scale),
        (g2, g_g2, m_norm_mix_post, v_norm_mix_post),
        (g3, g_g3, m_norm_ffn_pre, v_norm_ffn_pre),
        (g4, g_g4, m_norm_ffn_post, v_norm_ffn_post),
    ]
    small_out = _adamw_small(small_groups)

    grads_out = [g_meta, g_g1, g_win[None], g_conv[None], g_poolw.reshape(pool_w.shape), g_pscale, g_wout[None],
                 g_g2, g_g3, g_wg[None], g_wu[None], g_wd[None], g_g4]
    s_meta, s_g1, s_conv, s_poolw, s_pscale, s_g2, s_g3, s_g4 = small_out
    b_win, b_wout, b_wg, b_wu, b_wd = big_out

    def leaf(k):
        return [s_meta[k], s_g1[k], b_win[k][None], s_conv[k][None], s_poolw[k].reshape(pool_w.shape), s_pscale[k],
                b_wout[k][None], s_g2[k], s_g3[k], b_wg[k][None], b_wu[k][None], b_wd[k][None], s_g4[k]]

    return (loss, grad_x, *grads_out, *leaf(0), *leaf(1), *leaf(2))
```

```python
import jax
import jax.numpy as jnp
from jax import lax
from jax.experimental import pallas as pl
from jax.experimental.pallas import tpu as pltpu

F32 = jnp.float32
BF16 = jnp.bfloat16
MESH = pl.DeviceIdType.MESH

D_MODEL = 1024
D_CONV = 512
D_POOL = 512
POOL_GROUP = 128
N_POOL_GROUPS = 4
D_IN_PROJ = 2048
D_FF = 2816
N_CHIPS = 4
FF_SHARD = D_FF // N_CHIPS
IN_SHARD = D_IN_PROJ // N_CHIPS
OUT_SHARD = D_MODEL // N_CHIPS
D_Z = 3 * IN_SHARD
N_META = 16
HALO = 16
RMS_EPS = 1e-6

ADAM_LR = 0.001
ADAM_B1 = 0.9
ADAM_B2 = 0.999
ADAM_EPS = 1e-08
ADAM_WD = 0.01
ADAM_STEP = 10

TM_MIX_FWD = 512
TM_MIX_BWD = 512
SUB_MIX_BWD = 512
TM_FFN = 256
TK_DW = 1024
FF_CHUNK = 1024
VMEM_LIMIT = 56 * 1024 * 1024


def _cparams(n_grid):
    return pltpu.CompilerParams(dimension_semantics=("arbitrary",) * n_grid, vmem_limit_bytes=VMEM_LIMIT)


def _dot(a, b):
    return jnp.dot(a, b, preferred_element_type=F32)


def _dot_nt(a, b):
    return lax.dot_general(a, b, (((1,), (1,)), ((), ())), preferred_element_type=F32)


def _dot_tn(a, b):
    return lax.dot_general(a, b, (((0,), (0,)), ((), ())), preferred_element_type=F32)


def _rows8(v):
    r, c = v.shape
    return v.reshape(r // 8, 8, c).sum(axis=0)


def _rstd(v):
    return lax.rsqrt(jnp.mean(v * v, axis=-1, keepdims=True) + RMS_EPS)


def _rms_bwd(dy, xhat, rstd, gain):
    dyg = dy * gain
    return rstd * (dyg - xhat * jnp.mean(dyg * xhat, axis=-1, keepdims=True))


def _sigmoid(v):
    return 1.0 / (1.0 + jnp.exp(-v))


def _gcols(g):
    return slice(g * POOL_GROUP, (g + 1) * POOL_GROUP)


def _window_sum(e, g, ahead):
    n = e.shape[0]
    w = e
    for level in range(g + 1):
        shift = 1 << level
        w = w + pltpu.roll(w, (n - shift) if ahead else shift, 0)
    return w


def _pool_fwd(pb, g, n):
    e = pb[0:HALO + n, _gcols(g)]
    return _window_sum(e, g, False)[HALO:, :] * (1.0 / (2 << g)) - e[HALO:, :]


def _pool_bwd(qb, g, r0, n):
    e = qb[r0:r0 + n + HALO, _gcols(g)]
    return _window_sum(e, g, True)[0:n, :] * (1.0 / (2 << g)) - e[0:n, :]


def _full(shape):
    nd = len(shape)
    return pl.BlockSpec(shape, lambda *_: (0,) * nd)


ANY = pl.BlockSpec(memory_space=pl.ANY)


def _mesh_pos():
    x, y, c = lax.axis_index("x"), lax.axis_index("y"), lax.axis_index("c")
    chips = [(1 - x, y), (x, 1 - y), (1 - x, 1 - y)]
    return x, y, c, chips


def _half(ref, h):
    hr = ref.shape[0] // 2
    return ref.at[pl.ds(h * hr, hr), :]


class _AllGather:
    PER_ARRAY = 9

    def __init__(self, ins, outs, send_sems, recv_sems):
        self.ins, self.outs, self.send_sems, self.recv_sems = ins, outs, send_sems, recv_sems
        self.n = len(ins)

    @classmethod
    def scratch(cls, n):
        return [pltpu.SemaphoreType.DMA((cls.PER_ARRAY * n,)), pltpu.SemaphoreType.DMA((cls.PER_ARRAY * n,))]

    @staticmethod
    def out_shape(shards):
        return [jax.ShapeDtypeStruct((N_CHIPS,) + s.shape, s.dtype) for s in shards]

    def _copy(self, a, k, src, dst, to):
        i = self.PER_ARRAY * a + k
        return pltpu.make_async_remote_copy(src_ref=src, dst_ref=dst, send_sem=self.send_sems.at[i],
                                            recv_sem=self.recv_sems.at[i], device_id=to, device_id_type=MESH)

    def _piece(self, a, chip, piece, h=None):
        h = lax.axis_index("c") if h is None else h
        rows = self.ins[a].shape[0] // 4
        return self.outs[a].at[chip].at[pl.ds((2 * h + piece) * rows, rows), :]

    def _own(self, a, k):
        x, y, c, chips = _mesh_pos()
        piece = (1, 0, 0, 1)[k]
        rows = self.ins[a].shape[0] // 4
        src = self.ins[a].at[pl.ds((2 * c + piece) * rows, rows), :]
        return self._copy(a, k, src, self._piece(a, 2 * x + y, piece), (*chips[k // 2], c))

    def _relay(self, a, k):
        x, y, c, chips = _mesh_pos()
        source, to, piece = (chips[1], chips[0], 0) if k == 4 else (chips[0], chips[1], 1)
        rows = self._piece(a, 2 * source[0] + source[1], piece)
        return self._copy(a, k, rows, rows, (*to, c))

    def _sibling(self, a, k, h):
        x, y, c, chips = _mesh_pos()
        chip = chips[k - 6]
        slot = _half(self.outs[a].at[2 * chip[0] + chip[1]], h)
        return self._copy(a, k, slot, slot, (x, y, 1 - c))

    def start(self):
        for a in range(self.n):
            for k in range(4):
                self._own(a, k).start()

    def relay(self, a):
        self._own(a, 2).wait_recv()
        self._relay(a, 4).start()
        self._own(a, 0).wait_recv()
        self._relay(a, 5).start()

    def forward(self, a):
        c = lax.axis_index("c")
        self._own(a, 1).wait_recv()
        self._sibling(a, 6, c).start()
        self._own(a, 3).wait_recv()
        self._sibling(a, 7, c).start()
        self._relay(a, 4).wait_recv()
        self._relay(a, 5).wait_recv()
        self._sibling(a, 8, c).start()

    def finish(self):
        c = lax.axis_index("c")
        for a in range(self.n):
            for k in range(6, 9):
                self._sibling(a, k, 1 - c).wait_recv()
        for a in range(self.n):
            for k in range(4):
                self._own(a, k).wait_send()
            for k in range(4, 6):
                self._relay(a, k).wait_send()
            for k in range(6, 9):
                self._sibling(a, k, c).wait_send()


def _fill_own_slot(gathered, shards):
    chip = 2 * lax.axis_index("x") + lax.axis_index("y")
    return [lax.dynamic_update_slice(o, s[None], (chip, 0, 0)) for o, s in zip(gathered, shards)]


def _all_gather_shards(shards):
    n = len(shards)

    def body(*refs):
        ag = _AllGather(refs[:n], refs[n:2 * n], *refs[2 * n:])
        ag.start()
        for a in range(n):
            ag.relay(a)
        for a in range(n):
            ag.forward(a)
        ag.finish()

    outs = pl.pallas_call(
        body, name="all_gather_weights", out_shape=_AllGather.out_shape(shards),
        in_specs=[ANY] * n, out_specs=[ANY] * n, scratch_shapes=_AllGather.scratch(n),
    )(*shards)
    return _fill_own_slot(outs, shards)


class _ExchangeHalves:
    def __init__(self, ins, recvs, send_sems, recv_sems):
        self.ins, self.recvs, self.send_sems, self.recv_sems = ins, recvs, send_sems, recv_sems

    @staticmethod
    def scratch(n):
        return [pltpu.SemaphoreType.DMA((n,)), pltpu.SemaphoreType.DMA((n,))]

    @staticmethod
    def out_shape(grads):
        return [jax.ShapeDtypeStruct((g.shape[0], g.shape[1] // 2, g.shape[2]), g.dtype) for g in grads]

    def _copies(self):
        x, y, c, _ = _mesh_pos()
        out = []
        for a, (src, dst) in enumerate(zip(self.ins, self.recvs)):
            hr = src.shape[1] // 2
            out.append(pltpu.make_async_remote_copy(
                src_ref=src.at[:, pl.ds((1 - c) * hr, hr), :], dst_ref=dst, send_sem=self.send_sems.at[a],
                recv_sem=self.recv_sems.at[a], device_id=(x, y, 1 - c), device_id_type=MESH))
        return out

    def start(self):
        for cp in self._copies():
            cp.start()

    def finish(self):
        for cp in self._copies():
            cp.wait()


class _ScatterToChips:
    def __init__(self, ins, rbufs, send_sems, recv_sems):
        self.ins, self.rbufs, self.send_sems, self.recv_sems = ins, rbufs, send_sems, recv_sems

    @staticmethod
    def scratch(n):
        return [pltpu.SemaphoreType.DMA((3 * n,)), pltpu.SemaphoreType.DMA((3 * n,))]

    @staticmethod
    def out_shape(sums):
        return [jax.ShapeDtypeStruct((3,) + s.shape[1:], BF16) for s in sums]

    def _copies(self):
        x, y, c, chips = _mesh_pos()
        out = []
        for a, (src, dst) in enumerate(zip(self.ins, self.rbufs)):
            for k, chip in enumerate(chips):
                out.append(pltpu.make_async_remote_copy(
                    src_ref=src.at[2 * chip[0] + chip[1]], dst_ref=dst.at[k], send_sem=self.send_sems.at[3 * a + k],
                    recv_sem=self.recv_sems.at[3 * a + k], device_id=(*chip, c), device_id_type=MESH))
        return out

    def start(self):
        for cp in self._copies():
            cp.start()

    def finish(self):
        for cp in self._copies():
            cp.wait()


def _gather_halves(halves):
    n = len(halves)

    def body(*refs):
        ins, outs = refs[:n], refs[n:2 * n]
        send_sems, recv_sems = refs[2 * n:]
        x, y, c, _ = _mesh_pos()
        sib = (x, y, 1 - c)
        remote = [pltpu.make_async_remote_copy(src_ref=ins[a].at[c], dst_ref=outs[a].at[c],
                                               send_sem=send_sems.at[a], recv_sem=recv_sems.at[a],
                                               device_id=sib, device_id_type=MESH) for a in range(n)]
        for cp in remote:
            cp.start()
        for a in range(n):
            pltpu.make_async_remote_copy(src_ref=ins[a].at[1 - c], dst_ref=outs[a].at[1 - c], send_sem=send_sems.at[a],
                                         recv_sem=recv_sems.at[a], device_id=sib, device_id_type=MESH).wait_recv()
        for cp in remote:
            cp.wait_send()

    return pl.pallas_call(
        body, name="grad_gather_halves",
        out_shape=[jax.ShapeDtypeStruct(h.shape, F32) for h in halves],
        in_specs=[ANY] * n, out_specs=[ANY] * n, input_output_aliases={a: a for a in range(n)},
        scratch_shapes=[pltpu.SemaphoreType.DMA((n,)), pltpu.SemaphoreType.DMA((n,))],
    )(*halves)


SMALL_A_ROWS = 24
SMALL_B_ROWS = 8
SMALL_C_ROWS = N_POOL_GROUPS * POOL_GROUP


class _AllReduceSmall:
    N_IN = 10
    SHAPES = [(SMALL_A_ROWS, D_MODEL), (SMALL_B_ROWS, D_CONV), (SMALL_C_ROWS, POOL_GROUP)]

    def __init__(self, ins, outs, scratch):
        self.ins, self.outs = ins, outs
        self.bufs, self.rcvs, self.send_sems, self.recv_sems = scratch[:3], scratch[3:6], scratch[6], scratch[7]

    @classmethod
    def scratch(cls):
        return ([pltpu.VMEM((3,) + s, F32) for s in cls.SHAPES] + [pltpu.VMEM((3,) + s, F32) for s in cls.SHAPES]
                + [pltpu.SemaphoreType.DMA((9,)), pltpu.SemaphoreType.DMA((9,))])

    @classmethod
    def out_shape(cls):
        return [jax.ShapeDtypeStruct(s, F32) for s in cls.SHAPES]

    def _copies(self, st):
        x, y, c, _ = _mesh_pos()
        peer = [(x, y, 1 - c), (1 - x, y, c), (x, 1 - y, c)][st]
        return [pltpu.make_async_remote_copy(
            src_ref=buf.at[st], dst_ref=rcv.at[st], send_sem=self.send_sems.at[3 * st + i],
            recv_sem=self.recv_sems.at[3 * st + i], device_id=peer, device_id_type=MESH)
            for i, (buf, rcv) in enumerate(zip(self.bufs, self.rcvs))]

    def pack_and_send(self):
        dg1_ref, dg1m_ref, dg2_ref, dg3_ref, dg4_ref, loss_ref, dmeta_ref, dsc_ref, dcw_ref, dpw_ref = self.ins
        a_buf, b_buf, c_buf = self.bufs

        def rowsum(v):
            return jnp.sum(v, axis=0, keepdims=True)

        a_buf[0, 0:1, :] = rowsum(dg1_ref[...] + dg1m_ref[...])
        a_buf[0, 1:2, :] = rowsum(dg2_ref[...])
        a_buf[0, 2:3, :] = rowsum(dg3_ref[...])
        a_buf[0, 3:4, :] = rowsum(dg4_ref[...])
        loss = jnp.sum(rowsum(loss_ref[...]), axis=1, keepdims=True) * (0.5 / D_MODEL)
        a_buf[0, 4:5, :] = jnp.broadcast_to(loss, (1, D_MODEL))
        a_buf[0, 5:8, :] = jnp.zeros((3, D_MODEL), F32)
        a_buf[0, 8:24, :] = dmeta_ref[...]
        b_buf[0, 0:1, :] = rowsum(dsc_ref[...])
        for k in range(3):
            b_buf[0, 1 + k:2 + k, :] = rowsum(dcw_ref[8 * k:8 * k + 8, :])
        b_buf[0, 4:8, :] = jnp.zeros((4, D_CONV), F32)
        c_buf[0] = dpw_ref[...]
        for cp in self._copies(0):
            cp.start()

    def combine(self, st):
        for cp in self._copies(st):
            cp.wait()
        if st < 2:
            for buf, rcv in zip(self.bufs, self.rcvs):
                buf[st + 1] = buf[st] + rcv[st]
            for cp in self._copies(st + 1):
                cp.start()
        else:
            for out, buf, rcv in zip(self.outs, self.bufs, self.rcvs):
                out[...] = buf[st] + rcv[st]


def _row_block(rows):
    for cand in (512, 448, 384, 352, 320, 256, 128, 64, 32, 16):
        if rows % cand == 0:
            return cand
    return rows


def _add_pairs(grad, recv, place):
    n_sh, rows2, cols = grad.shape
    hr = rows2 // 2
    br = _row_block(hr)

    def body(place_ref, a_ref, b_ref, o_ref):
        o_ref[...] = (a_ref[0] + b_ref[...]).astype(BF16)

    return pl.pallas_call(
        body, name="grad_add_pairs",
        grid_spec=pltpu.PrefetchScalarGridSpec(
            num_scalar_prefetch=1, grid=(n_sh, hr // br),
            in_specs=[pl.BlockSpec((1, 1, br, cols), lambda j, i, p: (j, p[1], i, 0)),
                      pl.BlockSpec((1, br, cols), lambda j, i, p: (j, i, 0))],
            out_specs=pl.BlockSpec((1, br, cols), lambda j, i, p: (j, i, 0))),
        out_shape=jax.ShapeDtypeStruct((n_sh, hr, cols), BF16), compiler_params=_cparams(2),
    )(place, grad.reshape(n_sh, 2, hr, cols), recv)


def _add_chips(grads, recvs, rbufs, place, scattered=(), name="grad_add_chips"):
    n, n_sc = len(grads), len(scattered)
    n_sh, rows2, cols = grads[0].shape
    hr = rows2 // 2
    br = _row_block(hr)
    n_steps = hr // br

    def body(place_ref, *refs):
        a_refs, b_refs, r_refs = refs[:n], refs[n:2 * n], refs[2 * n:3 * n]
        o_refs = refs[3 * n + n_sc:4 * n + n_sc]
        if n_sc:
            scatter = _ScatterToChips(refs[3 * n:3 * n + n_sc], refs[4 * n + n_sc:4 * n + 2 * n_sc], *refs[-2:])

            @pl.when(pl.program_id(0) == 0)
            def _():
                scatter.start()

        for a_ref, b_ref, r_ref, o_ref in zip(a_refs, b_refs, r_refs, o_refs):
            own = a_ref[0, 0] + b_ref[0]
            o_ref[0] = ((own + r_ref[0].astype(F32)) + r_ref[1].astype(F32)) + r_ref[2].astype(F32)

        if n_sc:
            @pl.when(pl.program_id(0) == n_steps - 1)
            def _():
                scatter.finish()

    outs = pl.pallas_call(
        body, name=name,
        grid_spec=pltpu.PrefetchScalarGridSpec(
            num_scalar_prefetch=1, grid=(n_steps,),
            in_specs=[pl.BlockSpec((1, 1, br, cols), lambda i, p: (p[0], p[1], i, 0))] * n
            + [pl.BlockSpec((1, br, cols), lambda i, p: (p[0], i, 0))] * n
            + [pl.BlockSpec((3, br, cols), lambda i, p: (0, i, 0))] * n + [ANY] * n_sc,
            out_specs=[pl.BlockSpec((1, br, cols), lambda i, p: (p[1], i, 0))] * n + [ANY] * n_sc,
            scratch_shapes=_ScatterToChips.scratch(n_sc) if n_sc else []),
        out_shape=[jax.ShapeDtypeStruct((2, hr, cols), F32)] * n + _ScatterToChips.out_shape(list(scattered)),
        compiler_params=_cparams(1),
    )(place, *[g.reshape(n_sh, 2, hr, cols) for g in grads], *recvs, *rbufs, *scattered)
    return outs[:n], outs[n:]


def _adamw_math(w, g, m, v):
    m2 = ADAM_B1 * m + (1.0 - ADAM_B1) * g
    v2 = ADAM_B2 * v + (1.0 - ADAM_B2) * (g * g)
    m_hat = m2 / (1.0 - ADAM_B1 ** ADAM_STEP)
    v_hat = v2 / (1.0 - ADAM_B2 ** ADAM_STEP)
    delta = -ADAM_LR * (m_hat / (jnp.sqrt(v_hat) + ADAM_EPS) + ADAM_WD * w)
    return delta, m2, v2


def _adamw_big(w, g, m, v):
    rows, cols = w.shape
    br = _row_block(rows)

    def body(w_ref, g_ref, m_ref, v_ref, d_ref, m2_ref, v2_ref):
        d, m2, v2 = _adamw_math(w_ref[...], g_ref[...], m_ref[...], v_ref[...])
        d_ref[...] = d
        m2_ref[...] = m2
        v2_ref[...] = v2

    spec = pl.BlockSpec((br, cols), lambda i: (i, 0))
    return pl.pallas_call(
        body, name="adamw_big", grid=(rows // br,),
        out_shape=[jax.ShapeDtypeStruct((rows, cols), F32)] * 3,
        in_specs=[spec] * 4, out_specs=[spec] * 3, compiler_params=_cparams(1),
    )(w, g, m, v)


def _adamw_small(groups):
    n = len(groups)

    def body(*refs):
        ins, outs = refs[:4 * n], refs[4 * n:]
        for i in range(n):
            w, g, m, v = (r[...] for r in ins[4 * i:4 * i + 4])
            d, m2, v2 = _adamw_math(w, g, m, v)
            outs[3 * i][...] = d
            outs[3 * i + 1][...] = m2
            outs[3 * i + 2][...] = v2

    vm = pl.BlockSpec(memory_space=pltpu.VMEM)
    flat = [a for grp in groups for a in grp]
    out_shape = [jax.ShapeDtypeStruct(grp[0].shape, F32) for grp in groups for _ in range(3)]
    outs = pl.pallas_call(body, name="adamw_small", out_shape=out_shape,
                          in_specs=[vm] * (4 * n), out_specs=[vm] * (3 * n))(*flat)
    return [tuple(outs[3 * i:3 * i + 3]) for i in range(n)]


def _load_weights(pairs, sem):
    for src, dst in pairs:
        cp = pltpu.make_async_copy(src, dst, sem)
        cp.start()
        cp.wait()


def _meta_fwd(meta_full, g1, win_all):
    def body(meta_ref, g1_ref, win_ref, z_ref):
        xm = meta_ref[...]
        a = (xm * _rstd(xm) * g1_ref[...]).astype(BF16)
        for j in range(N_CHIPS):
            z_ref[:, j * IN_SHARD:(j + 1) * IN_SHARD] = _dot(a, win_ref[j])

    vm = pl.BlockSpec(memory_space=pltpu.VMEM)
    return pl.pallas_call(body, name="meta_fwd", out_shape=jax.ShapeDtypeStruct((N_META, D_IN_PROJ), F32),
                          in_specs=[vm] * 3, out_specs=vm)(meta_full, g1, win_all)


def _mixer_fwd(x3, zmeta, g1, g2, convw, poolw, pscale, win_all, wout, ffn_shards):
    n_seq, seq, _ = x3.shape
    tm = min(TM_MIX_FWD, seq)
    n_t = seq // tm
    n_steps = n_seq * n_t
    n_ag = len(ffn_shards)

    def body(x_ref, zm_ref, g1_ref, g2_ref, cw_ref, pw_ref, ps_ref, win_hbm, wout_hbm, *rest):
        ag = _AllGather(rest[:n_ag], rest[n_ag + 7:2 * n_ag + 7], *rest[-2:])
        z_ref, m_ref, h1_ref, a_ref, conv_ref, pooled_ref, yc_ref = rest[n_ag:n_ag + 7]
        win_v, wout_v, cvb, pb, sem = rest[2 * n_ag + 7:-2]
        s, t = pl.program_id(0), pl.program_id(1)
        step = s * n_t + t

        @pl.when(step == 0)
        def _():
            ag.start()
            _load_weights([(win_hbm, win_v), (wout_hbm, wout_v)], sem)

        for a in range(n_ag):
            @pl.when(step == ((a + 1) * n_steps) // (2 * n_ag + 2))
            def _():
                ag.relay(a)

        for a in range(n_ag):
            @pl.when(step == min(n_steps // 2 + ((a + 1) * n_steps) // (2 * n_ag + 2), n_steps - 1))
            def _():
                ag.forward(a)

        @pl.when(t == 0)
        def _():
            cvb[0:HALO, :] = zm_ref[:, IN_SHARD:2 * IN_SHARD] * zm_ref[:, 2 * IN_SHARD:3 * IN_SHARD]
            pb[0:HALO, :] = zm_ref[:, 3 * IN_SHARD:4 * IN_SHARD]

        @pl.when(t > 0)
        def _():
            cvb[0:HALO, :] = cvb[tm:tm + HALO, :]
            pb[0:HALO, :] = pb[tm:tm + HALO, :]

        xt = x_ref[0]
        a = (xt * _rstd(xt) * g1_ref[...]).astype(BF16)
        a_ref[...] = a
        zb = _dot(a, win_v[0])
        zc = _dot(a, win_v[1])
        zv = _dot(a, win_v[2])
        zp = _dot(a, win_v[3])
        z_ref[0, :, 0:IN_SHARD] = zb
        z_ref[0, :, IN_SHARD:2 * IN_SHARD] = zc
        z_ref[0, :, 2 * IN_SHARD:3 * IN_SHARD] = zv
        cv = zc * zv
        cvb[HALO:HALO + tm, :] = cv
        pb[HALO:HALO + tm, :] = zp
        cw = cw_ref[...]
        conv = cw[0:1] * cvb[HALO - 2:HALO - 2 + tm, :] + cw[1:2] * cvb[HALO - 1:HALO - 1 + tm, :] + cw[2:3] * cv
        conv_ref[...] = conv
        parts = [(zb * conv).astype(BF16)]
        for g in range(N_POOL_GROUPS):
            pooled = _pool_fwd(pb, g, tm).astype(BF16)
            pooled_ref[:, _gcols(g)] = pooled
            parts.append((_dot(pooled, pw_ref[g]) * ps_ref[:, _gcols(g)]).astype(BF16))
        ycat = jnp.concatenate(parts, axis=1)
        yc_ref[...] = ycat
        m = _dot(ycat, wout_v[...])
        m_ref[0] = m
        h1_ref[0] = xt + m * _rstd(m) * g2_ref[...]

        @pl.when(step == n_steps - 1)
        def _():
            ag.finish()

    n_rows = n_seq * seq
    row = lambda c: pl.BlockSpec((1, tm, c), lambda s, t: (s, t, 0))
    row2 = lambda c: pl.BlockSpec((tm, c), lambda s, t: (s * n_t + t, 0))
    outs = pl.pallas_call(
        body, name="mixer_fwd", grid=(n_seq, n_t),
        out_shape=[jax.ShapeDtypeStruct((n_seq, seq, D_Z), F32), jax.ShapeDtypeStruct((n_seq, seq, D_MODEL), F32),
                   jax.ShapeDtypeStruct((n_seq, seq, D_MODEL), F32), jax.ShapeDtypeStruct((n_rows, D_MODEL), BF16),
                   jax.ShapeDtypeStruct((n_rows, D_CONV), F32), jax.ShapeDtypeStruct((n_rows, D_POOL), BF16),
                   jax.ShapeDtypeStruct((n_rows, D_MODEL), BF16)] + _AllGather.out_shape(ffn_shards),
        in_specs=[row(D_MODEL), _full((N_META, D_IN_PROJ)), _full((1, D_MODEL)), _full((1, D_MODEL)),
                  _full((3, D_CONV)), _full((N_POOL_GROUPS, POOL_GROUP, POOL_GROUP)), _full((1, D_POOL)), ANY, ANY]
        + [ANY] * n_ag,
        out_specs=[row(D_Z), row(D_MODEL), row(D_MODEL), row2(D_MODEL), row2(D_CONV), row2(D_POOL), row2(D_MODEL)]
        + [ANY] * n_ag,
        scratch_shapes=[pltpu.VMEM((N_CHIPS, D_MODEL, IN_SHARD), BF16), pltpu.VMEM((D_MODEL, D_MODEL), BF16),
                        pltpu.VMEM((HALO + tm, D_CONV), F32), pltpu.VMEM((HALO + tm, D_POOL), F32),
                        pltpu.SemaphoreType.DMA] + _AllGather.scratch(n_ag),
        compiler_params=_cparams(2),
    )(x3, zmeta, g1, g2, convw, poolw, pscale, win_all, wout, *ffn_shards)
    return outs[:7], _fill_own_slot(outs[7:], ffn_shards)


def _ffn_chunks():
    out, r0 = [], 0
    while r0 < D_FF:
        out.append((r0, min(FF_CHUNK, D_FF - r0)))
        r0 += FF_CHUNK
    return out


def _ffn_fwd_bwd(h1, target, g3, g4, wg_t, wu_t, wd):
    n_rows = h1.shape[0]
    tm = min(TM_FFN, n_rows)
    chunks = _ffn_chunks()

    def body(h1_ref, t_ref, g3_ref, g4_ref, wg_hbm, wu_hbm, wd_hbm,
             dh1_ref, f_ref, dd_ref, ds_ref, du_ref, gg_ref, loss_ref, dg3_ref, dg4_ref,
             wg_v, wu_v, wd_v, s_sc, u_sc, sem):
        @pl.when(pl.program_id(0) == 0)
        def _():
            _load_weights([(wg_hbm, wg_v), (wu_hbm, wu_v), (wd_hbm, wd_v)], sem)
            loss_ref[...] = jnp.zeros_like(loss_ref)
            dg3_ref[...] = jnp.zeros_like(dg3_ref)
            dg4_ref[...] = jnp.zeros_like(dg4_ref)

        h1v = h1_ref[...]
        r3 = _rstd(h1v)
        hh = h1v * r3
        g3v, g4v = g3_ref[...], g4_ref[...]
        f = (hh * g3v).astype(BF16)
        f_ref[...] = f
        d = jnp.zeros((tm, D_MODEL), F32)
        for r0, sz in chunks:
            s = _dot_nt(f, wg_v[r0:r0 + sz, :])
            u = _dot_nt(f, wu_v[r0:r0 + sz, :])
            s_sc[:, r0:r0 + sz] = s
            u_sc[:, r0:r0 + sz] = u
            gc = (s * _sigmoid(s) * u).astype(BF16)
            gg_ref[:, r0:r0 + sz] = gc
            d = d + _dot(gc, wd_v[r0:r0 + sz, :])
        r4 = _rstd(d)
        dh = d * r4
        err = (h1v + dh * g4v) - t_ref[...]
        loss_ref[...] += _rows8(err * err)
        dy = err * (1.0 / D_MODEL)
        dg4_ref[...] += _rows8(dy * dh)
        ddb = _rms_bwd(dy, dh, r4, g4v).astype(BF16)
        dd_ref[...] = ddb
        df = jnp.zeros((tm, D_MODEL), F32)
        for r0, sz in chunks:
            dgg = _dot_nt(ddb, wd_v[r0:r0 + sz, :])
            s = s_sc[:, r0:r0 + sz]
            u = u_sc[:, r0:r0 + sz]
            sig = _sigmoid(s)
            dsc = (dgg * u * (sig * (1.0 + s * (1.0 - sig)))).astype(BF16)
            duc = (dgg * (s * sig)).astype(BF16)
            ds_ref[:, r0:r0 + sz] = dsc
            du_ref[:, r0:r0 + sz] = duc
            df = df + _dot(dsc, wg_v[r0:r0 + sz, :]) + _dot(duc, wu_v[r0:r0 + sz, :])
        dg3_ref[...] += _rows8(df * hh)
        dh1_ref[...] = dy + _rms_bwd(df, hh, r3, g3v)

    row = pl.BlockSpec((tm, D_MODEL), lambda i: (i, 0))
    ffrow = pl.BlockSpec((tm, D_FF), lambda i: (i, 0))
    acc = _full((8, D_MODEL))
    act_bf = jax.ShapeDtypeStruct((n_rows, D_MODEL), BF16)
    ff_bf = jax.ShapeDtypeStruct((n_rows, D_FF), BF16)
    acc_shape = jax.ShapeDtypeStruct((8, D_MODEL), F32)
    w_vmem = pltpu.VMEM((D_FF, D_MODEL), BF16)
    return pl.pallas_call(
        body, name="ffn_fwd_bwd", grid=(n_rows // tm,),
        out_shape=[jax.ShapeDtypeStruct((n_rows, D_MODEL), F32), act_bf, act_bf, ff_bf, ff_bf, ff_bf,
                   acc_shape, acc_shape, acc_shape],
        in_specs=[row, row, _full((1, D_MODEL)), _full((1, D_MODEL)), ANY, ANY, ANY],
        out_specs=[row, row, row, ffrow, ffrow, ffrow, acc, acc, acc],
        scratch_shapes=[w_vmem, w_vmem, w_vmem, pltpu.VMEM((tm, D_FF), F32), pltpu.VMEM((tm, D_FF), F32),
                        pltpu.SemaphoreType.DMA],
        compiler_params=_cparams(1),
    )(h1, target, g3, g4, wg_t, wu_t, wd)


def _ffn_weight_grads(name, acts, other, exchanged):
    n_rows = other.shape[0]
    n_a, n_ex = len(acts), len(exchanged)
    n_c = n_a
    tk = min(TK_DW, n_rows)
    n_k = n_rows // tk
    half = D_FF // n_c

    def body(other_ref, *rest):
        act_refs = rest[:n_a]
        out_refs = rest[n_a + n_ex:2 * n_a + n_ex]
        c, k = pl.program_id(0), pl.program_id(1)
        if n_ex:
            ex = _ExchangeHalves(rest[n_a:n_a + n_ex], rest[2 * n_a + n_ex:2 * n_a + 2 * n_ex], *rest[-2:])

            @pl.when((c == 0) & (k == 0))
            def _():
                ex.start()

        @pl.when(k == 0)
        def _():
            for o in out_refs:
                o[...] = jnp.zeros_like(o)

        ov = other_ref[...]
        for a, o in zip(act_refs, out_refs):
            o[...] += _dot_tn(a[...], ov)

        if n_ex:
            @pl.when((c == n_c - 1) & (k == n_k - 1))
            def _():
                ex.finish()

    row = pl.BlockSpec((tk, D_MODEL), lambda c, k: (k, 0))
    ffrow = pl.BlockSpec((tk, half), lambda c, k: (k, c))
    out = pl.BlockSpec((half, D_MODEL), lambda c, k: (c, 0))
    outs = pl.pallas_call(
        body, name=name, grid=(n_c, n_k),
        out_shape=[jax.ShapeDtypeStruct((D_FF, D_MODEL), F32)] * n_a + _ExchangeHalves.out_shape(exchanged),
        in_specs=[row] + [ffrow] * n_a + [ANY] * n_ex, out_specs=[out] * n_a + [ANY] * n_ex,
        scratch_shapes=_ExchangeHalves.scratch(n_ex) if n_ex else [],
        compiler_params=_cparams(2),
    )(other, *acts, *exchanged)
    return outs[:n_a], outs[n_a:]


def _mixer_bwd(dh1, m3, z3, conv2, pooled2, x3, zmeta, g1, g2, convw, poolw, pscale, win_all, wout, exchanged,
               scattered):
    n_seq, seq, _ = x3.shape
    tm = min(TM_MIX_BWD, seq)
    sub = min(SUB_MIX_BWD, tm)
    n_t = seq // tm
    n_ex, n_sc = len(exchanged), len(scattered)
    n_cm = n_ex + n_sc

    def body(dh1_ref, m_ref, z_ref, conv_ref, pooled_ref, x_ref, zm_ref, g1_ref, g2_ref, cw_ref, pw_ref, ps_ref,
             win_hbm, wout_hbm, *rest):
        outs0 = n_cm + 9
        ex = _ExchangeHalves(rest[:n_ex], rest[outs0:outs0 + n_ex], *rest[-4:-2])
        sc = _ScatterToChips(rest[n_ex:n_cm], rest[outs0 + n_ex:outs0 + n_cm], *rest[-2:])
        dx_ref, dz_ref, dm_ref, dg1_ref, dg2_ref, dsc_ref, dcw_ref, dpw_ref, dzm_ref = rest[n_cm:outs0]
        win_v, wout_v, dcb, dqb, mcb, mqb, sem = rest[outs0 + n_cm:-4]
        s, i = pl.program_id(0), pl.program_id(1)
        tr = n_t - 1 - i

        @pl.when((s == 0) & (i == 0))
        def _():
            sc.start()
            ex.start()
            _load_weights([(win_hbm, win_v), (wout_hbm, wout_v)], sem)
            for ref in (dg1_ref, dg2_ref, dsc_ref, dcw_ref, dpw_ref, dzm_ref):
                ref[...] = jnp.zeros_like(ref)

        @pl.when(i == 0)
        def _():
            dcb[tm:tm + HALO, :] = jnp.zeros((HALO, D_CONV), F32)
            dqb[tm:tm + HALO, :] = jnp.zeros((HALO, D_POOL), F32)

        @pl.when(i > 0)
        def _():
            dcb[tm:tm + HALO, :] = dcb[0:HALO, :]
            dqb[tm:tm + HALO, :] = dqb[0:HALO, :]

        g1v, g2v = g1_ref[...], g2_ref[...]
        cw = cw_ref[...]

        for r0 in range(tm - sub, -1, -sub):
            rows = slice(r0, r0 + sub)
            dh1v = dh1_ref[0, rows, :]
            mv = m_ref[0, rows, :]
            r2 = _rstd(mv)
            mh = mv * r2
            dg2_ref[...] += _rows8(dh1v * mh)
            dmb = _rms_bwd(dh1v, mh, r2, g2v).astype(BF16)
            dm_ref[rows, :] = dmb
            dyc = _dot_nt(dmb, wout_v[...])
            dyconv = dyc[:, 0:D_CONV]

            for g in range(N_POOL_GROUPS):
                pooled = pooled_ref[rows, _gcols(g)]
                mixed = _dot(pooled, pw_ref[g])
                scale = ps_ref[:, _gcols(g)]
                dyp = dyc[:, D_CONV + g * POOL_GROUP:D_CONV + (g + 1) * POOL_GROUP]
                dsc_ref[:, _gcols(g)] += _rows8(dyp * mixed)
                dmix = (dyp * scale).astype(BF16)
                dpw_ref[g] += _dot_tn(pooled, dmix)
                dqb[rows, _gcols(g)] = _dot_nt(dmix, pw_ref[g])

            zb = z_ref[0, rows, 0:IN_SHARD]
            zc = z_ref[0, rows, IN_SHARD:2 * IN_SHARD]
            zv = z_ref[0, rows, 2 * IN_SHARD:3 * IN_SHARD]
            dconv = dyconv * zb
            dcb[rows, :] = dconv
            d1 = dcb[r0 + 1:r0 + 1 + sub, :]
            d2 = dcb[r0 + 2:r0 + 2 + sub, :]
            dcv = cw[2:3] * dconv + cw[1:2] * d1 + cw[0:1] * d2
            cv = zc * zv
            dcw_ref[0:8, :] += _rows8(cv * d2)
            dcw_ref[8:16, :] += _rows8(cv * d1)
            dcw_ref[16:24, :] += _rows8(cv * dconv)
            dzs = [(dyconv * conv_ref[rows, :]).astype(BF16), (dcv * zv).astype(BF16), (dcv * zc).astype(BF16),
                   jnp.concatenate([_pool_bwd(dqb, g, r0, sub) for g in range(N_POOL_GROUPS)], axis=1).astype(BF16)]
            da = jnp.zeros((sub, D_MODEL), F32)
            for j in range(N_CHIPS):
                dz_ref[j, rows, :] = dzs[j]
                da = da + _dot_nt(dzs[j], win_v[j])
            xt = x_ref[0, rows, :]
            r1 = _rstd(xt)
            xh = xt * r1
            dg1_ref[...] += _rows8(da * xh)
            dx_ref[0, rows, :] = dh1v + _rms_bwd(da, xh, r1, g1v)

        @pl.when(tr == 0)
        def _():
            mcb[0:HALO, :] = jnp.zeros((HALO, D_CONV), F32)
            mqb[0:HALO, :] = jnp.zeros((HALO, D_POOL), F32)
            mcb[HALO:2 * HALO, :] = dcb[0:HALO, :]
            mqb[HALO:2 * HALO, :] = dqb[0:HALO, :]
            m1 = mcb[1:1 + HALO, :]
            m2 = mcb[2:2 + HALO, :]
            zc_m = zm_ref[:, IN_SHARD:2 * IN_SHARD]
            zv_m = zm_ref[:, 2 * IN_SHARD:3 * IN_SHARD]
            cv_m = zc_m * zv_m
            dcw_ref[0:8, :] += _rows8(cv_m * m2)
            dcw_ref[8:16, :] += _rows8(cv_m * m1)
            dcv_m = cw[1:2] * m1 + cw[0:1] * m2
            dzm_ref[:, IN_SHARD:2 * IN_SHARD] += dcv_m * zv_m
            dzm_ref[:, 2 * IN_SHARD:3 * IN_SHARD] += dcv_m * zc_m
            dzm_ref[:, 3 * IN_SHARD:4 * IN_SHARD] += jnp.concatenate(
                [_pool_bwd(mqb, g, 0, HALO) for g in range(N_POOL_GROUPS)], axis=1)

        @pl.when((s == n_seq - 1) & (i == n_t - 1))
        def _():
            ex.finish()
            sc.finish()

    row3 = lambda c: pl.BlockSpec((1, tm, c), lambda s, i: (s, n_t - 1 - i, 0))
    row2 = lambda c: pl.BlockSpec((tm, c), lambda s, i: (s * n_t + n_t - 1 - i, 0))
    n_rows = n_seq * seq
    outs = pl.pallas_call(
        body, name="mixer_bwd", grid=(n_seq, n_t),
        out_shape=[jax.ShapeDtypeStruct((n_seq, seq, D_MODEL), F32),
                   jax.ShapeDtypeStruct((N_CHIPS, n_rows, IN_SHARD), BF16), jax.ShapeDtypeStruct((n_rows, D_MODEL), BF16),
                   jax.ShapeDtypeStruct((8, D_MODEL), F32), jax.ShapeDtypeStruct((8, D_MODEL), F32),
                   jax.ShapeDtypeStruct((8, D_POOL), F32), jax.ShapeDtypeStruct((24, D_CONV), F32),
                   jax.ShapeDtypeStruct((N_POOL_GROUPS, POOL_GROUP, POOL_GROUP), F32),
                   jax.ShapeDtypeStruct((N_META, D_IN_PROJ), F32)]
        + _ExchangeHalves.out_shape(exchanged) + _ScatterToChips.out_shape(scattered),
        in_specs=[row3(D_MODEL), row3(D_MODEL), row3(D_Z), row2(D_CONV), row2(D_POOL), row3(D_MODEL),
                  _full((N_META, D_IN_PROJ)), _full((1, D_MODEL)), _full((1, D_MODEL)), _full((3, D_CONV)),
                  _full((N_POOL_GROUPS, POOL_GROUP, POOL_GROUP)), _full((1, D_POOL)), ANY, ANY] + [ANY] * n_cm,
        out_specs=[row3(D_MODEL), pl.BlockSpec((N_CHIPS, tm, IN_SHARD), lambda s, i: (0, s * n_t + n_t - 1 - i, 0)),
                   row2(D_MODEL),
                   _full((8, D_MODEL)), _full((8, D_MODEL)), _full((8, D_POOL)), _full((24, D_CONV)),
                   _full((N_POOL_GROUPS, POOL_GROUP, POOL_GROUP)), _full((N_META, D_IN_PROJ))] + [ANY] * n_cm,
        scratch_shapes=[pltpu.VMEM((N_CHIPS, D_MODEL, IN_SHARD), BF16), pltpu.VMEM((D_MODEL, D_MODEL), BF16),
                        pltpu.VMEM((tm + HALO, D_CONV), F32), pltpu.VMEM((tm + HALO, D_POOL), F32),
                        pltpu.VMEM((2 * HALO, D_CONV), F32), pltpu.VMEM((2 * HALO, D_POOL), F32),
                        pltpu.SemaphoreType.DMA] + _ExchangeHalves.scratch(n_ex) + _ScatterToChips.scratch(n_sc),
        compiler_params=_cparams(2),
    )(dh1, m3, z3, conv2, pooled2, x3, zmeta, g1, g2, convw, poolw, pscale, win_all, wout, *exchanged, *scattered)
    return outs[:9], outs[9:9 + n_ex], outs[9 + n_ex:]


def _meta_bwd(dzm, meta_full, g1, win_all):
    def body(dzm_ref, meta_ref, g1_ref, win_ref, dmeta_ref, dg1_ref, a_ref, dzb_ref):
        xm = meta_ref[...]
        r = _rstd(xm)
        xh = xm * r
        g1v = g1_ref[...]
        a_ref[...] = (xh * g1v).astype(BF16)
        da = jnp.zeros((N_META, D_MODEL), F32)
        for j in range(N_CHIPS):
            dzj = dzm_ref[:, j * IN_SHARD:(j + 1) * IN_SHARD].astype(BF16)
            dzb_ref[j] = dzj
            da = da + _dot_nt(dzj, win_ref[j])
        dg1_ref[...] = _rows8(da * xh)
        dmeta_ref[...] = _rms_bwd(da, xh, r, g1v)

    vm = pl.BlockSpec(memory_space=pltpu.VMEM)
    return pl.pallas_call(
        body, name="meta_bwd",
        out_shape=[jax.ShapeDtypeStruct((N_META, D_MODEL), F32), jax.ShapeDtypeStruct((8, D_MODEL), F32),
                   jax.ShapeDtypeStruct((N_META, D_MODEL), BF16), jax.ShapeDtypeStruct((N_CHIPS, N_META, IN_SHARD), BF16)],
        in_specs=[vm] * 4, out_specs=[vm] * 4,
    )(dzm, meta_full, g1, win_all)


def _w_out_grad(ycat, dm, small):
    n_rows = ycat.shape[0]
    tk = min(TK_DW, n_rows)
    n_k = n_rows // tk
    n_sm = _AllReduceSmall.N_IN

    def body(yc_ref, dm_ref, *rest):
        dwout_ref = rest[n_sm]
        reduce_small = _AllReduceSmall(rest[:n_sm], rest[n_sm + 1:n_sm + 4], rest[n_sm + 4:])
        k = pl.program_id(0)

        @pl.when(k == 0)
        def _():
            reduce_small.pack_and_send()
            dwout_ref[...] = jnp.zeros_like(dwout_ref)

        for st in range(2):
            @pl.when(k == ((st + 1) * n_k) // 3)
            def _():
                reduce_small.combine(st)

        dwout_ref[...] += _dot_tn(yc_ref[...], dm_ref[...])

        @pl.when(k == n_k - 1)
        def _():
            reduce_small.combine(2)

    row = pl.BlockSpec((tk, D_MODEL), lambda k: (k, 0))
    outs = pl.pallas_call(
        body, name="mixer_weight_grads_out", grid=(n_k,),
        out_shape=[jax.ShapeDtypeStruct((D_MODEL, D_MODEL), F32)] + _AllReduceSmall.out_shape(),
        in_specs=[row, row] + [_full(s.shape) for s in small],
        out_specs=[_full((D_MODEL, D_MODEL))] + [_full(s) for s in _AllReduceSmall.SHAPES],
        scratch_shapes=_AllReduceSmall.scratch(),
        compiler_params=_cparams(1),
    )(ycat, dm, *small)
    return outs[0].reshape(N_CHIPS, OUT_SHARD, D_MODEL), outs[1:]


def _w_in_grad(a, dz, a_meta, dz_meta, exchanged, scattered):
    n_rows = a.shape[0]
    tk = min(TK_DW, n_rows)
    n_k = n_rows // tk
    n_ex, n_sc = len(exchanged), len(scattered)
    n_cm = n_ex + n_sc

    def body(a_ref, dz_ref, am_ref, dzm_ref, *rest):
        dwin_ref = rest[n_cm]
        exchange = _ExchangeHalves(rest[:n_ex], rest[n_cm + 1:n_cm + 1 + n_ex], *rest[-4:-2])
        scatter = _ScatterToChips(rest[n_ex:n_cm], rest[n_cm + 1 + n_ex:2 * n_cm + 1], *rest[-2:])
        k = pl.program_id(0)

        @pl.when(k == 0)
        def _():
            scatter.start()
            exchange.start()
            am_t = am_ref[...].T
            for j in range(N_CHIPS):
                dwin_ref[j] = _dot(am_t, dzm_ref[j])

        a_t = a_ref[...].T
        for j in range(N_CHIPS):
            dwin_ref[j] += _dot(a_t, dz_ref[j])

        @pl.when(k == n_k - 1)
        def _():
            exchange.finish()
            scatter.finish()

    row = pl.BlockSpec((tk, D_MODEL), lambda k: (k, 0))
    outs = pl.pallas_call(
        body, name="mixer_weight_grads_in", grid=(n_k,),
        out_shape=[jax.ShapeDtypeStruct((N_CHIPS, D_MODEL, IN_SHARD), F32)]
        + _ExchangeHalves.out_shape(exchanged) + _ScatterToChips.out_shape(scattered),
        in_specs=[row, pl.BlockSpec((N_CHIPS, tk, IN_SHARD), lambda k: (0, k, 0)),
                  _full((N_META, D_MODEL)), _full((N_CHIPS, N_META, IN_SHARD))] + [ANY] * n_cm,
        out_specs=[_full((N_CHIPS, D_MODEL, IN_SHARD))] + [ANY] * n_cm,
        scratch_shapes=_ExchangeHalves.scratch(n_ex) + _ScatterToChips.scratch(n_sc),
        compiler_params=_cparams(1),
    )(a, dz, a_meta, dz_meta, *exchanged, *scattered)
    return outs[0], outs[1:1 + n_ex], outs[1 + n_ex:]


def _exchange_and_scatter(exchanged, scattered):
    n_ex, n_sc = len(exchanged), len(scattered)
    n_cm = n_ex + n_sc

    def body(*refs):
        exchange = _ExchangeHalves(refs[:n_ex], refs[n_cm:n_cm + n_ex], *refs[-4:-2])
        scatter = _ScatterToChips(refs[n_ex:n_cm], refs[n_cm + n_ex:2 * n_cm], *refs[-2:])
        scatter.start()
        exchange.start()
        exchange.finish()
        scatter.finish()

    outs = pl.pallas_call(
        body, name="grad_exchange_and_scatter",
        out_shape=_ExchangeHalves.out_shape(exchanged) + _ScatterToChips.out_shape(scattered),
        in_specs=[ANY] * n_cm, out_specs=[ANY] * n_cm,
        scratch_shapes=_ExchangeHalves.scratch(n_ex) + _ScatterToChips.scratch(n_sc),
    )(*exchanged, *scattered)
    return outs[:n_ex], outs[n_ex:]


def kernel(x, meta_tokens, norm_mix_pre, w_in, conv_w, pool_w, pool_scale, w_out, norm_mix_post, norm_ffn_pre, w_gate, w_up, w_down, norm_ffn_post, loss_target, m_meta_tokens, m_norm_mix_pre, m_w_in, m_conv_w, m_pool_w, m_pool_scale, m_w_out, m_norm_mix_post, m_norm_ffn_pre, m_w_gate, m_w_up, m_w_down, m_norm_ffn_post, v_meta_tokens, v_norm_mix_pre, v_w_in, v_conv_w, v_pool_w, v_pool_scale, v_w_out, v_norm_mix_post, v_norm_ffn_pre, v_w_gate, v_w_up, v_w_down, v_norm_ffn_post):
    n_seq, seq, _ = x.shape
    n_rows = n_seq * seq
    chip = 2 * lax.axis_index("x") + lax.axis_index("y")
    meta_cols = D_MODEL // N_CHIPS
    conv_cols = D_CONV // N_CHIPS

    small = jnp.zeros((2 * HALO, meta_cols), F32)
    small = small.at[0:N_META, :].set(meta_tokens).at[N_META:N_META + 3, 0:conv_cols].set(conv_w[0])
    win_all, wout_all, small_all = _all_gather_shards([w_in[0].astype(BF16), w_out[0].astype(BF16), small])
    meta_full = small_all[:, 0:N_META, :].transpose(1, 0, 2).reshape(N_META, D_MODEL)
    conv_full = small_all[:, N_META:N_META + 3, 0:conv_cols].transpose(1, 0, 2).reshape(3, D_CONV)
    wout_full = wout_all.reshape(D_MODEL, D_MODEL)
    poolw_bf = pool_w[0].astype(BF16)
    pscale = pool_scale
    g1, g2, g3, g4 = norm_mix_pre, norm_mix_post, norm_ffn_pre, norm_ffn_post
    place = jnp.stack([chip, lax.axis_index("c")]).astype(jnp.int32)

    zmeta = _meta_fwd(meta_full, g1, win_all)
    (z3, m3, h1, a_bf, conv2, pooled2, yc_bf), ffn_w = _mixer_fwd(
        x, zmeta, g1, g2, conv_full, poolw_bf, pscale, win_all, wout_full,
        [w_gate[0].T.astype(BF16), w_up[0].T.astype(BF16), w_down[0].astype(BF16)])
    wg_t, wu_t, wd_full = [w.reshape(D_FF, D_MODEL) for w in ffn_w]
    dh1, f_bf, dd_bf, ds_bf, du_bf, gg_bf, lossp, dg3p, dg4p = _ffn_fwd_bwd(
        h1.reshape(n_rows, D_MODEL), loss_target.reshape(n_rows, D_MODEL), g3, g4, wg_t, wu_t, wd_full)
    as_shards = lambda g: g.reshape(N_CHIPS, FF_SHARD, D_MODEL)
    (dwd,), _ = _ffn_weight_grads("ffn_weight_grads_down", [gg_bf], dd_bf, [])
    dwd = as_shards(dwd)
    (dwg_t,), (dwd_recv,) = _ffn_weight_grads("ffn_weight_grads_gate", [ds_bf], f_bf, [dwd])
    dwg_t = as_shards(dwg_t)
    (dwu_t,), (dwg_recv,) = _ffn_weight_grads("ffn_weight_grads_up", [du_bf], f_bf, [dwg_t])
    dwu_t = as_shards(dwu_t)
    ((grad_x, dz_bf, dm_bf, dg1p, dg2p, dscp, dcwp, dpw, dzm), (dwu_recv,), (dwd_rbuf, dwg_rbuf)) = _mixer_bwd(
        dh1.reshape(n_seq, seq, D_MODEL), m3, z3, conv2, pooled2, x, zmeta, g1, g2, conv_full, poolw_bf, pscale,
        win_all, wout_full, [dwu_t], [_add_pairs(dwd, dwd_recv, place), _add_pairs(dwg_t, dwg_recv, place)])
    dmeta, dg1m, a_meta, dz_meta = _meta_bwd(dzm, meta_full, g1, win_all)
    dwout, (a_red, b_red, c_red) = _w_out_grad(
        yc_bf, dm_bf,
        [dg1p, dg1m, dg2p, dg3p, dg4p, lossp, dmeta, dscp, dcwp, dpw.reshape(SMALL_C_ROWS, POOL_GROUP)])
    dwin, (dwout_recv,), (dwu_rbuf,) = _w_in_grad(
        a_bf, dz_bf, a_meta, dz_meta, [dwout], [_add_pairs(dwu_t, dwu_recv, place)])
    (dwin_recv,), (dwout_rbuf,) = _exchange_and_scatter([dwin], [_add_pairs(dwout, dwout_recv, place)])
    ffn_red, (dwin_rbuf,) = _add_chips([dwg_t, dwu_t, dwd], [dwg_recv, dwu_recv, dwd_recv],
                                       [dwg_rbuf, dwu_rbuf, dwd_rbuf], place,
                                       [_add_pairs(dwin, dwin_recv, place)], name="grad_add_chips_ffn")
    mix_red = [_add_chips([g], [r], [rb], place)[0][0]
               for g, r, rb in ((dwin, dwin_recv, dwin_rbuf), (dwout, dwout_recv, dwout_rbuf))]
    reduced = _gather_halves(mix_red + list(ffn_red))
    g_win, g_wout, g_wg_t, g_wu_t, g_wd = [r.reshape(2 * r.shape[1], r.shape[2]) for r in reduced]

    loss = a_red[4, 0]
    g_g1, g_g2, g_g3, g_g4 = a_red[0:1], a_red[1:2], a_red[2:3], a_red[3:4]
    g_meta = lax.dynamic_slice(a_red, (8, chip * meta_cols), (N_META, meta_cols))
    g_pscale = b_red[0:1]
    g_conv = lax.dynamic_slice(b_red, (1, chip * conv_cols), (3, conv_cols))
    g_poolw = c_red

    big = [(w_in[0], g_win, m_w_in[0], v_w_in[0]), (w_out[0], g_wout, m_w_out[0], v_w_out[0]),
           (w_gate[0].T, g_wg_t, m_w_gate[0].T, v_w_gate[0].T), (w_up[0].T, g_wu_t, m_w_up[0].T, v_w_up[0].T),
           (w_down[0], g_wd, m_w_down[0], v_w_down[0])]
    big_out = [_adamw_big(w, g, m, v) for (w, g, m, v) in big]
    big_out[2] = [o.T for o in big_out[2]]
    big_out[3] = [o.T for o in big_out[3]]
    g_wg, g_wu = g_wg_t.T, g_wu_t.T
    small_groups = [
        (meta_tokens, g_meta, m_meta_tokens, v_meta_tokens),
        (g1, g_g1, m_norm_mix_pre, v_norm_mix_pre),
        (conv_w[0], g_conv, m_conv_w[0], v_conv_w[0]),
        (pool_w.reshape(SMALL_C_ROWS, POOL_GROUP), g_poolw, m_pool_w.reshape(SMALL_C_ROWS, POOL_GROUP),
         v_pool_w.reshape(SMALL_C_ROWS, POOL_GROUP)),
        (pool_scale, g_pscale, m_pool_scale, v_pool_scale),
        (g2, g_g2, m_norm_mix_post, v_norm_mix_post),
        (g3, g_g3, m_norm_ffn_pre, v_norm_ffn_pre),
        (g4, g_g4, m_norm_ffn_post, v_norm_ffn_post),
    ]
    small_out = _adamw_small(small_groups)

    grads_out = [g_meta, g_g1, g_win[None], g_conv[None], g_poolw.reshape(pool_w.shape), g_pscale, g_wout[None],
                 g_g2, g_g3, g_wg[None], g_wu[None], g_wd[None], g_g4]
    s_meta, s_g1, s_conv, s_poolw, s_pscale, s_g2, s_g3, s_g4 = small_out
    b_win, b_wout, b_wg, b_wu, b_wd = big_out

    def leaf(k):
        return [s_meta[k], s_g1[k], b_win[k][None], s_conv[k][None], s_poolw[k].reshape(pool_w.shape), s_pscale[k],
                b_wout[k][None], s_g2[k], s_g3[k], b_wg[k][None], b_wu[k][None], b_wd[k][None], s_g4[k]]

    return (loss, grad_x, *grads_out, *leaf(0), *leaf(1), *leaf(2))
```

```python
import jax
import jax.numpy as jnp
from jax import lax
from jax.experimental import pallas as pl
from jax.experimental.pallas import tpu as pltpu

F32 = jnp.float32
BF16 = jnp.bfloat16
MESH = pl.DeviceIdType.MESH

D_MODEL = 1024
D_CONV = 512
D_POOL = 512
POOL_GROUP = 128
N_POOL_GROUPS = 4
D_IN_PROJ = 2048
D_FF = 2816
N_CHIPS = 4
FF_SHARD = D_FF // N_CHIPS
IN_SHARD = D_IN_PROJ // N_CHIPS
OUT_SHARD = D_MODEL // N_CHIPS
D_Z = 3 * IN_SHARD
N_META = 16
HALO = 16
RMS_EPS = 1e-6

ADAM_LR = 0.001
ADAM_B1 = 0.9
ADAM_B2 = 0.999
ADAM_EPS = 1e-08
ADAM_WD = 0.01
ADAM_STEP = 10

TM_MIX_FWD = 512
TM_MIX_BWD = 512
SUB_MIX_BWD = 512
TM_FFN = 256
TK_DW = 1024
FF_CHUNK = 1024
VMEM_LIMIT = 56 * 1024 * 1024


def _cparams(n_grid):
    return pltpu.CompilerParams(dimension_semantics=("arbitrary",) * n_grid, vmem_limit_bytes=VMEM_LIMIT)


def _dot(a, b):
    return jnp.dot(a, b, preferred_element_type=F32)


def _dot_nt(a, b):
    return lax.dot_general(a, b, (((1,), (1,)), ((), ())), preferred_element_type=F32)


def _dot_tn(a, b):
    return lax.dot_general(a, b, (((0,), (0,)), ((), ())), preferred_element_type=F32)


def _rows8(v):
    r, c = v.shape
    return v.reshape(r // 8, 8, c).sum(axis=0)


def _rstd(v):
    return lax.rsqrt(jnp.mean(v * v, axis=-1, keepdims=True) + RMS_EPS)


def _rms_bwd(dy, xhat, rstd, gain):
    dyg = dy * gain
    return rstd * (dyg - xhat * jnp.mean(dyg * xhat, axis=-1, keepdims=True))


def _sigmoid(v):
    return 1.0 / (1.0 + jnp.exp(-v))


def _gcols(g):
    return slice(g * POOL_GROUP, (g + 1) * POOL_GROUP)


def _window_sum(e, g, ahead):
    n = e.shape[0]
    w = e
    for level in range(g + 1):
        shift = 1 << level
        w = w + pltpu.roll(w, (n - shift) if ahead else shift, 0)
    return w


def _pool_fwd(pb, g, n):
    e = pb[0:HALO + n, _gcols(g)]
    return _window_sum(e, g, False)[HALO:, :] * (1.0 / (2 << g)) - e[HALO:, :]


def _pool_bwd(qb, g, r0, n):
    e = qb[r0:r0 + n + HALO, _gcols(g)]
    return _window_sum(e, g, True)[0:n, :] * (1.0 / (2 << g)) - e[0:n, :]


def _full(shape):
    nd = len(shape)
    return pl.BlockSpec(shape, lambda *_: (0,) * nd)


ANY = pl.BlockSpec(memory_space=pl.ANY)


def _mesh_pos():
    x, y, c = lax.axis_index("x"), lax.axis_index("y"), lax.axis_index("c")
    chips = [(1 - x, y), (x, 1 - y), (1 - x, 1 - y)]
    return x, y, c, chips


def _half(ref, h):
    hr = ref.shape[0] // 2
    return ref.at[pl.ds(h * hr, hr), :]


class _AllGather:
    PER_ARRAY = 9

    def __init__(self, ins, outs, send_sems, recv_sems):
        self.ins, self.outs, self.send_sems, self.recv_sems = ins, outs, send_sems, recv_sems
        self.n = len(ins)

    @classmethod
    def scratch(cls, n):
        return [pltpu.SemaphoreType.DMA((cls.PER_ARRAY * n,)), pltpu.SemaphoreType.DMA((cls.PER_ARRAY * n,))]

    @staticmethod
    def out_shape(shards):
        return [jax.ShapeDtypeStruct((N_CHIPS,) + s.shape, s.dtype) for s in shards]

    def _copy(self, a, k, src, dst, to):
        i = self.PER_ARRAY * a + k
        return pltpu.make_async_remote_copy(src_ref=src, dst_ref=dst, send_sem=self.send_sems.at[i],
                                            recv_sem=self.recv_sems.at[i], device_id=to, device_id_type=MESH)

    def _piece(self, a, chip, piece, h=None):
        h = lax.axis_index("c") if h is None else h
        rows = self.ins[a].shape[0] // 4
        return self.outs[a].at[chip].at[pl.ds((2 * h + piece) * rows, rows), :]

    def _own(self, a, k):
        x, y, c, chips = _mesh_pos()
        piece = (1, 0, 0, 1)[k]
        rows = self.ins[a].shape[0] // 4
        src = self.ins[a].at[pl.ds((2 * c + piece) * rows, rows), :]
        return self._copy(a, k, src, self._piece(a, 2 * x + y, piece), (*chips[k // 2], c))

    def _relay(self, a, k):
        x, y, c, chips = _mesh_pos()
        source, to, piece = (chips[1], chips[0], 0) if k == 4 else (chips[0], chips[1], 1)
        rows = self._piece(a, 2 * source[0] + source[1], piece)
        return self._copy(a, k, rows, rows, (*to, c))

    def _sibling(self, a, k, h):
        x, y, c, chips = _mesh_pos()
        chip = chips[k - 6]
        slot = _half(self.outs[a].at[2 * chip[0] + chip[1]], h)
        return self._copy(a, k, slot, slot, (x, y, 1 - c))

    def start(self):
        for a in range(self.n):
            for k in range(4):
                self._own(a, k).start()

    def relay(self, a):
        self._own(a, 2).wait_recv()
        self._relay(a, 4).start()
        self._own(a, 0).wait_recv()
        self._relay(a, 5).start()

    def forward(self, a):
        c = lax.axis_index("c")
        self._own(a, 1).wait_recv()
        self._sibling(a, 6, c).start()
        self._own(a, 3).wait_recv()
        self._sibling(a, 7, c).start()
        self._relay(a, 4).wait_recv()
        self._relay(a, 5).wait_recv()
        self._sibling(a, 8, c).start()

    def finish(self, arrays=None):
        c = lax.axis_index("c")
        arrays = range(self.n) if arrays is None else arrays
        for a in arrays:
            for k in range(6, 9):
                self._sibling(a, k, 1 - c).wait_recv()
        for a in arrays:
            for k in range(4):
                self._own(a, k).wait_send()
            for k in range(4, 6):
                self._relay(a, k).wait_send()
            for k in range(6, 9):
                self._sibling(a, k, c).wait_send()


def _fill_own_slot(gathered, shards):
    chip = 2 * lax.axis_index("x") + lax.axis_index("y")
    return [lax.dynamic_update_slice(o, s[None], (chip, 0, 0)) for o, s in zip(gathered, shards)]


class _ExchangeHalves:
    def __init__(self, ins, recvs, send_sems, recv_sems):
        self.ins, self.recvs, self.send_sems, self.recv_sems = ins, recvs, send_sems, recv_sems

    @staticmethod
    def scratch(n):
        return [pltpu.SemaphoreType.DMA((n,)), pltpu.SemaphoreType.DMA((n,))]

    @staticmethod
    def out_shape(grads):
        return [jax.ShapeDtypeStruct((g.shape[0], g.shape[1] // 2, g.shape[2]), g.dtype) for g in grads]

    def _copies(self):
        x, y, c, _ = _mesh_pos()
        out = []
        for a, (src, dst) in enumerate(zip(self.ins, self.recvs)):
            hr = src.shape[1] // 2
            out.append(pltpu.make_async_remote_copy(
                src_ref=src.at[:, pl.ds((1 - c) * hr, hr), :], dst_ref=dst, send_sem=self.send_sems.at[a],
                recv_sem=self.recv_sems.at[a], device_id=(x, y, 1 - c), device_id_type=MESH))
        return out

    def start(self):
        for cp in self._copies():
            cp.start()

    def finish(self):
        for cp in self._copies():
            cp.wait()


def _exchange_halves(grads):
    n = len(grads)

    def body(*refs):
        ex = _ExchangeHalves(refs[:n], refs[n:2 * n], *refs[2 * n:])
        ex.start()
        ex.finish()

    return pl.pallas_call(
        body, name="grad_exchange_halves", out_shape=_ExchangeHalves.out_shape(grads),
        in_specs=[ANY] * n, out_specs=[ANY] * n, scratch_shapes=_ExchangeHalves.scratch(n),
    )(*grads)


class _ScatterToChips:
    def __init__(self, ins, rbufs, send_sems, recv_sems):
        self.ins, self.rbufs, self.send_sems, self.recv_sems = ins, rbufs, send_sems, recv_sems

    @staticmethod
    def scratch(n):
        return [pltpu.SemaphoreType.DMA((3 * n,)), pltpu.SemaphoreType.DMA((3 * n,))]

    @staticmethod
    def out_shape(sums):
        return [jax.ShapeDtypeStruct((3,) + s.shape[1:], BF16) for s in sums]

    def _copies(self):
        x, y, c, chips = _mesh_pos()
        out = []
        for a, (src, dst) in enumerate(zip(self.ins, self.rbufs)):
            for k, chip in enumerate(chips):
                out.append(pltpu.make_async_remote_copy(
                    src_ref=src.at[2 * chip[0] + chip[1]], dst_ref=dst.at[k], send_sem=self.send_sems.at[3 * a + k],
                    recv_sem=self.recv_sems.at[3 * a + k], device_id=(*chip, c), device_id_type=MESH))
        return out

    def start(self):
        for cp in self._copies():
            cp.start()

    def finish(self):
        for cp in self._copies():
            cp.wait()


def _gather_halves(halves):
    n = len(halves)

    def body(*refs):
        ins, outs = refs[:n], refs[n:2 * n]
        send_sems, recv_sems = refs[2 * n:]
        x, y, c, _ = _mesh_pos()
        sib = (x, y, 1 - c)
        remote = [pltpu.make_async_remote_copy(src_ref=ins[a].at[c], dst_ref=outs[a].at[c],
                                               send_sem=send_sems.at[a], recv_sem=recv_sems.at[a],
                                               device_id=sib, device_id_type=MESH) for a in range(n)]
        for cp in remote:
            cp.start()
        for a in range(n):
            pltpu.make_async_remote_copy(src_ref=ins[a].at[1 - c], dst_ref=outs[a].at[1 - c], send_sem=send_sems.at[a],
                                         recv_sem=recv_sems.at[a], device_id=sib, device_id_type=MESH).wait_recv()
        for cp in remote:
            cp.wait_send()

    return pl.pallas_call(
        body, name="grad_gather_halves",
        out_shape=[jax.ShapeDtypeStruct(h.shape, F32) for h in halves],
        in_specs=[ANY] * n, out_specs=[ANY] * n, input_output_aliases={a: a for a in range(n)},
        scratch_shapes=[pltpu.SemaphoreType.DMA((n,)), pltpu.SemaphoreType.DMA((n,))],
    )(*halves)


SMALL_A_ROWS = 24
SMALL_B_ROWS = 8
SMALL_C_ROWS = N_POOL_GROUPS * POOL_GROUP


class _AllReduceSmall:
    N_IN = 10
    SHAPES = [(SMALL_A_ROWS, D_MODEL), (SMALL_B_ROWS, D_CONV), (SMALL_C_ROWS, POOL_GROUP)]

    def __init__(self, ins, outs, scratch):
        self.ins, self.outs = ins, outs
        self.bufs, self.rcvs, self.send_sems, self.recv_sems = scratch[:3], scratch[3:6], scratch[6], scratch[7]

    @classmethod
    def scratch(cls):
        return ([pltpu.VMEM((3,) + s, F32) for s in cls.SHAPES] + [pltpu.VMEM((3,) + s, F32) for s in cls.SHAPES]
                + [pltpu.SemaphoreType.DMA((9,)), pltpu.SemaphoreType.DMA((9,))])

    @classmethod
    def out_shape(cls):
        return [jax.ShapeDtypeStruct(s, F32) for s in cls.SHAPES]

    def _copies(self, st):
        x, y, c, _ = _mesh_pos()
        peer = [(x, y, 1 - c), (1 - x, y, c), (x, 1 - y, c)][st]
        return [pltpu.make_async_remote_copy(
            src_ref=buf.at[st], dst_ref=rcv.at[st], send_sem=self.send_sems.at[3 * st + i],
            recv_sem=self.recv_sems.at[3 * st + i], device_id=peer, device_id_type=MESH)
            for i, (buf, rcv) in enumerate(zip(self.bufs, self.rcvs))]

    def pack_and_send(self):
        dg1_ref, dg1m_ref, dg2_ref, dg3_ref, dg4_ref, loss_ref, dmeta_ref, dsc_ref, dcw_ref, dpw_ref = self.ins
        a_buf, b_buf, c_buf = self.bufs

        def rowsum(v):
            return jnp.sum(v, axis=0, keepdims=True)

        a_buf[0, 0:1, :] = rowsum(dg1_ref[...] + dg1m_ref[...])
        a_buf[0, 1:2, :] = rowsum(dg2_ref[...])
        a_buf[0, 2:3, :] = rowsum(dg3_ref[...])
        a_buf[0, 3:4, :] = rowsum(dg4_ref[...])
        loss = jnp.sum(rowsum(loss_ref[...]), axis=1, keepdims=True) * (0.5 / D_MODEL)
        a_buf[0, 4:5, :] = jnp.broadcast_to(loss, (1, D_MODEL))
        a_buf[0, 5:8, :] = jnp.zeros((3, D_MODEL), F32)
        a_buf[0, 8:24, :] = dmeta_ref[...]
        b_buf[0, 0:1, :] = rowsum(dsc_ref[...])
        for k in range(3):
            b_buf[0, 1 + k:2 + k, :] = rowsum(dcw_ref[8 * k:8 * k + 8, :])
        b_buf[0, 4:8, :] = jnp.zeros((4, D_CONV), F32)
        c_buf[0] = dpw_ref[...]
        for cp in self._copies(0):
            cp.start()

    def combine(self, st):
        for cp in self._copies(st):
            cp.wait()
        if st < 2:
            for buf, rcv in zip(self.bufs, self.rcvs):
                buf[st + 1] = buf[st] + rcv[st]
            for cp in self._copies(st + 1):
                cp.start()
        else:
            for out, buf, rcv in zip(self.outs, self.bufs, self.rcvs):
                out[...] = buf[st] + rcv[st]


def _row_block(rows):
    for cand in (512, 448, 384, 352, 320, 256, 128, 64, 32, 16):
        if rows % cand == 0:
            return cand
    return rows


def _add_pairs(grad, recv, place):
    n_sh, rows2, cols = grad.shape
    hr = rows2 // 2
    br = _row_block(hr)

    def body(place_ref, a_ref, b_ref, o_ref):
        o_ref[...] = (a_ref[0] + b_ref[...]).astype(BF16)

    return pl.pallas_call(
        body, name="grad_add_pairs",
        grid_spec=pltpu.PrefetchScalarGridSpec(
            num_scalar_prefetch=1, grid=(n_sh, hr // br),
            in_specs=[pl.BlockSpec((1, 1, br, cols), lambda j, i, p: (j, p[1], i, 0)),
                      pl.BlockSpec((1, br, cols), lambda j, i, p: (j, i, 0))],
            out_specs=pl.BlockSpec((1, br, cols), lambda j, i, p: (j, i, 0))),
        out_shape=jax.ShapeDtypeStruct((n_sh, hr, cols), BF16), compiler_params=_cparams(2),
    )(place, grad.reshape(n_sh, 2, hr, cols), recv)


def _add_chips(grads, recvs, rbufs, place, scattered=(), name="grad_add_chips"):
    n, n_sc = len(grads), len(scattered)
    n_sh, rows2, cols = grads[0].shape
    hr = rows2 // 2
    br = _row_block(hr)
    n_steps = hr // br

    def body(place_ref, *refs):
        a_refs, b_refs, r_refs = refs[:n], refs[n:2 * n], refs[2 * n:3 * n]
        o_refs = refs[3 * n + n_sc:4 * n + n_sc]
        if n_sc:
            scatter = _ScatterToChips(refs[3 * n:3 * n + n_sc], refs[4 * n + n_sc:4 * n + 2 * n_sc], *refs[-2:])

            @pl.when(pl.program_id(0) == 0)
            def _():
                scatter.start()

        for a_ref, b_ref, r_ref, o_ref in zip(a_refs, b_refs, r_refs, o_refs):
            own = a_ref[0, 0] + b_ref[0]
            o_ref[0] = ((own + r_ref[0].astype(F32)) + r_ref[1].astype(F32)) + r_ref[2].astype(F32)

        if n_sc:
            @pl.when(pl.program_id(0) == n_steps - 1)
            def _():
                scatter.finish()

    outs = pl.pallas_call(
        body, name=name,
        grid_spec=pltpu.PrefetchScalarGridSpec(
            num_scalar_prefetch=1, grid=(n_steps,),
            in_specs=[pl.BlockSpec((1, 1, br, cols), lambda i, p: (p[0], p[1], i, 0))] * n
            + [pl.BlockSpec((1, br, cols), lambda i, p: (p[0], i, 0))] * n
            + [pl.BlockSpec((3, br, cols), lambda i, p: (0, i, 0))] * n + [ANY] * n_sc,
            out_specs=[pl.BlockSpec((1, br, cols), lambda i, p: (p[1], i, 0))] * n + [ANY] * n_sc,
            scratch_shapes=_ScatterToChips.scratch(n_sc) if n_sc else []),
        out_shape=[jax.ShapeDtypeStruct((2, hr, cols), F32)] * n + _ScatterToChips.out_shape(list(scattered)),
        compiler_params=_cparams(1),
    )(place, *[g.reshape(n_sh, 2, hr, cols) for g in grads], *recvs, *rbufs, *scattered)
    return outs[:n], outs[n:]


def _adamw_math(w, g, m, v):
    m2 = ADAM_B1 * m + (1.0 - ADAM_B1) * g
    v2 = ADAM_B2 * v + (1.0 - ADAM_B2) * (g * g)
    m_hat = m2 / (1.0 - ADAM_B1 ** ADAM_STEP)
    v_hat = v2 / (1.0 - ADAM_B2 ** ADAM_STEP)
    delta = -ADAM_LR * (m_hat / (jnp.sqrt(v_hat) + ADAM_EPS) + ADAM_WD * w)
    return delta, m2, v2


def _adamw_big(w, g, m, v):
    rows, cols = w.shape
    br = _row_block(rows)

    def body(w_ref, g_ref, m_ref, v_ref, d_ref, m2_ref, v2_ref):
        d, m2, v2 = _adamw_math(w_ref[...], g_ref[...], m_ref[...], v_ref[...])
        d_ref[...] = d
        m2_ref[...] = m2
        v2_ref[...] = v2

    spec = pl.BlockSpec((br, cols), lambda i: (i, 0))
    return pl.pallas_call(
        body, name="adamw_big", grid=(rows // br,),
        out_shape=[jax.ShapeDtypeStruct((rows, cols), F32)] * 3,
        in_specs=[spec] * 4, out_specs=[spec] * 3, compiler_params=_cparams(1),
    )(w, g, m, v)


def _adamw_small(groups):
    n = len(groups)

    def body(*refs):
        ins, outs = refs[:4 * n], refs[4 * n:]
        for i in range(n):
            w, g, m, v = (r[...] for r in ins[4 * i:4 * i + 4])
            d, m2, v2 = _adamw_math(w, g, m, v)
            outs[3 * i][...] = d
            outs[3 * i + 1][...] = m2
            outs[3 * i + 2][...] = v2

    vm = pl.BlockSpec(memory_space=pltpu.VMEM)
    flat = [a for grp in groups for a in grp]
    out_shape = [jax.ShapeDtypeStruct(grp[0].shape, F32) for grp in groups for _ in range(3)]
    outs = pl.pallas_call(body, name="adamw_small", out_shape=out_shape,
                          in_specs=[vm] * (4 * n), out_specs=[vm] * (3 * n))(*flat)
    return [tuple(outs[3 * i:3 * i + 3]) for i in range(n)]


def _load_weights(pairs, sem):
    for src, dst in pairs:
        cp = pltpu.make_async_copy(src, dst, sem)
        cp.start()
        cp.wait()


N_MIX_SHARDS = 3


def _mixer_fwd(x3, g1, g2, poolw, pscale, shards):
    n_seq, seq, _ = x3.shape
    tm = min(TM_MIX_FWD, seq)
    n_t = seq // tm
    n_steps = n_seq * n_t
    n_ag = len(shards)
    n_ffn = n_ag - N_MIX_SHARDS
    small_rows = shards[2].shape[0]
    conv_cols = D_CONV // N_CHIPS

    def body(x_ref, g1_ref, g2_ref, pw_ref, ps_ref, *rest):
        ag = _AllGather(rest[:n_ag], rest[n_ag + 10:2 * n_ag + 10], *rest[-2:])
        (z_ref, m_ref, h1_ref, a_ref, conv_ref, pooled_ref, yc_ref, zm_ref, meta_ref,
         cw_ref) = rest[n_ag:n_ag + 10]
        win_v, wout_v, small_v, cvb, pb, load_sems = rest[2 * n_ag + 10:-2]
        s, t = pl.program_id(0), pl.program_id(1)
        step = s * n_t + t

        @pl.when(step == 0)
        def _():
            ag.start()
            for a in range(N_MIX_SHARDS):
                ag.relay(a)
            for a in range(N_MIX_SHARDS):
                ag.forward(a)
            ag.finish(range(N_MIX_SHARDS))
            me = 2 * lax.axis_index("x") + lax.axis_index("y")
            dsts = [lambda j: win_v.at[j], lambda j: wout_v.at[pl.ds(j * OUT_SHARD, OUT_SHARD), :],
                    lambda j: small_v.at[j]]

            def loads(j, own):
                return [pltpu.make_async_copy(ag.ins[a] if own else ag.outs[a].at[j], dsts[a](j),
                                              load_sems.at[N_MIX_SHARDS * j + a]) for a in range(N_MIX_SHARDS)]

            for wait in (False, True):
                for j in range(N_CHIPS):
                    for own in (False, True):
                        @pl.when((me == j) == own)
                        def _():
                            for cp in loads(j, own):
                                cp.wait() if wait else cp.start()

            meta = jnp.concatenate([small_v[j, 0:N_META, :] for j in range(N_CHIPS)], axis=1)
            meta_ref[...] = meta
            cw_ref[...] = jnp.concatenate([small_v[j, N_META:N_META + 3, 0:conv_cols] for j in range(N_CHIPS)], axis=1)
            a_meta = (meta * _rstd(meta) * g1_ref[...]).astype(BF16)
            for j in range(N_CHIPS):
                zm_ref[:, j * IN_SHARD:(j + 1) * IN_SHARD] = _dot(a_meta, win_v[j])

        for i in range(n_ffn):
            @pl.when(step == ((i + 1) * n_steps) // (2 * n_ffn + 2))
            def _():
                ag.relay(N_MIX_SHARDS + i)

        for i in range(n_ffn):
            @pl.when(step == min(n_steps // 2 + ((i + 1) * n_steps) // (2 * n_ffn + 2), n_steps - 1))
            def _():
                ag.forward(N_MIX_SHARDS + i)

        @pl.when(t == 0)
        def _():
            cvb[0:HALO, :] = zm_ref[:, IN_SHARD:2 * IN_SHARD] * zm_ref[:, 2 * IN_SHARD:3 * IN_SHARD]
            pb[0:HALO, :] = zm_ref[:, 3 * IN_SHARD:4 * IN_SHARD]

        @pl.when(t > 0)
        def _():
            cvb[0:HALO, :] = cvb[tm:tm + HALO, :]
            pb[0:HALO, :] = pb[tm:tm + HALO, :]

        xt = x_ref[0]
        a = (xt * _rstd(xt) * g1_ref[...]).astype(BF16)
        a_ref[...] = a
        zb = _dot(a, win_v[0])
        zc = _dot(a, win_v[1])
        zv = _dot(a, win_v[2])
        zp = _dot(a, win_v[3])
        z_ref[0, :, 0:IN_SHARD] = zb
        z_ref[0, :, IN_SHARD:2 * IN_SHARD] = zc
        z_ref[0, :, 2 * IN_SHARD:3 * IN_SHARD] = zv
        cv = zc * zv
        cvb[HALO:HALO + tm, :] = cv
        pb[HALO:HALO + tm, :] = zp
        cw = cw_ref[...]
        conv = cw[0:1] * cvb[HALO - 2:HALO - 2 + tm, :] + cw[1:2] * cvb[HALO - 1:HALO - 1 + tm, :] + cw[2:3] * cv
        conv_ref[...] = conv
        parts = [(zb * conv).astype(BF16)]
        for g in range(N_POOL_GROUPS):
            pooled = _pool_fwd(pb, g, tm).astype(BF16)
            pooled_ref[:, _gcols(g)] = pooled
            parts.append((_dot(pooled, pw_ref[g]) * ps_ref[:, _gcols(g)]).astype(BF16))
        ycat = jnp.concatenate(parts, axis=1)
        yc_ref[...] = ycat
        m = _dot(ycat, wout_v[...])
        m_ref[0] = m
        h1_ref[0] = xt + m * _rstd(m) * g2_ref[...]

        @pl.when(step == n_steps - 1)
        def _():
            ag.finish(range(N_MIX_SHARDS, n_ag))

    n_rows = n_seq * seq
    row = lambda c: pl.BlockSpec((1, tm, c), lambda s, t: (s, t, 0))
    row2 = lambda c: pl.BlockSpec((tm, c), lambda s, t: (s * n_t + t, 0))
    outs = pl.pallas_call(
        body, name="mixer_fwd", grid=(n_seq, n_t),
        out_shape=[jax.ShapeDtypeStruct((n_seq, seq, D_Z), F32), jax.ShapeDtypeStruct((n_seq, seq, D_MODEL), F32),
                   jax.ShapeDtypeStruct((n_seq, seq, D_MODEL), F32), jax.ShapeDtypeStruct((n_rows, D_MODEL), BF16),
                   jax.ShapeDtypeStruct((n_rows, D_CONV), F32), jax.ShapeDtypeStruct((n_rows, D_POOL), BF16),
                   jax.ShapeDtypeStruct((n_rows, D_MODEL), BF16), jax.ShapeDtypeStruct((N_META, D_IN_PROJ), F32),
                   jax.ShapeDtypeStruct((N_META, D_MODEL), F32), jax.ShapeDtypeStruct((3, D_CONV), F32)]
        + _AllGather.out_shape(shards),
        in_specs=[row(D_MODEL), _full((1, D_MODEL)), _full((1, D_MODEL)),
                  _full((N_POOL_GROUPS, POOL_GROUP, POOL_GROUP)), _full((1, D_POOL))] + [ANY] * n_ag,
        out_specs=[row(D_Z), row(D_MODEL), row(D_MODEL), row2(D_MODEL), row2(D_CONV), row2(D_POOL), row2(D_MODEL),
                   _full((N_META, D_IN_PROJ)), _full((N_META, D_MODEL)), _full((3, D_CONV))] + [ANY] * n_ag,
        scratch_shapes=[pltpu.VMEM((N_CHIPS, D_MODEL, IN_SHARD), BF16), pltpu.VMEM((D_MODEL, D_MODEL), BF16),
                        pltpu.VMEM((N_CHIPS, small_rows, D_MODEL // N_CHIPS), F32),
                        pltpu.VMEM((HALO + tm, D_CONV), F32), pltpu.VMEM((HALO + tm, D_POOL), F32),
                        pltpu.SemaphoreType.DMA((N_MIX_SHARDS * N_CHIPS,))] + _AllGather.scratch(n_ag),
        compiler_params=_cparams(2),
    )(x3, g1, g2, poolw, pscale, *shards)
    return outs[:10], _fill_own_slot(outs[10:], shards)


def _ffn_chunks():
    out, r0 = [], 0
    while r0 < D_FF:
        out.append((r0, min(FF_CHUNK, D_FF - r0)))
        r0 += FF_CHUNK
    return out


def _ffn_fwd_bwd(h1, target, g3, g4, wg_t, wu_t, wd):
    n_rows = h1.shape[0]
    tm = min(TM_FFN, n_rows)
    chunks = _ffn_chunks()

    def body(h1_ref, t_ref, g3_ref, g4_ref, wg_hbm, wu_hbm, wd_hbm,
             dh1_ref, f_ref, dd_ref, ds_ref, du_ref, gg_ref, loss_ref, dg3_ref, dg4_ref,
             wg_v, wu_v, wd_v, s_sc, u_sc, sem):
        @pl.when(pl.program_id(0) == 0)
        def _():
            _load_weights([(wg_hbm, wg_v), (wu_hbm, wu_v), (wd_hbm, wd_v)], sem)
            loss_ref[...] = jnp.zeros_like(loss_ref)
            dg3_ref[...] = jnp.zeros_like(dg3_ref)
            dg4_ref[...] = jnp.zeros_like(dg4_ref)

        h1v = h1_ref[...]
        r3 = _rstd(h1v)
        hh = h1v * r3
        g3v, g4v = g3_ref[...], g4_ref[...]
        f = (hh * g3v).astype(BF16)
        f_ref[...] = f
        d = jnp.zeros((tm, D_MODEL), F32)
        for r0, sz in chunks:
            s = _dot_nt(f, wg_v[r0:r0 + sz, :])
            u = _dot_nt(f, wu_v[r0:r0 + sz, :])
            s_sc[:, r0:r0 + sz] = s
            u_sc[:, r0:r0 + sz] = u
            gc = (s * _sigmoid(s) * u).astype(BF16)
            gg_ref[:, r0:r0 + sz] = gc
            d = d + _dot(gc, wd_v[r0:r0 + sz, :])
        r4 = _rstd(d)
        dh = d * r4
        err = (h1v + dh * g4v) - t_ref[...]
        loss_ref[...] += _rows8(err * err)
        dy = err * (1.0 / D_MODEL)
        dg4_ref[...] += _rows8(dy * dh)
        ddb = _rms_bwd(dy, dh, r4, g4v).astype(BF16)
        dd_ref[...] = ddb
        df = jnp.zeros((tm, D_MODEL), F32)
        for r0, sz in chunks:
            dgg = _dot_nt(ddb, wd_v[r0:r0 + sz, :])
            s = s_sc[:, r0:r0 + sz]
            u = u_sc[:, r0:r0 + sz]
            sig = _sigmoid(s)
            dsc = (dgg * u * (sig * (1.0 + s * (1.0 - sig)))).astype(BF16)
            duc = (dgg * (s * sig)).astype(BF16)
            ds_ref[:, r0:r0 + sz] = dsc
            du_ref[:, r0:r0 + sz] = duc
            df = df + _dot(dsc, wg_v[r0:r0 + sz, :]) + _dot(duc, wu_v[r0:r0 + sz, :])
        dg3_ref[...] += _rows8(df * hh)
        dh1_ref[...] = dy + _rms_bwd(df, hh, r3, g3v)

    row = pl.BlockSpec((tm, D_MODEL), lambda i: (i, 0))
    ffrow = pl.BlockSpec((tm, D_FF), lambda i: (i, 0))
    acc = _full((8, D_MODEL))
    act_bf = jax.ShapeDtypeStruct((n_rows, D_MODEL), BF16)
    ff_bf = jax.ShapeDtypeStruct((n_rows, D_FF), BF16)
    acc_shape = jax.ShapeDtypeStruct((8, D_MODEL), F32)
    w_vmem = pltpu.VMEM((D_FF, D_MODEL), BF16)
    return pl.pallas_call(
        body, name="ffn_fwd_bwd", grid=(n_rows // tm,),
        out_shape=[jax.ShapeDtypeStruct((n_rows, D_MODEL), F32), act_bf, act_bf, ff_bf, ff_bf, ff_bf,
                   acc_shape, acc_shape, acc_shape],
        in_specs=[row, row, _full((1, D_MODEL)), _full((1, D_MODEL)), ANY, ANY, ANY],
        out_specs=[row, row, row, ffrow, ffrow, ffrow, acc, acc, acc],
        scratch_shapes=[w_vmem, w_vmem, w_vmem, pltpu.VMEM((tm, D_FF), F32), pltpu.VMEM((tm, D_FF), F32),
                        pltpu.SemaphoreType.DMA],
        compiler_params=_cparams(1),
    )(h1, target, g3, g4, wg_t, wu_t, wd)


def _ffn_weight_grads(name, acts, other, exchanged):
    n_rows = other.shape[0]
    n_a, n_ex = len(acts), len(exchanged)
    n_c = n_a
    tk = min(TK_DW, n_rows)
    n_k = n_rows // tk
    half = D_FF // n_c

    def body(other_ref, *rest):
        act_refs = rest[:n_a]
        out_refs = rest[n_a + n_ex:2 * n_a + n_ex]
        c, k = pl.program_id(0), pl.program_id(1)
        if n_ex:
            ex = _ExchangeHalves(rest[n_a:n_a + n_ex], rest[2 * n_a + n_ex:2 * n_a + 2 * n_ex], *rest[-2:])

            @pl.when((c == 0) & (k == 0))
            def _():
                ex.start()

        @pl.when(k == 0)
        def _():
            for o in out_refs:
                o[...] = jnp.zeros_like(o)

        ov = other_ref[...]
        for a, o in zip(act_refs, out_refs):
            o[...] += _dot_tn(a[...], ov)

        if n_ex:
            @pl.when((c == n_c - 1) & (k == n_k - 1))
            def _():
                ex.finish()

    row = pl.BlockSpec((tk, D_MODEL), lambda c, k: (k, 0))
    ffrow = pl.BlockSpec((tk, half), lambda c, k: (k, c))
    out = pl.BlockSpec((half, D_MODEL), lambda c, k: (c, 0))
    outs = pl.pallas_call(
        body, name=name, grid=(n_c, n_k),
        out_shape=[jax.ShapeDtypeStruct((D_FF, D_MODEL), F32)] * n_a + _ExchangeHalves.out_shape(exchanged),
        in_specs=[row] + [ffrow] * n_a + [ANY] * n_ex, out_specs=[out] * n_a + [ANY] * n_ex,
        scratch_shapes=_ExchangeHalves.scratch(n_ex) if n_ex else [],
        compiler_params=_cparams(2),
    )(other, *acts, *exchanged)
    return outs[:n_a], outs[n_a:]


def _mixer_bwd(dh1, m3, z3, conv2, pooled2, x3, zmeta, g1, g2, convw, poolw, pscale, win_all, wout, exchanged,
               scattered):
    n_seq, seq, _ = x3.shape
    tm = min(TM_MIX_BWD, seq)
    sub = min(SUB_MIX_BWD, tm)
    n_t = seq // tm
    n_ex, n_sc = len(exchanged), len(scattered)
    n_cm = n_ex + n_sc

    def body(dh1_ref, m_ref, z_ref, conv_ref, pooled_ref, x_ref, zm_ref, g1_ref, g2_ref, cw_ref, pw_ref, ps_ref,
             win_hbm, wout_hbm, *rest):
        outs0 = n_cm + 9
        ex = _ExchangeHalves(rest[:n_ex], rest[outs0:outs0 + n_ex], *rest[-4:-2])
        sc = _ScatterToChips(rest[n_ex:n_cm], rest[outs0 + n_ex:outs0 + n_cm], *rest[-2:])
        dx_ref, dz_ref, dm_ref, dg1_ref, dg2_ref, dsc_ref, dcw_ref, dpw_ref, dzm_ref = rest[n_cm:outs0]
        win_v, wout_v, dcb, dqb, mcb, mqb, sem = rest[outs0 + n_cm:-4]
        s, i = pl.program_id(0), pl.program_id(1)
        tr = n_t - 1 - i

        @pl.when((s == 0) & (i == 0))
        def _():
            sc.start()
            ex.start()
            _load_weights([(win_hbm, win_v), (wout_hbm, wout_v)], sem)
            for ref in (dg1_ref, dg2_ref, dsc_ref, dcw_ref, dpw_ref, dzm_ref):
                ref[...] = jnp.zeros_like(ref)

        @pl.when(i == 0)
        def _():
            dcb[tm:tm + HALO, :] = jnp.zeros((HALO, D_CONV), F32)
            dqb[tm:tm + HALO, :] = jnp.zeros((HALO, D_POOL), F32)

        @pl.when(i > 0)
        def _():
            dcb[tm:tm + HALO, :] = dcb[0:HALO, :]
            dqb[tm:tm + HALO, :] = dqb[0:HALO, :]

        g1v, g2v = g1_ref[...], g2_ref[...]
        cw = cw_ref[...]

        for r0 in range(tm - sub, -1, -sub):
            rows = slice(r0, r0 + sub)
            dh1v = dh1_ref[0, rows, :]
            mv = m_ref[0, rows, :]
            r2 = _rstd(mv)
            mh = mv * r2
            dg2_ref[...] += _rows8(dh1v * mh)
            dmb = _rms_bwd(dh1v, mh, r2, g2v).astype(BF16)
            dm_ref[rows, :] = dmb
            dyc = _dot_nt(dmb, wout_v[...])
            dyconv = dyc[:, 0:D_CONV]

            for g in range(N_POOL_GROUPS):
                pooled = pooled_ref[rows, _gcols(g)]
                mixed = _dot(pooled, pw_ref[g])
                scale = ps_ref[:, _gcols(g)]
                dyp = dyc[:, D_CONV + g * POOL_GROUP:D_CONV + (g + 1) * POOL_GROUP]
                dsc_ref[:, _gcols(g)] += _rows8(dyp * mixed)
                dmix = (dyp * scale).astype(BF16)
                dpw_ref[g] += _dot_tn(pooled, dmix)
                dqb[rows, _gcols(g)] = _dot_nt(dmix, pw_ref[g])

            zb = z_ref[0, rows, 0:IN_SHARD]
            zc = z_ref[0, rows, IN_SHARD:2 * IN_SHARD]
            zv = z_ref[0, rows, 2 * IN_SHARD:3 * IN_SHARD]
            dconv = dyconv * zb
            dcb[rows, :] = dconv
            d1 = dcb[r0 + 1:r0 + 1 + sub, :]
            d2 = dcb[r0 + 2:r0 + 2 + sub, :]
            dcv = cw[2:3] * dconv + cw[1:2] * d1 + cw[0:1] * d2
            cv = zc * zv
            dcw_ref[0:8, :] += _rows8(cv * d2)
            dcw_ref[8:16, :] += _rows8(cv * d1)
            dcw_ref[16:24, :] += _rows8(cv * dconv)
            dzs = [(dyconv * conv_ref[rows, :]).astype(BF16), (dcv * zv).astype(BF16), (dcv * zc).astype(BF16),
                   jnp.concatenate([_pool_bwd(dqb, g, r0, sub) for g in range(N_POOL_GROUPS)], axis=1).astype(BF16)]
            da = jnp.zeros((sub, D_MODEL), F32)
            for j in range(N_CHIPS):
                dz_ref[j, rows, :] = dzs[j]
                da = da + _dot_nt(dzs[j], win_v[j])
            xt = x_ref[0, rows, :]
            r1 = _rstd(xt)
            xh = xt * r1
            dg1_ref[...] += _rows8(da * xh)
            dx_ref[0, rows, :] = dh1v + _rms_bwd(da, xh, r1, g1v)

        @pl.when(tr == 0)
        def _():
            mcb[0:HALO, :] = jnp.zeros((HALO, D_CONV), F32)
            mqb[0:HALO, :] = jnp.zeros((HALO, D_POOL), F32)
            mcb[HALO:2 * HALO, :] = dcb[0:HALO, :]
            mqb[HALO:2 * HALO, :] = dqb[0:HALO, :]
            m1 = mcb[1:1 + HALO, :]
            m2 = mcb[2:2 + HALO, :]
            zc_m = zm_ref[:, IN_SHARD:2 * IN_SHARD]
            zv_m = zm_ref[:, 2 * IN_SHARD:3 * IN_SHARD]
            cv_m = zc_m * zv_m
            dcw_ref[0:8, :] += _rows8(cv_m * m2)
            dcw_ref[8:16, :] += _rows8(cv_m * m1)
            dcv_m = cw[1:2] * m1 + cw[0:1] * m2
            dzm_ref[:, IN_SHARD:2 * IN_SHARD] += dcv_m * zv_m
            dzm_ref[:, 2 * IN_SHARD:3 * IN_SHARD] += dcv_m * zc_m
            dzm_ref[:, 3 * IN_SHARD:4 * IN_SHARD] += jnp.concatenate(
                [_pool_bwd(mqb, g, 0, HALO) for g in range(N_POOL_GROUPS)], axis=1)

        @pl.when((s == n_seq - 1) & (i == n_t - 1))
        def _():
            ex.finish()
            sc.finish()

    row3 = lambda c: pl.BlockSpec((1, tm, c), lambda s, i: (s, n_t - 1 - i, 0))
    row2 = lambda c: pl.BlockSpec((tm, c), lambda s, i: (s * n_t + n_t - 1 - i, 0))
    n_rows = n_seq * seq
    outs = pl.pallas_call(
        body, name="mixer_bwd", grid=(n_seq, n_t),
        out_shape=[jax.ShapeDtypeStruct((n_seq, seq, D_MODEL), F32),
                   jax.ShapeDtypeStruct((N_CHIPS, n_rows, IN_SHARD), BF16), jax.ShapeDtypeStruct((n_rows, D_MODEL), BF16),
                   jax.ShapeDtypeStruct((8, D_MODEL), F32), jax.ShapeDtypeStruct((8, D_MODEL), F32),
                   jax.ShapeDtypeStruct((8, D_POOL), F32), jax.ShapeDtypeStruct((24, D_CONV), F32),
                   jax.ShapeDtypeStruct((N_POOL_GROUPS, POOL_GROUP, POOL_GROUP), F32),
                   jax.ShapeDtypeStruct((N_META, D_IN_PROJ), F32)]
        + _ExchangeHalves.out_shape(exchanged) + _ScatterToChips.out_shape(scattered),
        in_specs=[row3(D_MODEL), row3(D_MODEL), row3(D_Z), row2(D_CONV), row2(D_POOL), row3(D_MODEL),
                  _full((N_META, D_IN_PROJ)), _full((1, D_MODEL)), _full((1, D_MODEL)), _full((3, D_CONV)),
                  _full((N_POOL_GROUPS, POOL_GROUP, POOL_GROUP)), _full((1, D_POOL)), ANY, ANY] + [ANY] * n_cm,
        out_specs=[row3(D_MODEL), pl.BlockSpec((N_CHIPS, tm, IN_SHARD), lambda s, i: (0, s * n_t + n_t - 1 - i, 0)),
                   row2(D_MODEL),
                   _full((8, D_MODEL)), _full((8, D_MODEL)), _full((8, D_POOL)), _full((24, D_CONV)),
                   _full((N_POOL_GROUPS, POOL_GROUP, POOL_GROUP)), _full((N_META, D_IN_PROJ))] + [ANY] * n_cm,
        scratch_shapes=[pltpu.VMEM((N_CHIPS, D_MODEL, IN_SHARD), BF16), pltpu.VMEM((D_MODEL, D_MODEL), BF16),
                        pltpu.VMEM((tm + HALO, D_CONV), F32), pltpu.VMEM((tm + HALO, D_POOL), F32),
                        pltpu.VMEM((2 * HALO, D_CONV), F32), pltpu.VMEM((2 * HALO, D_POOL), F32),
                        pltpu.SemaphoreType.DMA] + _ExchangeHalves.scratch(n_ex) + _ScatterToChips.scratch(n_sc),
        compiler_params=_cparams(2),
    )(dh1, m3, z3, conv2, pooled2, x3, zmeta, g1, g2, convw, poolw, pscale, win_all, wout, *exchanged, *scattered)
    return outs[:9], outs[9:9 + n_ex], outs[9 + n_ex:]


def _meta_bwd(dzm, meta_full, g1, win_all):
    def body(dzm_ref, meta_ref, g1_ref, win_ref, dmeta_ref, dg1_ref, a_ref, dzb_ref):
        xm = meta_ref[...]
        r = _rstd(xm)
        xh = xm * r
        g1v = g1_ref[...]
        a_ref[...] = (xh * g1v).astype(BF16)
        da = jnp.zeros((N_META, D_MODEL), F32)
        for j in range(N_CHIPS):
            dzj = dzm_ref[:, j * IN_SHARD:(j + 1) * IN_SHARD].astype(BF16)
            dzb_ref[j] = dzj
            da = da + _dot_nt(dzj, win_ref[j])
        dg1_ref[...] = _rows8(da * xh)
        dmeta_ref[...] = _rms_bwd(da, xh, r, g1v)

    vm = pl.BlockSpec(memory_space=pltpu.VMEM)
    return pl.pallas_call(
        body, name="meta_bwd",
        out_shape=[jax.ShapeDtypeStruct((N_META, D_MODEL), F32), jax.ShapeDtypeStruct((8, D_MODEL), F32),
                   jax.ShapeDtypeStruct((N_META, D_MODEL), BF16), jax.ShapeDtypeStruct((N_CHIPS, N_META, IN_SHARD), BF16)],
        in_specs=[vm] * 4, out_specs=[vm] * 4,
    )(dzm, meta_full, g1, win_all)


def _mixer_weight_grads(a, dz, ycat, dm, a_meta, dz_meta, ffn_sums, small):
    n_rows = a.shape[0]
    tk = min(TK_DW, n_rows)
    n_k = n_rows // tk
    n_sc, n_sm = len(ffn_sums), _AllReduceSmall.N_IN

    def body(a_ref, dz_ref, yc_ref, dm_ref, am_ref, dzm_ref, *rest):
        ins, outs, scratch = rest[:n_sc + n_sm], rest[n_sc + n_sm:2 * n_sc + n_sm + 5], rest[2 * n_sc + n_sm + 5:]
        dwin_ref, dwout_ref = outs[:2]
        scatter = _ScatterToChips(ins[:n_sc], outs[2:2 + n_sc], *scratch[:2])
        reduce_small = _AllReduceSmall(ins[n_sc:], outs[2 + n_sc:], scratch[2:])
        k = pl.program_id(0)

        @pl.when(k == 0)
        def _():
            scatter.start()
            reduce_small.pack_and_send()
            am_t = am_ref[...].T
            for j in range(N_CHIPS):
                dwin_ref[j] = _dot(am_t, dzm_ref[j])
            dwout_ref[...] = jnp.zeros_like(dwout_ref)

        for st in range(2):
            @pl.when(k == ((st + 1) * n_k) // 3)
            def _():
                reduce_small.combine(st)

        a_t = a_ref[...].T
        for j in range(N_CHIPS):
            dwin_ref[j] += _dot(a_t, dz_ref[j])
        dwout_ref[...] += _dot_tn(yc_ref[...], dm_ref[...])

        @pl.when(k == n_k - 1)
        def _():
            reduce_small.combine(2)
            scatter.finish()

    row = pl.BlockSpec((tk, D_MODEL), lambda k: (k, 0))
    outs = pl.pallas_call(
        body, name="mixer_weight_grads", grid=(n_k,),
        out_shape=[jax.ShapeDtypeStruct((N_CHIPS, D_MODEL, IN_SHARD), F32),
                   jax.ShapeDtypeStruct((D_MODEL, D_MODEL), F32)] + _ScatterToChips.out_shape(ffn_sums)
        + _AllReduceSmall.out_shape(),
        in_specs=[row, pl.BlockSpec((N_CHIPS, tk, IN_SHARD), lambda k: (0, k, 0)), row, row,
                  _full((N_META, D_MODEL)), _full((N_CHIPS, N_META, IN_SHARD))] + [ANY] * n_sc
        + [_full(s.shape) for s in small],
        out_specs=[_full((N_CHIPS, D_MODEL, IN_SHARD)), _full((D_MODEL, D_MODEL))] + [ANY] * n_sc
        + [_full(s) for s in _AllReduceSmall.SHAPES],
        scratch_shapes=_ScatterToChips.scratch(n_sc) + _AllReduceSmall.scratch(),
        compiler_params=_cparams(1),
    )(a, dz, ycat, dm, a_meta, dz_meta, *ffn_sums, *small)
    return ([outs[0], outs[1].reshape(N_CHIPS, OUT_SHARD, D_MODEL)], outs[2:2 + n_sc], outs[2 + n_sc:])


def kernel(x, meta_tokens, norm_mix_pre, w_in, conv_w, pool_w, pool_scale, w_out, norm_mix_post, norm_ffn_pre, w_gate, w_up, w_down, norm_ffn_post, loss_target, m_meta_tokens, m_norm_mix_pre, m_w_in, m_conv_w, m_pool_w, m_pool_scale, m_w_out, m_norm_mix_post, m_norm_ffn_pre, m_w_gate, m_w_up, m_w_down, m_norm_ffn_post, v_meta_tokens, v_norm_mix_pre, v_w_in, v_conv_w, v_pool_w, v_pool_scale, v_w_out, v_norm_mix_post, v_norm_ffn_pre, v_w_gate, v_w_up, v_w_down, v_norm_ffn_post):
    n_seq, seq, _ = x.shape
    n_rows = n_seq * seq
    chip = 2 * lax.axis_index("x") + lax.axis_index("y")
    meta_cols = D_MODEL // N_CHIPS
    conv_cols = D_CONV // N_CHIPS

    small = jnp.zeros((2 * HALO, meta_cols), F32)
    small = small.at[0:N_META, :].set(meta_tokens).at[N_META:N_META + 3, 0:conv_cols].set(conv_w[0])
    poolw_bf = pool_w[0].astype(BF16)
    pscale = pool_scale
    g1, g2, g3, g4 = norm_mix_pre, norm_mix_post, norm_ffn_pre, norm_ffn_post
    place = jnp.stack([chip, lax.axis_index("c")]).astype(jnp.int32)

    ((z3, m3, h1, a_bf, conv2, pooled2, yc_bf, zmeta, meta_full, conv_full),
     (win_all, wout_all, _, wg_all, wu_all, wd_all)) = _mixer_fwd(
        x, g1, g2, poolw_bf, pscale,
        [w_in[0].astype(BF16), w_out[0].astype(BF16), small,
         w_gate[0].T.astype(BF16), w_up[0].T.astype(BF16), w_down[0].astype(BF16)])
    wout_full = wout_all.reshape(D_MODEL, D_MODEL)
    wg_t, wu_t, wd_full = [w.reshape(D_FF, D_MODEL) for w in (wg_all, wu_all, wd_all)]
    dh1, f_bf, dd_bf, ds_bf, du_bf, gg_bf, lossp, dg3p, dg4p = _ffn_fwd_bwd(
        h1.reshape(n_rows, D_MODEL), loss_target.reshape(n_rows, D_MODEL), g3, g4, wg_t, wu_t, wd_full)
    as_shards = lambda g: g.reshape(N_CHIPS, FF_SHARD, D_MODEL)
    (dwd,), _ = _ffn_weight_grads("ffn_weight_grads_down", [gg_bf], dd_bf, [])
    dwd = as_shards(dwd)
    (dwg_t,), (dwd_recv,) = _ffn_weight_grads("ffn_weight_grads_gate", [ds_bf], f_bf, [dwd])
    dwg_t = as_shards(dwg_t)
    (dwu_t,), (dwg_recv,) = _ffn_weight_grads("ffn_weight_grads_up", [du_bf], f_bf, [dwg_t])
    dwu_t = as_shards(dwu_t)
    ((grad_x, dz_bf, dm_bf, dg1p, dg2p, dscp, dcwp, dpw, dzm), (dwu_recv,), (dwd_rbuf, dwg_rbuf)) = _mixer_bwd(
        dh1.reshape(n_seq, seq, D_MODEL), m3, z3, conv2, pooled2, x, zmeta, g1, g2, conv_full, poolw_bf, pscale,
        win_all, wout_full, [dwu_t], [_add_pairs(dwd, dwd_recv, place), _add_pairs(dwg_t, dwg_recv, place)])
    dmeta, dg1m, a_meta, dz_meta = _meta_bwd(dzm, meta_full, g1, win_all)
    mix_grads, (dwu_rbuf,), (a_red, b_red, c_red) = _mixer_weight_grads(
        a_bf, dz_bf, yc_bf, dm_bf, a_meta, dz_meta, [_add_pairs(dwu_t, dwu_recv, place)],
        [dg1p, dg1m, dg2p, dg3p, dg4p, lossp, dmeta, dscp, dcwp, dpw.reshape(SMALL_C_ROWS, POOL_GROUP)])

    mix_recvs = _exchange_halves(mix_grads)
    ffn_red, mix_rbufs = _add_chips([dwg_t, dwu_t, dwd], [dwg_recv, dwu_recv, dwd_recv],
                                    [dwg_rbuf, dwu_rbuf, dwd_rbuf], place,
                                    [_add_pairs(g, r, place) for g, r in zip(mix_grads, mix_recvs)],
                                    name="grad_add_chips_ffn")
    mix_red = [_add_chips([g], [r], [rb], place)[0][0] for g, r, rb in zip(mix_grads, mix_recvs, mix_rbufs)]
    reduced = _gather_halves(mix_red + list(ffn_red))
    g_win, g_wout, g_wg_t, g_wu_t, g_wd = [r.reshape(2 * r.shape[1], r.shape[2]) for r in reduced]

    loss = a_red[4, 0]
    g_g1, g_g2, g_g3, g_g4 = a_red[0:1], a_red[1:2], a_red[2:3], a_red[3:4]
    g_meta = lax.dynamic_slice(a_red, (8, chip * meta_cols), (N_META, meta_cols))
    g_pscale = b_red[0:1]
    g_conv = lax.dynamic_slice(b_red, (1, chip * conv_cols), (3, conv_cols))
    g_poolw = c_red

    big = [(w_in[0], g_win, m_w_in[0], v_w_in[0]), (w_out[0], g_wout, m_w_out[0], v_w_out[0]),
           (w_gate[0].T, g_wg_t, m_w_gate[0].T, v_w_gate[0].T), (w_up[0].T, g_wu_t, m_w_up[0].T, v_w_up[0].T),
           (w_down[0], g_wd, m_w_down[0], v_w_down[0])]
    big_out = [_adamw_big(w, g, m, v) for (w, g, m, v) in big]
    big_out[2] = [o.T for o in big_out[2]]
    big_out[3] = [o.T for o in big_out[3]]
    g_wg, g_wu = g_wg_t.T, g_wu_t.T
    small_groups = [
        (meta_tokens, g_meta, m_meta_tokens, v_meta_tokens),
        (g1, g_g1, m_norm_mix_pre, v_norm_mix_pre),
        (conv_w[0], g_conv, m_conv_w[0], v_conv_w[0]),
        (pool_w.reshape(SMALL_C_ROWS, POOL_GROUP), g_poolw, m_pool_w.reshape(SMALL_C_ROWS, POOL_GROUP),
         v_pool_w.reshape(SMALL_C_ROWS, POOL_GROUP)),
        (pool_scale, g_pscale, m_pool_scale, v_pool_scale),
        (g2, g_g2, m_norm_mix_post, v_norm_mix_post),
        (g3, g_g3, m_norm_ffn_pre, v_norm_ffn_pre),
        (g4, g_g4, m_norm_ffn_post, v_norm_ffn_post),
    ]
    small_out = _adamw_small(small_groups)

    grads_out = [g_meta, g_g1, g_win[None], g_conv[None], g_poolw.reshape(pool_w.shape), g_pscale, g_wout[None],
                 g_g2, g_g3, g_wg[None], g_wu[None], g_wd[None], g_g4]
    s_meta, s_g1, s_conv, s_poolw, s_pscale, s_g2, s_g3, s_g4 = small_out
    b_win, b_wout, b_wg, b_wu, b_wd = big_out

    def leaf(k):
        return [s_meta[k], s_g1[k], b_win[k][None], s_conv[k][None], s_poolw[k].reshape(pool_w.shape), s_pscale[k],
                b_wout[k][None], s_g2[k], s_g3[k], b_wg[k][None], b_wu[k][None], b_wd[k][None], s_g4[k]]

    return (loss, grad_x, *grads_out, *leaf(0), *leaf(1), *leaf(2))
```

```python
import jax
import jax.numpy as jnp
from jax import lax
from jax.experimental import pallas as pl
from jax.experimental.pallas import tpu as pltpu

F32 = jnp.float32
BF16 = jnp.bfloat16
MESH = pl.DeviceIdType.MESH

D_MODEL = 1024
D_CONV = 512
D_POOL = 512
POOL_GROUP = 128
N_POOL_GROUPS = 4
D_IN_PROJ = 2048
D_FF = 2816
N_CHIPS = 4
FF_SHARD = D_FF // N_CHIPS
IN_SHARD = D_IN_PROJ // N_CHIPS
OUT_SHARD = D_MODEL // N_CHIPS
D_Z = 3 * IN_SHARD
N_META = 16
HALO = 16
RMS_EPS = 1e-6

ADAM_LR = 0.001
ADAM_B1 = 0.9
ADAM_B2 = 0.999
ADAM_EPS = 1e-08
ADAM_WD = 0.01
ADAM_STEP = 10

TM_MIX_FWD = 512
TM_MIX_BWD = 512
SUB_MIX_BWD = 512
TM_FFN = 256
TK_DW = 1024
FF_CHUNK = 1024
VMEM_LIMIT = 56 * 1024 * 1024


def _cparams(n_grid):
    return pltpu.CompilerParams(dimension_semantics=("arbitrary",) * n_grid, vmem_limit_bytes=VMEM_LIMIT)


def _dot(a, b):
    return jnp.dot(a, b, preferred_element_type=F32)


def _dot_nt(a, b):
    return lax.dot_general(a, b, (((1,), (1,)), ((), ())), preferred_element_type=F32)


def _dot_tn(a, b):
    return lax.dot_general(a, b, (((0,), (0,)), ((), ())), preferred_element_type=F32)


def _rows8(v):
    r, c = v.shape
    return v.reshape(r // 8, 8, c).sum(axis=0)


def _rstd(v):
    return lax.rsqrt(jnp.mean(v * v, axis=-1, keepdims=True) + RMS_EPS)


def _rms_bwd(dy, xhat, rstd, gain):
    dyg = dy * gain
    return rstd * (dyg - xhat * jnp.mean(dyg * xhat, axis=-1, keepdims=True))


def _sigmoid(v):
    return 1.0 / (1.0 + jnp.exp(-v))


def _gcols(g):
    return slice(g * POOL_GROUP, (g + 1) * POOL_GROUP)


def _window_sum(e, g, ahead):
    n = e.shape[0]
    w = e
    for level in range(g + 1):
        shift = 1 << level
        w = w + pltpu.roll(w, (n - shift) if ahead else shift, 0)
    return w


def _pool_fwd(pb, g, n):
    e = pb[0:HALO + n, _gcols(g)]
    return _window_sum(e, g, False)[HALO:, :] * (1.0 / (2 << g)) - e[HALO:, :]


def _pool_bwd(qb, g, r0, n):
    e = qb[r0:r0 + n + HALO, _gcols(g)]
    return _window_sum(e, g, True)[0:n, :] * (1.0 / (2 << g)) - e[0:n, :]


def _full(shape):
    nd = len(shape)
    return pl.BlockSpec(shape, lambda *_: (0,) * nd)


ANY = pl.BlockSpec(memory_space=pl.ANY)


def _mesh_pos():
    x, y, c = lax.axis_index("x"), lax.axis_index("y"), lax.axis_index("c")
    chips = [(1 - x, y), (x, 1 - y), (1 - x, 1 - y)]
    return x, y, c, chips


def _half(ref, h):
    hr = ref.shape[0] // 2
    return ref.at[pl.ds(h * hr, hr), :]


class _AllGather:
    PER_ARRAY = 9

    def __init__(self, ins, outs, send_sems, recv_sems):
        self.ins, self.outs, self.send_sems, self.recv_sems = ins, outs, send_sems, recv_sems
        self.n = len(ins)

    @classmethod
    def scratch(cls, n):
        return [pltpu.SemaphoreType.DMA((cls.PER_ARRAY * n,)), pltpu.SemaphoreType.DMA((cls.PER_ARRAY * n,))]

    @staticmethod
    def out_shape(shards):
        return [jax.ShapeDtypeStruct((N_CHIPS,) + s.shape, s.dtype) for s in shards]

    def _copy(self, a, k, src, dst, to):
        i = self.PER_ARRAY * a + k
        return pltpu.make_async_remote_copy(src_ref=src, dst_ref=dst, send_sem=self.send_sems.at[i],
                                            recv_sem=self.recv_sems.at[i], device_id=to, device_id_type=MESH)

    def _piece(self, a, chip, piece, h=None):
        h = lax.axis_index("c") if h is None else h
        rows = self.ins[a].shape[0] // 4
        return self.outs[a].at[chip].at[pl.ds((2 * h + piece) * rows, rows), :]

    def _own(self, a, k):
        x, y, c, chips = _mesh_pos()
        piece = (1, 0, 0, 1)[k]
        rows = self.ins[a].shape[0] // 4
        src = self.ins[a].at[pl.ds((2 * c + piece) * rows, rows), :]
        return self._copy(a, k, src, self._piece(a, 2 * x + y, piece), (*chips[k // 2], c))

    def _relay(self, a, k):
        x, y, c, chips = _mesh_pos()
        source, to, piece = (chips[1], chips[0], 0) if k == 4 else (chips[0], chips[1], 1)
        rows = self._piece(a, 2 * source[0] + source[1], piece)
        return self._copy(a, k, rows, rows, (*to, c))

    def _sibling(self, a, k, h):
        x, y, c, chips = _mesh_pos()
        chip = chips[k - 6]
        slot = _half(self.outs[a].at[2 * chip[0] + chip[1]], h)
        return self._copy(a, k, slot, slot, (x, y, 1 - c))

    def start(self, arrays=None):
        for a in (range(self.n) if arrays is None else arrays):
            for k in range(4):
                self._own(a, k).start()

    def relay(self, a):
        self._own(a, 2).wait_recv()
        self._relay(a, 4).start()
        self._own(a, 0).wait_recv()
        self._relay(a, 5).start()

    def forward(self, a):
        c = lax.axis_index("c")
        self._own(a, 1).wait_recv()
        self._sibling(a, 6, c).start()
        self._own(a, 3).wait_recv()
        self._sibling(a, 7, c).start()
        self._relay(a, 4).wait_recv()
        self._relay(a, 5).wait_recv()
        self._sibling(a, 8, c).start()

    def finish(self, arrays=None):
        c = lax.axis_index("c")
        arrays = range(self.n) if arrays is None else arrays
        for a in arrays:
            for k in range(6, 9):
                self._sibling(a, k, 1 - c).wait_recv()
        for a in arrays:
            for k in range(4):
                self._own(a, k).wait_send()
            for k in range(4, 6):
                self._relay(a, k).wait_send()
            for k in range(6, 9):
                self._sibling(a, k, c).wait_send()


def _fill_own_slot(gathered, shards):
    chip = 2 * lax.axis_index("x") + lax.axis_index("y")
    return [lax.dynamic_update_slice(o, s[None], (chip, 0, 0)) for o, s in zip(gathered, shards)]


class _ExchangeHalves:
    def __init__(self, ins, recvs, send_sems, recv_sems):
        self.ins, self.recvs, self.send_sems, self.recv_sems = ins, recvs, send_sems, recv_sems

    @staticmethod
    def scratch(n):
        return [pltpu.SemaphoreType.DMA((n,)), pltpu.SemaphoreType.DMA((n,))]

    @staticmethod
    def out_shape(grads):
        return [jax.ShapeDtypeStruct((g.shape[0], g.shape[1] // 2, g.shape[2]), g.dtype) for g in grads]

    def _copies(self):
        x, y, c, _ = _mesh_pos()
        out = []
        for a, (src, dst) in enumerate(zip(self.ins, self.recvs)):
            hr = src.shape[1] // 2
            out.append(pltpu.make_async_remote_copy(
                src_ref=src.at[:, pl.ds((1 - c) * hr, hr), :], dst_ref=dst, send_sem=self.send_sems.at[a],
                recv_sem=self.recv_sems.at[a], device_id=(x, y, 1 - c), device_id_type=MESH))
        return out

    def start(self):
        for cp in self._copies():
            cp.start()

    def finish(self):
        for cp in self._copies():
            cp.wait()


def _exchange_halves(grads):
    n = len(grads)

    def body(*refs):
        ex = _ExchangeHalves(refs[:n], refs[n:2 * n], *refs[2 * n:])
        ex.start()
        ex.finish()

    return pl.pallas_call(
        body, name="grad_exchange_halves", out_shape=_ExchangeHalves.out_shape(grads),
        in_specs=[ANY] * n, out_specs=[ANY] * n, scratch_shapes=_ExchangeHalves.scratch(n),
    )(*grads)


class _ScatterToChips:
    def __init__(self, ins, rbufs, send_sems, recv_sems):
        self.ins, self.rbufs, self.send_sems, self.recv_sems = ins, rbufs, send_sems, recv_sems

    @staticmethod
    def scratch(n):
        return [pltpu.SemaphoreType.DMA((3 * n,)), pltpu.SemaphoreType.DMA((3 * n,))]

    @staticmethod
    def out_shape(sums):
        return [jax.ShapeDtypeStruct((3,) + s.shape[1:], BF16) for s in sums]

    def _copies(self):
        x, y, c, chips = _mesh_pos()
        out = []
        for a, (src, dst) in enumerate(zip(self.ins, self.rbufs)):
            for k, chip in enumerate(chips):
                out.append(pltpu.make_async_remote_copy(
                    src_ref=src.at[2 * chip[0] + chip[1]], dst_ref=dst.at[k], send_sem=self.send_sems.at[3 * a + k],
                    recv_sem=self.recv_sems.at[3 * a + k], device_id=(*chip, c), device_id_type=MESH))
        return out

    def start(self):
        for cp in self._copies():
            cp.start()

    def finish(self):
        for cp in self._copies():
            cp.wait()


def _gather_halves(halves):
    n = len(halves)

    def body(*refs):
        ins, outs = refs[:n], refs[n:2 * n]
        send_sems, recv_sems = refs[2 * n:]
        x, y, c, _ = _mesh_pos()
        sib = (x, y, 1 - c)
        remote = [pltpu.make_async_remote_copy(src_ref=ins[a].at[c], dst_ref=outs[a].at[c],
                                               send_sem=send_sems.at[a], recv_sem=recv_sems.at[a],
                                               device_id=sib, device_id_type=MESH) for a in range(n)]
        for cp in remote:
            cp.start()
        for a in range(n):
            pltpu.make_async_remote_copy(src_ref=ins[a].at[1 - c], dst_ref=outs[a].at[1 - c], send_sem=send_sems.at[a],
                                         recv_sem=recv_sems.at[a], device_id=sib, device_id_type=MESH).wait_recv()
        for cp in remote:
            cp.wait_send()

    return pl.pallas_call(
        body, name="grad_gather_halves",
        out_shape=[jax.ShapeDtypeStruct(h.shape, F32) for h in halves],
        in_specs=[ANY] * n, out_specs=[ANY] * n, input_output_aliases={a: a for a in range(n)},
        scratch_shapes=[pltpu.SemaphoreType.DMA((n,)), pltpu.SemaphoreType.DMA((n,))],
    )(*halves)


SMALL_A_ROWS = 24
SMALL_B_ROWS = 8
SMALL_C_ROWS = N_POOL_GROUPS * POOL_GROUP


class _AllReduceSmall:
    N_IN = 10
    SHAPES = [(SMALL_A_ROWS, D_MODEL), (SMALL_B_ROWS, D_CONV), (SMALL_C_ROWS, POOL_GROUP)]

    def __init__(self, ins, outs, scratch):
        self.ins, self.outs = ins, outs
        self.bufs, self.rcvs, self.send_sems, self.recv_sems = scratch[:3], scratch[3:6], scratch[6], scratch[7]

    @classmethod
    def scratch(cls):
        return ([pltpu.VMEM((3,) + s, F32) for s in cls.SHAPES] + [pltpu.VMEM((3,) + s, F32) for s in cls.SHAPES]
                + [pltpu.SemaphoreType.DMA((9,)), pltpu.SemaphoreType.DMA((9,))])

    @classmethod
    def out_shape(cls):
        return [jax.ShapeDtypeStruct(s, F32) for s in cls.SHAPES]

    def _copies(self, st):
        x, y, c, _ = _mesh_pos()
        peer = [(x, y, 1 - c), (1 - x, y, c), (x, 1 - y, c)][st]
        return [pltpu.make_async_remote_copy(
            src_ref=buf.at[st], dst_ref=rcv.at[st], send_sem=self.send_sems.at[3 * st + i],
            recv_sem=self.recv_sems.at[3 * st + i], device_id=peer, device_id_type=MESH)
            for i, (buf, rcv) in enumerate(zip(self.bufs, self.rcvs))]

    def pack_and_send(self):
        dg1_ref, dg1m_ref, dg2_ref, dg3_ref, dg4_ref, loss_ref, dmeta_ref, dsc_ref, dcw_ref, dpw_ref = self.ins
        a_buf, b_buf, c_buf = self.bufs

        def rowsum(v):
            return jnp.sum(v, axis=0, keepdims=True)

        a_buf[0, 0:1, :] = rowsum(dg1_ref[...] + dg1m_ref[...])
        a_buf[0, 1:2, :] = rowsum(dg2_ref[...])
        a_buf[0, 2:3, :] = rowsum(dg3_ref[...])
        a_buf[0, 3:4, :] = rowsum(dg4_ref[...])
        loss = jnp.sum(rowsum(loss_ref[...]), axis=1, keepdims=True) * (0.5 / D_MODEL)
        a_buf[0, 4:5, :] = jnp.broadcast_to(loss, (1, D_MODEL))
        a_buf[0, 5:8, :] = jnp.zeros((3, D_MODEL), F32)
        a_buf[0, 8:24, :] = dmeta_ref[...]
        b_buf[0, 0:1, :] = rowsum(dsc_ref[...])
        for k in range(3):
            b_buf[0, 1 + k:2 + k, :] = rowsum(dcw_ref[8 * k:8 * k + 8, :])
        b_buf[0, 4:8, :] = jnp.zeros((4, D_CONV), F32)
        c_buf[0] = dpw_ref[...]
        for cp in self._copies(0):
            cp.start()

    def combine(self, st):
        for cp in self._copies(st):
            cp.wait()
        if st < 2:
            for buf, rcv in zip(self.bufs, self.rcvs):
                buf[st + 1] = buf[st] + rcv[st]
            for cp in self._copies(st + 1):
                cp.start()
        else:
            for out, buf, rcv in zip(self.outs, self.bufs, self.rcvs):
                out[...] = buf[st] + rcv[st]


def _row_block(rows):
    for cand in (512, 448, 384, 352, 320, 256, 128, 64, 32, 16):
        if rows % cand == 0:
            return cand
    return rows


def _add_pairs(grad, recv, place):
    n_sh, rows2, cols = grad.shape
    hr = rows2 // 2
    br = _row_block(hr)

    def body(place_ref, a_ref, b_ref, o_ref):
        o_ref[...] = (a_ref[0] + b_ref[...]).astype(BF16)

    return pl.pallas_call(
        body, name="grad_add_pairs",
        grid_spec=pltpu.PrefetchScalarGridSpec(
            num_scalar_prefetch=1, grid=(n_sh, hr // br),
            in_specs=[pl.BlockSpec((1, 1, br, cols), lambda j, i, p: (j, p[1], i, 0)),
                      pl.BlockSpec((1, br, cols), lambda j, i, p: (j, i, 0))],
            out_specs=pl.BlockSpec((1, br, cols), lambda j, i, p: (j, i, 0))),
        out_shape=jax.ShapeDtypeStruct((n_sh, hr, cols), BF16), compiler_params=_cparams(2),
    )(place, grad.reshape(n_sh, 2, hr, cols), recv)


def _add_chips(grads, recvs, rbufs, place, scattered=(), name="grad_add_chips"):
    n, n_sc = len(grads), len(scattered)
    n_sh, rows2, cols = grads[0].shape
    hr = rows2 // 2
    br = _row_block(hr)
    n_steps = hr // br

    def body(place_ref, *refs):
        a_refs, b_refs, r_refs = refs[:n], refs[n:2 * n], refs[2 * n:3 * n]
        o_refs = refs[3 * n + n_sc:4 * n + n_sc]
        if n_sc:
            scatter = _ScatterToChips(refs[3 * n:3 * n + n_sc], refs[4 * n + n_sc:4 * n + 2 * n_sc], *refs[-2:])

            @pl.when(pl.program_id(0) == 0)
            def _():
                scatter.start()

        for a_ref, b_ref, r_ref, o_ref in zip(a_refs, b_refs, r_refs, o_refs):
            own = a_ref[0, 0] + b_ref[0]
            o_ref[0] = ((own + r_ref[0].astype(F32)) + r_ref[1].astype(F32)) + r_ref[2].astype(F32)

        if n_sc:
            @pl.when(pl.program_id(0) == n_steps - 1)
            def _():
                scatter.finish()

    outs = pl.pallas_call(
        body, name=name,
        grid_spec=pltpu.PrefetchScalarGridSpec(
            num_scalar_prefetch=1, grid=(n_steps,),
            in_specs=[pl.BlockSpec((1, 1, br, cols), lambda i, p: (p[0], p[1], i, 0))] * n
            + [pl.BlockSpec((1, br, cols), lambda i, p: (p[0], i, 0))] * n
            + [pl.BlockSpec((3, br, cols), lambda i, p: (0, i, 0))] * n + [ANY] * n_sc,
            out_specs=[pl.BlockSpec((1, br, cols), lambda i, p: (p[1], i, 0))] * n + [ANY] * n_sc,
            scratch_shapes=_ScatterToChips.scratch(n_sc) if n_sc else []),
        out_shape=[jax.ShapeDtypeStruct((2, hr, cols), F32)] * n + _ScatterToChips.out_shape(list(scattered)),
        compiler_params=_cparams(1),
    )(place, *[g.reshape(n_sh, 2, hr, cols) for g in grads], *recvs, *rbufs, *scattered)
    return outs[:n], outs[n:]


def _adamw_math(w, g, m, v):
    m2 = ADAM_B1 * m + (1.0 - ADAM_B1) * g
    v2 = ADAM_B2 * v + (1.0 - ADAM_B2) * (g * g)
    m_hat = m2 / (1.0 - ADAM_B1 ** ADAM_STEP)
    v_hat = v2 / (1.0 - ADAM_B2 ** ADAM_STEP)
    delta = -ADAM_LR * (m_hat / (jnp.sqrt(v_hat) + ADAM_EPS) + ADAM_WD * w)
    return delta, m2, v2


def _adamw_big(w, g, m, v):
    rows, cols = w.shape
    br = _row_block(rows)

    def body(w_ref, g_ref, m_ref, v_ref, d_ref, m2_ref, v2_ref):
        d, m2, v2 = _adamw_math(w_ref[...], g_ref[...], m_ref[...], v_ref[...])
        d_ref[...] = d
        m2_ref[...] = m2
        v2_ref[...] = v2

    spec = pl.BlockSpec((br, cols), lambda i: (i, 0))
    return pl.pallas_call(
        body, name="adamw_big", grid=(rows // br,),
        out_shape=[jax.ShapeDtypeStruct((rows, cols), F32)] * 3,
        in_specs=[spec] * 4, out_specs=[spec] * 3, compiler_params=_cparams(1),
    )(w, g, m, v)


def _adamw_small(groups):
    n = len(groups)

    def body(*refs):
        ins, outs = refs[:4 * n], refs[4 * n:]
        for i in range(n):
            w, g, m, v = (r[...] for r in ins[4 * i:4 * i + 4])
            d, m2, v2 = _adamw_math(w, g, m, v)
            outs[3 * i][...] = d
            outs[3 * i + 1][...] = m2
            outs[3 * i + 2][...] = v2

    vm = pl.BlockSpec(memory_space=pltpu.VMEM)
    flat = [a for grp in groups for a in grp]
    out_shape = [jax.ShapeDtypeStruct(grp[0].shape, F32) for grp in groups for _ in range(3)]
    outs = pl.pallas_call(body, name="adamw_small", out_shape=out_shape,
                          in_specs=[vm] * (4 * n), out_specs=[vm] * (3 * n))(*flat)
    return [tuple(outs[3 * i:3 * i + 3]) for i in range(n)]


def _load_weights(pairs, sem):
    for src, dst in pairs:
        cp = pltpu.make_async_copy(src, dst, sem)
        cp.start()
        cp.wait()


N_MIX_SHARDS = 3


def _mixer_fwd(x3, g1, g2, poolw, pscale, shards):
    n_seq, seq, _ = x3.shape
    tm = min(TM_MIX_FWD, seq)
    n_t = seq // tm
    n_steps = n_seq * n_t
    n_ag = len(shards)
    n_ffn = n_ag - N_MIX_SHARDS
    small_rows = shards[2].shape[0]
    conv_cols = D_CONV // N_CHIPS

    def body(x_ref, g1_ref, g2_ref, pw_ref, ps_ref, *rest):
        ag = _AllGather(rest[:n_ag], rest[n_ag + 10:2 * n_ag + 10], *rest[-2:])
        (z_ref, m_ref, h1_ref, a_ref, conv_ref, pooled_ref, yc_ref, zm_ref, meta_ref,
         cw_ref) = rest[n_ag:n_ag + 10]
        win_v, wout_v, small_v, cvb, pb, load_sems = rest[2 * n_ag + 10:-2]
        s, t = pl.program_id(0), pl.program_id(1)
        step = s * n_t + t

        @pl.when(step == 0)
        def _():
            ag.start(range(N_MIX_SHARDS))
            for a in range(N_MIX_SHARDS):
                ag.relay(a)
            for a in range(N_MIX_SHARDS):
                ag.forward(a)
            ag.finish(range(N_MIX_SHARDS))
            ag.start(range(N_MIX_SHARDS, n_ag))
            me = 2 * lax.axis_index("x") + lax.axis_index("y")
            dsts = [lambda j: win_v.at[j], lambda j: wout_v.at[pl.ds(j * OUT_SHARD, OUT_SHARD), :],
                    lambda j: small_v.at[j]]

            def loads(j, own):
                return [pltpu.make_async_copy(ag.ins[a] if own else ag.outs[a].at[j], dsts[a](j),
                                              load_sems.at[N_MIX_SHARDS * j + a]) for a in range(N_MIX_SHARDS)]

            for wait in (False, True):
                for j in range(N_CHIPS):
                    for own in (False, True):
                        @pl.when((me == j) == own)
                        def _():
                            for cp in loads(j, own):
                                cp.wait() if wait else cp.start()

            meta = jnp.concatenate([small_v[j, 0:N_META, :] for j in range(N_CHIPS)], axis=1)
            meta_ref[...] = meta
            cw_ref[...] = jnp.concatenate([small_v[j, N_META:N_META + 3, 0:conv_cols] for j in range(N_CHIPS)], axis=1)
            a_meta = (meta * _rstd(meta) * g1_ref[...]).astype(BF16)
            for j in range(N_CHIPS):
                zm_ref[:, j * IN_SHARD:(j + 1) * IN_SHARD] = _dot(a_meta, win_v[j])

        for i in range(n_ffn):
            @pl.when(step == ((i + 1) * n_steps) // (2 * n_ffn + 2))
            def _():
                ag.relay(N_MIX_SHARDS + i)

        for i in range(n_ffn):
            @pl.when(step == min(n_steps // 2 + ((i + 1) * n_steps) // (2 * n_ffn + 2), n_steps - 1))
            def _():
                ag.forward(N_MIX_SHARDS + i)

        @pl.when(t == 0)
        def _():
            cvb[0:HALO, :] = zm_ref[:, IN_SHARD:2 * IN_SHARD] * zm_ref[:, 2 * IN_SHARD:3 * IN_SHARD]
            pb[0:HALO, :] = zm_ref[:, 3 * IN_SHARD:4 * IN_SHARD]

        @pl.when(t > 0)
        def _():
            cvb[0:HALO, :] = cvb[tm:tm + HALO, :]
            pb[0:HALO, :] = pb[tm:tm + HALO, :]

        xt = x_ref[0]
        a = (xt * _rstd(xt) * g1_ref[...]).astype(BF16)
        a_ref[...] = a
        zb = _dot(a, win_v[0])
        zc = _dot(a, win_v[1])
        zv = _dot(a, win_v[2])
        zp = _dot(a, win_v[3])
        z_ref[0, :, 0:IN_SHARD] = zb
        z_ref[0, :, IN_SHARD:2 * IN_SHARD] = zc
        z_ref[0, :, 2 * IN_SHARD:3 * IN_SHARD] = zv
        cv = zc * zv
        cvb[HALO:HALO + tm, :] = cv
        pb[HALO:HALO + tm, :] = zp
        cw = cw_ref[...]
        conv = cw[0:1] * cvb[HALO - 2:HALO - 2 + tm, :] + cw[1:2] * cvb[HALO - 1:HALO - 1 + tm, :] + cw[2:3] * cv
        conv_ref[...] = conv
        parts = [(zb * conv).astype(BF16)]
        for g in range(N_POOL_GROUPS):
            pooled = _pool_fwd(pb, g, tm).astype(BF16)
            pooled_ref[:, _gcols(g)] = pooled
            parts.append((_dot(pooled, pw_ref[g]) * ps_ref[:, _gcols(g)]).astype(BF16))
        ycat = jnp.concatenate(parts, axis=1)
        yc_ref[...] = ycat
        m = _dot(ycat, wout_v[...])
        m_ref[0] = m
        h1_ref[0] = xt + m * _rstd(m) * g2_ref[...]

        @pl.when(step == n_steps - 1)
        def _():
            ag.finish(range(N_MIX_SHARDS, n_ag))

    n_rows = n_seq * seq
    row = lambda c: pl.BlockSpec((1, tm, c), lambda s, t: (s, t, 0))
    row2 = lambda c: pl.BlockSpec((tm, c), lambda s, t: (s * n_t + t, 0))
    outs = pl.pallas_call(
        body, name="mixer_fwd", grid=(n_seq, n_t),
        out_shape=[jax.ShapeDtypeStruct((n_seq, seq, D_Z), F32), jax.ShapeDtypeStruct((n_seq, seq, D_MODEL), F32),
                   jax.ShapeDtypeStruct((n_seq, seq, D_MODEL), F32), jax.ShapeDtypeStruct((n_rows, D_MODEL), BF16),
                   jax.ShapeDtypeStruct((n_rows, D_CONV), F32), jax.ShapeDtypeStruct((n_rows, D_POOL), BF16),
                   jax.ShapeDtypeStruct((n_rows, D_MODEL), BF16), jax.ShapeDtypeStruct((N_META, D_IN_PROJ), F32),
                   jax.ShapeDtypeStruct((N_META, D_MODEL), F32), jax.ShapeDtypeStruct((3, D_CONV), F32)]
        + _AllGather.out_shape(shards),
        in_specs=[row(D_MODEL), _full((1, D_MODEL)), _full((1, D_MODEL)),
                  _full((N_POOL_GROUPS, POOL_GROUP, POOL_GROUP)), _full((1, D_POOL))] + [ANY] * n_ag,
        out_specs=[row(D_Z), row(D_MODEL), row(D_MODEL), row2(D_MODEL), row2(D_CONV), row2(D_POOL), row2(D_MODEL),
                   _full((N_META, D_IN_PROJ)), _full((N_META, D_MODEL)), _full((3, D_CONV))] + [ANY] * n_ag,
        scratch_shapes=[pltpu.VMEM((N_CHIPS, D_MODEL, IN_SHARD), BF16), pltpu.VMEM((D_MODEL, D_MODEL), BF16),
                        pltpu.VMEM((N_CHIPS, small_rows, D_MODEL // N_CHIPS), F32),
                        pltpu.VMEM((HALO + tm, D_CONV), F32), pltpu.VMEM((HALO + tm, D_POOL), F32),
                        pltpu.SemaphoreType.DMA((N_MIX_SHARDS * N_CHIPS,))] + _AllGather.scratch(n_ag),
        compiler_params=_cparams(2),
    )(x3, g1, g2, poolw, pscale, *shards)
    return outs[:10], _fill_own_slot(outs[10:], shards)


def _ffn_chunks():
    out, r0 = [], 0
    while r0 < D_FF:
        out.append((r0, min(FF_CHUNK, D_FF - r0)))
        r0 += FF_CHUNK
    return out


def _ffn_fwd_bwd(h1, target, g3, g4, wg_t, wu_t, wd):
    n_rows = h1.shape[0]
    tm = min(TM_FFN, n_rows)
    chunks = _ffn_chunks()

    def body(h1_ref, t_ref, g3_ref, g4_ref, wg_hbm, wu_hbm, wd_hbm,
             dh1_ref, f_ref, dd_ref, ds_ref, du_ref, gg_ref, loss_ref, dg3_ref, dg4_ref,
             wg_v, wu_v, wd_v, s_sc, u_sc, sem):
        @pl.when(pl.program_id(0) == 0)
        def _():
            _load_weights([(wg_hbm, wg_v), (wu_hbm, wu_v), (wd_hbm, wd_v)], sem)
            loss_ref[...] = jnp.zeros_like(loss_ref)
            dg3_ref[...] = jnp.zeros_like(dg3_ref)
            dg4_ref[...] = jnp.zeros_like(dg4_ref)

        h1v = h1_ref[...]
        r3 = _rstd(h1v)
        hh = h1v * r3
        g3v, g4v = g3_ref[...], g4_ref[...]
        f = (hh * g3v).astype(BF16)
        f_ref[...] = f
        d = jnp.zeros((tm, D_MODEL), F32)
        for r0, sz in chunks:
            s = _dot_nt(f, wg_v[r0:r0 + sz, :])
            u = _dot_nt(f, wu_v[r0:r0 + sz, :])
            s_sc[:, r0:r0 + sz] = s
            u_sc[:, r0:r0 + sz] = u
            gc = (s * _sigmoid(s) * u).astype(BF16)
            gg_ref[:, r0:r0 + sz] = gc
            d = d + _dot(gc, wd_v[r0:r0 + sz, :])
        r4 = _rstd(d)
        dh = d * r4
        err = (h1v + dh * g4v) - t_ref[...]
        loss_ref[...] += _rows8(err * err)
        dy = err * (1.0 / D_MODEL)
        dg4_ref[...] += _rows8(dy * dh)
        ddb = _rms_bwd(dy, dh, r4, g4v).astype(BF16)
        dd_ref[...] = ddb
        df = jnp.zeros((tm, D_MODEL), F32)
        for r0, sz in chunks:
            dgg = _dot_nt(ddb, wd_v[r0:r0 + sz, :])
            s = s_sc[:, r0:r0 + sz]
            u = u_sc[:, r0:r0 + sz]
            sig = _sigmoid(s)
            dsc = (dgg * u * (sig * (1.0 + s * (1.0 - sig)))).astype(BF16)
            duc = (dgg * (s * sig)).astype(BF16)
            ds_ref[:, r0:r0 + sz] = dsc
            du_ref[:, r0:r0 + sz] = duc
            df = df + _dot(dsc, wg_v[r0:r0 + sz, :]) + _dot(duc, wu_v[r0:r0 + sz, :])
        dg3_ref[...] += _rows8(df * hh)
        dh1_ref[...] = dy + _rms_bwd(df, hh, r3, g3v)

    row = pl.BlockSpec((tm, D_MODEL), lambda i: (i, 0))
    ffrow = pl.BlockSpec((tm, D_FF), lambda i: (i, 0))
    acc = _full((8, D_MODEL))
    act_bf = jax.ShapeDtypeStruct((n_rows, D_MODEL), BF16)
    ff_bf = jax.ShapeDtypeStruct((n_rows, D_FF), BF16)
    acc_shape = jax.ShapeDtypeStruct((8, D_MODEL), F32)
    w_vmem = pltpu.VMEM((D_FF, D_MODEL), BF16)
    return pl.pallas_call(
        body, name="ffn_fwd_bwd", grid=(n_rows // tm,),
        out_shape=[jax.ShapeDtypeStruct((n_rows, D_MODEL), F32), act_bf, act_bf, ff_bf, ff_bf, ff_bf,
                   acc_shape, acc_shape, acc_shape],
        in_specs=[row, row, _full((1, D_MODEL)), _full((1, D_MODEL)), ANY, ANY, ANY],
        out_specs=[row, row, row, ffrow, ffrow, ffrow, acc, acc, acc],
        scratch_shapes=[w_vmem, w_vmem, w_vmem, pltpu.VMEM((tm, D_FF), F32), pltpu.VMEM((tm, D_FF), F32),
                        pltpu.SemaphoreType.DMA],
        compiler_params=_cparams(1),
    )(h1, target, g3, g4, wg_t, wu_t, wd)


def _ffn_weight_grads(name, acts, other, exchanged):
    n_rows = other.shape[0]
    n_a, n_ex = len(acts), len(exchanged)
    n_c = n_a
    tk = min(TK_DW, n_rows)
    n_k = n_rows // tk
    half = D_FF // n_c

    def body(other_ref, *rest):
        act_refs = rest[:n_a]
        out_refs = rest[n_a + n_ex:2 * n_a + n_ex]
        c, k = pl.program_id(0), pl.program_id(1)
        if n_ex:
            ex = _ExchangeHalves(rest[n_a:n_a + n_ex], rest[2 * n_a + n_ex:2 * n_a + 2 * n_ex], *rest[-2:])

            @pl.when((c == 0) & (k == 0))
            def _():
                ex.start()

        @pl.when(k == 0)
        def _():
            for o in out_refs:
                o[...] = jnp.zeros_like(o)

        ov = other_ref[...]
        for a, o in zip(act_refs, out_refs):
            o[...] += _dot_tn(a[...], ov)

        if n_ex:
            @pl.when((c == n_c - 1) & (k == n_k - 1))
            def _():
                ex.finish()

    row = pl.BlockSpec((tk, D_MODEL), lambda c, k: (k, 0))
    ffrow = pl.BlockSpec((tk, half), lambda c, k: (k, c))
    out = pl.BlockSpec((half, D_MODEL), lambda c, k: (c, 0))
    outs = pl.pallas_call(
        body, name=name, grid=(n_c, n_k),
        out_shape=[jax.ShapeDtypeStruct((D_FF, D_MODEL), F32)] * n_a + _ExchangeHalves.out_shape(exchanged),
        in_specs=[row] + [ffrow] * n_a + [ANY] * n_ex, out_specs=[out] * n_a + [ANY] * n_ex,
        scratch_shapes=_ExchangeHalves.scratch(n_ex) if n_ex else [],
        compiler_params=_cparams(2),
    )(other, *acts, *exchanged)
    return outs[:n_a], outs[n_a:]


def _mixer_bwd(dh1, m3, z3, conv2, pooled2, x3, zmeta, g1, g2, convw, poolw, pscale, win_all, wout, exchanged,
               scattered):
    n_seq, seq, _ = x3.shape
    tm = min(TM_MIX_BWD, seq)
    sub = min(SUB_MIX_BWD, tm)
    n_t = seq // tm
    n_ex, n_sc = len(exchanged), len(scattered)
    n_cm = n_ex + n_sc

    def body(dh1_ref, m_ref, z_ref, conv_ref, pooled_ref, x_ref, zm_ref, g1_ref, g2_ref, cw_ref, pw_ref, ps_ref,
             win_hbm, wout_hbm, *rest):
        outs0 = n_cm + 9
        ex = _ExchangeHalves(rest[:n_ex], rest[outs0:outs0 + n_ex], *rest[-4:-2])
        sc = _ScatterToChips(rest[n_ex:n_cm], rest[outs0 + n_ex:outs0 + n_cm], *rest[-2:])
        dx_ref, dz_ref, dm_ref, dg1_ref, dg2_ref, dsc_ref, dcw_ref, dpw_ref, dzm_ref = rest[n_cm:outs0]
        win_v, wout_v, dcb, dqb, mcb, mqb, sem = rest[outs0 + n_cm:-4]
        s, i = pl.program_id(0), pl.program_id(1)
        tr = n_t - 1 - i

        @pl.when((s == 0) & (i == 0))
        def _():
            sc.start()
            ex.start()
            _load_weights([(win_hbm, win_v), (wout_hbm, wout_v)], sem)
            for ref in (dg1_ref, dg2_ref, dsc_ref, dcw_ref, dpw_ref, dzm_ref):
                ref[...] = jnp.zeros_like(ref)

        @pl.when(i == 0)
        def _():
            dcb[tm:tm + HALO, :] = jnp.zeros((HALO, D_CONV), F32)
            dqb[tm:tm + HALO, :] = jnp.zeros((HALO, D_POOL), F32)

        @pl.when(i > 0)
        def _():
            dcb[tm:tm + HALO, :] = dcb[0:HALO, :]
            dqb[tm:tm + HALO, :] = dqb[0:HALO, :]

        g1v, g2v = g1_ref[...], g2_ref[...]
        cw = cw_ref[...]

        for r0 in range(tm - sub, -1, -sub):
            rows = slice(r0, r0 + sub)
            dh1v = dh1_ref[0, rows, :]
            mv = m_ref[0, rows, :]
            r2 = _rstd(mv)
            mh = mv * r2
            dg2_ref[...] += _rows8(dh1v * mh)
            dmb = _rms_bwd(dh1v, mh, r2, g2v).astype(BF16)
            dm_ref[rows, :] = dmb
            dyc = _dot_nt(dmb, wout_v[...])
            dyconv = dyc[:, 0:D_CONV]

            for g in range(N_POOL_GROUPS):
                pooled = pooled_ref[rows, _gcols(g)]
                mixed = _dot(pooled, pw_ref[g])
                scale = ps_ref[:, _gcols(g)]
                dyp = dyc[:, D_CONV + g * POOL_GROUP:D_CONV + (g + 1) * POOL_GROUP]
                dsc_ref[:, _gcols(g)] += _rows8(dyp * mixed)
                dmix = (dyp * scale).astype(BF16)
                dpw_ref[g] += _dot_tn(pooled, dmix)
                dqb[rows, _gcols(g)] = _dot_nt(dmix, pw_ref[g])

            zb = z_ref[0, rows, 0:IN_SHARD]
            zc = z_ref[0, rows, IN_SHARD:2 * IN_SHARD]
            zv = z_ref[0, rows, 2 * IN_SHARD:3 * IN_SHARD]
            dconv = dyconv * zb
            dcb[rows, :] = dconv
            d1 = dcb[r0 + 1:r0 + 1 + sub, :]
            d2 = dcb[r0 + 2:r0 + 2 + sub, :]
            dcv = cw[2:3] * dconv + cw[1:2] * d1 + cw[0:1] * d2
            cv = zc * zv
            dcw_ref[0:8, :] += _rows8(cv * d2)
            dcw_ref[8:16, :] += _rows8(cv * d1)
            dcw_ref[16:24, :] += _rows8(cv * dconv)
            dzs = [(dyconv * conv_ref[rows, :]).astype(BF16), (dcv * zv).astype(BF16), (dcv * zc).astype(BF16),
                   jnp.concatenate([_pool_bwd(dqb, g, r0, sub) for g in range(N_POOL_GROUPS)], axis=1).astype(BF16)]
            da = jnp.zeros((sub, D_MODEL), F32)
            for j in range(N_CHIPS):
                dz_ref[j, rows, :] = dzs[j]
                da = da + _dot_nt(dzs[j], win_v[j])
            xt = x_ref[0, rows, :]
            r1 = _rstd(xt)
            xh = xt * r1
            dg1_ref[...] += _rows8(da * xh)
            dx_ref[0, rows, :] = dh1v + _rms_bwd(da, xh, r1, g1v)

        @pl.when(tr == 0)
        def _():
            mcb[0:HALO, :] = jnp.zeros((HALO, D_CONV), F32)
            mqb[0:HALO, :] = jnp.zeros((HALO, D_POOL), F32)
            mcb[HALO:2 * HALO, :] = dcb[0:HALO, :]
            mqb[HALO:2 * HALO, :] = dqb[0:HALO, :]
            m1 = mcb[1:1 + HALO, :]
            m2 = mcb[2:2 + HALO, :]
            zc_m = zm_ref[:, IN_SHARD:2 * IN_SHARD]
            zv_m = zm_ref[:, 2 * IN_SHARD:3 * IN_SHARD]
            cv_m = zc_m * zv_m
            dcw_ref[0:8, :] += _rows8(cv_m * m2)
            dcw_ref[8:16, :] += _rows8(cv_m * m1)
            dcv_m = cw[1:2] * m1 + cw[0:1] * m2
            dzm_ref[:, IN_SHARD:2 * IN_SHARD] += dcv_m * zv_m
            dzm_ref[:, 2 * IN_SHARD:3 * IN_SHARD] += dcv_m * zc_m
            dzm_ref[:, 3 * IN_SHARD:4 * IN_SHARD] += jnp.concatenate(
                [_pool_bwd(mqb, g, 0, HALO) for g in range(N_POOL_GROUPS)], axis=1)

        @pl.when((s == n_seq - 1) & (i == n_t - 1))
        def _():
            ex.finish()
            sc.finish()

    row3 = lambda c: pl.BlockSpec((1, tm, c), lambda s, i: (s, n_t - 1 - i, 0))
    row2 = lambda c: pl.BlockSpec((tm, c), lambda s, i: (s * n_t + n_t - 1 - i, 0))
    n_rows = n_seq * seq
    outs = pl.pallas_call(
        body, name="mixer_bwd", grid=(n_seq, n_t),
        out_shape=[jax.ShapeDtypeStruct((n_seq, seq, D_MODEL), F32),
                   jax.ShapeDtypeStruct((N_CHIPS, n_rows, IN_SHARD), BF16), jax.ShapeDtypeStruct((n_rows, D_MODEL), BF16),
                   jax.ShapeDtypeStruct((8, D_MODEL), F32), jax.ShapeDtypeStruct((8, D_MODEL), F32),
                   jax.ShapeDtypeStruct((8, D_POOL), F32), jax.ShapeDtypeStruct((24, D_CONV), F32),
                   jax.ShapeDtypeStruct((N_POOL_GROUPS, POOL_GROUP, POOL_GROUP), F32),
                   jax.ShapeDtypeStruct((N_META, D_IN_PROJ), F32)]
        + _ExchangeHalves.out_shape(exchanged) + _ScatterToChips.out_shape(scattered),
        in_specs=[row3(D_MODEL), row3(D_MODEL), row3(D_Z), row2(D_CONV), row2(D_POOL), row3(D_MODEL),
                  _full((N_META, D_IN_PROJ)), _full((1, D_MODEL)), _full((1, D_MODEL)), _full((3, D_CONV)),
                  _full((N_POOL_GROUPS, POOL_GROUP, POOL_GROUP)), _full((1, D_POOL)), ANY, ANY] + [ANY] * n_cm,
        out_specs=[row3(D_MODEL), pl.BlockSpec((N_CHIPS, tm, IN_SHARD), lambda s, i: (0, s * n_t + n_t - 1 - i, 0)),
                   row2(D_MODEL),
                   _full((8, D_MODEL)), _full((8, D_MODEL)), _full((8, D_POOL)), _full((24, D_CONV)),
                   _full((N_POOL_GROUPS, POOL_GROUP, POOL_GROUP)), _full((N_META, D_IN_PROJ))] + [ANY] * n_cm,
        scratch_shapes=[pltpu.VMEM((N_CHIPS, D_MODEL, IN_SHARD), BF16), pltpu.VMEM((D_MODEL, D_MODEL), BF16),
                        pltpu.VMEM((tm + HALO, D_CONV), F32), pltpu.VMEM((tm + HALO, D_POOL), F32),
                        pltpu.VMEM((2 * HALO, D_CONV), F32), pltpu.VMEM((2 * HALO, D_POOL), F32),
                        pltpu.SemaphoreType.DMA] + _ExchangeHalves.scratch(n_ex) + _ScatterToChips.scratch(n_sc),
        compiler_params=_cparams(2),
    )(dh1, m3, z3, conv2, pooled2, x3, zmeta, g1, g2, convw, poolw, pscale, win_all, wout, *exchanged, *scattered)
    return outs[:9], outs[9:9 + n_ex], outs[9 + n_ex:]


def _meta_bwd(dzm, meta_full, g1, win_all):
    def body(dzm_ref, meta_ref, g1_ref, win_ref, dmeta_ref, dg1_ref, a_ref, dzb_ref):
        xm = meta_ref[...]
        r = _rstd(xm)
        xh = xm * r
        g1v = g1_ref[...]
        a_ref[...] = (xh * g1v).astype(BF16)
        da = jnp.zeros((N_META, D_MODEL), F32)
        for j in range(N_CHIPS):
            dzj = dzm_ref[:, j * IN_SHARD:(j + 1) * IN_SHARD].astype(BF16)
            dzb_ref[j] = dzj
            da = da + _dot_nt(dzj, win_ref[j])
        dg1_ref[...] = _rows8(da * xh)
        dmeta_ref[...] = _rms_bwd(da, xh, r, g1v)

    vm = pl.BlockSpec(memory_space=pltpu.VMEM)
    return pl.pallas_call(
        body, name="meta_bwd",
        out_shape=[jax.ShapeDtypeStruct((N_META, D_MODEL), F32), jax.ShapeDtypeStruct((8, D_MODEL), F32),
                   jax.ShapeDtypeStruct((N_META, D_MODEL), BF16), jax.ShapeDtypeStruct((N_CHIPS, N_META, IN_SHARD), BF16)],
        in_specs=[vm] * 4, out_specs=[vm] * 4,
    )(dzm, meta_full, g1, win_all)


def _mixer_weight_grads(a, dz, ycat, dm, a_meta, dz_meta, ffn_sums, small):
    n_rows = a.shape[0]
    tk = min(TK_DW, n_rows)
    n_k = n_rows // tk
    n_sc, n_sm = len(ffn_sums), _AllReduceSmall.N_IN

    def body(a_ref, dz_ref, yc_ref, dm_ref, am_ref, dzm_ref, *rest):
        ins, outs, scratch = rest[:n_sc + n_sm], rest[n_sc + n_sm:2 * n_sc + n_sm + 5], rest[2 * n_sc + n_sm + 5:]
        dwin_ref, dwout_ref = outs[:2]
        scatter = _ScatterToChips(ins[:n_sc], outs[2:2 + n_sc], *scratch[:2])
        reduce_small = _AllReduceSmall(ins[n_sc:], outs[2 + n_sc:], scratch[2:])
        k = pl.program_id(0)

        @pl.when(k == 0)
        def _():
            scatter.start()
            reduce_small.pack_and_send()
            am_t = am_ref[...].T
            for j in range(N_CHIPS):
                dwin_ref[j] = _dot(am_t, dzm_ref[j])
            dwout_ref[...] = jnp.zeros_like(dwout_ref)

        for st in range(2):
            @pl.when(k == ((st + 1) * n_k) // 3)
            def _():
                reduce_small.combine(st)

        a_t = a_ref[...].T
        for j in range(N_CHIPS):
            dwin_ref[j] += _dot(a_t, dz_ref[j])
        dwout_ref[...] += _dot_tn(yc_ref[...], dm_ref[...])

        @pl.when(k == n_k - 1)
        def _():
            reduce_small.combine(2)
            scatter.finish()

    row = pl.BlockSpec((tk, D_MODEL), lambda k: (k, 0))
    outs = pl.pallas_call(
        body, name="mixer_weight_grads", grid=(n_k,),
        out_shape=[jax.ShapeDtypeStruct((N_CHIPS, D_MODEL, IN_SHARD), F32),
                   jax.ShapeDtypeStruct((D_MODEL, D_MODEL), F32)] + _ScatterToChips.out_shape(ffn_sums)
        + _AllReduceSmall.out_shape(),
        in_specs=[row, pl.BlockSpec((N_CHIPS, tk, IN_SHARD), lambda k: (0, k, 0)), row, row,
                  _full((N_META, D_MODEL)), _full((N_CHIPS, N_META, IN_SHARD))] + [ANY] * n_sc
        + [_full(s.shape) for s in small],
        out_specs=[_full((N_CHIPS, D_MODEL, IN_SHARD)), _full((D_MODEL, D_MODEL))] + [ANY] * n_sc
        + [_full(s) for s in _AllReduceSmall.SHAPES],
        scratch_shapes=_ScatterToChips.scratch(n_sc) + _AllReduceSmall.scratch(),
        compiler_params=_cparams(1),
    )(a, dz, ycat, dm, a_meta, dz_meta, *ffn_sums, *small)
    return ([outs[0], outs[1].reshape(N_CHIPS, OUT_SHARD, D_MODEL)], outs[2:2 + n_sc], outs[2 + n_sc:])


def kernel(x, meta_tokens, norm_mix_pre, w_in, conv_w, pool_w, pool_scale, w_out, norm_mix_post, norm_ffn_pre, w_gate, w_up, w_down, norm_ffn_post, loss_target, m_meta_tokens, m_norm_mix_pre, m_w_in, m_conv_w, m_pool_w, m_pool_scale, m_w_out, m_norm_mix_post, m_norm_ffn_pre, m_w_gate, m_w_up, m_w_down, m_norm_ffn_post, v_meta_tokens, v_norm_mix_pre, v_w_in, v_conv_w, v_pool_w, v_pool_scale, v_w_out, v_norm_mix_post, v_norm_ffn_pre, v_w_gate, v_w_up, v_w_down, v_norm_ffn_post):
    n_seq, seq, _ = x.shape
    n_rows = n_seq * seq
    chip = 2 * lax.axis_index("x") + lax.axis_index("y")
    meta_cols = D_MODEL // N_CHIPS
    conv_cols = D_CONV // N_CHIPS

    small = jnp.zeros((2 * HALO, meta_cols), F32)
    small = small.at[0:N_META, :].set(meta_tokens).at[N_META:N_META + 3, 0:conv_cols].set(conv_w[0])
    poolw_bf = pool_w[0].astype(BF16)
    pscale = pool_scale
    g1, g2, g3, g4 = norm_mix_pre, norm_mix_post, norm_ffn_pre, norm_ffn_post
    place = jnp.stack([chip, lax.axis_index("c")]).astype(jnp.int32)

    ((z3, m3, h1, a_bf, conv2, pooled2, yc_bf, zmeta, meta_full, conv_full),
     (win_all, wout_all, _, wg_all, wu_all, wd_all)) = _mixer_fwd(
        x, g1, g2, poolw_bf, pscale,
        [w_in[0].astype(BF16), w_out[0].astype(BF16), small,
         w_gate[0].T.astype(BF16), w_up[0].T.astype(BF16), w_down[0].astype(BF16)])
    wout_full = wout_all.reshape(D_MODEL, D_MODEL)
    wg_t, wu_t, wd_full = [w.reshape(D_FF, D_MODEL) for w in (wg_all, wu_all, wd_all)]
    dh1, f_bf, dd_bf, ds_bf, du_bf, gg_bf, lossp, dg3p, dg4p = _ffn_fwd_bwd(
        h1.reshape(n_rows, D_MODEL), loss_target.reshape(n_rows, D_MODEL), g3, g4, wg_t, wu_t, wd_full)
    as_shards = lambda g: g.reshape(N_CHIPS, FF_SHARD, D_MODEL)
    (dwd,), _ = _ffn_weight_grads("ffn_weight_grads_down", [gg_bf], dd_bf, [])
    dwd = as_shards(dwd)
    (dwg_t,), (dwd_recv,) = _ffn_weight_grads("ffn_weight_grads_gate", [ds_bf], f_bf, [dwd])
    dwg_t = as_shards(dwg_t)
    (dwu_t,), (dwg_recv,) = _ffn_weight_grads("ffn_weight_grads_up", [du_bf], f_bf, [dwg_t])
    dwu_t = as_shards(dwu_t)
    ((grad_x, dz_bf, dm_bf, dg1p, dg2p, dscp, dcwp, dpw, dzm), (dwu_recv,), (dwd_rbuf, dwg_rbuf)) = _mixer_bwd(
        dh1.reshape(n_seq, seq, D_MODEL), m3, z3, conv2, pooled2, x, zmeta, g1, g2, conv_full, poolw_bf, pscale,
        win_all, wout_full, [dwu_t], [_add_pairs(dwd, dwd_recv, place), _add_pairs(dwg_t, dwg_recv, place)])
    dmeta, dg1m, a_meta, dz_meta = _meta_bwd(dzm, meta_full, g1, win_all)
    mix_grads, (dwu_rbuf,), (a_red, b_red, c_red) = _mixer_weight_grads(
        a_bf, dz_bf, yc_bf, dm_bf, a_meta, dz_meta, [_add_pairs(dwu_t, dwu_recv, place)],
        [dg1p, dg1m, dg2p, dg3p, dg4p, lossp, dmeta, dscp, dcwp, dpw.reshape(SMALL_C_ROWS, POOL_GROUP)])

    mix_recvs = _exchange_halves(mix_grads)
    ffn_red, mix_rbufs = _add_chips([dwg_t, dwu_t, dwd], [dwg_recv, dwu_recv, dwd_recv],
                                    [dwg_rbuf, dwu_rbuf, dwd_rbuf], place,
                                    [_add_pairs(g, r, place) for g, r in zip(mix_grads, mix_recvs)],
                                    name="grad_add_chips_ffn")
    mix_red = [_add_chips([g], [r], [rb], place)[0][0] for g, r, rb in zip(mix_grads, mix_recvs, mix_rbufs)]
    reduced = _gather_halves(mix_red + list(ffn_red))
    g_win, g_wout, g_wg_t, g_wu_t, g_wd = [r.reshape(2 * r.shape[1], r.shape[2]) for r in reduced]

    loss = a_red[4, 0]
    g_g1, g_g2, g_g3, g_g4 = a_red[0:1], a_red[1:2], a_red[2:3], a_red[3:4]
    g_meta = lax.dynamic_slice(a_red, (8, chip * meta_cols), (N_META, meta_cols))
    g_pscale = b_red[0:1]
    g_conv = lax.dynamic_slice(b_red, (1, chip * conv_cols), (3, conv_cols))
    g_poolw = c_red

    big = [(w_in[0], g_win, m_w_in[0], v_w_in[0]), (w_out[0], g_wout, m_w_out[0], v_w_out[0]),
           (w_gate[0].T, g_wg_t, m_w_gate[0].T, v_w_gate[0].T), (w_up[0].T, g_wu_t, m_w_up[0].T, v_w_up[0].T),
           (w_down[0], g_wd, m_w_down[0], v_w_down[0])]
    big_out = [_adamw_big(w, g, m, v) for (w, g, m, v) in big]
    big_out[2] = [o.T for o in big_out[2]]
    big_out[3] = [o.T for o in big_out[3]]
    g_wg, g_wu = g_wg_t.T, g_wu_t.T
    small_groups = [
        (meta_tokens, g_meta, m_meta_tokens, v_meta_tokens),
        (g1, g_g1, m_norm_mix_pre, v_norm_mix_pre),
        (conv_w[0], g_conv, m_conv_w[0], v_conv_w[0]),
        (pool_w.reshape(SMALL_C_ROWS, POOL_GROUP), g_poolw, m_pool_w.reshape(SMALL_C_ROWS, POOL_GROUP),
         v_pool_w.reshape(SMALL_C_ROWS, POOL_GROUP)),
        (pool_scale, g_pscale, m_pool_scale, v_pool_scale),
        (g2, g_g2, m_norm_mix_post, v_norm_mix_post),
        (g3, g_g3, m_norm_ffn_pre, v_norm_ffn_pre),
        (g4, g_g4, m_norm_ffn_post, v_norm_ffn_post),
    ]
    small_out = _adamw_small(small_groups)

    grads_out = [g_meta, g_g1, g_win[None], g_conv[None], g_poolw.reshape(pool_w.shape), g_pscale, g_wout[None],
                 g_g2, g_g3, g_wg[None], g_wu[None], g_wd[None], g_g4]
    s_meta, s_g1, s_conv, s_poolw, s_pscale, s_g2, s_g3, s_g4 = small_out
    b_win, b_wout, b_wg, b_wu, b_wd = big_out

    def leaf(k):
        return [s_meta[k], s_g1[k], b_win[k][None], s_conv[k][None], s_poolw[k].reshape(pool_w.shape), s_pscale[k],
                b_wout[k][None], s_g2[k], s_g3[k], b_wg[k][None], b_wu[k][None], b_wd[k][None], s_g4[k]]

    return (loss, grad_x, *grads_out, *leaf(0), *leaf(1), *leaf(2))
```

```python
import jax
import jax.numpy as jnp
from jax import lax
from jax.experimental import pallas as pl
from jax.experimental.pallas import tpu as pltpu

F32 = jnp.float32
BF16 = jnp.bfloat16
MESH = pl.DeviceIdType.MESH

D_MODEL = 1024
D_CONV = 512
D_POOL = 512
POOL_GROUP = 128
N_POOL_GROUPS = 4
D_IN_PROJ = 2048
D_FF = 2816
N_CHIPS = 4
FF_SHARD = D_FF // N_CHIPS
IN_SHARD = D_IN_PROJ // N_CHIPS
OUT_SHARD = D_MODEL // N_CHIPS
D_Z = 3 * IN_SHARD
N_META = 16
HALO = 16
RMS_EPS = 1e-6

ADAM_LR = 0.001
ADAM_B1 = 0.9
ADAM_B2 = 0.999
ADAM_EPS = 1e-08
ADAM_WD = 0.01
ADAM_STEP = 10

TM_MIX_FWD = 512
TM_MIX_BWD = 512
SUB_MIX_BWD = 512
TM_FFN = 256
TK_DW = 1024
FF_CHUNK = 1024
VMEM_LIMIT = 56 * 1024 * 1024


def _cparams(n_grid):
    return pltpu.CompilerParams(dimension_semantics=("arbitrary",) * n_grid, vmem_limit_bytes=VMEM_LIMIT)


def _dot(a, b):
    return jnp.dot(a, b, preferred_element_type=F32)


def _dot_nt(a, b):
    return lax.dot_general(a, b, (((1,), (1,)), ((), ())), preferred_element_type=F32)


def _dot_tn(a, b):
    return lax.dot_general(a, b, (((0,), (0,)), ((), ())), preferred_element_type=F32)


def _rows8(v):
    r, c = v.shape
    return v.reshape(r // 8, 8, c).sum(axis=0)


def _rstd(v):
    return lax.rsqrt(jnp.mean(v * v, axis=-1, keepdims=True) + RMS_EPS)


def _rms_bwd(dy, xhat, rstd, gain):
    dyg = dy * gain
    return rstd * (dyg - xhat * jnp.mean(dyg * xhat, axis=-1, keepdims=True))


def _sigmoid(v):
    return 1.0 / (1.0 + jnp.exp(-v))


def _gcols(g):
    return slice(g * POOL_GROUP, (g + 1) * POOL_GROUP)


def _window_sum(e, g, ahead):
    n = e.shape[0]
    w = e
    for level in range(g + 1):
        shift = 1 << level
        w = w + pltpu.roll(w, (n - shift) if ahead else shift, 0)
    return w


def _pool_fwd(pb, g, n):
    e = pb[0:HALO + n, _gcols(g)]
    return _window_sum(e, g, False)[HALO:, :] * (1.0 / (2 << g)) - e[HALO:, :]


def _pool_bwd(qb, g, r0, n):
    e = qb[r0:r0 + n + HALO, _gcols(g)]
    return _window_sum(e, g, True)[0:n, :] * (1.0 / (2 << g)) - e[0:n, :]


def _full(shape):
    nd = len(shape)
    return pl.BlockSpec(shape, lambda *_: (0,) * nd)


ANY = pl.BlockSpec(memory_space=pl.ANY)


def _mesh_pos():
    x, y, c = lax.axis_index("x"), lax.axis_index("y"), lax.axis_index("c")
    chips = [(1 - x, y), (x, 1 - y), (1 - x, 1 - y)]
    return x, y, c, chips


def _half(ref, h):
    hr = ref.shape[0] // 2
    return ref.at[pl.ds(h * hr, hr), :]


class _AllGather:
    PER_ARRAY = 9

    def __init__(self, ins, outs, send_sems, recv_sems):
        self.ins, self.outs, self.send_sems, self.recv_sems = ins, outs, send_sems, recv_sems
        self.n = len(ins)

    @classmethod
    def scratch(cls, n):
        return [pltpu.SemaphoreType.DMA((cls.PER_ARRAY * n,)), pltpu.SemaphoreType.DMA((cls.PER_ARRAY * n,))]

    @staticmethod
    def out_shape(shards):
        return [jax.ShapeDtypeStruct((N_CHIPS,) + s.shape, s.dtype) for s in shards]

    def _copy(self, a, k, src, dst, to):
        i = self.PER_ARRAY * a + k
        return pltpu.make_async_remote_copy(src_ref=src, dst_ref=dst, send_sem=self.send_sems.at[i],
                                            recv_sem=self.recv_sems.at[i], device_id=to, device_id_type=MESH)

    def _piece(self, a, chip, piece, h=None):
        h = lax.axis_index("c") if h is None else h
        rows = self.ins[a].shape[0] // 4
        return self.outs[a].at[chip].at[pl.ds((2 * h + piece) * rows, rows), :]

    def _own(self, a, k):
        x, y, c, chips = _mesh_pos()
        piece = (1, 0, 0, 1)[k]
        rows = self.ins[a].shape[0] // 4
        src = self.ins[a].at[pl.ds((2 * c + piece) * rows, rows), :]
        return self._copy(a, k, src, self._piece(a, 2 * x + y, piece), (*chips[k // 2], c))

    def _relay(self, a, k):
        x, y, c, chips = _mesh_pos()
        source, to, piece = (chips[1], chips[0], 0) if k == 4 else (chips[0], chips[1], 1)
        rows = self._piece(a, 2 * source[0] + source[1], piece)
        return self._copy(a, k, rows, rows, (*to, c))

    def _sibling(self, a, k, h):
        x, y, c, chips = _mesh_pos()
        chip = chips[k - 6]
        slot = _half(self.outs[a].at[2 * chip[0] + chip[1]], h)
        return self._copy(a, k, slot, slot, (x, y, 1 - c))

    def start(self, arrays=None):
        for a in (range(self.n) if arrays is None else arrays):
            for k in range(4):
                self._own(a, k).start()

    def relay(self, a):
        self._own(a, 2).wait_recv()
        self._relay(a, 4).start()
        self._own(a, 0).wait_recv()
        self._relay(a, 5).start()

    def forward(self, a):
        c = lax.axis_index("c")
        self._own(a, 1).wait_recv()
        self._sibling(a, 6, c).start()
        self._own(a, 3).wait_recv()
        self._sibling(a, 7, c).start()
        self._relay(a, 4).wait_recv()
        self._relay(a, 5).wait_recv()
        self._sibling(a, 8, c).start()

    def finish(self, arrays=None):
        c = lax.axis_index("c")
        arrays = range(self.n) if arrays is None else arrays
        for a in arrays:
            for k in range(6, 9):
                self._sibling(a, k, 1 - c).wait_recv()
        for a in arrays:
            for k in range(4):
                self._own(a, k).wait_send()
            for k in range(4, 6):
                self._relay(a, k).wait_send()
            for k in range(6, 9):
                self._sibling(a, k, c).wait_send()


def _fill_own_slot(gathered, shards):
    chip = 2 * lax.axis_index("x") + lax.axis_index("y")
    return [lax.dynamic_update_slice(o, s[None], (chip, 0, 0)) for o, s in zip(gathered, shards)]


class _ExchangeHalves:
    def __init__(self, ins, recvs, send_sems, recv_sems):
        self.ins, self.recvs, self.send_sems, self.recv_sems = ins, recvs, send_sems, recv_sems

    @staticmethod
    def scratch(n):
        return [pltpu.SemaphoreType.DMA((n,)), pltpu.SemaphoreType.DMA((n,))]

    @staticmethod
    def out_shape(grads):
        return [jax.ShapeDtypeStruct((g.shape[0], g.shape[1] // 2, g.shape[2]), g.dtype) for g in grads]

    def _copies(self):
        x, y, c, _ = _mesh_pos()
        out = []
        for a, (src, dst) in enumerate(zip(self.ins, self.recvs)):
            hr = src.shape[1] // 2
            out.append(pltpu.make_async_remote_copy(
                src_ref=src.at[:, pl.ds((1 - c) * hr, hr), :], dst_ref=dst, send_sem=self.send_sems.at[a],
                recv_sem=self.recv_sems.at[a], device_id=(x, y, 1 - c), device_id_type=MESH))
        return out

    def start(self):
        for cp in self._copies():
            cp.start()

    def finish(self):
        for cp in self._copies():
            cp.wait()


def _exchange_halves(grads):
    n = len(grads)

    def body(*refs):
        ex = _ExchangeHalves(refs[:n], refs[n:2 * n], *refs[2 * n:])
        ex.start()
        ex.finish()

    return pl.pallas_call(
        body, name="grad_exchange_halves", out_shape=_ExchangeHalves.out_shape(grads),
        in_specs=[ANY] * n, out_specs=[ANY] * n, scratch_shapes=_ExchangeHalves.scratch(n),
    )(*grads)


class _ScatterToChips:
    def __init__(self, ins, rbufs, send_sems, recv_sems):
        self.ins, self.rbufs, self.send_sems, self.recv_sems = ins, rbufs, send_sems, recv_sems

    @staticmethod
    def scratch(n):
        return [pltpu.SemaphoreType.DMA((3 * n,)), pltpu.SemaphoreType.DMA((3 * n,))]

    @staticmethod
    def out_shape(sums):
        return [jax.ShapeDtypeStruct((3,) + s.shape[1:], BF16) for s in sums]

    def _copies(self):
        x, y, c, chips = _mesh_pos()
        out = []
        for a, (src, dst) in enumerate(zip(self.ins, self.rbufs)):
            for k, chip in enumerate(chips):
                out.append(pltpu.make_async_remote_copy(
                    src_ref=src.at[2 * chip[0] + chip[1]], dst_ref=dst.at[k], send_sem=self.send_sems.at[3 * a + k],
                    recv_sem=self.recv_sems.at[3 * a + k], device_id=(*chip, c), device_id_type=MESH))
        return out

    def start(self):
        for cp in self._copies():
            cp.start()

    def finish(self):
        for cp in self._copies():
            cp.wait()


def _gather_halves(halves):
    n = len(halves)

    def body(*refs):
        ins, outs = refs[:n], refs[n:2 * n]
        send_sems, recv_sems = refs[2 * n:]
        x, y, c, _ = _mesh_pos()
        sib = (x, y, 1 - c)
        remote = [pltpu.make_async_remote_copy(src_ref=ins[a].at[c], dst_ref=outs[a].at[c],
                                               send_sem=send_sems.at[a], recv_sem=recv_sems.at[a],
                                               device_id=sib, device_id_type=MESH) for a in range(n)]
        for cp in remote:
            cp.start()
        for a in range(n):
            pltpu.make_async_remote_copy(src_ref=ins[a].at[1 - c], dst_ref=outs[a].at[1 - c], send_sem=send_sems.at[a],
                                         recv_sem=recv_sems.at[a], device_id=sib, device_id_type=MESH).wait_recv()
        for cp in remote:
            cp.wait_send()

    return pl.pallas_call(
        body, name="grad_gather_halves",
        out_shape=[jax.ShapeDtypeStruct(h.shape, F32) for h in halves],
        in_specs=[ANY] * n, out_specs=[ANY] * n, input_output_aliases={a: a for a in range(n)},
        scratch_shapes=[pltpu.SemaphoreType.DMA((n,)), pltpu.SemaphoreType.DMA((n,))],
    )(*halves)


SMALL_A_ROWS = 24
SMALL_B_ROWS = 8
SMALL_C_ROWS = N_POOL_GROUPS * POOL_GROUP


class _AllReduceSmall:
    N_IN = 10
    SHAPES = [(SMALL_A_ROWS, D_MODEL), (SMALL_B_ROWS, D_CONV), (SMALL_C_ROWS, POOL_GROUP)]

    def __init__(self, ins, outs, scratch):
        self.ins, self.outs = ins, outs
        self.bufs, self.rcvs, self.send_sems, self.recv_sems = scratch[:3], scratch[3:6], scratch[6], scratch[7]

    @classmethod
    def scratch(cls):
        return ([pltpu.VMEM((3,) + s, F32) for s in cls.SHAPES] + [pltpu.VMEM((3,) + s, F32) for s in cls.SHAPES]
                + [pltpu.SemaphoreType.DMA((9,)), pltpu.SemaphoreType.DMA((9,))])

    @classmethod
    def out_shape(cls):
        return [jax.ShapeDtypeStruct(s, F32) for s in cls.SHAPES]

    def _copies(self, st):
        x, y, c, _ = _mesh_pos()
        peer = [(x, y, 1 - c), (1 - x, y, c), (x, 1 - y, c)][st]
        return [pltpu.make_async_remote_copy(
            src_ref=buf.at[st], dst_ref=rcv.at[st], send_sem=self.send_sems.at[3 * st + i],
            recv_sem=self.recv_sems.at[3 * st + i], device_id=peer, device_id_type=MESH)
            for i, (buf, rcv) in enumerate(zip(self.bufs, self.rcvs))]

    def pack_and_send(self):
        dg1_ref, dg1m_ref, dg2_ref, dg3_ref, dg4_ref, loss_ref, dmeta_ref, dsc_ref, dcw_ref, dpw_ref = self.ins
        a_buf, b_buf, c_buf = self.bufs

        def rowsum(v):
            return jnp.sum(v, axis=0, keepdims=True)

        a_buf[0, 0:1, :] = rowsum(dg1_ref[...] + dg1m_ref[...])
        a_buf[0, 1:2, :] = rowsum(dg2_ref[...])
        a_buf[0, 2:3, :] = rowsum(dg3_ref[...])
        a_buf[0, 3:4, :] = rowsum(dg4_ref[...])
        loss = jnp.sum(rowsum(loss_ref[...]), axis=1, keepdims=True) * (0.5 / D_MODEL)
        a_buf[0, 4:5, :] = jnp.broadcast_to(loss, (1, D_MODEL))
        a_buf[0, 5:8, :] = jnp.zeros((3, D_MODEL), F32)
        a_buf[0, 8:24, :] = dmeta_ref[...]
        b_buf[0, 0:1, :] = rowsum(dsc_ref[...])
        for k in range(3):
            b_buf[0, 1 + k:2 + k, :] = rowsum(dcw_ref[8 * k:8 * k + 8, :])
        b_buf[0, 4:8, :] = jnp.zeros((4, D_CONV), F32)
        c_buf[0] = dpw_ref[...]
        for cp in self._copies(0):
            cp.start()

    def combine(self, st):
        for cp in self._copies(st):
            cp.wait()
        if st < 2:
            for buf, rcv in zip(self.bufs, self.rcvs):
                buf[st + 1] = buf[st] + rcv[st]
            for cp in self._copies(st + 1):
                cp.start()
        else:
            for out, buf, rcv in zip(self.outs, self.bufs, self.rcvs):
                out[...] = buf[st] + rcv[st]


def _row_block(rows):
    for cand in (512, 448, 384, 352, 320, 256, 128, 64, 32, 16):
        if rows % cand == 0:
            return cand
    return rows


def _add_pairs(grad, recv, place):
    n_sh, rows2, cols = grad.shape
    hr = rows2 // 2
    br = _row_block(hr)

    def body(place_ref, a_ref, b_ref, o_ref):
        o_ref[...] = (a_ref[0] + b_ref[...]).astype(BF16)

    return pl.pallas_call(
        body, name="grad_add_pairs",
        grid_spec=pltpu.PrefetchScalarGridSpec(
            num_scalar_prefetch=1, grid=(n_sh, hr // br),
            in_specs=[pl.BlockSpec((1, 1, br, cols), lambda j, i, p: (j, p[1], i, 0)),
                      pl.BlockSpec((1, br, cols), lambda j, i, p: (j, i, 0))],
            out_specs=pl.BlockSpec((1, br, cols), lambda j, i, p: (j, i, 0))),
        out_shape=jax.ShapeDtypeStruct((n_sh, hr, cols), BF16), compiler_params=_cparams(2),
    )(place, grad.reshape(n_sh, 2, hr, cols), recv)


def _add_chips(grads, recvs, rbufs, place, scattered=(), name="grad_add_chips"):
    n, n_sc = len(grads), len(scattered)
    n_sh, rows2, cols = grads[0].shape
    hr = rows2 // 2
    br = _row_block(hr)
    n_steps = hr // br

    def body(place_ref, *refs):
        a_refs, b_refs, r_refs = refs[:n], refs[n:2 * n], refs[2 * n:3 * n]
        o_refs = refs[3 * n + n_sc:4 * n + n_sc]
        if n_sc:
            scatter = _ScatterToChips(refs[3 * n:3 * n + n_sc], refs[4 * n + n_sc:4 * n + 2 * n_sc], *refs[-2:])

            @pl.when(pl.program_id(0) == 0)
            def _():
                scatter.start()

        for a_ref, b_ref, r_ref, o_ref in zip(a_refs, b_refs, r_refs, o_refs):
            own = a_ref[0, 0] + b_ref[0]
            o_ref[0] = ((own + r_ref[0].astype(F32)) + r_ref[1].astype(F32)) + r_ref[2].astype(F32)

        if n_sc:
            @pl.when(pl.program_id(0) == n_steps - 1)
            def _():
                scatter.finish()

    outs = pl.pallas_call(
        body, name=name,
        grid_spec=pltpu.PrefetchScalarGridSpec(
            num_scalar_prefetch=1, grid=(n_steps,),
            in_specs=[pl.BlockSpec((1, 1, br, cols), lambda i, p: (p[0], p[1], i, 0))] * n
            + [pl.BlockSpec((1, br, cols), lambda i, p: (p[0], i, 0))] * n
            + [pl.BlockSpec((3, br, cols), lambda i, p: (0, i, 0))] * n + [ANY] * n_sc,
            out_specs=[pl.BlockSpec((1, br, cols), lambda i, p: (p[1], i, 0))] * n + [ANY] * n_sc,
            scratch_shapes=_ScatterToChips.scratch(n_sc) if n_sc else []),
        out_shape=[jax.ShapeDtypeStruct((2, hr, cols), F32)] * n + _ScatterToChips.out_shape(list(scattered)),
        compiler_params=_cparams(1),
    )(place, *[g.reshape(n_sh, 2, hr, cols) for g in grads], *recvs, *rbufs, *scattered)
    return outs[:n], outs[n:]


def _adamw_math(w, g, m, v):
    m2 = ADAM_B1 * m + (1.0 - ADAM_B1) * g
    v2 = ADAM_B2 * v + (1.0 - ADAM_B2) * (g * g)
    m_hat = m2 / (1.0 - ADAM_B1 ** ADAM_STEP)
    v_hat = v2 / (1.0 - ADAM_B2 ** ADAM_STEP)
    delta = -ADAM_LR * (m_hat / (jnp.sqrt(v_hat) + ADAM_EPS) + ADAM_WD * w)
    return delta, m2, v2


def _adamw_big(w, g, m, v):
    rows, cols = w.shape
    br = _row_block(rows)

    def body(w_ref, g_ref, m_ref, v_ref, d_ref, m2_ref, v2_ref):
        d, m2, v2 = _adamw_math(w_ref[...], g_ref[...], m_ref[...], v_ref[...])
        d_ref[...] = d
        m2_ref[...] = m2
        v2_ref[...] = v2

    spec = pl.BlockSpec((br, cols), lambda i: (i, 0))
    return pl.pallas_call(
        body, name="adamw_big", grid=(rows // br,),
        out_shape=[jax.ShapeDtypeStruct((rows, cols), F32)] * 3,
        in_specs=[spec] * 4, out_specs=[spec] * 3, compiler_params=_cparams(1),
    )(w, g, m, v)


def _adamw_small(groups):
    n = len(groups)

    def body(*refs):
        ins, outs = refs[:4 * n], refs[4 * n:]
        for i in range(n):
            w, g, m, v = (r[...] for r in ins[4 * i:4 * i + 4])
            d, m2, v2 = _adamw_math(w, g, m, v)
            outs[3 * i][...] = d
            outs[3 * i + 1][...] = m2
            outs[3 * i + 2][...] = v2

    vm = pl.BlockSpec(memory_space=pltpu.VMEM)
    flat = [a for grp in groups for a in grp]
    out_shape = [jax.ShapeDtypeStruct(grp[0].shape, F32) for grp in groups for _ in range(3)]
    outs = pl.pallas_call(body, name="adamw_small", out_shape=out_shape,
                          in_specs=[vm] * (4 * n), out_specs=[vm] * (3 * n))(*flat)
    return [tuple(outs[3 * i:3 * i + 3]) for i in range(n)]


def _load_weights(pairs, sem):
    for src, dst in pairs:
        cp = pltpu.make_async_copy(src, dst, sem)
        cp.start()
        cp.wait()


N_MIX_SHARDS = 3


def _mixer_fwd(x3, g1, g2, poolw, pscale, shards):
    n_seq, seq, _ = x3.shape
    tm = min(TM_MIX_FWD, seq)
    n_t = seq // tm
    n_steps = n_seq * n_t
    n_ag = len(shards)
    n_ffn = n_ag - N_MIX_SHARDS
    small_rows = shards[2].shape[0]
    conv_cols = D_CONV // N_CHIPS

    def body(x_ref, g1_ref, g2_ref, pw_ref, ps_ref, *rest):
        ag = _AllGather(rest[:n_ag], rest[n_ag + 10:2 * n_ag + 10], *rest[-2:])
        (z_ref, m_ref, h1_ref, a_ref, conv_ref, pooled_ref, yc_ref, zm_ref, meta_ref,
         cw_ref) = rest[n_ag:n_ag + 10]
        win_v, wout_v, small_v, cvb, pb, load_sems = rest[2 * n_ag + 10:-2]
        s, t = pl.program_id(0), pl.program_id(1)
        step = s * n_t + t

        @pl.when(step == 0)
        def _():
            ag.start(range(N_MIX_SHARDS))
            for a in range(N_MIX_SHARDS):
                ag.relay(a)
            for a in range(N_MIX_SHARDS):
                ag.forward(a)
            ag.finish(range(N_MIX_SHARDS))
            ag.start(range(N_MIX_SHARDS, n_ag))
            me = 2 * lax.axis_index("x") + lax.axis_index("y")
            dsts = [lambda j: win_v.at[j], lambda j: wout_v.at[pl.ds(j * OUT_SHARD, OUT_SHARD), :],
                    lambda j: small_v.at[j]]

            def loads(j, own):
                return [pltpu.make_async_copy(ag.ins[a] if own else ag.outs[a].at[j], dsts[a](j),
                                              load_sems.at[N_MIX_SHARDS * j + a]) for a in range(N_MIX_SHARDS)]

            for wait in (False, True):
                for j in range(N_CHIPS):
                    for own in (False, True):
                        @pl.when((me == j) == own)
                        def _():
                            for cp in loads(j, own):
                                cp.wait() if wait else cp.start()

            meta = jnp.concatenate([small_v[j, 0:N_META, :] for j in range(N_CHIPS)], axis=1)
            meta_ref[...] = meta
            cw_ref[...] = jnp.concatenate([small_v[j, N_META:N_META + 3, 0:conv_cols] for j in range(N_CHIPS)], axis=1)
            a_meta = (meta * _rstd(meta) * g1_ref[...]).astype(BF16)
            for j in range(N_CHIPS):
                zm_ref[:, j * IN_SHARD:(j + 1) * IN_SHARD] = _dot(a_meta, win_v[j])

        for i in range(n_ffn):
            @pl.when(step == ((i + 1) * n_steps) // (2 * n_ffn + 2))
            def _():
                ag.relay(N_MIX_SHARDS + i)

        for i in range(n_ffn):
            @pl.when(step == min(n_steps // 2 + ((i + 1) * n_steps) // (2 * n_ffn + 2), n_steps - 1))
            def _():
                ag.forward(N_MIX_SHARDS + i)

        @pl.when(t == 0)
        def _():
            cvb[0:HALO, :] = zm_ref[:, IN_SHARD:2 * IN_SHARD] * zm_ref[:, 2 * IN_SHARD:3 * IN_SHARD]
            pb[0:HALO, :] = zm_ref[:, 3 * IN_SHARD:4 * IN_SHARD]

        @pl.when(t > 0)
        def _():
            cvb[0:HALO, :] = cvb[tm:tm + HALO, :]
            pb[0:HALO, :] = pb[tm:tm + HALO, :]

        xt = x_ref[0]
        a = (xt * _rstd(xt) * g1_ref[...]).astype(BF16)
        a_ref[...] = a
        zb = _dot(a, win_v[0])
        zc = _dot(a, win_v[1])
        zv = _dot(a, win_v[2])
        zp = _dot(a, win_v[3])
        z_ref[0, :, 0:IN_SHARD] = zb
        z_ref[0, :, IN_SHARD:2 * IN_SHARD] = zc
        z_ref[0, :, 2 * IN_SHARD:3 * IN_SHARD] = zv
        cv = zc * zv
        cvb[HALO:HALO + tm, :] = cv
        pb[HALO:HALO + tm, :] = zp
        cw = cw_ref[...]
        conv = cw[0:1] * cvb[HALO - 2:HALO - 2 + tm, :] + cw[1:2] * cvb[HALO - 1:HALO - 1 + tm, :] + cw[2:3] * cv
        conv_ref[...] = conv
        parts = [(zb * conv).astype(BF16)]
        for g in range(N_POOL_GROUPS):
            pooled = _pool_fwd(pb, g, tm).astype(BF16)
            pooled_ref[:, _gcols(g)] = pooled
            parts.append((_dot(pooled, pw_ref[g]) * ps_ref[:, _gcols(g)]).astype(BF16))
        ycat = jnp.concatenate(parts, axis=1)
        yc_ref[...] = ycat
        m = _dot(ycat, wout_v[...])
        m_ref[0] = m
        h1_ref[0] = xt + m * _rstd(m) * g2_ref[...]

        @pl.when(step == n_steps - 1)
        def _():
            ag.finish(range(N_MIX_SHARDS, n_ag))

    n_rows = n_seq * seq
    row = lambda c: pl.BlockSpec((1, tm, c), lambda s, t: (s, t, 0))
    row2 = lambda c: pl.BlockSpec((tm, c), lambda s, t: (s * n_t + t, 0))
    outs = pl.pallas_call(
        body, name="mixer_fwd", grid=(n_seq, n_t),
        out_shape=[jax.ShapeDtypeStruct((n_seq, seq, D_Z), F32), jax.ShapeDtypeStruct((n_seq, seq, D_MODEL), F32),
                   jax.ShapeDtypeStruct((n_seq, seq, D_MODEL), F32), jax.ShapeDtypeStruct((n_rows, D_MODEL), BF16),
                   jax.ShapeDtypeStruct((n_rows, D_CONV), F32), jax.ShapeDtypeStruct((n_rows, D_POOL), BF16),
                   jax.ShapeDtypeStruct((n_rows, D_MODEL), BF16), jax.ShapeDtypeStruct((N_META, D_IN_PROJ), F32),
                   jax.ShapeDtypeStruct((N_META, D_MODEL), F32), jax.ShapeDtypeStruct((3, D_CONV), F32)]
        + _AllGather.out_shape(shards),
        in_specs=[row(D_MODEL), _full((1, D_MODEL)), _full((1, D_MODEL)),
                  _full((N_POOL_GROUPS, POOL_GROUP, POOL_GROUP)), _full((1, D_POOL))] + [ANY] * n_ag,
        out_specs=[row(D_Z), row(D_MODEL), row(D_MODEL), row2(D_MODEL), row2(D_CONV), row2(D_POOL), row2(D_MODEL),
                   _full((N_META, D_IN_PROJ)), _full((N_META, D_MODEL)), _full((3, D_CONV))] + [ANY] * n_ag,
        scratch_shapes=[pltpu.VMEM((N_CHIPS, D_MODEL, IN_SHARD), BF16), pltpu.VMEM((D_MODEL, D_MODEL), BF16),
                        pltpu.VMEM((N_CHIPS, small_rows, D_MODEL // N_CHIPS), F32),
                        pltpu.VMEM((HALO + tm, D_CONV), F32), pltpu.VMEM((HALO + tm, D_POOL), F32),
                        pltpu.SemaphoreType.DMA((N_MIX_SHARDS * N_CHIPS,))] + _AllGather.scratch(n_ag),
        compiler_params=_cparams(2),
    )(x3, g1, g2, poolw, pscale, *shards)
    return outs[:10], _fill_own_slot(outs[10:], shards)


def _ffn_chunks():
    out, r0 = [], 0
    while r0 < D_FF:
        out.append((r0, min(FF_CHUNK, D_FF - r0)))
        r0 += FF_CHUNK
    return out


def _ffn_fwd_bwd(h1, target, g3, g4, wg_t, wu_t, wd):
    n_rows = h1.shape[0]
    tm = min(TM_FFN, n_rows)
    chunks = _ffn_chunks()

    def body(h1_ref, t_ref, g3_ref, g4_ref, wg_hbm, wu_hbm, wd_hbm,
             dh1_ref, f_ref, dd_ref, ds_ref, du_ref, gg_ref, loss_ref, dg3_ref, dg4_ref,
             wg_v, wu_v, wd_v, s_sc, u_sc, sem):
        @pl.when(pl.program_id(0) == 0)
        def _():
            _load_weights([(wg_hbm, wg_v), (wu_hbm, wu_v), (wd_hbm, wd_v)], sem)
            loss_ref[...] = jnp.zeros_like(loss_ref)
            dg3_ref[...] = jnp.zeros_like(dg3_ref)
            dg4_ref[...] = jnp.zeros_like(dg4_ref)

        h1v = h1_ref[...]
        r3 = _rstd(h1v)
        hh = h1v * r3
        g3v, g4v = g3_ref[...], g4_ref[...]
        f = (hh * g3v).astype(BF16)
        f_ref[...] = f
        d = jnp.zeros((tm, D_MODEL), F32)
        for r0, sz in chunks:
            s = _dot_nt(f, wg_v[r0:r0 + sz, :])
            u = _dot_nt(f, wu_v[r0:r0 + sz, :])
            s_sc[:, r0:r0 + sz] = s
            u_sc[:, r0:r0 + sz] = u
            gc = (s * _sigmoid(s) * u).astype(BF16)
            gg_ref[:, r0:r0 + sz] = gc
            d = d + _dot(gc, wd_v[r0:r0 + sz, :])
        r4 = _rstd(d)
        dh = d * r4
        err = (h1v + dh * g4v) - t_ref[...]
        loss_ref[...] += _rows8(err * err)
        dy = err * (1.0 / D_MODEL)
        dg4_ref[...] += _rows8(dy * dh)
        ddb = _rms_bwd(dy, dh, r4, g4v).astype(BF16)
        dd_ref[...] = ddb
        df = jnp.zeros((tm, D_MODEL), F32)
        for r0, sz in chunks:
            dgg = _dot_nt(ddb, wd_v[r0:r0 + sz, :])
            s = s_sc[:, r0:r0 + sz]
            u = u_sc[:, r0:r0 + sz]
            sig = _sigmoid(s)
            dsc = (dgg * u * (sig * (1.0 + s * (1.0 - sig)))).astype(BF16)
            duc = (dgg * (s * sig)).astype(BF16)
            ds_ref[:, r0:r0 + sz] = dsc
            du_ref[:, r0:r0 + sz] = duc
            df = df + _dot(dsc, wg_v[r0:r0 + sz, :]) + _dot(duc, wu_v[r0:r0 + sz, :])
        dg3_ref[...] += _rows8(df * hh)
        dh1_ref[...] = dy + _rms_bwd(df, hh, r3, g3v)

    row = pl.BlockSpec((tm, D_MODEL), lambda i: (i, 0))
    ffrow = pl.BlockSpec((tm, D_FF), lambda i: (i, 0))
    acc = _full((8, D_MODEL))
    act_bf = jax.ShapeDtypeStruct((n_rows, D_MODEL), BF16)
    ff_bf = jax.ShapeDtypeStruct((n_rows, D_FF), BF16)
    acc_shape = jax.ShapeDtypeStruct((8, D_MODEL), F32)
    w_vmem = pltpu.VMEM((D_FF, D_MODEL), BF16)
    return pl.pallas_call(
        body, name="ffn_fwd_bwd", grid=(n_rows // tm,),
        out_shape=[jax.ShapeDtypeStruct((n_rows, D_MODEL), F32), act_bf, act_bf, ff_bf, ff_bf, ff_bf,
                   acc_shape, acc_shape, acc_shape],
        in_specs=[row, row, _full((1, D_MODEL)), _full((1, D_MODEL)), ANY, ANY, ANY],
        out_specs=[row, row, row, ffrow, ffrow, ffrow, acc, acc, acc],
        scratch_shapes=[w_vmem, w_vmem, w_vmem, pltpu.VMEM((tm, D_FF), F32), pltpu.VMEM((tm, D_FF), F32),
                        pltpu.SemaphoreType.DMA],
        compiler_params=_cparams(1),
    )(h1, target, g3, g4, wg_t, wu_t, wd)


def _ffn_weight_grads(name, acts, other, exchanged):
    n_rows = other.shape[0]
    n_a, n_ex = len(acts), len(exchanged)
    n_c = n_a
    tk = min(TK_DW, n_rows)
    n_k = n_rows // tk
    half = D_FF // n_c

    def body(other_ref, *rest):
        act_refs = rest[:n_a]
        out_refs = rest[n_a + n_ex:2 * n_a + n_ex]
        c, k = pl.program_id(0), pl.program_id(1)
        if n_ex:
            ex = _ExchangeHalves(rest[n_a:n_a + n_ex], rest[2 * n_a + n_ex:2 * n_a + 2 * n_ex], *rest[-2:])

            @pl.when((c == 0) & (k == 0))
            def _():
                ex.start()

        @pl.when(k == 0)
        def _():
            for o in out_refs:
                o[...] = jnp.zeros_like(o)

        ov = other_ref[...]
        for a, o in zip(act_refs, out_refs):
            o[...] += _dot_tn(a[...], ov)

        if n_ex:
            @pl.when((c == n_c - 1) & (k == n_k - 1))
            def _():
                ex.finish()

    row = pl.BlockSpec((tk, D_MODEL), lambda c, k: (k, 0))
    ffrow = pl.BlockSpec((tk, half), lambda c, k: (k, c))
    out = pl.BlockSpec((half, D_MODEL), lambda c, k: (c, 0))
    outs = pl.pallas_call(
        body, name=name, grid=(n_c, n_k),
        out_shape=[jax.ShapeDtypeStruct((D_FF, D_MODEL), F32)] * n_a + _ExchangeHalves.out_shape(exchanged),
        in_specs=[row] + [ffrow] * n_a + [ANY] * n_ex, out_specs=[out] * n_a + [ANY] * n_ex,
        scratch_shapes=_ExchangeHalves.scratch(n_ex) if n_ex else [],
        compiler_params=_cparams(2),
    )(other, *acts, *exchanged)
    return outs[:n_a], outs[n_a:]


def _mixer_bwd(dh1, m3, z3, conv2, pooled2, x3, zmeta, g1, g2, convw, poolw, pscale, win_all, wout, exchanged,
               scattered):
    n_seq, seq, _ = x3.shape
    tm = min(TM_MIX_BWD, seq)
    sub = min(SUB_MIX_BWD, tm)
    n_t = seq // tm
    n_ex, n_sc = len(exchanged), len(scattered)
    n_cm = n_ex + n_sc

    def body(dh1_ref, m_ref, z_ref, conv_ref, pooled_ref, x_ref, zm_ref, g1_ref, g2_ref, cw_ref, pw_ref, ps_ref,
             win_hbm, wout_hbm, *rest):
        outs0 = n_cm + 9
        ex = _ExchangeHalves(rest[:n_ex], rest[outs0:outs0 + n_ex], *rest[-4:-2])
        sc = _ScatterToChips(rest[n_ex:n_cm], rest[outs0 + n_ex:outs0 + n_cm], *rest[-2:])
        dx_ref, dz_ref, dm_ref, dg1_ref, dg2_ref, dsc_ref, dcw_ref, dpw_ref, dzm_ref = rest[n_cm:outs0]
        win_v, wout_v, dcb, dqb, mcb, mqb, sem = rest[outs0 + n_cm:-4]
        s, i = pl.program_id(0), pl.program_id(1)
        tr = n_t - 1 - i

        @pl.when((s == 0) & (i == 0))
        def _():
            sc.start()
            ex.start()
            _load_weights([(win_hbm, win_v), (wout_hbm, wout_v)], sem)
            for ref in (dg1_ref, dg2_ref, dsc_ref, dcw_ref, dpw_ref, dzm_ref):
                ref[...] = jnp.zeros_like(ref)

        @pl.when(i == 0)
        def _():
            dcb[tm:tm + HALO, :] = jnp.zeros((HALO, D_CONV), F32)
            dqb[tm:tm + HALO, :] = jnp.zeros((HALO, D_POOL), F32)

        @pl.when(i > 0)
        def _():
            dcb[tm:tm + HALO, :] = dcb[0:HALO, :]
            dqb[tm:tm + HALO, :] = dqb[0:HALO, :]

        g1v, g2v = g1_ref[...], g2_ref[...]
        cw = cw_ref[...]

        for r0 in range(tm - sub, -1, -sub):
            rows = slice(r0, r0 + sub)
            dh1v = dh1_ref[0, rows, :]
            mv = m_ref[0, rows, :]
            r2 = _rstd(mv)
            mh = mv * r2
            dg2_ref[...] += _rows8(dh1v * mh)
            dmb = _rms_bwd(dh1v, mh, r2, g2v).astype(BF16)
            dm_ref[rows, :] = dmb
            dyc = _dot_nt(dmb, wout_v[...])
            dyconv = dyc[:, 0:D_CONV]

            for g in range(N_POOL_GROUPS):
                pooled = pooled_ref[rows, _gcols(g)]
                mixed = _dot(pooled, pw_ref[g])
                scale = ps_ref[:, _gcols(g)]
                dyp = dyc[:, D_CONV + g * POOL_GROUP:D_CONV + (g + 1) * POOL_GROUP]
                dsc_ref[:, _gcols(g)] += _rows8(dyp * mixed)
                dmix = (dyp * scale).astype(BF16)
                dpw_ref[g] += _dot_tn(pooled, dmix)
                dqb[rows, _gcols(g)] = _dot_nt(dmix, pw_ref[g])

            zb = z_ref[0, rows, 0:IN_SHARD]
            zc = z_ref[0, rows, IN_SHARD:2 * IN_SHARD]
            zv = z_ref[0, rows, 2 * IN_SHARD:3 * IN_SHARD]
            dconv = dyconv * zb
            dcb[rows, :] = dconv
            d1 = dcb[r0 + 1:r0 + 1 + sub, :]
            d2 = dcb[r0 + 2:r0 + 2 + sub, :]
            dcv = cw[2:3] * dconv + cw[1:2] * d1 + cw[0:1] * d2
            cv = zc * zv
            dcw_ref[0:8, :] += _rows8(cv * d2)
            dcw_ref[8:16, :] += _rows8(cv * d1)
            dcw_ref[16:24, :] += _rows8(cv * dconv)
            dzs = [(dyconv * conv_ref[rows, :]).astype(BF16), (dcv * zv).astype(BF16), (dcv * zc).astype(BF16),
                   jnp.concatenate([_pool_bwd(dqb, g, r0, sub) for g in range(N_POOL_GROUPS)], axis=1).astype(BF16)]
            da = jnp.zeros((sub, D_MODEL), F32)
            for j in range(N_CHIPS):
                dz_ref[j, rows, :] = dzs[j]
                da = da + _dot_nt(dzs[j], win_v[j])
            xt = x_ref[0, rows, :]
            r1 = _rstd(xt)
            xh = xt * r1
            dg1_ref[...] += _rows8(da * xh)
            dx_ref[0, rows, :] = dh1v + _rms_bwd(da, xh, r1, g1v)

        @pl.when(tr == 0)
        def _():
            mcb[0:HALO, :] = jnp.zeros((HALO, D_CONV), F32)
            mqb[0:HALO, :] = jnp.zeros((HALO, D_POOL), F32)
            mcb[HALO:2 * HALO, :] = dcb[0:HALO, :]
            mqb[HALO:2 * HALO, :] = dqb[0:HALO, :]
            m1 = mcb[1:1 + HALO, :]
            m2 = mcb[2:2 + HALO, :]
            zc_m = zm_ref[:, IN_SHARD:2 * IN_SHARD]
            zv_m = zm_ref[:, 2 * IN_SHARD:3 * IN_SHARD]
            cv_m = zc_m * zv_m
            dcw_ref[0:8, :] += _rows8(cv_m * m2)
            dcw_ref[8:16, :] += _rows8(cv_m * m1)
            dcv_m = cw[1:2] * m1 + cw[0:1] * m2
            dzm_ref[:, IN_SHARD:2 * IN_SHARD] += dcv_m * zv_m
            dzm_ref[:, 2 * IN_SHARD:3 * IN_SHARD] += dcv_m * zc_m
            dzm_ref[:, 3 * IN_SHARD:4 * IN_SHARD] += jnp.concatenate(
                [_pool_bwd(mqb, g, 0, HALO) for g in range(N_POOL_GROUPS)], axis=1)

        @pl.when((s == n_seq - 1) & (i == n_t - 1))
        def _():
            ex.finish()
            sc.finish()

    row3 = lambda c: pl.BlockSpec((1, tm, c), lambda s, i: (s, n_t - 1 - i, 0))
    row2 = lambda c: pl.BlockSpec((tm, c), lambda s, i: (s * n_t + n_t - 1 - i, 0))
    n_rows = n_seq * seq
    outs = pl.pallas_call(
        body, name="mixer_bwd", grid=(n_seq, n_t),
        out_shape=[jax.ShapeDtypeStruct((n_seq, seq, D_MODEL), F32),
                   jax.ShapeDtypeStruct((N_CHIPS, n_rows, IN_SHARD), BF16), jax.ShapeDtypeStruct((n_rows, D_MODEL), BF16),
                   jax.ShapeDtypeStruct((8, D_MODEL), F32), jax.ShapeDtypeStruct((8, D_MODEL), F32),
                   jax.ShapeDtypeStruct((8, D_POOL), F32), jax.ShapeDtypeStruct((24, D_CONV), F32),
                   jax.ShapeDtypeStruct((N_POOL_GROUPS, POOL_GROUP, POOL_GROUP), F32),
                   jax.ShapeDtypeStruct((N_META, D_IN_PROJ), F32)]
        + _ExchangeHalves.out_shape(exchanged) + _ScatterToChips.out_shape(scattered),
        in_specs=[row3(D_MODEL), row3(D_MODEL), row3(D_Z), row2(D_CONV), row2(D_POOL), row3(D_MODEL),
                  _full((N_META, D_IN_PROJ)), _full((1, D_MODEL)), _full((1, D_MODEL)), _full((3, D_CONV)),
                  _full((N_POOL_GROUPS, POOL_GROUP, POOL_GROUP)), _full((1, D_POOL)), ANY, ANY] + [ANY] * n_cm,
        out_specs=[row3(D_MODEL), pl.BlockSpec((N_CHIPS, tm, IN_SHARD), lambda s, i: (0, s * n_t + n_t - 1 - i, 0)),
                   row2(D_MODEL),
                   _full((8, D_MODEL)), _full((8, D_MODEL)), _full((8, D_POOL)), _full((24, D_CONV)),
                   _full((N_POOL_GROUPS, POOL_GROUP, POOL_GROUP)), _full((N_META, D_IN_PROJ))] + [ANY] * n_cm,
        scratch_shapes=[pltpu.VMEM((N_CHIPS, D_MODEL, IN_SHARD), BF16), pltpu.VMEM((D_MODEL, D_MODEL), BF16),
                        pltpu.VMEM((tm + HALO, D_CONV), F32), pltpu.VMEM((tm + HALO, D_POOL), F32),
                        pltpu.VMEM((2 * HALO, D_CONV), F32), pltpu.VMEM((2 * HALO, D_POOL), F32),
                        pltpu.SemaphoreType.DMA] + _ExchangeHalves.scratch(n_ex) + _ScatterToChips.scratch(n_sc),
        compiler_params=_cparams(2),
    )(dh1, m3, z3, conv2, pooled2, x3, zmeta, g1, g2, convw, poolw, pscale, win_all, wout, *exchanged, *scattered)
    return outs[:9], outs[9:9 + n_ex], outs[9 + n_ex:]


def _meta_bwd(dzm, meta_full, g1, win_all):
    def body(dzm_ref, meta_ref, g1_ref, win_ref, dmeta_ref, dg1_ref, a_ref, dzb_ref):
        xm = meta_ref[...]
        r = _rstd(xm)
        xh = xm * r
        g1v = g1_ref[...]
        a_ref[...] = (xh * g1v).astype(BF16)
        da = jnp.zeros((N_META, D_MODEL), F32)
        for j in range(N_CHIPS):
            dzj = dzm_ref[:, j * IN_SHARD:(j + 1) * IN_SHARD].astype(BF16)
            dzb_ref[j] = dzj
            da = da + _dot_nt(dzj, win_ref[j])
        dg1_ref[...] = _rows8(da * xh)
        dmeta_ref[...] = _rms_bwd(da, xh, r, g1v)

    vm = pl.BlockSpec(memory_space=pltpu.VMEM)
    return pl.pallas_call(
        body, name="meta_bwd",
        out_shape=[jax.ShapeDtypeStruct((N_META, D_MODEL), F32), jax.ShapeDtypeStruct((8, D_MODEL), F32),
                   jax.ShapeDtypeStruct((N_META, D_MODEL), BF16), jax.ShapeDtypeStruct((N_CHIPS, N_META, IN_SHARD), BF16)],
        in_specs=[vm] * 4, out_specs=[vm] * 4,
    )(dzm, meta_full, g1, win_all)


def _mixer_weight_grads(a, dz, ycat, dm, a_meta, dz_meta, ffn_sums, small):
    n_rows = a.shape[0]
    tk = min(TK_DW, n_rows)
    n_k = n_rows // tk
    n_sc, n_sm = len(ffn_sums), _AllReduceSmall.N_IN

    def body(a_ref, dz_ref, yc_ref, dm_ref, am_ref, dzm_ref, *rest):
        ins, outs, scratch = rest[:n_sc + n_sm], rest[n_sc + n_sm:2 * n_sc + n_sm + 5], rest[2 * n_sc + n_sm + 5:]
        dwin_ref, dwout_ref = outs[:2]
        scatter = _ScatterToChips(ins[:n_sc], outs[2:2 + n_sc], *scratch[:2])
        reduce_small = _AllReduceSmall(ins[n_sc:], outs[2 + n_sc:], scratch[2:])
        k = pl.program_id(0)

        @pl.when(k == 0)
        def _():
            scatter.start()
            reduce_small.pack_and_send()
            am_t = am_ref[...].T
            for j in range(N_CHIPS):
                dwin_ref[j] = _dot(am_t, dzm_ref[j])
            dwout_ref[...] = jnp.zeros_like(dwout_ref)

        for st in range(2):
            @pl.when(k == ((st + 1) * n_k) // 3)
            def _():
                reduce_small.combine(st)

        a_t = a_ref[...].T
        for j in range(N_CHIPS):
            dwin_ref[j] += _dot(a_t, dz_ref[j])
        dwout_ref[...] += _dot_tn(yc_ref[...], dm_ref[...])

        @pl.when(k == n_k - 1)
        def _():
            reduce_small.combine(2)
            scatter.finish()

    row = pl.BlockSpec((tk, D_MODEL), lambda k: (k, 0))
    outs = pl.pallas_call(
        body, name="mixer_weight_grads", grid=(n_k,),
        out_shape=[jax.ShapeDtypeStruct((N_CHIPS, D_MODEL, IN_SHARD), F32),
                   jax.ShapeDtypeStruct((D_MODEL, D_MODEL), F32)] + _ScatterToChips.out_shape(ffn_sums)
        + _AllReduceSmall.out_shape(),
        in_specs=[row, pl.BlockSpec((N_CHIPS, tk, IN_SHARD), lambda k: (0, k, 0)), row, row,
                  _full((N_META, D_MODEL)), _full((N_CHIPS, N_META, IN_SHARD))] + [ANY] * n_sc
        + [_full(s.shape) for s in small],
        out_specs=[_full((N_CHIPS, D_MODEL, IN_SHARD)), _full((D_MODEL, D_MODEL))] + [ANY] * n_sc
        + [_full(s) for s in _AllReduceSmall.SHAPES],
        scratch_shapes=_ScatterToChips.scratch(n_sc) + _AllReduceSmall.scratch(),
        compiler_params=_cparams(1),
    )(a, dz, ycat, dm, a_meta, dz_meta, *ffn_sums, *small)
    return ([outs[0], outs[1].reshape(N_CHIPS, OUT_SHARD, D_MODEL)], outs[2:2 + n_sc], outs[2 + n_sc:])


def kernel(x, meta_tokens, norm_mix_pre, w_in, conv_w, pool_w, pool_scale, w_out, norm_mix_post, norm_ffn_pre, w_gate, w_up, w_down, norm_ffn_post, loss_target, m_meta_tokens, m_norm_mix_pre, m_w_in, m_conv_w, m_pool_w, m_pool_scale, m_w_out, m_norm_mix_post, m_norm_ffn_pre, m_w_gate, m_w_up, m_w_down, m_norm_ffn_post, v_meta_tokens, v_norm_mix_pre, v_w_in, v_conv_w, v_pool_w, v_pool_scale, v_w_out, v_norm_mix_post, v_norm_ffn_pre, v_w_gate, v_w_up, v_w_down, v_norm_ffn_post):
    n_seq, seq, _ = x.shape
    n_rows = n_seq * seq
    chip = 2 * lax.axis_index("x") + lax.axis_index("y")
    meta_cols = D_MODEL // N_CHIPS
    conv_cols = D_CONV // N_CHIPS

    small = jnp.zeros((2 * HALO, meta_cols), F32)
    small = small.at[0:N_META, :].set(meta_tokens).at[N_META:N_META + 3, 0:conv_cols].set(conv_w[0])
    poolw_bf = pool_w[0].astype(BF16)
    pscale = pool_scale
    g1, g2, g3, g4 = norm_mix_pre, norm_mix_post, norm_ffn_pre, norm_ffn_post
    place = jnp.stack([chip, lax.axis_index("c")]).astype(jnp.int32)

    ((z3, m3, h1, a_bf, conv2, pooled2, yc_bf, zmeta, meta_full, conv_full),
     (win_all, wout_all, _, wg_all, wu_all, wd_all)) = _mixer_fwd(
        x, g1, g2, poolw_bf, pscale,
        [w_in[0].astype(BF16), w_out[0].astype(BF16), small,
         w_gate[0].T.astype(BF16), w_up[0].T.astype(BF16), w_down[0].astype(BF16)])
    wout_full = wout_all.reshape(D_MODEL, D_MODEL)
    wg_t, wu_t, wd_full = [w.reshape(D_FF, D_MODEL) for w in (wg_all, wu_all, wd_all)]
    dh1, f_bf, dd_bf, ds_bf, du_bf, gg_bf, lossp, dg3p, dg4p = _ffn_fwd_bwd(
        h1.reshape(n_rows, D_MODEL), loss_target.reshape(n_rows, D_MODEL), g3, g4, wg_t, wu_t, wd_full)
    as_shards = lambda g: g.reshape(N_CHIPS, FF_SHARD, D_MODEL)
    (dwg_t, dwu_t), _ = _ffn_weight_grads("ffn_weight_grads_gate_up", [ds_bf, du_bf], f_bf, [])
    dwg_t, dwu_t = as_shards(dwg_t), as_shards(dwu_t)
    (dwd,), (dwg_recv, dwu_recv) = _ffn_weight_grads("ffn_weight_grads_down", [gg_bf], dd_bf, [dwg_t, dwu_t])
    dwd = as_shards(dwd)
    ((grad_x, dz_bf, dm_bf, dg1p, dg2p, dscp, dcwp, dpw, dzm), (dwd_recv,), (dwg_rbuf, dwu_rbuf)) = _mixer_bwd(
        dh1.reshape(n_seq, seq, D_MODEL), m3, z3, conv2, pooled2, x, zmeta, g1, g2, conv_full, poolw_bf, pscale,
        win_all, wout_full, [dwd], [_add_pairs(dwg_t, dwg_recv, place), _add_pairs(dwu_t, dwu_recv, place)])
    dmeta, dg1m, a_meta, dz_meta = _meta_bwd(dzm, meta_full, g1, win_all)
    mix_grads, (dwd_rbuf,), (a_red, b_red, c_red) = _mixer_weight_grads(
        a_bf, dz_bf, yc_bf, dm_bf, a_meta, dz_meta, [_add_pairs(dwd, dwd_recv, place)],
        [dg1p, dg1m, dg2p, dg3p, dg4p, lossp, dmeta, dscp, dcwp, dpw.reshape(SMALL_C_ROWS, POOL_GROUP)])

    mix_recvs = _exchange_halves(mix_grads)
    ffn_red, mix_rbufs = _add_chips([dwg_t, dwu_t, dwd], [dwg_recv, dwu_recv, dwd_recv],
                                    [dwg_rbuf, dwu_rbuf, dwd_rbuf], place,
                                    [_add_pairs(g, r, place) for g, r in zip(mix_grads, mix_recvs)],
                                    name="grad_add_chips_ffn")
    mix_red = [_add_chips([g], [r], [rb], place)[0][0] for g, r, rb in zip(mix_grads, mix_recvs, mix_rbufs)]
    reduced = _gather_halves(mix_red + list(ffn_red))
    g_win, g_wout, g_wg_t, g_wu_t, g_wd = [r.reshape(2 * r.shape[1], r.shape[2]) for r in reduced]

    loss = a_red[4, 0]
    g_g1, g_g2, g_g3, g_g4 = a_red[0:1], a_red[1:2], a_red[2:3], a_red[3:4]
    g_meta = lax.dynamic_slice(a_red, (8, chip * meta_cols), (N_META, meta_cols))
    g_pscale = b_red[0:1]
    g_conv = lax.dynamic_slice(b_red, (1, chip * conv_cols), (3, conv_cols))
    g_poolw = c_red

    big = [(w_in[0], g_win, m_w_in[0], v_w_in[0]), (w_out[0], g_wout, m_w_out[0], v_w_out[0]),
           (w_gate[0].T, g_wg_t, m_w_gate[0].T, v_w_gate[0].T), (w_up[0].T, g_wu_t, m_w_up[0].T, v_w_up[0].T),
           (w_down[0], g_wd, m_w_down[0], v_w_down[0])]
    big_out = [_adamw_big(w, g, m, v) for (w, g, m, v) in big]
    big_out[2] = [o.T for o in big_out[2]]
    big_out[3] = [o.T for o in big_out[3]]
    g_wg, g_wu = g_wg_t.T, g_wu_t.T
    small_groups = [
        (meta_tokens, g_meta, m_meta_tokens, v_meta_tokens),
        (g1, g_g1, m_norm_mix_pre, v_norm_mix_pre),
        (conv_w[0], g_conv, m_conv_w[0], v_conv_w[0]),
        (pool_w.reshape(SMALL_C_ROWS, POOL_GROUP), g_poolw, m_pool_w.reshape(SMALL_C_ROWS, POOL_GROUP),
         v_pool_w.reshape(SMALL_C_ROWS, POOL_GROUP)),
        (pool_scale, g_pscale, m_pool_scale, v_pool_scale),
        (g2, g_g2, m_norm_mix_post, v_norm_mix_post),
        (g3, g_g3, m_norm_ffn_pre, v_norm_ffn_pre),
        (g4, g_g4, m_norm_ffn_post, v_norm_ffn_post),
    ]
    small_out = _adamw_small(small_groups)

    grads_out = [g_meta, g_g1, g_win[None], g_conv[None], g_poolw.reshape(pool_w.shape), g_pscale, g_wout[None],
                 g_g2, g_g3, g_wg[None], g_wu[None], g_wd[None], g_g4]
    s_meta, s_g1, s_conv, s_poolw, s_pscale, s_g2, s_g3, s_g4 = small_out
    b_win, b_wout, b_wg, b_wu, b_wd = big_out

    def leaf(k):
        return [s_meta[k], s_g1[k], b_win[k][None], s_conv[k][None], s_poolw[k].reshape(pool_w.shape), s_pscale[k],
                b_wout[k][None], s_g2[k], s_g3[k], b_wg[k][None], b_wu[k][None], b_wd[k][None], s_g4[k]]

    return (loss, grad_x, *grads_out, *leaf(0), *leaf(1), *leaf(2))
```

```python
import functools

import jax
import jax.numpy as jnp
from jax import lax
from jax.experimental import pallas as pl
from jax.experimental.pallas import tpu as pltpu

F32 = jnp.float32
BF16 = jnp.bfloat16
MESH = pl.DeviceIdType.MESH

D_MODEL = 1024
D_CONV = 512
D_POOL = 512
POOL_GROUP = 128
N_POOL_GROUPS = 4
D_IN_PROJ = 2048
D_FF = 2816
N_CHIPS = 4
FF_SHARD = D_FF // N_CHIPS
IN_SHARD = D_IN_PROJ // N_CHIPS
OUT_SHARD = D_MODEL // N_CHIPS
D_Z = 3 * IN_SHARD
N_META = 16
HALO = 16
RMS_EPS = 1e-6

ADAM_LR = 0.001
ADAM_B1 = 0.9
ADAM_B2 = 0.999
ADAM_EPS = 1e-08
ADAM_WD = 0.01
ADAM_STEP = 10

TM_MIX_FWD = 512
TM_MIX_BWD = 512
SUB_MIX_BWD = 512
TM_FFN = 256
TK_DW = 1024
FF_CHUNK = 1024
VMEM_LIMIT = 56 * 1024 * 1024


def _cparams(n_grid):
    return pltpu.CompilerParams(dimension_semantics=("arbitrary",) * n_grid, vmem_limit_bytes=VMEM_LIMIT)


def _dot(a, b):
    return jnp.dot(a, b, preferred_element_type=F32)


def _dot_nt(a, b):
    return lax.dot_general(a, b, (((1,), (1,)), ((), ())), preferred_element_type=F32)


def _dot_tn(a, b):
    return lax.dot_general(a, b, (((0,), (0,)), ((), ())), preferred_element_type=F32)


def _rows8(v):
    r, c = v.shape
    return v.reshape(r // 8, 8, c).sum(axis=0)


def _rstd(v):
    return lax.rsqrt(jnp.mean(v * v, axis=-1, keepdims=True) + RMS_EPS)


def _rms_bwd(dy, xhat, rstd, gain):
    dyg = dy * gain
    return rstd * (dyg - xhat * jnp.mean(dyg * xhat, axis=-1, keepdims=True))


def _sigmoid(v):
    return 1.0 / (1.0 + jnp.exp(-v))


def _gcols(g):
    return slice(g * POOL_GROUP, (g + 1) * POOL_GROUP)


def _window_sum(e, g, ahead):
    n = e.shape[0]
    w = e
    for level in range(g + 1):
        shift = 1 << level
        w = w + pltpu.roll(w, (n - shift) if ahead else shift, 0)
    return w


def _pool_fwd(pb, g, n):
    e = pb[0:HALO + n, _gcols(g)]
    return _window_sum(e, g, False)[HALO:, :] * (1.0 / (2 << g)) - e[HALO:, :]


def _pool_bwd(qb, g, r0, n):
    e = qb[r0:r0 + n + HALO, _gcols(g)]
    return _window_sum(e, g, True)[0:n, :] * (1.0 / (2 << g)) - e[0:n, :]


def _full(shape):
    nd = len(shape)
    return pl.BlockSpec(shape, lambda *_: (0,) * nd)


ANY = pl.BlockSpec(memory_space=pl.ANY)


def _mesh_pos():
    x, y, c = lax.axis_index("x"), lax.axis_index("y"), lax.axis_index("c")
    chips = [(1 - x, y), (x, 1 - y), (1 - x, 1 - y)]
    return x, y, c, chips


def _half(ref, h):
    hr = ref.shape[0] // 2
    return ref.at[pl.ds(h * hr, hr), :]


class _AllGather:
    PER_ARRAY = 9

    def __init__(self, ins, outs, send_sems, recv_sems):
        self.ins, self.outs, self.send_sems, self.recv_sems = ins, outs, send_sems, recv_sems
        self.n = len(ins)

    @classmethod
    def scratch(cls, n):
        return [pltpu.SemaphoreType.DMA((cls.PER_ARRAY * n,)), pltpu.SemaphoreType.DMA((cls.PER_ARRAY * n,))]

    @staticmethod
    def out_shape(shards):
        return [jax.ShapeDtypeStruct((N_CHIPS,) + s.shape, s.dtype) for s in shards]

    def _copy(self, a, k, src, dst, to):
        i = self.PER_ARRAY * a + k
        return pltpu.make_async_remote_copy(src_ref=src, dst_ref=dst, send_sem=self.send_sems.at[i],
                                            recv_sem=self.recv_sems.at[i], device_id=to, device_id_type=MESH)

    def _piece(self, a, chip, piece, h=None):
        h = lax.axis_index("c") if h is None else h
        rows = self.ins[a].shape[0] // 4
        return self.outs[a].at[chip].at[pl.ds((2 * h + piece) * rows, rows), :]

    def _own(self, a, k):
        x, y, c, chips = _mesh_pos()
        piece = (1, 0, 0, 1)[k]
        rows = self.ins[a].shape[0] // 4
        src = self.ins[a].at[pl.ds((2 * c + piece) * rows, rows), :]
        return self._copy(a, k, src, self._piece(a, 2 * x + y, piece), (*chips[k // 2], c))

    def _relay(self, a, k):
        x, y, c, chips = _mesh_pos()
        source, to, piece = (chips[1], chips[0], 0) if k == 4 else (chips[0], chips[1], 1)
        rows = self._piece(a, 2 * source[0] + source[1], piece)
        return self._copy(a, k, rows, rows, (*to, c))

    def _sibling(self, a, k, h):
        x, y, c, chips = _mesh_pos()
        chip = chips[k - 6]
        slot = _half(self.outs[a].at[2 * chip[0] + chip[1]], h)
        return self._copy(a, k, slot, slot, (x, y, 1 - c))

    def start(self, arrays=None):
        for a in (range(self.n) if arrays is None else arrays):
            for k in range(4):
                self._own(a, k).start()

    def relay(self, a):
        self._own(a, 2).wait_recv()
        self._relay(a, 4).start()
        self._own(a, 0).wait_recv()
        self._relay(a, 5).start()

    def forward(self, a):
        c = lax.axis_index("c")
        self._own(a, 1).wait_recv()
        self._sibling(a, 6, c).start()
        self._own(a, 3).wait_recv()
        self._sibling(a, 7, c).start()
        self._relay(a, 4).wait_recv()
        self._relay(a, 5).wait_recv()
        self._sibling(a, 8, c).start()

    def finish(self, arrays=None):
        c = lax.axis_index("c")
        arrays = range(self.n) if arrays is None else arrays
        for a in arrays:
            for k in range(6, 9):
                self._sibling(a, k, 1 - c).wait_recv()
        for a in arrays:
            for k in range(4):
                self._own(a, k).wait_send()
            for k in range(4, 6):
                self._relay(a, k).wait_send()
            for k in range(6, 9):
                self._sibling(a, k, c).wait_send()


def _fill_own_slot(gathered, shards):
    chip = 2 * lax.axis_index("x") + lax.axis_index("y")
    return [lax.dynamic_update_slice(o, s[None], (chip, 0, 0)) for o, s in zip(gathered, shards)]


class _ExchangeHalves:
    def __init__(self, ins, recvs, send_sems, recv_sems):
        self.ins, self.recvs, self.send_sems, self.recv_sems = ins, recvs, send_sems, recv_sems

    @staticmethod
    def scratch(n):
        return [pltpu.SemaphoreType.DMA((n,)), pltpu.SemaphoreType.DMA((n,))]

    @staticmethod
    def out_shape(grads):
        return [jax.ShapeDtypeStruct((g.shape[0], g.shape[1] // 2, g.shape[2]), g.dtype) for g in grads]

    def _copies(self):
        x, y, c, _ = _mesh_pos()
        out = []
        for a, (src, dst) in enumerate(zip(self.ins, self.recvs)):
            hr = src.shape[1] // 2
            out.append(pltpu.make_async_remote_copy(
                src_ref=src.at[:, pl.ds((1 - c) * hr, hr), :], dst_ref=dst, send_sem=self.send_sems.at[a],
                recv_sem=self.recv_sems.at[a], device_id=(x, y, 1 - c), device_id_type=MESH))
        return out

    def start(self):
        for cp in self._copies():
            cp.start()

    def finish(self):
        for cp in self._copies():
            cp.wait()


def _exchange_halves(grads):
    n = len(grads)

    def body(*refs):
        ex = _ExchangeHalves(refs[:n], refs[n:2 * n], *refs[2 * n:])
        ex.start()
        ex.finish()

    return pl.pallas_call(
        body, name="grad_exchange_halves", out_shape=_ExchangeHalves.out_shape(grads),
        in_specs=[ANY] * n, out_specs=[ANY] * n, scratch_shapes=_ExchangeHalves.scratch(n),
    )(*grads)


class _ScatterToChips:
    def __init__(self, ins, rbufs, send_sems, recv_sems):
        self.ins, self.rbufs, self.send_sems, self.recv_sems = ins, rbufs, send_sems, recv_sems

    @staticmethod
    def scratch(n):
        return [pltpu.SemaphoreType.DMA((3 * n,)), pltpu.SemaphoreType.DMA((3 * n,))]

    @staticmethod
    def out_shape(sums):
        return [jax.ShapeDtypeStruct((3,) + s.shape[1:], BF16) for s in sums]

    def _copies(self):
        x, y, c, chips = _mesh_pos()
        out = []
        for a, (src, dst) in enumerate(zip(self.ins, self.rbufs)):
            for k, chip in enumerate(chips):
                out.append(pltpu.make_async_remote_copy(
                    src_ref=src.at[2 * chip[0] + chip[1]], dst_ref=dst.at[k], send_sem=self.send_sems.at[3 * a + k],
                    recv_sem=self.recv_sems.at[3 * a + k], device_id=(*chip, c), device_id_type=MESH))
        return out

    def start(self):
        for cp in self._copies():
            cp.start()

    def finish(self):
        for cp in self._copies():
            cp.wait()


def _gather_halves(halves):
    n = len(halves)

    def body(*refs):
        ins, outs = refs[:n], refs[n:2 * n]
        send_sems, recv_sems = refs[2 * n:]
        x, y, c, _ = _mesh_pos()
        sib = (x, y, 1 - c)
        remote = [pltpu.make_async_remote_copy(src_ref=ins[a].at[c], dst_ref=outs[a].at[c],
                                               send_sem=send_sems.at[a], recv_sem=recv_sems.at[a],
                                               device_id=sib, device_id_type=MESH) for a in range(n)]
        for cp in remote:
            cp.start()
        for a in range(n):
            pltpu.make_async_remote_copy(src_ref=ins[a].at[1 - c], dst_ref=outs[a].at[1 - c], send_sem=send_sems.at[a],
                                         recv_sem=recv_sems.at[a], device_id=sib, device_id_type=MESH).wait_recv()
        for cp in remote:
            cp.wait_send()

    return pl.pallas_call(
        body, name="grad_gather_halves",
        out_shape=[jax.ShapeDtypeStruct(h.shape, F32) for h in halves],
        in_specs=[ANY] * n, out_specs=[ANY] * n, input_output_aliases={a: a for a in range(n)},
        scratch_shapes=[pltpu.SemaphoreType.DMA((n,)), pltpu.SemaphoreType.DMA((n,))],
    )(*halves)


SMALL_A_ROWS = 24
SMALL_B_ROWS = 8
SMALL_C_ROWS = N_POOL_GROUPS * POOL_GROUP


class _AllReduceSmall:
    N_IN = 10
    SHAPES = [(SMALL_A_ROWS, D_MODEL), (SMALL_B_ROWS, D_CONV), (SMALL_C_ROWS, POOL_GROUP)]

    def __init__(self, ins, outs, scratch):
        self.ins, self.outs = ins, outs
        self.bufs, self.rcvs, self.send_sems, self.recv_sems = scratch[:3], scratch[3:6], scratch[6], scratch[7]

    @classmethod
    def scratch(cls):
        return ([pltpu.VMEM((3,) + s, F32) for s in cls.SHAPES] + [pltpu.VMEM((3,) + s, F32) for s in cls.SHAPES]
                + [pltpu.SemaphoreType.DMA((9,)), pltpu.SemaphoreType.DMA((9,))])

    @classmethod
    def out_shape(cls):
        return [jax.ShapeDtypeStruct(s, F32) for s in cls.SHAPES]

    def _copies(self, st):
        x, y, c, _ = _mesh_pos()
        peer = [(x, y, 1 - c), (1 - x, y, c), (x, 1 - y, c)][st]
        return [pltpu.make_async_remote_copy(
            src_ref=buf.at[st], dst_ref=rcv.at[st], send_sem=self.send_sems.at[3 * st + i],
            recv_sem=self.recv_sems.at[3 * st + i], device_id=peer, device_id_type=MESH)
            for i, (buf, rcv) in enumerate(zip(self.bufs, self.rcvs))]

    def pack_and_send(self):
        dg1_ref, dg1m_ref, dg2_ref, dg3_ref, dg4_ref, loss_ref, dmeta_ref, dsc_ref, dcw_ref, dpw_ref = self.ins
        a_buf, b_buf, c_buf = self.bufs

        def rowsum(v):
            return jnp.sum(v, axis=0, keepdims=True)

        a_buf[0, 0:1, :] = rowsum(dg1_ref[...] + dg1m_ref[...])
        a_buf[0, 1:2, :] = rowsum(dg2_ref[...])
        a_buf[0, 2:3, :] = rowsum(dg3_ref[...])
        a_buf[0, 3:4, :] = rowsum(dg4_ref[...])
        loss = jnp.sum(rowsum(loss_ref[...]), axis=1, keepdims=True) * (0.5 / D_MODEL)
        a_buf[0, 4:5, :] = jnp.broadcast_to(loss, (1, D_MODEL))
        a_buf[0, 5:8, :] = jnp.zeros((3, D_MODEL), F32)
        a_buf[0, 8:24, :] = dmeta_ref[...]
        b_buf[0, 0:1, :] = rowsum(dsc_ref[...])
        for k in range(3):
            b_buf[0, 1 + k:2 + k, :] = rowsum(dcw_ref[8 * k:8 * k + 8, :])
        b_buf[0, 4:8, :] = jnp.zeros((4, D_CONV), F32)
        c_buf[0] = dpw_ref[...]
        for cp in self._copies(0):
            cp.start()

    def combine(self, st):
        for cp in self._copies(st):
            cp.wait()
        if st < 2:
            for buf, rcv in zip(self.bufs, self.rcvs):
                buf[st + 1] = buf[st] + rcv[st]
            for cp in self._copies(st + 1):
                cp.start()
        else:
            for out, buf, rcv in zip(self.outs, self.bufs, self.rcvs):
                out[...] = buf[st] + rcv[st]


def _row_block(rows):
    for cand in (512, 448, 384, 352, 320, 256, 128, 64, 32, 16):
        if rows % cand == 0:
            return cand
    return rows


def _add_pairs(grad, recv, place):
    n_sh, rows2, cols = grad.shape
    hr = rows2 // 2
    br = _row_block(hr)

    def body(place_ref, a_ref, b_ref, o_ref):
        o_ref[...] = (a_ref[0] + b_ref[...]).astype(BF16)

    return pl.pallas_call(
        body, name="grad_add_pairs",
        grid_spec=pltpu.PrefetchScalarGridSpec(
            num_scalar_prefetch=1, grid=(n_sh, hr // br),
            in_specs=[pl.BlockSpec((1, 1, br, cols), lambda j, i, p: (j, p[1], i, 0)),
                      pl.BlockSpec((1, br, cols), lambda j, i, p: (j, i, 0))],
            out_specs=pl.BlockSpec((1, br, cols), lambda j, i, p: (j, i, 0))),
        out_shape=jax.ShapeDtypeStruct((n_sh, hr, cols), BF16), compiler_params=_cparams(2),
    )(place, grad.reshape(n_sh, 2, hr, cols), recv)


def _add_chips(grads, recvs, rbufs, place, scattered=(), name="grad_add_chips"):
    n, n_sc = len(grads), len(scattered)
    n_sh, rows2, cols = grads[0].shape
    hr = rows2 // 2
    br = _row_block(hr)
    n_steps = hr // br

    def body(place_ref, *refs):
        a_refs, b_refs, r_refs = refs[:n], refs[n:2 * n], refs[2 * n:3 * n]
        o_refs = refs[3 * n + n_sc:4 * n + n_sc]
        if n_sc:
            scatter = _ScatterToChips(refs[3 * n:3 * n + n_sc], refs[4 * n + n_sc:4 * n + 2 * n_sc], *refs[-2:])

            @pl.when(pl.program_id(0) == 0)
            def _():
                scatter.start()

        for a_ref, b_ref, r_ref, o_ref in zip(a_refs, b_refs, r_refs, o_refs):
            own = a_ref[0, 0] + b_ref[0]
            o_ref[0] = ((own + r_ref[0].astype(F32)) + r_ref[1].astype(F32)) + r_ref[2].astype(F32)

        if n_sc:
            @pl.when(pl.program_id(0) == n_steps - 1)
            def _():
                scatter.finish()

    outs = pl.pallas_call(
        body, name=name,
        grid_spec=pltpu.PrefetchScalarGridSpec(
            num_scalar_prefetch=1, grid=(n_steps,),
            in_specs=[pl.BlockSpec((1, 1, br, cols), lambda i, p: (p[0], p[1], i, 0))] * n
            + [pl.BlockSpec((1, br, cols), lambda i, p: (p[0], i, 0))] * n
            + [pl.BlockSpec((3, br, cols), lambda i, p: (0, i, 0))] * n + [ANY] * n_sc,
            out_specs=[pl.BlockSpec((1, br, cols), lambda i, p: (p[1], i, 0))] * n + [ANY] * n_sc,
            scratch_shapes=_ScatterToChips.scratch(n_sc) if n_sc else []),
        out_shape=[jax.ShapeDtypeStruct((2, hr, cols), F32)] * n + _ScatterToChips.out_shape(list(scattered)),
        compiler_params=_cparams(1),
    )(place, *[g.reshape(n_sh, 2, hr, cols) for g in grads], *recvs, *rbufs, *scattered)
    return outs[:n], outs[n:]


def _adamw_math(w, g, m, v):
    m2 = ADAM_B1 * m + (1.0 - ADAM_B1) * g
    v2 = ADAM_B2 * v + (1.0 - ADAM_B2) * (g * g)
    m_hat = m2 / (1.0 - ADAM_B1 ** ADAM_STEP)
    v_hat = v2 / (1.0 - ADAM_B2 ** ADAM_STEP)
    delta = -ADAM_LR * (m_hat / (jnp.sqrt(v_hat) + ADAM_EPS) + ADAM_WD * w)
    return delta, m2, v2


def _adamw_big(w, g, m, v):
    rows, cols = w.shape
    br = _row_block(rows)

    def body(w_ref, g_ref, m_ref, v_ref, d_ref, m2_ref, v2_ref):
        d, m2, v2 = _adamw_math(w_ref[...], g_ref[...], m_ref[...], v_ref[...])
        d_ref[...] = d
        m2_ref[...] = m2
        v2_ref[...] = v2

    spec = pl.BlockSpec((br, cols), lambda i: (i, 0))
    return pl.pallas_call(
        body, name="adamw_big", grid=(rows // br,),
        out_shape=[jax.ShapeDtypeStruct((rows, cols), F32)] * 3,
        in_specs=[spec] * 4, out_specs=[spec] * 3, compiler_params=_cparams(1),
    )(w, g, m, v)


def _adamw_small(groups):
    n = len(groups)

    def body(*refs):
        ins, outs = refs[:4 * n], refs[4 * n:]
        for i in range(n):
            w, g, m, v = (r[...] for r in ins[4 * i:4 * i + 4])
            d, m2, v2 = _adamw_math(w, g, m, v)
            outs[3 * i][...] = d
            outs[3 * i + 1][...] = m2
            outs[3 * i + 2][...] = v2

    vm = pl.BlockSpec(memory_space=pltpu.VMEM)
    flat = [a for grp in groups for a in grp]
    out_shape = [jax.ShapeDtypeStruct(grp[0].shape, F32) for grp in groups for _ in range(3)]
    outs = pl.pallas_call(body, name="adamw_small", out_shape=out_shape,
                          in_specs=[vm] * (4 * n), out_specs=[vm] * (3 * n))(*flat)
    return [tuple(outs[3 * i:3 * i + 3]) for i in range(n)]


def _load_weights(pairs, sem):
    for src, dst in pairs:
        cp = pltpu.make_async_copy(src, dst, sem)
        cp.start()
        cp.wait()


def _load_gathered(gathered, shards, dst_slots, sems):
    n = len(gathered)
    me = 2 * lax.axis_index("x") + lax.axis_index("y")

    def copies(j, own):
        return [pltpu.make_async_copy(shards[a] if own else gathered[a].at[j], dst_slots[a](j), sems.at[n * j + a])
                for a in range(n)]

    for wait in (False, True):
        for j in range(N_CHIPS):
            for own in (False, True):
                @pl.when((me == j) == own)
                def _():
                    for cp in copies(j, own):
                        cp.wait() if wait else cp.start()


N_MIX_SHARDS = 3


def _mixer_fwd(x3, g1, g2, poolw, pscale, shards):
    n_seq, seq, _ = x3.shape
    tm = min(TM_MIX_FWD, seq)
    n_t = seq // tm
    n_steps = n_seq * n_t
    n_ag = len(shards)
    n_ffn = n_ag - N_MIX_SHARDS
    small_rows = shards[2].shape[0]
    conv_cols = D_CONV // N_CHIPS

    def body(x_ref, g1_ref, g2_ref, pw_ref, ps_ref, *rest):
        ag = _AllGather(rest[:n_ag], rest[n_ag + 10:2 * n_ag + 10], *rest[-2:])
        (z_ref, m_ref, h1_ref, a_ref, conv_ref, pooled_ref, yc_ref, zm_ref, meta_ref,
         cw_ref) = rest[n_ag:n_ag + 10]
        win_v, wout_v, small_v, cvb, pb, load_sems = rest[2 * n_ag + 10:-2]
        s, t = pl.program_id(0), pl.program_id(1)
        step = s * n_t + t

        @pl.when(step == 0)
        def _():
            ag.start(range(N_MIX_SHARDS))
            for a in range(N_MIX_SHARDS):
                ag.relay(a)
            for a in range(N_MIX_SHARDS):
                ag.forward(a)
            ag.finish(range(N_MIX_SHARDS))
            ag.start(range(N_MIX_SHARDS, n_ag))
            _load_gathered(ag.outs[:N_MIX_SHARDS], ag.ins[:N_MIX_SHARDS],
                           [lambda j: win_v.at[j], lambda j: wout_v.at[pl.ds(j * OUT_SHARD, OUT_SHARD), :],
                            lambda j: small_v.at[j]], load_sems)

            meta = jnp.concatenate([small_v[j, 0:N_META, :] for j in range(N_CHIPS)], axis=1)
            meta_ref[...] = meta
            cw_ref[...] = jnp.concatenate([small_v[j, N_META:N_META + 3, 0:conv_cols] for j in range(N_CHIPS)], axis=1)
            a_meta = (meta * _rstd(meta) * g1_ref[...]).astype(BF16)
            for j in range(N_CHIPS):
                zm_ref[:, j * IN_SHARD:(j + 1) * IN_SHARD] = _dot(a_meta, win_v[j])

        for i in range(n_ffn):
            @pl.when(step == ((i + 1) * n_steps) // (2 * n_ffn + 2))
            def _():
                ag.relay(N_MIX_SHARDS + i)

        for i in range(n_ffn):
            @pl.when(step == min(n_steps // 2 + ((i + 1) * n_steps) // (2 * n_ffn + 2), n_steps - 1))
            def _():
                ag.forward(N_MIX_SHARDS + i)

        @pl.when(t == 0)
        def _():
            cvb[0:HALO, :] = zm_ref[:, IN_SHARD:2 * IN_SHARD] * zm_ref[:, 2 * IN_SHARD:3 * IN_SHARD]
            pb[0:HALO, :] = zm_ref[:, 3 * IN_SHARD:4 * IN_SHARD]

        @pl.when(t > 0)
        def _():
            cvb[0:HALO, :] = cvb[tm:tm + HALO, :]
            pb[0:HALO, :] = pb[tm:tm + HALO, :]

        xt = x_ref[0]
        a = (xt * _rstd(xt) * g1_ref[...]).astype(BF16)
        a_ref[...] = a
        zb = _dot(a, win_v[0])
        zc = _dot(a, win_v[1])
        zv = _dot(a, win_v[2])
        zp = _dot(a, win_v[3])
        z_ref[0, :, 0:IN_SHARD] = zb
        z_ref[0, :, IN_SHARD:2 * IN_SHARD] = zc
        z_ref[0, :, 2 * IN_SHARD:3 * IN_SHARD] = zv
        cv = zc * zv
        cvb[HALO:HALO + tm, :] = cv
        pb[HALO:HALO + tm, :] = zp
        cw = cw_ref[...]
        conv = cw[0:1] * cvb[HALO - 2:HALO - 2 + tm, :] + cw[1:2] * cvb[HALO - 1:HALO - 1 + tm, :] + cw[2:3] * cv
        conv_ref[...] = conv
        parts = [(zb * conv).astype(BF16)]
        for g in range(N_POOL_GROUPS):
            pooled = _pool_fwd(pb, g, tm).astype(BF16)
            pooled_ref[:, _gcols(g)] = pooled
            parts.append((_dot(pooled, pw_ref[g]) * ps_ref[:, _gcols(g)]).astype(BF16))
        ycat = jnp.concatenate(parts, axis=1)
        yc_ref[...] = ycat
        m = _dot(ycat, wout_v[...])
        m_ref[0] = m
        h1_ref[0] = xt + m * _rstd(m) * g2_ref[...]

        @pl.when(step == n_steps - 1)
        def _():
            ag.finish(range(N_MIX_SHARDS, n_ag))

    n_rows = n_seq * seq
    row = lambda c: pl.BlockSpec((1, tm, c), lambda s, t: (s, t, 0))
    row2 = lambda c: pl.BlockSpec((tm, c), lambda s, t: (s * n_t + t, 0))
    outs = pl.pallas_call(
        body, name="mixer_fwd", grid=(n_seq, n_t),
        out_shape=[jax.ShapeDtypeStruct((n_seq, seq, D_Z), F32), jax.ShapeDtypeStruct((n_seq, seq, D_MODEL), F32),
                   jax.ShapeDtypeStruct((n_seq, seq, D_MODEL), F32), jax.ShapeDtypeStruct((n_rows, D_MODEL), BF16),
                   jax.ShapeDtypeStruct((n_rows, D_CONV), F32), jax.ShapeDtypeStruct((n_rows, D_POOL), BF16),
                   jax.ShapeDtypeStruct((n_rows, D_MODEL), BF16), jax.ShapeDtypeStruct((N_META, D_IN_PROJ), F32),
                   jax.ShapeDtypeStruct((N_META, D_MODEL), F32), jax.ShapeDtypeStruct((3, D_CONV), F32)]
        + _AllGather.out_shape(shards),
        in_specs=[row(D_MODEL), _full((1, D_MODEL)), _full((1, D_MODEL)),
                  _full((N_POOL_GROUPS, POOL_GROUP, POOL_GROUP)), _full((1, D_POOL))] + [ANY] * n_ag,
        out_specs=[row(D_Z), row(D_MODEL), row(D_MODEL), row2(D_MODEL), row2(D_CONV), row2(D_POOL), row2(D_MODEL),
                   _full((N_META, D_IN_PROJ)), _full((N_META, D_MODEL)), _full((3, D_CONV))] + [ANY] * n_ag,
        scratch_shapes=[pltpu.VMEM((N_CHIPS, D_MODEL, IN_SHARD), BF16), pltpu.VMEM((D_MODEL, D_MODEL), BF16),
                        pltpu.VMEM((N_CHIPS, small_rows, D_MODEL // N_CHIPS), F32),
                        pltpu.VMEM((HALO + tm, D_CONV), F32), pltpu.VMEM((HALO + tm, D_POOL), F32),
                        pltpu.SemaphoreType.DMA((N_MIX_SHARDS * N_CHIPS,))] + _AllGather.scratch(n_ag),
        compiler_params=_cparams(2),
    )(x3, g1, g2, poolw, pscale, *shards)
    gathered = outs[10:]
    return outs[:10], _fill_own_slot(gathered[:N_MIX_SHARDS], shards[:N_MIX_SHARDS]) + list(gathered[N_MIX_SHARDS:])


def _ffn_chunks():
    out, r0 = [], 0
    while r0 < D_FF:
        out.append((r0, min(FF_CHUNK, D_FF - r0)))
        r0 += FF_CHUNK
    return out


def _ffn_fwd_bwd(h1, target, g3, g4, gathered, shards):
    n_rows = h1.shape[0]
    tm = min(TM_FFN, n_rows)
    chunks = _ffn_chunks()

    def body(h1_ref, t_ref, g3_ref, g4_ref, wg_all, wu_all, wd_all, wg_s, wu_s, wd_s,
             dh1_ref, f_ref, dd_ref, ds_ref, du_ref, gg_ref, loss_ref, dg3_ref, dg4_ref,
             wg_v, wu_v, wd_v, s_sc, u_sc, sems):
        @pl.when(pl.program_id(0) == 0)
        def _():
            _load_gathered([wg_all, wu_all, wd_all], [wg_s, wu_s, wd_s],
                           [functools.partial(lambda v, j: v.at[pl.ds(j * FF_SHARD, FF_SHARD), :], v)
                            for v in (wg_v, wu_v, wd_v)], sems)
            loss_ref[...] = jnp.zeros_like(loss_ref)
            dg3_ref[...] = jnp.zeros_like(dg3_ref)
            dg4_ref[...] = jnp.zeros_like(dg4_ref)

        h1v = h1_ref[...]
        r3 = _rstd(h1v)
        hh = h1v * r3
        g3v, g4v = g3_ref[...], g4_ref[...]
        f = (hh * g3v).astype(BF16)
        f_ref[...] = f
        d = jnp.zeros((tm, D_MODEL), F32)
        for r0, sz in chunks:
            s = _dot_nt(f, wg_v[r0:r0 + sz, :])
            u = _dot_nt(f, wu_v[r0:r0 + sz, :])
            s_sc[:, r0:r0 + sz] = s
            u_sc[:, r0:r0 + sz] = u
            gc = (s * _sigmoid(s) * u).astype(BF16)
            gg_ref[:, r0:r0 + sz] = gc
            d = d + _dot(gc, wd_v[r0:r0 + sz, :])
        r4 = _rstd(d)
        dh = d * r4
        err = (h1v + dh * g4v) - t_ref[...]
        loss_ref[...] += _rows8(err * err)
        dy = err * (1.0 / D_MODEL)
        dg4_ref[...] += _rows8(dy * dh)
        ddb = _rms_bwd(dy, dh, r4, g4v).astype(BF16)
        dd_ref[...] = ddb
        df = jnp.zeros((tm, D_MODEL), F32)
        for r0, sz in chunks:
            dgg = _dot_nt(ddb, wd_v[r0:r0 + sz, :])
            s = s_sc[:, r0:r0 + sz]
            u = u_sc[:, r0:r0 + sz]
            sig = _sigmoid(s)
            dsc = (dgg * u * (sig * (1.0 + s * (1.0 - sig)))).astype(BF16)
            duc = (dgg * (s * sig)).astype(BF16)
            ds_ref[:, r0:r0 + sz] = dsc
            du_ref[:, r0:r0 + sz] = duc
            df = df + _dot(dsc, wg_v[r0:r0 + sz, :]) + _dot(duc, wu_v[r0:r0 + sz, :])
        dg3_ref[...] += _rows8(df * hh)
        dh1_ref[...] = dy + _rms_bwd(df, hh, r3, g3v)

    row = pl.BlockSpec((tm, D_MODEL), lambda i: (i, 0))
    ffrow = pl.BlockSpec((tm, D_FF), lambda i: (i, 0))
    acc = _full((8, D_MODEL))
    act_bf = jax.ShapeDtypeStruct((n_rows, D_MODEL), BF16)
    ff_bf = jax.ShapeDtypeStruct((n_rows, D_FF), BF16)
    acc_shape = jax.ShapeDtypeStruct((8, D_MODEL), F32)
    w_vmem = pltpu.VMEM((D_FF, D_MODEL), BF16)
    return pl.pallas_call(
        body, name="ffn_fwd_bwd", grid=(n_rows // tm,),
        out_shape=[jax.ShapeDtypeStruct((n_rows, D_MODEL), F32), act_bf, act_bf, ff_bf, ff_bf, ff_bf,
                   acc_shape, acc_shape, acc_shape],
        in_specs=[row, row, _full((1, D_MODEL)), _full((1, D_MODEL))] + [ANY] * 6,
        out_specs=[row, row, row, ffrow, ffrow, ffrow, acc, acc, acc],
        scratch_shapes=[w_vmem, w_vmem, w_vmem, pltpu.VMEM((tm, D_FF), F32), pltpu.VMEM((tm, D_FF), F32),
                        pltpu.SemaphoreType.DMA((3 * N_CHIPS,))],
        compiler_params=_cparams(1),
    )(h1, target, g3, g4, *gathered, *shards)


def _ffn_weight_grads(name, acts, other, exchanged):
    n_rows = other.shape[0]
    n_a, n_ex = len(acts), len(exchanged)
    n_c = n_a
    tk = min(TK_DW, n_rows)
    n_k = n_rows // tk
    half = D_FF // n_c

    def body(other_ref, *rest):
        act_refs = rest[:n_a]
        out_refs = rest[n_a + n_ex:2 * n_a + n_ex]
        c, k = pl.program_id(0), pl.program_id(1)
        if n_ex:
            ex = _ExchangeHalves(rest[n_a:n_a + n_ex], rest[2 * n_a + n_ex:2 * n_a + 2 * n_ex], *rest[-2:])

            @pl.when((c == 0) & (k == 0))
            def _():
                ex.start()

        @pl.when(k == 0)
        def _():
            for o in out_refs:
                o[...] = jnp.zeros_like(o)

        ov = other_ref[...]
        for a, o in zip(act_refs, out_refs):
            o[...] += _dot_tn(a[...], ov)

        if n_ex:
            @pl.when((c == n_c - 1) & (k == n_k - 1))
            def _():
                ex.finish()

    row = pl.BlockSpec((tk, D_MODEL), lambda c, k: (k, 0))
    ffrow = pl.BlockSpec((tk, half), lambda c, k: (k, c))
    out = pl.BlockSpec((half, D_MODEL), lambda c, k: (c, 0))
    outs = pl.pallas_call(
        body, name=name, grid=(n_c, n_k),
        out_shape=[jax.ShapeDtypeStruct((D_FF, D_MODEL), F32)] * n_a + _ExchangeHalves.out_shape(exchanged),
        in_specs=[row] + [ffrow] * n_a + [ANY] * n_ex, out_specs=[out] * n_a + [ANY] * n_ex,
        scratch_shapes=_ExchangeHalves.scratch(n_ex) if n_ex else [],
        compiler_params=_cparams(2),
    )(other, *acts, *exchanged)
    return outs[:n_a], outs[n_a:]


def _mixer_bwd(dh1, m3, z3, conv2, pooled2, x3, zmeta, g1, g2, convw, poolw, pscale, win_all, wout, exchanged,
               scattered):
    n_seq, seq, _ = x3.shape
    tm = min(TM_MIX_BWD, seq)
    sub = min(SUB_MIX_BWD, tm)
    n_t = seq // tm
    n_ex, n_sc = len(exchanged), len(scattered)
    n_cm = n_ex + n_sc

    def body(dh1_ref, m_ref, z_ref, conv_ref, pooled_ref, x_ref, zm_ref, g1_ref, g2_ref, cw_ref, pw_ref, ps_ref,
             win_hbm, wout_hbm, *rest):
        outs0 = n_cm + 9
        ex = _ExchangeHalves(rest[:n_ex], rest[outs0:outs0 + n_ex], *rest[-4:-2])
        sc = _ScatterToChips(rest[n_ex:n_cm], rest[outs0 + n_ex:outs0 + n_cm], *rest[-2:])
        dx_ref, dz_ref, dm_ref, dg1_ref, dg2_ref, dsc_ref, dcw_ref, dpw_ref, dzm_ref = rest[n_cm:outs0]
        win_v, wout_v, dcb, dqb, mcb, mqb, sem = rest[outs0 + n_cm:-4]
        s, i = pl.program_id(0), pl.program_id(1)
        tr = n_t - 1 - i

        @pl.when((s == 0) & (i == 0))
        def _():
            sc.start()
            ex.start()
            _load_weights([(win_hbm, win_v), (wout_hbm, wout_v)], sem)
            for ref in (dg1_ref, dg2_ref, dsc_ref, dcw_ref, dpw_ref, dzm_ref):
                ref[...] = jnp.zeros_like(ref)

        @pl.when(i == 0)
        def _():
            dcb[tm:tm + HALO, :] = jnp.zeros((HALO, D_CONV), F32)
            dqb[tm:tm + HALO, :] = jnp.zeros((HALO, D_POOL), F32)

        @pl.when(i > 0)
        def _():
            dcb[tm:tm + HALO, :] = dcb[0:HALO, :]
            dqb[tm:tm + HALO, :] = dqb[0:HALO, :]

        g1v, g2v = g1_ref[...], g2_ref[...]
        cw = cw_ref[...]

        for r0 in range(tm - sub, -1, -sub):
            rows = slice(r0, r0 + sub)
            dh1v = dh1_ref[0, rows, :]
            mv = m_ref[0, rows, :]
            r2 = _rstd(mv)
            mh = mv * r2
            dg2_ref[...] += _rows8(dh1v * mh)
            dmb = _rms_bwd(dh1v, mh, r2, g2v).astype(BF16)
            dm_ref[rows, :] = dmb
            dyc = _dot_nt(dmb, wout_v[...])
            dyconv = dyc[:, 0:D_CONV]

            for g in range(N_POOL_GROUPS):
                pooled = pooled_ref[rows, _gcols(g)]
                mixed = _dot(pooled, pw_ref[g])
                scale = ps_ref[:, _gcols(g)]
                dyp = dyc[:, D_CONV + g * POOL_GROUP:D_CONV + (g + 1) * POOL_GROUP]
                dsc_ref[:, _gcols(g)] += _rows8(dyp * mixed)
                dmix = (dyp * scale).astype(BF16)
                dpw_ref[g] += _dot_tn(pooled, dmix)
                dqb[rows, _gcols(g)] = _dot_nt(dmix, pw_ref[g])

            zb = z_ref[0, rows, 0:IN_SHARD]
            zc = z_ref[0, rows, IN_SHARD:2 * IN_SHARD]
            zv = z_ref[0, rows, 2 * IN_SHARD:3 * IN_SHARD]
            dconv = dyconv * zb
            dcb[rows, :] = dconv
            d1 = dcb[r0 + 1:r0 + 1 + sub, :]
            d2 = dcb[r0 + 2:r0 + 2 + sub, :]
            dcv = cw[2:3] * dconv + cw[1:2] * d1 + cw[0:1] * d2
            cv = zc * zv
            dcw_ref[0:8, :] += _rows8(cv * d2)
            dcw_ref[8:16, :] += _rows8(cv * d1)
            dcw_ref[16:24, :] += _rows8(cv * dconv)
            dzs = [(dyconv * conv_ref[rows, :]).astype(BF16), (dcv * zv).astype(BF16), (dcv * zc).astype(BF16),
                   jnp.concatenate([_pool_bwd(dqb, g, r0, sub) for g in range(N_POOL_GROUPS)], axis=1).astype(BF16)]
            da = jnp.zeros((sub, D_MODEL), F32)
            for j in range(N_CHIPS):
                dz_ref[j, rows, :] = dzs[j]
                da = da + _dot_nt(dzs[j], win_v[j])
            xt = x_ref[0, rows, :]
            r1 = _rstd(xt)
            xh = xt * r1
            dg1_ref[...] += _rows8(da * xh)
            dx_ref[0, rows, :] = dh1v + _rms_bwd(da, xh, r1, g1v)

        @pl.when(tr == 0)
        def _():
            mcb[0:HALO, :] = jnp.zeros((HALO, D_CONV), F32)
            mqb[0:HALO, :] = jnp.zeros((HALO, D_POOL), F32)
            mcb[HALO:2 * HALO, :] = dcb[0:HALO, :]
            mqb[HALO:2 * HALO, :] = dqb[0:HALO, :]
            m1 = mcb[1:1 + HALO, :]
            m2 = mcb[2:2 + HALO, :]
            zc_m = zm_ref[:, IN_SHARD:2 * IN_SHARD]
            zv_m = zm_ref[:, 2 * IN_SHARD:3 * IN_SHARD]
            cv_m = zc_m * zv_m
            dcw_ref[0:8, :] += _rows8(cv_m * m2)
            dcw_ref[8:16, :] += _rows8(cv_m * m1)
            dcv_m = cw[1:2] * m1 + cw[0:1] * m2
            dzm_ref[:, IN_SHARD:2 * IN_SHARD] += dcv_m * zv_m
            dzm_ref[:, 2 * IN_SHARD:3 * IN_SHARD] += dcv_m * zc_m
            dzm_ref[:, 3 * IN_SHARD:4 * IN_SHARD] += jnp.concatenate(
                [_pool_bwd(mqb, g, 0, HALO) for g in range(N_POOL_GROUPS)], axis=1)

        @pl.when((s == n_seq - 1) & (i == n_t - 1))
        def _():
            ex.finish()
            sc.finish()

    row3 = lambda c: pl.BlockSpec((1, tm, c), lambda s, i: (s, n_t - 1 - i, 0))
    row2 = lambda c: pl.BlockSpec((tm, c), lambda s, i: (s * n_t + n_t - 1 - i, 0))
    n_rows = n_seq * seq
    outs = pl.pallas_call(
        body, name="mixer_bwd", grid=(n_seq, n_t),
        out_shape=[jax.ShapeDtypeStruct((n_seq, seq, D_MODEL), F32),
                   jax.ShapeDtypeStruct((N_CHIPS, n_rows, IN_SHARD), BF16), jax.ShapeDtypeStruct((n_rows, D_MODEL), BF16),
                   jax.ShapeDtypeStruct((8, D_MODEL), F32), jax.ShapeDtypeStruct((8, D_MODEL), F32),
                   jax.ShapeDtypeStruct((8, D_POOL), F32), jax.ShapeDtypeStruct((24, D_CONV), F32),
                   jax.ShapeDtypeStruct((N_POOL_GROUPS, POOL_GROUP, POOL_GROUP), F32),
                   jax.ShapeDtypeStruct((N_META, D_IN_PROJ), F32)]
        + _ExchangeHalves.out_shape(exchanged) + _ScatterToChips.out_shape(scattered),
        in_specs=[row3(D_MODEL), row3(D_MODEL), row3(D_Z), row2(D_CONV), row2(D_POOL), row3(D_MODEL),
                  _full((N_META, D_IN_PROJ)), _full((1, D_MODEL)), _full((1, D_MODEL)), _full((3, D_CONV)),
                  _full((N_POOL_GROUPS, POOL_GROUP, POOL_GROUP)), _full((1, D_POOL)), ANY, ANY] + [ANY] * n_cm,
        out_specs=[row3(D_MODEL), pl.BlockSpec((N_CHIPS, tm, IN_SHARD), lambda s, i: (0, s * n_t + n_t - 1 - i, 0)),
                   row2(D_MODEL),
                   _full((8, D_MODEL)), _full((8, D_MODEL)), _full((8, D_POOL)), _full((24, D_CONV)),
                   _full((N_POOL_GROUPS, POOL_GROUP, POOL_GROUP)), _full((N_META, D_IN_PROJ))] + [ANY] * n_cm,
        scratch_shapes=[pltpu.VMEM((N_CHIPS, D_MODEL, IN_SHARD), BF16), pltpu.VMEM((D_MODEL, D_MODEL), BF16),
                        pltpu.VMEM((tm + HALO, D_CONV), F32), pltpu.VMEM((tm + HALO, D_POOL), F32),
                        pltpu.VMEM((2 * HALO, D_CONV), F32), pltpu.VMEM((2 * HALO, D_POOL), F32),
                        pltpu.SemaphoreType.DMA] + _ExchangeHalves.scratch(n_ex) + _ScatterToChips.scratch(n_sc),
        compiler_params=_cparams(2),
    )(dh1, m3, z3, conv2, pooled2, x3, zmeta, g1, g2, convw, poolw, pscale, win_all, wout, *exchanged, *scattered)
    return outs[:9], outs[9:9 + n_ex], outs[9 + n_ex:]


def _meta_bwd(dzm, meta_full, g1, win_all):
    def body(dzm_ref, meta_ref, g1_ref, win_ref, dmeta_ref, dg1_ref, a_ref, dzb_ref):
        xm = meta_ref[...]
        r = _rstd(xm)
        xh = xm * r
        g1v = g1_ref[...]
        a_ref[...] = (xh * g1v).astype(BF16)
        da = jnp.zeros((N_META, D_MODEL), F32)
        for j in range(N_CHIPS):
            dzj = dzm_ref[:, j * IN_SHARD:(j + 1) * IN_SHARD].astype(BF16)
            dzb_ref[j] = dzj
            da = da + _dot_nt(dzj, win_ref[j])
        dg1_ref[...] = _rows8(da * xh)
        dmeta_ref[...] = _rms_bwd(da, xh, r, g1v)

    vm = pl.BlockSpec(memory_space=pltpu.VMEM)
    return pl.pallas_call(
        body, name="meta_bwd",
        out_shape=[jax.ShapeDtypeStruct((N_META, D_MODEL), F32), jax.ShapeDtypeStruct((8, D_MODEL), F32),
                   jax.ShapeDtypeStruct((N_META, D_MODEL), BF16), jax.ShapeDtypeStruct((N_CHIPS, N_META, IN_SHARD), BF16)],
        in_specs=[vm] * 4, out_specs=[vm] * 4,
    )(dzm, meta_full, g1, win_all)


def _mixer_weight_grads(a, dz, ycat, dm, a_meta, dz_meta, ffn_sums, small):
    n_rows = a.shape[0]
    tk = min(TK_DW, n_rows)
    n_k = n_rows // tk
    n_sc, n_sm = len(ffn_sums), _AllReduceSmall.N_IN

    def body(a_ref, dz_ref, yc_ref, dm_ref, am_ref, dzm_ref, *rest):
        ins, outs, scratch = rest[:n_sc + n_sm], rest[n_sc + n_sm:2 * n_sc + n_sm + 5], rest[2 * n_sc + n_sm + 5:]
        dwin_ref, dwout_ref = outs[:2]
        scatter = _ScatterToChips(ins[:n_sc], outs[2:2 + n_sc], *scratch[:2])
        reduce_small = _AllReduceSmall(ins[n_sc:], outs[2 + n_sc:], scratch[2:])
        k = pl.program_id(0)

        @pl.when(k == 0)
        def _():
            scatter.start()
            reduce_small.pack_and_send()
            am_t = am_ref[...].T
            for j in range(N_CHIPS):
                dwin_ref[j] = _dot(am_t, dzm_ref[j])
            dwout_ref[...] = jnp.zeros_like(dwout_ref)

        for st in range(2):
            @pl.when(k == ((st + 1) * n_k) // 3)
            def _():
                reduce_small.combine(st)

        a_t = a_ref[...].T
        for j in range(N_CHIPS):
            dwin_ref[j] += _dot(a_t, dz_ref[j])
        dwout_ref[...] += _dot_tn(yc_ref[...], dm_ref[...])

        @pl.when(k == n_k - 1)
        def _():
            reduce_small.combine(2)
            scatter.finish()

    row = pl.BlockSpec((tk, D_MODEL), lambda k: (k, 0))
    outs = pl.pallas_call(
        body, name="mixer_weight_grads", grid=(n_k,),
        out_shape=[jax.ShapeDtypeStruct((N_CHIPS, D_MODEL, IN_SHARD), F32),
                   jax.ShapeDtypeStruct((D_MODEL, D_MODEL), F32)] + _ScatterToChips.out_shape(ffn_sums)
        + _AllReduceSmall.out_shape(),
        in_specs=[row, pl.BlockSpec((N_CHIPS, tk, IN_SHARD), lambda k: (0, k, 0)), row, row,
                  _full((N_META, D_MODEL)), _full((N_CHIPS, N_META, IN_SHARD))] + [ANY] * n_sc
        + [_full(s.shape) for s in small],
        out_specs=[_full((N_CHIPS, D_MODEL, IN_SHARD)), _full((D_MODEL, D_MODEL))] + [ANY] * n_sc
        + [_full(s) for s in _AllReduceSmall.SHAPES],
        scratch_shapes=_ScatterToChips.scratch(n_sc) + _AllReduceSmall.scratch(),
        compiler_params=_cparams(1),
    )(a, dz, ycat, dm, a_meta, dz_meta, *ffn_sums, *small)
    return ([outs[0], outs[1].reshape(N_CHIPS, OUT_SHARD, D_MODEL)], outs[2:2 + n_sc], outs[2 + n_sc:])


def kernel(x, meta_tokens, norm_mix_pre, w_in, conv_w, pool_w, pool_scale, w_out, norm_mix_post, norm_ffn_pre, w_gate, w_up, w_down, norm_ffn_post, loss_target, m_meta_tokens, m_norm_mix_pre, m_w_in, m_conv_w, m_pool_w, m_pool_scale, m_w_out, m_norm_mix_post, m_norm_ffn_pre, m_w_gate, m_w_up, m_w_down, m_norm_ffn_post, v_meta_tokens, v_norm_mix_pre, v_w_in, v_conv_w, v_pool_w, v_pool_scale, v_w_out, v_norm_mix_post, v_norm_ffn_pre, v_w_gate, v_w_up, v_w_down, v_norm_ffn_post):
    n_seq, seq, _ = x.shape
    n_rows = n_seq * seq
    chip = 2 * lax.axis_index("x") + lax.axis_index("y")
    meta_cols = D_MODEL // N_CHIPS
    conv_cols = D_CONV // N_CHIPS

    small = jnp.zeros((2 * HALO, meta_cols), F32)
    small = small.at[0:N_META, :].set(meta_tokens).at[N_META:N_META + 3, 0:conv_cols].set(conv_w[0])
    poolw_bf = pool_w[0].astype(BF16)
    pscale = pool_scale
    g1, g2, g3, g4 = norm_mix_pre, norm_mix_post, norm_ffn_pre, norm_ffn_post
    place = jnp.stack([chip, lax.axis_index("c")]).astype(jnp.int32)

    ffn_shards = [w_gate[0].T.astype(BF16), w_up[0].T.astype(BF16), w_down[0].astype(BF16)]
    ((z3, m3, h1, a_bf, conv2, pooled2, yc_bf, zmeta, meta_full, conv_full),
     (win_all, wout_all, _, *ffn_gathered)) = _mixer_fwd(
        x, g1, g2, poolw_bf, pscale, [w_in[0].astype(BF16), w_out[0].astype(BF16), small] + ffn_shards)
    wout_full = wout_all.reshape(D_MODEL, D_MODEL)
    dh1, f_bf, dd_bf, ds_bf, du_bf, gg_bf, lossp, dg3p, dg4p = _ffn_fwd_bwd(
        h1.reshape(n_rows, D_MODEL), loss_target.reshape(n_rows, D_MODEL), g3, g4, ffn_gathered, ffn_shards)
    as_shards = lambda g: g.reshape(N_CHIPS, FF_SHARD, D_MODEL)
    (dwg_t, dwu_t), _ = _ffn_weight_grads("ffn_weight_grads_gate_up", [ds_bf, du_bf], f_bf, [])
    dwg_t, dwu_t = as_shards(dwg_t), as_shards(dwu_t)
    (dwd,), (dwg_recv, dwu_recv) = _ffn_weight_grads("ffn_weight_grads_down", [gg_bf], dd_bf, [dwg_t, dwu_t])
    dwd = as_shards(dwd)
    ((grad_x, dz_bf, dm_bf, dg1p, dg2p, dscp, dcwp, dpw, dzm), (dwd_recv,), (dwg_rbuf, dwu_rbuf)) = _mixer_bwd(
        dh1.reshape(n_seq, seq, D_MODEL), m3, z3, conv2, pooled2, x, zmeta, g1, g2, conv_full, poolw_bf, pscale,
        win_all, wout_full, [dwd], [_add_pairs(dwg_t, dwg_recv, place), _add_pairs(dwu_t, dwu_recv, place)])
    dmeta, dg1m, a_meta, dz_meta = _meta_bwd(dzm, meta_full, g1, win_all)
    mix_grads, (dwd_rbuf,), (a_red, b_red, c_red) = _mixer_weight_grads(
        a_bf, dz_bf, yc_bf, dm_bf, a_meta, dz_meta, [_add_pairs(dwd, dwd_recv, place)],
        [dg1p, dg1m, dg2p, dg3p, dg4p, lossp, dmeta, dscp, dcwp, dpw.reshape(SMALL_C_ROWS, POOL_GROUP)])

    mix_recvs = _exchange_halves(mix_grads)
    ffn_red, mix_rbufs = _add_chips([dwg_t, dwu_t, dwd], [dwg_recv, dwu_recv, dwd_recv],
                                    [dwg_rbuf, dwu_rbuf, dwd_rbuf], place,
                                    [_add_pairs(g, r, place) for g, r in zip(mix_grads, mix_recvs)],
                                    name="grad_add_chips_ffn")
    mix_red = [_add_chips([g], [r], [rb], place)[0][0] for g, r, rb in zip(mix_grads, mix_recvs, mix_rbufs)]
    reduced = _gather_halves(mix_red + list(ffn_red))
    g_win, g_wout, g_wg_t, g_wu_t, g_wd = [r.reshape(2 * r.shape[1], r.shape[2]) for r in reduced]

    loss = a_red[4, 0]
    g_g1, g_g2, g_g3, g_g4 = a_red[0:1], a_red[1:2], a_red[2:3], a_red[3:4]
    g_meta = lax.dynamic_slice(a_red, (8, chip * meta_cols), (N_META, meta_cols))
    g_pscale = b_red[0:1]
    g_conv = lax.dynamic_slice(b_red, (1, chip * conv_cols), (3, conv_cols))
    g_poolw = c_red

    big = [(w_in[0], g_win, m_w_in[0], v_w_in[0]), (w_out[0], g_wout, m_w_out[0], v_w_out[0]),
           (w_gate[0].T, g_wg_t, m_w_gate[0].T, v_w_gate[0].T), (w_up[0].T, g_wu_t, m_w_up[0].T, v_w_up[0].T),
           (w_down[0], g_wd, m_w_down[0], v_w_down[0])]
    big_out = [_adamw_big(w, g, m, v) for (w, g, m, v) in big]
    big_out[2] = [o.T for o in big_out[2]]
    big_out[3] = [o.T for o in big_out[3]]
    g_wg, g_wu = g_wg_t.T, g_wu_t.T
    small_groups = [
        (meta_tokens, g_meta, m_meta_tokens, v_meta_tokens),
        (g1, g_g1, m_norm_mix_pre, v_norm_mix_pre),
        (conv_w[0], g_conv, m_conv_w[0], v_conv_w[0]),
        (pool_w.reshape(SMALL_C_ROWS, POOL_GROUP), g_poolw, m_pool_w.reshape(SMALL_C_ROWS, POOL_GROUP),
         v_pool_w.reshape(SMALL_C_ROWS, POOL_GROUP)),
        (pool_scale, g_pscale, m_pool_scale, v_pool_scale),
        (g2, g_g2, m_norm_mix_post, v_norm_mix_post),
        (g3, g_g3, m_norm_ffn_pre, v_norm_ffn_pre),
        (g4, g_g4, m_norm_ffn_post, v_norm_ffn_post),
    ]
    small_out = _adamw_small(small_groups)

    grads_out = [g_meta, g_g1, g_win[None], g_conv[None], g_poolw.reshape(pool_w.shape), g_pscale, g_wout[None],
                 g_g2, g_g3, g_wg[None], g_wu[None], g_wd[None], g_g4]
    s_meta, s_g1, s_conv, s_poolw, s_pscale, s_g2, s_g3, s_g4 = small_out
    b_win, b_wout, b_wg, b_wu, b_wd = big_out

    def leaf(k):
        return [s_meta[k], s_g1[k], b_win[k][None], s_conv[k][None], s_poolw[k].reshape(pool_w.shape), s_pscale[k],
                b_wout[k][None], s_g2[k], s_g3[k], b_wg[k][None], b_wu[k][None], b_wd[k][None], s_g4[k]]

    return (loss, grad_x, *grads_out, *leaf(0), *leaf(1), *leaf(2))
```

```python
import functools

import jax
import jax.numpy as jnp
from jax import lax
from jax.experimental import pallas as pl
from jax.experimental.pallas import tpu as pltpu

F32 = jnp.float32
BF16 = jnp.bfloat16
MESH = pl.DeviceIdType.MESH

D_MODEL = 1024
D_CONV = 512
D_POOL = 512
POOL_GROUP = 128
N_POOL_GROUPS = 4
D_IN_PROJ = 2048
D_FF = 2816
N_CHIPS = 4
FF_SHARD = D_FF // N_CHIPS
IN_SHARD = D_IN_PROJ // N_CHIPS
OUT_SHARD = D_MODEL // N_CHIPS
D_Z = 3 * IN_SHARD
N_META = 16
HALO = 16
RMS_EPS = 1e-6

ADAM_LR = 0.001
ADAM_B1 = 0.9
ADAM_B2 = 0.999
ADAM_EPS = 1e-08
ADAM_WD = 0.01
ADAM_STEP = 10

TM_MIX_FWD = 512
TM_MIX_BWD = 512
SUB_MIX_BWD = 512
TM_FFN = 256
TK_DW = 1024
FF_CHUNK = 1024
VMEM_LIMIT = 56 * 1024 * 1024


def _cparams(n_grid):
    return pltpu.CompilerParams(dimension_semantics=("arbitrary",) * n_grid, vmem_limit_bytes=VMEM_LIMIT)


def _dot(a, b):
    return jnp.dot(a, b, preferred_element_type=F32)


def _dot_nt(a, b):
    return lax.dot_general(a, b, (((1,), (1,)), ((), ())), preferred_element_type=F32)


def _dot_tn(a, b):
    return lax.dot_general(a, b, (((0,), (0,)), ((), ())), preferred_element_type=F32)


def _rows8(v):
    r, c = v.shape
    return v.reshape(r // 8, 8, c).sum(axis=0)


def _rstd(v):
    return lax.rsqrt(jnp.mean(v * v, axis=-1, keepdims=True) + RMS_EPS)


def _rms_bwd(dy, xhat, rstd, gain):
    dyg = dy * gain
    return rstd * (dyg - xhat * jnp.mean(dyg * xhat, axis=-1, keepdims=True))


def _sigmoid(v):
    return 1.0 / (1.0 + jnp.exp(-v))


def _gcols(g):
    return slice(g * POOL_GROUP, (g + 1) * POOL_GROUP)


def _window_sum(e, g, ahead):
    n = e.shape[0]
    w = e
    for level in range(g + 1):
        shift = 1 << level
        w = w + pltpu.roll(w, (n - shift) if ahead else shift, 0)
    return w


def _pool_fwd(pb, g, n):
    e = pb[0:HALO + n, _gcols(g)]
    return _window_sum(e, g, False)[HALO:, :] * (1.0 / (2 << g)) - e[HALO:, :]


def _pool_bwd(qb, g, r0, n):
    e = qb[r0:r0 + n + HALO, _gcols(g)]
    return _window_sum(e, g, True)[0:n, :] * (1.0 / (2 << g)) - e[0:n, :]


def _full(shape):
    nd = len(shape)
    return pl.BlockSpec(shape, lambda *_: (0,) * nd)


ANY = pl.BlockSpec(memory_space=pl.ANY)


def _mesh_pos():
    x, y, c = lax.axis_index("x"), lax.axis_index("y"), lax.axis_index("c")
    chips = [(1 - x, y), (x, 1 - y), (1 - x, 1 - y)]
    return x, y, c, chips


def _half(ref, h):
    hr = ref.shape[0] // 2
    return ref.at[pl.ds(h * hr, hr), :]


class _AllGather:
    PER_ARRAY = 9

    def __init__(self, ins, outs, send_sems, recv_sems):
        self.ins, self.outs, self.send_sems, self.recv_sems = ins, outs, send_sems, recv_sems
        self.n = len(ins)

    @classmethod
    def scratch(cls, n):
        return [pltpu.SemaphoreType.DMA((cls.PER_ARRAY * n,)), pltpu.SemaphoreType.DMA((cls.PER_ARRAY * n,))]

    @staticmethod
    def out_shape(shards):
        return [jax.ShapeDtypeStruct((N_CHIPS,) + s.shape, s.dtype) for s in shards]

    def _copy(self, a, k, src, dst, to):
        i = self.PER_ARRAY * a + k
        return pltpu.make_async_remote_copy(src_ref=src, dst_ref=dst, send_sem=self.send_sems.at[i],
                                            recv_sem=self.recv_sems.at[i], device_id=to, device_id_type=MESH)

    def _piece(self, a, chip, piece, h=None):
        h = lax.axis_index("c") if h is None else h
        rows = self.ins[a].shape[0] // 4
        return self.outs[a].at[chip].at[pl.ds((2 * h + piece) * rows, rows), :]

    def _own(self, a, k):
        x, y, c, chips = _mesh_pos()
        piece = (1, 0, 0, 1)[k]
        rows = self.ins[a].shape[0] // 4
        src = self.ins[a].at[pl.ds((2 * c + piece) * rows, rows), :]
        return self._copy(a, k, src, self._piece(a, 2 * x + y, piece), (*chips[k // 2], c))

    def _relay(self, a, k):
        x, y, c, chips = _mesh_pos()
        source, to, piece = (chips[1], chips[0], 0) if k == 4 else (chips[0], chips[1], 1)
        rows = self._piece(a, 2 * source[0] + source[1], piece)
        return self._copy(a, k, rows, rows, (*to, c))

    def _sibling(self, a, k, h):
        x, y, c, chips = _mesh_pos()
        chip = chips[k - 6]
        slot = _half(self.outs[a].at[2 * chip[0] + chip[1]], h)
        return self._copy(a, k, slot, slot, (x, y, 1 - c))

    def start(self, arrays=None):
        for a in (range(self.n) if arrays is None else arrays):
            for k in range(4):
                self._own(a, k).start()

    def relay(self, a):
        self._own(a, 2).wait_recv()
        self._relay(a, 4).start()
        self._own(a, 0).wait_recv()
        self._relay(a, 5).start()

    def forward(self, a):
        c = lax.axis_index("c")
        self._own(a, 1).wait_recv()
        self._sibling(a, 6, c).start()
        self._own(a, 3).wait_recv()
        self._sibling(a, 7, c).start()
        self._relay(a, 4).wait_recv()
        self._relay(a, 5).wait_recv()
        self._sibling(a, 8, c).start()

    def finish(self, arrays=None):
        c = lax.axis_index("c")
        arrays = range(self.n) if arrays is None else arrays
        for a in arrays:
            for k in range(6, 9):
                self._sibling(a, k, 1 - c).wait_recv()
        for a in arrays:
            for k in range(4):
                self._own(a, k).wait_send()
            for k in range(4, 6):
                self._relay(a, k).wait_send()
            for k in range(6, 9):
                self._sibling(a, k, c).wait_send()


class _ExchangeHalves:
    def __init__(self, ins, recvs, send_sems, recv_sems):
        self.ins, self.recvs, self.send_sems, self.recv_sems = ins, recvs, send_sems, recv_sems

    @staticmethod
    def scratch(n):
        return [pltpu.SemaphoreType.DMA((n,)), pltpu.SemaphoreType.DMA((n,))]

    @staticmethod
    def out_shape(grads):
        return [jax.ShapeDtypeStruct((g.shape[0], g.shape[1] // 2, g.shape[2]), g.dtype) for g in grads]

    def _copies(self):
        x, y, c, _ = _mesh_pos()
        out = []
        for a, (src, dst) in enumerate(zip(self.ins, self.recvs)):
            hr = src.shape[1] // 2
            out.append(pltpu.make_async_remote_copy(
                src_ref=src.at[:, pl.ds((1 - c) * hr, hr), :], dst_ref=dst, send_sem=self.send_sems.at[a],
                recv_sem=self.recv_sems.at[a], device_id=(x, y, 1 - c), device_id_type=MESH))
        return out

    def start(self):
        for cp in self._copies():
            cp.start()

    def finish(self):
        for cp in self._copies():
            cp.wait()


def _exchange_halves(grads):
    n = len(grads)

    def body(*refs):
        ex = _ExchangeHalves(refs[:n], refs[n:2 * n], *refs[2 * n:])
        ex.start()
        ex.finish()

    return pl.pallas_call(
        body, name="grad_exchange_halves", out_shape=_ExchangeHalves.out_shape(grads),
        in_specs=[ANY] * n, out_specs=[ANY] * n, scratch_shapes=_ExchangeHalves.scratch(n),
    )(*grads)


class _ScatterToChips:
    def __init__(self, ins, rbufs, send_sems, recv_sems):
        self.ins, self.rbufs, self.send_sems, self.recv_sems = ins, rbufs, send_sems, recv_sems

    @staticmethod
    def scratch(n):
        return [pltpu.SemaphoreType.DMA((3 * n,)), pltpu.SemaphoreType.DMA((3 * n,))]

    @staticmethod
    def out_shape(sums):
        return [jax.ShapeDtypeStruct((3,) + s.shape[1:], BF16) for s in sums]

    def _copies(self):
        x, y, c, chips = _mesh_pos()
        out = []
        for a, (src, dst) in enumerate(zip(self.ins, self.rbufs)):
            for k, chip in enumerate(chips):
                out.append(pltpu.make_async_remote_copy(
                    src_ref=src.at[2 * chip[0] + chip[1]], dst_ref=dst.at[k], send_sem=self.send_sems.at[3 * a + k],
                    recv_sem=self.recv_sems.at[3 * a + k], device_id=(*chip, c), device_id_type=MESH))
        return out

    def start(self):
        for cp in self._copies():
            cp.start()

    def finish(self):
        for cp in self._copies():
            cp.wait()


def _gather_halves(halves):
    n = len(halves)

    def body(*refs):
        ins, outs = refs[:n], refs[n:2 * n]
        send_sems, recv_sems = refs[2 * n:]
        x, y, c, _ = _mesh_pos()
        sib = (x, y, 1 - c)
        remote = [pltpu.make_async_remote_copy(src_ref=ins[a].at[c], dst_ref=outs[a].at[c],
                                               send_sem=send_sems.at[a], recv_sem=recv_sems.at[a],
                                               device_id=sib, device_id_type=MESH) for a in range(n)]
        for cp in remote:
            cp.start()
        for a in range(n):
            pltpu.make_async_remote_copy(src_ref=ins[a].at[1 - c], dst_ref=outs[a].at[1 - c], send_sem=send_sems.at[a],
                                         recv_sem=recv_sems.at[a], device_id=sib, device_id_type=MESH).wait_recv()
        for cp in remote:
            cp.wait_send()

    return pl.pallas_call(
        body, name="grad_gather_halves",
        out_shape=[jax.ShapeDtypeStruct(h.shape, F32) for h in halves],
        in_specs=[ANY] * n, out_specs=[ANY] * n, input_output_aliases={a: a for a in range(n)},
        scratch_shapes=[pltpu.SemaphoreType.DMA((n,)), pltpu.SemaphoreType.DMA((n,))],
    )(*halves)


SMALL_A_ROWS = 24
SMALL_B_ROWS = 8
SMALL_C_ROWS = N_POOL_GROUPS * POOL_GROUP


class _AllReduceSmall:
    N_IN = 10
    SHAPES = [(SMALL_A_ROWS, D_MODEL), (SMALL_B_ROWS, D_CONV), (SMALL_C_ROWS, POOL_GROUP)]

    def __init__(self, ins, outs, scratch):
        self.ins, self.outs = ins, outs
        self.bufs, self.rcvs, self.send_sems, self.recv_sems = scratch[:3], scratch[3:6], scratch[6], scratch[7]

    @classmethod
    def scratch(cls):
        return ([pltpu.VMEM((3,) + s, F32) for s in cls.SHAPES] + [pltpu.VMEM((3,) + s, F32) for s in cls.SHAPES]
                + [pltpu.SemaphoreType.DMA((9,)), pltpu.SemaphoreType.DMA((9,))])

    @classmethod
    def out_shape(cls):
        return [jax.ShapeDtypeStruct(s, F32) for s in cls.SHAPES]

    def _copies(self, st):
        x, y, c, _ = _mesh_pos()
        peer = [(x, y, 1 - c), (1 - x, y, c), (x, 1 - y, c)][st]
        return [pltpu.make_async_remote_copy(
            src_ref=buf.at[st], dst_ref=rcv.at[st], send_sem=self.send_sems.at[3 * st + i],
            recv_sem=self.recv_sems.at[3 * st + i], device_id=peer, device_id_type=MESH)
            for i, (buf, rcv) in enumerate(zip(self.bufs, self.rcvs))]

    def pack_and_send(self):
        dg1_ref, dg1m_ref, dg2_ref, dg3_ref, dg4_ref, loss_ref, dmeta_ref, dsc_ref, dcw_ref, dpw_ref = self.ins
        a_buf, b_buf, c_buf = self.bufs

        def rowsum(v):
            return jnp.sum(v, axis=0, keepdims=True)

        a_buf[0, 0:1, :] = rowsum(dg1_ref[...] + dg1m_ref[...])
        a_buf[0, 1:2, :] = rowsum(dg2_ref[...])
        a_buf[0, 2:3, :] = rowsum(dg3_ref[...])
        a_buf[0, 3:4, :] = rowsum(dg4_ref[...])
        loss = jnp.sum(rowsum(loss_ref[...]), axis=1, keepdims=True) * (0.5 / D_MODEL)
        a_buf[0, 4:5, :] = jnp.broadcast_to(loss, (1, D_MODEL))
        a_buf[0, 5:8, :] = jnp.zeros((3, D_MODEL), F32)
        a_buf[0, 8:24, :] = dmeta_ref[...]
        b_buf[0, 0:1, :] = rowsum(dsc_ref[...])
        for k in range(3):
            b_buf[0, 1 + k:2 + k, :] = rowsum(dcw_ref[8 * k:8 * k + 8, :])
        b_buf[0, 4:8, :] = jnp.zeros((4, D_CONV), F32)
        c_buf[0] = dpw_ref[...]
        for cp in self._copies(0):
            cp.start()

    def combine(self, st):
        for cp in self._copies(st):
            cp.wait()
        if st < 2:
            for buf, rcv in zip(self.bufs, self.rcvs):
                buf[st + 1] = buf[st] + rcv[st]
            for cp in self._copies(st + 1):
                cp.start()
        else:
            for out, buf, rcv in zip(self.outs, self.bufs, self.rcvs):
                out[...] = buf[st] + rcv[st]


def _row_block(rows):
    for cand in (512, 448, 384, 352, 320, 256, 128, 64, 32, 16):
        if rows % cand == 0:
            return cand
    return rows


def _add_pairs(grad, recv, place):
    n_sh, rows2, cols = grad.shape
    hr = rows2 // 2
    br = _row_block(hr)

    def body(place_ref, a_ref, b_ref, o_ref):
        o_ref[...] = (a_ref[0] + b_ref[...]).astype(BF16)

    return pl.pallas_call(
        body, name="grad_add_pairs",
        grid_spec=pltpu.PrefetchScalarGridSpec(
            num_scalar_prefetch=1, grid=(n_sh, hr // br),
            in_specs=[pl.BlockSpec((1, 1, br, cols), lambda j, i, p: (j, p[1], i, 0)),
                      pl.BlockSpec((1, br, cols), lambda j, i, p: (j, i, 0))],
            out_specs=pl.BlockSpec((1, br, cols), lambda j, i, p: (j, i, 0))),
        out_shape=jax.ShapeDtypeStruct((n_sh, hr, cols), BF16), compiler_params=_cparams(2),
    )(place, grad.reshape(n_sh, 2, hr, cols), recv)


def _add_chips(grads, recvs, rbufs, place, scattered=(), name="grad_add_chips"):
    n, n_sc = len(grads), len(scattered)
    n_sh, rows2, cols = grads[0].shape
    hr = rows2 // 2
    br = _row_block(hr)
    n_steps = hr // br

    def body(place_ref, *refs):
        a_refs, b_refs, r_refs = refs[:n], refs[n:2 * n], refs[2 * n:3 * n]
        o_refs = refs[3 * n + n_sc:4 * n + n_sc]
        if n_sc:
            scatter = _ScatterToChips(refs[3 * n:3 * n + n_sc], refs[4 * n + n_sc:4 * n + 2 * n_sc], *refs[-2:])

            @pl.when(pl.program_id(0) == 0)
            def _():
                scatter.start()

        for a_ref, b_ref, r_ref, o_ref in zip(a_refs, b_refs, r_refs, o_refs):
            own = a_ref[0, 0] + b_ref[0]
            o_ref[0] = ((own + r_ref[0].astype(F32)) + r_ref[1].astype(F32)) + r_ref[2].astype(F32)

        if n_sc:
            @pl.when(pl.program_id(0) == n_steps - 1)
            def _():
                scatter.finish()

    outs = pl.pallas_call(
        body, name=name,
        grid_spec=pltpu.PrefetchScalarGridSpec(
            num_scalar_prefetch=1, grid=(n_steps,),
            in_specs=[pl.BlockSpec((1, 1, br, cols), lambda i, p: (p[0], p[1], i, 0))] * n
            + [pl.BlockSpec((1, br, cols), lambda i, p: (p[0], i, 0))] * n
            + [pl.BlockSpec((3, br, cols), lambda i, p: (0, i, 0))] * n + [ANY] * n_sc,
            out_specs=[pl.BlockSpec((1, br, cols), lambda i, p: (p[1], i, 0))] * n + [ANY] * n_sc,
            scratch_shapes=_ScatterToChips.scratch(n_sc) if n_sc else []),
        out_shape=[jax.ShapeDtypeStruct((2, hr, cols), F32)] * n + _ScatterToChips.out_shape(list(scattered)),
        compiler_params=_cparams(1),
    )(place, *[g.reshape(n_sh, 2, hr, cols) for g in grads], *recvs, *rbufs, *scattered)
    return outs[:n], outs[n:]


def _adamw_math(w, g, m, v):
    m2 = ADAM_B1 * m + (1.0 - ADAM_B1) * g
    v2 = ADAM_B2 * v + (1.0 - ADAM_B2) * (g * g)
    m_hat = m2 / (1.0 - ADAM_B1 ** ADAM_STEP)
    v_hat = v2 / (1.0 - ADAM_B2 ** ADAM_STEP)
    delta = -ADAM_LR * (m_hat / (jnp.sqrt(v_hat) + ADAM_EPS) + ADAM_WD * w)
    return delta, m2, v2


def _adamw_big(w, g, m, v):
    rows, cols = w.shape
    br = _row_block(rows)

    def body(w_ref, g_ref, m_ref, v_ref, d_ref, m2_ref, v2_ref):
        d, m2, v2 = _adamw_math(w_ref[...], g_ref[...], m_ref[...], v_ref[...])
        d_ref[...] = d
        m2_ref[...] = m2
        v2_ref[...] = v2

    spec = pl.BlockSpec((br, cols), lambda i: (i, 0))
    return pl.pallas_call(
        body, name="adamw_big", grid=(rows // br,),
        out_shape=[jax.ShapeDtypeStruct((rows, cols), F32)] * 3,
        in_specs=[spec] * 4, out_specs=[spec] * 3, compiler_params=_cparams(1),
    )(w, g, m, v)


def _adamw_small(groups):
    n = len(groups)

    def body(*refs):
        ins, outs = refs[:4 * n], refs[4 * n:]
        for i in range(n):
            w, g, m, v = (r[...] for r in ins[4 * i:4 * i + 4])
            d, m2, v2 = _adamw_math(w, g, m, v)
            outs[3 * i][...] = d
            outs[3 * i + 1][...] = m2
            outs[3 * i + 2][...] = v2

    vm = pl.BlockSpec(memory_space=pltpu.VMEM)
    flat = [a for grp in groups for a in grp]
    out_shape = [jax.ShapeDtypeStruct(grp[0].shape, F32) for grp in groups for _ in range(3)]
    outs = pl.pallas_call(body, name="adamw_small", out_shape=out_shape,
                          in_specs=[vm] * (4 * n), out_specs=[vm] * (3 * n))(*flat)
    return [tuple(outs[3 * i:3 * i + 3]) for i in range(n)]


def _load_gathered(gathered, shards, dst_slots, sems):
    n = len(gathered)
    me = 2 * lax.axis_index("x") + lax.axis_index("y")

    def copies(j, own):
        return [pltpu.make_async_copy(shards[a] if own else gathered[a].at[j], dst_slots[a](j), sems.at[n * j + a])
                for a in range(n)]

    for wait in (False, True):
        for j in range(N_CHIPS):
            for own in (False, True):
                @pl.when((me == j) == own)
                def _():
                    for cp in copies(j, own):
                        cp.wait() if wait else cp.start()


N_MIX_SHARDS = 3


def _mixer_fwd(x3, g1, g2, poolw, pscale, shards):
    n_seq, seq, _ = x3.shape
    tm = min(TM_MIX_FWD, seq)
    n_t = seq // tm
    n_steps = n_seq * n_t
    n_ag = len(shards)
    n_ffn = n_ag - N_MIX_SHARDS
    small_rows = shards[2].shape[0]
    conv_cols = D_CONV // N_CHIPS

    def body(x_ref, g1_ref, g2_ref, pw_ref, ps_ref, *rest):
        ag = _AllGather(rest[:n_ag], rest[n_ag + 10:2 * n_ag + 10], *rest[-2:])
        (z_ref, m_ref, h1_ref, a_ref, conv_ref, pooled_ref, yc_ref, zm_ref, meta_ref,
         cw_ref) = rest[n_ag:n_ag + 10]
        win_v, wout_v, small_v, cvb, pb, load_sems = rest[2 * n_ag + 10:-2]
        s, t = pl.program_id(0), pl.program_id(1)
        step = s * n_t + t

        @pl.when(step == 0)
        def _():
            ag.start(range(N_MIX_SHARDS))
            for a in range(N_MIX_SHARDS):
                ag.relay(a)
            for a in range(N_MIX_SHARDS):
                ag.forward(a)
            ag.finish(range(N_MIX_SHARDS))
            ag.start(range(N_MIX_SHARDS, n_ag))
            _load_gathered(ag.outs[:N_MIX_SHARDS], ag.ins[:N_MIX_SHARDS],
                           [lambda j: win_v.at[j], lambda j: wout_v.at[pl.ds(j * OUT_SHARD, OUT_SHARD), :],
                            lambda j: small_v.at[j]], load_sems)

            meta = jnp.concatenate([small_v[j, 0:N_META, :] for j in range(N_CHIPS)], axis=1)
            meta_ref[...] = meta
            cw_ref[...] = jnp.concatenate([small_v[j, N_META:N_META + 3, 0:conv_cols] for j in range(N_CHIPS)], axis=1)
            a_meta = (meta * _rstd(meta) * g1_ref[...]).astype(BF16)
            for j in range(N_CHIPS):
                zm_ref[:, j * IN_SHARD:(j + 1) * IN_SHARD] = _dot(a_meta, win_v[j])

        for i in range(n_ffn):
            @pl.when(step == ((i + 1) * n_steps) // (2 * n_ffn + 2))
            def _():
                ag.relay(N_MIX_SHARDS + i)

        for i in range(n_ffn):
            @pl.when(step == min(n_steps // 2 + ((i + 1) * n_steps) // (2 * n_ffn + 2), n_steps - 1))
            def _():
                ag.forward(N_MIX_SHARDS + i)

        @pl.when(t == 0)
        def _():
            cvb[0:HALO, :] = zm_ref[:, IN_SHARD:2 * IN_SHARD] * zm_ref[:, 2 * IN_SHARD:3 * IN_SHARD]
            pb[0:HALO, :] = zm_ref[:, 3 * IN_SHARD:4 * IN_SHARD]

        @pl.when(t > 0)
        def _():
            cvb[0:HALO, :] = cvb[tm:tm + HALO, :]
            pb[0:HALO, :] = pb[tm:tm + HALO, :]

        xt = x_ref[0]
        a = (xt * _rstd(xt) * g1_ref[...]).astype(BF16)
        a_ref[...] = a
        zb = _dot(a, win_v[0])
        zc = _dot(a, win_v[1])
        zv = _dot(a, win_v[2])
        zp = _dot(a, win_v[3])
        z_ref[0, :, 0:IN_SHARD] = zb
        z_ref[0, :, IN_SHARD:2 * IN_SHARD] = zc
        z_ref[0, :, 2 * IN_SHARD:3 * IN_SHARD] = zv
        cv = zc * zv
        cvb[HALO:HALO + tm, :] = cv
        pb[HALO:HALO + tm, :] = zp
        cw = cw_ref[...]
        conv = cw[0:1] * cvb[HALO - 2:HALO - 2 + tm, :] + cw[1:2] * cvb[HALO - 1:HALO - 1 + tm, :] + cw[2:3] * cv
        conv_ref[...] = conv
        parts = [(zb * conv).astype(BF16)]
        for g in range(N_POOL_GROUPS):
            pooled = _pool_fwd(pb, g, tm).astype(BF16)
            pooled_ref[:, _gcols(g)] = pooled
            parts.append((_dot(pooled, pw_ref[g]) * ps_ref[:, _gcols(g)]).astype(BF16))
        ycat = jnp.concatenate(parts, axis=1)
        yc_ref[...] = ycat
        m = _dot(ycat, wout_v[...])
        m_ref[0] = m
        h1_ref[0] = xt + m * _rstd(m) * g2_ref[...]

        @pl.when(step == n_steps - 1)
        def _():
            ag.finish(range(N_MIX_SHARDS, n_ag))

    n_rows = n_seq * seq
    row = lambda c: pl.BlockSpec((1, tm, c), lambda s, t: (s, t, 0))
    row2 = lambda c: pl.BlockSpec((tm, c), lambda s, t: (s * n_t + t, 0))
    outs = pl.pallas_call(
        body, name="mixer_fwd", grid=(n_seq, n_t),
        out_shape=[jax.ShapeDtypeStruct((n_seq, seq, D_Z), F32), jax.ShapeDtypeStruct((n_seq, seq, D_MODEL), F32),
                   jax.ShapeDtypeStruct((n_seq, seq, D_MODEL), F32), jax.ShapeDtypeStruct((n_rows, D_MODEL), BF16),
                   jax.ShapeDtypeStruct((n_rows, D_CONV), F32), jax.ShapeDtypeStruct((n_rows, D_POOL), BF16),
                   jax.ShapeDtypeStruct((n_rows, D_MODEL), BF16), jax.ShapeDtypeStruct((N_META, D_IN_PROJ), F32),
                   jax.ShapeDtypeStruct((N_META, D_MODEL), F32), jax.ShapeDtypeStruct((3, D_CONV), F32)]
        + _AllGather.out_shape(shards),
        in_specs=[row(D_MODEL), _full((1, D_MODEL)), _full((1, D_MODEL)),
                  _full((N_POOL_GROUPS, POOL_GROUP, POOL_GROUP)), _full((1, D_POOL))] + [ANY] * n_ag,
        out_specs=[row(D_Z), row(D_MODEL), row(D_MODEL), row2(D_MODEL), row2(D_CONV), row2(D_POOL), row2(D_MODEL),
                   _full((N_META, D_IN_PROJ)), _full((N_META, D_MODEL)), _full((3, D_CONV))] + [ANY] * n_ag,
        scratch_shapes=[pltpu.VMEM((N_CHIPS, D_MODEL, IN_SHARD), BF16), pltpu.VMEM((D_MODEL, D_MODEL), BF16),
                        pltpu.VMEM((N_CHIPS, small_rows, D_MODEL // N_CHIPS), F32),
                        pltpu.VMEM((HALO + tm, D_CONV), F32), pltpu.VMEM((HALO + tm, D_POOL), F32),
                        pltpu.SemaphoreType.DMA((N_MIX_SHARDS * N_CHIPS,))] + _AllGather.scratch(n_ag),
        compiler_params=_cparams(2),
    )(x3, g1, g2, poolw, pscale, *shards)
    return outs[:10], outs[10:]


def _ffn_chunks():
    out, r0 = [], 0
    while r0 < D_FF:
        out.append((r0, min(FF_CHUNK, D_FF - r0)))
        r0 += FF_CHUNK
    return out


def _ffn_fwd_bwd(h1, target, g3, g4, gathered, shards):
    n_rows = h1.shape[0]
    tm = min(TM_FFN, n_rows)
    chunks = _ffn_chunks()

    def body(h1_ref, t_ref, g3_ref, g4_ref, wg_all, wu_all, wd_all, wg_s, wu_s, wd_s,
             dh1_ref, f_ref, dd_ref, ds_ref, du_ref, gg_ref, loss_ref, dg3_ref, dg4_ref,
             wg_v, wu_v, wd_v, s_sc, u_sc, sems):
        @pl.when(pl.program_id(0) == 0)
        def _():
            _load_gathered([wg_all, wu_all, wd_all], [wg_s, wu_s, wd_s],
                           [functools.partial(lambda v, j: v.at[pl.ds(j * FF_SHARD, FF_SHARD), :], v)
                            for v in (wg_v, wu_v, wd_v)], sems)
            loss_ref[...] = jnp.zeros_like(loss_ref)
            dg3_ref[...] = jnp.zeros_like(dg3_ref)
            dg4_ref[...] = jnp.zeros_like(dg4_ref)

        h1v = h1_ref[...]
        r3 = _rstd(h1v)
        hh = h1v * r3
        g3v, g4v = g3_ref[...], g4_ref[...]
        f = (hh * g3v).astype(BF16)
        f_ref[...] = f
        d = jnp.zeros((tm, D_MODEL), F32)
        for r0, sz in chunks:
            s = _dot_nt(f, wg_v[r0:r0 + sz, :])
            u = _dot_nt(f, wu_v[r0:r0 + sz, :])
            s_sc[:, r0:r0 + sz] = s
            u_sc[:, r0:r0 + sz] = u
            gc = (s * _sigmoid(s) * u).astype(BF16)
            gg_ref[:, r0:r0 + sz] = gc
            d = d + _dot(gc, wd_v[r0:r0 + sz, :])
        r4 = _rstd(d)
        dh = d * r4
        err = (h1v + dh * g4v) - t_ref[...]
        loss_ref[...] += _rows8(err * err)
        dy = err * (1.0 / D_MODEL)
        dg4_ref[...] += _rows8(dy * dh)
        ddb = _rms_bwd(dy, dh, r4, g4v).astype(BF16)
        dd_ref[...] = ddb
        df = jnp.zeros((tm, D_MODEL), F32)
        for r0, sz in chunks:
            dgg = _dot_nt(ddb, wd_v[r0:r0 + sz, :])
            s = s_sc[:, r0:r0 + sz]
            u = u_sc[:, r0:r0 + sz]
            sig = _sigmoid(s)
            dsc = (dgg * u * (sig * (1.0 + s * (1.0 - sig)))).astype(BF16)
            duc = (dgg * (s * sig)).astype(BF16)
            ds_ref[:, r0:r0 + sz] = dsc
            du_ref[:, r0:r0 + sz] = duc
            df = df + _dot(dsc, wg_v[r0:r0 + sz, :]) + _dot(duc, wu_v[r0:r0 + sz, :])
        dg3_ref[...] += _rows8(df * hh)
        dh1_ref[...] = dy + _rms_bwd(df, hh, r3, g3v)

    row = pl.BlockSpec((tm, D_MODEL), lambda i: (i, 0))
    ffrow = pl.BlockSpec((tm, D_FF), lambda i: (i, 0))
    acc = _full((8, D_MODEL))
    act_bf = jax.ShapeDtypeStruct((n_rows, D_MODEL), BF16)
    ff_bf = jax.ShapeDtypeStruct((n_rows, D_FF), BF16)
    acc_shape = jax.ShapeDtypeStruct((8, D_MODEL), F32)
    w_vmem = pltpu.VMEM((D_FF, D_MODEL), BF16)
    return pl.pallas_call(
        body, name="ffn_fwd_bwd", grid=(n_rows // tm,),
        out_shape=[jax.ShapeDtypeStruct((n_rows, D_MODEL), F32), act_bf, act_bf, ff_bf, ff_bf, ff_bf,
                   acc_shape, acc_shape, acc_shape],
        in_specs=[row, row, _full((1, D_MODEL)), _full((1, D_MODEL))] + [ANY] * 6,
        out_specs=[row, row, row, ffrow, ffrow, ffrow, acc, acc, acc],
        scratch_shapes=[w_vmem, w_vmem, w_vmem, pltpu.VMEM((tm, D_FF), F32), pltpu.VMEM((tm, D_FF), F32),
                        pltpu.SemaphoreType.DMA((3 * N_CHIPS,))],
        compiler_params=_cparams(1),
    )(h1, target, g3, g4, *gathered, *shards)


def _ffn_weight_grads(name, acts, other, exchanged):
    n_rows = other.shape[0]
    n_a, n_ex = len(acts), len(exchanged)
    n_c = n_a
    tk = min(TK_DW, n_rows)
    n_k = n_rows // tk
    half = D_FF // n_c

    def body(other_ref, *rest):
        act_refs = rest[:n_a]
        out_refs = rest[n_a + n_ex:2 * n_a + n_ex]
        c, k = pl.program_id(0), pl.program_id(1)
        if n_ex:
            ex = _ExchangeHalves(rest[n_a:n_a + n_ex], rest[2 * n_a + n_ex:2 * n_a + 2 * n_ex], *rest[-2:])

            @pl.when((c == 0) & (k == 0))
            def _():
                ex.start()

        @pl.when(k == 0)
        def _():
            for o in out_refs:
                o[...] = jnp.zeros_like(o)

        ov = other_ref[...]
        for a, o in zip(act_refs, out_refs):
            o[...] += _dot_tn(a[...], ov)

        if n_ex:
            @pl.when((c == n_c - 1) & (k == n_k - 1))
            def _():
                ex.finish()

    row = pl.BlockSpec((tk, D_MODEL), lambda c, k: (k, 0))
    ffrow = pl.BlockSpec((tk, half), lambda c, k: (k, c))
    out = pl.BlockSpec((half, D_MODEL), lambda c, k: (c, 0))
    outs = pl.pallas_call(
        body, name=name, grid=(n_c, n_k),
        out_shape=[jax.ShapeDtypeStruct((D_FF, D_MODEL), F32)] * n_a + _ExchangeHalves.out_shape(exchanged),
        in_specs=[row] + [ffrow] * n_a + [ANY] * n_ex, out_specs=[out] * n_a + [ANY] * n_ex,
        scratch_shapes=_ExchangeHalves.scratch(n_ex) if n_ex else [],
        compiler_params=_cparams(2),
    )(other, *acts, *exchanged)
    return outs[:n_a], outs[n_a:]


def _mixer_bwd(dh1, m3, z3, conv2, pooled2, x3, zmeta, meta_full, g1, g2, convw, poolw, pscale, gathered, shards,
               exchanged, scattered):
    n_seq, seq, _ = x3.shape
    tm = min(TM_MIX_BWD, seq)
    sub = min(SUB_MIX_BWD, tm)
    n_t = seq // tm
    n_ex, n_sc = len(exchanged), len(scattered)
    n_cm = n_ex + n_sc
    n_out = 13

    def body(dh1_ref, m_ref, z_ref, conv_ref, pooled_ref, x_ref, zm_ref, meta_ref, g1_ref, g2_ref, cw_ref, pw_ref,
             ps_ref, win_all, wout_all, win_s, wout_s, *rest):
        outs0 = n_cm + n_out
        ex = _ExchangeHalves(rest[:n_ex], rest[outs0:outs0 + n_ex], *rest[-4:-2])
        sc = _ScatterToChips(rest[n_ex:n_cm], rest[outs0 + n_ex:outs0 + n_cm], *rest[-2:])
        (dx_ref, dz_ref, dm_ref, dg1_ref, dg2_ref, dsc_ref, dcw_ref, dpw_ref, dzm_ref, dmeta_ref, dg1m_ref, am_ref,
         dzmb_ref) = rest[n_cm:outs0]
        win_v, wout_v, dcb, dqb, mcb, mqb, load_sems = rest[outs0 + n_cm:-4]
        s, i = pl.program_id(0), pl.program_id(1)
        tr = n_t - 1 - i

        @pl.when((s == 0) & (i == 0))
        def _():
            sc.start()
            ex.start()
            _load_gathered([win_all, wout_all], [win_s, wout_s],
                           [lambda j: win_v.at[j], lambda j: wout_v.at[pl.ds(j * OUT_SHARD, OUT_SHARD), :]], load_sems)
            for ref in (dg1_ref, dg2_ref, dsc_ref, dcw_ref, dpw_ref, dzm_ref):
                ref[...] = jnp.zeros_like(ref)

        @pl.when(i == 0)
        def _():
            dcb[tm:tm + HALO, :] = jnp.zeros((HALO, D_CONV), F32)
            dqb[tm:tm + HALO, :] = jnp.zeros((HALO, D_POOL), F32)

        @pl.when(i > 0)
        def _():
            dcb[tm:tm + HALO, :] = dcb[0:HALO, :]
            dqb[tm:tm + HALO, :] = dqb[0:HALO, :]

        g1v, g2v = g1_ref[...], g2_ref[...]
        cw = cw_ref[...]

        for r0 in range(tm - sub, -1, -sub):
            rows = slice(r0, r0 + sub)
            dh1v = dh1_ref[0, rows, :]
            mv = m_ref[0, rows, :]
            r2 = _rstd(mv)
            mh = mv * r2
            dg2_ref[...] += _rows8(dh1v * mh)
            dmb = _rms_bwd(dh1v, mh, r2, g2v).astype(BF16)
            dm_ref[rows, :] = dmb
            dyc = _dot_nt(dmb, wout_v[...])
            dyconv = dyc[:, 0:D_CONV]

            for g in range(N_POOL_GROUPS):
                pooled = pooled_ref[rows, _gcols(g)]
                mixed = _dot(pooled, pw_ref[g])
                scale = ps_ref[:, _gcols(g)]
                dyp = dyc[:, D_CONV + g * POOL_GROUP:D_CONV + (g + 1) * POOL_GROUP]
                dsc_ref[:, _gcols(g)] += _rows8(dyp * mixed)
                dmix = (dyp * scale).astype(BF16)
                dpw_ref[g] += _dot_tn(pooled, dmix)
                dqb[rows, _gcols(g)] = _dot_nt(dmix, pw_ref[g])

            zb = z_ref[0, rows, 0:IN_SHARD]
            zc = z_ref[0, rows, IN_SHARD:2 * IN_SHARD]
            zv = z_ref[0, rows, 2 * IN_SHARD:3 * IN_SHARD]
            dconv = dyconv * zb
            dcb[rows, :] = dconv
            d1 = dcb[r0 + 1:r0 + 1 + sub, :]
            d2 = dcb[r0 + 2:r0 + 2 + sub, :]
            dcv = cw[2:3] * dconv + cw[1:2] * d1 + cw[0:1] * d2
            cv = zc * zv
            dcw_ref[0:8, :] += _rows8(cv * d2)
            dcw_ref[8:16, :] += _rows8(cv * d1)
            dcw_ref[16:24, :] += _rows8(cv * dconv)
            dzs = [(dyconv * conv_ref[rows, :]).astype(BF16), (dcv * zv).astype(BF16), (dcv * zc).astype(BF16),
                   jnp.concatenate([_pool_bwd(dqb, g, r0, sub) for g in range(N_POOL_GROUPS)], axis=1).astype(BF16)]
            da = jnp.zeros((sub, D_MODEL), F32)
            for j in range(N_CHIPS):
                dz_ref[j, rows, :] = dzs[j]
                da = da + _dot_nt(dzs[j], win_v[j])
            xt = x_ref[0, rows, :]
            r1 = _rstd(xt)
            xh = xt * r1
            dg1_ref[...] += _rows8(da * xh)
            dx_ref[0, rows, :] = dh1v + _rms_bwd(da, xh, r1, g1v)

        @pl.when(tr == 0)
        def _():
            mcb[0:HALO, :] = jnp.zeros((HALO, D_CONV), F32)
            mqb[0:HALO, :] = jnp.zeros((HALO, D_POOL), F32)
            mcb[HALO:2 * HALO, :] = dcb[0:HALO, :]
            mqb[HALO:2 * HALO, :] = dqb[0:HALO, :]
            m1 = mcb[1:1 + HALO, :]
            m2 = mcb[2:2 + HALO, :]
            zc_m = zm_ref[:, IN_SHARD:2 * IN_SHARD]
            zv_m = zm_ref[:, 2 * IN_SHARD:3 * IN_SHARD]
            cv_m = zc_m * zv_m
            dcw_ref[0:8, :] += _rows8(cv_m * m2)
            dcw_ref[8:16, :] += _rows8(cv_m * m1)
            dcv_m = cw[1:2] * m1 + cw[0:1] * m2
            dzm_ref[:, IN_SHARD:2 * IN_SHARD] += dcv_m * zv_m
            dzm_ref[:, 2 * IN_SHARD:3 * IN_SHARD] += dcv_m * zc_m
            dzm_ref[:, 3 * IN_SHARD:4 * IN_SHARD] += jnp.concatenate(
                [_pool_bwd(mqb, g, 0, HALO) for g in range(N_POOL_GROUPS)], axis=1)

        @pl.when((s == n_seq - 1) & (i == n_t - 1))
        def _():
            xm = meta_ref[...]
            rm = _rstd(xm)
            xmh = xm * rm
            am_ref[...] = (xmh * g1v).astype(BF16)
            da_m = jnp.zeros((N_META, D_MODEL), F32)
            for j in range(N_CHIPS):
                dzj = dzm_ref[:, j * IN_SHARD:(j + 1) * IN_SHARD].astype(BF16)
                dzmb_ref[j] = dzj
                da_m = da_m + _dot_nt(dzj, win_v[j])
            dg1m_ref[...] = _rows8(da_m * xmh)
            dmeta_ref[...] = _rms_bwd(da_m, xmh, rm, g1v)
            ex.finish()
            sc.finish()

    row3 = lambda c: pl.BlockSpec((1, tm, c), lambda s, i: (s, n_t - 1 - i, 0))
    row2 = lambda c: pl.BlockSpec((tm, c), lambda s, i: (s * n_t + n_t - 1 - i, 0))
    n_rows = n_seq * seq
    outs = pl.pallas_call(
        body, name="mixer_bwd", grid=(n_seq, n_t),
        out_shape=[jax.ShapeDtypeStruct((n_seq, seq, D_MODEL), F32),
                   jax.ShapeDtypeStruct((N_CHIPS, n_rows, IN_SHARD), BF16), jax.ShapeDtypeStruct((n_rows, D_MODEL), BF16),
                   jax.ShapeDtypeStruct((8, D_MODEL), F32), jax.ShapeDtypeStruct((8, D_MODEL), F32),
                   jax.ShapeDtypeStruct((8, D_POOL), F32), jax.ShapeDtypeStruct((24, D_CONV), F32),
                   jax.ShapeDtypeStruct((N_POOL_GROUPS, POOL_GROUP, POOL_GROUP), F32),
                   jax.ShapeDtypeStruct((N_META, D_IN_PROJ), F32),
                   jax.ShapeDtypeStruct((N_META, D_MODEL), F32), jax.ShapeDtypeStruct((8, D_MODEL), F32),
                   jax.ShapeDtypeStruct((N_META, D_MODEL), BF16),
                   jax.ShapeDtypeStruct((N_CHIPS, N_META, IN_SHARD), BF16)]
        + _ExchangeHalves.out_shape(exchanged) + _ScatterToChips.out_shape(scattered),
        in_specs=[row3(D_MODEL), row3(D_MODEL), row3(D_Z), row2(D_CONV), row2(D_POOL), row3(D_MODEL),
                  _full((N_META, D_IN_PROJ)), _full((N_META, D_MODEL)), _full((1, D_MODEL)), _full((1, D_MODEL)),
                  _full((3, D_CONV)), _full((N_POOL_GROUPS, POOL_GROUP, POOL_GROUP)), _full((1, D_POOL))]
        + [ANY] * (4 + n_cm),
        out_specs=[row3(D_MODEL), pl.BlockSpec((N_CHIPS, tm, IN_SHARD), lambda s, i: (0, s * n_t + n_t - 1 - i, 0)),
                   row2(D_MODEL),
                   _full((8, D_MODEL)), _full((8, D_MODEL)), _full((8, D_POOL)), _full((24, D_CONV)),
                   _full((N_POOL_GROUPS, POOL_GROUP, POOL_GROUP)), _full((N_META, D_IN_PROJ)),
                   _full((N_META, D_MODEL)), _full((8, D_MODEL)), _full((N_META, D_MODEL)),
                   _full((N_CHIPS, N_META, IN_SHARD))] + [ANY] * n_cm,
        scratch_shapes=[pltpu.VMEM((N_CHIPS, D_MODEL, IN_SHARD), BF16), pltpu.VMEM((D_MODEL, D_MODEL), BF16),
                        pltpu.VMEM((tm + HALO, D_CONV), F32), pltpu.VMEM((tm + HALO, D_POOL), F32),
                        pltpu.VMEM((2 * HALO, D_CONV), F32), pltpu.VMEM((2 * HALO, D_POOL), F32),
                        pltpu.SemaphoreType.DMA((2 * N_CHIPS,))]
        + _ExchangeHalves.scratch(n_ex) + _ScatterToChips.scratch(n_sc),
        compiler_params=_cparams(2),
    )(dh1, m3, z3, conv2, pooled2, x3, zmeta, meta_full, g1, g2, convw, poolw, pscale, *gathered, *shards,
      *exchanged, *scattered)
    return outs[:n_out], outs[n_out:n_out + n_ex], outs[n_out + n_ex:]


def _mixer_weight_grads(a, dz, ycat, dm, a_meta, dz_meta, ffn_sums, small):
    n_rows = a.shape[0]
    tk = min(TK_DW, n_rows)
    n_k = n_rows // tk
    n_sc, n_sm = len(ffn_sums), _AllReduceSmall.N_IN

    def body(a_ref, dz_ref, yc_ref, dm_ref, am_ref, dzm_ref, *rest):
        ins, outs, scratch = rest[:n_sc + n_sm], rest[n_sc + n_sm:2 * n_sc + n_sm + 5], rest[2 * n_sc + n_sm + 5:]
        dwin_ref, dwout_ref = outs[:2]
        scatter = _ScatterToChips(ins[:n_sc], outs[2:2 + n_sc], *scratch[:2])
        reduce_small = _AllReduceSmall(ins[n_sc:], outs[2 + n_sc:], scratch[2:])
        k = pl.program_id(0)

        @pl.when(k == 0)
        def _():
            scatter.start()
            reduce_small.pack_and_send()
            am_t = am_ref[...].T
            for j in range(N_CHIPS):
                dwin_ref[j] = _dot(am_t, dzm_ref[j])
            dwout_ref[...] = jnp.zeros_like(dwout_ref)

        for st in range(2):
            @pl.when(k == ((st + 1) * n_k) // 3)
            def _():
                reduce_small.combine(st)

        a_t = a_ref[...].T
        for j in range(N_CHIPS):
            dwin_ref[j] += _dot(a_t, dz_ref[j])
        dwout_ref[...] += _dot_tn(yc_ref[...], dm_ref[...])

        @pl.when(k == n_k - 1)
        def _():
            reduce_small.combine(2)
            scatter.finish()

    row = pl.BlockSpec((tk, D_MODEL), lambda k: (k, 0))
    outs = pl.pallas_call(
        body, name="mixer_weight_grads", grid=(n_k,),
        out_shape=[jax.ShapeDtypeStruct((N_CHIPS, D_MODEL, IN_SHARD), F32),
                   jax.ShapeDtypeStruct((D_MODEL, D_MODEL), F32)] + _ScatterToChips.out_shape(ffn_sums)
        + _AllReduceSmall.out_shape(),
        in_specs=[row, pl.BlockSpec((N_CHIPS, tk, IN_SHARD), lambda k: (0, k, 0)), row, row,
                  _full((N_META, D_MODEL)), _full((N_CHIPS, N_META, IN_SHARD))] + [ANY] * n_sc
        + [_full(s.shape) for s in small],
        out_specs=[_full((N_CHIPS, D_MODEL, IN_SHARD)), _full((D_MODEL, D_MODEL))] + [ANY] * n_sc
        + [_full(s) for s in _AllReduceSmall.SHAPES],
        scratch_shapes=_ScatterToChips.scratch(n_sc) + _AllReduceSmall.scratch(),
        compiler_params=_cparams(1),
    )(a, dz, ycat, dm, a_meta, dz_meta, *ffn_sums, *small)
    return ([outs[0], outs[1].reshape(N_CHIPS, OUT_SHARD, D_MODEL)], outs[2:2 + n_sc], outs[2 + n_sc:])


def kernel(x, meta_tokens, norm_mix_pre, w_in, conv_w, pool_w, pool_scale, w_out, norm_mix_post, norm_ffn_pre, w_gate, w_up, w_down, norm_ffn_post, loss_target, m_meta_tokens, m_norm_mix_pre, m_w_in, m_conv_w, m_pool_w, m_pool_scale, m_w_out, m_norm_mix_post, m_norm_ffn_pre, m_w_gate, m_w_up, m_w_down, m_norm_ffn_post, v_meta_tokens, v_norm_mix_pre, v_w_in, v_conv_w, v_pool_w, v_pool_scale, v_w_out, v_norm_mix_post, v_norm_ffn_pre, v_w_gate, v_w_up, v_w_down, v_norm_ffn_post):
    n_seq, seq, _ = x.shape
    n_rows = n_seq * seq
    chip = 2 * lax.axis_index("x") + lax.axis_index("y")
    meta_cols = D_MODEL // N_CHIPS
    conv_cols = D_CONV // N_CHIPS

    small = jnp.zeros((2 * HALO, meta_cols), F32)
    small = small.at[0:N_META, :].set(meta_tokens).at[N_META:N_META + 3, 0:conv_cols].set(conv_w[0])
    poolw_bf = pool_w[0].astype(BF16)
    pscale = pool_scale
    g1, g2, g3, g4 = norm_mix_pre, norm_mix_post, norm_ffn_pre, norm_ffn_post
    place = jnp.stack([chip, lax.axis_index("c")]).astype(jnp.int32)

    mix_shards = [w_in[0].astype(BF16), w_out[0].astype(BF16)]
    ffn_shards = [w_gate[0].T.astype(BF16), w_up[0].T.astype(BF16), w_down[0].astype(BF16)]
    ((z3, m3, h1, a_bf, conv2, pooled2, yc_bf, zmeta, meta_full, conv_full),
     (win_all, wout_all, _, *ffn_gathered)) = _mixer_fwd(x, g1, g2, poolw_bf, pscale, mix_shards + [small] + ffn_shards)
    dh1, f_bf, dd_bf, ds_bf, du_bf, gg_bf, lossp, dg3p, dg4p = _ffn_fwd_bwd(
        h1.reshape(n_rows, D_MODEL), loss_target.reshape(n_rows, D_MODEL), g3, g4, ffn_gathered, ffn_shards)
    as_shards = lambda g: g.reshape(N_CHIPS, FF_SHARD, D_MODEL)
    (dwg_t, dwu_t), _ = _ffn_weight_grads("ffn_weight_grads_gate_up", [ds_bf, du_bf], f_bf, [])
    dwg_t, dwu_t = as_shards(dwg_t), as_shards(dwu_t)
    (dwd,), (dwg_recv, dwu_recv) = _ffn_weight_grads("ffn_weight_grads_down", [gg_bf], dd_bf, [dwg_t, dwu_t])
    dwd = as_shards(dwd)
    ((grad_x, dz_bf, dm_bf, dg1p, dg2p, dscp, dcwp, dpw, _, dmeta, dg1m, a_meta, dz_meta), (dwd_recv,),
     (dwg_rbuf, dwu_rbuf)) = _mixer_bwd(
        dh1.reshape(n_seq, seq, D_MODEL), m3, z3, conv2, pooled2, x, zmeta, meta_full, g1, g2, conv_full, poolw_bf,
        pscale, [win_all, wout_all], mix_shards, [dwd],
        [_add_pairs(dwg_t, dwg_recv, place), _add_pairs(dwu_t, dwu_recv, place)])
    mix_grads, (dwd_rbuf,), (a_red, b_red, c_red) = _mixer_weight_grads(
        a_bf, dz_bf, yc_bf, dm_bf, a_meta, dz_meta, [_add_pairs(dwd, dwd_recv, place)],
        [dg1p, dg1m, dg2p, dg3p, dg4p, lossp, dmeta, dscp, dcwp, dpw.reshape(SMALL_C_ROWS, POOL_GROUP)])

    mix_recvs = _exchange_halves(mix_grads)
    ffn_red, mix_rbufs = _add_chips([dwg_t, dwu_t, dwd], [dwg_recv, dwu_recv, dwd_recv],
                                    [dwg_rbuf, dwu_rbuf, dwd_rbuf], place,
                                    [_add_pairs(g, r, place) for g, r in zip(mix_grads, mix_recvs)],
                                    name="grad_add_chips_ffn")
    mix_red = [_add_chips([g], [r], [rb], place)[0][0] for g, r, rb in zip(mix_grads, mix_recvs, mix_rbufs)]
    reduced = _gather_halves(mix_red + list(ffn_red))
    g_win, g_wout, g_wg_t, g_wu_t, g_wd = [r.reshape(2 * r.shape[1], r.shape[2]) for r in reduced]

    loss = a_red[4, 0]
    g_g1, g_g2, g_g3, g_g4 = a_red[0:1], a_red[1:2], a_red[2:3], a_red[3:4]
    g_meta = lax.dynamic_slice(a_red, (8, chip * meta_cols), (N_META, meta_cols))
    g_pscale = b_red[0:1]
    g_conv = lax.dynamic_slice(b_red, (1, chip * conv_cols), (3, conv_cols))
    g_poolw = c_red

    big = [(w_in[0], g_win, m_w_in[0], v_w_in[0]), (w_out[0], g_wout, m_w_out[0], v_w_out[0]),
           (w_gate[0].T, g_wg_t, m_w_gate[0].T, v_w_gate[0].T), (w_up[0].T, g_wu_t, m_w_up[0].T, v_w_up[0].T),
           (w_down[0], g_wd, m_w_down[0], v_w_down[0])]
    big_out = [_adamw_big(w, g, m, v) for (w, g, m, v) in big]
    big_out[2] = [o.T for o in big_out[2]]
    big_out[3] = [o.T for o in big_out[3]]
    g_wg, g_wu = g_wg_t.T, g_wu_t.T
    small_groups = [
        (meta_tokens, g_meta, m_meta_tokens, v_meta_tokens),
        (g1, g_g1, m_norm_mix_pre, v_norm_mix_pre),
        (conv_w[0], g_conv, m_conv_w[0], v_conv_w[0]),
        (pool_w.reshape(SMALL_C_ROWS, POOL_GROUP), g_poolw, m_pool_w.reshape(SMALL_C_ROWS, POOL_GROUP),
         v_pool_w.reshape(SMALL_C_ROWS, POOL_GROUP)),
        (pool_scale, g_pscale, m_pool_scale, v_pool_scale),
        (g2, g_g2, m_norm_mix_post, v_norm_mix_post),
        (g3, g_g3, m_norm_ffn_pre, v_norm_ffn_pre),
        (g4, g_g4, m_norm_ffn_post, v_norm_ffn_post),
    ]
    small_out = _adamw_small(small_groups)

    grads_out = [g_meta, g_g1, g_win[None], g_conv[None], g_poolw.reshape(pool_w.shape), g_pscale, g_wout[None],
                 g_g2, g_g3, g_wg[None], g_wu[None], g_wd[None], g_g4]
    s_meta, s_g1, s_conv, s_poolw, s_pscale, s_g2, s_g3, s_g4 = small_out
    b_win, b_wout, b_wg, b_wu, b_wd = big_out

    def leaf(k):
        return [s_meta[k], s_g1[k], b_win[k][None], s_conv[k][None], s_poolw[k].reshape(pool_w.shape), s_pscale[k],
                b_wout[k][None], s_g2[k], s_g3[k], b_wg[k][None], b_wu[k][None], b_wd[k][None], s_g4[k]]

    return (loss, grad_x, *grads_out, *leaf(0), *leaf(1), *leaf(2))
```

```python
import functools

import jax
import jax.numpy as jnp
from jax import lax
from jax.experimental import pallas as pl
from jax.experimental.pallas import tpu as pltpu

F32 = jnp.float32
BF16 = jnp.bfloat16
MESH = pl.DeviceIdType.MESH

D_MODEL = 1024
D_CONV = 512
D_POOL = 512
POOL_GROUP = 128
N_POOL_GROUPS = 4
D_IN_PROJ = 2048
D_FF = 2816
N_CHIPS = 4
FF_SHARD = D_FF // N_CHIPS
IN_SHARD = D_IN_PROJ // N_CHIPS
OUT_SHARD = D_MODEL // N_CHIPS
D_Z = 3 * IN_SHARD
N_META = 16
HALO = 16
RMS_EPS = 1e-6

ADAM_LR = 0.001
ADAM_B1 = 0.9
ADAM_B2 = 0.999
ADAM_EPS = 1e-08
ADAM_WD = 0.01
ADAM_STEP = 10

TM_MIX_FWD = 512
TM_MIX_BWD = 512
SUB_MIX_BWD = 512
TM_FFN = 256
TK_DW = 1024
FF_CHUNK = 1024
VMEM_LIMIT = 56 * 1024 * 1024


def _cparams(n_grid):
    return pltpu.CompilerParams(dimension_semantics=("arbitrary",) * n_grid, vmem_limit_bytes=VMEM_LIMIT)


def _dot(a, b):
    return jnp.dot(a, b, preferred_element_type=F32)


def _dot_nt(a, b):
    return lax.dot_general(a, b, (((1,), (1,)), ((), ())), preferred_element_type=F32)


def _dot_tn(a, b):
    return lax.dot_general(a, b, (((0,), (0,)), ((), ())), preferred_element_type=F32)


def _rows8(v):
    r, c = v.shape
    return v.reshape(r // 8, 8, c).sum(axis=0)


def _rstd(v):
    return lax.rsqrt(jnp.mean(v * v, axis=-1, keepdims=True) + RMS_EPS)


def _rms_bwd(dy, xhat, rstd, gain):
    dyg = dy * gain
    return rstd * (dyg - xhat * jnp.mean(dyg * xhat, axis=-1, keepdims=True))


def _sigmoid(v):
    return 1.0 / (1.0 + jnp.exp(-v))


def _gcols(g):
    return slice(g * POOL_GROUP, (g + 1) * POOL_GROUP)


def _window_sum(e, g, ahead):
    n = e.shape[0]
    w = e
    for level in range(g + 1):
        shift = 1 << level
        w = w + pltpu.roll(w, (n - shift) if ahead else shift, 0)
    return w


def _pool_fwd(pb, g, n):
    e = pb[0:HALO + n, _gcols(g)]
    return _window_sum(e, g, False)[HALO:, :] * (1.0 / (2 << g)) - e[HALO:, :]


def _pool_bwd(qb, g, r0, n):
    e = qb[r0:r0 + n + HALO, _gcols(g)]
    return _window_sum(e, g, True)[0:n, :] * (1.0 / (2 << g)) - e[0:n, :]


def _full(shape):
    nd = len(shape)
    return pl.BlockSpec(shape, lambda *_: (0,) * nd)


ANY = pl.BlockSpec(memory_space=pl.ANY)


def _mesh_pos():
    x, y, c = lax.axis_index("x"), lax.axis_index("y"), lax.axis_index("c")
    chips = [(1 - x, y), (x, 1 - y), (1 - x, 1 - y)]
    return x, y, c, chips


def _half(ref, h):
    hr = ref.shape[0] // 2
    return ref.at[pl.ds(h * hr, hr), :]


class _AllGather:
    PER_ARRAY = 9

    def __init__(self, ins, outs, send_sems, recv_sems):
        self.ins, self.outs, self.send_sems, self.recv_sems = ins, outs, send_sems, recv_sems
        self.n = len(ins)

    @classmethod
    def scratch(cls, n):
        return [pltpu.SemaphoreType.DMA((cls.PER_ARRAY * n,)), pltpu.SemaphoreType.DMA((cls.PER_ARRAY * n,))]

    @staticmethod
    def out_shape(shards):
        return [jax.ShapeDtypeStruct((N_CHIPS,) + s.shape, s.dtype) for s in shards]

    def _copy(self, a, k, src, dst, to):
        i = self.PER_ARRAY * a + k
        return pltpu.make_async_remote_copy(src_ref=src, dst_ref=dst, send_sem=self.send_sems.at[i],
                                            recv_sem=self.recv_sems.at[i], device_id=to, device_id_type=MESH)

    def _piece(self, a, chip, piece, h=None):
        h = lax.axis_index("c") if h is None else h
        rows = self.ins[a].shape[0] // 4
        return self.outs[a].at[chip].at[pl.ds((2 * h + piece) * rows, rows), :]

    def _own(self, a, k):
        x, y, c, chips = _mesh_pos()
        piece = (1, 0, 0, 1)[k]
        rows = self.ins[a].shape[0] // 4
        src = self.ins[a].at[pl.ds((2 * c + piece) * rows, rows), :]
        return self._copy(a, k, src, self._piece(a, 2 * x + y, piece), (*chips[k // 2], c))

    def _relay(self, a, k):
        x, y, c, chips = _mesh_pos()
        source, to, piece = (chips[1], chips[0], 0) if k == 4 else (chips[0], chips[1], 1)
        rows = self._piece(a, 2 * source[0] + source[1], piece)
        return self._copy(a, k, rows, rows, (*to, c))

    def _sibling(self, a, k, h):
        x, y, c, chips = _mesh_pos()
        chip = chips[k - 6]
        slot = _half(self.outs[a].at[2 * chip[0] + chip[1]], h)
        return self._copy(a, k, slot, slot, (x, y, 1 - c))

    def start(self, arrays=None):
        for a in (range(self.n) if arrays is None else arrays):
            for k in range(4):
                self._own(a, k).start()

    def relay(self, a):
        self._own(a, 2).wait_recv()
        self._relay(a, 4).start()
        self._own(a, 0).wait_recv()
        self._relay(a, 5).start()

    def forward(self, a):
        c = lax.axis_index("c")
        self._own(a, 1).wait_recv()
        self._sibling(a, 6, c).start()
        self._own(a, 3).wait_recv()
        self._sibling(a, 7, c).start()
        self._relay(a, 4).wait_recv()
        self._relay(a, 5).wait_recv()
        self._sibling(a, 8, c).start()

    def finish(self, arrays=None):
        c = lax.axis_index("c")
        arrays = range(self.n) if arrays is None else arrays
        for a in arrays:
            for k in range(6, 9):
                self._sibling(a, k, 1 - c).wait_recv()
        for a in arrays:
            for k in range(4):
                self._own(a, k).wait_send()
            for k in range(4, 6):
                self._relay(a, k).wait_send()
            for k in range(6, 9):
                self._sibling(a, k, c).wait_send()


class _ExchangeHalves:
    def __init__(self, ins, recvs, send_sems, recv_sems):
        self.ins, self.recvs, self.send_sems, self.recv_sems = ins, recvs, send_sems, recv_sems

    @staticmethod
    def scratch(n):
        return [pltpu.SemaphoreType.DMA((n,)), pltpu.SemaphoreType.DMA((n,))]

    @staticmethod
    def out_shape(grads):
        return [jax.ShapeDtypeStruct((g.shape[0], g.shape[1] // 2, g.shape[2]), g.dtype) for g in grads]

    def _copies(self):
        x, y, c, _ = _mesh_pos()
        out = []
        for a, (src, dst) in enumerate(zip(self.ins, self.recvs)):
            hr = src.shape[1] // 2
            out.append(pltpu.make_async_remote_copy(
                src_ref=src.at[:, pl.ds((1 - c) * hr, hr), :], dst_ref=dst, send_sem=self.send_sems.at[a],
                recv_sem=self.recv_sems.at[a], device_id=(x, y, 1 - c), device_id_type=MESH))
        return out

    def start(self):
        for cp in self._copies():
            cp.start()

    def finish(self):
        for cp in self._copies():
            cp.wait()


def _exchange_halves(grads):
    n = len(grads)

    def body(*refs):
        ex = _ExchangeHalves(refs[:n], refs[n:2 * n], *refs[2 * n:])
        ex.start()
        ex.finish()

    return pl.pallas_call(
        body, name="grad_exchange_halves", out_shape=_ExchangeHalves.out_shape(grads),
        in_specs=[ANY] * n, out_specs=[ANY] * n, scratch_shapes=_ExchangeHalves.scratch(n),
    )(*grads)


class _ScatterToChips:
    def __init__(self, ins, rbufs, send_sems, recv_sems):
        self.ins, self.rbufs, self.send_sems, self.recv_sems = ins, rbufs, send_sems, recv_sems

    @staticmethod
    def scratch(n):
        return [pltpu.SemaphoreType.DMA((3 * n,)), pltpu.SemaphoreType.DMA((3 * n,))]

    @staticmethod
    def out_shape(sums):
        return [jax.ShapeDtypeStruct((3,) + s.shape[1:], BF16) for s in sums]

    def _copies(self):
        x, y, c, chips = _mesh_pos()
        out = []
        for a, (src, dst) in enumerate(zip(self.ins, self.rbufs)):
            for k, chip in enumerate(chips):
                out.append(pltpu.make_async_remote_copy(
                    src_ref=src.at[2 * chip[0] + chip[1]], dst_ref=dst.at[k], send_sem=self.send_sems.at[3 * a + k],
                    recv_sem=self.recv_sems.at[3 * a + k], device_id=(*chip, c), device_id_type=MESH))
        return out

    def start(self):
        for cp in self._copies():
            cp.start()

    def finish(self):
        for cp in self._copies():
            cp.wait()


def _gather_halves(halves):
    n = len(halves)

    def body(*refs):
        ins, outs = refs[:n], refs[n:2 * n]
        send_sems, recv_sems = refs[2 * n:]
        x, y, c, _ = _mesh_pos()
        sib = (x, y, 1 - c)
        remote = [pltpu.make_async_remote_copy(src_ref=ins[a].at[c], dst_ref=outs[a].at[c],
                                               send_sem=send_sems.at[a], recv_sem=recv_sems.at[a],
                                               device_id=sib, device_id_type=MESH) for a in range(n)]
        for cp in remote:
            cp.start()
        for a in range(n):
            pltpu.make_async_remote_copy(src_ref=ins[a].at[1 - c], dst_ref=outs[a].at[1 - c], send_sem=send_sems.at[a],
                                         recv_sem=recv_sems.at[a], device_id=sib, device_id_type=MESH).wait_recv()
        for cp in remote:
            cp.wait_send()

    return pl.pallas_call(
        body, name="grad_gather_halves",
        out_shape=[jax.ShapeDtypeStruct(h.shape, F32) for h in halves],
        in_specs=[ANY] * n, out_specs=[ANY] * n, input_output_aliases={a: a for a in range(n)},
        scratch_shapes=[pltpu.SemaphoreType.DMA((n,)), pltpu.SemaphoreType.DMA((n,))],
    )(*halves)


SMALL_A_ROWS = 24
SMALL_B_ROWS = 8
SMALL_C_ROWS = N_POOL_GROUPS * POOL_GROUP


class _AllReduceSmall:
    N_IN = 10
    SHAPES = [(SMALL_A_ROWS, D_MODEL), (SMALL_B_ROWS, D_CONV), (SMALL_C_ROWS, POOL_GROUP)]

    def __init__(self, ins, outs, scratch):
        self.ins, self.outs = ins, outs
        self.bufs, self.rcvs, self.send_sems, self.recv_sems = scratch[:3], scratch[3:6], scratch[6], scratch[7]

    @classmethod
    def scratch(cls):
        return ([pltpu.VMEM((3,) + s, F32) for s in cls.SHAPES] + [pltpu.VMEM((3,) + s, F32) for s in cls.SHAPES]
                + [pltpu.SemaphoreType.DMA((9,)), pltpu.SemaphoreType.DMA((9,))])

    @classmethod
    def out_shape(cls):
        return [jax.ShapeDtypeStruct(s, F32) for s in cls.SHAPES]

    def _copies(self, st):
        x, y, c, _ = _mesh_pos()
        peer = [(x, y, 1 - c), (1 - x, y, c), (x, 1 - y, c)][st]
        return [pltpu.make_async_remote_copy(
            src_ref=buf.at[st], dst_ref=rcv.at[st], send_sem=self.send_sems.at[3 * st + i],
            recv_sem=self.recv_sems.at[3 * st + i], device_id=peer, device_id_type=MESH)
            for i, (buf, rcv) in enumerate(zip(self.bufs, self.rcvs))]

    def pack_and_send(self):
        dg1_ref, dg1m_ref, dg2_ref, dg3_ref, dg4_ref, loss_ref, dmeta_ref, dsc_ref, dcw_ref, dpw_ref = self.ins
        a_buf, b_buf, c_buf = self.bufs

        def rowsum(v):
            return jnp.sum(v, axis=0, keepdims=True)

        a_buf[0, 0:1, :] = rowsum(dg1_ref[...] + dg1m_ref[...])
        a_buf[0, 1:2, :] = rowsum(dg2_ref[...])
        a_buf[0, 2:3, :] = rowsum(dg3_ref[...])
        a_buf[0, 3:4, :] = rowsum(dg4_ref[...])
        loss = jnp.sum(rowsum(loss_ref[...]), axis=1, keepdims=True) * (0.5 / D_MODEL)
        a_buf[0, 4:5, :] = jnp.broadcast_to(loss, (1, D_MODEL))
        a_buf[0, 5:8, :] = jnp.zeros((3, D_MODEL), F32)
        a_buf[0, 8:24, :] = dmeta_ref[...]
        b_buf[0, 0:1, :] = rowsum(dsc_ref[...])
        for k in range(3):
            b_buf[0, 1 + k:2 + k, :] = rowsum(dcw_ref[8 * k:8 * k + 8, :])
        b_buf[0, 4:8, :] = jnp.zeros((4, D_CONV), F32)
        c_buf[0] = dpw_ref[...]
        for cp in self._copies(0):
            cp.start()

    def combine(self, st):
        for cp in self._copies(st):
            cp.wait()
        if st < 2:
            for buf, rcv in zip(self.bufs, self.rcvs):
                buf[st + 1] = buf[st] + rcv[st]
            for cp in self._copies(st + 1):
                cp.start()
        else:
            for out, buf, rcv in zip(self.outs, self.bufs, self.rcvs):
                out[...] = buf[st] + rcv[st]


def _row_block(rows):
    for cand in (512, 448, 384, 352, 320, 256, 128, 64, 32, 16):
        if rows % cand == 0:
            return cand
    return rows


def _add_pairs_multi(grads, recvs, place):
    n = len(grads)
    n_sh, rows2, cols = grads[0].shape
    hr = rows2 // 2
    br = _row_block(hr)

    def body(place_ref, *refs):
        for a_ref, b_ref, o_ref in zip(refs[:n], refs[n:2 * n], refs[2 * n:]):
            o_ref[...] = (a_ref[0] + b_ref[...]).astype(BF16)

    return pl.pallas_call(
        body, name="grad_add_pairs",
        grid_spec=pltpu.PrefetchScalarGridSpec(
            num_scalar_prefetch=1, grid=(n_sh, hr // br),
            in_specs=[pl.BlockSpec((1, 1, br, cols), lambda j, i, p: (j, p[1], i, 0))] * n
            + [pl.BlockSpec((1, br, cols), lambda j, i, p: (j, i, 0))] * n,
            out_specs=[pl.BlockSpec((1, br, cols), lambda j, i, p: (j, i, 0))] * n),
        out_shape=[jax.ShapeDtypeStruct((n_sh, hr, cols), BF16)] * n, compiler_params=_cparams(2),
    )(place, *[g.reshape(n_sh, 2, hr, cols) for g in grads], *recvs)


def _add_pairs(grad, recv, place):
    return _add_pairs_multi([grad], [recv], place)[0]


def _add_chips(grads, recvs, rbufs, place, scattered=(), name="grad_add_chips"):
    n, n_sc = len(grads), len(scattered)
    n_sh, rows2, cols = grads[0].shape
    hr = rows2 // 2
    br = _row_block(hr)
    n_steps = hr // br

    def body(place_ref, *refs):
        a_refs, b_refs, r_refs = refs[:n], refs[n:2 * n], refs[2 * n:3 * n]
        o_refs = refs[3 * n + n_sc:4 * n + n_sc]
        if n_sc:
            scatter = _ScatterToChips(refs[3 * n:3 * n + n_sc], refs[4 * n + n_sc:4 * n + 2 * n_sc], *refs[-2:])

            @pl.when(pl.program_id(0) == 0)
            def _():
                scatter.start()

        for a_ref, b_ref, r_ref, o_ref in zip(a_refs, b_refs, r_refs, o_refs):
            own = a_ref[0, 0] + b_ref[0]
            o_ref[0] = ((own + r_ref[0].astype(F32)) + r_ref[1].astype(F32)) + r_ref[2].astype(F32)

        if n_sc:
            @pl.when(pl.program_id(0) == n_steps - 1)
            def _():
                scatter.finish()

    outs = pl.pallas_call(
        body, name=name,
        grid_spec=pltpu.PrefetchScalarGridSpec(
            num_scalar_prefetch=1, grid=(n_steps,),
            in_specs=[pl.BlockSpec((1, 1, br, cols), lambda i, p: (p[0], p[1], i, 0))] * n
            + [pl.BlockSpec((1, br, cols), lambda i, p: (p[0], i, 0))] * n
            + [pl.BlockSpec((3, br, cols), lambda i, p: (0, i, 0))] * n + [ANY] * n_sc,
            out_specs=[pl.BlockSpec((1, br, cols), lambda i, p: (p[1], i, 0))] * n + [ANY] * n_sc,
            scratch_shapes=_ScatterToChips.scratch(n_sc) if n_sc else []),
        out_shape=[jax.ShapeDtypeStruct((2, hr, cols), F32)] * n + _ScatterToChips.out_shape(list(scattered)),
        compiler_params=_cparams(1),
    )(place, *[g.reshape(n_sh, 2, hr, cols) for g in grads], *recvs, *rbufs, *scattered)
    return outs[:n], outs[n:]


def _adamw_math(w, g, m, v):
    m2 = ADAM_B1 * m + (1.0 - ADAM_B1) * g
    v2 = ADAM_B2 * v + (1.0 - ADAM_B2) * (g * g)
    m_hat = m2 / (1.0 - ADAM_B1 ** ADAM_STEP)
    v_hat = v2 / (1.0 - ADAM_B2 ** ADAM_STEP)
    delta = -ADAM_LR * (m_hat / (jnp.sqrt(v_hat) + ADAM_EPS) + ADAM_WD * w)
    return delta, m2, v2


def _adamw_big(groups):
    n = len(groups)
    rows, cols = groups[0][0].shape
    br = _row_block(rows)
    if n > 1 and br % 16 == 0:
        br //= 2

    def body(*refs):
        for i in range(n):
            w_ref, g_ref, m_ref, v_ref = refs[4 * i:4 * i + 4]
            d_ref, m2_ref, v2_ref = refs[4 * n + 3 * i:4 * n + 3 * i + 3]
            d, m2, v2 = _adamw_math(w_ref[...], g_ref[...], m_ref[...], v_ref[...])
            d_ref[...] = d
            m2_ref[...] = m2
            v2_ref[...] = v2

    spec = pl.BlockSpec((br, cols), lambda i: (i, 0))
    outs = pl.pallas_call(
        body, name="adamw_big", grid=(rows // br,),
        out_shape=[jax.ShapeDtypeStruct((rows, cols), F32)] * (3 * n),
        in_specs=[spec] * (4 * n), out_specs=[spec] * (3 * n), compiler_params=_cparams(1),
    )(*[a for grp in groups for a in grp])
    return [list(outs[3 * i:3 * i + 3]) for i in range(n)]


def _adamw_small(groups):
    n = len(groups)

    def body(*refs):
        ins, outs = refs[:4 * n], refs[4 * n:]
        for i in range(n):
            w, g, m, v = (r[...] for r in ins[4 * i:4 * i + 4])
            d, m2, v2 = _adamw_math(w, g, m, v)
            outs[3 * i][...] = d
            outs[3 * i + 1][...] = m2
            outs[3 * i + 2][...] = v2

    vm = pl.BlockSpec(memory_space=pltpu.VMEM)
    flat = [a for grp in groups for a in grp]
    out_shape = [jax.ShapeDtypeStruct(grp[0].shape, F32) for grp in groups for _ in range(3)]
    outs = pl.pallas_call(body, name="adamw_small", out_shape=out_shape,
                          in_specs=[vm] * (4 * n), out_specs=[vm] * (3 * n))(*flat)
    return [tuple(outs[3 * i:3 * i + 3]) for i in range(n)]


def _load_gathered(gathered, shards, dst_slots, sems):
    n = len(gathered)
    me = 2 * lax.axis_index("x") + lax.axis_index("y")

    def copies(j, own):
        return [pltpu.make_async_copy(shards[a] if own else gathered[a].at[j], dst_slots[a](j), sems.at[n * j + a])
                for a in range(n)]

    for wait in (False, True):
        for j in range(N_CHIPS):
            for own in (False, True):
                @pl.when((me == j) == own)
                def _():
                    for cp in copies(j, own):
                        cp.wait() if wait else cp.start()


N_MIX_SHARDS = 3


def _mixer_fwd(x3, g1, g2, poolw, pscale, shards):
    n_seq, seq, _ = x3.shape
    tm = min(TM_MIX_FWD, seq)
    n_t = seq // tm
    n_steps = n_seq * n_t
    n_ag = len(shards)
    n_ffn = n_ag - N_MIX_SHARDS
    small_rows = shards[2].shape[0]
    conv_cols = D_CONV // N_CHIPS

    def body(x_ref, g1_ref, g2_ref, pw_ref, ps_ref, *rest):
        ag = _AllGather(rest[:n_ag], rest[n_ag + 10:2 * n_ag + 10], *rest[-2:])
        (z_ref, m_ref, h1_ref, a_ref, conv_ref, pooled_ref, yc_ref, zm_ref, meta_ref,
         cw_ref) = rest[n_ag:n_ag + 10]
        win_v, wout_v, small_v, cvb, pb, load_sems = rest[2 * n_ag + 10:-2]
        s, t = pl.program_id(0), pl.program_id(1)
        step = s * n_t + t

        @pl.when(step == 0)
        def _():
            ag.start(range(N_MIX_SHARDS))
            for a in range(N_MIX_SHARDS):
                ag.relay(a)
            for a in range(N_MIX_SHARDS):
                ag.forward(a)
            ag.finish(range(N_MIX_SHARDS))
            ag.start(range(N_MIX_SHARDS, n_ag))
            _load_gathered(ag.outs[:N_MIX_SHARDS], ag.ins[:N_MIX_SHARDS],
                           [lambda j: win_v.at[j], lambda j: wout_v.at[pl.ds(j * OUT_SHARD, OUT_SHARD), :],
                            lambda j: small_v.at[j]], load_sems)

            meta = jnp.concatenate([small_v[j, 0:N_META, :] for j in range(N_CHIPS)], axis=1)
            meta_ref[...] = meta
            cw_ref[...] = jnp.concatenate([small_v[j, N_META:N_META + 3, 0:conv_cols] for j in range(N_CHIPS)], axis=1)
            a_meta = (meta * _rstd(meta) * g1_ref[...]).astype(BF16)
            for j in range(N_CHIPS):
                zm_ref[:, j * IN_SHARD:(j + 1) * IN_SHARD] = _dot(a_meta, win_v[j])

        for i in range(n_ffn):
            @pl.when(step == ((i + 1) * n_steps) // (2 * n_ffn + 2))
            def _():
                ag.relay(N_MIX_SHARDS + i)

        for i in range(n_ffn):
            @pl.when(step == min(n_steps // 2 + ((i + 1) * n_steps) // (2 * n_ffn + 2), n_steps - 1))
            def _():
                ag.forward(N_MIX_SHARDS + i)

        @pl.when(t == 0)
        def _():
            cvb[0:HALO, :] = zm_ref[:, IN_SHARD:2 * IN_SHARD] * zm_ref[:, 2 * IN_SHARD:3 * IN_SHARD]
            pb[0:HALO, :] = zm_ref[:, 3 * IN_SHARD:4 * IN_SHARD]

        @pl.when(t > 0)
        def _():
            cvb[0:HALO, :] = cvb[tm:tm + HALO, :]
            pb[0:HALO, :] = pb[tm:tm + HALO, :]

        xt = x_ref[0]
        a = (xt * _rstd(xt) * g1_ref[...]).astype(BF16)
        a_ref[...] = a
        zb = _dot(a, win_v[0])
        zc = _dot(a, win_v[1])
        zv = _dot(a, win_v[2])
        zp = _dot(a, win_v[3])
        z_ref[0, :, 0:IN_SHARD] = zb
        z_ref[0, :, IN_SHARD:2 * IN_SHARD] = zc
        z_ref[0, :, 2 * IN_SHARD:3 * IN_SHARD] = zv
        cv = zc * zv
        cvb[HALO:HALO + tm, :] = cv
        pb[HALO:HALO + tm, :] = zp
        cw = cw_ref[...]
        conv = cw[0:1] * cvb[HALO - 2:HALO - 2 + tm, :] + cw[1:2] * cvb[HALO - 1:HALO - 1 + tm, :] + cw[2:3] * cv
        conv_ref[...] = conv
        parts = [(zb * conv).astype(BF16)]
        for g in range(N_POOL_GROUPS):
            pooled = _pool_fwd(pb, g, tm).astype(BF16)
            pooled_ref[:, _gcols(g)] = pooled
            parts.append((_dot(pooled, pw_ref[g]) * ps_ref[:, _gcols(g)]).astype(BF16))
        ycat = jnp.concatenate(parts, axis=1)
        yc_ref[...] = ycat
        m = _dot(ycat, wout_v[...])
        m_ref[0] = m
        h1_ref[0] = xt + m * _rstd(m) * g2_ref[...]

        @pl.when(step == n_steps - 1)
        def _():
            ag.finish(range(N_MIX_SHARDS, n_ag))

    n_rows = n_seq * seq
    row = lambda c: pl.BlockSpec((1, tm, c), lambda s, t: (s, t, 0))
    row2 = lambda c: pl.BlockSpec((tm, c), lambda s, t: (s * n_t + t, 0))
    outs = pl.pallas_call(
        body, name="mixer_fwd", grid=(n_seq, n_t),
        out_shape=[jax.ShapeDtypeStruct((n_seq, seq, D_Z), F32), jax.ShapeDtypeStruct((n_seq, seq, D_MODEL), F32),
                   jax.ShapeDtypeStruct((n_seq, seq, D_MODEL), F32), jax.ShapeDtypeStruct((n_rows, D_MODEL), BF16),
                   jax.ShapeDtypeStruct((n_rows, D_CONV), F32), jax.ShapeDtypeStruct((n_rows, D_POOL), BF16),
                   jax.ShapeDtypeStruct((n_rows, D_MODEL), BF16), jax.ShapeDtypeStruct((N_META, D_IN_PROJ), F32),
                   jax.ShapeDtypeStruct((N_META, D_MODEL), F32), jax.ShapeDtypeStruct((3, D_CONV), F32)]
        + _AllGather.out_shape(shards),
        in_specs=[row(D_MODEL), _full((1, D_MODEL)), _full((1, D_MODEL)),
                  _full((N_POOL_GROUPS, POOL_GROUP, POOL_GROUP)), _full((1, D_POOL))] + [ANY] * n_ag,
        out_specs=[row(D_Z), row(D_MODEL), row(D_MODEL), row2(D_MODEL), row2(D_CONV), row2(D_POOL), row2(D_MODEL),
                   _full((N_META, D_IN_PROJ)), _full((N_META, D_MODEL)), _full((3, D_CONV))] + [ANY] * n_ag,
        scratch_shapes=[pltpu.VMEM((N_CHIPS, D_MODEL, IN_SHARD), BF16), pltpu.VMEM((D_MODEL, D_MODEL), BF16),
                        pltpu.VMEM((N_CHIPS, small_rows, D_MODEL // N_CHIPS), F32),
                        pltpu.VMEM((HALO + tm, D_CONV), F32), pltpu.VMEM((HALO + tm, D_POOL), F32),
                        pltpu.SemaphoreType.DMA((N_MIX_SHARDS * N_CHIPS,))] + _AllGather.scratch(n_ag),
        compiler_params=_cparams(2),
    )(x3, g1, g2, poolw, pscale, *shards)
    return outs[:10], outs[10:]


def _ffn_chunks():
    out, r0 = [], 0
    while r0 < D_FF:
        out.append((r0, min(FF_CHUNK, D_FF - r0)))
        r0 += FF_CHUNK
    return out


def _ffn_fwd_bwd(h1, target, g3, g4, gathered, shards):
    n_rows = h1.shape[0]
    tm = min(TM_FFN, n_rows)
    chunks = _ffn_chunks()

    def body(h1_ref, t_ref, g3_ref, g4_ref, wg_all, wu_all, wd_all, wg_s, wu_s, wd_s,
             dh1_ref, f_ref, dd_ref, ds_ref, du_ref, gg_ref, loss_ref, dg3_ref, dg4_ref,
             wg_v, wu_v, wd_v, s_sc, u_sc, sems):
        @pl.when(pl.program_id(0) == 0)
        def _():
            _load_gathered([wg_all, wu_all, wd_all], [wg_s, wu_s, wd_s],
                           [functools.partial(lambda v, j: v.at[pl.ds(j * FF_SHARD, FF_SHARD), :], v)
                            for v in (wg_v, wu_v, wd_v)], sems)
            loss_ref[...] = jnp.zeros_like(loss_ref)
            dg3_ref[...] = jnp.zeros_like(dg3_ref)
            dg4_ref[...] = jnp.zeros_like(dg4_ref)

        h1v = h1_ref[...]
        r3 = _rstd(h1v)
        hh = h1v * r3
        g3v, g4v = g3_ref[...], g4_ref[...]
        f = (hh * g3v).astype(BF16)
        f_ref[...] = f
        d = jnp.zeros((tm, D_MODEL), F32)
        for r0, sz in chunks:
            s = _dot_nt(f, wg_v[r0:r0 + sz, :])
            u = _dot_nt(f, wu_v[r0:r0 + sz, :])
            s_sc[:, r0:r0 + sz] = s
            u_sc[:, r0:r0 + sz] = u
            gc = (s * _sigmoid(s) * u).astype(BF16)
            gg_ref[:, r0:r0 + sz] = gc
            d = d + _dot(gc, wd_v[r0:r0 + sz, :])
        r4 = _rstd(d)
        dh = d * r4
        err = (h1v + dh * g4v) - t_ref[...]
        loss_ref[...] += _rows8(err * err)
        dy = err * (1.0 / D_MODEL)
        dg4_ref[...] += _rows8(dy * dh)
        ddb = _rms_bwd(dy, dh, r4, g4v).astype(BF16)
        dd_ref[...] = ddb
        df = jnp.zeros((tm, D_MODEL), F32)
        for r0, sz in chunks:
            dgg = _dot_nt(ddb, wd_v[r0:r0 + sz, :])
            s = s_sc[:, r0:r0 + sz]
            u = u_sc[:, r0:r0 + sz]
            sig = _sigmoid(s)
            dsc = (dgg * u * (sig * (1.0 + s * (1.0 - sig)))).astype(BF16)
            duc = (dgg * (s * sig)).astype(BF16)
            ds_ref[:, r0:r0 + sz] = dsc
            du_ref[:, r0:r0 + sz] = duc
            df = df + _dot(dsc, wg_v[r0:r0 + sz, :]) + _dot(duc, wu_v[r0:r0 + sz, :])
        dg3_ref[...] += _rows8(df * hh)
        dh1_ref[...] = dy + _rms_bwd(df, hh, r3, g3v)

    row = pl.BlockSpec((tm, D_MODEL), lambda i: (i, 0))
    ffrow = pl.BlockSpec((tm, D_FF), lambda i: (i, 0))
    acc = _full((8, D_MODEL))
    act_bf = jax.ShapeDtypeStruct((n_rows, D_MODEL), BF16)
    ff_bf = jax.ShapeDtypeStruct((n_rows, D_FF), BF16)
    acc_shape = jax.ShapeDtypeStruct((8, D_MODEL), F32)
    w_vmem = pltpu.VMEM((D_FF, D_MODEL), BF16)
    return pl.pallas_call(
        body, name="ffn_fwd_bwd", grid=(n_rows // tm,),
        out_shape=[jax.ShapeDtypeStruct((n_rows, D_MODEL), F32), act_bf, act_bf, ff_bf, ff_bf, ff_bf,
                   acc_shape, acc_shape, acc_shape],
        in_specs=[row, row, _full((1, D_MODEL)), _full((1, D_MODEL))] + [ANY] * 6,
        out_specs=[row, row, row, ffrow, ffrow, ffrow, acc, acc, acc],
        scratch_shapes=[w_vmem, w_vmem, w_vmem, pltpu.VMEM((tm, D_FF), F32), pltpu.VMEM((tm, D_FF), F32),
                        pltpu.SemaphoreType.DMA((3 * N_CHIPS,))],
        compiler_params=_cparams(1),
    )(h1, target, g3, g4, *gathered, *shards)


def _ffn_weight_grads(name, acts, other, exchanged):
    n_rows = other.shape[0]
    n_a, n_ex = len(acts), len(exchanged)
    n_c = n_a
    tk = min(TK_DW, n_rows)
    n_k = n_rows // tk
    half = D_FF // n_c

    def body(other_ref, *rest):
        act_refs = rest[:n_a]
        out_refs = rest[n_a + n_ex:2 * n_a + n_ex]
        c, k = pl.program_id(0), pl.program_id(1)
        if n_ex:
            ex = _ExchangeHalves(rest[n_a:n_a + n_ex], rest[2 * n_a + n_ex:2 * n_a + 2 * n_ex], *rest[-2:])

            @pl.when((c == 0) & (k == 0))
            def _():
                ex.start()

        @pl.when(k == 0)
        def _():
            for o in out_refs:
                o[...] = jnp.zeros_like(o)

        ov = other_ref[...]
        for a, o in zip(act_refs, out_refs):
            o[...] += _dot_tn(a[...], ov)

        if n_ex:
            @pl.when((c == n_c - 1) & (k == n_k - 1))
            def _():
                ex.finish()

    row = pl.BlockSpec((tk, D_MODEL), lambda c, k: (k, 0))
    ffrow = pl.BlockSpec((tk, half), lambda c, k: (k, c))
    out = pl.BlockSpec((half, D_MODEL), lambda c, k: (c, 0))
    outs = pl.pallas_call(
        body, name=name, grid=(n_c, n_k),
        out_shape=[jax.ShapeDtypeStruct((D_FF, D_MODEL), F32)] * n_a + _ExchangeHalves.out_shape(exchanged),
        in_specs=[row] + [ffrow] * n_a + [ANY] * n_ex, out_specs=[out] * n_a + [ANY] * n_ex,
        scratch_shapes=_ExchangeHalves.scratch(n_ex) if n_ex else [],
        compiler_params=_cparams(2),
    )(other, *acts, *exchanged)
    return outs[:n_a], outs[n_a:]


def _mixer_bwd(dh1, m3, z3, conv2, pooled2, x3, zmeta, meta_full, g1, g2, convw, poolw, pscale, gathered, shards,
               exchanged, scattered):
    n_seq, seq, _ = x3.shape
    tm = min(TM_MIX_BWD, seq)
    sub = min(SUB_MIX_BWD, tm)
    n_t = seq // tm
    n_ex, n_sc = len(exchanged), len(scattered)
    n_cm = n_ex + n_sc
    n_out = 13

    def body(dh1_ref, m_ref, z_ref, conv_ref, pooled_ref, x_ref, zm_ref, meta_ref, g1_ref, g2_ref, cw_ref, pw_ref,
             ps_ref, win_all, wout_all, win_s, wout_s, *rest):
        outs0 = n_cm + n_out
        ex = _ExchangeHalves(rest[:n_ex], rest[outs0:outs0 + n_ex], *rest[-4:-2])
        sc = _ScatterToChips(rest[n_ex:n_cm], rest[outs0 + n_ex:outs0 + n_cm], *rest[-2:])
        (dx_ref, dz_ref, dm_ref, dg1_ref, dg2_ref, dsc_ref, dcw_ref, dpw_ref, dzm_ref, dmeta_ref, dg1m_ref, am_ref,
         dzmb_ref) = rest[n_cm:outs0]
        win_v, wout_v, dcb, dqb, mcb, mqb, load_sems = rest[outs0 + n_cm:-4]
        s, i = pl.program_id(0), pl.program_id(1)
        tr = n_t - 1 - i

        @pl.when((s == 0) & (i == 0))
        def _():
            sc.start()
            ex.start()
            _load_gathered([win_all, wout_all], [win_s, wout_s],
                           [lambda j: win_v.at[j], lambda j: wout_v.at[pl.ds(j * OUT_SHARD, OUT_SHARD), :]], load_sems)
            for ref in (dg1_ref, dg2_ref, dsc_ref, dcw_ref, dpw_ref, dzm_ref):
                ref[...] = jnp.zeros_like(ref)

        @pl.when(i == 0)
        def _():
            dcb[tm:tm + HALO, :] = jnp.zeros((HALO, D_CONV), F32)
            dqb[tm:tm + HALO, :] = jnp.zeros((HALO, D_POOL), F32)

        @pl.when(i > 0)
        def _():
            dcb[tm:tm + HALO, :] = dcb[0:HALO, :]
            dqb[tm:tm + HALO, :] = dqb[0:HALO, :]

        g1v, g2v = g1_ref[...], g2_ref[...]
        cw = cw_ref[...]

        for r0 in range(tm - sub, -1, -sub):
            rows = slice(r0, r0 + sub)
            dh1v = dh1_ref[0, rows, :]
            mv = m_ref[0, rows, :]
            r2 = _rstd(mv)
            mh = mv * r2
            dg2_ref[...] += _rows8(dh1v * mh)
            dmb = _rms_bwd(dh1v, mh, r2, g2v).astype(BF16)
            dm_ref[rows, :] = dmb
            dyc = _dot_nt(dmb, wout_v[...])
            dyconv = dyc[:, 0:D_CONV]

            for g in range(N_POOL_GROUPS):
                pooled = pooled_ref[rows, _gcols(g)]
                mixed = _dot(pooled, pw_ref[g])
                scale = ps_ref[:, _gcols(g)]
                dyp = dyc[:, D_CONV + g * POOL_GROUP:D_CONV + (g + 1) * POOL_GROUP]
                dsc_ref[:, _gcols(g)] += _rows8(dyp * mixed)
                dmix = (dyp * scale).astype(BF16)
                dpw_ref[g] += _dot_tn(pooled, dmix)
                dqb[rows, _gcols(g)] = _dot_nt(dmix, pw_ref[g])

            zb = z_ref[0, rows, 0:IN_SHARD]
            zc = z_ref[0, rows, IN_SHARD:2 * IN_SHARD]
            zv = z_ref[0, rows, 2 * IN_SHARD:3 * IN_SHARD]
            dconv = dyconv * zb
            dcb[rows, :] = dconv
            d1 = dcb[r0 + 1:r0 + 1 + sub, :]
            d2 = dcb[r0 + 2:r0 + 2 + sub, :]
            dcv = cw[2:3] * dconv + cw[1:2] * d1 + cw[0:1] * d2
            cv = zc * zv
            dcw_ref[0:8, :] += _rows8(cv * d2)
            dcw_ref[8:16, :] += _rows8(cv * d1)
            dcw_ref[16:24, :] += _rows8(cv * dconv)
            dzs = [(dyconv * conv_ref[rows, :]).astype(BF16), (dcv * zv).astype(BF16), (dcv * zc).astype(BF16),
                   jnp.concatenate([_pool_bwd(dqb, g, r0, sub) for g in range(N_POOL_GROUPS)], axis=1).astype(BF16)]
            da = jnp.zeros((sub, D_MODEL), F32)
            for j in range(N_CHIPS):
                dz_ref[j, rows, :] = dzs[j]
                da = da + _dot_nt(dzs[j], win_v[j])
            xt = x_ref[0, rows, :]
            r1 = _rstd(xt)
            xh = xt * r1
            dg1_ref[...] += _rows8(da * xh)
            dx_ref[0, rows, :] = dh1v + _rms_bwd(da, xh, r1, g1v)

        @pl.when(tr == 0)
        def _():
            mcb[0:HALO, :] = jnp.zeros((HALO, D_CONV), F32)
            mqb[0:HALO, :] = jnp.zeros((HALO, D_POOL), F32)
            mcb[HALO:2 * HALO, :] = dcb[0:HALO, :]
            mqb[HALO:2 * HALO, :] = dqb[0:HALO, :]
            m1 = mcb[1:1 + HALO, :]
            m2 = mcb[2:2 + HALO, :]
            zc_m = zm_ref[:, IN_SHARD:2 * IN_SHARD]
            zv_m = zm_ref[:, 2 * IN_SHARD:3 * IN_SHARD]
            cv_m = zc_m * zv_m
            dcw_ref[0:8, :] += _rows8(cv_m * m2)
            dcw_ref[8:16, :] += _rows8(cv_m * m1)
            dcv_m = cw[1:2] * m1 + cw[0:1] * m2
            dzm_ref[:, IN_SHARD:2 * IN_SHARD] += dcv_m * zv_m
            dzm_ref[:, 2 * IN_SHARD:3 * IN_SHARD] += dcv_m * zc_m
            dzm_ref[:, 3 * IN_SHARD:4 * IN_SHARD] += jnp.concatenate(
                [_pool_bwd(mqb, g, 0, HALO) for g in range(N_POOL_GROUPS)], axis=1)

        @pl.when((s == n_seq - 1) & (i == n_t - 1))
        def _():
            xm = meta_ref[...]
            rm = _rstd(xm)
            xmh = xm * rm
            am_ref[...] = (xmh * g1v).astype(BF16)
            da_m = jnp.zeros((N_META, D_MODEL), F32)
            for j in range(N_CHIPS):
                dzj = dzm_ref[:, j * IN_SHARD:(j + 1) * IN_SHARD].astype(BF16)
                dzmb_ref[j] = dzj
                da_m = da_m + _dot_nt(dzj, win_v[j])
            dg1m_ref[...] = _rows8(da_m * xmh)
            dmeta_ref[...] = _rms_bwd(da_m, xmh, rm, g1v)
            ex.finish()
            sc.finish()

    row3 = lambda c: pl.BlockSpec((1, tm, c), lambda s, i: (s, n_t - 1 - i, 0))
    row2 = lambda c: pl.BlockSpec((tm, c), lambda s, i: (s * n_t + n_t - 1 - i, 0))
    n_rows = n_seq * seq
    outs = pl.pallas_call(
        body, name="mixer_bwd", grid=(n_seq, n_t),
        out_shape=[jax.ShapeDtypeStruct((n_seq, seq, D_MODEL), F32),
                   jax.ShapeDtypeStruct((N_CHIPS, n_rows, IN_SHARD), BF16), jax.ShapeDtypeStruct((n_rows, D_MODEL), BF16),
                   jax.ShapeDtypeStruct((8, D_MODEL), F32), jax.ShapeDtypeStruct((8, D_MODEL), F32),
                   jax.ShapeDtypeStruct((8, D_POOL), F32), jax.ShapeDtypeStruct((24, D_CONV), F32),
                   jax.ShapeDtypeStruct((N_POOL_GROUPS, POOL_GROUP, POOL_GROUP), F32),
                   jax.ShapeDtypeStruct((N_META, D_IN_PROJ), F32),
                   jax.ShapeDtypeStruct((N_META, D_MODEL), F32), jax.ShapeDtypeStruct((8, D_MODEL), F32),
                   jax.ShapeDtypeStruct((N_META, D_MODEL), BF16),
                   jax.ShapeDtypeStruct((N_CHIPS, N_META, IN_SHARD), BF16)]
        + _ExchangeHalves.out_shape(exchanged) + _ScatterToChips.out_shape(scattered),
        in_specs=[row3(D_MODEL), row3(D_MODEL), row3(D_Z), row2(D_CONV), row2(D_POOL), row3(D_MODEL),
                  _full((N_META, D_IN_PROJ)), _full((N_META, D_MODEL)), _full((1, D_MODEL)), _full((1, D_MODEL)),
                  _full((3, D_CONV)), _full((N_POOL_GROUPS, POOL_GROUP, POOL_GROUP)), _full((1, D_POOL))]
        + [ANY] * (4 + n_cm),
        out_specs=[row3(D_MODEL), pl.BlockSpec((N_CHIPS, tm, IN_SHARD), lambda s, i: (0, s * n_t + n_t - 1 - i, 0)),
                   row2(D_MODEL),
                   _full((8, D_MODEL)), _full((8, D_MODEL)), _full((8, D_POOL)), _full((24, D_CONV)),
                   _full((N_POOL_GROUPS, POOL_GROUP, POOL_GROUP)), _full((N_META, D_IN_PROJ)),
                   _full((N_META, D_MODEL)), _full((8, D_MODEL)), _full((N_META, D_MODEL)),
                   _full((N_CHIPS, N_META, IN_SHARD))] + [ANY] * n_cm,
        scratch_shapes=[pltpu.VMEM((N_CHIPS, D_MODEL, IN_SHARD), BF16), pltpu.VMEM((D_MODEL, D_MODEL), BF16),
                        pltpu.VMEM((tm + HALO, D_CONV), F32), pltpu.VMEM((tm + HALO, D_POOL), F32),
                        pltpu.VMEM((2 * HALO, D_CONV), F32), pltpu.VMEM((2 * HALO, D_POOL), F32),
                        pltpu.SemaphoreType.DMA((2 * N_CHIPS,))]
        + _ExchangeHalves.scratch(n_ex) + _ScatterToChips.scratch(n_sc),
        compiler_params=_cparams(2),
    )(dh1, m3, z3, conv2, pooled2, x3, zmeta, meta_full, g1, g2, convw, poolw, pscale, *gathered, *shards,
      *exchanged, *scattered)
    return outs[:n_out], outs[n_out:n_out + n_ex], outs[n_out + n_ex:]


def _mixer_weight_grads(a, dz, ycat, dm, a_meta, dz_meta, ffn_sums, small):
    n_rows = a.shape[0]
    tk = min(TK_DW, n_rows)
    n_k = n_rows // tk
    n_sc, n_sm = len(ffn_sums), _AllReduceSmall.N_IN

    def body(a_ref, dz_ref, yc_ref, dm_ref, am_ref, dzm_ref, *rest):
        ins, outs, scratch = rest[:n_sc + n_sm], rest[n_sc + n_sm:2 * n_sc + n_sm + 5], rest[2 * n_sc + n_sm + 5:]
        dwin_ref, dwout_ref = outs[:2]
        scatter = _ScatterToChips(ins[:n_sc], outs[2:2 + n_sc], *scratch[:2])
        reduce_small = _AllReduceSmall(ins[n_sc:], outs[2 + n_sc:], scratch[2:])
        k = pl.program_id(0)

        @pl.when(k == 0)
        def _():
            scatter.start()
            reduce_small.pack_and_send()
            am_t = am_ref[...].T
            for j in range(N_CHIPS):
                dwin_ref[j] = _dot(am_t, dzm_ref[j])
            dwout_ref[...] = jnp.zeros_like(dwout_ref)

        for st in range(2):
            @pl.when(k == ((st + 1) * n_k) // 3)
            def _():
                reduce_small.combine(st)

        a_t = a_ref[...].T
        for j in range(N_CHIPS):
            dwin_ref[j] += _dot(a_t, dz_ref[j])
        dwout_ref[...] += _dot_tn(yc_ref[...], dm_ref[...])

        @pl.when(k == n_k - 1)
        def _():
            reduce_small.combine(2)
            scatter.finish()

    row = pl.BlockSpec((tk, D_MODEL), lambda k: (k, 0))
    outs = pl.pallas_call(
        body, name="mixer_weight_grads", grid=(n_k,),
        out_shape=[jax.ShapeDtypeStruct((N_CHIPS, D_MODEL, IN_SHARD), F32),
                   jax.ShapeDtypeStruct((D_MODEL, D_MODEL), F32)] + _ScatterToChips.out_shape(ffn_sums)
        + _AllReduceSmall.out_shape(),
        in_specs=[row, pl.BlockSpec((N_CHIPS, tk, IN_SHARD), lambda k: (0, k, 0)), row, row,
                  _full((N_META, D_MODEL)), _full((N_CHIPS, N_META, IN_SHARD))] + [ANY] * n_sc
        + [_full(s.shape) for s in small],
        out_specs=[_full((N_CHIPS, D_MODEL, IN_SHARD)), _full((D_MODEL, D_MODEL))] + [ANY] * n_sc
        + [_full(s) for s in _AllReduceSmall.SHAPES],
        scratch_shapes=_ScatterToChips.scratch(n_sc) + _AllReduceSmall.scratch(),
        compiler_params=_cparams(1),
    )(a, dz, ycat, dm, a_meta, dz_meta, *ffn_sums, *small)
    return ([outs[0], outs[1].reshape(N_CHIPS, OUT_SHARD, D_MODEL)], outs[2:2 + n_sc], outs[2 + n_sc:])


def kernel(x, meta_tokens, norm_mix_pre, w_in, conv_w, pool_w, pool_scale, w_out, norm_mix_post, norm_ffn_pre, w_gate, w_up, w_down, norm_ffn_post, loss_target, m_meta_tokens, m_norm_mix_pre, m_w_in, m_conv_w, m_pool_w, m_pool_scale, m_w_out, m_norm_mix_post, m_norm_ffn_pre, m_w_gate, m_w_up, m_w_down, m_norm_ffn_post, v_meta_tokens, v_norm_mix_pre, v_w_in, v_conv_w, v_pool_w, v_pool_scale, v_w_out, v_norm_mix_post, v_norm_ffn_pre, v_w_gate, v_w_up, v_w_down, v_norm_ffn_post):
    n_seq, seq, _ = x.shape
    n_rows = n_seq * seq
    chip = 2 * lax.axis_index("x") + lax.axis_index("y")
    meta_cols = D_MODEL // N_CHIPS
    conv_cols = D_CONV // N_CHIPS

    small = jnp.zeros((2 * HALO, meta_cols), F32)
    small = small.at[0:N_META, :].set(meta_tokens).at[N_META:N_META + 3, 0:conv_cols].set(conv_w[0])
    poolw_bf = pool_w[0].astype(BF16)
    pscale = pool_scale
    g1, g2, g3, g4 = norm_mix_pre, norm_mix_post, norm_ffn_pre, norm_ffn_post
    place = jnp.stack([chip, lax.axis_index("c")]).astype(jnp.int32)

    mix_shards = [w_in[0].astype(BF16), w_out[0].astype(BF16)]
    ffn_shards = [w_gate[0].T.astype(BF16), w_up[0].T.astype(BF16), w_down[0].astype(BF16)]
    ((z3, m3, h1, a_bf, conv2, pooled2, yc_bf, zmeta, meta_full, conv_full),
     (win_all, wout_all, _, *ffn_gathered)) = _mixer_fwd(x, g1, g2, poolw_bf, pscale, mix_shards + [small] + ffn_shards)
    dh1, f_bf, dd_bf, ds_bf, du_bf, gg_bf, lossp, dg3p, dg4p = _ffn_fwd_bwd(
        h1.reshape(n_rows, D_MODEL), loss_target.reshape(n_rows, D_MODEL), g3, g4, ffn_gathered, ffn_shards)
    as_shards = lambda g: g.reshape(N_CHIPS, FF_SHARD, D_MODEL)
    (dwg_t, dwu_t), _ = _ffn_weight_grads("ffn_weight_grads_gate_up", [ds_bf, du_bf], f_bf, [])
    dwg_t, dwu_t = as_shards(dwg_t), as_shards(dwu_t)
    (dwd,), (dwg_recv, dwu_recv) = _ffn_weight_grads("ffn_weight_grads_down", [gg_bf], dd_bf, [dwg_t, dwu_t])
    dwd = as_shards(dwd)
    ((grad_x, dz_bf, dm_bf, dg1p, dg2p, dscp, dcwp, dpw, _, dmeta, dg1m, a_meta, dz_meta), (dwd_recv,),
     (dwg_rbuf, dwu_rbuf)) = _mixer_bwd(
        dh1.reshape(n_seq, seq, D_MODEL), m3, z3, conv2, pooled2, x, zmeta, meta_full, g1, g2, conv_full, poolw_bf,
        pscale, [win_all, wout_all], mix_shards, [dwd],
        _add_pairs_multi([dwg_t, dwu_t], [dwg_recv, dwu_recv], place))
    mix_grads, (dwd_rbuf,), (a_red, b_red, c_red) = _mixer_weight_grads(
        a_bf, dz_bf, yc_bf, dm_bf, a_meta, dz_meta, [_add_pairs(dwd, dwd_recv, place)],
        [dg1p, dg1m, dg2p, dg3p, dg4p, lossp, dmeta, dscp, dcwp, dpw.reshape(SMALL_C_ROWS, POOL_GROUP)])

    mix_recvs = _exchange_halves(mix_grads)
    ffn_red, mix_rbufs = _add_chips([dwg_t, dwu_t, dwd], [dwg_recv, dwu_recv, dwd_recv],
                                    [dwg_rbuf, dwu_rbuf, dwd_rbuf], place,
                                    [_add_pairs(g, r, place) for g, r in zip(mix_grads, mix_recvs)],
                                    name="grad_add_chips_ffn")
    mix_red = [_add_chips([g], [r], [rb], place)[0][0] for g, r, rb in zip(mix_grads, mix_recvs, mix_rbufs)]
    reduced = _gather_halves(mix_red + list(ffn_red))
    g_win, g_wout, g_wg_t, g_wu_t, g_wd = [r.reshape(2 * r.shape[1], r.shape[2]) for r in reduced]

    loss = a_red[4, 0]
    g_g1, g_g2, g_g3, g_g4 = a_red[0:1], a_red[1:2], a_red[2:3], a_red[3:4]
    g_meta = lax.dynamic_slice(a_red, (8, chip * meta_cols), (N_META, meta_cols))
    g_pscale = b_red[0:1]
    g_conv = lax.dynamic_slice(b_red, (1, chip * conv_cols), (3, conv_cols))
    g_poolw = c_red

    big = [(w_in[0], g_win, m_w_in[0], v_w_in[0]), (w_out[0], g_wout, m_w_out[0], v_w_out[0]),
           (w_gate[0].T, g_wg_t, m_w_gate[0].T, v_w_gate[0].T), (w_up[0].T, g_wu_t, m_w_up[0].T, v_w_up[0].T),
           (w_down[0], g_wd, m_w_down[0], v_w_down[0])]
    big_out = _adamw_big(big[0:1]) + _adamw_big(big[1:2]) + _adamw_big(big[2:5])
    big_out[2] = [o.T for o in big_out[2]]
    big_out[3] = [o.T for o in big_out[3]]
    g_wg, g_wu = g_wg_t.T, g_wu_t.T
    small_groups = [
        (meta_tokens, g_meta, m_meta_tokens, v_meta_tokens),
        (g1, g_g1, m_norm_mix_pre, v_norm_mix_pre),
        (conv_w[0], g_conv, m_conv_w[0], v_conv_w[0]),
        (pool_w.reshape(SMALL_C_ROWS, POOL_GROUP), g_poolw, m_pool_w.reshape(SMALL_C_ROWS, POOL_GROUP),
         v_pool_w.reshape(SMALL_C_ROWS, POOL_GROUP)),
        (pool_scale, g_pscale, m_pool_scale, v_pool_scale),
        (g2, g_g2, m_norm_mix_post, v_norm_mix_post),
        (g3, g_g3, m_norm_ffn_pre, v_norm_ffn_pre),
        (g4, g_g4, m_norm_ffn_post, v_norm_ffn_post),
    ]
    small_out = _adamw_small(small_groups)

    grads_out = [g_meta, g_g1, g_win[None], g_conv[None], g_poolw.reshape(pool_w.shape), g_pscale, g_wout[None],
                 g_g2, g_g3, g_wg[None], g_wu[None], g_wd[None], g_g4]
    s_meta, s_g1, s_conv, s_poolw, s_pscale, s_g2, s_g3, s_g4 = small_out
    b_win, b_wout, b_wg, b_wu, b_wd = big_out

    def leaf(k):
        return [s_meta[k], s_g1[k], b_win[k][None], s_conv[k][None], s_poolw[k].reshape(pool_w.shape), s_pscale[k],
                b_wout[k][None], s_g2[k], s_g3[k], b_wg[k][None], b_wu[k][None], b_wd[k][None], s_g4[k]]

    return (loss, grad_x, *grads_out, *leaf(0), *leaf(1), *leaf(2))
```

```python
import functools

import jax
import jax.numpy as jnp
from jax import lax
from jax.experimental import pallas as pl
from jax.experimental.pallas import tpu as pltpu

F32 = jnp.float32
BF16 = jnp.bfloat16
MESH = pl.DeviceIdType.MESH

D_MODEL = 1024
D_CONV = 512
D_POOL = 512
POOL_GROUP = 128
N_POOL_GROUPS = 4
D_IN_PROJ = 2048
D_FF = 2816
N_CHIPS = 4
FF_SHARD = D_FF // N_CHIPS
IN_SHARD = D_IN_PROJ // N_CHIPS
OUT_SHARD = D_MODEL // N_CHIPS
D_Z = 3 * IN_SHARD
N_META = 16
HALO = 16
RMS_EPS = 1e-6

ADAM_LR = 0.001
ADAM_B1 = 0.9
ADAM_B2 = 0.999
ADAM_EPS = 1e-08
ADAM_WD = 0.01
ADAM_STEP = 10

TM_MIX_FWD = 512
TM_MIX_BWD = 512
SUB_MIX_BWD = 512
TM_FFN = 256
TK_DW = 1024
FF_CHUNK = 1024
VMEM_LIMIT = 56 * 1024 * 1024


def _cparams(n_grid):
    return pltpu.CompilerParams(dimension_semantics=("arbitrary",) * n_grid, vmem_limit_bytes=VMEM_LIMIT)


def _dot(a, b):
    return jnp.dot(a, b, preferred_element_type=F32)


def _dot_nt(a, b):
    return lax.dot_general(a, b, (((1,), (1,)), ((), ())), preferred_element_type=F32)


def _dot_tn(a, b):
    return lax.dot_general(a, b, (((0,), (0,)), ((), ())), preferred_element_type=F32)


def _rows8(v):
    r, c = v.shape
    return v.reshape(r // 8, 8, c).sum(axis=0)


def _rstd(v):
    return lax.rsqrt(jnp.mean(v * v, axis=-1, keepdims=True) + RMS_EPS)


def _rms_bwd(dy, xhat, rstd, gain):
    dyg = dy * gain
    return rstd * (dyg - xhat * jnp.mean(dyg * xhat, axis=-1, keepdims=True))


def _sigmoid(v):
    return 1.0 / (1.0 + jnp.exp(-v))


def _gcols(g):
    return slice(g * POOL_GROUP, (g + 1) * POOL_GROUP)


def _window_sum(e, g, ahead):
    n = e.shape[0]
    w = e
    for level in range(g + 1):
        shift = 1 << level
        w = w + pltpu.roll(w, (n - shift) if ahead else shift, 0)
    return w


def _pool_fwd(pb, g, n):
    e = pb[0:HALO + n, _gcols(g)]
    return _window_sum(e, g, False)[HALO:, :] * (1.0 / (2 << g)) - e[HALO:, :]


def _pool_bwd(qb, g, r0, n):
    e = qb[r0:r0 + n + HALO, _gcols(g)]
    return _window_sum(e, g, True)[0:n, :] * (1.0 / (2 << g)) - e[0:n, :]


def _full(shape):
    nd = len(shape)
    return pl.BlockSpec(shape, lambda *_: (0,) * nd)


ANY = pl.BlockSpec(memory_space=pl.ANY)


def _mesh_pos():
    x, y, c = lax.axis_index("x"), lax.axis_index("y"), lax.axis_index("c")
    chips = [(1 - x, y), (x, 1 - y), (1 - x, 1 - y)]
    return x, y, c, chips


def _half(ref, h):
    hr = ref.shape[0] // 2
    return ref.at[pl.ds(h * hr, hr), :]


class _AllGather:
    PER_ARRAY = 9

    def __init__(self, ins, outs, send_sems, recv_sems):
        self.ins, self.outs, self.send_sems, self.recv_sems = ins, outs, send_sems, recv_sems
        self.n = len(ins)

    @classmethod
    def scratch(cls, n):
        return [pltpu.SemaphoreType.DMA((cls.PER_ARRAY * n,)), pltpu.SemaphoreType.DMA((cls.PER_ARRAY * n,))]

    @staticmethod
    def out_shape(shards):
        return [jax.ShapeDtypeStruct((N_CHIPS,) + s.shape, s.dtype) for s in shards]

    def _copy(self, a, k, src, dst, to):
        i = self.PER_ARRAY * a + k
        return pltpu.make_async_remote_copy(src_ref=src, dst_ref=dst, send_sem=self.send_sems.at[i],
                                            recv_sem=self.recv_sems.at[i], device_id=to, device_id_type=MESH)

    def _piece(self, a, chip, piece, h=None):
        h = lax.axis_index("c") if h is None else h
        rows = self.ins[a].shape[0] // 4
        return self.outs[a].at[chip].at[pl.ds((2 * h + piece) * rows, rows), :]

    def _own(self, a, k):
        x, y, c, chips = _mesh_pos()
        piece = (1, 0, 0, 1)[k]
        rows = self.ins[a].shape[0] // 4
        src = self.ins[a].at[pl.ds((2 * c + piece) * rows, rows), :]
        return self._copy(a, k, src, self._piece(a, 2 * x + y, piece), (*chips[k // 2], c))

    def _relay(self, a, k):
        x, y, c, chips = _mesh_pos()
        source, to, piece = (chips[1], chips[0], 0) if k == 4 else (chips[0], chips[1], 1)
        rows = self._piece(a, 2 * source[0] + source[1], piece)
        return self._copy(a, k, rows, rows, (*to, c))

    def _sibling(self, a, k, h):
        x, y, c, chips = _mesh_pos()
        chip = chips[k - 6]
        slot = _half(self.outs[a].at[2 * chip[0] + chip[1]], h)
        return self._copy(a, k, slot, slot, (x, y, 1 - c))

    def start(self, arrays=None):
        for a in (range(self.n) if arrays is None else arrays):
            for k in range(4):
                self._own(a, k).start()

    def relay(self, a):
        self._own(a, 2).wait_recv()
        self._relay(a, 4).start()
        self._own(a, 0).wait_recv()
        self._relay(a, 5).start()

    def forward(self, a):
        c = lax.axis_index("c")
        self._own(a, 1).wait_recv()
        self._sibling(a, 6, c).start()
        self._own(a, 3).wait_recv()
        self._sibling(a, 7, c).start()
        self._relay(a, 4).wait_recv()
        self._relay(a, 5).wait_recv()
        self._sibling(a, 8, c).start()

    def finish(self, arrays=None):
        c = lax.axis_index("c")
        arrays = range(self.n) if arrays is None else arrays
        for a in arrays:
            for k in range(6, 9):
                self._sibling(a, k, 1 - c).wait_recv()
        for a in arrays:
            for k in range(4):
                self._own(a, k).wait_send()
            for k in range(4, 6):
                self._relay(a, k).wait_send()
            for k in range(6, 9):
                self._sibling(a, k, c).wait_send()


class _ExchangeHalves:
    def __init__(self, ins, recvs, send_sems, recv_sems):
        self.ins, self.recvs, self.send_sems, self.recv_sems = ins, recvs, send_sems, recv_sems

    @staticmethod
    def scratch(n):
        return [pltpu.SemaphoreType.DMA((n,)), pltpu.SemaphoreType.DMA((n,))]

    @staticmethod
    def out_shape(grads):
        return [jax.ShapeDtypeStruct((g.shape[0], g.shape[1] // 2, g.shape[2]), g.dtype) for g in grads]

    def _copies(self):
        x, y, c, _ = _mesh_pos()
        out = []
        for a, (src, dst) in enumerate(zip(self.ins, self.recvs)):
            hr = src.shape[1] // 2
            out.append(pltpu.make_async_remote_copy(
                src_ref=src.at[:, pl.ds((1 - c) * hr, hr), :], dst_ref=dst, send_sem=self.send_sems.at[a],
                recv_sem=self.recv_sems.at[a], device_id=(x, y, 1 - c), device_id_type=MESH))
        return out

    def start(self):
        for cp in self._copies():
            cp.start()

    def finish(self):
        for cp in self._copies():
            cp.wait()


def _exchange_halves(grads):
    n = len(grads)

    def body(*refs):
        ex = _ExchangeHalves(refs[:n], refs[n:2 * n], *refs[2 * n:])
        ex.start()
        ex.finish()

    return pl.pallas_call(
        body, name="grad_exchange_halves", out_shape=_ExchangeHalves.out_shape(grads),
        in_specs=[ANY] * n, out_specs=[ANY] * n, scratch_shapes=_ExchangeHalves.scratch(n),
    )(*grads)


class _ScatterToChips:
    def __init__(self, ins, rbufs, send_sems, recv_sems):
        self.ins, self.rbufs, self.send_sems, self.recv_sems = ins, rbufs, send_sems, recv_sems

    @staticmethod
    def scratch(n):
        return [pltpu.SemaphoreType.DMA((3 * n,)), pltpu.SemaphoreType.DMA((3 * n,))]

    @staticmethod
    def out_shape(sums):
        return [jax.ShapeDtypeStruct((3,) + s.shape[1:], BF16) for s in sums]

    def _copies(self):
        x, y, c, chips = _mesh_pos()
        out = []
        for a, (src, dst) in enumerate(zip(self.ins, self.rbufs)):
            for k, chip in enumerate(chips):
                out.append(pltpu.make_async_remote_copy(
                    src_ref=src.at[2 * chip[0] + chip[1]], dst_ref=dst.at[k], send_sem=self.send_sems.at[3 * a + k],
                    recv_sem=self.recv_sems.at[3 * a + k], device_id=(*chip, c), device_id_type=MESH))
        return out

    def start(self):
        for cp in self._copies():
            cp.start()

    def finish(self):
        for cp in self._copies():
            cp.wait()


def _gather_halves(halves):
    n = len(halves)

    def body(*refs):
        ins, outs = refs[:n], refs[n:2 * n]
        send_sems, recv_sems = refs[2 * n:]
        x, y, c, _ = _mesh_pos()
        sib = (x, y, 1 - c)
        remote = [pltpu.make_async_remote_copy(src_ref=ins[a].at[c], dst_ref=outs[a].at[c],
                                               send_sem=send_sems.at[a], recv_sem=recv_sems.at[a],
                                               device_id=sib, device_id_type=MESH) for a in range(n)]
        for cp in remote:
            cp.start()
        for a in range(n):
            pltpu.make_async_remote_copy(src_ref=ins[a].at[1 - c], dst_ref=outs[a].at[1 - c], send_sem=send_sems.at[a],
                                         recv_sem=recv_sems.at[a], device_id=sib, device_id_type=MESH).wait_recv()
        for cp in remote:
            cp.wait_send()

    return pl.pallas_call(
        body, name="grad_gather_halves",
        out_shape=[jax.ShapeDtypeStruct(h.shape, F32) for h in halves],
        in_specs=[ANY] * n, out_specs=[ANY] * n, input_output_aliases={a: a for a in range(n)},
        scratch_shapes=[pltpu.SemaphoreType.DMA((n,)), pltpu.SemaphoreType.DMA((n,))],
    )(*halves)


SMALL_A_ROWS = 24
SMALL_B_ROWS = 8
SMALL_C_ROWS = N_POOL_GROUPS * POOL_GROUP


class _AllReduceSmall:
    N_IN = 10
    SHAPES = [(SMALL_A_ROWS, D_MODEL), (SMALL_B_ROWS, D_CONV), (SMALL_C_ROWS, POOL_GROUP)]

    def __init__(self, ins, outs, scratch):
        self.ins, self.outs = ins, outs
        self.bufs, self.rcvs, self.send_sems, self.recv_sems = scratch[:3], scratch[3:6], scratch[6], scratch[7]

    @classmethod
    def scratch(cls):
        return ([pltpu.VMEM((3,) + s, F32) for s in cls.SHAPES] + [pltpu.VMEM((3,) + s, F32) for s in cls.SHAPES]
                + [pltpu.SemaphoreType.DMA((9,)), pltpu.SemaphoreType.DMA((9,))])

    @classmethod
    def out_shape(cls):
        return [jax.ShapeDtypeStruct(s, F32) for s in cls.SHAPES]

    def _copies(self, st):
        x, y, c, _ = _mesh_pos()
        peer = [(x, y, 1 - c), (1 - x, y, c), (x, 1 - y, c)][st]
        return [pltpu.make_async_remote_copy(
            src_ref=buf.at[st], dst_ref=rcv.at[st], send_sem=self.send_sems.at[3 * st + i],
            recv_sem=self.recv_sems.at[3 * st + i], device_id=peer, device_id_type=MESH)
            for i, (buf, rcv) in enumerate(zip(self.bufs, self.rcvs))]

    def pack_and_send(self):
        dg1_ref, dg1m_ref, dg2_ref, dg3_ref, dg4_ref, loss_ref, dmeta_ref, dsc_ref, dcw_ref, dpw_ref = self.ins
        a_buf, b_buf, c_buf = self.bufs

        def rowsum(v):
            return jnp.sum(v, axis=0, keepdims=True)

        a_buf[0, 0:1, :] = rowsum(dg1_ref[...] + dg1m_ref[...])
        a_buf[0, 1:2, :] = rowsum(dg2_ref[...])
        a_buf[0, 2:3, :] = rowsum(dg3_ref[...])
        a_buf[0, 3:4, :] = rowsum(dg4_ref[...])
        loss = jnp.sum(rowsum(loss_ref[...]), axis=1, keepdims=True) * (0.5 / D_MODEL)
        a_buf[0, 4:5, :] = jnp.broadcast_to(loss, (1, D_MODEL))
        a_buf[0, 5:8, :] = jnp.zeros((3, D_MODEL), F32)
        a_buf[0, 8:24, :] = dmeta_ref[...]
        b_buf[0, 0:1, :] = rowsum(dsc_ref[...])
        for k in range(3):
            b_buf[0, 1 + k:2 + k, :] = rowsum(dcw_ref[8 * k:8 * k + 8, :])
        b_buf[0, 4:8, :] = jnp.zeros((4, D_CONV), F32)
        c_buf[0] = dpw_ref[...]
        for cp in self._copies(0):
            cp.start()

    def combine(self, st):
        for cp in self._copies(st):
            cp.wait()
        if st < 2:
            for buf, rcv in zip(self.bufs, self.rcvs):
                buf[st + 1] = buf[st] + rcv[st]
            for cp in self._copies(st + 1):
                cp.start()
        else:
            for out, buf, rcv in zip(self.outs, self.bufs, self.rcvs):
                out[...] = buf[st] + rcv[st]


def _row_block(rows):
    for cand in (512, 448, 384, 352, 320, 256, 128, 64, 32, 16):
        if rows % cand == 0:
            return cand
    return rows


def _add_pairs_multi(grads, recvs, place):
    n = len(grads)
    n_sh = grads[0].shape[0]
    halves = [g.shape[1] // 2 for g in grads]
    n_steps = halves[0] // _row_block(halves[0])
    blocks = [(hr // n_steps, g.shape[2]) for hr, g in zip(halves, grads)]

    def body(place_ref, *refs):
        for a_ref, b_ref, o_ref in zip(refs[:n], refs[n:2 * n], refs[2 * n:]):
            o_ref[...] = (a_ref[0] + b_ref[...]).astype(BF16)

    return pl.pallas_call(
        body, name="grad_add_pairs",
        grid_spec=pltpu.PrefetchScalarGridSpec(
            num_scalar_prefetch=1, grid=(n_sh, n_steps),
            in_specs=[pl.BlockSpec((1, 1, br, cols), lambda j, i, p: (j, p[1], i, 0)) for br, cols in blocks]
            + [pl.BlockSpec((1, br, cols), lambda j, i, p: (j, i, 0)) for br, cols in blocks],
            out_specs=[pl.BlockSpec((1, br, cols), lambda j, i, p: (j, i, 0)) for br, cols in blocks]),
        out_shape=[jax.ShapeDtypeStruct((n_sh, hr, g.shape[2]), BF16) for hr, g in zip(halves, grads)],
        compiler_params=_cparams(2),
    )(place, *[g.reshape(n_sh, 2, hr, g.shape[2]) for hr, g in zip(halves, grads)], *recvs)


def _add_pairs(grad, recv, place):
    return _add_pairs_multi([grad], [recv], place)[0]


def _add_chips(grads, recvs, rbufs, place, scattered=(), name="grad_add_chips"):
    n, n_sc = len(grads), len(scattered)
    n_sh = grads[0].shape[0]
    halves = [g.shape[1] // 2 for g in grads]
    n_steps = halves[0] // _row_block(halves[0])
    blocks = [(hr // n_steps, g.shape[2]) for hr, g in zip(halves, grads)]

    def body(place_ref, *refs):
        a_refs, b_refs, r_refs = refs[:n], refs[n:2 * n], refs[2 * n:3 * n]
        o_refs = refs[3 * n + n_sc:4 * n + n_sc]
        if n_sc:
            scatter = _ScatterToChips(refs[3 * n:3 * n + n_sc], refs[4 * n + n_sc:4 * n + 2 * n_sc], *refs[-2:])

            @pl.when(pl.program_id(0) == 0)
            def _():
                scatter.start()

        for a_ref, b_ref, r_ref, o_ref in zip(a_refs, b_refs, r_refs, o_refs):
            own = a_ref[0, 0] + b_ref[0]
            o_ref[0] = ((own + r_ref[0].astype(F32)) + r_ref[1].astype(F32)) + r_ref[2].astype(F32)

        if n_sc:
            @pl.when(pl.program_id(0) == n_steps - 1)
            def _():
                scatter.finish()

    outs = pl.pallas_call(
        body, name=name,
        grid_spec=pltpu.PrefetchScalarGridSpec(
            num_scalar_prefetch=1, grid=(n_steps,),
            in_specs=[pl.BlockSpec((1, 1, br, cols), lambda i, p: (p[0], p[1], i, 0)) for br, cols in blocks]
            + [pl.BlockSpec((1, br, cols), lambda i, p: (p[0], i, 0)) for br, cols in blocks]
            + [pl.BlockSpec((3, br, cols), lambda i, p: (0, i, 0)) for br, cols in blocks] + [ANY] * n_sc,
            out_specs=[pl.BlockSpec((1, br, cols), lambda i, p: (p[1], i, 0)) for br, cols in blocks] + [ANY] * n_sc,
            scratch_shapes=_ScatterToChips.scratch(n_sc) if n_sc else []),
        out_shape=[jax.ShapeDtypeStruct((2, hr, g.shape[2]), F32) for hr, g in zip(halves, grads)]
        + _ScatterToChips.out_shape(list(scattered)),
        compiler_params=_cparams(1),
    )(place, *[g.reshape(n_sh, 2, hr, g.shape[2]) for hr, g in zip(halves, grads)], *recvs, *rbufs, *scattered)
    return outs[:n], outs[n:]


def _adamw_math(w, g, m, v):
    m2 = ADAM_B1 * m + (1.0 - ADAM_B1) * g
    v2 = ADAM_B2 * v + (1.0 - ADAM_B2) * (g * g)
    m_hat = m2 / (1.0 - ADAM_B1 ** ADAM_STEP)
    v_hat = v2 / (1.0 - ADAM_B2 ** ADAM_STEP)
    delta = -ADAM_LR * (m_hat / (jnp.sqrt(v_hat) + ADAM_EPS) + ADAM_WD * w)
    return delta, m2, v2


def _adamw_big(groups):
    n = len(groups)
    rows, cols = groups[0][0].shape
    br = _row_block(rows)
    if n > 1 and br % 16 == 0:
        br //= 2

    def body(*refs):
        for i in range(n):
            w_ref, g_ref, m_ref, v_ref = refs[4 * i:4 * i + 4]
            d_ref, m2_ref, v2_ref = refs[4 * n + 3 * i:4 * n + 3 * i + 3]
            d, m2, v2 = _adamw_math(w_ref[...], g_ref[...], m_ref[...], v_ref[...])
            d_ref[...] = d
            m2_ref[...] = m2
            v2_ref[...] = v2

    spec = pl.BlockSpec((br, cols), lambda i: (i, 0))
    outs = pl.pallas_call(
        body, name="adamw_big", grid=(rows // br,),
        out_shape=[jax.ShapeDtypeStruct((rows, cols), F32)] * (3 * n),
        in_specs=[spec] * (4 * n), out_specs=[spec] * (3 * n), compiler_params=_cparams(1),
    )(*[a for grp in groups for a in grp])
    return [list(outs[3 * i:3 * i + 3]) for i in range(n)]


def _adamw_small(groups):
    n = len(groups)

    def body(*refs):
        ins, outs = refs[:4 * n], refs[4 * n:]
        for i in range(n):
            w, g, m, v = (r[...] for r in ins[4 * i:4 * i + 4])
            d, m2, v2 = _adamw_math(w, g, m, v)
            outs[3 * i][...] = d
            outs[3 * i + 1][...] = m2
            outs[3 * i + 2][...] = v2

    vm = pl.BlockSpec(memory_space=pltpu.VMEM)
    flat = [a for grp in groups for a in grp]
    out_shape = [jax.ShapeDtypeStruct(grp[0].shape, F32) for grp in groups for _ in range(3)]
    outs = pl.pallas_call(body, name="adamw_small", out_shape=out_shape,
                          in_specs=[vm] * (4 * n), out_specs=[vm] * (3 * n))(*flat)
    return [tuple(outs[3 * i:3 * i + 3]) for i in range(n)]


def _load_gathered(gathered, shards, dst_slots, sems):
    n = len(gathered)
    me = 2 * lax.axis_index("x") + lax.axis_index("y")

    def copies(j, own):
        return [pltpu.make_async_copy(shards[a] if own else gathered[a].at[j], dst_slots[a](j), sems.at[n * j + a])
                for a in range(n)]

    for wait in (False, True):
        for j in range(N_CHIPS):
            for own in (False, True):
                @pl.when((me == j) == own)
                def _():
                    for cp in copies(j, own):
                        cp.wait() if wait else cp.start()


N_MIX_SHARDS = 3


def _mixer_fwd(x3, g1, g2, poolw, pscale, shards):
    n_seq, seq, _ = x3.shape
    tm = min(TM_MIX_FWD, seq)
    n_t = seq // tm
    n_steps = n_seq * n_t
    n_ag = len(shards)
    n_ffn = n_ag - N_MIX_SHARDS
    small_rows = shards[2].shape[0]
    conv_cols = D_CONV // N_CHIPS

    def body(x_ref, g1_ref, g2_ref, pw_ref, ps_ref, *rest):
        ag = _AllGather(rest[:n_ag], rest[n_ag + 10:2 * n_ag + 10], *rest[-2:])
        (z_ref, m_ref, h1_ref, a_ref, conv_ref, pooled_ref, yc_ref, zm_ref, meta_ref,
         cw_ref) = rest[n_ag:n_ag + 10]
        win_v, wout_v, small_v, cvb, pb, load_sems = rest[2 * n_ag + 10:-2]
        s, t = pl.program_id(0), pl.program_id(1)
        step = s * n_t + t

        @pl.when(step == 0)
        def _():
            ag.start(range(N_MIX_SHARDS))
            for a in range(N_MIX_SHARDS):
                ag.relay(a)
            for a in range(N_MIX_SHARDS):
                ag.forward(a)
            ag.finish(range(N_MIX_SHARDS))
            ag.start(range(N_MIX_SHARDS, n_ag))
            _load_gathered(ag.outs[:N_MIX_SHARDS], ag.ins[:N_MIX_SHARDS],
                           [lambda j: win_v.at[j], lambda j: wout_v.at[pl.ds(j * OUT_SHARD, OUT_SHARD), :],
                            lambda j: small_v.at[j]], load_sems)

            meta = jnp.concatenate([small_v[j, 0:N_META, :] for j in range(N_CHIPS)], axis=1)
            meta_ref[...] = meta
            cw_ref[...] = jnp.concatenate([small_v[j, N_META:N_META + 3, 0:conv_cols] for j in range(N_CHIPS)], axis=1)
            a_meta = (meta * _rstd(meta) * g1_ref[...]).astype(BF16)
            for j in range(N_CHIPS):
                zm_ref[:, j * IN_SHARD:(j + 1) * IN_SHARD] = _dot(a_meta, win_v[j])

        for i in range(n_ffn):
            @pl.when(step == ((i + 1) * n_steps) // (2 * n_ffn + 2))
            def _():
                ag.relay(N_MIX_SHARDS + i)

        for i in range(n_ffn):
            @pl.when(step == min(n_steps // 2 + ((i + 1) * n_steps) // (2 * n_ffn + 2), n_steps - 1))
            def _():
                ag.forward(N_MIX_SHARDS + i)

        @pl.when(t == 0)
        def _():
            cvb[0:HALO, :] = zm_ref[:, IN_SHARD:2 * IN_SHARD] * zm_ref[:, 2 * IN_SHARD:3 * IN_SHARD]
            pb[0:HALO, :] = zm_ref[:, 3 * IN_SHARD:4 * IN_SHARD]

        @pl.when(t > 0)
        def _():
            cvb[0:HALO, :] = cvb[tm:tm + HALO, :]
            pb[0:HALO, :] = pb[tm:tm + HALO, :]

        xt = x_ref[0]
        a = (xt * _rstd(xt) * g1_ref[...]).astype(BF16)
        a_ref[...] = a
        zb = _dot(a, win_v[0])
        zc = _dot(a, win_v[1])
        zv = _dot(a, win_v[2])
        zp = _dot(a, win_v[3])
        z_ref[0, :, 0:IN_SHARD] = zb
        z_ref[0, :, IN_SHARD:2 * IN_SHARD] = zc
        z_ref[0, :, 2 * IN_SHARD:3 * IN_SHARD] = zv
        cv = zc * zv
        cvb[HALO:HALO + tm, :] = cv
        pb[HALO:HALO + tm, :] = zp
        cw = cw_ref[...]
        conv = cw[0:1] * cvb[HALO - 2:HALO - 2 + tm, :] + cw[1:2] * cvb[HALO - 1:HALO - 1 + tm, :] + cw[2:3] * cv
        conv_ref[...] = conv
        parts = [(zb * conv).astype(BF16)]
        for g in range(N_POOL_GROUPS):
            pooled = _pool_fwd(pb, g, tm).astype(BF16)
            pooled_ref[:, _gcols(g)] = pooled
            parts.append((_dot(pooled, pw_ref[g]) * ps_ref[:, _gcols(g)]).astype(BF16))
        ycat = jnp.concatenate(parts, axis=1)
        yc_ref[...] = ycat
        m = _dot(ycat, wout_v[...])
        m_ref[0] = m
        h1_ref[0] = xt + m * _rstd(m) * g2_ref[...]

        @pl.when(step == n_steps - 1)
        def _():
            ag.finish(range(N_MIX_SHARDS, n_ag))

    n_rows = n_seq * seq
    row = lambda c: pl.BlockSpec((1, tm, c), lambda s, t: (s, t, 0))
    row2 = lambda c: pl.BlockSpec((tm, c), lambda s, t: (s * n_t + t, 0))
    outs = pl.pallas_call(
        body, name="mixer_fwd", grid=(n_seq, n_t),
        out_shape=[jax.ShapeDtypeStruct((n_seq, seq, D_Z), F32), jax.ShapeDtypeStruct((n_seq, seq, D_MODEL), F32),
                   jax.ShapeDtypeStruct((n_seq, seq, D_MODEL), F32), jax.ShapeDtypeStruct((n_rows, D_MODEL), BF16),
                   jax.ShapeDtypeStruct((n_rows, D_CONV), F32), jax.ShapeDtypeStruct((n_rows, D_POOL), BF16),
                   jax.ShapeDtypeStruct((n_rows, D_MODEL), BF16), jax.ShapeDtypeStruct((N_META, D_IN_PROJ), F32),
                   jax.ShapeDtypeStruct((N_META, D_MODEL), F32), jax.ShapeDtypeStruct((3, D_CONV), F32)]
        + _AllGather.out_shape(shards),
        in_specs=[row(D_MODEL), _full((1, D_MODEL)), _full((1, D_MODEL)),
                  _full((N_POOL_GROUPS, POOL_GROUP, POOL_GROUP)), _full((1, D_POOL))] + [ANY] * n_ag,
        out_specs=[row(D_Z), row(D_MODEL), row(D_MODEL), row2(D_MODEL), row2(D_CONV), row2(D_POOL), row2(D_MODEL),
                   _full((N_META, D_IN_PROJ)), _full((N_META, D_MODEL)), _full((3, D_CONV))] + [ANY] * n_ag,
        scratch_shapes=[pltpu.VMEM((N_CHIPS, D_MODEL, IN_SHARD), BF16), pltpu.VMEM((D_MODEL, D_MODEL), BF16),
                        pltpu.VMEM((N_CHIPS, small_rows, D_MODEL // N_CHIPS), F32),
                        pltpu.VMEM((HALO + tm, D_CONV), F32), pltpu.VMEM((HALO + tm, D_POOL), F32),
                        pltpu.SemaphoreType.DMA((N_MIX_SHARDS * N_CHIPS,))] + _AllGather.scratch(n_ag),
        compiler_params=_cparams(2),
    )(x3, g1, g2, poolw, pscale, *shards)
    return outs[:10], outs[10:]


def _ffn_chunks():
    out, r0 = [], 0
    while r0 < D_FF:
        out.append((r0, min(FF_CHUNK, D_FF - r0)))
        r0 += FF_CHUNK
    return out


def _ffn_fwd_bwd(h1, target, g3, g4, gathered, shards):
    n_rows = h1.shape[0]
    tm = min(TM_FFN, n_rows)
    chunks = _ffn_chunks()

    def body(h1_ref, t_ref, g3_ref, g4_ref, wg_all, wu_all, wd_all, wg_s, wu_s, wd_s,
             dh1_ref, f_ref, dd_ref, ds_ref, du_ref, gg_ref, loss_ref, dg3_ref, dg4_ref,
             wg_v, wu_v, wd_v, s_sc, u_sc, sems):
        @pl.when(pl.program_id(0) == 0)
        def _():
            _load_gathered([wg_all, wu_all, wd_all], [wg_s, wu_s, wd_s],
                           [functools.partial(lambda v, j: v.at[pl.ds(j * FF_SHARD, FF_SHARD), :], v)
                            for v in (wg_v, wu_v, wd_v)], sems)
            loss_ref[...] = jnp.zeros_like(loss_ref)
            dg3_ref[...] = jnp.zeros_like(dg3_ref)
            dg4_ref[...] = jnp.zeros_like(dg4_ref)

        h1v = h1_ref[...]
        r3 = _rstd(h1v)
        hh = h1v * r3
        g3v, g4v = g3_ref[...], g4_ref[...]
        f = (hh * g3v).astype(BF16)
        f_ref[...] = f
        d = jnp.zeros((tm, D_MODEL), F32)
        for r0, sz in chunks:
            s = _dot_nt(f, wg_v[r0:r0 + sz, :])
            u = _dot_nt(f, wu_v[r0:r0 + sz, :])
            s_sc[:, r0:r0 + sz] = s
            u_sc[:, r0:r0 + sz] = u
            gc = (s * _sigmoid(s) * u).astype(BF16)
            gg_ref[:, r0:r0 + sz] = gc
            d = d + _dot(gc, wd_v[r0:r0 + sz, :])
        r4 = _rstd(d)
        dh = d * r4
        err = (h1v + dh * g4v) - t_ref[...]
        loss_ref[...] += _rows8(err * err)
        dy = err * (1.0 / D_MODEL)
        dg4_ref[...] += _rows8(dy * dh)
        ddb = _rms_bwd(dy, dh, r4, g4v).astype(BF16)
        dd_ref[...] = ddb
        df = jnp.zeros((tm, D_MODEL), F32)
        for r0, sz in chunks:
            dgg = _dot_nt(ddb, wd_v[r0:r0 + sz, :])
            s = s_sc[:, r0:r0 + sz]
            u = u_sc[:, r0:r0 + sz]
            sig = _sigmoid(s)
            dsc = (dgg * u * (sig * (1.0 + s * (1.0 - sig)))).astype(BF16)
            duc = (dgg * (s * sig)).astype(BF16)
            ds_ref[:, r0:r0 + sz] = dsc
            du_ref[:, r0:r0 + sz] = duc
            df = df + _dot(dsc, wg_v[r0:r0 + sz, :]) + _dot(duc, wu_v[r0:r0 + sz, :])
        dg3_ref[...] += _rows8(df * hh)
        dh1_ref[...] = dy + _rms_bwd(df, hh, r3, g3v)

    row = pl.BlockSpec((tm, D_MODEL), lambda i: (i, 0))
    ffrow = pl.BlockSpec((tm, D_FF), lambda i: (i, 0))
    acc = _full((8, D_MODEL))
    act_bf = jax.ShapeDtypeStruct((n_rows, D_MODEL), BF16)
    ff_bf = jax.ShapeDtypeStruct((n_rows, D_FF), BF16)
    acc_shape = jax.ShapeDtypeStruct((8, D_MODEL), F32)
    w_vmem = pltpu.VMEM((D_FF, D_MODEL), BF16)
    return pl.pallas_call(
        body, name="ffn_fwd_bwd", grid=(n_rows // tm,),
        out_shape=[jax.ShapeDtypeStruct((n_rows, D_MODEL), F32), act_bf, act_bf, ff_bf, ff_bf, ff_bf,
                   acc_shape, acc_shape, acc_shape],
        in_specs=[row, row, _full((1, D_MODEL)), _full((1, D_MODEL))] + [ANY] * 6,
        out_specs=[row, row, row, ffrow, ffrow, ffrow, acc, acc, acc],
        scratch_shapes=[w_vmem, w_vmem, w_vmem, pltpu.VMEM((tm, D_FF), F32), pltpu.VMEM((tm, D_FF), F32),
                        pltpu.SemaphoreType.DMA((3 * N_CHIPS,))],
        compiler_params=_cparams(1),
    )(h1, target, g3, g4, *gathered, *shards)


def _ffn_weight_grads(name, acts, other, exchanged):
    n_rows = other.shape[0]
    n_a, n_ex = len(acts), len(exchanged)
    n_c = n_a
    tk = min(TK_DW, n_rows)
    n_k = n_rows // tk
    half = D_FF // n_c

    def body(other_ref, *rest):
        act_refs = rest[:n_a]
        out_refs = rest[n_a + n_ex:2 * n_a + n_ex]
        c, k = pl.program_id(0), pl.program_id(1)
        if n_ex:
            ex = _ExchangeHalves(rest[n_a:n_a + n_ex], rest[2 * n_a + n_ex:2 * n_a + 2 * n_ex], *rest[-2:])

            @pl.when((c == 0) & (k == 0))
            def _():
                ex.start()

        @pl.when(k == 0)
        def _():
            for o in out_refs:
                o[...] = jnp.zeros_like(o)

        ov = other_ref[...]
        for a, o in zip(act_refs, out_refs):
            o[...] += _dot_tn(a[...], ov)

        if n_ex:
            @pl.when((c == n_c - 1) & (k == n_k - 1))
            def _():
                ex.finish()

    row = pl.BlockSpec((tk, D_MODEL), lambda c, k: (k, 0))
    ffrow = pl.BlockSpec((tk, half), lambda c, k: (k, c))
    out = pl.BlockSpec((half, D_MODEL), lambda c, k: (c, 0))
    outs = pl.pallas_call(
        body, name=name, grid=(n_c, n_k),
        out_shape=[jax.ShapeDtypeStruct((D_FF, D_MODEL), F32)] * n_a + _ExchangeHalves.out_shape(exchanged),
        in_specs=[row] + [ffrow] * n_a + [ANY] * n_ex, out_specs=[out] * n_a + [ANY] * n_ex,
        scratch_shapes=_ExchangeHalves.scratch(n_ex) if n_ex else [],
        compiler_params=_cparams(2),
    )(other, *acts, *exchanged)
    return outs[:n_a], outs[n_a:]


def _mixer_bwd(dh1, m3, z3, conv2, pooled2, x3, zmeta, meta_full, g1, g2, convw, poolw, pscale, gathered, shards,
               exchanged, scattered):
    n_seq, seq, _ = x3.shape
    tm = min(TM_MIX_BWD, seq)
    sub = min(SUB_MIX_BWD, tm)
    n_t = seq // tm
    n_ex, n_sc = len(exchanged), len(scattered)
    n_cm = n_ex + n_sc
    n_out = 13

    def body(dh1_ref, m_ref, z_ref, conv_ref, pooled_ref, x_ref, zm_ref, meta_ref, g1_ref, g2_ref, cw_ref, pw_ref,
             ps_ref, win_all, wout_all, win_s, wout_s, *rest):
        outs0 = n_cm + n_out
        ex = _ExchangeHalves(rest[:n_ex], rest[outs0:outs0 + n_ex], *rest[-4:-2])
        sc = _ScatterToChips(rest[n_ex:n_cm], rest[outs0 + n_ex:outs0 + n_cm], *rest[-2:])
        (dx_ref, dz_ref, dm_ref, dg1_ref, dg2_ref, dsc_ref, dcw_ref, dpw_ref, dzm_ref, dmeta_ref, dg1m_ref, am_ref,
         dzmb_ref) = rest[n_cm:outs0]
        win_v, wout_v, dcb, dqb, mcb, mqb, load_sems = rest[outs0 + n_cm:-4]
        s, i = pl.program_id(0), pl.program_id(1)
        tr = n_t - 1 - i

        @pl.when((s == 0) & (i == 0))
        def _():
            sc.start()
            ex.start()
            _load_gathered([win_all, wout_all], [win_s, wout_s],
                           [lambda j: win_v.at[j], lambda j: wout_v.at[pl.ds(j * OUT_SHARD, OUT_SHARD), :]], load_sems)
            for ref in (dg1_ref, dg2_ref, dsc_ref, dcw_ref, dpw_ref, dzm_ref):
                ref[...] = jnp.zeros_like(ref)

        @pl.when(i == 0)
        def _():
            dcb[tm:tm + HALO, :] = jnp.zeros((HALO, D_CONV), F32)
            dqb[tm:tm + HALO, :] = jnp.zeros((HALO, D_POOL), F32)

        @pl.when(i > 0)
        def _():
            dcb[tm:tm + HALO, :] = dcb[0:HALO, :]
            dqb[tm:tm + HALO, :] = dqb[0:HALO, :]

        g1v, g2v = g1_ref[...], g2_ref[...]
        cw = cw_ref[...]

        for r0 in range(tm - sub, -1, -sub):
            rows = slice(r0, r0 + sub)
            dh1v = dh1_ref[0, rows, :]
            mv = m_ref[0, rows, :]
            r2 = _rstd(mv)
            mh = mv * r2
            dg2_ref[...] += _rows8(dh1v * mh)
            dmb = _rms_bwd(dh1v, mh, r2, g2v).astype(BF16)
            dm_ref[rows, :] = dmb
            dyc = _dot_nt(dmb, wout_v[...])
            dyconv = dyc[:, 0:D_CONV]

            for g in range(N_POOL_GROUPS):
                pooled = pooled_ref[rows, _gcols(g)]
                mixed = _dot(pooled, pw_ref[g])
                scale = ps_ref[:, _gcols(g)]
                dyp = dyc[:, D_CONV + g * POOL_GROUP:D_CONV + (g + 1) * POOL_GROUP]
                dsc_ref[:, _gcols(g)] += _rows8(dyp * mixed)
                dmix = (dyp * scale).astype(BF16)
                dpw_ref[g] += _dot_tn(pooled, dmix)
                dqb[rows, _gcols(g)] = _dot_nt(dmix, pw_ref[g])

            zb = z_ref[0, rows, 0:IN_SHARD]
            zc = z_ref[0, rows, IN_SHARD:2 * IN_SHARD]
            zv = z_ref[0, rows, 2 * IN_SHARD:3 * IN_SHARD]
            dconv = dyconv * zb
            dcb[rows, :] = dconv
            d1 = dcb[r0 + 1:r0 + 1 + sub, :]
            d2 = dcb[r0 + 2:r0 + 2 + sub, :]
            dcv = cw[2:3] * dconv + cw[1:2] * d1 + cw[0:1] * d2
            cv = zc * zv
            dcw_ref[0:8, :] += _rows8(cv * d2)
            dcw_ref[8:16, :] += _rows8(cv * d1)
            dcw_ref[16:24, :] += _rows8(cv * dconv)
            dzs = [(dyconv * conv_ref[rows, :]).astype(BF16), (dcv * zv).astype(BF16), (dcv * zc).astype(BF16),
                   jnp.concatenate([_pool_bwd(dqb, g, r0, sub) for g in range(N_POOL_GROUPS)], axis=1).astype(BF16)]
            da = jnp.zeros((sub, D_MODEL), F32)
            for j in range(N_CHIPS):
                dz_ref[j, rows, :] = dzs[j]
                da = da + _dot_nt(dzs[j], win_v[j])
            xt = x_ref[0, rows, :]
            r1 = _rstd(xt)
            xh = xt * r1
            dg1_ref[...] += _rows8(da * xh)
            dx_ref[0, rows, :] = dh1v + _rms_bwd(da, xh, r1, g1v)

        @pl.when(tr == 0)
        def _():
            mcb[0:HALO, :] = jnp.zeros((HALO, D_CONV), F32)
            mqb[0:HALO, :] = jnp.zeros((HALO, D_POOL), F32)
            mcb[HALO:2 * HALO, :] = dcb[0:HALO, :]
            mqb[HALO:2 * HALO, :] = dqb[0:HALO, :]
            m1 = mcb[1:1 + HALO, :]
            m2 = mcb[2:2 + HALO, :]
            zc_m = zm_ref[:, IN_SHARD:2 * IN_SHARD]
            zv_m = zm_ref[:, 2 * IN_SHARD:3 * IN_SHARD]
            cv_m = zc_m * zv_m
            dcw_ref[0:8, :] += _rows8(cv_m * m2)
            dcw_ref[8:16, :] += _rows8(cv_m * m1)
            dcv_m = cw[1:2] * m1 + cw[0:1] * m2
            dzm_ref[:, IN_SHARD:2 * IN_SHARD] += dcv_m * zv_m
            dzm_ref[:, 2 * IN_SHARD:3 * IN_SHARD] += dcv_m * zc_m
            dzm_ref[:, 3 * IN_SHARD:4 * IN_SHARD] += jnp.concatenate(
                [_pool_bwd(mqb, g, 0, HALO) for g in range(N_POOL_GROUPS)], axis=1)

        @pl.when((s == n_seq - 1) & (i == n_t - 1))
        def _():
            xm = meta_ref[...]
            rm = _rstd(xm)
            xmh = xm * rm
            am_ref[...] = (xmh * g1v).astype(BF16)
            da_m = jnp.zeros((N_META, D_MODEL), F32)
            for j in range(N_CHIPS):
                dzj = dzm_ref[:, j * IN_SHARD:(j + 1) * IN_SHARD].astype(BF16)
                dzmb_ref[j] = dzj
                da_m = da_m + _dot_nt(dzj, win_v[j])
            dg1m_ref[...] = _rows8(da_m * xmh)
            dmeta_ref[...] = _rms_bwd(da_m, xmh, rm, g1v)
            ex.finish()
            sc.finish()

    row3 = lambda c: pl.BlockSpec((1, tm, c), lambda s, i: (s, n_t - 1 - i, 0))
    row2 = lambda c: pl.BlockSpec((tm, c), lambda s, i: (s * n_t + n_t - 1 - i, 0))
    n_rows = n_seq * seq
    outs = pl.pallas_call(
        body, name="mixer_bwd", grid=(n_seq, n_t),
        out_shape=[jax.ShapeDtypeStruct((n_seq, seq, D_MODEL), F32),
                   jax.ShapeDtypeStruct((N_CHIPS, n_rows, IN_SHARD), BF16), jax.ShapeDtypeStruct((n_rows, D_MODEL), BF16),
                   jax.ShapeDtypeStruct((8, D_MODEL), F32), jax.ShapeDtypeStruct((8, D_MODEL), F32),
                   jax.ShapeDtypeStruct((8, D_POOL), F32), jax.ShapeDtypeStruct((24, D_CONV), F32),
                   jax.ShapeDtypeStruct((N_POOL_GROUPS, POOL_GROUP, POOL_GROUP), F32),
                   jax.ShapeDtypeStruct((N_META, D_IN_PROJ), F32),
                   jax.ShapeDtypeStruct((N_META, D_MODEL), F32), jax.ShapeDtypeStruct((8, D_MODEL), F32),
                   jax.ShapeDtypeStruct((N_META, D_MODEL), BF16),
                   jax.ShapeDtypeStruct((N_CHIPS, N_META, IN_SHARD), BF16)]
        + _ExchangeHalves.out_shape(exchanged) + _ScatterToChips.out_shape(scattered),
        in_specs=[row3(D_MODEL), row3(D_MODEL), row3(D_Z), row2(D_CONV), row2(D_POOL), row3(D_MODEL),
                  _full((N_META, D_IN_PROJ)), _full((N_META, D_MODEL)), _full((1, D_MODEL)), _full((1, D_MODEL)),
                  _full((3, D_CONV)), _full((N_POOL_GROUPS, POOL_GROUP, POOL_GROUP)), _full((1, D_POOL))]
        + [ANY] * (4 + n_cm),
        out_specs=[row3(D_MODEL), pl.BlockSpec((N_CHIPS, tm, IN_SHARD), lambda s, i: (0, s * n_t + n_t - 1 - i, 0)),
                   row2(D_MODEL),
                   _full((8, D_MODEL)), _full((8, D_MODEL)), _full((8, D_POOL)), _full((24, D_CONV)),
                   _full((N_POOL_GROUPS, POOL_GROUP, POOL_GROUP)), _full((N_META, D_IN_PROJ)),
                   _full((N_META, D_MODEL)), _full((8, D_MODEL)), _full((N_META, D_MODEL)),
                   _full((N_CHIPS, N_META, IN_SHARD))] + [ANY] * n_cm,
        scratch_shapes=[pltpu.VMEM((N_CHIPS, D_MODEL, IN_SHARD), BF16), pltpu.VMEM((D_MODEL, D_MODEL), BF16),
                        pltpu.VMEM((tm + HALO, D_CONV), F32), pltpu.VMEM((tm + HALO, D_POOL), F32),
                        pltpu.VMEM((2 * HALO, D_CONV), F32), pltpu.VMEM((2 * HALO, D_POOL), F32),
                        pltpu.SemaphoreType.DMA((2 * N_CHIPS,))]
        + _ExchangeHalves.scratch(n_ex) + _ScatterToChips.scratch(n_sc),
        compiler_params=_cparams(2),
    )(dh1, m3, z3, conv2, pooled2, x3, zmeta, meta_full, g1, g2, convw, poolw, pscale, *gathered, *shards,
      *exchanged, *scattered)
    return outs[:n_out], outs[n_out:n_out + n_ex], outs[n_out + n_ex:]


def _mixer_weight_grads(a, dz, ycat, dm, a_meta, dz_meta, ffn_sums, small):
    n_rows = a.shape[0]
    tk = min(TK_DW, n_rows)
    n_k = n_rows // tk
    n_sc, n_sm = len(ffn_sums), _AllReduceSmall.N_IN

    def body(a_ref, dz_ref, yc_ref, dm_ref, am_ref, dzm_ref, *rest):
        ins, outs, scratch = rest[:n_sc + n_sm], rest[n_sc + n_sm:2 * n_sc + n_sm + 5], rest[2 * n_sc + n_sm + 5:]
        dwin_ref, dwout_ref = outs[:2]
        scatter = _ScatterToChips(ins[:n_sc], outs[2:2 + n_sc], *scratch[:2])
        reduce_small = _AllReduceSmall(ins[n_sc:], outs[2 + n_sc:], scratch[2:])
        k = pl.program_id(0)

        @pl.when(k == 0)
        def _():
            scatter.start()
            reduce_small.pack_and_send()
            am_t = am_ref[...].T
            for j in range(N_CHIPS):
                dwin_ref[j] = _dot(am_t, dzm_ref[j])
            dwout_ref[...] = jnp.zeros_like(dwout_ref)

        for st in range(2):
            @pl.when(k == ((st + 1) * n_k) // 3)
            def _():
                reduce_small.combine(st)

        a_t = a_ref[...].T
        for j in range(N_CHIPS):
            dwin_ref[j] += _dot(a_t, dz_ref[j])
        dwout_ref[...] += _dot_tn(yc_ref[...], dm_ref[...])

        @pl.when(k == n_k - 1)
        def _():
            reduce_small.combine(2)
            scatter.finish()

    row = pl.BlockSpec((tk, D_MODEL), lambda k: (k, 0))
    outs = pl.pallas_call(
        body, name="mixer_weight_grads", grid=(n_k,),
        out_shape=[jax.ShapeDtypeStruct((N_CHIPS, D_MODEL, IN_SHARD), F32),
                   jax.ShapeDtypeStruct((D_MODEL, D_MODEL), F32)] + _ScatterToChips.out_shape(ffn_sums)
        + _AllReduceSmall.out_shape(),
        in_specs=[row, pl.BlockSpec((N_CHIPS, tk, IN_SHARD), lambda k: (0, k, 0)), row, row,
                  _full((N_META, D_MODEL)), _full((N_CHIPS, N_META, IN_SHARD))] + [ANY] * n_sc
        + [_full(s.shape) for s in small],
        out_specs=[_full((N_CHIPS, D_MODEL, IN_SHARD)), _full((D_MODEL, D_MODEL))] + [ANY] * n_sc
        + [_full(s) for s in _AllReduceSmall.SHAPES],
        scratch_shapes=_ScatterToChips.scratch(n_sc) + _AllReduceSmall.scratch(),
        compiler_params=_cparams(1),
    )(a, dz, ycat, dm, a_meta, dz_meta, *ffn_sums, *small)
    return ([outs[0], outs[1].reshape(N_CHIPS, OUT_SHARD, D_MODEL)], outs[2:2 + n_sc], outs[2 + n_sc:])


def kernel(x, meta_tokens, norm_mix_pre, w_in, conv_w, pool_w, pool_scale, w_out, norm_mix_post, norm_ffn_pre, w_gate, w_up, w_down, norm_ffn_post, loss_target, m_meta_tokens, m_norm_mix_pre, m_w_in, m_conv_w, m_pool_w, m_pool_scale, m_w_out, m_norm_mix_post, m_norm_ffn_pre, m_w_gate, m_w_up, m_w_down, m_norm_ffn_post, v_meta_tokens, v_norm_mix_pre, v_w_in, v_conv_w, v_pool_w, v_pool_scale, v_w_out, v_norm_mix_post, v_norm_ffn_pre, v_w_gate, v_w_up, v_w_down, v_norm_ffn_post):
    n_seq, seq, _ = x.shape
    n_rows = n_seq * seq
    chip = 2 * lax.axis_index("x") + lax.axis_index("y")
    meta_cols = D_MODEL // N_CHIPS
    conv_cols = D_CONV // N_CHIPS

    small = jnp.zeros((2 * HALO, meta_cols), F32)
    small = small.at[0:N_META, :].set(meta_tokens).at[N_META:N_META + 3, 0:conv_cols].set(conv_w[0])
    poolw_bf = pool_w[0].astype(BF16)
    pscale = pool_scale
    g1, g2, g3, g4 = norm_mix_pre, norm_mix_post, norm_ffn_pre, norm_ffn_post
    place = jnp.stack([chip, lax.axis_index("c")]).astype(jnp.int32)

    mix_shards = [w_in[0].astype(BF16), w_out[0].astype(BF16)]
    ffn_shards = [w_gate[0].T.astype(BF16), w_up[0].T.astype(BF16), w_down[0].astype(BF16)]
    ((z3, m3, h1, a_bf, conv2, pooled2, yc_bf, zmeta, meta_full, conv_full),
     (win_all, wout_all, _, *ffn_gathered)) = _mixer_fwd(x, g1, g2, poolw_bf, pscale, mix_shards + [small] + ffn_shards)
    dh1, f_bf, dd_bf, ds_bf, du_bf, gg_bf, lossp, dg3p, dg4p = _ffn_fwd_bwd(
        h1.reshape(n_rows, D_MODEL), loss_target.reshape(n_rows, D_MODEL), g3, g4, ffn_gathered, ffn_shards)
    as_shards = lambda g: g.reshape(N_CHIPS, FF_SHARD, D_MODEL)
    (dwg_t, dwu_t), _ = _ffn_weight_grads("ffn_weight_grads_gate_up", [ds_bf, du_bf], f_bf, [])
    dwg_t, dwu_t = as_shards(dwg_t), as_shards(dwu_t)
    (dwd,), (dwg_recv, dwu_recv) = _ffn_weight_grads("ffn_weight_grads_down", [gg_bf], dd_bf, [dwg_t, dwu_t])
    dwd = as_shards(dwd)
    ((grad_x, dz_bf, dm_bf, dg1p, dg2p, dscp, dcwp, dpw, _, dmeta, dg1m, a_meta, dz_meta), (dwd_recv,),
     (dwg_rbuf, dwu_rbuf)) = _mixer_bwd(
        dh1.reshape(n_seq, seq, D_MODEL), m3, z3, conv2, pooled2, x, zmeta, meta_full, g1, g2, conv_full, poolw_bf,
        pscale, [win_all, wout_all], mix_shards, [dwd],
        _add_pairs_multi([dwg_t, dwu_t], [dwg_recv, dwu_recv], place))
    mix_grads, (dwd_rbuf,), (a_red, b_red, c_red) = _mixer_weight_grads(
        a_bf, dz_bf, yc_bf, dm_bf, a_meta, dz_meta, [_add_pairs(dwd, dwd_recv, place)],
        [dg1p, dg1m, dg2p, dg3p, dg4p, lossp, dmeta, dscp, dcwp, dpw.reshape(SMALL_C_ROWS, POOL_GROUP)])

    mix_recvs = _exchange_halves(mix_grads)
    ffn_red, mix_rbufs = _add_chips([dwg_t, dwu_t, dwd], [dwg_recv, dwu_recv, dwd_recv],
                                    [dwg_rbuf, dwu_rbuf, dwd_rbuf], place,
                                    _add_pairs_multi(mix_grads, mix_recvs, place), name="grad_add_chips_ffn")
    mix_red, _ = _add_chips(mix_grads, mix_recvs, mix_rbufs, place)
    reduced = _gather_halves(list(mix_red) + list(ffn_red))
    g_win, g_wout, g_wg_t, g_wu_t, g_wd = [r.reshape(2 * r.shape[1], r.shape[2]) for r in reduced]

    loss = a_red[4, 0]
    g_g1, g_g2, g_g3, g_g4 = a_red[0:1], a_red[1:2], a_red[2:3], a_red[3:4]
    g_meta = lax.dynamic_slice(a_red, (8, chip * meta_cols), (N_META, meta_cols))
    g_pscale = b_red[0:1]
    g_conv = lax.dynamic_slice(b_red, (1, chip * conv_cols), (3, conv_cols))
    g_poolw = c_red

    big = [(w_in[0], g_win, m_w_in[0], v_w_in[0]), (w_out[0], g_wout, m_w_out[0], v_w_out[0]),
           (w_gate[0].T, g_wg_t, m_w_gate[0].T, v_w_gate[0].T), (w_up[0].T, g_wu_t, m_w_up[0].T, v_w_up[0].T),
           (w_down[0], g_wd, m_w_down[0], v_w_down[0])]
    big_out = _adamw_big(big[0:1]) + _adamw_big(big[1:2]) + _adamw_big(big[2:5])
    big_out[2] = [o.T for o in big_out[2]]
    big_out[3] = [o.T for o in big_out[3]]
    g_wg, g_wu = g_wg_t.T, g_wu_t.T
    small_groups = [
        (meta_tokens, g_meta, m_meta_tokens, v_meta_tokens),
        (g1, g_g1, m_norm_mix_pre, v_norm_mix_pre),
        (conv_w[0], g_conv, m_conv_w[0], v_conv_w[0]),
        (pool_w.reshape(SMALL_C_ROWS, POOL_GROUP), g_poolw, m_pool_w.reshape(SMALL_C_ROWS, POOL_GROUP),
         v_pool_w.reshape(SMALL_C_ROWS, POOL_GROUP)),
        (pool_scale, g_pscale, m_pool_scale, v_pool_scale),
        (g2, g_g2, m_norm_mix_post, v_norm_mix_post),
        (g3, g_g3, m_norm_ffn_pre, v_norm_ffn_pre),
        (g4, g_g4, m_norm_ffn_post, v_norm_ffn_post),
    ]
    small_out = _adamw_small(small_groups)

    grads_out = [g_meta, g_g1, g_win[None], g_conv[None], g_poolw.reshape(pool_w.shape), g_pscale, g_wout[None],
                 g_g2, g_g3, g_wg[None], g_wu[None], g_wd[None], g_g4]
    s_meta, s_g1, s_conv, s_poolw, s_pscale, s_g2, s_g3, s_g4 = small_out
    b_win, b_wout, b_wg, b_wu, b_wd = big_out

    def leaf(k):
        return [s_meta[k], s_g1[k], b_win[k][None], s_conv[k][None], s_poolw[k].reshape(pool_w.shape), s_pscale[k],
                b_wout[k][None], s_g2[k], s_g3[k], b_wg[k][None], b_wu[k][None], b_wd[k][None], s_g4[k]]

    return (loss, grad_x, *grads_out, *leaf(0), *leaf(1), *leaf(2))
```

```python
import functools

import jax
import jax.numpy as jnp
from jax import lax
from jax.experimental import pallas as pl
from jax.experimental.pallas import tpu as pltpu

F32 = jnp.float32
BF16 = jnp.bfloat16
MESH = pl.DeviceIdType.MESH

D_MODEL = 1024
D_CONV = 512
D_POOL = 512
POOL_GROUP = 128
N_POOL_GROUPS = 4
D_IN_PROJ = 2048
D_FF = 2816
N_CHIPS = 4
FF_SHARD = D_FF // N_CHIPS
IN_SHARD = D_IN_PROJ // N_CHIPS
OUT_SHARD = D_MODEL // N_CHIPS
D_Z = 3 * IN_SHARD
N_META = 16
HALO = 16
RMS_EPS = 1e-6

ADAM_LR = 0.001
ADAM_B1 = 0.9
ADAM_B2 = 0.999
ADAM_EPS = 1e-08
ADAM_WD = 0.01
ADAM_STEP = 10

TM_MIX_FWD = 512
TM_MIX_BWD = 512
SUB_MIX_BWD = 512
TM_FFN = 256
TK_DW = 1024
FF_CHUNK = 1024
VMEM_LIMIT = 56 * 1024 * 1024


def _cparams(n_grid):
    return pltpu.CompilerParams(dimension_semantics=("arbitrary",) * n_grid, vmem_limit_bytes=VMEM_LIMIT)


def _dot(a, b):
    return jnp.dot(a, b, preferred_element_type=F32)


def _dot_nt(a, b):
    return lax.dot_general(a, b, (((1,), (1,)), ((), ())), preferred_element_type=F32)


def _dot_tn(a, b):
    return lax.dot_general(a, b, (((0,), (0,)), ((), ())), preferred_element_type=F32)


def _rows8(v):
    r, c = v.shape
    return v.reshape(r // 8, 8, c).sum(axis=0)


def _rstd(v):
    return lax.rsqrt(jnp.mean(v * v, axis=-1, keepdims=True) + RMS_EPS)


def _rms_bwd(dy, xhat, rstd, gain):
    dyg = dy * gain
    return rstd * (dyg - xhat * jnp.mean(dyg * xhat, axis=-1, keepdims=True))


def _sigmoid(v):
    return 1.0 / (1.0 + jnp.exp(-v))


def _gcols(g):
    return slice(g * POOL_GROUP, (g + 1) * POOL_GROUP)


def _window_sum(e, g, ahead):
    n = e.shape[0]
    w = e
    for level in range(g + 1):
        shift = 1 << level
        w = w + pltpu.roll(w, (n - shift) if ahead else shift, 0)
    return w


def _pool_fwd(pb, g, n):
    e = pb[0:HALO + n, _gcols(g)]
    return _window_sum(e, g, False)[HALO:, :] * (1.0 / (2 << g)) - e[HALO:, :]


def _pool_bwd(qb, g, r0, n):
    e = qb[r0:r0 + n + HALO, _gcols(g)]
    return _window_sum(e, g, True)[0:n, :] * (1.0 / (2 << g)) - e[0:n, :]


def _full(shape):
    nd = len(shape)
    return pl.BlockSpec(shape, lambda *_: (0,) * nd)


ANY = pl.BlockSpec(memory_space=pl.ANY)


def _mesh_pos():
    x, y, c = lax.axis_index("x"), lax.axis_index("y"), lax.axis_index("c")
    chips = [(1 - x, y), (x, 1 - y), (1 - x, 1 - y)]
    return x, y, c, chips


def _half(ref, h):
    hr = ref.shape[0] // 2
    return ref.at[pl.ds(h * hr, hr), :]


class _AllGather:
    PER_ARRAY = 9

    def __init__(self, ins, outs, send_sems, recv_sems):
        self.ins, self.outs, self.send_sems, self.recv_sems = ins, outs, send_sems, recv_sems
        self.n = len(ins)

    @classmethod
    def scratch(cls, n):
        return [pltpu.SemaphoreType.DMA((cls.PER_ARRAY * n,)), pltpu.SemaphoreType.DMA((cls.PER_ARRAY * n,))]

    @staticmethod
    def out_shape(shards):
        return [jax.ShapeDtypeStruct((N_CHIPS,) + s.shape, s.dtype) for s in shards]

    def _copy(self, a, k, src, dst, to):
        i = self.PER_ARRAY * a + k
        return pltpu.make_async_remote_copy(src_ref=src, dst_ref=dst, send_sem=self.send_sems.at[i],
                                            recv_sem=self.recv_sems.at[i], device_id=to, device_id_type=MESH)

    def _piece(self, a, chip, piece, h=None):
        h = lax.axis_index("c") if h is None else h
        rows = self.ins[a].shape[0] // 4
        return self.outs[a].at[chip].at[pl.ds((2 * h + piece) * rows, rows), :]

    def _own(self, a, k):
        x, y, c, chips = _mesh_pos()
        piece = (1, 0, 0, 1)[k]
        rows = self.ins[a].shape[0] // 4
        src = self.ins[a].at[pl.ds((2 * c + piece) * rows, rows), :]
        return self._copy(a, k, src, self._piece(a, 2 * x + y, piece), (*chips[k // 2], c))

    def _relay(self, a, k):
        x, y, c, chips = _mesh_pos()
        source, to, piece = (chips[1], chips[0], 0) if k == 4 else (chips[0], chips[1], 1)
        rows = self._piece(a, 2 * source[0] + source[1], piece)
        return self._copy(a, k, rows, rows, (*to, c))

    def _sibling(self, a, k, h):
        x, y, c, chips = _mesh_pos()
        chip = chips[k - 6]
        slot = _half(self.outs[a].at[2 * chip[0] + chip[1]], h)
        return self._copy(a, k, slot, slot, (x, y, 1 - c))

    def start(self, arrays=None):
        for a in (range(self.n) if arrays is None else arrays):
            for k in range(4):
                self._own(a, k).start()

    def relay(self, a):
        self._own(a, 2).wait_recv()
        self._relay(a, 4).start()
        self._own(a, 0).wait_recv()
        self._relay(a, 5).start()

    def forward(self, a):
        c = lax.axis_index("c")
        self._own(a, 1).wait_recv()
        self._sibling(a, 6, c).start()
        self._own(a, 3).wait_recv()
        self._sibling(a, 7, c).start()
        self._relay(a, 4).wait_recv()
        self._relay(a, 5).wait_recv()
        self._sibling(a, 8, c).start()

    def finish(self, arrays=None):
        c = lax.axis_index("c")
        arrays = range(self.n) if arrays is None else arrays
        for a in arrays:
            for k in range(6, 9):
                self._sibling(a, k, 1 - c).wait_recv()
        for a in arrays:
            for k in range(4):
                self._own(a, k).wait_send()
            for k in range(4, 6):
                self._relay(a, k).wait_send()
            for k in range(6, 9):
                self._sibling(a, k, c).wait_send()


class _ExchangeHalves:
    def __init__(self, ins, recvs, send_sems, recv_sems):
        self.ins, self.recvs, self.send_sems, self.recv_sems = ins, recvs, send_sems, recv_sems

    @staticmethod
    def scratch(n):
        return [pltpu.SemaphoreType.DMA((n,)), pltpu.SemaphoreType.DMA((n,))]

    @staticmethod
    def out_shape(grads):
        return [jax.ShapeDtypeStruct((g.shape[0], g.shape[1] // 2, g.shape[2]), g.dtype) for g in grads]

    def _copies(self):
        x, y, c, _ = _mesh_pos()
        out = []
        for a, (src, dst) in enumerate(zip(self.ins, self.recvs)):
            hr = src.shape[1] // 2
            out.append(pltpu.make_async_remote_copy(
                src_ref=src.at[:, pl.ds((1 - c) * hr, hr), :], dst_ref=dst, send_sem=self.send_sems.at[a],
                recv_sem=self.recv_sems.at[a], device_id=(x, y, 1 - c), device_id_type=MESH))
        return out

    def start(self):
        for cp in self._copies():
            cp.start()

    def finish(self):
        for cp in self._copies():
            cp.wait()


def _exchange_halves(grads):
    n = len(grads)

    def body(*refs):
        ex = _ExchangeHalves(refs[:n], refs[n:2 * n], *refs[2 * n:])
        ex.start()
        ex.finish()

    return pl.pallas_call(
        body, name="grad_exchange_halves", out_shape=_ExchangeHalves.out_shape(grads),
        in_specs=[ANY] * n, out_specs=[ANY] * n, scratch_shapes=_ExchangeHalves.scratch(n),
    )(*grads)


class _ScatterToChips:
    def __init__(self, ins, rbufs, send_sems, recv_sems):
        self.ins, self.rbufs, self.send_sems, self.recv_sems = ins, rbufs, send_sems, recv_sems

    @staticmethod
    def scratch(n):
        return [pltpu.SemaphoreType.DMA((3 * n,)), pltpu.SemaphoreType.DMA((3 * n,))]

    @staticmethod
    def out_shape(sums):
        return [jax.ShapeDtypeStruct((3,) + s.shape[1:], BF16) for s in sums]

    def _copies(self):
        x, y, c, chips = _mesh_pos()
        out = []
        for a, (src, dst) in enumerate(zip(self.ins, self.rbufs)):
            for k, chip in enumerate(chips):
                out.append(pltpu.make_async_remote_copy(
                    src_ref=src.at[2 * chip[0] + chip[1]], dst_ref=dst.at[k], send_sem=self.send_sems.at[3 * a + k],
                    recv_sem=self.recv_sems.at[3 * a + k], device_id=(*chip, c), device_id_type=MESH))
        return out

    def start(self):
        for cp in self._copies():
            cp.start()

    def finish(self):
        for cp in self._copies():
            cp.wait()


HBM = pl.BlockSpec(memory_space=pltpu.HBM)
SEM = pl.BlockSpec(memory_space=pltpu.SEMAPHORE)


def _split_scatter_copies(srcs, lands, send_sems, recv_sems):
    x, y, c, chips = _mesh_pos()
    return [pltpu.make_async_remote_copy(
        src_ref=src.at[2 * chip[0] + chip[1]], dst_ref=land.at[k], send_sem=send_sems[3 * a + k],
        recv_sem=recv_sems[3 * a + k], device_id=(*chip, c), device_id_type=MESH)
        for a, (src, land) in enumerate(zip(srcs, lands)) for k, chip in enumerate(chips)]


def _scatter_start(sums):
    n = len(sums)

    def body(*refs):
        srcs, lands = refs[:n], refs[n:2 * n]
        send_sems, recv_sems = refs[2 * n:5 * n], refs[5 * n:8 * n]
        for cp in _split_scatter_copies(srcs, lands, send_sems, recv_sems):
            cp.start()
        refs[-1][...] = jnp.zeros_like(refs[-1])

    lands = [lax.empty((3,) + s.shape[1:], BF16) for s in sums]
    outs = pl.pallas_call(
        body, name="grad_scatter_start",
        out_shape=[pltpu.SemaphoreType.DMA(())] * (6 * n) + [pltpu.HBM(s.shape, s.dtype) for s in sums]
        + [pltpu.HBM(l.shape, l.dtype) for l in lands] + [jax.ShapeDtypeStruct((8, 128), F32)],
        in_specs=[HBM] * (2 * n), out_specs=[SEM] * (6 * n) + [HBM] * (2 * n) + [pl.BlockSpec(memory_space=pltpu.VMEM)],
        input_output_aliases={i: 6 * n + i for i in range(2 * n)},
        compiler_params=pltpu.CompilerParams(has_side_effects=pltpu.SideEffectType.DATAFLOW_SIDE_EFFECTING),
    )(*[pltpu.with_memory_space_constraint(v, pltpu.HBM) for v in list(sums) + lands])
    return outs[:6 * n], outs[6 * n:7 * n], outs[7 * n:8 * n], outs[-1]


def _scatter_wait(sems, srcs, lands, after):
    n = len(srcs)

    def body(*refs):
        src_refs, land_refs = refs[:n], refs[n:2 * n]
        send_sems, recv_sems = refs[2 * n:5 * n], refs[5 * n:8 * n]
        for cp in _split_scatter_copies(src_refs, land_refs, send_sems, recv_sems):
            cp.wait_send()
            cp.wait_recv()

    outs = pl.pallas_call(
        body, name="grad_scatter_wait",
        out_shape=[pltpu.HBM(s.shape, s.dtype) for s in srcs] + [pltpu.HBM(l.shape, l.dtype) for l in lands],
        in_specs=[HBM] * (2 * n) + [SEM] * (6 * n) + [ANY], out_specs=[HBM] * (2 * n),
        input_output_aliases={i: i for i in range(2 * n)},
        compiler_params=pltpu.CompilerParams(has_side_effects=pltpu.SideEffectType.DATAFLOW_SIDE_EFFECTING),
    )(*srcs, *lands, *sems, after)
    return outs[n:]


def _gather_halves(halves, name):
    n = len(halves)

    def body(*refs):
        ins, outs = refs[:n], refs[n:2 * n]
        send_sems, recv_sems = refs[2 * n:]
        x, y, c, _ = _mesh_pos()
        sib = (x, y, 1 - c)
        remote = [pltpu.make_async_remote_copy(src_ref=ins[a].at[c], dst_ref=outs[a].at[c],
                                               send_sem=send_sems.at[a], recv_sem=recv_sems.at[a],
                                               device_id=sib, device_id_type=MESH) for a in range(n)]
        for cp in remote:
            cp.start()
        for a in range(n):
            pltpu.make_async_remote_copy(src_ref=ins[a].at[1 - c], dst_ref=outs[a].at[1 - c], send_sem=send_sems.at[a],
                                         recv_sem=recv_sems.at[a], device_id=sib, device_id_type=MESH).wait_recv()
        for cp in remote:
            cp.wait_send()

    return pl.pallas_call(
        body, name=name,
        out_shape=[jax.ShapeDtypeStruct(h.shape, F32) for h in halves],
        in_specs=[ANY] * n, out_specs=[ANY] * n, input_output_aliases={a: a for a in range(n)},
        scratch_shapes=[pltpu.SemaphoreType.DMA((n,)), pltpu.SemaphoreType.DMA((n,))],
    )(*halves)


SMALL_A_ROWS = 24
SMALL_B_ROWS = 8
SMALL_C_ROWS = N_POOL_GROUPS * POOL_GROUP


class _AllReduceSmall:
    N_IN = 10
    SHAPES = [(SMALL_A_ROWS, D_MODEL), (SMALL_B_ROWS, D_CONV), (SMALL_C_ROWS, POOL_GROUP)]

    def __init__(self, ins, outs, scratch):
        self.ins, self.outs = ins, outs
        self.bufs, self.rcvs, self.send_sems, self.recv_sems = scratch[:3], scratch[3:6], scratch[6], scratch[7]

    @classmethod
    def scratch(cls):
        return ([pltpu.VMEM((3,) + s, F32) for s in cls.SHAPES] + [pltpu.VMEM((3,) + s, F32) for s in cls.SHAPES]
                + [pltpu.SemaphoreType.DMA((9,)), pltpu.SemaphoreType.DMA((9,))])

    @classmethod
    def out_shape(cls):
        return [jax.ShapeDtypeStruct(s, F32) for s in cls.SHAPES]

    def _copies(self, st):
        x, y, c, _ = _mesh_pos()
        peer = [(x, y, 1 - c), (1 - x, y, c), (x, 1 - y, c)][st]
        return [pltpu.make_async_remote_copy(
            src_ref=buf.at[st], dst_ref=rcv.at[st], send_sem=self.send_sems.at[3 * st + i],
            recv_sem=self.recv_sems.at[3 * st + i], device_id=peer, device_id_type=MESH)
            for i, (buf, rcv) in enumerate(zip(self.bufs, self.rcvs))]

    def pack_and_send(self):
        dg1_ref, dg1m_ref, dg2_ref, dg3_ref, dg4_ref, loss_ref, dmeta_ref, dsc_ref, dcw_ref, dpw_ref = self.ins
        a_buf, b_buf, c_buf = self.bufs

        def rowsum(v):
            return jnp.sum(v, axis=0, keepdims=True)

        a_buf[0, 0:1, :] = rowsum(dg1_ref[...] + dg1m_ref[...])
        a_buf[0, 1:2, :] = rowsum(dg2_ref[...])
        a_buf[0, 2:3, :] = rowsum(dg3_ref[...])
        a_buf[0, 3:4, :] = rowsum(dg4_ref[...])
        loss = jnp.sum(rowsum(loss_ref[...]), axis=1, keepdims=True) * (0.5 / D_MODEL)
        a_buf[0, 4:5, :] = jnp.broadcast_to(loss, (1, D_MODEL))
        a_buf[0, 5:8, :] = jnp.zeros((3, D_MODEL), F32)
        a_buf[0, 8:24, :] = dmeta_ref[...]
        b_buf[0, 0:1, :] = rowsum(dsc_ref[...])
        for k in range(3):
            b_buf[0, 1 + k:2 + k, :] = rowsum(dcw_ref[8 * k:8 * k + 8, :])
        b_buf[0, 4:8, :] = jnp.zeros((4, D_CONV), F32)
        c_buf[0] = dpw_ref[...]
        for cp in self._copies(0):
            cp.start()

    def combine(self, st):
        for cp in self._copies(st):
            cp.wait()
        if st < 2:
            for buf, rcv in zip(self.bufs, self.rcvs):
                buf[st + 1] = buf[st] + rcv[st]
            for cp in self._copies(st + 1):
                cp.start()
        else:
            for out, buf, rcv in zip(self.outs, self.bufs, self.rcvs):
                out[...] = buf[st] + rcv[st]


def _row_block(rows):
    for cand in (512, 448, 384, 352, 320, 256, 128, 64, 32, 16):
        if rows % cand == 0:
            return cand
    return rows


def _add_pairs_multi(grads, recvs, place):
    n = len(grads)
    n_sh = grads[0].shape[0]
    halves = [g.shape[1] // 2 for g in grads]
    n_steps = halves[0] // _row_block(halves[0])
    blocks = [(hr // n_steps, g.shape[2]) for hr, g in zip(halves, grads)]

    def body(place_ref, *refs):
        for a_ref, b_ref, o_ref in zip(refs[:n], refs[n:2 * n], refs[2 * n:]):
            o_ref[...] = (a_ref[0] + b_ref[...]).astype(BF16)

    return pl.pallas_call(
        body, name="grad_add_pairs",
        grid_spec=pltpu.PrefetchScalarGridSpec(
            num_scalar_prefetch=1, grid=(n_sh, n_steps),
            in_specs=[pl.BlockSpec((1, 1, br, cols), lambda j, i, p: (j, p[1], i, 0)) for br, cols in blocks]
            + [pl.BlockSpec((1, br, cols), lambda j, i, p: (j, i, 0)) for br, cols in blocks],
            out_specs=[pl.BlockSpec((1, br, cols), lambda j, i, p: (j, i, 0)) for br, cols in blocks]),
        out_shape=[jax.ShapeDtypeStruct((n_sh, hr, g.shape[2]), BF16) for hr, g in zip(halves, grads)],
        compiler_params=_cparams(2),
    )(place, *[g.reshape(n_sh, 2, hr, g.shape[2]) for hr, g in zip(halves, grads)], *recvs)


def _add_pairs(grad, recv, place):
    return _add_pairs_multi([grad], [recv], place)[0]


def _add_chips(grads, recvs, rbufs, place, after=None, name="grad_add_chips"):
    n = len(grads)
    n_sh = grads[0].shape[0]
    halves = [g.shape[1] // 2 for g in grads]
    n_steps = halves[0] // _row_block(halves[0])
    blocks = [(hr // n_steps, g.shape[2]) for hr, g in zip(halves, grads)]
    extra = [] if after is None else [after]

    def body(place_ref, *refs):
        for a_ref, b_ref, r_ref, o_ref in zip(refs[:n], refs[n:2 * n], refs[2 * n:3 * n], refs[3 * n + len(extra):]):
            own = a_ref[0, 0] + b_ref[0]
            o_ref[0] = ((own + r_ref[0].astype(F32)) + r_ref[1].astype(F32)) + r_ref[2].astype(F32)

    return pl.pallas_call(
        body, name=name,
        grid_spec=pltpu.PrefetchScalarGridSpec(
            num_scalar_prefetch=1, grid=(n_steps,),
            in_specs=[pl.BlockSpec((1, 1, br, cols), lambda i, p: (p[0], p[1], i, 0)) for br, cols in blocks]
            + [pl.BlockSpec((1, br, cols), lambda i, p: (p[0], i, 0)) for br, cols in blocks]
            + [pl.BlockSpec((3, br, cols), lambda i, p: (0, i, 0)) for br, cols in blocks]
            + [pl.BlockSpec((8, 128), lambda i, p: (0, 0))] * len(extra),
            out_specs=[pl.BlockSpec((1, br, cols), lambda i, p: (p[1], i, 0)) for br, cols in blocks]),
        out_shape=[jax.ShapeDtypeStruct((2, hr, g.shape[2]), F32) for hr, g in zip(halves, grads)],
        compiler_params=_cparams(1),
    )(place, *[g.reshape(n_sh, 2, hr, g.shape[2]) for hr, g in zip(halves, grads)], *recvs, *rbufs, *extra)


def _adamw_math(w, g, m, v):
    m2 = ADAM_B1 * m + (1.0 - ADAM_B1) * g
    v2 = ADAM_B2 * v + (1.0 - ADAM_B2) * (g * g)
    m_hat = m2 / (1.0 - ADAM_B1 ** ADAM_STEP)
    v_hat = v2 / (1.0 - ADAM_B2 ** ADAM_STEP)
    delta = -ADAM_LR * (m_hat / (jnp.sqrt(v_hat) + ADAM_EPS) + ADAM_WD * w)
    return delta, m2, v2


def _adamw_big(groups):
    n = len(groups)
    rows, cols = groups[0][0].shape
    br = _row_block(rows)
    if n > 1 and br % 16 == 0:
        br //= 2

    def body(*refs):
        for i in range(n):
            w_ref, g_ref, m_ref, v_ref = refs[4 * i:4 * i + 4]
            d_ref, m2_ref, v2_ref = refs[4 * n + 3 * i:4 * n + 3 * i + 3]
            d, m2, v2 = _adamw_math(w_ref[...], g_ref[...], m_ref[...], v_ref[...])
            d_ref[...] = d
            m2_ref[...] = m2
            v2_ref[...] = v2

    spec = pl.BlockSpec((br, cols), lambda i: (i, 0))
    outs = pl.pallas_call(
        body, name="adamw_big", grid=(rows // br,),
        out_shape=[jax.ShapeDtypeStruct((rows, cols), F32)] * (3 * n),
        in_specs=[spec] * (4 * n), out_specs=[spec] * (3 * n), compiler_params=_cparams(1),
    )(*[a for grp in groups for a in grp])
    return [list(outs[3 * i:3 * i + 3]) for i in range(n)]


def _adamw_small(groups):
    n = len(groups)

    def body(*refs):
        ins, outs = refs[:4 * n], refs[4 * n:]
        for i in range(n):
            w, g, m, v = (r[...] for r in ins[4 * i:4 * i + 4])
            d, m2, v2 = _adamw_math(w, g, m, v)
            outs[3 * i][...] = d
            outs[3 * i + 1][...] = m2
            outs[3 * i + 2][...] = v2

    vm = pl.BlockSpec(memory_space=pltpu.VMEM)
    flat = [a for grp in groups for a in grp]
    out_shape = [jax.ShapeDtypeStruct(grp[0].shape, F32) for grp in groups for _ in range(3)]
    outs = pl.pallas_call(body, name="adamw_small", out_shape=out_shape,
                          in_specs=[vm] * (4 * n), out_specs=[vm] * (3 * n))(*flat)
    return [tuple(outs[3 * i:3 * i + 3]) for i in range(n)]


def _load_gathered(gathered, shards, dst_slots, sems):
    n = len(gathered)
    me = 2 * lax.axis_index("x") + lax.axis_index("y")

    def copies(j, own):
        return [pltpu.make_async_copy(shards[a] if own else gathered[a].at[j], dst_slots[a](j), sems.at[n * j + a])
                for a in range(n)]

    for wait in (False, True):
        for j in range(N_CHIPS):
            for own in (False, True):
                @pl.when((me == j) == own)
                def _():
                    for cp in copies(j, own):
                        cp.wait() if wait else cp.start()


N_MIX_SHARDS = 3


def _mixer_fwd(x3, g1, g2, poolw, pscale, shards):
    n_seq, seq, _ = x3.shape
    tm = min(TM_MIX_FWD, seq)
    n_t = seq // tm
    n_steps = n_seq * n_t
    n_ag = len(shards)
    n_ffn = n_ag - N_MIX_SHARDS
    small_rows = shards[2].shape[0]
    conv_cols = D_CONV // N_CHIPS

    def body(x_ref, g1_ref, g2_ref, pw_ref, ps_ref, *rest):
        ag = _AllGather(rest[:n_ag], rest[n_ag + 10:2 * n_ag + 10], *rest[-2:])
        (z_ref, m_ref, h1_ref, a_ref, conv_ref, pooled_ref, yc_ref, zm_ref, meta_ref,
         cw_ref) = rest[n_ag:n_ag + 10]
        win_v, wout_v, small_v, cvb, pb, load_sems = rest[2 * n_ag + 10:-2]
        s, t = pl.program_id(0), pl.program_id(1)
        step = s * n_t + t

        @pl.when(step == 0)
        def _():
            ag.start(range(N_MIX_SHARDS))
            for a in range(N_MIX_SHARDS):
                ag.relay(a)
            for a in range(N_MIX_SHARDS):
                ag.forward(a)
            ag.finish(range(N_MIX_SHARDS))
            ag.start(range(N_MIX_SHARDS, n_ag))
            _load_gathered(ag.outs[:N_MIX_SHARDS], ag.ins[:N_MIX_SHARDS],
                           [lambda j: win_v.at[j], lambda j: wout_v.at[pl.ds(j * OUT_SHARD, OUT_SHARD), :],
                            lambda j: small_v.at[j]], load_sems)

            meta = jnp.concatenate([small_v[j, 0:N_META, :] for j in range(N_CHIPS)], axis=1)
            meta_ref[...] = meta
            cw_ref[...] = jnp.concatenate([small_v[j, N_META:N_META + 3, 0:conv_cols] for j in range(N_CHIPS)], axis=1)
            a_meta = (meta * _rstd(meta) * g1_ref[...]).astype(BF16)
            for j in range(N_CHIPS):
                zm_ref[:, j * IN_SHARD:(j + 1) * IN_SHARD] = _dot(a_meta, win_v[j])

        for i in range(n_ffn):
            @pl.when(step == ((i + 1) * n_steps) // (2 * n_ffn + 2))
            def _():
                ag.relay(N_MIX_SHARDS + i)

        for i in range(n_ffn):
            @pl.when(step == min(n_steps // 2 + ((i + 1) * n_steps) // (2 * n_ffn + 2), n_steps - 1))
            def _():
                ag.forward(N_MIX_SHARDS + i)

        @pl.when(t == 0)
        def _():
            cvb[0:HALO, :] = zm_ref[:, IN_SHARD:2 * IN_SHARD] * zm_ref[:, 2 * IN_SHARD:3 * IN_SHARD]
            pb[0:HALO, :] = zm_ref[:, 3 * IN_SHARD:4 * IN_SHARD]

        @pl.when(t > 0)
        def _():
            cvb[0:HALO, :] = cvb[tm:tm + HALO, :]
            pb[0:HALO, :] = pb[tm:tm + HALO, :]

        xt = x_ref[0]
        a = (xt * _rstd(xt) * g1_ref[...]).astype(BF16)
        a_ref[...] = a
        zb = _dot(a, win_v[0])
        zc = _dot(a, win_v[1])
        zv = _dot(a, win_v[2])
        zp = _dot(a, win_v[3])
        z_ref[0, :, 0:IN_SHARD] = zb
        z_ref[0, :, IN_SHARD:2 * IN_SHARD] = zc
        z_ref[0, :, 2 * IN_SHARD:3 * IN_SHARD] = zv
        cv = zc * zv
        cvb[HALO:HALO + tm, :] = cv
        pb[HALO:HALO + tm, :] = zp
        cw = cw_ref[...]
        conv = cw[0:1] * cvb[HALO - 2:HALO - 2 + tm, :] + cw[1:2] * cvb[HALO - 1:HALO - 1 + tm, :] + cw[2:3] * cv
        conv_ref[...] = conv
        parts = [(zb * conv).astype(BF16)]
        for g in range(N_POOL_GROUPS):
            pooled = _pool_fwd(pb, g, tm).astype(BF16)
            pooled_ref[:, _gcols(g)] = pooled
            parts.append((_dot(pooled, pw_ref[g]) * ps_ref[:, _gcols(g)]).astype(BF16))
        ycat = jnp.concatenate(parts, axis=1)
        yc_ref[...] = ycat
        m = _dot(ycat, wout_v[...])
        m_ref[0] = m
        h1_ref[0] = xt + m * _rstd(m) * g2_ref[...]

        @pl.when(step == n_steps - 1)
        def _():
            ag.finish(range(N_MIX_SHARDS, n_ag))

    n_rows = n_seq * seq
    row = lambda c: pl.BlockSpec((1, tm, c), lambda s, t: (s, t, 0))
    row2 = lambda c: pl.BlockSpec((tm, c), lambda s, t: (s * n_t + t, 0))
    outs = pl.pallas_call(
        body, name="mixer_fwd", grid=(n_seq, n_t),
        out_shape=[jax.ShapeDtypeStruct((n_seq, seq, D_Z), F32), jax.ShapeDtypeStruct((n_seq, seq, D_MODEL), F32),
                   jax.ShapeDtypeStruct((n_seq, seq, D_MODEL), F32), jax.ShapeDtypeStruct((n_rows, D_MODEL), BF16),
                   jax.ShapeDtypeStruct((n_rows, D_CONV), F32), jax.ShapeDtypeStruct((n_rows, D_POOL), BF16),
                   jax.ShapeDtypeStruct((n_rows, D_MODEL), BF16), jax.ShapeDtypeStruct((N_META, D_IN_PROJ), F32),
                   jax.ShapeDtypeStruct((N_META, D_MODEL), F32), jax.ShapeDtypeStruct((3, D_CONV), F32)]
        + _AllGather.out_shape(shards),
        in_specs=[row(D_MODEL), _full((1, D_MODEL)), _full((1, D_MODEL)),
                  _full((N_POOL_GROUPS, POOL_GROUP, POOL_GROUP)), _full((1, D_POOL))] + [ANY] * n_ag,
        out_specs=[row(D_Z), row(D_MODEL), row(D_MODEL), row2(D_MODEL), row2(D_CONV), row2(D_POOL), row2(D_MODEL),
                   _full((N_META, D_IN_PROJ)), _full((N_META, D_MODEL)), _full((3, D_CONV))] + [ANY] * n_ag,
        scratch_shapes=[pltpu.VMEM((N_CHIPS, D_MODEL, IN_SHARD), BF16), pltpu.VMEM((D_MODEL, D_MODEL), BF16),
                        pltpu.VMEM((N_CHIPS, small_rows, D_MODEL // N_CHIPS), F32),
                        pltpu.VMEM((HALO + tm, D_CONV), F32), pltpu.VMEM((HALO + tm, D_POOL), F32),
                        pltpu.SemaphoreType.DMA((N_MIX_SHARDS * N_CHIPS,))] + _AllGather.scratch(n_ag),
        compiler_params=_cparams(2),
    )(x3, g1, g2, poolw, pscale, *shards)
    return outs[:10], outs[10:]


def _ffn_chunks():
    out, r0 = [], 0
    while r0 < D_FF:
        out.append((r0, min(FF_CHUNK, D_FF - r0)))
        r0 += FF_CHUNK
    return out


def _ffn_fwd_bwd(h1, target, g3, g4, gathered, shards):
    n_rows = h1.shape[0]
    tm = min(TM_FFN, n_rows)
    chunks = _ffn_chunks()

    def body(h1_ref, t_ref, g3_ref, g4_ref, wg_all, wu_all, wd_all, wg_s, wu_s, wd_s,
             dh1_ref, f_ref, dd_ref, ds_ref, du_ref, gg_ref, loss_ref, dg3_ref, dg4_ref,
             wg_v, wu_v, wd_v, s_sc, u_sc, sems):
        @pl.when(pl.program_id(0) == 0)
        def _():
            _load_gathered([wg_all, wu_all, wd_all], [wg_s, wu_s, wd_s],
                           [functools.partial(lambda v, j: v.at[pl.ds(j * FF_SHARD, FF_SHARD), :], v)
                            for v in (wg_v, wu_v, wd_v)], sems)
            loss_ref[...] = jnp.zeros_like(loss_ref)
            dg3_ref[...] = jnp.zeros_like(dg3_ref)
            dg4_ref[...] = jnp.zeros_like(dg4_ref)

        h1v = h1_ref[...]
        r3 = _rstd(h1v)
        hh = h1v * r3
        g3v, g4v = g3_ref[...], g4_ref[...]
        f = (hh * g3v).astype(BF16)
        f_ref[...] = f
        d = jnp.zeros((tm, D_MODEL), F32)
        for r0, sz in chunks:
            s = _dot_nt(f, wg_v[r0:r0 + sz, :])
            u = _dot_nt(f, wu_v[r0:r0 + sz, :])
            s_sc[:, r0:r0 + sz] = s
            u_sc[:, r0:r0 + sz] = u
            gc = (s * _sigmoid(s) * u).astype(BF16)
            gg_ref[:, r0:r0 + sz] = gc
            d = d + _dot(gc, wd_v[r0:r0 + sz, :])
        r4 = _rstd(d)
        dh = d * r4
        err = (h1v + dh * g4v) - t_ref[...]
        loss_ref[...] += _rows8(err * err)
        dy = err * (1.0 / D_MODEL)
        dg4_ref[...] += _rows8(dy * dh)
        ddb = _rms_bwd(dy, dh, r4, g4v).astype(BF16)
        dd_ref[...] = ddb
        df = jnp.zeros((tm, D_MODEL), F32)
        for r0, sz in chunks:
            dgg = _dot_nt(ddb, wd_v[r0:r0 + sz, :])
            s = s_sc[:, r0:r0 + sz]
            u = u_sc[:, r0:r0 + sz]
            sig = _sigmoid(s)
            dsc = (dgg * u * (sig * (1.0 + s * (1.0 - sig)))).astype(BF16)
            duc = (dgg * (s * sig)).astype(BF16)
            ds_ref[:, r0:r0 + sz] = dsc
            du_ref[:, r0:r0 + sz] = duc
            df = df + _dot(dsc, wg_v[r0:r0 + sz, :]) + _dot(duc, wu_v[r0:r0 + sz, :])
        dg3_ref[...] += _rows8(df * hh)
        dh1_ref[...] = dy + _rms_bwd(df, hh, r3, g3v)

    row = pl.BlockSpec((tm, D_MODEL), lambda i: (i, 0))
    ffrow = pl.BlockSpec((tm, D_FF), lambda i: (i, 0))
    acc = _full((8, D_MODEL))
    act_bf = jax.ShapeDtypeStruct((n_rows, D_MODEL), BF16)
    ff_bf = jax.ShapeDtypeStruct((n_rows, D_FF), BF16)
    acc_shape = jax.ShapeDtypeStruct((8, D_MODEL), F32)
    w_vmem = pltpu.VMEM((D_FF, D_MODEL), BF16)
    return pl.pallas_call(
        body, name="ffn_fwd_bwd", grid=(n_rows // tm,),
        out_shape=[jax.ShapeDtypeStruct((n_rows, D_MODEL), F32), act_bf, act_bf, ff_bf, ff_bf, ff_bf,
                   acc_shape, acc_shape, acc_shape],
        in_specs=[row, row, _full((1, D_MODEL)), _full((1, D_MODEL))] + [ANY] * 6,
        out_specs=[row, row, row, ffrow, ffrow, ffrow, acc, acc, acc],
        scratch_shapes=[w_vmem, w_vmem, w_vmem, pltpu.VMEM((tm, D_FF), F32), pltpu.VMEM((tm, D_FF), F32),
                        pltpu.SemaphoreType.DMA((3 * N_CHIPS,))],
        compiler_params=_cparams(1),
    )(h1, target, g3, g4, *gathered, *shards)


def _ffn_weight_grads(name, acts, other, exchanged):
    n_rows = other.shape[0]
    n_a, n_ex = len(acts), len(exchanged)
    n_c = n_a
    tk = min(TK_DW, n_rows)
    n_k = n_rows // tk
    half = D_FF // n_c

    def body(other_ref, *rest):
        act_refs = rest[:n_a]
        out_refs = rest[n_a + n_ex:2 * n_a + n_ex]
        c, k = pl.program_id(0), pl.program_id(1)
        if n_ex:
            ex = _ExchangeHalves(rest[n_a:n_a + n_ex], rest[2 * n_a + n_ex:2 * n_a + 2 * n_ex], *rest[-2:])

            @pl.when((c == 0) & (k == 0))
            def _():
                ex.start()

        @pl.when(k == 0)
        def _():
            for o in out_refs:
                o[...] = jnp.zeros_like(o)

        ov = other_ref[...]
        for a, o in zip(act_refs, out_refs):
            o[...] += _dot_tn(a[...], ov)

        if n_ex:
            @pl.when((c == n_c - 1) & (k == n_k - 1))
            def _():
                ex.finish()

    row = pl.BlockSpec((tk, D_MODEL), lambda c, k: (k, 0))
    ffrow = pl.BlockSpec((tk, half), lambda c, k: (k, c))
    out = pl.BlockSpec((half, D_MODEL), lambda c, k: (c, 0))
    outs = pl.pallas_call(
        body, name=name, grid=(n_c, n_k),
        out_shape=[jax.ShapeDtypeStruct((D_FF, D_MODEL), F32)] * n_a + _ExchangeHalves.out_shape(exchanged),
        in_specs=[row] + [ffrow] * n_a + [ANY] * n_ex, out_specs=[out] * n_a + [ANY] * n_ex,
        scratch_shapes=_ExchangeHalves.scratch(n_ex) if n_ex else [],
        compiler_params=_cparams(2),
    )(other, *acts, *exchanged)
    return outs[:n_a], outs[n_a:]


def _mixer_bwd(dh1, m3, z3, conv2, pooled2, x3, zmeta, meta_full, g1, g2, convw, poolw, pscale, gathered, shards,
               exchanged, scattered):
    n_seq, seq, _ = x3.shape
    tm = min(TM_MIX_BWD, seq)
    sub = min(SUB_MIX_BWD, tm)
    n_t = seq // tm
    n_ex, n_sc = len(exchanged), len(scattered)
    n_cm = n_ex + n_sc
    n_out = 13

    def body(dh1_ref, m_ref, z_ref, conv_ref, pooled_ref, x_ref, zm_ref, meta_ref, g1_ref, g2_ref, cw_ref, pw_ref,
             ps_ref, win_all, wout_all, win_s, wout_s, *rest):
        outs0 = n_cm + n_out
        ex = _ExchangeHalves(rest[:n_ex], rest[outs0:outs0 + n_ex], *rest[-4:-2])
        sc = _ScatterToChips(rest[n_ex:n_cm], rest[outs0 + n_ex:outs0 + n_cm], *rest[-2:])
        (dx_ref, dz_ref, dm_ref, dg1_ref, dg2_ref, dsc_ref, dcw_ref, dpw_ref, dzm_ref, dmeta_ref, dg1m_ref, am_ref,
         dzmb_ref) = rest[n_cm:outs0]
        win_v, wout_v, dcb, dqb, mcb, mqb, load_sems = rest[outs0 + n_cm:-4]
        s, i = pl.program_id(0), pl.program_id(1)
        tr = n_t - 1 - i

        @pl.when((s == 0) & (i == 0))
        def _():
            sc.start()
            ex.start()
            _load_gathered([win_all, wout_all], [win_s, wout_s],
                           [lambda j: win_v.at[j], lambda j: wout_v.at[pl.ds(j * OUT_SHARD, OUT_SHARD), :]], load_sems)
            for ref in (dg1_ref, dg2_ref, dsc_ref, dcw_ref, dpw_ref, dzm_ref):
                ref[...] = jnp.zeros_like(ref)

        @pl.when(i == 0)
        def _():
            dcb[tm:tm + HALO, :] = jnp.zeros((HALO, D_CONV), F32)
            dqb[tm:tm + HALO, :] = jnp.zeros((HALO, D_POOL), F32)

        @pl.when(i > 0)
        def _():
            dcb[tm:tm + HALO, :] = dcb[0:HALO, :]
            dqb[tm:tm + HALO, :] = dqb[0:HALO, :]

        g1v, g2v = g1_ref[...], g2_ref[...]
        cw = cw_ref[...]

        for r0 in range(tm - sub, -1, -sub):
            rows = slice(r0, r0 + sub)
            dh1v = dh1_ref[0, rows, :]
            mv = m_ref[0, rows, :]
            r2 = _rstd(mv)
            mh = mv * r2
            dg2_ref[...] += _rows8(dh1v * mh)
            dmb = _rms_bwd(dh1v, mh, r2, g2v).astype(BF16)
            dm_ref[rows, :] = dmb
            dyc = _dot_nt(dmb, wout_v[...])
            dyconv = dyc[:, 0:D_CONV]

            for g in range(N_POOL_GROUPS):
                pooled = pooled_ref[rows, _gcols(g)]
                mixed = _dot(pooled, pw_ref[g])
                scale = ps_ref[:, _gcols(g)]
                dyp = dyc[:, D_CONV + g * POOL_GROUP:D_CONV + (g + 1) * POOL_GROUP]
                dsc_ref[:, _gcols(g)] += _rows8(dyp * mixed)
                dmix = (dyp * scale).astype(BF16)
                dpw_ref[g] += _dot_tn(pooled, dmix)
                dqb[rows, _gcols(g)] = _dot_nt(dmix, pw_ref[g])

            zb = z_ref[0, rows, 0:IN_SHARD]
            zc = z_ref[0, rows, IN_SHARD:2 * IN_SHARD]
            zv = z_ref[0, rows, 2 * IN_SHARD:3 * IN_SHARD]
            dconv = dyconv * zb
            dcb[rows, :] = dconv
            d1 = dcb[r0 + 1:r0 + 1 + sub, :]
            d2 = dcb[r0 + 2:r0 + 2 + sub, :]
            dcv = cw[2:3] * dconv + cw[1:2] * d1 + cw[0:1] * d2
            cv = zc * zv
            dcw_ref[0:8, :] += _rows8(cv * d2)
            dcw_ref[8:16, :] += _rows8(cv * d1)
            dcw_ref[16:24, :] += _rows8(cv * dconv)
            dzs = [(dyconv * conv_ref[rows, :]).astype(BF16), (dcv * zv).astype(BF16), (dcv * zc).astype(BF16),
                   jnp.concatenate([_pool_bwd(dqb, g, r0, sub) for g in range(N_POOL_GROUPS)], axis=1).astype(BF16)]
            da = jnp.zeros((sub, D_MODEL), F32)
            for j in range(N_CHIPS):
                dz_ref[j, rows, :] = dzs[j]
                da = da + _dot_nt(dzs[j], win_v[j])
            xt = x_ref[0, rows, :]
            r1 = _rstd(xt)
            xh = xt * r1
            dg1_ref[...] += _rows8(da * xh)
            dx_ref[0, rows, :] = dh1v + _rms_bwd(da, xh, r1, g1v)

        @pl.when(tr == 0)
        def _():
            mcb[0:HALO, :] = jnp.zeros((HALO, D_CONV), F32)
            mqb[0:HALO, :] = jnp.zeros((HALO, D_POOL), F32)
            mcb[HALO:2 * HALO, :] = dcb[0:HALO, :]
            mqb[HALO:2 * HALO, :] = dqb[0:HALO, :]
            m1 = mcb[1:1 + HALO, :]
            m2 = mcb[2:2 + HALO, :]
            zc_m = zm_ref[:, IN_SHARD:2 * IN_SHARD]
            zv_m = zm_ref[:, 2 * IN_SHARD:3 * IN_SHARD]
            cv_m = zc_m * zv_m
            dcw_ref[0:8, :] += _rows8(cv_m * m2)
            dcw_ref[8:16, :] += _rows8(cv_m * m1)
            dcv_m = cw[1:2] * m1 + cw[0:1] * m2
            dzm_ref[:, IN_SHARD:2 * IN_SHARD] += dcv_m * zv_m
            dzm_ref[:, 2 * IN_SHARD:3 * IN_SHARD] += dcv_m * zc_m
            dzm_ref[:, 3 * IN_SHARD:4 * IN_SHARD] += jnp.concatenate(
                [_pool_bwd(mqb, g, 0, HALO) for g in range(N_POOL_GROUPS)], axis=1)

        @pl.when((s == n_seq - 1) & (i == n_t - 1))
        def _():
            xm = meta_ref[...]
            rm = _rstd(xm)
            xmh = xm * rm
            am_ref[...] = (xmh * g1v).astype(BF16)
            da_m = jnp.zeros((N_META, D_MODEL), F32)
            for j in range(N_CHIPS):
                dzj = dzm_ref[:, j * IN_SHARD:(j + 1) * IN_SHARD].astype(BF16)
                dzmb_ref[j] = dzj
                da_m = da_m + _dot_nt(dzj, win_v[j])
            dg1m_ref[...] = _rows8(da_m * xmh)
            dmeta_ref[...] = _rms_bwd(da_m, xmh, rm, g1v)
            ex.finish()
            sc.finish()

    row3 = lambda c: pl.BlockSpec((1, tm, c), lambda s, i: (s, n_t - 1 - i, 0))
    row2 = lambda c: pl.BlockSpec((tm, c), lambda s, i: (s * n_t + n_t - 1 - i, 0))
    n_rows = n_seq * seq
    outs = pl.pallas_call(
        body, name="mixer_bwd", grid=(n_seq, n_t),
        out_shape=[jax.ShapeDtypeStruct((n_seq, seq, D_MODEL), F32),
                   jax.ShapeDtypeStruct((N_CHIPS, n_rows, IN_SHARD), BF16), jax.ShapeDtypeStruct((n_rows, D_MODEL), BF16),
                   jax.ShapeDtypeStruct((8, D_MODEL), F32), jax.ShapeDtypeStruct((8, D_MODEL), F32),
                   jax.ShapeDtypeStruct((8, D_POOL), F32), jax.ShapeDtypeStruct((24, D_CONV), F32),
                   jax.ShapeDtypeStruct((N_POOL_GROUPS, POOL_GROUP, POOL_GROUP), F32),
                   jax.ShapeDtypeStruct((N_META, D_IN_PROJ), F32),
                   jax.ShapeDtypeStruct((N_META, D_MODEL), F32), jax.ShapeDtypeStruct((8, D_MODEL), F32),
                   jax.ShapeDtypeStruct((N_META, D_MODEL), BF16),
                   jax.ShapeDtypeStruct((N_CHIPS, N_META, IN_SHARD), BF16)]
        + _ExchangeHalves.out_shape(exchanged) + _ScatterToChips.out_shape(scattered),
        in_specs=[row3(D_MODEL), row3(D_MODEL), row3(D_Z), row2(D_CONV), row2(D_POOL), row3(D_MODEL),
                  _full((N_META, D_IN_PROJ)), _full((N_META, D_MODEL)), _full((1, D_MODEL)), _full((1, D_MODEL)),
                  _full((3, D_CONV)), _full((N_POOL_GROUPS, POOL_GROUP, POOL_GROUP)), _full((1, D_POOL))]
        + [ANY] * (4 + n_cm),
        out_specs=[row3(D_MODEL), pl.BlockSpec((N_CHIPS, tm, IN_SHARD), lambda s, i: (0, s * n_t + n_t - 1 - i, 0)),
                   row2(D_MODEL),
                   _full((8, D_MODEL)), _full((8, D_MODEL)), _full((8, D_POOL)), _full((24, D_CONV)),
                   _full((N_POOL_GROUPS, POOL_GROUP, POOL_GROUP)), _full((N_META, D_IN_PROJ)),
                   _full((N_META, D_MODEL)), _full((8, D_MODEL)), _full((N_META, D_MODEL)),
                   _full((N_CHIPS, N_META, IN_SHARD))] + [ANY] * n_cm,
        scratch_shapes=[pltpu.VMEM((N_CHIPS, D_MODEL, IN_SHARD), BF16), pltpu.VMEM((D_MODEL, D_MODEL), BF16),
                        pltpu.VMEM((tm + HALO, D_CONV), F32), pltpu.VMEM((tm + HALO, D_POOL), F32),
                        pltpu.VMEM((2 * HALO, D_CONV), F32), pltpu.VMEM((2 * HALO, D_POOL), F32),
                        pltpu.SemaphoreType.DMA((2 * N_CHIPS,))]
        + _ExchangeHalves.scratch(n_ex) + _ScatterToChips.scratch(n_sc),
        compiler_params=_cparams(2),
    )(dh1, m3, z3, conv2, pooled2, x3, zmeta, meta_full, g1, g2, convw, poolw, pscale, *gathered, *shards,
      *exchanged, *scattered)
    return outs[:n_out], outs[n_out:n_out + n_ex], outs[n_out + n_ex:]


def _mixer_weight_grads(a, dz, ycat, dm, a_meta, dz_meta, ffn_sums, small):
    n_rows = a.shape[0]
    tk = min(TK_DW, n_rows)
    n_k = n_rows // tk
    n_sc, n_sm = len(ffn_sums), _AllReduceSmall.N_IN

    def body(a_ref, dz_ref, yc_ref, dm_ref, am_ref, dzm_ref, *rest):
        ins, outs, scratch = rest[:n_sc + n_sm], rest[n_sc + n_sm:2 * n_sc + n_sm + 5], rest[2 * n_sc + n_sm + 5:]
        dwin_ref, dwout_ref = outs[:2]
        scatter = _ScatterToChips(ins[:n_sc], outs[2:2 + n_sc], *scratch[:2])
        reduce_small = _AllReduceSmall(ins[n_sc:], outs[2 + n_sc:], scratch[2:])
        k = pl.program_id(0)

        @pl.when(k == 0)
        def _():
            scatter.start()
            reduce_small.pack_and_send()
            am_t = am_ref[...].T
            for j in range(N_CHIPS):
                dwin_ref[j] = _dot(am_t, dzm_ref[j])
            dwout_ref[...] = jnp.zeros_like(dwout_ref)

        for st in range(2):
            @pl.when(k == ((st + 1) * n_k) // 3)
            def _():
                reduce_small.combine(st)

        a_t = a_ref[...].T
        for j in range(N_CHIPS):
            dwin_ref[j] += _dot(a_t, dz_ref[j])
        dwout_ref[...] += _dot_tn(yc_ref[...], dm_ref[...])

        @pl.when(k == n_k - 1)
        def _():
            reduce_small.combine(2)
            scatter.finish()

    row = pl.BlockSpec((tk, D_MODEL), lambda k: (k, 0))
    outs = pl.pallas_call(
        body, name="mixer_weight_grads", grid=(n_k,),
        out_shape=[jax.ShapeDtypeStruct((N_CHIPS, D_MODEL, IN_SHARD), F32),
                   jax.ShapeDtypeStruct((D_MODEL, D_MODEL), F32)] + _ScatterToChips.out_shape(ffn_sums)
        + _AllReduceSmall.out_shape(),
        in_specs=[row, pl.BlockSpec((N_CHIPS, tk, IN_SHARD), lambda k: (0, k, 0)), row, row,
                  _full((N_META, D_MODEL)), _full((N_CHIPS, N_META, IN_SHARD))] + [ANY] * n_sc
        + [_full(s.shape) for s in small],
        out_specs=[_full((N_CHIPS, D_MODEL, IN_SHARD)), _full((D_MODEL, D_MODEL))] + [ANY] * n_sc
        + [_full(s) for s in _AllReduceSmall.SHAPES],
        scratch_shapes=_ScatterToChips.scratch(n_sc) + _AllReduceSmall.scratch(),
        compiler_params=_cparams(1),
    )(a, dz, ycat, dm, a_meta, dz_meta, *ffn_sums, *small)
    return ([outs[0], outs[1].reshape(N_CHIPS, OUT_SHARD, D_MODEL)], outs[2:2 + n_sc], outs[2 + n_sc:])


def kernel(x, meta_tokens, norm_mix_pre, w_in, conv_w, pool_w, pool_scale, w_out, norm_mix_post, norm_ffn_pre, w_gate, w_up, w_down, norm_ffn_post, loss_target, m_meta_tokens, m_norm_mix_pre, m_w_in, m_conv_w, m_pool_w, m_pool_scale, m_w_out, m_norm_mix_post, m_norm_ffn_pre, m_w_gate, m_w_up, m_w_down, m_norm_ffn_post, v_meta_tokens, v_norm_mix_pre, v_w_in, v_conv_w, v_pool_w, v_pool_scale, v_w_out, v_norm_mix_post, v_norm_ffn_pre, v_w_gate, v_w_up, v_w_down, v_norm_ffn_post):
    n_seq, seq, _ = x.shape
    n_rows = n_seq * seq
    chip = 2 * lax.axis_index("x") + lax.axis_index("y")
    meta_cols = D_MODEL // N_CHIPS
    conv_cols = D_CONV // N_CHIPS

    small = jnp.zeros((2 * HALO, meta_cols), F32)
    small = small.at[0:N_META, :].set(meta_tokens).at[N_META:N_META + 3, 0:conv_cols].set(conv_w[0])
    poolw_bf = pool_w[0].astype(BF16)
    pscale = pool_scale
    g1, g2, g3, g4 = norm_mix_pre, norm_mix_post, norm_ffn_pre, norm_ffn_post
    place = jnp.stack([chip, lax.axis_index("c")]).astype(jnp.int32)

    mix_shards = [w_in[0].astype(BF16), w_out[0].astype(BF16)]
    ffn_shards = [w_gate[0].T.astype(BF16), w_up[0].T.astype(BF16), w_down[0].astype(BF16)]
    ((z3, m3, h1, a_bf, conv2, pooled2, yc_bf, zmeta, meta_full, conv_full),
     (win_all, wout_all, _, *ffn_gathered)) = _mixer_fwd(x, g1, g2, poolw_bf, pscale, mix_shards + [small] + ffn_shards)
    dh1, f_bf, dd_bf, ds_bf, du_bf, gg_bf, lossp, dg3p, dg4p = _ffn_fwd_bwd(
        h1.reshape(n_rows, D_MODEL), loss_target.reshape(n_rows, D_MODEL), g3, g4, ffn_gathered, ffn_shards)
    as_shards = lambda g: g.reshape(N_CHIPS, FF_SHARD, D_MODEL)
    (dwg_t, dwu_t), _ = _ffn_weight_grads("ffn_weight_grads_gate_up", [ds_bf, du_bf], f_bf, [])
    dwg_t, dwu_t = as_shards(dwg_t), as_shards(dwu_t)
    (dwd,), (dwg_recv, dwu_recv) = _ffn_weight_grads("ffn_weight_grads_down", [gg_bf], dd_bf, [dwg_t, dwu_t])
    dwd = as_shards(dwd)
    ((grad_x, dz_bf, dm_bf, dg1p, dg2p, dscp, dcwp, dpw, _, dmeta, dg1m, a_meta, dz_meta), (dwd_recv,),
     (dwg_rbuf, dwu_rbuf)) = _mixer_bwd(
        dh1.reshape(n_seq, seq, D_MODEL), m3, z3, conv2, pooled2, x, zmeta, meta_full, g1, g2, conv_full, poolw_bf,
        pscale, [win_all, wout_all], mix_shards, [dwd],
        _add_pairs_multi([dwg_t, dwu_t], [dwg_recv, dwu_recv], place))
    mix_grads, (dwd_rbuf,), (a_red, b_red, c_red) = _mixer_weight_grads(
        a_bf, dz_bf, yc_bf, dm_bf, a_meta, dz_meta, [_add_pairs(dwd, dwd_recv, place)],
        [dg1p, dg1m, dg2p, dg3p, dg4p, lossp, dmeta, dscp, dcwp, dpw.reshape(SMALL_C_ROWS, POOL_GROUP)])

    mix_recvs = _exchange_halves(mix_grads)
    scatter_sems, sums_thru, lands_thru, started = _scatter_start(_add_pairs_multi(mix_grads, mix_recvs, place))
    ffn_red = _add_chips([dwg_t, dwu_t, dwd], [dwg_recv, dwu_recv, dwd_recv], [dwg_rbuf, dwu_rbuf, dwd_rbuf],
                         place, after=started, name="grad_add_chips_ffn")
    as_full = lambda r: r.reshape(2 * r.shape[1], r.shape[2])
    g_wg_t, g_wu_t, g_wd = [as_full(r) for r in _gather_halves(list(ffn_red), "grad_gather_halves_ffn")]
    ffn_out = _adamw_big([(w_gate[0].T, g_wg_t, m_w_gate[0].T, v_w_gate[0].T),
                          (w_up[0].T, g_wu_t, m_w_up[0].T, v_w_up[0].T), (w_down[0], g_wd, m_w_down[0], v_w_down[0])])
    mix_rbufs = _scatter_wait(scatter_sems, sums_thru, lands_thru, ffn_out[2][0])
    mix_red = _add_chips(mix_grads, mix_recvs, mix_rbufs, place)
    g_win, g_wout = [as_full(r) for r in _gather_halves(list(mix_red), "grad_gather_halves_mixer")]

    loss = a_red[4, 0]
    g_g1, g_g2, g_g3, g_g4 = a_red[0:1], a_red[1:2], a_red[2:3], a_red[3:4]
    g_meta = lax.dynamic_slice(a_red, (8, chip * meta_cols), (N_META, meta_cols))
    g_pscale = b_red[0:1]
    g_conv = lax.dynamic_slice(b_red, (1, chip * conv_cols), (3, conv_cols))
    g_poolw = c_red

    big_out = (_adamw_big([(w_in[0], g_win, m_w_in[0], v_w_in[0])])
               + _adamw_big([(w_out[0], g_wout, m_w_out[0], v_w_out[0])]) + ffn_out)
    big_out[2] = [o.T for o in big_out[2]]
    big_out[3] = [o.T for o in big_out[3]]
    g_wg, g_wu = g_wg_t.T, g_wu_t.T
    small_groups = [
        (meta_tokens, g_meta, m_meta_tokens, v_meta_tokens),
        (g1, g_g1, m_norm_mix_pre, v_norm_mix_pre),
        (conv_w[0], g_conv, m_conv_w[0], v_conv_w[0]),
        (pool_w.reshape(SMALL_C_ROWS, POOL_GROUP), g_poolw, m_pool_w.reshape(SMALL_C_ROWS, POOL_GROUP),
         v_pool_w.reshape(SMALL_C_ROWS, POOL_GROUP)),
        (pool_scale, g_pscale, m_pool_scale, v_pool_scale),
        (g2, g_g2, m_norm_mix_post, v_norm_mix_post),
        (g3, g_g3, m_norm_ffn_pre, v_norm_ffn_pre),
        (g4, g_g4, m_norm_ffn_post, v_norm_ffn_post),
    ]
    small_out = _adamw_small(small_groups)

    grads_out = [g_meta, g_g1, g_win[None], g_conv[None], g_poolw.reshape(pool_w.shape), g_pscale, g_wout[None],
                 g_g2, g_g3, g_wg[None], g_wu[None], g_wd[None], g_g4]
    s_meta, s_g1, s_conv, s_poolw, s_pscale, s_g2, s_g3, s_g4 = small_out
    b_win, b_wout, b_wg, b_wu, b_wd = big_out

    def leaf(k):
        return [s_meta[k], s_g1[k], b_win[k][None], s_conv[k][None], s_poolw[k].reshape(pool_w.shape), s_pscale[k],
                b_wout[k][None], s_g2[k], s_g3[k], b_wg[k][None], b_wu[k][None], b_wd[k][None], s_g4[k]]

    return (loss, grad_x, *grads_out, *leaf(0), *leaf(1), *leaf(2))
```

```python
import functools

import jax
import jax.numpy as jnp
from jax import lax
from jax.experimental import pallas as pl
from jax.experimental.pallas import tpu as pltpu

F32 = jnp.float32
BF16 = jnp.bfloat16
MESH = pl.DeviceIdType.MESH

D_MODEL = 1024
D_CONV = 512
D_POOL = 512
POOL_GROUP = 128
N_POOL_GROUPS = 4
D_IN_PROJ = 2048
D_FF = 2816
N_CHIPS = 4
FF_SHARD = D_FF // N_CHIPS
IN_SHARD = D_IN_PROJ // N_CHIPS
OUT_SHARD = D_MODEL // N_CHIPS
D_Z = 3 * IN_SHARD
N_META = 16
HALO = 16
RMS_EPS = 1e-6

ADAM_LR = 0.001
ADAM_B1 = 0.9
ADAM_B2 = 0.999
ADAM_EPS = 1e-08
ADAM_WD = 0.01
ADAM_STEP = 10

TM_MIX_FWD = 512
TM_MIX_BWD = 512
SUB_MIX_BWD = 512
TM_FFN = 256
TK_DW = 1024
FF_CHUNK = 1024
VMEM_LIMIT = 56 * 1024 * 1024


def _cparams(n_grid):
    return pltpu.CompilerParams(dimension_semantics=("arbitrary",) * n_grid, vmem_limit_bytes=VMEM_LIMIT)


def _dot(a, b):
    return jnp.dot(a, b, preferred_element_type=F32)


def _dot_nt(a, b):
    return lax.dot_general(a, b, (((1,), (1,)), ((), ())), preferred_element_type=F32)


def _dot_tn(a, b):
    return lax.dot_general(a, b, (((0,), (0,)), ((), ())), preferred_element_type=F32)


def _rows8(v):
    r, c = v.shape
    return v.reshape(r // 8, 8, c).sum(axis=0)


def _rstd(v):
    return lax.rsqrt(jnp.mean(v * v, axis=-1, keepdims=True) + RMS_EPS)


def _rms_bwd(dy, xhat, rstd, gain):
    dyg = dy * gain
    return rstd * (dyg - xhat * jnp.mean(dyg * xhat, axis=-1, keepdims=True))


def _sigmoid(v):
    return 1.0 / (1.0 + jnp.exp(-v))


def _gcols(g):
    return slice(g * POOL_GROUP, (g + 1) * POOL_GROUP)


def _window_sum(e, g, ahead):
    n = e.shape[0]
    w = e
    for level in range(g + 1):
        shift = 1 << level
        w = w + pltpu.roll(w, (n - shift) if ahead else shift, 0)
    return w


def _pool_fwd(pb, g, n):
    e = pb[0:HALO + n, _gcols(g)]
    return _window_sum(e, g, False)[HALO:, :] * (1.0 / (2 << g)) - e[HALO:, :]


def _pool_bwd(qb, g, r0, n):
    e = qb[r0:r0 + n + HALO, _gcols(g)]
    return _window_sum(e, g, True)[0:n, :] * (1.0 / (2 << g)) - e[0:n, :]


def _full(shape):
    nd = len(shape)
    return pl.BlockSpec(shape, lambda *_: (0,) * nd)


ANY = pl.BlockSpec(memory_space=pl.ANY)


def _mesh_pos():
    x, y, c = lax.axis_index("x"), lax.axis_index("y"), lax.axis_index("c")
    chips = [(1 - x, y), (x, 1 - y), (1 - x, 1 - y)]
    return x, y, c, chips


def _half(ref, h):
    hr = ref.shape[0] // 2
    return ref.at[pl.ds(h * hr, hr), :]


class _AllGather:
    PER_ARRAY = 9

    def __init__(self, ins, outs, send_sems, recv_sems):
        self.ins, self.outs, self.send_sems, self.recv_sems = ins, outs, send_sems, recv_sems
        self.n = len(ins)

    @classmethod
    def scratch(cls, n):
        return [pltpu.SemaphoreType.DMA((cls.PER_ARRAY * n,)), pltpu.SemaphoreType.DMA((cls.PER_ARRAY * n,))]

    @staticmethod
    def out_shape(shards):
        return [jax.ShapeDtypeStruct((N_CHIPS,) + s.shape, s.dtype) for s in shards]

    def _copy(self, a, k, src, dst, to):
        i = self.PER_ARRAY * a + k
        return pltpu.make_async_remote_copy(src_ref=src, dst_ref=dst, send_sem=self.send_sems.at[i],
                                            recv_sem=self.recv_sems.at[i], device_id=to, device_id_type=MESH)

    def _piece(self, a, chip, piece, h=None):
        h = lax.axis_index("c") if h is None else h
        rows = self.ins[a].shape[0] // 4
        return self.outs[a].at[chip].at[pl.ds((2 * h + piece) * rows, rows), :]

    def _own(self, a, k):
        x, y, c, chips = _mesh_pos()
        piece = (1, 0, 0, 1)[k]
        rows = self.ins[a].shape[0] // 4
        src = self.ins[a].at[pl.ds((2 * c + piece) * rows, rows), :]
        return self._copy(a, k, src, self._piece(a, 2 * x + y, piece), (*chips[k // 2], c))

    def _relay(self, a, k):
        x, y, c, chips = _mesh_pos()
        source, to, piece = (chips[1], chips[0], 0) if k == 4 else (chips[0], chips[1], 1)
        rows = self._piece(a, 2 * source[0] + source[1], piece)
        return self._copy(a, k, rows, rows, (*to, c))

    def _sibling(self, a, k, h):
        x, y, c, chips = _mesh_pos()
        chip = chips[k - 6]
        slot = _half(self.outs[a].at[2 * chip[0] + chip[1]], h)
        return self._copy(a, k, slot, slot, (x, y, 1 - c))

    def start(self, arrays=None):
        for a in (range(self.n) if arrays is None else arrays):
            for k in range(4):
                self._own(a, k).start()

    def relay(self, a):
        self._own(a, 2).wait_recv()
        self._relay(a, 4).start()
        self._own(a, 0).wait_recv()
        self._relay(a, 5).start()

    def forward(self, a):
        c = lax.axis_index("c")
        self._own(a, 1).wait_recv()
        self._sibling(a, 6, c).start()
        self._own(a, 3).wait_recv()
        self._sibling(a, 7, c).start()
        self._relay(a, 4).wait_recv()
        self._relay(a, 5).wait_recv()
        self._sibling(a, 8, c).start()

    def finish(self, arrays=None):
        c = lax.axis_index("c")
        arrays = range(self.n) if arrays is None else arrays
        for a in arrays:
            for k in range(6, 9):
                self._sibling(a, k, 1 - c).wait_recv()
        for a in arrays:
            for k in range(4):
                self._own(a, k).wait_send()
            for k in range(4, 6):
                self._relay(a, k).wait_send()
            for k in range(6, 9):
                self._sibling(a, k, c).wait_send()


class _ExchangeHalves:
    def __init__(self, ins, recvs, send_sems, recv_sems):
        self.ins, self.recvs, self.send_sems, self.recv_sems = ins, recvs, send_sems, recv_sems

    @staticmethod
    def scratch(n):
        return [pltpu.SemaphoreType.DMA((n,)), pltpu.SemaphoreType.DMA((n,))]

    @staticmethod
    def out_shape(grads):
        return [jax.ShapeDtypeStruct((g.shape[0], g.shape[1] // 2, g.shape[2]), g.dtype) for g in grads]

    def _copies(self):
        x, y, c, _ = _mesh_pos()
        out = []
        for a, (src, dst) in enumerate(zip(self.ins, self.recvs)):
            hr = src.shape[1] // 2
            out.append(pltpu.make_async_remote_copy(
                src_ref=src.at[:, pl.ds((1 - c) * hr, hr), :], dst_ref=dst, send_sem=self.send_sems.at[a],
                recv_sem=self.recv_sems.at[a], device_id=(x, y, 1 - c), device_id_type=MESH))
        return out

    def start(self):
        for cp in self._copies():
            cp.start()

    def finish(self):
        for cp in self._copies():
            cp.wait()


def _exchange_halves(grads):
    n = len(grads)

    def body(*refs):
        ex = _ExchangeHalves(refs[:n], refs[n:2 * n], *refs[2 * n:])
        ex.start()
        ex.finish()

    return pl.pallas_call(
        body, name="grad_exchange_halves", out_shape=_ExchangeHalves.out_shape(grads),
        in_specs=[ANY] * n, out_specs=[ANY] * n, scratch_shapes=_ExchangeHalves.scratch(n),
    )(*grads)


class _ScatterToChips:
    def __init__(self, ins, rbufs, send_sems, recv_sems):
        self.ins, self.rbufs, self.send_sems, self.recv_sems = ins, rbufs, send_sems, recv_sems

    @staticmethod
    def scratch(n):
        return [pltpu.SemaphoreType.DMA((3 * n,)), pltpu.SemaphoreType.DMA((3 * n,))]

    @staticmethod
    def out_shape(sums):
        return [jax.ShapeDtypeStruct((3,) + s.shape[1:], BF16) for s in sums]

    def _copies(self):
        x, y, c, chips = _mesh_pos()
        out = []
        for a, (src, dst) in enumerate(zip(self.ins, self.rbufs)):
            for k, chip in enumerate(chips):
                out.append(pltpu.make_async_remote_copy(
                    src_ref=src.at[2 * chip[0] + chip[1]], dst_ref=dst.at[k], send_sem=self.send_sems.at[3 * a + k],
                    recv_sem=self.recv_sems.at[3 * a + k], device_id=(*chip, c), device_id_type=MESH))
        return out

    def start(self):
        for cp in self._copies():
            cp.start()

    def finish(self):
        for cp in self._copies():
            cp.wait()


HBM = pl.BlockSpec(memory_space=pltpu.HBM)
SEM = pl.BlockSpec(memory_space=pltpu.SEMAPHORE)


class _SplitComm:
    def __init__(self, name, exchanged, scattered):
        self.name, self.n_ex, self.n_sc = name, len(exchanged), len(scattered)
        self.n_copies = self.n_ex + 3 * self.n_sc
        zones = ([lax.empty((g.shape[0], g.shape[1] // 2, g.shape[2]), g.dtype) for g in exchanged]
                 + [lax.empty((3,) + s.shape[1:], s.dtype) for s in scattered])
        self.buffers = [pltpu.with_memory_space_constraint(v, pltpu.HBM)
                        for v in list(exchanged) + list(scattered) + zones]

    def _copies(self, bufs, send_sems, recv_sems):
        x, y, c, chips = _mesh_pos()
        n_src = self.n_ex + self.n_sc
        out = []
        for a in range(self.n_ex):
            hr = bufs[a].shape[1] // 2
            out.append(pltpu.make_async_remote_copy(
                src_ref=bufs[a].at[:, pl.ds((1 - c) * hr, hr), :], dst_ref=bufs[n_src + a], send_sem=send_sems[a],
                recv_sem=recv_sems[a], device_id=(x, y, 1 - c), device_id_type=MESH))
        for a in range(self.n_sc):
            for k, chip in enumerate(chips):
                i = self.n_ex + 3 * a + k
                out.append(pltpu.make_async_remote_copy(
                    src_ref=bufs[self.n_ex + a].at[2 * chip[0] + chip[1]], dst_ref=bufs[n_src + self.n_ex + a].at[k],
                    send_sem=send_sems[i], recv_sem=recv_sems[i], device_id=(*chip, c), device_id_type=MESH))
        return out

    def start(self):
        n_buf, n_cp = len(self.buffers), self.n_copies

        def body(*refs):
            bufs = refs[:n_buf]
            send_sems, recv_sems = refs[n_buf:n_buf + n_cp], refs[n_buf + n_cp:n_buf + 2 * n_cp]
            for cp in self._copies(bufs, send_sems, recv_sems):
                cp.start()
            refs[-1][...] = jnp.zeros_like(refs[-1])

        outs = pl.pallas_call(
            body, name=self.name + "_start",
            out_shape=[pltpu.SemaphoreType.DMA(())] * (2 * n_cp) + [pltpu.HBM(b.shape, b.dtype) for b in self.buffers]
            + [jax.ShapeDtypeStruct((8, 128), F32)],
            in_specs=[HBM] * n_buf, out_specs=[SEM] * (2 * n_cp) + [HBM] * n_buf + [pl.BlockSpec(memory_space=pltpu.VMEM)],
            input_output_aliases={i: 2 * n_cp + i for i in range(n_buf)},
            compiler_params=pltpu.CompilerParams(has_side_effects=pltpu.SideEffectType.DATAFLOW_SIDE_EFFECTING),
        )(*self.buffers)
        self.sems, self.buffers = outs[:2 * n_cp], outs[2 * n_cp:2 * n_cp + n_buf]
        return outs[-1]

    def wait(self, after):
        n_buf, n_cp = len(self.buffers), self.n_copies

        def body(*refs):
            bufs = refs[:n_buf]
            send_sems, recv_sems = refs[n_buf:n_buf + n_cp], refs[n_buf + n_cp:n_buf + 2 * n_cp]
            for cp in self._copies(bufs, send_sems, recv_sems):
                cp.wait_send()
                cp.wait_recv()

        outs = pl.pallas_call(
            body, name=self.name + "_wait", out_shape=[pltpu.HBM(b.shape, b.dtype) for b in self.buffers],
            in_specs=[HBM] * n_buf + [SEM] * (2 * n_cp) + [ANY], out_specs=[HBM] * n_buf,
            input_output_aliases={i: i for i in range(n_buf)},
            compiler_params=pltpu.CompilerParams(has_side_effects=pltpu.SideEffectType.DATAFLOW_SIDE_EFFECTING),
        )(*self.buffers, *self.sems, after)
        zones = outs[self.n_ex + self.n_sc:]
        return zones[:self.n_ex], zones[self.n_ex:]


def _gather_halves(halves, name):
    n = len(halves)

    def body(*refs):
        ins, outs = refs[:n], refs[n:2 * n]
        send_sems, recv_sems = refs[2 * n:]
        x, y, c, _ = _mesh_pos()
        sib = (x, y, 1 - c)
        remote = [pltpu.make_async_remote_copy(src_ref=ins[a].at[c], dst_ref=outs[a].at[c],
                                               send_sem=send_sems.at[a], recv_sem=recv_sems.at[a],
                                               device_id=sib, device_id_type=MESH) for a in range(n)]
        for cp in remote:
            cp.start()
        for a in range(n):
            pltpu.make_async_remote_copy(src_ref=ins[a].at[1 - c], dst_ref=outs[a].at[1 - c], send_sem=send_sems.at[a],
                                         recv_sem=recv_sems.at[a], device_id=sib, device_id_type=MESH).wait_recv()
        for cp in remote:
            cp.wait_send()

    return pl.pallas_call(
        body, name=name,
        out_shape=[jax.ShapeDtypeStruct(h.shape, F32) for h in halves],
        in_specs=[ANY] * n, out_specs=[ANY] * n, input_output_aliases={a: a for a in range(n)},
        scratch_shapes=[pltpu.SemaphoreType.DMA((n,)), pltpu.SemaphoreType.DMA((n,))],
    )(*halves)


SMALL_A_ROWS = 24
SMALL_B_ROWS = 8
SMALL_C_ROWS = N_POOL_GROUPS * POOL_GROUP


class _AllReduceSmall:
    N_IN = 10
    SHAPES = [(SMALL_A_ROWS, D_MODEL), (SMALL_B_ROWS, D_CONV), (SMALL_C_ROWS, POOL_GROUP)]

    def __init__(self, ins, outs, scratch):
        self.ins, self.outs = ins, outs
        self.bufs, self.rcvs, self.send_sems, self.recv_sems = scratch[:3], scratch[3:6], scratch[6], scratch[7]

    @classmethod
    def scratch(cls):
        return ([pltpu.VMEM((3,) + s, F32) for s in cls.SHAPES] + [pltpu.VMEM((3,) + s, F32) for s in cls.SHAPES]
                + [pltpu.SemaphoreType.DMA((9,)), pltpu.SemaphoreType.DMA((9,))])

    @classmethod
    def out_shape(cls):
        return [jax.ShapeDtypeStruct(s, F32) for s in cls.SHAPES]

    def _copies(self, st):
        x, y, c, _ = _mesh_pos()
        peer = [(x, y, 1 - c), (1 - x, y, c), (x, 1 - y, c)][st]
        return [pltpu.make_async_remote_copy(
            src_ref=buf.at[st], dst_ref=rcv.at[st], send_sem=self.send_sems.at[3 * st + i],
            recv_sem=self.recv_sems.at[3 * st + i], device_id=peer, device_id_type=MESH)
            for i, (buf, rcv) in enumerate(zip(self.bufs, self.rcvs))]

    def pack_and_send(self):
        dg1_ref, dg1m_ref, dg2_ref, dg3_ref, dg4_ref, loss_ref, dmeta_ref, dsc_ref, dcw_ref, dpw_ref = self.ins
        a_buf, b_buf, c_buf = self.bufs

        def rowsum(v):
            return jnp.sum(v, axis=0, keepdims=True)

        a_buf[0, 0:1, :] = rowsum(dg1_ref[...] + dg1m_ref[...])
        a_buf[0, 1:2, :] = rowsum(dg2_ref[...])
        a_buf[0, 2:3, :] = rowsum(dg3_ref[...])
        a_buf[0, 3:4, :] = rowsum(dg4_ref[...])
        loss = jnp.sum(rowsum(loss_ref[...]), axis=1, keepdims=True) * (0.5 / D_MODEL)
        a_buf[0, 4:5, :] = jnp.broadcast_to(loss, (1, D_MODEL))
        a_buf[0, 5:8, :] = jnp.zeros((3, D_MODEL), F32)
        a_buf[0, 8:24, :] = dmeta_ref[...]
        b_buf[0, 0:1, :] = rowsum(dsc_ref[...])
        for k in range(3):
            b_buf[0, 1 + k:2 + k, :] = rowsum(dcw_ref[8 * k:8 * k + 8, :])
        b_buf[0, 4:8, :] = jnp.zeros((4, D_CONV), F32)
        c_buf[0] = dpw_ref[...]
        for cp in self._copies(0):
            cp.start()

    def combine(self, st):
        for cp in self._copies(st):
            cp.wait()
        if st < 2:
            for buf, rcv in zip(self.bufs, self.rcvs):
                buf[st + 1] = buf[st] + rcv[st]
            for cp in self._copies(st + 1):
                cp.start()
        else:
            for out, buf, rcv in zip(self.outs, self.bufs, self.rcvs):
                out[...] = buf[st] + rcv[st]


def _row_block(rows):
    for cand in (512, 448, 384, 352, 320, 256, 128, 64, 32, 16):
        if rows % cand == 0:
            return cand
    return rows


def _add_pairs_multi(grads, recvs, place):
    n = len(grads)
    n_sh = grads[0].shape[0]
    halves = [g.shape[1] // 2 for g in grads]
    n_steps = halves[0] // _row_block(halves[0])
    blocks = [(hr // n_steps, g.shape[2]) for hr, g in zip(halves, grads)]

    def body(place_ref, *refs):
        for a_ref, b_ref, o_ref in zip(refs[:n], refs[n:2 * n], refs[2 * n:]):
            o_ref[...] = (a_ref[0] + b_ref[...]).astype(BF16)

    return pl.pallas_call(
        body, name="grad_add_pairs",
        grid_spec=pltpu.PrefetchScalarGridSpec(
            num_scalar_prefetch=1, grid=(n_sh, n_steps),
            in_specs=[pl.BlockSpec((1, 1, br, cols), lambda j, i, p: (j, p[1], i, 0)) for br, cols in blocks]
            + [pl.BlockSpec((1, br, cols), lambda j, i, p: (j, i, 0)) for br, cols in blocks],
            out_specs=[pl.BlockSpec((1, br, cols), lambda j, i, p: (j, i, 0)) for br, cols in blocks]),
        out_shape=[jax.ShapeDtypeStruct((n_sh, hr, g.shape[2]), BF16) for hr, g in zip(halves, grads)],
        compiler_params=_cparams(2),
    )(place, *[g.reshape(n_sh, 2, hr, g.shape[2]) for hr, g in zip(halves, grads)], *recvs)


def _add_pairs(grad, recv, place):
    return _add_pairs_multi([grad], [recv], place)[0]


def _add_chips(grads, recvs, rbufs, place, after=None, name="grad_add_chips"):
    n = len(grads)
    n_sh = grads[0].shape[0]
    halves = [g.shape[1] // 2 for g in grads]
    n_steps = halves[0] // _row_block(halves[0])
    blocks = [(hr // n_steps, g.shape[2]) for hr, g in zip(halves, grads)]
    extra = [] if after is None else [after]

    def body(place_ref, *refs):
        for a_ref, b_ref, r_ref, o_ref in zip(refs[:n], refs[n:2 * n], refs[2 * n:3 * n], refs[3 * n + len(extra):]):
            own = a_ref[0, 0] + b_ref[0]
            o_ref[0] = ((own + r_ref[0].astype(F32)) + r_ref[1].astype(F32)) + r_ref[2].astype(F32)

    return pl.pallas_call(
        body, name=name,
        grid_spec=pltpu.PrefetchScalarGridSpec(
            num_scalar_prefetch=1, grid=(n_steps,),
            in_specs=[pl.BlockSpec((1, 1, br, cols), lambda i, p: (p[0], p[1], i, 0)) for br, cols in blocks]
            + [pl.BlockSpec((1, br, cols), lambda i, p: (p[0], i, 0)) for br, cols in blocks]
            + [pl.BlockSpec((3, br, cols), lambda i, p: (0, i, 0)) for br, cols in blocks]
            + [pl.BlockSpec((8, 128), lambda i, p: (0, 0))] * len(extra),
            out_specs=[pl.BlockSpec((1, br, cols), lambda i, p: (p[1], i, 0)) for br, cols in blocks]),
        out_shape=[jax.ShapeDtypeStruct((2, hr, g.shape[2]), F32) for hr, g in zip(halves, grads)],
        compiler_params=_cparams(1),
    )(place, *[g.reshape(n_sh, 2, hr, g.shape[2]) for hr, g in zip(halves, grads)], *recvs, *rbufs, *extra)


def _adamw_math(w, g, m, v):
    m2 = ADAM_B1 * m + (1.0 - ADAM_B1) * g
    v2 = ADAM_B2 * v + (1.0 - ADAM_B2) * (g * g)
    m_hat = m2 / (1.0 - ADAM_B1 ** ADAM_STEP)
    v_hat = v2 / (1.0 - ADAM_B2 ** ADAM_STEP)
    delta = -ADAM_LR * (m_hat / (jnp.sqrt(v_hat) + ADAM_EPS) + ADAM_WD * w)
    return delta, m2, v2


def _adamw_big(groups):
    n = len(groups)
    rows, cols = groups[0][0].shape
    br = _row_block(rows)
    if n > 1 and br % 16 == 0:
        br //= 2

    def body(*refs):
        for i in range(n):
            w_ref, g_ref, m_ref, v_ref = refs[4 * i:4 * i + 4]
            d_ref, m2_ref, v2_ref = refs[4 * n + 3 * i:4 * n + 3 * i + 3]
            d, m2, v2 = _adamw_math(w_ref[...], g_ref[...], m_ref[...], v_ref[...])
            d_ref[...] = d
            m2_ref[...] = m2
            v2_ref[...] = v2

    spec = pl.BlockSpec((br, cols), lambda i: (i, 0))
    outs = pl.pallas_call(
        body, name="adamw_big", grid=(rows // br,),
        out_shape=[jax.ShapeDtypeStruct((rows, cols), F32)] * (3 * n),
        in_specs=[spec] * (4 * n), out_specs=[spec] * (3 * n), compiler_params=_cparams(1),
    )(*[a for grp in groups for a in grp])
    return [list(outs[3 * i:3 * i + 3]) for i in range(n)]


def _adamw_small(groups):
    n = len(groups)

    def body(*refs):
        ins, outs = refs[:4 * n], refs[4 * n:]
        for i in range(n):
            w, g, m, v = (r[...] for r in ins[4 * i:4 * i + 4])
            d, m2, v2 = _adamw_math(w, g, m, v)
            outs[3 * i][...] = d
            outs[3 * i + 1][...] = m2
            outs[3 * i + 2][...] = v2

    vm = pl.BlockSpec(memory_space=pltpu.VMEM)
    flat = [a for grp in groups for a in grp]
    out_shape = [jax.ShapeDtypeStruct(grp[0].shape, F32) for grp in groups for _ in range(3)]
    outs = pl.pallas_call(body, name="adamw_small", out_shape=out_shape,
                          in_specs=[vm] * (4 * n), out_specs=[vm] * (3 * n))(*flat)
    return [tuple(outs[3 * i:3 * i + 3]) for i in range(n)]


def _load_gathered(gathered, shards, dst_slots, sems):
    n = len(gathered)
    me = 2 * lax.axis_index("x") + lax.axis_index("y")

    def copies(j, own):
        return [pltpu.make_async_copy(shards[a] if own else gathered[a].at[j], dst_slots[a](j), sems.at[n * j + a])
                for a in range(n)]

    for wait in (False, True):
        for j in range(N_CHIPS):
            for own in (False, True):
                @pl.when((me == j) == own)
                def _():
                    for cp in copies(j, own):
                        cp.wait() if wait else cp.start()


N_MIX_SHARDS = 3


def _mixer_fwd(x3, g1, g2, poolw, pscale, shards):
    n_seq, seq, _ = x3.shape
    tm = min(TM_MIX_FWD, seq)
    n_t = seq // tm
    n_steps = n_seq * n_t
    n_ag = len(shards)
    n_ffn = n_ag - N_MIX_SHARDS
    small_rows = shards[2].shape[0]
    conv_cols = D_CONV // N_CHIPS

    def body(x_ref, g1_ref, g2_ref, pw_ref, ps_ref, *rest):
        ag = _AllGather(rest[:n_ag], rest[n_ag + 10:2 * n_ag + 10], *rest[-2:])
        (z_ref, m_ref, h1_ref, a_ref, conv_ref, pooled_ref, yc_ref, zm_ref, meta_ref,
         cw_ref) = rest[n_ag:n_ag + 10]
        win_v, wout_v, small_v, cvb, pb, load_sems = rest[2 * n_ag + 10:-2]
        s, t = pl.program_id(0), pl.program_id(1)
        step = s * n_t + t

        @pl.when(step == 0)
        def _():
            ag.start(range(N_MIX_SHARDS))
            for a in range(N_MIX_SHARDS):
                ag.relay(a)
            for a in range(N_MIX_SHARDS):
                ag.forward(a)
            ag.finish(range(N_MIX_SHARDS))
            ag.start(range(N_MIX_SHARDS, n_ag))
            _load_gathered(ag.outs[:N_MIX_SHARDS], ag.ins[:N_MIX_SHARDS],
                           [lambda j: win_v.at[j], lambda j: wout_v.at[pl.ds(j * OUT_SHARD, OUT_SHARD), :],
                            lambda j: small_v.at[j]], load_sems)

            meta = jnp.concatenate([small_v[j, 0:N_META, :] for j in range(N_CHIPS)], axis=1)
            meta_ref[...] = meta
            cw_ref[...] = jnp.concatenate([small_v[j, N_META:N_META + 3, 0:conv_cols] for j in range(N_CHIPS)], axis=1)
            a_meta = (meta * _rstd(meta) * g1_ref[...]).astype(BF16)
            for j in range(N_CHIPS):
                zm_ref[:, j * IN_SHARD:(j + 1) * IN_SHARD] = _dot(a_meta, win_v[j])

        for i in range(n_ffn):
            @pl.when(step == ((i + 1) * n_steps) // (2 * n_ffn + 2))
            def _():
                ag.relay(N_MIX_SHARDS + i)

        for i in range(n_ffn):
            @pl.when(step == min(n_steps // 2 + ((i + 1) * n_steps) // (2 * n_ffn + 2), n_steps - 1))
            def _():
                ag.forward(N_MIX_SHARDS + i)

        @pl.when(t == 0)
        def _():
            cvb[0:HALO, :] = zm_ref[:, IN_SHARD:2 * IN_SHARD] * zm_ref[:, 2 * IN_SHARD:3 * IN_SHARD]
            pb[0:HALO, :] = zm_ref[:, 3 * IN_SHARD:4 * IN_SHARD]

        @pl.when(t > 0)
        def _():
            cvb[0:HALO, :] = cvb[tm:tm + HALO, :]
            pb[0:HALO, :] = pb[tm:tm + HALO, :]

        xt = x_ref[0]
        a = (xt * _rstd(xt) * g1_ref[...]).astype(BF16)
        a_ref[...] = a
        zb = _dot(a, win_v[0])
        zc = _dot(a, win_v[1])
        zv = _dot(a, win_v[2])
        zp = _dot(a, win_v[3])
        z_ref[0, :, 0:IN_SHARD] = zb
        z_ref[0, :, IN_SHARD:2 * IN_SHARD] = zc
        z_ref[0, :, 2 * IN_SHARD:3 * IN_SHARD] = zv
        cv = zc * zv
        cvb[HALO:HALO + tm, :] = cv
        pb[HALO:HALO + tm, :] = zp
        cw = cw_ref[...]
        conv = cw[0:1] * cvb[HALO - 2:HALO - 2 + tm, :] + cw[1:2] * cvb[HALO - 1:HALO - 1 + tm, :] + cw[2:3] * cv
        conv_ref[...] = conv
        parts = [(zb * conv).astype(BF16)]
        for g in range(N_POOL_GROUPS):
            pooled = _pool_fwd(pb, g, tm).astype(BF16)
            pooled_ref[:, _gcols(g)] = pooled
            parts.append((_dot(pooled, pw_ref[g]) * ps_ref[:, _gcols(g)]).astype(BF16))
        ycat = jnp.concatenate(parts, axis=1)
        yc_ref[...] = ycat
        m = _dot(ycat, wout_v[...])
        m_ref[0] = m
        h1_ref[0] = xt + m * _rstd(m) * g2_ref[...]

        @pl.when(step == n_steps - 1)
        def _():
            ag.finish(range(N_MIX_SHARDS, n_ag))

    n_rows = n_seq * seq
    row = lambda c: pl.BlockSpec((1, tm, c), lambda s, t: (s, t, 0))
    row2 = lambda c: pl.BlockSpec((tm, c), lambda s, t: (s * n_t + t, 0))
    outs = pl.pallas_call(
        body, name="mixer_fwd", grid=(n_seq, n_t),
        out_shape=[jax.ShapeDtypeStruct((n_seq, seq, D_Z), F32), jax.ShapeDtypeStruct((n_seq, seq, D_MODEL), F32),
                   jax.ShapeDtypeStruct((n_seq, seq, D_MODEL), F32), jax.ShapeDtypeStruct((n_rows, D_MODEL), BF16),
                   jax.ShapeDtypeStruct((n_rows, D_CONV), F32), jax.ShapeDtypeStruct((n_rows, D_POOL), BF16),
                   jax.ShapeDtypeStruct((n_rows, D_MODEL), BF16), jax.ShapeDtypeStruct((N_META, D_IN_PROJ), F32),
                   jax.ShapeDtypeStruct((N_META, D_MODEL), F32), jax.ShapeDtypeStruct((3, D_CONV), F32)]
        + _AllGather.out_shape(shards),
        in_specs=[row(D_MODEL), _full((1, D_MODEL)), _full((1, D_MODEL)),
                  _full((N_POOL_GROUPS, POOL_GROUP, POOL_GROUP)), _full((1, D_POOL))] + [ANY] * n_ag,
        out_specs=[row(D_Z), row(D_MODEL), row(D_MODEL), row2(D_MODEL), row2(D_CONV), row2(D_POOL), row2(D_MODEL),
                   _full((N_META, D_IN_PROJ)), _full((N_META, D_MODEL)), _full((3, D_CONV))] + [ANY] * n_ag,
        scratch_shapes=[pltpu.VMEM((N_CHIPS, D_MODEL, IN_SHARD), BF16), pltpu.VMEM((D_MODEL, D_MODEL), BF16),
                        pltpu.VMEM((N_CHIPS, small_rows, D_MODEL // N_CHIPS), F32),
                        pltpu.VMEM((HALO + tm, D_CONV), F32), pltpu.VMEM((HALO + tm, D_POOL), F32),
                        pltpu.SemaphoreType.DMA((N_MIX_SHARDS * N_CHIPS,))] + _AllGather.scratch(n_ag),
        compiler_params=_cparams(2),
    )(x3, g1, g2, poolw, pscale, *shards)
    return outs[:10], outs[10:]


def _ffn_chunks():
    out, r0 = [], 0
    while r0 < D_FF:
        out.append((r0, min(FF_CHUNK, D_FF - r0)))
        r0 += FF_CHUNK
    return out


def _ffn_fwd_bwd(h1, target, g3, g4, gathered, shards):
    n_rows = h1.shape[0]
    tm = min(TM_FFN, n_rows)
    chunks = _ffn_chunks()

    def body(h1_ref, t_ref, g3_ref, g4_ref, wg_all, wu_all, wd_all, wg_s, wu_s, wd_s,
             dh1_ref, f_ref, dd_ref, ds_ref, du_ref, gg_ref, loss_ref, dg3_ref, dg4_ref,
             wg_v, wu_v, wd_v, s_sc, u_sc, sems):
        @pl.when(pl.program_id(0) == 0)
        def _():
            _load_gathered([wg_all, wu_all, wd_all], [wg_s, wu_s, wd_s],
                           [functools.partial(lambda v, j: v.at[pl.ds(j * FF_SHARD, FF_SHARD), :], v)
                            for v in (wg_v, wu_v, wd_v)], sems)
            loss_ref[...] = jnp.zeros_like(loss_ref)
            dg3_ref[...] = jnp.zeros_like(dg3_ref)
            dg4_ref[...] = jnp.zeros_like(dg4_ref)

        h1v = h1_ref[...]
        r3 = _rstd(h1v)
        hh = h1v * r3
        g3v, g4v = g3_ref[...], g4_ref[...]
        f = (hh * g3v).astype(BF16)
        f_ref[...] = f
        d = jnp.zeros((tm, D_MODEL), F32)
        for r0, sz in chunks:
            s = _dot_nt(f, wg_v[r0:r0 + sz, :])
            u = _dot_nt(f, wu_v[r0:r0 + sz, :])
            s_sc[:, r0:r0 + sz] = s
            u_sc[:, r0:r0 + sz] = u
            gc = (s * _sigmoid(s) * u).astype(BF16)
            gg_ref[:, r0:r0 + sz] = gc
            d = d + _dot(gc, wd_v[r0:r0 + sz, :])
        r4 = _rstd(d)
        dh = d * r4
        err = (h1v + dh * g4v) - t_ref[...]
        loss_ref[...] += _rows8(err * err)
        dy = err * (1.0 / D_MODEL)
        dg4_ref[...] += _rows8(dy * dh)
        ddb = _rms_bwd(dy, dh, r4, g4v).astype(BF16)
        dd_ref[...] = ddb
        df = jnp.zeros((tm, D_MODEL), F32)
        for r0, sz in chunks:
            dgg = _dot_nt(ddb, wd_v[r0:r0 + sz, :])
            s = s_sc[:, r0:r0 + sz]
            u = u_sc[:, r0:r0 + sz]
            sig = _sigmoid(s)
            dsc = (dgg * u * (sig * (1.0 + s * (1.0 - sig)))).astype(BF16)
            duc = (dgg * (s * sig)).astype(BF16)
            ds_ref[:, r0:r0 + sz] = dsc
            du_ref[:, r0:r0 + sz] = duc
            df = df + _dot(dsc, wg_v[r0:r0 + sz, :]) + _dot(duc, wu_v[r0:r0 + sz, :])
        dg3_ref[...] += _rows8(df * hh)
        dh1_ref[...] = dy + _rms_bwd(df, hh, r3, g3v)

    row = pl.BlockSpec((tm, D_MODEL), lambda i: (i, 0))
    ffrow = pl.BlockSpec((tm, D_FF), lambda i: (i, 0))
    acc = _full((8, D_MODEL))
    act_bf = jax.ShapeDtypeStruct((n_rows, D_MODEL), BF16)
    ff_bf = jax.ShapeDtypeStruct((n_rows, D_FF), BF16)
    acc_shape = jax.ShapeDtypeStruct((8, D_MODEL), F32)
    w_vmem = pltpu.VMEM((D_FF, D_MODEL), BF16)
    return pl.pallas_call(
        body, name="ffn_fwd_bwd", grid=(n_rows // tm,),
        out_shape=[jax.ShapeDtypeStruct((n_rows, D_MODEL), F32), act_bf, act_bf, ff_bf, ff_bf, ff_bf,
                   acc_shape, acc_shape, acc_shape],
        in_specs=[row, row, _full((1, D_MODEL)), _full((1, D_MODEL))] + [ANY] * 6,
        out_specs=[row, row, row, ffrow, ffrow, ffrow, acc, acc, acc],
        scratch_shapes=[w_vmem, w_vmem, w_vmem, pltpu.VMEM((tm, D_FF), F32), pltpu.VMEM((tm, D_FF), F32),
                        pltpu.SemaphoreType.DMA((3 * N_CHIPS,))],
        compiler_params=_cparams(1),
    )(h1, target, g3, g4, *gathered, *shards)


def _ffn_weight_grads(name, acts, other, exchanged):
    n_rows = other.shape[0]
    n_a, n_ex = len(acts), len(exchanged)
    n_c = n_a
    tk = min(TK_DW, n_rows)
    n_k = n_rows // tk
    half = D_FF // n_c

    def body(other_ref, *rest):
        act_refs = rest[:n_a]
        out_refs = rest[n_a + n_ex:2 * n_a + n_ex]
        c, k = pl.program_id(0), pl.program_id(1)
        if n_ex:
            ex = _ExchangeHalves(rest[n_a:n_a + n_ex], rest[2 * n_a + n_ex:2 * n_a + 2 * n_ex], *rest[-2:])

            @pl.when((c == 0) & (k == 0))
            def _():
                ex.start()

        @pl.when(k == 0)
        def _():
            for o in out_refs:
                o[...] = jnp.zeros_like(o)

        ov = other_ref[...]
        for a, o in zip(act_refs, out_refs):
            o[...] += _dot_tn(a[...], ov)

        if n_ex:
            @pl.when((c == n_c - 1) & (k == n_k - 1))
            def _():
                ex.finish()

    row = pl.BlockSpec((tk, D_MODEL), lambda c, k: (k, 0))
    ffrow = pl.BlockSpec((tk, half), lambda c, k: (k, c))
    out = pl.BlockSpec((half, D_MODEL), lambda c, k: (c, 0))
    outs = pl.pallas_call(
        body, name=name, grid=(n_c, n_k),
        out_shape=[jax.ShapeDtypeStruct((D_FF, D_MODEL), F32)] * n_a + _ExchangeHalves.out_shape(exchanged),
        in_specs=[row] + [ffrow] * n_a + [ANY] * n_ex, out_specs=[out] * n_a + [ANY] * n_ex,
        scratch_shapes=_ExchangeHalves.scratch(n_ex) if n_ex else [],
        compiler_params=_cparams(2),
    )(other, *acts, *exchanged)
    return outs[:n_a], outs[n_a:]


def _mixer_bwd(dh1, m3, z3, conv2, pooled2, x3, zmeta, meta_full, g1, g2, convw, poolw, pscale, gathered, shards,
               after):
    n_seq, seq, _ = x3.shape
    tm = min(TM_MIX_BWD, seq)
    sub = min(SUB_MIX_BWD, tm)
    n_t = seq // tm
    n_out = 13

    def body(dh1_ref, m_ref, z_ref, conv_ref, pooled_ref, x_ref, zm_ref, meta_ref, g1_ref, g2_ref, cw_ref, pw_ref,
             ps_ref, after_ref, win_all, wout_all, win_s, wout_s, *rest):
        (dx_ref, dz_ref, dm_ref, dg1_ref, dg2_ref, dsc_ref, dcw_ref, dpw_ref, dzm_ref, dmeta_ref, dg1m_ref, am_ref,
         dzmb_ref) = rest[:n_out]
        win_v, wout_v, dcb, dqb, mcb, mqb, load_sems = rest[n_out:]
        s, i = pl.program_id(0), pl.program_id(1)
        tr = n_t - 1 - i

        @pl.when((s == 0) & (i == 0))
        def _():
            _load_gathered([win_all, wout_all], [win_s, wout_s],
                           [lambda j: win_v.at[j], lambda j: wout_v.at[pl.ds(j * OUT_SHARD, OUT_SHARD), :]], load_sems)
            for ref in (dg1_ref, dg2_ref, dsc_ref, dcw_ref, dpw_ref, dzm_ref):
                ref[...] = jnp.zeros_like(ref)

        @pl.when(i == 0)
        def _():
            dcb[tm:tm + HALO, :] = jnp.zeros((HALO, D_CONV), F32)
            dqb[tm:tm + HALO, :] = jnp.zeros((HALO, D_POOL), F32)

        @pl.when(i > 0)
        def _():
            dcb[tm:tm + HALO, :] = dcb[0:HALO, :]
            dqb[tm:tm + HALO, :] = dqb[0:HALO, :]

        g1v, g2v = g1_ref[...], g2_ref[...]
        cw = cw_ref[...]

        for r0 in range(tm - sub, -1, -sub):
            rows = slice(r0, r0 + sub)
            dh1v = dh1_ref[0, rows, :]
            mv = m_ref[0, rows, :]
            r2 = _rstd(mv)
            mh = mv * r2
            dg2_ref[...] += _rows8(dh1v * mh)
            dmb = _rms_bwd(dh1v, mh, r2, g2v).astype(BF16)
            dm_ref[rows, :] = dmb
            dyc = _dot_nt(dmb, wout_v[...])
            dyconv = dyc[:, 0:D_CONV]

            for g in range(N_POOL_GROUPS):
                pooled = pooled_ref[rows, _gcols(g)]
                mixed = _dot(pooled, pw_ref[g])
                scale = ps_ref[:, _gcols(g)]
                dyp = dyc[:, D_CONV + g * POOL_GROUP:D_CONV + (g + 1) * POOL_GROUP]
                dsc_ref[:, _gcols(g)] += _rows8(dyp * mixed)
                dmix = (dyp * scale).astype(BF16)
                dpw_ref[g] += _dot_tn(pooled, dmix)
                dqb[rows, _gcols(g)] = _dot_nt(dmix, pw_ref[g])

            zb = z_ref[0, rows, 0:IN_SHARD]
            zc = z_ref[0, rows, IN_SHARD:2 * IN_SHARD]
            zv = z_ref[0, rows, 2 * IN_SHARD:3 * IN_SHARD]
            dconv = dyconv * zb
            dcb[rows, :] = dconv
            d1 = dcb[r0 + 1:r0 + 1 + sub, :]
            d2 = dcb[r0 + 2:r0 + 2 + sub, :]
            dcv = cw[2:3] * dconv + cw[1:2] * d1 + cw[0:1] * d2
            cv = zc * zv
            dcw_ref[0:8, :] += _rows8(cv * d2)
            dcw_ref[8:16, :] += _rows8(cv * d1)
            dcw_ref[16:24, :] += _rows8(cv * dconv)
            dzs = [(dyconv * conv_ref[rows, :]).astype(BF16), (dcv * zv).astype(BF16), (dcv * zc).astype(BF16),
                   jnp.concatenate([_pool_bwd(dqb, g, r0, sub) for g in range(N_POOL_GROUPS)], axis=1).astype(BF16)]
            da = jnp.zeros((sub, D_MODEL), F32)
            for j in range(N_CHIPS):
                dz_ref[j, rows, :] = dzs[j]
                da = da + _dot_nt(dzs[j], win_v[j])
            xt = x_ref[0, rows, :]
            r1 = _rstd(xt)
            xh = xt * r1
            dg1_ref[...] += _rows8(da * xh)
            dx_ref[0, rows, :] = dh1v + _rms_bwd(da, xh, r1, g1v)

        @pl.when(tr == 0)
        def _():
            mcb[0:HALO, :] = jnp.zeros((HALO, D_CONV), F32)
            mqb[0:HALO, :] = jnp.zeros((HALO, D_POOL), F32)
            mcb[HALO:2 * HALO, :] = dcb[0:HALO, :]
            mqb[HALO:2 * HALO, :] = dqb[0:HALO, :]
            m1 = mcb[1:1 + HALO, :]
            m2 = mcb[2:2 + HALO, :]
            zc_m = zm_ref[:, IN_SHARD:2 * IN_SHARD]
            zv_m = zm_ref[:, 2 * IN_SHARD:3 * IN_SHARD]
            cv_m = zc_m * zv_m
            dcw_ref[0:8, :] += _rows8(cv_m * m2)
            dcw_ref[8:16, :] += _rows8(cv_m * m1)
            dcv_m = cw[1:2] * m1 + cw[0:1] * m2
            dzm_ref[:, IN_SHARD:2 * IN_SHARD] += dcv_m * zv_m
            dzm_ref[:, 2 * IN_SHARD:3 * IN_SHARD] += dcv_m * zc_m
            dzm_ref[:, 3 * IN_SHARD:4 * IN_SHARD] += jnp.concatenate(
                [_pool_bwd(mqb, g, 0, HALO) for g in range(N_POOL_GROUPS)], axis=1)

        @pl.when((s == n_seq - 1) & (i == n_t - 1))
        def _():
            xm = meta_ref[...]
            rm = _rstd(xm)
            xmh = xm * rm
            am_ref[...] = (xmh * g1v).astype(BF16)
            da_m = jnp.zeros((N_META, D_MODEL), F32)
            for j in range(N_CHIPS):
                dzj = dzm_ref[:, j * IN_SHARD:(j + 1) * IN_SHARD].astype(BF16)
                dzmb_ref[j] = dzj
                da_m = da_m + _dot_nt(dzj, win_v[j])
            dg1m_ref[...] = _rows8(da_m * xmh)
            dmeta_ref[...] = _rms_bwd(da_m, xmh, rm, g1v)

    row3 = lambda c: pl.BlockSpec((1, tm, c), lambda s, i: (s, n_t - 1 - i, 0))
    row2 = lambda c: pl.BlockSpec((tm, c), lambda s, i: (s * n_t + n_t - 1 - i, 0))
    n_rows = n_seq * seq
    outs = pl.pallas_call(
        body, name="mixer_bwd", grid=(n_seq, n_t),
        out_shape=[jax.ShapeDtypeStruct((n_seq, seq, D_MODEL), F32),
                   jax.ShapeDtypeStruct((N_CHIPS, n_rows, IN_SHARD), BF16), jax.ShapeDtypeStruct((n_rows, D_MODEL), BF16),
                   jax.ShapeDtypeStruct((8, D_MODEL), F32), jax.ShapeDtypeStruct((8, D_MODEL), F32),
                   jax.ShapeDtypeStruct((8, D_POOL), F32), jax.ShapeDtypeStruct((24, D_CONV), F32),
                   jax.ShapeDtypeStruct((N_POOL_GROUPS, POOL_GROUP, POOL_GROUP), F32),
                   jax.ShapeDtypeStruct((N_META, D_IN_PROJ), F32),
                   jax.ShapeDtypeStruct((N_META, D_MODEL), F32), jax.ShapeDtypeStruct((8, D_MODEL), F32),
                   jax.ShapeDtypeStruct((N_META, D_MODEL), BF16),
                   jax.ShapeDtypeStruct((N_CHIPS, N_META, IN_SHARD), BF16)],
        in_specs=[row3(D_MODEL), row3(D_MODEL), row3(D_Z), row2(D_CONV), row2(D_POOL), row3(D_MODEL),
                  _full((N_META, D_IN_PROJ)), _full((N_META, D_MODEL)), _full((1, D_MODEL)), _full((1, D_MODEL)),
                  _full((3, D_CONV)), _full((N_POOL_GROUPS, POOL_GROUP, POOL_GROUP)), _full((1, D_POOL)),
                  _full((8, 128))] + [ANY] * 4,
        out_specs=[row3(D_MODEL), pl.BlockSpec((N_CHIPS, tm, IN_SHARD), lambda s, i: (0, s * n_t + n_t - 1 - i, 0)),
                   row2(D_MODEL),
                   _full((8, D_MODEL)), _full((8, D_MODEL)), _full((8, D_POOL)), _full((24, D_CONV)),
                   _full((N_POOL_GROUPS, POOL_GROUP, POOL_GROUP)), _full((N_META, D_IN_PROJ)),
                   _full((N_META, D_MODEL)), _full((8, D_MODEL)), _full((N_META, D_MODEL)),
                   _full((N_CHIPS, N_META, IN_SHARD))],
        scratch_shapes=[pltpu.VMEM((N_CHIPS, D_MODEL, IN_SHARD), BF16), pltpu.VMEM((D_MODEL, D_MODEL), BF16),
                        pltpu.VMEM((tm + HALO, D_CONV), F32), pltpu.VMEM((tm + HALO, D_POOL), F32),
                        pltpu.VMEM((2 * HALO, D_CONV), F32), pltpu.VMEM((2 * HALO, D_POOL), F32),
                        pltpu.SemaphoreType.DMA((2 * N_CHIPS,))],
        compiler_params=_cparams(2),
    )(dh1, m3, z3, conv2, pooled2, x3, zmeta, meta_full, g1, g2, convw, poolw, pscale, after, *gathered, *shards)
    return outs


def _mixer_weight_grads(a, dz, ycat, dm, a_meta, dz_meta, ffn_sums, small):
    n_rows = a.shape[0]
    tk = min(TK_DW, n_rows)
    n_k = n_rows // tk
    n_sc, n_sm = len(ffn_sums), _AllReduceSmall.N_IN

    def body(a_ref, dz_ref, yc_ref, dm_ref, am_ref, dzm_ref, *rest):
        ins, outs, scratch = rest[:n_sc + n_sm], rest[n_sc + n_sm:2 * n_sc + n_sm + 5], rest[2 * n_sc + n_sm + 5:]
        dwin_ref, dwout_ref = outs[:2]
        scatter = _ScatterToChips(ins[:n_sc], outs[2:2 + n_sc], *scratch[:2])
        reduce_small = _AllReduceSmall(ins[n_sc:], outs[2 + n_sc:], scratch[2:])
        k = pl.program_id(0)

        @pl.when(k == 0)
        def _():
            scatter.start()
            reduce_small.pack_and_send()
            am_t = am_ref[...].T
            for j in range(N_CHIPS):
                dwin_ref[j] = _dot(am_t, dzm_ref[j])
            dwout_ref[...] = jnp.zeros_like(dwout_ref)

        for st in range(2):
            @pl.when(k == ((st + 1) * n_k) // 3)
            def _():
                reduce_small.combine(st)

        a_t = a_ref[...].T
        for j in range(N_CHIPS):
            dwin_ref[j] += _dot(a_t, dz_ref[j])
        dwout_ref[...] += _dot_tn(yc_ref[...], dm_ref[...])

        @pl.when(k == n_k - 1)
        def _():
            reduce_small.combine(2)
            scatter.finish()

    row = pl.BlockSpec((tk, D_MODEL), lambda k: (k, 0))
    outs = pl.pallas_call(
        body, name="mixer_weight_grads", grid=(n_k,),
        out_shape=[jax.ShapeDtypeStruct((N_CHIPS, D_MODEL, IN_SHARD), F32),
                   jax.ShapeDtypeStruct((D_MODEL, D_MODEL), F32)] + _ScatterToChips.out_shape(ffn_sums)
        + _AllReduceSmall.out_shape(),
        in_specs=[row, pl.BlockSpec((N_CHIPS, tk, IN_SHARD), lambda k: (0, k, 0)), row, row,
                  _full((N_META, D_MODEL)), _full((N_CHIPS, N_META, IN_SHARD))] + [ANY] * n_sc
        + [_full(s.shape) for s in small],
        out_specs=[_full((N_CHIPS, D_MODEL, IN_SHARD)), _full((D_MODEL, D_MODEL))] + [ANY] * n_sc
        + [_full(s) for s in _AllReduceSmall.SHAPES],
        scratch_shapes=_ScatterToChips.scratch(n_sc) + _AllReduceSmall.scratch(),
        compiler_params=_cparams(1),
    )(a, dz, ycat, dm, a_meta, dz_meta, *ffn_sums, *small)
    return ([outs[0], outs[1].reshape(N_CHIPS, OUT_SHARD, D_MODEL)], outs[2:2 + n_sc], outs[2 + n_sc:])


def kernel(x, meta_tokens, norm_mix_pre, w_in, conv_w, pool_w, pool_scale, w_out, norm_mix_post, norm_ffn_pre, w_gate, w_up, w_down, norm_ffn_post, loss_target, m_meta_tokens, m_norm_mix_pre, m_w_in, m_conv_w, m_pool_w, m_pool_scale, m_w_out, m_norm_mix_post, m_norm_ffn_pre, m_w_gate, m_w_up, m_w_down, m_norm_ffn_post, v_meta_tokens, v_norm_mix_pre, v_w_in, v_conv_w, v_pool_w, v_pool_scale, v_w_out, v_norm_mix_post, v_norm_ffn_pre, v_w_gate, v_w_up, v_w_down, v_norm_ffn_post):
    n_seq, seq, _ = x.shape
    n_rows = n_seq * seq
    chip = 2 * lax.axis_index("x") + lax.axis_index("y")
    meta_cols = D_MODEL // N_CHIPS
    conv_cols = D_CONV // N_CHIPS

    small = jnp.zeros((2 * HALO, meta_cols), F32)
    small = small.at[0:N_META, :].set(meta_tokens).at[N_META:N_META + 3, 0:conv_cols].set(conv_w[0])
    poolw_bf = pool_w[0].astype(BF16)
    pscale = pool_scale
    g1, g2, g3, g4 = norm_mix_pre, norm_mix_post, norm_ffn_pre, norm_ffn_post
    place = jnp.stack([chip, lax.axis_index("c")]).astype(jnp.int32)

    mix_shards = [w_in[0].astype(BF16), w_out[0].astype(BF16)]
    ffn_shards = [w_gate[0].T.astype(BF16), w_up[0].T.astype(BF16), w_down[0].astype(BF16)]
    ((z3, m3, h1, a_bf, conv2, pooled2, yc_bf, zmeta, meta_full, conv_full),
     (win_all, wout_all, _, *ffn_gathered)) = _mixer_fwd(x, g1, g2, poolw_bf, pscale, mix_shards + [small] + ffn_shards)
    dh1, f_bf, dd_bf, ds_bf, du_bf, gg_bf, lossp, dg3p, dg4p = _ffn_fwd_bwd(
        h1.reshape(n_rows, D_MODEL), loss_target.reshape(n_rows, D_MODEL), g3, g4, ffn_gathered, ffn_shards)
    as_shards = lambda g: g.reshape(N_CHIPS, FF_SHARD, D_MODEL)
    (dwg_t, dwu_t), _ = _ffn_weight_grads("ffn_weight_grads_gate_up", [ds_bf, du_bf], f_bf, [])
    dwg_t, dwu_t = as_shards(dwg_t), as_shards(dwu_t)
    (dwd,), (dwg_recv, dwu_recv) = _ffn_weight_grads("ffn_weight_grads_down", [gg_bf], dd_bf, [dwg_t, dwu_t])
    dwd = as_shards(dwd)
    behind_bwd = _SplitComm("grad_comm_behind_mixer_bwd", [dwd],
                            _add_pairs_multi([dwg_t, dwu_t], [dwg_recv, dwu_recv], place))
    (grad_x, dz_bf, dm_bf, dg1p, dg2p, dscp, dcwp, dpw, _, dmeta, dg1m, a_meta, dz_meta) = _mixer_bwd(
        dh1.reshape(n_seq, seq, D_MODEL), m3, z3, conv2, pooled2, x, zmeta, meta_full, g1, g2, conv_full, poolw_bf,
        pscale, [win_all, wout_all], mix_shards, behind_bwd.start())
    (dwd_recv,), (dwg_rbuf, dwu_rbuf) = behind_bwd.wait(dg2p)
    mix_grads, (dwd_rbuf,), (a_red, b_red, c_red) = _mixer_weight_grads(
        a_bf, dz_bf, yc_bf, dm_bf, a_meta, dz_meta, [_add_pairs(dwd, dwd_recv, place)],
        [dg1p, dg1m, dg2p, dg3p, dg4p, lossp, dmeta, dscp, dcwp, dpw.reshape(SMALL_C_ROWS, POOL_GROUP)])

    mix_recvs = _exchange_halves(mix_grads)
    behind_tail = _SplitComm("grad_comm_behind_ffn_tail", [], _add_pairs_multi(mix_grads, mix_recvs, place))
    ffn_red = _add_chips([dwg_t, dwu_t, dwd], [dwg_recv, dwu_recv, dwd_recv], [dwg_rbuf, dwu_rbuf, dwd_rbuf],
                         place, after=behind_tail.start(), name="grad_add_chips_ffn")
    as_full = lambda r: r.reshape(2 * r.shape[1], r.shape[2])
    g_wg_t, g_wu_t, g_wd = [as_full(r) for r in _gather_halves(list(ffn_red), "grad_gather_halves_ffn")]
    ffn_out = _adamw_big([(w_gate[0].T, g_wg_t, m_w_gate[0].T, v_w_gate[0].T),
                          (w_up[0].T, g_wu_t, m_w_up[0].T, v_w_up[0].T), (w_down[0], g_wd, m_w_down[0], v_w_down[0])])
    _, mix_rbufs = behind_tail.wait(ffn_out[2][0])
    mix_red = _add_chips(mix_grads, mix_recvs, mix_rbufs, place)
    g_win, g_wout = [as_full(r) for r in _gather_halves(list(mix_red), "grad_gather_halves_mixer")]

    loss = a_red[4, 0]
    g_g1, g_g2, g_g3, g_g4 = a_red[0:1], a_red[1:2], a_red[2:3], a_red[3:4]
    g_meta = lax.dynamic_slice(a_red, (8, chip * meta_cols), (N_META, meta_cols))
    g_pscale = b_red[0:1]
    g_conv = lax.dynamic_slice(b_red, (1, chip * conv_cols), (3, conv_cols))
    g_poolw = c_red

    big_out = (_adamw_big([(w_in[0], g_win, m_w_in[0], v_w_in[0])])
               + _adamw_big([(w_out[0], g_wout, m_w_out[0], v_w_out[0])]) + ffn_out)
    big_out[2] = [o.T for o in big_out[2]]
    big_out[3] = [o.T for o in big_out[3]]
    g_wg, g_wu = g_wg_t.T, g_wu_t.T
    small_groups = [
        (meta_tokens, g_meta, m_meta_tokens, v_meta_tokens),
        (g1, g_g1, m_norm_mix_pre, v_norm_mix_pre),
        (conv_w[0], g_conv, m_conv_w[0], v_conv_w[0]),
        (pool_w.reshape(SMALL_C_ROWS, POOL_GROUP), g_poolw, m_pool_w.reshape(SMALL_C_ROWS, POOL_GROUP),
         v_pool_w.reshape(SMALL_C_ROWS, POOL_GROUP)),
        (pool_scale, g_pscale, m_pool_scale, v_pool_scale),
        (g2, g_g2, m_norm_mix_post, v_norm_mix_post),
        (g3, g_g3, m_norm_ffn_pre, v_norm_ffn_pre),
        (g4, g_g4, m_norm_ffn_post, v_norm_ffn_post),
    ]
    small_out = _adamw_small(small_groups)

    grads_out = [g_meta, g_g1, g_win[None], g_conv[None], g_poolw.reshape(pool_w.shape), g_pscale, g_wout[None],
                 g_g2, g_g3, g_wg[None], g_wu[None], g_wd[None], g_g4]
    s_meta, s_g1, s_conv, s_poolw, s_pscale, s_g2, s_g3, s_g4 = small_out
    b_win, b_wout, b_wg, b_wu, b_wd = big_out

    def leaf(k):
        return [s_meta[k], s_g1[k], b_win[k][None], s_conv[k][None], s_poolw[k].reshape(pool_w.shape), s_pscale[k],
                b_wout[k][None], s_g2[k], s_g3[k], b_wg[k][None], b_wu[k][None], b_wd[k][None], s_g4[k]]

    return (loss, grad_x, *grads_out, *leaf(0), *leaf(1), *leaf(2))
```

```python
import functools

import jax
import jax.numpy as jnp
from jax import lax
from jax.experimental import pallas as pl
from jax.experimental.pallas import tpu as pltpu

F32 = jnp.float32
BF16 = jnp.bfloat16
MESH = pl.DeviceIdType.MESH

D_MODEL = 1024
D_CONV = 512
D_POOL = 512
POOL_GROUP = 128
N_POOL_GROUPS = 4
D_IN_PROJ = 2048
D_FF = 2816
N_CHIPS = 4
FF_SHARD = D_FF // N_CHIPS
IN_SHARD = D_IN_PROJ // N_CHIPS
OUT_SHARD = D_MODEL // N_CHIPS
D_Z = 3 * IN_SHARD
N_META = 16
HALO = 16
RMS_EPS = 1e-6

ADAM_LR = 0.001
ADAM_B1 = 0.9
ADAM_B2 = 0.999
ADAM_EPS = 1e-08
ADAM_WD = 0.01
ADAM_STEP = 10

TM_MIX_FWD = 512
TM_MIX_BWD = 512
SUB_MIX_BWD = 512
TM_FFN = 256
TK_DW = 1024
FF_CHUNK = 1024
VMEM_LIMIT = 56 * 1024 * 1024


def _cparams(n_grid):
    return pltpu.CompilerParams(dimension_semantics=("arbitrary",) * n_grid, vmem_limit_bytes=VMEM_LIMIT)


def _dot(a, b):
    return jnp.dot(a, b, preferred_element_type=F32)


def _dot_nt(a, b):
    return lax.dot_general(a, b, (((1,), (1,)), ((), ())), preferred_element_type=F32)


def _dot_tn(a, b):
    return lax.dot_general(a, b, (((0,), (0,)), ((), ())), preferred_element_type=F32)


def _rows8(v):
    r, c = v.shape
    return v.reshape(r // 8, 8, c).sum(axis=0)


def _rstd(v):
    return lax.rsqrt(jnp.mean(v * v, axis=-1, keepdims=True) + RMS_EPS)


def _rms_bwd(dy, xhat, rstd, gain):
    dyg = dy * gain
    return rstd * (dyg - xhat * jnp.mean(dyg * xhat, axis=-1, keepdims=True))


def _sigmoid(v):
    return 1.0 / (1.0 + jnp.exp(-v))


def _gcols(g):
    return slice(g * POOL_GROUP, (g + 1) * POOL_GROUP)


def _window_sum(e, g, ahead):
    n = e.shape[0]
    w = e
    for level in range(g + 1):
        shift = 1 << level
        w = w + pltpu.roll(w, (n - shift) if ahead else shift, 0)
    return w


def _pool_fwd(pb, g, n):
    e = pb[0:HALO + n, _gcols(g)]
    return _window_sum(e, g, False)[HALO:, :] * (1.0 / (2 << g)) - e[HALO:, :]


def _pool_bwd(qb, g, r0, n):
    e = qb[r0:r0 + n + HALO, _gcols(g)]
    return _window_sum(e, g, True)[0:n, :] * (1.0 / (2 << g)) - e[0:n, :]


def _full(shape):
    nd = len(shape)
    return pl.BlockSpec(shape, lambda *_: (0,) * nd)


ANY = pl.BlockSpec(memory_space=pl.ANY)


def _mesh_pos():
    x, y, c = lax.axis_index("x"), lax.axis_index("y"), lax.axis_index("c")
    chips = [(1 - x, y), (x, 1 - y), (1 - x, 1 - y)]
    return x, y, c, chips


def _half(ref, h):
    hr = ref.shape[0] // 2
    return ref.at[pl.ds(h * hr, hr), :]


class _AllGather:
    PER_ARRAY = 9

    def __init__(self, ins, outs, send_sems, recv_sems):
        self.ins, self.outs, self.send_sems, self.recv_sems = ins, outs, send_sems, recv_sems
        self.n = len(ins)

    @classmethod
    def scratch(cls, n):
        return [pltpu.SemaphoreType.DMA((cls.PER_ARRAY * n,)), pltpu.SemaphoreType.DMA((cls.PER_ARRAY * n,))]

    @staticmethod
    def out_shape(shards):
        return [jax.ShapeDtypeStruct((N_CHIPS,) + s.shape, s.dtype) for s in shards]

    def _copy(self, a, k, src, dst, to):
        i = self.PER_ARRAY * a + k
        return pltpu.make_async_remote_copy(src_ref=src, dst_ref=dst, send_sem=self.send_sems.at[i],
                                            recv_sem=self.recv_sems.at[i], device_id=to, device_id_type=MESH)

    def _piece(self, a, chip, piece, h=None):
        h = lax.axis_index("c") if h is None else h
        rows = self.ins[a].shape[0] // 4
        return self.outs[a].at[chip].at[pl.ds((2 * h + piece) * rows, rows), :]

    def _own(self, a, k):
        x, y, c, chips = _mesh_pos()
        piece = (1, 0, 0, 1)[k]
        rows = self.ins[a].shape[0] // 4
        src = self.ins[a].at[pl.ds((2 * c + piece) * rows, rows), :]
        return self._copy(a, k, src, self._piece(a, 2 * x + y, piece), (*chips[k // 2], c))

    def _relay(self, a, k):
        x, y, c, chips = _mesh_pos()
        source, to, piece = (chips[1], chips[0], 0) if k == 4 else (chips[0], chips[1], 1)
        rows = self._piece(a, 2 * source[0] + source[1], piece)
        return self._copy(a, k, rows, rows, (*to, c))

    def _sibling(self, a, k, h):
        x, y, c, chips = _mesh_pos()
        chip = chips[k - 6]
        slot = _half(self.outs[a].at[2 * chip[0] + chip[1]], h)
        return self._copy(a, k, slot, slot, (x, y, 1 - c))

    def start(self, arrays=None):
        for a in (range(self.n) if arrays is None else arrays):
            for k in range(4):
                self._own(a, k).start()

    def relay(self, a):
        self._own(a, 2).wait_recv()
        self._relay(a, 4).start()
        self._own(a, 0).wait_recv()
        self._relay(a, 5).start()

    def forward(self, a):
        c = lax.axis_index("c")
        self._own(a, 1).wait_recv()
        self._sibling(a, 6, c).start()
        self._own(a, 3).wait_recv()
        self._sibling(a, 7, c).start()
        self._relay(a, 4).wait_recv()
        self._relay(a, 5).wait_recv()
        self._sibling(a, 8, c).start()

    def finish(self, arrays=None):
        c = lax.axis_index("c")
        arrays = range(self.n) if arrays is None else arrays
        for a in arrays:
            for k in range(6, 9):
                self._sibling(a, k, 1 - c).wait_recv()
        for a in arrays:
            for k in range(4):
                self._own(a, k).wait_send()
            for k in range(4, 6):
                self._relay(a, k).wait_send()
            for k in range(6, 9):
                self._sibling(a, k, c).wait_send()


class _ExchangeHalves:
    def __init__(self, ins, recvs, send_sems, recv_sems):
        self.ins, self.recvs, self.send_sems, self.recv_sems = ins, recvs, send_sems, recv_sems

    @staticmethod
    def scratch(n):
        return [pltpu.SemaphoreType.DMA((n,)), pltpu.SemaphoreType.DMA((n,))]

    @staticmethod
    def out_shape(grads):
        return [jax.ShapeDtypeStruct((g.shape[0], g.shape[1] // 2, g.shape[2]), g.dtype) for g in grads]

    def _copies(self):
        x, y, c, _ = _mesh_pos()
        out = []
        for a, (src, dst) in enumerate(zip(self.ins, self.recvs)):
            hr = src.shape[1] // 2
            out.append(pltpu.make_async_remote_copy(
                src_ref=src.at[:, pl.ds((1 - c) * hr, hr), :], dst_ref=dst, send_sem=self.send_sems.at[a],
                recv_sem=self.recv_sems.at[a], device_id=(x, y, 1 - c), device_id_type=MESH))
        return out

    def start(self):
        for cp in self._copies():
            cp.start()

    def finish(self):
        for cp in self._copies():
            cp.wait()


def _exchange_halves(grads):
    n = len(grads)

    def body(*refs):
        ex = _ExchangeHalves(refs[:n], refs[n:2 * n], *refs[2 * n:])
        ex.start()
        ex.finish()

    return pl.pallas_call(
        body, name="grad_exchange_halves", out_shape=_ExchangeHalves.out_shape(grads),
        in_specs=[ANY] * n, out_specs=[ANY] * n, scratch_shapes=_ExchangeHalves.scratch(n),
    )(*grads)


class _ScatterToChips:
    def __init__(self, ins, rbufs, send_sems, recv_sems):
        self.ins, self.rbufs, self.send_sems, self.recv_sems = ins, rbufs, send_sems, recv_sems

    @staticmethod
    def scratch(n):
        return [pltpu.SemaphoreType.DMA((3 * n,)), pltpu.SemaphoreType.DMA((3 * n,))]

    @staticmethod
    def out_shape(sums):
        return [jax.ShapeDtypeStruct((3,) + s.shape[1:], BF16) for s in sums]

    def _copies(self):
        x, y, c, chips = _mesh_pos()
        out = []
        for a, (src, dst) in enumerate(zip(self.ins, self.rbufs)):
            for k, chip in enumerate(chips):
                out.append(pltpu.make_async_remote_copy(
                    src_ref=src.at[2 * chip[0] + chip[1]], dst_ref=dst.at[k], send_sem=self.send_sems.at[3 * a + k],
                    recv_sem=self.recv_sems.at[3 * a + k], device_id=(*chip, c), device_id_type=MESH))
        return out

    def start(self):
        for cp in self._copies():
            cp.start()

    def finish(self):
        for cp in self._copies():
            cp.wait()


HBM = pl.BlockSpec(memory_space=pltpu.HBM)
SEM = pl.BlockSpec(memory_space=pltpu.SEMAPHORE)


class _SplitComm:
    def __init__(self, name, exchanged, scattered):
        self.name, self.n_ex, self.n_sc = name, len(exchanged), len(scattered)
        self.n_copies = self.n_ex + 3 * self.n_sc
        zones = ([lax.empty((g.shape[0], g.shape[1] // 2, g.shape[2]), g.dtype) for g in exchanged]
                 + [lax.empty((3,) + s.shape[1:], s.dtype) for s in scattered])
        self.buffers = [pltpu.with_memory_space_constraint(v, pltpu.HBM)
                        for v in list(exchanged) + list(scattered) + zones]

    def _copies(self, bufs, send_sems, recv_sems):
        x, y, c, chips = _mesh_pos()
        n_src = self.n_ex + self.n_sc
        out = []
        for a in range(self.n_ex):
            hr = bufs[a].shape[1] // 2
            out.append(pltpu.make_async_remote_copy(
                src_ref=bufs[a].at[:, pl.ds((1 - c) * hr, hr), :], dst_ref=bufs[n_src + a], send_sem=send_sems[a],
                recv_sem=recv_sems[a], device_id=(x, y, 1 - c), device_id_type=MESH))
        for a in range(self.n_sc):
            for k, chip in enumerate(chips):
                i = self.n_ex + 3 * a + k
                out.append(pltpu.make_async_remote_copy(
                    src_ref=bufs[self.n_ex + a].at[2 * chip[0] + chip[1]], dst_ref=bufs[n_src + self.n_ex + a].at[k],
                    send_sem=send_sems[i], recv_sem=recv_sems[i], device_id=(*chip, c), device_id_type=MESH))
        return out

    def start(self):
        n_buf, n_cp = len(self.buffers), self.n_copies

        def body(*refs):
            bufs = refs[:n_buf]
            send_sems, recv_sems = refs[n_buf:n_buf + n_cp], refs[n_buf + n_cp:n_buf + 2 * n_cp]
            for cp in self._copies(bufs, send_sems, recv_sems):
                cp.start()
            refs[-1][...] = jnp.zeros_like(refs[-1])

        outs = pl.pallas_call(
            body, name=self.name + "_start",
            out_shape=[pltpu.SemaphoreType.DMA(())] * (2 * n_cp) + [pltpu.HBM(b.shape, b.dtype) for b in self.buffers]
            + [jax.ShapeDtypeStruct((8, 128), F32)],
            in_specs=[HBM] * n_buf, out_specs=[SEM] * (2 * n_cp) + [HBM] * n_buf + [pl.BlockSpec(memory_space=pltpu.VMEM)],
            input_output_aliases={i: 2 * n_cp + i for i in range(n_buf)},
            compiler_params=pltpu.CompilerParams(has_side_effects=pltpu.SideEffectType.DATAFLOW_SIDE_EFFECTING),
        )(*self.buffers)
        self.sems, self.buffers = outs[:2 * n_cp], outs[2 * n_cp:2 * n_cp + n_buf]
        return outs[-1]

    def wait(self, after):
        n_buf, n_cp = len(self.buffers), self.n_copies

        def body(*refs):
            bufs = refs[:n_buf]
            send_sems, recv_sems = refs[n_buf:n_buf + n_cp], refs[n_buf + n_cp:n_buf + 2 * n_cp]
            for cp in self._copies(bufs, send_sems, recv_sems):
                cp.wait_send()
                cp.wait_recv()

        outs = pl.pallas_call(
            body, name=self.name + "_wait", out_shape=[pltpu.HBM(b.shape, b.dtype) for b in self.buffers],
            in_specs=[HBM] * n_buf + [SEM] * (2 * n_cp) + [ANY], out_specs=[HBM] * n_buf,
            input_output_aliases={i: i for i in range(n_buf)},
            compiler_params=pltpu.CompilerParams(has_side_effects=pltpu.SideEffectType.DATAFLOW_SIDE_EFFECTING),
        )(*self.buffers, *self.sems, after)
        self.exchanged = outs[:self.n_ex]
        zones = outs[self.n_ex + self.n_sc:]
        return zones[:self.n_ex], zones[self.n_ex:]


def _gather_halves(halves, name):
    n = len(halves)

    def body(*refs):
        ins, outs = refs[:n], refs[n:2 * n]
        send_sems, recv_sems = refs[2 * n:]
        x, y, c, _ = _mesh_pos()
        sib = (x, y, 1 - c)
        remote = [pltpu.make_async_remote_copy(src_ref=ins[a].at[c], dst_ref=outs[a].at[c],
                                               send_sem=send_sems.at[a], recv_sem=recv_sems.at[a],
                                               device_id=sib, device_id_type=MESH) for a in range(n)]
        for cp in remote:
            cp.start()
        for a in range(n):
            pltpu.make_async_remote_copy(src_ref=ins[a].at[1 - c], dst_ref=outs[a].at[1 - c], send_sem=send_sems.at[a],
                                         recv_sem=recv_sems.at[a], device_id=sib, device_id_type=MESH).wait_recv()
        for cp in remote:
            cp.wait_send()

    return pl.pallas_call(
        body, name=name,
        out_shape=[jax.ShapeDtypeStruct(h.shape, F32) for h in halves],
        in_specs=[ANY] * n, out_specs=[ANY] * n, input_output_aliases={a: a for a in range(n)},
        scratch_shapes=[pltpu.SemaphoreType.DMA((n,)), pltpu.SemaphoreType.DMA((n,))],
    )(*halves)


SMALL_A_ROWS = 24
SMALL_B_ROWS = 8
SMALL_C_ROWS = N_POOL_GROUPS * POOL_GROUP


class _AllReduceSmall:
    N_IN = 10
    SHAPES = [(SMALL_A_ROWS, D_MODEL), (SMALL_B_ROWS, D_CONV), (SMALL_C_ROWS, POOL_GROUP)]

    def __init__(self, ins, outs, scratch):
        self.ins, self.outs = ins, outs
        self.bufs, self.rcvs, self.send_sems, self.recv_sems = scratch[:3], scratch[3:6], scratch[6], scratch[7]

    @classmethod
    def scratch(cls):
        return ([pltpu.VMEM((3,) + s, F32) for s in cls.SHAPES] + [pltpu.VMEM((3,) + s, F32) for s in cls.SHAPES]
                + [pltpu.SemaphoreType.DMA((9,)), pltpu.SemaphoreType.DMA((9,))])

    @classmethod
    def out_shape(cls):
        return [jax.ShapeDtypeStruct(s, F32) for s in cls.SHAPES]

    def _copies(self, st):
        x, y, c, _ = _mesh_pos()
        peer = [(x, y, 1 - c), (1 - x, y, c), (x, 1 - y, c)][st]
        return [pltpu.make_async_remote_copy(
            src_ref=buf.at[st], dst_ref=rcv.at[st], send_sem=self.send_sems.at[3 * st + i],
            recv_sem=self.recv_sems.at[3 * st + i], device_id=peer, device_id_type=MESH)
            for i, (buf, rcv) in enumerate(zip(self.bufs, self.rcvs))]

    def pack_and_send(self):
        dg1_ref, dg1m_ref, dg2_ref, dg3_ref, dg4_ref, loss_ref, dmeta_ref, dsc_ref, dcw_ref, dpw_ref = self.ins
        a_buf, b_buf, c_buf = self.bufs

        def rowsum(v):
            return jnp.sum(v, axis=0, keepdims=True)

        a_buf[0, 0:1, :] = rowsum(dg1_ref[...] + dg1m_ref[...])
        a_buf[0, 1:2, :] = rowsum(dg2_ref[...])
        a_buf[0, 2:3, :] = rowsum(dg3_ref[...])
        a_buf[0, 3:4, :] = rowsum(dg4_ref[...])
        loss = jnp.sum(rowsum(loss_ref[...]), axis=1, keepdims=True) * (0.5 / D_MODEL)
        a_buf[0, 4:5, :] = jnp.broadcast_to(loss, (1, D_MODEL))
        a_buf[0, 5:8, :] = jnp.zeros((3, D_MODEL), F32)
        a_buf[0, 8:24, :] = dmeta_ref[...]
        b_buf[0, 0:1, :] = rowsum(dsc_ref[...])
        for k in range(3):
            b_buf[0, 1 + k:2 + k, :] = rowsum(dcw_ref[8 * k:8 * k + 8, :])
        b_buf[0, 4:8, :] = jnp.zeros((4, D_CONV), F32)
        c_buf[0] = dpw_ref[...]
        for cp in self._copies(0):
            cp.start()

    def combine(self, st):
        for cp in self._copies(st):
            cp.wait()
        if st < 2:
            for buf, rcv in zip(self.bufs, self.rcvs):
                buf[st + 1] = buf[st] + rcv[st]
            for cp in self._copies(st + 1):
                cp.start()
        else:
            for out, buf, rcv in zip(self.outs, self.bufs, self.rcvs):
                out[...] = buf[st] + rcv[st]


def _row_block(rows):
    for cand in (512, 448, 384, 352, 320, 256, 128, 64, 32, 16):
        if rows % cand == 0:
            return cand
    return rows


def _add_pairs_multi(grads, recvs, place):
    n = len(grads)
    n_sh = grads[0].shape[0]
    halves = [g.shape[1] // 2 for g in grads]
    n_steps = halves[0] // _row_block(halves[0])
    blocks = [(hr // n_steps, g.shape[2]) for hr, g in zip(halves, grads)]

    def body(place_ref, *refs):
        for a_ref, b_ref, o_ref in zip(refs[:n], refs[n:2 * n], refs[2 * n:]):
            o_ref[...] = (a_ref[0] + b_ref[...]).astype(BF16)

    return pl.pallas_call(
        body, name="grad_add_pairs",
        grid_spec=pltpu.PrefetchScalarGridSpec(
            num_scalar_prefetch=1, grid=(n_sh, n_steps),
            in_specs=[pl.BlockSpec((1, 1, br, cols), lambda j, i, p: (j, p[1], i, 0)) for br, cols in blocks]
            + [pl.BlockSpec((1, br, cols), lambda j, i, p: (j, i, 0)) for br, cols in blocks],
            out_specs=[pl.BlockSpec((1, br, cols), lambda j, i, p: (j, i, 0)) for br, cols in blocks]),
        out_shape=[jax.ShapeDtypeStruct((n_sh, hr, g.shape[2]), BF16) for hr, g in zip(halves, grads)],
        compiler_params=_cparams(2),
    )(place, *[g.reshape(n_sh, 2, hr, g.shape[2]) for hr, g in zip(halves, grads)], *recvs)


def _add_pairs(grad, recv, place):
    return _add_pairs_multi([grad], [recv], place)[0]


def _add_chips(grads, recvs, rbufs, place, after=None, name="grad_add_chips"):
    n = len(grads)
    n_sh = grads[0].shape[0]
    halves = [g.shape[1] // 2 for g in grads]
    n_steps = halves[0] // _row_block(halves[0])
    blocks = [(hr // n_steps, g.shape[2]) for hr, g in zip(halves, grads)]
    extra = [] if after is None else [after]

    def body(place_ref, *refs):
        for a_ref, b_ref, r_ref, o_ref in zip(refs[:n], refs[n:2 * n], refs[2 * n:3 * n], refs[3 * n + len(extra):]):
            own = a_ref[0, 0] + b_ref[0]
            o_ref[0] = ((own + r_ref[0].astype(F32)) + r_ref[1].astype(F32)) + r_ref[2].astype(F32)

    return pl.pallas_call(
        body, name=name,
        grid_spec=pltpu.PrefetchScalarGridSpec(
            num_scalar_prefetch=1, grid=(n_steps,),
            in_specs=[pl.BlockSpec((1, 1, br, cols), lambda i, p: (p[0], p[1], i, 0)) for br, cols in blocks]
            + [pl.BlockSpec((1, br, cols), lambda i, p: (p[0], i, 0)) for br, cols in blocks]
            + [pl.BlockSpec((3, br, cols), lambda i, p: (0, i, 0)) for br, cols in blocks]
            + [pl.BlockSpec((8, 128), lambda i, p: (0, 0))] * len(extra),
            out_specs=[pl.BlockSpec((1, br, cols), lambda i, p: (p[1], i, 0)) for br, cols in blocks]),
        out_shape=[jax.ShapeDtypeStruct((2, hr, g.shape[2]), F32) for hr, g in zip(halves, grads)],
        compiler_params=_cparams(1),
    )(place, *[g.reshape(n_sh, 2, hr, g.shape[2]) for hr, g in zip(halves, grads)], *recvs, *rbufs, *extra)


def _adamw_math(w, g, m, v):
    m2 = ADAM_B1 * m + (1.0 - ADAM_B1) * g
    v2 = ADAM_B2 * v + (1.0 - ADAM_B2) * (g * g)
    m_hat = m2 / (1.0 - ADAM_B1 ** ADAM_STEP)
    v_hat = v2 / (1.0 - ADAM_B2 ** ADAM_STEP)
    delta = -ADAM_LR * (m_hat / (jnp.sqrt(v_hat) + ADAM_EPS) + ADAM_WD * w)
    return delta, m2, v2


def _adamw_big(groups):
    n = len(groups)
    rows, cols = groups[0][0].shape
    br = _row_block(rows)
    if n > 1 and br % 16 == 0:
        br //= 2

    def body(*refs):
        for i in range(n):
            w_ref, g_ref, m_ref, v_ref = refs[4 * i:4 * i + 4]
            d_ref, m2_ref, v2_ref = refs[4 * n + 3 * i:4 * n + 3 * i + 3]
            d, m2, v2 = _adamw_math(w_ref[...], g_ref[...], m_ref[...], v_ref[...])
            d_ref[...] = d
            m2_ref[...] = m2
            v2_ref[...] = v2

    spec = pl.BlockSpec((br, cols), lambda i: (i, 0))
    outs = pl.pallas_call(
        body, name="adamw_big", grid=(rows // br,),
        out_shape=[jax.ShapeDtypeStruct((rows, cols), F32)] * (3 * n),
        in_specs=[spec] * (4 * n), out_specs=[spec] * (3 * n), compiler_params=_cparams(1),
    )(*[a for grp in groups for a in grp])
    return [list(outs[3 * i:3 * i + 3]) for i in range(n)]


def _adamw_small(groups):
    n = len(groups)

    def body(*refs):
        ins, outs = refs[:4 * n], refs[4 * n:]
        for i in range(n):
            w, g, m, v = (r[...] for r in ins[4 * i:4 * i + 4])
            d, m2, v2 = _adamw_math(w, g, m, v)
            outs[3 * i][...] = d
            outs[3 * i + 1][...] = m2
            outs[3 * i + 2][...] = v2

    vm = pl.BlockSpec(memory_space=pltpu.VMEM)
    flat = [a for grp in groups for a in grp]
    out_shape = [jax.ShapeDtypeStruct(grp[0].shape, F32) for grp in groups for _ in range(3)]
    outs = pl.pallas_call(body, name="adamw_small", out_shape=out_shape,
                          in_specs=[vm] * (4 * n), out_specs=[vm] * (3 * n))(*flat)
    return [tuple(outs[3 * i:3 * i + 3]) for i in range(n)]


def _load_gathered(gathered, shards, dst_slots, sems):
    n = len(gathered)
    me = 2 * lax.axis_index("x") + lax.axis_index("y")

    def copies(j, own):
        return [pltpu.make_async_copy(shards[a] if own else gathered[a].at[j], dst_slots[a](j), sems.at[n * j + a])
                for a in range(n)]

    for wait in (False, True):
        for j in range(N_CHIPS):
            for own in (False, True):
                @pl.when((me == j) == own)
                def _():
                    for cp in copies(j, own):
                        cp.wait() if wait else cp.start()


N_MIX_SHARDS = 3


def _mixer_fwd(x3, g1, g2, poolw, pscale, shards):
    n_seq, seq, _ = x3.shape
    tm = min(TM_MIX_FWD, seq)
    n_t = seq // tm
    n_steps = n_seq * n_t
    n_ag = len(shards)
    n_ffn = n_ag - N_MIX_SHARDS
    small_rows = shards[2].shape[0]
    conv_cols = D_CONV // N_CHIPS

    def body(x_ref, g1_ref, g2_ref, pw_ref, ps_ref, *rest):
        ag = _AllGather(rest[:n_ag], rest[n_ag + 10:2 * n_ag + 10], *rest[-2:])
        (z_ref, m_ref, h1_ref, a_ref, conv_ref, pooled_ref, yc_ref, zm_ref, meta_ref,
         cw_ref) = rest[n_ag:n_ag + 10]
        win_v, wout_v, small_v, cvb, pb, load_sems = rest[2 * n_ag + 10:-2]
        s, t = pl.program_id(0), pl.program_id(1)
        step = s * n_t + t

        @pl.when(step == 0)
        def _():
            ag.start(range(N_MIX_SHARDS))
            for a in range(N_MIX_SHARDS):
                ag.relay(a)
            for a in range(N_MIX_SHARDS):
                ag.forward(a)
            ag.finish(range(N_MIX_SHARDS))
            ag.start(range(N_MIX_SHARDS, n_ag))
            _load_gathered(ag.outs[:N_MIX_SHARDS], ag.ins[:N_MIX_SHARDS],
                           [lambda j: win_v.at[j], lambda j: wout_v.at[pl.ds(j * OUT_SHARD, OUT_SHARD), :],
                            lambda j: small_v.at[j]], load_sems)

            meta = jnp.concatenate([small_v[j, 0:N_META, :] for j in range(N_CHIPS)], axis=1)
            meta_ref[...] = meta
            cw_ref[...] = jnp.concatenate([small_v[j, N_META:N_META + 3, 0:conv_cols] for j in range(N_CHIPS)], axis=1)
            a_meta = (meta * _rstd(meta) * g1_ref[...]).astype(BF16)
            for j in range(N_CHIPS):
                zm_ref[:, j * IN_SHARD:(j + 1) * IN_SHARD] = _dot(a_meta, win_v[j])

        for i in range(n_ffn):
            @pl.when(step == ((i + 1) * n_steps) // (2 * n_ffn + 2))
            def _():
                ag.relay(N_MIX_SHARDS + i)

        for i in range(n_ffn):
            @pl.when(step == min(n_steps // 2 + ((i + 1) * n_steps) // (2 * n_ffn + 2), n_steps - 1))
            def _():
                ag.forward(N_MIX_SHARDS + i)

        @pl.when(t == 0)
        def _():
            cvb[0:HALO, :] = zm_ref[:, IN_SHARD:2 * IN_SHARD] * zm_ref[:, 2 * IN_SHARD:3 * IN_SHARD]
            pb[0:HALO, :] = zm_ref[:, 3 * IN_SHARD:4 * IN_SHARD]

        @pl.when(t > 0)
        def _():
            cvb[0:HALO, :] = cvb[tm:tm + HALO, :]
            pb[0:HALO, :] = pb[tm:tm + HALO, :]

        xt = x_ref[0]
        a = (xt * _rstd(xt) * g1_ref[...]).astype(BF16)
        a_ref[...] = a
        zb = _dot(a, win_v[0])
        zc = _dot(a, win_v[1])
        zv = _dot(a, win_v[2])
        zp = _dot(a, win_v[3])
        z_ref[0, :, 0:IN_SHARD] = zb
        z_ref[0, :, IN_SHARD:2 * IN_SHARD] = zc
        z_ref[0, :, 2 * IN_SHARD:3 * IN_SHARD] = zv
        cv = zc * zv
        cvb[HALO:HALO + tm, :] = cv
        pb[HALO:HALO + tm, :] = zp
        cw = cw_ref[...]
        conv = cw[0:1] * cvb[HALO - 2:HALO - 2 + tm, :] + cw[1:2] * cvb[HALO - 1:HALO - 1 + tm, :] + cw[2:3] * cv
        conv_ref[...] = conv
        parts = [(zb * conv).astype(BF16)]
        for g in range(N_POOL_GROUPS):
            pooled = _pool_fwd(pb, g, tm).astype(BF16)
            pooled_ref[:, _gcols(g)] = pooled
            parts.append((_dot(pooled, pw_ref[g]) * ps_ref[:, _gcols(g)]).astype(BF16))
        ycat = jnp.concatenate(parts, axis=1)
        yc_ref[...] = ycat
        m = _dot(ycat, wout_v[...])
        m_ref[0] = m
        h1_ref[0] = xt + m * _rstd(m) * g2_ref[...]

        @pl.when(step == n_steps - 1)
        def _():
            ag.finish(range(N_MIX_SHARDS, n_ag))

    n_rows = n_seq * seq
    row = lambda c: pl.BlockSpec((1, tm, c), lambda s, t: (s, t, 0))
    row2 = lambda c: pl.BlockSpec((tm, c), lambda s, t: (s * n_t + t, 0))
    outs = pl.pallas_call(
        body, name="mixer_fwd", grid=(n_seq, n_t),
        out_shape=[jax.ShapeDtypeStruct((n_seq, seq, D_Z), F32), jax.ShapeDtypeStruct((n_seq, seq, D_MODEL), F32),
                   jax.ShapeDtypeStruct((n_seq, seq, D_MODEL), F32), jax.ShapeDtypeStruct((n_rows, D_MODEL), BF16),
                   jax.ShapeDtypeStruct((n_rows, D_CONV), F32), jax.ShapeDtypeStruct((n_rows, D_POOL), BF16),
                   jax.ShapeDtypeStruct((n_rows, D_MODEL), BF16), jax.ShapeDtypeStruct((N_META, D_IN_PROJ), F32),
                   jax.ShapeDtypeStruct((N_META, D_MODEL), F32), jax.ShapeDtypeStruct((3, D_CONV), F32)]
        + _AllGather.out_shape(shards),
        in_specs=[row(D_MODEL), _full((1, D_MODEL)), _full((1, D_MODEL)),
                  _full((N_POOL_GROUPS, POOL_GROUP, POOL_GROUP)), _full((1, D_POOL))] + [ANY] * n_ag,
        out_specs=[row(D_Z), row(D_MODEL), row(D_MODEL), row2(D_MODEL), row2(D_CONV), row2(D_POOL), row2(D_MODEL),
                   _full((N_META, D_IN_PROJ)), _full((N_META, D_MODEL)), _full((3, D_CONV))] + [ANY] * n_ag,
        scratch_shapes=[pltpu.VMEM((N_CHIPS, D_MODEL, IN_SHARD), BF16), pltpu.VMEM((D_MODEL, D_MODEL), BF16),
                        pltpu.VMEM((N_CHIPS, small_rows, D_MODEL // N_CHIPS), F32),
                        pltpu.VMEM((HALO + tm, D_CONV), F32), pltpu.VMEM((HALO + tm, D_POOL), F32),
                        pltpu.SemaphoreType.DMA((N_MIX_SHARDS * N_CHIPS,))] + _AllGather.scratch(n_ag),
        compiler_params=_cparams(2),
    )(x3, g1, g2, poolw, pscale, *shards)
    return outs[:10], outs[10:]


def _ffn_chunks():
    out, r0 = [], 0
    while r0 < D_FF:
        out.append((r0, min(FF_CHUNK, D_FF - r0)))
        r0 += FF_CHUNK
    return out


def _ffn_fwd_bwd(h1, target, g3, g4, gathered, shards):
    n_rows = h1.shape[0]
    tm = min(TM_FFN, n_rows)
    chunks = _ffn_chunks()

    def body(h1_ref, t_ref, g3_ref, g4_ref, wg_all, wu_all, wd_all, wg_s, wu_s, wd_s,
             dh1_ref, f_ref, dd_ref, ds_ref, du_ref, gg_ref, loss_ref, dg3_ref, dg4_ref,
             wg_v, wu_v, wd_v, s_sc, u_sc, sems):
        @pl.when(pl.program_id(0) == 0)
        def _():
            _load_gathered([wg_all, wu_all, wd_all], [wg_s, wu_s, wd_s],
                           [functools.partial(lambda v, j: v.at[pl.ds(j * FF_SHARD, FF_SHARD), :], v)
                            for v in (wg_v, wu_v, wd_v)], sems)
            loss_ref[...] = jnp.zeros_like(loss_ref)
            dg3_ref[...] = jnp.zeros_like(dg3_ref)
            dg4_ref[...] = jnp.zeros_like(dg4_ref)

        h1v = h1_ref[...]
        r3 = _rstd(h1v)
        hh = h1v * r3
        g3v, g4v = g3_ref[...], g4_ref[...]
        f = (hh * g3v).astype(BF16)
        f_ref[...] = f
        d = jnp.zeros((tm, D_MODEL), F32)
        for r0, sz in chunks:
            s = _dot_nt(f, wg_v[r0:r0 + sz, :])
            u = _dot_nt(f, wu_v[r0:r0 + sz, :])
            s_sc[:, r0:r0 + sz] = s
            u_sc[:, r0:r0 + sz] = u
            gc = (s * _sigmoid(s) * u).astype(BF16)
            gg_ref[:, r0:r0 + sz] = gc
            d = d + _dot(gc, wd_v[r0:r0 + sz, :])
        r4 = _rstd(d)
        dh = d * r4
        err = (h1v + dh * g4v) - t_ref[...]
        loss_ref[...] += _rows8(err * err)
        dy = err * (1.0 / D_MODEL)
        dg4_ref[...] += _rows8(dy * dh)
        ddb = _rms_bwd(dy, dh, r4, g4v).astype(BF16)
        dd_ref[...] = ddb
        df = jnp.zeros((tm, D_MODEL), F32)
        for r0, sz in chunks:
            dgg = _dot_nt(ddb, wd_v[r0:r0 + sz, :])
            s = s_sc[:, r0:r0 + sz]
            u = u_sc[:, r0:r0 + sz]
            sig = _sigmoid(s)
            dsc = (dgg * u * (sig * (1.0 + s * (1.0 - sig)))).astype(BF16)
            duc = (dgg * (s * sig)).astype(BF16)
            ds_ref[:, r0:r0 + sz] = dsc
            du_ref[:, r0:r0 + sz] = duc
            df = df + _dot(dsc, wg_v[r0:r0 + sz, :]) + _dot(duc, wu_v[r0:r0 + sz, :])
        dg3_ref[...] += _rows8(df * hh)
        dh1_ref[...] = dy + _rms_bwd(df, hh, r3, g3v)

    row = pl.BlockSpec((tm, D_MODEL), lambda i: (i, 0))
    ffrow = pl.BlockSpec((tm, D_FF), lambda i: (i, 0))
    acc = _full((8, D_MODEL))
    act_bf = jax.ShapeDtypeStruct((n_rows, D_MODEL), BF16)
    ff_bf = jax.ShapeDtypeStruct((n_rows, D_FF), BF16)
    acc_shape = jax.ShapeDtypeStruct((8, D_MODEL), F32)
    w_vmem = pltpu.VMEM((D_FF, D_MODEL), BF16)
    return pl.pallas_call(
        body, name="ffn_fwd_bwd", grid=(n_rows // tm,),
        out_shape=[jax.ShapeDtypeStruct((n_rows, D_MODEL), F32), act_bf, act_bf, ff_bf, ff_bf, ff_bf,
                   acc_shape, acc_shape, acc_shape],
        in_specs=[row, row, _full((1, D_MODEL)), _full((1, D_MODEL))] + [ANY] * 6,
        out_specs=[row, row, row, ffrow, ffrow, ffrow, acc, acc, acc],
        scratch_shapes=[w_vmem, w_vmem, w_vmem, pltpu.VMEM((tm, D_FF), F32), pltpu.VMEM((tm, D_FF), F32),
                        pltpu.SemaphoreType.DMA((3 * N_CHIPS,))],
        compiler_params=_cparams(1),
    )(h1, target, g3, g4, *gathered, *shards)


def _ffn_weight_grads(name, acts, other, exchanged):
    n_rows = other.shape[0]
    n_a, n_ex = len(acts), len(exchanged)
    n_c = n_a
    tk = min(TK_DW, n_rows)
    n_k = n_rows // tk
    half = D_FF // n_c

    def body(other_ref, *rest):
        act_refs = rest[:n_a]
        out_refs = rest[n_a + n_ex:2 * n_a + n_ex]
        c, k = pl.program_id(0), pl.program_id(1)
        if n_ex:
            ex = _ExchangeHalves(rest[n_a:n_a + n_ex], rest[2 * n_a + n_ex:2 * n_a + 2 * n_ex], *rest[-2:])

            @pl.when((c == 0) & (k == 0))
            def _():
                ex.start()

        @pl.when(k == 0)
        def _():
            for o in out_refs:
                o[...] = jnp.zeros_like(o)

        ov = other_ref[...]
        for a, o in zip(act_refs, out_refs):
            o[...] += _dot_tn(a[...], ov)

        if n_ex:
            @pl.when((c == n_c - 1) & (k == n_k - 1))
            def _():
                ex.finish()

    row = pl.BlockSpec((tk, D_MODEL), lambda c, k: (k, 0))
    ffrow = pl.BlockSpec((tk, half), lambda c, k: (k, c))
    out = pl.BlockSpec((half, D_MODEL), lambda c, k: (c, 0))
    outs = pl.pallas_call(
        body, name=name, grid=(n_c, n_k),
        out_shape=[jax.ShapeDtypeStruct((D_FF, D_MODEL), F32)] * n_a + _ExchangeHalves.out_shape(exchanged),
        in_specs=[row] + [ffrow] * n_a + [ANY] * n_ex, out_specs=[out] * n_a + [ANY] * n_ex,
        scratch_shapes=_ExchangeHalves.scratch(n_ex) if n_ex else [],
        compiler_params=_cparams(2),
    )(other, *acts, *exchanged)
    return outs[:n_a], outs[n_a:]


def _mixer_bwd(dh1, m3, z3, conv2, pooled2, x3, zmeta, meta_full, g1, g2, convw, poolw, pscale, gathered, shards,
               after):
    n_seq, seq, _ = x3.shape
    tm = min(TM_MIX_BWD, seq)
    sub = min(SUB_MIX_BWD, tm)
    n_t = seq // tm
    n_out = 13

    def body(dh1_ref, m_ref, z_ref, conv_ref, pooled_ref, x_ref, zm_ref, meta_ref, g1_ref, g2_ref, cw_ref, pw_ref,
             ps_ref, after_ref, win_all, wout_all, win_s, wout_s, *rest):
        (dx_ref, dz_ref, dm_ref, dg1_ref, dg2_ref, dsc_ref, dcw_ref, dpw_ref, dzm_ref, dmeta_ref, dg1m_ref, am_ref,
         dzmb_ref) = rest[:n_out]
        win_v, wout_v, dcb, dqb, mcb, mqb, load_sems = rest[n_out:]
        s, i = pl.program_id(0), pl.program_id(1)
        tr = n_t - 1 - i

        @pl.when((s == 0) & (i == 0))
        def _():
            _load_gathered([win_all, wout_all], [win_s, wout_s],
                           [lambda j: win_v.at[j], lambda j: wout_v.at[pl.ds(j * OUT_SHARD, OUT_SHARD), :]], load_sems)
            for ref in (dg1_ref, dg2_ref, dsc_ref, dcw_ref, dpw_ref, dzm_ref):
                ref[...] = jnp.zeros_like(ref)

        @pl.when(i == 0)
        def _():
            dcb[tm:tm + HALO, :] = jnp.zeros((HALO, D_CONV), F32)
            dqb[tm:tm + HALO, :] = jnp.zeros((HALO, D_POOL), F32)

        @pl.when(i > 0)
        def _():
            dcb[tm:tm + HALO, :] = dcb[0:HALO, :]
            dqb[tm:tm + HALO, :] = dqb[0:HALO, :]

        g1v, g2v = g1_ref[...], g2_ref[...]
        cw = cw_ref[...]

        for r0 in range(tm - sub, -1, -sub):
            rows = slice(r0, r0 + sub)
            dh1v = dh1_ref[0, rows, :]
            mv = m_ref[0, rows, :]
            r2 = _rstd(mv)
            mh = mv * r2
            dg2_ref[...] += _rows8(dh1v * mh)
            dmb = _rms_bwd(dh1v, mh, r2, g2v).astype(BF16)
            dm_ref[rows, :] = dmb
            dyc = _dot_nt(dmb, wout_v[...])
            dyconv = dyc[:, 0:D_CONV]

            for g in range(N_POOL_GROUPS):
                pooled = pooled_ref[rows, _gcols(g)]
                mixed = _dot(pooled, pw_ref[g])
                scale = ps_ref[:, _gcols(g)]
                dyp = dyc[:, D_CONV + g * POOL_GROUP:D_CONV + (g + 1) * POOL_GROUP]
                dsc_ref[:, _gcols(g)] += _rows8(dyp * mixed)
                dmix = (dyp * scale).astype(BF16)
                dpw_ref[g] += _dot_tn(pooled, dmix)
                dqb[rows, _gcols(g)] = _dot_nt(dmix, pw_ref[g])

            zb = z_ref[0, rows, 0:IN_SHARD]
            zc = z_ref[0, rows, IN_SHARD:2 * IN_SHARD]
            zv = z_ref[0, rows, 2 * IN_SHARD:3 * IN_SHARD]
            dconv = dyconv * zb
            dcb[rows, :] = dconv
            d1 = dcb[r0 + 1:r0 + 1 + sub, :]
            d2 = dcb[r0 + 2:r0 + 2 + sub, :]
            dcv = cw[2:3] * dconv + cw[1:2] * d1 + cw[0:1] * d2
            cv = zc * zv
            dcw_ref[0:8, :] += _rows8(cv * d2)
            dcw_ref[8:16, :] += _rows8(cv * d1)
            dcw_ref[16:24, :] += _rows8(cv * dconv)
            dzs = [(dyconv * conv_ref[rows, :]).astype(BF16), (dcv * zv).astype(BF16), (dcv * zc).astype(BF16),
                   jnp.concatenate([_pool_bwd(dqb, g, r0, sub) for g in range(N_POOL_GROUPS)], axis=1).astype(BF16)]
            da = jnp.zeros((sub, D_MODEL), F32)
            for j in range(N_CHIPS):
                dz_ref[j, rows, :] = dzs[j]
                da = da + _dot_nt(dzs[j], win_v[j])
            xt = x_ref[0, rows, :]
            r1 = _rstd(xt)
            xh = xt * r1
            dg1_ref[...] += _rows8(da * xh)
            dx_ref[0, rows, :] = dh1v + _rms_bwd(da, xh, r1, g1v)

        @pl.when(tr == 0)
        def _():
            mcb[0:HALO, :] = jnp.zeros((HALO, D_CONV), F32)
            mqb[0:HALO, :] = jnp.zeros((HALO, D_POOL), F32)
            mcb[HALO:2 * HALO, :] = dcb[0:HALO, :]
            mqb[HALO:2 * HALO, :] = dqb[0:HALO, :]
            m1 = mcb[1:1 + HALO, :]
            m2 = mcb[2:2 + HALO, :]
            zc_m = zm_ref[:, IN_SHARD:2 * IN_SHARD]
            zv_m = zm_ref[:, 2 * IN_SHARD:3 * IN_SHARD]
            cv_m = zc_m * zv_m
            dcw_ref[0:8, :] += _rows8(cv_m * m2)
            dcw_ref[8:16, :] += _rows8(cv_m * m1)
            dcv_m = cw[1:2] * m1 + cw[0:1] * m2
            dzm_ref[:, IN_SHARD:2 * IN_SHARD] += dcv_m * zv_m
            dzm_ref[:, 2 * IN_SHARD:3 * IN_SHARD] += dcv_m * zc_m
            dzm_ref[:, 3 * IN_SHARD:4 * IN_SHARD] += jnp.concatenate(
                [_pool_bwd(mqb, g, 0, HALO) for g in range(N_POOL_GROUPS)], axis=1)

        @pl.when((s == n_seq - 1) & (i == n_t - 1))
        def _():
            xm = meta_ref[...]
            rm = _rstd(xm)
            xmh = xm * rm
            am_ref[...] = (xmh * g1v).astype(BF16)
            da_m = jnp.zeros((N_META, D_MODEL), F32)
            for j in range(N_CHIPS):
                dzj = dzm_ref[:, j * IN_SHARD:(j + 1) * IN_SHARD].astype(BF16)
                dzmb_ref[j] = dzj
                da_m = da_m + _dot_nt(dzj, win_v[j])
            dg1m_ref[...] = _rows8(da_m * xmh)
            dmeta_ref[...] = _rms_bwd(da_m, xmh, rm, g1v)

    row3 = lambda c: pl.BlockSpec((1, tm, c), lambda s, i: (s, n_t - 1 - i, 0))
    row2 = lambda c: pl.BlockSpec((tm, c), lambda s, i: (s * n_t + n_t - 1 - i, 0))
    n_rows = n_seq * seq
    outs = pl.pallas_call(
        body, name="mixer_bwd", grid=(n_seq, n_t),
        out_shape=[jax.ShapeDtypeStruct((n_seq, seq, D_MODEL), F32),
                   jax.ShapeDtypeStruct((N_CHIPS, n_rows, IN_SHARD), BF16), jax.ShapeDtypeStruct((n_rows, D_MODEL), BF16),
                   jax.ShapeDtypeStruct((8, D_MODEL), F32), jax.ShapeDtypeStruct((8, D_MODEL), F32),
                   jax.ShapeDtypeStruct((8, D_POOL), F32), jax.ShapeDtypeStruct((24, D_CONV), F32),
                   jax.ShapeDtypeStruct((N_POOL_GROUPS, POOL_GROUP, POOL_GROUP), F32),
                   jax.ShapeDtypeStruct((N_META, D_IN_PROJ), F32),
                   jax.ShapeDtypeStruct((N_META, D_MODEL), F32), jax.ShapeDtypeStruct((8, D_MODEL), F32),
                   jax.ShapeDtypeStruct((N_META, D_MODEL), BF16),
                   jax.ShapeDtypeStruct((N_CHIPS, N_META, IN_SHARD), BF16)],
        in_specs=[row3(D_MODEL), row3(D_MODEL), row3(D_Z), row2(D_CONV), row2(D_POOL), row3(D_MODEL),
                  _full((N_META, D_IN_PROJ)), _full((N_META, D_MODEL)), _full((1, D_MODEL)), _full((1, D_MODEL)),
                  _full((3, D_CONV)), _full((N_POOL_GROUPS, POOL_GROUP, POOL_GROUP)), _full((1, D_POOL)),
                  _full((8, 128))] + [ANY] * 4,
        out_specs=[row3(D_MODEL), pl.BlockSpec((N_CHIPS, tm, IN_SHARD), lambda s, i: (0, s * n_t + n_t - 1 - i, 0)),
                   row2(D_MODEL),
                   _full((8, D_MODEL)), _full((8, D_MODEL)), _full((8, D_POOL)), _full((24, D_CONV)),
                   _full((N_POOL_GROUPS, POOL_GROUP, POOL_GROUP)), _full((N_META, D_IN_PROJ)),
                   _full((N_META, D_MODEL)), _full((8, D_MODEL)), _full((N_META, D_MODEL)),
                   _full((N_CHIPS, N_META, IN_SHARD))],
        scratch_shapes=[pltpu.VMEM((N_CHIPS, D_MODEL, IN_SHARD), BF16), pltpu.VMEM((D_MODEL, D_MODEL), BF16),
                        pltpu.VMEM((tm + HALO, D_CONV), F32), pltpu.VMEM((tm + HALO, D_POOL), F32),
                        pltpu.VMEM((2 * HALO, D_CONV), F32), pltpu.VMEM((2 * HALO, D_POOL), F32),
                        pltpu.SemaphoreType.DMA((2 * N_CHIPS,))],
        compiler_params=_cparams(2),
    )(dh1, m3, z3, conv2, pooled2, x3, zmeta, meta_full, g1, g2, convw, poolw, pscale, after, *gathered, *shards)
    return outs


def _mixer_weight_grads(a, dz, ycat, dm, a_meta, dz_meta, ffn_sums, small):
    n_rows = a.shape[0]
    tk = min(TK_DW, n_rows)
    n_k = n_rows // tk
    n_sc, n_sm = len(ffn_sums), _AllReduceSmall.N_IN

    def body(a_ref, dz_ref, yc_ref, dm_ref, am_ref, dzm_ref, *rest):
        ins, outs, scratch = rest[:n_sc + n_sm], rest[n_sc + n_sm:2 * n_sc + n_sm + 5], rest[2 * n_sc + n_sm + 5:]
        dwin_ref, dwout_ref = outs[:2]
        scatter = _ScatterToChips(ins[:n_sc], outs[2:2 + n_sc], *scratch[:2])
        reduce_small = _AllReduceSmall(ins[n_sc:], outs[2 + n_sc:], scratch[2:])
        k = pl.program_id(0)

        @pl.when(k == 0)
        def _():
            scatter.start()
            reduce_small.pack_and_send()
            am_t = am_ref[...].T
            for j in range(N_CHIPS):
                dwin_ref[j] = _dot(am_t, dzm_ref[j])
            dwout_ref[...] = jnp.zeros_like(dwout_ref)

        for st in range(2):
            @pl.when(k == ((st + 1) * n_k) // 3)
            def _():
                reduce_small.combine(st)

        a_t = a_ref[...].T
        for j in range(N_CHIPS):
            dwin_ref[j] += _dot(a_t, dz_ref[j])
        dwout_ref[...] += _dot_tn(yc_ref[...], dm_ref[...])

        @pl.when(k == n_k - 1)
        def _():
            reduce_small.combine(2)
            scatter.finish()

    row = pl.BlockSpec((tk, D_MODEL), lambda k: (k, 0))
    outs = pl.pallas_call(
        body, name="mixer_weight_grads", grid=(n_k,),
        out_shape=[jax.ShapeDtypeStruct((N_CHIPS, D_MODEL, IN_SHARD), F32),
                   jax.ShapeDtypeStruct((D_MODEL, D_MODEL), F32)] + _ScatterToChips.out_shape(ffn_sums)
        + _AllReduceSmall.out_shape(),
        in_specs=[row, pl.BlockSpec((N_CHIPS, tk, IN_SHARD), lambda k: (0, k, 0)), row, row,
                  _full((N_META, D_MODEL)), _full((N_CHIPS, N_META, IN_SHARD))] + [ANY] * n_sc
        + [_full(s.shape) for s in small],
        out_specs=[_full((N_CHIPS, D_MODEL, IN_SHARD)), _full((D_MODEL, D_MODEL))] + [ANY] * n_sc
        + [_full(s) for s in _AllReduceSmall.SHAPES],
        scratch_shapes=_ScatterToChips.scratch(n_sc) + _AllReduceSmall.scratch(),
        compiler_params=_cparams(1),
    )(a, dz, ycat, dm, a_meta, dz_meta, *ffn_sums, *small)
    return ([outs[0], outs[1].reshape(N_CHIPS, OUT_SHARD, D_MODEL)], outs[2:2 + n_sc], outs[2 + n_sc:])


def kernel(x, meta_tokens, norm_mix_pre, w_in, conv_w, pool_w, pool_scale, w_out, norm_mix_post, norm_ffn_pre, w_gate, w_up, w_down, norm_ffn_post, loss_target, m_meta_tokens, m_norm_mix_pre, m_w_in, m_conv_w, m_pool_w, m_pool_scale, m_w_out, m_norm_mix_post, m_norm_ffn_pre, m_w_gate, m_w_up, m_w_down, m_norm_ffn_post, v_meta_tokens, v_norm_mix_pre, v_w_in, v_conv_w, v_pool_w, v_pool_scale, v_w_out, v_norm_mix_post, v_norm_ffn_pre, v_w_gate, v_w_up, v_w_down, v_norm_ffn_post):
    n_seq, seq, _ = x.shape
    n_rows = n_seq * seq
    chip = 2 * lax.axis_index("x") + lax.axis_index("y")
    meta_cols = D_MODEL // N_CHIPS
    conv_cols = D_CONV // N_CHIPS

    small = jnp.zeros((2 * HALO, meta_cols), F32)
    small = small.at[0:N_META, :].set(meta_tokens).at[N_META:N_META + 3, 0:conv_cols].set(conv_w[0])
    poolw_bf = pool_w[0].astype(BF16)
    pscale = pool_scale
    g1, g2, g3, g4 = norm_mix_pre, norm_mix_post, norm_ffn_pre, norm_ffn_post
    place = jnp.stack([chip, lax.axis_index("c")]).astype(jnp.int32)

    mix_shards = [w_in[0].astype(BF16), w_out[0].astype(BF16)]
    ffn_shards = [w_gate[0].T.astype(BF16), w_up[0].T.astype(BF16), w_down[0].astype(BF16)]
    ((z3, m3, h1, a_bf, conv2, pooled2, yc_bf, zmeta, meta_full, conv_full),
     (win_all, wout_all, _, *ffn_gathered)) = _mixer_fwd(x, g1, g2, poolw_bf, pscale, mix_shards + [small] + ffn_shards)
    dh1, f_bf, dd_bf, ds_bf, du_bf, gg_bf, lossp, dg3p, dg4p = _ffn_fwd_bwd(
        h1.reshape(n_rows, D_MODEL), loss_target.reshape(n_rows, D_MODEL), g3, g4, ffn_gathered, ffn_shards)
    as_shards = lambda g: g.reshape(N_CHIPS, FF_SHARD, D_MODEL)
    (dwg_t, dwu_t), _ = _ffn_weight_grads("ffn_weight_grads_gate_up", [ds_bf, du_bf], f_bf, [])
    dwg_t, dwu_t = as_shards(dwg_t), as_shards(dwu_t)
    (dwd,), (dwg_recv, dwu_recv) = _ffn_weight_grads("ffn_weight_grads_down", [gg_bf], dd_bf, [dwg_t, dwu_t])
    dwd = as_shards(dwd)
    behind_bwd = _SplitComm("grad_comm_behind_mixer_bwd", [dwd],
                            _add_pairs_multi([dwg_t, dwu_t], [dwg_recv, dwu_recv], place))
    (grad_x, dz_bf, dm_bf, dg1p, dg2p, dscp, dcwp, dpw, _, dmeta, dg1m, a_meta, dz_meta) = _mixer_bwd(
        dh1.reshape(n_seq, seq, D_MODEL), m3, z3, conv2, pooled2, x, zmeta, meta_full, g1, g2, conv_full, poolw_bf,
        pscale, [win_all, wout_all], mix_shards, behind_bwd.start())
    (dwd_recv,), (dwg_rbuf, dwu_rbuf) = behind_bwd.wait(dg2p)
    (dwd,) = behind_bwd.exchanged
    mix_grads, (dwd_rbuf,), (a_red, b_red, c_red) = _mixer_weight_grads(
        a_bf, dz_bf, yc_bf, dm_bf, a_meta, dz_meta, [_add_pairs(dwd, dwd_recv, place)],
        [dg1p, dg1m, dg2p, dg3p, dg4p, lossp, dmeta, dscp, dcwp, dpw.reshape(SMALL_C_ROWS, POOL_GROUP)])

    mix_recvs = _exchange_halves(mix_grads)
    behind_tail = _SplitComm("grad_comm_behind_ffn_tail", [], _add_pairs_multi(mix_grads, mix_recvs, place))
    ffn_red = _add_chips([dwg_t, dwu_t, dwd], [dwg_recv, dwu_recv, dwd_recv], [dwg_rbuf, dwu_rbuf, dwd_rbuf],
                         place, after=behind_tail.start(), name="grad_add_chips_ffn")
    as_full = lambda r: r.reshape(2 * r.shape[1], r.shape[2])
    g_wg_t, g_wu_t, g_wd = [as_full(r) for r in _gather_halves(list(ffn_red), "grad_gather_halves_ffn")]
    ffn_out = _adamw_big([(w_gate[0].T, g_wg_t, m_w_gate[0].T, v_w_gate[0].T),
                          (w_up[0].T, g_wu_t, m_w_up[0].T, v_w_up[0].T), (w_down[0], g_wd, m_w_down[0], v_w_down[0])])
    _, mix_rbufs = behind_tail.wait(ffn_out[2][0])
    mix_red = _add_chips(mix_grads, mix_recvs, mix_rbufs, place)
    g_win, g_wout = [as_full(r) for r in _gather_halves(list(mix_red), "grad_gather_halves_mixer")]

    loss = a_red[4, 0]
    g_g1, g_g2, g_g3, g_g4 = a_red[0:1], a_red[1:2], a_red[2:3], a_red[3:4]
    g_meta = lax.dynamic_slice(a_red, (8, chip * meta_cols), (N_META, meta_cols))
    g_pscale = b_red[0:1]
    g_conv = lax.dynamic_slice(b_red, (1, chip * conv_cols), (3, conv_cols))
    g_poolw = c_red

    big_out = (_adamw_big([(w_in[0], g_win, m_w_in[0], v_w_in[0])])
               + _adamw_big([(w_out[0], g_wout, m_w_out[0], v_w_out[0])]) + ffn_out)
    big_out[2] = [o.T for o in big_out[2]]
    big_out[3] = [o.T for o in big_out[3]]
    g_wg, g_wu = g_wg_t.T, g_wu_t.T
    small_groups = [
        (meta_tokens, g_meta, m_meta_tokens, v_meta_tokens),
        (g1, g_g1, m_norm_mix_pre, v_norm_mix_pre),
        (conv_w[0], g_conv, m_conv_w[0], v_conv_w[0]),
        (pool_w.reshape(SMALL_C_ROWS, POOL_GROUP), g_poolw, m_pool_w.reshape(SMALL_C_ROWS, POOL_GROUP),
         v_pool_w.reshape(SMALL_C_ROWS, POOL_GROUP)),
        (pool_scale, g_pscale, m_pool_scale, v_pool_scale),
        (g2, g_g2, m_norm_mix_post, v_norm_mix_post),
        (g3, g_g3, m_norm_ffn_pre, v_norm_ffn_pre),
        (g4, g_g4, m_norm_ffn_post, v_norm_ffn_post),
    ]
    small_out = _adamw_small(small_groups)

    grads_out = [g_meta, g_g1, g_win[None], g_conv[None], g_poolw.reshape(pool_w.shape), g_pscale, g_wout[None],
                 g_g2, g_g3, g_wg[None], g_wu[None], g_wd[None], g_g4]
    s_meta, s_g1, s_conv, s_poolw, s_pscale, s_g2, s_g3, s_g4 = small_out
    b_win, b_wout, b_wg, b_wu, b_wd = big_out

    def leaf(k):
        return [s_meta[k], s_g1[k], b_win[k][None], s_conv[k][None], s_poolw[k].reshape(pool_w.shape), s_pscale[k],
                b_wout[k][None], s_g2[k], s_g3[k], b_wg[k][None], b_wu[k][None], b_wd[k][None], s_g4[k]]

    return (loss, grad_x, *grads_out, *leaf(0), *leaf(1), *leaf(2))
```

```python
import functools

import jax
import jax.numpy as jnp
from jax import lax
from jax.experimental import pallas as pl
from jax.experimental.pallas import tpu as pltpu

F32 = jnp.float32
BF16 = jnp.bfloat16
MESH = pl.DeviceIdType.MESH

D_MODEL = 1024
D_CONV = 512
D_POOL = 512
POOL_GROUP = 128
N_POOL_GROUPS = 4
D_IN_PROJ = 2048
D_FF = 2816
N_CHIPS = 4
FF_SHARD = D_FF // N_CHIPS
IN_SHARD = D_IN_PROJ // N_CHIPS
OUT_SHARD = D_MODEL // N_CHIPS
D_Z = 3 * IN_SHARD
N_META = 16
HALO = 16
RMS_EPS = 1e-6

ADAM_LR = 0.001
ADAM_B1 = 0.9
ADAM_B2 = 0.999
ADAM_EPS = 1e-08
ADAM_WD = 0.01
ADAM_STEP = 10

TM_MIX_FWD = 512
TM_MIX_BWD = 512
SUB_MIX_BWD = 512
TM_FFN = 256
TK_DW = 1024
FF_CHUNK = 1024
VMEM_LIMIT = 56 * 1024 * 1024


def _cparams(n_grid):
    return pltpu.CompilerParams(dimension_semantics=("arbitrary",) * n_grid, vmem_limit_bytes=VMEM_LIMIT)


def _dot(a, b):
    return jnp.dot(a, b, preferred_element_type=F32)


def _dot_nt(a, b):
    return lax.dot_general(a, b, (((1,), (1,)), ((), ())), preferred_element_type=F32)


def _dot_tn(a, b):
    return lax.dot_general(a, b, (((0,), (0,)), ((), ())), preferred_element_type=F32)


def _rows8(v):
    r, c = v.shape
    return v.reshape(r // 8, 8, c).sum(axis=0)


def _rstd(v):
    return lax.rsqrt(jnp.mean(v * v, axis=-1, keepdims=True) + RMS_EPS)


def _rms_bwd(dy, xhat, rstd, gain):
    dyg = dy * gain
    return rstd * (dyg - xhat * jnp.mean(dyg * xhat, axis=-1, keepdims=True))


def _sigmoid(v):
    return 1.0 / (1.0 + jnp.exp(-v))


def _gcols(g):
    return slice(g * POOL_GROUP, (g + 1) * POOL_GROUP)


def _window_sum(e, g, ahead):
    n = e.shape[0]
    w = e
    for level in range(g + 1):
        shift = 1 << level
        w = w + pltpu.roll(w, (n - shift) if ahead else shift, 0)
    return w


def _pool_fwd(pb, g, n):
    e = pb[0:HALO + n, _gcols(g)]
    return _window_sum(e, g, False)[HALO:, :] * (1.0 / (2 << g)) - e[HALO:, :]


def _pool_bwd(qb, g, r0, n):
    e = qb[r0:r0 + n + HALO, _gcols(g)]
    return _window_sum(e, g, True)[0:n, :] * (1.0 / (2 << g)) - e[0:n, :]


def _full(shape):
    nd = len(shape)
    return pl.BlockSpec(shape, lambda *_: (0,) * nd)


ANY = pl.BlockSpec(memory_space=pl.ANY)


def _mesh_pos():
    x, y, c = lax.axis_index("x"), lax.axis_index("y"), lax.axis_index("c")
    chips = [(1 - x, y), (x, 1 - y), (1 - x, 1 - y)]
    return x, y, c, chips


def _half(ref, h):
    hr = ref.shape[0] // 2
    return ref.at[pl.ds(h * hr, hr), :]


class _AllGather:
    PER_ARRAY = 9

    def __init__(self, ins, outs, send_sems, recv_sems):
        self.ins, self.outs, self.send_sems, self.recv_sems = ins, outs, send_sems, recv_sems
        self.n = len(ins)

    @classmethod
    def scratch(cls, n):
        return [pltpu.SemaphoreType.DMA((cls.PER_ARRAY * n,)), pltpu.SemaphoreType.DMA((cls.PER_ARRAY * n,))]

    @staticmethod
    def out_shape(shards):
        return [jax.ShapeDtypeStruct((N_CHIPS,) + s.shape, s.dtype) for s in shards]

    def _copy(self, a, k, src, dst, to):
        i = self.PER_ARRAY * a + k
        return pltpu.make_async_remote_copy(src_ref=src, dst_ref=dst, send_sem=self.send_sems.at[i],
                                            recv_sem=self.recv_sems.at[i], device_id=to, device_id_type=MESH)

    def _piece(self, a, chip, piece, h=None):
        h = lax.axis_index("c") if h is None else h
        rows = self.ins[a].shape[0] // 4
        return self.outs[a].at[chip].at[pl.ds((2 * h + piece) * rows, rows), :]

    def _own(self, a, k):
        x, y, c, chips = _mesh_pos()
        piece = (1, 0, 0, 1)[k]
        rows = self.ins[a].shape[0] // 4
        src = self.ins[a].at[pl.ds((2 * c + piece) * rows, rows), :]
        return self._copy(a, k, src, self._piece(a, 2 * x + y, piece), (*chips[k // 2], c))

    def _relay(self, a, k):
        x, y, c, chips = _mesh_pos()
        source, to, piece = (chips[1], chips[0], 0) if k == 4 else (chips[0], chips[1], 1)
        rows = self._piece(a, 2 * source[0] + source[1], piece)
        return self._copy(a, k, rows, rows, (*to, c))

    def _sibling(self, a, k, h):
        x, y, c, chips = _mesh_pos()
        chip = chips[k - 6]
        slot = _half(self.outs[a].at[2 * chip[0] + chip[1]], h)
        return self._copy(a, k, slot, slot, (x, y, 1 - c))

    def start(self, arrays=None):
        for a in (range(self.n) if arrays is None else arrays):
            for k in range(4):
                self._own(a, k).start()

    def relay(self, a):
        self._own(a, 2).wait_recv()
        self._relay(a, 4).start()
        self._own(a, 0).wait_recv()
        self._relay(a, 5).start()

    def forward(self, a):
        c = lax.axis_index("c")
        self._own(a, 1).wait_recv()
        self._sibling(a, 6, c).start()
        self._own(a, 3).wait_recv()
        self._sibling(a, 7, c).start()
        self._relay(a, 4).wait_recv()
        self._relay(a, 5).wait_recv()
        self._sibling(a, 8, c).start()

    def finish(self, arrays=None):
        c = lax.axis_index("c")
        arrays = range(self.n) if arrays is None else arrays
        for a in arrays:
            for k in range(6, 9):
                self._sibling(a, k, 1 - c).wait_recv()
        for a in arrays:
            for k in range(4):
                self._own(a, k).wait_send()
            for k in range(4, 6):
                self._relay(a, k).wait_send()
            for k in range(6, 9):
                self._sibling(a, k, c).wait_send()


class _ExchangeHalves:
    def __init__(self, ins, recvs, send_sems, recv_sems):
        self.ins, self.recvs, self.send_sems, self.recv_sems = ins, recvs, send_sems, recv_sems

    @staticmethod
    def scratch(n):
        return [pltpu.SemaphoreType.DMA((n,)), pltpu.SemaphoreType.DMA((n,))]

    @staticmethod
    def out_shape(grads):
        return [jax.ShapeDtypeStruct((g.shape[0], g.shape[1] // 2, g.shape[2]), g.dtype) for g in grads]

    def _copies(self):
        x, y, c, _ = _mesh_pos()
        out = []
        for a, (src, dst) in enumerate(zip(self.ins, self.recvs)):
            hr = src.shape[1] // 2
            out.append(pltpu.make_async_remote_copy(
                src_ref=src.at[:, pl.ds((1 - c) * hr, hr), :], dst_ref=dst, send_sem=self.send_sems.at[a],
                recv_sem=self.recv_sems.at[a], device_id=(x, y, 1 - c), device_id_type=MESH))
        return out

    def start(self):
        for cp in self._copies():
            cp.start()

    def finish(self):
        for cp in self._copies():
            cp.wait()


def _exchange_halves(grads):
    n = len(grads)

    def body(*refs):
        ex = _ExchangeHalves(refs[:n], refs[n:2 * n], *refs[2 * n:])
        ex.start()
        ex.finish()

    return pl.pallas_call(
        body, name="grad_exchange_halves", out_shape=_ExchangeHalves.out_shape(grads),
        in_specs=[ANY] * n, out_specs=[ANY] * n, scratch_shapes=_ExchangeHalves.scratch(n),
    )(*grads)


class _ScatterToChips:
    def __init__(self, ins, rbufs, send_sems, recv_sems):
        self.ins, self.rbufs, self.send_sems, self.recv_sems = ins, rbufs, send_sems, recv_sems

    @staticmethod
    def scratch(n):
        return [pltpu.SemaphoreType.DMA((3 * n,)), pltpu.SemaphoreType.DMA((3 * n,))]

    @staticmethod
    def out_shape(sums):
        return [jax.ShapeDtypeStruct((3,) + s.shape[1:], BF16) for s in sums]

    def _copies(self):
        x, y, c, chips = _mesh_pos()
        out = []
        for a, (src, dst) in enumerate(zip(self.ins, self.rbufs)):
            for k, chip in enumerate(chips):
                out.append(pltpu.make_async_remote_copy(
                    src_ref=src.at[2 * chip[0] + chip[1]], dst_ref=dst.at[k], send_sem=self.send_sems.at[3 * a + k],
                    recv_sem=self.recv_sems.at[3 * a + k], device_id=(*chip, c), device_id_type=MESH))
        return out

    def start(self):
        for cp in self._copies():
            cp.start()

    def finish(self):
        for cp in self._copies():
            cp.wait()


HBM = pl.BlockSpec(memory_space=pltpu.HBM)
SEM = pl.BlockSpec(memory_space=pltpu.SEMAPHORE)


class _SplitComm:
    def __init__(self, name, exchanged, scattered):
        self.name, self.n_ex, self.n_sc = name, len(exchanged), len(scattered)
        self.n_copies = self.n_ex + 3 * self.n_sc
        zones = ([lax.empty((g.shape[0], g.shape[1] // 2, g.shape[2]), g.dtype) for g in exchanged]
                 + [lax.empty((3,) + s.shape[1:], s.dtype) for s in scattered])
        self.buffers = [pltpu.with_memory_space_constraint(v, pltpu.HBM)
                        for v in list(exchanged) + list(scattered) + zones]

    def _copies(self, bufs, send_sems, recv_sems):
        x, y, c, chips = _mesh_pos()
        n_src = self.n_ex + self.n_sc
        out = []
        for a in range(self.n_ex):
            hr = bufs[a].shape[1] // 2
            out.append(pltpu.make_async_remote_copy(
                src_ref=bufs[a].at[:, pl.ds((1 - c) * hr, hr), :], dst_ref=bufs[n_src + a], send_sem=send_sems[a],
                recv_sem=recv_sems[a], device_id=(x, y, 1 - c), device_id_type=MESH))
        for a in range(self.n_sc):
            for k, chip in enumerate(chips):
                i = self.n_ex + 3 * a + k
                out.append(pltpu.make_async_remote_copy(
                    src_ref=bufs[self.n_ex + a].at[2 * chip[0] + chip[1]], dst_ref=bufs[n_src + self.n_ex + a].at[k],
                    send_sem=send_sems[i], recv_sem=recv_sems[i], device_id=(*chip, c), device_id_type=MESH))
        return out

    def start(self):
        n_buf, n_cp = len(self.buffers), self.n_copies

        def body(*refs):
            bufs = refs[:n_buf]
            send_sems, recv_sems = refs[n_buf:n_buf + n_cp], refs[n_buf + n_cp:n_buf + 2 * n_cp]
            for cp in self._copies(bufs, send_sems, recv_sems):
                cp.start()
            refs[-1][...] = jnp.zeros_like(refs[-1])

        outs = pl.pallas_call(
            body, name=self.name + "_start",
            out_shape=[pltpu.SemaphoreType.DMA(())] * (2 * n_cp) + [pltpu.HBM(b.shape, b.dtype) for b in self.buffers]
            + [jax.ShapeDtypeStruct((8, 128), F32)],
            in_specs=[HBM] * n_buf, out_specs=[SEM] * (2 * n_cp) + [HBM] * n_buf + [pl.BlockSpec(memory_space=pltpu.VMEM)],
            input_output_aliases={i: 2 * n_cp + i for i in range(n_buf)},
            compiler_params=pltpu.CompilerParams(has_side_effects=pltpu.SideEffectType.DATAFLOW_SIDE_EFFECTING),
        )(*self.buffers)
        self.sems, self.buffers = outs[:2 * n_cp], outs[2 * n_cp:2 * n_cp + n_buf]
        return outs[-1]

    def wait(self, after):
        n_buf, n_cp = len(self.buffers), self.n_copies

        def body(*refs):
            bufs = refs[:n_buf]
            send_sems, recv_sems = refs[n_buf:n_buf + n_cp], refs[n_buf + n_cp:n_buf + 2 * n_cp]
            for cp in self._copies(bufs, send_sems, recv_sems):
                cp.wait_send()
                cp.wait_recv()

        outs = pl.pallas_call(
            body, name=self.name + "_wait", out_shape=[pltpu.HBM(b.shape, b.dtype) for b in self.buffers],
            in_specs=[HBM] * n_buf + [SEM] * (2 * n_cp) + [ANY], out_specs=[HBM] * n_buf,
            input_output_aliases={i: i for i in range(n_buf)},
            compiler_params=pltpu.CompilerParams(has_side_effects=pltpu.SideEffectType.DATAFLOW_SIDE_EFFECTING),
        )(*self.buffers, *self.sems, after)
        self.exchanged = outs[:self.n_ex]
        zones = outs[self.n_ex + self.n_sc:]
        return zones[:self.n_ex], zones[self.n_ex:]


def _gather_halves(halves, name, after=None):
    n = len(halves)
    extra = [] if after is None else [after]

    def body(*refs):
        ins, outs = refs[:n], refs[n + len(extra):2 * n + len(extra)]
        send_sems, recv_sems = refs[2 * n + len(extra):]
        x, y, c, _ = _mesh_pos()
        sib = (x, y, 1 - c)
        remote = [pltpu.make_async_remote_copy(src_ref=ins[a].at[c], dst_ref=outs[a].at[c],
                                               send_sem=send_sems.at[a], recv_sem=recv_sems.at[a],
                                               device_id=sib, device_id_type=MESH) for a in range(n)]
        for cp in remote:
            cp.start()
        for a in range(n):
            pltpu.make_async_remote_copy(src_ref=ins[a].at[1 - c], dst_ref=outs[a].at[1 - c], send_sem=send_sems.at[a],
                                         recv_sem=recv_sems.at[a], device_id=sib, device_id_type=MESH).wait_recv()
        for cp in remote:
            cp.wait_send()

    return pl.pallas_call(
        body, name=name,
        out_shape=[jax.ShapeDtypeStruct(h.shape, F32) for h in halves],
        in_specs=[ANY] * (n + len(extra)), out_specs=[ANY] * n, input_output_aliases={a: a for a in range(n)},
        scratch_shapes=[pltpu.SemaphoreType.DMA((n,)), pltpu.SemaphoreType.DMA((n,))],
    )(*halves, *extra)


SMALL_A_ROWS = 24
SMALL_B_ROWS = 8
SMALL_C_ROWS = N_POOL_GROUPS * POOL_GROUP


class _AllReduceSmall:
    N_IN = 10
    SHAPES = [(SMALL_A_ROWS, D_MODEL), (SMALL_B_ROWS, D_CONV), (SMALL_C_ROWS, POOL_GROUP)]

    def __init__(self, ins, outs, scratch):
        self.ins, self.outs = ins, outs
        self.bufs, self.rcvs, self.send_sems, self.recv_sems = scratch[:3], scratch[3:6], scratch[6], scratch[7]

    @classmethod
    def scratch(cls):
        return ([pltpu.VMEM((3,) + s, F32) for s in cls.SHAPES] + [pltpu.VMEM((3,) + s, F32) for s in cls.SHAPES]
                + [pltpu.SemaphoreType.DMA((9,)), pltpu.SemaphoreType.DMA((9,))])

    @classmethod
    def out_shape(cls):
        return [jax.ShapeDtypeStruct(s, F32) for s in cls.SHAPES]

    def _copies(self, st):
        x, y, c, _ = _mesh_pos()
        peer = [(x, y, 1 - c), (1 - x, y, c), (x, 1 - y, c)][st]
        return [pltpu.make_async_remote_copy(
            src_ref=buf.at[st], dst_ref=rcv.at[st], send_sem=self.send_sems.at[3 * st + i],
            recv_sem=self.recv_sems.at[3 * st + i], device_id=peer, device_id_type=MESH)
            for i, (buf, rcv) in enumerate(zip(self.bufs, self.rcvs))]

    def pack_and_send(self):
        dg1_ref, dg1m_ref, dg2_ref, dg3_ref, dg4_ref, loss_ref, dmeta_ref, dsc_ref, dcw_ref, dpw_ref = self.ins
        a_buf, b_buf, c_buf = self.bufs

        def rowsum(v):
            return jnp.sum(v, axis=0, keepdims=True)

        a_buf[0, 0:1, :] = rowsum(dg1_ref[...] + dg1m_ref[...])
        a_buf[0, 1:2, :] = rowsum(dg2_ref[...])
        a_buf[0, 2:3, :] = rowsum(dg3_ref[...])
        a_buf[0, 3:4, :] = rowsum(dg4_ref[...])
        loss = jnp.sum(rowsum(loss_ref[...]), axis=1, keepdims=True) * (0.5 / D_MODEL)
        a_buf[0, 4:5, :] = jnp.broadcast_to(loss, (1, D_MODEL))
        a_buf[0, 5:8, :] = jnp.zeros((3, D_MODEL), F32)
        a_buf[0, 8:24, :] = dmeta_ref[...]
        b_buf[0, 0:1, :] = rowsum(dsc_ref[...])
        for k in range(3):
            b_buf[0, 1 + k:2 + k, :] = rowsum(dcw_ref[8 * k:8 * k + 8, :])
        b_buf[0, 4:8, :] = jnp.zeros((4, D_CONV), F32)
        c_buf[0] = dpw_ref[...]
        for cp in self._copies(0):
            cp.start()

    def combine(self, st):
        for cp in self._copies(st):
            cp.wait()
        if st < 2:
            for buf, rcv in zip(self.bufs, self.rcvs):
                buf[st + 1] = buf[st] + rcv[st]
            for cp in self._copies(st + 1):
                cp.start()
        else:
            for out, buf, rcv in zip(self.outs, self.bufs, self.rcvs):
                out[...] = buf[st] + rcv[st]


def _row_block(rows):
    for cand in (512, 448, 384, 352, 320, 256, 128, 64, 32, 16):
        if rows % cand == 0:
            return cand
    return rows


def _add_pairs_multi(grads, recvs, place):
    n = len(grads)
    n_sh = grads[0].shape[0]
    halves = [g.shape[1] // 2 for g in grads]
    n_steps = halves[0] // _row_block(halves[0])
    blocks = [(hr // n_steps, g.shape[2]) for hr, g in zip(halves, grads)]

    def body(place_ref, *refs):
        for a_ref, b_ref, o_ref in zip(refs[:n], refs[n:2 * n], refs[2 * n:]):
            o_ref[...] = (a_ref[0] + b_ref[...]).astype(BF16)

    return pl.pallas_call(
        body, name="grad_add_pairs",
        grid_spec=pltpu.PrefetchScalarGridSpec(
            num_scalar_prefetch=1, grid=(n_sh, n_steps),
            in_specs=[pl.BlockSpec((1, 1, br, cols), lambda j, i, p: (j, p[1], i, 0)) for br, cols in blocks]
            + [pl.BlockSpec((1, br, cols), lambda j, i, p: (j, i, 0)) for br, cols in blocks],
            out_specs=[pl.BlockSpec((1, br, cols), lambda j, i, p: (j, i, 0)) for br, cols in blocks]),
        out_shape=[jax.ShapeDtypeStruct((n_sh, hr, g.shape[2]), BF16) for hr, g in zip(halves, grads)],
        compiler_params=_cparams(2),
    )(place, *[g.reshape(n_sh, 2, hr, g.shape[2]) for hr, g in zip(halves, grads)], *recvs)


def _add_pairs(grad, recv, place):
    return _add_pairs_multi([grad], [recv], place)[0]


def _add_chips(grads, recvs, rbufs, place, after=None, name="grad_add_chips"):
    n = len(grads)
    n_sh = grads[0].shape[0]
    halves = [g.shape[1] // 2 for g in grads]
    n_steps = halves[0] // _row_block(halves[0])
    blocks = [(hr // n_steps, g.shape[2]) for hr, g in zip(halves, grads)]
    extra = [] if after is None else [after]

    def body(place_ref, *refs):
        for a_ref, b_ref, r_ref, o_ref in zip(refs[:n], refs[n:2 * n], refs[2 * n:3 * n], refs[3 * n + len(extra):]):
            own = a_ref[0, 0] + b_ref[0]
            o_ref[0] = ((own + r_ref[0].astype(F32)) + r_ref[1].astype(F32)) + r_ref[2].astype(F32)

    return pl.pallas_call(
        body, name=name,
        grid_spec=pltpu.PrefetchScalarGridSpec(
            num_scalar_prefetch=1, grid=(n_steps,),
            in_specs=[pl.BlockSpec((1, 1, br, cols), lambda i, p: (p[0], p[1], i, 0)) for br, cols in blocks]
            + [pl.BlockSpec((1, br, cols), lambda i, p: (p[0], i, 0)) for br, cols in blocks]
            + [pl.BlockSpec((3, br, cols), lambda i, p: (0, i, 0)) for br, cols in blocks]
            + [pl.BlockSpec((8, 128), lambda i, p: (0, 0))] * len(extra),
            out_specs=[pl.BlockSpec((1, br, cols), lambda i, p: (p[1], i, 0)) for br, cols in blocks]),
        out_shape=[jax.ShapeDtypeStruct((2, hr, g.shape[2]), F32) for hr, g in zip(halves, grads)],
        compiler_params=_cparams(1),
    )(place, *[g.reshape(n_sh, 2, hr, g.shape[2]) for hr, g in zip(halves, grads)], *recvs, *rbufs, *extra)


def _adamw_math(w, g, m, v):
    m2 = ADAM_B1 * m + (1.0 - ADAM_B1) * g
    v2 = ADAM_B2 * v + (1.0 - ADAM_B2) * (g * g)
    m_hat = m2 / (1.0 - ADAM_B1 ** ADAM_STEP)
    v_hat = v2 / (1.0 - ADAM_B2 ** ADAM_STEP)
    delta = -ADAM_LR * (m_hat / (jnp.sqrt(v_hat) + ADAM_EPS) + ADAM_WD * w)
    return delta, m2, v2


def _adamw_big(groups):
    n = len(groups)
    rows, cols = groups[0][0].shape
    br = _row_block(rows)
    if n > 1 and br % 16 == 0:
        br //= 2

    def body(*refs):
        for i in range(n):
            w_ref, g_ref, m_ref, v_ref = refs[4 * i:4 * i + 4]
            d_ref, m2_ref, v2_ref = refs[4 * n + 3 * i:4 * n + 3 * i + 3]
            d, m2, v2 = _adamw_math(w_ref[...], g_ref[...], m_ref[...], v_ref[...])
            d_ref[...] = d
            m2_ref[...] = m2
            v2_ref[...] = v2

    spec = pl.BlockSpec((br, cols), lambda i: (i, 0))
    outs = pl.pallas_call(
        body, name="adamw_big", grid=(rows // br,),
        out_shape=[jax.ShapeDtypeStruct((rows, cols), F32)] * (3 * n),
        in_specs=[spec] * (4 * n), out_specs=[spec] * (3 * n), compiler_params=_cparams(1),
    )(*[a for grp in groups for a in grp])
    return [list(outs[3 * i:3 * i + 3]) for i in range(n)]


def _adamw_small(groups):
    n = len(groups)

    def body(*refs):
        ins, outs = refs[:4 * n], refs[4 * n:]
        for i in range(n):
            w, g, m, v = (r[...] for r in ins[4 * i:4 * i + 4])
            d, m2, v2 = _adamw_math(w, g, m, v)
            outs[3 * i][...] = d
            outs[3 * i + 1][...] = m2
            outs[3 * i + 2][...] = v2

    vm = pl.BlockSpec(memory_space=pltpu.VMEM)
    flat = [a for grp in groups for a in grp]
    out_shape = [jax.ShapeDtypeStruct(grp[0].shape, F32) for grp in groups for _ in range(3)]
    outs = pl.pallas_call(body, name="adamw_small", out_shape=out_shape,
                          in_specs=[vm] * (4 * n), out_specs=[vm] * (3 * n))(*flat)
    return [tuple(outs[3 * i:3 * i + 3]) for i in range(n)]


def _load_gathered(gathered, shards, dst_slots, sems):
    n = len(gathered)
    me = 2 * lax.axis_index("x") + lax.axis_index("y")

    def copies(j, own):
        return [pltpu.make_async_copy(shards[a] if own else gathered[a].at[j], dst_slots[a](j), sems.at[n * j + a])
                for a in range(n)]

    for wait in (False, True):
        for j in range(N_CHIPS):
            for own in (False, True):
                @pl.when((me == j) == own)
                def _():
                    for cp in copies(j, own):
                        cp.wait() if wait else cp.start()


N_MIX_SHARDS = 3


def _mixer_fwd(x3, g1, g2, poolw, pscale, shards):
    n_seq, seq, _ = x3.shape
    tm = min(TM_MIX_FWD, seq)
    n_t = seq // tm
    n_steps = n_seq * n_t
    n_ag = len(shards)
    n_ffn = n_ag - N_MIX_SHARDS
    small_rows = shards[2].shape[0]
    conv_cols = D_CONV // N_CHIPS

    def body(x_ref, g1_ref, g2_ref, pw_ref, ps_ref, *rest):
        ag = _AllGather(rest[:n_ag], rest[n_ag + 10:2 * n_ag + 10], *rest[-2:])
        (z_ref, m_ref, h1_ref, a_ref, conv_ref, pooled_ref, yc_ref, zm_ref, meta_ref,
         cw_ref) = rest[n_ag:n_ag + 10]
        win_v, wout_v, small_v, cvb, pb, load_sems = rest[2 * n_ag + 10:-2]
        s, t = pl.program_id(0), pl.program_id(1)
        step = s * n_t + t

        @pl.when(step == 0)
        def _():
            ag.start(range(N_MIX_SHARDS))
            for a in range(N_MIX_SHARDS):
                ag.relay(a)
            for a in range(N_MIX_SHARDS):
                ag.forward(a)
            ag.finish(range(N_MIX_SHARDS))
            ag.start(range(N_MIX_SHARDS, n_ag))
            _load_gathered(ag.outs[:N_MIX_SHARDS], ag.ins[:N_MIX_SHARDS],
                           [lambda j: win_v.at[j], lambda j: wout_v.at[pl.ds(j * OUT_SHARD, OUT_SHARD), :],
                            lambda j: small_v.at[j]], load_sems)

            meta = jnp.concatenate([small_v[j, 0:N_META, :] for j in range(N_CHIPS)], axis=1)
            meta_ref[...] = meta
            cw_ref[...] = jnp.concatenate([small_v[j, N_META:N_META + 3, 0:conv_cols] for j in range(N_CHIPS)], axis=1)
            a_meta = (meta * _rstd(meta) * g1_ref[...]).astype(BF16)
            for j in range(N_CHIPS):
                zm_ref[:, j * IN_SHARD:(j + 1) * IN_SHARD] = _dot(a_meta, win_v[j])

        for i in range(n_ffn):
            @pl.when(step == ((i + 1) * n_steps) // (2 * n_ffn + 2))
            def _():
                ag.relay(N_MIX_SHARDS + i)

        for i in range(n_ffn):
            @pl.when(step == min(n_steps // 2 + ((i + 1) * n_steps) // (2 * n_ffn + 2), n_steps - 1))
            def _():
                ag.forward(N_MIX_SHARDS + i)

        @pl.when(t == 0)
        def _():
            cvb[0:HALO, :] = zm_ref[:, IN_SHARD:2 * IN_SHARD] * zm_ref[:, 2 * IN_SHARD:3 * IN_SHARD]
            pb[0:HALO, :] = zm_ref[:, 3 * IN_SHARD:4 * IN_SHARD]

        @pl.when(t > 0)
        def _():
            cvb[0:HALO, :] = cvb[tm:tm + HALO, :]
            pb[0:HALO, :] = pb[tm:tm + HALO, :]

        xt = x_ref[0]
        a = (xt * _rstd(xt) * g1_ref[...]).astype(BF16)
        a_ref[...] = a
        zb = _dot(a, win_v[0])
        zc = _dot(a, win_v[1])
        zv = _dot(a, win_v[2])
        zp = _dot(a, win_v[3])
        z_ref[0, :, 0:IN_SHARD] = zb
        z_ref[0, :, IN_SHARD:2 * IN_SHARD] = zc
        z_ref[0, :, 2 * IN_SHARD:3 * IN_SHARD] = zv
        cv = zc * zv
        cvb[HALO:HALO + tm, :] = cv
        pb[HALO:HALO + tm, :] = zp
        cw = cw_ref[...]
        conv = cw[0:1] * cvb[HALO - 2:HALO - 2 + tm, :] + cw[1:2] * cvb[HALO - 1:HALO - 1 + tm, :] + cw[2:3] * cv
        conv_ref[...] = conv
        parts = [(zb * conv).astype(BF16)]
        for g in range(N_POOL_GROUPS):
            pooled = _pool_fwd(pb, g, tm).astype(BF16)
            pooled_ref[:, _gcols(g)] = pooled
            parts.append((_dot(pooled, pw_ref[g]) * ps_ref[:, _gcols(g)]).astype(BF16))
        ycat = jnp.concatenate(parts, axis=1)
        yc_ref[...] = ycat
        m = _dot(ycat, wout_v[...])
        m_ref[0] = m
        h1_ref[0] = xt + m * _rstd(m) * g2_ref[...]

        @pl.when(step == n_steps - 1)
        def _():
            ag.finish(range(N_MIX_SHARDS, n_ag))

    n_rows = n_seq * seq
    row = lambda c: pl.BlockSpec((1, tm, c), lambda s, t: (s, t, 0))
    row2 = lambda c: pl.BlockSpec((tm, c), lambda s, t: (s * n_t + t, 0))
    outs = pl.pallas_call(
        body, name="mixer_fwd", grid=(n_seq, n_t),
        out_shape=[jax.ShapeDtypeStruct((n_seq, seq, D_Z), F32), jax.ShapeDtypeStruct((n_seq, seq, D_MODEL), F32),
                   jax.ShapeDtypeStruct((n_seq, seq, D_MODEL), F32), jax.ShapeDtypeStruct((n_rows, D_MODEL), BF16),
                   jax.ShapeDtypeStruct((n_rows, D_CONV), F32), jax.ShapeDtypeStruct((n_rows, D_POOL), BF16),
                   jax.ShapeDtypeStruct((n_rows, D_MODEL), BF16), jax.ShapeDtypeStruct((N_META, D_IN_PROJ), F32),
                   jax.ShapeDtypeStruct((N_META, D_MODEL), F32), jax.ShapeDtypeStruct((3, D_CONV), F32)]
        + _AllGather.out_shape(shards),
        in_specs=[row(D_MODEL), _full((1, D_MODEL)), _full((1, D_MODEL)),
                  _full((N_POOL_GROUPS, POOL_GROUP, POOL_GROUP)), _full((1, D_POOL))] + [ANY] * n_ag,
        out_specs=[row(D_Z), row(D_MODEL), row(D_MODEL), row2(D_MODEL), row2(D_CONV), row2(D_POOL), row2(D_MODEL),
                   _full((N_META, D_IN_PROJ)), _full((N_META, D_MODEL)), _full((3, D_CONV))] + [ANY] * n_ag,
        scratch_shapes=[pltpu.VMEM((N_CHIPS, D_MODEL, IN_SHARD), BF16), pltpu.VMEM((D_MODEL, D_MODEL), BF16),
                        pltpu.VMEM((N_CHIPS, small_rows, D_MODEL // N_CHIPS), F32),
                        pltpu.VMEM((HALO + tm, D_CONV), F32), pltpu.VMEM((HALO + tm, D_POOL), F32),
                        pltpu.SemaphoreType.DMA((N_MIX_SHARDS * N_CHIPS,))] + _AllGather.scratch(n_ag),
        compiler_params=_cparams(2),
    )(x3, g1, g2, poolw, pscale, *shards)
    return outs[:10], outs[10:]


def _ffn_chunks():
    out, r0 = [], 0
    while r0 < D_FF:
        out.append((r0, min(FF_CHUNK, D_FF - r0)))
        r0 += FF_CHUNK
    return out


def _ffn_fwd_bwd(h1, target, g3, g4, gathered, shards):
    n_rows = h1.shape[0]
    tm = min(TM_FFN, n_rows)
    chunks = _ffn_chunks()

    def body(h1_ref, t_ref, g3_ref, g4_ref, wg_all, wu_all, wd_all, wg_s, wu_s, wd_s,
             dh1_ref, f_ref, dd_ref, ds_ref, du_ref, gg_ref, loss_ref, dg3_ref, dg4_ref,
             wg_v, wu_v, wd_v, s_sc, u_sc, sems):
        @pl.when(pl.program_id(0) == 0)
        def _():
            _load_gathered([wg_all, wu_all, wd_all], [wg_s, wu_s, wd_s],
                           [functools.partial(lambda v, j: v.at[pl.ds(j * FF_SHARD, FF_SHARD), :], v)
                            for v in (wg_v, wu_v, wd_v)], sems)
            loss_ref[...] = jnp.zeros_like(loss_ref)
            dg3_ref[...] = jnp.zeros_like(dg3_ref)
            dg4_ref[...] = jnp.zeros_like(dg4_ref)

        h1v = h1_ref[...]
        r3 = _rstd(h1v)
        hh = h1v * r3
        g3v, g4v = g3_ref[...], g4_ref[...]
        f = (hh * g3v).astype(BF16)
        f_ref[...] = f
        d = jnp.zeros((tm, D_MODEL), F32)
        for r0, sz in chunks:
            s = _dot_nt(f, wg_v[r0:r0 + sz, :])
            u = _dot_nt(f, wu_v[r0:r0 + sz, :])
            s_sc[:, r0:r0 + sz] = s
            u_sc[:, r0:r0 + sz] = u
            gc = (s * _sigmoid(s) * u).astype(BF16)
            gg_ref[:, r0:r0 + sz] = gc
            d = d + _dot(gc, wd_v[r0:r0 + sz, :])
        r4 = _rstd(d)
        dh = d * r4
        err = (h1v + dh * g4v) - t_ref[...]
        loss_ref[...] += _rows8(err * err)
        dy = err * (1.0 / D_MODEL)
        dg4_ref[...] += _rows8(dy * dh)
        ddb = _rms_bwd(dy, dh, r4, g4v).astype(BF16)
        dd_ref[...] = ddb
        df = jnp.zeros((tm, D_MODEL), F32)
        for r0, sz in chunks:
            dgg = _dot_nt(ddb, wd_v[r0:r0 + sz, :])
            s = s_sc[:, r0:r0 + sz]
            u = u_sc[:, r0:r0 + sz]
            sig = _sigmoid(s)
            dsc = (dgg * u * (sig * (1.0 + s * (1.0 - sig)))).astype(BF16)
            duc = (dgg * (s * sig)).astype(BF16)
            ds_ref[:, r0:r0 + sz] = dsc
            du_ref[:, r0:r0 + sz] = duc
            df = df + _dot(dsc, wg_v[r0:r0 + sz, :]) + _dot(duc, wu_v[r0:r0 + sz, :])
        dg3_ref[...] += _rows8(df * hh)
        dh1_ref[...] = dy + _rms_bwd(df, hh, r3, g3v)

    row = pl.BlockSpec((tm, D_MODEL), lambda i: (i, 0))
    ffrow = pl.BlockSpec((tm, D_FF), lambda i: (i, 0))
    acc = _full((8, D_MODEL))
    act_bf = jax.ShapeDtypeStruct((n_rows, D_MODEL), BF16)
    ff_bf = jax.ShapeDtypeStruct((n_rows, D_FF), BF16)
    acc_shape = jax.ShapeDtypeStruct((8, D_MODEL), F32)
    w_vmem = pltpu.VMEM((D_FF, D_MODEL), BF16)
    return pl.pallas_call(
        body, name="ffn_fwd_bwd", grid=(n_rows // tm,),
        out_shape=[jax.ShapeDtypeStruct((n_rows, D_MODEL), F32), act_bf, act_bf, ff_bf, ff_bf, ff_bf,
                   acc_shape, acc_shape, acc_shape],
        in_specs=[row, row, _full((1, D_MODEL)), _full((1, D_MODEL))] + [ANY] * 6,
        out_specs=[row, row, row, ffrow, ffrow, ffrow, acc, acc, acc],
        scratch_shapes=[w_vmem, w_vmem, w_vmem, pltpu.VMEM((tm, D_FF), F32), pltpu.VMEM((tm, D_FF), F32),
                        pltpu.SemaphoreType.DMA((3 * N_CHIPS,))],
        compiler_params=_cparams(1),
    )(h1, target, g3, g4, *gathered, *shards)


def _ffn_weight_grads(name, acts, other, exchanged):
    n_rows = other.shape[0]
    n_a, n_ex = len(acts), len(exchanged)
    n_c = n_a
    tk = min(TK_DW, n_rows)
    n_k = n_rows // tk
    half = D_FF // n_c

    def body(other_ref, *rest):
        act_refs = rest[:n_a]
        out_refs = rest[n_a + n_ex:2 * n_a + n_ex]
        c, k = pl.program_id(0), pl.program_id(1)
        if n_ex:
            ex = _ExchangeHalves(rest[n_a:n_a + n_ex], rest[2 * n_a + n_ex:2 * n_a + 2 * n_ex], *rest[-2:])

            @pl.when((c == 0) & (k == 0))
            def _():
                ex.start()

        @pl.when(k == 0)
        def _():
            for o in out_refs:
                o[...] = jnp.zeros_like(o)

        ov = other_ref[...]
        for a, o in zip(act_refs, out_refs):
            o[...] += _dot_tn(a[...], ov)

        if n_ex:
            @pl.when((c == n_c - 1) & (k == n_k - 1))
            def _():
                ex.finish()

    row = pl.BlockSpec((tk, D_MODEL), lambda c, k: (k, 0))
    ffrow = pl.BlockSpec((tk, half), lambda c, k: (k, c))
    out = pl.BlockSpec((half, D_MODEL), lambda c, k: (c, 0))
    outs = pl.pallas_call(
        body, name=name, grid=(n_c, n_k),
        out_shape=[jax.ShapeDtypeStruct((D_FF, D_MODEL), F32)] * n_a + _ExchangeHalves.out_shape(exchanged),
        in_specs=[row] + [ffrow] * n_a + [ANY] * n_ex, out_specs=[out] * n_a + [ANY] * n_ex,
        scratch_shapes=_ExchangeHalves.scratch(n_ex) if n_ex else [],
        compiler_params=_cparams(2),
    )(other, *acts, *exchanged)
    return outs[:n_a], outs[n_a:]


def _mixer_bwd(dh1, m3, z3, conv2, pooled2, x3, zmeta, meta_full, g1, g2, convw, poolw, pscale, gathered, shards,
               after):
    n_seq, seq, _ = x3.shape
    tm = min(TM_MIX_BWD, seq)
    sub = min(SUB_MIX_BWD, tm)
    n_t = seq // tm
    n_out = 13

    def body(dh1_ref, m_ref, z_ref, conv_ref, pooled_ref, x_ref, zm_ref, meta_ref, g1_ref, g2_ref, cw_ref, pw_ref,
             ps_ref, after_ref, win_all, wout_all, win_s, wout_s, *rest):
        (dx_ref, dz_ref, dm_ref, dg1_ref, dg2_ref, dsc_ref, dcw_ref, dpw_ref, dzm_ref, dmeta_ref, dg1m_ref, am_ref,
         dzmb_ref) = rest[:n_out]
        win_v, wout_v, dcb, dqb, mcb, mqb, load_sems = rest[n_out:]
        s, i = pl.program_id(0), pl.program_id(1)
        tr = n_t - 1 - i

        @pl.when((s == 0) & (i == 0))
        def _():
            _load_gathered([win_all, wout_all], [win_s, wout_s],
                           [lambda j: win_v.at[j], lambda j: wout_v.at[pl.ds(j * OUT_SHARD, OUT_SHARD), :]], load_sems)
            for ref in (dg1_ref, dg2_ref, dsc_ref, dcw_ref, dpw_ref, dzm_ref):
                ref[...] = jnp.zeros_like(ref)

        @pl.when(i == 0)
        def _():
            dcb[tm:tm + HALO, :] = jnp.zeros((HALO, D_CONV), F32)
            dqb[tm:tm + HALO, :] = jnp.zeros((HALO, D_POOL), F32)

        @pl.when(i > 0)
        def _():
            dcb[tm:tm + HALO, :] = dcb[0:HALO, :]
            dqb[tm:tm + HALO, :] = dqb[0:HALO, :]

        g1v, g2v = g1_ref[...], g2_ref[...]
        cw = cw_ref[...]

        for r0 in range(tm - sub, -1, -sub):
            rows = slice(r0, r0 + sub)
            dh1v = dh1_ref[0, rows, :]
            mv = m_ref[0, rows, :]
            r2 = _rstd(mv)
            mh = mv * r2
            dg2_ref[...] += _rows8(dh1v * mh)
            dmb = _rms_bwd(dh1v, mh, r2, g2v).astype(BF16)
            dm_ref[rows, :] = dmb
            dyc = _dot_nt(dmb, wout_v[...])
            dyconv = dyc[:, 0:D_CONV]

            for g in range(N_POOL_GROUPS):
                pooled = pooled_ref[rows, _gcols(g)]
                mixed = _dot(pooled, pw_ref[g])
                scale = ps_ref[:, _gcols(g)]
                dyp = dyc[:, D_CONV + g * POOL_GROUP:D_CONV + (g + 1) * POOL_GROUP]
                dsc_ref[:, _gcols(g)] += _rows8(dyp * mixed)
                dmix = (dyp * scale).astype(BF16)
                dpw_ref[g] += _dot_tn(pooled, dmix)
                dqb[rows, _gcols(g)] = _dot_nt(dmix, pw_ref[g])

            zb = z_ref[0, rows, 0:IN_SHARD]
            zc = z_ref[0, rows, IN_SHARD:2 * IN_SHARD]
            zv = z_ref[0, rows, 2 * IN_SHARD:3 * IN_SHARD]
            dconv = dyconv * zb
            dcb[rows, :] = dconv
            d1 = dcb[r0 + 1:r0 + 1 + sub, :]
            d2 = dcb[r0 + 2:r0 + 2 + sub, :]
            dcv = cw[2:3] * dconv + cw[1:2] * d1 + cw[0:1] * d2
            cv = zc * zv
            dcw_ref[0:8, :] += _rows8(cv * d2)
            dcw_ref[8:16, :] += _rows8(cv * d1)
            dcw_ref[16:24, :] += _rows8(cv * dconv)
            dzs = [(dyconv * conv_ref[rows, :]).astype(BF16), (dcv * zv).astype(BF16), (dcv * zc).astype(BF16),
                   jnp.concatenate([_pool_bwd(dqb, g, r0, sub) for g in range(N_POOL_GROUPS)], axis=1).astype(BF16)]
            da = jnp.zeros((sub, D_MODEL), F32)
            for j in range(N_CHIPS):
                dz_ref[j, rows, :] = dzs[j]
                da = da + _dot_nt(dzs[j], win_v[j])
            xt = x_ref[0, rows, :]
            r1 = _rstd(xt)
            xh = xt * r1
            dg1_ref[...] += _rows8(da * xh)
            dx_ref[0, rows, :] = dh1v + _rms_bwd(da, xh, r1, g1v)

        @pl.when(tr == 0)
        def _():
            mcb[0:HALO, :] = jnp.zeros((HALO, D_CONV), F32)
            mqb[0:HALO, :] = jnp.zeros((HALO, D_POOL), F32)
            mcb[HALO:2 * HALO, :] = dcb[0:HALO, :]
            mqb[HALO:2 * HALO, :] = dqb[0:HALO, :]
            m1 = mcb[1:1 + HALO, :]
            m2 = mcb[2:2 + HALO, :]
            zc_m = zm_ref[:, IN_SHARD:2 * IN_SHARD]
            zv_m = zm_ref[:, 2 * IN_SHARD:3 * IN_SHARD]
            cv_m = zc_m * zv_m
            dcw_ref[0:8, :] += _rows8(cv_m * m2)
            dcw_ref[8:16, :] += _rows8(cv_m * m1)
            dcv_m = cw[1:2] * m1 + cw[0:1] * m2
            dzm_ref[:, IN_SHARD:2 * IN_SHARD] += dcv_m * zv_m
            dzm_ref[:, 2 * IN_SHARD:3 * IN_SHARD] += dcv_m * zc_m
            dzm_ref[:, 3 * IN_SHARD:4 * IN_SHARD] += jnp.concatenate(
                [_pool_bwd(mqb, g, 0, HALO) for g in range(N_POOL_GROUPS)], axis=1)

        @pl.when((s == n_seq - 1) & (i == n_t - 1))
        def _():
            xm = meta_ref[...]
            rm = _rstd(xm)
            xmh = xm * rm
            am_ref[...] = (xmh * g1v).astype(BF16)
            da_m = jnp.zeros((N_META, D_MODEL), F32)
            for j in range(N_CHIPS):
                dzj = dzm_ref[:, j * IN_SHARD:(j + 1) * IN_SHARD].astype(BF16)
                dzmb_ref[j] = dzj
                da_m = da_m + _dot_nt(dzj, win_v[j])
            dg1m_ref[...] = _rows8(da_m * xmh)
            dmeta_ref[...] = _rms_bwd(da_m, xmh, rm, g1v)

    row3 = lambda c: pl.BlockSpec((1, tm, c), lambda s, i: (s, n_t - 1 - i, 0))
    row2 = lambda c: pl.BlockSpec((tm, c), lambda s, i: (s * n_t + n_t - 1 - i, 0))
    n_rows = n_seq * seq
    outs = pl.pallas_call(
        body, name="mixer_bwd", grid=(n_seq, n_t),
        out_shape=[jax.ShapeDtypeStruct((n_seq, seq, D_MODEL), F32),
                   jax.ShapeDtypeStruct((N_CHIPS, n_rows, IN_SHARD), BF16), jax.ShapeDtypeStruct((n_rows, D_MODEL), BF16),
                   jax.ShapeDtypeStruct((8, D_MODEL), F32), jax.ShapeDtypeStruct((8, D_MODEL), F32),
                   jax.ShapeDtypeStruct((8, D_POOL), F32), jax.ShapeDtypeStruct((24, D_CONV), F32),
                   jax.ShapeDtypeStruct((N_POOL_GROUPS, POOL_GROUP, POOL_GROUP), F32),
                   jax.ShapeDtypeStruct((N_META, D_IN_PROJ), F32),
                   jax.ShapeDtypeStruct((N_META, D_MODEL), F32), jax.ShapeDtypeStruct((8, D_MODEL), F32),
                   jax.ShapeDtypeStruct((N_META, D_MODEL), BF16),
                   jax.ShapeDtypeStruct((N_CHIPS, N_META, IN_SHARD), BF16)],
        in_specs=[row3(D_MODEL), row3(D_MODEL), row3(D_Z), row2(D_CONV), row2(D_POOL), row3(D_MODEL),
                  _full((N_META, D_IN_PROJ)), _full((N_META, D_MODEL)), _full((1, D_MODEL)), _full((1, D_MODEL)),
                  _full((3, D_CONV)), _full((N_POOL_GROUPS, POOL_GROUP, POOL_GROUP)), _full((1, D_POOL)),
                  _full((8, 128))] + [ANY] * 4,
        out_specs=[row3(D_MODEL), pl.BlockSpec((N_CHIPS, tm, IN_SHARD), lambda s, i: (0, s * n_t + n_t - 1 - i, 0)),
                   row2(D_MODEL),
                   _full((8, D_MODEL)), _full((8, D_MODEL)), _full((8, D_POOL)), _full((24, D_CONV)),
                   _full((N_POOL_GROUPS, POOL_GROUP, POOL_GROUP)), _full((N_META, D_IN_PROJ)),
                   _full((N_META, D_MODEL)), _full((8, D_MODEL)), _full((N_META, D_MODEL)),
                   _full((N_CHIPS, N_META, IN_SHARD))],
        scratch_shapes=[pltpu.VMEM((N_CHIPS, D_MODEL, IN_SHARD), BF16), pltpu.VMEM((D_MODEL, D_MODEL), BF16),
                        pltpu.VMEM((tm + HALO, D_CONV), F32), pltpu.VMEM((tm + HALO, D_POOL), F32),
                        pltpu.VMEM((2 * HALO, D_CONV), F32), pltpu.VMEM((2 * HALO, D_POOL), F32),
                        pltpu.SemaphoreType.DMA((2 * N_CHIPS,))],
        compiler_params=_cparams(2),
    )(dh1, m3, z3, conv2, pooled2, x3, zmeta, meta_full, g1, g2, convw, poolw, pscale, after, *gathered, *shards)
    return outs


def _mixer_weight_grads(a, dz, ycat, dm, a_meta, dz_meta, ffn_sums, small):
    n_rows = a.shape[0]
    tk = min(TK_DW, n_rows)
    n_k = n_rows // tk
    n_sc, n_sm = len(ffn_sums), _AllReduceSmall.N_IN

    def body(a_ref, dz_ref, yc_ref, dm_ref, am_ref, dzm_ref, *rest):
        ins, outs, scratch = rest[:n_sc + n_sm], rest[n_sc + n_sm:2 * n_sc + n_sm + 5], rest[2 * n_sc + n_sm + 5:]
        dwin_ref, dwout_ref = outs[:2]
        scatter = _ScatterToChips(ins[:n_sc], outs[2:2 + n_sc], *scratch[:2])
        reduce_small = _AllReduceSmall(ins[n_sc:], outs[2 + n_sc:], scratch[2:])
        k = pl.program_id(0)

        @pl.when(k == 0)
        def _():
            scatter.start()
            reduce_small.pack_and_send()
            am_t = am_ref[...].T
            for j in range(N_CHIPS):
                dwin_ref[j] = _dot(am_t, dzm_ref[j])
            dwout_ref[...] = jnp.zeros_like(dwout_ref)

        for st in range(2):
            @pl.when(k == ((st + 1) * n_k) // 3)
            def _():
                reduce_small.combine(st)

        a_t = a_ref[...].T
        for j in range(N_CHIPS):
            dwin_ref[j] += _dot(a_t, dz_ref[j])
        dwout_ref[...] += _dot_tn(yc_ref[...], dm_ref[...])

        @pl.when(k == n_k - 1)
        def _():
            reduce_small.combine(2)
            scatter.finish()

    row = pl.BlockSpec((tk, D_MODEL), lambda k: (k, 0))
    outs = pl.pallas_call(
        body, name="mixer_weight_grads", grid=(n_k,),
        out_shape=[jax.ShapeDtypeStruct((N_CHIPS, D_MODEL, IN_SHARD), F32),
                   jax.ShapeDtypeStruct((D_MODEL, D_MODEL), F32)] + _ScatterToChips.out_shape(ffn_sums)
        + _AllReduceSmall.out_shape(),
        in_specs=[row, pl.BlockSpec((N_CHIPS, tk, IN_SHARD), lambda k: (0, k, 0)), row, row,
                  _full((N_META, D_MODEL)), _full((N_CHIPS, N_META, IN_SHARD))] + [ANY] * n_sc
        + [_full(s.shape) for s in small],
        out_specs=[_full((N_CHIPS, D_MODEL, IN_SHARD)), _full((D_MODEL, D_MODEL))] + [ANY] * n_sc
        + [_full(s) for s in _AllReduceSmall.SHAPES],
        scratch_shapes=_ScatterToChips.scratch(n_sc) + _AllReduceSmall.scratch(),
        compiler_params=_cparams(1),
    )(a, dz, ycat, dm, a_meta, dz_meta, *ffn_sums, *small)
    return ([outs[0], outs[1].reshape(N_CHIPS, OUT_SHARD, D_MODEL)], outs[2:2 + n_sc], outs[2 + n_sc:])


def kernel(x, meta_tokens, norm_mix_pre, w_in, conv_w, pool_w, pool_scale, w_out, norm_mix_post, norm_ffn_pre, w_gate, w_up, w_down, norm_ffn_post, loss_target, m_meta_tokens, m_norm_mix_pre, m_w_in, m_conv_w, m_pool_w, m_pool_scale, m_w_out, m_norm_mix_post, m_norm_ffn_pre, m_w_gate, m_w_up, m_w_down, m_norm_ffn_post, v_meta_tokens, v_norm_mix_pre, v_w_in, v_conv_w, v_pool_w, v_pool_scale, v_w_out, v_norm_mix_post, v_norm_ffn_pre, v_w_gate, v_w_up, v_w_down, v_norm_ffn_post):
    n_seq, seq, _ = x.shape
    n_rows = n_seq * seq
    chip = 2 * lax.axis_index("x") + lax.axis_index("y")
    meta_cols = D_MODEL // N_CHIPS
    conv_cols = D_CONV // N_CHIPS

    small = jnp.zeros((2 * HALO, meta_cols), F32)
    small = small.at[0:N_META, :].set(meta_tokens).at[N_META:N_META + 3, 0:conv_cols].set(conv_w[0])
    poolw_bf = pool_w[0].astype(BF16)
    pscale = pool_scale
    g1, g2, g3, g4 = norm_mix_pre, norm_mix_post, norm_ffn_pre, norm_ffn_post
    place = jnp.stack([chip, lax.axis_index("c")]).astype(jnp.int32)

    mix_shards = [w_in[0].astype(BF16), w_out[0].astype(BF16)]
    ffn_shards = [w_gate[0].T.astype(BF16), w_up[0].T.astype(BF16), w_down[0].astype(BF16)]
    ((z3, m3, h1, a_bf, conv2, pooled2, yc_bf, zmeta, meta_full, conv_full),
     (win_all, wout_all, _, *ffn_gathered)) = _mixer_fwd(x, g1, g2, poolw_bf, pscale, mix_shards + [small] + ffn_shards)
    dh1, f_bf, dd_bf, ds_bf, du_bf, gg_bf, lossp, dg3p, dg4p = _ffn_fwd_bwd(
        h1.reshape(n_rows, D_MODEL), loss_target.reshape(n_rows, D_MODEL), g3, g4, ffn_gathered, ffn_shards)
    as_shards = lambda g: g.reshape(N_CHIPS, FF_SHARD, D_MODEL)
    (dwg_t, dwu_t), _ = _ffn_weight_grads("ffn_weight_grads_gate_up", [ds_bf, du_bf], f_bf, [])
    dwg_t, dwu_t = as_shards(dwg_t), as_shards(dwu_t)
    (dwd,), (dwg_recv, dwu_recv) = _ffn_weight_grads("ffn_weight_grads_down", [gg_bf], dd_bf, [dwg_t, dwu_t])
    dwd = as_shards(dwd)
    behind_bwd = _SplitComm("grad_comm_behind_mixer_bwd", [dwd],
                            _add_pairs_multi([dwg_t, dwu_t], [dwg_recv, dwu_recv], place))
    (grad_x, dz_bf, dm_bf, dg1p, dg2p, dscp, dcwp, dpw, _, dmeta, dg1m, a_meta, dz_meta) = _mixer_bwd(
        dh1.reshape(n_seq, seq, D_MODEL), m3, z3, conv2, pooled2, x, zmeta, meta_full, g1, g2, conv_full, poolw_bf,
        pscale, [win_all, wout_all], mix_shards, behind_bwd.start())
    (dwd_recv,), (dwg_rbuf, dwu_rbuf) = behind_bwd.wait(dg2p)
    (dwd,) = behind_bwd.exchanged
    mix_grads, (dwd_rbuf,), (a_red, b_red, c_red) = _mixer_weight_grads(
        a_bf, dz_bf, yc_bf, dm_bf, a_meta, dz_meta, [_add_pairs(dwd, dwd_recv, place)],
        [dg1p, dg1m, dg2p, dg3p, dg4p, lossp, dmeta, dscp, dcwp, dpw.reshape(SMALL_C_ROWS, POOL_GROUP)])

    behind_sums = _SplitComm("grad_comm_behind_ffn_sums", mix_grads, [])
    ffn_red = _add_chips([dwg_t, dwu_t, dwd], [dwg_recv, dwu_recv, dwd_recv], [dwg_rbuf, dwu_rbuf, dwd_rbuf],
                         place, after=behind_sums.start(), name="grad_add_chips_ffn")
    mix_recvs, _ = behind_sums.wait(ffn_red[0])
    mix_grads = behind_sums.exchanged
    behind_tail = _SplitComm("grad_comm_behind_ffn_tail", [], _add_pairs_multi(mix_grads, mix_recvs, place))
    as_full = lambda r: r.reshape(2 * r.shape[1], r.shape[2])
    g_wg_t, g_wu_t, g_wd = [as_full(r) for r in _gather_halves(list(ffn_red), "grad_gather_halves_ffn",
                                                                 after=behind_tail.start())]
    ffn_out = _adamw_big([(w_gate[0].T, g_wg_t, m_w_gate[0].T, v_w_gate[0].T),
                          (w_up[0].T, g_wu_t, m_w_up[0].T, v_w_up[0].T), (w_down[0], g_wd, m_w_down[0], v_w_down[0])])
    _, mix_rbufs = behind_tail.wait(ffn_out[2][0])
    mix_red = _add_chips(mix_grads, mix_recvs, mix_rbufs, place)
    g_win, g_wout = [as_full(r) for r in _gather_halves(list(mix_red), "grad_gather_halves_mixer")]

    loss = a_red[4, 0]
    g_g1, g_g2, g_g3, g_g4 = a_red[0:1], a_red[1:2], a_red[2:3], a_red[3:4]
    g_meta = lax.dynamic_slice(a_red, (8, chip * meta_cols), (N_META, meta_cols))
    g_pscale = b_red[0:1]
    g_conv = lax.dynamic_slice(b_red, (1, chip * conv_cols), (3, conv_cols))
    g_poolw = c_red

    big_out = (_adamw_big([(w_in[0], g_win, m_w_in[0], v_w_in[0])])
               + _adamw_big([(w_out[0], g_wout, m_w_out[0], v_w_out[0])]) + ffn_out)
    big_out[2] = [o.T for o in big_out[2]]
    big_out[3] = [o.T for o in big_out[3]]
    g_wg, g_wu = g_wg_t.T, g_wu_t.T
    small_groups = [
        (meta_tokens, g_meta, m_meta_tokens, v_meta_tokens),
        (g1, g_g1, m_norm_mix_pre, v_norm_mix_pre),
        (conv_w[0], g_conv, m_conv_w[0], v_conv_w[0]),
        (pool_w.reshape(SMALL_C_ROWS, POOL_GROUP), g_poolw, m_pool_w.reshape(SMALL_C_ROWS, POOL_GROUP),
         v_pool_w.reshape(SMALL_C_ROWS, POOL_GROUP)),
        (pool_scale, g_pscale, m_pool_scale, v_pool_scale),
        (g2, g_g2, m_norm_mix_post, v_norm_mix_post),
        (g3, g_g3, m_norm_ffn_pre, v_norm_ffn_pre),
        (g4, g_g4, m_norm_ffn_post, v_norm_ffn_post),
    ]
    small_out = _adamw_small(small_groups)

    grads_out = [g_meta, g_g1, g_win[None], g_conv[None], g_poolw.reshape(pool_w.shape), g_pscale, g_wout[None],
                 g_g2, g_g3, g_wg[None], g_wu[None], g_wd[None], g_g4]
    s_meta, s_g1, s_conv, s_poolw, s_pscale, s_g2, s_g3, s_g4 = small_out
    b_win, b_wout, b_wg, b_wu, b_wd = big_out

    def leaf(k):
        return [s_meta[k], s_g1[k], b_win[k][None], s_conv[k][None], s_poolw[k].reshape(pool_w.shape), s_pscale[k],
                b_wout[k][None], s_g2[k], s_g3[k], b_wg[k][None], b_wu[k][None], b_wd[k][None], s_g4[k]]

    return (loss, grad_x, *grads_out, *leaf(0), *leaf(1), *leaf(2))
```

```python
import functools

import jax
import jax.numpy as jnp
from jax import lax
from jax.experimental import pallas as pl
from jax.experimental.pallas import tpu as pltpu

F32 = jnp.float32
BF16 = jnp.bfloat16
MESH = pl.DeviceIdType.MESH

D_MODEL = 1024
D_CONV = 512
D_POOL = 512
POOL_GROUP = 128
N_POOL_GROUPS = 4
D_IN_PROJ = 2048
D_FF = 2816
N_CHIPS = 4
FF_SHARD = D_FF // N_CHIPS
IN_SHARD = D_IN_PROJ // N_CHIPS
OUT_SHARD = D_MODEL // N_CHIPS
D_Z = 3 * IN_SHARD
N_META = 16
HALO = 16
RMS_EPS = 1e-6

ADAM_LR = 0.001
ADAM_B1 = 0.9
ADAM_B2 = 0.999
ADAM_EPS = 1e-08
ADAM_WD = 0.01
ADAM_STEP = 10

TM_MIX_FWD = 512
TM_MIX_BWD = 512
SUB_MIX_BWD = 512
TM_FFN = 256
TK_DW = 1024
FF_CHUNK = 1024
VMEM_LIMIT = 56 * 1024 * 1024


def _cparams(n_grid):
    return pltpu.CompilerParams(dimension_semantics=("arbitrary",) * n_grid, vmem_limit_bytes=VMEM_LIMIT)


def _dot(a, b):
    return jnp.dot(a, b, preferred_element_type=F32)


def _dot_nt(a, b):
    return lax.dot_general(a, b, (((1,), (1,)), ((), ())), preferred_element_type=F32)


def _dot_tn(a, b):
    return lax.dot_general(a, b, (((0,), (0,)), ((), ())), preferred_element_type=F32)


def _rows8(v):
    r, c = v.shape
    return v.reshape(r // 8, 8, c).sum(axis=0)


def _rstd(v):
    return lax.rsqrt(jnp.mean(v * v, axis=-1, keepdims=True) + RMS_EPS)


def _rms_bwd(dy, xhat, rstd, gain):
    dyg = dy * gain
    return rstd * (dyg - xhat * jnp.mean(dyg * xhat, axis=-1, keepdims=True))


def _sigmoid(v):
    return 1.0 / (1.0 + jnp.exp(-v))


def _gcols(g):
    return slice(g * POOL_GROUP, (g + 1) * POOL_GROUP)


def _window_sum(e, g, ahead):
    n = e.shape[0]
    w = e
    for level in range(g + 1):
        shift = 1 << level
        w = w + pltpu.roll(w, (n - shift) if ahead else shift, 0)
    return w


def _pool_fwd(pb, g, n):
    e = pb[0:HALO + n, _gcols(g)]
    return _window_sum(e, g, False)[HALO:, :] * (1.0 / (2 << g)) - e[HALO:, :]


def _pool_bwd(qb, g, r0, n):
    e = qb[r0:r0 + n + HALO, _gcols(g)]
    return _window_sum(e, g, True)[0:n, :] * (1.0 / (2 << g)) - e[0:n, :]


def _full(shape):
    nd = len(shape)
    return pl.BlockSpec(shape, lambda *_: (0,) * nd)


ANY = pl.BlockSpec(memory_space=pl.ANY)


def _mesh_pos():
    x, y, c = lax.axis_index("x"), lax.axis_index("y"), lax.axis_index("c")
    chips = [(1 - x, y), (x, 1 - y), (1 - x, 1 - y)]
    return x, y, c, chips


def _half(ref, h):
    hr = ref.shape[0] // 2
    return ref.at[pl.ds(h * hr, hr), :]


class _AllGather:
    PER_ARRAY = 9

    def __init__(self, ins, outs, send_sems, recv_sems):
        self.ins, self.outs, self.send_sems, self.recv_sems = ins, outs, send_sems, recv_sems
        self.n = len(ins)

    @classmethod
    def scratch(cls, n):
        return [pltpu.SemaphoreType.DMA((cls.PER_ARRAY * n,)), pltpu.SemaphoreType.DMA((cls.PER_ARRAY * n,))]

    @staticmethod
    def out_shape(shards):
        return [jax.ShapeDtypeStruct((N_CHIPS,) + s.shape, s.dtype) for s in shards]

    def _copy(self, a, k, src, dst, to):
        i = self.PER_ARRAY * a + k
        return pltpu.make_async_remote_copy(src_ref=src, dst_ref=dst, send_sem=self.send_sems.at[i],
                                            recv_sem=self.recv_sems.at[i], device_id=to, device_id_type=MESH)

    def _piece(self, a, chip, piece, h=None):
        h = lax.axis_index("c") if h is None else h
        rows = self.ins[a].shape[0] // 4
        return self.outs[a].at[chip].at[pl.ds((2 * h + piece) * rows, rows), :]

    def _own(self, a, k):
        x, y, c, chips = _mesh_pos()
        piece = (1, 0, 0, 1)[k]
        rows = self.ins[a].shape[0] // 4
        src = self.ins[a].at[pl.ds((2 * c + piece) * rows, rows), :]
        return self._copy(a, k, src, self._piece(a, 2 * x + y, piece), (*chips[k // 2], c))

    def _relay(self, a, k):
        x, y, c, chips = _mesh_pos()
        source, to, piece = (chips[1], chips[0], 0) if k == 4 else (chips[0], chips[1], 1)
        rows = self._piece(a, 2 * source[0] + source[1], piece)
        return self._copy(a, k, rows, rows, (*to, c))

    def _sibling(self, a, k, h):
        x, y, c, chips = _mesh_pos()
        chip = chips[k - 6]
        slot = _half(self.outs[a].at[2 * chip[0] + chip[1]], h)
        return self._copy(a, k, slot, slot, (x, y, 1 - c))

    def start(self, arrays=None):
        for a in (range(self.n) if arrays is None else arrays):
            for k in range(4):
                self._own(a, k).start()

    def relay(self, a):
        self._own(a, 2).wait_recv()
        self._relay(a, 4).start()
        self._own(a, 0).wait_recv()
        self._relay(a, 5).start()

    def forward(self, a):
        c = lax.axis_index("c")
        self._own(a, 1).wait_recv()
        self._sibling(a, 6, c).start()
        self._own(a, 3).wait_recv()
        self._sibling(a, 7, c).start()
        self._relay(a, 4).wait_recv()
        self._relay(a, 5).wait_recv()
        self._sibling(a, 8, c).start()

    def finish(self, arrays=None):
        c = lax.axis_index("c")
        arrays = range(self.n) if arrays is None else arrays
        for a in arrays:
            for k in range(6, 9):
                self._sibling(a, k, 1 - c).wait_recv()
        for a in arrays:
            for k in range(4):
                self._own(a, k).wait_send()
            for k in range(4, 6):
                self._relay(a, k).wait_send()
            for k in range(6, 9):
                self._sibling(a, k, c).wait_send()


class _ExchangeHalves:
    def __init__(self, ins, recvs, send_sems, recv_sems):
        self.ins, self.recvs, self.send_sems, self.recv_sems = ins, recvs, send_sems, recv_sems

    @staticmethod
    def scratch(n):
        return [pltpu.SemaphoreType.DMA((n,)), pltpu.SemaphoreType.DMA((n,))]

    @staticmethod
    def out_shape(grads):
        return [jax.ShapeDtypeStruct((g.shape[0], g.shape[1] // 2, g.shape[2]), g.dtype) for g in grads]

    def _copies(self):
        x, y, c, _ = _mesh_pos()
        out = []
        for a, (src, dst) in enumerate(zip(self.ins, self.recvs)):
            hr = src.shape[1] // 2
            out.append(pltpu.make_async_remote_copy(
                src_ref=src.at[:, pl.ds((1 - c) * hr, hr), :], dst_ref=dst, send_sem=self.send_sems.at[a],
                recv_sem=self.recv_sems.at[a], device_id=(x, y, 1 - c), device_id_type=MESH))
        return out

    def start(self):
        for cp in self._copies():
            cp.start()

    def finish(self):
        for cp in self._copies():
            cp.wait()


def _exchange_halves(grads):
    n = len(grads)

    def body(*refs):
        ex = _ExchangeHalves(refs[:n], refs[n:2 * n], *refs[2 * n:])
        ex.start()
        ex.finish()

    return pl.pallas_call(
        body, name="grad_exchange_halves", out_shape=_ExchangeHalves.out_shape(grads),
        in_specs=[ANY] * n, out_specs=[ANY] * n, scratch_shapes=_ExchangeHalves.scratch(n),
    )(*grads)


class _ScatterToChips:
    def __init__(self, ins, rbufs, send_sems, recv_sems):
        self.ins, self.rbufs, self.send_sems, self.recv_sems = ins, rbufs, send_sems, recv_sems

    @staticmethod
    def scratch(n):
        return [pltpu.SemaphoreType.DMA((3 * n,)), pltpu.SemaphoreType.DMA((3 * n,))]

    @staticmethod
    def out_shape(sums):
        return [jax.ShapeDtypeStruct((3,) + s.shape[1:], BF16) for s in sums]

    def _copies(self):
        x, y, c, chips = _mesh_pos()
        out = []
        for a, (src, dst) in enumerate(zip(self.ins, self.rbufs)):
            for k, chip in enumerate(chips):
                out.append(pltpu.make_async_remote_copy(
                    src_ref=src.at[2 * chip[0] + chip[1]], dst_ref=dst.at[k], send_sem=self.send_sems.at[3 * a + k],
                    recv_sem=self.recv_sems.at[3 * a + k], device_id=(*chip, c), device_id_type=MESH))
        return out

    def start(self):
        for cp in self._copies():
            cp.start()

    def finish(self):
        for cp in self._copies():
            cp.wait()


HBM = pl.BlockSpec(memory_space=pltpu.HBM)
SEM = pl.BlockSpec(memory_space=pltpu.SEMAPHORE)


class _SplitComm:
    def __init__(self, name, exchanged, scattered):
        self.name, self.n_ex, self.n_sc = name, len(exchanged), len(scattered)
        self.n_copies = self.n_ex + 3 * self.n_sc
        zones = ([lax.empty((g.shape[0], g.shape[1] // 2, g.shape[2]), g.dtype) for g in exchanged]
                 + [lax.empty((3,) + s.shape[1:], s.dtype) for s in scattered])
        self.buffers = [pltpu.with_memory_space_constraint(v, pltpu.HBM)
                        for v in list(exchanged) + list(scattered) + zones]

    def _copies(self, bufs, send_sems, recv_sems):
        x, y, c, chips = _mesh_pos()
        n_src = self.n_ex + self.n_sc
        out = []
        for a in range(self.n_ex):
            hr = bufs[a].shape[1] // 2
            out.append(pltpu.make_async_remote_copy(
                src_ref=bufs[a].at[:, pl.ds((1 - c) * hr, hr), :], dst_ref=bufs[n_src + a], send_sem=send_sems[a],
                recv_sem=recv_sems[a], device_id=(x, y, 1 - c), device_id_type=MESH))
        for a in range(self.n_sc):
            for k, chip in enumerate(chips):
                i = self.n_ex + 3 * a + k
                out.append(pltpu.make_async_remote_copy(
                    src_ref=bufs[self.n_ex + a].at[2 * chip[0] + chip[1]], dst_ref=bufs[n_src + self.n_ex + a].at[k],
                    send_sem=send_sems[i], recv_sem=recv_sems[i], device_id=(*chip, c), device_id_type=MESH))
        return out

    def start(self):
        n_buf, n_cp = len(self.buffers), self.n_copies

        def body(*refs):
            bufs = refs[:n_buf]
            send_sems, recv_sems = refs[n_buf:n_buf + n_cp], refs[n_buf + n_cp:n_buf + 2 * n_cp]
            for cp in self._copies(bufs, send_sems, recv_sems):
                cp.start()
            refs[-1][...] = jnp.zeros_like(refs[-1])

        outs = pl.pallas_call(
            body, name=self.name + "_start",
            out_shape=[pltpu.SemaphoreType.DMA(())] * (2 * n_cp) + [pltpu.HBM(b.shape, b.dtype) for b in self.buffers]
            + [jax.ShapeDtypeStruct((8, 128), F32)],
            in_specs=[HBM] * n_buf, out_specs=[SEM] * (2 * n_cp) + [HBM] * n_buf + [pl.BlockSpec(memory_space=pltpu.VMEM)],
            input_output_aliases={i: 2 * n_cp + i for i in range(n_buf)},
            compiler_params=pltpu.CompilerParams(has_side_effects=pltpu.SideEffectType.DATAFLOW_SIDE_EFFECTING),
        )(*self.buffers)
        self.sems, self.buffers = outs[:2 * n_cp], outs[2 * n_cp:2 * n_cp + n_buf]
        return outs[-1]

    def wait(self, *after):
        n_buf, n_cp = len(self.buffers), self.n_copies

        def body(*refs):
            bufs = refs[:n_buf]
            send_sems, recv_sems = refs[n_buf:n_buf + n_cp], refs[n_buf + n_cp:n_buf + 2 * n_cp]
            for cp in self._copies(bufs, send_sems, recv_sems):
                cp.wait_send()
                cp.wait_recv()

        outs = pl.pallas_call(
            body, name=self.name + "_wait", out_shape=[pltpu.HBM(b.shape, b.dtype) for b in self.buffers],
            in_specs=[HBM] * n_buf + [SEM] * (2 * n_cp) + [ANY] * len(after), out_specs=[HBM] * n_buf,
            input_output_aliases={i: i for i in range(n_buf)},
            compiler_params=pltpu.CompilerParams(has_side_effects=pltpu.SideEffectType.DATAFLOW_SIDE_EFFECTING),
        )(*self.buffers, *self.sems, *after)
        self.exchanged = outs[:self.n_ex]
        zones = outs[self.n_ex + self.n_sc:]
        return zones[:self.n_ex], zones[self.n_ex:]


def _gather_halves(halves, name, after=None):
    n = len(halves)
    extra = [] if after is None else [after]

    def body(*refs):
        ins, outs = refs[:n], refs[n + len(extra):2 * n + len(extra)]
        send_sems, recv_sems = refs[2 * n + len(extra):]
        x, y, c, _ = _mesh_pos()
        sib = (x, y, 1 - c)
        remote = [pltpu.make_async_remote_copy(src_ref=ins[a].at[c], dst_ref=outs[a].at[c],
                                               send_sem=send_sems.at[a], recv_sem=recv_sems.at[a],
                                               device_id=sib, device_id_type=MESH) for a in range(n)]
        for cp in remote:
            cp.start()
        for a in range(n):
            pltpu.make_async_remote_copy(src_ref=ins[a].at[1 - c], dst_ref=outs[a].at[1 - c], send_sem=send_sems.at[a],
                                         recv_sem=recv_sems.at[a], device_id=sib, device_id_type=MESH).wait_recv()
        for cp in remote:
            cp.wait_send()

    return pl.pallas_call(
        body, name=name,
        out_shape=[jax.ShapeDtypeStruct(h.shape, F32) for h in halves],
        in_specs=[ANY] * (n + len(extra)), out_specs=[ANY] * n, input_output_aliases={a: a for a in range(n)},
        scratch_shapes=[pltpu.SemaphoreType.DMA((n,)), pltpu.SemaphoreType.DMA((n,))],
    )(*halves, *extra)


SMALL_A_ROWS = 24
SMALL_B_ROWS = 8
SMALL_C_ROWS = N_POOL_GROUPS * POOL_GROUP


class _AllReduceSmall:
    N_IN = 10
    SHAPES = [(SMALL_A_ROWS, D_MODEL), (SMALL_B_ROWS, D_CONV), (SMALL_C_ROWS, POOL_GROUP)]

    def __init__(self, ins, outs, scratch):
        self.ins, self.outs = ins, outs
        self.bufs, self.rcvs, self.send_sems, self.recv_sems = scratch[:3], scratch[3:6], scratch[6], scratch[7]

    @classmethod
    def scratch(cls):
        return ([pltpu.VMEM((3,) + s, F32) for s in cls.SHAPES] + [pltpu.VMEM((3,) + s, F32) for s in cls.SHAPES]
                + [pltpu.SemaphoreType.DMA((9,)), pltpu.SemaphoreType.DMA((9,))])

    @classmethod
    def out_shape(cls):
        return [jax.ShapeDtypeStruct(s, F32) for s in cls.SHAPES]

    def _copies(self, st):
        x, y, c, _ = _mesh_pos()
        peer = [(x, y, 1 - c), (1 - x, y, c), (x, 1 - y, c)][st]
        return [pltpu.make_async_remote_copy(
            src_ref=buf.at[st], dst_ref=rcv.at[st], send_sem=self.send_sems.at[3 * st + i],
            recv_sem=self.recv_sems.at[3 * st + i], device_id=peer, device_id_type=MESH)
            for i, (buf, rcv) in enumerate(zip(self.bufs, self.rcvs))]

    def pack_and_send(self):
        dg1_ref, dg1m_ref, dg2_ref, dg3_ref, dg4_ref, loss_ref, dmeta_ref, dsc_ref, dcw_ref, dpw_ref = self.ins
        a_buf, b_buf, c_buf = self.bufs

        def rowsum(v):
            return jnp.sum(v, axis=0, keepdims=True)

        a_buf[0, 0:1, :] = rowsum(dg1_ref[...] + dg1m_ref[...])
        a_buf[0, 1:2, :] = rowsum(dg2_ref[...])
        a_buf[0, 2:3, :] = rowsum(dg3_ref[...])
        a_buf[0, 3:4, :] = rowsum(dg4_ref[...])
        loss = jnp.sum(rowsum(loss_ref[...]), axis=1, keepdims=True) * (0.5 / D_MODEL)
        a_buf[0, 4:5, :] = jnp.broadcast_to(loss, (1, D_MODEL))
        a_buf[0, 5:8, :] = jnp.zeros((3, D_MODEL), F32)
        a_buf[0, 8:24, :] = dmeta_ref[...]
        b_buf[0, 0:1, :] = rowsum(dsc_ref[...])
        for k in range(3):
            b_buf[0, 1 + k:2 + k, :] = rowsum(dcw_ref[8 * k:8 * k + 8, :])
        b_buf[0, 4:8, :] = jnp.zeros((4, D_CONV), F32)
        c_buf[0] = dpw_ref[...]
        for cp in self._copies(0):
            cp.start()

    def combine(self, st):
        for cp in self._copies(st):
            cp.wait()
        if st < 2:
            for buf, rcv in zip(self.bufs, self.rcvs):
                buf[st + 1] = buf[st] + rcv[st]
            for cp in self._copies(st + 1):
                cp.start()
        else:
            for out, buf, rcv in zip(self.outs, self.bufs, self.rcvs):
                out[...] = buf[st] + rcv[st]


def _row_block(rows):
    for cand in (512, 448, 384, 352, 320, 256, 128, 64, 32, 16):
        if rows % cand == 0:
            return cand
    return rows


def _add_pairs_multi(grads, recvs, place):
    n = len(grads)
    n_sh = grads[0].shape[0]
    halves = [g.shape[1] // 2 for g in grads]
    n_steps = halves[0] // _row_block(halves[0])
    blocks = [(hr // n_steps, g.shape[2]) for hr, g in zip(halves, grads)]

    def body(place_ref, *refs):
        for a_ref, b_ref, o_ref in zip(refs[:n], refs[n:2 * n], refs[2 * n:]):
            o_ref[...] = (a_ref[0] + b_ref[...]).astype(BF16)

    return pl.pallas_call(
        body, name="grad_add_pairs",
        grid_spec=pltpu.PrefetchScalarGridSpec(
            num_scalar_prefetch=1, grid=(n_sh, n_steps),
            in_specs=[pl.BlockSpec((1, 1, br, cols), lambda j, i, p: (j, p[1], i, 0)) for br, cols in blocks]
            + [pl.BlockSpec((1, br, cols), lambda j, i, p: (j, i, 0)) for br, cols in blocks],
            out_specs=[pl.BlockSpec((1, br, cols), lambda j, i, p: (j, i, 0)) for br, cols in blocks]),
        out_shape=[jax.ShapeDtypeStruct((n_sh, hr, g.shape[2]), BF16) for hr, g in zip(halves, grads)],
        compiler_params=_cparams(2),
    )(place, *[g.reshape(n_sh, 2, hr, g.shape[2]) for hr, g in zip(halves, grads)], *recvs)


def _add_pairs(grad, recv, place):
    return _add_pairs_multi([grad], [recv], place)[0]


def _add_chips(grads, recvs, rbufs, place, after=None, name="grad_add_chips"):
    n = len(grads)
    n_sh = grads[0].shape[0]
    halves = [g.shape[1] // 2 for g in grads]
    n_steps = halves[0] // _row_block(halves[0])
    blocks = [(hr // n_steps, g.shape[2]) for hr, g in zip(halves, grads)]
    extra = [] if after is None else [after]

    def body(place_ref, *refs):
        for a_ref, b_ref, r_ref, o_ref in zip(refs[:n], refs[n:2 * n], refs[2 * n:3 * n], refs[3 * n + len(extra):]):
            own = a_ref[0, 0] + b_ref[0]
            o_ref[0] = ((own + r_ref[0].astype(F32)) + r_ref[1].astype(F32)) + r_ref[2].astype(F32)

    return pl.pallas_call(
        body, name=name,
        grid_spec=pltpu.PrefetchScalarGridSpec(
            num_scalar_prefetch=1, grid=(n_steps,),
            in_specs=[pl.BlockSpec((1, 1, br, cols), lambda i, p: (p[0], p[1], i, 0)) for br, cols in blocks]
            + [pl.BlockSpec((1, br, cols), lambda i, p: (p[0], i, 0)) for br, cols in blocks]
            + [pl.BlockSpec((3, br, cols), lambda i, p: (0, i, 0)) for br, cols in blocks]
            + [pl.BlockSpec((8, 128), lambda i, p: (0, 0))] * len(extra),
            out_specs=[pl.BlockSpec((1, br, cols), lambda i, p: (p[1], i, 0)) for br, cols in blocks]),
        out_shape=[jax.ShapeDtypeStruct((2, hr, g.shape[2]), F32) for hr, g in zip(halves, grads)],
        compiler_params=_cparams(1),
    )(place, *[g.reshape(n_sh, 2, hr, g.shape[2]) for hr, g in zip(halves, grads)], *recvs, *rbufs, *extra)


def _adamw_math(w, g, m, v):
    m2 = ADAM_B1 * m + (1.0 - ADAM_B1) * g
    v2 = ADAM_B2 * v + (1.0 - ADAM_B2) * (g * g)
    m_hat = m2 / (1.0 - ADAM_B1 ** ADAM_STEP)
    v_hat = v2 / (1.0 - ADAM_B2 ** ADAM_STEP)
    delta = -ADAM_LR * (m_hat / (jnp.sqrt(v_hat) + ADAM_EPS) + ADAM_WD * w)
    return delta, m2, v2


def _adamw_big(groups):
    n = len(groups)
    rows, cols = groups[0][0].shape
    br = _row_block(rows)
    if n > 1 and br % 16 == 0:
        br //= 2

    def body(*refs):
        for i in range(n):
            w_ref, g_ref, m_ref, v_ref = refs[4 * i:4 * i + 4]
            d_ref, m2_ref, v2_ref = refs[4 * n + 3 * i:4 * n + 3 * i + 3]
            d, m2, v2 = _adamw_math(w_ref[...], g_ref[...], m_ref[...], v_ref[...])
            d_ref[...] = d
            m2_ref[...] = m2
            v2_ref[...] = v2

    spec = pl.BlockSpec((br, cols), lambda i: (i, 0))
    outs = pl.pallas_call(
        body, name="adamw_big", grid=(rows // br,),
        out_shape=[jax.ShapeDtypeStruct((rows, cols), F32)] * (3 * n),
        in_specs=[spec] * (4 * n), out_specs=[spec] * (3 * n), compiler_params=_cparams(1),
    )(*[a for grp in groups for a in grp])
    return [list(outs[3 * i:3 * i + 3]) for i in range(n)]


def _adamw_small(groups):
    n = len(groups)

    def body(*refs):
        ins, outs = refs[:4 * n], refs[4 * n:]
        for i in range(n):
            w, g, m, v = (r[...] for r in ins[4 * i:4 * i + 4])
            d, m2, v2 = _adamw_math(w, g, m, v)
            outs[3 * i][...] = d
            outs[3 * i + 1][...] = m2
            outs[3 * i + 2][...] = v2

    vm = pl.BlockSpec(memory_space=pltpu.VMEM)
    flat = [a for grp in groups for a in grp]
    out_shape = [jax.ShapeDtypeStruct(grp[0].shape, F32) for grp in groups for _ in range(3)]
    outs = pl.pallas_call(body, name="adamw_small", out_shape=out_shape,
                          in_specs=[vm] * (4 * n), out_specs=[vm] * (3 * n))(*flat)
    return [tuple(outs[3 * i:3 * i + 3]) for i in range(n)]


def _load_gathered(gathered, shards, dst_slots, sems):
    n = len(gathered)
    me = 2 * lax.axis_index("x") + lax.axis_index("y")

    def copies(j, own):
        return [pltpu.make_async_copy(shards[a] if own else gathered[a].at[j], dst_slots[a](j), sems.at[n * j + a])
                for a in range(n)]

    for wait in (False, True):
        for j in range(N_CHIPS):
            for own in (False, True):
                @pl.when((me == j) == own)
                def _():
                    for cp in copies(j, own):
                        cp.wait() if wait else cp.start()


N_MIX_SHARDS = 3


def _mixer_fwd(x3, g1, g2, poolw, pscale, shards):
    n_seq, seq, _ = x3.shape
    tm = min(TM_MIX_FWD, seq)
    n_t = seq // tm
    n_steps = n_seq * n_t
    n_ag = len(shards)
    n_ffn = n_ag - N_MIX_SHARDS
    small_rows = shards[2].shape[0]
    conv_cols = D_CONV // N_CHIPS

    def body(x_ref, g1_ref, g2_ref, pw_ref, ps_ref, *rest):
        ag = _AllGather(rest[:n_ag], rest[n_ag + 10:2 * n_ag + 10], *rest[-2:])
        (z_ref, m_ref, h1_ref, a_ref, conv_ref, pooled_ref, yc_ref, zm_ref, meta_ref,
         cw_ref) = rest[n_ag:n_ag + 10]
        win_v, wout_v, small_v, cvb, pb, load_sems = rest[2 * n_ag + 10:-2]
        s, t = pl.program_id(0), pl.program_id(1)
        step = s * n_t + t

        @pl.when(step == 0)
        def _():
            ag.start(range(N_MIX_SHARDS))
            for a in range(N_MIX_SHARDS):
                ag.relay(a)
            for a in range(N_MIX_SHARDS):
                ag.forward(a)
            ag.finish(range(N_MIX_SHARDS))
            ag.start(range(N_MIX_SHARDS, n_ag))
            _load_gathered(ag.outs[:N_MIX_SHARDS], ag.ins[:N_MIX_SHARDS],
                           [lambda j: win_v.at[j], lambda j: wout_v.at[pl.ds(j * OUT_SHARD, OUT_SHARD), :],
                            lambda j: small_v.at[j]], load_sems)

            meta = jnp.concatenate([small_v[j, 0:N_META, :] for j in range(N_CHIPS)], axis=1)
            meta_ref[...] = meta
            cw_ref[...] = jnp.concatenate([small_v[j, N_META:N_META + 3, 0:conv_cols] for j in range(N_CHIPS)], axis=1)
            a_meta = (meta * _rstd(meta) * g1_ref[...]).astype(BF16)
            for j in range(N_CHIPS):
                zm_ref[:, j * IN_SHARD:(j + 1) * IN_SHARD] = _dot(a_meta, win_v[j])

        for i in range(n_ffn):
            @pl.when(step == ((i + 1) * n_steps) // (2 * n_ffn + 2))
            def _():
                ag.relay(N_MIX_SHARDS + i)

        for i in range(n_ffn):
            @pl.when(step == min(n_steps // 2 + ((i + 1) * n_steps) // (2 * n_ffn + 2), n_steps - 1))
            def _():
                ag.forward(N_MIX_SHARDS + i)

        @pl.when(t == 0)
        def _():
            cvb[0:HALO, :] = zm_ref[:, IN_SHARD:2 * IN_SHARD] * zm_ref[:, 2 * IN_SHARD:3 * IN_SHARD]
            pb[0:HALO, :] = zm_ref[:, 3 * IN_SHARD:4 * IN_SHARD]

        @pl.when(t > 0)
        def _():
            cvb[0:HALO, :] = cvb[tm:tm + HALO, :]
            pb[0:HALO, :] = pb[tm:tm + HALO, :]

        xt = x_ref[0]
        a = (xt * _rstd(xt) * g1_ref[...]).astype(BF16)
        a_ref[...] = a
        zb = _dot(a, win_v[0])
        zc = _dot(a, win_v[1])
        zv = _dot(a, win_v[2])
        zp = _dot(a, win_v[3])
        z_ref[0, :, 0:IN_SHARD] = zb
        z_ref[0, :, IN_SHARD:2 * IN_SHARD] = zc
        z_ref[0, :, 2 * IN_SHARD:3 * IN_SHARD] = zv
        cv = zc * zv
        cvb[HALO:HALO + tm, :] = cv
        pb[HALO:HALO + tm, :] = zp
        cw = cw_ref[...]
        conv = cw[0:1] * cvb[HALO - 2:HALO - 2 + tm, :] + cw[1:2] * cvb[HALO - 1:HALO - 1 + tm, :] + cw[2:3] * cv
        conv_ref[...] = conv
        parts = [(zb * conv).astype(BF16)]
        for g in range(N_POOL_GROUPS):
            pooled = _pool_fwd(pb, g, tm).astype(BF16)
            pooled_ref[:, _gcols(g)] = pooled
            parts.append((_dot(pooled, pw_ref[g]) * ps_ref[:, _gcols(g)]).astype(BF16))
        ycat = jnp.concatenate(parts, axis=1)
        yc_ref[...] = ycat
        m = _dot(ycat, wout_v[...])
        m_ref[0] = m
        h1_ref[0] = xt + m * _rstd(m) * g2_ref[...]

        @pl.when(step == n_steps - 1)
        def _():
            ag.finish(range(N_MIX_SHARDS, n_ag))

    n_rows = n_seq * seq
    row = lambda c: pl.BlockSpec((1, tm, c), lambda s, t: (s, t, 0))
    row2 = lambda c: pl.BlockSpec((tm, c), lambda s, t: (s * n_t + t, 0))
    outs = pl.pallas_call(
        body, name="mixer_fwd", grid=(n_seq, n_t),
        out_shape=[jax.ShapeDtypeStruct((n_seq, seq, D_Z), F32), jax.ShapeDtypeStruct((n_seq, seq, D_MODEL), F32),
                   jax.ShapeDtypeStruct((n_seq, seq, D_MODEL), F32), jax.ShapeDtypeStruct((n_rows, D_MODEL), BF16),
                   jax.ShapeDtypeStruct((n_rows, D_CONV), F32), jax.ShapeDtypeStruct((n_rows, D_POOL), BF16),
                   jax.ShapeDtypeStruct((n_rows, D_MODEL), BF16), jax.ShapeDtypeStruct((N_META, D_IN_PROJ), F32),
                   jax.ShapeDtypeStruct((N_META, D_MODEL), F32), jax.ShapeDtypeStruct((3, D_CONV), F32)]
        + _AllGather.out_shape(shards),
        in_specs=[row(D_MODEL), _full((1, D_MODEL)), _full((1, D_MODEL)),
                  _full((N_POOL_GROUPS, POOL_GROUP, POOL_GROUP)), _full((1, D_POOL))] + [ANY] * n_ag,
        out_specs=[row(D_Z), row(D_MODEL), row(D_MODEL), row2(D_MODEL), row2(D_CONV), row2(D_POOL), row2(D_MODEL),
                   _full((N_META, D_IN_PROJ)), _full((N_META, D_MODEL)), _full((3, D_CONV))] + [ANY] * n_ag,
        scratch_shapes=[pltpu.VMEM((N_CHIPS, D_MODEL, IN_SHARD), BF16), pltpu.VMEM((D_MODEL, D_MODEL), BF16),
                        pltpu.VMEM((N_CHIPS, small_rows, D_MODEL // N_CHIPS), F32),
                        pltpu.VMEM((HALO + tm, D_CONV), F32), pltpu.VMEM((HALO + tm, D_POOL), F32),
                        pltpu.SemaphoreType.DMA((N_MIX_SHARDS * N_CHIPS,))] + _AllGather.scratch(n_ag),
        compiler_params=_cparams(2),
    )(x3, g1, g2, poolw, pscale, *shards)
    return outs[:10], outs[10:]


def _ffn_chunks():
    out, r0 = [], 0
    while r0 < D_FF:
        out.append((r0, min(FF_CHUNK, D_FF - r0)))
        r0 += FF_CHUNK
    return out


def _ffn_fwd_bwd(h1, target, g3, g4, gathered, shards):
    n_rows = h1.shape[0]
    tm = min(TM_FFN, n_rows)
    chunks = _ffn_chunks()

    def body(h1_ref, t_ref, g3_ref, g4_ref, wg_all, wu_all, wd_all, wg_s, wu_s, wd_s,
             dh1_ref, f_ref, dd_ref, ds_ref, du_ref, gg_ref, loss_ref, dg3_ref, dg4_ref,
             wg_v, wu_v, wd_v, s_sc, u_sc, sems):
        @pl.when(pl.program_id(0) == 0)
        def _():
            _load_gathered([wg_all, wu_all, wd_all], [wg_s, wu_s, wd_s],
                           [functools.partial(lambda v, j: v.at[pl.ds(j * FF_SHARD, FF_SHARD), :], v)
                            for v in (wg_v, wu_v, wd_v)], sems)
            loss_ref[...] = jnp.zeros_like(loss_ref)
            dg3_ref[...] = jnp.zeros_like(dg3_ref)
            dg4_ref[...] = jnp.zeros_like(dg4_ref)

        h1v = h1_ref[...]
        r3 = _rstd(h1v)
        hh = h1v * r3
        g3v, g4v = g3_ref[...], g4_ref[...]
        f = (hh * g3v).astype(BF16)
        f_ref[...] = f
        d = jnp.zeros((tm, D_MODEL), F32)
        for r0, sz in chunks:
            s = _dot_nt(f, wg_v[r0:r0 + sz, :])
            u = _dot_nt(f, wu_v[r0:r0 + sz, :])
            s_sc[:, r0:r0 + sz] = s
            u_sc[:, r0:r0 + sz] = u
            gc = (s * _sigmoid(s) * u).astype(BF16)
            gg_ref[:, r0:r0 + sz] = gc
            d = d + _dot(gc, wd_v[r0:r0 + sz, :])
        r4 = _rstd(d)
        dh = d * r4
        err = (h1v + dh * g4v) - t_ref[...]
        loss_ref[...] += _rows8(err * err)
        dy = err * (1.0 / D_MODEL)
        dg4_ref[...] += _rows8(dy * dh)
        ddb = _rms_bwd(dy, dh, r4, g4v).astype(BF16)
        dd_ref[...] = ddb
        df = jnp.zeros((tm, D_MODEL), F32)
        for r0, sz in chunks:
            dgg = _dot_nt(ddb, wd_v[r0:r0 + sz, :])
            s = s_sc[:, r0:r0 + sz]
            u = u_sc[:, r0:r0 + sz]
            sig = _sigmoid(s)
            dsc = (dgg * u * (sig * (1.0 + s * (1.0 - sig)))).astype(BF16)
            duc = (dgg * (s * sig)).astype(BF16)
            ds_ref[:, r0:r0 + sz] = dsc
            du_ref[:, r0:r0 + sz] = duc
            df = df + _dot(dsc, wg_v[r0:r0 + sz, :]) + _dot(duc, wu_v[r0:r0 + sz, :])
        dg3_ref[...] += _rows8(df * hh)
        dh1_ref[...] = dy + _rms_bwd(df, hh, r3, g3v)

    row = pl.BlockSpec((tm, D_MODEL), lambda i: (i, 0))
    ffrow = pl.BlockSpec((tm, D_FF), lambda i: (i, 0))
    acc = _full((8, D_MODEL))
    act_bf = jax.ShapeDtypeStruct((n_rows, D_MODEL), BF16)
    ff_bf = jax.ShapeDtypeStruct((n_rows, D_FF), BF16)
    acc_shape = jax.ShapeDtypeStruct((8, D_MODEL), F32)
    w_vmem = pltpu.VMEM((D_FF, D_MODEL), BF16)
    return pl.pallas_call(
        body, name="ffn_fwd_bwd", grid=(n_rows // tm,),
        out_shape=[jax.ShapeDtypeStruct((n_rows, D_MODEL), F32), act_bf, act_bf, ff_bf, ff_bf, ff_bf,
                   acc_shape, acc_shape, acc_shape],
        in_specs=[row, row, _full((1, D_MODEL)), _full((1, D_MODEL))] + [ANY] * 6,
        out_specs=[row, row, row, ffrow, ffrow, ffrow, acc, acc, acc],
        scratch_shapes=[w_vmem, w_vmem, w_vmem, pltpu.VMEM((tm, D_FF), F32), pltpu.VMEM((tm, D_FF), F32),
                        pltpu.SemaphoreType.DMA((3 * N_CHIPS,))],
        compiler_params=_cparams(1),
    )(h1, target, g3, g4, *gathered, *shards)


def _ffn_weight_grads(name, acts, other, exchanged):
    n_rows = other.shape[0]
    n_a, n_ex = len(acts), len(exchanged)
    n_c = 2
    tk = min(TK_DW, n_rows)
    n_k = n_rows // tk
    half = D_FF // n_c

    def body(other_ref, *rest):
        act_refs = rest[:n_a]
        out_refs = rest[n_a + n_ex:2 * n_a + n_ex]
        c, k = pl.program_id(0), pl.program_id(1)
        if n_ex:
            ex = _ExchangeHalves(rest[n_a:n_a + n_ex], rest[2 * n_a + n_ex:2 * n_a + 2 * n_ex], *rest[-2:])

            @pl.when((c == 0) & (k == 0))
            def _():
                ex.start()

        @pl.when(k == 0)
        def _():
            for o in out_refs:
                o[...] = jnp.zeros_like(o)

        ov = other_ref[...]
        for a, o in zip(act_refs, out_refs):
            o[...] += _dot_tn(a[...], ov)

        if n_ex:
            @pl.when((c == n_c - 1) & (k == n_k - 1))
            def _():
                ex.finish()

    row = pl.BlockSpec((tk, D_MODEL), lambda c, k: (k, 0))
    ffrow = pl.BlockSpec((tk, half), lambda c, k: (k, c))
    out = pl.BlockSpec((half, D_MODEL), lambda c, k: (c, 0))
    outs = pl.pallas_call(
        body, name=name, grid=(n_c, n_k),
        out_shape=[jax.ShapeDtypeStruct((D_FF, D_MODEL), F32)] * n_a + _ExchangeHalves.out_shape(exchanged),
        in_specs=[row] + [ffrow] * n_a + [ANY] * n_ex, out_specs=[out] * n_a + [ANY] * n_ex,
        scratch_shapes=_ExchangeHalves.scratch(n_ex) if n_ex else [],
        compiler_params=_cparams(2),
    )(other, *acts, *exchanged)
    return outs[:n_a], outs[n_a:]


def _mixer_bwd(dh1, m3, z3, conv2, pooled2, x3, zmeta, meta_full, g1, g2, convw, poolw, pscale, gathered, shards,
               after):
    n_seq, seq, _ = x3.shape
    tm = min(TM_MIX_BWD, seq)
    sub = min(SUB_MIX_BWD, tm)
    n_t = seq // tm
    n_out = 13

    def body(dh1_ref, m_ref, z_ref, conv_ref, pooled_ref, x_ref, zm_ref, meta_ref, g1_ref, g2_ref, cw_ref, pw_ref,
             ps_ref, after_ref, win_all, wout_all, win_s, wout_s, *rest):
        (dx_ref, dz_ref, dm_ref, dg1_ref, dg2_ref, dsc_ref, dcw_ref, dpw_ref, dzm_ref, dmeta_ref, dg1m_ref, am_ref,
         dzmb_ref) = rest[:n_out]
        win_v, wout_v, dcb, dqb, mcb, mqb, load_sems = rest[n_out:]
        s, i = pl.program_id(0), pl.program_id(1)
        tr = n_t - 1 - i

        @pl.when((s == 0) & (i == 0))
        def _():
            _load_gathered([win_all, wout_all], [win_s, wout_s],
                           [lambda j: win_v.at[j], lambda j: wout_v.at[pl.ds(j * OUT_SHARD, OUT_SHARD), :]], load_sems)
            for ref in (dg1_ref, dg2_ref, dsc_ref, dcw_ref, dpw_ref, dzm_ref):
                ref[...] = jnp.zeros_like(ref)

        @pl.when(i == 0)
        def _():
            dcb[tm:tm + HALO, :] = jnp.zeros((HALO, D_CONV), F32)
            dqb[tm:tm + HALO, :] = jnp.zeros((HALO, D_POOL), F32)

        @pl.when(i > 0)
        def _():
            dcb[tm:tm + HALO, :] = dcb[0:HALO, :]
            dqb[tm:tm + HALO, :] = dqb[0:HALO, :]

        g1v, g2v = g1_ref[...], g2_ref[...]
        cw = cw_ref[...]

        for r0 in range(tm - sub, -1, -sub):
            rows = slice(r0, r0 + sub)
            dh1v = dh1_ref[0, rows, :]
            mv = m_ref[0, rows, :]
            r2 = _rstd(mv)
            mh = mv * r2
            dg2_ref[...] += _rows8(dh1v * mh)
            dmb = _rms_bwd(dh1v, mh, r2, g2v).astype(BF16)
            dm_ref[rows, :] = dmb
            dyc = _dot_nt(dmb, wout_v[...])
            dyconv = dyc[:, 0:D_CONV]

            for g in range(N_POOL_GROUPS):
                pooled = pooled_ref[rows, _gcols(g)]
                mixed = _dot(pooled, pw_ref[g])
                scale = ps_ref[:, _gcols(g)]
                dyp = dyc[:, D_CONV + g * POOL_GROUP:D_CONV + (g + 1) * POOL_GROUP]
                dsc_ref[:, _gcols(g)] += _rows8(dyp * mixed)
                dmix = (dyp * scale).astype(BF16)
                dpw_ref[g] += _dot_tn(pooled, dmix)
                dqb[rows, _gcols(g)] = _dot_nt(dmix, pw_ref[g])

            zb = z_ref[0, rows, 0:IN_SHARD]
            zc = z_ref[0, rows, IN_SHARD:2 * IN_SHARD]
            zv = z_ref[0, rows, 2 * IN_SHARD:3 * IN_SHARD]
            dconv = dyconv * zb
            dcb[rows, :] = dconv
            d1 = dcb[r0 + 1:r0 + 1 + sub, :]
            d2 = dcb[r0 + 2:r0 + 2 + sub, :]
            dcv = cw[2:3] * dconv + cw[1:2] * d1 + cw[0:1] * d2
            cv = zc * zv
            dcw_ref[0:8, :] += _rows8(cv * d2)
            dcw_ref[8:16, :] += _rows8(cv * d1)
            dcw_ref[16:24, :] += _rows8(cv * dconv)
            dzs = [(dyconv * conv_ref[rows, :]).astype(BF16), (dcv * zv).astype(BF16), (dcv * zc).astype(BF16),
                   jnp.concatenate([_pool_bwd(dqb, g, r0, sub) for g in range(N_POOL_GROUPS)], axis=1).astype(BF16)]
            da = jnp.zeros((sub, D_MODEL), F32)
            for j in range(N_CHIPS):
                dz_ref[j, rows, :] = dzs[j]
                da = da + _dot_nt(dzs[j], win_v[j])
            xt = x_ref[0, rows, :]
            r1 = _rstd(xt)
            xh = xt * r1
            dg1_ref[...] += _rows8(da * xh)
            dx_ref[0, rows, :] = dh1v + _rms_bwd(da, xh, r1, g1v)

        @pl.when(tr == 0)
        def _():
            mcb[0:HALO, :] = jnp.zeros((HALO, D_CONV), F32)
            mqb[0:HALO, :] = jnp.zeros((HALO, D_POOL), F32)
            mcb[HALO:2 * HALO, :] = dcb[0:HALO, :]
            mqb[HALO:2 * HALO, :] = dqb[0:HALO, :]
            m1 = mcb[1:1 + HALO, :]
            m2 = mcb[2:2 + HALO, :]
            zc_m = zm_ref[:, IN_SHARD:2 * IN_SHARD]
            zv_m = zm_ref[:, 2 * IN_SHARD:3 * IN_SHARD]
            cv_m = zc_m * zv_m
            dcw_ref[0:8, :] += _rows8(cv_m * m2)
            dcw_ref[8:16, :] += _rows8(cv_m * m1)
            dcv_m = cw[1:2] * m1 + cw[0:1] * m2
            dzm_ref[:, IN_SHARD:2 * IN_SHARD] += dcv_m * zv_m
            dzm_ref[:, 2 * IN_SHARD:3 * IN_SHARD] += dcv_m * zc_m
            dzm_ref[:, 3 * IN_SHARD:4 * IN_SHARD] += jnp.concatenate(
                [_pool_bwd(mqb, g, 0, HALO) for g in range(N_POOL_GROUPS)], axis=1)

        @pl.when((s == n_seq - 1) & (i == n_t - 1))
        def _():
            xm = meta_ref[...]
            rm = _rstd(xm)
            xmh = xm * rm
            am_ref[...] = (xmh * g1v).astype(BF16)
            da_m = jnp.zeros((N_META, D_MODEL), F32)
            for j in range(N_CHIPS):
                dzj = dzm_ref[:, j * IN_SHARD:(j + 1) * IN_SHARD].astype(BF16)
                dzmb_ref[j] = dzj
                da_m = da_m + _dot_nt(dzj, win_v[j])
            dg1m_ref[...] = _rows8(da_m * xmh)
            dmeta_ref[...] = _rms_bwd(da_m, xmh, rm, g1v)

    row3 = lambda c: pl.BlockSpec((1, tm, c), lambda s, i: (s, n_t - 1 - i, 0))
    row2 = lambda c: pl.BlockSpec((tm, c), lambda s, i: (s * n_t + n_t - 1 - i, 0))
    n_rows = n_seq * seq
    outs = pl.pallas_call(
        body, name="mixer_bwd", grid=(n_seq, n_t),
        out_shape=[jax.ShapeDtypeStruct((n_seq, seq, D_MODEL), F32),
                   jax.ShapeDtypeStruct((N_CHIPS, n_rows, IN_SHARD), BF16), jax.ShapeDtypeStruct((n_rows, D_MODEL), BF16),
                   jax.ShapeDtypeStruct((8, D_MODEL), F32), jax.ShapeDtypeStruct((8, D_MODEL), F32),
                   jax.ShapeDtypeStruct((8, D_POOL), F32), jax.ShapeDtypeStruct((24, D_CONV), F32),
                   jax.ShapeDtypeStruct((N_POOL_GROUPS, POOL_GROUP, POOL_GROUP), F32),
                   jax.ShapeDtypeStruct((N_META, D_IN_PROJ), F32),
                   jax.ShapeDtypeStruct((N_META, D_MODEL), F32), jax.ShapeDtypeStruct((8, D_MODEL), F32),
                   jax.ShapeDtypeStruct((N_META, D_MODEL), BF16),
                   jax.ShapeDtypeStruct((N_CHIPS, N_META, IN_SHARD), BF16)],
        in_specs=[row3(D_MODEL), row3(D_MODEL), row3(D_Z), row2(D_CONV), row2(D_POOL), row3(D_MODEL),
                  _full((N_META, D_IN_PROJ)), _full((N_META, D_MODEL)), _full((1, D_MODEL)), _full((1, D_MODEL)),
                  _full((3, D_CONV)), _full((N_POOL_GROUPS, POOL_GROUP, POOL_GROUP)), _full((1, D_POOL)),
                  _full((8, 128))] + [ANY] * 4,
        out_specs=[row3(D_MODEL), pl.BlockSpec((N_CHIPS, tm, IN_SHARD), lambda s, i: (0, s * n_t + n_t - 1 - i, 0)),
                   row2(D_MODEL),
                   _full((8, D_MODEL)), _full((8, D_MODEL)), _full((8, D_POOL)), _full((24, D_CONV)),
                   _full((N_POOL_GROUPS, POOL_GROUP, POOL_GROUP)), _full((N_META, D_IN_PROJ)),
                   _full((N_META, D_MODEL)), _full((8, D_MODEL)), _full((N_META, D_MODEL)),
                   _full((N_CHIPS, N_META, IN_SHARD))],
        scratch_shapes=[pltpu.VMEM((N_CHIPS, D_MODEL, IN_SHARD), BF16), pltpu.VMEM((D_MODEL, D_MODEL), BF16),
                        pltpu.VMEM((tm + HALO, D_CONV), F32), pltpu.VMEM((tm + HALO, D_POOL), F32),
                        pltpu.VMEM((2 * HALO, D_CONV), F32), pltpu.VMEM((2 * HALO, D_POOL), F32),
                        pltpu.SemaphoreType.DMA((2 * N_CHIPS,))],
        compiler_params=_cparams(2),
    )(dh1, m3, z3, conv2, pooled2, x3, zmeta, meta_full, g1, g2, convw, poolw, pscale, after, *gathered, *shards)
    return outs


def _mixer_weight_grads(a, dz, ycat, dm, a_meta, dz_meta, ffn_sums, small):
    n_rows = a.shape[0]
    tk = min(TK_DW, n_rows)
    n_k = n_rows // tk
    n_sc, n_sm = len(ffn_sums), _AllReduceSmall.N_IN

    def body(a_ref, dz_ref, yc_ref, dm_ref, am_ref, dzm_ref, *rest):
        ins, outs, scratch = rest[:n_sc + n_sm], rest[n_sc + n_sm:2 * n_sc + n_sm + 5], rest[2 * n_sc + n_sm + 5:]
        dwin_ref, dwout_ref = outs[:2]
        scatter = _ScatterToChips(ins[:n_sc], outs[2:2 + n_sc], *scratch[:2])
        reduce_small = _AllReduceSmall(ins[n_sc:], outs[2 + n_sc:], scratch[2:])
        k = pl.program_id(0)

        @pl.when(k == 0)
        def _():
            scatter.start()
            reduce_small.pack_and_send()
            am_t = am_ref[...].T
            for j in range(N_CHIPS):
                dwin_ref[j] = _dot(am_t, dzm_ref[j])
            dwout_ref[...] = jnp.zeros_like(dwout_ref)

        for st in range(2):
            @pl.when(k == ((st + 1) * n_k) // 3)
            def _():
                reduce_small.combine(st)

        a_t = a_ref[...].T
        for j in range(N_CHIPS):
            dwin_ref[j] += _dot(a_t, dz_ref[j])
        dwout_ref[...] += _dot_tn(yc_ref[...], dm_ref[...])

        @pl.when(k == n_k - 1)
        def _():
            reduce_small.combine(2)
            scatter.finish()

    row = pl.BlockSpec((tk, D_MODEL), lambda k: (k, 0))
    outs = pl.pallas_call(
        body, name="mixer_weight_grads", grid=(n_k,),
        out_shape=[jax.ShapeDtypeStruct((N_CHIPS, D_MODEL, IN_SHARD), F32),
                   jax.ShapeDtypeStruct((D_MODEL, D_MODEL), F32)] + _ScatterToChips.out_shape(ffn_sums)
        + _AllReduceSmall.out_shape(),
        in_specs=[row, pl.BlockSpec((N_CHIPS, tk, IN_SHARD), lambda k: (0, k, 0)), row, row,
                  _full((N_META, D_MODEL)), _full((N_CHIPS, N_META, IN_SHARD))] + [ANY] * n_sc
        + [_full(s.shape) for s in small],
        out_specs=[_full((N_CHIPS, D_MODEL, IN_SHARD)), _full((D_MODEL, D_MODEL))] + [ANY] * n_sc
        + [_full(s) for s in _AllReduceSmall.SHAPES],
        scratch_shapes=_ScatterToChips.scratch(n_sc) + _AllReduceSmall.scratch(),
        compiler_params=_cparams(1),
    )(a, dz, ycat, dm, a_meta, dz_meta, *ffn_sums, *small)
    return ([outs[0], outs[1].reshape(N_CHIPS, OUT_SHARD, D_MODEL)], outs[2:2 + n_sc], outs[2 + n_sc:])


def kernel(x, meta_tokens, norm_mix_pre, w_in, conv_w, pool_w, pool_scale, w_out, norm_mix_post, norm_ffn_pre, w_gate, w_up, w_down, norm_ffn_post, loss_target, m_meta_tokens, m_norm_mix_pre, m_w_in, m_conv_w, m_pool_w, m_pool_scale, m_w_out, m_norm_mix_post, m_norm_ffn_pre, m_w_gate, m_w_up, m_w_down, m_norm_ffn_post, v_meta_tokens, v_norm_mix_pre, v_w_in, v_conv_w, v_pool_w, v_pool_scale, v_w_out, v_norm_mix_post, v_norm_ffn_pre, v_w_gate, v_w_up, v_w_down, v_norm_ffn_post):
    n_seq, seq, _ = x.shape
    n_rows = n_seq * seq
    chip = 2 * lax.axis_index("x") + lax.axis_index("y")
    meta_cols = D_MODEL // N_CHIPS
    conv_cols = D_CONV // N_CHIPS

    small = jnp.zeros((2 * HALO, meta_cols), F32)
    small = small.at[0:N_META, :].set(meta_tokens).at[N_META:N_META + 3, 0:conv_cols].set(conv_w[0])
    poolw_bf = pool_w[0].astype(BF16)
    pscale = pool_scale
    g1, g2, g3, g4 = norm_mix_pre, norm_mix_post, norm_ffn_pre, norm_ffn_post
    place = jnp.stack([chip, lax.axis_index("c")]).astype(jnp.int32)

    mix_shards = [w_in[0].astype(BF16), w_out[0].astype(BF16)]
    ffn_shards = [w_gate[0].T.astype(BF16), w_up[0].T.astype(BF16), w_down[0].astype(BF16)]
    ((z3, m3, h1, a_bf, conv2, pooled2, yc_bf, zmeta, meta_full, conv_full),
     (win_all, wout_all, _, *ffn_gathered)) = _mixer_fwd(x, g1, g2, poolw_bf, pscale, mix_shards + [small] + ffn_shards)
    dh1, f_bf, dd_bf, ds_bf, du_bf, gg_bf, lossp, dg3p, dg4p = _ffn_fwd_bwd(
        h1.reshape(n_rows, D_MODEL), loss_target.reshape(n_rows, D_MODEL), g3, g4, ffn_gathered, ffn_shards)
    as_shards = lambda g: g.reshape(N_CHIPS, FF_SHARD, D_MODEL)
    (dwg_t, dwu_t), _ = _ffn_weight_grads("ffn_weight_grads_gate_up", [ds_bf, du_bf], f_bf, [])
    dwg_t, dwu_t = as_shards(dwg_t), as_shards(dwu_t)
    (dwd,), (dwg_recv, dwu_recv) = _ffn_weight_grads("ffn_weight_grads_down", [gg_bf], dd_bf, [dwg_t, dwu_t])
    dwd = as_shards(dwd)
    behind_bwd = _SplitComm("grad_comm_behind_mixer_bwd", [dwd],
                            _add_pairs_multi([dwg_t, dwu_t], [dwg_recv, dwu_recv], place))
    (grad_x, dz_bf, dm_bf, dg1p, dg2p, dscp, dcwp, dpw, _, dmeta, dg1m, a_meta, dz_meta) = _mixer_bwd(
        dh1.reshape(n_seq, seq, D_MODEL), m3, z3, conv2, pooled2, x, zmeta, meta_full, g1, g2, conv_full, poolw_bf,
        pscale, [win_all, wout_all], mix_shards, behind_bwd.start())
    (dwd_recv,), (dwg_rbuf, dwu_rbuf) = behind_bwd.wait(dg2p)
    (dwd,) = behind_bwd.exchanged
    mix_grads, (dwd_rbuf,), (a_red, b_red, c_red) = _mixer_weight_grads(
        a_bf, dz_bf, yc_bf, dm_bf, a_meta, dz_meta, [_add_pairs(dwd, dwd_recv, place)],
        [dg1p, dg1m, dg2p, dg3p, dg4p, lossp, dmeta, dscp, dcwp, dpw.reshape(SMALL_C_ROWS, POOL_GROUP)])

    behind_sums = _SplitComm("grad_comm_behind_ffn_sums", mix_grads, [])
    ffn_red = _add_chips([dwg_t, dwu_t, dwd], [dwg_recv, dwu_recv, dwd_recv], [dwg_rbuf, dwu_rbuf, dwd_rbuf],
                         place, after=behind_sums.start(), name="grad_add_chips_ffn")
    mix_recvs, _ = behind_sums.wait(ffn_red[0])
    mix_grads = behind_sums.exchanged
    behind_tail = _SplitComm("grad_comm_behind_ffn_tail", [], _add_pairs_multi(mix_grads, mix_recvs, place))
    as_full = lambda r: r.reshape(2 * r.shape[1], r.shape[2])
    g_wg_t, g_wu_t, g_wd = [as_full(r) for r in _gather_halves(list(ffn_red), "grad_gather_halves_ffn",
                                                                 after=behind_tail.start())]
    ffn_out = _adamw_big([(w_gate[0].T, g_wg_t, m_w_gate[0].T, v_w_gate[0].T),
                          (w_up[0].T, g_wu_t, m_w_up[0].T, v_w_up[0].T), (w_down[0], g_wd, m_w_down[0], v_w_down[0])])

    loss = a_red[4, 0]
    g_g1, g_g2, g_g3, g_g4 = a_red[0:1], a_red[1:2], a_red[2:3], a_red[3:4]
    g_meta = lax.dynamic_slice(a_red, (8, chip * meta_cols), (N_META, meta_cols))
    g_pscale = b_red[0:1]
    g_conv = lax.dynamic_slice(b_red, (1, chip * conv_cols), (3, conv_cols))
    g_poolw = c_red

    small_groups = [
        (meta_tokens, g_meta, m_meta_tokens, v_meta_tokens),
        (g1, g_g1, m_norm_mix_pre, v_norm_mix_pre),
        (conv_w[0], g_conv, m_conv_w[0], v_conv_w[0]),
        (pool_w.reshape(SMALL_C_ROWS, POOL_GROUP), g_poolw, m_pool_w.reshape(SMALL_C_ROWS, POOL_GROUP),
         v_pool_w.reshape(SMALL_C_ROWS, POOL_GROUP)),
        (pool_scale, g_pscale, m_pool_scale, v_pool_scale),
        (g2, g_g2, m_norm_mix_post, v_norm_mix_post),
        (g3, g_g3, m_norm_ffn_pre, v_norm_ffn_pre),
        (g4, g_g4, m_norm_ffn_post, v_norm_ffn_post),
    ]
    small_out = _adamw_small(small_groups)
    _, mix_rbufs = behind_tail.wait(ffn_out[2][0], small_out[0][0])
    mix_red = _add_chips(mix_grads, mix_recvs, mix_rbufs, place)
    g_win, g_wout = [as_full(r) for r in _gather_halves(list(mix_red), "grad_gather_halves_mixer")]
    big_out = (_adamw_big([(w_in[0], g_win, m_w_in[0], v_w_in[0])])
               + _adamw_big([(w_out[0], g_wout, m_w_out[0], v_w_out[0])]) + ffn_out)
    big_out[2] = [o.T for o in big_out[2]]
    big_out[3] = [o.T for o in big_out[3]]
    g_wg, g_wu = g_wg_t.T, g_wu_t.T

    grads_out = [g_meta, g_g1, g_win[None], g_conv[None], g_poolw.reshape(pool_w.shape), g_pscale, g_wout[None],
                 g_g2, g_g3, g_wg[None], g_wu[None], g_wd[None], g_g4]
    s_meta, s_g1, s_conv, s_poolw, s_pscale, s_g2, s_g3, s_g4 = small_out
    b_win, b_wout, b_wg, b_wu, b_wd = big_out

    def leaf(k):
        return [s_meta[k], s_g1[k], b_win[k][None], s_conv[k][None], s_poolw[k].reshape(pool_w.shape), s_pscale[k],
                b_wout[k][None], s_g2[k], s_g3[k], b_wg[k][None], b_wu[k][None], b_wd[k][None], s_g4[k]]

    return (loss, grad_x, *grads_out, *leaf(0), *leaf(1), *leaf(2))
```

```python
import functools

import jax
import jax.numpy as jnp
from jax import lax
from jax.experimental import pallas as pl
from jax.experimental.pallas import tpu as pltpu

F32 = jnp.float32
BF16 = jnp.bfloat16
MESH = pl.DeviceIdType.MESH

D_MODEL = 1024
D_CONV = 512
D_POOL = 512
POOL_GROUP = 128
N_POOL_GROUPS = 4
D_IN_PROJ = 2048
D_FF = 2816
N_CHIPS = 4
FF_SHARD = D_FF // N_CHIPS
IN_SHARD = D_IN_PROJ // N_CHIPS
OUT_SHARD = D_MODEL // N_CHIPS
D_Z = 3 * IN_SHARD
N_META = 16
HALO = 16
RMS_EPS = 1e-6

ADAM_LR = 0.001
ADAM_B1 = 0.9
ADAM_B2 = 0.999
ADAM_EPS = 1e-08
ADAM_WD = 0.01
ADAM_STEP = 10

TM_MIX_FWD = 512
TM_MIX_BWD = 512
SUB_MIX_BWD = 512
TM_FFN = 256
TK_DW = 1024
FF_CHUNK = 1536
VMEM_LIMIT = 56 * 1024 * 1024


def _cparams(n_grid):
    return pltpu.CompilerParams(dimension_semantics=("arbitrary",) * n_grid, vmem_limit_bytes=VMEM_LIMIT)


def _dot(a, b):
    return jnp.dot(a, b, preferred_element_type=F32)


def _dot_nt(a, b):
    return lax.dot_general(a, b, (((1,), (1,)), ((), ())), preferred_element_type=F32)


def _dot_tn(a, b):
    return lax.dot_general(a, b, (((0,), (0,)), ((), ())), preferred_element_type=F32)


def _rows8(v):
    r, c = v.shape
    return v.reshape(r // 8, 8, c).sum(axis=0)


def _rstd(v):
    return lax.rsqrt(jnp.mean(v * v, axis=-1, keepdims=True) + RMS_EPS)


def _rms_bwd(dy, xhat, rstd, gain):
    dyg = dy * gain
    return rstd * (dyg - xhat * jnp.mean(dyg * xhat, axis=-1, keepdims=True))


def _sigmoid(v):
    return 1.0 / (1.0 + jnp.exp(-v))


def _gcols(g):
    return slice(g * POOL_GROUP, (g + 1) * POOL_GROUP)


def _window_sum(e, g, ahead):
    n = e.shape[0]
    w = e
    for level in range(g + 1):
        shift = 1 << level
        w = w + pltpu.roll(w, (n - shift) if ahead else shift, 0)
    return w


def _pool_fwd(pb, g, n):
    e = pb[0:HALO + n, _gcols(g)]
    return _window_sum(e, g, False)[HALO:, :] * (1.0 / (2 << g)) - e[HALO:, :]


def _pool_bwd(qb, g, r0, n):
    e = qb[r0:r0 + n + HALO, _gcols(g)]
    return _window_sum(e, g, True)[0:n, :] * (1.0 / (2 << g)) - e[0:n, :]


def _full(shape):
    nd = len(shape)
    return pl.BlockSpec(shape, lambda *_: (0,) * nd)


ANY = pl.BlockSpec(memory_space=pl.ANY)


def _mesh_pos():
    x, y, c = lax.axis_index("x"), lax.axis_index("y"), lax.axis_index("c")
    chips = [(1 - x, y), (x, 1 - y), (1 - x, 1 - y)]
    return x, y, c, chips


def _half(ref, h):
    hr = ref.shape[0] // 2
    return ref.at[pl.ds(h * hr, hr), :]


class _AllGather:
    PER_ARRAY = 9

    def __init__(self, ins, outs, send_sems, recv_sems):
        self.ins, self.outs, self.send_sems, self.recv_sems = ins, outs, send_sems, recv_sems
        self.n = len(ins)

    @classmethod
    def scratch(cls, n):
        return [pltpu.SemaphoreType.DMA((cls.PER_ARRAY * n,)), pltpu.SemaphoreType.DMA((cls.PER_ARRAY * n,))]

    @staticmethod
    def out_shape(shards):
        return [jax.ShapeDtypeStruct((N_CHIPS,) + s.shape, s.dtype) for s in shards]

    def _copy(self, a, k, src, dst, to):
        i = self.PER_ARRAY * a + k
        return pltpu.make_async_remote_copy(src_ref=src, dst_ref=dst, send_sem=self.send_sems.at[i],
                                            recv_sem=self.recv_sems.at[i], device_id=to, device_id_type=MESH)

    def _piece(self, a, chip, piece, h=None):
        h = lax.axis_index("c") if h is None else h
        rows = self.ins[a].shape[0] // 4
        return self.outs[a].at[chip].at[pl.ds((2 * h + piece) * rows, rows), :]

    def _own(self, a, k):
        x, y, c, chips = _mesh_pos()
        piece = (1, 0, 0, 1)[k]
        rows = self.ins[a].shape[0] // 4
        src = self.ins[a].at[pl.ds((2 * c + piece) * rows, rows), :]
        return self._copy(a, k, src, self._piece(a, 2 * x + y, piece), (*chips[k // 2], c))

    def _relay(self, a, k):
        x, y, c, chips = _mesh_pos()
        source, to, piece = (chips[1], chips[0], 0) if k == 4 else (chips[0], chips[1], 1)
        rows = self._piece(a, 2 * source[0] + source[1], piece)
        return self._copy(a, k, rows, rows, (*to, c))

    def _sibling(self, a, k, h):
        x, y, c, chips = _mesh_pos()
        chip = chips[k - 6]
        slot = _half(self.outs[a].at[2 * chip[0] + chip[1]], h)
        return self._copy(a, k, slot, slot, (x, y, 1 - c))

    def start(self, arrays=None):
        for a in (range(self.n) if arrays is None else arrays):
            for k in range(4):
                self._own(a, k).start()

    def relay(self, a):
        self._own(a, 2).wait_recv()
        self._relay(a, 4).start()
        self._own(a, 0).wait_recv()
        self._relay(a, 5).start()

    def forward(self, a):
        c = lax.axis_index("c")
        self._own(a, 1).wait_recv()
        self._sibling(a, 6, c).start()
        self._own(a, 3).wait_recv()
        self._sibling(a, 7, c).start()
        self._relay(a, 4).wait_recv()
        self._relay(a, 5).wait_recv()
        self._sibling(a, 8, c).start()

    def finish(self, arrays=None):
        c = lax.axis_index("c")
        arrays = range(self.n) if arrays is None else arrays
        for a in arrays:
            for k in range(6, 9):
                self._sibling(a, k, 1 - c).wait_recv()
        for a in arrays:
            for k in range(4):
                self._own(a, k).wait_send()
            for k in range(4, 6):
                self._relay(a, k).wait_send()
            for k in range(6, 9):
                self._sibling(a, k, c).wait_send()


class _ExchangeHalves:
    def __init__(self, ins, recvs, send_sems, recv_sems):
        self.ins, self.recvs, self.send_sems, self.recv_sems = ins, recvs, send_sems, recv_sems

    @staticmethod
    def scratch(n):
        return [pltpu.SemaphoreType.DMA((n,)), pltpu.SemaphoreType.DMA((n,))]

    @staticmethod
    def out_shape(grads):
        return [jax.ShapeDtypeStruct((g.shape[0], g.shape[1] // 2, g.shape[2]), g.dtype) for g in grads]

    def _copies(self):
        x, y, c, _ = _mesh_pos()
        out = []
        for a, (src, dst) in enumerate(zip(self.ins, self.recvs)):
            hr = src.shape[1] // 2
            out.append(pltpu.make_async_remote_copy(
                src_ref=src.at[:, pl.ds((1 - c) * hr, hr), :], dst_ref=dst, send_sem=self.send_sems.at[a],
                recv_sem=self.recv_sems.at[a], device_id=(x, y, 1 - c), device_id_type=MESH))
        return out

    def start(self):
        for cp in self._copies():
            cp.start()

    def finish(self):
        for cp in self._copies():
            cp.wait()


def _exchange_halves(grads):
    n = len(grads)

    def body(*refs):
        ex = _ExchangeHalves(refs[:n], refs[n:2 * n], *refs[2 * n:])
        ex.start()
        ex.finish()

    return pl.pallas_call(
        body, name="grad_exchange_halves", out_shape=_ExchangeHalves.out_shape(grads),
        in_specs=[ANY] * n, out_specs=[ANY] * n, scratch_shapes=_ExchangeHalves.scratch(n),
    )(*grads)


class _ScatterToChips:
    def __init__(self, ins, rbufs, send_sems, recv_sems):
        self.ins, self.rbufs, self.send_sems, self.recv_sems = ins, rbufs, send_sems, recv_sems

    @staticmethod
    def scratch(n):
        return [pltpu.SemaphoreType.DMA((3 * n,)), pltpu.SemaphoreType.DMA((3 * n,))]

    @staticmethod
    def out_shape(sums):
        return [jax.ShapeDtypeStruct((3,) + s.shape[1:], BF16) for s in sums]

    def _copies(self):
        x, y, c, chips = _mesh_pos()
        out = []
        for a, (src, dst) in enumerate(zip(self.ins, self.rbufs)):
            for k, chip in enumerate(chips):
                out.append(pltpu.make_async_remote_copy(
                    src_ref=src.at[2 * chip[0] + chip[1]], dst_ref=dst.at[k], send_sem=self.send_sems.at[3 * a + k],
                    recv_sem=self.recv_sems.at[3 * a + k], device_id=(*chip, c), device_id_type=MESH))
        return out

    def start(self):
        for cp in self._copies():
            cp.start()

    def finish(self):
        for cp in self._copies():
            cp.wait()


HBM = pl.BlockSpec(memory_space=pltpu.HBM)
SEM = pl.BlockSpec(memory_space=pltpu.SEMAPHORE)


class _SplitComm:
    def __init__(self, name, exchanged, scattered):
        self.name, self.n_ex, self.n_sc = name, len(exchanged), len(scattered)
        self.n_copies = self.n_ex + 3 * self.n_sc
        zones = ([lax.empty((g.shape[0], g.shape[1] // 2, g.shape[2]), g.dtype) for g in exchanged]
                 + [lax.empty((3,) + s.shape[1:], s.dtype) for s in scattered])
        self.buffers = [pltpu.with_memory_space_constraint(v, pltpu.HBM)
                        for v in list(exchanged) + list(scattered) + zones]

    def _copies(self, bufs, send_sems, recv_sems):
        x, y, c, chips = _mesh_pos()
        n_src = self.n_ex + self.n_sc
        out = []
        for a in range(self.n_ex):
            hr = bufs[a].shape[1] // 2
            out.append(pltpu.make_async_remote_copy(
                src_ref=bufs[a].at[:, pl.ds((1 - c) * hr, hr), :], dst_ref=bufs[n_src + a], send_sem=send_sems[a],
                recv_sem=recv_sems[a], device_id=(x, y, 1 - c), device_id_type=MESH))
        for a in range(self.n_sc):
            for k, chip in enumerate(chips):
                i = self.n_ex + 3 * a + k
                out.append(pltpu.make_async_remote_copy(
                    src_ref=bufs[self.n_ex + a].at[2 * chip[0] + chip[1]], dst_ref=bufs[n_src + self.n_ex + a].at[k],
                    send_sem=send_sems[i], recv_sem=recv_sems[i], device_id=(*chip, c), device_id_type=MESH))
        return out

    def start(self):
        n_buf, n_cp = len(self.buffers), self.n_copies

        def body(*refs):
            bufs = refs[:n_buf]
            send_sems, recv_sems = refs[n_buf:n_buf + n_cp], refs[n_buf + n_cp:n_buf + 2 * n_cp]
            for cp in self._copies(bufs, send_sems, recv_sems):
                cp.start()
            refs[-1][...] = jnp.zeros_like(refs[-1])

        outs = pl.pallas_call(
            body, name=self.name + "_start",
            out_shape=[pltpu.SemaphoreType.DMA(())] * (2 * n_cp) + [pltpu.HBM(b.shape, b.dtype) for b in self.buffers]
            + [jax.ShapeDtypeStruct((8, 128), F32)],
            in_specs=[HBM] * n_buf, out_specs=[SEM] * (2 * n_cp) + [HBM] * n_buf + [pl.BlockSpec(memory_space=pltpu.VMEM)],
            input_output_aliases={i: 2 * n_cp + i for i in range(n_buf)},
            compiler_params=pltpu.CompilerParams(has_side_effects=pltpu.SideEffectType.DATAFLOW_SIDE_EFFECTING),
        )(*self.buffers)
        self.sems, self.buffers = outs[:2 * n_cp], outs[2 * n_cp:2 * n_cp + n_buf]
        return outs[-1]

    def wait(self, *after):
        n_buf, n_cp = len(self.buffers), self.n_copies

        def body(*refs):
            bufs = refs[:n_buf]
            send_sems, recv_sems = refs[n_buf:n_buf + n_cp], refs[n_buf + n_cp:n_buf + 2 * n_cp]
            for cp in self._copies(bufs, send_sems, recv_sems):
                cp.wait_send()
                cp.wait_recv()

        outs = pl.pallas_call(
            body, name=self.name + "_wait", out_shape=[pltpu.HBM(b.shape, b.dtype) for b in self.buffers],
            in_specs=[HBM] * n_buf + [SEM] * (2 * n_cp) + [ANY] * len(after), out_specs=[HBM] * n_buf,
            input_output_aliases={i: i for i in range(n_buf)},
            compiler_params=pltpu.CompilerParams(has_side_effects=pltpu.SideEffectType.DATAFLOW_SIDE_EFFECTING),
        )(*self.buffers, *self.sems, *after)
        self.exchanged = outs[:self.n_ex]
        zones = outs[self.n_ex + self.n_sc:]
        return zones[:self.n_ex], zones[self.n_ex:]


def _gather_halves(halves, name, after=None):
    n = len(halves)
    extra = [] if after is None else [after]

    def body(*refs):
        ins, outs = refs[:n], refs[n + len(extra):2 * n + len(extra)]
        send_sems, recv_sems = refs[2 * n + len(extra):]
        x, y, c, _ = _mesh_pos()
        sib = (x, y, 1 - c)
        remote = [pltpu.make_async_remote_copy(src_ref=ins[a].at[c], dst_ref=outs[a].at[c],
                                               send_sem=send_sems.at[a], recv_sem=recv_sems.at[a],
                                               device_id=sib, device_id_type=MESH) for a in range(n)]
        for cp in remote:
            cp.start()
        for a in range(n):
            pltpu.make_async_remote_copy(src_ref=ins[a].at[1 - c], dst_ref=outs[a].at[1 - c], send_sem=send_sems.at[a],
                                         recv_sem=recv_sems.at[a], device_id=sib, device_id_type=MESH).wait_recv()
        for cp in remote:
            cp.wait_send()

    return pl.pallas_call(
        body, name=name,
        out_shape=[jax.ShapeDtypeStruct(h.shape, F32) for h in halves],
        in_specs=[ANY] * (n + len(extra)), out_specs=[ANY] * n, input_output_aliases={a: a for a in range(n)},
        scratch_shapes=[pltpu.SemaphoreType.DMA((n,)), pltpu.SemaphoreType.DMA((n,))],
    )(*halves, *extra)


SMALL_A_ROWS = 24
SMALL_B_ROWS = 8
SMALL_C_ROWS = N_POOL_GROUPS * POOL_GROUP


class _AllReduceSmall:
    N_IN = 10
    SHAPES = [(SMALL_A_ROWS, D_MODEL), (SMALL_B_ROWS, D_CONV), (SMALL_C_ROWS, POOL_GROUP)]

    def __init__(self, ins, outs, scratch):
        self.ins, self.outs = ins, outs
        self.bufs, self.rcvs, self.send_sems, self.recv_sems = scratch[:3], scratch[3:6], scratch[6], scratch[7]

    @classmethod
    def scratch(cls):
        return ([pltpu.VMEM((3,) + s, F32) for s in cls.SHAPES] + [pltpu.VMEM((3,) + s, F32) for s in cls.SHAPES]
                + [pltpu.SemaphoreType.DMA((9,)), pltpu.SemaphoreType.DMA((9,))])

    @classmethod
    def out_shape(cls):
        return [jax.ShapeDtypeStruct(s, F32) for s in cls.SHAPES]

    def _copies(self, st):
        x, y, c, _ = _mesh_pos()
        peer = [(x, y, 1 - c), (1 - x, y, c), (x, 1 - y, c)][st]
        return [pltpu.make_async_remote_copy(
            src_ref=buf.at[st], dst_ref=rcv.at[st], send_sem=self.send_sems.at[3 * st + i],
            recv_sem=self.recv_sems.at[3 * st + i], device_id=peer, device_id_type=MESH)
            for i, (buf, rcv) in enumerate(zip(self.bufs, self.rcvs))]

    def pack_and_send(self):
        dg1_ref, dg1m_ref, dg2_ref, dg3_ref, dg4_ref, loss_ref, dmeta_ref, dsc_ref, dcw_ref, dpw_ref = self.ins
        a_buf, b_buf, c_buf = self.bufs

        def rowsum(v):
            return jnp.sum(v, axis=0, keepdims=True)

        a_buf[0, 0:1, :] = rowsum(dg1_ref[...] + dg1m_ref[...])
        a_buf[0, 1:2, :] = rowsum(dg2_ref[...])
        a_buf[0, 2:3, :] = rowsum(dg3_ref[...])
        a_buf[0, 3:4, :] = rowsum(dg4_ref[...])
        loss = jnp.sum(rowsum(loss_ref[...]), axis=1, keepdims=True) * (0.5 / D_MODEL)
        a_buf[0, 4:5, :] = jnp.broadcast_to(loss, (1, D_MODEL))
        a_buf[0, 5:8, :] = jnp.zeros((3, D_MODEL), F32)
        a_buf[0, 8:24, :] = dmeta_ref[...]
        b_buf[0, 0:1, :] = rowsum(dsc_ref[...])
        for k in range(3):
            b_buf[0, 1 + k:2 + k, :] = rowsum(dcw_ref[8 * k:8 * k + 8, :])
        b_buf[0, 4:8, :] = jnp.zeros((4, D_CONV), F32)
        c_buf[0] = dpw_ref[...]
        for cp in self._copies(0):
            cp.start()

    def combine(self, st):
        for cp in self._copies(st):
            cp.wait()
        if st < 2:
            for buf, rcv in zip(self.bufs, self.rcvs):
                buf[st + 1] = buf[st] + rcv[st]
            for cp in self._copies(st + 1):
                cp.start()
        else:
            for out, buf, rcv in zip(self.outs, self.bufs, self.rcvs):
                out[...] = buf[st] + rcv[st]


def _row_block(rows):
    for cand in (512, 448, 384, 352, 320, 256, 128, 64, 32, 16):
        if rows % cand == 0:
            return cand
    return rows


def _add_pairs_multi(grads, recvs, place):
    n = len(grads)
    n_sh = grads[0].shape[0]
    halves = [g.shape[1] // 2 for g in grads]
    n_steps = halves[0] // _row_block(halves[0])
    blocks = [(hr // n_steps, g.shape[2]) for hr, g in zip(halves, grads)]

    def body(place_ref, *refs):
        for a_ref, b_ref, o_ref in zip(refs[:n], refs[n:2 * n], refs[2 * n:]):
            o_ref[...] = (a_ref[0] + b_ref[...]).astype(BF16)

    return pl.pallas_call(
        body, name="grad_add_pairs",
        grid_spec=pltpu.PrefetchScalarGridSpec(
            num_scalar_prefetch=1, grid=(n_sh, n_steps),
            in_specs=[pl.BlockSpec((1, 1, br, cols), lambda j, i, p: (j, p[1], i, 0)) for br, cols in blocks]
            + [pl.BlockSpec((1, br, cols), lambda j, i, p: (j, i, 0)) for br, cols in blocks],
            out_specs=[pl.BlockSpec((1, br, cols), lambda j, i, p: (j, i, 0)) for br, cols in blocks]),
        out_shape=[jax.ShapeDtypeStruct((n_sh, hr, g.shape[2]), BF16) for hr, g in zip(halves, grads)],
        compiler_params=_cparams(2),
    )(place, *[g.reshape(n_sh, 2, hr, g.shape[2]) for hr, g in zip(halves, grads)], *recvs)


def _add_pairs(grad, recv, place):
    return _add_pairs_multi([grad], [recv], place)[0]


def _add_chips(grads, recvs, rbufs, place, after=None, name="grad_add_chips"):
    n = len(grads)
    n_sh = grads[0].shape[0]
    halves = [g.shape[1] // 2 for g in grads]
    n_steps = halves[0] // _row_block(halves[0])
    blocks = [(hr // n_steps, g.shape[2]) for hr, g in zip(halves, grads)]
    extra = [] if after is None else [after]

    def body(place_ref, *refs):
        for a_ref, b_ref, r_ref, o_ref in zip(refs[:n], refs[n:2 * n], refs[2 * n:3 * n], refs[3 * n + len(extra):]):
            own = a_ref[0, 0] + b_ref[0]
            o_ref[0] = ((own + r_ref[0].astype(F32)) + r_ref[1].astype(F32)) + r_ref[2].astype(F32)

    return pl.pallas_call(
        body, name=name,
        grid_spec=pltpu.PrefetchScalarGridSpec(
            num_scalar_prefetch=1, grid=(n_steps,),
            in_specs=[pl.BlockSpec((1, 1, br, cols), lambda i, p: (p[0], p[1], i, 0)) for br, cols in blocks]
            + [pl.BlockSpec((1, br, cols), lambda i, p: (p[0], i, 0)) for br, cols in blocks]
            + [pl.BlockSpec((3, br, cols), lambda i, p: (0, i, 0)) for br, cols in blocks]
            + [pl.BlockSpec((8, 128), lambda i, p: (0, 0))] * len(extra),
            out_specs=[pl.BlockSpec((1, br, cols), lambda i, p: (p[1], i, 0)) for br, cols in blocks]),
        out_shape=[jax.ShapeDtypeStruct((2, hr, g.shape[2]), F32) for hr, g in zip(halves, grads)],
        compiler_params=_cparams(1),
    )(place, *[g.reshape(n_sh, 2, hr, g.shape[2]) for hr, g in zip(halves, grads)], *recvs, *rbufs, *extra)


def _adamw_math(w, g, m, v):
    m2 = ADAM_B1 * m + (1.0 - ADAM_B1) * g
    v2 = ADAM_B2 * v + (1.0 - ADAM_B2) * (g * g)
    m_hat = m2 / (1.0 - ADAM_B1 ** ADAM_STEP)
    v_hat = v2 / (1.0 - ADAM_B2 ** ADAM_STEP)
    delta = -ADAM_LR * (m_hat / (jnp.sqrt(v_hat) + ADAM_EPS) + ADAM_WD * w)
    return delta, m2, v2


def _adamw_big(groups):
    n = len(groups)
    rows, cols = groups[0][0].shape
    br = _row_block(rows)
    if n > 1 and br % 16 == 0:
        br //= 2

    def body(*refs):
        for i in range(n):
            w_ref, g_ref, m_ref, v_ref = refs[4 * i:4 * i + 4]
            d_ref, m2_ref, v2_ref = refs[4 * n + 3 * i:4 * n + 3 * i + 3]
            d, m2, v2 = _adamw_math(w_ref[...], g_ref[...], m_ref[...], v_ref[...])
            d_ref[...] = d
            m2_ref[...] = m2
            v2_ref[...] = v2

    spec = pl.BlockSpec((br, cols), lambda i: (i, 0))
    outs = pl.pallas_call(
        body, name="adamw_big", grid=(rows // br,),
        out_shape=[jax.ShapeDtypeStruct((rows, cols), F32)] * (3 * n),
        in_specs=[spec] * (4 * n), out_specs=[spec] * (3 * n), compiler_params=_cparams(1),
    )(*[a for grp in groups for a in grp])
    return [list(outs[3 * i:3 * i + 3]) for i in range(n)]


def _adamw_small(groups):
    n = len(groups)

    def body(*refs):
        ins, outs = refs[:4 * n], refs[4 * n:]
        for i in range(n):
            w, g, m, v = (r[...] for r in ins[4 * i:4 * i + 4])
            d, m2, v2 = _adamw_math(w, g, m, v)
            outs[3 * i][...] = d
            outs[3 * i + 1][...] = m2
            outs[3 * i + 2][...] = v2

    vm = pl.BlockSpec(memory_space=pltpu.VMEM)
    flat = [a for grp in groups for a in grp]
    out_shape = [jax.ShapeDtypeStruct(grp[0].shape, F32) for grp in groups for _ in range(3)]
    outs = pl.pallas_call(body, name="adamw_small", out_shape=out_shape,
                          in_specs=[vm] * (4 * n), out_specs=[vm] * (3 * n))(*flat)
    return [tuple(outs[3 * i:3 * i + 3]) for i in range(n)]


def _load_gathered(gathered, shards, dst_slots, sems):
    n = len(gathered)
    me = 2 * lax.axis_index("x") + lax.axis_index("y")

    def copies(j, own):
        return [pltpu.make_async_copy(shards[a] if own else gathered[a].at[j], dst_slots[a](j), sems.at[n * j + a])
                for a in range(n)]

    for wait in (False, True):
        for j in range(N_CHIPS):
            for own in (False, True):
                @pl.when((me == j) == own)
                def _():
                    for cp in copies(j, own):
                        cp.wait() if wait else cp.start()


N_MIX_SHARDS = 3


def _mixer_fwd(x3, g1, g2, poolw, pscale, shards):
    n_seq, seq, _ = x3.shape
    tm = min(TM_MIX_FWD, seq)
    n_t = seq // tm
    n_steps = n_seq * n_t
    n_ag = len(shards)
    n_ffn = n_ag - N_MIX_SHARDS
    small_rows = shards[2].shape[0]
    conv_cols = D_CONV // N_CHIPS

    def body(x_ref, g1_ref, g2_ref, pw_ref, ps_ref, *rest):
        ag = _AllGather(rest[:n_ag], rest[n_ag + 10:2 * n_ag + 10], *rest[-2:])
        (z_ref, m_ref, h1_ref, a_ref, conv_ref, pooled_ref, yc_ref, zm_ref, meta_ref,
         cw_ref) = rest[n_ag:n_ag + 10]
        win_v, wout_v, small_v, cvb, pb, load_sems = rest[2 * n_ag + 10:-2]
        s, t = pl.program_id(0), pl.program_id(1)
        step = s * n_t + t

        @pl.when(step == 0)
        def _():
            ag.start(range(N_MIX_SHARDS))
            for a in range(N_MIX_SHARDS):
                ag.relay(a)
            for a in range(N_MIX_SHARDS):
                ag.forward(a)
            ag.finish(range(N_MIX_SHARDS))
            ag.start(range(N_MIX_SHARDS, n_ag))
            _load_gathered(ag.outs[:N_MIX_SHARDS], ag.ins[:N_MIX_SHARDS],
                           [lambda j: win_v.at[j], lambda j: wout_v.at[pl.ds(j * OUT_SHARD, OUT_SHARD), :],
                            lambda j: small_v.at[j]], load_sems)

            meta = jnp.concatenate([small_v[j, 0:N_META, :] for j in range(N_CHIPS)], axis=1)
            meta_ref[...] = meta
            cw_ref[...] = jnp.concatenate([small_v[j, N_META:N_META + 3, 0:conv_cols] for j in range(N_CHIPS)], axis=1)
            a_meta = (meta * _rstd(meta) * g1_ref[...]).astype(BF16)
            for j in range(N_CHIPS):
                zm_ref[:, j * IN_SHARD:(j + 1) * IN_SHARD] = _dot(a_meta, win_v[j])

        for i in range(n_ffn):
            @pl.when(step == ((i + 1) * n_steps) // (2 * n_ffn + 2))
            def _():
                ag.relay(N_MIX_SHARDS + i)

        for i in range(n_ffn):
            @pl.when(step == min(n_steps // 2 + ((i + 1) * n_steps) // (2 * n_ffn + 2), n_steps - 1))
            def _():
                ag.forward(N_MIX_SHARDS + i)

        @pl.when(t == 0)
        def _():
            cvb[0:HALO, :] = zm_ref[:, IN_SHARD:2 * IN_SHARD] * zm_ref[:, 2 * IN_SHARD:3 * IN_SHARD]
            pb[0:HALO, :] = zm_ref[:, 3 * IN_SHARD:4 * IN_SHARD]

        @pl.when(t > 0)
        def _():
            cvb[0:HALO, :] = cvb[tm:tm + HALO, :]
            pb[0:HALO, :] = pb[tm:tm + HALO, :]

        xt = x_ref[0]
        a = (xt * _rstd(xt) * g1_ref[...]).astype(BF16)
        a_ref[...] = a
        zb = _dot(a, win_v[0])
        zc = _dot(a, win_v[1])
        zv = _dot(a, win_v[2])
        zp = _dot(a, win_v[3])
        z_ref[0, :, 0:IN_SHARD] = zb
        z_ref[0, :, IN_SHARD:2 * IN_SHARD] = zc
        z_ref[0, :, 2 * IN_SHARD:3 * IN_SHARD] = zv
        cv = zc * zv
        cvb[HALO:HALO + tm, :] = cv
        pb[HALO:HALO + tm, :] = zp
        cw = cw_ref[...]
        conv = cw[0:1] * cvb[HALO - 2:HALO - 2 + tm, :] + cw[1:2] * cvb[HALO - 1:HALO - 1 + tm, :] + cw[2:3] * cv
        conv_ref[...] = conv
        parts = [(zb * conv).astype(BF16)]
        for g in range(N_POOL_GROUPS):
            pooled = _pool_fwd(pb, g, tm).astype(BF16)
            pooled_ref[:, _gcols(g)] = pooled
            parts.append((_dot(pooled, pw_ref[g]) * ps_ref[:, _gcols(g)]).astype(BF16))
        ycat = jnp.concatenate(parts, axis=1)
        yc_ref[...] = ycat
        m = _dot(ycat, wout_v[...])
        m_ref[0] = m
        h1_ref[0] = xt + m * _rstd(m) * g2_ref[...]

        @pl.when(step == n_steps - 1)
        def _():
            ag.finish(range(N_MIX_SHARDS, n_ag))

    n_rows = n_seq * seq
    row = lambda c: pl.BlockSpec((1, tm, c), lambda s, t: (s, t, 0))
    row2 = lambda c: pl.BlockSpec((tm, c), lambda s, t: (s * n_t + t, 0))
    outs = pl.pallas_call(
        body, name="mixer_fwd", grid=(n_seq, n_t),
        out_shape=[jax.ShapeDtypeStruct((n_seq, seq, D_Z), F32), jax.ShapeDtypeStruct((n_seq, seq, D_MODEL), F32),
                   jax.ShapeDtypeStruct((n_seq, seq, D_MODEL), F32), jax.ShapeDtypeStruct((n_rows, D_MODEL), BF16),
                   jax.ShapeDtypeStruct((n_rows, D_CONV), F32), jax.ShapeDtypeStruct((n_rows, D_POOL), BF16),
                   jax.ShapeDtypeStruct((n_rows, D_MODEL), BF16), jax.ShapeDtypeStruct((N_META, D_IN_PROJ), F32),
                   jax.ShapeDtypeStruct((N_META, D_MODEL), F32), jax.ShapeDtypeStruct((3, D_CONV), F32)]
        + _AllGather.out_shape(shards),
        in_specs=[row(D_MODEL), _full((1, D_MODEL)), _full((1, D_MODEL)),
                  _full((N_POOL_GROUPS, POOL_GROUP, POOL_GROUP)), _full((1, D_POOL))] + [ANY] * n_ag,
        out_specs=[row(D_Z), row(D_MODEL), row(D_MODEL), row2(D_MODEL), row2(D_CONV), row2(D_POOL), row2(D_MODEL),
                   _full((N_META, D_IN_PROJ)), _full((N_META, D_MODEL)), _full((3, D_CONV))] + [ANY] * n_ag,
        scratch_shapes=[pltpu.VMEM((N_CHIPS, D_MODEL, IN_SHARD), BF16), pltpu.VMEM((D_MODEL, D_MODEL), BF16),
                        pltpu.VMEM((N_CHIPS, small_rows, D_MODEL // N_CHIPS), F32),
                        pltpu.VMEM((HALO + tm, D_CONV), F32), pltpu.VMEM((HALO + tm, D_POOL), F32),
                        pltpu.SemaphoreType.DMA((N_MIX_SHARDS * N_CHIPS,))] + _AllGather.scratch(n_ag),
        compiler_params=_cparams(2),
    )(x3, g1, g2, poolw, pscale, *shards)
    return outs[:10], outs[10:]


def _ffn_chunks():
    out, r0 = [], 0
    while r0 < D_FF:
        out.append((r0, min(FF_CHUNK, D_FF - r0)))
        r0 += FF_CHUNK
    return out


def _ffn_fwd_bwd(h1, target, g3, g4, gathered, shards):
    n_rows = h1.shape[0]
    tm = min(TM_FFN, n_rows)
    chunks = _ffn_chunks()

    def body(h1_ref, t_ref, g3_ref, g4_ref, wg_all, wu_all, wd_all, wg_s, wu_s, wd_s,
             dh1_ref, f_ref, dd_ref, ds_ref, du_ref, gg_ref, loss_ref, dg3_ref, dg4_ref,
             wg_v, wu_v, wd_v, s_sc, u_sc, sems):
        @pl.when(pl.program_id(0) == 0)
        def _():
            _load_gathered([wg_all, wu_all, wd_all], [wg_s, wu_s, wd_s],
                           [functools.partial(lambda v, j: v.at[pl.ds(j * FF_SHARD, FF_SHARD), :], v)
                            for v in (wg_v, wu_v, wd_v)], sems)
            loss_ref[...] = jnp.zeros_like(loss_ref)
            dg3_ref[...] = jnp.zeros_like(dg3_ref)
            dg4_ref[...] = jnp.zeros_like(dg4_ref)

        h1v = h1_ref[...]
        r3 = _rstd(h1v)
        hh = h1v * r3
        g3v, g4v = g3_ref[...], g4_ref[...]
        f = (hh * g3v).astype(BF16)
        f_ref[...] = f
        d = jnp.zeros((tm, D_MODEL), F32)
        for r0, sz in chunks:
            s = _dot_nt(f, wg_v[r0:r0 + sz, :])
            u = _dot_nt(f, wu_v[r0:r0 + sz, :])
            s_sc[:, r0:r0 + sz] = s
            u_sc[:, r0:r0 + sz] = u
            gc = (s * _sigmoid(s) * u).astype(BF16)
            gg_ref[:, r0:r0 + sz] = gc
            d = d + _dot(gc, wd_v[r0:r0 + sz, :])
        r4 = _rstd(d)
        dh = d * r4
        err = (h1v + dh * g4v) - t_ref[...]
        loss_ref[...] += _rows8(err * err)
        dy = err * (1.0 / D_MODEL)
        dg4_ref[...] += _rows8(dy * dh)
        ddb = _rms_bwd(dy, dh, r4, g4v).astype(BF16)
        dd_ref[...] = ddb
        df = jnp.zeros((tm, D_MODEL), F32)
        for r0, sz in chunks:
            dgg = _dot_nt(ddb, wd_v[r0:r0 + sz, :])
            s = s_sc[:, r0:r0 + sz]
            u = u_sc[:, r0:r0 + sz]
            sig = _sigmoid(s)
            dsc = (dgg * u * (sig * (1.0 + s * (1.0 - sig)))).astype(BF16)
            duc = (dgg * (s * sig)).astype(BF16)
            ds_ref[:, r0:r0 + sz] = dsc
            du_ref[:, r0:r0 + sz] = duc
            df = df + _dot(dsc, wg_v[r0:r0 + sz, :]) + _dot(duc, wu_v[r0:r0 + sz, :])
        dg3_ref[...] += _rows8(df * hh)
        dh1_ref[...] = dy + _rms_bwd(df, hh, r3, g3v)

    row = pl.BlockSpec((tm, D_MODEL), lambda i: (i, 0))
    ffrow = pl.BlockSpec((tm, D_FF), lambda i: (i, 0))
    acc = _full((8, D_MODEL))
    act_bf = jax.ShapeDtypeStruct((n_rows, D_MODEL), BF16)
    ff_bf = jax.ShapeDtypeStruct((n_rows, D_FF), BF16)
    acc_shape = jax.ShapeDtypeStruct((8, D_MODEL), F32)
    w_vmem = pltpu.VMEM((D_FF, D_MODEL), BF16)
    return pl.pallas_call(
        body, name="ffn_fwd_bwd", grid=(n_rows // tm,),
        out_shape=[jax.ShapeDtypeStruct((n_rows, D_MODEL), F32), act_bf, act_bf, ff_bf, ff_bf, ff_bf,
                   acc_shape, acc_shape, acc_shape],
        in_specs=[row, row, _full((1, D_MODEL)), _full((1, D_MODEL))] + [ANY] * 6,
        out_specs=[row, row, row, ffrow, ffrow, ffrow, acc, acc, acc],
        scratch_shapes=[w_vmem, w_vmem, w_vmem, pltpu.VMEM((tm, D_FF), F32), pltpu.VMEM((tm, D_FF), F32),
                        pltpu.SemaphoreType.DMA((3 * N_CHIPS,))],
        compiler_params=_cparams(1),
    )(h1, target, g3, g4, *gathered, *shards)


def _ffn_weight_grads(name, acts, other, exchanged):
    n_rows = other.shape[0]
    n_a, n_ex = len(acts), len(exchanged)
    n_c = n_a
    tk = min(TK_DW, n_rows)
    n_k = n_rows // tk
    half = D_FF // n_c

    def body(other_ref, *rest):
        act_refs = rest[:n_a]
        out_refs = rest[n_a + n_ex:2 * n_a + n_ex]
        c, k = pl.program_id(0), pl.program_id(1)
        if n_ex:
            ex = _ExchangeHalves(rest[n_a:n_a + n_ex], rest[2 * n_a + n_ex:2 * n_a + 2 * n_ex], *rest[-2:])

            @pl.when((c == 0) & (k == 0))
            def _():
                ex.start()

        @pl.when(k == 0)
        def _():
            for o in out_refs:
                o[...] = jnp.zeros_like(o)

        ov = other_ref[...]
        for a, o in zip(act_refs, out_refs):
            o[...] += _dot_tn(a[...], ov)

        if n_ex:
            @pl.when((c == n_c - 1) & (k == n_k - 1))
            def _():
                ex.finish()

    row = pl.BlockSpec((tk, D_MODEL), lambda c, k: (k, 0))
    ffrow = pl.BlockSpec((tk, half), lambda c, k: (k, c))
    out = pl.BlockSpec((half, D_MODEL), lambda c, k: (c, 0))
    outs = pl.pallas_call(
        body, name=name, grid=(n_c, n_k),
        out_shape=[jax.ShapeDtypeStruct((D_FF, D_MODEL), F32)] * n_a + _ExchangeHalves.out_shape(exchanged),
        in_specs=[row] + [ffrow] * n_a + [ANY] * n_ex, out_specs=[out] * n_a + [ANY] * n_ex,
        scratch_shapes=_ExchangeHalves.scratch(n_ex) if n_ex else [],
        compiler_params=_cparams(2),
    )(other, *acts, *exchanged)
    return outs[:n_a], outs[n_a:]


def _mixer_bwd(dh1, m3, z3, conv2, pooled2, x3, zmeta, meta_full, g1, g2, convw, poolw, pscale, gathered, shards,
               after):
    n_seq, seq, _ = x3.shape
    tm = min(TM_MIX_BWD, seq)
    sub = min(SUB_MIX_BWD, tm)
    n_t = seq // tm
    n_out = 13

    def body(dh1_ref, m_ref, z_ref, conv_ref, pooled_ref, x_ref, zm_ref, meta_ref, g1_ref, g2_ref, cw_ref, pw_ref,
             ps_ref, after_ref, win_all, wout_all, win_s, wout_s, *rest):
        (dx_ref, dz_ref, dm_ref, dg1_ref, dg2_ref, dsc_ref, dcw_ref, dpw_ref, dzm_ref, dmeta_ref, dg1m_ref, am_ref,
         dzmb_ref) = rest[:n_out]
        win_v, wout_v, dcb, dqb, mcb, mqb, load_sems = rest[n_out:]
        s, i = pl.program_id(0), pl.program_id(1)
        tr = n_t - 1 - i

        @pl.when((s == 0) & (i == 0))
        def _():
            _load_gathered([win_all, wout_all], [win_s, wout_s],
                           [lambda j: win_v.at[j], lambda j: wout_v.at[pl.ds(j * OUT_SHARD, OUT_SHARD), :]], load_sems)
            for ref in (dg1_ref, dg2_ref, dsc_ref, dcw_ref, dpw_ref, dzm_ref):
                ref[...] = jnp.zeros_like(ref)

        @pl.when(i == 0)
        def _():
            dcb[tm:tm + HALO, :] = jnp.zeros((HALO, D_CONV), F32)
            dqb[tm:tm + HALO, :] = jnp.zeros((HALO, D_POOL), F32)

        @pl.when(i > 0)
        def _():
            dcb[tm:tm + HALO, :] = dcb[0:HALO, :]
            dqb[tm:tm + HALO, :] = dqb[0:HALO, :]

        g1v, g2v = g1_ref[...], g2_ref[...]
        cw = cw_ref[...]

        for r0 in range(tm - sub, -1, -sub):
            rows = slice(r0, r0 + sub)
            dh1v = dh1_ref[0, rows, :]
            mv = m_ref[0, rows, :]
            r2 = _rstd(mv)
            mh = mv * r2
            dg2_ref[...] += _rows8(dh1v * mh)
            dmb = _rms_bwd(dh1v, mh, r2, g2v).astype(BF16)
            dm_ref[rows, :] = dmb
            dyc = _dot_nt(dmb, wout_v[...])
            dyconv = dyc[:, 0:D_CONV]

            for g in range(N_POOL_GROUPS):
                pooled = pooled_ref[rows, _gcols(g)]
                mixed = _dot(pooled, pw_ref[g])
                scale = ps_ref[:, _gcols(g)]
                dyp = dyc[:, D_CONV + g * POOL_GROUP:D_CONV + (g + 1) * POOL_GROUP]
                dsc_ref[:, _gcols(g)] += _rows8(dyp * mixed)
                dmix = (dyp * scale).astype(BF16)
                dpw_ref[g] += _dot_tn(pooled, dmix)
                dqb[rows, _gcols(g)] = _dot_nt(dmix, pw_ref[g])

            zb = z_ref[0, rows, 0:IN_SHARD]
            zc = z_ref[0, rows, IN_SHARD:2 * IN_SHARD]
            zv = z_ref[0, rows, 2 * IN_SHARD:3 * IN_SHARD]
            dconv = dyconv * zb
            dcb[rows, :] = dconv
            d1 = dcb[r0 + 1:r0 + 1 + sub, :]
            d2 = dcb[r0 + 2:r0 + 2 + sub, :]
            dcv = cw[2:3] * dconv + cw[1:2] * d1 + cw[0:1] * d2
            cv = zc * zv
            dcw_ref[0:8, :] += _rows8(cv * d2)
            dcw_ref[8:16, :] += _rows8(cv * d1)
            dcw_ref[16:24, :] += _rows8(cv * dconv)
            dzs = [(dyconv * conv_ref[rows, :]).astype(BF16), (dcv * zv).astype(BF16), (dcv * zc).astype(BF16),
                   jnp.concatenate([_pool_bwd(dqb, g, r0, sub) for g in range(N_POOL_GROUPS)], axis=1).astype(BF16)]
            da = jnp.zeros((sub, D_MODEL), F32)
            for j in range(N_CHIPS):
                dz_ref[j, rows, :] = dzs[j]
                da = da + _dot_nt(dzs[j], win_v[j])
            xt = x_ref[0, rows, :]
            r1 = _rstd(xt)
            xh = xt * r1
            dg1_ref[...] += _rows8(da * xh)
            dx_ref[0, rows, :] = dh1v + _rms_bwd(da, xh, r1, g1v)

        @pl.when(tr == 0)
        def _():
            mcb[0:HALO, :] = jnp.zeros((HALO, D_CONV), F32)
            mqb[0:HALO, :] = jnp.zeros((HALO, D_POOL), F32)
            mcb[HALO:2 * HALO, :] = dcb[0:HALO, :]
            mqb[HALO:2 * HALO, :] = dqb[0:HALO, :]
            m1 = mcb[1:1 + HALO, :]
            m2 = mcb[2:2 + HALO, :]
            zc_m = zm_ref[:, IN_SHARD:2 * IN_SHARD]
            zv_m = zm_ref[:, 2 * IN_SHARD:3 * IN_SHARD]
            cv_m = zc_m * zv_m
            dcw_ref[0:8, :] += _rows8(cv_m * m2)
            dcw_ref[8:16, :] += _rows8(cv_m * m1)
            dcv_m = cw[1:2] * m1 + cw[0:1] * m2
            dzm_ref[:, IN_SHARD:2 * IN_SHARD] += dcv_m * zv_m
            dzm_ref[:, 2 * IN_SHARD:3 * IN_SHARD] += dcv_m * zc_m
            dzm_ref[:, 3 * IN_SHARD:4 * IN_SHARD] += jnp.concatenate(
                [_pool_bwd(mqb, g, 0, HALO) for g in range(N_POOL_GROUPS)], axis=1)

        @pl.when((s == n_seq - 1) & (i == n_t - 1))
        def _():
            xm = meta_ref[...]
            rm = _rstd(xm)
            xmh = xm * rm
            am_ref[...] = (xmh * g1v).astype(BF16)
            da_m = jnp.zeros((N_META, D_MODEL), F32)
            for j in range(N_CHIPS):
                dzj = dzm_ref[:, j * IN_SHARD:(j + 1) * IN_SHARD].astype(BF16)
                dzmb_ref[j] = dzj
                da_m = da_m + _dot_nt(dzj, win_v[j])
            dg1m_ref[...] = _rows8(da_m * xmh)
            dmeta_ref[...] = _rms_bwd(da_m, xmh, rm, g1v)

    row3 = lambda c: pl.BlockSpec((1, tm, c), lambda s, i: (s, n_t - 1 - i, 0))
    row2 = lambda c: pl.BlockSpec((tm, c), lambda s, i: (s * n_t + n_t - 1 - i, 0))
    n_rows = n_seq * seq
    outs = pl.pallas_call(
        body, name="mixer_bwd", grid=(n_seq, n_t),
        out_shape=[jax.ShapeDtypeStruct((n_seq, seq, D_MODEL), F32),
                   jax.ShapeDtypeStruct((N_CHIPS, n_rows, IN_SHARD), BF16), jax.ShapeDtypeStruct((n_rows, D_MODEL), BF16),
                   jax.ShapeDtypeStruct((8, D_MODEL), F32), jax.ShapeDtypeStruct((8, D_MODEL), F32),
                   jax.ShapeDtypeStruct((8, D_POOL), F32), jax.ShapeDtypeStruct((24, D_CONV), F32),
                   jax.ShapeDtypeStruct((N_POOL_GROUPS, POOL_GROUP, POOL_GROUP), F32),
                   jax.ShapeDtypeStruct((N_META, D_IN_PROJ), F32),
                   jax.ShapeDtypeStruct((N_META, D_MODEL), F32), jax.ShapeDtypeStruct((8, D_MODEL), F32),
                   jax.ShapeDtypeStruct((N_META, D_MODEL), BF16),
                   jax.ShapeDtypeStruct((N_CHIPS, N_META, IN_SHARD), BF16)],
        in_specs=[row3(D_MODEL), row3(D_MODEL), row3(D_Z), row2(D_CONV), row2(D_POOL), row3(D_MODEL),
                  _full((N_META, D_IN_PROJ)), _full((N_META, D_MODEL)), _full((1, D_MODEL)), _full((1, D_MODEL)),
                  _full((3, D_CONV)), _full((N_POOL_GROUPS, POOL_GROUP, POOL_GROUP)), _full((1, D_POOL)),
                  _full((8, 128))] + [ANY] * 4,
        out_specs=[row3(D_MODEL), pl.BlockSpec((N_CHIPS, tm, IN_SHARD), lambda s, i: (0, s * n_t + n_t - 1 - i, 0)),
                   row2(D_MODEL),
                   _full((8, D_MODEL)), _full((8, D_MODEL)), _full((8, D_POOL)), _full((24, D_CONV)),
                   _full((N_POOL_GROUPS, POOL_GROUP, POOL_GROUP)), _full((N_META, D_IN_PROJ)),
                   _full((N_META, D_MODEL)), _full((8, D_MODEL)), _full((N_META, D_MODEL)),
                   _full((N_CHIPS, N_META, IN_SHARD))],
        scratch_shapes=[pltpu.VMEM((N_CHIPS, D_MODEL, IN_SHARD), BF16), pltpu.VMEM((D_MODEL, D_MODEL), BF16),
                        pltpu.VMEM((tm + HALO, D_CONV), F32), pltpu.VMEM((tm + HALO, D_POOL), F32),
                        pltpu.VMEM((2 * HALO, D_CONV), F32), pltpu.VMEM((2 * HALO, D_POOL), F32),
                        pltpu.SemaphoreType.DMA((2 * N_CHIPS,))],
        compiler_params=_cparams(2),
    )(dh1, m3, z3, conv2, pooled2, x3, zmeta, meta_full, g1, g2, convw, poolw, pscale, after, *gathered, *shards)
    return outs


def _mixer_weight_grads(a, dz, ycat, dm, a_meta, dz_meta, ffn_sums, small):
    n_rows = a.shape[0]
    tk = min(TK_DW, n_rows)
    n_k = n_rows // tk
    n_sc, n_sm = len(ffn_sums), _AllReduceSmall.N_IN

    def body(a_ref, dz_ref, yc_ref, dm_ref, am_ref, dzm_ref, *rest):
        ins, outs, scratch = rest[:n_sc + n_sm], rest[n_sc + n_sm:2 * n_sc + n_sm + 5], rest[2 * n_sc + n_sm + 5:]
        dwin_ref, dwout_ref = outs[:2]
        scatter = _ScatterToChips(ins[:n_sc], outs[2:2 + n_sc], *scratch[:2])
        reduce_small = _AllReduceSmall(ins[n_sc:], outs[2 + n_sc:], scratch[2:])
        k = pl.program_id(0)

        @pl.when(k == 0)
        def _():
            scatter.start()
            reduce_small.pack_and_send()
            am_t = am_ref[...].T
            for j in range(N_CHIPS):
                dwin_ref[j] = _dot(am_t, dzm_ref[j])
            dwout_ref[...] = jnp.zeros_like(dwout_ref)

        for st in range(2):
            @pl.when(k == ((st + 1) * n_k) // 3)
            def _():
                reduce_small.combine(st)

        a_t = a_ref[...].T
        for j in range(N_CHIPS):
            dwin_ref[j] += _dot(a_t, dz_ref[j])
        dwout_ref[...] += _dot_tn(yc_ref[...], dm_ref[...])

        @pl.when(k == n_k - 1)
        def _():
            reduce_small.combine(2)
            scatter.finish()

    row = pl.BlockSpec((tk, D_MODEL), lambda k: (k, 0))
    outs = pl.pallas_call(
        body, name="mixer_weight_grads", grid=(n_k,),
        out_shape=[jax.ShapeDtypeStruct((N_CHIPS, D_MODEL, IN_SHARD), F32),
                   jax.ShapeDtypeStruct((D_MODEL, D_MODEL), F32)] + _ScatterToChips.out_shape(ffn_sums)
        + _AllReduceSmall.out_shape(),
        in_specs=[row, pl.BlockSpec((N_CHIPS, tk, IN_SHARD), lambda k: (0, k, 0)), row, row,
                  _full((N_META, D_MODEL)), _full((N_CHIPS, N_META, IN_SHARD))] + [ANY] * n_sc
        + [_full(s.shape) for s in small],
        out_specs=[_full((N_CHIPS, D_MODEL, IN_SHARD)), _full((D_MODEL, D_MODEL))] + [ANY] * n_sc
        + [_full(s) for s in _AllReduceSmall.SHAPES],
        scratch_shapes=_ScatterToChips.scratch(n_sc) + _AllReduceSmall.scratch(),
        compiler_params=_cparams(1),
    )(a, dz, ycat, dm, a_meta, dz_meta, *ffn_sums, *small)
    return ([outs[0], outs[1].reshape(N_CHIPS, OUT_SHARD, D_MODEL)], outs[2:2 + n_sc], outs[2 + n_sc:])


def kernel(x, meta_tokens, norm_mix_pre, w_in, conv_w, pool_w, pool_scale, w_out, norm_mix_post, norm_ffn_pre, w_gate, w_up, w_down, norm_ffn_post, loss_target, m_meta_tokens, m_norm_mix_pre, m_w_in, m_conv_w, m_pool_w, m_pool_scale, m_w_out, m_norm_mix_post, m_norm_ffn_pre, m_w_gate, m_w_up, m_w_down, m_norm_ffn_post, v_meta_tokens, v_norm_mix_pre, v_w_in, v_conv_w, v_pool_w, v_pool_scale, v_w_out, v_norm_mix_post, v_norm_ffn_pre, v_w_gate, v_w_up, v_w_down, v_norm_ffn_post):
    n_seq, seq, _ = x.shape
    n_rows = n_seq * seq
    chip = 2 * lax.axis_index("x") + lax.axis_index("y")
    meta_cols = D_MODEL // N_CHIPS
    conv_cols = D_CONV // N_CHIPS

    small = jnp.zeros((2 * HALO, meta_cols), F32)
    small = small.at[0:N_META, :].set(meta_tokens).at[N_META:N_META + 3, 0:conv_cols].set(conv_w[0])
    poolw_bf = pool_w[0].astype(BF16)
    pscale = pool_scale
    g1, g2, g3, g4 = norm_mix_pre, norm_mix_post, norm_ffn_pre, norm_ffn_post
    place = jnp.stack([chip, lax.axis_index("c")]).astype(jnp.int32)

    mix_shards = [w_in[0].astype(BF16), w_out[0].astype(BF16)]
    ffn_shards = [w_gate[0].T.astype(BF16), w_up[0].T.astype(BF16), w_down[0].astype(BF16)]
    ((z3, m3, h1, a_bf, conv2, pooled2, yc_bf, zmeta, meta_full, conv_full),
     (win_all, wout_all, _, *ffn_gathered)) = _mixer_fwd(x, g1, g2, poolw_bf, pscale, mix_shards + [small] + ffn_shards)
    dh1, f_bf, dd_bf, ds_bf, du_bf, gg_bf, lossp, dg3p, dg4p = _ffn_fwd_bwd(
        h1.reshape(n_rows, D_MODEL), loss_target.reshape(n_rows, D_MODEL), g3, g4, ffn_gathered, ffn_shards)
    as_shards = lambda g: g.reshape(N_CHIPS, FF_SHARD, D_MODEL)
    (dwg_t, dwu_t), _ = _ffn_weight_grads("ffn_weight_grads_gate_up", [ds_bf, du_bf], f_bf, [])
    dwg_t, dwu_t = as_shards(dwg_t), as_shards(dwu_t)
    (dwd,), (dwg_recv, dwu_recv) = _ffn_weight_grads("ffn_weight_grads_down", [gg_bf], dd_bf, [dwg_t, dwu_t])
    dwd = as_shards(dwd)
    behind_bwd = _SplitComm("grad_comm_behind_mixer_bwd", [dwd],
                            _add_pairs_multi([dwg_t, dwu_t], [dwg_recv, dwu_recv], place))
    (grad_x, dz_bf, dm_bf, dg1p, dg2p, dscp, dcwp, dpw, _, dmeta, dg1m, a_meta, dz_meta) = _mixer_bwd(
        dh1.reshape(n_seq, seq, D_MODEL), m3, z3, conv2, pooled2, x, zmeta, meta_full, g1, g2, conv_full, poolw_bf,
        pscale, [win_all, wout_all], mix_shards, behind_bwd.start())
    (dwd_recv,), (dwg_rbuf, dwu_rbuf) = behind_bwd.wait(dg2p)
    (dwd,) = behind_bwd.exchanged
    mix_grads, (dwd_rbuf,), (a_red, b_red, c_red) = _mixer_weight_grads(
        a_bf, dz_bf, yc_bf, dm_bf, a_meta, dz_meta, [_add_pairs(dwd, dwd_recv, place)],
        [dg1p, dg1m, dg2p, dg3p, dg4p, lossp, dmeta, dscp, dcwp, dpw.reshape(SMALL_C_ROWS, POOL_GROUP)])

    behind_sums = _SplitComm("grad_comm_behind_ffn_sums", mix_grads, [])
    ffn_red = _add_chips([dwg_t, dwu_t, dwd], [dwg_recv, dwu_recv, dwd_recv], [dwg_rbuf, dwu_rbuf, dwd_rbuf],
                         place, after=behind_sums.start(), name="grad_add_chips_ffn")
    mix_recvs, _ = behind_sums.wait(ffn_red[0])
    mix_grads = behind_sums.exchanged
    behind_tail = _SplitComm("grad_comm_behind_ffn_tail", [], _add_pairs_multi(mix_grads, mix_recvs, place))
    as_full = lambda r: r.reshape(2 * r.shape[1], r.shape[2])
    g_wg_t, g_wu_t, g_wd = [as_full(r) for r in _gather_halves(list(ffn_red), "grad_gather_halves_ffn",
                                                                 after=behind_tail.start())]
    ffn_out = _adamw_big([(w_gate[0].T, g_wg_t, m_w_gate[0].T, v_w_gate[0].T),
                          (w_up[0].T, g_wu_t, m_w_up[0].T, v_w_up[0].T), (w_down[0], g_wd, m_w_down[0], v_w_down[0])])

    loss = a_red[4, 0]
    g_g1, g_g2, g_g3, g_g4 = a_red[0:1], a_red[1:2], a_red[2:3], a_red[3:4]
    g_meta = lax.dynamic_slice(a_red, (8, chip * meta_cols), (N_META, meta_cols))
    g_pscale = b_red[0:1]
    g_conv = lax.dynamic_slice(b_red, (1, chip * conv_cols), (3, conv_cols))
    g_poolw = c_red

    small_groups = [
        (meta_tokens, g_meta, m_meta_tokens, v_meta_tokens),
        (g1, g_g1, m_norm_mix_pre, v_norm_mix_pre),
        (conv_w[0], g_conv, m_conv_w[0], v_conv_w[0]),
        (pool_w.reshape(SMALL_C_ROWS, POOL_GROUP), g_poolw, m_pool_w.reshape(SMALL_C_ROWS, POOL_GROUP),
         v_pool_w.reshape(SMALL_C_ROWS, POOL_GROUP)),
        (pool_scale, g_pscale, m_pool_scale, v_pool_scale),
        (g2, g_g2, m_norm_mix_post, v_norm_mix_post),
        (g3, g_g3, m_norm_ffn_pre, v_norm_ffn_pre),
        (g4, g_g4, m_norm_ffn_post, v_norm_ffn_post),
    ]
    small_out = _adamw_small(small_groups)
    _, mix_rbufs = behind_tail.wait(ffn_out[2][0], small_out[0][0])
    mix_red = _add_chips(mix_grads, mix_recvs, mix_rbufs, place)
    g_win, g_wout = [as_full(r) for r in _gather_halves(list(mix_red), "grad_gather_halves_mixer")]
    big_out = (_adamw_big([(w_in[0], g_win, m_w_in[0], v_w_in[0])])
               + _adamw_big([(w_out[0], g_wout, m_w_out[0], v_w_out[0])]) + ffn_out)
    big_out[2] = [o.T for o in big_out[2]]
    big_out[3] = [o.T for o in big_out[3]]
    g_wg, g_wu = g_wg_t.T, g_wu_t.T

    grads_out = [g_meta, g_g1, g_win[None], g_conv[None], g_poolw.reshape(pool_w.shape), g_pscale, g_wout[None],
                 g_g2, g_g3, g_wg[None], g_wu[None], g_wd[None], g_g4]
    s_meta, s_g1, s_conv, s_poolw, s_pscale, s_g2, s_g3, s_g4 = small_out
    b_win, b_wout, b_wg, b_wu, b_wd = big_out

    def leaf(k):
        return [s_meta[k], s_g1[k], b_win[k][None], s_conv[k][None], s_poolw[k].reshape(pool_w.shape), s_pscale[k],
                b_wout[k][None], s_g2[k], s_g3[k], b_wg[k][None], b_wu[k][None], b_wd[k][None], s_g4[k]]

    return (loss, grad_x, *grads_out, *leaf(0), *leaf(1), *leaf(2))
```

```python
import functools

import jax
import jax.numpy as jnp
from jax import lax
from jax.experimental import pallas as pl
from jax.experimental.pallas import tpu as pltpu

F32 = jnp.float32
BF16 = jnp.bfloat16
MESH = pl.DeviceIdType.MESH

D_MODEL = 1024
D_CONV = 512
D_POOL = 512
POOL_GROUP = 128
N_POOL_GROUPS = 4
D_IN_PROJ = 2048
D_FF = 2816
N_CHIPS = 4
FF_SHARD = D_FF // N_CHIPS
IN_SHARD = D_IN_PROJ // N_CHIPS
OUT_SHARD = D_MODEL // N_CHIPS
D_Z = 3 * IN_SHARD
N_META = 16
HALO = 16
RMS_EPS = 1e-6

ADAM_LR = 0.001
ADAM_B1 = 0.9
ADAM_B2 = 0.999
ADAM_EPS = 1e-08
ADAM_WD = 0.01
ADAM_STEP = 10

TM_MIX_FWD = 512
TM_MIX_BWD = 512
SUB_MIX_BWD = 512
TM_FFN = 256
TK_DW = 1024
FF_CHUNK = 1024
VMEM_LIMIT = 56 * 1024 * 1024


def _cparams(n_grid):
    return pltpu.CompilerParams(dimension_semantics=("arbitrary",) * n_grid, vmem_limit_bytes=VMEM_LIMIT)


def _dot(a, b):
    return jnp.dot(a, b, preferred_element_type=F32)


def _dot_nt(a, b):
    return lax.dot_general(a, b, (((1,), (1,)), ((), ())), preferred_element_type=F32)


def _dot_tn(a, b):
    return lax.dot_general(a, b, (((0,), (0,)), ((), ())), preferred_element_type=F32)


def _rows8(v):
    r, c = v.shape
    return v.reshape(r // 8, 8, c).sum(axis=0)


def _rstd(v):
    return lax.rsqrt(jnp.mean(v * v, axis=-1, keepdims=True) + RMS_EPS)


def _rms_bwd(dy, xhat, rstd, gain):
    dyg = dy * gain
    return rstd * (dyg - xhat * jnp.mean(dyg * xhat, axis=-1, keepdims=True))


def _sigmoid(v):
    return 1.0 / (1.0 + jnp.exp(-v))


def _gcols(g):
    return slice(g * POOL_GROUP, (g + 1) * POOL_GROUP)


def _window_sum(e, g, ahead):
    n = e.shape[0]
    w = e
    for level in range(g + 1):
        shift = 1 << level
        w = w + pltpu.roll(w, (n - shift) if ahead else shift, 0)
    return w


def _pool_fwd(pb, g, n):
    e = pb[0:HALO + n, _gcols(g)]
    return _window_sum(e, g, False)[HALO:, :] * (1.0 / (2 << g)) - e[HALO:, :]


def _pool_bwd(qb, g, r0, n):
    e = qb[r0:r0 + n + HALO, _gcols(g)]
    return _window_sum(e, g, True)[0:n, :] * (1.0 / (2 << g)) - e[0:n, :]


def _full(shape):
    nd = len(shape)
    return pl.BlockSpec(shape, lambda *_: (0,) * nd)


ANY = pl.BlockSpec(memory_space=pl.ANY)


def _mesh_pos():
    x, y, c = lax.axis_index("x"), lax.axis_index("y"), lax.axis_index("c")
    chips = [(1 - x, y), (x, 1 - y), (1 - x, 1 - y)]
    return x, y, c, chips


def _half(ref, h):
    hr = ref.shape[0] // 2
    return ref.at[pl.ds(h * hr, hr), :]


class _AllGather:
    PER_ARRAY = 9

    def __init__(self, ins, outs, send_sems, recv_sems):
        self.ins, self.outs, self.send_sems, self.recv_sems = ins, outs, send_sems, recv_sems
        self.n = len(ins)

    @classmethod
    def scratch(cls, n):
        return [pltpu.SemaphoreType.DMA((cls.PER_ARRAY * n,)), pltpu.SemaphoreType.DMA((cls.PER_ARRAY * n,))]

    @staticmethod
    def out_shape(shards):
        return [jax.ShapeDtypeStruct((N_CHIPS,) + s.shape, s.dtype) for s in shards]

    def _copy(self, a, k, src, dst, to):
        i = self.PER_ARRAY * a + k
        return pltpu.make_async_remote_copy(src_ref=src, dst_ref=dst, send_sem=self.send_sems.at[i],
                                            recv_sem=self.recv_sems.at[i], device_id=to, device_id_type=MESH)

    def _piece(self, a, chip, piece, h=None):
        h = lax.axis_index("c") if h is None else h
        rows = self.ins[a].shape[0] // 4
        return self.outs[a].at[chip].at[pl.ds((2 * h + piece) * rows, rows), :]

    def _own(self, a, k):
        x, y, c, chips = _mesh_pos()
        piece = (1, 0, 0, 1)[k]
        rows = self.ins[a].shape[0] // 4
        src = self.ins[a].at[pl.ds((2 * c + piece) * rows, rows), :]
        return self._copy(a, k, src, self._piece(a, 2 * x + y, piece), (*chips[k // 2], c))

    def _relay(self, a, k):
        x, y, c, chips = _mesh_pos()
        source, to, piece = (chips[1], chips[0], 0) if k == 4 else (chips[0], chips[1], 1)
        rows = self._piece(a, 2 * source[0] + source[1], piece)
        return self._copy(a, k, rows, rows, (*to, c))

    def _sibling(self, a, k, h):
        x, y, c, chips = _mesh_pos()
        chip = chips[k - 6]
        slot = _half(self.outs[a].at[2 * chip[0] + chip[1]], h)
        return self._copy(a, k, slot, slot, (x, y, 1 - c))

    def start(self, arrays=None):
        for a in (range(self.n) if arrays is None else arrays):
            for k in range(4):
                self._own(a, k).start()

    def relay(self, a):
        self._own(a, 2).wait_recv()
        self._relay(a, 4).start()
        self._own(a, 0).wait_recv()
        self._relay(a, 5).start()

    def forward(self, a):
        c = lax.axis_index("c")
        self._own(a, 1).wait_recv()
        self._sibling(a, 6, c).start()
        self._own(a, 3).wait_recv()
        self._sibling(a, 7, c).start()
        self._relay(a, 4).wait_recv()
        self._relay(a, 5).wait_recv()
        self._sibling(a, 8, c).start()

    def finish(self, arrays=None):
        c = lax.axis_index("c")
        arrays = range(self.n) if arrays is None else arrays
        for a in arrays:
            for k in range(6, 9):
                self._sibling(a, k, 1 - c).wait_recv()
        for a in arrays:
            for k in range(4):
                self._own(a, k).wait_send()
            for k in range(4, 6):
                self._relay(a, k).wait_send()
            for k in range(6, 9):
                self._sibling(a, k, c).wait_send()


class _ExchangeHalves:
    def __init__(self, ins, recvs, send_sems, recv_sems):
        self.ins, self.recvs, self.send_sems, self.recv_sems = ins, recvs, send_sems, recv_sems

    @staticmethod
    def scratch(n):
        return [pltpu.SemaphoreType.DMA((n,)), pltpu.SemaphoreType.DMA((n,))]

    @staticmethod
    def out_shape(grads):
        return [jax.ShapeDtypeStruct((g.shape[0], g.shape[1] // 2, g.shape[2]), g.dtype) for g in grads]

    def _copies(self):
        x, y, c, _ = _mesh_pos()
        out = []
        for a, (src, dst) in enumerate(zip(self.ins, self.recvs)):
            hr = src.shape[1] // 2
            out.append(pltpu.make_async_remote_copy(
                src_ref=src.at[:, pl.ds((1 - c) * hr, hr), :], dst_ref=dst, send_sem=self.send_sems.at[a],
                recv_sem=self.recv_sems.at[a], device_id=(x, y, 1 - c), device_id_type=MESH))
        return out

    def start(self):
        for cp in self._copies():
            cp.start()

    def finish(self):
        for cp in self._copies():
            cp.wait()


def _exchange_halves(grads):
    n = len(grads)

    def body(*refs):
        ex = _ExchangeHalves(refs[:n], refs[n:2 * n], *refs[2 * n:])
        ex.start()
        ex.finish()

    return pl.pallas_call(
        body, name="grad_exchange_halves", out_shape=_ExchangeHalves.out_shape(grads),
        in_specs=[ANY] * n, out_specs=[ANY] * n, scratch_shapes=_ExchangeHalves.scratch(n),
    )(*grads)


class _ScatterToChips:
    def __init__(self, ins, rbufs, send_sems, recv_sems):
        self.ins, self.rbufs, self.send_sems, self.recv_sems = ins, rbufs, send_sems, recv_sems

    @staticmethod
    def scratch(n):
        return [pltpu.SemaphoreType.DMA((3 * n,)), pltpu.SemaphoreType.DMA((3 * n,))]

    @staticmethod
    def out_shape(sums):
        return [jax.ShapeDtypeStruct((3,) + s.shape[1:], BF16) for s in sums]

    def _copies(self):
        x, y, c, chips = _mesh_pos()
        out = []
        for a, (src, dst) in enumerate(zip(self.ins, self.rbufs)):
            for k, chip in enumerate(chips):
                out.append(pltpu.make_async_remote_copy(
                    src_ref=src.at[2 * chip[0] + chip[1]], dst_ref=dst.at[k], send_sem=self.send_sems.at[3 * a + k],
                    recv_sem=self.recv_sems.at[3 * a + k], device_id=(*chip, c), device_id_type=MESH))
        return out

    def start(self):
        for cp in self._copies():
            cp.start()

    def finish(self):
        for cp in self._copies():
            cp.wait()


HBM = pl.BlockSpec(memory_space=pltpu.HBM)
SEM = pl.BlockSpec(memory_space=pltpu.SEMAPHORE)


class _SplitComm:
    def __init__(self, name, exchanged, scattered):
        self.name, self.n_ex, self.n_sc = name, len(exchanged), len(scattered)
        self.n_copies = self.n_ex + 3 * self.n_sc
        zones = ([lax.empty((g.shape[0], g.shape[1] // 2, g.shape[2]), g.dtype) for g in exchanged]
                 + [lax.empty((3,) + s.shape[1:], s.dtype) for s in scattered])
        self.buffers = [pltpu.with_memory_space_constraint(v, pltpu.HBM)
                        for v in list(exchanged) + list(scattered) + zones]

    def _copies(self, bufs, send_sems, recv_sems):
        x, y, c, chips = _mesh_pos()
        n_src = self.n_ex + self.n_sc
        out = []
        for a in range(self.n_ex):
            hr = bufs[a].shape[1] // 2
            out.append(pltpu.make_async_remote_copy(
                src_ref=bufs[a].at[:, pl.ds((1 - c) * hr, hr), :], dst_ref=bufs[n_src + a], send_sem=send_sems[a],
                recv_sem=recv_sems[a], device_id=(x, y, 1 - c), device_id_type=MESH))
        for a in range(self.n_sc):
            for k, chip in enumerate(chips):
                i = self.n_ex + 3 * a + k
                out.append(pltpu.make_async_remote_copy(
                    src_ref=bufs[self.n_ex + a].at[2 * chip[0] + chip[1]], dst_ref=bufs[n_src + self.n_ex + a].at[k],
                    send_sem=send_sems[i], recv_sem=recv_sems[i], device_id=(*chip, c), device_id_type=MESH))
        return out

    def start(self):
        n_buf, n_cp = len(self.buffers), self.n_copies

        def body(*refs):
            bufs = refs[:n_buf]
            send_sems, recv_sems = refs[n_buf:n_buf + n_cp], refs[n_buf + n_cp:n_buf + 2 * n_cp]
            for cp in self._copies(bufs, send_sems, recv_sems):
                cp.start()
            refs[-1][...] = jnp.zeros_like(refs[-1])

        outs = pl.pallas_call(
            body, name=self.name + "_start",
            out_shape=[pltpu.SemaphoreType.DMA(())] * (2 * n_cp) + [pltpu.HBM(b.shape, b.dtype) for b in self.buffers]
            + [jax.ShapeDtypeStruct((8, 128), F32)],
            in_specs=[HBM] * n_buf, out_specs=[SEM] * (2 * n_cp) + [HBM] * n_buf + [pl.BlockSpec(memory_space=pltpu.VMEM)],
            input_output_aliases={i: 2 * n_cp + i for i in range(n_buf)},
            compiler_params=pltpu.CompilerParams(has_side_effects=pltpu.SideEffectType.DATAFLOW_SIDE_EFFECTING),
        )(*self.buffers)
        self.sems, self.buffers = outs[:2 * n_cp], outs[2 * n_cp:2 * n_cp + n_buf]
        return outs[-1]

    def wait(self, *after):
        n_buf, n_cp = len(self.buffers), self.n_copies

        def body(*refs):
            bufs = refs[:n_buf]
            send_sems, recv_sems = refs[n_buf:n_buf + n_cp], refs[n_buf + n_cp:n_buf + 2 * n_cp]
            for cp in self._copies(bufs, send_sems, recv_sems):
                cp.wait_send()
                cp.wait_recv()

        outs = pl.pallas_call(
            body, name=self.name + "_wait", out_shape=[pltpu.HBM(b.shape, b.dtype) for b in self.buffers],
            in_specs=[HBM] * n_buf + [SEM] * (2 * n_cp) + [ANY] * len(after), out_specs=[HBM] * n_buf,
            input_output_aliases={i: i for i in range(n_buf)},
            compiler_params=pltpu.CompilerParams(has_side_effects=pltpu.SideEffectType.DATAFLOW_SIDE_EFFECTING),
        )(*self.buffers, *self.sems, *after)
        self.exchanged = outs[:self.n_ex]
        zones = outs[self.n_ex + self.n_sc:]
        return zones[:self.n_ex], zones[self.n_ex:]


def _gather_halves(halves, name, after=None):
    n = len(halves)
    extra = [] if after is None else [after]

    def body(*refs):
        ins, outs = refs[:n], refs[n + len(extra):2 * n + len(extra)]
        send_sems, recv_sems = refs[2 * n + len(extra):]
        x, y, c, _ = _mesh_pos()
        sib = (x, y, 1 - c)
        remote = [pltpu.make_async_remote_copy(src_ref=ins[a].at[c], dst_ref=outs[a].at[c],
                                               send_sem=send_sems.at[a], recv_sem=recv_sems.at[a],
                                               device_id=sib, device_id_type=MESH) for a in range(n)]
        for cp in remote:
            cp.start()
        for a in range(n):
            pltpu.make_async_remote_copy(src_ref=ins[a].at[1 - c], dst_ref=outs[a].at[1 - c], send_sem=send_sems.at[a],
                                         recv_sem=recv_sems.at[a], device_id=sib, device_id_type=MESH).wait_recv()
        for cp in remote:
            cp.wait_send()

    return pl.pallas_call(
        body, name=name,
        out_shape=[jax.ShapeDtypeStruct(h.shape, F32) for h in halves],
        in_specs=[ANY] * (n + len(extra)), out_specs=[ANY] * n, input_output_aliases={a: a for a in range(n)},
        scratch_shapes=[pltpu.SemaphoreType.DMA((n,)), pltpu.SemaphoreType.DMA((n,))],
    )(*halves, *extra)


SMALL_A_ROWS = 24
SMALL_B_ROWS = 8
SMALL_C_ROWS = N_POOL_GROUPS * POOL_GROUP


class _AllReduceSmall:
    N_IN = 10
    SHAPES = [(SMALL_A_ROWS, D_MODEL), (SMALL_B_ROWS, D_CONV), (SMALL_C_ROWS, POOL_GROUP)]

    def __init__(self, ins, outs, scratch):
        self.ins, self.outs = ins, outs
        self.bufs, self.rcvs, self.send_sems, self.recv_sems = scratch[:3], scratch[3:6], scratch[6], scratch[7]

    @classmethod
    def scratch(cls):
        return ([pltpu.VMEM((3,) + s, F32) for s in cls.SHAPES] + [pltpu.VMEM((3,) + s, F32) for s in cls.SHAPES]
                + [pltpu.SemaphoreType.DMA((9,)), pltpu.SemaphoreType.DMA((9,))])

    @classmethod
    def out_shape(cls):
        return [jax.ShapeDtypeStruct(s, F32) for s in cls.SHAPES]

    def _copies(self, st):
        x, y, c, _ = _mesh_pos()
        peer = [(x, y, 1 - c), (1 - x, y, c), (x, 1 - y, c)][st]
        return [pltpu.make_async_remote_copy(
            src_ref=buf.at[st], dst_ref=rcv.at[st], send_sem=self.send_sems.at[3 * st + i],
            recv_sem=self.recv_sems.at[3 * st + i], device_id=peer, device_id_type=MESH)
            for i, (buf, rcv) in enumerate(zip(self.bufs, self.rcvs))]

    def pack_and_send(self):
        dg1_ref, dg1m_ref, dg2_ref, dg3_ref, dg4_ref, loss_ref, dmeta_ref, dsc_ref, dcw_ref, dpw_ref = self.ins
        a_buf, b_buf, c_buf = self.bufs

        def rowsum(v):
            return jnp.sum(v, axis=0, keepdims=True)

        a_buf[0, 0:1, :] = rowsum(dg1_ref[...] + dg1m_ref[...])
        a_buf[0, 1:2, :] = rowsum(dg2_ref[...])
        a_buf[0, 2:3, :] = rowsum(dg3_ref[...])
        a_buf[0, 3:4, :] = rowsum(dg4_ref[...])
        loss = jnp.sum(rowsum(loss_ref[...]), axis=1, keepdims=True) * (0.5 / D_MODEL)
        a_buf[0, 4:5, :] = jnp.broadcast_to(loss, (1, D_MODEL))
        a_buf[0, 5:8, :] = jnp.zeros((3, D_MODEL), F32)
        a_buf[0, 8:24, :] = dmeta_ref[...]
        b_buf[0, 0:1, :] = rowsum(dsc_ref[...])
        for k in range(3):
            b_buf[0, 1 + k:2 + k, :] = rowsum(dcw_ref[8 * k:8 * k + 8, :])
        b_buf[0, 4:8, :] = jnp.zeros((4, D_CONV), F32)
        c_buf[0] = dpw_ref[...]
        for cp in self._copies(0):
            cp.start()

    def combine(self, st):
        for cp in self._copies(st):
            cp.wait()
        if st < 2:
            for buf, rcv in zip(self.bufs, self.rcvs):
                buf[st + 1] = buf[st] + rcv[st]
            for cp in self._copies(st + 1):
                cp.start()
        else:
            for out, buf, rcv in zip(self.outs, self.bufs, self.rcvs):
                out[...] = buf[st] + rcv[st]


def _row_block(rows):
    for cand in (512, 448, 384, 352, 320, 256, 128, 64, 32, 16):
        if rows % cand == 0:
            return cand
    return rows


def _add_pairs_multi(grads, recvs, place):
    n = len(grads)
    n_sh = grads[0].shape[0]
    halves = [g.shape[1] // 2 for g in grads]
    n_steps = halves[0] // _row_block(halves[0])
    blocks = [(hr // n_steps, g.shape[2]) for hr, g in zip(halves, grads)]

    def body(place_ref, *refs):
        for a_ref, b_ref, o_ref in zip(refs[:n], refs[n:2 * n], refs[2 * n:]):
            o_ref[...] = (a_ref[0] + b_ref[...]).astype(BF16)

    return pl.pallas_call(
        body, name="grad_add_pairs",
        grid_spec=pltpu.PrefetchScalarGridSpec(
            num_scalar_prefetch=1, grid=(n_sh, n_steps),
            in_specs=[pl.BlockSpec((1, 1, br, cols), lambda j, i, p: (j, p[1], i, 0)) for br, cols in blocks]
            + [pl.BlockSpec((1, br, cols), lambda j, i, p: (j, i, 0)) for br, cols in blocks],
            out_specs=[pl.BlockSpec((1, br, cols), lambda j, i, p: (j, i, 0)) for br, cols in blocks]),
        out_shape=[jax.ShapeDtypeStruct((n_sh, hr, g.shape[2]), BF16) for hr, g in zip(halves, grads)],
        compiler_params=_cparams(2),
    )(place, *[g.reshape(n_sh, 2, hr, g.shape[2]) for hr, g in zip(halves, grads)], *recvs)


def _add_pairs(grad, recv, place):
    return _add_pairs_multi([grad], [recv], place)[0]


def _add_chips(grads, recvs, rbufs, place, after=None, name="grad_add_chips"):
    n = len(grads)
    n_sh = grads[0].shape[0]
    halves = [g.shape[1] // 2 for g in grads]
    n_steps = halves[0] // _row_block(halves[0])
    blocks = [(hr // n_steps, g.shape[2]) for hr, g in zip(halves, grads)]
    extra = [] if after is None else [after]

    def body(place_ref, *refs):
        for a_ref, b_ref, r_ref, o_ref in zip(refs[:n], refs[n:2 * n], refs[2 * n:3 * n], refs[3 * n + len(extra):]):
            own = a_ref[0, 0] + b_ref[0]
            o_ref[0] = ((own + r_ref[0].astype(F32)) + r_ref[1].astype(F32)) + r_ref[2].astype(F32)

    return pl.pallas_call(
        body, name=name,
        grid_spec=pltpu.PrefetchScalarGridSpec(
            num_scalar_prefetch=1, grid=(n_steps,),
            in_specs=[pl.BlockSpec((1, 1, br, cols), lambda i, p: (p[0], p[1], i, 0)) for br, cols in blocks]
            + [pl.BlockSpec((1, br, cols), lambda i, p: (p[0], i, 0)) for br, cols in blocks]
            + [pl.BlockSpec((3, br, cols), lambda i, p: (0, i, 0)) for br, cols in blocks]
            + [pl.BlockSpec((8, 128), lambda i, p: (0, 0))] * len(extra),
            out_specs=[pl.BlockSpec((1, br, cols), lambda i, p: (p[1], i, 0)) for br, cols in blocks]),
        out_shape=[jax.ShapeDtypeStruct((2, hr, g.shape[2]), F32) for hr, g in zip(halves, grads)],
        compiler_params=_cparams(1),
    )(place, *[g.reshape(n_sh, 2, hr, g.shape[2]) for hr, g in zip(halves, grads)], *recvs, *rbufs, *extra)


def _adamw_math(w, g, m, v):
    m2 = ADAM_B1 * m + (1.0 - ADAM_B1) * g
    v2 = ADAM_B2 * v + (1.0 - ADAM_B2) * (g * g)
    m_hat = m2 / (1.0 - ADAM_B1 ** ADAM_STEP)
    v_hat = v2 / (1.0 - ADAM_B2 ** ADAM_STEP)
    delta = -ADAM_LR * (m_hat / (jnp.sqrt(v_hat) + ADAM_EPS) + ADAM_WD * w)
    return delta, m2, v2


def _adamw_big(groups):
    n = len(groups)
    rows, cols = groups[0][0].shape
    br = _row_block(rows)
    if n > 1 and br % 16 == 0:
        br //= 2

    def body(*refs):
        for i in range(n):
            w_ref, g_ref, m_ref, v_ref = refs[4 * i:4 * i + 4]
            g_out_ref, d_ref, m2_ref, v2_ref = refs[4 * n + 4 * i:4 * n + 4 * i + 4]
            g = g_ref[...]
            d, m2, v2 = _adamw_math(w_ref[...], g, m_ref[...], v_ref[...])
            g_out_ref[...] = g
            d_ref[...] = d
            m2_ref[...] = m2
            v2_ref[...] = v2

    spec = pl.BlockSpec((br, cols), lambda i: (i, 0))
    outs = pl.pallas_call(
        body, name="adamw_big", grid=(rows // br,),
        out_shape=[jax.ShapeDtypeStruct((rows, cols), F32)] * (4 * n),
        in_specs=[spec] * (4 * n), out_specs=[spec] * (4 * n), compiler_params=_cparams(1),
    )(*[a for grp in groups for a in grp])
    return [list(outs[4 * i:4 * i + 4]) for i in range(n)]


def _adamw_small(place, reduced, params):
    n = len(params)
    meta_cols, conv_cols = D_MODEL // N_CHIPS, D_CONV // N_CHIPS

    def body(place_ref, a_ref, b_ref, c_ref, *refs):
        ins, loss_ref, outs = refs[:3 * n], refs[3 * n], refs[3 * n + 1:]
        chip = place_ref[0]

        def own_cols(ref, r0, n_r, width):
            out = ref[r0:r0 + n_r, 0:width]
            for j in range(1, N_CHIPS):
                out = jnp.where(chip == j, ref[r0:r0 + n_r, j * width:(j + 1) * width], out)
            return out

        grads = [own_cols(a_ref, 8, N_META, meta_cols), a_ref[0:1, :], own_cols(b_ref, 1, 3, conv_cols), c_ref[...],
                 b_ref[0:1, :], a_ref[1:2, :], a_ref[2:3, :], a_ref[3:4, :]]
        loss_ref[...] = a_ref[4:5, 0:1]
        for i, g in enumerate(grads):
            w, m, v = (r[...] for r in ins[3 * i:3 * i + 3])
            for o, val in zip(outs[4 * i:4 * i + 4], (g,) + _adamw_math(w, g, m, v)):
                o[...] = val

    vm = pl.BlockSpec(memory_space=pltpu.VMEM)
    flat = [a for grp in params for a in grp]
    out_shape = ([jax.ShapeDtypeStruct((1, 1), F32)]
                 + [jax.ShapeDtypeStruct(grp[0].shape, F32) for grp in params for _ in range(4)])
    outs = pl.pallas_call(body, name="adamw_small", out_shape=out_shape,
                          in_specs=[pl.BlockSpec(memory_space=pltpu.SMEM)] + [vm] * (3 + 3 * n),
                          out_specs=[vm] * (1 + 4 * n))(place, *reduced, *flat)
    return outs[0], [tuple(outs[1 + 4 * i:5 + 4 * i]) for i in range(n)]


def _load_gathered(gathered, shards, dst_slots, sems):
    n = len(gathered)
    me = 2 * lax.axis_index("x") + lax.axis_index("y")

    def copies(j, own):
        return [pltpu.make_async_copy(shards[a] if own else gathered[a].at[j], dst_slots[a](j), sems.at[n * j + a])
                for a in range(n)]

    for wait in (False, True):
        for j in range(N_CHIPS):
            for own in (False, True):
                @pl.when((me == j) == own)
                def _():
                    for cp in copies(j, own):
                        cp.wait() if wait else cp.start()


N_MIX_SHARDS = 3


def _mixer_fwd(x3, g1, g2, poolw, pscale, shards):
    n_seq, seq, _ = x3.shape
    tm = min(TM_MIX_FWD, seq)
    n_t = seq // tm
    n_steps = n_seq * n_t
    n_ag = len(shards)
    n_ffn = n_ag - N_MIX_SHARDS
    small_rows = shards[2].shape[0]
    conv_cols = D_CONV // N_CHIPS

    def body(x_ref, g1_ref, g2_ref, pw_ref, ps_ref, *rest):
        ag = _AllGather(rest[:n_ag], rest[n_ag + 10:2 * n_ag + 10], *rest[-2:])
        (z_ref, m_ref, h1_ref, a_ref, conv_ref, pooled_ref, yc_ref, zm_ref, meta_ref,
         cw_ref) = rest[n_ag:n_ag + 10]
        win_v, wout_v, small_v, cvb, pb, load_sems = rest[2 * n_ag + 10:-2]
        s, t = pl.program_id(0), pl.program_id(1)
        step = s * n_t + t

        @pl.when(step == 0)
        def _():
            ag.start(range(N_MIX_SHARDS))
            for a in range(N_MIX_SHARDS):
                ag.relay(a)
            for a in range(N_MIX_SHARDS):
                ag.forward(a)
            ag.finish(range(N_MIX_SHARDS))
            ag.start(range(N_MIX_SHARDS, n_ag))
            _load_gathered(ag.outs[:N_MIX_SHARDS], ag.ins[:N_MIX_SHARDS],
                           [lambda j: win_v.at[j], lambda j: wout_v.at[pl.ds(j * OUT_SHARD, OUT_SHARD), :],
                            lambda j: small_v.at[j]], load_sems)

            meta = jnp.concatenate([small_v[j, 0:N_META, :] for j in range(N_CHIPS)], axis=1)
            meta_ref[...] = meta
            cw_ref[...] = jnp.concatenate([small_v[j, N_META:N_META + 3, 0:conv_cols] for j in range(N_CHIPS)], axis=1)
            a_meta = (meta * _rstd(meta) * g1_ref[...]).astype(BF16)
            for j in range(N_CHIPS):
                zm_ref[:, j * IN_SHARD:(j + 1) * IN_SHARD] = _dot(a_meta, win_v[j])

        for i in range(n_ffn):
            @pl.when(step == ((i + 1) * n_steps) // (2 * n_ffn + 2))
            def _():
                ag.relay(N_MIX_SHARDS + i)

        for i in range(n_ffn):
            @pl.when(step == min(n_steps // 2 + ((i + 1) * n_steps) // (2 * n_ffn + 2), n_steps - 1))
            def _():
                ag.forward(N_MIX_SHARDS + i)

        @pl.when(t == 0)
        def _():
            cvb[0:HALO, :] = zm_ref[:, IN_SHARD:2 * IN_SHARD] * zm_ref[:, 2 * IN_SHARD:3 * IN_SHARD]
            pb[0:HALO, :] = zm_ref[:, 3 * IN_SHARD:4 * IN_SHARD]

        @pl.when(t > 0)
        def _():
            cvb[0:HALO, :] = cvb[tm:tm + HALO, :]
            pb[0:HALO, :] = pb[tm:tm + HALO, :]

        xt = x_ref[0]
        a = (xt * _rstd(xt) * g1_ref[...]).astype(BF16)
        a_ref[...] = a
        zb = _dot(a, win_v[0])
        zc = _dot(a, win_v[1])
        zv = _dot(a, win_v[2])
        zp = _dot(a, win_v[3])
        z_ref[0, :, 0:IN_SHARD] = zb
        z_ref[0, :, IN_SHARD:2 * IN_SHARD] = zc
        z_ref[0, :, 2 * IN_SHARD:3 * IN_SHARD] = zv
        cv = zc * zv
        cvb[HALO:HALO + tm, :] = cv
        pb[HALO:HALO + tm, :] = zp
        cw = cw_ref[...]
        conv = cw[0:1] * cvb[HALO - 2:HALO - 2 + tm, :] + cw[1:2] * cvb[HALO - 1:HALO - 1 + tm, :] + cw[2:3] * cv
        conv_ref[...] = conv
        parts = [(zb * conv).astype(BF16)]
        for g in range(N_POOL_GROUPS):
            pooled = _pool_fwd(pb, g, tm).astype(BF16)
            pooled_ref[:, _gcols(g)] = pooled
            parts.append((_dot(pooled, pw_ref[g]) * ps_ref[:, _gcols(g)]).astype(BF16))
        ycat = jnp.concatenate(parts, axis=1)
        yc_ref[...] = ycat
        m = _dot(ycat, wout_v[...])
        m_ref[0] = m
        h1_ref[0] = xt + m * _rstd(m) * g2_ref[...]

        @pl.when(step == n_steps - 1)
        def _():
            ag.finish(range(N_MIX_SHARDS, n_ag))

    n_rows = n_seq * seq
    row = lambda c: pl.BlockSpec((1, tm, c), lambda s, t: (s, t, 0))
    row2 = lambda c: pl.BlockSpec((tm, c), lambda s, t: (s * n_t + t, 0))
    outs = pl.pallas_call(
        body, name="mixer_fwd", grid=(n_seq, n_t),
        out_shape=[jax.ShapeDtypeStruct((n_seq, seq, D_Z), F32), jax.ShapeDtypeStruct((n_seq, seq, D_MODEL), F32),
                   jax.ShapeDtypeStruct((n_seq, seq, D_MODEL), F32), jax.ShapeDtypeStruct((n_rows, D_MODEL), BF16),
                   jax.ShapeDtypeStruct((n_rows, D_CONV), F32), jax.ShapeDtypeStruct((n_rows, D_POOL), BF16),
                   jax.ShapeDtypeStruct((n_rows, D_MODEL), BF16), jax.ShapeDtypeStruct((N_META, D_IN_PROJ), F32),
                   jax.ShapeDtypeStruct((N_META, D_MODEL), F32), jax.ShapeDtypeStruct((3, D_CONV), F32)]
        + _AllGather.out_shape(shards),
        in_specs=[row(D_MODEL), _full((1, D_MODEL)), _full((1, D_MODEL)),
                  _full((N_POOL_GROUPS, POOL_GROUP, POOL_GROUP)), _full((1, D_POOL))] + [ANY] * n_ag,
        out_specs=[row(D_Z), row(D_MODEL), row(D_MODEL), row2(D_MODEL), row2(D_CONV), row2(D_POOL), row2(D_MODEL),
                   _full((N_META, D_IN_PROJ)), _full((N_META, D_MODEL)), _full((3, D_CONV))] + [ANY] * n_ag,
        scratch_shapes=[pltpu.VMEM((N_CHIPS, D_MODEL, IN_SHARD), BF16), pltpu.VMEM((D_MODEL, D_MODEL), BF16),
                        pltpu.VMEM((N_CHIPS, small_rows, D_MODEL // N_CHIPS), F32),
                        pltpu.VMEM((HALO + tm, D_CONV), F32), pltpu.VMEM((HALO + tm, D_POOL), F32),
                        pltpu.SemaphoreType.DMA((N_MIX_SHARDS * N_CHIPS,))] + _AllGather.scratch(n_ag),
        compiler_params=_cparams(2),
    )(x3, g1, g2, poolw, pscale, *shards)
    return outs[:10], outs[10:]


def _ffn_chunks():
    out, r0 = [], 0
    while r0 < D_FF:
        out.append((r0, min(FF_CHUNK, D_FF - r0)))
        r0 += FF_CHUNK
    return out


def _ffn_fwd_bwd(h1, target, g3, g4, gathered, shards):
    n_rows = h1.shape[0]
    tm = min(TM_FFN, n_rows)
    chunks = _ffn_chunks()

    def body(h1_ref, t_ref, g3_ref, g4_ref, wg_all, wu_all, wd_all, wg_s, wu_s, wd_s,
             dh1_ref, f_ref, dd_ref, ds_ref, du_ref, gg_ref, loss_ref, dg3_ref, dg4_ref,
             wg_v, wu_v, wd_v, s_sc, u_sc, sems):
        @pl.when(pl.program_id(0) == 0)
        def _():
            _load_gathered([wg_all, wu_all, wd_all], [wg_s, wu_s, wd_s],
                           [functools.partial(lambda v, j: v.at[pl.ds(j * FF_SHARD, FF_SHARD), :], v)
                            for v in (wg_v, wu_v, wd_v)], sems)
            loss_ref[...] = jnp.zeros_like(loss_ref)
            dg3_ref[...] = jnp.zeros_like(dg3_ref)
            dg4_ref[...] = jnp.zeros_like(dg4_ref)

        h1v = h1_ref[...]
        r3 = _rstd(h1v)
        hh = h1v * r3
        g3v, g4v = g3_ref[...], g4_ref[...]
        f = (hh * g3v).astype(BF16)
        f_ref[...] = f
        d = jnp.zeros((tm, D_MODEL), F32)
        for r0, sz in chunks:
            s = _dot_nt(f, wg_v[r0:r0 + sz, :])
            u = _dot_nt(f, wu_v[r0:r0 + sz, :])
            s_sc[:, r0:r0 + sz] = s
            u_sc[:, r0:r0 + sz] = u
            gc = (s * _sigmoid(s) * u).astype(BF16)
            gg_ref[:, r0:r0 + sz] = gc
            d = d + _dot(gc, wd_v[r0:r0 + sz, :])
        r4 = _rstd(d)
        dh = d * r4
        err = (h1v + dh * g4v) - t_ref[...]
        loss_ref[...] += _rows8(err * err)
        dy = err * (1.0 / D_MODEL)
        dg4_ref[...] += _rows8(dy * dh)
        ddb = _rms_bwd(dy, dh, r4, g4v).astype(BF16)
        dd_ref[...] = ddb
        df = jnp.zeros((tm, D_MODEL), F32)
        for r0, sz in chunks:
            dgg = _dot_nt(ddb, wd_v[r0:r0 + sz, :])
            s = s_sc[:, r0:r0 + sz]
            u = u_sc[:, r0:r0 + sz]
            sig = _sigmoid(s)
            dsc = (dgg * u * (sig * (1.0 + s * (1.0 - sig)))).astype(BF16)
            duc = (dgg * (s * sig)).astype(BF16)
            ds_ref[:, r0:r0 + sz] = dsc
            du_ref[:, r0:r0 + sz] = duc
            df = df + _dot(dsc, wg_v[r0:r0 + sz, :]) + _dot(duc, wu_v[r0:r0 + sz, :])
        dg3_ref[...] += _rows8(df * hh)
        dh1_ref[...] = dy + _rms_bwd(df, hh, r3, g3v)

    row = pl.BlockSpec((tm, D_MODEL), lambda i: (i, 0))
    ffrow = pl.BlockSpec((tm, D_FF), lambda i: (i, 0))
    acc = _full((8, D_MODEL))
    act_bf = jax.ShapeDtypeStruct((n_rows, D_MODEL), BF16)
    ff_bf = jax.ShapeDtypeStruct((n_rows, D_FF), BF16)
    acc_shape = jax.ShapeDtypeStruct((8, D_MODEL), F32)
    w_vmem = pltpu.VMEM((D_FF, D_MODEL), BF16)
    return pl.pallas_call(
        body, name="ffn_fwd_bwd", grid=(n_rows // tm,),
        out_shape=[jax.ShapeDtypeStruct((n_rows, D_MODEL), F32), act_bf, act_bf, ff_bf, ff_bf, ff_bf,
                   acc_shape, acc_shape, acc_shape],
        in_specs=[row, row, _full((1, D_MODEL)), _full((1, D_MODEL))] + [ANY] * 6,
        out_specs=[row, row, row, ffrow, ffrow, ffrow, acc, acc, acc],
        scratch_shapes=[w_vmem, w_vmem, w_vmem, pltpu.VMEM((tm, D_FF), F32), pltpu.VMEM((tm, D_FF), F32),
                        pltpu.SemaphoreType.DMA((3 * N_CHIPS,))],
        compiler_params=_cparams(1),
    )(h1, target, g3, g4, *gathered, *shards)


def _ffn_weight_grads(name, acts, other, exchanged):
    n_rows = other.shape[0]
    n_a, n_ex = len(acts), len(exchanged)
    n_c = n_a
    tk = min(TK_DW, n_rows)
    n_k = n_rows // tk
    half = D_FF // n_c

    def body(other_ref, *rest):
        act_refs = rest[:n_a]
        out_refs = rest[n_a + n_ex:2 * n_a + n_ex]
        c, k = pl.program_id(0), pl.program_id(1)
        if n_ex:
            ex = _ExchangeHalves(rest[n_a:n_a + n_ex], rest[2 * n_a + n_ex:2 * n_a + 2 * n_ex], *rest[-2:])

            @pl.when((c == 0) & (k == 0))
            def _():
                ex.start()

        @pl.when(k == 0)
        def _():
            for o in out_refs:
                o[...] = jnp.zeros_like(o)

        ov = other_ref[...]
        for a, o in zip(act_refs, out_refs):
            o[...] += _dot_tn(a[...], ov)

        if n_ex:
            @pl.when((c == n_c - 1) & (k == n_k - 1))
            def _():
                ex.finish()

    row = pl.BlockSpec((tk, D_MODEL), lambda c, k: (k, 0))
    ffrow = pl.BlockSpec((tk, half), lambda c, k: (k, c))
    out = pl.BlockSpec((half, D_MODEL), lambda c, k: (c, 0))
    outs = pl.pallas_call(
        body, name=name, grid=(n_c, n_k),
        out_shape=[jax.ShapeDtypeStruct((D_FF, D_MODEL), F32)] * n_a + _ExchangeHalves.out_shape(exchanged),
        in_specs=[row] + [ffrow] * n_a + [ANY] * n_ex, out_specs=[out] * n_a + [ANY] * n_ex,
        scratch_shapes=_ExchangeHalves.scratch(n_ex) if n_ex else [],
        compiler_params=_cparams(2),
    )(other, *acts, *exchanged)
    return outs[:n_a], outs[n_a:]


def _mixer_bwd(dh1, m3, z3, conv2, pooled2, x3, zmeta, meta_full, g1, g2, convw, poolw, pscale, gathered, shards,
               after):
    n_seq, seq, _ = x3.shape
    tm = min(TM_MIX_BWD, seq)
    sub = min(SUB_MIX_BWD, tm)
    n_t = seq // tm
    n_out = 13

    def body(dh1_ref, m_ref, z_ref, conv_ref, pooled_ref, x_ref, zm_ref, meta_ref, g1_ref, g2_ref, cw_ref, pw_ref,
             ps_ref, after_ref, win_all, wout_all, win_s, wout_s, *rest):
        (dx_ref, dz_ref, dm_ref, dg1_ref, dg2_ref, dsc_ref, dcw_ref, dpw_ref, dzm_ref, dmeta_ref, dg1m_ref, am_ref,
         dzmb_ref) = rest[:n_out]
        win_v, wout_v, dcb, dqb, mcb, mqb, load_sems = rest[n_out:]
        s, i = pl.program_id(0), pl.program_id(1)
        tr = n_t - 1 - i

        @pl.when((s == 0) & (i == 0))
        def _():
            _load_gathered([win_all, wout_all], [win_s, wout_s],
                           [lambda j: win_v.at[j], lambda j: wout_v.at[pl.ds(j * OUT_SHARD, OUT_SHARD), :]], load_sems)
            for ref in (dg1_ref, dg2_ref, dsc_ref, dcw_ref, dpw_ref, dzm_ref):
                ref[...] = jnp.zeros_like(ref)

        @pl.when(i == 0)
        def _():
            dcb[tm:tm + HALO, :] = jnp.zeros((HALO, D_CONV), F32)
            dqb[tm:tm + HALO, :] = jnp.zeros((HALO, D_POOL), F32)

        @pl.when(i > 0)
        def _():
            dcb[tm:tm + HALO, :] = dcb[0:HALO, :]
            dqb[tm:tm + HALO, :] = dqb[0:HALO, :]

        g1v, g2v = g1_ref[...], g2_ref[...]
        cw = cw_ref[...]

        for r0 in range(tm - sub, -1, -sub):
            rows = slice(r0, r0 + sub)
            dh1v = dh1_ref[0, rows, :]
            mv = m_ref[0, rows, :]
            r2 = _rstd(mv)
            mh = mv * r2
            dg2_ref[...] += _rows8(dh1v * mh)
            dmb = _rms_bwd(dh1v, mh, r2, g2v).astype(BF16)
            dm_ref[rows, :] = dmb
            dyc = _dot_nt(dmb, wout_v[...])
            dyconv = dyc[:, 0:D_CONV]

            for g in range(N_POOL_GROUPS):
                pooled = pooled_ref[rows, _gcols(g)]
                mixed = _dot(pooled, pw_ref[g])
                scale = ps_ref[:, _gcols(g)]
                dyp = dyc[:, D_CONV + g * POOL_GROUP:D_CONV + (g + 1) * POOL_GROUP]
                dsc_ref[:, _gcols(g)] += _rows8(dyp * mixed)
                dmix = (dyp * scale).astype(BF16)
                dpw_ref[g] += _dot_tn(pooled, dmix)
                dqb[rows, _gcols(g)] = _dot_nt(dmix, pw_ref[g])

            zb = z_ref[0, rows, 0:IN_SHARD]
            zc = z_ref[0, rows, IN_SHARD:2 * IN_SHARD]
            zv = z_ref[0, rows, 2 * IN_SHARD:3 * IN_SHARD]
            dconv = dyconv * zb
            dcb[rows, :] = dconv
            d1 = dcb[r0 + 1:r0 + 1 + sub, :]
            d2 = dcb[r0 + 2:r0 + 2 + sub, :]
            dcv = cw[2:3] * dconv + cw[1:2] * d1 + cw[0:1] * d2
            cv = zc * zv
            dcw_ref[0:8, :] += _rows8(cv * d2)
            dcw_ref[8:16, :] += _rows8(cv * d1)
            dcw_ref[16:24, :] += _rows8(cv * dconv)
            dzs = [(dyconv * conv_ref[rows, :]).astype(BF16), (dcv * zv).astype(BF16), (dcv * zc).astype(BF16),
                   jnp.concatenate([_pool_bwd(dqb, g, r0, sub) for g in range(N_POOL_GROUPS)], axis=1).astype(BF16)]
            da = jnp.zeros((sub, D_MODEL), F32)
            for j in range(N_CHIPS):
                dz_ref[j, rows, :] = dzs[j]
                da = da + _dot_nt(dzs[j], win_v[j])
            xt = x_ref[0, rows, :]
            r1 = _rstd(xt)
            xh = xt * r1
            dg1_ref[...] += _rows8(da * xh)
            dx_ref[0, rows, :] = dh1v + _rms_bwd(da, xh, r1, g1v)

        @pl.when(tr == 0)
        def _():
            mcb[0:HALO, :] = jnp.zeros((HALO, D_CONV), F32)
            mqb[0:HALO, :] = jnp.zeros((HALO, D_POOL), F32)
            mcb[HALO:2 * HALO, :] = dcb[0:HALO, :]
            mqb[HALO:2 * HALO, :] = dqb[0:HALO, :]
            m1 = mcb[1:1 + HALO, :]
            m2 = mcb[2:2 + HALO, :]
            zc_m = zm_ref[:, IN_SHARD:2 * IN_SHARD]
            zv_m = zm_ref[:, 2 * IN_SHARD:3 * IN_SHARD]
            cv_m = zc_m * zv_m
            dcw_ref[0:8, :] += _rows8(cv_m * m2)
            dcw_ref[8:16, :] += _rows8(cv_m * m1)
            dcv_m = cw[1:2] * m1 + cw[0:1] * m2
            dzm_ref[:, IN_SHARD:2 * IN_SHARD] += dcv_m * zv_m
            dzm_ref[:, 2 * IN_SHARD:3 * IN_SHARD] += dcv_m * zc_m
            dzm_ref[:, 3 * IN_SHARD:4 * IN_SHARD] += jnp.concatenate(
                [_pool_bwd(mqb, g, 0, HALO) for g in range(N_POOL_GROUPS)], axis=1)

        @pl.when((s == n_seq - 1) & (i == n_t - 1))
        def _():
            xm = meta_ref[...]
            rm = _rstd(xm)
            xmh = xm * rm
            am_ref[...] = (xmh * g1v).astype(BF16)
            da_m = jnp.zeros((N_META, D_MODEL), F32)
            for j in range(N_CHIPS):
                dzj = dzm_ref[:, j * IN_SHARD:(j + 1) * IN_SHARD].astype(BF16)
                dzmb_ref[j] = dzj
                da_m = da_m + _dot_nt(dzj, win_v[j])
            dg1m_ref[...] = _rows8(da_m * xmh)
            dmeta_ref[...] = _rms_bwd(da_m, xmh, rm, g1v)

    row3 = lambda c: pl.BlockSpec((1, tm, c), lambda s, i: (s, n_t - 1 - i, 0))
    row2 = lambda c: pl.BlockSpec((tm, c), lambda s, i: (s * n_t + n_t - 1 - i, 0))
    n_rows = n_seq * seq
    outs = pl.pallas_call(
        body, name="mixer_bwd", grid=(n_seq, n_t),
        out_shape=[jax.ShapeDtypeStruct((n_seq, seq, D_MODEL), F32),
                   jax.ShapeDtypeStruct((N_CHIPS, n_rows, IN_SHARD), BF16), jax.ShapeDtypeStruct((n_rows, D_MODEL), BF16),
                   jax.ShapeDtypeStruct((8, D_MODEL), F32), jax.ShapeDtypeStruct((8, D_MODEL), F32),
                   jax.ShapeDtypeStruct((8, D_POOL), F32), jax.ShapeDtypeStruct((24, D_CONV), F32),
                   jax.ShapeDtypeStruct((N_POOL_GROUPS, POOL_GROUP, POOL_GROUP), F32),
                   jax.ShapeDtypeStruct((N_META, D_IN_PROJ), F32),
                   jax.ShapeDtypeStruct((N_META, D_MODEL), F32), jax.ShapeDtypeStruct((8, D_MODEL), F32),
                   jax.ShapeDtypeStruct((N_META, D_MODEL), BF16),
                   jax.ShapeDtypeStruct((N_CHIPS, N_META, IN_SHARD), BF16)],
        in_specs=[row3(D_MODEL), row3(D_MODEL), row3(D_Z), row2(D_CONV), row2(D_POOL), row3(D_MODEL),
                  _full((N_META, D_IN_PROJ)), _full((N_META, D_MODEL)), _full((1, D_MODEL)), _full((1, D_MODEL)),
                  _full((3, D_CONV)), _full((N_POOL_GROUPS, POOL_GROUP, POOL_GROUP)), _full((1, D_POOL)),
                  _full((8, 128))] + [ANY] * 4,
        out_specs=[row3(D_MODEL), pl.BlockSpec((N_CHIPS, tm, IN_SHARD), lambda s, i: (0, s * n_t + n_t - 1 - i, 0)),
                   row2(D_MODEL),
                   _full((8, D_MODEL)), _full((8, D_MODEL)), _full((8, D_POOL)), _full((24, D_CONV)),
                   _full((N_POOL_GROUPS, POOL_GROUP, POOL_GROUP)), _full((N_META, D_IN_PROJ)),
                   _full((N_META, D_MODEL)), _full((8, D_MODEL)), _full((N_META, D_MODEL)),
                   _full((N_CHIPS, N_META, IN_SHARD))],
        scratch_shapes=[pltpu.VMEM((N_CHIPS, D_MODEL, IN_SHARD), BF16), pltpu.VMEM((D_MODEL, D_MODEL), BF16),
                        pltpu.VMEM((tm + HALO, D_CONV), F32), pltpu.VMEM((tm + HALO, D_POOL), F32),
                        pltpu.VMEM((2 * HALO, D_CONV), F32), pltpu.VMEM((2 * HALO, D_POOL), F32),
                        pltpu.SemaphoreType.DMA((2 * N_CHIPS,))],
        compiler_params=_cparams(2),
    )(dh1, m3, z3, conv2, pooled2, x3, zmeta, meta_full, g1, g2, convw, poolw, pscale, after, *gathered, *shards)
    return outs


def _mixer_weight_grads(a, dz, ycat, dm, a_meta, dz_meta, ffn_sums, small):
    n_rows = a.shape[0]
    tk = min(TK_DW, n_rows)
    n_k = n_rows // tk
    n_sc, n_sm = len(ffn_sums), _AllReduceSmall.N_IN

    def body(a_ref, dz_ref, yc_ref, dm_ref, am_ref, dzm_ref, *rest):
        ins, outs, scratch = rest[:n_sc + n_sm], rest[n_sc + n_sm:2 * n_sc + n_sm + 5], rest[2 * n_sc + n_sm + 5:]
        dwin_ref, dwout_ref = outs[:2]
        scatter = _ScatterToChips(ins[:n_sc], outs[2:2 + n_sc], *scratch[:2])
        reduce_small = _AllReduceSmall(ins[n_sc:], outs[2 + n_sc:], scratch[2:])
        k = pl.program_id(0)

        @pl.when(k == 0)
        def _():
            scatter.start()
            reduce_small.pack_and_send()
            am_t = am_ref[...].T
            for j in range(N_CHIPS):
                dwin_ref[j] = _dot(am_t, dzm_ref[j])
            dwout_ref[...] = jnp.zeros_like(dwout_ref)

        for st in range(2):
            @pl.when(k == ((st + 1) * n_k) // 3)
            def _():
                reduce_small.combine(st)

        a_t = a_ref[...].T
        for j in range(N_CHIPS):
            dwin_ref[j] += _dot(a_t, dz_ref[j])
        dwout_ref[...] += _dot_tn(yc_ref[...], dm_ref[...])

        @pl.when(k == n_k - 1)
        def _():
            reduce_small.combine(2)
            scatter.finish()

    row = pl.BlockSpec((tk, D_MODEL), lambda k: (k, 0))
    outs = pl.pallas_call(
        body, name="mixer_weight_grads", grid=(n_k,),
        out_shape=[jax.ShapeDtypeStruct((N_CHIPS, D_MODEL, IN_SHARD), F32),
                   jax.ShapeDtypeStruct((D_MODEL, D_MODEL), F32)] + _ScatterToChips.out_shape(ffn_sums)
        + _AllReduceSmall.out_shape(),
        in_specs=[row, pl.BlockSpec((N_CHIPS, tk, IN_SHARD), lambda k: (0, k, 0)), row, row,
                  _full((N_META, D_MODEL)), _full((N_CHIPS, N_META, IN_SHARD))] + [ANY] * n_sc
        + [_full(s.shape) for s in small],
        out_specs=[_full((N_CHIPS, D_MODEL, IN_SHARD)), _full((D_MODEL, D_MODEL))] + [ANY] * n_sc
        + [_full(s) for s in _AllReduceSmall.SHAPES],
        scratch_shapes=_ScatterToChips.scratch(n_sc) + _AllReduceSmall.scratch(),
        compiler_params=_cparams(1),
    )(a, dz, ycat, dm, a_meta, dz_meta, *ffn_sums, *small)
    return ([outs[0], outs[1].reshape(N_CHIPS, OUT_SHARD, D_MODEL)], outs[2:2 + n_sc], outs[2 + n_sc:])


def kernel(x, meta_tokens, norm_mix_pre, w_in, conv_w, pool_w, pool_scale, w_out, norm_mix_post, norm_ffn_pre, w_gate, w_up, w_down, norm_ffn_post, loss_target, m_meta_tokens, m_norm_mix_pre, m_w_in, m_conv_w, m_pool_w, m_pool_scale, m_w_out, m_norm_mix_post, m_norm_ffn_pre, m_w_gate, m_w_up, m_w_down, m_norm_ffn_post, v_meta_tokens, v_norm_mix_pre, v_w_in, v_conv_w, v_pool_w, v_pool_scale, v_w_out, v_norm_mix_post, v_norm_ffn_pre, v_w_gate, v_w_up, v_w_down, v_norm_ffn_post):
    n_seq, seq, _ = x.shape
    n_rows = n_seq * seq
    chip = 2 * lax.axis_index("x") + lax.axis_index("y")
    meta_cols = D_MODEL // N_CHIPS
    conv_cols = D_CONV // N_CHIPS

    small = jnp.zeros((2 * HALO, meta_cols), F32)
    small = small.at[0:N_META, :].set(meta_tokens).at[N_META:N_META + 3, 0:conv_cols].set(conv_w[0])
    poolw_bf = pool_w[0].astype(BF16)
    pscale = pool_scale
    g1, g2, g3, g4 = norm_mix_pre, norm_mix_post, norm_ffn_pre, norm_ffn_post
    place = jnp.stack([chip, lax.axis_index("c")]).astype(jnp.int32)

    mix_shards = [w_in[0].astype(BF16), w_out[0].astype(BF16)]
    ffn_shards = [w_gate[0].T.astype(BF16), w_up[0].T.astype(BF16), w_down[0].astype(BF16)]
    ((z3, m3, h1, a_bf, conv2, pooled2, yc_bf, zmeta, meta_full, conv_full),
     (win_all, wout_all, _, *ffn_gathered)) = _mixer_fwd(x, g1, g2, poolw_bf, pscale, mix_shards + [small] + ffn_shards)
    dh1, f_bf, dd_bf, ds_bf, du_bf, gg_bf, lossp, dg3p, dg4p = _ffn_fwd_bwd(
        h1.reshape(n_rows, D_MODEL), loss_target.reshape(n_rows, D_MODEL), g3, g4, ffn_gathered, ffn_shards)
    as_shards = lambda g: g.reshape(N_CHIPS, FF_SHARD, D_MODEL)
    (dwg_t, dwu_t), _ = _ffn_weight_grads("ffn_weight_grads_gate_up", [ds_bf, du_bf], f_bf, [])
    dwg_t, dwu_t = as_shards(dwg_t), as_shards(dwu_t)
    (dwd,), (dwg_recv, dwu_recv) = _ffn_weight_grads("ffn_weight_grads_down", [gg_bf], dd_bf, [dwg_t, dwu_t])
    dwd = as_shards(dwd)
    behind_bwd = _SplitComm("grad_comm_behind_mixer_bwd", [dwd],
                            _add_pairs_multi([dwg_t, dwu_t], [dwg_recv, dwu_recv], place))
    (grad_x, dz_bf, dm_bf, dg1p, dg2p, dscp, dcwp, dpw, _, dmeta, dg1m, a_meta, dz_meta) = _mixer_bwd(
        dh1.reshape(n_seq, seq, D_MODEL), m3, z3, conv2, pooled2, x, zmeta, meta_full, g1, g2, conv_full, poolw_bf,
        pscale, [win_all, wout_all], mix_shards, behind_bwd.start())
    (dwd_recv,), (dwg_rbuf, dwu_rbuf) = behind_bwd.wait(dg2p)
    (dwd,) = behind_bwd.exchanged
    mix_grads, (dwd_rbuf,), (a_red, b_red, c_red) = _mixer_weight_grads(
        a_bf, dz_bf, yc_bf, dm_bf, a_meta, dz_meta, [_add_pairs(dwd, dwd_recv, place)],
        [dg1p, dg1m, dg2p, dg3p, dg4p, lossp, dmeta, dscp, dcwp, dpw.reshape(SMALL_C_ROWS, POOL_GROUP)])

    behind_sums = _SplitComm("grad_comm_behind_ffn_sums", mix_grads, [])
    ffn_red = _add_chips([dwg_t, dwu_t, dwd], [dwg_recv, dwu_recv, dwd_recv], [dwg_rbuf, dwu_rbuf, dwd_rbuf],
                         place, after=behind_sums.start(), name="grad_add_chips_ffn")
    mix_recvs, _ = behind_sums.wait(ffn_red[0])
    mix_grads = behind_sums.exchanged
    behind_tail = _SplitComm("grad_comm_behind_ffn_tail", [], _add_pairs_multi(mix_grads, mix_recvs, place))
    as_full = lambda r: r.reshape(2 * r.shape[1], r.shape[2])
    g_wg_t, g_wu_t, g_wd = [as_full(r) for r in _gather_halves(list(ffn_red), "grad_gather_halves_ffn",
                                                                 after=behind_tail.start())]
    ffn_out = _adamw_big([(w_gate[0].T, g_wg_t, m_w_gate[0].T, v_w_gate[0].T),
                          (w_up[0].T, g_wu_t, m_w_up[0].T, v_w_up[0].T), (w_down[0], g_wd, m_w_down[0], v_w_down[0])])

    as_c = lambda p: p.reshape(SMALL_C_ROWS, POOL_GROUP)
    loss, small_out = _adamw_small(place, [a_red, b_red, c_red], [
        (meta_tokens, m_meta_tokens, v_meta_tokens),
        (g1, m_norm_mix_pre, v_norm_mix_pre),
        (conv_w[0], m_conv_w[0], v_conv_w[0]),
        (as_c(pool_w), as_c(m_pool_w), as_c(v_pool_w)),
        (pool_scale, m_pool_scale, v_pool_scale),
        (g2, m_norm_mix_post, v_norm_mix_post),
        (g3, m_norm_ffn_pre, v_norm_ffn_pre),
        (g4, m_norm_ffn_post, v_norm_ffn_post),
    ])
    _, mix_rbufs = behind_tail.wait(ffn_out[2][0], small_out[0][0])
    mix_red = _add_chips(mix_grads, mix_recvs, mix_rbufs, place)
    g_win, g_wout = [as_full(r) for r in _gather_halves(list(mix_red), "grad_gather_halves_mixer")]
    big_out = (_adamw_big([(w_in[0], g_win, m_w_in[0], v_w_in[0])])
               + _adamw_big([(w_out[0], g_wout, m_w_out[0], v_w_out[0])]) + ffn_out)
    big_out[2] = [o.T for o in big_out[2]]
    big_out[3] = [o.T for o in big_out[3]]

    s_meta, s_g1, s_conv, s_poolw, s_pscale, s_g2, s_g3, s_g4 = small_out
    b_win, b_wout, b_wg, b_wu, b_wd = big_out

    def leaf(k):
        return [s_meta[k], s_g1[k], b_win[k][None], s_conv[k][None], s_poolw[k].reshape(pool_w.shape), s_pscale[k],
                b_wout[k][None], s_g2[k], s_g3[k], b_wg[k][None], b_wu[k][None], b_wd[k][None], s_g4[k]]

    return (loss.reshape(()), grad_x, *leaf(0), *leaf(1), *leaf(2), *leaf(3))
```

```python
import functools

import jax
import jax.numpy as jnp
from jax import lax
from jax.experimental import pallas as pl
from jax.experimental.pallas import tpu as pltpu

F32 = jnp.float32
BF16 = jnp.bfloat16
MESH = pl.DeviceIdType.MESH

D_MODEL = 1024
D_CONV = 512
D_POOL = 512
POOL_GROUP = 128
N_POOL_GROUPS = 4
D_IN_PROJ = 2048
D_FF = 2816
N_CHIPS = 4
FF_SHARD = D_FF // N_CHIPS
IN_SHARD = D_IN_PROJ // N_CHIPS
OUT_SHARD = D_MODEL // N_CHIPS
D_Z = 3 * IN_SHARD
N_META = 16
HALO = 16
RMS_EPS = 1e-6

ADAM_LR = 0.001
ADAM_B1 = 0.9
ADAM_B2 = 0.999
ADAM_EPS = 1e-08
ADAM_WD = 0.01
ADAM_STEP = 10

TM_MIX_FWD = 512
TM_MIX_BWD = 512
SUB_MIX_BWD = 512
TM_FFN = 256
TK_DW = 1024
FF_CHUNK = 1024
VMEM_LIMIT = 56 * 1024 * 1024


def _cparams(n_grid):
    return pltpu.CompilerParams(dimension_semantics=("arbitrary",) * n_grid, vmem_limit_bytes=VMEM_LIMIT)


def _dot(a, b):
    return jnp.dot(a, b, preferred_element_type=F32)


def _dot_nt(a, b):
    return lax.dot_general(a, b, (((1,), (1,)), ((), ())), preferred_element_type=F32)


def _dot_tn(a, b):
    return lax.dot_general(a, b, (((0,), (0,)), ((), ())), preferred_element_type=F32)


def _rows8(v):
    r, c = v.shape
    return v.reshape(r // 8, 8, c).sum(axis=0)


def _rstd(v):
    return lax.rsqrt(jnp.mean(v * v, axis=-1, keepdims=True) + RMS_EPS)


def _rms_bwd(dy, xhat, rstd, gain):
    dyg = dy * gain
    return rstd * (dyg - xhat * jnp.mean(dyg * xhat, axis=-1, keepdims=True))


def _sigmoid(v):
    return 1.0 / (1.0 + jnp.exp(-v))


def _gcols(g):
    return slice(g * POOL_GROUP, (g + 1) * POOL_GROUP)


def _window_sum(e, g, ahead):
    n = e.shape[0]
    w = e
    for level in range(g + 1):
        shift = 1 << level
        w = w + pltpu.roll(w, (n - shift) if ahead else shift, 0)
    return w


def _pool_fwd(pb, g, n):
    e = pb[0:HALO + n, _gcols(g)]
    return _window_sum(e, g, False)[HALO:, :] * (1.0 / (2 << g)) - e[HALO:, :]


def _pool_bwd(qb, g, r0, n):
    e = qb[r0:r0 + n + HALO, _gcols(g)]
    return _window_sum(e, g, True)[0:n, :] * (1.0 / (2 << g)) - e[0:n, :]


def _full(shape):
    nd = len(shape)
    return pl.BlockSpec(shape, lambda *_: (0,) * nd)


ANY = pl.BlockSpec(memory_space=pl.ANY)


def _mesh_pos():
    x, y, c = lax.axis_index("x"), lax.axis_index("y"), lax.axis_index("c")
    chips = [(1 - x, y), (x, 1 - y), (1 - x, 1 - y)]
    return x, y, c, chips


def _half(ref, h):
    hr = ref.shape[0] // 2
    return ref.at[pl.ds(h * hr, hr), :]


class _AllGather:
    PER_ARRAY = 9

    def __init__(self, ins, outs, send_sems, recv_sems):
        self.ins, self.outs, self.send_sems, self.recv_sems = ins, outs, send_sems, recv_sems
        self.n = len(ins)

    @classmethod
    def scratch(cls, n):
        return [pltpu.SemaphoreType.DMA((cls.PER_ARRAY * n,)), pltpu.SemaphoreType.DMA((cls.PER_ARRAY * n,))]

    @staticmethod
    def out_shape(shards):
        return [jax.ShapeDtypeStruct((N_CHIPS,) + s.shape, s.dtype) for s in shards]

    def _copy(self, a, k, src, dst, to):
        i = self.PER_ARRAY * a + k
        return pltpu.make_async_remote_copy(src_ref=src, dst_ref=dst, send_sem=self.send_sems.at[i],
                                            recv_sem=self.recv_sems.at[i], device_id=to, device_id_type=MESH)

    def _piece(self, a, chip, piece, h=None):
        h = lax.axis_index("c") if h is None else h
        rows = self.ins[a].shape[0] // 4
        return self.outs[a].at[chip].at[pl.ds((2 * h + piece) * rows, rows), :]

    def _own(self, a, k):
        x, y, c, chips = _mesh_pos()
        piece = (1, 0, 0, 1)[k]
        rows = self.ins[a].shape[0] // 4
        src = self.ins[a].at[pl.ds((2 * c + piece) * rows, rows), :]
        return self._copy(a, k, src, self._piece(a, 2 * x + y, piece), (*chips[k // 2], c))

    def _relay(self, a, k):
        x, y, c, chips = _mesh_pos()
        source, to, piece = (chips[1], chips[0], 0) if k == 4 else (chips[0], chips[1], 1)
        rows = self._piece(a, 2 * source[0] + source[1], piece)
        return self._copy(a, k, rows, rows, (*to, c))

    def _sibling(self, a, k, h):
        x, y, c, chips = _mesh_pos()
        chip = chips[k - 6]
        slot = _half(self.outs[a].at[2 * chip[0] + chip[1]], h)
        return self._copy(a, k, slot, slot, (x, y, 1 - c))

    def start(self, arrays=None):
        for a in (range(self.n) if arrays is None else arrays):
            for k in range(4):
                self._own(a, k).start()

    def relay(self, a):
        self._own(a, 2).wait_recv()
        self._relay(a, 4).start()
        self._own(a, 0).wait_recv()
        self._relay(a, 5).start()

    def forward(self, a):
        c = lax.axis_index("c")
        self._own(a, 1).wait_recv()
        self._sibling(a, 6, c).start()
        self._own(a, 3).wait_recv()
        self._sibling(a, 7, c).start()
        self._relay(a, 4).wait_recv()
        self._relay(a, 5).wait_recv()
        self._sibling(a, 8, c).start()

    def finish(self, arrays=None):
        c = lax.axis_index("c")
        arrays = range(self.n) if arrays is None else arrays
        for a in arrays:
            for k in range(6, 9):
                self._sibling(a, k, 1 - c).wait_recv()
        for a in arrays:
            for k in range(4):
                self._own(a, k).wait_send()
            for k in range(4, 6):
                self._relay(a, k).wait_send()
            for k in range(6, 9):
                self._sibling(a, k, c).wait_send()


class _ExchangeHalves:
    def __init__(self, ins, recvs, send_sems, recv_sems):
        self.ins, self.recvs, self.send_sems, self.recv_sems = ins, recvs, send_sems, recv_sems

    @staticmethod
    def scratch(n):
        return [pltpu.SemaphoreType.DMA((n,)), pltpu.SemaphoreType.DMA((n,))]

    @staticmethod
    def out_shape(grads):
        return [jax.ShapeDtypeStruct((g.shape[0], g.shape[1] // 2, g.shape[2]), g.dtype) for g in grads]

    def _copies(self):
        x, y, c, _ = _mesh_pos()
        out = []
        for a, (src, dst) in enumerate(zip(self.ins, self.recvs)):
            hr = src.shape[1] // 2
            out.append(pltpu.make_async_remote_copy(
                src_ref=src.at[:, pl.ds((1 - c) * hr, hr), :], dst_ref=dst, send_sem=self.send_sems.at[a],
                recv_sem=self.recv_sems.at[a], device_id=(x, y, 1 - c), device_id_type=MESH))
        return out

    def start(self):
        for cp in self._copies():
            cp.start()

    def finish(self):
        for cp in self._copies():
            cp.wait()


class _ScatterToChips:
    def __init__(self, ins, rbufs, send_sems, recv_sems):
        self.ins, self.rbufs, self.send_sems, self.recv_sems = ins, rbufs, send_sems, recv_sems

    @staticmethod
    def scratch(n):
        return [pltpu.SemaphoreType.DMA((3 * n,)), pltpu.SemaphoreType.DMA((3 * n,))]

    @staticmethod
    def out_shape(sums):
        return [jax.ShapeDtypeStruct((3,) + s.shape[1:], BF16) for s in sums]

    def _copies(self):
        x, y, c, chips = _mesh_pos()
        out = []
        for a, (src, dst) in enumerate(zip(self.ins, self.rbufs)):
            for k, chip in enumerate(chips):
                out.append(pltpu.make_async_remote_copy(
                    src_ref=src.at[2 * chip[0] + chip[1]], dst_ref=dst.at[k], send_sem=self.send_sems.at[3 * a + k],
                    recv_sem=self.recv_sems.at[3 * a + k], device_id=(*chip, c), device_id_type=MESH))
        return out

    def start(self):
        for cp in self._copies():
            cp.start()

    def finish(self):
        for cp in self._copies():
            cp.wait()


HBM = pl.BlockSpec(memory_space=pltpu.HBM)
SEM = pl.BlockSpec(memory_space=pltpu.SEMAPHORE)


class _SplitComm:
    def __init__(self, name, exchanged, scattered):
        self.name, self.n_ex, self.n_sc = name, len(exchanged), len(scattered)
        self.n_copies = self.n_ex + 3 * self.n_sc
        zones = ([lax.empty((g.shape[0], g.shape[1] // 2, g.shape[2]), g.dtype) for g in exchanged]
                 + [lax.empty((3,) + s.shape[1:], s.dtype) for s in scattered])
        self.buffers = [pltpu.with_memory_space_constraint(v, pltpu.HBM)
                        for v in list(exchanged) + list(scattered) + zones]

    def _copies(self, bufs, send_sems, recv_sems):
        x, y, c, chips = _mesh_pos()
        n_src = self.n_ex + self.n_sc
        out = []
        for a in range(self.n_ex):
            hr = bufs[a].shape[1] // 2
            out.append(pltpu.make_async_remote_copy(
                src_ref=bufs[a].at[:, pl.ds((1 - c) * hr, hr), :], dst_ref=bufs[n_src + a], send_sem=send_sems[a],
                recv_sem=recv_sems[a], device_id=(x, y, 1 - c), device_id_type=MESH))
        for a in range(self.n_sc):
            for k, chip in enumerate(chips):
                i = self.n_ex + 3 * a + k
                out.append(pltpu.make_async_remote_copy(
                    src_ref=bufs[self.n_ex + a].at[2 * chip[0] + chip[1]], dst_ref=bufs[n_src + self.n_ex + a].at[k],
                    send_sem=send_sems[i], recv_sem=recv_sems[i], device_id=(*chip, c), device_id_type=MESH))
        return out

    def start(self):
        n_buf, n_cp = len(self.buffers), self.n_copies

        def body(*refs):
            bufs = refs[:n_buf]
            send_sems, recv_sems = refs[n_buf:n_buf + n_cp], refs[n_buf + n_cp:n_buf + 2 * n_cp]
            for cp in self._copies(bufs, send_sems, recv_sems):
                cp.start()
            refs[-1][...] = jnp.zeros_like(refs[-1])

        outs = pl.pallas_call(
            body, name=self.name + "_start",
            out_shape=[pltpu.SemaphoreType.DMA(())] * (2 * n_cp) + [pltpu.HBM(b.shape, b.dtype) for b in self.buffers]
            + [jax.ShapeDtypeStruct((8, 128), F32)],
            in_specs=[HBM] * n_buf, out_specs=[SEM] * (2 * n_cp) + [HBM] * n_buf + [pl.BlockSpec(memory_space=pltpu.VMEM)],
            input_output_aliases={i: 2 * n_cp + i for i in range(n_buf)},
            compiler_params=pltpu.CompilerParams(has_side_effects=pltpu.SideEffectType.DATAFLOW_SIDE_EFFECTING),
        )(*self.buffers)
        self.sems, self.buffers = outs[:2 * n_cp], outs[2 * n_cp:2 * n_cp + n_buf]
        return outs[-1]

    def wait(self, *after):
        n_buf, n_cp = len(self.buffers), self.n_copies

        def body(*refs):
            bufs = refs[:n_buf]
            send_sems, recv_sems = refs[n_buf:n_buf + n_cp], refs[n_buf + n_cp:n_buf + 2 * n_cp]
            for cp in self._copies(bufs, send_sems, recv_sems):
                cp.wait_send()
                cp.wait_recv()

        outs = pl.pallas_call(
            body, name=self.name + "_wait", out_shape=[pltpu.HBM(b.shape, b.dtype) for b in self.buffers],
            in_specs=[HBM] * n_buf + [SEM] * (2 * n_cp) + [ANY] * len(after), out_specs=[HBM] * n_buf,
            input_output_aliases={i: i for i in range(n_buf)},
            compiler_params=pltpu.CompilerParams(has_side_effects=pltpu.SideEffectType.DATAFLOW_SIDE_EFFECTING),
        )(*self.buffers, *self.sems, *after)
        self.exchanged = outs[:self.n_ex]
        zones = outs[self.n_ex + self.n_sc:]
        return zones[:self.n_ex], zones[self.n_ex:]


def _gather_halves(halves, name, after=None):
    n = len(halves)
    extra = [] if after is None else [after]

    def body(*refs):
        ins, outs = refs[:n], refs[n + len(extra):2 * n + len(extra)]
        send_sems, recv_sems = refs[2 * n + len(extra):]
        x, y, c, _ = _mesh_pos()
        sib = (x, y, 1 - c)
        remote = [pltpu.make_async_remote_copy(src_ref=ins[a].at[c], dst_ref=outs[a].at[c],
                                               send_sem=send_sems.at[a], recv_sem=recv_sems.at[a],
                                               device_id=sib, device_id_type=MESH) for a in range(n)]
        for cp in remote:
            cp.start()
        for a in range(n):
            pltpu.make_async_remote_copy(src_ref=ins[a].at[1 - c], dst_ref=outs[a].at[1 - c], send_sem=send_sems.at[a],
                                         recv_sem=recv_sems.at[a], device_id=sib, device_id_type=MESH).wait_recv()
        for cp in remote:
            cp.wait_send()

    return pl.pallas_call(
        body, name=name,
        out_shape=[jax.ShapeDtypeStruct(h.shape, F32) for h in halves],
        in_specs=[ANY] * (n + len(extra)), out_specs=[ANY] * n, input_output_aliases={a: a for a in range(n)},
        scratch_shapes=[pltpu.SemaphoreType.DMA((n,)), pltpu.SemaphoreType.DMA((n,))],
    )(*halves, *extra)


SMALL_A_ROWS = 24
SMALL_B_ROWS = 8
SMALL_C_ROWS = N_POOL_GROUPS * POOL_GROUP


class _AllReduceSmall:
    N_IN = 10
    SHAPES = [(SMALL_A_ROWS, D_MODEL), (SMALL_B_ROWS, D_CONV), (SMALL_C_ROWS, POOL_GROUP)]

    def __init__(self, ins, outs, scratch):
        self.ins, self.outs = ins, outs
        self.bufs, self.rcvs, self.send_sems, self.recv_sems = scratch[:3], scratch[3:6], scratch[6], scratch[7]

    @classmethod
    def scratch(cls):
        return ([pltpu.VMEM((3,) + s, F32) for s in cls.SHAPES] + [pltpu.VMEM((3,) + s, F32) for s in cls.SHAPES]
                + [pltpu.SemaphoreType.DMA((9,)), pltpu.SemaphoreType.DMA((9,))])

    @classmethod
    def out_shape(cls):
        return [jax.ShapeDtypeStruct(s, F32) for s in cls.SHAPES]

    def _copies(self, st):
        x, y, c, _ = _mesh_pos()
        peer = [(x, y, 1 - c), (1 - x, y, c), (x, 1 - y, c)][st]
        return [pltpu.make_async_remote_copy(
            src_ref=buf.at[st], dst_ref=rcv.at[st], send_sem=self.send_sems.at[3 * st + i],
            recv_sem=self.recv_sems.at[3 * st + i], device_id=peer, device_id_type=MESH)
            for i, (buf, rcv) in enumerate(zip(self.bufs, self.rcvs))]

    def pack_and_send(self):
        dg1_ref, dg1m_ref, dg2_ref, dg3_ref, dg4_ref, loss_ref, dmeta_ref, dsc_ref, dcw_ref, dpw_ref = self.ins
        a_buf, b_buf, c_buf = self.bufs

        def rowsum(v):
            return jnp.sum(v, axis=0, keepdims=True)

        a_buf[0, 0:1, :] = rowsum(dg1_ref[...] + dg1m_ref[...])
        a_buf[0, 1:2, :] = rowsum(dg2_ref[...])
        a_buf[0, 2:3, :] = rowsum(dg3_ref[...])
        a_buf[0, 3:4, :] = rowsum(dg4_ref[...])
        loss = jnp.sum(rowsum(loss_ref[...]), axis=1, keepdims=True) * (0.5 / D_MODEL)
        a_buf[0, 4:5, :] = jnp.broadcast_to(loss, (1, D_MODEL))
        a_buf[0, 5:8, :] = jnp.zeros((3, D_MODEL), F32)
        a_buf[0, 8:24, :] = dmeta_ref[...]
        b_buf[0, 0:1, :] = rowsum(dsc_ref[...])
        for k in range(3):
            b_buf[0, 1 + k:2 + k, :] = rowsum(dcw_ref[8 * k:8 * k + 8, :])
        b_buf[0, 4:8, :] = jnp.zeros((4, D_CONV), F32)
        c_buf[0] = dpw_ref[...]
        for cp in self._copies(0):
            cp.start()

    def combine(self, st):
        for cp in self._copies(st):
            cp.wait()
        if st < 2:
            for buf, rcv in zip(self.bufs, self.rcvs):
                buf[st + 1] = buf[st] + rcv[st]
            for cp in self._copies(st + 1):
                cp.start()
        else:
            for out, buf, rcv in zip(self.outs, self.bufs, self.rcvs):
                out[...] = buf[st] + rcv[st]


def _row_block(rows):
    for cand in (512, 448, 384, 352, 320, 256, 128, 64, 32, 16):
        if rows % cand == 0:
            return cand
    return rows


def _add_pairs_multi(grads, recvs, place):
    n = len(grads)
    n_sh = grads[0].shape[0]
    halves = [g.shape[1] // 2 for g in grads]
    n_steps = halves[0] // _row_block(halves[0])
    blocks = [(hr // n_steps, g.shape[2]) for hr, g in zip(halves, grads)]

    def body(place_ref, *refs):
        for a_ref, b_ref, o_ref in zip(refs[:n], refs[n:2 * n], refs[2 * n:]):
            o_ref[...] = (a_ref[0] + b_ref[...]).astype(BF16)

    return pl.pallas_call(
        body, name="grad_add_pairs",
        grid_spec=pltpu.PrefetchScalarGridSpec(
            num_scalar_prefetch=1, grid=(n_sh, n_steps),
            in_specs=[pl.BlockSpec((1, 1, br, cols), lambda j, i, p: (j, p[1], i, 0)) for br, cols in blocks]
            + [pl.BlockSpec((1, br, cols), lambda j, i, p: (j, i, 0)) for br, cols in blocks],
            out_specs=[pl.BlockSpec((1, br, cols), lambda j, i, p: (j, i, 0)) for br, cols in blocks]),
        out_shape=[jax.ShapeDtypeStruct((n_sh, hr, g.shape[2]), BF16) for hr, g in zip(halves, grads)],
        compiler_params=_cparams(2),
    )(place, *[g.reshape(n_sh, 2, hr, g.shape[2]) for hr, g in zip(halves, grads)], *recvs)


def _add_pairs(grad, recv, place):
    return _add_pairs_multi([grad], [recv], place)[0]


def _add_chips(grads, recvs, rbufs, place, after=None, name="grad_add_chips"):
    n = len(grads)
    n_sh = grads[0].shape[0]
    halves = [g.shape[1] // 2 for g in grads]
    n_steps = halves[0] // _row_block(halves[0])
    blocks = [(hr // n_steps, g.shape[2]) for hr, g in zip(halves, grads)]
    extra = [] if after is None else [after]

    def body(place_ref, *refs):
        for a_ref, b_ref, r_ref, o_ref in zip(refs[:n], refs[n:2 * n], refs[2 * n:3 * n], refs[3 * n + len(extra):]):
            own = a_ref[0, 0] + b_ref[0]
            o_ref[0] = ((own + r_ref[0].astype(F32)) + r_ref[1].astype(F32)) + r_ref[2].astype(F32)

    return pl.pallas_call(
        body, name=name,
        grid_spec=pltpu.PrefetchScalarGridSpec(
            num_scalar_prefetch=1, grid=(n_steps,),
            in_specs=[pl.BlockSpec((1, 1, br, cols), lambda i, p: (p[0], p[1], i, 0)) for br, cols in blocks]
            + [pl.BlockSpec((1, br, cols), lambda i, p: (p[0], i, 0)) for br, cols in blocks]
            + [pl.BlockSpec((3, br, cols), lambda i, p: (0, i, 0)) for br, cols in blocks]
            + [pl.BlockSpec((8, 128), lambda i, p: (0, 0))] * len(extra),
            out_specs=[pl.BlockSpec((1, br, cols), lambda i, p: (p[1], i, 0)) for br, cols in blocks]),
        out_shape=[jax.ShapeDtypeStruct((2, hr, g.shape[2]), F32) for hr, g in zip(halves, grads)],
        compiler_params=_cparams(1),
    )(place, *[g.reshape(n_sh, 2, hr, g.shape[2]) for hr, g in zip(halves, grads)], *recvs, *rbufs, *extra)


def _adamw_math(w, g, m, v):
    m2 = ADAM_B1 * m + (1.0 - ADAM_B1) * g
    v2 = ADAM_B2 * v + (1.0 - ADAM_B2) * (g * g)
    m_hat = m2 / (1.0 - ADAM_B1 ** ADAM_STEP)
    v_hat = v2 / (1.0 - ADAM_B2 ** ADAM_STEP)
    delta = -ADAM_LR * (m_hat / (jnp.sqrt(v_hat) + ADAM_EPS) + ADAM_WD * w)
    return delta, m2, v2


def _adamw_big(groups):
    n = len(groups)
    rows, cols = groups[0][0].shape
    br = _row_block(rows)
    if n > 1 and br % 16 == 0:
        br //= 2

    def body(*refs):
        for i in range(n):
            w_ref, g_ref, m_ref, v_ref = refs[4 * i:4 * i + 4]
            g_out_ref, d_ref, m2_ref, v2_ref = refs[4 * n + 4 * i:4 * n + 4 * i + 4]
            g = g_ref[...]
            d, m2, v2 = _adamw_math(w_ref[...], g, m_ref[...], v_ref[...])
            g_out_ref[...] = g
            d_ref[...] = d
            m2_ref[...] = m2
            v2_ref[...] = v2

    spec = pl.BlockSpec((br, cols), lambda i: (i, 0))
    outs = pl.pallas_call(
        body, name="adamw_big", grid=(rows // br,),
        out_shape=[jax.ShapeDtypeStruct((rows, cols), F32)] * (4 * n),
        in_specs=[spec] * (4 * n), out_specs=[spec] * (4 * n), compiler_params=_cparams(1),
    )(*[a for grp in groups for a in grp])
    return [list(outs[4 * i:4 * i + 4]) for i in range(n)]


def _adamw_small(place, reduced, params):
    n = len(params)
    meta_cols, conv_cols = D_MODEL // N_CHIPS, D_CONV // N_CHIPS

    def body(place_ref, a_ref, b_ref, c_ref, *refs):
        ins, loss_ref, outs = refs[:3 * n], refs[3 * n], refs[3 * n + 1:]
        chip = place_ref[0]

        def own_cols(ref, r0, n_r, width):
            out = ref[r0:r0 + n_r, 0:width]
            for j in range(1, N_CHIPS):
                out = jnp.where(chip == j, ref[r0:r0 + n_r, j * width:(j + 1) * width], out)
            return out

        grads = [own_cols(a_ref, 8, N_META, meta_cols), a_ref[0:1, :], own_cols(b_ref, 1, 3, conv_cols), c_ref[...],
                 b_ref[0:1, :], a_ref[1:2, :], a_ref[2:3, :], a_ref[3:4, :]]
        loss_ref[...] = a_ref[4:5, 0:1]
        for i, g in enumerate(grads):
            w, m, v = (r[...] for r in ins[3 * i:3 * i + 3])
            for o, val in zip(outs[4 * i:4 * i + 4], (g,) + _adamw_math(w, g, m, v)):
                o[...] = val

    vm = pl.BlockSpec(memory_space=pltpu.VMEM)
    flat = [a for grp in params for a in grp]
    out_shape = ([jax.ShapeDtypeStruct((1, 1), F32)]
                 + [jax.ShapeDtypeStruct(grp[0].shape, F32) for grp in params for _ in range(4)])
    outs = pl.pallas_call(body, name="adamw_small", out_shape=out_shape,
                          in_specs=[pl.BlockSpec(memory_space=pltpu.SMEM)] + [vm] * (3 + 3 * n),
                          out_specs=[vm] * (1 + 4 * n))(place, *reduced, *flat)
    return outs[0], [tuple(outs[1 + 4 * i:5 + 4 * i]) for i in range(n)]


def _load_gathered(gathered, shards, dst_slots, sems):
    n = len(gathered)
    me = 2 * lax.axis_index("x") + lax.axis_index("y")

    def copies(j, own):
        return [pltpu.make_async_copy(shards[a] if own else gathered[a].at[j], dst_slots[a](j), sems.at[n * j + a])
                for a in range(n)]

    for wait in (False, True):
        for j in range(N_CHIPS):
            for own in (False, True):
                @pl.when((me == j) == own)
                def _():
                    for cp in copies(j, own):
                        cp.wait() if wait else cp.start()


N_MIX_SHARDS = 3


def _mixer_fwd(x3, g1, g2, poolw, pscale, shards, ffn_f32):
    n_seq, seq, _ = x3.shape
    tm = min(TM_MIX_FWD, seq)
    n_t = seq // tm
    n_steps = n_seq * n_t
    n_ffn = len(ffn_f32)
    n_ag = N_MIX_SHARDS + n_ffn
    ffn_bf16 = [jax.ShapeDtypeStruct(w.shape, BF16) for w in ffn_f32]
    small_rows = shards[2].shape[0]
    conv_cols = D_CONV // N_CHIPS

    def body(x_ref, g1_ref, g2_ref, pw_ref, ps_ref, *rest):
        ffn_f32_refs, ffn_bf_refs = rest[N_MIX_SHARDS:n_ag], rest[2 * n_ag + 10:2 * n_ag + 10 + n_ffn]
        ag = _AllGather(list(rest[:N_MIX_SHARDS]) + list(ffn_bf_refs), rest[n_ag + 10:2 * n_ag + 10], *rest[-2:])
        (z_ref, m_ref, h1_ref, a_ref, conv_ref, pooled_ref, yc_ref, zm_ref, meta_ref,
         cw_ref) = rest[n_ag:n_ag + 10]
        win_v, wout_v, small_v, cvb, pb, load_sems, stage_v, cast_v, cast_sems = rest[2 * n_ag + 10 + n_ffn:-2]
        s, t = pl.program_id(0), pl.program_id(1)
        step = s * n_t + t

        def round_ffn_piece(i):
            load = pltpu.make_async_copy(ffn_f32_refs[i], stage_v, cast_sems.at[0])
            load.start()
            load.wait()
            cast_v[...] = stage_v[...].astype(BF16)
            store = pltpu.make_async_copy(cast_v, ffn_bf_refs[i], cast_sems.at[1])
            store.start()
            store.wait()

        @pl.when(step == 0)
        def _():
            ag.start(range(N_MIX_SHARDS))
            round_ffn_piece(0)
            for a in range(N_MIX_SHARDS):
                ag.relay(a)
            round_ffn_piece(1)
            for a in range(N_MIX_SHARDS):
                ag.forward(a)
            round_ffn_piece(2)
            ag.finish(range(N_MIX_SHARDS))
            ag.start(range(N_MIX_SHARDS, n_ag))
            _load_gathered(ag.outs[:N_MIX_SHARDS], ag.ins[:N_MIX_SHARDS],
                           [lambda j: win_v.at[j], lambda j: wout_v.at[pl.ds(j * OUT_SHARD, OUT_SHARD), :],
                            lambda j: small_v.at[j]], load_sems)

            meta = jnp.concatenate([small_v[j, 0:N_META, :] for j in range(N_CHIPS)], axis=1)
            meta_ref[...] = meta
            cw_ref[...] = jnp.concatenate([small_v[j, N_META:N_META + 3, 0:conv_cols] for j in range(N_CHIPS)], axis=1)
            a_meta = (meta * _rstd(meta) * g1_ref[...]).astype(BF16)
            for j in range(N_CHIPS):
                zm_ref[:, j * IN_SHARD:(j + 1) * IN_SHARD] = _dot(a_meta, win_v[j])

        for i in range(n_ffn):
            @pl.when(step == ((i + 1) * n_steps) // (2 * n_ffn + 2))
            def _():
                ag.relay(N_MIX_SHARDS + i)

        for i in range(n_ffn):
            @pl.when(step == min(n_steps // 2 + ((i + 1) * n_steps) // (2 * n_ffn + 2), n_steps - 1))
            def _():
                ag.forward(N_MIX_SHARDS + i)

        @pl.when(t == 0)
        def _():
            cvb[0:HALO, :] = zm_ref[:, IN_SHARD:2 * IN_SHARD] * zm_ref[:, 2 * IN_SHARD:3 * IN_SHARD]
            pb[0:HALO, :] = zm_ref[:, 3 * IN_SHARD:4 * IN_SHARD]

        @pl.when(t > 0)
        def _():
            cvb[0:HALO, :] = cvb[tm:tm + HALO, :]
            pb[0:HALO, :] = pb[tm:tm + HALO, :]

        xt = x_ref[0]
        a = (xt * _rstd(xt) * g1_ref[...]).astype(BF16)
        a_ref[...] = a
        zb = _dot(a, win_v[0])
        zc = _dot(a, win_v[1])
        zv = _dot(a, win_v[2])
        zp = _dot(a, win_v[3])
        z_ref[0, :, 0:IN_SHARD] = zb
        z_ref[0, :, IN_SHARD:2 * IN_SHARD] = zc
        z_ref[0, :, 2 * IN_SHARD:3 * IN_SHARD] = zv
        cv = zc * zv
        cvb[HALO:HALO + tm, :] = cv
        pb[HALO:HALO + tm, :] = zp
        cw = cw_ref[...]
        conv = cw[0:1] * cvb[HALO - 2:HALO - 2 + tm, :] + cw[1:2] * cvb[HALO - 1:HALO - 1 + tm, :] + cw[2:3] * cv
        conv_ref[...] = conv
        parts = [(zb * conv).astype(BF16)]
        for g in range(N_POOL_GROUPS):
            pooled = _pool_fwd(pb, g, tm).astype(BF16)
            pooled_ref[:, _gcols(g)] = pooled
            parts.append((_dot(pooled, pw_ref[g]) * ps_ref[:, _gcols(g)]).astype(BF16))
        ycat = jnp.concatenate(parts, axis=1)
        yc_ref[...] = ycat
        m = _dot(ycat, wout_v[...])
        m_ref[0] = m
        h1_ref[0] = xt + m * _rstd(m) * g2_ref[...]

        @pl.when(step == n_steps - 1)
        def _():
            ag.finish(range(N_MIX_SHARDS, n_ag))

    n_rows = n_seq * seq
    row = lambda c: pl.BlockSpec((1, tm, c), lambda s, t: (s, t, 0))
    row2 = lambda c: pl.BlockSpec((tm, c), lambda s, t: (s * n_t + t, 0))
    outs = pl.pallas_call(
        body, name="mixer_fwd", grid=(n_seq, n_t),
        out_shape=[jax.ShapeDtypeStruct((n_seq, seq, D_Z), F32), jax.ShapeDtypeStruct((n_seq, seq, D_MODEL), F32),
                   jax.ShapeDtypeStruct((n_seq, seq, D_MODEL), F32), jax.ShapeDtypeStruct((n_rows, D_MODEL), BF16),
                   jax.ShapeDtypeStruct((n_rows, D_CONV), F32), jax.ShapeDtypeStruct((n_rows, D_POOL), BF16),
                   jax.ShapeDtypeStruct((n_rows, D_MODEL), BF16), jax.ShapeDtypeStruct((N_META, D_IN_PROJ), F32),
                   jax.ShapeDtypeStruct((N_META, D_MODEL), F32), jax.ShapeDtypeStruct((3, D_CONV), F32)]
        + _AllGather.out_shape(list(shards) + ffn_bf16) + ffn_bf16,
        in_specs=[row(D_MODEL), _full((1, D_MODEL)), _full((1, D_MODEL)),
                  _full((N_POOL_GROUPS, POOL_GROUP, POOL_GROUP)), _full((1, D_POOL))] + [ANY] * n_ag,
        out_specs=[row(D_Z), row(D_MODEL), row(D_MODEL), row2(D_MODEL), row2(D_CONV), row2(D_POOL), row2(D_MODEL),
                   _full((N_META, D_IN_PROJ)), _full((N_META, D_MODEL)), _full((3, D_CONV))] + [ANY] * (n_ag + n_ffn),
        scratch_shapes=[pltpu.VMEM((N_CHIPS, D_MODEL, IN_SHARD), BF16), pltpu.VMEM((D_MODEL, D_MODEL), BF16),
                        pltpu.VMEM((N_CHIPS, small_rows, D_MODEL // N_CHIPS), F32),
                        pltpu.VMEM((HALO + tm, D_CONV), F32), pltpu.VMEM((HALO + tm, D_POOL), F32),
                        pltpu.SemaphoreType.DMA((N_MIX_SHARDS * N_CHIPS,)),
                        pltpu.VMEM(ffn_f32[0].shape, F32), pltpu.VMEM(ffn_f32[0].shape, BF16),
                        pltpu.SemaphoreType.DMA((2,))] + _AllGather.scratch(n_ag),
        compiler_params=_cparams(2),
    )(x3, g1, g2, poolw, pscale, *shards, *ffn_f32)
    return outs[:10], outs[10:10 + n_ag], outs[10 + n_ag:]


def _ffn_chunks():
    out, r0 = [], 0
    while r0 < D_FF:
        out.append((r0, min(FF_CHUNK, D_FF - r0)))
        r0 += FF_CHUNK
    return out


def _ffn_fwd_bwd(h1, target, g3, g4, gathered, shards):
    n_rows = h1.shape[0]
    tm = min(TM_FFN, n_rows)
    chunks = _ffn_chunks()

    def body(h1_ref, t_ref, g3_ref, g4_ref, wg_all, wu_all, wd_all, wg_s, wu_s, wd_s,
             dh1_ref, f_ref, dd_ref, ds_ref, du_ref, gg_ref, loss_ref, dg3_ref, dg4_ref,
             wg_v, wu_v, wd_v, s_sc, u_sc, sems):
        @pl.when(pl.program_id(0) == 0)
        def _():
            _load_gathered([wg_all, wu_all, wd_all], [wg_s, wu_s, wd_s],
                           [functools.partial(lambda v, j: v.at[pl.ds(j * FF_SHARD, FF_SHARD), :], v)
                            for v in (wg_v, wu_v, wd_v)], sems)
            loss_ref[...] = jnp.zeros_like(loss_ref)
            dg3_ref[...] = jnp.zeros_like(dg3_ref)
            dg4_ref[...] = jnp.zeros_like(dg4_ref)

        h1v = h1_ref[...]
        r3 = _rstd(h1v)
        hh = h1v * r3
        g3v, g4v = g3_ref[...], g4_ref[...]
        f = (hh * g3v).astype(BF16)
        f_ref[...] = f
        d = jnp.zeros((tm, D_MODEL), F32)
        for r0, sz in chunks:
            s = _dot_nt(f, wg_v[r0:r0 + sz, :])
            u = _dot_nt(f, wu_v[r0:r0 + sz, :])
            s_sc[:, r0:r0 + sz] = s
            u_sc[:, r0:r0 + sz] = u
            gc = (s * _sigmoid(s) * u).astype(BF16)
            gg_ref[:, r0:r0 + sz] = gc
            d = d + _dot(gc, wd_v[r0:r0 + sz, :])
        r4 = _rstd(d)
        dh = d * r4
        err = (h1v + dh * g4v) - t_ref[...]
        loss_ref[...] += _rows8(err * err)
        dy = err * (1.0 / D_MODEL)
        dg4_ref[...] += _rows8(dy * dh)
        ddb = _rms_bwd(dy, dh, r4, g4v).astype(BF16)
        dd_ref[...] = ddb
        df = jnp.zeros((tm, D_MODEL), F32)
        for r0, sz in chunks:
            dgg = _dot_nt(ddb, wd_v[r0:r0 + sz, :])
            s = s_sc[:, r0:r0 + sz]
            u = u_sc[:, r0:r0 + sz]
            sig = _sigmoid(s)
            dsc = (dgg * u * (sig * (1.0 + s * (1.0 - sig)))).astype(BF16)
            duc = (dgg * (s * sig)).astype(BF16)
            ds_ref[:, r0:r0 + sz] = dsc
            du_ref[:, r0:r0 + sz] = duc
            df = df + _dot(dsc, wg_v[r0:r0 + sz, :]) + _dot(duc, wu_v[r0:r0 + sz, :])
        dg3_ref[...] += _rows8(df * hh)
        dh1_ref[...] = dy + _rms_bwd(df, hh, r3, g3v)

    row = pl.BlockSpec((tm, D_MODEL), lambda i: (i, 0))
    ffrow = pl.BlockSpec((tm, D_FF), lambda i: (i, 0))
    acc = _full((8, D_MODEL))
    act_bf = jax.ShapeDtypeStruct((n_rows, D_MODEL), BF16)
    ff_bf = jax.ShapeDtypeStruct((n_rows, D_FF), BF16)
    acc_shape = jax.ShapeDtypeStruct((8, D_MODEL), F32)
    w_vmem = pltpu.VMEM((D_FF, D_MODEL), BF16)
    return pl.pallas_call(
        body, name="ffn_fwd_bwd", grid=(n_rows // tm,),
        out_shape=[jax.ShapeDtypeStruct((n_rows, D_MODEL), F32), act_bf, act_bf, ff_bf, ff_bf, ff_bf,
                   acc_shape, acc_shape, acc_shape],
        in_specs=[row, row, _full((1, D_MODEL)), _full((1, D_MODEL))] + [ANY] * 6,
        out_specs=[row, row, row, ffrow, ffrow, ffrow, acc, acc, acc],
        scratch_shapes=[w_vmem, w_vmem, w_vmem, pltpu.VMEM((tm, D_FF), F32), pltpu.VMEM((tm, D_FF), F32),
                        pltpu.SemaphoreType.DMA((3 * N_CHIPS,))],
        compiler_params=_cparams(1),
    )(h1, target, g3, g4, *gathered, *shards)


def _ffn_weight_grads(name, acts, other, exchanged):
    n_rows = other.shape[0]
    n_a, n_ex = len(acts), len(exchanged)
    n_c = n_a
    tk = min(TK_DW, n_rows)
    n_k = n_rows // tk
    half = D_FF // n_c

    def body(other_ref, *rest):
        act_refs = rest[:n_a]
        out_refs = rest[n_a + n_ex:2 * n_a + n_ex]
        c, k = pl.program_id(0), pl.program_id(1)
        if n_ex:
            ex = _ExchangeHalves(rest[n_a:n_a + n_ex], rest[2 * n_a + n_ex:2 * n_a + 2 * n_ex], *rest[-2:])

            @pl.when((c == 0) & (k == 0))
            def _():
                ex.start()

        @pl.when(k == 0)
        def _():
            for o in out_refs:
                o[...] = jnp.zeros_like(o)

        ov = other_ref[...]
        for a, o in zip(act_refs, out_refs):
            o[...] += _dot_tn(a[...], ov)

        if n_ex:
            @pl.when((c == n_c - 1) & (k == n_k - 1))
            def _():
                ex.finish()

    row = pl.BlockSpec((tk, D_MODEL), lambda c, k: (k, 0))
    ffrow = pl.BlockSpec((tk, half), lambda c, k: (k, c))
    out = pl.BlockSpec((half, D_MODEL), lambda c, k: (c, 0))
    outs = pl.pallas_call(
        body, name=name, grid=(n_c, n_k),
        out_shape=[jax.ShapeDtypeStruct((D_FF, D_MODEL), F32)] * n_a + _ExchangeHalves.out_shape(exchanged),
        in_specs=[row] + [ffrow] * n_a + [ANY] * n_ex, out_specs=[out] * n_a + [ANY] * n_ex,
        scratch_shapes=_ExchangeHalves.scratch(n_ex) if n_ex else [],
        compiler_params=_cparams(2),
    )(other, *acts, *exchanged)
    return outs[:n_a], outs[n_a:]


def _mixer_bwd(dh1, m3, z3, conv2, pooled2, x3, zmeta, meta_full, g1, g2, convw, poolw, pscale, gathered, shards,
               after):
    n_seq, seq, _ = x3.shape
    tm = min(TM_MIX_BWD, seq)
    sub = min(SUB_MIX_BWD, tm)
    n_t = seq // tm
    n_out = 13

    def body(dh1_ref, m_ref, z_ref, conv_ref, pooled_ref, x_ref, zm_ref, meta_ref, g1_ref, g2_ref, cw_ref, pw_ref,
             ps_ref, after_ref, win_all, wout_all, win_s, wout_s, *rest):
        (dx_ref, dz_ref, dm_ref, dg1_ref, dg2_ref, dsc_ref, dcw_ref, dpw_ref, dzm_ref, dmeta_ref, dg1m_ref, am_ref,
         dzmb_ref) = rest[:n_out]
        win_v, wout_v, dcb, dqb, mcb, mqb, load_sems = rest[n_out:]
        s, i = pl.program_id(0), pl.program_id(1)
        tr = n_t - 1 - i

        @pl.when((s == 0) & (i == 0))
        def _():
            _load_gathered([win_all, wout_all], [win_s, wout_s],
                           [lambda j: win_v.at[j], lambda j: wout_v.at[pl.ds(j * OUT_SHARD, OUT_SHARD), :]], load_sems)
            for ref in (dg1_ref, dg2_ref, dsc_ref, dcw_ref, dpw_ref, dzm_ref):
                ref[...] = jnp.zeros_like(ref)

        @pl.when(i == 0)
        def _():
            dcb[tm:tm + HALO, :] = jnp.zeros((HALO, D_CONV), F32)
            dqb[tm:tm + HALO, :] = jnp.zeros((HALO, D_POOL), F32)

        @pl.when(i > 0)
        def _():
            dcb[tm:tm + HALO, :] = dcb[0:HALO, :]
            dqb[tm:tm + HALO, :] = dqb[0:HALO, :]

        g1v, g2v = g1_ref[...], g2_ref[...]
        cw = cw_ref[...]

        for r0 in range(tm - sub, -1, -sub):
            rows = slice(r0, r0 + sub)
            dh1v = dh1_ref[0, rows, :]
            mv = m_ref[0, rows, :]
            r2 = _rstd(mv)
            mh = mv * r2
            dg2_ref[...] += _rows8(dh1v * mh)
            dmb = _rms_bwd(dh1v, mh, r2, g2v).astype(BF16)
            dm_ref[rows, :] = dmb
            dyc = _dot_nt(dmb, wout_v[...])
            dyconv = dyc[:, 0:D_CONV]

            for g in range(N_POOL_GROUPS):
                pooled = pooled_ref[rows, _gcols(g)]
                mixed = _dot(pooled, pw_ref[g])
                scale = ps_ref[:, _gcols(g)]
                dyp = dyc[:, D_CONV + g * POOL_GROUP:D_CONV + (g + 1) * POOL_GROUP]
                dsc_ref[:, _gcols(g)] += _rows8(dyp * mixed)
                dmix = (dyp * scale).astype(BF16)
                dpw_ref[g] += _dot_tn(pooled, dmix)
                dqb[rows, _gcols(g)] = _dot_nt(dmix, pw_ref[g])

            zb = z_ref[0, rows, 0:IN_SHARD]
            zc = z_ref[0, rows, IN_SHARD:2 * IN_SHARD]
            zv = z_ref[0, rows, 2 * IN_SHARD:3 * IN_SHARD]
            dconv = dyconv * zb
            dcb[rows, :] = dconv
            d1 = dcb[r0 + 1:r0 + 1 + sub, :]
            d2 = dcb[r0 + 2:r0 + 2 + sub, :]
            dcv = cw[2:3] * dconv + cw[1:2] * d1 + cw[0:1] * d2
            cv = zc * zv
            dcw_ref[0:8, :] += _rows8(cv * d2)
            dcw_ref[8:16, :] += _rows8(cv * d1)
            dcw_ref[16:24, :] += _rows8(cv * dconv)
            dzs = [(dyconv * conv_ref[rows, :]).astype(BF16), (dcv * zv).astype(BF16), (dcv * zc).astype(BF16),
                   jnp.concatenate([_pool_bwd(dqb, g, r0, sub) for g in range(N_POOL_GROUPS)], axis=1).astype(BF16)]
            da = jnp.zeros((sub, D_MODEL), F32)
            for j in range(N_CHIPS):
                dz_ref[j, rows, :] = dzs[j]
                da = da + _dot_nt(dzs[j], win_v[j])
            xt = x_ref[0, rows, :]
            r1 = _rstd(xt)
            xh = xt * r1
            dg1_ref[...] += _rows8(da * xh)
            dx_ref[0, rows, :] = dh1v + _rms_bwd(da, xh, r1, g1v)

        @pl.when(tr == 0)
        def _():
            mcb[0:HALO, :] = jnp.zeros((HALO, D_CONV), F32)
            mqb[0:HALO, :] = jnp.zeros((HALO, D_POOL), F32)
            mcb[HALO:2 * HALO, :] = dcb[0:HALO, :]
            mqb[HALO:2 * HALO, :] = dqb[0:HALO, :]
            m1 = mcb[1:1 + HALO, :]
            m2 = mcb[2:2 + HALO, :]
            zc_m = zm_ref[:, IN_SHARD:2 * IN_SHARD]
            zv_m = zm_ref[:, 2 * IN_SHARD:3 * IN_SHARD]
            cv_m = zc_m * zv_m
            dcw_ref[0:8, :] += _rows8(cv_m * m2)
            dcw_ref[8:16, :] += _rows8(cv_m * m1)
            dcv_m = cw[1:2] * m1 + cw[0:1] * m2
            dzm_ref[:, IN_SHARD:2 * IN_SHARD] += dcv_m * zv_m
            dzm_ref[:, 2 * IN_SHARD:3 * IN_SHARD] += dcv_m * zc_m
            dzm_ref[:, 3 * IN_SHARD:4 * IN_SHARD] += jnp.concatenate(
                [_pool_bwd(mqb, g, 0, HALO) for g in range(N_POOL_GROUPS)], axis=1)

        @pl.when((s == n_seq - 1) & (i == n_t - 1))
        def _():
            xm = meta_ref[...]
            rm = _rstd(xm)
            xmh = xm * rm
            am_ref[...] = (xmh * g1v).astype(BF16)
            da_m = jnp.zeros((N_META, D_MODEL), F32)
            for j in range(N_CHIPS):
                dzj = dzm_ref[:, j * IN_SHARD:(j + 1) * IN_SHARD].astype(BF16)
                dzmb_ref[j] = dzj
                da_m = da_m + _dot_nt(dzj, win_v[j])
            dg1m_ref[...] = _rows8(da_m * xmh)
            dmeta_ref[...] = _rms_bwd(da_m, xmh, rm, g1v)

    row3 = lambda c: pl.BlockSpec((1, tm, c), lambda s, i: (s, n_t - 1 - i, 0))
    row2 = lambda c: pl.BlockSpec((tm, c), lambda s, i: (s * n_t + n_t - 1 - i, 0))
    n_rows = n_seq * seq
    outs = pl.pallas_call(
        body, name="mixer_bwd", grid=(n_seq, n_t),
        out_shape=[jax.ShapeDtypeStruct((n_seq, seq, D_MODEL), F32),
                   jax.ShapeDtypeStruct((N_CHIPS, n_rows, IN_SHARD), BF16), jax.ShapeDtypeStruct((n_rows, D_MODEL), BF16),
                   jax.ShapeDtypeStruct((8, D_MODEL), F32), jax.ShapeDtypeStruct((8, D_MODEL), F32),
                   jax.ShapeDtypeStruct((8, D_POOL), F32), jax.ShapeDtypeStruct((24, D_CONV), F32),
                   jax.ShapeDtypeStruct((N_POOL_GROUPS, POOL_GROUP, POOL_GROUP), F32),
                   jax.ShapeDtypeStruct((N_META, D_IN_PROJ), F32),
                   jax.ShapeDtypeStruct((N_META, D_MODEL), F32), jax.ShapeDtypeStruct((8, D_MODEL), F32),
                   jax.ShapeDtypeStruct((N_META, D_MODEL), BF16),
                   jax.ShapeDtypeStruct((N_CHIPS, N_META, IN_SHARD), BF16)],
        in_specs=[row3(D_MODEL), row3(D_MODEL), row3(D_Z), row2(D_CONV), row2(D_POOL), row3(D_MODEL),
                  _full((N_META, D_IN_PROJ)), _full((N_META, D_MODEL)), _full((1, D_MODEL)), _full((1, D_MODEL)),
                  _full((3, D_CONV)), _full((N_POOL_GROUPS, POOL_GROUP, POOL_GROUP)), _full((1, D_POOL)),
                  _full((8, 128))] + [ANY] * 4,
        out_specs=[row3(D_MODEL), pl.BlockSpec((N_CHIPS, tm, IN_SHARD), lambda s, i: (0, s * n_t + n_t - 1 - i, 0)),
                   row2(D_MODEL),
                   _full((8, D_MODEL)), _full((8, D_MODEL)), _full((8, D_POOL)), _full((24, D_CONV)),
                   _full((N_POOL_GROUPS, POOL_GROUP, POOL_GROUP)), _full((N_META, D_IN_PROJ)),
                   _full((N_META, D_MODEL)), _full((8, D_MODEL)), _full((N_META, D_MODEL)),
                   _full((N_CHIPS, N_META, IN_SHARD))],
        scratch_shapes=[pltpu.VMEM((N_CHIPS, D_MODEL, IN_SHARD), BF16), pltpu.VMEM((D_MODEL, D_MODEL), BF16),
                        pltpu.VMEM((tm + HALO, D_CONV), F32), pltpu.VMEM((tm + HALO, D_POOL), F32),
                        pltpu.VMEM((2 * HALO, D_CONV), F32), pltpu.VMEM((2 * HALO, D_POOL), F32),
                        pltpu.SemaphoreType.DMA((2 * N_CHIPS,))],
        compiler_params=_cparams(2),
    )(dh1, m3, z3, conv2, pooled2, x3, zmeta, meta_full, g1, g2, convw, poolw, pscale, after, *gathered, *shards)
    return outs


def _mixer_weight_grads(a, dz, ycat, dm, a_meta, dz_meta, ffn_sums, small):
    n_rows = a.shape[0]
    tk = min(TK_DW, n_rows)
    n_k = n_rows // tk
    n_sc, n_sm = len(ffn_sums), _AllReduceSmall.N_IN

    def body(a_ref, dz_ref, yc_ref, dm_ref, am_ref, dzm_ref, *rest):
        ins, outs, scratch = rest[:n_sc + n_sm], rest[n_sc + n_sm:2 * n_sc + n_sm + 5], rest[2 * n_sc + n_sm + 5:]
        dwin_ref, dwout_ref = outs[:2]
        scatter = _ScatterToChips(ins[:n_sc], outs[2:2 + n_sc], *scratch[:2])
        reduce_small = _AllReduceSmall(ins[n_sc:], outs[2 + n_sc:], scratch[2:])
        k = pl.program_id(0)

        @pl.when(k == 0)
        def _():
            scatter.start()
            reduce_small.pack_and_send()
            am_t = am_ref[...].T
            for j in range(N_CHIPS):
                dwin_ref[j] = _dot(am_t, dzm_ref[j])
            dwout_ref[...] = jnp.zeros_like(dwout_ref)

        for st in range(2):
            @pl.when(k == ((st + 1) * n_k) // 3)
            def _():
                reduce_small.combine(st)

        a_t = a_ref[...].T
        for j in range(N_CHIPS):
            dwin_ref[j] += _dot(a_t, dz_ref[j])
        dwout_ref[...] += _dot_tn(yc_ref[...], dm_ref[...])

        @pl.when(k == n_k - 1)
        def _():
            reduce_small.combine(2)
            scatter.finish()

    row = pl.BlockSpec((tk, D_MODEL), lambda k: (k, 0))
    outs = pl.pallas_call(
        body, name="mixer_weight_grads", grid=(n_k,),
        out_shape=[jax.ShapeDtypeStruct((N_CHIPS, D_MODEL, IN_SHARD), F32),
                   jax.ShapeDtypeStruct((D_MODEL, D_MODEL), F32)] + _ScatterToChips.out_shape(ffn_sums)
        + _AllReduceSmall.out_shape(),
        in_specs=[row, pl.BlockSpec((N_CHIPS, tk, IN_SHARD), lambda k: (0, k, 0)), row, row,
                  _full((N_META, D_MODEL)), _full((N_CHIPS, N_META, IN_SHARD))] + [ANY] * n_sc
        + [_full(s.shape) for s in small],
        out_specs=[_full((N_CHIPS, D_MODEL, IN_SHARD)), _full((D_MODEL, D_MODEL))] + [ANY] * n_sc
        + [_full(s) for s in _AllReduceSmall.SHAPES],
        scratch_shapes=_ScatterToChips.scratch(n_sc) + _AllReduceSmall.scratch(),
        compiler_params=_cparams(1),
    )(a, dz, ycat, dm, a_meta, dz_meta, *ffn_sums, *small)
    return ([outs[0], outs[1].reshape(N_CHIPS, OUT_SHARD, D_MODEL)], outs[2:2 + n_sc], outs[2 + n_sc:])


def kernel(x, meta_tokens, norm_mix_pre, w_in, conv_w, pool_w, pool_scale, w_out, norm_mix_post, norm_ffn_pre, w_gate, w_up, w_down, norm_ffn_post, loss_target, m_meta_tokens, m_norm_mix_pre, m_w_in, m_conv_w, m_pool_w, m_pool_scale, m_w_out, m_norm_mix_post, m_norm_ffn_pre, m_w_gate, m_w_up, m_w_down, m_norm_ffn_post, v_meta_tokens, v_norm_mix_pre, v_w_in, v_conv_w, v_pool_w, v_pool_scale, v_w_out, v_norm_mix_post, v_norm_ffn_pre, v_w_gate, v_w_up, v_w_down, v_norm_ffn_post):
    n_seq, seq, _ = x.shape
    n_rows = n_seq * seq
    chip = 2 * lax.axis_index("x") + lax.axis_index("y")
    meta_cols = D_MODEL // N_CHIPS
    conv_cols = D_CONV // N_CHIPS

    small = jnp.zeros((2 * HALO, meta_cols), F32)
    small = small.at[0:N_META, :].set(meta_tokens).at[N_META:N_META + 3, 0:conv_cols].set(conv_w[0])
    poolw_bf = pool_w[0].astype(BF16)
    pscale = pool_scale
    g1, g2, g3, g4 = norm_mix_pre, norm_mix_post, norm_ffn_pre, norm_ffn_post
    place = jnp.stack([chip, lax.axis_index("c")]).astype(jnp.int32)

    mix_shards = [w_in[0].astype(BF16), w_out[0].astype(BF16)]
    ((z3, m3, h1, a_bf, conv2, pooled2, yc_bf, zmeta, meta_full, conv_full), (win_all, wout_all, _, *ffn_gathered),
     ffn_shards) = _mixer_fwd(x, g1, g2, poolw_bf, pscale, mix_shards + [small], [w_gate[0].T, w_up[0].T, w_down[0]])
    dh1, f_bf, dd_bf, ds_bf, du_bf, gg_bf, lossp, dg3p, dg4p = _ffn_fwd_bwd(
        h1.reshape(n_rows, D_MODEL), loss_target.reshape(n_rows, D_MODEL), g3, g4, ffn_gathered, ffn_shards)
    as_shards = lambda g: g.reshape(N_CHIPS, FF_SHARD, D_MODEL)
    (dwg_t, dwu_t), _ = _ffn_weight_grads("ffn_weight_grads_gate_up", [ds_bf, du_bf], f_bf, [])
    dwg_t, dwu_t = as_shards(dwg_t), as_shards(dwu_t)
    (dwd,), (dwg_recv, dwu_recv) = _ffn_weight_grads("ffn_weight_grads_down", [gg_bf], dd_bf, [dwg_t, dwu_t])
    dwd = as_shards(dwd)
    behind_bwd = _SplitComm("grad_comm_behind_mixer_bwd", [dwd],
                            _add_pairs_multi([dwg_t, dwu_t], [dwg_recv, dwu_recv], place))
    (grad_x, dz_bf, dm_bf, dg1p, dg2p, dscp, dcwp, dpw, _, dmeta, dg1m, a_meta, dz_meta) = _mixer_bwd(
        dh1.reshape(n_seq, seq, D_MODEL), m3, z3, conv2, pooled2, x, zmeta, meta_full, g1, g2, conv_full, poolw_bf,
        pscale, [win_all, wout_all], mix_shards, behind_bwd.start())
    (dwd_recv,), (dwg_rbuf, dwu_rbuf) = behind_bwd.wait(dg2p)
    (dwd,) = behind_bwd.exchanged
    mix_grads, (dwd_rbuf,), (a_red, b_red, c_red) = _mixer_weight_grads(
        a_bf, dz_bf, yc_bf, dm_bf, a_meta, dz_meta, [_add_pairs(dwd, dwd_recv, place)],
        [dg1p, dg1m, dg2p, dg3p, dg4p, lossp, dmeta, dscp, dcwp, dpw.reshape(SMALL_C_ROWS, POOL_GROUP)])

    behind_sums = _SplitComm("grad_comm_behind_ffn_sums", mix_grads, [])
    ffn_red = _add_chips([dwg_t, dwu_t, dwd], [dwg_recv, dwu_recv, dwd_recv], [dwg_rbuf, dwu_rbuf, dwd_rbuf],
                         place, after=behind_sums.start(), name="grad_add_chips_ffn")
    mix_recvs, _ = behind_sums.wait(ffn_red[0])
    mix_grads = behind_sums.exchanged
    behind_tail = _SplitComm("grad_comm_behind_ffn_tail", [], _add_pairs_multi(mix_grads, mix_recvs, place))
    as_full = lambda r: r.reshape(2 * r.shape[1], r.shape[2])
    g_wg_t, g_wu_t, g_wd = [as_full(r) for r in _gather_halves(list(ffn_red), "grad_gather_halves_ffn",
                                                                 after=behind_tail.start())]
    ffn_out = _adamw_big([(w_gate[0].T, g_wg_t, m_w_gate[0].T, v_w_gate[0].T),
                          (w_up[0].T, g_wu_t, m_w_up[0].T, v_w_up[0].T), (w_down[0], g_wd, m_w_down[0], v_w_down[0])])

    as_c = lambda p: p.reshape(SMALL_C_ROWS, POOL_GROUP)
    loss, small_out = _adamw_small(place, [a_red, b_red, c_red], [
        (meta_tokens, m_meta_tokens, v_meta_tokens),
        (g1, m_norm_mix_pre, v_norm_mix_pre),
        (conv_w[0], m_conv_w[0], v_conv_w[0]),
        (as_c(pool_w), as_c(m_pool_w), as_c(v_pool_w)),
        (pool_scale, m_pool_scale, v_pool_scale),
        (g2, m_norm_mix_post, v_norm_mix_post),
        (g3, m_norm_ffn_pre, v_norm_ffn_pre),
        (g4, m_norm_ffn_post, v_norm_ffn_post),
    ])
    _, mix_rbufs = behind_tail.wait(ffn_out[2][0], small_out[0][0])
    mix_red = _add_chips(mix_grads, mix_recvs, mix_rbufs, place)
    g_win, g_wout = [as_full(r) for r in _gather_halves(list(mix_red), "grad_gather_halves_mixer")]
    big_out = (_adamw_big([(w_in[0], g_win, m_w_in[0], v_w_in[0])])
               + _adamw_big([(w_out[0], g_wout, m_w_out[0], v_w_out[0])]) + ffn_out)
    big_out[2] = [o.T for o in big_out[2]]
    big_out[3] = [o.T for o in big_out[3]]

    s_meta, s_g1, s_conv, s_poolw, s_pscale, s_g2, s_g3, s_g4 = small_out
    b_win, b_wout, b_wg, b_wu, b_wd = big_out

    def leaf(k):
        return [s_meta[k], s_g1[k], b_win[k][None], s_conv[k][None], s_poolw[k].reshape(pool_w.shape), s_pscale[k],
                b_wout[k][None], s_g2[k], s_g3[k], b_wg[k][None], b_wu[k][None], b_wd[k][None], s_g4[k]]

    return (loss.reshape(()), grad_x, *leaf(0), *leaf(1), *leaf(2), *leaf(3))
```

```python
import functools

import jax
import jax.numpy as jnp
from jax import lax
from jax.experimental import pallas as pl
from jax.experimental.pallas import tpu as pltpu

F32 = jnp.float32
BF16 = jnp.bfloat16
MESH = pl.DeviceIdType.MESH

D_MODEL = 1024
D_CONV = 512
D_POOL = 512
POOL_GROUP = 128
N_POOL_GROUPS = 4
D_IN_PROJ = 2048
D_FF = 2816
N_CHIPS = 4
FF_SHARD = D_FF // N_CHIPS
IN_SHARD = D_IN_PROJ // N_CHIPS
OUT_SHARD = D_MODEL // N_CHIPS
D_Z = 3 * IN_SHARD
N_META = 16
HALO = 16
RMS_EPS = 1e-6

ADAM_LR = 0.001
ADAM_B1 = 0.9
ADAM_B2 = 0.999
ADAM_EPS = 1e-08
ADAM_WD = 0.01
ADAM_STEP = 10

TM_MIX_FWD = 512
TM_MIX_BWD = 512
SUB_MIX_BWD = 512
TM_FFN = 256
TK_DW = 1024
FF_CHUNK = 1024
VMEM_LIMIT = 56 * 1024 * 1024


def _cparams(n_grid):
    return pltpu.CompilerParams(dimension_semantics=("arbitrary",) * n_grid, vmem_limit_bytes=VMEM_LIMIT)


def _dot(a, b):
    return jnp.dot(a, b, preferred_element_type=F32)


def _dot_nt(a, b):
    return lax.dot_general(a, b, (((1,), (1,)), ((), ())), preferred_element_type=F32)


def _dot_tn(a, b):
    return lax.dot_general(a, b, (((0,), (0,)), ((), ())), preferred_element_type=F32)


def _rows8(v):
    r, c = v.shape
    return v.reshape(r // 8, 8, c).sum(axis=0)


def _rstd(v):
    return lax.rsqrt(jnp.mean(v * v, axis=-1, keepdims=True) + RMS_EPS)


def _rms_bwd(dy, xhat, rstd, gain):
    dyg = dy * gain
    return rstd * (dyg - xhat * jnp.mean(dyg * xhat, axis=-1, keepdims=True))


def _sigmoid(v):
    return 1.0 / (1.0 + jnp.exp(-v))


def _gcols(g):
    return slice(g * POOL_GROUP, (g + 1) * POOL_GROUP)


def _window_sum(e, g, ahead):
    n = e.shape[0]
    w = e
    for level in range(g + 1):
        shift = 1 << level
        w = w + pltpu.roll(w, (n - shift) if ahead else shift, 0)
    return w


def _pool_fwd(pb, g, n):
    e = pb[0:HALO + n, _gcols(g)]
    return _window_sum(e, g, False)[HALO:, :] * (1.0 / (2 << g)) - e[HALO:, :]


def _pool_bwd(qb, g, r0, n):
    e = qb[r0:r0 + n + HALO, _gcols(g)]
    return _window_sum(e, g, True)[0:n, :] * (1.0 / (2 << g)) - e[0:n, :]


def _full(shape):
    nd = len(shape)
    return pl.BlockSpec(shape, lambda *_: (0,) * nd)


ANY = pl.BlockSpec(memory_space=pl.ANY)


def _mesh_pos():
    x, y, c = lax.axis_index("x"), lax.axis_index("y"), lax.axis_index("c")
    chips = [(1 - x, y), (x, 1 - y), (1 - x, 1 - y)]
    return x, y, c, chips


def _half(ref, h):
    hr = ref.shape[0] // 2
    return ref.at[pl.ds(h * hr, hr), :]


class _AllGather:
    PER_ARRAY = 9

    def __init__(self, ins, outs, send_sems, recv_sems):
        self.ins, self.outs, self.send_sems, self.recv_sems = ins, outs, send_sems, recv_sems
        self.n = len(ins)

    @classmethod
    def scratch(cls, n):
        return [pltpu.SemaphoreType.DMA((cls.PER_ARRAY * n,)), pltpu.SemaphoreType.DMA((cls.PER_ARRAY * n,))]

    @staticmethod
    def out_shape(shards):
        return [jax.ShapeDtypeStruct((N_CHIPS,) + s.shape, s.dtype) for s in shards]

    def _copy(self, a, k, src, dst, to):
        i = self.PER_ARRAY * a + k
        return pltpu.make_async_remote_copy(src_ref=src, dst_ref=dst, send_sem=self.send_sems.at[i],
                                            recv_sem=self.recv_sems.at[i], device_id=to, device_id_type=MESH)

    def _piece(self, a, chip, piece, h=None):
        h = lax.axis_index("c") if h is None else h
        rows = self.ins[a].shape[0] // 4
        return self.outs[a].at[chip].at[pl.ds((2 * h + piece) * rows, rows), :]

    def _own(self, a, k):
        x, y, c, chips = _mesh_pos()
        piece = (1, 0, 0, 1)[k]
        rows = self.ins[a].shape[0] // 4
        src = self.ins[a].at[pl.ds((2 * c + piece) * rows, rows), :]
        return self._copy(a, k, src, self._piece(a, 2 * x + y, piece), (*chips[k // 2], c))

    def _relay(self, a, k):
        x, y, c, chips = _mesh_pos()
        source, to, piece = (chips[1], chips[0], 0) if k == 4 else (chips[0], chips[1], 1)
        rows = self._piece(a, 2 * source[0] + source[1], piece)
        return self._copy(a, k, rows, rows, (*to, c))

    def _sibling(self, a, k, h):
        x, y, c, chips = _mesh_pos()
        chip = chips[k - 6]
        slot = _half(self.outs[a].at[2 * chip[0] + chip[1]], h)
        return self._copy(a, k, slot, slot, (x, y, 1 - c))

    def start(self, arrays=None):
        for a in (range(self.n) if arrays is None else arrays):
            for k in range(4):
                self._own(a, k).start()

    def relay(self, a):
        self._own(a, 2).wait_recv()
        self._relay(a, 4).start()
        self._own(a, 0).wait_recv()
        self._relay(a, 5).start()

    def forward(self, a):
        c = lax.axis_index("c")
        self._own(a, 1).wait_recv()
        self._sibling(a, 6, c).start()
        self._own(a, 3).wait_recv()
        self._sibling(a, 7, c).start()
        self._relay(a, 4).wait_recv()
        self._relay(a, 5).wait_recv()
        self._sibling(a, 8, c).start()

    def finish(self, arrays=None):
        c = lax.axis_index("c")
        arrays = range(self.n) if arrays is None else arrays
        for a in arrays:
            for k in range(6, 9):
                self._sibling(a, k, 1 - c).wait_recv()
        for a in arrays:
            for k in range(4):
                self._own(a, k).wait_send()
            for k in range(4, 6):
                self._relay(a, k).wait_send()
            for k in range(6, 9):
                self._sibling(a, k, c).wait_send()


class _ExchangeHalves:
    def __init__(self, ins, recvs, send_sems, recv_sems):
        self.ins, self.recvs, self.send_sems, self.recv_sems = ins, recvs, send_sems, recv_sems

    @staticmethod
    def scratch(n):
        return [pltpu.SemaphoreType.DMA((n,)), pltpu.SemaphoreType.DMA((n,))]

    @staticmethod
    def out_shape(grads):
        return [jax.ShapeDtypeStruct((g.shape[0], g.shape[1] // 2, g.shape[2]), g.dtype) for g in grads]

    def _copies(self):
        x, y, c, _ = _mesh_pos()
        out = []
        for a, (src, dst) in enumerate(zip(self.ins, self.recvs)):
            hr = src.shape[1] // 2
            out.append(pltpu.make_async_remote_copy(
                src_ref=src.at[:, pl.ds((1 - c) * hr, hr), :], dst_ref=dst, send_sem=self.send_sems.at[a],
                recv_sem=self.recv_sems.at[a], device_id=(x, y, 1 - c), device_id_type=MESH))
        return out

    def start(self):
        for cp in self._copies():
            cp.start()

    def finish(self):
        for cp in self._copies():
            cp.wait()


class _ScatterToChips:
    def __init__(self, ins, rbufs, send_sems, recv_sems):
        self.ins, self.rbufs, self.send_sems, self.recv_sems = ins, rbufs, send_sems, recv_sems

    @staticmethod
    def scratch(n):
        return [pltpu.SemaphoreType.DMA((3 * n,)), pltpu.SemaphoreType.DMA((3 * n,))]

    @staticmethod
    def out_shape(sums):
        return [jax.ShapeDtypeStruct((3,) + s.shape[1:], BF16) for s in sums]

    def _copies(self):
        x, y, c, chips = _mesh_pos()
        out = []
        for a, (src, dst) in enumerate(zip(self.ins, self.rbufs)):
            for k, chip in enumerate(chips):
                out.append(pltpu.make_async_remote_copy(
                    src_ref=src.at[2 * chip[0] + chip[1]], dst_ref=dst.at[k], send_sem=self.send_sems.at[3 * a + k],
                    recv_sem=self.recv_sems.at[3 * a + k], device_id=(*chip, c), device_id_type=MESH))
        return out

    def start(self):
        for cp in self._copies():
            cp.start()

    def finish(self):
        for cp in self._copies():
            cp.wait()


HBM = pl.BlockSpec(memory_space=pltpu.HBM)
SEM = pl.BlockSpec(memory_space=pltpu.SEMAPHORE)


class _SplitComm:
    def __init__(self, name, exchanged, scattered):
        self.name, self.n_ex, self.n_sc = name, len(exchanged), len(scattered)
        self.n_copies = self.n_ex + 3 * self.n_sc
        zones = ([lax.empty((g.shape[0], g.shape[1] // 2, g.shape[2]), g.dtype) for g in exchanged]
                 + [lax.empty((3,) + s.shape[1:], s.dtype) for s in scattered])
        self.buffers = [pltpu.with_memory_space_constraint(v, pltpu.HBM)
                        for v in list(exchanged) + list(scattered) + zones]

    def _copies(self, bufs, send_sems, recv_sems):
        x, y, c, chips = _mesh_pos()
        n_src = self.n_ex + self.n_sc
        out = []
        for a in range(self.n_ex):
            hr = bufs[a].shape[1] // 2
            out.append(pltpu.make_async_remote_copy(
                src_ref=bufs[a].at[:, pl.ds((1 - c) * hr, hr), :], dst_ref=bufs[n_src + a], send_sem=send_sems[a],
                recv_sem=recv_sems[a], device_id=(x, y, 1 - c), device_id_type=MESH))
        for a in range(self.n_sc):
            for k, chip in enumerate(chips):
                i = self.n_ex + 3 * a + k
                out.append(pltpu.make_async_remote_copy(
                    src_ref=bufs[self.n_ex + a].at[2 * chip[0] + chip[1]], dst_ref=bufs[n_src + self.n_ex + a].at[k],
                    send_sem=send_sems[i], recv_sem=recv_sems[i], device_id=(*chip, c), device_id_type=MESH))
        return out

    def start(self):
        n_buf, n_cp = len(self.buffers), self.n_copies

        def body(*refs):
            bufs = refs[:n_buf]
            send_sems, recv_sems = refs[n_buf:n_buf + n_cp], refs[n_buf + n_cp:n_buf + 2 * n_cp]
            for cp in self._copies(bufs, send_sems, recv_sems):
                cp.start()
            refs[-1][...] = jnp.zeros_like(refs[-1])

        outs = pl.pallas_call(
            body, name=self.name + "_start",
            out_shape=[pltpu.SemaphoreType.DMA(())] * (2 * n_cp) + [pltpu.HBM(b.shape, b.dtype) for b in self.buffers]
            + [jax.ShapeDtypeStruct((8, 128), F32)],
            in_specs=[HBM] * n_buf, out_specs=[SEM] * (2 * n_cp) + [HBM] * n_buf + [pl.BlockSpec(memory_space=pltpu.VMEM)],
            input_output_aliases={i: 2 * n_cp + i for i in range(n_buf)},
            compiler_params=pltpu.CompilerParams(has_side_effects=pltpu.SideEffectType.DATAFLOW_SIDE_EFFECTING),
        )(*self.buffers)
        self.sems, self.buffers = outs[:2 * n_cp], outs[2 * n_cp:2 * n_cp + n_buf]
        return outs[-1]

    def wait(self, *after):
        n_buf, n_cp = len(self.buffers), self.n_copies

        def body(*refs):
            bufs = refs[:n_buf]
            send_sems, recv_sems = refs[n_buf:n_buf + n_cp], refs[n_buf + n_cp:n_buf + 2 * n_cp]
            for cp in self._copies(bufs, send_sems, recv_sems):
                cp.wait_send()
                cp.wait_recv()

        outs = pl.pallas_call(
            body, name=self.name + "_wait", out_shape=[pltpu.HBM(b.shape, b.dtype) for b in self.buffers],
            in_specs=[HBM] * n_buf + [SEM] * (2 * n_cp) + [ANY] * len(after), out_specs=[HBM] * n_buf,
            input_output_aliases={i: i for i in range(n_buf)},
            compiler_params=pltpu.CompilerParams(has_side_effects=pltpu.SideEffectType.DATAFLOW_SIDE_EFFECTING),
        )(*self.buffers, *self.sems, *after)
        self.exchanged = outs[:self.n_ex]
        zones = outs[self.n_ex + self.n_sc:]
        return zones[:self.n_ex], zones[self.n_ex:]


def _gather_halves(halves, name, after=None):
    n = len(halves)
    extra = [] if after is None else [after]

    def body(*refs):
        ins, outs = refs[:n], refs[n + len(extra):2 * n + len(extra)]
        send_sems, recv_sems = refs[2 * n + len(extra):]
        x, y, c, _ = _mesh_pos()
        sib = (x, y, 1 - c)
        remote = [pltpu.make_async_remote_copy(src_ref=ins[a].at[c], dst_ref=outs[a].at[c],
                                               send_sem=send_sems.at[a], recv_sem=recv_sems.at[a],
                                               device_id=sib, device_id_type=MESH) for a in range(n)]
        for cp in remote:
            cp.start()
        for a in range(n):
            pltpu.make_async_remote_copy(src_ref=ins[a].at[1 - c], dst_ref=outs[a].at[1 - c], send_sem=send_sems.at[a],
                                         recv_sem=recv_sems.at[a], device_id=sib, device_id_type=MESH).wait_recv()
        for cp in remote:
            cp.wait_send()

    return pl.pallas_call(
        body, name=name,
        out_shape=[jax.ShapeDtypeStruct(h.shape, F32) for h in halves],
        in_specs=[ANY] * (n + len(extra)), out_specs=[ANY] * n, input_output_aliases={a: a for a in range(n)},
        scratch_shapes=[pltpu.SemaphoreType.DMA((n,)), pltpu.SemaphoreType.DMA((n,))],
    )(*halves, *extra)


SMALL_A_ROWS = 24
SMALL_B_ROWS = 8
SMALL_C_ROWS = N_POOL_GROUPS * POOL_GROUP


class _AllReduceSmall:
    N_IN = 10
    SHAPES = [(SMALL_A_ROWS, D_MODEL), (SMALL_B_ROWS, D_CONV), (SMALL_C_ROWS, POOL_GROUP)]

    def __init__(self, ins, outs, scratch):
        self.ins, self.outs = ins, outs
        self.bufs, self.rcvs, self.send_sems, self.recv_sems = scratch[:3], scratch[3:6], scratch[6], scratch[7]

    @classmethod
    def scratch(cls):
        return ([pltpu.VMEM((3,) + s, F32) for s in cls.SHAPES] + [pltpu.VMEM((3,) + s, F32) for s in cls.SHAPES]
                + [pltpu.SemaphoreType.DMA((9,)), pltpu.SemaphoreType.DMA((9,))])

    @classmethod
    def out_shape(cls):
        return [jax.ShapeDtypeStruct(s, F32) for s in cls.SHAPES]

    def _copies(self, st):
        x, y, c, _ = _mesh_pos()
        peer = [(x, y, 1 - c), (1 - x, y, c), (x, 1 - y, c)][st]
        return [pltpu.make_async_remote_copy(
            src_ref=buf.at[st], dst_ref=rcv.at[st], send_sem=self.send_sems.at[3 * st + i],
            recv_sem=self.recv_sems.at[3 * st + i], device_id=peer, device_id_type=MESH)
            for i, (buf, rcv) in enumerate(zip(self.bufs, self.rcvs))]

    def pack_and_send(self):
        dg1_ref, dg1m_ref, dg2_ref, dg3_ref, dg4_ref, loss_ref, dmeta_ref, dsc_ref, dcw_ref, dpw_ref = self.ins
        a_buf, b_buf, c_buf = self.bufs

        def rowsum(v):
            return jnp.sum(v, axis=0, keepdims=True)

        a_buf[0, 0:1, :] = rowsum(dg1_ref[...] + dg1m_ref[...])
        a_buf[0, 1:2, :] = rowsum(dg2_ref[...])
        a_buf[0, 2:3, :] = rowsum(dg3_ref[...])
        a_buf[0, 3:4, :] = rowsum(dg4_ref[...])
        loss = jnp.sum(rowsum(loss_ref[...]), axis=1, keepdims=True) * (0.5 / D_MODEL)
        a_buf[0, 4:5, :] = jnp.broadcast_to(loss, (1, D_MODEL))
        a_buf[0, 5:8, :] = jnp.zeros((3, D_MODEL), F32)
        a_buf[0, 8:24, :] = dmeta_ref[...]
        b_buf[0, 0:1, :] = rowsum(dsc_ref[...])
        for k in range(3):
            b_buf[0, 1 + k:2 + k, :] = rowsum(dcw_ref[8 * k:8 * k + 8, :])
        b_buf[0, 4:8, :] = jnp.zeros((4, D_CONV), F32)
        c_buf[0] = dpw_ref[...]
        for cp in self._copies(0):
            cp.start()

    def combine(self, st):
        for cp in self._copies(st):
            cp.wait()
        if st < 2:
            for buf, rcv in zip(self.bufs, self.rcvs):
                buf[st + 1] = buf[st] + rcv[st]
            for cp in self._copies(st + 1):
                cp.start()
        else:
            for out, buf, rcv in zip(self.outs, self.bufs, self.rcvs):
                out[...] = buf[st] + rcv[st]


def _row_block(rows):
    for cand in (512, 448, 384, 352, 320, 256, 128, 64, 32, 16):
        if rows % cand == 0:
            return cand
    return rows


def _add_pairs_multi(grads, recvs, place):
    n = len(grads)
    n_sh = grads[0].shape[0]
    halves = [g.shape[1] // 2 for g in grads]
    n_steps = halves[0] // _row_block(halves[0])
    blocks = [(hr // n_steps, g.shape[2]) for hr, g in zip(halves, grads)]

    def body(place_ref, *refs):
        for a_ref, b_ref, o_ref in zip(refs[:n], refs[n:2 * n], refs[2 * n:]):
            o_ref[...] = (a_ref[0] + b_ref[...]).astype(BF16)

    return pl.pallas_call(
        body, name="grad_add_pairs",
        grid_spec=pltpu.PrefetchScalarGridSpec(
            num_scalar_prefetch=1, grid=(n_sh, n_steps),
            in_specs=[pl.BlockSpec((1, 1, br, cols), lambda j, i, p: (j, p[1], i, 0)) for br, cols in blocks]
            + [pl.BlockSpec((1, br, cols), lambda j, i, p: (j, i, 0)) for br, cols in blocks],
            out_specs=[pl.BlockSpec((1, br, cols), lambda j, i, p: (j, i, 0)) for br, cols in blocks]),
        out_shape=[jax.ShapeDtypeStruct((n_sh, hr, g.shape[2]), BF16) for hr, g in zip(halves, grads)],
        compiler_params=_cparams(2),
    )(place, *[g.reshape(n_sh, 2, hr, g.shape[2]) for hr, g in zip(halves, grads)], *recvs)


def _add_pairs(grad, recv, place):
    return _add_pairs_multi([grad], [recv], place)[0]


def _add_chips(grads, recvs, rbufs, place, after=None, name="grad_add_chips"):
    n = len(grads)
    n_sh = grads[0].shape[0]
    halves = [g.shape[1] // 2 for g in grads]
    n_steps = halves[0] // _row_block(halves[0])
    blocks = [(hr // n_steps, g.shape[2]) for hr, g in zip(halves, grads)]
    extra = [] if after is None else [after]

    def body(place_ref, *refs):
        for a_ref, b_ref, r_ref, o_ref in zip(refs[:n], refs[n:2 * n], refs[2 * n:3 * n], refs[3 * n + len(extra):]):
            own = a_ref[0, 0] + b_ref[0]
            o_ref[0] = ((own + r_ref[0].astype(F32)) + r_ref[1].astype(F32)) + r_ref[2].astype(F32)

    return pl.pallas_call(
        body, name=name,
        grid_spec=pltpu.PrefetchScalarGridSpec(
            num_scalar_prefetch=1, grid=(n_steps,),
            in_specs=[pl.BlockSpec((1, 1, br, cols), lambda i, p: (p[0], p[1], i, 0)) for br, cols in blocks]
            + [pl.BlockSpec((1, br, cols), lambda i, p: (p[0], i, 0)) for br, cols in blocks]
            + [pl.BlockSpec((3, br, cols), lambda i, p: (0, i, 0)) for br, cols in blocks]
            + [pl.BlockSpec((8, 128), lambda i, p: (0, 0))] * len(extra),
            out_specs=[pl.BlockSpec((1, br, cols), lambda i, p: (p[1], i, 0)) for br, cols in blocks]),
        out_shape=[jax.ShapeDtypeStruct((2, hr, g.shape[2]), F32) for hr, g in zip(halves, grads)],
        compiler_params=_cparams(1),
    )(place, *[g.reshape(n_sh, 2, hr, g.shape[2]) for hr, g in zip(halves, grads)], *recvs, *rbufs, *extra)


def _adamw_math(w, g, m, v):
    m2 = ADAM_B1 * m + (1.0 - ADAM_B1) * g
    v2 = ADAM_B2 * v + (1.0 - ADAM_B2) * (g * g)
    m_hat = m2 / (1.0 - ADAM_B1 ** ADAM_STEP)
    v_hat = v2 / (1.0 - ADAM_B2 ** ADAM_STEP)
    delta = -ADAM_LR * (m_hat / (jnp.sqrt(v_hat) + ADAM_EPS) + ADAM_WD * w)
    return delta, m2, v2


def _adamw_big(groups):
    n = len(groups)
    rows, cols = groups[0][0].shape
    br = _row_block(rows)
    if n > 1 and br % 16 == 0:
        br //= 2

    def body(*refs):
        for i in range(n):
            w_ref, g_ref, m_ref, v_ref = refs[4 * i:4 * i + 4]
            g_out_ref, d_ref, m2_ref, v2_ref = refs[4 * n + 4 * i:4 * n + 4 * i + 4]
            g = g_ref[...]
            d, m2, v2 = _adamw_math(w_ref[...], g, m_ref[...], v_ref[...])
            g_out_ref[...] = g
            d_ref[...] = d
            m2_ref[...] = m2
            v2_ref[...] = v2

    spec = pl.BlockSpec((br, cols), lambda i: (i, 0))
    outs = pl.pallas_call(
        body, name="adamw_big", grid=(rows // br,),
        out_shape=[jax.ShapeDtypeStruct((rows, cols), F32)] * (4 * n),
        in_specs=[spec] * (4 * n), out_specs=[spec] * (4 * n), compiler_params=_cparams(1),
    )(*[a for grp in groups for a in grp])
    return [list(outs[4 * i:4 * i + 4]) for i in range(n)]


def _adamw_small(place, reduced, params):
    n = len(params)
    meta_cols, conv_cols = D_MODEL // N_CHIPS, D_CONV // N_CHIPS

    def body(place_ref, a_ref, b_ref, c_ref, *refs):
        ins, loss_ref, outs = refs[:3 * n], refs[3 * n], refs[3 * n + 1:]
        chip = place_ref[0]

        def own_cols(ref, r0, n_r, width):
            out = ref[r0:r0 + n_r, 0:width]
            for j in range(1, N_CHIPS):
                out = jnp.where(chip == j, ref[r0:r0 + n_r, j * width:(j + 1) * width], out)
            return out

        grads = [own_cols(a_ref, 8, N_META, meta_cols), a_ref[0:1, :], own_cols(b_ref, 1, 3, conv_cols), c_ref[...],
                 b_ref[0:1, :], a_ref[1:2, :], a_ref[2:3, :], a_ref[3:4, :]]
        loss_ref[...] = a_ref[4:5, 0:1]
        for i, g in enumerate(grads):
            w, m, v = (r[...] for r in ins[3 * i:3 * i + 3])
            for o, val in zip(outs[4 * i:4 * i + 4], (g,) + _adamw_math(w, g, m, v)):
                o[...] = val

    vm = pl.BlockSpec(memory_space=pltpu.VMEM)
    flat = [a for grp in params for a in grp]
    out_shape = ([jax.ShapeDtypeStruct((1, 1), F32)]
                 + [jax.ShapeDtypeStruct(grp[0].shape, F32) for grp in params for _ in range(4)])
    outs = pl.pallas_call(body, name="adamw_small", out_shape=out_shape,
                          in_specs=[pl.BlockSpec(memory_space=pltpu.SMEM)] + [vm] * (3 + 3 * n),
                          out_specs=[vm] * (1 + 4 * n))(place, *reduced, *flat)
    return outs[0], [tuple(outs[1 + 4 * i:5 + 4 * i]) for i in range(n)]


def _load_gathered(gathered, shards, dst_slots, sems):
    n = len(gathered)
    me = 2 * lax.axis_index("x") + lax.axis_index("y")

    def copies(j, own):
        return [pltpu.make_async_copy(shards[a] if own else gathered[a].at[j], dst_slots[a](j), sems.at[n * j + a])
                for a in range(n)]

    for wait in (False, True):
        for j in range(N_CHIPS):
            for own in (False, True):
                @pl.when((me == j) == own)
                def _():
                    for cp in copies(j, own):
                        cp.wait() if wait else cp.start()


N_MIX_SHARDS = 3


def _mixer_fwd(x3, g1, g2, poolw, pscale, shards, ffn_f32):
    n_seq, seq, _ = x3.shape
    tm = min(TM_MIX_FWD, seq)
    n_t = seq // tm
    n_steps = n_seq * n_t
    n_ffn = len(ffn_f32)
    n_ag = N_MIX_SHARDS + n_ffn
    ffn_bf16 = [jax.ShapeDtypeStruct(w.shape, BF16) for w in ffn_f32]
    small_rows = shards[2].shape[0]
    conv_cols = D_CONV // N_CHIPS

    def body(x_ref, g1_ref, g2_ref, pw_ref, ps_ref, *rest):
        ffn_f32_refs, ffn_bf_refs = rest[N_MIX_SHARDS:n_ag], rest[2 * n_ag + 10:2 * n_ag + 10 + n_ffn]
        ag = _AllGather(list(rest[:N_MIX_SHARDS]) + list(ffn_bf_refs), rest[n_ag + 10:2 * n_ag + 10], *rest[-2:])
        (z_ref, m_ref, h1_ref, a_ref, conv_ref, pooled_ref, yc_ref, zm_ref, meta_ref,
         cw_ref) = rest[n_ag:n_ag + 10]
        win_v, wout_v, small_v, cvb, pb, load_sems, stage_v, cast_v, cast_sems = rest[2 * n_ag + 10 + n_ffn:-2]
        s, t = pl.program_id(0), pl.program_id(1)
        step = s * n_t + t

        def round_ffn_piece(i):
            load = pltpu.make_async_copy(ffn_f32_refs[i], stage_v, cast_sems.at[0])
            load.start()
            load.wait()
            cast_v[...] = stage_v[...].astype(BF16)
            store = pltpu.make_async_copy(cast_v, ffn_bf_refs[i], cast_sems.at[1])
            store.start()
            store.wait()

        @pl.when(step == 0)
        def _():
            ag.start(range(N_MIX_SHARDS))
            round_ffn_piece(0)
            for a in range(N_MIX_SHARDS):
                ag.relay(a)
            round_ffn_piece(1)
            for a in range(N_MIX_SHARDS):
                ag.forward(a)
            round_ffn_piece(2)
            ag.finish(range(N_MIX_SHARDS))
            ag.start(range(N_MIX_SHARDS, n_ag))
            _load_gathered(ag.outs[:N_MIX_SHARDS], ag.ins[:N_MIX_SHARDS],
                           [lambda j: win_v.at[j], lambda j: wout_v.at[pl.ds(j * OUT_SHARD, OUT_SHARD), :],
                            lambda j: small_v.at[j]], load_sems)

            meta = jnp.concatenate([small_v[j, 0:N_META, :] for j in range(N_CHIPS)], axis=1)
            meta_ref[...] = meta
            cw_ref[...] = jnp.concatenate([small_v[j, N_META:N_META + 3, 0:conv_cols] for j in range(N_CHIPS)], axis=1)
            a_meta = (meta * _rstd(meta) * g1_ref[...]).astype(BF16)
            for j in range(N_CHIPS):
                zm_ref[:, j * IN_SHARD:(j + 1) * IN_SHARD] = _dot(a_meta, win_v[j])

        for i in range(n_ffn):
            @pl.when(step == (3 * (i + 1) * n_steps) // (4 * n_ffn + 4))
            def _():
                ag.relay(N_MIX_SHARDS + i)

        for i in range(n_ffn):
            @pl.when(step == min(n_steps // 2 + ((i + 1) * n_steps) // (2 * n_ffn + 2), n_steps - 1))
            def _():
                ag.forward(N_MIX_SHARDS + i)

        @pl.when(t == 0)
        def _():
            cvb[0:HALO, :] = zm_ref[:, IN_SHARD:2 * IN_SHARD] * zm_ref[:, 2 * IN_SHARD:3 * IN_SHARD]
            pb[0:HALO, :] = zm_ref[:, 3 * IN_SHARD:4 * IN_SHARD]

        @pl.when(t > 0)
        def _():
            cvb[0:HALO, :] = cvb[tm:tm + HALO, :]
            pb[0:HALO, :] = pb[tm:tm + HALO, :]

        xt = x_ref[0]
        a = (xt * _rstd(xt) * g1_ref[...]).astype(BF16)
        a_ref[...] = a
        zb = _dot(a, win_v[0])
        zc = _dot(a, win_v[1])
        zv = _dot(a, win_v[2])
        zp = _dot(a, win_v[3])
        z_ref[0, :, 0:IN_SHARD] = zb
        z_ref[0, :, IN_SHARD:2 * IN_SHARD] = zc
        z_ref[0, :, 2 * IN_SHARD:3 * IN_SHARD] = zv
        cv = zc * zv
        cvb[HALO:HALO + tm, :] = cv
        pb[HALO:HALO + tm, :] = zp
        cw = cw_ref[...]
        conv = cw[0:1] * cvb[HALO - 2:HALO - 2 + tm, :] + cw[1:2] * cvb[HALO - 1:HALO - 1 + tm, :] + cw[2:3] * cv
        conv_ref[...] = conv
        parts = [(zb * conv).astype(BF16)]
        for g in range(N_POOL_GROUPS):
            pooled = _pool_fwd(pb, g, tm).astype(BF16)
            pooled_ref[:, _gcols(g)] = pooled
            parts.append((_dot(pooled, pw_ref[g]) * ps_ref[:, _gcols(g)]).astype(BF16))
        ycat = jnp.concatenate(parts, axis=1)
        yc_ref[...] = ycat
        m = _dot(ycat, wout_v[...])
        m_ref[0] = m
        h1_ref[0] = xt + m * _rstd(m) * g2_ref[...]

        @pl.when(step == n_steps - 1)
        def _():
            ag.finish(range(N_MIX_SHARDS, n_ag))

    n_rows = n_seq * seq
    row = lambda c: pl.BlockSpec((1, tm, c), lambda s, t: (s, t, 0))
    row2 = lambda c: pl.BlockSpec((tm, c), lambda s, t: (s * n_t + t, 0))
    outs = pl.pallas_call(
        body, name="mixer_fwd", grid=(n_seq, n_t),
        out_shape=[jax.ShapeDtypeStruct((n_seq, seq, D_Z), F32), jax.ShapeDtypeStruct((n_seq, seq, D_MODEL), F32),
                   jax.ShapeDtypeStruct((n_seq, seq, D_MODEL), F32), jax.ShapeDtypeStruct((n_rows, D_MODEL), BF16),
                   jax.ShapeDtypeStruct((n_rows, D_CONV), F32), jax.ShapeDtypeStruct((n_rows, D_POOL), BF16),
                   jax.ShapeDtypeStruct((n_rows, D_MODEL), BF16), jax.ShapeDtypeStruct((N_META, D_IN_PROJ), F32),
                   jax.ShapeDtypeStruct((N_META, D_MODEL), F32), jax.ShapeDtypeStruct((3, D_CONV), F32)]
        + _AllGather.out_shape(list(shards) + ffn_bf16) + ffn_bf16,
        in_specs=[row(D_MODEL), _full((1, D_MODEL)), _full((1, D_MODEL)),
                  _full((N_POOL_GROUPS, POOL_GROUP, POOL_GROUP)), _full((1, D_POOL))] + [ANY] * n_ag,
        out_specs=[row(D_Z), row(D_MODEL), row(D_MODEL), row2(D_MODEL), row2(D_CONV), row2(D_POOL), row2(D_MODEL),
                   _full((N_META, D_IN_PROJ)), _full((N_META, D_MODEL)), _full((3, D_CONV))] + [ANY] * (n_ag + n_ffn),
        scratch_shapes=[pltpu.VMEM((N_CHIPS, D_MODEL, IN_SHARD), BF16), pltpu.VMEM((D_MODEL, D_MODEL), BF16),
                        pltpu.VMEM((N_CHIPS, small_rows, D_MODEL // N_CHIPS), F32),
                        pltpu.VMEM((HALO + tm, D_CONV), F32), pltpu.VMEM((HALO + tm, D_POOL), F32),
                        pltpu.SemaphoreType.DMA((N_MIX_SHARDS * N_CHIPS,)),
                        pltpu.VMEM(ffn_f32[0].shape, F32), pltpu.VMEM(ffn_f32[0].shape, BF16),
                        pltpu.SemaphoreType.DMA((2,))] + _AllGather.scratch(n_ag),
        compiler_params=_cparams(2),
    )(x3, g1, g2, poolw, pscale, *shards, *ffn_f32)
    return outs[:10], outs[10:10 + n_ag], outs[10 + n_ag:]


def _ffn_chunks():
    out, r0 = [], 0
    while r0 < D_FF:
        out.append((r0, min(FF_CHUNK, D_FF - r0)))
        r0 += FF_CHUNK
    return out


def _ffn_fwd_bwd(h1, target, g3, g4, gathered, shards):
    n_rows = h1.shape[0]
    tm = min(TM_FFN, n_rows)
    chunks = _ffn_chunks()

    def body(h1_ref, t_ref, g3_ref, g4_ref, wg_all, wu_all, wd_all, wg_s, wu_s, wd_s,
             dh1_ref, f_ref, dd_ref, ds_ref, du_ref, gg_ref, loss_ref, dg3_ref, dg4_ref,
             wg_v, wu_v, wd_v, s_sc, u_sc, sems):
        @pl.when(pl.program_id(0) == 0)
        def _():
            _load_gathered([wg_all, wu_all, wd_all], [wg_s, wu_s, wd_s],
                           [functools.partial(lambda v, j: v.at[pl.ds(j * FF_SHARD, FF_SHARD), :], v)
                            for v in (wg_v, wu_v, wd_v)], sems)
            loss_ref[...] = jnp.zeros_like(loss_ref)
            dg3_ref[...] = jnp.zeros_like(dg3_ref)
            dg4_ref[...] = jnp.zeros_like(dg4_ref)

        h1v = h1_ref[...]
        r3 = _rstd(h1v)
        hh = h1v * r3
        g3v, g4v = g3_ref[...], g4_ref[...]
        f = (hh * g3v).astype(BF16)
        f_ref[...] = f
        d = jnp.zeros((tm, D_MODEL), F32)
        for r0, sz in chunks:
            s = _dot_nt(f, wg_v[r0:r0 + sz, :])
            u = _dot_nt(f, wu_v[r0:r0 + sz, :])
            s_sc[:, r0:r0 + sz] = s
            u_sc[:, r0:r0 + sz] = u
            gc = (s * _sigmoid(s) * u).astype(BF16)
            gg_ref[:, r0:r0 + sz] = gc
            d = d + _dot(gc, wd_v[r0:r0 + sz, :])
        r4 = _rstd(d)
        dh = d * r4
        err = (h1v + dh * g4v) - t_ref[...]
        loss_ref[...] += _rows8(err * err)
        dy = err * (1.0 / D_MODEL)
        dg4_ref[...] += _rows8(dy * dh)
        ddb = _rms_bwd(dy, dh, r4, g4v).astype(BF16)
        dd_ref[...] = ddb
        df = jnp.zeros((tm, D_MODEL), F32)
        for r0, sz in chunks:
            dgg = _dot_nt(ddb, wd_v[r0:r0 + sz, :])
            s = s_sc[:, r0:r0 + sz]
            u = u_sc[:, r0:r0 + sz]
            sig = _sigmoid(s)
            dsc = (dgg * u * (sig * (1.0 + s * (1.0 - sig)))).astype(BF16)
            duc = (dgg * (s * sig)).astype(BF16)
            ds_ref[:, r0:r0 + sz] = dsc
            du_ref[:, r0:r0 + sz] = duc
            df = df + _dot(dsc, wg_v[r0:r0 + sz, :]) + _dot(duc, wu_v[r0:r0 + sz, :])
        dg3_ref[...] += _rows8(df * hh)
        dh1_ref[...] = dy + _rms_bwd(df, hh, r3, g3v)

    row = pl.BlockSpec((tm, D_MODEL), lambda i: (i, 0))
    ffrow = pl.BlockSpec((tm, D_FF), lambda i: (i, 0))
    acc = _full((8, D_MODEL))
    act_bf = jax.ShapeDtypeStruct((n_rows, D_MODEL), BF16)
    ff_bf = jax.ShapeDtypeStruct((n_rows, D_FF), BF16)
    acc_shape = jax.ShapeDtypeStruct((8, D_MODEL), F32)
    w_vmem = pltpu.VMEM((D_FF, D_MODEL), BF16)
    return pl.pallas_call(
        body, name="ffn_fwd_bwd", grid=(n_rows // tm,),
        out_shape=[jax.ShapeDtypeStruct((n_rows, D_MODEL), F32), act_bf, act_bf, ff_bf, ff_bf, ff_bf,
                   acc_shape, acc_shape, acc_shape],
        in_specs=[row, row, _full((1, D_MODEL)), _full((1, D_MODEL))] + [ANY] * 6,
        out_specs=[row, row, row, ffrow, ffrow, ffrow, acc, acc, acc],
        scratch_shapes=[w_vmem, w_vmem, w_vmem, pltpu.VMEM((tm, D_FF), F32), pltpu.VMEM((tm, D_FF), F32),
                        pltpu.SemaphoreType.DMA((3 * N_CHIPS,))],
        compiler_params=_cparams(1),
    )(h1, target, g3, g4, *gathered, *shards)


def _ffn_weight_grads(name, acts, other, exchanged):
    n_rows = other.shape[0]
    n_a, n_ex = len(acts), len(exchanged)
    n_c = n_a
    tk = min(TK_DW, n_rows)
    n_k = n_rows // tk
    half = D_FF // n_c

    def body(other_ref, *rest):
        act_refs = rest[:n_a]
        out_refs = rest[n_a + n_ex:2 * n_a + n_ex]
        c, k = pl.program_id(0), pl.program_id(1)
        if n_ex:
            ex = _ExchangeHalves(rest[n_a:n_a + n_ex], rest[2 * n_a + n_ex:2 * n_a + 2 * n_ex], *rest[-2:])

            @pl.when((c == 0) & (k == 0))
            def _():
                ex.start()

        @pl.when(k == 0)
        def _():
            for o in out_refs:
                o[...] = jnp.zeros_like(o)

        ov = other_ref[...]
        for a, o in zip(act_refs, out_refs):
            o[...] += _dot_tn(a[...], ov)

        if n_ex:
            @pl.when((c == n_c - 1) & (k == n_k - 1))
            def _():
                ex.finish()

    row = pl.BlockSpec((tk, D_MODEL), lambda c, k: (k, 0))
    ffrow = pl.BlockSpec((tk, half), lambda c, k: (k, c))
    out = pl.BlockSpec((half, D_MODEL), lambda c, k: (c, 0))
    outs = pl.pallas_call(
        body, name=name, grid=(n_c, n_k),
        out_shape=[jax.ShapeDtypeStruct((D_FF, D_MODEL), F32)] * n_a + _ExchangeHalves.out_shape(exchanged),
        in_specs=[row] + [ffrow] * n_a + [ANY] * n_ex, out_specs=[out] * n_a + [ANY] * n_ex,
        scratch_shapes=_ExchangeHalves.scratch(n_ex) if n_ex else [],
        compiler_params=_cparams(2),
    )(other, *acts, *exchanged)
    return outs[:n_a], outs[n_a:]


def _mixer_bwd(dh1, m3, z3, conv2, pooled2, x3, zmeta, meta_full, g1, g2, convw, poolw, pscale, gathered, shards,
               after):
    n_seq, seq, _ = x3.shape
    tm = min(TM_MIX_BWD, seq)
    sub = min(SUB_MIX_BWD, tm)
    n_t = seq // tm
    n_out = 13

    def body(dh1_ref, m_ref, z_ref, conv_ref, pooled_ref, x_ref, zm_ref, meta_ref, g1_ref, g2_ref, cw_ref, pw_ref,
             ps_ref, after_ref, win_all, wout_all, win_s, wout_s, *rest):
        (dx_ref, dz_ref, dm_ref, dg1_ref, dg2_ref, dsc_ref, dcw_ref, dpw_ref, dzm_ref, dmeta_ref, dg1m_ref, am_ref,
         dzmb_ref) = rest[:n_out]
        win_v, wout_v, dcb, dqb, mcb, mqb, load_sems = rest[n_out:]
        s, i = pl.program_id(0), pl.program_id(1)
        tr = n_t - 1 - i

        @pl.when((s == 0) & (i == 0))
        def _():
            _load_gathered([win_all, wout_all], [win_s, wout_s],
                           [lambda j: win_v.at[j], lambda j: wout_v.at[pl.ds(j * OUT_SHARD, OUT_SHARD), :]], load_sems)
            for ref in (dg1_ref, dg2_ref, dsc_ref, dcw_ref, dpw_ref, dzm_ref):
                ref[...] = jnp.zeros_like(ref)

        @pl.when(i == 0)
        def _():
            dcb[tm:tm + HALO, :] = jnp.zeros((HALO, D_CONV), F32)
            dqb[tm:tm + HALO, :] = jnp.zeros((HALO, D_POOL), F32)

        @pl.when(i > 0)
        def _():
            dcb[tm:tm + HALO, :] = dcb[0:HALO, :]
            dqb[tm:tm + HALO, :] = dqb[0:HALO, :]

        g1v, g2v = g1_ref[...], g2_ref[...]
        cw = cw_ref[...]

        for r0 in range(tm - sub, -1, -sub):
            rows = slice(r0, r0 + sub)
            dh1v = dh1_ref[0, rows, :]
            mv = m_ref[0, rows, :]
            r2 = _rstd(mv)
            mh = mv * r2
            dg2_ref[...] += _rows8(dh1v * mh)
            dmb = _rms_bwd(dh1v, mh, r2, g2v).astype(BF16)
            dm_ref[rows, :] = dmb
            dyc = _dot_nt(dmb, wout_v[...])
            dyconv = dyc[:, 0:D_CONV]

            for g in range(N_POOL_GROUPS):
                pooled = pooled_ref[rows, _gcols(g)]
                mixed = _dot(pooled, pw_ref[g])
                scale = ps_ref[:, _gcols(g)]
                dyp = dyc[:, D_CONV + g * POOL_GROUP:D_CONV + (g + 1) * POOL_GROUP]
                dsc_ref[:, _gcols(g)] += _rows8(dyp * mixed)
                dmix = (dyp * scale).astype(BF16)
                dpw_ref[g] += _dot_tn(pooled, dmix)
                dqb[rows, _gcols(g)] = _dot_nt(dmix, pw_ref[g])

            zb = z_ref[0, rows, 0:IN_SHARD]
            zc = z_ref[0, rows, IN_SHARD:2 * IN_SHARD]
            zv = z_ref[0, rows, 2 * IN_SHARD:3 * IN_SHARD]
            dconv = dyconv * zb
            dcb[rows, :] = dconv
            d1 = dcb[r0 + 1:r0 + 1 + sub, :]
            d2 = dcb[r0 + 2:r0 + 2 + sub, :]
            dcv = cw[2:3] * dconv + cw[1:2] * d1 + cw[0:1] * d2
            cv = zc * zv
            dcw_ref[0:8, :] += _rows8(cv * d2)
            dcw_ref[8:16, :] += _rows8(cv * d1)
            dcw_ref[16:24, :] += _rows8(cv * dconv)
            dzs = [(dyconv * conv_ref[rows, :]).astype(BF16), (dcv * zv).astype(BF16), (dcv * zc).astype(BF16),
                   jnp.concatenate([_pool_bwd(dqb, g, r0, sub) for g in range(N_POOL_GROUPS)], axis=1).astype(BF16)]
            da = jnp.zeros((sub, D_MODEL), F32)
            for j in range(N_CHIPS):
                dz_ref[j, rows, :] = dzs[j]
                da = da + _dot_nt(dzs[j], win_v[j])
            xt = x_ref[0, rows, :]
            r1 = _rstd(xt)
            xh = xt * r1
            dg1_ref[...] += _rows8(da * xh)
            dx_ref[0, rows, :] = dh1v + _rms_bwd(da, xh, r1, g1v)

        @pl.when(tr == 0)
        def _():
            mcb[0:HALO, :] = jnp.zeros((HALO, D_CONV), F32)
            mqb[0:HALO, :] = jnp.zeros((HALO, D_POOL), F32)
            mcb[HALO:2 * HALO, :] = dcb[0:HALO, :]
            mqb[HALO:2 * HALO, :] = dqb[0:HALO, :]
            m1 = mcb[1:1 + HALO, :]
            m2 = mcb[2:2 + HALO, :]
            zc_m = zm_ref[:, IN_SHARD:2 * IN_SHARD]
            zv_m = zm_ref[:, 2 * IN_SHARD:3 * IN_SHARD]
            cv_m = zc_m * zv_m
            dcw_ref[0:8, :] += _rows8(cv_m * m2)
            dcw_ref[8:16, :] += _rows8(cv_m * m1)
            dcv_m = cw[1:2] * m1 + cw[0:1] * m2
            dzm_ref[:, IN_SHARD:2 * IN_SHARD] += dcv_m * zv_m
            dzm_ref[:, 2 * IN_SHARD:3 * IN_SHARD] += dcv_m * zc_m
            dzm_ref[:, 3 * IN_SHARD:4 * IN_SHARD] += jnp.concatenate(
                [_pool_bwd(mqb, g, 0, HALO) for g in range(N_POOL_GROUPS)], axis=1)

        @pl.when((s == n_seq - 1) & (i == n_t - 1))
        def _():
            xm = meta_ref[...]
            rm = _rstd(xm)
            xmh = xm * rm
            am_ref[...] = (xmh * g1v).astype(BF16)
            da_m = jnp.zeros((N_META, D_MODEL), F32)
            for j in range(N_CHIPS):
                dzj = dzm_ref[:, j * IN_SHARD:(j + 1) * IN_SHARD].astype(BF16)
                dzmb_ref[j] = dzj
                da_m = da_m + _dot_nt(dzj, win_v[j])
            dg1m_ref[...] = _rows8(da_m * xmh)
            dmeta_ref[...] = _rms_bwd(da_m, xmh, rm, g1v)

    row3 = lambda c: pl.BlockSpec((1, tm, c), lambda s, i: (s, n_t - 1 - i, 0))
    row2 = lambda c: pl.BlockSpec((tm, c), lambda s, i: (s * n_t + n_t - 1 - i, 0))
    n_rows = n_seq * seq
    outs = pl.pallas_call(
        body, name="mixer_bwd", grid=(n_seq, n_t),
        out_shape=[jax.ShapeDtypeStruct((n_seq, seq, D_MODEL), F32),
                   jax.ShapeDtypeStruct((N_CHIPS, n_rows, IN_SHARD), BF16), jax.ShapeDtypeStruct((n_rows, D_MODEL), BF16),
                   jax.ShapeDtypeStruct((8, D_MODEL), F32), jax.ShapeDtypeStruct((8, D_MODEL), F32),
                   jax.ShapeDtypeStruct((8, D_POOL), F32), jax.ShapeDtypeStruct((24, D_CONV), F32),
                   jax.ShapeDtypeStruct((N_POOL_GROUPS, POOL_GROUP, POOL_GROUP), F32),
                   jax.ShapeDtypeStruct((N_META, D_IN_PROJ), F32),
                   jax.ShapeDtypeStruct((N_META, D_MODEL), F32), jax.ShapeDtypeStruct((8, D_MODEL), F32),
                   jax.ShapeDtypeStruct((N_META, D_MODEL), BF16),
                   jax.ShapeDtypeStruct((N_CHIPS, N_META, IN_SHARD), BF16)],
        in_specs=[row3(D_MODEL), row3(D_MODEL), row3(D_Z), row2(D_CONV), row2(D_POOL), row3(D_MODEL),
                  _full((N_META, D_IN_PROJ)), _full((N_META, D_MODEL)), _full((1, D_MODEL)), _full((1, D_MODEL)),
                  _full((3, D_CONV)), _full((N_POOL_GROUPS, POOL_GROUP, POOL_GROUP)), _full((1, D_POOL)),
                  _full((8, 128))] + [ANY] * 4,
        out_specs=[row3(D_MODEL), pl.BlockSpec((N_CHIPS, tm, IN_SHARD), lambda s, i: (0, s * n_t + n_t - 1 - i, 0)),
                   row2(D_MODEL),
                   _full((8, D_MODEL)), _full((8, D_MODEL)), _full((8, D_POOL)), _full((24, D_CONV)),
                   _full((N_POOL_GROUPS, POOL_GROUP, POOL_GROUP)), _full((N_META, D_IN_PROJ)),
                   _full((N_META, D_MODEL)), _full((8, D_MODEL)), _full((N_META, D_MODEL)),
                   _full((N_CHIPS, N_META, IN_SHARD))],
        scratch_shapes=[pltpu.VMEM((N_CHIPS, D_MODEL, IN_SHARD), BF16), pltpu.VMEM((D_MODEL, D_MODEL), BF16),
                        pltpu.VMEM((tm + HALO, D_CONV), F32), pltpu.VMEM((tm + HALO, D_POOL), F32),
                        pltpu.VMEM((2 * HALO, D_CONV), F32), pltpu.VMEM((2 * HALO, D_POOL), F32),
                        pltpu.SemaphoreType.DMA((2 * N_CHIPS,))],
        compiler_params=_cparams(2),
    )(dh1, m3, z3, conv2, pooled2, x3, zmeta, meta_full, g1, g2, convw, poolw, pscale, after, *gathered, *shards)
    return outs


def _mixer_weight_grads(a, dz, ycat, dm, a_meta, dz_meta, ffn_sums, small):
    n_rows = a.shape[0]
    tk = min(TK_DW, n_rows)
    n_k = n_rows // tk
    n_sc, n_sm = len(ffn_sums), _AllReduceSmall.N_IN

    def body(a_ref, dz_ref, yc_ref, dm_ref, am_ref, dzm_ref, *rest):
        ins, outs, scratch = rest[:n_sc + n_sm], rest[n_sc + n_sm:2 * n_sc + n_sm + 5], rest[2 * n_sc + n_sm + 5:]
        dwin_ref, dwout_ref = outs[:2]
        scatter = _ScatterToChips(ins[:n_sc], outs[2:2 + n_sc], *scratch[:2])
        reduce_small = _AllReduceSmall(ins[n_sc:], outs[2 + n_sc:], scratch[2:])
        k = pl.program_id(0)

        @pl.when(k == 0)
        def _():
            scatter.start()
            reduce_small.pack_and_send()
            am_t = am_ref[...].T
            for j in range(N_CHIPS):
                dwin_ref[j] = _dot(am_t, dzm_ref[j])
            dwout_ref[...] = jnp.zeros_like(dwout_ref)

        for st in range(2):
            @pl.when(k == ((st + 1) * n_k) // 3)
            def _():
                reduce_small.combine(st)

        a_t = a_ref[...].T
        for j in range(N_CHIPS):
            dwin_ref[j] += _dot(a_t, dz_ref[j])
        dwout_ref[...] += _dot_tn(yc_ref[...], dm_ref[...])

        @pl.when(k == n_k - 1)
        def _():
            reduce_small.combine(2)
            scatter.finish()

    row = pl.BlockSpec((tk, D_MODEL), lambda k: (k, 0))
    outs = pl.pallas_call(
        body, name="mixer_weight_grads", grid=(n_k,),
        out_shape=[jax.ShapeDtypeStruct((N_CHIPS, D_MODEL, IN_SHARD), F32),
                   jax.ShapeDtypeStruct((D_MODEL, D_MODEL), F32)] + _ScatterToChips.out_shape(ffn_sums)
        + _AllReduceSmall.out_shape(),
        in_specs=[row, pl.BlockSpec((N_CHIPS, tk, IN_SHARD), lambda k: (0, k, 0)), row, row,
                  _full((N_META, D_MODEL)), _full((N_CHIPS, N_META, IN_SHARD))] + [ANY] * n_sc
        + [_full(s.shape) for s in small],
        out_specs=[_full((N_CHIPS, D_MODEL, IN_SHARD)), _full((D_MODEL, D_MODEL))] + [ANY] * n_sc
        + [_full(s) for s in _AllReduceSmall.SHAPES],
        scratch_shapes=_ScatterToChips.scratch(n_sc) + _AllReduceSmall.scratch(),
        compiler_params=_cparams(1),
    )(a, dz, ycat, dm, a_meta, dz_meta, *ffn_sums, *small)
    return ([outs[0], outs[1].reshape(N_CHIPS, OUT_SHARD, D_MODEL)], outs[2:2 + n_sc], outs[2 + n_sc:])


def kernel(x, meta_tokens, norm_mix_pre, w_in, conv_w, pool_w, pool_scale, w_out, norm_mix_post, norm_ffn_pre, w_gate, w_up, w_down, norm_ffn_post, loss_target, m_meta_tokens, m_norm_mix_pre, m_w_in, m_conv_w, m_pool_w, m_pool_scale, m_w_out, m_norm_mix_post, m_norm_ffn_pre, m_w_gate, m_w_up, m_w_down, m_norm_ffn_post, v_meta_tokens, v_norm_mix_pre, v_w_in, v_conv_w, v_pool_w, v_pool_scale, v_w_out, v_norm_mix_post, v_norm_ffn_pre, v_w_gate, v_w_up, v_w_down, v_norm_ffn_post):
    n_seq, seq, _ = x.shape
    n_rows = n_seq * seq
    chip = 2 * lax.axis_index("x") + lax.axis_index("y")
    meta_cols = D_MODEL // N_CHIPS
    conv_cols = D_CONV // N_CHIPS

    small = jnp.zeros((2 * HALO, meta_cols), F32)
    small = small.at[0:N_META, :].set(meta_tokens).at[N_META:N_META + 3, 0:conv_cols].set(conv_w[0])
    poolw_bf = pool_w[0].astype(BF16)
    pscale = pool_scale
    g1, g2, g3, g4 = norm_mix_pre, norm_mix_post, norm_ffn_pre, norm_ffn_post
    place = jnp.stack([chip, lax.axis_index("c")]).astype(jnp.int32)

    mix_shards = [w_in[0].astype(BF16), w_out[0].astype(BF16)]
    ((z3, m3, h1, a_bf, conv2, pooled2, yc_bf, zmeta, meta_full, conv_full), (win_all, wout_all, _, *ffn_gathered),
     ffn_shards) = _mixer_fwd(x, g1, g2, poolw_bf, pscale, mix_shards + [small], [w_gate[0].T, w_up[0].T, w_down[0]])
    dh1, f_bf, dd_bf, ds_bf, du_bf, gg_bf, lossp, dg3p, dg4p = _ffn_fwd_bwd(
        h1.reshape(n_rows, D_MODEL), loss_target.reshape(n_rows, D_MODEL), g3, g4, ffn_gathered, ffn_shards)
    as_shards = lambda g: g.reshape(N_CHIPS, FF_SHARD, D_MODEL)
    (dwg_t, dwu_t), _ = _ffn_weight_grads("ffn_weight_grads_gate_up", [ds_bf, du_bf], f_bf, [])
    dwg_t, dwu_t = as_shards(dwg_t), as_shards(dwu_t)
    (dwd,), (dwg_recv, dwu_recv) = _ffn_weight_grads("ffn_weight_grads_down", [gg_bf], dd_bf, [dwg_t, dwu_t])
    dwd = as_shards(dwd)
    behind_bwd = _SplitComm("grad_comm_behind_mixer_bwd", [dwd],
                            _add_pairs_multi([dwg_t, dwu_t], [dwg_recv, dwu_recv], place))
    (grad_x, dz_bf, dm_bf, dg1p, dg2p, dscp, dcwp, dpw, _, dmeta, dg1m, a_meta, dz_meta) = _mixer_bwd(
        dh1.reshape(n_seq, seq, D_MODEL), m3, z3, conv2, pooled2, x, zmeta, meta_full, g1, g2, conv_full, poolw_bf,
        pscale, [win_all, wout_all], mix_shards, behind_bwd.start())
    (dwd_recv,), (dwg_rbuf, dwu_rbuf) = behind_bwd.wait(dg2p)
    (dwd,) = behind_bwd.exchanged
    mix_grads, (dwd_rbuf,), (a_red, b_red, c_red) = _mixer_weight_grads(
        a_bf, dz_bf, yc_bf, dm_bf, a_meta, dz_meta, [_add_pairs(dwd, dwd_recv, place)],
        [dg1p, dg1m, dg2p, dg3p, dg4p, lossp, dmeta, dscp, dcwp, dpw.reshape(SMALL_C_ROWS, POOL_GROUP)])

    behind_sums = _SplitComm("grad_comm_behind_ffn_sums", mix_grads, [])
    ffn_red = _add_chips([dwg_t, dwu_t, dwd], [dwg_recv, dwu_recv, dwd_recv], [dwg_rbuf, dwu_rbuf, dwd_rbuf],
                         place, after=behind_sums.start(), name="grad_add_chips_ffn")
    mix_recvs, _ = behind_sums.wait(ffn_red[0])
    mix_grads = behind_sums.exchanged
    behind_tail = _SplitComm("grad_comm_behind_ffn_tail", [], _add_pairs_multi(mix_grads, mix_recvs, place))
    as_full = lambda r: r.reshape(2 * r.shape[1], r.shape[2])
    g_wg_t, g_wu_t, g_wd = [as_full(r) for r in _gather_halves(list(ffn_red), "grad_gather_halves_ffn",
                                                                 after=behind_tail.start())]
    ffn_out = _adamw_big([(w_gate[0].T, g_wg_t, m_w_gate[0].T, v_w_gate[0].T),
                          (w_up[0].T, g_wu_t, m_w_up[0].T, v_w_up[0].T), (w_down[0], g_wd, m_w_down[0], v_w_down[0])])

    as_c = lambda p: p.reshape(SMALL_C_ROWS, POOL_GROUP)
    loss, small_out = _adamw_small(place, [a_red, b_red, c_red], [
        (meta_tokens, m_meta_tokens, v_meta_tokens),
        (g1, m_norm_mix_pre, v_norm_mix_pre),
        (conv_w[0], m_conv_w[0], v_conv_w[0]),
        (as_c(pool_w), as_c(m_pool_w), as_c(v_pool_w)),
        (pool_scale, m_pool_scale, v_pool_scale),
        (g2, m_norm_mix_post, v_norm_mix_post),
        (g3, m_norm_ffn_pre, v_norm_ffn_pre),
        (g4, m_norm_ffn_post, v_norm_ffn_post),
    ])
    _, mix_rbufs = behind_tail.wait(ffn_out[2][0], small_out[0][0])
    mix_red = _add_chips(mix_grads, mix_recvs, mix_rbufs, place)
    g_win, g_wout = [as_full(r) for r in _gather_halves(list(mix_red), "grad_gather_halves_mixer")]
    big_out = (_adamw_big([(w_in[0], g_win, m_w_in[0], v_w_in[0])])
               + _adamw_big([(w_out[0], g_wout, m_w_out[0], v_w_out[0])]) + ffn_out)
    big_out[2] = [o.T for o in big_out[2]]
    big_out[3] = [o.T for o in big_out[3]]

    s_meta, s_g1, s_conv, s_poolw, s_pscale, s_g2, s_g3, s_g4 = small_out
    b_win, b_wout, b_wg, b_wu, b_wd = big_out

    def leaf(k):
        return [s_meta[k], s_g1[k], b_win[k][None], s_conv[k][None], s_poolw[k].reshape(pool_w.shape), s_pscale[k],
                b_wout[k][None], s_g2[k], s_g3[k], b_wg[k][None], b_wu[k][None], b_wd[k][None], s_g4[k]]

    return (loss.reshape(()), grad_x, *leaf(0), *leaf(1), *leaf(2), *leaf(3))
```

```python
import functools

import jax
import jax.numpy as jnp
from jax import lax
from jax.experimental import pallas as pl
from jax.experimental.pallas import tpu as pltpu

F32 = jnp.float32
BF16 = jnp.bfloat16
MESH = pl.DeviceIdType.MESH

D_MODEL = 1024
D_CONV = 512
D_POOL = 512
POOL_GROUP = 128
N_POOL_GROUPS = 4
D_IN_PROJ = 2048
D_FF = 2816
N_CHIPS = 4
FF_SHARD = D_FF // N_CHIPS
IN_SHARD = D_IN_PROJ // N_CHIPS
OUT_SHARD = D_MODEL // N_CHIPS
D_Z = 3 * IN_SHARD
N_META = 16
HALO = 16
RMS_EPS = 1e-6

ADAM_LR = 0.001
ADAM_B1 = 0.9
ADAM_B2 = 0.999
ADAM_EPS = 1e-08
ADAM_WD = 0.01
ADAM_STEP = 10

TM_MIX_FWD = 512
TM_MIX_BWD = 512
SUB_MIX_BWD = 512
TM_FFN = 256
TK_DW = 1024
FF_CHUNK = 1024
VMEM_LIMIT = 56 * 1024 * 1024


def _cparams(n_grid):
    return pltpu.CompilerParams(dimension_semantics=("arbitrary",) * n_grid, vmem_limit_bytes=VMEM_LIMIT)


def _dot(a, b):
    return jnp.dot(a, b, preferred_element_type=F32)


def _dot_nt(a, b):
    return lax.dot_general(a, b, (((1,), (1,)), ((), ())), preferred_element_type=F32)


def _dot_tn(a, b):
    return lax.dot_general(a, b, (((0,), (0,)), ((), ())), preferred_element_type=F32)


def _rows8(v):
    r, c = v.shape
    return v.reshape(r // 8, 8, c).sum(axis=0)


def _rstd(v):
    return lax.rsqrt(jnp.mean(v * v, axis=-1, keepdims=True) + RMS_EPS)


def _rms_bwd(dy, xhat, rstd, gain):
    dyg = dy * gain
    return rstd * (dyg - xhat * jnp.mean(dyg * xhat, axis=-1, keepdims=True))


def _sigmoid(v):
    return 1.0 / (1.0 + jnp.exp(-v))


def _gcols(g):
    return slice(g * POOL_GROUP, (g + 1) * POOL_GROUP)


def _window_sum(e, g, ahead):
    n = e.shape[0]
    w = e
    for level in range(g + 1):
        shift = 1 << level
        w = w + pltpu.roll(w, (n - shift) if ahead else shift, 0)
    return w


def _pool_fwd(pb, g, n):
    e = pb[0:HALO + n, _gcols(g)]
    return _window_sum(e, g, False)[HALO:, :] * (1.0 / (2 << g)) - e[HALO:, :]


def _pool_bwd(qb, g, r0, n):
    e = qb[r0:r0 + n + HALO, _gcols(g)]
    return _window_sum(e, g, True)[0:n, :] * (1.0 / (2 << g)) - e[0:n, :]


def _full(shape):
    nd = len(shape)
    return pl.BlockSpec(shape, lambda *_: (0,) * nd)


ANY = pl.BlockSpec(memory_space=pl.ANY)


def _mesh_pos():
    x, y, c = lax.axis_index("x"), lax.axis_index("y"), lax.axis_index("c")
    chips = [(1 - x, y), (x, 1 - y), (1 - x, 1 - y)]
    return x, y, c, chips


def _half(ref, h):
    hr = ref.shape[0] // 2
    return ref.at[pl.ds(h * hr, hr), :]


class _AllGather:
    PER_ARRAY = 9

    def __init__(self, ins, outs, send_sems, recv_sems):
        self.ins, self.outs, self.send_sems, self.recv_sems = ins, outs, send_sems, recv_sems
        self.n = len(ins)

    @classmethod
    def scratch(cls, n):
        return [pltpu.SemaphoreType.DMA((cls.PER_ARRAY * n,)), pltpu.SemaphoreType.DMA((cls.PER_ARRAY * n,))]

    @staticmethod
    def out_shape(shards):
        return [jax.ShapeDtypeStruct((N_CHIPS,) + s.shape, s.dtype) for s in shards]

    def _copy(self, a, k, src, dst, to):
        i = self.PER_ARRAY * a + k
        return pltpu.make_async_remote_copy(src_ref=src, dst_ref=dst, send_sem=self.send_sems.at[i],
                                            recv_sem=self.recv_sems.at[i], device_id=to, device_id_type=MESH)

    def _piece(self, a, chip, piece, h=None):
        h = lax.axis_index("c") if h is None else h
        rows = self.ins[a].shape[0] // 4
        return self.outs[a].at[chip].at[pl.ds((2 * h + piece) * rows, rows), :]

    def _own(self, a, k):
        x, y, c, chips = _mesh_pos()
        piece = (1, 0, 0, 1)[k]
        rows = self.ins[a].shape[0] // 4
        src = self.ins[a].at[pl.ds((2 * c + piece) * rows, rows), :]
        return self._copy(a, k, src, self._piece(a, 2 * x + y, piece), (*chips[k // 2], c))

    def _relay(self, a, k):
        x, y, c, chips = _mesh_pos()
        source, to, piece = (chips[1], chips[0], 0) if k == 4 else (chips[0], chips[1], 1)
        rows = self._piece(a, 2 * source[0] + source[1], piece)
        return self._copy(a, k, rows, rows, (*to, c))

    def _sibling(self, a, k, h):
        x, y, c, chips = _mesh_pos()
        chip = chips[k - 6]
        slot = _half(self.outs[a].at[2 * chip[0] + chip[1]], h)
        return self._copy(a, k, slot, slot, (x, y, 1 - c))

    def start(self, arrays=None):
        for a in (range(self.n) if arrays is None else arrays):
            for k in range(4):
                self._own(a, k).start()

    def relay(self, a):
        self._own(a, 2).wait_recv()
        self._relay(a, 4).start()
        self._own(a, 0).wait_recv()
        self._relay(a, 5).start()

    def forward(self, a):
        c = lax.axis_index("c")
        self._own(a, 1).wait_recv()
        self._sibling(a, 6, c).start()
        self._own(a, 3).wait_recv()
        self._sibling(a, 7, c).start()
        self._relay(a, 4).wait_recv()
        self._relay(a, 5).wait_recv()
        self._sibling(a, 8, c).start()

    def finish(self, arrays=None):
        c = lax.axis_index("c")
        arrays = range(self.n) if arrays is None else arrays
        for a in arrays:
            for k in range(6, 9):
                self._sibling(a, k, 1 - c).wait_recv()
        for a in arrays:
            for k in range(4):
                self._own(a, k).wait_send()
            for k in range(4, 6):
                self._relay(a, k).wait_send()
            for k in range(6, 9):
                self._sibling(a, k, c).wait_send()


class _ExchangeHalves:
    def __init__(self, ins, recvs, send_sems, recv_sems):
        self.ins, self.recvs, self.send_sems, self.recv_sems = ins, recvs, send_sems, recv_sems

    @staticmethod
    def scratch(n):
        return [pltpu.SemaphoreType.DMA((n,)), pltpu.SemaphoreType.DMA((n,))]

    @staticmethod
    def out_shape(grads):
        return [jax.ShapeDtypeStruct((g.shape[0], g.shape[1] // 2, g.shape[2]), g.dtype) for g in grads]

    def _copies(self):
        x, y, c, _ = _mesh_pos()
        out = []
        for a, (src, dst) in enumerate(zip(self.ins, self.recvs)):
            hr = src.shape[1] // 2
            out.append(pltpu.make_async_remote_copy(
                src_ref=src.at[:, pl.ds((1 - c) * hr, hr), :], dst_ref=dst, send_sem=self.send_sems.at[a],
                recv_sem=self.recv_sems.at[a], device_id=(x, y, 1 - c), device_id_type=MESH))
        return out

    def start(self):
        for cp in self._copies():
            cp.start()

    def finish(self):
        for cp in self._copies():
            cp.wait()


class _ScatterToChips:
    def __init__(self, ins, rbufs, send_sems, recv_sems):
        self.ins, self.rbufs, self.send_sems, self.recv_sems = ins, rbufs, send_sems, recv_sems

    @staticmethod
    def scratch(n):
        return [pltpu.SemaphoreType.DMA((3 * n,)), pltpu.SemaphoreType.DMA((3 * n,))]

    @staticmethod
    def out_shape(sums):
        return [jax.ShapeDtypeStruct((3,) + s.shape[1:], BF16) for s in sums]

    def _copies(self):
        x, y, c, chips = _mesh_pos()
        out = []
        for a, (src, dst) in enumerate(zip(self.ins, self.rbufs)):
            for k, chip in enumerate(chips):
                out.append(pltpu.make_async_remote_copy(
                    src_ref=src.at[2 * chip[0] + chip[1]], dst_ref=dst.at[k], send_sem=self.send_sems.at[3 * a + k],
                    recv_sem=self.recv_sems.at[3 * a + k], device_id=(*chip, c), device_id_type=MESH))
        return out

    def start(self):
        for cp in self._copies():
            cp.start()

    def finish(self):
        for cp in self._copies():
            cp.wait()


HBM = pl.BlockSpec(memory_space=pltpu.HBM)
SEM = pl.BlockSpec(memory_space=pltpu.SEMAPHORE)


class _SplitComm:
    def __init__(self, name, exchanged, scattered):
        self.name, self.n_ex, self.n_sc = name, len(exchanged), len(scattered)
        self.n_copies = self.n_ex + 3 * self.n_sc
        zones = ([lax.empty((g.shape[0], g.shape[1] // 2, g.shape[2]), g.dtype) for g in exchanged]
                 + [lax.empty((3,) + s.shape[1:], s.dtype) for s in scattered])
        self.buffers = [pltpu.with_memory_space_constraint(v, pltpu.HBM)
                        for v in list(exchanged) + list(scattered) + zones]

    def _copies(self, bufs, send_sems, recv_sems):
        x, y, c, chips = _mesh_pos()
        n_src = self.n_ex + self.n_sc
        out = []
        for a in range(self.n_ex):
            hr = bufs[a].shape[1] // 2
            out.append(pltpu.make_async_remote_copy(
                src_ref=bufs[a].at[:, pl.ds((1 - c) * hr, hr), :], dst_ref=bufs[n_src + a], send_sem=send_sems[a],
                recv_sem=recv_sems[a], device_id=(x, y, 1 - c), device_id_type=MESH))
        for a in range(self.n_sc):
            for k, chip in enumerate(chips):
                i = self.n_ex + 3 * a + k
                out.append(pltpu.make_async_remote_copy(
                    src_ref=bufs[self.n_ex + a].at[2 * chip[0] + chip[1]], dst_ref=bufs[n_src + self.n_ex + a].at[k],
                    send_sem=send_sems[i], recv_sem=recv_sems[i], device_id=(*chip, c), device_id_type=MESH))
        return out

    def start(self):
        n_buf, n_cp = len(self.buffers), self.n_copies

        def body(*refs):
            bufs = refs[:n_buf]
            send_sems, recv_sems = refs[n_buf:n_buf + n_cp], refs[n_buf + n_cp:n_buf + 2 * n_cp]
            for cp in self._copies(bufs, send_sems, recv_sems):
                cp.start()
            refs[-1][...] = jnp.zeros_like(refs[-1])

        outs = pl.pallas_call(
            body, name=self.name + "_start",
            out_shape=[pltpu.SemaphoreType.DMA(())] * (2 * n_cp) + [pltpu.HBM(b.shape, b.dtype) for b in self.buffers]
            + [jax.ShapeDtypeStruct((8, 128), F32)],
            in_specs=[HBM] * n_buf, out_specs=[SEM] * (2 * n_cp) + [HBM] * n_buf + [pl.BlockSpec(memory_space=pltpu.VMEM)],
            input_output_aliases={i: 2 * n_cp + i for i in range(n_buf)},
            compiler_params=pltpu.CompilerParams(has_side_effects=pltpu.SideEffectType.DATAFLOW_SIDE_EFFECTING),
        )(*self.buffers)
        self.sems, self.buffers = outs[:2 * n_cp], outs[2 * n_cp:2 * n_cp + n_buf]
        return outs[-1]

    def wait(self, *after):
        n_buf, n_cp = len(self.buffers), self.n_copies

        def body(*refs):
            bufs = refs[:n_buf]
            send_sems, recv_sems = refs[n_buf:n_buf + n_cp], refs[n_buf + n_cp:n_buf + 2 * n_cp]
            for cp in self._copies(bufs, send_sems, recv_sems):
                cp.wait_send()
                cp.wait_recv()

        outs = pl.pallas_call(
            body, name=self.name + "_wait", out_shape=[pltpu.HBM(b.shape, b.dtype) for b in self.buffers],
            in_specs=[HBM] * n_buf + [SEM] * (2 * n_cp) + [ANY] * len(after), out_specs=[HBM] * n_buf,
            input_output_aliases={i: i for i in range(n_buf)},
            compiler_params=pltpu.CompilerParams(has_side_effects=pltpu.SideEffectType.DATAFLOW_SIDE_EFFECTING),
        )(*self.buffers, *self.sems, *after)
        self.exchanged = outs[:self.n_ex]
        zones = outs[self.n_ex + self.n_sc:]
        return zones[:self.n_ex], zones[self.n_ex:]


def _gather_halves(halves, name, after=None):
    n = len(halves)
    extra = [] if after is None else [after]

    def body(*refs):
        ins, outs = refs[:n], refs[n + len(extra):2 * n + len(extra)]
        send_sems, recv_sems = refs[2 * n + len(extra):]
        x, y, c, _ = _mesh_pos()
        sib = (x, y, 1 - c)
        remote = [pltpu.make_async_remote_copy(src_ref=ins[a].at[c], dst_ref=outs[a].at[c],
                                               send_sem=send_sems.at[a], recv_sem=recv_sems.at[a],
                                               device_id=sib, device_id_type=MESH) for a in range(n)]
        for cp in remote:
            cp.start()
        for a in range(n):
            pltpu.make_async_remote_copy(src_ref=ins[a].at[1 - c], dst_ref=outs[a].at[1 - c], send_sem=send_sems.at[a],
                                         recv_sem=recv_sems.at[a], device_id=sib, device_id_type=MESH).wait_recv()
        for cp in remote:
            cp.wait_send()

    return pl.pallas_call(
        body, name=name,
        out_shape=[jax.ShapeDtypeStruct(h.shape, F32) for h in halves],
        in_specs=[ANY] * (n + len(extra)), out_specs=[ANY] * n, input_output_aliases={a: a for a in range(n)},
        scratch_shapes=[pltpu.SemaphoreType.DMA((n,)), pltpu.SemaphoreType.DMA((n,))],
    )(*halves, *extra)


SMALL_A_ROWS = 24
SMALL_B_ROWS = 8
SMALL_C_ROWS = N_POOL_GROUPS * POOL_GROUP


class _AllReduceSmall:
    N_IN = 10
    SHAPES = [(SMALL_A_ROWS, D_MODEL), (SMALL_B_ROWS, D_CONV), (SMALL_C_ROWS, POOL_GROUP)]

    def __init__(self, ins, outs, scratch):
        self.ins, self.outs = ins, outs
        self.bufs, self.rcvs, self.send_sems, self.recv_sems = scratch[:3], scratch[3:6], scratch[6], scratch[7]

    @classmethod
    def scratch(cls):
        return ([pltpu.VMEM((3,) + s, F32) for s in cls.SHAPES] + [pltpu.VMEM((3,) + s, F32) for s in cls.SHAPES]
                + [pltpu.SemaphoreType.DMA((9,)), pltpu.SemaphoreType.DMA((9,))])

    @classmethod
    def out_shape(cls):
        return [jax.ShapeDtypeStruct(s, F32) for s in cls.SHAPES]

    def _copies(self, st):
        x, y, c, _ = _mesh_pos()
        peer = [(x, y, 1 - c), (1 - x, y, c), (x, 1 - y, c)][st]
        return [pltpu.make_async_remote_copy(
            src_ref=buf.at[st], dst_ref=rcv.at[st], send_sem=self.send_sems.at[3 * st + i],
            recv_sem=self.recv_sems.at[3 * st + i], device_id=peer, device_id_type=MESH)
            for i, (buf, rcv) in enumerate(zip(self.bufs, self.rcvs))]

    def pack_and_send(self):
        dg1_ref, dg1m_ref, dg2_ref, dg3_ref, dg4_ref, loss_ref, dmeta_ref, dsc_ref, dcw_ref, dpw_ref = self.ins
        a_buf, b_buf, c_buf = self.bufs

        def rowsum(v):
            return jnp.sum(v, axis=0, keepdims=True)

        a_buf[0, 0:1, :] = rowsum(dg1_ref[...] + dg1m_ref[...])
        a_buf[0, 1:2, :] = rowsum(dg2_ref[...])
        a_buf[0, 2:3, :] = rowsum(dg3_ref[...])
        a_buf[0, 3:4, :] = rowsum(dg4_ref[...])
        loss = jnp.sum(rowsum(loss_ref[...]), axis=1, keepdims=True) * (0.5 / D_MODEL)
        a_buf[0, 4:5, :] = jnp.broadcast_to(loss, (1, D_MODEL))
        a_buf[0, 5:8, :] = jnp.zeros((3, D_MODEL), F32)
        a_buf[0, 8:24, :] = dmeta_ref[...]
        b_buf[0, 0:1, :] = rowsum(dsc_ref[...])
        for k in range(3):
            b_buf[0, 1 + k:2 + k, :] = rowsum(dcw_ref[8 * k:8 * k + 8, :])
        b_buf[0, 4:8, :] = jnp.zeros((4, D_CONV), F32)
        c_buf[0] = dpw_ref[...]
        for cp in self._copies(0):
            cp.start()

    def combine(self, st):
        for cp in self._copies(st):
            cp.wait()
        if st < 2:
            for buf, rcv in zip(self.bufs, self.rcvs):
                buf[st + 1] = buf[st] + rcv[st]
            for cp in self._copies(st + 1):
                cp.start()
        else:
            for out, buf, rcv in zip(self.outs, self.bufs, self.rcvs):
                out[...] = buf[st] + rcv[st]


def _row_block(rows):
    for cand in (512, 448, 384, 352, 320, 256, 128, 64, 32, 16):
        if rows % cand == 0:
            return cand
    return rows


def _add_pairs_multi(grads, recvs, place):
    n = len(grads)
    n_sh = grads[0].shape[0]
    halves = [g.shape[1] // 2 for g in grads]
    n_steps = halves[0] // _row_block(halves[0])
    blocks = [(hr // n_steps, g.shape[2]) for hr, g in zip(halves, grads)]

    def body(place_ref, *refs):
        for a_ref, b_ref, o_ref in zip(refs[:n], refs[n:2 * n], refs[2 * n:]):
            o_ref[...] = (a_ref[0] + b_ref[...]).astype(BF16)

    return pl.pallas_call(
        body, name="grad_add_pairs",
        grid_spec=pltpu.PrefetchScalarGridSpec(
            num_scalar_prefetch=1, grid=(n_sh, n_steps),
            in_specs=[pl.BlockSpec((1, 1, br, cols), lambda j, i, p: (j, p[1], i, 0)) for br, cols in blocks]
            + [pl.BlockSpec((1, br, cols), lambda j, i, p: (j, i, 0)) for br, cols in blocks],
            out_specs=[pl.BlockSpec((1, br, cols), lambda j, i, p: (j, i, 0)) for br, cols in blocks]),
        out_shape=[jax.ShapeDtypeStruct((n_sh, hr, g.shape[2]), BF16) for hr, g in zip(halves, grads)],
        compiler_params=_cparams(2),
    )(place, *[g.reshape(n_sh, 2, hr, g.shape[2]) for hr, g in zip(halves, grads)], *recvs)


def _add_pairs(grad, recv, place):
    return _add_pairs_multi([grad], [recv], place)[0]


def _add_chips(grads, recvs, rbufs, place, after=None, name="grad_add_chips"):
    n = len(grads)
    n_sh = grads[0].shape[0]
    halves = [g.shape[1] // 2 for g in grads]
    n_steps = halves[0] // _row_block(halves[0])
    blocks = [(hr // n_steps, g.shape[2]) for hr, g in zip(halves, grads)]
    extra = [] if after is None else [after]

    def body(place_ref, *refs):
        for a_ref, b_ref, r_ref, o_ref in zip(refs[:n], refs[n:2 * n], refs[2 * n:3 * n], refs[3 * n + len(extra):]):
            own = a_ref[0, 0] + b_ref[0]
            o_ref[0] = ((own + r_ref[0].astype(F32)) + r_ref[1].astype(F32)) + r_ref[2].astype(F32)

    return pl.pallas_call(
        body, name=name,
        grid_spec=pltpu.PrefetchScalarGridSpec(
            num_scalar_prefetch=1, grid=(n_steps,),
            in_specs=[pl.BlockSpec((1, 1, br, cols), lambda i, p: (p[0], p[1], i, 0)) for br, cols in blocks]
            + [pl.BlockSpec((1, br, cols), lambda i, p: (p[0], i, 0)) for br, cols in blocks]
            + [pl.BlockSpec((3, br, cols), lambda i, p: (0, i, 0)) for br, cols in blocks]
            + [pl.BlockSpec((8, 128), lambda i, p: (0, 0))] * len(extra),
            out_specs=[pl.BlockSpec((1, br, cols), lambda i, p: (p[1], i, 0)) for br, cols in blocks]),
        out_shape=[jax.ShapeDtypeStruct((2, hr, g.shape[2]), F32) for hr, g in zip(halves, grads)],
        compiler_params=_cparams(1),
    )(place, *[g.reshape(n_sh, 2, hr, g.shape[2]) for hr, g in zip(halves, grads)], *recvs, *rbufs, *extra)


def _adamw_math(w, g, m, v):
    m2 = ADAM_B1 * m + (1.0 - ADAM_B1) * g
    v2 = ADAM_B2 * v + (1.0 - ADAM_B2) * (g * g)
    m_hat = m2 / (1.0 - ADAM_B1 ** ADAM_STEP)
    v_hat = v2 / (1.0 - ADAM_B2 ** ADAM_STEP)
    delta = -ADAM_LR * (m_hat / (jnp.sqrt(v_hat) + ADAM_EPS) + ADAM_WD * w)
    return delta, m2, v2


def _adamw_big(groups):
    n = len(groups)
    rows, cols = groups[0][0].shape
    br = _row_block(rows)
    if n > 1 and br % 16 == 0:
        br //= 2

    def body(*refs):
        for i in range(n):
            w_ref, g_ref, m_ref, v_ref = refs[4 * i:4 * i + 4]
            g_out_ref, d_ref, m2_ref, v2_ref = refs[4 * n + 4 * i:4 * n + 4 * i + 4]
            g = g_ref[...]
            d, m2, v2 = _adamw_math(w_ref[...], g, m_ref[...], v_ref[...])
            g_out_ref[...] = g
            d_ref[...] = d
            m2_ref[...] = m2
            v2_ref[...] = v2

    spec = pl.BlockSpec((br, cols), lambda i: (i, 0))
    outs = pl.pallas_call(
        body, name="adamw_big", grid=(rows // br,),
        out_shape=[jax.ShapeDtypeStruct((rows, cols), F32)] * (4 * n),
        in_specs=[spec] * (4 * n), out_specs=[spec] * (4 * n), compiler_params=_cparams(1),
    )(*[a for grp in groups for a in grp])
    return [list(outs[4 * i:4 * i + 4]) for i in range(n)]


def _adamw_small(place, reduced, params):
    n = len(params)
    meta_cols, conv_cols = D_MODEL // N_CHIPS, D_CONV // N_CHIPS

    def body(place_ref, a_ref, b_ref, c_ref, *refs):
        ins, loss_ref, outs = refs[:3 * n], refs[3 * n], refs[3 * n + 1:]
        chip = place_ref[0]

        def own_cols(ref, r0, n_r, width):
            out = ref[r0:r0 + n_r, 0:width]
            for j in range(1, N_CHIPS):
                out = jnp.where(chip == j, ref[r0:r0 + n_r, j * width:(j + 1) * width], out)
            return out

        grads = [own_cols(a_ref, 8, N_META, meta_cols), a_ref[0:1, :], own_cols(b_ref, 1, 3, conv_cols), c_ref[...],
                 b_ref[0:1, :], a_ref[1:2, :], a_ref[2:3, :], a_ref[3:4, :]]
        loss_ref[...] = a_ref[4:5, 0:1]
        for i, g in enumerate(grads):
            w, m, v = (r[...] for r in ins[3 * i:3 * i + 3])
            for o, val in zip(outs[4 * i:4 * i + 4], (g,) + _adamw_math(w, g, m, v)):
                o[...] = val

    vm = pl.BlockSpec(memory_space=pltpu.VMEM)
    flat = [a for grp in params for a in grp]
    out_shape = ([jax.ShapeDtypeStruct((1, 1), F32)]
                 + [jax.ShapeDtypeStruct(grp[0].shape, F32) for grp in params for _ in range(4)])
    outs = pl.pallas_call(body, name="adamw_small", out_shape=out_shape,
                          in_specs=[pl.BlockSpec(memory_space=pltpu.SMEM)] + [vm] * (3 + 3 * n),
                          out_specs=[vm] * (1 + 4 * n))(place, *reduced, *flat)
    return outs[0], [tuple(outs[1 + 4 * i:5 + 4 * i]) for i in range(n)]


def _load_gathered(gathered, shards, dst_slots, sems):
    n = len(gathered)
    me = 2 * lax.axis_index("x") + lax.axis_index("y")

    def copies(j, own):
        return [pltpu.make_async_copy(shards[a] if own else gathered[a].at[j], dst_slots[a](j), sems.at[n * j + a])
                for a in range(n)]

    for wait in (False, True):
        for j in range(N_CHIPS):
            for own in (False, True):
                @pl.when((me == j) == own)
                def _():
                    for cp in copies(j, own):
                        cp.wait() if wait else cp.start()


N_MIX_SHARDS = 3


def _mixer_fwd(x3, g1, g2, poolw, pscale, shards, ffn_f32):
    n_seq, seq, _ = x3.shape
    tm = min(TM_MIX_FWD, seq)
    n_t = seq // tm
    n_steps = n_seq * n_t
    n_ffn = len(ffn_f32)
    n_ag = N_MIX_SHARDS + n_ffn
    ffn_bf16 = [jax.ShapeDtypeStruct(w.shape, BF16) for w in ffn_f32]
    small_rows = shards[2].shape[0]
    conv_cols = D_CONV // N_CHIPS

    def body(x_ref, g1_ref, g2_ref, pw_ref, ps_ref, *rest):
        ffn_f32_refs, ffn_bf_refs = rest[N_MIX_SHARDS:n_ag], rest[2 * n_ag + 10:2 * n_ag + 10 + n_ffn]
        ag = _AllGather(list(rest[:N_MIX_SHARDS]) + list(ffn_bf_refs), rest[n_ag + 10:2 * n_ag + 10], *rest[-2:])
        (z_ref, m_ref, h1_ref, a_ref, conv_ref, pooled_ref, yc_ref, zm_ref, meta_ref,
         cw_ref) = rest[n_ag:n_ag + 10]
        win_v, wout_v, small_v, cvb, pb, load_sems, stage_v, cast_v, cast_sems = rest[2 * n_ag + 10 + n_ffn:-2]
        s, t = pl.program_id(0), pl.program_id(1)
        step = s * n_t + t

        def round_ffn_piece(i):
            load = pltpu.make_async_copy(ffn_f32_refs[i], stage_v, cast_sems.at[0])
            load.start()
            load.wait()
            cast_v[...] = stage_v[...].astype(BF16)
            store = pltpu.make_async_copy(cast_v, ffn_bf_refs[i], cast_sems.at[1])
            store.start()
            store.wait()

        @pl.when(step == 0)
        def _():
            ag.start(range(N_MIX_SHARDS))
            round_ffn_piece(0)
            for a in range(N_MIX_SHARDS):
                ag.relay(a)
            round_ffn_piece(1)
            for a in range(N_MIX_SHARDS):
                ag.forward(a)
            round_ffn_piece(2)
            ag.finish(range(N_MIX_SHARDS))
            ag.start(range(N_MIX_SHARDS, n_ag))
            _load_gathered(ag.outs[:N_MIX_SHARDS], ag.ins[:N_MIX_SHARDS],
                           [lambda j: win_v.at[j], lambda j: wout_v.at[pl.ds(j * OUT_SHARD, OUT_SHARD), :],
                            lambda j: small_v.at[j]], load_sems)

            meta = jnp.concatenate([small_v[j, 0:N_META, :] for j in range(N_CHIPS)], axis=1)
            meta_ref[...] = meta
            cw_ref[...] = jnp.concatenate([small_v[j, N_META:N_META + 3, 0:conv_cols] for j in range(N_CHIPS)], axis=1)
            a_meta = (meta * _rstd(meta) * g1_ref[...]).astype(BF16)
            for j in range(N_CHIPS):
                zm_ref[:, j * IN_SHARD:(j + 1) * IN_SHARD] = _dot(a_meta, win_v[j])

        for i in range(n_ffn):
            @pl.when(step == ((3 * i + 4) * n_steps) // (4 * n_ffn + 4))
            def _():
                ag.relay(N_MIX_SHARDS + i)

        for i in range(n_ffn):
            @pl.when(step == min(n_steps // 2 + ((i + 1) * n_steps) // (2 * n_ffn + 2), n_steps - 1))
            def _():
                ag.forward(N_MIX_SHARDS + i)

        @pl.when(t == 0)
        def _():
            cvb[0:HALO, :] = zm_ref[:, IN_SHARD:2 * IN_SHARD] * zm_ref[:, 2 * IN_SHARD:3 * IN_SHARD]
            pb[0:HALO, :] = zm_ref[:, 3 * IN_SHARD:4 * IN_SHARD]

        @pl.when(t > 0)
        def _():
            cvb[0:HALO, :] = cvb[tm:tm + HALO, :]
            pb[0:HALO, :] = pb[tm:tm + HALO, :]

        xt = x_ref[0]
        a = (xt * _rstd(xt) * g1_ref[...]).astype(BF16)
        a_ref[...] = a
        zb = _dot(a, win_v[0])
        zc = _dot(a, win_v[1])
        zv = _dot(a, win_v[2])
        zp = _dot(a, win_v[3])
        z_ref[0, :, 0:IN_SHARD] = zb
        z_ref[0, :, IN_SHARD:2 * IN_SHARD] = zc
        z_ref[0, :, 2 * IN_SHARD:3 * IN_SHARD] = zv
        cv = zc * zv
        cvb[HALO:HALO + tm, :] = cv
        pb[HALO:HALO + tm, :] = zp
        cw = cw_ref[...]
        conv = cw[0:1] * cvb[HALO - 2:HALO - 2 + tm, :] + cw[1:2] * cvb[HALO - 1:HALO - 1 + tm, :] + cw[2:3] * cv
        conv_ref[...] = conv
        parts = [(zb * conv).astype(BF16)]
        for g in range(N_POOL_GROUPS):
            pooled = _pool_fwd(pb, g, tm).astype(BF16)
            pooled_ref[:, _gcols(g)] = pooled
            parts.append((_dot(pooled, pw_ref[g]) * ps_ref[:, _gcols(g)]).astype(BF16))
        ycat = jnp.concatenate(parts, axis=1)
        yc_ref[...] = ycat
        m = _dot(ycat, wout_v[...])
        m_ref[0] = m
        h1_ref[0] = xt + m * _rstd(m) * g2_ref[...]

        @pl.when(step == n_steps - 1)
        def _():
            ag.finish(range(N_MIX_SHARDS, n_ag))

    n_rows = n_seq * seq
    row = lambda c: pl.BlockSpec((1, tm, c), lambda s, t: (s, t, 0))
    row2 = lambda c: pl.BlockSpec((tm, c), lambda s, t: (s * n_t + t, 0))
    outs = pl.pallas_call(
        body, name="mixer_fwd", grid=(n_seq, n_t),
        out_shape=[jax.ShapeDtypeStruct((n_seq, seq, D_Z), F32), jax.ShapeDtypeStruct((n_seq, seq, D_MODEL), F32),
                   jax.ShapeDtypeStruct((n_seq, seq, D_MODEL), F32), jax.ShapeDtypeStruct((n_rows, D_MODEL), BF16),
                   jax.ShapeDtypeStruct((n_rows, D_CONV), F32), jax.ShapeDtypeStruct((n_rows, D_POOL), BF16),
                   jax.ShapeDtypeStruct((n_rows, D_MODEL), BF16), jax.ShapeDtypeStruct((N_META, D_IN_PROJ), F32),
                   jax.ShapeDtypeStruct((N_META, D_MODEL), F32), jax.ShapeDtypeStruct((3, D_CONV), F32)]
        + _AllGather.out_shape(list(shards) + ffn_bf16) + ffn_bf16,
        in_specs=[row(D_MODEL), _full((1, D_MODEL)), _full((1, D_MODEL)),
                  _full((N_POOL_GROUPS, POOL_GROUP, POOL_GROUP)), _full((1, D_POOL))] + [ANY] * n_ag,
        out_specs=[row(D_Z), row(D_MODEL), row(D_MODEL), row2(D_MODEL), row2(D_CONV), row2(D_POOL), row2(D_MODEL),
                   _full((N_META, D_IN_PROJ)), _full((N_META, D_MODEL)), _full((3, D_CONV))] + [ANY] * (n_ag + n_ffn),
        scratch_shapes=[pltpu.VMEM((N_CHIPS, D_MODEL, IN_SHARD), BF16), pltpu.VMEM((D_MODEL, D_MODEL), BF16),
                        pltpu.VMEM((N_CHIPS, small_rows, D_MODEL // N_CHIPS), F32),
                        pltpu.VMEM((HALO + tm, D_CONV), F32), pltpu.VMEM((HALO + tm, D_POOL), F32),
                        pltpu.SemaphoreType.DMA((N_MIX_SHARDS * N_CHIPS,)),
                        pltpu.VMEM(ffn_f32[0].shape, F32), pltpu.VMEM(ffn_f32[0].shape, BF16),
                        pltpu.SemaphoreType.DMA((2,))] + _AllGather.scratch(n_ag),
        compiler_params=_cparams(2),
    )(x3, g1, g2, poolw, pscale, *shards, *ffn_f32)
    return outs[:10], outs[10:10 + n_ag], outs[10 + n_ag:]


def _ffn_chunks():
    out, r0 = [], 0
    while r0 < D_FF:
        out.append((r0, min(FF_CHUNK, D_FF - r0)))
        r0 += FF_CHUNK
    return out


def _ffn_fwd_bwd(h1, target, g3, g4, gathered, shards):
    n_rows = h1.shape[0]
    tm = min(TM_FFN, n_rows)
    chunks = _ffn_chunks()

    def body(h1_ref, t_ref, g3_ref, g4_ref, wg_all, wu_all, wd_all, wg_s, wu_s, wd_s,
             dh1_ref, f_ref, dd_ref, ds_ref, du_ref, gg_ref, loss_ref, dg3_ref, dg4_ref,
             wg_v, wu_v, wd_v, s_sc, u_sc, sems):
        @pl.when(pl.program_id(0) == 0)
        def _():
            _load_gathered([wg_all, wu_all, wd_all], [wg_s, wu_s, wd_s],
                           [functools.partial(lambda v, j: v.at[pl.ds(j * FF_SHARD, FF_SHARD), :], v)
                            for v in (wg_v, wu_v, wd_v)], sems)
            loss_ref[...] = jnp.zeros_like(loss_ref)
            dg3_ref[...] = jnp.zeros_like(dg3_ref)
            dg4_ref[...] = jnp.zeros_like(dg4_ref)

        h1v = h1_ref[...]
        r3 = _rstd(h1v)
        hh = h1v * r3
        g3v, g4v = g3_ref[...], g4_ref[...]
        f = (hh * g3v).astype(BF16)
        f_ref[...] = f
        d = jnp.zeros((tm, D_MODEL), F32)
        for r0, sz in chunks:
            s = _dot_nt(f, wg_v[r0:r0 + sz, :])
            u = _dot_nt(f, wu_v[r0:r0 + sz, :])
            s_sc[:, r0:r0 + sz] = s
            u_sc[:, r0:r0 + sz] = u
            gc = (s * _sigmoid(s) * u).astype(BF16)
            gg_ref[:, r0:r0 + sz] = gc
            d = d + _dot(gc, wd_v[r0:r0 + sz, :])
        r4 = _rstd(d)
        dh = d * r4
        err = (h1v + dh * g4v) - t_ref[...]
        loss_ref[...] += _rows8(err * err)
        dy = err * (1.0 / D_MODEL)
        dg4_ref[...] += _rows8(dy * dh)
        ddb = _rms_bwd(dy, dh, r4, g4v).astype(BF16)
        dd_ref[...] = ddb
        df = jnp.zeros((tm, D_MODEL), F32)
        for r0, sz in chunks:
            dgg = _dot_nt(ddb, wd_v[r0:r0 + sz, :])
            s = s_sc[:, r0:r0 + sz]
            u = u_sc[:, r0:r0 + sz]
            sig = _sigmoid(s)
            dsc = (dgg * u * (sig * (1.0 + s * (1.0 - sig)))).astype(BF16)
            duc = (dgg * (s * sig)).astype(BF16)
            ds_ref[:, r0:r0 + sz] = dsc
            du_ref[:, r0:r0 + sz] = duc
            df = df + _dot(dsc, wg_v[r0:r0 + sz, :]) + _dot(duc, wu_v[r0:r0 + sz, :])
        dg3_ref[...] += _rows8(df * hh)
        dh1_ref[...] = dy + _rms_bwd(df, hh, r3, g3v)

    row = pl.BlockSpec((tm, D_MODEL), lambda i: (i, 0))
    ffrow = pl.BlockSpec((tm, D_FF), lambda i: (i, 0))
    acc = _full((8, D_MODEL))
    act_bf = jax.ShapeDtypeStruct((n_rows, D_MODEL), BF16)
    ff_bf = jax.ShapeDtypeStruct((n_rows, D_FF), BF16)
    acc_shape = jax.ShapeDtypeStruct((8, D_MODEL), F32)
    w_vmem = pltpu.VMEM((D_FF, D_MODEL), BF16)
    return pl.pallas_call(
        body, name="ffn_fwd_bwd", grid=(n_rows // tm,),
        out_shape=[jax.ShapeDtypeStruct((n_rows, D_MODEL), F32), act_bf, act_bf, ff_bf, ff_bf, ff_bf,
                   acc_shape, acc_shape, acc_shape],
        in_specs=[row, row, _full((1, D_MODEL)), _full((1, D_MODEL))] + [ANY] * 6,
        out_specs=[row, row, row, ffrow, ffrow, ffrow, acc, acc, acc],
        scratch_shapes=[w_vmem, w_vmem, w_vmem, pltpu.VMEM((tm, D_FF), F32), pltpu.VMEM((tm, D_FF), F32),
                        pltpu.SemaphoreType.DMA((3 * N_CHIPS,))],
        compiler_params=_cparams(1),
    )(h1, target, g3, g4, *gathered, *shards)


def _ffn_weight_grads(name, acts, other, exchanged):
    n_rows = other.shape[0]
    n_a, n_ex = len(acts), len(exchanged)
    n_c = n_a
    tk = min(TK_DW, n_rows)
    n_k = n_rows // tk
    half = D_FF // n_c

    def body(other_ref, *rest):
        act_refs = rest[:n_a]
        out_refs = rest[n_a + n_ex:2 * n_a + n_ex]
        c, k = pl.program_id(0), pl.program_id(1)
        if n_ex:
            ex = _ExchangeHalves(rest[n_a:n_a + n_ex], rest[2 * n_a + n_ex:2 * n_a + 2 * n_ex], *rest[-2:])

            @pl.when((c == 0) & (k == 0))
            def _():
                ex.start()

        @pl.when(k == 0)
        def _():
            for o in out_refs:
                o[...] = jnp.zeros_like(o)

        ov = other_ref[...]
        for a, o in zip(act_refs, out_refs):
            o[...] += _dot_tn(a[...], ov)

        if n_ex:
            @pl.when((c == n_c - 1) & (k == n_k - 1))
            def _():
                ex.finish()

    row = pl.BlockSpec((tk, D_MODEL), lambda c, k: (k, 0))
    ffrow = pl.BlockSpec((tk, half), lambda c, k: (k, c))
    out = pl.BlockSpec((half, D_MODEL), lambda c, k: (c, 0))
    outs = pl.pallas_call(
        body, name=name, grid=(n_c, n_k),
        out_shape=[jax.ShapeDtypeStruct((D_FF, D_MODEL), F32)] * n_a + _ExchangeHalves.out_shape(exchanged),
        in_specs=[row] + [ffrow] * n_a + [ANY] * n_ex, out_specs=[out] * n_a + [ANY] * n_ex,
        scratch_shapes=_ExchangeHalves.scratch(n_ex) if n_ex else [],
        compiler_params=_cparams(2),
    )(other, *acts, *exchanged)
    return outs[:n_a], outs[n_a:]


def _mixer_bwd(dh1, m3, z3, conv2, pooled2, x3, zmeta, meta_full, g1, g2, convw, poolw, pscale, gathered, shards,
               after):
    n_seq, seq, _ = x3.shape
    tm = min(TM_MIX_BWD, seq)
    sub = min(SUB_MIX_BWD, tm)
    n_t = seq // tm
    n_out = 13

    def body(dh1_ref, m_ref, z_ref, conv_ref, pooled_ref, x_ref, zm_ref, meta_ref, g1_ref, g2_ref, cw_ref, pw_ref,
             ps_ref, after_ref, win_all, wout_all, win_s, wout_s, *rest):
        (dx_ref, dz_ref, dm_ref, dg1_ref, dg2_ref, dsc_ref, dcw_ref, dpw_ref, dzm_ref, dmeta_ref, dg1m_ref, am_ref,
         dzmb_ref) = rest[:n_out]
        win_v, wout_v, dcb, dqb, mcb, mqb, load_sems = rest[n_out:]
        s, i = pl.program_id(0), pl.program_id(1)
        tr = n_t - 1 - i

        @pl.when((s == 0) & (i == 0))
        def _():
            _load_gathered([win_all, wout_all], [win_s, wout_s],
                           [lambda j: win_v.at[j], lambda j: wout_v.at[pl.ds(j * OUT_SHARD, OUT_SHARD), :]], load_sems)
            for ref in (dg1_ref, dg2_ref, dsc_ref, dcw_ref, dpw_ref, dzm_ref):
                ref[...] = jnp.zeros_like(ref)

        @pl.when(i == 0)
        def _():
            dcb[tm:tm + HALO, :] = jnp.zeros((HALO, D_CONV), F32)
            dqb[tm:tm + HALO, :] = jnp.zeros((HALO, D_POOL), F32)

        @pl.when(i > 0)
        def _():
            dcb[tm:tm + HALO, :] = dcb[0:HALO, :]
            dqb[tm:tm + HALO, :] = dqb[0:HALO, :]

        g1v, g2v = g1_ref[...], g2_ref[...]
        cw = cw_ref[...]

        for r0 in range(tm - sub, -1, -sub):
            rows = slice(r0, r0 + sub)
            dh1v = dh1_ref[0, rows, :]
            mv = m_ref[0, rows, :]
            r2 = _rstd(mv)
            mh = mv * r2
            dg2_ref[...] += _rows8(dh1v * mh)
            dmb = _rms_bwd(dh1v, mh, r2, g2v).astype(BF16)
            dm_ref[rows, :] = dmb
            dyc = _dot_nt(dmb, wout_v[...])
            dyconv = dyc[:, 0:D_CONV]

            for g in range(N_POOL_GROUPS):
                pooled = pooled_ref[rows, _gcols(g)]
                mixed = _dot(pooled, pw_ref[g])
                scale = ps_ref[:, _gcols(g)]
                dyp = dyc[:, D_CONV + g * POOL_GROUP:D_CONV + (g + 1) * POOL_GROUP]
                dsc_ref[:, _gcols(g)] += _rows8(dyp * mixed)
                dmix = (dyp * scale).astype(BF16)
                dpw_ref[g] += _dot_tn(pooled, dmix)
                dqb[rows, _gcols(g)] = _dot_nt(dmix, pw_ref[g])

            zb = z_ref[0, rows, 0:IN_SHARD]
            zc = z_ref[0, rows, IN_SHARD:2 * IN_SHARD]
            zv = z_ref[0, rows, 2 * IN_SHARD:3 * IN_SHARD]
            dconv = dyconv * zb
            dcb[rows, :] = dconv
            d1 = dcb[r0 + 1:r0 + 1 + sub, :]
            d2 = dcb[r0 + 2:r0 + 2 + sub, :]
            dcv = cw[2:3] * dconv + cw[1:2] * d1 + cw[0:1] * d2
            cv = zc * zv
            dcw_ref[0:8, :] += _rows8(cv * d2)
            dcw_ref[8:16, :] += _rows8(cv * d1)
            dcw_ref[16:24, :] += _rows8(cv * dconv)
            dzs = [(dyconv * conv_ref[rows, :]).astype(BF16), (dcv * zv).astype(BF16), (dcv * zc).astype(BF16),
                   jnp.concatenate([_pool_bwd(dqb, g, r0, sub) for g in range(N_POOL_GROUPS)], axis=1).astype(BF16)]
            da = jnp.zeros((sub, D_MODEL), F32)
            for j in range(N_CHIPS):
                dz_ref[j, rows, :] = dzs[j]
                da = da + _dot_nt(dzs[j], win_v[j])
            xt = x_ref[0, rows, :]
            r1 = _rstd(xt)
            xh = xt * r1
            dg1_ref[...] += _rows8(da * xh)
            dx_ref[0, rows, :] = dh1v + _rms_bwd(da, xh, r1, g1v)

        @pl.when(tr == 0)
        def _():
            mcb[0:HALO, :] = jnp.zeros((HALO, D_CONV), F32)
            mqb[0:HALO, :] = jnp.zeros((HALO, D_POOL), F32)
            mcb[HALO:2 * HALO, :] = dcb[0:HALO, :]
            mqb[HALO:2 * HALO, :] = dqb[0:HALO, :]
            m1 = mcb[1:1 + HALO, :]
            m2 = mcb[2:2 + HALO, :]
            zc_m = zm_ref[:, IN_SHARD:2 * IN_SHARD]
            zv_m = zm_ref[:, 2 * IN_SHARD:3 * IN_SHARD]
            cv_m = zc_m * zv_m
            dcw_ref[0:8, :] += _rows8(cv_m * m2)
            dcw_ref[8:16, :] += _rows8(cv_m * m1)
            dcv_m = cw[1:2] * m1 + cw[0:1] * m2
            dzm_ref[:, IN_SHARD:2 * IN_SHARD] += dcv_m * zv_m
            dzm_ref[:, 2 * IN_SHARD:3 * IN_SHARD] += dcv_m * zc_m
            dzm_ref[:, 3 * IN_SHARD:4 * IN_SHARD] += jnp.concatenate(
                [_pool_bwd(mqb, g, 0, HALO) for g in range(N_POOL_GROUPS)], axis=1)

        @pl.when((s == n_seq - 1) & (i == n_t - 1))
        def _():
            xm = meta_ref[...]
            rm = _rstd(xm)
            xmh = xm * rm
            am_ref[...] = (xmh * g1v).astype(BF16)
            da_m = jnp.zeros((N_META, D_MODEL), F32)
            for j in range(N_CHIPS):
                dzj = dzm_ref[:, j * IN_SHARD:(j + 1) * IN_SHARD].astype(BF16)
                dzmb_ref[j] = dzj
                da_m = da_m + _dot_nt(dzj, win_v[j])
            dg1m_ref[...] = _rows8(da_m * xmh)
            dmeta_ref[...] = _rms_bwd(da_m, xmh, rm, g1v)

    row3 = lambda c: pl.BlockSpec((1, tm, c), lambda s, i: (s, n_t - 1 - i, 0))
    row2 = lambda c: pl.BlockSpec((tm, c), lambda s, i: (s * n_t + n_t - 1 - i, 0))
    n_rows = n_seq * seq
    outs = pl.pallas_call(
        body, name="mixer_bwd", grid=(n_seq, n_t),
        out_shape=[jax.ShapeDtypeStruct((n_seq, seq, D_MODEL), F32),
                   jax.ShapeDtypeStruct((N_CHIPS, n_rows, IN_SHARD), BF16), jax.ShapeDtypeStruct((n_rows, D_MODEL), BF16),
                   jax.ShapeDtypeStruct((8, D_MODEL), F32), jax.ShapeDtypeStruct((8, D_MODEL), F32),
                   jax.ShapeDtypeStruct((8, D_POOL), F32), jax.ShapeDtypeStruct((24, D_CONV), F32),
                   jax.ShapeDtypeStruct((N_POOL_GROUPS, POOL_GROUP, POOL_GROUP), F32),
                   jax.ShapeDtypeStruct((N_META, D_IN_PROJ), F32),
                   jax.ShapeDtypeStruct((N_META, D_MODEL), F32), jax.ShapeDtypeStruct((8, D_MODEL), F32),
                   jax.ShapeDtypeStruct((N_META, D_MODEL), BF16),
                   jax.ShapeDtypeStruct((N_CHIPS, N_META, IN_SHARD), BF16)],
        in_specs=[row3(D_MODEL), row3(D_MODEL), row3(D_Z), row2(D_CONV), row2(D_POOL), row3(D_MODEL),
                  _full((N_META, D_IN_PROJ)), _full((N_META, D_MODEL)), _full((1, D_MODEL)), _full((1, D_MODEL)),
                  _full((3, D_CONV)), _full((N_POOL_GROUPS, POOL_GROUP, POOL_GROUP)), _full((1, D_POOL)),
                  _full((8, 128))] + [ANY] * 4,
        out_specs=[row3(D_MODEL), pl.BlockSpec((N_CHIPS, tm, IN_SHARD), lambda s, i: (0, s * n_t + n_t - 1 - i, 0)),
                   row2(D_MODEL),
                   _full((8, D_MODEL)), _full((8, D_MODEL)), _full((8, D_POOL)), _full((24, D_CONV)),
                   _full((N_POOL_GROUPS, POOL_GROUP, POOL_GROUP)), _full((N_META, D_IN_PROJ)),
                   _full((N_META, D_MODEL)), _full((8, D_MODEL)), _full((N_META, D_MODEL)),
                   _full((N_CHIPS, N_META, IN_SHARD))],
        scratch_shapes=[pltpu.VMEM((N_CHIPS, D_MODEL, IN_SHARD), BF16), pltpu.VMEM((D_MODEL, D_MODEL), BF16),
                        pltpu.VMEM((tm + HALO, D_CONV), F32), pltpu.VMEM((tm + HALO, D_POOL), F32),
                        pltpu.VMEM((2 * HALO, D_CONV), F32), pltpu.VMEM((2 * HALO, D_POOL), F32),
                        pltpu.SemaphoreType.DMA((2 * N_CHIPS,))],
        compiler_params=_cparams(2),
    )(dh1, m3, z3, conv2, pooled2, x3, zmeta, meta_full, g1, g2, convw, poolw, pscale, after, *gathered, *shards)
    return outs


def _mixer_weight_grads(a, dz, ycat, dm, a_meta, dz_meta, ffn_sums, small):
    n_rows = a.shape[0]
    tk = min(TK_DW, n_rows)
    n_k = n_rows // tk
    n_sc, n_sm = len(ffn_sums), _AllReduceSmall.N_IN

    def body(a_ref, dz_ref, yc_ref, dm_ref, am_ref, dzm_ref, *rest):
        ins, outs, scratch = rest[:n_sc + n_sm], rest[n_sc + n_sm:2 * n_sc + n_sm + 5], rest[2 * n_sc + n_sm + 5:]
        dwin_ref, dwout_ref = outs[:2]
        scatter = _ScatterToChips(ins[:n_sc], outs[2:2 + n_sc], *scratch[:2])
        reduce_small = _AllReduceSmall(ins[n_sc:], outs[2 + n_sc:], scratch[2:])
        k = pl.program_id(0)

        @pl.when(k == 0)
        def _():
            scatter.start()
            reduce_small.pack_and_send()
            am_t = am_ref[...].T
            for j in range(N_CHIPS):
                dwin_ref[j] = _dot(am_t, dzm_ref[j])
            dwout_ref[...] = jnp.zeros_like(dwout_ref)

        for st in range(2):
            @pl.when(k == ((st + 1) * n_k) // 3)
            def _():
                reduce_small.combine(st)

        a_t = a_ref[...].T
        for j in range(N_CHIPS):
            dwin_ref[j] += _dot(a_t, dz_ref[j])
        dwout_ref[...] += _dot_tn(yc_ref[...], dm_ref[...])

        @pl.when(k == n_k - 1)
        def _():
            reduce_small.combine(2)
            scatter.finish()

    row = pl.BlockSpec((tk, D_MODEL), lambda k: (k, 0))
    outs = pl.pallas_call(
        body, name="mixer_weight_grads", grid=(n_k,),
        out_shape=[jax.ShapeDtypeStruct((N_CHIPS, D_MODEL, IN_SHARD), F32),
                   jax.ShapeDtypeStruct((D_MODEL, D_MODEL), F32)] + _ScatterToChips.out_shape(ffn_sums)
        + _AllReduceSmall.out_shape(),
        in_specs=[row, pl.BlockSpec((N_CHIPS, tk, IN_SHARD), lambda k: (0, k, 0)), row, row,
                  _full((N_META, D_MODEL)), _full((N_CHIPS, N_META, IN_SHARD))] + [ANY] * n_sc
        + [_full(s.shape) for s in small],
        out_specs=[_full((N_CHIPS, D_MODEL, IN_SHARD)), _full((D_MODEL, D_MODEL))] + [ANY] * n_sc
        + [_full(s) for s in _AllReduceSmall.SHAPES],
        scratch_shapes=_ScatterToChips.scratch(n_sc) + _AllReduceSmall.scratch(),
        compiler_params=_cparams(1),
    )(a, dz, ycat, dm, a_meta, dz_meta, *ffn_sums, *small)
    return ([outs[0], outs[1].reshape(N_CHIPS, OUT_SHARD, D_MODEL)], outs[2:2 + n_sc], outs[2 + n_sc:])


def kernel(x, meta_tokens, norm_mix_pre, w_in, conv_w, pool_w, pool_scale, w_out, norm_mix_post, norm_ffn_pre, w_gate, w_up, w_down, norm_ffn_post, loss_target, m_meta_tokens, m_norm_mix_pre, m_w_in, m_conv_w, m_pool_w, m_pool_scale, m_w_out, m_norm_mix_post, m_norm_ffn_pre, m_w_gate, m_w_up, m_w_down, m_norm_ffn_post, v_meta_tokens, v_norm_mix_pre, v_w_in, v_conv_w, v_pool_w, v_pool_scale, v_w_out, v_norm_mix_post, v_norm_ffn_pre, v_w_gate, v_w_up, v_w_down, v_norm_ffn_post):
    n_seq, seq, _ = x.shape
    n_rows = n_seq * seq
    chip = 2 * lax.axis_index("x") + lax.axis_index("y")
    meta_cols = D_MODEL // N_CHIPS
    conv_cols = D_CONV // N_CHIPS

    small = jnp.zeros((2 * HALO, meta_cols), F32)
    small = small.at[0:N_META, :].set(meta_tokens).at[N_META:N_META + 3, 0:conv_cols].set(conv_w[0])
    poolw_bf = pool_w[0].astype(BF16)
    pscale = pool_scale
    g1, g2, g3, g4 = norm_mix_pre, norm_mix_post, norm_ffn_pre, norm_ffn_post
    place = jnp.stack([chip, lax.axis_index("c")]).astype(jnp.int32)

    mix_shards = [w_in[0].astype(BF16), w_out[0].astype(BF16)]
    ((z3, m3, h1, a_bf, conv2, pooled2, yc_bf, zmeta, meta_full, conv_full), (win_all, wout_all, _, *ffn_gathered),
     ffn_shards) = _mixer_fwd(x, g1, g2, poolw_bf, pscale, mix_shards + [small], [w_gate[0].T, w_up[0].T, w_down[0]])
    dh1, f_bf, dd_bf, ds_bf, du_bf, gg_bf, lossp, dg3p, dg4p = _ffn_fwd_bwd(
        h1.reshape(n_rows, D_MODEL), loss_target.reshape(n_rows, D_MODEL), g3, g4, ffn_gathered, ffn_shards)
    as_shards = lambda g: g.reshape(N_CHIPS, FF_SHARD, D_MODEL)
    (dwg_t, dwu_t), _ = _ffn_weight_grads("ffn_weight_grads_gate_up", [ds_bf, du_bf], f_bf, [])
    dwg_t, dwu_t = as_shards(dwg_t), as_shards(dwu_t)
    (dwd,), (dwg_recv, dwu_recv) = _ffn_weight_grads("ffn_weight_grads_down", [gg_bf], dd_bf, [dwg_t, dwu_t])
    dwd = as_shards(dwd)
    behind_bwd = _SplitComm("grad_comm_behind_mixer_bwd", [dwd],
                            _add_pairs_multi([dwg_t, dwu_t], [dwg_recv, dwu_recv], place))
    (grad_x, dz_bf, dm_bf, dg1p, dg2p, dscp, dcwp, dpw, _, dmeta, dg1m, a_meta, dz_meta) = _mixer_bwd(
        dh1.reshape(n_seq, seq, D_MODEL), m3, z3, conv2, pooled2, x, zmeta, meta_full, g1, g2, conv_full, poolw_bf,
        pscale, [win_all, wout_all], mix_shards, behind_bwd.start())
    (dwd_recv,), (dwg_rbuf, dwu_rbuf) = behind_bwd.wait(dg2p)
    (dwd,) = behind_bwd.exchanged
    mix_grads, (dwd_rbuf,), (a_red, b_red, c_red) = _mixer_weight_grads(
        a_bf, dz_bf, yc_bf, dm_bf, a_meta, dz_meta, [_add_pairs(dwd, dwd_recv, place)],
        [dg1p, dg1m, dg2p, dg3p, dg4p, lossp, dmeta, dscp, dcwp, dpw.reshape(SMALL_C_ROWS, POOL_GROUP)])

    behind_sums = _SplitComm("grad_comm_behind_ffn_sums", mix_grads, [])
    ffn_red = _add_chips([dwg_t, dwu_t, dwd], [dwg_recv, dwu_recv, dwd_recv], [dwg_rbuf, dwu_rbuf, dwd_rbuf],
                         place, after=behind_sums.start(), name="grad_add_chips_ffn")
    mix_recvs, _ = behind_sums.wait(ffn_red[0])
    mix_grads = behind_sums.exchanged
    behind_tail = _SplitComm("grad_comm_behind_ffn_tail", [], _add_pairs_multi(mix_grads, mix_recvs, place))
    as_full = lambda r: r.reshape(2 * r.shape[1], r.shape[2])
    g_wg_t, g_wu_t, g_wd = [as_full(r) for r in _gather_halves(list(ffn_red), "grad_gather_halves_ffn",
                                                                 after=behind_tail.start())]
    ffn_out = _adamw_big([(w_gate[0].T, g_wg_t, m_w_gate[0].T, v_w_gate[0].T),
                          (w_up[0].T, g_wu_t, m_w_up[0].T, v_w_up[0].T), (w_down[0], g_wd, m_w_down[0], v_w_down[0])])

    as_c = lambda p: p.reshape(SMALL_C_ROWS, POOL_GROUP)
    loss, small_out = _adamw_small(place, [a_red, b_red, c_red], [
        (meta_tokens, m_meta_tokens, v_meta_tokens),
        (g1, m_norm_mix_pre, v_norm_mix_pre),
        (conv_w[0], m_conv_w[0], v_conv_w[0]),
        (as_c(pool_w), as_c(m_pool_w), as_c(v_pool_w)),
        (pool_scale, m_pool_scale, v_pool_scale),
        (g2, m_norm_mix_post, v_norm_mix_post),
        (g3, m_norm_ffn_pre, v_norm_ffn_pre),
        (g4, m_norm_ffn_post, v_norm_ffn_post),
    ])
    _, mix_rbufs = behind_tail.wait(ffn_out[2][0], small_out[0][0])
    mix_red = _add_chips(mix_grads, mix_recvs, mix_rbufs, place)
    g_win, g_wout = [as_full(r) for r in _gather_halves(list(mix_red), "grad_gather_halves_mixer")]
    big_out = (_adamw_big([(w_in[0], g_win, m_w_in[0], v_w_in[0])])
               + _adamw_big([(w_out[0], g_wout, m_w_out[0], v_w_out[0])]) + ffn_out)
    big_out[2] = [o.T for o in big_out[2]]
    big_out[3] = [o.T for o in big_out[3]]

    s_meta, s_g1, s_conv, s_poolw, s_pscale, s_g2, s_g3, s_g4 = small_out
    b_win, b_wout, b_wg, b_wu, b_wd = big_out

    def leaf(k):
        return [s_meta[k], s_g1[k], b_win[k][None], s_conv[k][None], s_poolw[k].reshape(pool_w.shape), s_pscale[k],
                b_wout[k][None], s_g2[k], s_g3[k], b_wg[k][None], b_wu[k][None], b_wd[k][None], s_g4[k]]

    return (loss.reshape(()), grad_x, *leaf(0), *leaf(1), *leaf(2), *leaf(3))
```

```python
import functools

import jax
import jax.numpy as jnp
from jax import lax
from jax.experimental import pallas as pl
from jax.experimental.pallas import tpu as pltpu

F32 = jnp.float32
BF16 = jnp.bfloat16
MESH = pl.DeviceIdType.MESH

D_MODEL = 1024
D_CONV = 512
D_POOL = 512
POOL_GROUP = 128
N_POOL_GROUPS = 4
D_IN_PROJ = 2048
D_FF = 2816
N_CHIPS = 4
FF_SHARD = D_FF // N_CHIPS
IN_SHARD = D_IN_PROJ // N_CHIPS
OUT_SHARD = D_MODEL // N_CHIPS
D_Z = 3 * IN_SHARD
N_META = 16
HALO = 16
RMS_EPS = 1e-6

ADAM_LR = 0.001
ADAM_B1 = 0.9
ADAM_B2 = 0.999
ADAM_EPS = 1e-08
ADAM_WD = 0.01
ADAM_STEP = 10

TM_MIX_FWD = 512
TM_MIX_BWD = 512
SUB_MIX_BWD = 512
TM_FFN = 256
TK_DW = 1024
FF_CHUNK = 1024
VMEM_LIMIT = 56 * 1024 * 1024


def _cparams(n_grid):
    return pltpu.CompilerParams(dimension_semantics=("arbitrary",) * n_grid, vmem_limit_bytes=VMEM_LIMIT)


def _dot(a, b):
    return jnp.dot(a, b, preferred_element_type=F32)


def _dot_nt(a, b):
    return lax.dot_general(a, b, (((1,), (1,)), ((), ())), preferred_element_type=F32)


def _dot_tn(a, b):
    return lax.dot_general(a, b, (((0,), (0,)), ((), ())), preferred_element_type=F32)


def _rows8(v):
    r, c = v.shape
    return v.reshape(r // 8, 8, c).sum(axis=0)


def _rstd(v):
    return lax.rsqrt(jnp.mean(v * v, axis=-1, keepdims=True) + RMS_EPS)


def _rms_bwd(dy, xhat, rstd, gain):
    dyg = dy * gain
    return rstd * (dyg - xhat * jnp.mean(dyg * xhat, axis=-1, keepdims=True))


def _sigmoid(v):
    return 1.0 / (1.0 + jnp.exp(-v))


def _gcols(g):
    return slice(g * POOL_GROUP, (g + 1) * POOL_GROUP)


def _window_sum(e, g, ahead):
    n = e.shape[0]
    w = e
    for level in range(g + 1):
        shift = 1 << level
        w = w + pltpu.roll(w, (n - shift) if ahead else shift, 0)
    return w


def _pool_fwd(pb, g, n):
    e = pb[0:HALO + n, _gcols(g)]
    return _window_sum(e, g, False)[HALO:, :] * (1.0 / (2 << g)) - e[HALO:, :]


def _pool_bwd(qb, g, r0, n):
    e = qb[r0:r0 + n + HALO, _gcols(g)]
    return _window_sum(e, g, True)[0:n, :] * (1.0 / (2 << g)) - e[0:n, :]


def _full(shape):
    nd = len(shape)
    return pl.BlockSpec(shape, lambda *_: (0,) * nd)


ANY = pl.BlockSpec(memory_space=pl.ANY)


def _mesh_pos():
    x, y, c = lax.axis_index("x"), lax.axis_index("y"), lax.axis_index("c")
    chips = [(1 - x, y), (x, 1 - y), (1 - x, 1 - y)]
    return x, y, c, chips


def _half(ref, h):
    hr = ref.shape[0] // 2
    return ref.at[pl.ds(h * hr, hr), :]


class _AllGather:
    PER_ARRAY = 9

    def __init__(self, ins, outs, send_sems, recv_sems):
        self.ins, self.outs, self.send_sems, self.recv_sems = ins, outs, send_sems, recv_sems
        self.n = len(ins)

    @classmethod
    def scratch(cls, n):
        return [pltpu.SemaphoreType.DMA((cls.PER_ARRAY * n,)), pltpu.SemaphoreType.DMA((cls.PER_ARRAY * n,))]

    @staticmethod
    def out_shape(shards):
        return [jax.ShapeDtypeStruct((N_CHIPS,) + s.shape, s.dtype) for s in shards]

    def _copy(self, a, k, src, dst, to):
        i = self.PER_ARRAY * a + k
        return pltpu.make_async_remote_copy(src_ref=src, dst_ref=dst, send_sem=self.send_sems.at[i],
                                            recv_sem=self.recv_sems.at[i], device_id=to, device_id_type=MESH)

    def _piece(self, a, chip, piece, h=None):
        h = lax.axis_index("c") if h is None else h
        rows = self.ins[a].shape[0] // 4
        return self.outs[a].at[chip].at[pl.ds((2 * h + piece) * rows, rows), :]

    def _own(self, a, k):
        x, y, c, chips = _mesh_pos()
        piece = (1, 0, 0, 1)[k]
        rows = self.ins[a].shape[0] // 4
        src = self.ins[a].at[pl.ds((2 * c + piece) * rows, rows), :]
        return self._copy(a, k, src, self._piece(a, 2 * x + y, piece), (*chips[k // 2], c))

    def _relay(self, a, k):
        x, y, c, chips = _mesh_pos()
        source, to, piece = (chips[1], chips[0], 0) if k == 4 else (chips[0], chips[1], 1)
        rows = self._piece(a, 2 * source[0] + source[1], piece)
        return self._copy(a, k, rows, rows, (*to, c))

    def _sibling(self, a, k, h):
        x, y, c, chips = _mesh_pos()
        chip = chips[k - 6]
        slot = _half(self.outs[a].at[2 * chip[0] + chip[1]], h)
        return self._copy(a, k, slot, slot, (x, y, 1 - c))

    def start(self, arrays=None):
        for a in (range(self.n) if arrays is None else arrays):
            for k in range(4):
                self._own(a, k).start()

    def relay(self, a):
        self._own(a, 2).wait_recv()
        self._relay(a, 4).start()
        self._own(a, 0).wait_recv()
        self._relay(a, 5).start()

    def forward(self, a):
        c = lax.axis_index("c")
        self._own(a, 1).wait_recv()
        self._sibling(a, 6, c).start()
        self._own(a, 3).wait_recv()
        self._sibling(a, 7, c).start()
        self._relay(a, 4).wait_recv()
        self._relay(a, 5).wait_recv()
        self._sibling(a, 8, c).start()

    def finish(self, arrays=None):
        c = lax.axis_index("c")
        arrays = range(self.n) if arrays is None else arrays
        for a in arrays:
            for k in range(6, 9):
                self._sibling(a, k, 1 - c).wait_recv()
        for a in arrays:
            for k in range(4):
                self._own(a, k).wait_send()
            for k in range(4, 6):
                self._relay(a, k).wait_send()
            for k in range(6, 9):
                self._sibling(a, k, c).wait_send()


class _ExchangeHalves:
    def __init__(self, ins, recvs, send_sems, recv_sems):
        self.ins, self.recvs, self.send_sems, self.recv_sems = ins, recvs, send_sems, recv_sems

    @staticmethod
    def scratch(n):
        return [pltpu.SemaphoreType.DMA((n,)), pltpu.SemaphoreType.DMA((n,))]

    @staticmethod
    def out_shape(grads):
        return [jax.ShapeDtypeStruct((g.shape[0], g.shape[1] // 2, g.shape[2]), g.dtype) for g in grads]

    def _copies(self):
        x, y, c, _ = _mesh_pos()
        out = []
        for a, (src, dst) in enumerate(zip(self.ins, self.recvs)):
            hr = src.shape[1] // 2
            out.append(pltpu.make_async_remote_copy(
                src_ref=src.at[:, pl.ds((1 - c) * hr, hr), :], dst_ref=dst, send_sem=self.send_sems.at[a],
                recv_sem=self.recv_sems.at[a], device_id=(x, y, 1 - c), device_id_type=MESH))
        return out

    def start(self):
        for cp in self._copies():
            cp.start()

    def finish(self):
        for cp in self._copies():
            cp.wait()


class _ScatterToChips:
    def __init__(self, ins, rbufs, send_sems, recv_sems):
        self.ins, self.rbufs, self.send_sems, self.recv_sems = ins, rbufs, send_sems, recv_sems

    @staticmethod
    def scratch(n):
        return [pltpu.SemaphoreType.DMA((3 * n,)), pltpu.SemaphoreType.DMA((3 * n,))]

    @staticmethod
    def out_shape(sums):
        return [jax.ShapeDtypeStruct((3,) + s.shape[1:], BF16) for s in sums]

    def _copies(self):
        x, y, c, chips = _mesh_pos()
        out = []
        for a, (src, dst) in enumerate(zip(self.ins, self.rbufs)):
            for k, chip in enumerate(chips):
                out.append(pltpu.make_async_remote_copy(
                    src_ref=src.at[2 * chip[0] + chip[1]], dst_ref=dst.at[k], send_sem=self.send_sems.at[3 * a + k],
                    recv_sem=self.recv_sems.at[3 * a + k], device_id=(*chip, c), device_id_type=MESH))
        return out

    def start(self):
        for cp in self._copies():
            cp.start()

    def finish(self):
        for cp in self._copies():
            cp.wait()


HBM = pl.BlockSpec(memory_space=pltpu.HBM)
SEM = pl.BlockSpec(memory_space=pltpu.SEMAPHORE)


class _SplitComm:
    def __init__(self, name, exchanged, scattered):
        self.name, self.n_ex, self.n_sc = name, len(exchanged), len(scattered)
        self.n_copies = self.n_ex + 3 * self.n_sc
        zones = ([lax.empty((g.shape[0], g.shape[1] // 2, g.shape[2]), g.dtype) for g in exchanged]
                 + [lax.empty((3,) + s.shape[1:], s.dtype) for s in scattered])
        self.buffers = [pltpu.with_memory_space_constraint(v, pltpu.HBM)
                        for v in list(exchanged) + list(scattered) + zones]

    def _copies(self, bufs, send_sems, recv_sems):
        x, y, c, chips = _mesh_pos()
        n_src = self.n_ex + self.n_sc
        out = []
        for a in range(self.n_ex):
            hr = bufs[a].shape[1] // 2
            out.append(pltpu.make_async_remote_copy(
                src_ref=bufs[a].at[:, pl.ds((1 - c) * hr, hr), :], dst_ref=bufs[n_src + a], send_sem=send_sems[a],
                recv_sem=recv_sems[a], device_id=(x, y, 1 - c), device_id_type=MESH))
        for a in range(self.n_sc):
            for k, chip in enumerate(chips):
                i = self.n_ex + 3 * a + k
                out.append(pltpu.make_async_remote_copy(
                    src_ref=bufs[self.n_ex + a].at[2 * chip[0] + chip[1]], dst_ref=bufs[n_src + self.n_ex + a].at[k],
                    send_sem=send_sems[i], recv_sem=recv_sems[i], device_id=(*chip, c), device_id_type=MESH))
        return out

    def start(self):
        n_buf, n_cp = len(self.buffers), self.n_copies

        def body(*refs):
            bufs = refs[:n_buf]
            send_sems, recv_sems = refs[n_buf:n_buf + n_cp], refs[n_buf + n_cp:n_buf + 2 * n_cp]
            for cp in self._copies(bufs, send_sems, recv_sems):
                cp.start()
            refs[-1][...] = jnp.zeros_like(refs[-1])

        outs = pl.pallas_call(
            body, name=self.name + "_start",
            out_shape=[pltpu.SemaphoreType.DMA(())] * (2 * n_cp) + [pltpu.HBM(b.shape, b.dtype) for b in self.buffers]
            + [jax.ShapeDtypeStruct((8, 128), F32)],
            in_specs=[HBM] * n_buf, out_specs=[SEM] * (2 * n_cp) + [HBM] * n_buf + [pl.BlockSpec(memory_space=pltpu.VMEM)],
            input_output_aliases={i: 2 * n_cp + i for i in range(n_buf)},
            compiler_params=pltpu.CompilerParams(has_side_effects=pltpu.SideEffectType.DATAFLOW_SIDE_EFFECTING),
        )(*self.buffers)
        self.sems, self.buffers = outs[:2 * n_cp], outs[2 * n_cp:2 * n_cp + n_buf]
        return outs[-1]

    def wait(self, *after):
        n_buf, n_cp = len(self.buffers), self.n_copies

        def body(*refs):
            bufs = refs[:n_buf]
            send_sems, recv_sems = refs[n_buf:n_buf + n_cp], refs[n_buf + n_cp:n_buf + 2 * n_cp]
            for cp in self._copies(bufs, send_sems, recv_sems):
                cp.wait_send()
                cp.wait_recv()

        outs = pl.pallas_call(
            body, name=self.name + "_wait", out_shape=[pltpu.HBM(b.shape, b.dtype) for b in self.buffers],
            in_specs=[HBM] * n_buf + [SEM] * (2 * n_cp) + [ANY] * len(after), out_specs=[HBM] * n_buf,
            input_output_aliases={i: i for i in range(n_buf)},
            compiler_params=pltpu.CompilerParams(has_side_effects=pltpu.SideEffectType.DATAFLOW_SIDE_EFFECTING),
        )(*self.buffers, *self.sems, *after)
        self.exchanged = outs[:self.n_ex]
        zones = outs[self.n_ex + self.n_sc:]
        return zones[:self.n_ex], zones[self.n_ex:]


def _gather_halves(halves, name, after=None):
    n = len(halves)
    extra = [] if after is None else [after]

    def body(*refs):
        ins, outs = refs[:n], refs[n + len(extra):2 * n + len(extra)]
        send_sems, recv_sems = refs[2 * n + len(extra):]
        x, y, c, _ = _mesh_pos()
        sib = (x, y, 1 - c)
        remote = [pltpu.make_async_remote_copy(src_ref=ins[a].at[c], dst_ref=outs[a].at[c],
                                               send_sem=send_sems.at[a], recv_sem=recv_sems.at[a],
                                               device_id=sib, device_id_type=MESH) for a in range(n)]
        for cp in remote:
            cp.start()
        for a in range(n):
            pltpu.make_async_remote_copy(src_ref=ins[a].at[1 - c], dst_ref=outs[a].at[1 - c], send_sem=send_sems.at[a],
                                         recv_sem=recv_sems.at[a], device_id=sib, device_id_type=MESH).wait_recv()
        for cp in remote:
            cp.wait_send()

    return pl.pallas_call(
        body, name=name,
        out_shape=[jax.ShapeDtypeStruct(h.shape, F32) for h in halves],
        in_specs=[ANY] * (n + len(extra)), out_specs=[ANY] * n, input_output_aliases={a: a for a in range(n)},
        scratch_shapes=[pltpu.SemaphoreType.DMA((n,)), pltpu.SemaphoreType.DMA((n,))],
    )(*halves, *extra)


SMALL_A_ROWS = 24
SMALL_B_ROWS = 8
SMALL_C_ROWS = N_POOL_GROUPS * POOL_GROUP


class _AllReduceSmall:
    N_IN = 10
    SHAPES = [(SMALL_A_ROWS, D_MODEL), (SMALL_B_ROWS, D_CONV), (SMALL_C_ROWS, POOL_GROUP)]

    def __init__(self, ins, outs, scratch):
        self.ins, self.outs = ins, outs
        self.bufs, self.rcvs, self.send_sems, self.recv_sems = scratch[:3], scratch[3:6], scratch[6], scratch[7]

    @classmethod
    def scratch(cls):
        return ([pltpu.VMEM((3,) + s, F32) for s in cls.SHAPES] + [pltpu.VMEM((3,) + s, F32) for s in cls.SHAPES]
                + [pltpu.SemaphoreType.DMA((9,)), pltpu.SemaphoreType.DMA((9,))])

    @classmethod
    def out_shape(cls):
        return [jax.ShapeDtypeStruct(s, F32) for s in cls.SHAPES]

    def _copies(self, st):
        x, y, c, _ = _mesh_pos()
        peer = [(x, y, 1 - c), (1 - x, y, c), (x, 1 - y, c)][st]
        return [pltpu.make_async_remote_copy(
            src_ref=buf.at[st], dst_ref=rcv.at[st], send_sem=self.send_sems.at[3 * st + i],
            recv_sem=self.recv_sems.at[3 * st + i], device_id=peer, device_id_type=MESH)
            for i, (buf, rcv) in enumerate(zip(self.bufs, self.rcvs))]

    def pack_and_send(self):
        dg1_ref, dg1m_ref, dg2_ref, dg3_ref, dg4_ref, loss_ref, dmeta_ref, dsc_ref, dcw_ref, dpw_ref = self.ins
        a_buf, b_buf, c_buf = self.bufs

        def rowsum(v):
            return jnp.sum(v, axis=0, keepdims=True)

        a_buf[0, 0:1, :] = rowsum(dg1_ref[...] + dg1m_ref[...])
        a_buf[0, 1:2, :] = rowsum(dg2_ref[...])
        a_buf[0, 2:3, :] = rowsum(dg3_ref[...])
        a_buf[0, 3:4, :] = rowsum(dg4_ref[...])
        loss = jnp.sum(rowsum(loss_ref[...]), axis=1, keepdims=True) * (0.5 / D_MODEL)
        a_buf[0, 4:5, :] = jnp.broadcast_to(loss, (1, D_MODEL))
        a_buf[0, 5:8, :] = jnp.zeros((3, D_MODEL), F32)
        a_buf[0, 8:24, :] = dmeta_ref[...]
        b_buf[0, 0:1, :] = rowsum(dsc_ref[...])
        for k in range(3):
            b_buf[0, 1 + k:2 + k, :] = rowsum(dcw_ref[8 * k:8 * k + 8, :])
        b_buf[0, 4:8, :] = jnp.zeros((4, D_CONV), F32)
        c_buf[0] = dpw_ref[...]
        for cp in self._copies(0):
            cp.start()

    def combine(self, st):
        for cp in self._copies(st):
            cp.wait()
        if st < 2:
            for buf, rcv in zip(self.bufs, self.rcvs):
                buf[st + 1] = buf[st] + rcv[st]
            for cp in self._copies(st + 1):
                cp.start()
        else:
            for out, buf, rcv in zip(self.outs, self.bufs, self.rcvs):
                out[...] = buf[st] + rcv[st]


def _row_block(rows):
    for cand in (512, 448, 384, 352, 320, 256, 128, 64, 32, 16):
        if rows % cand == 0:
            return cand
    return rows


def _add_pairs_multi(grads, recvs, place):
    n = len(grads)
    n_sh = grads[0].shape[0]
    halves = [g.shape[1] // 2 for g in grads]
    n_steps = halves[0] // _row_block(halves[0])
    blocks = [(hr // n_steps, g.shape[2]) for hr, g in zip(halves, grads)]

    def body(place_ref, *refs):
        for a_ref, b_ref, o_ref in zip(refs[:n], refs[n:2 * n], refs[2 * n:]):
            o_ref[...] = (a_ref[0] + b_ref[...]).astype(BF16)

    return pl.pallas_call(
        body, name="grad_add_pairs",
        grid_spec=pltpu.PrefetchScalarGridSpec(
            num_scalar_prefetch=1, grid=(n_sh, n_steps),
            in_specs=[pl.BlockSpec((1, 1, br, cols), lambda j, i, p: (j, p[1], i, 0)) for br, cols in blocks]
            + [pl.BlockSpec((1, br, cols), lambda j, i, p: (j, i, 0)) for br, cols in blocks],
            out_specs=[pl.BlockSpec((1, br, cols), lambda j, i, p: (j, i, 0)) for br, cols in blocks]),
        out_shape=[jax.ShapeDtypeStruct((n_sh, hr, g.shape[2]), BF16) for hr, g in zip(halves, grads)],
        compiler_params=_cparams(2),
    )(place, *[g.reshape(n_sh, 2, hr, g.shape[2]) for hr, g in zip(halves, grads)], *recvs)


def _add_pairs(grad, recv, place):
    return _add_pairs_multi([grad], [recv], place)[0]


def _add_chips(grads, recvs, rbufs, place, after=None, name="grad_add_chips"):
    n = len(grads)
    n_sh = grads[0].shape[0]
    halves = [g.shape[1] // 2 for g in grads]
    n_steps = halves[0] // _row_block(halves[0])
    blocks = [(hr // n_steps, g.shape[2]) for hr, g in zip(halves, grads)]
    extra = [] if after is None else [after]

    def body(place_ref, *refs):
        for a_ref, b_ref, r_ref, o_ref in zip(refs[:n], refs[n:2 * n], refs[2 * n:3 * n], refs[3 * n + len(extra):]):
            own = a_ref[0, 0] + b_ref[0]
            o_ref[0] = ((own + r_ref[0].astype(F32)) + r_ref[1].astype(F32)) + r_ref[2].astype(F32)

    return pl.pallas_call(
        body, name=name,
        grid_spec=pltpu.PrefetchScalarGridSpec(
            num_scalar_prefetch=1, grid=(n_steps,),
            in_specs=[pl.BlockSpec((1, 1, br, cols), lambda i, p: (p[0], p[1], i, 0)) for br, cols in blocks]
            + [pl.BlockSpec((1, br, cols), lambda i, p: (p[0], i, 0)) for br, cols in blocks]
            + [pl.BlockSpec((3, br, cols), lambda i, p: (0, i, 0)) for br, cols in blocks]
            + [pl.BlockSpec((8, 128), lambda i, p: (0, 0))] * len(extra),
            out_specs=[pl.BlockSpec((1, br, cols), lambda i, p: (p[1], i, 0)) for br, cols in blocks]),
        out_shape=[jax.ShapeDtypeStruct((2, hr, g.shape[2]), F32) for hr, g in zip(halves, grads)],
        compiler_params=_cparams(1),
    )(place, *[g.reshape(n_sh, 2, hr, g.shape[2]) for hr, g in zip(halves, grads)], *recvs, *rbufs, *extra)


def _adamw_math(w, g, m, v):
    m2 = ADAM_B1 * m + (1.0 - ADAM_B1) * g
    v2 = ADAM_B2 * v + (1.0 - ADAM_B2) * (g * g)
    m_hat = m2 / (1.0 - ADAM_B1 ** ADAM_STEP)
    v_hat = v2 / (1.0 - ADAM_B2 ** ADAM_STEP)
    delta = -ADAM_LR * (m_hat / (jnp.sqrt(v_hat) + ADAM_EPS) + ADAM_WD * w)
    return delta, m2, v2


def _adamw_big(groups):
    n = len(groups)
    rows, cols = groups[0][0].shape
    br = _row_block(rows)
    if n > 1 and br % 16 == 0:
        br //= 2

    def body(*refs):
        for i in range(n):
            w_ref, g_ref, m_ref, v_ref = refs[4 * i:4 * i + 4]
            g_out_ref, d_ref, m2_ref, v2_ref = refs[4 * n + 4 * i:4 * n + 4 * i + 4]
            g = g_ref[...]
            d, m2, v2 = _adamw_math(w_ref[...], g, m_ref[...], v_ref[...])
            g_out_ref[...] = g
            d_ref[...] = d
            m2_ref[...] = m2
            v2_ref[...] = v2

    spec = pl.BlockSpec((br, cols), lambda i: (i, 0))
    outs = pl.pallas_call(
        body, name="adamw_big", grid=(rows // br,),
        out_shape=[jax.ShapeDtypeStruct((rows, cols), F32)] * (4 * n),
        in_specs=[spec] * (4 * n), out_specs=[spec] * (4 * n), compiler_params=_cparams(1),
    )(*[a for grp in groups for a in grp])
    return [list(outs[4 * i:4 * i + 4]) for i in range(n)]


def _adamw_small(place, reduced, params):
    n = len(params)
    meta_cols, conv_cols = D_MODEL // N_CHIPS, D_CONV // N_CHIPS

    def body(place_ref, a_ref, b_ref, c_ref, *refs):
        ins, loss_ref, outs = refs[:3 * n], refs[3 * n], refs[3 * n + 1:]
        chip = place_ref[0]

        def own_cols(ref, r0, n_r, width):
            out = ref[r0:r0 + n_r, 0:width]
            for j in range(1, N_CHIPS):
                out = jnp.where(chip == j, ref[r0:r0 + n_r, j * width:(j + 1) * width], out)
            return out

        grads = [own_cols(a_ref, 8, N_META, meta_cols), a_ref[0:1, :], own_cols(b_ref, 1, 3, conv_cols), c_ref[...],
                 b_ref[0:1, :], a_ref[1:2, :], a_ref[2:3, :], a_ref[3:4, :]]
        loss_ref[...] = a_ref[4:5, 0:1]
        for i, g in enumerate(grads):
            w, m, v = (r[...] for r in ins[3 * i:3 * i + 3])
            for o, val in zip(outs[4 * i:4 * i + 4], (g,) + _adamw_math(w, g, m, v)):
                o[...] = val

    vm = pl.BlockSpec(memory_space=pltpu.VMEM)
    flat = [a for grp in params for a in grp]
    out_shape = ([jax.ShapeDtypeStruct((1, 1), F32)]
                 + [jax.ShapeDtypeStruct(grp[0].shape, F32) for grp in params for _ in range(4)])
    outs = pl.pallas_call(body, name="adamw_small", out_shape=out_shape,
                          in_specs=[pl.BlockSpec(memory_space=pltpu.SMEM)] + [vm] * (3 + 3 * n),
                          out_specs=[vm] * (1 + 4 * n))(place, *reduced, *flat)
    return outs[0], [tuple(outs[1 + 4 * i:5 + 4 * i]) for i in range(n)]


def _load_gathered(gathered, shards, dst_slots, sems):
    n = len(gathered)
    me = 2 * lax.axis_index("x") + lax.axis_index("y")

    def copies(j, own):
        return [pltpu.make_async_copy(shards[a] if own else gathered[a].at[j], dst_slots[a](j), sems.at[n * j + a])
                for a in range(n)]

    for wait in (False, True):
        for j in range(N_CHIPS):
            for own in (False, True):
                @pl.when((me == j) == own)
                def _():
                    for cp in copies(j, own):
                        cp.wait() if wait else cp.start()


N_MIX_SHARDS = 3


def _mixer_fwd(x3, g1, g2, poolw, pscale, shards, ffn_f32):
    n_seq, seq, _ = x3.shape
    tm = min(TM_MIX_FWD, seq)
    n_t = seq // tm
    n_steps = n_seq * n_t
    n_ffn = len(ffn_f32)
    n_ag = N_MIX_SHARDS + n_ffn
    ffn_bf16 = [jax.ShapeDtypeStruct(w.shape, BF16) for w in ffn_f32]
    small_rows = shards[2].shape[0]
    conv_cols = D_CONV // N_CHIPS

    def body(x_ref, g1_ref, g2_ref, pw_ref, ps_ref, *rest):
        ffn_f32_refs, ffn_bf_refs = rest[N_MIX_SHARDS:n_ag], rest[2 * n_ag + 10:2 * n_ag + 10 + n_ffn]
        ag = _AllGather(list(rest[:N_MIX_SHARDS]) + list(ffn_bf_refs), rest[n_ag + 10:2 * n_ag + 10], *rest[-2:])
        (z_ref, m_ref, h1_ref, a_ref, conv_ref, pooled_ref, yc_ref, zm_ref, meta_ref,
         cw_ref) = rest[n_ag:n_ag + 10]
        win_v, wout_v, small_v, cvb, pb, load_sems, stage_v, cast_v, cast_sems = rest[2 * n_ag + 10 + n_ffn:-2]
        s, t = pl.program_id(0), pl.program_id(1)
        step = s * n_t + t

        def round_ffn_piece(i):
            load = pltpu.make_async_copy(ffn_f32_refs[i], stage_v, cast_sems.at[0])
            load.start()
            load.wait()
            cast_v[...] = stage_v[...].astype(BF16)
            store = pltpu.make_async_copy(cast_v, ffn_bf_refs[i], cast_sems.at[1])
            store.start()
            store.wait()

        @pl.when(step == 0)
        def _():
            ag.start(range(N_MIX_SHARDS))
            round_ffn_piece(0)
            round_ffn_piece(1)
            for a in range(N_MIX_SHARDS):
                ag.relay(a)
            round_ffn_piece(2)
            ag.start(range(N_MIX_SHARDS, n_ag))
            for a in range(N_MIX_SHARDS):
                ag.forward(a)
            ag.finish(range(N_MIX_SHARDS))
            _load_gathered(ag.outs[:N_MIX_SHARDS], ag.ins[:N_MIX_SHARDS],
                           [lambda j: win_v.at[j], lambda j: wout_v.at[pl.ds(j * OUT_SHARD, OUT_SHARD), :],
                            lambda j: small_v.at[j]], load_sems)

            meta = jnp.concatenate([small_v[j, 0:N_META, :] for j in range(N_CHIPS)], axis=1)
            meta_ref[...] = meta
            cw_ref[...] = jnp.concatenate([small_v[j, N_META:N_META + 3, 0:conv_cols] for j in range(N_CHIPS)], axis=1)
            a_meta = (meta * _rstd(meta) * g1_ref[...]).astype(BF16)
            for j in range(N_CHIPS):
                zm_ref[:, j * IN_SHARD:(j + 1) * IN_SHARD] = _dot(a_meta, win_v[j])

        for i in range(n_ffn):
            @pl.when(step == (3 * (i + 1) * n_steps) // (4 * n_ffn + 4))
            def _():
                ag.relay(N_MIX_SHARDS + i)

        for i in range(n_ffn):
            @pl.when(step == min(n_steps // 2 + ((i + 1) * n_steps) // (2 * n_ffn + 2), n_steps - 1))
            def _():
                ag.forward(N_MIX_SHARDS + i)

        @pl.when(t == 0)
        def _():
            cvb[0:HALO, :] = zm_ref[:, IN_SHARD:2 * IN_SHARD] * zm_ref[:, 2 * IN_SHARD:3 * IN_SHARD]
            pb[0:HALO, :] = zm_ref[:, 3 * IN_SHARD:4 * IN_SHARD]

        @pl.when(t > 0)
        def _():
            cvb[0:HALO, :] = cvb[tm:tm + HALO, :]
            pb[0:HALO, :] = pb[tm:tm + HALO, :]

        xt = x_ref[0]
        a = (xt * _rstd(xt) * g1_ref[...]).astype(BF16)
        a_ref[...] = a
        zb = _dot(a, win_v[0])
        zc = _dot(a, win_v[1])
        zv = _dot(a, win_v[2])
        zp = _dot(a, win_v[3])
        z_ref[0, :, 0:IN_SHARD] = zb
        z_ref[0, :, IN_SHARD:2 * IN_SHARD] = zc
        z_ref[0, :, 2 * IN_SHARD:3 * IN_SHARD] = zv
        cv = zc * zv
        cvb[HALO:HALO + tm, :] = cv
        pb[HALO:HALO + tm, :] = zp
        cw = cw_ref[...]
        conv = cw[0:1] * cvb[HALO - 2:HALO - 2 + tm, :] + cw[1:2] * cvb[HALO - 1:HALO - 1 + tm, :] + cw[2:3] * cv
        conv_ref[...] = conv
        parts = [(zb * conv).astype(BF16)]
        for g in range(N_POOL_GROUPS):
            pooled = _pool_fwd(pb, g, tm).astype(BF16)
            pooled_ref[:, _gcols(g)] = pooled
            parts.append((_dot(pooled, pw_ref[g]) * ps_ref[:, _gcols(g)]).astype(BF16))
        ycat = jnp.concatenate(parts, axis=1)
        yc_ref[...] = ycat
        m = _dot(ycat, wout_v[...])
        m_ref[0] = m
        h1_ref[0] = xt + m * _rstd(m) * g2_ref[...]

        @pl.when(step == n_steps - 1)
        def _():
            ag.finish(range(N_MIX_SHARDS, n_ag))

    n_rows = n_seq * seq
    row = lambda c: pl.BlockSpec((1, tm, c), lambda s, t: (s, t, 0))
    row2 = lambda c: pl.BlockSpec((tm, c), lambda s, t: (s * n_t + t, 0))
    outs = pl.pallas_call(
        body, name="mixer_fwd", grid=(n_seq, n_t),
        out_shape=[jax.ShapeDtypeStruct((n_seq, seq, D_Z), F32), jax.ShapeDtypeStruct((n_seq, seq, D_MODEL), F32),
                   jax.ShapeDtypeStruct((n_seq, seq, D_MODEL), F32), jax.ShapeDtypeStruct((n_rows, D_MODEL), BF16),
                   jax.ShapeDtypeStruct((n_rows, D_CONV), F32), jax.ShapeDtypeStruct((n_rows, D_POOL), BF16),
                   jax.ShapeDtypeStruct((n_rows, D_MODEL), BF16), jax.ShapeDtypeStruct((N_META, D_IN_PROJ), F32),
                   jax.ShapeDtypeStruct((N_META, D_MODEL), F32), jax.ShapeDtypeStruct((3, D_CONV), F32)]
        + _AllGather.out_shape(list(shards) + ffn_bf16) + ffn_bf16,
        in_specs=[row(D_MODEL), _full((1, D_MODEL)), _full((1, D_MODEL)),
                  _full((N_POOL_GROUPS, POOL_GROUP, POOL_GROUP)), _full((1, D_POOL))] + [ANY] * n_ag,
        out_specs=[row(D_Z), row(D_MODEL), row(D_MODEL), row2(D_MODEL), row2(D_CONV), row2(D_POOL), row2(D_MODEL),
                   _full((N_META, D_IN_PROJ)), _full((N_META, D_MODEL)), _full((3, D_CONV))] + [ANY] * (n_ag + n_ffn),
        scratch_shapes=[pltpu.VMEM((N_CHIPS, D_MODEL, IN_SHARD), BF16), pltpu.VMEM((D_MODEL, D_MODEL), BF16),
                        pltpu.VMEM((N_CHIPS, small_rows, D_MODEL // N_CHIPS), F32),
                        pltpu.VMEM((HALO + tm, D_CONV), F32), pltpu.VMEM((HALO + tm, D_POOL), F32),
                        pltpu.SemaphoreType.DMA((N_MIX_SHARDS * N_CHIPS,)),
                        pltpu.VMEM(ffn_f32[0].shape, F32), pltpu.VMEM(ffn_f32[0].shape, BF16),
                        pltpu.SemaphoreType.DMA((2,))] + _AllGather.scratch(n_ag),
        compiler_params=_cparams(2),
    )(x3, g1, g2, poolw, pscale, *shards, *ffn_f32)
    return outs[:10], outs[10:10 + n_ag], outs[10 + n_ag:]


def _ffn_chunks():
    out, r0 = [], 0
    while r0 < D_FF:
        out.append((r0, min(FF_CHUNK, D_FF - r0)))
        r0 += FF_CHUNK
    return out


def _ffn_fwd_bwd(h1, target, g3, g4, gathered, shards):
    n_rows = h1.shape[0]
    tm = min(TM_FFN, n_rows)
    chunks = _ffn_chunks()

    def body(h1_ref, t_ref, g3_ref, g4_ref, wg_all, wu_all, wd_all, wg_s, wu_s, wd_s,
             dh1_ref, f_ref, dd_ref, ds_ref, du_ref, gg_ref, loss_ref, dg3_ref, dg4_ref,
             wg_v, wu_v, wd_v, s_sc, u_sc, sems):
        @pl.when(pl.program_id(0) == 0)
        def _():
            _load_gathered([wg_all, wu_all, wd_all], [wg_s, wu_s, wd_s],
                           [functools.partial(lambda v, j: v.at[pl.ds(j * FF_SHARD, FF_SHARD), :], v)
                            for v in (wg_v, wu_v, wd_v)], sems)
            loss_ref[...] = jnp.zeros_like(loss_ref)
            dg3_ref[...] = jnp.zeros_like(dg3_ref)
            dg4_ref[...] = jnp.zeros_like(dg4_ref)

        h1v = h1_ref[...]
        r3 = _rstd(h1v)
        hh = h1v * r3
        g3v, g4v = g3_ref[...], g4_ref[...]
        f = (hh * g3v).astype(BF16)
        f_ref[...] = f
        d = jnp.zeros((tm, D_MODEL), F32)
        for r0, sz in chunks:
            s = _dot_nt(f, wg_v[r0:r0 + sz, :])
            u = _dot_nt(f, wu_v[r0:r0 + sz, :])
            s_sc[:, r0:r0 + sz] = s
            u_sc[:, r0:r0 + sz] = u
            gc = (s * _sigmoid(s) * u).astype(BF16)
            gg_ref[:, r0:r0 + sz] = gc
            d = d + _dot(gc, wd_v[r0:r0 + sz, :])
        r4 = _rstd(d)
        dh = d * r4
        err = (h1v + dh * g4v) - t_ref[...]
        loss_ref[...] += _rows8(err * err)
        dy = err * (1.0 / D_MODEL)
        dg4_ref[...] += _rows8(dy * dh)
        ddb = _rms_bwd(dy, dh, r4, g4v).astype(BF16)
        dd_ref[...] = ddb
        df = jnp.zeros((tm, D_MODEL), F32)
        for r0, sz in chunks:
            dgg = _dot_nt(ddb, wd_v[r0:r0 + sz, :])
            s = s_sc[:, r0:r0 + sz]
            u = u_sc[:, r0:r0 + sz]
            sig = _sigmoid(s)
            dsc = (dgg * u * (sig * (1.0 + s * (1.0 - sig)))).astype(BF16)
            duc = (dgg * (s * sig)).astype(BF16)
            ds_ref[:, r0:r0 + sz] = dsc
            du_ref[:, r0:r0 + sz] = duc
            df = df + _dot(dsc, wg_v[r0:r0 + sz, :]) + _dot(duc, wu_v[r0:r0 + sz, :])
        dg3_ref[...] += _rows8(df * hh)
        dh1_ref[...] = dy + _rms_bwd(df, hh, r3, g3v)

    row = pl.BlockSpec((tm, D_MODEL), lambda i: (i, 0))
    ffrow = pl.BlockSpec((tm, D_FF), lambda i: (i, 0))
    acc = _full((8, D_MODEL))
    act_bf = jax.ShapeDtypeStruct((n_rows, D_MODEL), BF16)
    ff_bf = jax.ShapeDtypeStruct((n_rows, D_FF), BF16)
    acc_shape = jax.ShapeDtypeStruct((8, D_MODEL), F32)
    w_vmem = pltpu.VMEM((D_FF, D_MODEL), BF16)
    return pl.pallas_call(
        body, name="ffn_fwd_bwd", grid=(n_rows // tm,),
        out_shape=[jax.ShapeDtypeStruct((n_rows, D_MODEL), F32), act_bf, act_bf, ff_bf, ff_bf, ff_bf,
                   acc_shape, acc_shape, acc_shape],
        in_specs=[row, row, _full((1, D_MODEL)), _full((1, D_MODEL))] + [ANY] * 6,
        out_specs=[row, row, row, ffrow, ffrow, ffrow, acc, acc, acc],
        scratch_shapes=[w_vmem, w_vmem, w_vmem, pltpu.VMEM((tm, D_FF), F32), pltpu.VMEM((tm, D_FF), F32),
                        pltpu.SemaphoreType.DMA((3 * N_CHIPS,))],
        compiler_params=_cparams(1),
    )(h1, target, g3, g4, *gathered, *shards)


def _ffn_weight_grads(name, acts, other, exchanged):
    n_rows = other.shape[0]
    n_a, n_ex = len(acts), len(exchanged)
    n_c = n_a
    tk = min(TK_DW, n_rows)
    n_k = n_rows // tk
    half = D_FF // n_c

    def body(other_ref, *rest):
        act_refs = rest[:n_a]
        out_refs = rest[n_a + n_ex:2 * n_a + n_ex]
        c, k = pl.program_id(0), pl.program_id(1)
        if n_ex:
            ex = _ExchangeHalves(rest[n_a:n_a + n_ex], rest[2 * n_a + n_ex:2 * n_a + 2 * n_ex], *rest[-2:])

            @pl.when((c == 0) & (k == 0))
            def _():
                ex.start()

        @pl.when(k == 0)
        def _():
            for o in out_refs:
                o[...] = jnp.zeros_like(o)

        ov = other_ref[...]
        for a, o in zip(act_refs, out_refs):
            o[...] += _dot_tn(a[...], ov)

        if n_ex:
            @pl.when((c == n_c - 1) & (k == n_k - 1))
            def _():
                ex.finish()

    row = pl.BlockSpec((tk, D_MODEL), lambda c, k: (k, 0))
    ffrow = pl.BlockSpec((tk, half), lambda c, k: (k, c))
    out = pl.BlockSpec((half, D_MODEL), lambda c, k: (c, 0))
    outs = pl.pallas_call(
        body, name=name, grid=(n_c, n_k),
        out_shape=[jax.ShapeDtypeStruct((D_FF, D_MODEL), F32)] * n_a + _ExchangeHalves.out_shape(exchanged),
        in_specs=[row] + [ffrow] * n_a + [ANY] * n_ex, out_specs=[out] * n_a + [ANY] * n_ex,
        scratch_shapes=_ExchangeHalves.scratch(n_ex) if n_ex else [],
        compiler_params=_cparams(2),
    )(other, *acts, *exchanged)
    return outs[:n_a], outs[n_a:]


def _mixer_bwd(dh1, m3, z3, conv2, pooled2, x3, zmeta, meta_full, g1, g2, convw, poolw, pscale, gathered, shards,
               after):
    n_seq, seq, _ = x3.shape
    tm = min(TM_MIX_BWD, seq)
    sub = min(SUB_MIX_BWD, tm)
    n_t = seq // tm
    n_out = 13

    def body(dh1_ref, m_ref, z_ref, conv_ref, pooled_ref, x_ref, zm_ref, meta_ref, g1_ref, g2_ref, cw_ref, pw_ref,
             ps_ref, after_ref, win_all, wout_all, win_s, wout_s, *rest):
        (dx_ref, dz_ref, dm_ref, dg1_ref, dg2_ref, dsc_ref, dcw_ref, dpw_ref, dzm_ref, dmeta_ref, dg1m_ref, am_ref,
         dzmb_ref) = rest[:n_out]
        win_v, wout_v, dcb, dqb, mcb, mqb, load_sems = rest[n_out:]
        s, i = pl.program_id(0), pl.program_id(1)
        tr = n_t - 1 - i

        @pl.when((s == 0) & (i == 0))
        def _():
            _load_gathered([win_all, wout_all], [win_s, wout_s],
                           [lambda j: win_v.at[j], lambda j: wout_v.at[pl.ds(j * OUT_SHARD, OUT_SHARD), :]], load_sems)
            for ref in (dg1_ref, dg2_ref, dsc_ref, dcw_ref, dpw_ref, dzm_ref):
                ref[...] = jnp.zeros_like(ref)

        @pl.when(i == 0)
        def _():
            dcb[tm:tm + HALO, :] = jnp.zeros((HALO, D_CONV), F32)
            dqb[tm:tm + HALO, :] = jnp.zeros((HALO, D_POOL), F32)

        @pl.when(i > 0)
        def _():
            dcb[tm:tm + HALO, :] = dcb[0:HALO, :]
            dqb[tm:tm + HALO, :] = dqb[0:HALO, :]

        g1v, g2v = g1_ref[...], g2_ref[...]
        cw = cw_ref[...]

        for r0 in range(tm - sub, -1, -sub):
            rows = slice(r0, r0 + sub)
            dh1v = dh1_ref[0, rows, :]
            mv = m_ref[0, rows, :]
            r2 = _rstd(mv)
            mh = mv * r2
            dg2_ref[...] += _rows8(dh1v * mh)
            dmb = _rms_bwd(dh1v, mh, r2, g2v).astype(BF16)
            dm_ref[rows, :] = dmb
            dyc = _dot_nt(dmb, wout_v[...])
            dyconv = dyc[:, 0:D_CONV]

            for g in range(N_POOL_GROUPS):
                pooled = pooled_ref[rows, _gcols(g)]
                mixed = _dot(pooled, pw_ref[g])
                scale = ps_ref[:, _gcols(g)]
                dyp = dyc[:, D_CONV + g * POOL_GROUP:D_CONV + (g + 1) * POOL_GROUP]
                dsc_ref[:, _gcols(g)] += _rows8(dyp * mixed)
                dmix = (dyp * scale).astype(BF16)
                dpw_ref[g] += _dot_tn(pooled, dmix)
                dqb[rows, _gcols(g)] = _dot_nt(dmix, pw_ref[g])

            zb = z_ref[0, rows, 0:IN_SHARD]
            zc = z_ref[0, rows, IN_SHARD:2 * IN_SHARD]
            zv = z_ref[0, rows, 2 * IN_SHARD:3 * IN_SHARD]
            dconv = dyconv * zb
            dcb[rows, :] = dconv
            d1 = dcb[r0 + 1:r0 + 1 + sub, :]
            d2 = dcb[r0 + 2:r0 + 2 + sub, :]
            dcv = cw[2:3] * dconv + cw[1:2] * d1 + cw[0:1] * d2
            cv = zc * zv
            dcw_ref[0:8, :] += _rows8(cv * d2)
            dcw_ref[8:16, :] += _rows8(cv * d1)
            dcw_ref[16:24, :] += _rows8(cv * dconv)
            dzs = [(dyconv * conv_ref[rows, :]).astype(BF16), (dcv * zv).astype(BF16), (dcv * zc).astype(BF16),
                   jnp.concatenate([_pool_bwd(dqb, g, r0, sub) for g in range(N_POOL_GROUPS)], axis=1).astype(BF16)]
            da = jnp.zeros((sub, D_MODEL), F32)
            for j in range(N_CHIPS):
                dz_ref[j, rows, :] = dzs[j]
                da = da + _dot_nt(dzs[j], win_v[j])
            xt = x_ref[0, rows, :]
            r1 = _rstd(xt)
            xh = xt * r1
            dg1_ref[...] += _rows8(da * xh)
            dx_ref[0, rows, :] = dh1v + _rms_bwd(da, xh, r1, g1v)

        @pl.when(tr == 0)
        def _():
            mcb[0:HALO, :] = jnp.zeros((HALO, D_CONV), F32)
            mqb[0:HALO, :] = jnp.zeros((HALO, D_POOL), F32)
            mcb[HALO:2 * HALO, :] = dcb[0:HALO, :]
            mqb[HALO:2 * HALO, :] = dqb[0:HALO, :]
            m1 = mcb[1:1 + HALO, :]
            m2 = mcb[2:2 + HALO, :]
            zc_m = zm_ref[:, IN_SHARD:2 * IN_SHARD]
            zv_m = zm_ref[:, 2 * IN_SHARD:3 * IN_SHARD]
            cv_m = zc_m * zv_m
            dcw_ref[0:8, :] += _rows8(cv_m * m2)
            dcw_ref[8:16, :] += _rows8(cv_m * m1)
            dcv_m = cw[1:2] * m1 + cw[0:1] * m2
            dzm_ref[:, IN_SHARD:2 * IN_SHARD] += dcv_m * zv_m
            dzm_ref[:, 2 * IN_SHARD:3 * IN_SHARD] += dcv_m * zc_m
            dzm_ref[:, 3 * IN_SHARD:4 * IN_SHARD] += jnp.concatenate(
                [_pool_bwd(mqb, g, 0, HALO) for g in range(N_POOL_GROUPS)], axis=1)

        @pl.when((s == n_seq - 1) & (i == n_t - 1))
        def _():
            xm = meta_ref[...]
            rm = _rstd(xm)
            xmh = xm * rm
            am_ref[...] = (xmh * g1v).astype(BF16)
            da_m = jnp.zeros((N_META, D_MODEL), F32)
            for j in range(N_CHIPS):
                dzj = dzm_ref[:, j * IN_SHARD:(j + 1) * IN_SHARD].astype(BF16)
                dzmb_ref[j] = dzj
                da_m = da_m + _dot_nt(dzj, win_v[j])
            dg1m_ref[...] = _rows8(da_m * xmh)
            dmeta_ref[...] = _rms_bwd(da_m, xmh, rm, g1v)

    row3 = lambda c: pl.BlockSpec((1, tm, c), lambda s, i: (s, n_t - 1 - i, 0))
    row2 = lambda c: pl.BlockSpec((tm, c), lambda s, i: (s * n_t + n_t - 1 - i, 0))
    n_rows = n_seq * seq
    outs = pl.pallas_call(
        body, name="mixer_bwd", grid=(n_seq, n_t),
        out_shape=[jax.ShapeDtypeStruct((n_seq, seq, D_MODEL), F32),
                   jax.ShapeDtypeStruct((N_CHIPS, n_rows, IN_SHARD), BF16), jax.ShapeDtypeStruct((n_rows, D_MODEL), BF16),
                   jax.ShapeDtypeStruct((8, D_MODEL), F32), jax.ShapeDtypeStruct((8, D_MODEL), F32),
                   jax.ShapeDtypeStruct((8, D_POOL), F32), jax.ShapeDtypeStruct((24, D_CONV), F32),
                   jax.ShapeDtypeStruct((N_POOL_GROUPS, POOL_GROUP, POOL_GROUP), F32),
                   jax.ShapeDtypeStruct((N_META, D_IN_PROJ), F32),
                   jax.ShapeDtypeStruct((N_META, D_MODEL), F32), jax.ShapeDtypeStruct((8, D_MODEL), F32),
                   jax.ShapeDtypeStruct((N_META, D_MODEL), BF16),
                   jax.ShapeDtypeStruct((N_CHIPS, N_META, IN_SHARD), BF16)],
        in_specs=[row3(D_MODEL), row3(D_MODEL), row3(D_Z), row2(D_CONV), row2(D_POOL), row3(D_MODEL),
                  _full((N_META, D_IN_PROJ)), _full((N_META, D_MODEL)), _full((1, D_MODEL)), _full((1, D_MODEL)),
                  _full((3, D_CONV)), _full((N_POOL_GROUPS, POOL_GROUP, POOL_GROUP)), _full((1, D_POOL)),
                  _full((8, 128))] + [ANY] * 4,
        out_specs=[row3(D_MODEL), pl.BlockSpec((N_CHIPS, tm, IN_SHARD), lambda s, i: (0, s * n_t + n_t - 1 - i, 0)),
                   row2(D_MODEL),
                   _full((8, D_MODEL)), _full((8, D_MODEL)), _full((8, D_POOL)), _full((24, D_CONV)),
                   _full((N_POOL_GROUPS, POOL_GROUP, POOL_GROUP)), _full((N_META, D_IN_PROJ)),
                   _full((N_META, D_MODEL)), _full((8, D_MODEL)), _full((N_META, D_MODEL)),
                   _full((N_CHIPS, N_META, IN_SHARD))],
        scratch_shapes=[pltpu.VMEM((N_CHIPS, D_MODEL, IN_SHARD), BF16), pltpu.VMEM((D_MODEL, D_MODEL), BF16),
                        pltpu.VMEM((tm + HALO, D_CONV), F32), pltpu.VMEM((tm + HALO, D_POOL), F32),
                        pltpu.VMEM((2 * HALO, D_CONV), F32), pltpu.VMEM((2 * HALO, D_POOL), F32),
                        pltpu.SemaphoreType.DMA((2 * N_CHIPS,))],
        compiler_params=_cparams(2),
    )(dh1, m3, z3, conv2, pooled2, x3, zmeta, meta_full, g1, g2, convw, poolw, pscale, after, *gathered, *shards)
    return outs


def _mixer_weight_grads(a, dz, ycat, dm, a_meta, dz_meta, ffn_sums, small):
    n_rows = a.shape[0]
    tk = min(TK_DW, n_rows)
    n_k = n_rows // tk
    n_sc, n_sm = len(ffn_sums), _AllReduceSmall.N_IN

    def body(a_ref, dz_ref, yc_ref, dm_ref, am_ref, dzm_ref, *rest):
        ins, outs, scratch = rest[:n_sc + n_sm], rest[n_sc + n_sm:2 * n_sc + n_sm + 5], rest[2 * n_sc + n_sm + 5:]
        dwin_ref, dwout_ref = outs[:2]
        scatter = _ScatterToChips(ins[:n_sc], outs[2:2 + n_sc], *scratch[:2])
        reduce_small = _AllReduceSmall(ins[n_sc:], outs[2 + n_sc:], scratch[2:])
        k = pl.program_id(0)

        @pl.when(k == 0)
        def _():
            scatter.start()
            reduce_small.pack_and_send()
            am_t = am_ref[...].T
            for j in range(N_CHIPS):
                dwin_ref[j] = _dot(am_t, dzm_ref[j])
            dwout_ref[...] = jnp.zeros_like(dwout_ref)

        for st in range(2):
            @pl.when(k == ((st + 1) * n_k) // 3)
            def _():
                reduce_small.combine(st)

        a_t = a_ref[...].T
        for j in range(N_CHIPS):
            dwin_ref[j] += _dot(a_t, dz_ref[j])
        dwout_ref[...] += _dot_tn(yc_ref[...], dm_ref[...])

        @pl.when(k == n_k - 1)
        def _():
            reduce_small.combine(2)
            scatter.finish()

    row = pl.BlockSpec((tk, D_MODEL), lambda k: (k, 0))
    outs = pl.pallas_call(
        body, name="mixer_weight_grads", grid=(n_k,),
        out_shape=[jax.ShapeDtypeStruct((N_CHIPS, D_MODEL, IN_SHARD), F32),
                   jax.ShapeDtypeStruct((D_MODEL, D_MODEL), F32)] + _ScatterToChips.out_shape(ffn_sums)
        + _AllReduceSmall.out_shape(),
        in_specs=[row, pl.BlockSpec((N_CHIPS, tk, IN_SHARD), lambda k: (0, k, 0)), row, row,
                  _full((N_META, D_MODEL)), _full((N_CHIPS, N_META, IN_SHARD))] + [ANY] * n_sc
        + [_full(s.shape) for s in small],
        out_specs=[_full((N_CHIPS, D_MODEL, IN_SHARD)), _full((D_MODEL, D_MODEL))] + [ANY] * n_sc
        + [_full(s) for s in _AllReduceSmall.SHAPES],
        scratch_shapes=_ScatterToChips.scratch(n_sc) + _AllReduceSmall.scratch(),
        compiler_params=_cparams(1),
    )(a, dz, ycat, dm, a_meta, dz_meta, *ffn_sums, *small)
    return ([outs[0], outs[1].reshape(N_CHIPS, OUT_SHARD, D_MODEL)], outs[2:2 + n_sc], outs[2 + n_sc:])


def kernel(x, meta_tokens, norm_mix_pre, w_in, conv_w, pool_w, pool_scale, w_out, norm_mix_post, norm_ffn_pre, w_gate, w_up, w_down, norm_ffn_post, loss_target, m_meta_tokens, m_norm_mix_pre, m_w_in, m_conv_w, m_pool_w, m_pool_scale, m_w_out, m_norm_mix_post, m_norm_ffn_pre, m_w_gate, m_w_up, m_w_down, m_norm_ffn_post, v_meta_tokens, v_norm_mix_pre, v_w_in, v_conv_w, v_pool_w, v_pool_scale, v_w_out, v_norm_mix_post, v_norm_ffn_pre, v_w_gate, v_w_up, v_w_down, v_norm_ffn_post):
    n_seq, seq, _ = x.shape
    n_rows = n_seq * seq
    chip = 2 * lax.axis_index("x") + lax.axis_index("y")
    meta_cols = D_MODEL // N_CHIPS
    conv_cols = D_CONV // N_CHIPS

    small = jnp.zeros((2 * HALO, meta_cols), F32)
    small = small.at[0:N_META, :].set(meta_tokens).at[N_META:N_META + 3, 0:conv_cols].set(conv_w[0])
    poolw_bf = pool_w[0].astype(BF16)
    pscale = pool_scale
    g1, g2, g3, g4 = norm_mix_pre, norm_mix_post, norm_ffn_pre, norm_ffn_post
    place = jnp.stack([chip, lax.axis_index("c")]).astype(jnp.int32)

    mix_shards = [w_in[0].astype(BF16), w_out[0].astype(BF16)]
    ((z3, m3, h1, a_bf, conv2, pooled2, yc_bf, zmeta, meta_full, conv_full), (win_all, wout_all, _, *ffn_gathered),
     ffn_shards) = _mixer_fwd(x, g1, g2, poolw_bf, pscale, mix_shards + [small], [w_gate[0].T, w_up[0].T, w_down[0]])
    dh1, f_bf, dd_bf, ds_bf, du_bf, gg_bf, lossp, dg3p, dg4p = _ffn_fwd_bwd(
        h1.reshape(n_rows, D_MODEL), loss_target.reshape(n_rows, D_MODEL), g3, g4, ffn_gathered, ffn_shards)
    as_shards = lambda g: g.reshape(N_CHIPS, FF_SHARD, D_MODEL)
    (dwg_t, dwu_t), _ = _ffn_weight_grads("ffn_weight_grads_gate_up", [ds_bf, du_bf], f_bf, [])
    dwg_t, dwu_t = as_shards(dwg_t), as_shards(dwu_t)
    (dwd,), (dwg_recv, dwu_recv) = _ffn_weight_grads("ffn_weight_grads_down", [gg_bf], dd_bf, [dwg_t, dwu_t])
    dwd = as_shards(dwd)
    behind_bwd = _SplitComm("grad_comm_behind_mixer_bwd", [dwd],
                            _add_pairs_multi([dwg_t, dwu_t], [dwg_recv, dwu_recv], place))
    (grad_x, dz_bf, dm_bf, dg1p, dg2p, dscp, dcwp, dpw, _, dmeta, dg1m, a_meta, dz_meta) = _mixer_bwd(
        dh1.reshape(n_seq, seq, D_MODEL), m3, z3, conv2, pooled2, x, zmeta, meta_full, g1, g2, conv_full, poolw_bf,
        pscale, [win_all, wout_all], mix_shards, behind_bwd.start())
    (dwd_recv,), (dwg_rbuf, dwu_rbuf) = behind_bwd.wait(dg2p)
    (dwd,) = behind_bwd.exchanged
    mix_grads, (dwd_rbuf,), (a_red, b_red, c_red) = _mixer_weight_grads(
        a_bf, dz_bf, yc_bf, dm_bf, a_meta, dz_meta, [_add_pairs(dwd, dwd_recv, place)],
        [dg1p, dg1m, dg2p, dg3p, dg4p, lossp, dmeta, dscp, dcwp, dpw.reshape(SMALL_C_ROWS, POOL_GROUP)])

    behind_sums = _SplitComm("grad_comm_behind_ffn_sums", mix_grads, [])
    ffn_red = _add_chips([dwg_t, dwu_t, dwd], [dwg_recv, dwu_recv, dwd_recv], [dwg_rbuf, dwu_rbuf, dwd_rbuf],
                         place, after=behind_sums.start(), name="grad_add_chips_ffn")
    mix_recvs, _ = behind_sums.wait(ffn_red[0])
    mix_grads = behind_sums.exchanged
    behind_tail = _SplitComm("grad_comm_behind_ffn_tail", [], _add_pairs_multi(mix_grads, mix_recvs, place))
    as_full = lambda r: r.reshape(2 * r.shape[1], r.shape[2])
    g_wg_t, g_wu_t, g_wd = [as_full(r) for r in _gather_halves(list(ffn_red), "grad_gather_halves_ffn",
                                                                 after=behind_tail.start())]
    ffn_out = _adamw_big([(w_gate[0].T, g_wg_t, m_w_gate[0].T, v_w_gate[0].T),
                          (w_up[0].T, g_wu_t, m_w_up[0].T, v_w_up[0].T), (w_down[0], g_wd, m_w_down[0], v_w_down[0])])

    as_c = lambda p: p.reshape(SMALL_C_ROWS, POOL_GROUP)
    loss, small_out = _adamw_small(place, [a_red, b_red, c_red], [
        (meta_tokens, m_meta_tokens, v_meta_tokens),
        (g1, m_norm_mix_pre, v_norm_mix_pre),
        (conv_w[0], m_conv_w[0], v_conv_w[0]),
        (as_c(pool_w), as_c(m_pool_w), as_c(v_pool_w)),
        (pool_scale, m_pool_scale, v_pool_scale),
        (g2, m_norm_mix_post, v_norm_mix_post),
        (g3, m_norm_ffn_pre, v_norm_ffn_pre),
        (g4, m_norm_ffn_post, v_norm_ffn_post),
    ])
    _, mix_rbufs = behind_tail.wait(ffn_out[2][0], small_out[0][0])
    mix_red = _add_chips(mix_grads, mix_recvs, mix_rbufs, place)
    g_win, g_wout = [as_full(r) for r in _gather_halves(list(mix_red), "grad_gather_halves_mixer")]
    big_out = (_adamw_big([(w_in[0], g_win, m_w_in[0], v_w_in[0])])
               + _adamw_big([(w_out[0], g_wout, m_w_out[0], v_w_out[0])]) + ffn_out)
    big_out[2] = [o.T for o in big_out[2]]
    big_out[3] = [o.T for o in big_out[3]]

    s_meta, s_g1, s_conv, s_poolw, s_pscale, s_g2, s_g3, s_g4 = small_out
    b_win, b_wout, b_wg, b_wu, b_wd = big_out

    def leaf(k):
        return [s_meta[k], s_g1[k], b_win[k][None], s_conv[k][None], s_poolw[k].reshape(pool_w.shape), s_pscale[k],
                b_wout[k][None], s_g2[k], s_g3[k], b_wg[k][None], b_wu[k][None], b_wd[k][None], s_g4[k]]

    return (loss.reshape(()), grad_x, *leaf(0), *leaf(1), *leaf(2), *leaf(3))
```

```python
import functools

import jax
import jax.numpy as jnp
from jax import lax
from jax.experimental import pallas as pl
from jax.experimental.pallas import tpu as pltpu

F32 = jnp.float32
BF16 = jnp.bfloat16
MESH = pl.DeviceIdType.MESH

D_MODEL = 1024
D_CONV = 512
D_POOL = 512
POOL_GROUP = 128
N_POOL_GROUPS = 4
D_IN_PROJ = 2048
D_FF = 2816
N_CHIPS = 4
FF_SHARD = D_FF // N_CHIPS
IN_SHARD = D_IN_PROJ // N_CHIPS
OUT_SHARD = D_MODEL // N_CHIPS
D_Z = 3 * IN_SHARD
N_META = 16
HALO = 16
RMS_EPS = 1e-6

ADAM_LR = 0.001
ADAM_B1 = 0.9
ADAM_B2 = 0.999
ADAM_EPS = 1e-08
ADAM_WD = 0.01
ADAM_STEP = 10

TM_MIX_FWD = 512
TM_MIX_BWD = 512
SUB_MIX_BWD = 512
TM_FFN = 256
TK_DW = 1024
FF_CHUNK = 1024
VMEM_LIMIT = 56 * 1024 * 1024


def _cparams(n_grid):
    return pltpu.CompilerParams(dimension_semantics=("arbitrary",) * n_grid, vmem_limit_bytes=VMEM_LIMIT)


def _dot(a, b):
    return jnp.dot(a, b, preferred_element_type=F32)


def _dot_nt(a, b):
    return lax.dot_general(a, b, (((1,), (1,)), ((), ())), preferred_element_type=F32)


def _dot_tn(a, b):
    return lax.dot_general(a, b, (((0,), (0,)), ((), ())), preferred_element_type=F32)


def _rows8(v):
    r, c = v.shape
    return v.reshape(r // 8, 8, c).sum(axis=0)


def _rstd(v):
    return lax.rsqrt(jnp.mean(v * v, axis=-1, keepdims=True) + RMS_EPS)


def _rms_bwd(dy, xhat, rstd, gain):
    dyg = dy * gain
    return rstd * (dyg - xhat * jnp.mean(dyg * xhat, axis=-1, keepdims=True))


def _sigmoid(v):
    return 1.0 / (1.0 + jnp.exp(-v))


def _gcols(g):
    return slice(g * POOL_GROUP, (g + 1) * POOL_GROUP)


def _window_sum(e, g, ahead):
    n = e.shape[0]
    w = e
    for level in range(g + 1):
        shift = 1 << level
        w = w + pltpu.roll(w, (n - shift) if ahead else shift, 0)
    return w


def _pool_fwd(pb, g, n):
    e = pb[0:HALO + n, _gcols(g)]
    return _window_sum(e, g, False)[HALO:, :] * (1.0 / (2 << g)) - e[HALO:, :]


def _pool_bwd(qb, g, r0, n):
    e = qb[r0:r0 + n + HALO, _gcols(g)]
    return _window_sum(e, g, True)[0:n, :] * (1.0 / (2 << g)) - e[0:n, :]


def _full(shape):
    nd = len(shape)
    return pl.BlockSpec(shape, lambda *_: (0,) * nd)


ANY = pl.BlockSpec(memory_space=pl.ANY)


def _mesh_pos():
    x, y, c = lax.axis_index("x"), lax.axis_index("y"), lax.axis_index("c")
    chips = [(1 - x, y), (x, 1 - y), (1 - x, 1 - y)]
    return x, y, c, chips


def _half(ref, h):
    hr = ref.shape[0] // 2
    return ref.at[pl.ds(h * hr, hr), :]


class _AllGather:
    PER_ARRAY = 9

    def __init__(self, ins, outs, send_sems, recv_sems):
        self.ins, self.outs, self.send_sems, self.recv_sems = ins, outs, send_sems, recv_sems
        self.n = len(ins)

    @classmethod
    def scratch(cls, n):
        return [pltpu.SemaphoreType.DMA((cls.PER_ARRAY * n,)), pltpu.SemaphoreType.DMA((cls.PER_ARRAY * n,))]

    @staticmethod
    def out_shape(shards):
        return [jax.ShapeDtypeStruct((N_CHIPS,) + s.shape, s.dtype) for s in shards]

    def _copy(self, a, k, src, dst, to):
        i = self.PER_ARRAY * a + k
        return pltpu.make_async_remote_copy(src_ref=src, dst_ref=dst, send_sem=self.send_sems.at[i],
                                            recv_sem=self.recv_sems.at[i], device_id=to, device_id_type=MESH)

    def _piece(self, a, chip, piece, h=None):
        h = lax.axis_index("c") if h is None else h
        rows = self.ins[a].shape[0] // 4
        return self.outs[a].at[chip].at[pl.ds((2 * h + piece) * rows, rows), :]

    def _own(self, a, k):
        x, y, c, chips = _mesh_pos()
        piece = (1, 0, 0, 1)[k]
        rows = self.ins[a].shape[0] // 4
        src = self.ins[a].at[pl.ds((2 * c + piece) * rows, rows), :]
        return self._copy(a, k, src, self._piece(a, 2 * x + y, piece), (*chips[k // 2], c))

    def _relay(self, a, k):
        x, y, c, chips = _mesh_pos()
        source, to, piece = (chips[1], chips[0], 0) if k == 4 else (chips[0], chips[1], 1)
        rows = self._piece(a, 2 * source[0] + source[1], piece)
        return self._copy(a, k, rows, rows, (*to, c))

    def _sibling(self, a, k, h):
        x, y, c, chips = _mesh_pos()
        chip = chips[k - 6]
        slot = _half(self.outs[a].at[2 * chip[0] + chip[1]], h)
        return self._copy(a, k, slot, slot, (x, y, 1 - c))

    def start(self, arrays=None):
        for a in (range(self.n) if arrays is None else arrays):
            for k in range(4):
                self._own(a, k).start()

    def relay(self, a):
        self._own(a, 2).wait_recv()
        self._relay(a, 4).start()
        self._own(a, 0).wait_recv()
        self._relay(a, 5).start()

    def forward(self, a):
        c = lax.axis_index("c")
        self._own(a, 1).wait_recv()
        self._sibling(a, 6, c).start()
        self._own(a, 3).wait_recv()
        self._sibling(a, 7, c).start()
        self._relay(a, 4).wait_recv()
        self._relay(a, 5).wait_recv()
        self._sibling(a, 8, c).start()

    def finish(self, arrays=None):
        c = lax.axis_index("c")
        arrays = range(self.n) if arrays is None else arrays
        for a in arrays:
            for k in range(6, 9):
                self._sibling(a, k, 1 - c).wait_recv()
        for a in arrays:
            for k in range(4):
                self._own(a, k).wait_send()
            for k in range(4, 6):
                self._relay(a, k).wait_send()
            for k in range(6, 9):
                self._sibling(a, k, c).wait_send()


class _ExchangeHalves:
    def __init__(self, ins, recvs, send_sems, recv_sems):
        self.ins, self.recvs, self.send_sems, self.recv_sems = ins, recvs, send_sems, recv_sems

    @staticmethod
    def scratch(n):
        return [pltpu.SemaphoreType.DMA((n,)), pltpu.SemaphoreType.DMA((n,))]

    @staticmethod
    def out_shape(grads):
        return [jax.ShapeDtypeStruct((g.shape[0], g.shape[1] // 2, g.shape[2]), g.dtype) for g in grads]

    def _copies(self):
        x, y, c, _ = _mesh_pos()
        out = []
        for a, (src, dst) in enumerate(zip(self.ins, self.recvs)):
            hr = src.shape[1] // 2
            out.append(pltpu.make_async_remote_copy(
                src_ref=src.at[:, pl.ds((1 - c) * hr, hr), :], dst_ref=dst, send_sem=self.send_sems.at[a],
                recv_sem=self.recv_sems.at[a], device_id=(x, y, 1 - c), device_id_type=MESH))
        return out

    def start(self):
        for cp in self._copies():
            cp.start()

    def finish(self):
        for cp in self._copies():
            cp.wait()


class _ScatterToChips:
    def __init__(self, ins, rbufs, send_sems, recv_sems):
        self.ins, self.rbufs, self.send_sems, self.recv_sems = ins, rbufs, send_sems, recv_sems

    @staticmethod
    def scratch(n):
        return [pltpu.SemaphoreType.DMA((3 * n,)), pltpu.SemaphoreType.DMA((3 * n,))]

    @staticmethod
    def out_shape(sums):
        return [jax.ShapeDtypeStruct((3,) + s.shape[1:], BF16) for s in sums]

    def _copies(self):
        x, y, c, chips = _mesh_pos()
        out = []
        for a, (src, dst) in enumerate(zip(self.ins, self.rbufs)):
            for k, chip in enumerate(chips):
                out.append(pltpu.make_async_remote_copy(
                    src_ref=src.at[2 * chip[0] + chip[1]], dst_ref=dst.at[k], send_sem=self.send_sems.at[3 * a + k],
                    recv_sem=self.recv_sems.at[3 * a + k], device_id=(*chip, c), device_id_type=MESH))
        return out

    def start(self):
        for cp in self._copies():
            cp.start()

    def finish(self):
        for cp in self._copies():
            cp.wait()


HBM = pl.BlockSpec(memory_space=pltpu.HBM)
SEM = pl.BlockSpec(memory_space=pltpu.SEMAPHORE)


class _SplitComm:
    def __init__(self, name, exchanged, scattered):
        self.name, self.n_ex, self.n_sc = name, len(exchanged), len(scattered)
        self.n_copies = self.n_ex + 3 * self.n_sc
        zones = ([lax.empty((g.shape[0], g.shape[1] // 2, g.shape[2]), g.dtype) for g in exchanged]
                 + [lax.empty((3,) + s.shape[1:], s.dtype) for s in scattered])
        self.buffers = [pltpu.with_memory_space_constraint(v, pltpu.HBM)
                        for v in list(exchanged) + list(scattered) + zones]

    def _copies(self, bufs, send_sems, recv_sems):
        x, y, c, chips = _mesh_pos()
        n_src = self.n_ex + self.n_sc
        out = []
        for a in range(self.n_ex):
            hr = bufs[a].shape[1] // 2
            out.append(pltpu.make_async_remote_copy(
                src_ref=bufs[a].at[:, pl.ds((1 - c) * hr, hr), :], dst_ref=bufs[n_src + a], send_sem=send_sems[a],
                recv_sem=recv_sems[a], device_id=(x, y, 1 - c), device_id_type=MESH))
        for a in range(self.n_sc):
            for k, chip in enumerate(chips):
                i = self.n_ex + 3 * a + k
                out.append(pltpu.make_async_remote_copy(
                    src_ref=bufs[self.n_ex + a].at[2 * chip[0] + chip[1]], dst_ref=bufs[n_src + self.n_ex + a].at[k],
                    send_sem=send_sems[i], recv_sem=recv_sems[i], device_id=(*chip, c), device_id_type=MESH))
        return out

    def start(self):
        n_buf, n_cp = len(self.buffers), self.n_copies

        def body(*refs):
            bufs = refs[:n_buf]
            send_sems, recv_sems = refs[n_buf:n_buf + n_cp], refs[n_buf + n_cp:n_buf + 2 * n_cp]
            for cp in self._copies(bufs, send_sems, recv_sems):
                cp.start()
            refs[-1][...] = jnp.zeros_like(refs[-1])

        outs = pl.pallas_call(
            body, name=self.name + "_start",
            out_shape=[pltpu.SemaphoreType.DMA(())] * (2 * n_cp) + [pltpu.HBM(b.shape, b.dtype) for b in self.buffers]
            + [jax.ShapeDtypeStruct((8, 128), F32)],
            in_specs=[HBM] * n_buf, out_specs=[SEM] * (2 * n_cp) + [HBM] * n_buf + [pl.BlockSpec(memory_space=pltpu.VMEM)],
            input_output_aliases={i: 2 * n_cp + i for i in range(n_buf)},
            compiler_params=pltpu.CompilerParams(has_side_effects=pltpu.SideEffectType.DATAFLOW_SIDE_EFFECTING),
        )(*self.buffers)
        self.sems, self.buffers = outs[:2 * n_cp], outs[2 * n_cp:2 * n_cp + n_buf]
        return outs[-1]

    def wait(self, *after):
        n_buf, n_cp = len(self.buffers), self.n_copies

        def body(*refs):
            bufs = refs[:n_buf]
            send_sems, recv_sems = refs[n_buf:n_buf + n_cp], refs[n_buf + n_cp:n_buf + 2 * n_cp]
            for cp in self._copies(bufs, send_sems, recv_sems):
                cp.wait_send()
                cp.wait_recv()

        outs = pl.pallas_call(
            body, name=self.name + "_wait", out_shape=[pltpu.HBM(b.shape, b.dtype) for b in self.buffers],
            in_specs=[HBM] * n_buf + [SEM] * (2 * n_cp) + [ANY] * len(after), out_specs=[HBM] * n_buf,
            input_output_aliases={i: i for i in range(n_buf)},
            compiler_params=pltpu.CompilerParams(has_side_effects=pltpu.SideEffectType.DATAFLOW_SIDE_EFFECTING),
        )(*self.buffers, *self.sems, *after)
        self.exchanged = outs[:self.n_ex]
        zones = outs[self.n_ex + self.n_sc:]
        return zones[:self.n_ex], zones[self.n_ex:]


def _gather_halves(halves, name, after=None):
    n = len(halves)
    extra = [] if after is None else [after]

    def body(*refs):
        ins, outs = refs[:n], refs[n + len(extra):2 * n + len(extra)]
        send_sems, recv_sems = refs[2 * n + len(extra):]
        x, y, c, _ = _mesh_pos()
        sib = (x, y, 1 - c)
        remote = [pltpu.make_async_remote_copy(src_ref=ins[a].at[c], dst_ref=outs[a].at[c],
                                               send_sem=send_sems.at[a], recv_sem=recv_sems.at[a],
                                               device_id=sib, device_id_type=MESH) for a in range(n)]
        for cp in remote:
            cp.start()
        for a in range(n):
            pltpu.make_async_remote_copy(src_ref=ins[a].at[1 - c], dst_ref=outs[a].at[1 - c], send_sem=send_sems.at[a],
                                         recv_sem=recv_sems.at[a], device_id=sib, device_id_type=MESH).wait_recv()
        for cp in remote:
            cp.wait_send()

    return pl.pallas_call(
        body, name=name,
        out_shape=[jax.ShapeDtypeStruct(h.shape, F32) for h in halves],
        in_specs=[ANY] * (n + len(extra)), out_specs=[ANY] * n, input_output_aliases={a: a for a in range(n)},
        scratch_shapes=[pltpu.SemaphoreType.DMA((n,)), pltpu.SemaphoreType.DMA((n,))],
    )(*halves, *extra)


SMALL_A_ROWS = 24
SMALL_B_ROWS = 8
SMALL_C_ROWS = N_POOL_GROUPS * POOL_GROUP


class _AllReduceSmall:
    N_IN = 10
    SHAPES = [(SMALL_A_ROWS, D_MODEL), (SMALL_B_ROWS, D_CONV), (SMALL_C_ROWS, POOL_GROUP)]

    def __init__(self, ins, outs, scratch):
        self.ins, self.outs = ins, outs
        self.bufs, self.rcvs, self.send_sems, self.recv_sems = scratch[:3], scratch[3:6], scratch[6], scratch[7]

    @classmethod
    def scratch(cls):
        return ([pltpu.VMEM((3,) + s, F32) for s in cls.SHAPES] + [pltpu.VMEM((3,) + s, F32) for s in cls.SHAPES]
                + [pltpu.SemaphoreType.DMA((9,)), pltpu.SemaphoreType.DMA((9,))])

    @classmethod
    def out_shape(cls):
        return [jax.ShapeDtypeStruct(s, F32) for s in cls.SHAPES]

    def _copies(self, st):
        x, y, c, _ = _mesh_pos()
        peer = [(x, y, 1 - c), (1 - x, y, c), (x, 1 - y, c)][st]
        return [pltpu.make_async_remote_copy(
            src_ref=buf.at[st], dst_ref=rcv.at[st], send_sem=self.send_sems.at[3 * st + i],
            recv_sem=self.recv_sems.at[3 * st + i], device_id=peer, device_id_type=MESH)
            for i, (buf, rcv) in enumerate(zip(self.bufs, self.rcvs))]

    def pack_and_send(self):
        dg1_ref, dg1m_ref, dg2_ref, dg3_ref, dg4_ref, loss_ref, dmeta_ref, dsc_ref, dcw_ref, dpw_ref = self.ins
        a_buf, b_buf, c_buf = self.bufs

        def rowsum(v):
            return jnp.sum(v, axis=0, keepdims=True)

        a_buf[0, 0:1, :] = rowsum(dg1_ref[...] + dg1m_ref[...])
        a_buf[0, 1:2, :] = rowsum(dg2_ref[...])
        a_buf[0, 2:3, :] = rowsum(dg3_ref[...])
        a_buf[0, 3:4, :] = rowsum(dg4_ref[...])
        loss = jnp.sum(rowsum(loss_ref[...]), axis=1, keepdims=True) * (0.5 / D_MODEL)
        a_buf[0, 4:5, :] = jnp.broadcast_to(loss, (1, D_MODEL))
        a_buf[0, 5:8, :] = jnp.zeros((3, D_MODEL), F32)
        a_buf[0, 8:24, :] = dmeta_ref[...]
        b_buf[0, 0:1, :] = rowsum(dsc_ref[...])
        for k in range(3):
            b_buf[0, 1 + k:2 + k, :] = rowsum(dcw_ref[8 * k:8 * k + 8, :])
        b_buf[0, 4:8, :] = jnp.zeros((4, D_CONV), F32)
        c_buf[0] = dpw_ref[...]
        for cp in self._copies(0):
            cp.start()

    def combine(self, st):
        for cp in self._copies(st):
            cp.wait()
        if st < 2:
            for buf, rcv in zip(self.bufs, self.rcvs):
                buf[st + 1] = buf[st] + rcv[st]
            for cp in self._copies(st + 1):
                cp.start()
        else:
            for out, buf, rcv in zip(self.outs, self.bufs, self.rcvs):
                out[...] = buf[st] + rcv[st]


def _row_block(rows):
    for cand in (512, 448, 384, 352, 320, 256, 128, 64, 32, 16):
        if rows % cand == 0:
            return cand
    return rows


def _add_pairs_multi(grads, recvs, place):
    n = len(grads)
    n_sh = grads[0].shape[0]
    halves = [g.shape[1] // 2 for g in grads]
    n_steps = halves[0] // _row_block(halves[0])
    blocks = [(hr // n_steps, g.shape[2]) for hr, g in zip(halves, grads)]

    def body(place_ref, *refs):
        for a_ref, b_ref, o_ref in zip(refs[:n], refs[n:2 * n], refs[2 * n:]):
            o_ref[...] = (a_ref[0] + b_ref[...]).astype(BF16)

    return pl.pallas_call(
        body, name="grad_add_pairs",
        grid_spec=pltpu.PrefetchScalarGridSpec(
            num_scalar_prefetch=1, grid=(n_sh, n_steps),
            in_specs=[pl.BlockSpec((1, 1, br, cols), lambda j, i, p: (j, p[1], i, 0)) for br, cols in blocks]
            + [pl.BlockSpec((1, br, cols), lambda j, i, p: (j, i, 0)) for br, cols in blocks],
            out_specs=[pl.BlockSpec((1, br, cols), lambda j, i, p: (j, i, 0)) for br, cols in blocks]),
        out_shape=[jax.ShapeDtypeStruct((n_sh, hr, g.shape[2]), BF16) for hr, g in zip(halves, grads)],
        compiler_params=_cparams(2),
    )(place, *[g.reshape(n_sh, 2, hr, g.shape[2]) for hr, g in zip(halves, grads)], *recvs)


def _add_pairs(grad, recv, place):
    return _add_pairs_multi([grad], [recv], place)[0]


def _add_chips(grads, recvs, rbufs, place, after=None, name="grad_add_chips"):
    n = len(grads)
    n_sh = grads[0].shape[0]
    halves = [g.shape[1] // 2 for g in grads]
    n_steps = halves[0] // _row_block(halves[0])
    blocks = [(hr // n_steps, g.shape[2]) for hr, g in zip(halves, grads)]
    extra = [] if after is None else [after]

    def body(place_ref, *refs):
        for a_ref, b_ref, r_ref, o_ref in zip(refs[:n], refs[n:2 * n], refs[2 * n:3 * n], refs[3 * n + len(extra):]):
            own = a_ref[0, 0] + b_ref[0]
            o_ref[0] = ((own + r_ref[0].astype(F32)) + r_ref[1].astype(F32)) + r_ref[2].astype(F32)

    return pl.pallas_call(
        body, name=name,
        grid_spec=pltpu.PrefetchScalarGridSpec(
            num_scalar_prefetch=1, grid=(n_steps,),
            in_specs=[pl.BlockSpec((1, 1, br, cols), lambda i, p: (p[0], p[1], i, 0)) for br, cols in blocks]
            + [pl.BlockSpec((1, br, cols), lambda i, p: (p[0], i, 0)) for br, cols in blocks]
            + [pl.BlockSpec((3, br, cols), lambda i, p: (0, i, 0)) for br, cols in blocks]
            + [pl.BlockSpec((8, 128), lambda i, p: (0, 0))] * len(extra),
            out_specs=[pl.BlockSpec((1, br, cols), lambda i, p: (p[1], i, 0)) for br, cols in blocks]),
        out_shape=[jax.ShapeDtypeStruct((2, hr, g.shape[2]), F32) for hr, g in zip(halves, grads)],
        compiler_params=_cparams(1),
    )(place, *[g.reshape(n_sh, 2, hr, g.shape[2]) for hr, g in zip(halves, grads)], *recvs, *rbufs, *extra)


def _adamw_math(w, g, m, v):
    m2 = ADAM_B1 * m + (1.0 - ADAM_B1) * g
    v2 = ADAM_B2 * v + (1.0 - ADAM_B2) * (g * g)
    m_hat = m2 / (1.0 - ADAM_B1 ** ADAM_STEP)
    v_hat = v2 / (1.0 - ADAM_B2 ** ADAM_STEP)
    delta = -ADAM_LR * (m_hat / (jnp.sqrt(v_hat) + ADAM_EPS) + ADAM_WD * w)
    return delta, m2, v2


def _adamw_big(groups):
    n = len(groups)
    rows, cols = groups[0][0].shape
    br = _row_block(rows)
    if n > 1 and br % 16 == 0:
        br //= 2

    def body(*refs):
        for i in range(n):
            w_ref, g_ref, m_ref, v_ref = refs[4 * i:4 * i + 4]
            g_out_ref, d_ref, m2_ref, v2_ref = refs[4 * n + 4 * i:4 * n + 4 * i + 4]
            g = g_ref[...]
            d, m2, v2 = _adamw_math(w_ref[...], g, m_ref[...], v_ref[...])
            g_out_ref[...] = g
            d_ref[...] = d
            m2_ref[...] = m2
            v2_ref[...] = v2

    spec = pl.BlockSpec((br, cols), lambda i: (i, 0))
    outs = pl.pallas_call(
        body, name="adamw_big", grid=(rows // br,),
        out_shape=[jax.ShapeDtypeStruct((rows, cols), F32)] * (4 * n),
        in_specs=[spec] * (4 * n), out_specs=[spec] * (4 * n), compiler_params=_cparams(1),
    )(*[a for grp in groups for a in grp])
    return [list(outs[4 * i:4 * i + 4]) for i in range(n)]


def _adamw_small(place, reduced, params):
    n = len(params)
    meta_cols, conv_cols = D_MODEL // N_CHIPS, D_CONV // N_CHIPS

    def body(place_ref, a_ref, b_ref, c_ref, *refs):
        ins, loss_ref, outs = refs[:3 * n], refs[3 * n], refs[3 * n + 1:]
        chip = place_ref[0]

        def own_cols(ref, r0, n_r, width):
            out = ref[r0:r0 + n_r, 0:width]
            for j in range(1, N_CHIPS):
                out = jnp.where(chip == j, ref[r0:r0 + n_r, j * width:(j + 1) * width], out)
            return out

        grads = [own_cols(a_ref, 8, N_META, meta_cols), a_ref[0:1, :], own_cols(b_ref, 1, 3, conv_cols), c_ref[...],
                 b_ref[0:1, :], a_ref[1:2, :], a_ref[2:3, :], a_ref[3:4, :]]
        loss_ref[...] = a_ref[4:5, 0:1]
        for i, g in enumerate(grads):
            w, m, v = (r[...] for r in ins[3 * i:3 * i + 3])
            for o, val in zip(outs[4 * i:4 * i + 4], (g,) + _adamw_math(w, g, m, v)):
                o[...] = val

    vm = pl.BlockSpec(memory_space=pltpu.VMEM)
    flat = [a for grp in params for a in grp]
    out_shape = ([jax.ShapeDtypeStruct((1, 1), F32)]
                 + [jax.ShapeDtypeStruct(grp[0].shape, F32) for grp in params for _ in range(4)])
    outs = pl.pallas_call(body, name="adamw_small", out_shape=out_shape,
                          in_specs=[pl.BlockSpec(memory_space=pltpu.SMEM)] + [vm] * (3 + 3 * n),
                          out_specs=[vm] * (1 + 4 * n))(place, *reduced, *flat)
    return outs[0], [tuple(outs[1 + 4 * i:5 + 4 * i]) for i in range(n)]


def _load_gathered(gathered, shards, dst_slots, sems):
    n = len(gathered)
    me = 2 * lax.axis_index("x") + lax.axis_index("y")

    def copies(j, own):
        return [pltpu.make_async_copy(shards[a] if own else gathered[a].at[j], dst_slots[a](j), sems.at[n * j + a])
                for a in range(n)]

    for wait in (False, True):
        for j in range(N_CHIPS):
            for own in (False, True):
                @pl.when((me == j) == own)
                def _():
                    for cp in copies(j, own):
                        cp.wait() if wait else cp.start(priority=1)


N_MIX_SHARDS = 3


def _mixer_fwd(x3, g1, g2, poolw, pscale, shards, ffn_f32):
    n_seq, seq, _ = x3.shape
    tm = min(TM_MIX_FWD, seq)
    n_t = seq // tm
    n_steps = n_seq * n_t
    n_ffn = len(ffn_f32)
    n_ag = N_MIX_SHARDS + n_ffn
    ffn_bf16 = [jax.ShapeDtypeStruct(w.shape, BF16) for w in ffn_f32]
    small_rows = shards[2].shape[0]
    conv_cols = D_CONV // N_CHIPS

    def body(x_ref, g1_ref, g2_ref, pw_ref, ps_ref, *rest):
        ffn_f32_refs, ffn_bf_refs = rest[N_MIX_SHARDS:n_ag], rest[2 * n_ag + 10:2 * n_ag + 10 + n_ffn]
        ag = _AllGather(list(rest[:N_MIX_SHARDS]) + list(ffn_bf_refs), rest[n_ag + 10:2 * n_ag + 10], *rest[-2:])
        (z_ref, m_ref, h1_ref, a_ref, conv_ref, pooled_ref, yc_ref, zm_ref, meta_ref,
         cw_ref) = rest[n_ag:n_ag + 10]
        win_v, wout_v, small_v, cvb, pb, load_sems, stage_v, cast_v, cast_sems = rest[2 * n_ag + 10 + n_ffn:-2]
        s, t = pl.program_id(0), pl.program_id(1)
        step = s * n_t + t

        def round_ffn_piece(i):
            load = pltpu.make_async_copy(ffn_f32_refs[i], stage_v, cast_sems.at[0])
            load.start()
            load.wait()
            cast_v[...] = stage_v[...].astype(BF16)
            store = pltpu.make_async_copy(cast_v, ffn_bf_refs[i], cast_sems.at[1])
            store.start()
            store.wait()

        @pl.when(step == 0)
        def _():
            ag.start(range(N_MIX_SHARDS))
            round_ffn_piece(0)
            round_ffn_piece(1)
            for a in range(N_MIX_SHARDS):
                ag.relay(a)
            round_ffn_piece(2)
            ag.start(range(N_MIX_SHARDS, n_ag))
            for a in range(N_MIX_SHARDS):
                ag.forward(a)
            ag.finish(range(N_MIX_SHARDS))
            _load_gathered(ag.outs[:N_MIX_SHARDS], ag.ins[:N_MIX_SHARDS],
                           [lambda j: win_v.at[j], lambda j: wout_v.at[pl.ds(j * OUT_SHARD, OUT_SHARD), :],
                            lambda j: small_v.at[j]], load_sems)

            meta = jnp.concatenate([small_v[j, 0:N_META, :] for j in range(N_CHIPS)], axis=1)
            meta_ref[...] = meta
            cw_ref[...] = jnp.concatenate([small_v[j, N_META:N_META + 3, 0:conv_cols] for j in range(N_CHIPS)], axis=1)
            a_meta = (meta * _rstd(meta) * g1_ref[...]).astype(BF16)
            for j in range(N_CHIPS):
                zm_ref[:, j * IN_SHARD:(j + 1) * IN_SHARD] = _dot(a_meta, win_v[j])

        for i in range(n_ffn):
            @pl.when(step == (3 * (i + 1) * n_steps) // (4 * n_ffn + 4))
            def _():
                ag.relay(N_MIX_SHARDS + i)

        for i in range(n_ffn):
            @pl.when(step == min(n_steps // 2 + ((i + 1) * n_steps) // (2 * n_ffn + 2), n_steps - 1))
            def _():
                ag.forward(N_MIX_SHARDS + i)

        @pl.when(t == 0)
        def _():
            cvb[0:HALO, :] = zm_ref[:, IN_SHARD:2 * IN_SHARD] * zm_ref[:, 2 * IN_SHARD:3 * IN_SHARD]
            pb[0:HALO, :] = zm_ref[:, 3 * IN_SHARD:4 * IN_SHARD]

        @pl.when(t > 0)
        def _():
            cvb[0:HALO, :] = cvb[tm:tm + HALO, :]
            pb[0:HALO, :] = pb[tm:tm + HALO, :]

        xt = x_ref[0]
        a = (xt * _rstd(xt) * g1_ref[...]).astype(BF16)
        a_ref[...] = a
        zb = _dot(a, win_v[0])
        zc = _dot(a, win_v[1])
        zv = _dot(a, win_v[2])
        zp = _dot(a, win_v[3])
        z_ref[0, :, 0:IN_SHARD] = zb
        z_ref[0, :, IN_SHARD:2 * IN_SHARD] = zc
        z_ref[0, :, 2 * IN_SHARD:3 * IN_SHARD] = zv
        cv = zc * zv
        cvb[HALO:HALO + tm, :] = cv
        pb[HALO:HALO + tm, :] = zp
        cw = cw_ref[...]
        conv = cw[0:1] * cvb[HALO - 2:HALO - 2 + tm, :] + cw[1:2] * cvb[HALO - 1:HALO - 1 + tm, :] + cw[2:3] * cv
        conv_ref[...] = conv
        parts = [(zb * conv).astype(BF16)]
        for g in range(N_POOL_GROUPS):
            pooled = _pool_fwd(pb, g, tm).astype(BF16)
            pooled_ref[:, _gcols(g)] = pooled
            parts.append((_dot(pooled, pw_ref[g]) * ps_ref[:, _gcols(g)]).astype(BF16))
        ycat = jnp.concatenate(parts, axis=1)
        yc_ref[...] = ycat
        m = _dot(ycat, wout_v[...])
        m_ref[0] = m
        h1_ref[0] = xt + m * _rstd(m) * g2_ref[...]

        @pl.when(step == n_steps - 1)
        def _():
            ag.finish(range(N_MIX_SHARDS, n_ag))

    n_rows = n_seq * seq
    row = lambda c: pl.BlockSpec((1, tm, c), lambda s, t: (s, t, 0))
    row2 = lambda c: pl.BlockSpec((tm, c), lambda s, t: (s * n_t + t, 0))
    outs = pl.pallas_call(
        body, name="mixer_fwd", grid=(n_seq, n_t),
        out_shape=[jax.ShapeDtypeStruct((n_seq, seq, D_Z), F32), jax.ShapeDtypeStruct((n_seq, seq, D_MODEL), F32),
                   jax.ShapeDtypeStruct((n_seq, seq, D_MODEL), F32), jax.ShapeDtypeStruct((n_rows, D_MODEL), BF16),
                   jax.ShapeDtypeStruct((n_rows, D_CONV), F32), jax.ShapeDtypeStruct((n_rows, D_POOL), BF16),
                   jax.ShapeDtypeStruct((n_rows, D_MODEL), BF16), jax.ShapeDtypeStruct((N_META, D_IN_PROJ), F32),
                   jax.ShapeDtypeStruct((N_META, D_MODEL), F32), jax.ShapeDtypeStruct((3, D_CONV), F32)]
        + _AllGather.out_shape(list(shards) + ffn_bf16) + ffn_bf16,
        in_specs=[row(D_MODEL), _full((1, D_MODEL)), _full((1, D_MODEL)),
                  _full((N_POOL_GROUPS, POOL_GROUP, POOL_GROUP)), _full((1, D_POOL))] + [ANY] * n_ag,
        out_specs=[row(D_Z), row(D_MODEL), row(D_MODEL), row2(D_MODEL), row2(D_CONV), row2(D_POOL), row2(D_MODEL),
                   _full((N_META, D_IN_PROJ)), _full((N_META, D_MODEL)), _full((3, D_CONV))] + [ANY] * (n_ag + n_ffn),
        scratch_shapes=[pltpu.VMEM((N_CHIPS, D_MODEL, IN_SHARD), BF16), pltpu.VMEM((D_MODEL, D_MODEL), BF16),
                        pltpu.VMEM((N_CHIPS, small_rows, D_MODEL // N_CHIPS), F32),
                        pltpu.VMEM((HALO + tm, D_CONV), F32), pltpu.VMEM((HALO + tm, D_POOL), F32),
                        pltpu.SemaphoreType.DMA((N_MIX_SHARDS * N_CHIPS,)),
                        pltpu.VMEM(ffn_f32[0].shape, F32), pltpu.VMEM(ffn_f32[0].shape, BF16),
                        pltpu.SemaphoreType.DMA((2,))] + _AllGather.scratch(n_ag),
        compiler_params=_cparams(2),
    )(x3, g1, g2, poolw, pscale, *shards, *ffn_f32)
    return outs[:10], outs[10:10 + n_ag], outs[10 + n_ag:]


def _ffn_chunks():
    out, r0 = [], 0
    while r0 < D_FF:
        out.append((r0, min(FF_CHUNK, D_FF - r0)))
        r0 += FF_CHUNK
    return out


def _ffn_fwd_bwd(h1, target, g3, g4, gathered, shards):
    n_rows = h1.shape[0]
    tm = min(TM_FFN, n_rows)
    chunks = _ffn_chunks()

    def body(h1_ref, t_ref, g3_ref, g4_ref, wg_all, wu_all, wd_all, wg_s, wu_s, wd_s,
             dh1_ref, f_ref, dd_ref, ds_ref, du_ref, gg_ref, loss_ref, dg3_ref, dg4_ref,
             wg_v, wu_v, wd_v, s_sc, u_sc, sems):
        @pl.when(pl.program_id(0) == 0)
        def _():
            _load_gathered([wg_all, wu_all, wd_all], [wg_s, wu_s, wd_s],
                           [functools.partial(lambda v, j: v.at[pl.ds(j * FF_SHARD, FF_SHARD), :], v)
                            for v in (wg_v, wu_v, wd_v)], sems)
            loss_ref[...] = jnp.zeros_like(loss_ref)
            dg3_ref[...] = jnp.zeros_like(dg3_ref)
            dg4_ref[...] = jnp.zeros_like(dg4_ref)

        h1v = h1_ref[...]
        r3 = _rstd(h1v)
        hh = h1v * r3
        g3v, g4v = g3_ref[...], g4_ref[...]
        f = (hh * g3v).astype(BF16)
        f_ref[...] = f
        d = jnp.zeros((tm, D_MODEL), F32)
        for r0, sz in chunks:
            s = _dot_nt(f, wg_v[r0:r0 + sz, :])
            u = _dot_nt(f, wu_v[r0:r0 + sz, :])
            s_sc[:, r0:r0 + sz] = s
            u_sc[:, r0:r0 + sz] = u
            gc = (s * _sigmoid(s) * u).astype(BF16)
            gg_ref[:, r0:r0 + sz] = gc
            d = d + _dot(gc, wd_v[r0:r0 + sz, :])
        r4 = _rstd(d)
        dh = d * r4
        err = (h1v + dh * g4v) - t_ref[...]
        loss_ref[...] += _rows8(err * err)
        dy = err * (1.0 / D_MODEL)
        dg4_ref[...] += _rows8(dy * dh)
        ddb = _rms_bwd(dy, dh, r4, g4v).astype(BF16)
        dd_ref[...] = ddb
        df = jnp.zeros((tm, D_MODEL), F32)
        for r0, sz in chunks:
            dgg = _dot_nt(ddb, wd_v[r0:r0 + sz, :])
            s = s_sc[:, r0:r0 + sz]
            u = u_sc[:, r0:r0 + sz]
            sig = _sigmoid(s)
            dsc = (dgg * u * (sig * (1.0 + s * (1.0 - sig)))).astype(BF16)
            duc = (dgg * (s * sig)).astype(BF16)
            ds_ref[:, r0:r0 + sz] = dsc
            du_ref[:, r0:r0 + sz] = duc
            df = df + _dot(dsc, wg_v[r0:r0 + sz, :]) + _dot(duc, wu_v[r0:r0 + sz, :])
        dg3_ref[...] += _rows8(df * hh)
        dh1_ref[...] = dy + _rms_bwd(df, hh, r3, g3v)

    row = pl.BlockSpec((tm, D_MODEL), lambda i: (i, 0))
    ffrow = pl.BlockSpec((tm, D_FF), lambda i: (i, 0))
    acc = _full((8, D_MODEL))
    act_bf = jax.ShapeDtypeStruct((n_rows, D_MODEL), BF16)
    ff_bf = jax.ShapeDtypeStruct((n_rows, D_FF), BF16)
    acc_shape = jax.ShapeDtypeStruct((8, D_MODEL), F32)
    w_vmem = pltpu.VMEM((D_FF, D_MODEL), BF16)
    return pl.pallas_call(
        body, name="ffn_fwd_bwd", grid=(n_rows // tm,),
        out_shape=[jax.ShapeDtypeStruct((n_rows, D_MODEL), F32), act_bf, act_bf, ff_bf, ff_bf, ff_bf,
                   acc_shape, acc_shape, acc_shape],
        in_specs=[row, row, _full((1, D_MODEL)), _full((1, D_MODEL))] + [ANY] * 6,
        out_specs=[row, row, row, ffrow, ffrow, ffrow, acc, acc, acc],
        scratch_shapes=[w_vmem, w_vmem, w_vmem, pltpu.VMEM((tm, D_FF), F32), pltpu.VMEM((tm, D_FF), F32),
                        pltpu.SemaphoreType.DMA((3 * N_CHIPS,))],
        compiler_params=_cparams(1),
    )(h1, target, g3, g4, *gathered, *shards)


def _ffn_weight_grads(name, acts, other, exchanged):
    n_rows = other.shape[0]
    n_a, n_ex = len(acts), len(exchanged)
    n_c = n_a
    tk = min(TK_DW, n_rows)
    n_k = n_rows // tk
    half = D_FF // n_c

    def body(other_ref, *rest):
        act_refs = rest[:n_a]
        out_refs = rest[n_a + n_ex:2 * n_a + n_ex]
        c, k = pl.program_id(0), pl.program_id(1)
        if n_ex:
            ex = _ExchangeHalves(rest[n_a:n_a + n_ex], rest[2 * n_a + n_ex:2 * n_a + 2 * n_ex], *rest[-2:])

            @pl.when((c == 0) & (k == 0))
            def _():
                ex.start()

        @pl.when(k == 0)
        def _():
            for o in out_refs:
                o[...] = jnp.zeros_like(o)

        ov = other_ref[...]
        for a, o in zip(act_refs, out_refs):
            o[...] += _dot_tn(a[...], ov)

        if n_ex:
            @pl.when((c == n_c - 1) & (k == n_k - 1))
            def _():
                ex.finish()

    row = pl.BlockSpec((tk, D_MODEL), lambda c, k: (k, 0))
    ffrow = pl.BlockSpec((tk, half), lambda c, k: (k, c))
    out = pl.BlockSpec((half, D_MODEL), lambda c, k: (c, 0))
    outs = pl.pallas_call(
        body, name=name, grid=(n_c, n_k),
        out_shape=[jax.ShapeDtypeStruct((D_FF, D_MODEL), F32)] * n_a + _ExchangeHalves.out_shape(exchanged),
        in_specs=[row] + [ffrow] * n_a + [ANY] * n_ex, out_specs=[out] * n_a + [ANY] * n_ex,
        scratch_shapes=_ExchangeHalves.scratch(n_ex) if n_ex else [],
        compiler_params=_cparams(2),
    )(other, *acts, *exchanged)
    return outs[:n_a], outs[n_a:]


def _mixer_bwd(dh1, m3, z3, conv2, pooled2, x3, zmeta, meta_full, g1, g2, convw, poolw, pscale, gathered, shards,
               after):
    n_seq, seq, _ = x3.shape
    tm = min(TM_MIX_BWD, seq)
    sub = min(SUB_MIX_BWD, tm)
    n_t = seq // tm
    n_out = 13

    def body(dh1_ref, m_ref, z_ref, conv_ref, pooled_ref, x_ref, zm_ref, meta_ref, g1_ref, g2_ref, cw_ref, pw_ref,
             ps_ref, after_ref, win_all, wout_all, win_s, wout_s, *rest):
        (dx_ref, dz_ref, dm_ref, dg1_ref, dg2_ref, dsc_ref, dcw_ref, dpw_ref, dzm_ref, dmeta_ref, dg1m_ref, am_ref,
         dzmb_ref) = rest[:n_out]
        win_v, wout_v, dcb, dqb, mcb, mqb, load_sems = rest[n_out:]
        s, i = pl.program_id(0), pl.program_id(1)
        tr = n_t - 1 - i

        @pl.when((s == 0) & (i == 0))
        def _():
            _load_gathered([win_all, wout_all], [win_s, wout_s],
                           [lambda j: win_v.at[j], lambda j: wout_v.at[pl.ds(j * OUT_SHARD, OUT_SHARD), :]], load_sems)
            for ref in (dg1_ref, dg2_ref, dsc_ref, dcw_ref, dpw_ref, dzm_ref):
                ref[...] = jnp.zeros_like(ref)

        @pl.when(i == 0)
        def _():
            dcb[tm:tm + HALO, :] = jnp.zeros((HALO, D_CONV), F32)
            dqb[tm:tm + HALO, :] = jnp.zeros((HALO, D_POOL), F32)

        @pl.when(i > 0)
        def _():
            dcb[tm:tm + HALO, :] = dcb[0:HALO, :]
            dqb[tm:tm + HALO, :] = dqb[0:HALO, :]

        g1v, g2v = g1_ref[...], g2_ref[...]
        cw = cw_ref[...]

        for r0 in range(tm - sub, -1, -sub):
            rows = slice(r0, r0 + sub)
            dh1v = dh1_ref[0, rows, :]
            mv = m_ref[0, rows, :]
            r2 = _rstd(mv)
            mh = mv * r2
            dg2_ref[...] += _rows8(dh1v * mh)
            dmb = _rms_bwd(dh1v, mh, r2, g2v).astype(BF16)
            dm_ref[rows, :] = dmb
            dyc = _dot_nt(dmb, wout_v[...])
            dyconv = dyc[:, 0:D_CONV]

            for g in range(N_POOL_GROUPS):
                pooled = pooled_ref[rows, _gcols(g)]
                mixed = _dot(pooled, pw_ref[g])
                scale = ps_ref[:, _gcols(g)]
                dyp = dyc[:, D_CONV + g * POOL_GROUP:D_CONV + (g + 1) * POOL_GROUP]
                dsc_ref[:, _gcols(g)] += _rows8(dyp * mixed)
                dmix = (dyp * scale).astype(BF16)
                dpw_ref[g] += _dot_tn(pooled, dmix)
                dqb[rows, _gcols(g)] = _dot_nt(dmix, pw_ref[g])

            zb = z_ref[0, rows, 0:IN_SHARD]
            zc = z_ref[0, rows, IN_SHARD:2 * IN_SHARD]
            zv = z_ref[0, rows, 2 * IN_SHARD:3 * IN_SHARD]
            dconv = dyconv * zb
            dcb[rows, :] = dconv
            d1 = dcb[r0 + 1:r0 + 1 + sub, :]
            d2 = dcb[r0 + 2:r0 + 2 + sub, :]
            dcv = cw[2:3] * dconv + cw[1:2] * d1 + cw[0:1] * d2
            cv = zc * zv
            dcw_ref[0:8, :] += _rows8(cv * d2)
            dcw_ref[8:16, :] += _rows8(cv * d1)
            dcw_ref[16:24, :] += _rows8(cv * dconv)
            dzs = [(dyconv * conv_ref[rows, :]).astype(BF16), (dcv * zv).astype(BF16), (dcv * zc).astype(BF16),
                   jnp.concatenate([_pool_bwd(dqb, g, r0, sub) for g in range(N_POOL_GROUPS)], axis=1).astype(BF16)]
            da = jnp.zeros((sub, D_MODEL), F32)
            for j in range(N_CHIPS):
                dz_ref[j, rows, :] = dzs[j]
                da = da + _dot_nt(dzs[j], win_v[j])
            xt = x_ref[0, rows, :]
            r1 = _rstd(xt)
            xh = xt * r1
            dg1_ref[...] += _rows8(da * xh)
            dx_ref[0, rows, :] = dh1v + _rms_bwd(da, xh, r1, g1v)

        @pl.when(tr == 0)
        def _():
            mcb[0:HALO, :] = jnp.zeros((HALO, D_CONV), F32)
            mqb[0:HALO, :] = jnp.zeros((HALO, D_POOL), F32)
            mcb[HALO:2 * HALO, :] = dcb[0:HALO, :]
            mqb[HALO:2 * HALO, :] = dqb[0:HALO, :]
            m1 = mcb[1:1 + HALO, :]
            m2 = mcb[2:2 + HALO, :]
            zc_m = zm_ref[:, IN_SHARD:2 * IN_SHARD]
            zv_m = zm_ref[:, 2 * IN_SHARD:3 * IN_SHARD]
            cv_m = zc_m * zv_m
            dcw_ref[0:8, :] += _rows8(cv_m * m2)
            dcw_ref[8:16, :] += _rows8(cv_m * m1)
            dcv_m = cw[1:2] * m1 + cw[0:1] * m2
            dzm_ref[:, IN_SHARD:2 * IN_SHARD] += dcv_m * zv_m
            dzm_ref[:, 2 * IN_SHARD:3 * IN_SHARD] += dcv_m * zc_m
            dzm_ref[:, 3 * IN_SHARD:4 * IN_SHARD] += jnp.concatenate(
                [_pool_bwd(mqb, g, 0, HALO) for g in range(N_POOL_GROUPS)], axis=1)

        @pl.when((s == n_seq - 1) & (i == n_t - 1))
        def _():
            xm = meta_ref[...]
            rm = _rstd(xm)
            xmh = xm * rm
            am_ref[...] = (xmh * g1v).astype(BF16)
            da_m = jnp.zeros((N_META, D_MODEL), F32)
            for j in range(N_CHIPS):
                dzj = dzm_ref[:, j * IN_SHARD:(j + 1) * IN_SHARD].astype(BF16)
                dzmb_ref[j] = dzj
                da_m = da_m + _dot_nt(dzj, win_v[j])
            dg1m_ref[...] = _rows8(da_m * xmh)
            dmeta_ref[...] = _rms_bwd(da_m, xmh, rm, g1v)

    row3 = lambda c: pl.BlockSpec((1, tm, c), lambda s, i: (s, n_t - 1 - i, 0))
    row2 = lambda c: pl.BlockSpec((tm, c), lambda s, i: (s * n_t + n_t - 1 - i, 0))
    n_rows = n_seq * seq
    outs = pl.pallas_call(
        body, name="mixer_bwd", grid=(n_seq, n_t),
        out_shape=[jax.ShapeDtypeStruct((n_seq, seq, D_MODEL), F32),
                   jax.ShapeDtypeStruct((N_CHIPS, n_rows, IN_SHARD), BF16), jax.ShapeDtypeStruct((n_rows, D_MODEL), BF16),
                   jax.ShapeDtypeStruct((8, D_MODEL), F32), jax.ShapeDtypeStruct((8, D_MODEL), F32),
                   jax.ShapeDtypeStruct((8, D_POOL), F32), jax.ShapeDtypeStruct((24, D_CONV), F32),
                   jax.ShapeDtypeStruct((N_POOL_GROUPS, POOL_GROUP, POOL_GROUP), F32),
                   jax.ShapeDtypeStruct((N_META, D_IN_PROJ), F32),
                   jax.ShapeDtypeStruct((N_META, D_MODEL), F32), jax.ShapeDtypeStruct((8, D_MODEL), F32),
                   jax.ShapeDtypeStruct((N_META, D_MODEL), BF16),
                   jax.ShapeDtypeStruct((N_CHIPS, N_META, IN_SHARD), BF16)],
        in_specs=[row3(D_MODEL), row3(D_MODEL), row3(D_Z), row2(D_CONV), row2(D_POOL), row3(D_MODEL),
                  _full((N_META, D_IN_PROJ)), _full((N_META, D_MODEL)), _full((1, D_MODEL)), _full((1, D_MODEL)),
                  _full((3, D_CONV)), _full((N_POOL_GROUPS, POOL_GROUP, POOL_GROUP)), _full((1, D_POOL)),
                  _full((8, 128))] + [ANY] * 4,
        out_specs=[row3(D_MODEL), pl.BlockSpec((N_CHIPS, tm, IN_SHARD), lambda s, i: (0, s * n_t + n_t - 1 - i, 0)),
                   row2(D_MODEL),
                   _full((8, D_MODEL)), _full((8, D_MODEL)), _full((8, D_POOL)), _full((24, D_CONV)),
                   _full((N_POOL_GROUPS, POOL_GROUP, POOL_GROUP)), _full((N_META, D_IN_PROJ)),
                   _full((N_META, D_MODEL)), _full((8, D_MODEL)), _full((N_META, D_MODEL)),
                   _full((N_CHIPS, N_META, IN_SHARD))],
        scratch_shapes=[pltpu.VMEM((N_CHIPS, D_MODEL, IN_SHARD), BF16), pltpu.VMEM((D_MODEL, D_MODEL), BF16),
                        pltpu.VMEM((tm + HALO, D_CONV), F32), pltpu.VMEM((tm + HALO, D_POOL), F32),
                        pltpu.VMEM((2 * HALO, D_CONV), F32), pltpu.VMEM((2 * HALO, D_POOL), F32),
                        pltpu.SemaphoreType.DMA((2 * N_CHIPS,))],
        compiler_params=_cparams(2),
    )(dh1, m3, z3, conv2, pooled2, x3, zmeta, meta_full, g1, g2, convw, poolw, pscale, after, *gathered, *shards)
    return outs


def _mixer_weight_grads(a, dz, ycat, dm, a_meta, dz_meta, ffn_sums, small):
    n_rows = a.shape[0]
    tk = min(TK_DW, n_rows)
    n_k = n_rows // tk
    n_sc, n_sm = len(ffn_sums), _AllReduceSmall.N_IN

    def body(a_ref, dz_ref, yc_ref, dm_ref, am_ref, dzm_ref, *rest):
        ins, outs, scratch = rest[:n_sc + n_sm], rest[n_sc + n_sm:2 * n_sc + n_sm + 5], rest[2 * n_sc + n_sm + 5:]
        dwin_ref, dwout_ref = outs[:2]
        scatter = _ScatterToChips(ins[:n_sc], outs[2:2 + n_sc], *scratch[:2])
        reduce_small = _AllReduceSmall(ins[n_sc:], outs[2 + n_sc:], scratch[2:])
        k = pl.program_id(0)

        @pl.when(k == 0)
        def _():
            scatter.start()
            reduce_small.pack_and_send()
            am_t = am_ref[...].T
            for j in range(N_CHIPS):
                dwin_ref[j] = _dot(am_t, dzm_ref[j])
            dwout_ref[...] = jnp.zeros_like(dwout_ref)

        for st in range(2):
            @pl.when(k == ((st + 1) * n_k) // 3)
            def _():
                reduce_small.combine(st)

        a_t = a_ref[...].T
        for j in range(N_CHIPS):
            dwin_ref[j] += _dot(a_t, dz_ref[j])
        dwout_ref[...] += _dot_tn(yc_ref[...], dm_ref[...])

        @pl.when(k == n_k - 1)
        def _():
            reduce_small.combine(2)
            scatter.finish()

    row = pl.BlockSpec((tk, D_MODEL), lambda k: (k, 0))
    outs = pl.pallas_call(
        body, name="mixer_weight_grads", grid=(n_k,),
        out_shape=[jax.ShapeDtypeStruct((N_CHIPS, D_MODEL, IN_SHARD), F32),
                   jax.ShapeDtypeStruct((D_MODEL, D_MODEL), F32)] + _ScatterToChips.out_shape(ffn_sums)
        + _AllReduceSmall.out_shape(),
        in_specs=[row, pl.BlockSpec((N_CHIPS, tk, IN_SHARD), lambda k: (0, k, 0)), row, row,
                  _full((N_META, D_MODEL)), _full((N_CHIPS, N_META, IN_SHARD))] + [ANY] * n_sc
        + [_full(s.shape) for s in small],
        out_specs=[_full((N_CHIPS, D_MODEL, IN_SHARD)), _full((D_MODEL, D_MODEL))] + [ANY] * n_sc
        + [_full(s) for s in _AllReduceSmall.SHAPES],
        scratch_shapes=_ScatterToChips.scratch(n_sc) + _AllReduceSmall.scratch(),
        compiler_params=_cparams(1),
    )(a, dz, ycat, dm, a_meta, dz_meta, *ffn_sums, *small)
    return ([outs[0], outs[1].reshape(N_CHIPS, OUT_SHARD, D_MODEL)], outs[2:2 + n_sc], outs[2 + n_sc:])


def kernel(x, meta_tokens, norm_mix_pre, w_in, conv_w, pool_w, pool_scale, w_out, norm_mix_post, norm_ffn_pre, w_gate, w_up, w_down, norm_ffn_post, loss_target, m_meta_tokens, m_norm_mix_pre, m_w_in, m_conv_w, m_pool_w, m_pool_scale, m_w_out, m_norm_mix_post, m_norm_ffn_pre, m_w_gate, m_w_up, m_w_down, m_norm_ffn_post, v_meta_tokens, v_norm_mix_pre, v_w_in, v_conv_w, v_pool_w, v_pool_scale, v_w_out, v_norm_mix_post, v_norm_ffn_pre, v_w_gate, v_w_up, v_w_down, v_norm_ffn_post):
    n_seq, seq, _ = x.shape
    n_rows = n_seq * seq
    chip = 2 * lax.axis_index("x") + lax.axis_index("y")
    meta_cols = D_MODEL // N_CHIPS
    conv_cols = D_CONV // N_CHIPS

    small = jnp.zeros((2 * HALO, meta_cols), F32)
    small = small.at[0:N_META, :].set(meta_tokens).at[N_META:N_META + 3, 0:conv_cols].set(conv_w[0])
    poolw_bf = pool_w[0].astype(BF16)
    pscale = pool_scale
    g1, g2, g3, g4 = norm_mix_pre, norm_mix_post, norm_ffn_pre, norm_ffn_post
    place = jnp.stack([chip, lax.axis_index("c")]).astype(jnp.int32)

    mix_shards = [w_in[0].astype(BF16), w_out[0].astype(BF16)]
    ((z3, m3, h1, a_bf, conv2, pooled2, yc_bf, zmeta, meta_full, conv_full), (win_all, wout_all, _, *ffn_gathered),
     ffn_shards) = _mixer_fwd(x, g1, g2, poolw_bf, pscale, mix_shards + [small], [w_gate[0].T, w_up[0].T, w_down[0]])
    dh1, f_bf, dd_bf, ds_bf, du_bf, gg_bf, lossp, dg3p, dg4p = _ffn_fwd_bwd(
        h1.reshape(n_rows, D_MODEL), loss_target.reshape(n_rows, D_MODEL), g3, g4, ffn_gathered, ffn_shards)
    as_shards = lambda g: g.reshape(N_CHIPS, FF_SHARD, D_MODEL)
    (dwg_t, dwu_t), _ = _ffn_weight_grads("ffn_weight_grads_gate_up", [ds_bf, du_bf], f_bf, [])
    dwg_t, dwu_t = as_shards(dwg_t), as_shards(dwu_t)
    (dwd,), (dwg_recv, dwu_recv) = _ffn_weight_grads("ffn_weight_grads_down", [gg_bf], dd_bf, [dwg_t, dwu_t])
    dwd = as_shards(dwd)
    behind_bwd = _SplitComm("grad_comm_behind_mixer_bwd", [dwd],
                            _add_pairs_multi([dwg_t, dwu_t], [dwg_recv, dwu_recv], place))
    (grad_x, dz_bf, dm_bf, dg1p, dg2p, dscp, dcwp, dpw, _, dmeta, dg1m, a_meta, dz_meta) = _mixer_bwd(
        dh1.reshape(n_seq, seq, D_MODEL), m3, z3, conv2, pooled2, x, zmeta, meta_full, g1, g2, conv_full, poolw_bf,
        pscale, [win_all, wout_all], mix_shards, behind_bwd.start())
    (dwd_recv,), (dwg_rbuf, dwu_rbuf) = behind_bwd.wait(dg2p)
    (dwd,) = behind_bwd.exchanged
    mix_grads, (dwd_rbuf,), (a_red, b_red, c_red) = _mixer_weight_grads(
        a_bf, dz_bf, yc_bf, dm_bf, a_meta, dz_meta, [_add_pairs(dwd, dwd_recv, place)],
        [dg1p, dg1m, dg2p, dg3p, dg4p, lossp, dmeta, dscp, dcwp, dpw.reshape(SMALL_C_ROWS, POOL_GROUP)])

    behind_sums = _SplitComm("grad_comm_behind_ffn_sums", mix_grads, [])
    ffn_red = _add_chips([dwg_t, dwu_t, dwd], [dwg_recv, dwu_recv, dwd_recv], [dwg_rbuf, dwu_rbuf, dwd_rbuf],
                         place, after=behind_sums.start(), name="grad_add_chips_ffn")
    mix_recvs, _ = behind_sums.wait(ffn_red[0])
    mix_grads = behind_sums.exchanged
    behind_tail = _SplitComm("grad_comm_behind_ffn_tail", [], _add_pairs_multi(mix_grads, mix_recvs, place))
    as_full = lambda r: r.reshape(2 * r.shape[1], r.shape[2])
    g_wg_t, g_wu_t, g_wd = [as_full(r) for r in _gather_halves(list(ffn_red), "grad_gather_halves_ffn",
                                                                 after=behind_tail.start())]
    ffn_out = _adamw_big([(w_gate[0].T, g_wg_t, m_w_gate[0].T, v_w_gate[0].T),
                          (w_up[0].T, g_wu_t, m_w_up[0].T, v_w_up[0].T), (w_down[0], g_wd, m_w_down[0], v_w_down[0])])

    as_c = lambda p: p.reshape(SMALL_C_ROWS, POOL_GROUP)
    loss, small_out = _adamw_small(place, [a_red, b_red, c_red], [
        (meta_tokens, m_meta_tokens, v_meta_tokens),
        (g1, m_norm_mix_pre, v_norm_mix_pre),
        (conv_w[0], m_conv_w[0], v_conv_w[0]),
        (as_c(pool_w), as_c(m_pool_w), as_c(v_pool_w)),
        (pool_scale, m_pool_scale, v_pool_scale),
        (g2, m_norm_mix_post, v_norm_mix_post),
        (g3, m_norm_ffn_pre, v_norm_ffn_pre),
        (g4, m_norm_ffn_post, v_norm_ffn_post),
    ])
    _, mix_rbufs = behind_tail.wait(ffn_out[2][0], small_out[0][0])
    mix_red = _add_chips(mix_grads, mix_recvs, mix_rbufs, place)
    g_win, g_wout = [as_full(r) for r in _gather_halves(list(mix_red), "grad_gather_halves_mixer")]
    big_out = (_adamw_big([(w_in[0], g_win, m_w_in[0], v_w_in[0])])
               + _adamw_big([(w_out[0], g_wout, m_w_out[0], v_w_out[0])]) + ffn_out)
    big_out[2] = [o.T for o in big_out[2]]
    big_out[3] = [o.T for o in big_out[3]]

    s_meta, s_g1, s_conv, s_poolw, s_pscale, s_g2, s_g3, s_g4 = small_out
    b_win, b_wout, b_wg, b_wu, b_wd = big_out

    def leaf(k):
        return [s_meta[k], s_g1[k], b_win[k][None], s_conv[k][None], s_poolw[k].reshape(pool_w.shape), s_pscale[k],
                b_wout[k][None], s_g2[k], s_g3[k], b_wg[k][None], b_wu[k][None], b_wd[k][None], s_g4[k]]

    return (loss.reshape(()), grad_x, *leaf(0), *leaf(1), *leaf(2), *leaf(3))
```

```python
import functools

import jax
import jax.numpy as jnp
from jax import lax
from jax.experimental import pallas as pl
from jax.experimental.pallas import tpu as pltpu

F32 = jnp.float32
BF16 = jnp.bfloat16
MESH = pl.DeviceIdType.MESH

D_MODEL = 1024
D_CONV = 512
D_POOL = 512
POOL_GROUP = 128
N_POOL_GROUPS = 4
D_IN_PROJ = 2048
D_FF = 2816
N_CHIPS = 4
FF_SHARD = D_FF // N_CHIPS
IN_SHARD = D_IN_PROJ // N_CHIPS
OUT_SHARD = D_MODEL // N_CHIPS
D_Z = 3 * IN_SHARD
N_META = 16
HALO = 16
RMS_EPS = 1e-6

ADAM_LR = 0.001
ADAM_B1 = 0.9
ADAM_B2 = 0.999
ADAM_EPS = 1e-08
ADAM_WD = 0.01
ADAM_STEP = 10

TM_MIX_FWD = 512
TM_MIX_BWD = 512
SUB_MIX_BWD = 512
TM_FFN = 256
TK_DW = 1024
FF_CHUNK = 1024
VMEM_LIMIT = 56 * 1024 * 1024


def _cparams(n_grid):
    return pltpu.CompilerParams(dimension_semantics=("arbitrary",) * n_grid, vmem_limit_bytes=VMEM_LIMIT)


def _dot(a, b):
    return jnp.dot(a, b, preferred_element_type=F32)


def _dot_nt(a, b):
    return lax.dot_general(a, b, (((1,), (1,)), ((), ())), preferred_element_type=F32)


def _dot_tn(a, b):
    return lax.dot_general(a, b, (((0,), (0,)), ((), ())), preferred_element_type=F32)


def _rows8(v):
    r, c = v.shape
    return v.reshape(r // 8, 8, c).sum(axis=0)


def _rstd(v):
    return lax.rsqrt(jnp.mean(v * v, axis=-1, keepdims=True) + RMS_EPS)


def _rms_bwd(dy, xhat, rstd, gain):
    dyg = dy * gain
    return rstd * (dyg - xhat * jnp.mean(dyg * xhat, axis=-1, keepdims=True))


def _sigmoid(v):
    return 1.0 / (1.0 + jnp.exp(-v))


def _gcols(g):
    return slice(g * POOL_GROUP, (g + 1) * POOL_GROUP)


def _window_sum(e, g, ahead):
    n = e.shape[0]
    w = e
    for level in range(g + 1):
        shift = 1 << level
        w = w + pltpu.roll(w, (n - shift) if ahead else shift, 0)
    return w


def _pool_fwd(pb, g, n):
    e = pb[0:HALO + n, _gcols(g)]
    return _window_sum(e, g, False)[HALO:, :] * (1.0 / (2 << g)) - e[HALO:, :]


def _pool_bwd(qb, g, r0, n):
    e = qb[r0:r0 + n + HALO, _gcols(g)]
    return _window_sum(e, g, True)[0:n, :] * (1.0 / (2 << g)) - e[0:n, :]


def _full(shape):
    nd = len(shape)
    return pl.BlockSpec(shape, lambda *_: (0,) * nd)


ANY = pl.BlockSpec(memory_space=pl.ANY)


def _mesh_pos():
    x, y, c = lax.axis_index("x"), lax.axis_index("y"), lax.axis_index("c")
    chips = [(1 - x, y), (x, 1 - y), (1 - x, 1 - y)]
    return x, y, c, chips


def _half(ref, h):
    hr = ref.shape[0] // 2
    return ref.at[pl.ds(h * hr, hr), :]


class _AllGather:
    PER_ARRAY = 9

    def __init__(self, ins, outs, send_sems, recv_sems):
        self.ins, self.outs, self.send_sems, self.recv_sems = ins, outs, send_sems, recv_sems
        self.n = len(ins)

    @classmethod
    def scratch(cls, n):
        return [pltpu.SemaphoreType.DMA((cls.PER_ARRAY * n,)), pltpu.SemaphoreType.DMA((cls.PER_ARRAY * n,))]

    @staticmethod
    def out_shape(shards):
        return [jax.ShapeDtypeStruct((N_CHIPS,) + s.shape, s.dtype) for s in shards]

    def _copy(self, a, k, src, dst, to):
        i = self.PER_ARRAY * a + k
        return pltpu.make_async_remote_copy(src_ref=src, dst_ref=dst, send_sem=self.send_sems.at[i],
                                            recv_sem=self.recv_sems.at[i], device_id=to, device_id_type=MESH)

    def _piece(self, a, chip, piece, h=None):
        h = lax.axis_index("c") if h is None else h
        rows = self.ins[a].shape[0] // 4
        return self.outs[a].at[chip].at[pl.ds((2 * h + piece) * rows, rows), :]

    def _own(self, a, k):
        x, y, c, chips = _mesh_pos()
        piece = (1, 0, 0, 1)[k]
        rows = self.ins[a].shape[0] // 4
        src = self.ins[a].at[pl.ds((2 * c + piece) * rows, rows), :]
        return self._copy(a, k, src, self._piece(a, 2 * x + y, piece), (*chips[k // 2], c))

    def _relay(self, a, k):
        x, y, c, chips = _mesh_pos()
        source, to, piece = (chips[1], chips[0], 0) if k == 4 else (chips[0], chips[1], 1)
        rows = self._piece(a, 2 * source[0] + source[1], piece)
        return self._copy(a, k, rows, rows, (*to, c))

    def _sibling(self, a, k, h):
        x, y, c, chips = _mesh_pos()
        chip = chips[k - 6]
        slot = _half(self.outs[a].at[2 * chip[0] + chip[1]], h)
        return self._copy(a, k, slot, slot, (x, y, 1 - c))

    def start(self, arrays=None):
        for a in (range(self.n) if arrays is None else arrays):
            for k in range(4):
                self._own(a, k).start()

    def relay(self, a):
        self._own(a, 2).wait_recv()
        self._relay(a, 4).start()
        self._own(a, 0).wait_recv()
        self._relay(a, 5).start()

    def forward(self, a):
        c = lax.axis_index("c")
        self._own(a, 1).wait_recv()
        self._sibling(a, 6, c).start()
        self._own(a, 3).wait_recv()
        self._sibling(a, 7, c).start()
        self._relay(a, 4).wait_recv()
        self._relay(a, 5).wait_recv()
        self._sibling(a, 8, c).start()

    def finish(self, arrays=None):
        c = lax.axis_index("c")
        arrays = range(self.n) if arrays is None else arrays
        for a in arrays:
            for k in range(6, 9):
                self._sibling(a, k, 1 - c).wait_recv()
        for a in arrays:
            for k in range(4):
                self._own(a, k).wait_send()
            for k in range(4, 6):
                self._relay(a, k).wait_send()
            for k in range(6, 9):
                self._sibling(a, k, c).wait_send()


class _ExchangeHalves:
    def __init__(self, ins, recvs, send_sems, recv_sems):
        self.ins, self.recvs, self.send_sems, self.recv_sems = ins, recvs, send_sems, recv_sems

    @staticmethod
    def scratch(n):
        return [pltpu.SemaphoreType.DMA((n,)), pltpu.SemaphoreType.DMA((n,))]

    @staticmethod
    def out_shape(grads):
        return [jax.ShapeDtypeStruct((g.shape[0], g.shape[1] // 2, g.shape[2]), g.dtype) for g in grads]

    def _copies(self):
        x, y, c, _ = _mesh_pos()
        out = []
        for a, (src, dst) in enumerate(zip(self.ins, self.recvs)):
            hr = src.shape[1] // 2
            out.append(pltpu.make_async_remote_copy(
                src_ref=src.at[:, pl.ds((1 - c) * hr, hr), :], dst_ref=dst, send_sem=self.send_sems.at[a],
                recv_sem=self.recv_sems.at[a], device_id=(x, y, 1 - c), device_id_type=MESH))
        return out

    def start(self):
        for cp in self._copies():
            cp.start()

    def finish(self):
        for cp in self._copies():
            cp.wait()


class _ScatterToChips:
    def __init__(self, ins, rbufs, send_sems, recv_sems):
        self.ins, self.rbufs, self.send_sems, self.recv_sems = ins, rbufs, send_sems, recv_sems

    @staticmethod
    def scratch(n):
        return [pltpu.SemaphoreType.DMA((3 * n,)), pltpu.SemaphoreType.DMA((3 * n,))]

    @staticmethod
    def out_shape(sums):
        return [jax.ShapeDtypeStruct((3,) + s.shape[1:], BF16) for s in sums]

    def _copies(self):
        x, y, c, chips = _mesh_pos()
        out = []
        for a, (src, dst) in enumerate(zip(self.ins, self.rbufs)):
            for k, chip in enumerate(chips):
                out.append(pltpu.make_async_remote_copy(
                    src_ref=src.at[2 * chip[0] + chip[1]], dst_ref=dst.at[k], send_sem=self.send_sems.at[3 * a + k],
                    recv_sem=self.recv_sems.at[3 * a + k], device_id=(*chip, c), device_id_type=MESH))
        return out

    def start(self):
        for cp in self._copies():
            cp.start()

    def finish(self):
        for cp in self._copies():
            cp.wait()


HBM = pl.BlockSpec(memory_space=pltpu.HBM)
SEM = pl.BlockSpec(memory_space=pltpu.SEMAPHORE)


class _SplitComm:
    def __init__(self, name, exchanged, scattered):
        self.name, self.n_ex, self.n_sc = name, len(exchanged), len(scattered)
        self.n_copies = self.n_ex + 3 * self.n_sc
        zones = ([lax.empty((g.shape[0], g.shape[1] // 2, g.shape[2]), g.dtype) for g in exchanged]
                 + [lax.empty((3,) + s.shape[1:], s.dtype) for s in scattered])
        self.buffers = [pltpu.with_memory_space_constraint(v, pltpu.HBM)
                        for v in list(exchanged) + list(scattered) + zones]

    def _copies(self, bufs, send_sems, recv_sems):
        x, y, c, chips = _mesh_pos()
        n_src = self.n_ex + self.n_sc
        out = []
        for a in range(self.n_ex):
            hr = bufs[a].shape[1] // 2
            out.append(pltpu.make_async_remote_copy(
                src_ref=bufs[a].at[:, pl.ds((1 - c) * hr, hr), :], dst_ref=bufs[n_src + a], send_sem=send_sems[a],
                recv_sem=recv_sems[a], device_id=(x, y, 1 - c), device_id_type=MESH))
        for a in range(self.n_sc):
            for k, chip in enumerate(chips):
                i = self.n_ex + 3 * a + k
                out.append(pltpu.make_async_remote_copy(
                    src_ref=bufs[self.n_ex + a].at[2 * chip[0] + chip[1]], dst_ref=bufs[n_src + self.n_ex + a].at[k],
                    send_sem=send_sems[i], recv_sem=recv_sems[i], device_id=(*chip, c), device_id_type=MESH))
        return out

    def start(self):
        n_buf, n_cp = len(self.buffers), self.n_copies

        def body(*refs):
            bufs = refs[:n_buf]
            send_sems, recv_sems = refs[n_buf:n_buf + n_cp], refs[n_buf + n_cp:n_buf + 2 * n_cp]
            for cp in self._copies(bufs, send_sems, recv_sems):
                cp.start()
            refs[-1][...] = jnp.zeros_like(refs[-1])

        outs = pl.pallas_call(
            body, name=self.name + "_start",
            out_shape=[pltpu.SemaphoreType.DMA(())] * (2 * n_cp) + [pltpu.HBM(b.shape, b.dtype) for b in self.buffers]
            + [jax.ShapeDtypeStruct((8, 128), F32)],
            in_specs=[HBM] * n_buf, out_specs=[SEM] * (2 * n_cp) + [HBM] * n_buf + [pl.BlockSpec(memory_space=pltpu.VMEM)],
            input_output_aliases={i: 2 * n_cp + i for i in range(n_buf)},
            compiler_params=pltpu.CompilerParams(has_side_effects=pltpu.SideEffectType.DATAFLOW_SIDE_EFFECTING),
        )(*self.buffers)
        self.sems, self.buffers = outs[:2 * n_cp], outs[2 * n_cp:2 * n_cp + n_buf]
        return outs[-1]

    def wait(self, *after):
        n_buf, n_cp = len(self.buffers), self.n_copies

        def body(*refs):
            bufs = refs[:n_buf]
            send_sems, recv_sems = refs[n_buf:n_buf + n_cp], refs[n_buf + n_cp:n_buf + 2 * n_cp]
            for cp in self._copies(bufs, send_sems, recv_sems):
                cp.wait_send()
                cp.wait_recv()

        outs = pl.pallas_call(
            body, name=self.name + "_wait", out_shape=[pltpu.HBM(b.shape, b.dtype) for b in self.buffers],
            in_specs=[HBM] * n_buf + [SEM] * (2 * n_cp) + [ANY] * len(after), out_specs=[HBM] * n_buf,
            input_output_aliases={i: i for i in range(n_buf)},
            compiler_params=pltpu.CompilerParams(has_side_effects=pltpu.SideEffectType.DATAFLOW_SIDE_EFFECTING),
        )(*self.buffers, *self.sems, *after)
        self.exchanged = outs[:self.n_ex]
        zones = outs[self.n_ex + self.n_sc:]
        return zones[:self.n_ex], zones[self.n_ex:]


def _gather_halves(halves, name, after=None):
    n = len(halves)
    extra = [] if after is None else [after]

    def body(*refs):
        ins, outs = refs[:n], refs[n + len(extra):2 * n + len(extra)]
        send_sems, recv_sems = refs[2 * n + len(extra):]
        x, y, c, _ = _mesh_pos()
        sib = (x, y, 1 - c)
        remote = [pltpu.make_async_remote_copy(src_ref=ins[a].at[c], dst_ref=outs[a].at[c],
                                               send_sem=send_sems.at[a], recv_sem=recv_sems.at[a],
                                               device_id=sib, device_id_type=MESH) for a in range(n)]
        for cp in remote:
            cp.start()
        for a in range(n):
            pltpu.make_async_remote_copy(src_ref=ins[a].at[1 - c], dst_ref=outs[a].at[1 - c], send_sem=send_sems.at[a],
                                         recv_sem=recv_sems.at[a], device_id=sib, device_id_type=MESH).wait_recv()
        for cp in remote:
            cp.wait_send()

    return pl.pallas_call(
        body, name=name,
        out_shape=[jax.ShapeDtypeStruct(h.shape, F32) for h in halves],
        in_specs=[ANY] * (n + len(extra)), out_specs=[ANY] * n, input_output_aliases={a: a for a in range(n)},
        scratch_shapes=[pltpu.SemaphoreType.DMA((n,)), pltpu.SemaphoreType.DMA((n,))],
    )(*halves, *extra)


SMALL_A_ROWS = 24
SMALL_B_ROWS = 8
SMALL_C_ROWS = N_POOL_GROUPS * POOL_GROUP


class _AllReduceSmall:
    N_IN = 10
    SHAPES = [(SMALL_A_ROWS, D_MODEL), (SMALL_B_ROWS, D_CONV), (SMALL_C_ROWS, POOL_GROUP)]

    def __init__(self, ins, outs, scratch):
        self.ins, self.outs = ins, outs
        self.bufs, self.rcvs, self.send_sems, self.recv_sems = scratch[:3], scratch[3:6], scratch[6], scratch[7]

    @classmethod
    def scratch(cls):
        return ([pltpu.VMEM((3,) + s, F32) for s in cls.SHAPES] + [pltpu.VMEM((3,) + s, F32) for s in cls.SHAPES]
                + [pltpu.SemaphoreType.DMA((9,)), pltpu.SemaphoreType.DMA((9,))])

    @classmethod
    def out_shape(cls):
        return [jax.ShapeDtypeStruct(s, F32) for s in cls.SHAPES]

    def _copies(self, st):
        x, y, c, _ = _mesh_pos()
        peer = [(x, y, 1 - c), (1 - x, y, c), (x, 1 - y, c)][st]
        return [pltpu.make_async_remote_copy(
            src_ref=buf.at[st], dst_ref=rcv.at[st], send_sem=self.send_sems.at[3 * st + i],
            recv_sem=self.recv_sems.at[3 * st + i], device_id=peer, device_id_type=MESH)
            for i, (buf, rcv) in enumerate(zip(self.bufs, self.rcvs))]

    def pack_and_send(self):
        dg1_ref, dg1m_ref, dg2_ref, dg3_ref, dg4_ref, loss_ref, dmeta_ref, dsc_ref, dcw_ref, dpw_ref = self.ins
        a_buf, b_buf, c_buf = self.bufs

        def rowsum(v):
            return jnp.sum(v, axis=0, keepdims=True)

        a_buf[0, 0:1, :] = rowsum(dg1_ref[...] + dg1m_ref[...])
        a_buf[0, 1:2, :] = rowsum(dg2_ref[...])
        a_buf[0, 2:3, :] = rowsum(dg3_ref[...])
        a_buf[0, 3:4, :] = rowsum(dg4_ref[...])
        loss = jnp.sum(rowsum(loss_ref[...]), axis=1, keepdims=True) * (0.5 / D_MODEL)
        a_buf[0, 4:5, :] = jnp.broadcast_to(loss, (1, D_MODEL))
        a_buf[0, 5:8, :] = jnp.zeros((3, D_MODEL), F32)
        a_buf[0, 8:24, :] = dmeta_ref[...]
        b_buf[0, 0:1, :] = rowsum(dsc_ref[...])
        for k in range(3):
            b_buf[0, 1 + k:2 + k, :] = rowsum(dcw_ref[8 * k:8 * k + 8, :])
        b_buf[0, 4:8, :] = jnp.zeros((4, D_CONV), F32)
        c_buf[0] = dpw_ref[...]
        for cp in self._copies(0):
            cp.start()

    def combine(self, st):
        for cp in self._copies(st):
            cp.wait()
        if st < 2:
            for buf, rcv in zip(self.bufs, self.rcvs):
                buf[st + 1] = buf[st] + rcv[st]
            for cp in self._copies(st + 1):
                cp.start()
        else:
            for out, buf, rcv in zip(self.outs, self.bufs, self.rcvs):
                out[...] = buf[st] + rcv[st]


def _row_block(rows):
    for cand in (512, 448, 384, 352, 320, 256, 128, 64, 32, 16):
        if rows % cand == 0:
            return cand
    return rows


def _add_pairs_multi(grads, recvs, place):
    n = len(grads)
    n_sh = grads[0].shape[0]
    halves = [g.shape[1] // 2 for g in grads]
    n_steps = halves[0] // _row_block(halves[0])
    blocks = [(hr // n_steps, g.shape[2]) for hr, g in zip(halves, grads)]

    def body(place_ref, *refs):
        for a_ref, b_ref, o_ref in zip(refs[:n], refs[n:2 * n], refs[2 * n:]):
            o_ref[...] = (a_ref[0] + b_ref[...]).astype(BF16)

    return pl.pallas_call(
        body, name="grad_add_pairs",
        grid_spec=pltpu.PrefetchScalarGridSpec(
            num_scalar_prefetch=1, grid=(n_sh, n_steps),
            in_specs=[pl.BlockSpec((1, 1, br, cols), lambda j, i, p: (j, p[1], i, 0)) for br, cols in blocks]
            + [pl.BlockSpec((1, br, cols), lambda j, i, p: (j, i, 0)) for br, cols in blocks],
            out_specs=[pl.BlockSpec((1, br, cols), lambda j, i, p: (j, i, 0)) for br, cols in blocks]),
        out_shape=[jax.ShapeDtypeStruct((n_sh, hr, g.shape[2]), BF16) for hr, g in zip(halves, grads)],
        compiler_params=_cparams(2),
    )(place, *[g.reshape(n_sh, 2, hr, g.shape[2]) for hr, g in zip(halves, grads)], *recvs)


def _add_pairs(grad, recv, place):
    return _add_pairs_multi([grad], [recv], place)[0]


def _add_chips(grads, recvs, rbufs, place, after=None, name="grad_add_chips"):
    n = len(grads)
    n_sh = grads[0].shape[0]
    halves = [g.shape[1] // 2 for g in grads]
    n_steps = halves[0] // _row_block(halves[0])
    blocks = [(hr // n_steps, g.shape[2]) for hr, g in zip(halves, grads)]
    extra = [] if after is None else [after]

    def body(place_ref, *refs):
        for a_ref, b_ref, r_ref, o_ref in zip(refs[:n], refs[n:2 * n], refs[2 * n:3 * n], refs[3 * n + len(extra):]):
            own = a_ref[0, 0] + b_ref[0]
            o_ref[0] = ((own + r_ref[0].astype(F32)) + r_ref[1].astype(F32)) + r_ref[2].astype(F32)

    return pl.pallas_call(
        body, name=name,
        grid_spec=pltpu.PrefetchScalarGridSpec(
            num_scalar_prefetch=1, grid=(n_steps,),
            in_specs=[pl.BlockSpec((1, 1, br, cols), lambda i, p: (p[0], p[1], i, 0)) for br, cols in blocks]
            + [pl.BlockSpec((1, br, cols), lambda i, p: (p[0], i, 0)) for br, cols in blocks]
            + [pl.BlockSpec((3, br, cols), lambda i, p: (0, i, 0)) for br, cols in blocks]
            + [pl.BlockSpec((8, 128), lambda i, p: (0, 0))] * len(extra),
            out_specs=[pl.BlockSpec((1, br, cols), lambda i, p: (p[1], i, 0)) for br, cols in blocks]),
        out_shape=[jax.ShapeDtypeStruct((2, hr, g.shape[2]), F32) for hr, g in zip(halves, grads)],
        compiler_params=_cparams(1),
    )(place, *[g.reshape(n_sh, 2, hr, g.shape[2]) for hr, g in zip(halves, grads)], *recvs, *rbufs, *extra)


def _adamw_math(w, g, m, v):
    m2 = ADAM_B1 * m + (1.0 - ADAM_B1) * g
    v2 = ADAM_B2 * v + (1.0 - ADAM_B2) * (g * g)
    m_hat = m2 / (1.0 - ADAM_B1 ** ADAM_STEP)
    v_hat = v2 / (1.0 - ADAM_B2 ** ADAM_STEP)
    delta = -ADAM_LR * (m_hat / (jnp.sqrt(v_hat) + ADAM_EPS) + ADAM_WD * w)
    return delta, m2, v2


def _adamw_big(groups):
    n = len(groups)
    rows, cols = groups[0][0].shape
    br = _row_block(rows)
    if n > 1 and br % 16 == 0:
        br //= 2

    def body(*refs):
        for i in range(n):
            w_ref, g_ref, m_ref, v_ref = refs[4 * i:4 * i + 4]
            g_out_ref, d_ref, m2_ref, v2_ref = refs[4 * n + 4 * i:4 * n + 4 * i + 4]
            g = g_ref[...]
            d, m2, v2 = _adamw_math(w_ref[...], g, m_ref[...], v_ref[...])
            g_out_ref[...] = g
            d_ref[...] = d
            m2_ref[...] = m2
            v2_ref[...] = v2

    spec = pl.BlockSpec((br, cols), lambda i: (i, 0))
    outs = pl.pallas_call(
        body, name="adamw_big", grid=(rows // br,),
        out_shape=[jax.ShapeDtypeStruct((rows, cols), F32)] * (4 * n),
        in_specs=[spec] * (4 * n), out_specs=[spec] * (4 * n), compiler_params=_cparams(1),
    )(*[a for grp in groups for a in grp])
    return [list(outs[4 * i:4 * i + 4]) for i in range(n)]


def _adamw_small(place, reduced, params):
    n = len(params)
    meta_cols, conv_cols = D_MODEL // N_CHIPS, D_CONV // N_CHIPS

    def body(place_ref, a_ref, b_ref, c_ref, *refs):
        ins, loss_ref, outs = refs[:3 * n], refs[3 * n], refs[3 * n + 1:]
        chip = place_ref[0]

        def own_cols(ref, r0, n_r, width):
            out = ref[r0:r0 + n_r, 0:width]
            for j in range(1, N_CHIPS):
                out = jnp.where(chip == j, ref[r0:r0 + n_r, j * width:(j + 1) * width], out)
            return out

        grads = [own_cols(a_ref, 8, N_META, meta_cols), a_ref[0:1, :], own_cols(b_ref, 1, 3, conv_cols), c_ref[...],
                 b_ref[0:1, :], a_ref[1:2, :], a_ref[2:3, :], a_ref[3:4, :]]
        loss_ref[...] = a_ref[4:5, 0:1]
        for i, g in enumerate(grads):
            w, m, v = (r[...] for r in ins[3 * i:3 * i + 3])
            for o, val in zip(outs[4 * i:4 * i + 4], (g,) + _adamw_math(w, g, m, v)):
                o[...] = val

    vm = pl.BlockSpec(memory_space=pltpu.VMEM)
    flat = [a for grp in params for a in grp]
    out_shape = ([jax.ShapeDtypeStruct((1, 1), F32)]
                 + [jax.ShapeDtypeStruct(grp[0].shape, F32) for grp in params for _ in range(4)])
    outs = pl.pallas_call(body, name="adamw_small", out_shape=out_shape,
                          in_specs=[pl.BlockSpec(memory_space=pltpu.SMEM)] + [vm] * (3 + 3 * n),
                          out_specs=[vm] * (1 + 4 * n))(place, *reduced, *flat)
    return outs[0], [tuple(outs[1 + 4 * i:5 + 4 * i]) for i in range(n)]


def _load_gathered(gathered, shards, dst_slots, sems):
    n = len(gathered)
    me = 2 * lax.axis_index("x") + lax.axis_index("y")

    def copies(j, own):
        return [pltpu.make_async_copy(shards[a] if own else gathered[a].at[j], dst_slots[a](j), sems.at[n * j + a])
                for a in range(n)]

    for wait in (False, True):
        for j in range(N_CHIPS):
            for own in (False, True):
                @pl.when((me == j) == own)
                def _():
                    for cp in copies(j, own):
                        cp.wait() if wait else cp.start(priority=1)


N_MIX_SHARDS = 3


def _mixer_fwd(x3, g1, g2, poolw, pscale, shards, ffn_f32):
    n_seq, seq, _ = x3.shape
    tm = min(TM_MIX_FWD, seq)
    n_t = seq // tm
    n_steps = n_seq * n_t
    n_ffn = len(ffn_f32)
    n_ag = N_MIX_SHARDS + n_ffn
    ffn_bf16 = [jax.ShapeDtypeStruct(w.shape, BF16) for w in ffn_f32]
    small_rows = shards[2].shape[0]
    conv_cols = D_CONV // N_CHIPS

    def body(x_ref, g1_ref, g2_ref, pw_ref, ps_ref, *rest):
        ffn_f32_refs, ffn_bf_refs = rest[N_MIX_SHARDS:n_ag], rest[2 * n_ag + 10:2 * n_ag + 10 + n_ffn]
        ag = _AllGather(list(rest[:N_MIX_SHARDS]) + list(ffn_bf_refs), rest[n_ag + 10:2 * n_ag + 10], *rest[-2:])
        (z_ref, m_ref, h1_ref, a_ref, conv_ref, pooled_ref, yc_ref, zm_ref, meta_ref,
         cw_ref) = rest[n_ag:n_ag + 10]
        win_v, wout_v, small_v, cvb, pb, load_sems, stage_v, cast_v, cast_sems = rest[2 * n_ag + 10 + n_ffn:-2]
        s, t = pl.program_id(0), pl.program_id(1)
        step = s * n_t + t

        def ffn_load(i):
            return pltpu.make_async_copy(ffn_f32_refs[i], stage_v.at[i], cast_sems.at[i])

        def ffn_store(i):
            return pltpu.make_async_copy(cast_v.at[i], ffn_bf_refs[i], cast_sems.at[n_ffn + i])

        def round_ffn_piece(i):
            ffn_load(i).wait()
            cast_v[i] = stage_v[i].astype(BF16)
            ffn_store(i).start()

        @pl.when(step == 0)
        def _():
            ag.start(range(N_MIX_SHARDS))
            for i in range(n_ffn):
                ffn_load(i).start(priority=1)
            round_ffn_piece(0)
            round_ffn_piece(1)
            for a in range(N_MIX_SHARDS):
                ag.relay(a)
            round_ffn_piece(2)
            for i in range(n_ffn):
                ffn_store(i).wait()
            ag.start(range(N_MIX_SHARDS, n_ag))
            for a in range(N_MIX_SHARDS):
                ag.forward(a)
            ag.finish(range(N_MIX_SHARDS))
            _load_gathered(ag.outs[:N_MIX_SHARDS], ag.ins[:N_MIX_SHARDS],
                           [lambda j: win_v.at[j], lambda j: wout_v.at[pl.ds(j * OUT_SHARD, OUT_SHARD), :],
                            lambda j: small_v.at[j]], load_sems)

            meta = jnp.concatenate([small_v[j, 0:N_META, :] for j in range(N_CHIPS)], axis=1)
            meta_ref[...] = meta
            cw_ref[...] = jnp.concatenate([small_v[j, N_META:N_META + 3, 0:conv_cols] for j in range(N_CHIPS)], axis=1)
            a_meta = (meta * _rstd(meta) * g1_ref[...]).astype(BF16)
            for j in range(N_CHIPS):
                zm_ref[:, j * IN_SHARD:(j + 1) * IN_SHARD] = _dot(a_meta, win_v[j])

        for i in range(n_ffn):
            @pl.when(step == (3 * (i + 1) * n_steps) // (4 * n_ffn + 4))
            def _():
                ag.relay(N_MIX_SHARDS + i)

        for i in range(n_ffn):
            @pl.when(step == min(n_steps // 2 + ((i + 1) * n_steps) // (2 * n_ffn + 2), n_steps - 1))
            def _():
                ag.forward(N_MIX_SHARDS + i)

        @pl.when(t == 0)
        def _():
            cvb[0:HALO, :] = zm_ref[:, IN_SHARD:2 * IN_SHARD] * zm_ref[:, 2 * IN_SHARD:3 * IN_SHARD]
            pb[0:HALO, :] = zm_ref[:, 3 * IN_SHARD:4 * IN_SHARD]

        @pl.when(t > 0)
        def _():
            cvb[0:HALO, :] = cvb[tm:tm + HALO, :]
            pb[0:HALO, :] = pb[tm:tm + HALO, :]

        xt = x_ref[0]
        a = (xt * _rstd(xt) * g1_ref[...]).astype(BF16)
        a_ref[...] = a
        zb = _dot(a, win_v[0])
        zc = _dot(a, win_v[1])
        zv = _dot(a, win_v[2])
        zp = _dot(a, win_v[3])
        z_ref[0, :, 0:IN_SHARD] = zb
        z_ref[0, :, IN_SHARD:2 * IN_SHARD] = zc
        z_ref[0, :, 2 * IN_SHARD:3 * IN_SHARD] = zv
        cv = zc * zv
        cvb[HALO:HALO + tm, :] = cv
        pb[HALO:HALO + tm, :] = zp
        cw = cw_ref[...]
        conv = cw[0:1] * cvb[HALO - 2:HALO - 2 + tm, :] + cw[1:2] * cvb[HALO - 1:HALO - 1 + tm, :] + cw[2:3] * cv
        conv_ref[...] = conv
        parts = [(zb * conv).astype(BF16)]
        for g in range(N_POOL_GROUPS):
            pooled = _pool_fwd(pb, g, tm).astype(BF16)
            pooled_ref[:, _gcols(g)] = pooled
            parts.append((_dot(pooled, pw_ref[g]) * ps_ref[:, _gcols(g)]).astype(BF16))
        ycat = jnp.concatenate(parts, axis=1)
        yc_ref[...] = ycat
        m = _dot(ycat, wout_v[...])
        m_ref[0] = m
        h1_ref[0] = xt + m * _rstd(m) * g2_ref[...]

        @pl.when(step == n_steps - 1)
        def _():
            ag.finish(range(N_MIX_SHARDS, n_ag))

    n_rows = n_seq * seq
    row = lambda c: pl.BlockSpec((1, tm, c), lambda s, t: (s, t, 0))
    row2 = lambda c: pl.BlockSpec((tm, c), lambda s, t: (s * n_t + t, 0))
    outs = pl.pallas_call(
        body, name="mixer_fwd", grid=(n_seq, n_t),
        out_shape=[jax.ShapeDtypeStruct((n_seq, seq, D_Z), F32), jax.ShapeDtypeStruct((n_seq, seq, D_MODEL), F32),
                   jax.ShapeDtypeStruct((n_seq, seq, D_MODEL), F32), jax.ShapeDtypeStruct((n_rows, D_MODEL), BF16),
                   jax.ShapeDtypeStruct((n_rows, D_CONV), F32), jax.ShapeDtypeStruct((n_rows, D_POOL), BF16),
                   jax.ShapeDtypeStruct((n_rows, D_MODEL), BF16), jax.ShapeDtypeStruct((N_META, D_IN_PROJ), F32),
                   jax.ShapeDtypeStruct((N_META, D_MODEL), F32), jax.ShapeDtypeStruct((3, D_CONV), F32)]
        + _AllGather.out_shape(list(shards) + ffn_bf16) + ffn_bf16,
        in_specs=[row(D_MODEL), _full((1, D_MODEL)), _full((1, D_MODEL)),
                  _full((N_POOL_GROUPS, POOL_GROUP, POOL_GROUP)), _full((1, D_POOL))] + [ANY] * n_ag,
        out_specs=[row(D_Z), row(D_MODEL), row(D_MODEL), row2(D_MODEL), row2(D_CONV), row2(D_POOL), row2(D_MODEL),
                   _full((N_META, D_IN_PROJ)), _full((N_META, D_MODEL)), _full((3, D_CONV))] + [ANY] * (n_ag + n_ffn),
        scratch_shapes=[pltpu.VMEM((N_CHIPS, D_MODEL, IN_SHARD), BF16), pltpu.VMEM((D_MODEL, D_MODEL), BF16),
                        pltpu.VMEM((N_CHIPS, small_rows, D_MODEL // N_CHIPS), F32),
                        pltpu.VMEM((HALO + tm, D_CONV), F32), pltpu.VMEM((HALO + tm, D_POOL), F32),
                        pltpu.SemaphoreType.DMA((N_MIX_SHARDS * N_CHIPS,)),
                        pltpu.VMEM((n_ffn,) + ffn_f32[0].shape, F32), pltpu.VMEM((n_ffn,) + ffn_f32[0].shape, BF16),
                        pltpu.SemaphoreType.DMA((2 * n_ffn,))] + _AllGather.scratch(n_ag),
        compiler_params=_cparams(2),
    )(x3, g1, g2, poolw, pscale, *shards, *ffn_f32)
    return outs[:10], outs[10:10 + n_ag], outs[10 + n_ag:]


def _ffn_chunks():
    out, r0 = [], 0
    while r0 < D_FF:
        out.append((r0, min(FF_CHUNK, D_FF - r0)))
        r0 += FF_CHUNK
    return out


def _ffn_fwd_bwd(h1, target, g3, g4, gathered, shards):
    n_rows = h1.shape[0]
    tm = min(TM_FFN, n_rows)
    chunks = _ffn_chunks()

    def body(h1_ref, t_ref, g3_ref, g4_ref, wg_all, wu_all, wd_all, wg_s, wu_s, wd_s,
             dh1_ref, f_ref, dd_ref, ds_ref, du_ref, gg_ref, loss_ref, dg3_ref, dg4_ref,
             wg_v, wu_v, wd_v, s_sc, u_sc, sems):
        @pl.when(pl.program_id(0) == 0)
        def _():
            _load_gathered([wg_all, wu_all, wd_all], [wg_s, wu_s, wd_s],
                           [functools.partial(lambda v, j: v.at[pl.ds(j * FF_SHARD, FF_SHARD), :], v)
                            for v in (wg_v, wu_v, wd_v)], sems)
            loss_ref[...] = jnp.zeros_like(loss_ref)
            dg3_ref[...] = jnp.zeros_like(dg3_ref)
            dg4_ref[...] = jnp.zeros_like(dg4_ref)

        h1v = h1_ref[...]
        r3 = _rstd(h1v)
        hh = h1v * r3
        g3v, g4v = g3_ref[...], g4_ref[...]
        f = (hh * g3v).astype(BF16)
        f_ref[...] = f
        d = jnp.zeros((tm, D_MODEL), F32)
        for r0, sz in chunks:
            s = _dot_nt(f, wg_v[r0:r0 + sz, :])
            u = _dot_nt(f, wu_v[r0:r0 + sz, :])
            s_sc[:, r0:r0 + sz] = s
            u_sc[:, r0:r0 + sz] = u
            gc = (s * _sigmoid(s) * u).astype(BF16)
            gg_ref[:, r0:r0 + sz] = gc
            d = d + _dot(gc, wd_v[r0:r0 + sz, :])
        r4 = _rstd(d)
        dh = d * r4
        err = (h1v + dh * g4v) - t_ref[...]
        loss_ref[...] += _rows8(err * err)
        dy = err * (1.0 / D_MODEL)
        dg4_ref[...] += _rows8(dy * dh)
        ddb = _rms_bwd(dy, dh, r4, g4v).astype(BF16)
        dd_ref[...] = ddb
        df = jnp.zeros((tm, D_MODEL), F32)
        for r0, sz in chunks:
            dgg = _dot_nt(ddb, wd_v[r0:r0 + sz, :])
            s = s_sc[:, r0:r0 + sz]
            u = u_sc[:, r0:r0 + sz]
            sig = _sigmoid(s)
            dsc = (dgg * u * (sig * (1.0 + s * (1.0 - sig)))).astype(BF16)
            duc = (dgg * (s * sig)).astype(BF16)
            ds_ref[:, r0:r0 + sz] = dsc
            du_ref[:, r0:r0 + sz] = duc
            df = df + _dot(dsc, wg_v[r0:r0 + sz, :]) + _dot(duc, wu_v[r0:r0 + sz, :])
        dg3_ref[...] += _rows8(df * hh)
        dh1_ref[...] = dy + _rms_bwd(df, hh, r3, g3v)

    row = pl.BlockSpec((tm, D_MODEL), lambda i: (i, 0))
    ffrow = pl.BlockSpec((tm, D_FF), lambda i: (i, 0))
    acc = _full((8, D_MODEL))
    act_bf = jax.ShapeDtypeStruct((n_rows, D_MODEL), BF16)
    ff_bf = jax.ShapeDtypeStruct((n_rows, D_FF), BF16)
    acc_shape = jax.ShapeDtypeStruct((8, D_MODEL), F32)
    w_vmem = pltpu.VMEM((D_FF, D_MODEL), BF16)
    return pl.pallas_call(
        body, name="ffn_fwd_bwd", grid=(n_rows // tm,),
        out_shape=[jax.ShapeDtypeStruct((n_rows, D_MODEL), F32), act_bf, act_bf, ff_bf, ff_bf, ff_bf,
                   acc_shape, acc_shape, acc_shape],
        in_specs=[row, row, _full((1, D_MODEL)), _full((1, D_MODEL))] + [ANY] * 6,
        out_specs=[row, row, row, ffrow, ffrow, ffrow, acc, acc, acc],
        scratch_shapes=[w_vmem, w_vmem, w_vmem, pltpu.VMEM((tm, D_FF), F32), pltpu.VMEM((tm, D_FF), F32),
                        pltpu.SemaphoreType.DMA((3 * N_CHIPS,))],
        compiler_params=_cparams(1),
    )(h1, target, g3, g4, *gathered, *shards)


def _ffn_weight_grads(name, acts, other, exchanged):
    n_rows = other.shape[0]
    n_a, n_ex = len(acts), len(exchanged)
    n_c = n_a
    tk = min(TK_DW, n_rows)
    n_k = n_rows // tk
    half = D_FF // n_c

    def body(other_ref, *rest):
        act_refs = rest[:n_a]
        out_refs = rest[n_a + n_ex:2 * n_a + n_ex]
        c, k = pl.program_id(0), pl.program_id(1)
        if n_ex:
            ex = _ExchangeHalves(rest[n_a:n_a + n_ex], rest[2 * n_a + n_ex:2 * n_a + 2 * n_ex], *rest[-2:])

            @pl.when((c == 0) & (k == 0))
            def _():
                ex.start()

        @pl.when(k == 0)
        def _():
            for o in out_refs:
                o[...] = jnp.zeros_like(o)

        ov = other_ref[...]
        for a, o in zip(act_refs, out_refs):
            o[...] += _dot_tn(a[...], ov)

        if n_ex:
            @pl.when((c == n_c - 1) & (k == n_k - 1))
            def _():
                ex.finish()

    row = pl.BlockSpec((tk, D_MODEL), lambda c, k: (k, 0))
    ffrow = pl.BlockSpec((tk, half), lambda c, k: (k, c))
    out = pl.BlockSpec((half, D_MODEL), lambda c, k: (c, 0))
    outs = pl.pallas_call(
        body, name=name, grid=(n_c, n_k),
        out_shape=[jax.ShapeDtypeStruct((D_FF, D_MODEL), F32)] * n_a + _ExchangeHalves.out_shape(exchanged),
        in_specs=[row] + [ffrow] * n_a + [ANY] * n_ex, out_specs=[out] * n_a + [ANY] * n_ex,
        scratch_shapes=_ExchangeHalves.scratch(n_ex) if n_ex else [],
        compiler_params=_cparams(2),
    )(other, *acts, *exchanged)
    return outs[:n_a], outs[n_a:]


def _mixer_bwd(dh1, m3, z3, conv2, pooled2, x3, zmeta, meta_full, g1, g2, convw, poolw, pscale, gathered, shards,
               after):
    n_seq, seq, _ = x3.shape
    tm = min(TM_MIX_BWD, seq)
    sub = min(SUB_MIX_BWD, tm)
    n_t = seq // tm
    n_out = 13

    def body(dh1_ref, m_ref, z_ref, conv_ref, pooled_ref, x_ref, zm_ref, meta_ref, g1_ref, g2_ref, cw_ref, pw_ref,
             ps_ref, after_ref, win_all, wout_all, win_s, wout_s, *rest):
        (dx_ref, dz_ref, dm_ref, dg1_ref, dg2_ref, dsc_ref, dcw_ref, dpw_ref, dzm_ref, dmeta_ref, dg1m_ref, am_ref,
         dzmb_ref) = rest[:n_out]
        win_v, wout_v, dcb, dqb, mcb, mqb, load_sems = rest[n_out:]
        s, i = pl.program_id(0), pl.program_id(1)
        tr = n_t - 1 - i

        @pl.when((s == 0) & (i == 0))
        def _():
            _load_gathered([win_all, wout_all], [win_s, wout_s],
                           [lambda j: win_v.at[j], lambda j: wout_v.at[pl.ds(j * OUT_SHARD, OUT_SHARD), :]], load_sems)
            for ref in (dg1_ref, dg2_ref, dsc_ref, dcw_ref, dpw_ref, dzm_ref):
                ref[...] = jnp.zeros_like(ref)

        @pl.when(i == 0)
        def _():
            dcb[tm:tm + HALO, :] = jnp.zeros((HALO, D_CONV), F32)
            dqb[tm:tm + HALO, :] = jnp.zeros((HALO, D_POOL), F32)

        @pl.when(i > 0)
        def _():
            dcb[tm:tm + HALO, :] = dcb[0:HALO, :]
            dqb[tm:tm + HALO, :] = dqb[0:HALO, :]

        g1v, g2v = g1_ref[...], g2_ref[...]
        cw = cw_ref[...]

        for r0 in range(tm - sub, -1, -sub):
            rows = slice(r0, r0 + sub)
            dh1v = dh1_ref[0, rows, :]
            mv = m_ref[0, rows, :]
            r2 = _rstd(mv)
            mh = mv * r2
            dg2_ref[...] += _rows8(dh1v * mh)
            dmb = _rms_bwd(dh1v, mh, r2, g2v).astype(BF16)
            dm_ref[rows, :] = dmb
            dyc = _dot_nt(dmb, wout_v[...])
            dyconv = dyc[:, 0:D_CONV]

            for g in range(N_POOL_GROUPS):
                pooled = pooled_ref[rows, _gcols(g)]
                mixed = _dot(pooled, pw_ref[g])
                scale = ps_ref[:, _gcols(g)]
                dyp = dyc[:, D_CONV + g * POOL_GROUP:D_CONV + (g + 1) * POOL_GROUP]
                dsc_ref[:, _gcols(g)] += _rows8(dyp * mixed)
                dmix = (dyp * scale).astype(BF16)
                dpw_ref[g] += _dot_tn(pooled, dmix)
                dqb[rows, _gcols(g)] = _dot_nt(dmix, pw_ref[g])

            zb = z_ref[0, rows, 0:IN_SHARD]
            zc = z_ref[0, rows, IN_SHARD:2 * IN_SHARD]
            zv = z_ref[0, rows, 2 * IN_SHARD:3 * IN_SHARD]
            dconv = dyconv * zb
            dcb[rows, :] = dconv
            d1 = dcb[r0 + 1:r0 + 1 + sub, :]
            d2 = dcb[r0 + 2:r0 + 2 + sub, :]
            dcv = cw[2:3] * dconv + cw[1:2] * d1 + cw[0:1] * d2
            cv = zc * zv
            dcw_ref[0:8, :] += _rows8(cv * d2)
            dcw_ref[8:16, :] += _rows8(cv * d1)
            dcw_ref[16:24, :] += _rows8(cv * dconv)
            dzs = [(dyconv * conv_ref[rows, :]).astype(BF16), (dcv * zv).astype(BF16), (dcv * zc).astype(BF16),
                   jnp.concatenate([_pool_bwd(dqb, g, r0, sub) for g in range(N_POOL_GROUPS)], axis=1).astype(BF16)]
            da = jnp.zeros((sub, D_MODEL), F32)
            for j in range(N_CHIPS):
                dz_ref[j, rows, :] = dzs[j]
                da = da + _dot_nt(dzs[j], win_v[j])
            xt = x_ref[0, rows, :]
            r1 = _rstd(xt)
            xh = xt * r1
            dg1_ref[...] += _rows8(da * xh)
            dx_ref[0, rows, :] = dh1v + _rms_bwd(da, xh, r1, g1v)

        @pl.when(tr == 0)
        def _():
            mcb[0:HALO, :] = jnp.zeros((HALO, D_CONV), F32)
            mqb[0:HALO, :] = jnp.zeros((HALO, D_POOL), F32)
            mcb[HALO:2 * HALO, :] = dcb[0:HALO, :]
            mqb[HALO:2 * HALO, :] = dqb[0:HALO, :]
            m1 = mcb[1:1 + HALO, :]
            m2 = mcb[2:2 + HALO, :]
            zc_m = zm_ref[:, IN_SHARD:2 * IN_SHARD]
            zv_m = zm_ref[:, 2 * IN_SHARD:3 * IN_SHARD]
            cv_m = zc_m * zv_m
            dcw_ref[0:8, :] += _rows8(cv_m * m2)
            dcw_ref[8:16, :] += _rows8(cv_m * m1)
            dcv_m = cw[1:2] * m1 + cw[0:1] * m2
            dzm_ref[:, IN_SHARD:2 * IN_SHARD] += dcv_m * zv_m
            dzm_ref[:, 2 * IN_SHARD:3 * IN_SHARD] += dcv_m * zc_m
            dzm_ref[:, 3 * IN_SHARD:4 * IN_SHARD] += jnp.concatenate(
                [_pool_bwd(mqb, g, 0, HALO) for g in range(N_POOL_GROUPS)], axis=1)

        @pl.when((s == n_seq - 1) & (i == n_t - 1))
        def _():
            xm = meta_ref[...]
            rm = _rstd(xm)
            xmh = xm * rm
            am_ref[...] = (xmh * g1v).astype(BF16)
            da_m = jnp.zeros((N_META, D_MODEL), F32)
            for j in range(N_CHIPS):
                dzj = dzm_ref[:, j * IN_SHARD:(j + 1) * IN_SHARD].astype(BF16)
                dzmb_ref[j] = dzj
                da_m = da_m + _dot_nt(dzj, win_v[j])
            dg1m_ref[...] = _rows8(da_m * xmh)
            dmeta_ref[...] = _rms_bwd(da_m, xmh, rm, g1v)

    row3 = lambda c: pl.BlockSpec((1, tm, c), lambda s, i: (s, n_t - 1 - i, 0))
    row2 = lambda c: pl.BlockSpec((tm, c), lambda s, i: (s * n_t + n_t - 1 - i, 0))
    n_rows = n_seq * seq
    outs = pl.pallas_call(
        body, name="mixer_bwd", grid=(n_seq, n_t),
        out_shape=[jax.ShapeDtypeStruct((n_seq, seq, D_MODEL), F32),
                   jax.ShapeDtypeStruct((N_CHIPS, n_rows, IN_SHARD), BF16), jax.ShapeDtypeStruct((n_rows, D_MODEL), BF16),
                   jax.ShapeDtypeStruct((8, D_MODEL), F32), jax.ShapeDtypeStruct((8, D_MODEL), F32),
                   jax.ShapeDtypeStruct((8, D_POOL), F32), jax.ShapeDtypeStruct((24, D_CONV), F32),
                   jax.ShapeDtypeStruct((N_POOL_GROUPS, POOL_GROUP, POOL_GROUP), F32),
                   jax.ShapeDtypeStruct((N_META, D_IN_PROJ), F32),
                   jax.ShapeDtypeStruct((N_META, D_MODEL), F32), jax.ShapeDtypeStruct((8, D_MODEL), F32),
                   jax.ShapeDtypeStruct((N_META, D_MODEL), BF16),
                   jax.ShapeDtypeStruct((N_CHIPS, N_META, IN_SHARD), BF16)],
        in_specs=[row3(D_MODEL), row3(D_MODEL), row3(D_Z), row2(D_CONV), row2(D_POOL), row3(D_MODEL),
                  _full((N_META, D_IN_PROJ)), _full((N_META, D_MODEL)), _full((1, D_MODEL)), _full((1, D_MODEL)),
                  _full((3, D_CONV)), _full((N_POOL_GROUPS, POOL_GROUP, POOL_GROUP)), _full((1, D_POOL)),
                  _full((8, 128))] + [ANY] * 4,
        out_specs=[row3(D_MODEL), pl.BlockSpec((N_CHIPS, tm, IN_SHARD), lambda s, i: (0, s * n_t + n_t - 1 - i, 0)),
                   row2(D_MODEL),
                   _full((8, D_MODEL)), _full((8, D_MODEL)), _full((8, D_POOL)), _full((24, D_CONV)),
                   _full((N_POOL_GROUPS, POOL_GROUP, POOL_GROUP)), _full((N_META, D_IN_PROJ)),
                   _full((N_META, D_MODEL)), _full((8, D_MODEL)), _full((N_META, D_MODEL)),
                   _full((N_CHIPS, N_META, IN_SHARD))],
        scratch_shapes=[pltpu.VMEM((N_CHIPS, D_MODEL, IN_SHARD), BF16), pltpu.VMEM((D_MODEL, D_MODEL), BF16),
                        pltpu.VMEM((tm + HALO, D_CONV), F32), pltpu.VMEM((tm + HALO, D_POOL), F32),
                        pltpu.VMEM((2 * HALO, D_CONV), F32), pltpu.VMEM((2 * HALO, D_POOL), F32),
                        pltpu.SemaphoreType.DMA((2 * N_CHIPS,))],
        compiler_params=_cparams(2),
    )(dh1, m3, z3, conv2, pooled2, x3, zmeta, meta_full, g1, g2, convw, poolw, pscale, after, *gathered, *shards)
    return outs


def _mixer_weight_grads(a, dz, ycat, dm, a_meta, dz_meta, ffn_sums, small):
    n_rows = a.shape[0]
    tk = min(TK_DW, n_rows)
    n_k = n_rows // tk
    n_sc, n_sm = len(ffn_sums), _AllReduceSmall.N_IN

    def body(a_ref, dz_ref, yc_ref, dm_ref, am_ref, dzm_ref, *rest):
        ins, outs, scratch = rest[:n_sc + n_sm], rest[n_sc + n_sm:2 * n_sc + n_sm + 5], rest[2 * n_sc + n_sm + 5:]
        dwin_ref, dwout_ref = outs[:2]
        scatter = _ScatterToChips(ins[:n_sc], outs[2:2 + n_sc], *scratch[:2])
        reduce_small = _AllReduceSmall(ins[n_sc:], outs[2 + n_sc:], scratch[2:])
        k = pl.program_id(0)

        @pl.when(k == 0)
        def _():
            scatter.start()
            reduce_small.pack_and_send()
            am_t = am_ref[...].T
            for j in range(N_CHIPS):
                dwin_ref[j] = _dot(am_t, dzm_ref[j])
            dwout_ref[...] = jnp.zeros_like(dwout_ref)

        for st in range(2):
            @pl.when(k == ((st + 1) * n_k) // 3)
            def _():
                reduce_small.combine(st)

        a_t = a_ref[...].T
        for j in range(N_CHIPS):
            dwin_ref[j] += _dot(a_t, dz_ref[j])
        dwout_ref[...] += _dot_tn(yc_ref[...], dm_ref[...])

        @pl.when(k == n_k - 1)
        def _():
            reduce_small.combine(2)
            scatter.finish()

    row = pl.BlockSpec((tk, D_MODEL), lambda k: (k, 0))
    outs = pl.pallas_call(
        body, name="mixer_weight_grads", grid=(n_k,),
        out_shape=[jax.ShapeDtypeStruct((N_CHIPS, D_MODEL, IN_SHARD), F32),
                   jax.ShapeDtypeStruct((D_MODEL, D_MODEL), F32)] + _ScatterToChips.out_shape(ffn_sums)
        + _AllReduceSmall.out_shape(),
        in_specs=[row, pl.BlockSpec((N_CHIPS, tk, IN_SHARD), lambda k: (0, k, 0)), row, row,
                  _full((N_META, D_MODEL)), _full((N_CHIPS, N_META, IN_SHARD))] + [ANY] * n_sc
        + [_full(s.shape) for s in small],
        out_specs=[_full((N_CHIPS, D_MODEL, IN_SHARD)), _full((D_MODEL, D_MODEL))] + [ANY] * n_sc
        + [_full(s) for s in _AllReduceSmall.SHAPES],
        scratch_shapes=_ScatterToChips.scratch(n_sc) + _AllReduceSmall.scratch(),
        compiler_params=_cparams(1),
    )(a, dz, ycat, dm, a_meta, dz_meta, *ffn_sums, *small)
    return ([outs[0], outs[1].reshape(N_CHIPS, OUT_SHARD, D_MODEL)], outs[2:2 + n_sc], outs[2 + n_sc:])


def kernel(x, meta_tokens, norm_mix_pre, w_in, conv_w, pool_w, pool_scale, w_out, norm_mix_post, norm_ffn_pre, w_gate, w_up, w_down, norm_ffn_post, loss_target, m_meta_tokens, m_norm_mix_pre, m_w_in, m_conv_w, m_pool_w, m_pool_scale, m_w_out, m_norm_mix_post, m_norm_ffn_pre, m_w_gate, m_w_up, m_w_down, m_norm_ffn_post, v_meta_tokens, v_norm_mix_pre, v_w_in, v_conv_w, v_pool_w, v_pool_scale, v_w_out, v_norm_mix_post, v_norm_ffn_pre, v_w_gate, v_w_up, v_w_down, v_norm_ffn_post):
    n_seq, seq, _ = x.shape
    n_rows = n_seq * seq
    chip = 2 * lax.axis_index("x") + lax.axis_index("y")
    meta_cols = D_MODEL // N_CHIPS
    conv_cols = D_CONV // N_CHIPS

    small = jnp.zeros((2 * HALO, meta_cols), F32)
    small = small.at[0:N_META, :].set(meta_tokens).at[N_META:N_META + 3, 0:conv_cols].set(conv_w[0])
    poolw_bf = pool_w[0].astype(BF16)
    pscale = pool_scale
    g1, g2, g3, g4 = norm_mix_pre, norm_mix_post, norm_ffn_pre, norm_ffn_post
    place = jnp.stack([chip, lax.axis_index("c")]).astype(jnp.int32)

    mix_shards = [w_in[0].astype(BF16), w_out[0].astype(BF16)]
    ((z3, m3, h1, a_bf, conv2, pooled2, yc_bf, zmeta, meta_full, conv_full), (win_all, wout_all, _, *ffn_gathered),
     ffn_shards) = _mixer_fwd(x, g1, g2, poolw_bf, pscale, mix_shards + [small], [w_gate[0].T, w_up[0].T, w_down[0]])
    dh1, f_bf, dd_bf, ds_bf, du_bf, gg_bf, lossp, dg3p, dg4p = _ffn_fwd_bwd(
        h1.reshape(n_rows, D_MODEL), loss_target.reshape(n_rows, D_MODEL), g3, g4, ffn_gathered, ffn_shards)
    as_shards = lambda g: g.reshape(N_CHIPS, FF_SHARD, D_MODEL)
    (dwg_t, dwu_t), _ = _ffn_weight_grads("ffn_weight_grads_gate_up", [ds_bf, du_bf], f_bf, [])
    dwg_t, dwu_t = as_shards(dwg_t), as_shards(dwu_t)
    (dwd,), (dwg_recv, dwu_recv) = _ffn_weight_grads("ffn_weight_grads_down", [gg_bf], dd_bf, [dwg_t, dwu_t])
    dwd = as_shards(dwd)
    behind_bwd = _SplitComm("grad_comm_behind_mixer_bwd", [dwd],
                            _add_pairs_multi([dwg_t, dwu_t], [dwg_recv, dwu_recv], place))
    (grad_x, dz_bf, dm_bf, dg1p, dg2p, dscp, dcwp, dpw, _, dmeta, dg1m, a_meta, dz_meta) = _mixer_bwd(
        dh1.reshape(n_seq, seq, D_MODEL), m3, z3, conv2, pooled2, x, zmeta, meta_full, g1, g2, conv_full, poolw_bf,
        pscale, [win_all, wout_all], mix_shards, behind_bwd.start())
    (dwd_recv,), (dwg_rbuf, dwu_rbuf) = behind_bwd.wait(dg2p)
    (dwd,) = behind_bwd.exchanged
    mix_grads, (dwd_rbuf,), (a_red, b_red, c_red) = _mixer_weight_grads(
        a_bf, dz_bf, yc_bf, dm_bf, a_meta, dz_meta, [_add_pairs(dwd, dwd_recv, place)],
        [dg1p, dg1m, dg2p, dg3p, dg4p, lossp, dmeta, dscp, dcwp, dpw.reshape(SMALL_C_ROWS, POOL_GROUP)])

    behind_sums = _SplitComm("grad_comm_behind_ffn_sums", mix_grads, [])
    ffn_red = _add_chips([dwg_t, dwu_t, dwd], [dwg_recv, dwu_recv, dwd_recv], [dwg_rbuf, dwu_rbuf, dwd_rbuf],
                         place, after=behind_sums.start(), name="grad_add_chips_ffn")
    mix_recvs, _ = behind_sums.wait(ffn_red[0])
    mix_grads = behind_sums.exchanged
    behind_tail = _SplitComm("grad_comm_behind_ffn_tail", [], _add_pairs_multi(mix_grads, mix_recvs, place))
    as_full = lambda r: r.reshape(2 * r.shape[1], r.shape[2])
    g_wg_t, g_wu_t, g_wd = [as_full(r) for r in _gather_halves(list(ffn_red), "grad_gather_halves_ffn",
                                                                 after=behind_tail.start())]
    ffn_out = _adamw_big([(w_gate[0].T, g_wg_t, m_w_gate[0].T, v_w_gate[0].T),
                          (w_up[0].T, g_wu_t, m_w_up[0].T, v_w_up[0].T), (w_down[0], g_wd, m_w_down[0], v_w_down[0])])

    as_c = lambda p: p.reshape(SMALL_C_ROWS, POOL_GROUP)
    loss, small_out = _adamw_small(place, [a_red, b_red, c_red], [
        (meta_tokens, m_meta_tokens, v_meta_tokens),
        (g1, m_norm_mix_pre, v_norm_mix_pre),
        (conv_w[0], m_conv_w[0], v_conv_w[0]),
        (as_c(pool_w), as_c(m_pool_w), as_c(v_pool_w)),
        (pool_scale, m_pool_scale, v_pool_scale),
        (g2, m_norm_mix_post, v_norm_mix_post),
        (g3, m_norm_ffn_pre, v_norm_ffn_pre),
        (g4, m_norm_ffn_post, v_norm_ffn_post),
    ])
    _, mix_rbufs = behind_tail.wait(ffn_out[2][0], small_out[0][0])
    mix_red = _add_chips(mix_grads, mix_recvs, mix_rbufs, place)
    g_win, g_wout = [as_full(r) for r in _gather_halves(list(mix_red), "grad_gather_halves_mixer")]
    big_out = (_adamw_big([(w_in[0], g_win, m_w_in[0], v_w_in[0])])
               + _adamw_big([(w_out[0], g_wout, m_w_out[0], v_w_out[0])]) + ffn_out)
    big_out[2] = [o.T for o in big_out[2]]
    big_out[3] = [o.T for o in big_out[3]]

    s_meta, s_g1, s_conv, s_poolw, s_pscale, s_g2, s_g3, s_g4 = small_out
    b_win, b_wout, b_wg, b_wu, b_wd = big_out

    def leaf(k):
        return [s_meta[k], s_g1[k], b_win[k][None], s_conv[k][None], s_poolw[k].reshape(pool_w.shape), s_pscale[k],
                b_wout[k][None], s_g2[k], s_g3[k], b_wg[k][None], b_wu[k][None], b_wd[k][None], s_g4[k]]

    return (loss.reshape(()), grad_x, *leaf(0), *leaf(1), *leaf(2), *leaf(3))
```
